```python
import math
import jax, jax.numpy as jnp
from jax import lax
import numpy as np

D_MODEL = 2048
BATCH = 8
SEQ = 4096
DEPTH = 2

MIX = D_MODEL
D_MLA = MIX // 2
D_CONV = MIX - D_MLA
N_HEADS = 8
NOPE_DIM = 128
ROPE_DIM = 64
V_DIM = D_MLA // N_HEADS
QK_DIM = NOPE_DIM + ROPE_DIM
Q_LORA = 512
KV_LORA = 256
ROPE_THETA = 10000.0
Q_BLOCK = 128
CONV_K = 31
IN_COLS = Q_LORA + KV_LORA + ROPE_DIM + D_MLA + 2 * D_CONV + D_CONV
EPS = 1e-6

kernel_name = "hybrid_mla_conformer_conv_headgroups"


def rms_norm(x, g):
    xf = x.astype(jnp.float32)
    y = xf * lax.rsqrt(jnp.mean(xf * xf, axis=-1, keepdims=True) + EPS)
    return (y * g.astype(jnp.float32)).astype(x.dtype)


def layer_norm(x, g, b):
    xf = x.astype(jnp.float32)
    mu = jnp.mean(xf, axis=-1, keepdims=True)
    var = jnp.mean(jnp.square(xf - mu), axis=-1, keepdims=True)
    y = (xf - mu) * lax.rsqrt(var + EPS)
    return (y * g.astype(jnp.float32) + b.astype(jnp.float32)).astype(x.dtype)


def rope_tables(positions, dtype):
    inv_freq = 1.0 / (ROPE_THETA ** (jnp.arange(0, ROPE_DIM, 2, dtype=jnp.float32) / ROPE_DIM))
    ang = positions.astype(jnp.float32)[..., None] * inv_freq
    return jnp.cos(ang)[:, :, None, :].astype(dtype), jnp.sin(ang)[:, :, None, :].astype(dtype)


def apply_rope(x, cos, sin):
    x1, x2 = jnp.split(x, 2, axis=-1)
    return jnp.concatenate([x1 * cos - x2 * sin, x2 * cos + x1 * sin], axis=-1)


def causal_block_attention(q, k, v):
    b, s, h, dq = q.shape
    nb = s // Q_BLOCK
    scale = 1.0 / math.sqrt(dq)
    qb = q.reshape(b, nb, Q_BLOCK, h, dq).transpose(1, 0, 2, 3, 4)
    key_pos = jnp.arange(s)

    def one_block(args):
        q_i, blk = args
        scores = jnp.einsum('bqhd,bkhd->bhqk', q_i, k).astype(jnp.float32) * scale
        q_pos = blk * Q_BLOCK + jnp.arange(Q_BLOCK)
        mask = key_pos[None, :] <= q_pos[:, None]
        scores = jnp.where(mask[None, None], scores, -jnp.inf)
        p = jax.nn.softmax(scores, axis=-1).astype(v.dtype)
        return jnp.einsum('bhqk,bkhd->bqhd', p, v)

    out = lax.map(one_block, (qb, jnp.arange(nb)))
    return out.transpose(1, 0, 2, 3, 4).reshape(b, s, h, v.shape[-1])


def causal_depthwise_conv(u, w, bias):
    out = lax.conv_general_dilated(
        u, w[:, None, :].astype(u.dtype),
        window_strides=(1,), padding=((CONV_K - 1, 0),),
        dimension_numbers=('NWC', 'WIO', 'NWC'),
        feature_group_count=u.shape[-1])
    return out + bias


def _fwd_setup_inputs(seed: int = 0) -> dict:
    key = jax.random.key(seed)
    ks = jax.random.split(key, 24)
    f32 = jnp.float32

    def w(k, shape, fan_in):
        return jax.random.normal(k, shape, f32) * (fan_in ** -0.5)

    def gain(k, shape):
        return 1.0 + 0.05 * jax.random.normal(k, shape, f32)

    def small(k, shape):
        return 0.02 * jax.random.normal(k, shape, f32)

    x = jax.random.normal(ks[0], (BATCH, SEQ, D_MODEL), f32)
    c = jax.random.normal(ks[1], (BATCH, D_MODEL), f32)
    offsets = jax.random.randint(ks[2], (BATCH, 1), 0, 1024, dtype=jnp.int32)
    positions = offsets + jnp.arange(SEQ, dtype=jnp.int32)[None, :]
    return {
        'x': x,
        'c': c,
        'positions': positions,
        'ada_w': w(ks[3], (DEPTH, D_MODEL, 3 * D_MODEL), D_MODEL),
        'ada_b': small(ks[4], (DEPTH, 3 * D_MODEL)),
        'norm_g': gain(ks[5], (DEPTH, D_MODEL)),
        'w_in': w(ks[6], (DEPTH, D_MODEL, IN_COLS), D_MODEL),
        'q_lat_g': gain(ks[7], (DEPTH, Q_LORA)),
        'w_q_up': w(ks[8], (DEPTH, Q_LORA, N_HEADS * QK_DIM), Q_LORA),
        'kv_lat_g': gain(ks[9], (DEPTH, KV_LORA)),
        'w_kv_up': w(ks[10], (DEPTH, KV_LORA, N_HEADS * (NOPE_DIM + V_DIM)), KV_LORA),
        'q_norm_g': gain(ks[11], (DEPTH, QK_DIM)),
        'k_norm_g': gain(ks[12], (DEPTH, QK_DIM)),
        'glu_b': small(ks[13], (DEPTH, 2 * D_CONV)),
        'dw_w': w(ks[14], (DEPTH, CONV_K, D_CONV), CONV_K),
        'dw_b': small(ks[15], (DEPTH, D_CONV)),
        'conv_ln_g': gain(ks[16], (DEPTH, D_CONV)),
        'conv_ln_b': small(ks[17], (DEPTH, D_CONV)),
        'w_pw': w(ks[18], (DEPTH, D_CONV, D_CONV), D_CONV),
        'b_pw': small(ks[19], (DEPTH, D_CONV)),
        'w_out': w(ks[20], (DEPTH, MIX, D_MODEL), MIX),
    }


def _fwd_reference(x, c, positions, ada_w, ada_b, norm_g, w_in, q_lat_g, w_q_up, kv_lat_g,
              w_kv_up, q_norm_g, k_norm_g, glu_b, dw_w, dw_b, conv_ln_g, conv_ln_b,
              w_pw, b_pw, w_out):
    b, s, _ = x.shape
    cos, sin = rope_tables(positions, x.dtype)
    c_act = jax.nn.silu(c)
    splits = np.cumsum([Q_LORA, KV_LORA, ROPE_DIM, D_MLA, 2 * D_CONV]).tolist()

    for l in range(DEPTH):
        mod = c_act @ ada_w[l] + ada_b[l]
        shift, scale, gate = [m[:, None, :] for m in jnp.split(mod, 3, axis=-1)]
        h = rms_norm(x, norm_g[l]) * (1.0 + scale) + shift

        z = h @ w_in[l]
        q_lat, kv_lat, k_rope, mla_gate, conv_in, conv_gate = jnp.split(z, splits, axis=-1)

        q = (rms_norm(q_lat, q_lat_g[l]) @ w_q_up[l]).reshape(b, s, N_HEADS, QK_DIM)
        kv = (rms_norm(kv_lat, kv_lat_g[l]) @ w_kv_up[l]).reshape(b, s, N_HEADS, NOPE_DIM + V_DIM)
        k_nope, v = kv[..., :NOPE_DIM], kv[..., NOPE_DIM:]
        k_rope_h = jnp.broadcast_to(k_rope[:, :, None, :], (b, s, N_HEADS, ROPE_DIM))
        k = jnp.concatenate([k_nope, k_rope_h], axis=-1)
        q = rms_norm(q, q_norm_g[l])
        k = rms_norm(k, k_norm_g[l])
        q = jnp.concatenate([q[..., :NOPE_DIM], apply_rope(q[..., NOPE_DIM:], cos, sin)], axis=-1)
        k = jnp.concatenate([k[..., :NOPE_DIM], apply_rope(k[..., NOPE_DIM:], cos, sin)], axis=-1)
        attn = causal_block_attention(q, k, v).reshape(b, s, D_MLA)
        mla_out = attn * jax.nn.silu(mla_gate)

        u_val, u_gate = jnp.split(conv_in + glu_b[l], 2, axis=-1)
        u = u_val * jax.nn.sigmoid(u_gate)
        u = causal_depthwise_conv(u, dw_w[l], dw_b[l])
        u = jax.nn.silu(layer_norm(u, conv_ln_g[l], conv_ln_b[l]))
        u = u @ w_pw[l] + b_pw[l]
        conv_out = u * jax.nn.silu(conv_gate)

        y = jnp.concatenate([mla_out, conv_out], axis=-1) @ w_out[l]
        x = x + gate * y
    return x


import jax as _jax
import jax.numpy as _jnp

TWIN_FORMAT = 'train_step'
FWD_PARAMS = ['x', 'c', 'positions', 'ada_w', 'ada_b', 'norm_g', 'w_in', 'q_lat_g', 'w_q_up', 'kv_lat_g', 'w_kv_up', 'q_norm_g', 'k_norm_g', 'glu_b', 'dw_w', 'dw_b', 'conv_ln_g', 'conv_ln_b', 'w_pw', 'b_pw', 'w_out']
TWIN_WEIGHTS = ['ada_w', 'ada_b', 'norm_g', 'w_in', 'q_lat_g', 'w_q_up', 'kv_lat_g', 'w_kv_up', 'q_norm_g', 'k_norm_g', 'glu_b', 'dw_w', 'dw_b', 'conv_ln_g', 'conv_ln_b', 'w_pw', 'b_pw', 'w_out']
TWIN_DIFF_INPUT = 'x'
TWIN_INPUTS = ['x', 'c', 'positions', 'ada_w', 'ada_b', 'norm_g', 'w_in', 'q_lat_g', 'w_q_up', 'kv_lat_g', 'w_kv_up', 'q_norm_g', 'k_norm_g', 'glu_b', 'dw_w', 'dw_b', 'conv_ln_g', 'conv_ln_b', 'w_pw', 'b_pw', 'w_out', 'loss_target', 'm_ada_w', 'm_ada_b', 'm_norm_g', 'm_w_in', 'm_q_lat_g', 'm_w_q_up', 'm_kv_lat_g', 'm_w_kv_up', 'm_q_norm_g', 'm_k_norm_g', 'm_glu_b', 'm_dw_w', 'm_dw_b', 'm_conv_ln_g', 'm_conv_ln_b', 'm_w_pw', 'm_b_pw', 'm_w_out', 'v_ada_w', 'v_ada_b', 'v_norm_g', 'v_w_in', 'v_q_lat_g', 'v_w_q_up', 'v_kv_lat_g', 'v_w_kv_up', 'v_q_norm_g', 'v_k_norm_g', 'v_glu_b', 'v_dw_w', 'v_dw_b', 'v_conv_ln_g', 'v_conv_ln_b', 'v_w_pw', 'v_b_pw', 'v_w_out']
TWIN_OUTPUTS = ['loss', 'grad_x', 'grad_ada_w', 'grad_ada_b', 'grad_norm_g', 'grad_w_in', 'grad_q_lat_g', 'grad_w_q_up', 'grad_kv_lat_g', 'grad_w_kv_up', 'grad_q_norm_g', 'grad_k_norm_g', 'grad_glu_b', 'grad_dw_w', 'grad_dw_b', 'grad_conv_ln_g', 'grad_conv_ln_b', 'grad_w_pw', 'grad_b_pw', 'grad_w_out', 'delta_ada_w', 'delta_ada_b', 'delta_norm_g', 'delta_w_in', 'delta_q_lat_g', 'delta_w_q_up', 'delta_kv_lat_g', 'delta_w_kv_up', 'delta_q_norm_g', 'delta_k_norm_g', 'delta_glu_b', 'delta_dw_w', 'delta_dw_b', 'delta_conv_ln_g', 'delta_conv_ln_b', 'delta_w_pw', 'delta_b_pw', 'delta_w_out', 'new_m_ada_w', 'new_m_ada_b', 'new_m_norm_g', 'new_m_w_in', 'new_m_q_lat_g', 'new_m_w_q_up', 'new_m_kv_lat_g', 'new_m_w_kv_up', 'new_m_q_norm_g', 'new_m_k_norm_g', 'new_m_glu_b', 'new_m_dw_w', 'new_m_dw_b', 'new_m_conv_ln_g', 'new_m_conv_ln_b', 'new_m_w_pw', 'new_m_b_pw', 'new_m_w_out', 'new_v_ada_w', 'new_v_ada_b', 'new_v_norm_g', 'new_v_w_in', 'new_v_q_lat_g', 'new_v_w_q_up', 'new_v_kv_lat_g', 'new_v_w_kv_up', 'new_v_q_norm_g', 'new_v_k_norm_g', 'new_v_glu_b', 'new_v_dw_w', 'new_v_dw_b', 'new_v_conv_ln_g', 'new_v_conv_ln_b', 'new_v_w_pw', 'new_v_b_pw', 'new_v_w_out']
TWIN_LEAF_KINDS = {'loss': 'loss', 'grad_x': 'grad_x', 'grad_ada_w': 'grad_w', 'grad_ada_b': 'grad_w', 'grad_norm_g': 'grad_w', 'grad_w_in': 'grad_w', 'grad_q_lat_g': 'grad_w', 'grad_w_q_up': 'grad_w', 'grad_kv_lat_g': 'grad_w', 'grad_w_kv_up': 'grad_w', 'grad_q_norm_g': 'grad_w', 'grad_k_norm_g': 'grad_w', 'grad_glu_b': 'grad_w', 'grad_dw_w': 'grad_w', 'grad_dw_b': 'grad_w', 'grad_conv_ln_g': 'grad_w', 'grad_conv_ln_b': 'grad_w', 'grad_w_pw': 'grad_w', 'grad_b_pw': 'grad_w', 'grad_w_out': 'grad_w', 'delta_ada_w': 'delta_w', 'delta_ada_b': 'delta_w', 'delta_norm_g': 'delta_w', 'delta_w_in': 'delta_w', 'delta_q_lat_g': 'delta_w', 'delta_w_q_up': 'delta_w', 'delta_kv_lat_g': 'delta_w', 'delta_w_kv_up': 'delta_w', 'delta_q_norm_g': 'delta_w', 'delta_k_norm_g': 'delta_w', 'delta_glu_b': 'delta_w', 'delta_dw_w': 'delta_w', 'delta_dw_b': 'delta_w', 'delta_conv_ln_g': 'delta_w', 'delta_conv_ln_b': 'delta_w', 'delta_w_pw': 'delta_w', 'delta_b_pw': 'delta_w', 'delta_w_out': 'delta_w', 'new_m_ada_w': 'new_m', 'new_m_ada_b': 'new_m', 'new_m_norm_g': 'new_m', 'new_m_w_in': 'new_m', 'new_m_q_lat_g': 'new_m', 'new_m_w_q_up': 'new_m', 'new_m_kv_lat_g': 'new_m', 'new_m_w_kv_up': 'new_m', 'new_m_q_norm_g': 'new_m', 'new_m_k_norm_g': 'new_m', 'new_m_glu_b': 'new_m', 'new_m_dw_w': 'new_m', 'new_m_dw_b': 'new_m', 'new_m_conv_ln_g': 'new_m', 'new_m_conv_ln_b': 'new_m', 'new_m_w_pw': 'new_m', 'new_m_b_pw': 'new_m', 'new_m_w_out': 'new_m', 'new_v_ada_w': 'new_v', 'new_v_ada_b': 'new_v', 'new_v_norm_g': 'new_v', 'new_v_w_in': 'new_v', 'new_v_q_lat_g': 'new_v', 'new_v_w_q_up': 'new_v', 'new_v_kv_lat_g': 'new_v', 'new_v_w_kv_up': 'new_v', 'new_v_q_norm_g': 'new_v', 'new_v_k_norm_g': 'new_v', 'new_v_glu_b': 'new_v', 'new_v_dw_w': 'new_v', 'new_v_dw_b': 'new_v', 'new_v_conv_ln_g': 'new_v', 'new_v_conv_ln_b': 'new_v', 'new_v_w_pw': 'new_v', 'new_v_b_pw': 'new_v', 'new_v_w_out': 'new_v'}


def _forward(args):
    return _fwd_reference(*[args[k] for k in FWD_PARAMS])


def _output_shape():
    def fwd():
        inp = _fwd_setup_inputs(0)
        return _fwd_reference(*[inp[k] for k in FWD_PARAMS])
    out = _jax.eval_shape(fwd)
    return out.shape, out.dtype

N_MICROBATCH = 1
ADAM_LR = 0.001
ADAM_B1 = 0.9
ADAM_B2 = 0.999
ADAM_EPS = 1e-08
ADAM_WD = 0.01
ADAM_STEP = 10
PER_EXAMPLE_BATCH_AXIS = {'x': 0, 'c': 0, 'positions': 0, 'loss_target': 0}
SHARED_INPUTS = []
_WEIGHT_DTYPES = {'ada_w': _jnp.float32, 'ada_b': _jnp.float32, 'norm_g': _jnp.float32, 'w_in': _jnp.float32, 'q_lat_g': _jnp.float32, 'w_q_up': _jnp.float32, 'kv_lat_g': _jnp.float32, 'w_kv_up': _jnp.float32, 'q_norm_g': _jnp.float32, 'k_norm_g': _jnp.float32, 'glu_b': _jnp.float32, 'dw_w': _jnp.float32, 'dw_b': _jnp.float32, 'conv_ln_g': _jnp.float32, 'conv_ln_b': _jnp.float32, 'w_pw': _jnp.float32, 'b_pw': _jnp.float32, 'w_out': _jnp.float32}
MOMENT_SCALE = {'ada_w': 4.425347e-01, 'ada_b': 1.012494e+00, 'norm_g': 7.392273e-01, 'w_in': 3.675972e-01, 'q_lat_g': 2.083999e-02, 'w_q_up': 1.185796e-02, 'kv_lat_g': 3.729560e+00, 'w_kv_up': 3.047889e-01, 'q_norm_g': 9.543318e-02, 'k_norm_g': 9.570910e-02, 'glu_b': 4.371737e-01, 'dw_w': 2.572202e-01, 'dw_b': 8.389992e-01, 'conv_ln_g': 1.985824e+00, 'conv_ln_b': 1.262294e+00, 'w_pw': 2.722149e-01, 'b_pw': 1.102539e+00, 'w_out': 1.459105e-01}


def _to_microbatches(a, axis):
    t = _jnp.moveaxis(a, axis, 0)
    t = t.reshape((N_MICROBATCH, t.shape[0] // N_MICROBATCH) + t.shape[1:])
    return _jnp.moveaxis(t, 1, axis + 1)


def setup_inputs(seed: int = 0) -> dict:
    inp = _fwd_setup_inputs(seed)
    key = _jax.random.fold_in(_jax.random.key(seed), 7919)
    shape, _ = _output_shape()
    out = dict(inp)
    out["loss_target"] = _jax.random.normal(_jax.random.fold_in(key, 0), shape, _jnp.float32)
    for i, name in enumerate(TWIN_WEIGHTS):
        w = inp[name].astype(_jnp.float32)
        if MOMENT_SCALE is None:
            s = _jnp.sqrt(_jnp.mean(_jnp.square(w)) + 1e-30)
        else:
            s = MOMENT_SCALE[name]
        km, kv = _jax.random.split(_jax.random.fold_in(key, i + 1))
        out[name] = w
        out["m_" + name] = s * _jax.random.normal(km, w.shape, _jnp.float32)
        out["v_" + name] = (s * s) * _jax.random.uniform(kv, w.shape, _jnp.float32, 0.5, 1.5)
    if N_MICROBATCH > 1:
        for name, axis in PER_EXAMPLE_BATCH_AXIS.items():
            out[name] = _to_microbatches(out[name], axis)
    return {'x': out['x'], 'c': out['c'], 'positions': out['positions'], 'ada_w': out['ada_w'], 'ada_b': out['ada_b'], 'norm_g': out['norm_g'], 'w_in': out['w_in'], 'q_lat_g': out['q_lat_g'], 'w_q_up': out['w_q_up'], 'kv_lat_g': out['kv_lat_g'], 'w_kv_up': out['w_kv_up'], 'q_norm_g': out['q_norm_g'], 'k_norm_g': out['k_norm_g'], 'glu_b': out['glu_b'], 'dw_w': out['dw_w'], 'dw_b': out['dw_b'], 'conv_ln_g': out['conv_ln_g'], 'conv_ln_b': out['conv_ln_b'], 'w_pw': out['w_pw'], 'b_pw': out['b_pw'], 'w_out': out['w_out'], 'loss_target': out['loss_target'], 'm_ada_w': out['m_ada_w'], 'm_ada_b': out['m_ada_b'], 'm_norm_g': out['m_norm_g'], 'm_w_in': out['m_w_in'], 'm_q_lat_g': out['m_q_lat_g'], 'm_w_q_up': out['m_w_q_up'], 'm_kv_lat_g': out['m_kv_lat_g'], 'm_w_kv_up': out['m_w_kv_up'], 'm_q_norm_g': out['m_q_norm_g'], 'm_k_norm_g': out['m_k_norm_g'], 'm_glu_b': out['m_glu_b'], 'm_dw_w': out['m_dw_w'], 'm_dw_b': out['m_dw_b'], 'm_conv_ln_g': out['m_conv_ln_g'], 'm_conv_ln_b': out['m_conv_ln_b'], 'm_w_pw': out['m_w_pw'], 'm_b_pw': out['m_b_pw'], 'm_w_out': out['m_w_out'], 'v_ada_w': out['v_ada_w'], 'v_ada_b': out['v_ada_b'], 'v_norm_g': out['v_norm_g'], 'v_w_in': out['v_w_in'], 'v_q_lat_g': out['v_q_lat_g'], 'v_w_q_up': out['v_w_q_up'], 'v_kv_lat_g': out['v_kv_lat_g'], 'v_w_kv_up': out['v_w_kv_up'], 'v_q_norm_g': out['v_q_norm_g'], 'v_k_norm_g': out['v_k_norm_g'], 'v_glu_b': out['v_glu_b'], 'v_dw_w': out['v_dw_w'], 'v_dw_b': out['v_dw_b'], 'v_conv_ln_g': out['v_conv_ln_g'], 'v_conv_ln_b': out['v_conv_ln_b'], 'v_w_pw': out['v_w_pw'], 'v_b_pw': out['v_b_pw'], 'v_w_out': out['v_w_out']}


def _loss(weights, diff, rest, loss_target):
    with _jax.named_scope("forward"):
        args = {**rest, TWIN_DIFF_INPUT: diff, **{k: w.astype(_WEIGHT_DTYPES[k]) for k, w in weights.items()}}
        y = _forward(args)
    with _jax.named_scope("loss_head"):
        err = _jnp.square(y.astype(_jnp.float32) - loss_target)
        return 0.5 * _jnp.sum(_jnp.mean(err, axis=-1)) if err.ndim else 0.5 * err


def _adamw(w, g, m, v):
    m = ADAM_B1 * m + (1.0 - ADAM_B1) * g
    v = ADAM_B2 * v + (1.0 - ADAM_B2) * _jnp.square(g)
    m_hat = m / (1.0 - ADAM_B1 ** ADAM_STEP)
    v_hat = v / (1.0 - ADAM_B2 ** ADAM_STEP)
    delta = -ADAM_LR * (m_hat / (_jnp.sqrt(v_hat) + ADAM_EPS) + ADAM_WD * w)
    return delta, m, v


def reference(x, c, positions, ada_w, ada_b, norm_g, w_in, q_lat_g, w_q_up, kv_lat_g, w_kv_up, q_norm_g, k_norm_g, glu_b, dw_w, dw_b, conv_ln_g, conv_ln_b, w_pw, b_pw, w_out, loss_target, m_ada_w, m_ada_b, m_norm_g, m_w_in, m_q_lat_g, m_w_q_up, m_kv_lat_g, m_w_kv_up, m_q_norm_g, m_k_norm_g, m_glu_b, m_dw_w, m_dw_b, m_conv_ln_g, m_conv_ln_b, m_w_pw, m_b_pw, m_w_out, v_ada_w, v_ada_b, v_norm_g, v_w_in, v_q_lat_g, v_w_q_up, v_kv_lat_g, v_w_kv_up, v_q_norm_g, v_k_norm_g, v_glu_b, v_dw_w, v_dw_b, v_conv_ln_g, v_conv_ln_b, v_w_pw, v_b_pw, v_w_out):
    given = dict(x=x, c=c, positions=positions, ada_w=ada_w, ada_b=ada_b, norm_g=norm_g, w_in=w_in, q_lat_g=q_lat_g, w_q_up=w_q_up, kv_lat_g=kv_lat_g, w_kv_up=w_kv_up, q_norm_g=q_norm_g, k_norm_g=k_norm_g, glu_b=glu_b, dw_w=dw_w, dw_b=dw_b, conv_ln_g=conv_ln_g, conv_ln_b=conv_ln_b, w_pw=w_pw, b_pw=b_pw, w_out=w_out, loss_target=loss_target, m_ada_w=m_ada_w, m_ada_b=m_ada_b, m_norm_g=m_norm_g, m_w_in=m_w_in, m_q_lat_g=m_q_lat_g, m_w_q_up=m_w_q_up, m_kv_lat_g=m_kv_lat_g, m_w_kv_up=m_w_kv_up, m_q_norm_g=m_q_norm_g, m_k_norm_g=m_k_norm_g, m_glu_b=m_glu_b, m_dw_w=m_dw_w, m_dw_b=m_dw_b, m_conv_ln_g=m_conv_ln_g, m_conv_ln_b=m_conv_ln_b, m_w_pw=m_w_pw, m_b_pw=m_b_pw, m_w_out=m_w_out, v_ada_w=v_ada_w, v_ada_b=v_ada_b, v_norm_g=v_norm_g, v_w_in=v_w_in, v_q_lat_g=v_q_lat_g, v_w_q_up=v_w_q_up, v_kv_lat_g=v_kv_lat_g, v_w_kv_up=v_w_kv_up, v_q_norm_g=v_q_norm_g, v_k_norm_g=v_k_norm_g, v_glu_b=v_glu_b, v_dw_w=v_dw_w, v_dw_b=v_dw_b, v_conv_ln_g=v_conv_ln_g, v_conv_ln_b=v_conv_ln_b, v_w_pw=v_w_pw, v_b_pw=v_b_pw, v_w_out=v_w_out)
    weights = {n: given[n] for n in TWIN_WEIGHTS}
    shared = {n: given[n] for n in SHARED_INPUTS}
    per_example = {n: given[n] for n in ['x', 'c', 'positions']}
    grad_fn = _jax.value_and_grad(_loss, argnums=(0, 1))

    def one_microbatch(ex, loss_target):
        ex = dict(ex)
        diff = ex.pop(TWIN_DIFF_INPUT)
        return grad_fn(weights, diff, {**shared, **ex}, loss_target)

    if N_MICROBATCH == 1:
        loss, (grad_w, grad_x) = one_microbatch(per_example, given["loss_target"])
    else:
        def body(carry, xs):
            loss_sum, grad_sum = carry
            l_k, (gw_k, gx_k) = one_microbatch(xs[0], xs[1])
            with _jax.named_scope("update"):
                return (loss_sum + l_k, _jax.tree.map(_jnp.add, grad_sum, gw_k)), gx_k

        init = (_jnp.zeros((), _jnp.float32), _jax.tree.map(_jnp.zeros_like, weights))
        (loss, grad_w), grad_x = _jax.lax.scan(body, init, (per_example, given["loss_target"]))
    with _jax.named_scope("update"):
        delta_w, new_m, new_v = {}, {}, {}
        for n in TWIN_WEIGHTS:
            delta_w[n], new_m[n], new_v[n] = _adamw(weights[n], grad_w[n], given["m_" + n], given["v_" + n])
    return (loss, grad_x, *[grad_w[n] for n in TWIN_WEIGHTS], *[delta_w[n] for n in TWIN_WEIGHTS],
            *[new_m[n] for n in TWIN_WEIGHTS], *[new_v[n] for n in TWIN_WEIGHTS])
```

```python
import functools
import math

import jax
import jax.numpy as jnp
from jax import lax
from jax.experimental import pallas as pl
from jax.experimental.pallas import tpu as pltpu

F32 = jnp.float32
MXU_DTYPE = jnp.bfloat16
WIRE_DTYPE = jnp.bfloat16

D_MODEL = 2048
N_LAYERS = 2
N_DEV = 8
N_HEADS = 8
NOPE = 128
ROPE = 64
V_DIM = 128
QK_DIM = NOPE + ROPE
Q_LORA = 512
KV_LORA = 256
D_MLA = N_HEADS * V_DIM
D_CONV = 1024
CONV_K = 31
ROPE_THETA = 10000.0
EPS = 1e-6
LANE = 128
HEAD_PAD = 2 * LANE
HALO = 32

SEG_CI = (0, 2 * D_CONV)
SEG_MG = (2 * D_CONV, D_MLA)
SEG_CG = (2 * D_CONV + D_MLA, D_CONV)
SEG_QL = (2 * D_CONV + D_MLA + D_CONV, Q_LORA)
SEG_KVL = (SEG_QL[0] + Q_LORA, KV_LORA)
SEG_KR = (SEG_KVL[0] + KV_LORA, LANE)
IN_PAD = SEG_KR[0] + LANE
IN_COLS = Q_LORA + KV_LORA + ROPE + D_MLA + 2 * D_CONV + D_CONV

ADAM_LR = 0.001
ADAM_B1 = 0.9
ADAM_B2 = 0.999
ADAM_EPS = 1e-08
ADAM_WD = 0.01
ADAM_STEP = 10

VMEM_LIMIT = 56 * 1024 * 1024
ATT_T = 512
ROW_T = 256
CONV_T = 128
MESH_ID = pl.DeviceIdType.MESH


def _cp(sem=None):
    kw = dict(vmem_limit_bytes=VMEM_LIMIT)
    if sem is not None:
        kw["dimension_semantics"] = sem
    return pltpu.CompilerParams(**kw)


def _sds(shape, dtype):
    return jax.ShapeDtypeStruct(shape, dtype)


def _silu(x):
    return x * jax.nn.sigmoid(x)


def _dsilu(x):
    s = jax.nn.sigmoid(x)
    return s * (1.0 + x * (1.0 - s))


def _rowspec(t, width, col=0):
    return pl.BlockSpec((t, width), lambda i: (i, col))


def _vecspec(width):
    return pl.BlockSpec((1, width), lambda i: (0, 0))


def _colsum(v):
    return jnp.sum(v, axis=0, keepdims=True)


def _mm(a, b, *, name, ta=False, tb=False, out_dtype=F32, tm=512, tn=512, tk=None, n_outer=False):
    if ta:
        kdim, m = a.shape
    else:
        m, kdim = a.shape
    if tb:
        n, k2 = b.shape
    else:
        k2, n = b.shape
    assert kdim == k2, (a.shape, b.shape)
    tm, tn = min(tm, m), min(tn, n)
    tk = kdim if tk is None else min(tk, kdim)
    assert m % tm == 0 and n % tn == 0 and kdim % tk == 0, (m, n, kdim, tm, tn, tk)
    nk = kdim // tk
    dims = (((0 if ta else 1,), (1 if tb else 0,)), ((), ()))

    def body(a_ref, b_ref, o_ref, *scratch):
        prod = lax.dot_general(a_ref[...].astype(MXU_DTYPE), b_ref[...].astype(MXU_DTYPE), dims,
                               preferred_element_type=F32)
        if nk == 1:
            o_ref[...] = prod.astype(o_ref.dtype)
        else:
            acc = scratch[0]
            k = pl.program_id(2)

            @pl.when(k == 0)
            def _():
                acc[...] = prod

            @pl.when(k > 0)
            def _():
                acc[...] += prod

            @pl.when(k == nk - 1)
            def _():
                o_ref[...] = acc[...].astype(o_ref.dtype)

    if n_outer:
        ij = lambda g0, g1: (g1, g0)
        grid = (n // tn, m // tm, nk)
    else:
        ij = lambda g0, g1: (g0, g1)
        grid = (m // tm, n // tn, nk)

    def a_map(g0, g1, k):
        i, _ = ij(g0, g1)
        return (k, i) if ta else (i, k)

    def b_map(g0, g1, k):
        _, j = ij(g0, g1)
        return (j, k) if tb else (k, j)

    def o_map(g0, g1, k):
        return ij(g0, g1)

    return pl.pallas_call(
        body, name=name, grid=grid,
        in_specs=[pl.BlockSpec((tk, tm) if ta else (tm, tk), a_map),
                  pl.BlockSpec((tn, tk) if tb else (tk, tn), b_map)],
        out_specs=pl.BlockSpec((tm, tn), o_map),
        out_shape=_sds((m, n), out_dtype),
        scratch_shapes=[pltpu.VMEM((tm, tn), F32)] if nk > 1 else [],
        compiler_params=_cp(("parallel", "parallel", "arbitrary")),
    )(a, b)


def _prenorm(x, g, shift, sc1p, *, name):
    s, d = x.shape
    t = min(ROW_T, s)

    def body(x_ref, g_ref, sh_ref, sc_ref, h_ref):
        xv = x_ref[...]
        r = lax.rsqrt(jnp.mean(xv * xv, axis=-1, keepdims=True) + EPS)
        h_ref[...] = ((xv * r) * g_ref[...] * sc_ref[...] + sh_ref[...]).astype(h_ref.dtype)

    return pl.pallas_call(
        body, name=name, grid=(s // t,),
        in_specs=[_rowspec(t, d), _vecspec(d), _vecspec(d), _vecspec(d)],
        out_specs=_rowspec(t, d), out_shape=_sds((s, d), MXU_DTYPE),
        compiler_params=_cp(("parallel",)),
    )(x, g, shift, sc1p)


def _lat_norm(z, g_ql, g_kvl, *, name):
    s = z.shape[0]
    t = min(ROW_T, s)

    def body(ql_ref, kvl_ref, gq_ref, gk_ref, qn_ref, kn_ref):
        for src, g_ref, dst in ((ql_ref, gq_ref, qn_ref), (kvl_ref, gk_ref, kn_ref)):
            v = src[...]
            r = lax.rsqrt(jnp.mean(v * v, axis=-1, keepdims=True) + EPS)
            dst[...] = ((v * r) * g_ref[...]).astype(dst.dtype)

    return pl.pallas_call(
        body, name=name, grid=(s // t,),
        in_specs=[_rowspec(t, Q_LORA, SEG_QL[0] // Q_LORA), _rowspec(t, KV_LORA, SEG_KVL[0] // KV_LORA),
                  _vecspec(Q_LORA), _vecspec(KV_LORA)],
        out_specs=[_rowspec(t, Q_LORA), _rowspec(t, KV_LORA)],
        out_shape=[_sds((s, Q_LORA), MXU_DTYPE), _sds((s, KV_LORA), MXU_DTYPE)],
        compiler_params=_cp(("parallel",)),
    )(z, z, g_ql, g_kvl)


def _rope_fwd(r, c_t, s1_t, s2_t):
    return r * c_t + pltpu.roll(r, LANE - ROPE // 2, 1) * s1_t + pltpu.roll(r, ROPE // 2, 1) * s2_t


def _rope_bwd(d, c_t, s1_t, s2_t):
    return d * c_t + pltpu.roll(d * s1_t, ROPE // 2, 1) + pltpu.roll(d * s2_t, LANE - ROPE // 2, 1)


def _lanesum(v):
    return jnp.sum(v, axis=-1, keepdims=True)


def _qk_prep(q_raw, kv, z, c_t, s1_t, s2_t, gqn, gqr, gkn, gkr, *, name):
    s = q_raw.shape[0]
    t = min(ROW_T, s)
    scale = 1.0 / math.sqrt(QK_DIM)

    def body(q_ref, kv_ref, kr_ref, c_ref, s1_ref, s2_ref, gqn_ref, gqr_ref, gkn_ref, gkr_ref,
             qf_ref, kf_ref, vf_ref):
        c_v, s1_v, s2_v = c_ref[...], s1_ref[...], s2_ref[...]
        kr = kr_ref[...]
        kr_ss = _lanesum(kr * kr)
        for h in range(N_HEADS):
            n = q_ref[:, h * LANE:(h + 1) * LANE]
            r = q_ref[:, N_HEADS * LANE + h * LANE:N_HEADS * LANE + (h + 1) * LANE]
            rs = lax.rsqrt((_lanesum(n * n) + _lanesum(r * r)) * (1.0 / QK_DIM) + EPS)
            qf_ref[h, :, 0:LANE] = (((n * rs) * gqn_ref[...]) * scale).astype(qf_ref.dtype)
            rr = _rope_fwd((r * rs) * gqr_ref[...], c_v, s1_v, s2_v)
            qf_ref[h, :, LANE:HEAD_PAD] = (rr * scale).astype(qf_ref.dtype)

            n = kv_ref[:, h * 2 * LANE:h * 2 * LANE + LANE]
            rs = lax.rsqrt((_lanesum(n * n) + kr_ss) * (1.0 / QK_DIM) + EPS)
            kf_ref[h, :, 0:LANE] = ((n * rs) * gkn_ref[...]).astype(kf_ref.dtype)
            kf_ref[h, :, LANE:HEAD_PAD] = _rope_fwd((kr * rs) * gkr_ref[...], c_v, s1_v, s2_v).astype(kf_ref.dtype)
            vf_ref[h] = kv_ref[:, h * 2 * LANE + LANE:(h + 1) * 2 * LANE].astype(vf_ref.dtype)

    hspec = lambda w: pl.BlockSpec((N_HEADS, t, w), lambda i: (0, i, 0))
    return pl.pallas_call(
        body, name=name, grid=(s // t,),
        in_specs=[_rowspec(t, 2 * N_HEADS * LANE), _rowspec(t, 2 * N_HEADS * LANE),
                  _rowspec(t, LANE, SEG_KR[0] // LANE),
                  _rowspec(t, LANE), _rowspec(t, LANE), _rowspec(t, LANE),
                  _vecspec(LANE), _vecspec(LANE), _vecspec(LANE), _vecspec(LANE)],
        out_specs=[hspec(HEAD_PAD), hspec(HEAD_PAD), hspec(V_DIM)],
        out_shape=[_sds((N_HEADS, s, HEAD_PAD), MXU_DTYPE), _sds((N_HEADS, s, HEAD_PAD), MXU_DTYPE),
                   _sds((N_HEADS, s, V_DIM), MXU_DTYPE)],
        compiler_params=_cp(("parallel",)),
    )(q_raw, kv, z, c_t, s1_t, s2_t, gqn, gqr, gkn, gkr)


def _causal_mask(t):
    row = lax.broadcasted_iota(jnp.int32, (t, t), 0)
    col = lax.broadcasted_iota(jnp.int32, (t, t), 1)
    return col <= row


NEG = -1e30


def _flash_fwd(qf, kf, vf, *, name):
    nh, s, dk = qf.shape
    dv = vf.shape[-1]
    t = min(ATT_T, s)
    n = s // t

    def body(q_ref, k_ref, v_ref, o_ref, lse_ref, m_s, l_s, acc_s):
        i, j = pl.program_id(1), pl.program_id(2)

        @pl.when(j == 0)
        def _():
            m_s[...] = jnp.full(m_s.shape, NEG, F32)
            l_s[...] = jnp.zeros(l_s.shape, F32)
            acc_s[...] = jnp.zeros(acc_s.shape, F32)

        def step(masked):
            sc = lax.dot_general(q_ref[0], k_ref[0], (((1,), (1,)), ((), ())), preferred_element_type=F32)
            if masked:
                sc = jnp.where(_causal_mask(t), sc, NEG)
            m_prev = m_s[...]
            m_new = jnp.maximum(m_prev, jnp.max(sc, axis=-1, keepdims=True))
            alpha = jnp.exp(m_prev - m_new)
            p = jnp.exp(sc - m_new)
            l_s[...] = alpha * l_s[...] + _lanesum(p)
            acc_s[...] = alpha * acc_s[...] + jnp.dot(p.astype(MXU_DTYPE), v_ref[0], preferred_element_type=F32)
            m_s[...] = m_new

        @pl.when(j < i)
        def _():
            step(False)

        @pl.when(j == i)
        def _():
            step(True)
            o_ref[...] = acc_s[...] / l_s[...]
            lse_ref[0] = m_s[...] + jnp.log(l_s[...])

    return pl.pallas_call(
        body, name=name, grid=(nh, n, n),
        in_specs=[pl.BlockSpec((1, t, dk), lambda h, i, j: (h, i, 0)),
                  pl.BlockSpec((1, t, dk), lambda h, i, j: (h, jnp.minimum(i, j), 0)),
                  pl.BlockSpec((1, t, dv), lambda h, i, j: (h, jnp.minimum(i, j), 0))],
        out_specs=[pl.BlockSpec((t, dv), lambda h, i, j: (i, h)),
                   pl.BlockSpec((1, t, 1), lambda h, i, j: (h, i, 0))],
        out_shape=[_sds((s, nh * dv), F32), _sds((nh, s, 1), F32)],
        scratch_shapes=[pltpu.VMEM((t, 1), F32), pltpu.VMEM((t, 1), F32), pltpu.VMEM((t, dv), F32)],
        compiler_params=_cp(("parallel", "parallel", "arbitrary")),
    )(qf, kf, vf)


def _dw_taps(ext_ref, w_ref, row0, t_rows, lane0, lanes, first_off):
    acc = None
    for k in range(CONV_K):
        term = w_ref[k:k + 1, lane0:lane0 + lanes] * ext_ref[pl.ds(row0 + first_off + k, t_rows), lane0:lane0 + lanes]
        acc = term if acc is None else acc + term
    return acc


CONV_RC = 32
CONV_LC = 256


def _conv_fwd(z, glu_b, dw_w, dw_b, ln_g, ln_b, *, name):
    s = z.shape[0]
    t = min(CONV_T, s)
    c2 = 2 * D_CONV
    hb = t // HALO

    def body(zm_ref, zh_ref, gb_ref, w_ref, wb_ref, g_ref, b_ref, u1_ref, u3_ref, ext):
        i = pl.program_id(0)

        def glu(zv):
            ci = zv + gb_ref[...]
            return ci[:, :D_CONV] * jax.nn.sigmoid(ci[:, D_CONV:])

        ext[HALO:, :] = glu(zm_ref[...])
        ext[0:HALO, :] = jnp.where(i > 0, glu(zh_ref[...]), 0.0)
        for rc in range(0, t, CONV_RC):
            for lc in range(0, D_CONV, CONV_LC):
                acc = _dw_taps(ext, w_ref, rc, CONV_RC, lc, CONV_LC, HALO - (CONV_K - 1))
                u1_ref[rc:rc + CONV_RC, lc:lc + CONV_LC] = acc + wb_ref[:, lc:lc + CONV_LC]
        u1 = u1_ref[...]
        mu = jnp.mean(u1, axis=-1, keepdims=True)
        cen = u1 - mu
        var = jnp.mean(cen * cen, axis=-1, keepdims=True)
        u2 = (cen * lax.rsqrt(var + EPS)) * g_ref[...] + b_ref[...]
        u3_ref[...] = _silu(u2).astype(u3_ref.dtype)

    return pl.pallas_call(
        body, name=name, grid=(s // t,),
        in_specs=[_rowspec(t, c2), pl.BlockSpec((HALO, c2), lambda i: (jnp.maximum(i * hb - 1, 0), 0)),
                  _vecspec(c2), pl.BlockSpec((HALO, D_CONV), lambda i: (0, 0)), _vecspec(D_CONV),
                  _vecspec(D_CONV), _vecspec(D_CONV)],
        out_specs=[_rowspec(t, D_CONV), _rowspec(t, D_CONV)],
        out_shape=[_sds((s, D_CONV), F32), _sds((s, D_CONV), MXU_DTYPE)],
        scratch_shapes=[pltpu.VMEM((t + HALO, D_CONV), F32)],
        compiler_params=_cp(("parallel",)),
    )(z, z, glu_b, dw_w, dw_b, ln_g, ln_b)


def _gate_cat(o, z, u4m, b_pw, *, name):
    s = o.shape[0]
    t = min(ROW_T, s)

    def body(o_ref, mg_ref, u4_ref, cg_ref, b_ref, cat_ref):
        cat_ref[:, :D_MLA] = (o_ref[...] * _silu(mg_ref[...])).astype(cat_ref.dtype)
        cat_ref[:, D_MLA:] = ((u4_ref[...] + b_ref[...]) * _silu(cg_ref[...])).astype(cat_ref.dtype)

    return pl.pallas_call(
        body, name=name, grid=(s // t,),
        in_specs=[_rowspec(t, D_MLA), _rowspec(t, D_MLA, SEG_MG[0] // D_MLA), _rowspec(t, D_CONV),
                  _rowspec(t, D_CONV, SEG_CG[0] // D_CONV), _vecspec(D_CONV)],
        out_specs=_rowspec(t, D_MLA + D_CONV), out_shape=_sds((s, D_MLA + D_CONV), MXU_DTYPE),
        compiler_params=_cp(("parallel",)),
    )(o, z, u4m, z, b_pw)


def _residual(x, y, gate, *, name):
    s, d = x.shape
    t = min(ROW_T, s)

    def body(x_ref, y_ref, g_ref, o_ref):
        o_ref[...] = x_ref[...] + g_ref[...] * y_ref[...]

    return pl.pallas_call(
        body, name=name, grid=(s // t,),
        in_specs=[_rowspec(t, d), _rowspec(t, d), _vecspec(d)],
        out_specs=_rowspec(t, d), out_shape=_sds((s, d), F32),
        compiler_params=_cp(("parallel",)),
    )(x, y, gate)


def _loss_head(xf, target, *, name):
    s, d = xf.shape
    t = min(ROW_T, s)

    def body(x_ref, t_ref, gx_ref, loss_ref):
        @pl.when(pl.program_id(0) == 0)
        def _():
            loss_ref[...] = jnp.zeros(loss_ref.shape, F32)

        err = x_ref[...] - t_ref[...]
        gx_ref[...] = err * (1.0 / d)
        loss_ref[...] += 0.5 * jnp.sum(_lanesum(err * err) * (1.0 / d), axis=0, keepdims=True)

    return pl.pallas_call(
        body, name=name, grid=(s // t,),
        in_specs=[_rowspec(t, d), _rowspec(t, d)],
        out_specs=[_rowspec(t, d), pl.BlockSpec((1, 1), lambda i: (0, 0))],
        out_shape=[_sds((s, d), F32), _sds((1, 1), F32)],
        compiler_params=_cp(("arbitrary",)),
    )(xf, target)


def _acc_init(refs):
    @pl.when(pl.program_id(0) == 0)
    def _():
        for r in refs:
            r[...] = jnp.zeros(r.shape, r.dtype)


def _out_bwd(gxo, y, gate, *, name):
    s, d = gxo.shape
    t = min(ROW_T, s)

    def body(g_ref, y_ref, gate_ref, dy_ref, dgate_ref):
        _acc_init([dgate_ref])
        gv = g_ref[...]
        dy_ref[...] = (gv * gate_ref[...]).astype(dy_ref.dtype)
        dgate_ref[...] += _colsum(gv * y_ref[...])

    return pl.pallas_call(
        body, name=name, grid=(s // t,),
        in_specs=[_rowspec(t, d), _rowspec(t, d), _vecspec(d)],
        out_specs=[_rowspec(t, d), _vecspec(d)],
        out_shape=[_sds((s, d), MXU_DTYPE), _sds((1, d), F32)],
        compiler_params=_cp(("arbitrary",)),
    )(gxo, y, gate)


def _gate_bwd(dcat, o, z, u4m, b_pw, *, name):
    s = o.shape[0]
    t = min(ROW_T, s)

    def body(dm_ref, dc_ref, o_ref, mg_ref, u4_ref, cg_ref, b_ref,
             do_ref, delta_ref, dmg_ref, du4_ref, dcg_ref, gb_ref):
        _acc_init([gb_ref])
        dm, ov, mg = dm_ref[...], o_ref[...], mg_ref[...]
        do = dm * _silu(mg)
        do_ref[...] = do.astype(do_ref.dtype)
        dmg_ref[...] = (dm * ov * _dsilu(mg)).astype(dmg_ref.dtype)
        prod = do * ov
        for h in range(N_HEADS):
            delta_ref[h] = _lanesum(prod[:, h * V_DIM:(h + 1) * V_DIM])
        dc, cg = dc_ref[...], cg_ref[...]
        du4 = dc * _silu(cg)
        du4_ref[...] = du4.astype(du4_ref.dtype)
        dcg_ref[...] = (dc * (u4_ref[...] + b_ref[...]) * _dsilu(cg)).astype(dcg_ref.dtype)
        gb_ref[...] += _colsum(du4)

    return pl.pallas_call(
        body, name=name, grid=(s // t,),
        in_specs=[_rowspec(t, D_MLA, 0), _rowspec(t, D_CONV, 1), _rowspec(t, D_MLA),
                  _rowspec(t, D_MLA, SEG_MG[0] // D_MLA), _rowspec(t, D_CONV),
                  _rowspec(t, D_CONV, SEG_CG[0] // D_CONV), _vecspec(D_CONV)],
        out_specs=[_rowspec(t, D_MLA), pl.BlockSpec((N_HEADS, t, 1), lambda i: (0, i, 0)),
                   _rowspec(t, D_MLA), _rowspec(t, D_CONV), _rowspec(t, D_CONV), _vecspec(D_CONV)],
        out_shape=[_sds((s, D_MLA), MXU_DTYPE), _sds((N_HEADS, s, 1), F32), _sds((s, D_MLA), MXU_DTYPE),
                   _sds((s, D_CONV), MXU_DTYPE), _sds((s, D_CONV), MXU_DTYPE), _sds((1, D_CONV), F32)],
        compiler_params=_cp(("arbitrary",)),
    )(dcat, dcat, o, z, u4m, z, b_pw)


def _conv_bwd(du3, u1, z, glu_b, dw_w, ln_g, ln_b, *, name):
    s = z.shape[0]
    t = min(CONV_T, s)
    c2 = 2 * D_CONV
    hb = t // HALO
    n_blk = s // t
    last_halo = s // HALO - 1

    def body(d3m_ref, d3h_ref, u1m_ref, u1h_ref, zm_ref, zh_ref, gb_ref, w_ref, g_ref, b_ref,
             dci_ref, gg_ref, gbn_ref, gwb_ref, ggb_ref, gw_ref, dext, uext, du0_s, gw_acc):
        i = pl.program_id(0)
        _acc_init([gg_ref, gbn_ref, gwb_ref, ggb_ref, gw_acc])

        def ln_bwd(d3, u1v):
            mu = jnp.mean(u1v, axis=-1, keepdims=True)
            cen = u1v - mu
            rstd = lax.rsqrt(jnp.mean(cen * cen, axis=-1, keepdims=True) + EPS)
            uh = cen * rstd
            d2 = d3 * _dsilu(uh * g_ref[...] + b_ref[...])
            dh = d2 * g_ref[...]
            d1 = rstd * (dh - jnp.mean(dh, axis=-1, keepdims=True) - uh * jnp.mean(dh * uh, axis=-1, keepdims=True))
            return d1, d2, uh

        d1, d2, uh = ln_bwd(d3m_ref[...], u1m_ref[...])
        gg_ref[...] += _colsum(d2 * uh)
        gbn_ref[...] += _colsum(d2)
        gwb_ref[...] += _colsum(d1)
        dext[0:t, :] = d1
        d1h, _, _ = ln_bwd(d3h_ref[...], u1h_ref[...])
        dext[t:, :] = jnp.where(i < n_blk - 1, d1h, 0.0)

        def glu_parts(zv):
            ci = zv + gb_ref[...]
            return ci[:, :D_CONV], jax.nn.sigmoid(ci[:, D_CONV:])

        val, sg = glu_parts(zm_ref[...])
        uext[HALO:, :] = val * sg
        valh, sgh = glu_parts(zh_ref[...])
        uext[0:HALO, :] = jnp.where(i > 0, valh * sgh, 0.0)

        for rc in range(0, t, CONV_RC):
            for lc in range(0, D_CONV, CONV_LC):
                acc = None
                dchunk = dext[rc:rc + CONV_RC, lc:lc + CONV_LC]
                for k in range(CONV_K):
                    term = w_ref[k:k + 1, lc:lc + CONV_LC] * dext[pl.ds(rc + (CONV_K - 1) - k, CONV_RC), lc:lc + CONV_LC]
                    acc = term if acc is None else acc + term
                    pr = dchunk * uext[pl.ds(rc + HALO - (CONV_K - 1) + k, CONV_RC), lc:lc + CONV_LC]
                    part = pr[0:8]
                    for r8 in range(8, CONV_RC, 8):
                        part = part + pr[r8:r8 + 8]
                    gw_acc[k, :, lc:lc + CONV_LC] += part
                du0_s[rc:rc + CONV_RC, lc:lc + CONV_LC] = acc

        du0 = du0_s[...]
        dval = du0 * sg
        dgt = du0 * val * sg * (1.0 - sg)
        dci_ref[:, :D_CONV] = dval.astype(dci_ref.dtype)
        dci_ref[:, D_CONV:] = dgt.astype(dci_ref.dtype)
        ggb_ref[:, :D_CONV] += _colsum(dval)
        ggb_ref[:, D_CONV:] += _colsum(dgt)

        @pl.when(i == n_blk - 1)
        def _():
            gw_ref[...] = jnp.sum(gw_acc[...], axis=1)

    halo_next = lambda w: pl.BlockSpec((HALO, w), lambda i: (jnp.minimum((i + 1) * hb, last_halo), 0))
    return pl.pallas_call(
        body, name=name, grid=(n_blk,),
        in_specs=[_rowspec(t, D_CONV), halo_next(D_CONV), _rowspec(t, D_CONV), halo_next(D_CONV),
                  _rowspec(t, c2), pl.BlockSpec((HALO, c2), lambda i: (jnp.maximum(i * hb - 1, 0), 0)),
                  _vecspec(c2), pl.BlockSpec((HALO, D_CONV), lambda i: (0, 0)), _vecspec(D_CONV), _vecspec(D_CONV)],
        out_specs=[_rowspec(t, c2), _vecspec(D_CONV), _vecspec(D_CONV), _vecspec(D_CONV), _vecspec(c2),
                   pl.BlockSpec((HALO, D_CONV), lambda i: (0, 0))],
        out_shape=[_sds((s, c2), MXU_DTYPE), _sds((1, D_CONV), F32), _sds((1, D_CONV), F32), _sds((1, D_CONV), F32),
                   _sds((1, c2), F32), _sds((HALO, D_CONV), F32)],
        scratch_shapes=[pltpu.VMEM((t + HALO, D_CONV), F32), pltpu.VMEM((t + HALO, D_CONV), F32),
                        pltpu.VMEM((t, D_CONV), F32), pltpu.VMEM((HALO, 8, D_CONV), F32)],
        compiler_params=_cp(("arbitrary",)),
    )(du3, du3, u1, u1, z, z, glu_b, dw_w, ln_g, ln_b)


def _flash_bwd(qf, kf, vf, do, lse, delta, *, name):
    nh, s, dk = qf.shape
    dv = vf.shape[-1]
    t = min(ATT_T, s)
    n = s // t
    tdims = (((0,), (0,)), ((), ()))

    def body(q_ref, k_ref, v_ref, do_ref, lse_ref, dl_ref, dq_ref, dk_ref, dv_ref, dk_s, dv_s):
        j, i = pl.program_id(1), pl.program_id(2)

        @pl.when((j == 0) & (i == 0))
        def _():
            dq_ref[...] = jnp.zeros(dq_ref.shape, F32)

        def step(masked):
            q, k, v, dov = q_ref[0], k_ref[0], v_ref[0], do_ref[...]
            sc = lax.dot_general(q, k, (((1,), (1,)), ((), ())), preferred_element_type=F32)
            p = jnp.exp(sc - lse_ref[0])
            if masked:
                p = jnp.where(_causal_mask(t), p, 0.0)
            dvp = lax.dot_general(p.astype(MXU_DTYPE), dov, tdims, preferred_element_type=F32)
            dp = lax.dot_general(dov, v, (((1,), (1,)), ((), ())), preferred_element_type=F32)
            ds = (p * (dp - dl_ref[0])).astype(MXU_DTYPE)
            dkp = lax.dot_general(ds, q, tdims, preferred_element_type=F32)
            if masked:
                dv_s[...] = dvp
                dk_s[...] = dkp
            else:
                dv_s[...] += dvp
                dk_s[...] += dkp
            rows = pl.ds(pl.multiple_of(i * t, t), t)
            dq_ref[0, rows, :] += jnp.dot(ds, k, preferred_element_type=F32)

        @pl.when(i == j)
        def _():
            step(True)

        @pl.when(i > j)
        def _():
            step(False)

        @pl.when(i == n - 1)
        def _():
            dk_ref[0] = dk_s[...]
            dv_ref[0] = dv_s[...]

    qi = lambda h, j, i: (h, jnp.maximum(i, j), 0)
    return pl.pallas_call(
        body, name=name, grid=(nh, n, n),
        in_specs=[pl.BlockSpec((1, t, dk), qi),
                  pl.BlockSpec((1, t, dk), lambda h, j, i: (h, j, 0)),
                  pl.BlockSpec((1, t, dv), lambda h, j, i: (h, j, 0)),
                  pl.BlockSpec((t, dv), lambda h, j, i: (jnp.maximum(i, j), h)),
                  pl.BlockSpec((1, t, 1), qi), pl.BlockSpec((1, t, 1), qi)],
        out_specs=[pl.BlockSpec((1, s, dk), lambda h, j, i: (h, 0, 0)),
                   pl.BlockSpec((1, t, dk), lambda h, j, i: (h, j, 0)),
                   pl.BlockSpec((1, t, dv), lambda h, j, i: (h, j, 0))],
        out_shape=[_sds((nh, s, dk), F32), _sds((nh, s, dk), F32), _sds((nh, s, dv), F32)],
        scratch_shapes=[pltpu.VMEM((t, dk), F32), pltpu.VMEM((t, dv), F32)],
        compiler_params=_cp(("parallel", "arbitrary", "arbitrary")),
    )(qf, kf, vf, do, lse, delta)


def _qk_bwd(dqf, dkf, dvf, q_raw, kv, z, c_t, s1_t, s2_t, gqn, gqr, gkn, gkr, *, name):
    s = q_raw.shape[0]
    t = min(ROW_T, s)
    scale = 1.0 / math.sqrt(QK_DIM)

    def body(dq_ref, dk_ref, dv_ref, q_ref, kv_ref, kr_ref, c_ref, s1_ref, s2_ref,
             gqn_ref, gqr_ref, gkn_ref, gkr_ref, dqr_ref, dkv_ref, dkr_ref, ggq_ref, ggk_ref):
        _acc_init([ggq_ref, ggk_ref])
        c_v, s1_v, s2_v = c_ref[...], s1_ref[...], s2_ref[...]
        kr = kr_ref[...]
        kr_ss = _lanesum(kr * kr)
        dkr = jnp.zeros(kr.shape, F32)
        ggq_n = ggq_r = ggk_n = ggk_r = jnp.zeros((1, LANE), F32)

        def norm_bwd(n, r, rs, dyn, dyr, gn, gr):
            nh_, rh_ = n * rs, r * rs
            dnh, drh = dyn * gn, dyr * gr
            dot = (_lanesum(dnh * nh_) + _lanesum(drh * rh_)) * (1.0 / QK_DIM)
            return rs * (dnh - nh_ * dot), rs * (drh - rh_ * dot), _colsum(dyn * nh_), _colsum(dyr * rh_)

        for h in range(N_HEADS):
            n = q_ref[:, h * LANE:(h + 1) * LANE]
            r = q_ref[:, N_HEADS * LANE + h * LANE:N_HEADS * LANE + (h + 1) * LANE]
            rs = lax.rsqrt((_lanesum(n * n) + _lanesum(r * r)) * (1.0 / QK_DIM) + EPS)
            dyn = dq_ref[h, :, 0:LANE] * scale
            dyr = _rope_bwd(dq_ref[h, :, LANE:HEAD_PAD] * scale, c_v, s1_v, s2_v)
            dn, dr, g_n, g_r = norm_bwd(n, r, rs, dyn, dyr, gqn_ref[...], gqr_ref[...])
            dqr_ref[:, h * LANE:(h + 1) * LANE] = dn.astype(dqr_ref.dtype)
            dqr_ref[:, N_HEADS * LANE + h * LANE:N_HEADS * LANE + (h + 1) * LANE] = dr.astype(dqr_ref.dtype)
            ggq_n, ggq_r = ggq_n + g_n, ggq_r + g_r

            n = kv_ref[:, h * 2 * LANE:h * 2 * LANE + LANE]
            rs = lax.rsqrt((_lanesum(n * n) + kr_ss) * (1.0 / QK_DIM) + EPS)
            dyn = dk_ref[h, :, 0:LANE]
            dyr = _rope_bwd(dk_ref[h, :, LANE:HEAD_PAD], c_v, s1_v, s2_v)
            dn, dr, g_n, g_r = norm_bwd(n, kr, rs, dyn, dyr, gkn_ref[...], gkr_ref[...])
            dkv_ref[:, h * 2 * LANE:h * 2 * LANE + LANE] = dn.astype(dkv_ref.dtype)
            dkv_ref[:, h * 2 * LANE + LANE:(h + 1) * 2 * LANE] = dv_ref[h].astype(dkv_ref.dtype)
            dkr = dkr + dr
            ggk_n, ggk_r = ggk_n + g_n, ggk_r + g_r

        dkr_ref[...] = dkr.astype(dkr_ref.dtype)
        ggq_ref[:, 0:LANE] += ggq_n
        ggq_ref[:, LANE:] += ggq_r
        ggk_ref[:, 0:LANE] += ggk_n
        ggk_ref[:, LANE:] += ggk_r

    hspec = lambda w: pl.BlockSpec((N_HEADS, t, w), lambda i: (0, i, 0))
    wide = 2 * N_HEADS * LANE
    return pl.pallas_call(
        body, name=name, grid=(s // t,),
        in_specs=[hspec(HEAD_PAD), hspec(HEAD_PAD), hspec(V_DIM), _rowspec(t, wide), _rowspec(t, wide),
                  _rowspec(t, LANE, SEG_KR[0] // LANE), _rowspec(t, LANE), _rowspec(t, LANE), _rowspec(t, LANE),
                  _vecspec(LANE), _vecspec(LANE), _vecspec(LANE), _vecspec(LANE)],
        out_specs=[_rowspec(t, wide), _rowspec(t, wide), _rowspec(t, LANE), _vecspec(2 * LANE), _vecspec(2 * LANE)],
        out_shape=[_sds((s, wide), MXU_DTYPE), _sds((s, wide), MXU_DTYPE), _sds((s, LANE), MXU_DTYPE),
                   _sds((1, 2 * LANE), F32), _sds((1, 2 * LANE), F32)],
        compiler_params=_cp(("arbitrary",)),
    )(dqf, dkf, dvf, q_raw, kv, z, c_t, s1_t, s2_t, gqn, gqr, gkn, gkr)


def _lat_bwd(dqn, dkn, z, g_ql, g_kvl, *, name):
    s = z.shape[0]
    t = min(ROW_T, s)

    def body(dq_ref, dk_ref, ql_ref, kvl_ref, gq_ref, gk_ref, dql_ref, dkvl_ref, ggq_ref, ggk_ref):
        _acc_init([ggq_ref, ggk_ref])
        for d_ref, src, g_ref, dst, gg_ref in ((dq_ref, ql_ref, gq_ref, dql_ref, ggq_ref),
                                               (dk_ref, kvl_ref, gk_ref, dkvl_ref, ggk_ref)):
            v, dy = src[...], d_ref[...]
            r = lax.rsqrt(jnp.mean(v * v, axis=-1, keepdims=True) + EPS)
            vh = v * r
            dvh = dy * g_ref[...]
            dst[...] = (r * (dvh - vh * jnp.mean(dvh * vh, axis=-1, keepdims=True))).astype(dst.dtype)
            gg_ref[...] += _colsum(dy * vh)

    return pl.pallas_call(
        body, name=name, grid=(s // t,),
        in_specs=[_rowspec(t, Q_LORA), _rowspec(t, KV_LORA),
                  _rowspec(t, Q_LORA, SEG_QL[0] // Q_LORA), _rowspec(t, KV_LORA, SEG_KVL[0] // KV_LORA),
                  _vecspec(Q_LORA), _vecspec(KV_LORA)],
        out_specs=[_rowspec(t, Q_LORA), _rowspec(t, KV_LORA), _vecspec(Q_LORA), _vecspec(KV_LORA)],
        out_shape=[_sds((s, Q_LORA), MXU_DTYPE), _sds((s, KV_LORA), MXU_DTYPE),
                   _sds((1, Q_LORA), F32), _sds((1, KV_LORA), F32)],
        compiler_params=_cp(("arbitrary",)),
    )(dqn, dkn, z, z, g_ql, g_kvl)


def _prenorm_bwd(dh, x, gxo, g, sc1p, *, name):
    s, d = x.shape
    t = min(ROW_T, s)

    def body(dh_ref, x_ref, gx_ref, g_ref, sc_ref, dx_ref, dsh_ref, dsc_ref, gg_ref):
        _acc_init([dsh_ref, dsc_ref, gg_ref])
        xv, dhv = x_ref[...], dh_ref[...]
        r = lax.rsqrt(jnp.mean(xv * xv, axis=-1, keepdims=True) + EPS)
        xn = xv * r
        dsh_ref[...] += _colsum(dhv)
        dsc_ref[...] += _colsum(dhv * (xn * g_ref[...]))
        dm = dhv * sc_ref[...]
        gg_ref[...] += _colsum(dm * xn)
        dxn = dm * g_ref[...]
        dx_ref[...] = gx_ref[...] + r * (dxn - xn * jnp.mean(dxn * xn, axis=-1, keepdims=True))

    return pl.pallas_call(
        body, name=name, grid=(s // t,),
        in_specs=[_rowspec(t, d), _rowspec(t, d), _rowspec(t, d), _vecspec(d), _vecspec(d)],
        out_specs=[_rowspec(t, d), _vecspec(d), _vecspec(d), _vecspec(d)],
        out_shape=[_sds((s, d), F32), _sds((1, d), F32), _sds((1, d), F32), _sds((1, d), F32)],
        compiler_params=_cp(("arbitrary",)),
    )(dh, x, gxo, g, sc1p)


def _ada_fwd(c_all, ada_w, ada_b_cols, *, name):
    nl, d, cols = ada_w.shape

    def body(c_ref, w_ref, b_ref, o_ref):
        ca = _silu(c_ref[...]).astype(MXU_DTYPE)
        o_ref[0] = jnp.dot(ca, w_ref[0].astype(MXU_DTYPE), preferred_element_type=F32) + b_ref[0]

    return pl.pallas_call(
        body, name=name, grid=(nl,),
        in_specs=[pl.BlockSpec((N_DEV, d), lambda l: (0, 0)), pl.BlockSpec((1, d, cols), lambda l: (l, 0, 0)),
                  pl.BlockSpec((1, 1, cols), lambda l: (l, 0, 0))],
        out_specs=pl.BlockSpec((1, N_DEV, cols), lambda l: (l, 0, 0)),
        out_shape=_sds((nl, N_DEV, cols), F32),
        compiler_params=_cp(("parallel",)),
    )(c_all, ada_w, ada_b_cols)


def _ada_bwd(c_all_t, dmod_cols, *, name):
    nl, _, cols = dmod_cols.shape
    d = c_all_t.shape[0]

    def body(c_ref, dm_ref, o_ref):
        ca = _silu(c_ref[...]).astype(MXU_DTYPE)
        o_ref[0] = jnp.dot(ca, dm_ref[0].astype(MXU_DTYPE), preferred_element_type=F32)

    return pl.pallas_call(
        body, name=name, grid=(nl,),
        in_specs=[pl.BlockSpec((d, N_DEV), lambda l: (0, 0)), pl.BlockSpec((1, N_DEV, cols), lambda l: (l, 0, 0))],
        out_specs=pl.BlockSpec((1, d, cols), lambda l: (l, 0, 0)),
        out_shape=_sds((nl, d, cols), F32),
        compiler_params=_cp(("parallel",)),
    )(c_all_t, dmod_cols)


def _adamw(gparts, w, m, v, *, name):
    shape = w.shape
    cols = shape[-1]
    rows = w.size // cols
    npart = gparts.shape[0]
    gp2 = gparts.reshape(npart, rows, cols)
    w2, m2, v2 = (a.reshape(rows, cols) for a in (w, m, v))
    t = ROW_T if rows % ROW_T == 0 else rows

    def body(g_ref, w_ref, m_ref, v_ref, go_ref, d_ref, mo_ref, vo_ref):
        g = g_ref[0].astype(F32)
        for p in range(1, npart):
            g = g + g_ref[p].astype(F32)
        mn = ADAM_B1 * m_ref[...] + (1.0 - ADAM_B1) * g
        vn = ADAM_B2 * v_ref[...] + (1.0 - ADAM_B2) * (g * g)
        m_hat = mn / (1.0 - ADAM_B1 ** ADAM_STEP)
        v_hat = vn / (1.0 - ADAM_B2 ** ADAM_STEP)
        go_ref[...] = g
        d_ref[...] = -ADAM_LR * (m_hat / (jnp.sqrt(v_hat) + ADAM_EPS) + ADAM_WD * w_ref[...])
        mo_ref[...] = mn
        vo_ref[...] = vn

    spec = _rowspec(t, cols)
    outs = pl.pallas_call(
        body, name=name, grid=(rows // t,),
        in_specs=[pl.BlockSpec((npart, t, cols), lambda i: (0, i, 0)), spec, spec, spec],
        out_specs=[spec] * 4, out_shape=[_sds((rows, cols), F32)] * 4,
        compiler_params=_cp(("parallel",)),
    )(gp2, w2, m2, v2)
    return tuple(o.reshape(shape) for o in outs)


_ANY = pl.BlockSpec(memory_space=pl.ANY)


def _all_gather(block, *, name):
    def body(x_ref, out_ref, send_sems, recv_sems, local_sem):
        x, y, c = lax.axis_index("x"), lax.axis_index("y"), lax.axis_index("c")
        me, sibling = (x, y, c), (x, y, 1 - c)
        chips = [(1 - x, y), (x, 1 - y), (1 - x, 1 - y)]

        def slot(px, py, pc):
            return out_ref.at[4 * px + 2 * py + pc]

        def copy(k, blk, to, src=None):
            return pltpu.make_async_remote_copy(
                src_ref=slot(*blk) if src is None else src, dst_ref=slot(*blk),
                send_sem=send_sems.at[k], recv_sem=recv_sems.at[k], device_id=to, device_id_type=MESH_ID)

        mine = pltpu.make_async_copy(x_ref, slot(*me), local_sem)
        mine.start()
        first = [copy(0, me, sibling, src=x_ref)]
        first += [copy(1 + j, me, (*chip, c), src=x_ref) for j, chip in enumerate(chips)]
        for cp in first:
            cp.start()
        passed = [copy(4 + j, (*chip, c), sibling) for j, chip in enumerate(chips)]
        for j, chip in enumerate(chips):
            copy(1 + j, (*chip, c), me).wait_recv()
            passed[j].start()
        copy(0, sibling, me).wait_recv()
        for j, chip in enumerate(chips):
            copy(4 + j, (*chip, 1 - c), me).wait_recv()
        for cp in first + passed:
            cp.wait_send()
        mine.wait()

    return pl.pallas_call(
        body, name=name, in_specs=[_ANY], out_specs=_ANY,
        out_shape=_sds((N_DEV,) + block.shape, block.dtype),
        scratch_shapes=[pltpu.SemaphoreType.DMA((7,)), pltpu.SemaphoreType.DMA((7,)), pltpu.SemaphoreType.DMA],
    )(block)


def _all_to_all(parts, *, name):
    def body(in_ref, out_ref, send_sems, recv_sems, local_sem):
        x, y, c = lax.axis_index("x"), lax.axis_index("y"), lax.axis_index("c")
        me = 4 * x + 2 * y + c
        mine = pltpu.make_async_copy(in_ref.at[me], out_ref.at[me], local_sem)
        mine.start()
        copies = []
        for k in range(1, N_DEV):
            px = 1 - x if k & 4 else x
            py = 1 - y if k & 2 else y
            pc = 1 - c if k & 1 else c
            cp = pltpu.make_async_remote_copy(
                src_ref=in_ref.at[4 * px + 2 * py + pc], dst_ref=out_ref.at[me],
                send_sem=send_sems.at[k - 1], recv_sem=recv_sems.at[k - 1],
                device_id=(px, py, pc), device_id_type=MESH_ID)
            cp.start()
            copies.append(cp)
        for cp in copies:
            cp.wait()
        mine.wait()

    return pl.pallas_call(
        body, name=name, in_specs=[_ANY], out_specs=_ANY,
        out_shape=_sds(parts.shape, parts.dtype),
        scratch_shapes=[pltpu.SemaphoreType.DMA((7,)), pltpu.SemaphoreType.DMA((7,)), pltpu.SemaphoreType.DMA],
    )(parts)


def _cols_to_shards(a):
    r, n = a.shape
    return a.reshape(r, N_DEV, n // N_DEV).transpose(1, 0, 2)


def _shards_to_cols(a):
    nd, r, w = a.shape
    return a.transpose(1, 0, 2).reshape(r, nd * w)


def _win_permute(w_in):
    o_ql, o_kvl, o_kr, o_mg = 0, Q_LORA, Q_LORA + KV_LORA, Q_LORA + KV_LORA + ROPE
    o_ci = o_mg + D_MLA
    o_cg = o_ci + 2 * D_CONV
    seg = lambda o, n: w_in[:, o:o + n]
    pad = jnp.zeros((w_in.shape[0], LANE - ROPE), w_in.dtype)
    return jnp.concatenate([seg(o_ci, 2 * D_CONV), seg(o_mg, D_MLA), seg(o_cg, D_CONV), seg(o_ql, Q_LORA),
                            seg(o_kvl, KV_LORA), seg(o_kr, ROPE), pad], axis=1)


def _win_unpermute(g):
    seg = lambda s, n=None: g[:, s[0]:s[0] + (s[1] if n is None else n)]
    return jnp.concatenate([seg(SEG_QL), seg(SEG_KVL), seg(SEG_KR, ROPE), seg(SEG_MG), seg(SEG_CI), seg(SEG_CG)], axis=1)


def _qup_permute(w):
    w3 = w.reshape(w.shape[0], N_HEADS, QK_DIM)
    nope = w3[:, :, :NOPE].reshape(w.shape[0], N_HEADS * NOPE)
    rope = jnp.pad(w3[:, :, NOPE:], ((0, 0), (0, 0), (0, LANE - ROPE))).reshape(w.shape[0], N_HEADS * LANE)
    return jnp.concatenate([nope, rope], axis=1)


def _qup_unpermute(g):
    r = g.shape[0]
    nope = g[:, :N_HEADS * NOPE].reshape(r, N_HEADS, NOPE)
    rope = g[:, N_HEADS * NOPE:].reshape(r, N_HEADS, LANE)[:, :, :ROPE]
    return jnp.concatenate([nope, rope], axis=2).reshape(r, N_HEADS * QK_DIM)


def _norm_tiles(g):
    return g[:NOPE].reshape(1, LANE), jnp.pad(g[NOPE:], (0, LANE - ROPE)).reshape(1, LANE)


def _norm_untile(gt):
    return jnp.concatenate([gt[0, :NOPE], gt[0, LANE:LANE + ROPE]])


def _rope_tiles(positions):
    inv_freq = 1.0 / (ROPE_THETA ** (jnp.arange(0, ROPE, 2, dtype=F32) / ROPE))
    ang = positions.astype(F32)[:, None] * inv_freq
    cos, sin = jnp.cos(ang), jnp.sin(ang)
    zq = jnp.zeros_like(cos)
    c_t = jnp.concatenate([cos, cos, zq, zq], axis=1)
    s1_t = jnp.concatenate([-sin, zq, zq, zq], axis=1)
    s2_t = jnp.concatenate([zq, sin, zq, zq], axis=1)
    return c_t, s1_t, s2_t


_BIG = ("w_in", "w_q_up", "w_kv_up", "w_pw", "w_out")
_COL_SHARDED = ("w_in", "w_q_up", "w_kv_up")


def _pack_rows(arrs):
    return jnp.concatenate([a.reshape(-1, LANE) for a in arrs], axis=0)


def _unpack_rows(buf, shapes):
    out, r0 = [], 0
    lead = buf.shape[:-2]
    for shp in shapes:
        n = math.prod(shp) // LANE
        out.append(buf[..., r0:r0 + n, :].reshape(lead + tuple(shp)))
        r0 += n
    return out


_SMALL = (("dmod", 3 * D_MODEL), ("norm_g", D_MODEL), ("q_lat_g", Q_LORA), ("kv_lat_g", KV_LORA),
          ("q_norm_g", 2 * LANE), ("k_norm_g", 2 * LANE), ("glu_b", 2 * D_CONV), ("dw_w", HALO * D_CONV),
          ("dw_b", D_CONV), ("conv_ln_g", D_CONV), ("conv_ln_b", D_CONV), ("b_pw", D_CONV))


def _layer_fwd(x, p, rope, l):
    n = lambda s: f"{s}_l{l}"
    c_t, s1_t, s2_t = rope
    h = _prenorm(x, p["norm_g"], p["shift"], p["sc1p"], name=n("prenorm"))
    z = _mm(h, p["w_in"], name=n("in_proj"), tn=IN_PAD // 3, n_outer=True)
    qn, kn = _lat_norm(z, p["q_lat_g"], p["kv_lat_g"], name=n("lat_norm"))
    q_raw = _mm(qn, p["w_q_up"], name=n("q_up"), tn=1024)
    kv = _mm(kn, p["w_kv_up"], name=n("kv_up"), tn=1024)
    qf, kf, vf = _qk_prep(q_raw, kv, z, c_t, s1_t, s2_t, *p["qk_tiles"], name=n("qk_prep"))
    o, lse = _flash_fwd(qf, kf, vf, name=n("flash_fwd"))
    u1, u3 = _conv_fwd(z, p["glu_b"], p["dw_w"], p["dw_b"], p["conv_ln_g"], p["conv_ln_b"], name=n("conv_fwd"))
    u4m = _mm(u3, p["w_pw"], name=n("pw"), tn=1024)
    cat = _gate_cat(o, z, u4m, p["b_pw"], name=n("gate_cat"))
    y = _mm(cat, p["w_out"], name=n("out_proj"), tn=1024)
    x_next = _residual(x, y, p["gate"], name=n("residual"))
    saved = dict(x=x, h=h, z=z, qn=qn, kn=kn, q_raw=q_raw, kv=kv, qf=qf, kf=kf, vf=vf, o=o, lse=lse,
                 u1=u1, u3=u3, u4m=u4m, cat=cat, y=y)
    return x_next, saved


def _layer_bwd(gxo, p, sv, rope, l):
    n = lambda s: f"{s}_l{l}"
    c_t, s1_t, s2_t = rope
    z = sv["z"]
    dy, dgate = _out_bwd(gxo, sv["y"], p["gate"], name=n("out_bwd"))
    g_w_out = _mm(sv["cat"], dy, ta=True, name=n("g_w_out"), tm=1024, tn=1024, tk=512)
    dcat = _mm(dy, p["w_out"], tb=True, name=n("d_cat"), tn=1024)
    do, delta, dmg, du4, dcg, g_b_pw = _gate_bwd(dcat, sv["o"], z, sv["u4m"], p["b_pw"], name=n("gate_bwd"))
    g_w_pw = _mm(sv["u3"], du4, ta=True, name=n("g_w_pw"), tm=1024, tn=1024, tk=512)
    du3 = _mm(du4, p["w_pw"], tb=True, name=n("d_u3"), tn=1024)
    dci, g_ln_g, g_ln_b, g_dw_b, g_glu_b, g_dw_w = _conv_bwd(
        du3, sv["u1"], z, p["glu_b"], p["dw_w"], p["conv_ln_g"], p["conv_ln_b"], name=n("conv_bwd"))
    dqf, dkf, dvf = _flash_bwd(sv["qf"], sv["kf"], sv["vf"], do, sv["lse"], delta, name=n("flash_bwd"))
    dq_raw, dkv, dkr, g_qn, g_kn = _qk_bwd(dqf, dkf, dvf, sv["q_raw"], sv["kv"], z, c_t, s1_t, s2_t,
                                            *p["qk_tiles"], name=n("qk_bwd"))
    g_w_q_up = _mm(sv["qn"], dq_raw, ta=True, name=n("g_w_q_up"), tm=512, tn=1024, tk=512)
    dqn = _mm(dq_raw, p["w_q_up"], tb=True, name=n("d_qn"))
    g_w_kv_up = _mm(sv["kn"], dkv, ta=True, name=n("g_w_kv_up"), tm=256, tn=1024, tk=512)
    dkn = _mm(dkv, p["w_kv_up"], tb=True, name=n("d_kn"))
    dql, dkvl, g_ql, g_kvl = _lat_bwd(dqn, dkn, z, p["q_lat_g"], p["kv_lat_g"], name=n("lat_bwd"))
    dz = jnp.concatenate([dci, dmg, dcg, dql, dkvl, dkr], axis=1)
    g_w_in = _mm(sv["h"], dz, ta=True, name=n("g_w_in"), tm=1024, tn=IN_PAD // 3, tk=512)
    dh = _mm(dz, p["w_in"], tb=True, name=n("d_h"), tn=1024, tk=IN_PAD // 3)
    dx, dshift, dscale, g_norm = _prenorm_bwd(dh, sv["x"], gxo, p["norm_g"], p["sc1p"], name=n("prenorm_bwd"))
    big = dict(w_in=g_w_in, w_q_up=g_w_q_up, w_kv_up=g_w_kv_up, w_pw=g_w_pw, w_out=g_w_out)
    small = dict(dmod=jnp.concatenate([dshift, dscale, dgate], axis=1), norm_g=g_norm, q_lat_g=g_ql, kv_lat_g=g_kvl,
                 q_norm_g=g_qn, k_norm_g=g_kn, glu_b=g_glu_b, dw_w=g_dw_w, dw_b=g_dw_b,
                 conv_ln_g=g_ln_g, conv_ln_b=g_ln_b, b_pw=g_b_pw)
    return dx, big, small


def _layer_params(l, full, mod_l, small):
    d = D_MODEL
    row = lambda a: a.reshape(1, -1)
    shift, scale, gate = mod_l[:, :d], mod_l[:, d:2 * d], mod_l[:, 2 * d:]
    dw_w = jnp.pad(full["dw_w"][l], ((0, HALO - CONV_K), (0, 0)))
    return dict(
        shift=shift, sc1p=1.0 + scale, gate=gate, norm_g=row(small["norm_g"][l]),
        w_in=full["w_in"][l], w_q_up=full["w_q_up"][l], w_kv_up=full["w_kv_up"][l],
        w_pw=full["w_pw"][l], w_out=full["w_out"][l], dw_w=dw_w,
        q_lat_g=row(small["q_lat_g"][l]), kv_lat_g=row(small["kv_lat_g"][l]),
        qk_tiles=_norm_tiles(small["q_norm_g"][l]) + _norm_tiles(small["k_norm_g"][l]),
        glu_b=row(small["glu_b"][l]), dw_b=row(small["dw_b"][l]), conv_ln_g=row(small["conv_ln_g"][l]),
        conv_ln_b=row(small["conv_ln_b"][l]), b_pw=row(small["b_pw"][l]))


def kernel(x, c, positions, ada_w, ada_b, norm_g, w_in, q_lat_g, w_q_up, kv_lat_g, w_kv_up, q_norm_g, k_norm_g, glu_b, dw_w, dw_b, conv_ln_g, conv_ln_b, w_pw, b_pw, w_out, loss_target, m_ada_w, m_ada_b, m_norm_g, m_w_in, m_q_lat_g, m_w_q_up, m_kv_lat_g, m_w_kv_up, m_q_norm_g, m_k_norm_g, m_glu_b, m_dw_w, m_dw_b, m_conv_ln_g, m_conv_ln_b, m_w_pw, m_b_pw, m_w_out, v_ada_w, v_ada_b, v_norm_g, v_w_in, v_q_lat_g, v_w_q_up, v_kv_lat_g, v_w_kv_up, v_q_norm_g, v_k_norm_g, v_glu_b, v_dw_w, v_dw_b, v_conv_ln_g, v_conv_ln_b, v_w_pw, v_b_pw, v_w_out):
    names = ("ada_w", "ada_b", "norm_g", "w_in", "q_lat_g", "w_q_up", "kv_lat_g", "w_kv_up", "q_norm_g",
             "k_norm_g", "glu_b", "dw_w", "dw_b", "conv_ln_g", "conv_ln_b", "w_pw", "b_pw", "w_out")
    w_loc = dict(zip(names, (ada_w, ada_b, norm_g, w_in, q_lat_g, w_q_up, kv_lat_g, w_kv_up, q_norm_g, k_norm_g,
                             glu_b, dw_w, dw_b, conv_ln_g, conv_ln_b, w_pw, b_pw, w_out)))
    m_loc = dict(zip(names, (m_ada_w, m_ada_b, m_norm_g, m_w_in, m_q_lat_g, m_w_q_up, m_kv_lat_g, m_w_kv_up,
                             m_q_norm_g, m_k_norm_g, m_glu_b, m_dw_w, m_dw_b, m_conv_ln_g, m_conv_ln_b, m_w_pw,
                             m_b_pw, m_w_out)))
    v_loc = dict(zip(names, (v_ada_w, v_ada_b, v_norm_g, v_w_in, v_q_lat_g, v_w_q_up, v_kv_lat_g, v_w_kv_up,
                             v_q_norm_g, v_k_norm_g, v_glu_b, v_dw_w, v_dw_b, v_conv_ln_g, v_conv_ln_b, v_w_pw,
                             v_b_pw, v_w_out)))
    nl, d = N_LAYERS, D_MODEL
    me = 4 * lax.axis_index("x") + 2 * lax.axis_index("y") + lax.axis_index("c")
    x2, tgt = x[0], loss_target[0]
    ada_cols = ada_w.shape[-1]

    c_all = _all_gather(c.reshape(d // LANE, LANE), name="gather_c").reshape(N_DEV, d)
    ada_b_cols = lax.dynamic_slice_in_dim(ada_b, me * ada_cols, ada_cols, axis=1).reshape(nl, 1, ada_cols)
    mod_cols = _ada_fwd(c_all, ada_w, ada_b_cols, name="ada_fwd")
    mod_all = _all_gather(mod_cols.reshape(nl * N_DEV * ada_cols // LANE, LANE), name="gather_mod")
    mod_all = mod_all.reshape(N_DEV, nl, N_DEV, ada_cols)
    mod_me = lax.dynamic_index_in_dim(mod_all, me, axis=2, keepdims=False)
    mod = mod_me.transpose(1, 0, 2).reshape(nl, 1, N_DEV * ada_cols)

    shard_shapes = [w_loc[k].shape[1:] for k in _BIG]
    pk = jnp.stack([_pack_rows([w_loc[k][l].astype(WIRE_DTYPE) for k in _BIG]) for l in range(nl)])
    pk_all = _all_gather(pk, name="gather_weights")
    parts = dict(zip(_BIG, _unpack_rows(pk_all, shard_shapes)))
    dw_pad = jnp.pad(dw_w, ((0, 0), (0, HALO - CONV_K), (0, 0)))
    dw_all = _all_gather(dw_pad.reshape(nl * HALO, LANE), name="gather_dw").reshape(N_DEV, nl, HALO, LANE)
    full = dict(
        w_in=[_win_permute(_shards_to_cols(parts["w_in"][:, l])) for l in range(nl)],
        w_q_up=[_qup_permute(_shards_to_cols(parts["w_q_up"][:, l])) for l in range(nl)],
        w_kv_up=[_shards_to_cols(parts["w_kv_up"][:, l]) for l in range(nl)],
        w_pw=[parts["w_pw"][:, l].reshape(D_CONV, D_CONV) for l in range(nl)],
        w_out=[parts["w_out"][:, l].reshape(D_MLA + D_CONV, d) for l in range(nl)],
        dw_w=[_shards_to_cols(dw_all[:, l])[:CONV_K] for l in range(nl)])
    small_in = dict(norm_g=norm_g, q_lat_g=q_lat_g, kv_lat_g=kv_lat_g, q_norm_g=q_norm_g, k_norm_g=k_norm_g,
                    glu_b=glu_b, dw_b=dw_b, conv_ln_g=conv_ln_g, conv_ln_b=conv_ln_b, b_pw=b_pw)
    params = [_layer_params(l, full, mod[l], small_in) for l in range(nl)]
    rope = _rope_tiles(positions[0])

    xs, saved = x2, []
    for l in range(nl):
        xs, sv = _layer_fwd(xs, params[l], rope, l)
        saved.append(sv)
    gx, loss_part = _loss_head(xs, tgt, name="loss_head")
    loss = lax.psum(loss_part[0, 0], ("x", "y", "c"))
    big_g, small_g = [None] * nl, [None] * nl
    for l in reversed(range(nl)):
        gx, big_g[l], small_g[l] = _layer_bwd(gx, params[l], saved[l], rope, l)

    def shard_major(k, g):
        if k == "w_in":
            g = _win_unpermute(g)
        if k == "w_q_up":
            g = _qup_unpermute(g)
        if k in _COL_SHARDED:
            return _cols_to_shards(g)
        return g.reshape((N_DEV, g.shape[0] // N_DEV, g.shape[1]))

    gpk = jnp.stack([
        jnp.concatenate([shard_major(k, big_g[l][k]).astype(WIRE_DTYPE).reshape(N_DEV, -1, LANE) for k in _BIG], axis=1)
        for l in range(nl)], axis=1)
    g_recv = _all_to_all(gpk, name="scatter_grads")
    g_parts = dict(zip(_BIG, _unpack_rows(g_recv, shard_shapes)))

    tile = 8 * LANE
    padded = [(k, nn, -(-nn // tile) * tile) for k, nn in _SMALL]
    spk = jnp.concatenate([jnp.pad(small_g[l][k].reshape(-1), (0, np_ - nn)).reshape(-1, LANE)
                           for l in range(nl) for k, nn, np_ in padded], axis=0)
    s_all = _all_gather(spk, name="gather_small_grads")
    s_rows = sum(np_ for _, _, np_ in padded) // LANE
    s_all = s_all.reshape(N_DEV, nl, s_rows, LANE)
    s_parts = {k: a[..., :nn] for (k, nn, _), a in
               zip(padded, _unpack_rows(s_all, [(np_,) for _, _, np_ in padded]))}

    dmod_all = s_parts["dmod"]
    dmod_cols = lax.dynamic_slice_in_dim(dmod_all, me * ada_cols, ada_cols, axis=2).transpose(1, 0, 2)
    g_ada_w = _ada_bwd(c_all.T, dmod_cols, name="ada_bwd")
    gp = dict(g_parts)
    gp["ada_w"] = g_ada_w[None]
    gp["ada_b"] = dmod_all
    for k in ("norm_g", "q_lat_g", "kv_lat_g", "glu_b", "dw_b", "conv_ln_g", "conv_ln_b", "b_pw"):
        gp[k] = s_parts[k]
    for k in ("q_norm_g", "k_norm_g"):
        t = s_parts[k]
        gp[k] = jnp.concatenate([t[..., :NOPE], t[..., LANE:LANE + ROPE]], axis=-1)
    dw_g = s_parts["dw_w"].reshape(N_DEV, nl, HALO, D_CONV)[:, :, :CONV_K]
    gp["dw_w"] = lax.dynamic_slice_in_dim(dw_g, me * LANE, LANE, axis=3)

    res = {k: _adamw(gp[k], w_loc[k], m_loc[k], v_loc[k], name=f"adamw_{k}") for k in names}
    out = [loss, gx[None]]
    for idx in range(4):
        out += [res[k][idx] for k in names]
    return tuple(out)
```

```python
import functools
import math

import jax
import jax.numpy as jnp
from jax import lax
from jax.experimental import pallas as pl
from jax.experimental.pallas import tpu as pltpu

F32 = jnp.float32
MXU_DTYPE = jnp.bfloat16
WIRE_DTYPE = jnp.bfloat16

D_MODEL = 2048
N_LAYERS = 2
N_DEV = 8
N_HEADS = 8
NOPE = 128
ROPE = 64
V_DIM = 128
QK_DIM = NOPE + ROPE
Q_LORA = 512
KV_LORA = 256
D_MLA = N_HEADS * V_DIM
D_CONV = 1024
CONV_K = 31
ROPE_THETA = 10000.0
EPS = 1e-6
LANE = 128
HEAD_PAD = 2 * LANE
HALO = 32

SEG_CI = (0, 2 * D_CONV)
SEG_MG = (2 * D_CONV, D_MLA)
SEG_CG = (2 * D_CONV + D_MLA, D_CONV)
SEG_QL = (2 * D_CONV + D_MLA + D_CONV, Q_LORA)
SEG_KVL = (SEG_QL[0] + Q_LORA, KV_LORA)
SEG_KR = (SEG_KVL[0] + KV_LORA, LANE)
IN_PAD = SEG_KR[0] + LANE
IN_COLS = Q_LORA + KV_LORA + ROPE + D_MLA + 2 * D_CONV + D_CONV

ADAM_LR = 0.001
ADAM_B1 = 0.9
ADAM_B2 = 0.999
ADAM_EPS = 1e-08
ADAM_WD = 0.01
ADAM_STEP = 10

VMEM_LIMIT = 56 * 1024 * 1024
ATT_T = 512
ROW_T = 256
CONV_T = 128
MESH_ID = pl.DeviceIdType.MESH


def _cp(sem=None):
    kw = dict(vmem_limit_bytes=VMEM_LIMIT)
    if sem is not None:
        kw["dimension_semantics"] = sem
    return pltpu.CompilerParams(**kw)


def _sds(shape, dtype):
    return jax.ShapeDtypeStruct(shape, dtype)


def _silu(x):
    return x * jax.nn.sigmoid(x)


def _dsilu(x):
    s = jax.nn.sigmoid(x)
    return s * (1.0 + x * (1.0 - s))


def _rowspec(t, width, col=0):
    return pl.BlockSpec((t, width), lambda i: (i, col))


def _vecspec(width):
    return pl.BlockSpec((1, width), lambda i: (0, 0))


def _colsum(v):
    return jnp.sum(v, axis=0, keepdims=True)


def _mm(a, b, *, name, ta=False, tb=False, out_dtype=F32, tm=512, tn=512, tk=None, n_outer=False):
    if ta:
        kdim, m = a.shape
    else:
        m, kdim = a.shape
    if tb:
        n, k2 = b.shape
    else:
        k2, n = b.shape
    assert kdim == k2, (a.shape, b.shape)
    tm, tn = min(tm, m), min(tn, n)
    tk = kdim if tk is None else min(tk, kdim)
    assert m % tm == 0 and n % tn == 0 and kdim % tk == 0, (m, n, kdim, tm, tn, tk)
    nk = kdim // tk
    dims = (((0 if ta else 1,), (1 if tb else 0,)), ((), ()))

    def body(a_ref, b_ref, o_ref, *scratch):
        prod = lax.dot_general(a_ref[...].astype(MXU_DTYPE), b_ref[...].astype(MXU_DTYPE), dims,
                               preferred_element_type=F32)
        if nk == 1:
            o_ref[...] = prod.astype(o_ref.dtype)
        else:
            acc = scratch[0]
            k = pl.program_id(2)

            @pl.when(k == 0)
            def _():
                acc[...] = prod

            @pl.when(k > 0)
            def _():
                acc[...] += prod

            @pl.when(k == nk - 1)
            def _():
                o_ref[...] = acc[...].astype(o_ref.dtype)

    if n_outer:
        ij = lambda g0, g1: (g1, g0)
        grid = (n // tn, m // tm, nk)
    else:
        ij = lambda g0, g1: (g0, g1)
        grid = (m // tm, n // tn, nk)

    def a_map(g0, g1, k):
        i, _ = ij(g0, g1)
        return (k, i) if ta else (i, k)

    def b_map(g0, g1, k):
        _, j = ij(g0, g1)
        return (j, k) if tb else (k, j)

    def o_map(g0, g1, k):
        return ij(g0, g1)

    return pl.pallas_call(
        body, name=name, grid=grid,
        in_specs=[pl.BlockSpec((tk, tm) if ta else (tm, tk), a_map),
                  pl.BlockSpec((tn, tk) if tb else (tk, tn), b_map)],
        out_specs=pl.BlockSpec((tm, tn), o_map),
        out_shape=_sds((m, n), out_dtype),
        scratch_shapes=[pltpu.VMEM((tm, tn), F32)] if nk > 1 else [],
        compiler_params=_cp(("parallel", "parallel", "arbitrary")),
    )(a, b)


def _prenorm(x, g, shift, sc1p, *, name):
    s, d = x.shape
    t = min(ROW_T, s)

    def body(x_ref, g_ref, sh_ref, sc_ref, h_ref):
        xv = x_ref[...]
        r = lax.rsqrt(jnp.mean(xv * xv, axis=-1, keepdims=True) + EPS)
        h_ref[...] = ((xv * r) * g_ref[...] * sc_ref[...] + sh_ref[...]).astype(h_ref.dtype)

    return pl.pallas_call(
        body, name=name, grid=(s // t,),
        in_specs=[_rowspec(t, d), _vecspec(d), _vecspec(d), _vecspec(d)],
        out_specs=_rowspec(t, d), out_shape=_sds((s, d), MXU_DTYPE),
        compiler_params=_cp(("parallel",)),
    )(x, g, shift, sc1p)


def _lat_norm(z, g_ql, g_kvl, *, name):
    s = z.shape[0]
    t = min(ROW_T, s)

    def body(ql_ref, kvl_ref, gq_ref, gk_ref, qn_ref, kn_ref):
        for src, g_ref, dst in ((ql_ref, gq_ref, qn_ref), (kvl_ref, gk_ref, kn_ref)):
            v = src[...]
            r = lax.rsqrt(jnp.mean(v * v, axis=-1, keepdims=True) + EPS)
            dst[...] = ((v * r) * g_ref[...]).astype(dst.dtype)

    return pl.pallas_call(
        body, name=name, grid=(s // t,),
        in_specs=[_rowspec(t, Q_LORA, SEG_QL[0] // Q_LORA), _rowspec(t, KV_LORA, SEG_KVL[0] // KV_LORA),
                  _vecspec(Q_LORA), _vecspec(KV_LORA)],
        out_specs=[_rowspec(t, Q_LORA), _rowspec(t, KV_LORA)],
        out_shape=[_sds((s, Q_LORA), MXU_DTYPE), _sds((s, KV_LORA), MXU_DTYPE)],
        compiler_params=_cp(("parallel",)),
    )(z, z, g_ql, g_kvl)


def _rope_fwd(r, c_t, s1_t, s2_t):
    return r * c_t + pltpu.roll(r, LANE - ROPE // 2, 1) * s1_t + pltpu.roll(r, ROPE // 2, 1) * s2_t


def _rope_bwd(d, c_t, s1_t, s2_t):
    return d * c_t + pltpu.roll(d * s1_t, ROPE // 2, 1) + pltpu.roll(d * s2_t, LANE - ROPE // 2, 1)


def _lanesum(v):
    return jnp.sum(v, axis=-1, keepdims=True)


def _qk_prep(q_raw, kv, z, c_t, s1_t, s2_t, gqn, gqr, gkn, gkr, *, name):
    s = q_raw.shape[0]
    t = min(ROW_T, s)
    scale = 1.0 / math.sqrt(QK_DIM)

    def body(q_ref, kv_ref, kr_ref, c_ref, s1_ref, s2_ref, gqn_ref, gqr_ref, gkn_ref, gkr_ref,
             qf_ref, kf_ref, vf_ref):
        c_v, s1_v, s2_v = c_ref[...], s1_ref[...], s2_ref[...]
        kr = kr_ref[...]
        kr_ss = _lanesum(kr * kr)
        for h in range(N_HEADS):
            n = q_ref[:, h * LANE:(h + 1) * LANE]
            r = q_ref[:, N_HEADS * LANE + h * LANE:N_HEADS * LANE + (h + 1) * LANE]
            rs = lax.rsqrt((_lanesum(n * n) + _lanesum(r * r)) * (1.0 / QK_DIM) + EPS)
            qf_ref[h, :, 0:LANE] = (((n * rs) * gqn_ref[...]) * scale).astype(qf_ref.dtype)
            rr = _rope_fwd((r * rs) * gqr_ref[...], c_v, s1_v, s2_v)
            qf_ref[h, :, LANE:HEAD_PAD] = (rr * scale).astype(qf_ref.dtype)

            n = kv_ref[:, h * 2 * LANE:h * 2 * LANE + LANE]
            rs = lax.rsqrt((_lanesum(n * n) + kr_ss) * (1.0 / QK_DIM) + EPS)
            kf_ref[h, :, 0:LANE] = ((n * rs) * gkn_ref[...]).astype(kf_ref.dtype)
            kf_ref[h, :, LANE:HEAD_PAD] = _rope_fwd((kr * rs) * gkr_ref[...], c_v, s1_v, s2_v).astype(kf_ref.dtype)
            vf_ref[h, :, 0:V_DIM] = kv_ref[:, h * 2 * LANE + LANE:(h + 1) * 2 * LANE].astype(vf_ref.dtype)
            vf_ref[h, :, V_DIM:] = jnp.ones((t, V_DIM), vf_ref.dtype)

    hspec = lambda w: pl.BlockSpec((N_HEADS, t, w), lambda i: (0, i, 0))
    return pl.pallas_call(
        body, name=name, grid=(s // t,),
        in_specs=[_rowspec(t, 2 * N_HEADS * LANE), _rowspec(t, 2 * N_HEADS * LANE),
                  _rowspec(t, LANE, SEG_KR[0] // LANE),
                  _rowspec(t, LANE), _rowspec(t, LANE), _rowspec(t, LANE),
                  _vecspec(LANE), _vecspec(LANE), _vecspec(LANE), _vecspec(LANE)],
        out_specs=[hspec(HEAD_PAD), hspec(HEAD_PAD), hspec(2 * V_DIM)],
        out_shape=[_sds((N_HEADS, s, HEAD_PAD), MXU_DTYPE), _sds((N_HEADS, s, HEAD_PAD), MXU_DTYPE),
                   _sds((N_HEADS, s, 2 * V_DIM), MXU_DTYPE)],
        compiler_params=_cp(("parallel",)),
    )(q_raw, kv, z, c_t, s1_t, s2_t, gqn, gqr, gkn, gkr)


def _causal_mask(t):
    row = lax.broadcasted_iota(jnp.int32, (t, t), 0)
    col = lax.broadcasted_iota(jnp.int32, (t, t), 1)
    return col <= row


NEG = -1e30


def _flash_fwd(qf, kf, va, *, name):
    nh, s, dk = qf.shape
    dv = va.shape[-1] // 2
    t = min(ATT_T, s)
    n = s // t
    assert dv == LANE and t % LANE == 0

    def body(q_ref, k_ref, v_ref, o_ref, lse_ref, m_s, acc_s, s_buf):
        i = pl.program_id(1)
        m_s[...] = jnp.full(m_s.shape, NEG, F32)
        acc_s[...] = jnp.zeros(acc_s.shape, F32)
        q = q_ref[0]

        def rows_of(j):
            return pl.ds(pl.multiple_of(j * t, t), t)

        def scores(j):
            return lax.dot_general(q, k_ref[0, rows_of(j), :], (((1,), (1,)), ((), ())), preferred_element_type=F32)

        def consume(j, slot, masked):
            sc = s_buf[slot]
            if masked:
                sc = jnp.where(_causal_mask(t), sc, NEG)
            m_prev = m_s[...]
            m_new = jnp.maximum(m_prev, jnp.max(sc, axis=-1, keepdims=True))
            alpha = jnp.exp(m_prev - m_new)
            p = jnp.exp(sc - jnp.tile(m_new, (1, t // LANE)))
            acc_s[...] = jnp.tile(alpha, (1, 2)) * acc_s[...] + jnp.dot(
                p.astype(MXU_DTYPE), v_ref[0, rows_of(j), :], preferred_element_type=F32)
            m_s[...] = m_new

        s_buf[0] = scores(0)

        def pair(a, carry):
            s_buf[1] = scores(2 * a + 1)
            consume(2 * a, 0, False)
            s_buf[0] = scores(2 * a + 2)
            consume(2 * a + 1, 1, False)
            return carry

        lax.fori_loop(0, i // 2, pair, 0)

        @pl.when(i % 2 == 1)
        def _():
            s_buf[1] = scores(i)
            consume(i - 1, 0, False)
            consume(i, 1, True)

        @pl.when(i % 2 == 0)
        def _():
            consume(i, 0, True)

        den = acc_s[:, dv:]
        o_ref[...] = acc_s[:, :dv] / den
        lse_ref[0] = m_s[...] + jnp.log(den)

    return pl.pallas_call(
        body, name=name, grid=(nh, n),
        in_specs=[pl.BlockSpec((1, t, dk), lambda h, i: (h, i, 0)),
                  pl.BlockSpec((1, s, dk), lambda h, i: (h, 0, 0)),
                  pl.BlockSpec((1, s, 2 * dv), lambda h, i: (h, 0, 0))],
        out_specs=[pl.BlockSpec((t, dv), lambda h, i: (i, h)),
                   pl.BlockSpec((1, t, LANE), lambda h, i: (h, i, 0))],
        out_shape=[_sds((s, nh * dv), F32), _sds((nh, s, LANE), F32)],
        scratch_shapes=[pltpu.VMEM((t, LANE), F32), pltpu.VMEM((t, 2 * dv), F32), pltpu.VMEM((2, t, t), F32)],
        compiler_params=_cp(("parallel", "arbitrary")),
    )(qf, kf, va)


def _dw_taps(ext_ref, w_ref, row0, t_rows, lane0, lanes, first_off):
    acc = None
    for k in range(CONV_K):
        term = w_ref[k:k + 1, lane0:lane0 + lanes] * ext_ref[pl.ds(row0 + first_off + k, t_rows), lane0:lane0 + lanes]
        acc = term if acc is None else acc + term
    return acc


CONV_RC = 32
CONV_LC = 256


def _conv_fwd(z, glu_b, dw_w, dw_b, ln_g, ln_b, *, name):
    s = z.shape[0]
    t = min(CONV_T, s)
    c2 = 2 * D_CONV
    hb = t // HALO

    def body(zm_ref, zh_ref, gb_ref, w_ref, wb_ref, g_ref, b_ref, u1_ref, u3_ref, ext):
        i = pl.program_id(0)

        def glu(zv):
            ci = zv + gb_ref[...]
            return ci[:, :D_CONV] * jax.nn.sigmoid(ci[:, D_CONV:])

        ext[HALO:, :] = glu(zm_ref[...])
        ext[0:HALO, :] = jnp.where(i > 0, glu(zh_ref[...]), 0.0)
        for rc in range(0, t, CONV_RC):
            for lc in range(0, D_CONV, CONV_LC):
                acc = _dw_taps(ext, w_ref, rc, CONV_RC, lc, CONV_LC, HALO - (CONV_K - 1))
                u1_ref[rc:rc + CONV_RC, lc:lc + CONV_LC] = acc + wb_ref[:, lc:lc + CONV_LC]
        u1 = u1_ref[...]
        mu = jnp.mean(u1, axis=-1, keepdims=True)
        cen = u1 - mu
        var = jnp.mean(cen * cen, axis=-1, keepdims=True)
        u2 = (cen * lax.rsqrt(var + EPS)) * g_ref[...] + b_ref[...]
        u3_ref[...] = _silu(u2).astype(u3_ref.dtype)

    return pl.pallas_call(
        body, name=name, grid=(s // t,),
        in_specs=[_rowspec(t, c2), pl.BlockSpec((HALO, c2), lambda i: (jnp.maximum(i * hb - 1, 0), 0)),
                  _vecspec(c2), pl.BlockSpec((HALO, D_CONV), lambda i: (0, 0)), _vecspec(D_CONV),
                  _vecspec(D_CONV), _vecspec(D_CONV)],
        out_specs=[_rowspec(t, D_CONV), _rowspec(t, D_CONV)],
        out_shape=[_sds((s, D_CONV), F32), _sds((s, D_CONV), MXU_DTYPE)],
        scratch_shapes=[pltpu.VMEM((t + HALO, D_CONV), F32)],
        compiler_params=_cp(("parallel",)),
    )(z, z, glu_b, dw_w, dw_b, ln_g, ln_b)


def _gate_cat(o, z, u4m, b_pw, *, name):
    s = o.shape[0]
    t = min(ROW_T, s)

    def body(o_ref, mg_ref, u4_ref, cg_ref, b_ref, cat_ref):
        cat_ref[:, :D_MLA] = (o_ref[...] * _silu(mg_ref[...])).astype(cat_ref.dtype)
        cat_ref[:, D_MLA:] = ((u4_ref[...] + b_ref[...]) * _silu(cg_ref[...])).astype(cat_ref.dtype)

    return pl.pallas_call(
        body, name=name, grid=(s // t,),
        in_specs=[_rowspec(t, D_MLA), _rowspec(t, D_MLA, SEG_MG[0] // D_MLA), _rowspec(t, D_CONV),
                  _rowspec(t, D_CONV, SEG_CG[0] // D_CONV), _vecspec(D_CONV)],
        out_specs=_rowspec(t, D_MLA + D_CONV), out_shape=_sds((s, D_MLA + D_CONV), MXU_DTYPE),
        compiler_params=_cp(("parallel",)),
    )(o, z, u4m, z, b_pw)


def _residual(x, y, gate, *, name):
    s, d = x.shape
    t = min(ROW_T, s)

    def body(x_ref, y_ref, g_ref, o_ref):
        o_ref[...] = x_ref[...] + g_ref[...] * y_ref[...]

    return pl.pallas_call(
        body, name=name, grid=(s // t,),
        in_specs=[_rowspec(t, d), _rowspec(t, d), _vecspec(d)],
        out_specs=_rowspec(t, d), out_shape=_sds((s, d), F32),
        compiler_params=_cp(("parallel",)),
    )(x, y, gate)


def _loss_head(xf, target, *, name):
    s, d = xf.shape
    t = min(ROW_T, s)

    def body(x_ref, t_ref, gx_ref, loss_ref):
        @pl.when(pl.program_id(0) == 0)
        def _():
            loss_ref[...] = jnp.zeros(loss_ref.shape, F32)

        err = x_ref[...] - t_ref[...]
        gx_ref[...] = err * (1.0 / d)
        loss_ref[...] += 0.5 * jnp.sum(_lanesum(err * err) * (1.0 / d), axis=0, keepdims=True)

    return pl.pallas_call(
        body, name=name, grid=(s // t,),
        in_specs=[_rowspec(t, d), _rowspec(t, d)],
        out_specs=[_rowspec(t, d), pl.BlockSpec((1, 1), lambda i: (0, 0))],
        out_shape=[_sds((s, d), F32), _sds((1, 1), F32)],
        compiler_params=_cp(("arbitrary",)),
    )(xf, target)


def _acc_init(refs):
    @pl.when(pl.program_id(0) == 0)
    def _():
        for r in refs:
            r[...] = jnp.zeros(r.shape, r.dtype)


def _out_bwd(gxo, y, gate, *, name):
    s, d = gxo.shape
    t = min(ROW_T, s)

    def body(g_ref, y_ref, gate_ref, dy_ref, dgate_ref):
        _acc_init([dgate_ref])
        gv = g_ref[...]
        dy_ref[...] = (gv * gate_ref[...]).astype(dy_ref.dtype)
        dgate_ref[...] += _colsum(gv * y_ref[...])

    return pl.pallas_call(
        body, name=name, grid=(s // t,),
        in_specs=[_rowspec(t, d), _rowspec(t, d), _vecspec(d)],
        out_specs=[_rowspec(t, d), _vecspec(d)],
        out_shape=[_sds((s, d), MXU_DTYPE), _sds((1, d), F32)],
        compiler_params=_cp(("arbitrary",)),
    )(gxo, y, gate)


def _gate_bwd(dcat, o, z, u4m, b_pw, *, name):
    s = o.shape[0]
    t = min(ROW_T, s)

    def body(dm_ref, dc_ref, o_ref, mg_ref, u4_ref, cg_ref, b_ref,
             do_ref, delta_ref, dmg_ref, du4_ref, dcg_ref, gb_ref):
        _acc_init([gb_ref])
        dm, ov, mg = dm_ref[...], o_ref[...], mg_ref[...]
        do = dm * _silu(mg)
        do_ref[...] = do.astype(do_ref.dtype)
        dmg_ref[...] = (dm * ov * _dsilu(mg)).astype(dmg_ref.dtype)
        prod = do * ov
        for h in range(N_HEADS):
            delta_ref[h] = _lanesum(prod[:, h * V_DIM:(h + 1) * V_DIM])
        dc, cg = dc_ref[...], cg_ref[...]
        du4 = dc * _silu(cg)
        du4_ref[...] = du4.astype(du4_ref.dtype)
        dcg_ref[...] = (dc * (u4_ref[...] + b_ref[...]) * _dsilu(cg)).astype(dcg_ref.dtype)
        gb_ref[...] += _colsum(du4)

    return pl.pallas_call(
        body, name=name, grid=(s // t,),
        in_specs=[_rowspec(t, D_MLA, 0), _rowspec(t, D_CONV, 1), _rowspec(t, D_MLA),
                  _rowspec(t, D_MLA, SEG_MG[0] // D_MLA), _rowspec(t, D_CONV),
                  _rowspec(t, D_CONV, SEG_CG[0] // D_CONV), _vecspec(D_CONV)],
        out_specs=[_rowspec(t, D_MLA), pl.BlockSpec((N_HEADS, t, 1), lambda i: (0, i, 0)),
                   _rowspec(t, D_MLA), _rowspec(t, D_CONV), _rowspec(t, D_CONV), _vecspec(D_CONV)],
        out_shape=[_sds((s, D_MLA), MXU_DTYPE), _sds((N_HEADS, s, 1), F32), _sds((s, D_MLA), MXU_DTYPE),
                   _sds((s, D_CONV), MXU_DTYPE), _sds((s, D_CONV), MXU_DTYPE), _sds((1, D_CONV), F32)],
        compiler_params=_cp(("arbitrary",)),
    )(dcat, dcat, o, z, u4m, z, b_pw)


def _conv_bwd(du3, u1, z, glu_b, dw_w, ln_g, ln_b, *, name):
    s = z.shape[0]
    t = min(CONV_T, s)
    c2 = 2 * D_CONV
    hb = t // HALO
    n_blk = s // t
    last_halo = s // HALO - 1

    def body(d3m_ref, d3h_ref, u1m_ref, u1h_ref, zm_ref, zh_ref, gb_ref, w_ref, g_ref, b_ref,
             dci_ref, gg_ref, gbn_ref, gwb_ref, ggb_ref, gw_ref, dext, uext, du0_s, gw_acc):
        i = pl.program_id(0)
        _acc_init([gg_ref, gbn_ref, gwb_ref, ggb_ref, gw_acc])

        def ln_bwd(d3, u1v):
            mu = jnp.mean(u1v, axis=-1, keepdims=True)
            cen = u1v - mu
            rstd = lax.rsqrt(jnp.mean(cen * cen, axis=-1, keepdims=True) + EPS)
            uh = cen * rstd
            d2 = d3 * _dsilu(uh * g_ref[...] + b_ref[...])
            dh = d2 * g_ref[...]
            d1 = rstd * (dh - jnp.mean(dh, axis=-1, keepdims=True) - uh * jnp.mean(dh * uh, axis=-1, keepdims=True))
            return d1, d2, uh

        d1, d2, uh = ln_bwd(d3m_ref[...], u1m_ref[...])
        gg_ref[...] += _colsum(d2 * uh)
        gbn_ref[...] += _colsum(d2)
        gwb_ref[...] += _colsum(d1)
        dext[0:t, :] = d1
        d1h, _, _ = ln_bwd(d3h_ref[...], u1h_ref[...])
        dext[t:, :] = jnp.where(i < n_blk - 1, d1h, 0.0)

        def glu_parts(zv):
            ci = zv + gb_ref[...]
            return ci[:, :D_CONV], jax.nn.sigmoid(ci[:, D_CONV:])

        val, sg = glu_parts(zm_ref[...])
        uext[HALO:, :] = val * sg
        valh, sgh = glu_parts(zh_ref[...])
        uext[0:HALO, :] = jnp.where(i > 0, valh * sgh, 0.0)

        for rc in range(0, t, CONV_RC):
            for lc in range(0, D_CONV, CONV_LC):
                acc = None
                dchunk = dext[rc:rc + CONV_RC, lc:lc + CONV_LC]
                for k in range(CONV_K):
                    term = w_ref[k:k + 1, lc:lc + CONV_LC] * dext[pl.ds(rc + (CONV_K - 1) - k, CONV_RC), lc:lc + CONV_LC]
                    acc = term if acc is None else acc + term
                    pr = dchunk * uext[pl.ds(rc + HALO - (CONV_K - 1) + k, CONV_RC), lc:lc + CONV_LC]
                    part = pr[0:8]
                    for r8 in range(8, CONV_RC, 8):
                        part = part + pr[r8:r8 + 8]
                    gw_acc[k, :, lc:lc + CONV_LC] += part
                du0_s[rc:rc + CONV_RC, lc:lc + CONV_LC] = acc

        du0 = du0_s[...]
        dval = du0 * sg
        dgt = du0 * val * sg * (1.0 - sg)
        dci_ref[:, :D_CONV] = dval.astype(dci_ref.dtype)
        dci_ref[:, D_CONV:] = dgt.astype(dci_ref.dtype)
        ggb_ref[:, :D_CONV] += _colsum(dval)
        ggb_ref[:, D_CONV:] += _colsum(dgt)

        @pl.when(i == n_blk - 1)
        def _():
            gw_ref[...] = jnp.sum(gw_acc[...], axis=1)

    halo_next = lambda w: pl.BlockSpec((HALO, w), lambda i: (jnp.minimum((i + 1) * hb, last_halo), 0))
    return pl.pallas_call(
        body, name=name, grid=(n_blk,),
        in_specs=[_rowspec(t, D_CONV), halo_next(D_CONV), _rowspec(t, D_CONV), halo_next(D_CONV),
                  _rowspec(t, c2), pl.BlockSpec((HALO, c2), lambda i: (jnp.maximum(i * hb - 1, 0), 0)),
                  _vecspec(c2), pl.BlockSpec((HALO, D_CONV), lambda i: (0, 0)), _vecspec(D_CONV), _vecspec(D_CONV)],
        out_specs=[_rowspec(t, c2), _vecspec(D_CONV), _vecspec(D_CONV), _vecspec(D_CONV), _vecspec(c2),
                   pl.BlockSpec((HALO, D_CONV), lambda i: (0, 0))],
        out_shape=[_sds((s, c2), MXU_DTYPE), _sds((1, D_CONV), F32), _sds((1, D_CONV), F32), _sds((1, D_CONV), F32),
                   _sds((1, c2), F32), _sds((HALO, D_CONV), F32)],
        scratch_shapes=[pltpu.VMEM((t + HALO, D_CONV), F32), pltpu.VMEM((t + HALO, D_CONV), F32),
                        pltpu.VMEM((t, D_CONV), F32), pltpu.VMEM((HALO, 8, D_CONV), F32)],
        compiler_params=_cp(("arbitrary",)),
    )(du3, du3, u1, u1, z, z, glu_b, dw_w, ln_g, ln_b)


def _flash_bwd(qf, kf, kft, va, do, lse_t, delta_t, *, name):
    nh, s, dk = qf.shape
    dv = va.shape[-1] // 2
    t = min(ATT_T, s)
    n = s // t
    nt = (((1,), (1,)), ((), ()))

    def body(q_ref, do_ref, lse_ref, dl_ref, k_ref, kt_ref, v_ref, dqt_ref, dk_ref, dv_ref,
             dk_s, dv_s, st_buf, dpt_buf):
        j = pl.program_id(1)

        @pl.when(j == 0)
        def _():
            dqt_ref[...] = jnp.zeros(dqt_ref.shape, F32)

        dk_s[...] = jnp.zeros(dk_s.shape, F32)
        dv_s[...] = jnp.zeros(dv_s.shape, F32)
        k, kt, v = k_ref[0], kt_ref[0], v_ref[0]
        n_un = n - 1 - j

        def rows_of(b):
            return pl.ds(pl.multiple_of((n - 1 - b) * t, t), t)

        def produce(b, slot):
            rows = rows_of(b)
            st_buf[slot] = lax.dot_general(k, q_ref[0, rows, :], nt, preferred_element_type=F32)
            dpt_buf[slot] = lax.dot_general(v, do_ref[rows, :], nt, preferred_element_type=F32)

        def consume(b, slot, masked):
            i = n - 1 - b
            rows = rows_of(b)
            q, dov = q_ref[0, rows, :], do_ref[rows, :]
            pt = jnp.exp(st_buf[slot] - lse_ref[0, i])
            if masked:
                key = lax.broadcasted_iota(jnp.int32, (t, t), 0)
                qry = lax.broadcasted_iota(jnp.int32, (t, t), 1)
                pt = jnp.where(key <= qry, pt, 0.0)
            dv_s[...] += jnp.dot(pt.astype(MXU_DTYPE), dov, preferred_element_type=F32)
            dst = (pt * (dpt_buf[slot] - dl_ref[0, i])).astype(MXU_DTYPE)
            dk_s[...] += jnp.dot(dst, q, preferred_element_type=F32)
            dqt_ref[0, i] += jnp.dot(kt, dst, preferred_element_type=F32)

        produce(0, 0)

        def pair(a, carry):
            produce(2 * a + 1, 1)
            consume(2 * a, 0, False)
            produce(2 * a + 2, 0)
            consume(2 * a + 1, 1, False)
            return carry

        lax.fori_loop(0, n_un // 2, pair, 0)

        @pl.when(n_un % 2 == 1)
        def _():
            produce(n_un, 1)
            consume(n_un - 1, 0, False)
            consume(n_un, 1, True)

        @pl.when(n_un % 2 == 0)
        def _():
            consume(n_un, 0, True)

        dk_ref[0] = dk_s[...]
        dv_ref[0] = dv_s[...]

    head = lambda h, j: (h, 0, 0)
    rowv = pl.BlockSpec((1, n, 1, t), lambda h, j: (h, 0, 0, 0))
    return pl.pallas_call(
        body, name=name, grid=(nh, n),
        in_specs=[pl.BlockSpec((1, s, dk), head),
                  pl.BlockSpec((s, dv), lambda h, j: (0, h)),
                  rowv, rowv,
                  pl.BlockSpec((1, t, dk), lambda h, j: (h, j, 0)),
                  pl.BlockSpec((1, dk, t), lambda h, j: (h, 0, j)),
                  pl.BlockSpec((1, t, dv), lambda h, j: (h, j, 0))],
        out_specs=[pl.BlockSpec((1, n, dk, t), lambda h, j: (h, 0, 0, 0)),
                   pl.BlockSpec((1, t, dk), lambda h, j: (h, j, 0)),
                   pl.BlockSpec((1, t, dv), lambda h, j: (h, j, 0))],
        out_shape=[_sds((nh, n, dk, t), F32), _sds((nh, s, dk), F32), _sds((nh, s, dv), F32)],
        scratch_shapes=[pltpu.VMEM((t, dk), F32), pltpu.VMEM((t, dv), F32),
                        pltpu.VMEM((2, t, t), F32), pltpu.VMEM((2, t, t), F32)],
        compiler_params=_cp(("parallel", "arbitrary")),
    )(qf, do, lse_t, delta_t, kf, kft, va)


def _qk_bwd(dqf, dkf, dvf, q_raw, kv, z, c_t, s1_t, s2_t, gqn, gqr, gkn, gkr, *, name):
    s = q_raw.shape[0]
    t = min(ROW_T, s)
    scale = 1.0 / math.sqrt(QK_DIM)

    def body(dq_ref, dk_ref, dv_ref, q_ref, kv_ref, kr_ref, c_ref, s1_ref, s2_ref,
             gqn_ref, gqr_ref, gkn_ref, gkr_ref, dqr_ref, dkv_ref, dkr_ref, ggq_ref, ggk_ref):
        _acc_init([ggq_ref, ggk_ref])
        c_v, s1_v, s2_v = c_ref[...], s1_ref[...], s2_ref[...]
        kr = kr_ref[...]
        kr_ss = _lanesum(kr * kr)
        dkr = jnp.zeros(kr.shape, F32)
        ggq_n = ggq_r = ggk_n = ggk_r = jnp.zeros((1, LANE), F32)

        def norm_bwd(n, r, rs, dyn, dyr, gn, gr):
            nh_, rh_ = n * rs, r * rs
            dnh, drh = dyn * gn, dyr * gr
            dot = (_lanesum(dnh * nh_) + _lanesum(drh * rh_)) * (1.0 / QK_DIM)
            return rs * (dnh - nh_ * dot), rs * (drh - rh_ * dot), _colsum(dyn * nh_), _colsum(dyr * rh_)

        for h in range(N_HEADS):
            n = q_ref[:, h * LANE:(h + 1) * LANE]
            r = q_ref[:, N_HEADS * LANE + h * LANE:N_HEADS * LANE + (h + 1) * LANE]
            rs = lax.rsqrt((_lanesum(n * n) + _lanesum(r * r)) * (1.0 / QK_DIM) + EPS)
            dyn = dq_ref[h, :, 0:LANE] * scale
            dyr = _rope_bwd(dq_ref[h, :, LANE:HEAD_PAD] * scale, c_v, s1_v, s2_v)
            dn, dr, g_n, g_r = norm_bwd(n, r, rs, dyn, dyr, gqn_ref[...], gqr_ref[...])
            dqr_ref[:, h * LANE:(h + 1) * LANE] = dn.astype(dqr_ref.dtype)
            dqr_ref[:, N_HEADS * LANE + h * LANE:N_HEADS * LANE + (h + 1) * LANE] = dr.astype(dqr_ref.dtype)
            ggq_n, ggq_r = ggq_n + g_n, ggq_r + g_r

            n = kv_ref[:, h * 2 * LANE:h * 2 * LANE + LANE]
            rs = lax.rsqrt((_lanesum(n * n) + kr_ss) * (1.0 / QK_DIM) + EPS)
            dyn = dk_ref[h, :, 0:LANE]
            dyr = _rope_bwd(dk_ref[h, :, LANE:HEAD_PAD], c_v, s1_v, s2_v)
            dn, dr, g_n, g_r = norm_bwd(n, kr, rs, dyn, dyr, gkn_ref[...], gkr_ref[...])
            dkv_ref[:, h * 2 * LANE:h * 2 * LANE + LANE] = dn.astype(dkv_ref.dtype)
            dkv_ref[:, h * 2 * LANE + LANE:(h + 1) * 2 * LANE] = dv_ref[h].astype(dkv_ref.dtype)
            dkr = dkr + dr
            ggk_n, ggk_r = ggk_n + g_n, ggk_r + g_r

        dkr_ref[...] = dkr.astype(dkr_ref.dtype)
        ggq_ref[:, 0:LANE] += ggq_n
        ggq_ref[:, LANE:] += ggq_r
        ggk_ref[:, 0:LANE] += ggk_n
        ggk_ref[:, LANE:] += ggk_r

    hspec = lambda w: pl.BlockSpec((N_HEADS, t, w), lambda i: (0, i, 0))
    wide = 2 * N_HEADS * LANE
    return pl.pallas_call(
        body, name=name, grid=(s // t,),
        in_specs=[hspec(HEAD_PAD), hspec(HEAD_PAD), hspec(V_DIM), _rowspec(t, wide), _rowspec(t, wide),
                  _rowspec(t, LANE, SEG_KR[0] // LANE), _rowspec(t, LANE), _rowspec(t, LANE), _rowspec(t, LANE),
                  _vecspec(LANE), _vecspec(LANE), _vecspec(LANE), _vecspec(LANE)],
        out_specs=[_rowspec(t, wide), _rowspec(t, wide), _rowspec(t, LANE), _vecspec(2 * LANE), _vecspec(2 * LANE)],
        out_shape=[_sds((s, wide), MXU_DTYPE), _sds((s, wide), MXU_DTYPE), _sds((s, LANE), MXU_DTYPE),
                   _sds((1, 2 * LANE), F32), _sds((1, 2 * LANE), F32)],
        compiler_params=_cp(("arbitrary",)),
    )(dqf, dkf, dvf, q_raw, kv, z, c_t, s1_t, s2_t, gqn, gqr, gkn, gkr)


def _lat_bwd(dqn, dkn, z, g_ql, g_kvl, *, name):
    s = z.shape[0]
    t = min(ROW_T, s)

    def body(dq_ref, dk_ref, ql_ref, kvl_ref, gq_ref, gk_ref, dql_ref, dkvl_ref, ggq_ref, ggk_ref):
        _acc_init([ggq_ref, ggk_ref])
        for d_ref, src, g_ref, dst, gg_ref in ((dq_ref, ql_ref, gq_ref, dql_ref, ggq_ref),
                                               (dk_ref, kvl_ref, gk_ref, dkvl_ref, ggk_ref)):
            v, dy = src[...], d_ref[...]
            r = lax.rsqrt(jnp.mean(v * v, axis=-1, keepdims=True) + EPS)
            vh = v * r
            dvh = dy * g_ref[...]
            dst[...] = (r * (dvh - vh * jnp.mean(dvh * vh, axis=-1, keepdims=True))).astype(dst.dtype)
            gg_ref[...] += _colsum(dy * vh)

    return pl.pallas_call(
        body, name=name, grid=(s // t,),
        in_specs=[_rowspec(t, Q_LORA), _rowspec(t, KV_LORA),
                  _rowspec(t, Q_LORA, SEG_QL[0] // Q_LORA), _rowspec(t, KV_LORA, SEG_KVL[0] // KV_LORA),
                  _vecspec(Q_LORA), _vecspec(KV_LORA)],
        out_specs=[_rowspec(t, Q_LORA), _rowspec(t, KV_LORA), _vecspec(Q_LORA), _vecspec(KV_LORA)],
        out_shape=[_sds((s, Q_LORA), MXU_DTYPE), _sds((s, KV_LORA), MXU_DTYPE),
                   _sds((1, Q_LORA), F32), _sds((1, KV_LORA), F32)],
        compiler_params=_cp(("arbitrary",)),
    )(dqn, dkn, z, z, g_ql, g_kvl)


def _prenorm_bwd(dh, x, gxo, g, sc1p, *, name):
    s, d = x.shape
    t = min(ROW_T, s)

    def body(dh_ref, x_ref, gx_ref, g_ref, sc_ref, dx_ref, dsh_ref, dsc_ref, gg_ref):
        _acc_init([dsh_ref, dsc_ref, gg_ref])
        xv, dhv = x_ref[...], dh_ref[...]
        r = lax.rsqrt(jnp.mean(xv * xv, axis=-1, keepdims=True) + EPS)
        xn = xv * r
        dsh_ref[...] += _colsum(dhv)
        dsc_ref[...] += _colsum(dhv * (xn * g_ref[...]))
        dm = dhv * sc_ref[...]
        gg_ref[...] += _colsum(dm * xn)
        dxn = dm * g_ref[...]
        dx_ref[...] = gx_ref[...] + r * (dxn - xn * jnp.mean(dxn * xn, axis=-1, keepdims=True))

    return pl.pallas_call(
        body, name=name, grid=(s // t,),
        in_specs=[_rowspec(t, d), _rowspec(t, d), _rowspec(t, d), _vecspec(d), _vecspec(d)],
        out_specs=[_rowspec(t, d), _vecspec(d), _vecspec(d), _vecspec(d)],
        out_shape=[_sds((s, d), F32), _sds((1, d), F32), _sds((1, d), F32), _sds((1, d), F32)],
        compiler_params=_cp(("arbitrary",)),
    )(dh, x, gxo, g, sc1p)


def _ada_fwd(c_all, ada_w, ada_b_cols, *, name):
    nl, d, cols = ada_w.shape

    def body(c_ref, w_ref, b_ref, o_ref):
        ca = _silu(c_ref[...]).astype(MXU_DTYPE)
        o_ref[0] = jnp.dot(ca, w_ref[0].astype(MXU_DTYPE), preferred_element_type=F32) + b_ref[0]

    return pl.pallas_call(
        body, name=name, grid=(nl,),
        in_specs=[pl.BlockSpec((N_DEV, d), lambda l: (0, 0)), pl.BlockSpec((1, d, cols), lambda l: (l, 0, 0)),
                  pl.BlockSpec((1, 1, cols), lambda l: (l, 0, 0))],
        out_specs=pl.BlockSpec((1, N_DEV, cols), lambda l: (l, 0, 0)),
        out_shape=_sds((nl, N_DEV, cols), F32),
        compiler_params=_cp(("parallel",)),
    )(c_all, ada_w, ada_b_cols)


def _ada_bwd(c_all_t, dmod_cols, *, name):
    nl, _, cols = dmod_cols.shape
    d = c_all_t.shape[0]

    def body(c_ref, dm_ref, o_ref):
        ca = _silu(c_ref[...]).astype(MXU_DTYPE)
        o_ref[0] = jnp.dot(ca, dm_ref[0].astype(MXU_DTYPE), preferred_element_type=F32)

    return pl.pallas_call(
        body, name=name, grid=(nl,),
        in_specs=[pl.BlockSpec((d, N_DEV), lambda l: (0, 0)), pl.BlockSpec((1, N_DEV, cols), lambda l: (l, 0, 0))],
        out_specs=pl.BlockSpec((1, d, cols), lambda l: (l, 0, 0)),
        out_shape=_sds((nl, d, cols), F32),
        compiler_params=_cp(("parallel",)),
    )(c_all_t, dmod_cols)


def _adamw(gparts, w, m, v, *, name):
    shape = w.shape
    cols = shape[-1]
    rows = w.size // cols
    npart = gparts.shape[0]
    gp2 = gparts.reshape(npart, rows, cols)
    w2, m2, v2 = (a.reshape(rows, cols) for a in (w, m, v))
    t = ROW_T if rows % ROW_T == 0 else rows

    def body(g_ref, w_ref, m_ref, v_ref, go_ref, d_ref, mo_ref, vo_ref):
        g = g_ref[0].astype(F32)
        for p in range(1, npart):
            g = g + g_ref[p].astype(F32)
        mn = ADAM_B1 * m_ref[...] + (1.0 - ADAM_B1) * g
        vn = ADAM_B2 * v_ref[...] + (1.0 - ADAM_B2) * (g * g)
        m_hat = mn / (1.0 - ADAM_B1 ** ADAM_STEP)
        v_hat = vn / (1.0 - ADAM_B2 ** ADAM_STEP)
        go_ref[...] = g
        d_ref[...] = -ADAM_LR * (m_hat / (jnp.sqrt(v_hat) + ADAM_EPS) + ADAM_WD * w_ref[...])
        mo_ref[...] = mn
        vo_ref[...] = vn

    spec = _rowspec(t, cols)
    outs = pl.pallas_call(
        body, name=name, grid=(rows // t,),
        in_specs=[pl.BlockSpec((npart, t, cols), lambda i: (0, i, 0)), spec, spec, spec],
        out_specs=[spec] * 4, out_shape=[_sds((rows, cols), F32)] * 4,
        compiler_params=_cp(("parallel",)),
    )(gp2, w2, m2, v2)
    return tuple(o.reshape(shape) for o in outs)


_ANY = pl.BlockSpec(memory_space=pl.ANY)


def _all_gather(blocks, *, name):
    na = len(blocks)

    def body(*refs):
        x_refs, out_refs = refs[:na], refs[na:2 * na]
        send_sems, recv_sems, local_sems = refs[2 * na:]
        x, y, c = lax.axis_index("x"), lax.axis_index("y"), lax.axis_index("c")
        me, sibling = (x, y, c), (x, y, 1 - c)
        chips = [(1 - x, y), (x, 1 - y), (1 - x, 1 - y)]

        def slot(a, px, py, pc):
            return out_refs[a].at[4 * px + 2 * py + pc]

        def copy(a, k, blk, to, src=None):
            return pltpu.make_async_remote_copy(
                src_ref=slot(a, *blk) if src is None else src, dst_ref=slot(a, *blk),
                send_sem=send_sems.at[7 * a + k], recv_sem=recv_sems.at[7 * a + k],
                device_id=to, device_id_type=MESH_ID)

        mine = [pltpu.make_async_copy(x_refs[a], slot(a, *me), local_sems.at[a]) for a in range(na)]
        for cp in mine:
            cp.start()
        first = []
        for a in range(na):
            first.append(copy(a, 0, me, sibling, src=x_refs[a]))
            first += [copy(a, 1 + j, me, (*chip, c), src=x_refs[a]) for j, chip in enumerate(chips)]
        for cp in first:
            cp.start()
        passed = []
        for a in range(na):
            for j, chip in enumerate(chips):
                copy(a, 1 + j, (*chip, c), me).wait_recv()
                fwd = copy(a, 4 + j, (*chip, c), sibling)
                fwd.start()
                passed.append(fwd)
        for a in range(na):
            copy(a, 0, sibling, me).wait_recv()
            for j, chip in enumerate(chips):
                copy(a, 4 + j, (*chip, 1 - c), me).wait_recv()
        for cp in first + passed:
            cp.wait_send()
        for cp in mine:
            cp.wait()

    outs = pl.pallas_call(
        body, name=name, in_specs=[_ANY] * na, out_specs=[_ANY] * na,
        out_shape=[_sds((N_DEV,) + b.shape, b.dtype) for b in blocks],
        scratch_shapes=[pltpu.SemaphoreType.DMA((7 * na,)), pltpu.SemaphoreType.DMA((7 * na,)),
                        pltpu.SemaphoreType.DMA((na,))],
    )(*blocks)
    return list(outs)


def _all_to_all(parts, *, name):
    na = len(parts)

    def body(*refs):
        in_refs, out_refs = refs[:na], refs[na:2 * na]
        send_sems, recv_sems, local_sems = refs[2 * na:]
        x, y, c = lax.axis_index("x"), lax.axis_index("y"), lax.axis_index("c")
        me = 4 * x + 2 * y + c
        mine = [pltpu.make_async_copy(in_refs[a].at[me], out_refs[a].at[me], local_sems.at[a]) for a in range(na)]
        for cp in mine:
            cp.start()
        copies = []
        for a in range(na):
            for k in range(1, N_DEV):
                px = 1 - x if k & 4 else x
                py = 1 - y if k & 2 else y
                pc = 1 - c if k & 1 else c
                cp = pltpu.make_async_remote_copy(
                    src_ref=in_refs[a].at[4 * px + 2 * py + pc], dst_ref=out_refs[a].at[me],
                    send_sem=send_sems.at[7 * a + k - 1], recv_sem=recv_sems.at[7 * a + k - 1],
                    device_id=(px, py, pc), device_id_type=MESH_ID)
                cp.start()
                copies.append(cp)
        for cp in copies:
            cp.wait()
        for cp in mine:
            cp.wait()

    outs = pl.pallas_call(
        body, name=name, in_specs=[_ANY] * na, out_specs=[_ANY] * na,
        out_shape=[_sds(p.shape, p.dtype) for p in parts],
        scratch_shapes=[pltpu.SemaphoreType.DMA((7 * na,)), pltpu.SemaphoreType.DMA((7 * na,)),
                        pltpu.SemaphoreType.DMA((na,))],
    )(*parts)
    return list(outs)


_WIN_SEGS = (("ql", 0, Q_LORA, SEG_QL[0]), ("kvl", Q_LORA, KV_LORA, SEG_KVL[0]),
             ("kr", Q_LORA + KV_LORA, ROPE, SEG_KR[0]), ("mg", Q_LORA + KV_LORA + ROPE, D_MLA, SEG_MG[0]),
             ("ci", Q_LORA + KV_LORA + ROPE + D_MLA, 2 * D_CONV, SEG_CI[0]),
             ("cg", Q_LORA + KV_LORA + ROPE + D_MLA + 2 * D_CONV, D_CONV, SEG_CG[0]))
_WIN_SHARD = IN_COLS // N_DEV


def _win_pieces():
    out = []
    for _, o, n, new in _WIN_SEGS:
        for j in range(N_DEV):
            lo, hi = max(o, j * _WIN_SHARD), min(o + n, (j + 1) * _WIN_SHARD)
            if lo < hi:
                out.append((j, lo - j * _WIN_SHARD, new + lo - o, hi - lo))
    return out


def _win_assemble(w_all, l, *, name):
    d = w_all.shape[2]
    t = min(ROW_T, d)
    pieces = sorted(_win_pieces(), key=lambda p: p[2])

    def body(w_ref, o_ref):
        cols = [w_ref[j, 0, :, lo:lo + n].astype(F32) for j, lo, _, n in pieces]
        cols.append(jnp.zeros((t, IN_PAD - (SEG_KR[0] + ROPE)), F32))
        o_ref[...] = jnp.concatenate(cols, axis=1).astype(o_ref.dtype)

    return pl.pallas_call(
        body, name=name, grid=(d // t,),
        in_specs=[pl.BlockSpec((N_DEV, 1, t, _WIN_SHARD), lambda i: (0, l, i, 0))],
        out_specs=_rowspec(t, IN_PAD), out_shape=_sds((d, IN_PAD), w_all.dtype),
        compiler_params=_cp(("parallel",)),
    )(w_all)


def _win_split(grads, *, name):
    nl = len(grads)
    d = grads[0].shape[0]
    t = min(ROW_T, d)
    n_blk = d // t
    by_shard = [sorted([p for p in _win_pieces() if p[0] == j], key=lambda p: p[1]) for j in range(N_DEV)]

    def body(*refs):
        g_refs, o_ref = refs[:nl], refs[nl]
        l = pl.program_id(0)
        for ll in range(nl):
            @pl.when(l == ll)
            def _(ll=ll):
                g = g_refs[ll][...]
                for j in range(N_DEV):
                    cols = [g[:, new:new + n] for _, _, new, n in by_shard[j]]
                    o_ref[j, 0] = jnp.concatenate(cols, axis=1).astype(o_ref.dtype)

    def in_map(ll):
        return lambda l, i: (jnp.where(l == ll, i, jnp.where(l < ll, 0, n_blk - 1)), 0)

    return pl.pallas_call(
        body, name=name, grid=(nl, n_blk),
        in_specs=[pl.BlockSpec((t, IN_PAD), in_map(ll)) for ll in range(nl)],
        out_specs=pl.BlockSpec((N_DEV, 1, t, _WIN_SHARD), lambda l, i: (0, l, i, 0)),
        out_shape=_sds((N_DEV, nl, d, _WIN_SHARD), WIRE_DTYPE),
        compiler_params=_cp(("arbitrary", "arbitrary")),
    )(*grads)


def _cols_to_shards(a):
    r, n = a.shape
    return a.reshape(r, N_DEV, n // N_DEV).transpose(1, 0, 2)


def _shards_to_cols(a):
    nd, r, w = a.shape
    return a.transpose(1, 0, 2).reshape(r, nd * w)


def _win_permute(w_in):
    o_ql, o_kvl, o_kr, o_mg = 0, Q_LORA, Q_LORA + KV_LORA, Q_LORA + KV_LORA + ROPE
    o_ci = o_mg + D_MLA
    o_cg = o_ci + 2 * D_CONV
    seg = lambda o, n: w_in[:, o:o + n]
    pad = jnp.zeros((w_in.shape[0], LANE - ROPE), w_in.dtype)
    return jnp.concatenate([seg(o_ci, 2 * D_CONV), seg(o_mg, D_MLA), seg(o_cg, D_CONV), seg(o_ql, Q_LORA),
                            seg(o_kvl, KV_LORA), seg(o_kr, ROPE), pad], axis=1)


def _win_unpermute(g):
    seg = lambda s, n=None: g[:, s[0]:s[0] + (s[1] if n is None else n)]
    return jnp.concatenate([seg(SEG_QL), seg(SEG_KVL), seg(SEG_KR, ROPE), seg(SEG_MG), seg(SEG_CI), seg(SEG_CG)], axis=1)


def _qup_permute(w):
    w3 = w.reshape(w.shape[0], N_HEADS, QK_DIM)
    nope = w3[:, :, :NOPE].reshape(w.shape[0], N_HEADS * NOPE)
    rope = jnp.pad(w3[:, :, NOPE:], ((0, 0), (0, 0), (0, LANE - ROPE))).reshape(w.shape[0], N_HEADS * LANE)
    return jnp.concatenate([nope, rope], axis=1)


def _qup_unpermute(g):
    r = g.shape[0]
    nope = g[:, :N_HEADS * NOPE].reshape(r, N_HEADS, NOPE)
    rope = g[:, N_HEADS * NOPE:].reshape(r, N_HEADS, LANE)[:, :, :ROPE]
    return jnp.concatenate([nope, rope], axis=2).reshape(r, N_HEADS * QK_DIM)


def _norm_tiles(g):
    return g[:NOPE].reshape(1, LANE), jnp.pad(g[NOPE:], (0, LANE - ROPE)).reshape(1, LANE)


def _norm_untile(gt):
    return jnp.concatenate([gt[0, :NOPE], gt[0, LANE:LANE + ROPE]])


def _rope_tiles(positions):
    inv_freq = 1.0 / (ROPE_THETA ** (jnp.arange(0, ROPE, 2, dtype=F32) / ROPE))
    ang = positions.astype(F32)[:, None] * inv_freq
    cos, sin = jnp.cos(ang), jnp.sin(ang)
    zq = jnp.zeros_like(cos)
    c_t = jnp.concatenate([cos, cos, zq, zq], axis=1)
    s1_t = jnp.concatenate([-sin, zq, zq, zq], axis=1)
    s2_t = jnp.concatenate([zq, sin, zq, zq], axis=1)
    return c_t, s1_t, s2_t


_BIG = ("w_in", "w_q_up", "w_kv_up", "w_pw", "w_out")
_COL_SHARDED = ("w_in", "w_q_up", "w_kv_up")


def _pack_rows(arrs):
    return jnp.concatenate([a.reshape(-1, LANE) for a in arrs], axis=0)


def _unpack_rows(buf, shapes):
    out, r0 = [], 0
    lead = buf.shape[:-2]
    for shp in shapes:
        n = math.prod(shp) // LANE
        out.append(buf[..., r0:r0 + n, :].reshape(lead + tuple(shp)))
        r0 += n
    return out


_SMALL = (("dmod", 3 * D_MODEL), ("norm_g", D_MODEL), ("q_lat_g", Q_LORA), ("kv_lat_g", KV_LORA),
          ("q_norm_g", 2 * LANE), ("k_norm_g", 2 * LANE), ("glu_b", 2 * D_CONV), ("dw_w", HALO * D_CONV),
          ("dw_b", D_CONV), ("conv_ln_g", D_CONV), ("conv_ln_b", D_CONV), ("b_pw", D_CONV))


def _layer_fwd(x, p, rope, l):
    n = lambda s: f"{s}_l{l}"
    c_t, s1_t, s2_t = rope
    h = _prenorm(x, p["norm_g"], p["shift"], p["sc1p"], name=n("prenorm"))
    z = _mm(h, p["w_in"], name=n("in_proj"), tn=IN_PAD // 3, n_outer=True)
    qn, kn = _lat_norm(z, p["q_lat_g"], p["kv_lat_g"], name=n("lat_norm"))
    q_raw = _mm(qn, p["w_q_up"], name=n("q_up"), tn=1024)
    kv = _mm(kn, p["w_kv_up"], name=n("kv_up"), tn=1024)
    qf, kf, vf = _qk_prep(q_raw, kv, z, c_t, s1_t, s2_t, *p["qk_tiles"], name=n("qk_prep"))
    o, lse = _flash_fwd(qf, kf, vf, name=n("flash_fwd"))
    u1, u3 = _conv_fwd(z, p["glu_b"], p["dw_w"], p["dw_b"], p["conv_ln_g"], p["conv_ln_b"], name=n("conv_fwd"))
    u4m = _mm(u3, p["w_pw"], name=n("pw"), tn=1024)
    cat = _gate_cat(o, z, u4m, p["b_pw"], name=n("gate_cat"))
    y = _mm(cat, p["w_out"], name=n("out_proj"), tn=1024)
    x_next = _residual(x, y, p["gate"], name=n("residual"))
    saved = dict(x=x, h=h, z=z, qn=qn, kn=kn, q_raw=q_raw, kv=kv, qf=qf, kf=kf, vf=vf, o=o, lse=lse,
                 u1=u1, u3=u3, u4m=u4m, cat=cat, y=y)
    return x_next, saved


def _layer_bwd(gxo, p, sv, rope, l):
    n = lambda s: f"{s}_l{l}"
    c_t, s1_t, s2_t = rope
    z = sv["z"]
    dy, dgate = _out_bwd(gxo, sv["y"], p["gate"], name=n("out_bwd"))
    g_w_out = _mm(sv["cat"], dy, ta=True, name=n("g_w_out"), tm=1024, tn=1024, tk=512)
    dcat = _mm(dy, p["w_out"], tb=True, name=n("d_cat"), tn=1024)
    do, delta, dmg, du4, dcg, g_b_pw = _gate_bwd(dcat, sv["o"], z, sv["u4m"], p["b_pw"], name=n("gate_bwd"))
    g_w_pw = _mm(sv["u3"], du4, ta=True, name=n("g_w_pw"), tm=1024, tn=1024, tk=512)
    du3 = _mm(du4, p["w_pw"], tb=True, name=n("d_u3"), tn=1024)
    dci, g_ln_g, g_ln_b, g_dw_b, g_glu_b, g_dw_w = _conv_bwd(
        du3, sv["u1"], z, p["glu_b"], p["dw_w"], p["conv_ln_g"], p["conv_ln_b"], name=n("conv_bwd"))
    t_att = min(ATT_T, z.shape[0])
    to_lanes = lambda a: a.reshape(N_HEADS, z.shape[0] // t_att, 1, t_att)
    dqt, dkf, dvf = _flash_bwd(sv["qf"], sv["kf"], jnp.swapaxes(sv["kf"], 1, 2), sv["vf"], do,
                               to_lanes(sv["lse"][:, :, 0]), to_lanes(delta), name=n("flash_bwd"))
    dqf = jnp.swapaxes(dqt, 2, 3).reshape(N_HEADS, z.shape[0], HEAD_PAD)
    dq_raw, dkv, dkr, g_qn, g_kn = _qk_bwd(dqf, dkf, dvf, sv["q_raw"], sv["kv"], z, c_t, s1_t, s2_t,
                                            *p["qk_tiles"], name=n("qk_bwd"))
    g_w_q_up = _mm(sv["qn"], dq_raw, ta=True, name=n("g_w_q_up"), tm=512, tn=1024, tk=512)
    dqn = _mm(dq_raw, p["w_q_up"], tb=True, name=n("d_qn"))
    g_w_kv_up = _mm(sv["kn"], dkv, ta=True, name=n("g_w_kv_up"), tm=256, tn=1024, tk=512)
    dkn = _mm(dkv, p["w_kv_up"], tb=True, name=n("d_kn"))
    dql, dkvl, g_ql, g_kvl = _lat_bwd(dqn, dkn, z, p["q_lat_g"], p["kv_lat_g"], name=n("lat_bwd"))
    dz = jnp.concatenate([dci, dmg, dcg, dql, dkvl, dkr], axis=1)
    g_w_in = _mm(sv["h"], dz, ta=True, name=n("g_w_in"), tm=1024, tn=IN_PAD // 3, tk=512)
    dh = _mm(dz, p["w_in"], tb=True, name=n("d_h"), tn=1024, tk=IN_PAD // 3)
    dx, dshift, dscale, g_norm = _prenorm_bwd(dh, sv["x"], gxo, p["norm_g"], p["sc1p"], name=n("prenorm_bwd"))
    big = dict(w_in=g_w_in, w_q_up=g_w_q_up, w_kv_up=g_w_kv_up, w_pw=g_w_pw, w_out=g_w_out)
    small = dict(dmod=jnp.concatenate([dshift, dscale, dgate], axis=1), norm_g=g_norm, q_lat_g=g_ql, kv_lat_g=g_kvl,
                 q_norm_g=g_qn, k_norm_g=g_kn, glu_b=g_glu_b, dw_w=g_dw_w, dw_b=g_dw_b,
                 conv_ln_g=g_ln_g, conv_ln_b=g_ln_b, b_pw=g_b_pw)
    return dx, big, small


def _layer_params(l, full, mod_l, small):
    d = D_MODEL
    row = lambda a: a.reshape(1, -1)
    shift, scale, gate = mod_l[:, :d], mod_l[:, d:2 * d], mod_l[:, 2 * d:]
    dw_w = jnp.pad(full["dw_w"][l], ((0, HALO - CONV_K), (0, 0)))
    return dict(
        shift=shift, sc1p=1.0 + scale, gate=gate, norm_g=row(small["norm_g"][l]),
        w_in=full["w_in"][l], w_q_up=full["w_q_up"][l], w_kv_up=full["w_kv_up"][l],
        w_pw=full["w_pw"][l], w_out=full["w_out"][l], dw_w=dw_w,
        q_lat_g=row(small["q_lat_g"][l]), kv_lat_g=row(small["kv_lat_g"][l]),
        qk_tiles=_norm_tiles(small["q_norm_g"][l]) + _norm_tiles(small["k_norm_g"][l]),
        glu_b=row(small["glu_b"][l]), dw_b=row(small["dw_b"][l]), conv_ln_g=row(small["conv_ln_g"][l]),
        conv_ln_b=row(small["conv_ln_b"][l]), b_pw=row(small["b_pw"][l]))


def kernel(x, c, positions, ada_w, ada_b, norm_g, w_in, q_lat_g, w_q_up, kv_lat_g, w_kv_up, q_norm_g, k_norm_g, glu_b, dw_w, dw_b, conv_ln_g, conv_ln_b, w_pw, b_pw, w_out, loss_target, m_ada_w, m_ada_b, m_norm_g, m_w_in, m_q_lat_g, m_w_q_up, m_kv_lat_g, m_w_kv_up, m_q_norm_g, m_k_norm_g, m_glu_b, m_dw_w, m_dw_b, m_conv_ln_g, m_conv_ln_b, m_w_pw, m_b_pw, m_w_out, v_ada_w, v_ada_b, v_norm_g, v_w_in, v_q_lat_g, v_w_q_up, v_kv_lat_g, v_w_kv_up, v_q_norm_g, v_k_norm_g, v_glu_b, v_dw_w, v_dw_b, v_conv_ln_g, v_conv_ln_b, v_w_pw, v_b_pw, v_w_out):
    names = ("ada_w", "ada_b", "norm_g", "w_in", "q_lat_g", "w_q_up", "kv_lat_g", "w_kv_up", "q_norm_g",
             "k_norm_g", "glu_b", "dw_w", "dw_b", "conv_ln_g", "conv_ln_b", "w_pw", "b_pw", "w_out")
    w_loc = dict(zip(names, (ada_w, ada_b, norm_g, w_in, q_lat_g, w_q_up, kv_lat_g, w_kv_up, q_norm_g, k_norm_g,
                             glu_b, dw_w, dw_b, conv_ln_g, conv_ln_b, w_pw, b_pw, w_out)))
    m_loc = dict(zip(names, (m_ada_w, m_ada_b, m_norm_g, m_w_in, m_q_lat_g, m_w_q_up, m_kv_lat_g, m_w_kv_up,
                             m_q_norm_g, m_k_norm_g, m_glu_b, m_dw_w, m_dw_b, m_conv_ln_g, m_conv_ln_b, m_w_pw,
                             m_b_pw, m_w_out)))
    v_loc = dict(zip(names, (v_ada_w, v_ada_b, v_norm_g, v_w_in, v_q_lat_g, v_w_q_up, v_kv_lat_g, v_w_kv_up,
                             v_q_norm_g, v_k_norm_g, v_glu_b, v_dw_w, v_dw_b, v_conv_ln_g, v_conv_ln_b, v_w_pw,
                             v_b_pw, v_w_out)))
    nl, d = N_LAYERS, D_MODEL
    me = 4 * lax.axis_index("x") + 2 * lax.axis_index("y") + lax.axis_index("c")
    x2, tgt = x[0], loss_target[0]
    ada_cols = ada_w.shape[-1]

    c_all = _all_gather([c.reshape(d // LANE, LANE)], name="gather_c")[0].reshape(N_DEV, d)
    ada_b_cols = lax.dynamic_slice_in_dim(ada_b, me * ada_cols, ada_cols, axis=1).reshape(nl, 1, ada_cols)
    mod_cols = _ada_fwd(c_all, ada_w, ada_b_cols, name="ada_fwd")
    mod_all = _all_gather([mod_cols], name="gather_mod")[0]
    mod_me = lax.dynamic_index_in_dim(mod_all, me, axis=2, keepdims=False)
    mod = mod_me.transpose(1, 0, 2).reshape(nl, 1, N_DEV * ada_cols)

    dw_pad = jnp.pad(dw_w, ((0, 0), (0, HALO - CONV_K), (0, 0)))
    gathered = _all_gather([w_loc[k].astype(WIRE_DTYPE) for k in _BIG] + [dw_pad], name="gather_weights")
    parts = dict(zip(_BIG, gathered[:-1]))
    dw_all = gathered[-1]
    full = dict(
        w_in=[_win_assemble(parts["w_in"], l, name=f"w_in_assemble_l{l}") for l in range(nl)],
        w_q_up=[_qup_permute(_shards_to_cols(parts["w_q_up"][:, l])) for l in range(nl)],
        w_kv_up=[_shards_to_cols(parts["w_kv_up"][:, l]) for l in range(nl)],
        w_pw=[parts["w_pw"][:, l].reshape(D_CONV, D_CONV) for l in range(nl)],
        w_out=[parts["w_out"][:, l].reshape(D_MLA + D_CONV, d) for l in range(nl)],
        dw_w=[_shards_to_cols(dw_all[:, l])[:CONV_K] for l in range(nl)])
    small_in = dict(norm_g=norm_g, q_lat_g=q_lat_g, kv_lat_g=kv_lat_g, q_norm_g=q_norm_g, k_norm_g=k_norm_g,
                    glu_b=glu_b, dw_b=dw_b, conv_ln_g=conv_ln_g, conv_ln_b=conv_ln_b, b_pw=b_pw)
    params = [_layer_params(l, full, mod[l], small_in) for l in range(nl)]
    rope = _rope_tiles(positions[0])

    xs, saved = x2, []
    for l in range(nl):
        xs, sv = _layer_fwd(xs, params[l], rope, l)
        saved.append(sv)
    gx, loss_part = _loss_head(xs, tgt, name="loss_head")
    loss = lax.psum(loss_part[0, 0], ("x", "y", "c"))
    big_g, small_g = [None] * nl, [None] * nl
    for l in reversed(range(nl)):
        gx, big_g[l], small_g[l] = _layer_bwd(gx, params[l], saved[l], rope, l)

    def shard_major(k, g):
        if k == "w_q_up":
            g = _qup_unpermute(g)
        if k in _COL_SHARDED:
            return _cols_to_shards(g)
        return g.reshape((N_DEV, g.shape[0] // N_DEV, g.shape[1]))

    to_send = [_win_split([big_g[l]["w_in"] for l in range(nl)], name="w_in_split")]
    to_send += [jnp.stack([shard_major(k, big_g[l][k]).astype(WIRE_DTYPE) for l in range(nl)], axis=1)
                for k in _BIG[1:]]
    g_parts = dict(zip(_BIG, _all_to_all(to_send, name="scatter_grads")))

    tile = 8 * LANE
    padded = [(k, nn, -(-nn // tile) * tile) for k, nn in _SMALL]
    spk = jnp.concatenate([jnp.pad(small_g[l][k].reshape(-1), (0, np_ - nn)).reshape(-1, LANE)
                           for l in range(nl) for k, nn, np_ in padded], axis=0)
    s_all = _all_gather([spk], name="gather_small_grads")[0]
    s_rows = sum(np_ for _, _, np_ in padded) // LANE
    s_all = s_all.reshape(N_DEV, nl, s_rows, LANE)
    s_parts = {k: a[..., :nn] for (k, nn, _), a in
               zip(padded, _unpack_rows(s_all, [(np_,) for _, _, np_ in padded]))}

    dmod_all = s_parts["dmod"]
    dmod_cols = lax.dynamic_slice_in_dim(dmod_all, me * ada_cols, ada_cols, axis=2).transpose(1, 0, 2)
    g_ada_w = _ada_bwd(c_all.T, dmod_cols, name="ada_bwd")
    gp = dict(g_parts)
    gp["ada_w"] = g_ada_w[None]
    gp["ada_b"] = dmod_all
    for k in ("norm_g", "q_lat_g", "kv_lat_g", "glu_b", "dw_b", "conv_ln_g", "conv_ln_b", "b_pw"):
        gp[k] = s_parts[k]
    for k in ("q_norm_g", "k_norm_g"):
        t = s_parts[k]
        gp[k] = jnp.concatenate([t[..., :NOPE], t[..., LANE:LANE + ROPE]], axis=-1)
    dw_g = s_parts["dw_w"].reshape(N_DEV, nl, HALO, D_CONV)[:, :, :CONV_K]
    gp["dw_w"] = lax.dynamic_slice_in_dim(dw_g, me * LANE, LANE, axis=3)

    res = {k: _adamw(gp[k], w_loc[k], m_loc[k], v_loc[k], name=f"adamw_{k}") for k in names}
    out = [loss, gx[None]]
    for idx in range(4):
        out += [res[k][idx] for k in names]
    return tuple(out)
```

```python
import functools
import math

import jax
import jax.numpy as jnp
from jax import lax
from jax.experimental import pallas as pl
from jax.experimental.pallas import tpu as pltpu

F32 = jnp.float32
MXU_DTYPE = jnp.bfloat16
WIRE_DTYPE = jnp.bfloat16

D_MODEL = 2048
N_LAYERS = 2
N_DEV = 8
N_HEADS = 8
NOPE = 128
ROPE = 64
V_DIM = 128
QK_DIM = NOPE + ROPE
Q_LORA = 512
KV_LORA = 256
D_MLA = N_HEADS * V_DIM
D_CONV = 1024
CONV_K = 31
ROPE_THETA = 10000.0
EPS = 1e-6
LANE = 128
HEAD_PAD = 2 * LANE
HALO = 32

SEG_CI = (0, 2 * D_CONV)
SEG_MG = (2 * D_CONV, D_MLA)
SEG_CG = (2 * D_CONV + D_MLA, D_CONV)
SEG_QL = (2 * D_CONV + D_MLA + D_CONV, Q_LORA)
SEG_KVL = (SEG_QL[0] + Q_LORA, KV_LORA)
SEG_KR = (SEG_KVL[0] + KV_LORA, LANE)
IN_PAD = SEG_KR[0] + LANE
IN_COLS = Q_LORA + KV_LORA + ROPE + D_MLA + 2 * D_CONV + D_CONV

ADAM_LR = 0.001
ADAM_B1 = 0.9
ADAM_B2 = 0.999
ADAM_EPS = 1e-08
ADAM_WD = 0.01
ADAM_STEP = 10

VMEM_LIMIT = 56 * 1024 * 1024
ATT_T = 512
ROW_T = 256
CONV_T = 128
MESH_ID = pl.DeviceIdType.MESH


def _cp(sem=None):
    kw = dict(vmem_limit_bytes=VMEM_LIMIT)
    if sem is not None:
        kw["dimension_semantics"] = sem
    return pltpu.CompilerParams(**kw)


def _sds(shape, dtype):
    return jax.ShapeDtypeStruct(shape, dtype)


def _silu(x):
    return x * jax.nn.sigmoid(x)


def _dsilu(x):
    s = jax.nn.sigmoid(x)
    return s * (1.0 + x * (1.0 - s))


def _rowspec(t, width, col=0):
    return pl.BlockSpec((t, width), lambda i: (i, col))


def _vecspec(width):
    return pl.BlockSpec((1, width), lambda i: (0, 0))


def _colsum(v):
    return jnp.sum(v, axis=0, keepdims=True)


def _mm(a, b, *, name, ta=False, tb=False, out_dtype=F32, tm=512, tn=512, tk=None, n_outer=False, after=None):
    if ta:
        kdim, m = a.shape
    else:
        m, kdim = a.shape
    if tb:
        n, k2 = b.shape
    else:
        k2, n = b.shape
    assert kdim == k2, (a.shape, b.shape)
    tm, tn = min(tm, m), min(tn, n)
    tk = kdim if tk is None else min(tk, kdim)
    assert m % tm == 0 and n % tn == 0 and kdim % tk == 0, (m, n, kdim, tm, tn, tk)
    nk = kdim // tk
    dims = (((0 if ta else 1,), (1 if tb else 0,)), ((), ()))

    n_extra = 0 if after is None else 1

    def body(a_ref, b_ref, *rest):
        o_ref, scratch = rest[n_extra], rest[n_extra + 1:]
        prod = lax.dot_general(a_ref[...].astype(MXU_DTYPE), b_ref[...].astype(MXU_DTYPE), dims,
                               preferred_element_type=F32)
        if nk == 1:
            o_ref[...] = prod.astype(o_ref.dtype)
        else:
            acc = scratch[0]
            k = pl.program_id(2)

            @pl.when(k == 0)
            def _():
                acc[...] = prod

            @pl.when(k > 0)
            def _():
                acc[...] += prod

            @pl.when(k == nk - 1)
            def _():
                o_ref[...] = acc[...].astype(o_ref.dtype)

    if n_outer:
        ij = lambda g0, g1: (g1, g0)
        grid = (n // tn, m // tm, nk)
    else:
        ij = lambda g0, g1: (g0, g1)
        grid = (m // tm, n // tn, nk)

    def a_map(g0, g1, k):
        i, _ = ij(g0, g1)
        return (k, i) if ta else (i, k)

    def b_map(g0, g1, k):
        _, j = ij(g0, g1)
        return (j, k) if tb else (k, j)

    def o_map(g0, g1, k):
        return ij(g0, g1)

    return pl.pallas_call(
        body, name=name, grid=grid,
        in_specs=[pl.BlockSpec((tk, tm) if ta else (tm, tk), a_map),
                  pl.BlockSpec((tn, tk) if tb else (tk, tn), b_map)] + [_ANY] * n_extra,
        out_specs=pl.BlockSpec((tm, tn), o_map),
        out_shape=_sds((m, n), out_dtype),
        scratch_shapes=[pltpu.VMEM((tm, tn), F32)] if nk > 1 else [],
        compiler_params=_cp(("parallel", "parallel", "arbitrary")),
    )(a, b, *([] if after is None else [after]))


def _prenorm(x, g, shift, sc1p, *, name):
    s, d = x.shape
    t = min(ROW_T, s)

    def body(x_ref, g_ref, sh_ref, sc_ref, h_ref):
        xv = x_ref[...]
        r = lax.rsqrt(jnp.mean(xv * xv, axis=-1, keepdims=True) + EPS)
        h_ref[...] = ((xv * r) * g_ref[...] * sc_ref[...] + sh_ref[...]).astype(h_ref.dtype)

    return pl.pallas_call(
        body, name=name, grid=(s // t,),
        in_specs=[_rowspec(t, d), _vecspec(d), _vecspec(d), _vecspec(d)],
        out_specs=_rowspec(t, d), out_shape=_sds((s, d), MXU_DTYPE),
        compiler_params=_cp(("parallel",)),
    )(x, g, shift, sc1p)


def _lat_norm(z, g_ql, g_kvl, *, name):
    s = z.shape[0]
    t = min(ROW_T, s)

    def body(ql_ref, kvl_ref, gq_ref, gk_ref, qn_ref, kn_ref):
        for src, g_ref, dst in ((ql_ref, gq_ref, qn_ref), (kvl_ref, gk_ref, kn_ref)):
            v = src[...]
            r = lax.rsqrt(jnp.mean(v * v, axis=-1, keepdims=True) + EPS)
            dst[...] = ((v * r) * g_ref[...]).astype(dst.dtype)

    return pl.pallas_call(
        body, name=name, grid=(s // t,),
        in_specs=[_rowspec(t, Q_LORA, SEG_QL[0] // Q_LORA), _rowspec(t, KV_LORA, SEG_KVL[0] // KV_LORA),
                  _vecspec(Q_LORA), _vecspec(KV_LORA)],
        out_specs=[_rowspec(t, Q_LORA), _rowspec(t, KV_LORA)],
        out_shape=[_sds((s, Q_LORA), MXU_DTYPE), _sds((s, KV_LORA), MXU_DTYPE)],
        compiler_params=_cp(("parallel",)),
    )(z, z, g_ql, g_kvl)


def _rope_fwd(r, c_t, s1_t, s2_t):
    return r * c_t + pltpu.roll(r, LANE - ROPE // 2, 1) * s1_t + pltpu.roll(r, ROPE // 2, 1) * s2_t


def _rope_bwd(d, c_t, s1_t, s2_t):
    return d * c_t + pltpu.roll(d * s1_t, ROPE // 2, 1) + pltpu.roll(d * s2_t, LANE - ROPE // 2, 1)


def _lanesum(v):
    return jnp.sum(v, axis=-1, keepdims=True)


def _qk_prep(q_raw, kv, z, c_t, s1_t, s2_t, gqn, gqr, gkn, gkr, *, name):
    s = q_raw.shape[0]
    t = min(ROW_T, s)
    scale = 1.0 / math.sqrt(QK_DIM)

    def body(q_ref, kv_ref, kr_ref, c_ref, s1_ref, s2_ref, gqn_ref, gqr_ref, gkn_ref, gkr_ref,
             qf_ref, kf_ref, vf_ref):
        c_v, s1_v, s2_v = c_ref[...], s1_ref[...], s2_ref[...]
        kr = kr_ref[...]
        kr_ss = _lanesum(kr * kr)
        for h in range(N_HEADS):
            n = q_ref[:, h * LANE:(h + 1) * LANE]
            r = q_ref[:, N_HEADS * LANE + h * LANE:N_HEADS * LANE + (h + 1) * LANE]
            rs = lax.rsqrt((_lanesum(n * n) + _lanesum(r * r)) * (1.0 / QK_DIM) + EPS)
            qf_ref[h, :, 0:LANE] = (((n * rs) * gqn_ref[...]) * scale).astype(qf_ref.dtype)
            rr = _rope_fwd((r * rs) * gqr_ref[...], c_v, s1_v, s2_v)
            qf_ref[h, :, LANE:HEAD_PAD] = (rr * scale).astype(qf_ref.dtype)

            n = kv_ref[:, h * 2 * LANE:h * 2 * LANE + LANE]
            rs = lax.rsqrt((_lanesum(n * n) + kr_ss) * (1.0 / QK_DIM) + EPS)
            kf_ref[h, :, 0:LANE] = ((n * rs) * gkn_ref[...]).astype(kf_ref.dtype)
            kf_ref[h, :, LANE:HEAD_PAD] = _rope_fwd((kr * rs) * gkr_ref[...], c_v, s1_v, s2_v).astype(kf_ref.dtype)
            vf_ref[h, :, 0:V_DIM] = kv_ref[:, h * 2 * LANE + LANE:(h + 1) * 2 * LANE].astype(vf_ref.dtype)
            vf_ref[h, :, V_DIM:] = jnp.ones((t, V_DIM), vf_ref.dtype)

    hspec = lambda w: pl.BlockSpec((N_HEADS, t, w), lambda i: (0, i, 0))
    return pl.pallas_call(
        body, name=name, grid=(s // t,),
        in_specs=[_rowspec(t, 2 * N_HEADS * LANE), _rowspec(t, 2 * N_HEADS * LANE),
                  _rowspec(t, LANE, SEG_KR[0] // LANE),
                  _rowspec(t, LANE), _rowspec(t, LANE), _rowspec(t, LANE),
                  _vecspec(LANE), _vecspec(LANE), _vecspec(LANE), _vecspec(LANE)],
        out_specs=[hspec(HEAD_PAD), hspec(HEAD_PAD), hspec(2 * V_DIM)],
        out_shape=[_sds((N_HEADS, s, HEAD_PAD), MXU_DTYPE), _sds((N_HEADS, s, HEAD_PAD), MXU_DTYPE),
                   _sds((N_HEADS, s, 2 * V_DIM), MXU_DTYPE)],
        compiler_params=_cp(("parallel",)),
    )(q_raw, kv, z, c_t, s1_t, s2_t, gqn, gqr, gkn, gkr)


def _causal_mask(t):
    row = lax.broadcasted_iota(jnp.int32, (t, t), 0)
    col = lax.broadcasted_iota(jnp.int32, (t, t), 1)
    return col <= row


NEG = -1e30


def _flash_fwd(qf, kf, va, *, name):
    nh, s, dk = qf.shape
    dv = va.shape[-1] // 2
    t = min(ATT_T, s)
    n = s // t
    assert dv == LANE and t % LANE == 0

    def body(q_ref, k_ref, v_ref, o_ref, lse_ref, m_s, acc_s, s_buf):
        i = pl.program_id(1)
        m_s[...] = jnp.full(m_s.shape, NEG, F32)
        acc_s[...] = jnp.zeros(acc_s.shape, F32)
        q = q_ref[0]

        def rows_of(j):
            return pl.ds(pl.multiple_of(j * t, t), t)

        def scores(j):
            return lax.dot_general(q, k_ref[0, rows_of(j), :], (((1,), (1,)), ((), ())), preferred_element_type=F32)

        def consume(j, slot, masked):
            sc = s_buf[slot]
            if masked:
                sc = jnp.where(_causal_mask(t), sc, NEG)
            m_prev = m_s[...]
            m_new = jnp.maximum(m_prev, jnp.max(sc, axis=-1, keepdims=True))
            alpha = jnp.exp(m_prev - m_new)
            p = jnp.exp(sc - jnp.tile(m_new, (1, t // LANE)))
            acc_s[...] = jnp.tile(alpha, (1, 2)) * acc_s[...] + jnp.dot(
                p.astype(MXU_DTYPE), v_ref[0, rows_of(j), :], preferred_element_type=F32)
            m_s[...] = m_new

        s_buf[0] = scores(0)

        def pair(a, carry):
            s_buf[1] = scores(2 * a + 1)
            consume(2 * a, 0, False)
            s_buf[0] = scores(2 * a + 2)
            consume(2 * a + 1, 1, False)
            return carry

        lax.fori_loop(0, i // 2, pair, 0)

        @pl.when(i % 2 == 1)
        def _():
            s_buf[1] = scores(i)
            consume(i - 1, 0, False)
            consume(i, 1, True)

        @pl.when(i % 2 == 0)
        def _():
            consume(i, 0, True)

        den = acc_s[:, dv:]
        o_ref[...] = acc_s[:, :dv] / den
        lse_ref[0] = m_s[...] + jnp.log(den)

    return pl.pallas_call(
        body, name=name, grid=(nh, n),
        in_specs=[pl.BlockSpec((1, t, dk), lambda h, i: (h, i, 0)),
                  pl.BlockSpec((1, s, dk), lambda h, i: (h, 0, 0)),
                  pl.BlockSpec((1, s, 2 * dv), lambda h, i: (h, 0, 0))],
        out_specs=[pl.BlockSpec((t, dv), lambda h, i: (i, h)),
                   pl.BlockSpec((1, t, LANE), lambda h, i: (h, i, 0))],
        out_shape=[_sds((s, nh * dv), F32), _sds((nh, s, LANE), F32)],
        scratch_shapes=[pltpu.VMEM((t, LANE), F32), pltpu.VMEM((t, 2 * dv), F32), pltpu.VMEM((2, t, t), F32)],
        compiler_params=_cp(("parallel", "arbitrary")),
    )(qf, kf, va)


def _dw_taps(ext_ref, w_ref, row0, t_rows, lane0, lanes, first_off):
    acc = None
    for k in range(CONV_K):
        term = w_ref[k:k + 1, lane0:lane0 + lanes] * ext_ref[pl.ds(row0 + first_off + k, t_rows), lane0:lane0 + lanes]
        acc = term if acc is None else acc + term
    return acc


CONV_RC = 32
CONV_LC = 256


def _conv_fwd(z, glu_b, dw_w, dw_b, ln_g, ln_b, *, name):
    s = z.shape[0]
    t = min(CONV_T, s)
    c2 = 2 * D_CONV
    hb = t // HALO

    def body(zm_ref, zh_ref, gb_ref, w_ref, wb_ref, g_ref, b_ref, u1_ref, u3_ref, ext):
        i = pl.program_id(0)

        def glu(zv):
            ci = zv + gb_ref[...]
            return ci[:, :D_CONV] * jax.nn.sigmoid(ci[:, D_CONV:])

        ext[HALO:, :] = glu(zm_ref[...])
        ext[0:HALO, :] = jnp.where(i > 0, glu(zh_ref[...]), 0.0)
        for rc in range(0, t, CONV_RC):
            for lc in range(0, D_CONV, CONV_LC):
                acc = _dw_taps(ext, w_ref, rc, CONV_RC, lc, CONV_LC, HALO - (CONV_K - 1))
                u1_ref[rc:rc + CONV_RC, lc:lc + CONV_LC] = acc + wb_ref[:, lc:lc + CONV_LC]
        u1 = u1_ref[...]
        mu = jnp.mean(u1, axis=-1, keepdims=True)
        cen = u1 - mu
        var = jnp.mean(cen * cen, axis=-1, keepdims=True)
        u2 = (cen * lax.rsqrt(var + EPS)) * g_ref[...] + b_ref[...]
        u3_ref[...] = _silu(u2).astype(u3_ref.dtype)

    return pl.pallas_call(
        body, name=name, grid=(s // t,),
        in_specs=[_rowspec(t, c2), pl.BlockSpec((HALO, c2), lambda i: (jnp.maximum(i * hb - 1, 0), 0)),
                  _vecspec(c2), pl.BlockSpec((HALO, D_CONV), lambda i: (0, 0)), _vecspec(D_CONV),
                  _vecspec(D_CONV), _vecspec(D_CONV)],
        out_specs=[_rowspec(t, D_CONV), _rowspec(t, D_CONV)],
        out_shape=[_sds((s, D_CONV), F32), _sds((s, D_CONV), MXU_DTYPE)],
        scratch_shapes=[pltpu.VMEM((t + HALO, D_CONV), F32)],
        compiler_params=_cp(("parallel",)),
    )(z, z, glu_b, dw_w, dw_b, ln_g, ln_b)


def _gate_cat(o, z, u4m, b_pw, *, name):
    s = o.shape[0]
    t = min(ROW_T, s)

    def body(o_ref, mg_ref, u4_ref, cg_ref, b_ref, cat_ref):
        cat_ref[:, :D_MLA] = (o_ref[...] * _silu(mg_ref[...])).astype(cat_ref.dtype)
        cat_ref[:, D_MLA:] = ((u4_ref[...] + b_ref[...]) * _silu(cg_ref[...])).astype(cat_ref.dtype)

    return pl.pallas_call(
        body, name=name, grid=(s // t,),
        in_specs=[_rowspec(t, D_MLA), _rowspec(t, D_MLA, SEG_MG[0] // D_MLA), _rowspec(t, D_CONV),
                  _rowspec(t, D_CONV, SEG_CG[0] // D_CONV), _vecspec(D_CONV)],
        out_specs=_rowspec(t, D_MLA + D_CONV), out_shape=_sds((s, D_MLA + D_CONV), MXU_DTYPE),
        compiler_params=_cp(("parallel",)),
    )(o, z, u4m, z, b_pw)


def _residual(x, y, gate, *, name):
    s, d = x.shape
    t = min(ROW_T, s)

    def body(x_ref, y_ref, g_ref, o_ref):
        o_ref[...] = x_ref[...] + g_ref[...] * y_ref[...]

    return pl.pallas_call(
        body, name=name, grid=(s // t,),
        in_specs=[_rowspec(t, d), _rowspec(t, d), _vecspec(d)],
        out_specs=_rowspec(t, d), out_shape=_sds((s, d), F32),
        compiler_params=_cp(("parallel",)),
    )(x, y, gate)


def _loss_head(xf, target, *, name):
    s, d = xf.shape
    t = min(ROW_T, s)

    def body(x_ref, t_ref, gx_ref, loss_ref):
        @pl.when(pl.program_id(0) == 0)
        def _():
            loss_ref[...] = jnp.zeros(loss_ref.shape, F32)

        err = x_ref[...] - t_ref[...]
        gx_ref[...] = err * (1.0 / d)
        loss_ref[...] += 0.5 * jnp.sum(_lanesum(err * err) * (1.0 / d), axis=0, keepdims=True)

    return pl.pallas_call(
        body, name=name, grid=(s // t,),
        in_specs=[_rowspec(t, d), _rowspec(t, d)],
        out_specs=[_rowspec(t, d), pl.BlockSpec((1, 1), lambda i: (0, 0))],
        out_shape=[_sds((s, d), F32), _sds((1, 1), F32)],
        compiler_params=_cp(("arbitrary",)),
    )(xf, target)


def _acc_init(refs):
    @pl.when(pl.program_id(0) == 0)
    def _():
        for r in refs:
            r[...] = jnp.zeros(r.shape, r.dtype)


def _out_bwd(gxo, y, gate, *, name):
    s, d = gxo.shape
    t = min(ROW_T, s)

    def body(g_ref, y_ref, gate_ref, dy_ref, dgate_ref):
        _acc_init([dgate_ref])
        gv = g_ref[...]
        dy_ref[...] = (gv * gate_ref[...]).astype(dy_ref.dtype)
        dgate_ref[...] += _colsum(gv * y_ref[...])

    return pl.pallas_call(
        body, name=name, grid=(s // t,),
        in_specs=[_rowspec(t, d), _rowspec(t, d), _vecspec(d)],
        out_specs=[_rowspec(t, d), _vecspec(d)],
        out_shape=[_sds((s, d), MXU_DTYPE), _sds((1, d), F32)],
        compiler_params=_cp(("arbitrary",)),
    )(gxo, y, gate)


def _gate_bwd(dcat, o, z, u4m, b_pw, *, name):
    s = o.shape[0]
    t = min(ROW_T, s)

    def body(dm_ref, dc_ref, o_ref, mg_ref, u4_ref, cg_ref, b_ref,
             do_ref, delta_ref, dmg_ref, du4_ref, dcg_ref, gb_ref):
        _acc_init([gb_ref])
        dm, ov, mg = dm_ref[...], o_ref[...], mg_ref[...]
        do = dm * _silu(mg)
        do_ref[...] = do.astype(do_ref.dtype)
        dmg_ref[...] = (dm * ov * _dsilu(mg)).astype(dmg_ref.dtype)
        prod = do * ov
        for h in range(N_HEADS):
            delta_ref[h] = _lanesum(prod[:, h * V_DIM:(h + 1) * V_DIM])
        dc, cg = dc_ref[...], cg_ref[...]
        du4 = dc * _silu(cg)
        du4_ref[...] = du4.astype(du4_ref.dtype)
        dcg_ref[...] = (dc * (u4_ref[...] + b_ref[...]) * _dsilu(cg)).astype(dcg_ref.dtype)
        gb_ref[...] += _colsum(du4)

    return pl.pallas_call(
        body, name=name, grid=(s // t,),
        in_specs=[_rowspec(t, D_MLA, 0), _rowspec(t, D_CONV, 1), _rowspec(t, D_MLA),
                  _rowspec(t, D_MLA, SEG_MG[0] // D_MLA), _rowspec(t, D_CONV),
                  _rowspec(t, D_CONV, SEG_CG[0] // D_CONV), _vecspec(D_CONV)],
        out_specs=[_rowspec(t, D_MLA), pl.BlockSpec((N_HEADS, t, 1), lambda i: (0, i, 0)),
                   _rowspec(t, D_MLA), _rowspec(t, D_CONV), _rowspec(t, D_CONV), _vecspec(D_CONV)],
        out_shape=[_sds((s, D_MLA), MXU_DTYPE), _sds((N_HEADS, s, 1), F32), _sds((s, D_MLA), MXU_DTYPE),
                   _sds((s, D_CONV), MXU_DTYPE), _sds((s, D_CONV), MXU_DTYPE), _sds((1, D_CONV), F32)],
        compiler_params=_cp(("arbitrary",)),
    )(dcat, dcat, o, z, u4m, z, b_pw)


def _conv_bwd(du3, u1, z, glu_b, dw_w, ln_g, ln_b, *, name):
    s = z.shape[0]
    t = min(CONV_T, s)
    c2 = 2 * D_CONV
    hb = t // HALO
    n_blk = s // t
    last_halo = s // HALO - 1

    def body(d3m_ref, d3h_ref, u1m_ref, u1h_ref, zm_ref, zh_ref, gb_ref, w_ref, g_ref, b_ref,
             dci_ref, gg_ref, gbn_ref, gwb_ref, ggb_ref, gw_ref, dext, uext, du0_s, gw_acc):
        i = pl.program_id(0)
        _acc_init([gg_ref, gbn_ref, gwb_ref, ggb_ref, gw_acc])

        def ln_bwd(d3, u1v):
            mu = jnp.mean(u1v, axis=-1, keepdims=True)
            cen = u1v - mu
            rstd = lax.rsqrt(jnp.mean(cen * cen, axis=-1, keepdims=True) + EPS)
            uh = cen * rstd
            d2 = d3 * _dsilu(uh * g_ref[...] + b_ref[...])
            dh = d2 * g_ref[...]
            d1 = rstd * (dh - jnp.mean(dh, axis=-1, keepdims=True) - uh * jnp.mean(dh * uh, axis=-1, keepdims=True))
            return d1, d2, uh

        d1, d2, uh = ln_bwd(d3m_ref[...], u1m_ref[...])
        gg_ref[...] += _colsum(d2 * uh)
        gbn_ref[...] += _colsum(d2)
        gwb_ref[...] += _colsum(d1)
        dext[0:t, :] = d1
        d1h, _, _ = ln_bwd(d3h_ref[...], u1h_ref[...])
        dext[t:, :] = jnp.where(i < n_blk - 1, d1h, 0.0)

        def glu_parts(zv):
            ci = zv + gb_ref[...]
            return ci[:, :D_CONV], jax.nn.sigmoid(ci[:, D_CONV:])

        val, sg = glu_parts(zm_ref[...])
        uext[HALO:, :] = val * sg
        valh, sgh = glu_parts(zh_ref[...])
        uext[0:HALO, :] = jnp.where(i > 0, valh * sgh, 0.0)

        for rc in range(0, t, CONV_RC):
            for lc in range(0, D_CONV, CONV_LC):
                acc = None
                dchunk = dext[rc:rc + CONV_RC, lc:lc + CONV_LC]
                for k in range(CONV_K):
                    term = w_ref[k:k + 1, lc:lc + CONV_LC] * dext[pl.ds(rc + (CONV_K - 1) - k, CONV_RC), lc:lc + CONV_LC]
                    acc = term if acc is None else acc + term
                    pr = dchunk * uext[pl.ds(rc + HALO - (CONV_K - 1) + k, CONV_RC), lc:lc + CONV_LC]
                    part = pr[0:8]
                    for r8 in range(8, CONV_RC, 8):
                        part = part + pr[r8:r8 + 8]
                    gw_acc[k, :, lc:lc + CONV_LC] += part
                du0_s[rc:rc + CONV_RC, lc:lc + CONV_LC] = acc

        du0 = du0_s[...]
        dval = du0 * sg
        dgt = du0 * val * sg * (1.0 - sg)
        dci_ref[:, :D_CONV] = dval.astype(dci_ref.dtype)
        dci_ref[:, D_CONV:] = dgt.astype(dci_ref.dtype)
        ggb_ref[:, :D_CONV] += _colsum(dval)
        ggb_ref[:, D_CONV:] += _colsum(dgt)

        @pl.when(i == n_blk - 1)
        def _():
            gw_ref[...] = jnp.sum(gw_acc[...], axis=1)

    halo_next = lambda w: pl.BlockSpec((HALO, w), lambda i: (jnp.minimum((i + 1) * hb, last_halo), 0))
    return pl.pallas_call(
        body, name=name, grid=(n_blk,),
        in_specs=[_rowspec(t, D_CONV), halo_next(D_CONV), _rowspec(t, D_CONV), halo_next(D_CONV),
                  _rowspec(t, c2), pl.BlockSpec((HALO, c2), lambda i: (jnp.maximum(i * hb - 1, 0), 0)),
                  _vecspec(c2), pl.BlockSpec((HALO, D_CONV), lambda i: (0, 0)), _vecspec(D_CONV), _vecspec(D_CONV)],
        out_specs=[_rowspec(t, c2), _vecspec(D_CONV), _vecspec(D_CONV), _vecspec(D_CONV), _vecspec(c2),
                   pl.BlockSpec((HALO, D_CONV), lambda i: (0, 0))],
        out_shape=[_sds((s, c2), MXU_DTYPE), _sds((1, D_CONV), F32), _sds((1, D_CONV), F32), _sds((1, D_CONV), F32),
                   _sds((1, c2), F32), _sds((HALO, D_CONV), F32)],
        scratch_shapes=[pltpu.VMEM((t + HALO, D_CONV), F32), pltpu.VMEM((t + HALO, D_CONV), F32),
                        pltpu.VMEM((t, D_CONV), F32), pltpu.VMEM((HALO, 8, D_CONV), F32)],
        compiler_params=_cp(("arbitrary",)),
    )(du3, du3, u1, u1, z, z, glu_b, dw_w, ln_g, ln_b)


def _flash_bwd(qf, kf, kft, va, do, lse_t, delta_t, *, name):
    nh, s, dk = qf.shape
    dv = va.shape[-1] // 2
    t = min(ATT_T, s)
    n = s // t
    nt = (((1,), (1,)), ((), ()))

    def body(q_ref, do_ref, lse_ref, dl_ref, k_ref, kt_ref, v_ref, dqt_ref, dk_ref, dv_ref,
             dk_s, dv_s, st_buf, dpt_buf):
        j = pl.program_id(1)

        @pl.when(j == 0)
        def _():
            dqt_ref[...] = jnp.zeros(dqt_ref.shape, F32)

        dk_s[...] = jnp.zeros(dk_s.shape, F32)
        dv_s[...] = jnp.zeros(dv_s.shape, F32)
        k, kt, v = k_ref[0], kt_ref[0], v_ref[0]
        n_un = n - 1 - j

        def rows_of(b):
            return pl.ds(pl.multiple_of((n - 1 - b) * t, t), t)

        def produce(b, slot):
            rows = rows_of(b)
            st_buf[slot] = lax.dot_general(k, q_ref[0, rows, :], nt, preferred_element_type=F32)
            dpt_buf[slot] = lax.dot_general(v, do_ref[rows, :], nt, preferred_element_type=F32)

        def consume(b, slot, masked):
            i = n - 1 - b
            rows = rows_of(b)
            q, dov = q_ref[0, rows, :], do_ref[rows, :]
            pt = jnp.exp(st_buf[slot] - lse_ref[0, i])
            if masked:
                key = lax.broadcasted_iota(jnp.int32, (t, t), 0)
                qry = lax.broadcasted_iota(jnp.int32, (t, t), 1)
                pt = jnp.where(key <= qry, pt, 0.0)
            dv_s[...] += jnp.dot(pt.astype(MXU_DTYPE), dov, preferred_element_type=F32)
            dst = (pt * (dpt_buf[slot] - dl_ref[0, i])).astype(MXU_DTYPE)
            dk_s[...] += jnp.dot(dst, q, preferred_element_type=F32)
            dqt_ref[0, i] += jnp.dot(kt, dst, preferred_element_type=F32)

        produce(0, 0)

        def pair(a, carry):
            produce(2 * a + 1, 1)
            consume(2 * a, 0, False)
            produce(2 * a + 2, 0)
            consume(2 * a + 1, 1, False)
            return carry

        lax.fori_loop(0, n_un // 2, pair, 0)

        @pl.when(n_un % 2 == 1)
        def _():
            produce(n_un, 1)
            consume(n_un - 1, 0, False)
            consume(n_un, 1, True)

        @pl.when(n_un % 2 == 0)
        def _():
            consume(n_un, 0, True)

        dk_ref[0] = dk_s[...]
        dv_ref[0] = dv_s[...]

    head = lambda h, j: (h, 0, 0)
    rowv = pl.BlockSpec((1, n, 1, t), lambda h, j: (h, 0, 0, 0))
    return pl.pallas_call(
        body, name=name, grid=(nh, n),
        in_specs=[pl.BlockSpec((1, s, dk), head),
                  pl.BlockSpec((s, dv), lambda h, j: (0, h)),
                  rowv, rowv,
                  pl.BlockSpec((1, t, dk), lambda h, j: (h, j, 0)),
                  pl.BlockSpec((1, dk, t), lambda h, j: (h, 0, j)),
                  pl.BlockSpec((1, t, dv), lambda h, j: (h, j, 0))],
        out_specs=[pl.BlockSpec((1, n, dk, t), lambda h, j: (h, 0, 0, 0)),
                   pl.BlockSpec((1, t, dk), lambda h, j: (h, j, 0)),
                   pl.BlockSpec((1, t, dv), lambda h, j: (h, j, 0))],
        out_shape=[_sds((nh, n, dk, t), F32), _sds((nh, s, dk), F32), _sds((nh, s, dv), F32)],
        scratch_shapes=[pltpu.VMEM((t, dk), F32), pltpu.VMEM((t, dv), F32),
                        pltpu.VMEM((2, t, t), F32), pltpu.VMEM((2, t, t), F32)],
        compiler_params=_cp(("parallel", "arbitrary")),
    )(qf, do, lse_t, delta_t, kf, kft, va)


def _qk_bwd(dqf, dkf, dvf, q_raw, kv, z, c_t, s1_t, s2_t, gqn, gqr, gkn, gkr, *, name):
    s = q_raw.shape[0]
    t = min(ROW_T, s)
    scale = 1.0 / math.sqrt(QK_DIM)

    def body(dq_ref, dk_ref, dv_ref, q_ref, kv_ref, kr_ref, c_ref, s1_ref, s2_ref,
             gqn_ref, gqr_ref, gkn_ref, gkr_ref, dqr_ref, dkv_ref, dkr_ref, ggq_ref, ggk_ref):
        _acc_init([ggq_ref, ggk_ref])
        c_v, s1_v, s2_v = c_ref[...], s1_ref[...], s2_ref[...]
        kr = kr_ref[...]
        kr_ss = _lanesum(kr * kr)
        dkr = jnp.zeros(kr.shape, F32)
        ggq_n = ggq_r = ggk_n = ggk_r = jnp.zeros((1, LANE), F32)

        def norm_bwd(n, r, rs, dyn, dyr, gn, gr):
            nh_, rh_ = n * rs, r * rs
            dnh, drh = dyn * gn, dyr * gr
            dot = (_lanesum(dnh * nh_) + _lanesum(drh * rh_)) * (1.0 / QK_DIM)
            return rs * (dnh - nh_ * dot), rs * (drh - rh_ * dot), _colsum(dyn * nh_), _colsum(dyr * rh_)

        for h in range(N_HEADS):
            n = q_ref[:, h * LANE:(h + 1) * LANE]
            r = q_ref[:, N_HEADS * LANE + h * LANE:N_HEADS * LANE + (h + 1) * LANE]
            rs = lax.rsqrt((_lanesum(n * n) + _lanesum(r * r)) * (1.0 / QK_DIM) + EPS)
            dyn = dq_ref[h, :, 0:LANE] * scale
            dyr = _rope_bwd(dq_ref[h, :, LANE:HEAD_PAD] * scale, c_v, s1_v, s2_v)
            dn, dr, g_n, g_r = norm_bwd(n, r, rs, dyn, dyr, gqn_ref[...], gqr_ref[...])
            dqr_ref[:, h * LANE:(h + 1) * LANE] = dn.astype(dqr_ref.dtype)
            dqr_ref[:, N_HEADS * LANE + h * LANE:N_HEADS * LANE + (h + 1) * LANE] = dr.astype(dqr_ref.dtype)
            ggq_n, ggq_r = ggq_n + g_n, ggq_r + g_r

            n = kv_ref[:, h * 2 * LANE:h * 2 * LANE + LANE]
            rs = lax.rsqrt((_lanesum(n * n) + kr_ss) * (1.0 / QK_DIM) + EPS)
            dyn = dk_ref[h, :, 0:LANE]
            dyr = _rope_bwd(dk_ref[h, :, LANE:HEAD_PAD], c_v, s1_v, s2_v)
            dn, dr, g_n, g_r = norm_bwd(n, kr, rs, dyn, dyr, gkn_ref[...], gkr_ref[...])
            dkv_ref[:, h * 2 * LANE:h * 2 * LANE + LANE] = dn.astype(dkv_ref.dtype)
            dkv_ref[:, h * 2 * LANE + LANE:(h + 1) * 2 * LANE] = dv_ref[h].astype(dkv_ref.dtype)
            dkr = dkr + dr
            ggk_n, ggk_r = ggk_n + g_n, ggk_r + g_r

        dkr_ref[...] = dkr.astype(dkr_ref.dtype)
        ggq_ref[:, 0:LANE] += ggq_n
        ggq_ref[:, LANE:] += ggq_r
        ggk_ref[:, 0:LANE] += ggk_n
        ggk_ref[:, LANE:] += ggk_r

    hspec = lambda w: pl.BlockSpec((N_HEADS, t, w), lambda i: (0, i, 0))
    wide = 2 * N_HEADS * LANE
    return pl.pallas_call(
        body, name=name, grid=(s // t,),
        in_specs=[hspec(HEAD_PAD), hspec(HEAD_PAD), hspec(V_DIM), _rowspec(t, wide), _rowspec(t, wide),
                  _rowspec(t, LANE, SEG_KR[0] // LANE), _rowspec(t, LANE), _rowspec(t, LANE), _rowspec(t, LANE),
                  _vecspec(LANE), _vecspec(LANE), _vecspec(LANE), _vecspec(LANE)],
        out_specs=[_rowspec(t, wide), _rowspec(t, wide), _rowspec(t, LANE), _vecspec(2 * LANE), _vecspec(2 * LANE)],
        out_shape=[_sds((s, wide), MXU_DTYPE), _sds((s, wide), MXU_DTYPE), _sds((s, LANE), MXU_DTYPE),
                   _sds((1, 2 * LANE), F32), _sds((1, 2 * LANE), F32)],
        compiler_params=_cp(("arbitrary",)),
    )(dqf, dkf, dvf, q_raw, kv, z, c_t, s1_t, s2_t, gqn, gqr, gkn, gkr)


def _lat_bwd(dqn, dkn, z, g_ql, g_kvl, *, name):
    s = z.shape[0]
    t = min(ROW_T, s)

    def body(dq_ref, dk_ref, ql_ref, kvl_ref, gq_ref, gk_ref, dql_ref, dkvl_ref, ggq_ref, ggk_ref):
        _acc_init([ggq_ref, ggk_ref])
        for d_ref, src, g_ref, dst, gg_ref in ((dq_ref, ql_ref, gq_ref, dql_ref, ggq_ref),
                                               (dk_ref, kvl_ref, gk_ref, dkvl_ref, ggk_ref)):
            v, dy = src[...], d_ref[...]
            r = lax.rsqrt(jnp.mean(v * v, axis=-1, keepdims=True) + EPS)
            vh = v * r
            dvh = dy * g_ref[...]
            dst[...] = (r * (dvh - vh * jnp.mean(dvh * vh, axis=-1, keepdims=True))).astype(dst.dtype)
            gg_ref[...] += _colsum(dy * vh)

    return pl.pallas_call(
        body, name=name, grid=(s // t,),
        in_specs=[_rowspec(t, Q_LORA), _rowspec(t, KV_LORA),
                  _rowspec(t, Q_LORA, SEG_QL[0] // Q_LORA), _rowspec(t, KV_LORA, SEG_KVL[0] // KV_LORA),
                  _vecspec(Q_LORA), _vecspec(KV_LORA)],
        out_specs=[_rowspec(t, Q_LORA), _rowspec(t, KV_LORA), _vecspec(Q_LORA), _vecspec(KV_LORA)],
        out_shape=[_sds((s, Q_LORA), MXU_DTYPE), _sds((s, KV_LORA), MXU_DTYPE),
                   _sds((1, Q_LORA), F32), _sds((1, KV_LORA), F32)],
        compiler_params=_cp(("arbitrary",)),
    )(dqn, dkn, z, z, g_ql, g_kvl)


def _prenorm_bwd(dh, x, gxo, g, sc1p, *, name):
    s, d = x.shape
    t = min(ROW_T, s)

    def body(dh_ref, x_ref, gx_ref, g_ref, sc_ref, dx_ref, dsh_ref, dsc_ref, gg_ref):
        _acc_init([dsh_ref, dsc_ref, gg_ref])
        xv, dhv = x_ref[...], dh_ref[...]
        r = lax.rsqrt(jnp.mean(xv * xv, axis=-1, keepdims=True) + EPS)
        xn = xv * r
        dsh_ref[...] += _colsum(dhv)
        dsc_ref[...] += _colsum(dhv * (xn * g_ref[...]))
        dm = dhv * sc_ref[...]
        gg_ref[...] += _colsum(dm * xn)
        dxn = dm * g_ref[...]
        dx_ref[...] = gx_ref[...] + r * (dxn - xn * jnp.mean(dxn * xn, axis=-1, keepdims=True))

    return pl.pallas_call(
        body, name=name, grid=(s // t,),
        in_specs=[_rowspec(t, d), _rowspec(t, d), _rowspec(t, d), _vecspec(d), _vecspec(d)],
        out_specs=[_rowspec(t, d), _vecspec(d), _vecspec(d), _vecspec(d)],
        out_shape=[_sds((s, d), F32), _sds((1, d), F32), _sds((1, d), F32), _sds((1, d), F32)],
        compiler_params=_cp(("arbitrary",)),
    )(dh, x, gxo, g, sc1p)


def _ada_fwd(c_all, ada_w, ada_b_cols, *, name):
    nl, d, cols = ada_w.shape

    def body(c_ref, w_ref, b_ref, o_ref):
        ca = _silu(c_ref[...]).astype(MXU_DTYPE)
        o_ref[0] = jnp.dot(ca, w_ref[0].astype(MXU_DTYPE), preferred_element_type=F32) + b_ref[0]

    return pl.pallas_call(
        body, name=name, grid=(nl,),
        in_specs=[pl.BlockSpec((N_DEV, d), lambda l: (0, 0)), pl.BlockSpec((1, d, cols), lambda l: (l, 0, 0)),
                  pl.BlockSpec((1, 1, cols), lambda l: (l, 0, 0))],
        out_specs=pl.BlockSpec((1, N_DEV, cols), lambda l: (l, 0, 0)),
        out_shape=_sds((nl, N_DEV, cols), F32),
        compiler_params=_cp(("parallel",)),
    )(c_all, ada_w, ada_b_cols)


def _ada_bwd(c_all_t, dmod_cols, *, name):
    nl, _, cols = dmod_cols.shape
    d = c_all_t.shape[0]

    def body(c_ref, dm_ref, o_ref):
        ca = _silu(c_ref[...]).astype(MXU_DTYPE)
        o_ref[0] = jnp.dot(ca, dm_ref[0].astype(MXU_DTYPE), preferred_element_type=F32)

    return pl.pallas_call(
        body, name=name, grid=(nl,),
        in_specs=[pl.BlockSpec((d, N_DEV), lambda l: (0, 0)), pl.BlockSpec((1, N_DEV, cols), lambda l: (l, 0, 0))],
        out_specs=pl.BlockSpec((1, d, cols), lambda l: (l, 0, 0)),
        out_shape=_sds((nl, d, cols), F32),
        compiler_params=_cp(("parallel",)),
    )(c_all_t, dmod_cols)


def _adamw(gparts, w, m, v, *, name):
    shape = w.shape
    cols = shape[-1]
    per_layer = isinstance(gparts, (list, tuple))
    nl = shape[0] if per_layer else 1
    rows = w.size // cols // nl
    glist = list(gparts) if per_layer else [gparts]
    npart = glist[0].shape[0]
    glist = [g.reshape(npart, rows, cols) for g in glist]
    w3, m3, v3 = (a.reshape(nl, rows, cols) for a in (w, m, v))
    budget = 2 * 1024 * 1024
    fits = [t for t in (256, 128, 64, 32, 16, 8)
            if rows % t == 0 and npart * t * cols * glist[0].dtype.itemsize <= budget]
    t = fits[0] if fits else rows
    nb = rows // t

    def body(*refs):
        g_refs = refs[:nl]
        w_ref, m_ref, v_ref, go_ref, d_ref, mo_ref, vo_ref, g_s = refs[nl:]
        layer = pl.program_id(0)
        for l in range(nl):
            @pl.when(layer == l)
            def _(l=l):
                g = g_refs[l][0].astype(F32)
                for p in range(1, npart):
                    g = g + g_refs[l][p].astype(F32)
                g_s[...] = g

        g = g_s[...]
        mn = ADAM_B1 * m_ref[0] + (1.0 - ADAM_B1) * g
        vn = ADAM_B2 * v_ref[0] + (1.0 - ADAM_B2) * (g * g)
        m_hat = mn / (1.0 - ADAM_B1 ** ADAM_STEP)
        v_hat = vn / (1.0 - ADAM_B2 ** ADAM_STEP)
        go_ref[0] = g
        d_ref[0] = -ADAM_LR * (m_hat / (jnp.sqrt(v_hat) + ADAM_EPS) + ADAM_WD * w_ref[0])
        mo_ref[0] = mn
        vo_ref[0] = vn

    def g_map(l):
        return lambda layer, i: (0, jnp.where(layer == l, i, jnp.where(layer < l, 0, nb - 1)), 0)

    spec = pl.BlockSpec((1, t, cols), lambda layer, i: (layer, i, 0))
    outs = pl.pallas_call(
        body, name=name, grid=(nl, nb),
        in_specs=[pl.BlockSpec((npart, t, cols), g_map(l)) for l in range(nl)] + [spec, spec, spec],
        out_specs=[spec] * 4, out_shape=[_sds((nl, rows, cols), F32)] * 4,
        scratch_shapes=[pltpu.VMEM((t, cols), F32)],
        compiler_params=_cp(("arbitrary", "arbitrary")),
    )(*glist, w3, m3, v3)
    return tuple(o.reshape(shape) for o in outs)


_ANY = pl.BlockSpec(memory_space=pl.ANY)


def _all_gather(blocks, *, name):
    na = len(blocks)

    def body(*refs):
        x_refs, out_refs = refs[:na], refs[na:2 * na]
        send_sems, recv_sems, local_sems = refs[2 * na:]
        x, y, c = lax.axis_index("x"), lax.axis_index("y"), lax.axis_index("c")
        me, sibling = (x, y, c), (x, y, 1 - c)
        chips = [(1 - x, y), (x, 1 - y), (1 - x, 1 - y)]

        def slot(a, px, py, pc):
            return out_refs[a].at[4 * px + 2 * py + pc]

        def copy(a, k, blk, to, src=None):
            return pltpu.make_async_remote_copy(
                src_ref=slot(a, *blk) if src is None else src, dst_ref=slot(a, *blk),
                send_sem=send_sems.at[7 * a + k], recv_sem=recv_sems.at[7 * a + k],
                device_id=to, device_id_type=MESH_ID)

        mine = [pltpu.make_async_copy(x_refs[a], slot(a, *me), local_sems.at[a]) for a in range(na)]
        for cp in mine:
            cp.start()
        first = []
        for a in range(na):
            first.append(copy(a, 0, me, sibling, src=x_refs[a]))
            first += [copy(a, 1 + j, me, (*chip, c), src=x_refs[a]) for j, chip in enumerate(chips)]
        for cp in first:
            cp.start()
        passed = []
        for a in range(na):
            for j, chip in enumerate(chips):
                copy(a, 1 + j, (*chip, c), me).wait_recv()
                fwd = copy(a, 4 + j, (*chip, c), sibling)
                fwd.start()
                passed.append(fwd)
        for a in range(na):
            copy(a, 0, sibling, me).wait_recv()
            for j, chip in enumerate(chips):
                copy(a, 4 + j, (*chip, 1 - c), me).wait_recv()
        for cp in first + passed:
            cp.wait_send()
        for cp in mine:
            cp.wait()

    outs = pl.pallas_call(
        body, name=name, in_specs=[_ANY] * na, out_specs=[_ANY] * na,
        out_shape=[_sds((N_DEV,) + b.shape, b.dtype) for b in blocks],
        scratch_shapes=[pltpu.SemaphoreType.DMA((7 * na,)), pltpu.SemaphoreType.DMA((7 * na,)),
                        pltpu.SemaphoreType.DMA((na,))],
    )(*blocks)
    return list(outs)


def _all_to_all(parts, *, name):
    na = len(parts)

    def body(*refs):
        in_refs, out_refs = refs[:na], refs[na:2 * na]
        send_sems, recv_sems, local_sems = refs[2 * na:]
        x, y, c = lax.axis_index("x"), lax.axis_index("y"), lax.axis_index("c")
        me = 4 * x + 2 * y + c
        mine = [pltpu.make_async_copy(in_refs[a].at[me], out_refs[a].at[me], local_sems.at[a]) for a in range(na)]
        for cp in mine:
            cp.start()
        copies = []
        for a in range(na):
            for k in range(1, N_DEV):
                px = 1 - x if k & 4 else x
                py = 1 - y if k & 2 else y
                pc = 1 - c if k & 1 else c
                cp = pltpu.make_async_remote_copy(
                    src_ref=in_refs[a].at[4 * px + 2 * py + pc], dst_ref=out_refs[a].at[me],
                    send_sem=send_sems.at[7 * a + k - 1], recv_sem=recv_sems.at[7 * a + k - 1],
                    device_id=(px, py, pc), device_id_type=MESH_ID)
                cp.start()
                copies.append(cp)
        for cp in copies:
            cp.wait()
        for cp in mine:
            cp.wait()

    outs = pl.pallas_call(
        body, name=name, in_specs=[_ANY] * na, out_specs=[_ANY] * na,
        out_shape=[_sds(p.shape, p.dtype) for p in parts],
        scratch_shapes=[pltpu.SemaphoreType.DMA((7 * na,)), pltpu.SemaphoreType.DMA((7 * na,)),
                        pltpu.SemaphoreType.DMA((na,))],
    )(*parts)
    return list(outs)


_HBM = pl.BlockSpec(memory_space=pltpu.HBM)
_SEM = pl.BlockSpec(memory_space=pltpu.SEMAPHORE)
_EFFECT = pltpu.SideEffectType.DATAFLOW_SIDE_EFFECTING


def _peers(x, y, c):
    out = []
    for k in range(1, N_DEV):
        out.append((1 - x if k & 4 else x, 1 - y if k & 2 else y, 1 - c if k & 1 else c))
    return out


def _own_slots(srcs, scatter, *, name, after=None):
    na = len(srcs)
    n_extra = 0 if after is None else 1

    def body(*refs):
        in_refs, out_refs, sems = refs[:na], refs[na + n_extra:2 * na + n_extra], refs[2 * na + n_extra]
        me = 4 * lax.axis_index("x") + 2 * lax.axis_index("y") + lax.axis_index("c")
        cps = [pltpu.make_async_copy(in_refs[a].at[me] if scatter else in_refs[a], out_refs[a].at[me], sems.at[a])
               for a in range(na)]
        for cp in cps:
            cp.start()
        for cp in cps:
            cp.wait()

    shapes = [s.shape if scatter else (N_DEV,) + s.shape for s in srcs]
    outs = pl.pallas_call(
        body, name=name, in_specs=[_ANY] * (na + n_extra), out_specs=[_ANY] * na,
        out_shape=[_sds(shp, s.dtype) for shp, s in zip(shapes, srcs)],
        scratch_shapes=[pltpu.SemaphoreType.DMA((na,))],
    )(*srcs, *([] if after is None else [after]))
    return list(outs)


def _exchange_copies(src_refs, land_refs, send_sems, recv_sems, scatter):
    x, y, c = lax.axis_index("x"), lax.axis_index("y"), lax.axis_index("c")
    me = 4 * x + 2 * y + c
    cps = []
    for a in range(len(src_refs)):
        for k, (px, py, pc) in enumerate(_peers(x, y, c)):
            src = src_refs[a].at[4 * px + 2 * py + pc] if scatter else src_refs[a]
            cps.append(pltpu.make_async_remote_copy(
                src_ref=src, dst_ref=land_refs[a].at[me], send_sem=send_sems.at[7 * a + k],
                recv_sem=recv_sems.at[7 * a + k], device_id=(px, py, pc), device_id_type=MESH_ID))
    return cps


def _exchange_start(srcs, lands, scatter, *, name):
    na = len(srcs)

    def body(*refs):
        src_refs, land_refs = refs[:na], refs[na:2 * na]
        send_sems, recv_sems = refs[2 * na], refs[2 * na + 1]
        token = refs[-1]
        for cp in _exchange_copies(src_refs, land_refs, send_sems, recv_sems, scatter):
            cp.start()
        token[...] = jnp.zeros(token.shape, token.dtype)

    hbm = lambda a: pltpu.HBM(a.shape, a.dtype)
    outs = pl.pallas_call(
        body, name=name,
        out_shape=(pltpu.SemaphoreType.DMA((7 * na,)), pltpu.SemaphoreType.DMA((7 * na,)),
                   *[hbm(a) for a in srcs], *[hbm(a) for a in lands], _sds((8, LANE), F32)),
        in_specs=[_HBM] * (2 * na),
        out_specs=(_SEM, _SEM, *[_HBM] * (2 * na), pl.BlockSpec(memory_space=pltpu.VMEM)),
        input_output_aliases={i: 2 + i for i in range(2 * na)},
        compiler_params=pltpu.CompilerParams(has_side_effects=_EFFECT),
    )(*[pltpu.with_memory_space_constraint(a, pltpu.HBM) for a in list(srcs) + list(lands)])
    return outs[0], outs[1], list(outs[2:2 + na]), list(outs[2 + na:2 + 2 * na]), outs[-1]


def _exchange_wait(send_sems, recv_sems, srcs, lands, after, scatter, *, name):
    na = len(srcs)

    def body(*refs):
        src_refs, land_refs = refs[:na], refs[na:2 * na]
        s_sems, r_sems = refs[2 * na], refs[2 * na + 1]
        for cp in _exchange_copies(src_refs, land_refs, s_sems, r_sems, scatter):
            cp.wait_send()
            cp.wait_recv()

    hbm = lambda a: pltpu.HBM(a.shape, a.dtype)
    outs = pl.pallas_call(
        body, name=name,
        out_shape=(*[hbm(a) for a in srcs], *[hbm(a) for a in lands]),
        in_specs=[_HBM] * (2 * na) + [_SEM, _SEM, _ANY],
        out_specs=tuple([_HBM] * (2 * na)),
        input_output_aliases={i: i for i in range(2 * na)},
        compiler_params=pltpu.CompilerParams(has_side_effects=_EFFECT),
    )(*srcs, *lands, send_sems, recv_sems, after)
    return list(outs[na:])


_WIN_SEGS = (("ql", 0, Q_LORA, SEG_QL[0]), ("kvl", Q_LORA, KV_LORA, SEG_KVL[0]),
             ("kr", Q_LORA + KV_LORA, ROPE, SEG_KR[0]), ("mg", Q_LORA + KV_LORA + ROPE, D_MLA, SEG_MG[0]),
             ("ci", Q_LORA + KV_LORA + ROPE + D_MLA, 2 * D_CONV, SEG_CI[0]),
             ("cg", Q_LORA + KV_LORA + ROPE + D_MLA + 2 * D_CONV, D_CONV, SEG_CG[0]))
_WIN_SHARD = IN_COLS // N_DEV


def _win_pieces():
    out = []
    for _, o, n, new in _WIN_SEGS:
        for j in range(N_DEV):
            lo, hi = max(o, j * _WIN_SHARD), min(o + n, (j + 1) * _WIN_SHARD)
            if lo < hi:
                out.append((j, lo - j * _WIN_SHARD, new + lo - o, hi - lo))
    return out


def _win_assemble(w_all, *, name):
    d = w_all.shape[1]
    t = min(ROW_T, d)
    pieces = sorted(_win_pieces(), key=lambda p: p[2])

    def body(w_ref, o_ref):
        cols = [w_ref[j, :, lo:lo + n].astype(F32) for j, lo, _, n in pieces]
        cols.append(jnp.zeros((t, IN_PAD - (SEG_KR[0] + ROPE)), F32))
        o_ref[...] = jnp.concatenate(cols, axis=1).astype(o_ref.dtype)

    return pl.pallas_call(
        body, name=name, grid=(d // t,),
        in_specs=[pl.BlockSpec((N_DEV, t, _WIN_SHARD), lambda i: (0, i, 0))],
        out_specs=_rowspec(t, IN_PAD), out_shape=_sds((d, IN_PAD), w_all.dtype),
        compiler_params=_cp(("parallel",)),
    )(w_all)


def _win_split(grad, *, name):
    d = grad.shape[0]
    t = min(ROW_T, d)
    by_shard = [sorted([p for p in _win_pieces() if p[0] == j], key=lambda p: p[1]) for j in range(N_DEV)]

    def body(g_ref, o_ref):
        g = g_ref[...]
        for j in range(N_DEV):
            cols = [g[:, new:new + n] for _, _, new, n in by_shard[j]]
            o_ref[j] = jnp.concatenate(cols, axis=1).astype(o_ref.dtype)

    return pl.pallas_call(
        body, name=name, grid=(d // t,),
        in_specs=[_rowspec(t, IN_PAD)],
        out_specs=pl.BlockSpec((N_DEV, t, _WIN_SHARD), lambda i: (0, i, 0)),
        out_shape=_sds((N_DEV, d, _WIN_SHARD), WIRE_DTYPE),
        compiler_params=_cp(("parallel",)),
    )(grad)


def _cols_to_shards(a):
    r, n = a.shape
    return a.reshape(r, N_DEV, n // N_DEV).transpose(1, 0, 2)


def _shards_to_cols(a):
    nd, r, w = a.shape
    return a.transpose(1, 0, 2).reshape(r, nd * w)


def _win_permute(w_in):
    o_ql, o_kvl, o_kr, o_mg = 0, Q_LORA, Q_LORA + KV_LORA, Q_LORA + KV_LORA + ROPE
    o_ci = o_mg + D_MLA
    o_cg = o_ci + 2 * D_CONV
    seg = lambda o, n: w_in[:, o:o + n]
    pad = jnp.zeros((w_in.shape[0], LANE - ROPE), w_in.dtype)
    return jnp.concatenate([seg(o_ci, 2 * D_CONV), seg(o_mg, D_MLA), seg(o_cg, D_CONV), seg(o_ql, Q_LORA),
                            seg(o_kvl, KV_LORA), seg(o_kr, ROPE), pad], axis=1)


def _win_unpermute(g):
    seg = lambda s, n=None: g[:, s[0]:s[0] + (s[1] if n is None else n)]
    return jnp.concatenate([seg(SEG_QL), seg(SEG_KVL), seg(SEG_KR, ROPE), seg(SEG_MG), seg(SEG_CI), seg(SEG_CG)], axis=1)


def _qup_permute(w):
    w3 = w.reshape(w.shape[0], N_HEADS, QK_DIM)
    nope = w3[:, :, :NOPE].reshape(w.shape[0], N_HEADS * NOPE)
    rope = jnp.pad(w3[:, :, NOPE:], ((0, 0), (0, 0), (0, LANE - ROPE))).reshape(w.shape[0], N_HEADS * LANE)
    return jnp.concatenate([nope, rope], axis=1)


def _qup_unpermute(g):
    r = g.shape[0]
    nope = g[:, :N_HEADS * NOPE].reshape(r, N_HEADS, NOPE)
    rope = g[:, N_HEADS * NOPE:].reshape(r, N_HEADS, LANE)[:, :, :ROPE]
    return jnp.concatenate([nope, rope], axis=2).reshape(r, N_HEADS * QK_DIM)


def _norm_tiles(g):
    return g[:NOPE].reshape(1, LANE), jnp.pad(g[NOPE:], (0, LANE - ROPE)).reshape(1, LANE)


def _norm_untile(gt):
    return jnp.concatenate([gt[0, :NOPE], gt[0, LANE:LANE + ROPE]])


def _rope_tiles(positions):
    inv_freq = 1.0 / (ROPE_THETA ** (jnp.arange(0, ROPE, 2, dtype=F32) / ROPE))
    ang = positions.astype(F32)[:, None] * inv_freq
    cos, sin = jnp.cos(ang), jnp.sin(ang)
    zq = jnp.zeros_like(cos)
    c_t = jnp.concatenate([cos, cos, zq, zq], axis=1)
    s1_t = jnp.concatenate([-sin, zq, zq, zq], axis=1)
    s2_t = jnp.concatenate([zq, sin, zq, zq], axis=1)
    return c_t, s1_t, s2_t


_BIG = ("w_in", "w_q_up", "w_kv_up", "w_pw", "w_out")
_COL_SHARDED = ("w_in", "w_q_up", "w_kv_up")


def _pack_rows(arrs):
    return jnp.concatenate([a.reshape(-1, LANE) for a in arrs], axis=0)


def _unpack_rows(buf, shapes):
    out, r0 = [], 0
    lead = buf.shape[:-2]
    for shp in shapes:
        n = math.prod(shp) // LANE
        out.append(buf[..., r0:r0 + n, :].reshape(lead + tuple(shp)))
        r0 += n
    return out


_SMALL = (("dmod", 3 * D_MODEL), ("norm_g", D_MODEL), ("q_lat_g", Q_LORA), ("kv_lat_g", KV_LORA),
          ("q_norm_g", 2 * LANE), ("k_norm_g", 2 * LANE), ("glu_b", 2 * D_CONV), ("dw_w", HALO * D_CONV),
          ("dw_b", D_CONV), ("conv_ln_g", D_CONV), ("conv_ln_b", D_CONV), ("b_pw", D_CONV))


def _layer_fwd(x, p, rope, l):
    n = lambda s: f"{s}_l{l}"
    c_t, s1_t, s2_t = rope
    h = _prenorm(x, p["norm_g"], p["shift"], p["sc1p"], name=n("prenorm"))
    z = _mm(h, p["w_in"], name=n("in_proj"), tn=IN_PAD // 3, n_outer=True)
    qn, kn = _lat_norm(z, p["q_lat_g"], p["kv_lat_g"], name=n("lat_norm"))
    q_raw = _mm(qn, p["w_q_up"], name=n("q_up"), tn=1024)
    kv = _mm(kn, p["w_kv_up"], name=n("kv_up"), tn=1024)
    qf, kf, vf = _qk_prep(q_raw, kv, z, c_t, s1_t, s2_t, *p["qk_tiles"], name=n("qk_prep"))
    o, lse = _flash_fwd(qf, kf, vf, name=n("flash_fwd"))
    u1, u3 = _conv_fwd(z, p["glu_b"], p["dw_w"], p["dw_b"], p["conv_ln_g"], p["conv_ln_b"], name=n("conv_fwd"))
    u4m = _mm(u3, p["w_pw"], name=n("pw"), tn=1024)
    cat = _gate_cat(o, z, u4m, p["b_pw"], name=n("gate_cat"))
    y = _mm(cat, p["w_out"], name=n("out_proj"), tn=1024)
    x_next = _residual(x, y, p["gate"], name=n("residual"))
    saved = dict(x=x, h=h, z=z, qn=qn, kn=kn, q_raw=q_raw, kv=kv, qf=qf, kf=kf, vf=vf, o=o, lse=lse,
                 u1=u1, u3=u3, u4m=u4m, cat=cat, y=y)
    return x_next, saved


def _layer_bwd(gxo, p, sv, rope, l, hook=None):
    n = lambda s: f"{s}_l{l}"
    c_t, s1_t, s2_t = rope
    z = sv["z"]
    dy, dgate = _out_bwd(gxo, sv["y"], p["gate"], name=n("out_bwd"))
    g_w_out = _mm(sv["cat"], dy, ta=True, name=n("g_w_out"), tm=1024, tn=1024, tk=512)
    dcat = _mm(dy, p["w_out"], tb=True, name=n("d_cat"), tn=1024)
    do, delta, dmg, du4, dcg, g_b_pw = _gate_bwd(dcat, sv["o"], z, sv["u4m"], p["b_pw"], name=n("gate_bwd"))
    g_w_pw = _mm(sv["u3"], du4, ta=True, name=n("g_w_pw"), tm=1024, tn=1024, tk=512)
    du3 = _mm(du4, p["w_pw"], tb=True, name=n("d_u3"), tn=1024)
    dci, g_ln_g, g_ln_b, g_dw_b, g_glu_b, g_dw_w = _conv_bwd(
        du3, sv["u1"], z, p["glu_b"], p["dw_w"], p["conv_ln_g"], p["conv_ln_b"], name=n("conv_bwd"))
    t_att = min(ATT_T, z.shape[0])
    to_lanes = lambda a: a.reshape(N_HEADS, z.shape[0] // t_att, 1, t_att)
    dqt, dkf, dvf = _flash_bwd(sv["qf"], sv["kf"], jnp.swapaxes(sv["kf"], 1, 2), sv["vf"], do,
                               to_lanes(sv["lse"][:, :, 0]), to_lanes(delta), name=n("flash_bwd"))
    dqf = jnp.swapaxes(dqt, 2, 3).reshape(N_HEADS, z.shape[0], HEAD_PAD)
    dq_raw, dkv, dkr, g_qn, g_kn = _qk_bwd(dqf, dkf, dvf, sv["q_raw"], sv["kv"], z, c_t, s1_t, s2_t,
                                            *p["qk_tiles"], name=n("qk_bwd"))
    g_w_q_up = _mm(sv["qn"], dq_raw, ta=True, name=n("g_w_q_up"), tm=512, tn=1024, tk=512)
    dqn = _mm(dq_raw, p["w_q_up"], tb=True, name=n("d_qn"))
    g_w_kv_up = _mm(sv["kn"], dkv, ta=True, name=n("g_w_kv_up"), tm=256, tn=1024, tk=512)
    dkn = _mm(dkv, p["w_kv_up"], tb=True, name=n("d_kn"))
    dql, dkvl, g_ql, g_kvl = _lat_bwd(dqn, dkn, z, p["q_lat_g"], p["kv_lat_g"], name=n("lat_bwd"))
    dz = jnp.concatenate([dci, dmg, dcg, dql, dkvl, dkr], axis=1)
    g_w_in = _mm(sv["h"], dz, ta=True, name=n("g_w_in"), tm=1024, tn=IN_PAD // 3, tk=512)
    big = dict(w_in=g_w_in, w_q_up=g_w_q_up, w_kv_up=g_w_kv_up, w_pw=g_w_pw, w_out=g_w_out)
    after = None if hook is None else hook(big)
    dh = _mm(dz, p["w_in"], tb=True, name=n("d_h"), tn=1024, tk=IN_PAD // 3, after=after)
    dx, dshift, dscale, g_norm = _prenorm_bwd(dh, sv["x"], gxo, p["norm_g"], p["sc1p"], name=n("prenorm_bwd"))
    small = dict(dmod=jnp.concatenate([dshift, dscale, dgate], axis=1), norm_g=g_norm, q_lat_g=g_ql, kv_lat_g=g_kvl,
                 q_norm_g=g_qn, k_norm_g=g_kn, glu_b=g_glu_b, dw_w=g_dw_w, dw_b=g_dw_b,
                 conv_ln_g=g_ln_g, conv_ln_b=g_ln_b, b_pw=g_b_pw)
    return dx, big, small


def _layer_params(l, full, mod_l, small):
    d = D_MODEL
    row = lambda a: a.reshape(1, -1)
    shift, scale, gate = mod_l[:, :d], mod_l[:, d:2 * d], mod_l[:, 2 * d:]
    dw_w = jnp.pad(full["dw_w"][l], ((0, HALO - CONV_K), (0, 0)))
    return dict(
        shift=shift, sc1p=1.0 + scale, gate=gate, norm_g=row(small["norm_g"][l]),
        w_in=full["w_in"][l], w_q_up=full["w_q_up"][l], w_kv_up=full["w_kv_up"][l],
        w_pw=full["w_pw"][l], w_out=full["w_out"][l], dw_w=dw_w,
        q_lat_g=row(small["q_lat_g"][l]), kv_lat_g=row(small["kv_lat_g"][l]),
        qk_tiles=_norm_tiles(small["q_norm_g"][l]) + _norm_tiles(small["k_norm_g"][l]),
        glu_b=row(small["glu_b"][l]), dw_b=row(small["dw_b"][l]), conv_ln_g=row(small["conv_ln_g"][l]),
        conv_ln_b=row(small["conv_ln_b"][l]), b_pw=row(small["b_pw"][l]))


def kernel(x, c, positions, ada_w, ada_b, norm_g, w_in, q_lat_g, w_q_up, kv_lat_g, w_kv_up, q_norm_g, k_norm_g, glu_b, dw_w, dw_b, conv_ln_g, conv_ln_b, w_pw, b_pw, w_out, loss_target, m_ada_w, m_ada_b, m_norm_g, m_w_in, m_q_lat_g, m_w_q_up, m_kv_lat_g, m_w_kv_up, m_q_norm_g, m_k_norm_g, m_glu_b, m_dw_w, m_dw_b, m_conv_ln_g, m_conv_ln_b, m_w_pw, m_b_pw, m_w_out, v_ada_w, v_ada_b, v_norm_g, v_w_in, v_q_lat_g, v_w_q_up, v_kv_lat_g, v_w_kv_up, v_q_norm_g, v_k_norm_g, v_glu_b, v_dw_w, v_dw_b, v_conv_ln_g, v_conv_ln_b, v_w_pw, v_b_pw, v_w_out):
    names = ("ada_w", "ada_b", "norm_g", "w_in", "q_lat_g", "w_q_up", "kv_lat_g", "w_kv_up", "q_norm_g",
             "k_norm_g", "glu_b", "dw_w", "dw_b", "conv_ln_g", "conv_ln_b", "w_pw", "b_pw", "w_out")
    w_loc = dict(zip(names, (ada_w, ada_b, norm_g, w_in, q_lat_g, w_q_up, kv_lat_g, w_kv_up, q_norm_g, k_norm_g,
                             glu_b, dw_w, dw_b, conv_ln_g, conv_ln_b, w_pw, b_pw, w_out)))
    m_loc = dict(zip(names, (m_ada_w, m_ada_b, m_norm_g, m_w_in, m_q_lat_g, m_w_q_up, m_kv_lat_g, m_w_kv_up,
                             m_q_norm_g, m_k_norm_g, m_glu_b, m_dw_w, m_dw_b, m_conv_ln_g, m_conv_ln_b, m_w_pw,
                             m_b_pw, m_w_out)))
    v_loc = dict(zip(names, (v_ada_w, v_ada_b, v_norm_g, v_w_in, v_q_lat_g, v_w_q_up, v_kv_lat_g, v_w_kv_up,
                             v_q_norm_g, v_k_norm_g, v_glu_b, v_dw_w, v_dw_b, v_conv_ln_g, v_conv_ln_b, v_w_pw,
                             v_b_pw, v_w_out)))
    nl, d = N_LAYERS, D_MODEL
    me = 4 * lax.axis_index("x") + 2 * lax.axis_index("y") + lax.axis_index("c")
    x2, tgt = x[0], loss_target[0]
    ada_cols = ada_w.shape[-1]

    c_all = _all_gather([c.reshape(d // LANE, LANE)], name="gather_c")[0].reshape(N_DEV, d)
    ada_b_cols = lax.dynamic_slice_in_dim(ada_b, me * ada_cols, ada_cols, axis=1).reshape(nl, 1, ada_cols)
    mod_cols = _ada_fwd(c_all, ada_w, ada_b_cols, name="ada_fwd")
    mod_all = _all_gather([mod_cols], name="gather_mod")[0]
    mod_me = lax.dynamic_index_in_dim(mod_all, me, axis=2, keepdims=False)
    mod = mod_me.transpose(1, 0, 2).reshape(nl, 1, N_DEV * ada_cols)

    dw_pad = jnp.pad(dw_w, ((0, 0), (0, HALO - CONV_K), (0, 0)))
    wire = {k: w_loc[k].astype(WIRE_DTYPE) for k in _BIG}
    gathered = _all_gather([wire[k][0] for k in _BIG] + [dw_pad], name="gather_weights_l0")
    dw_all = gathered[-1]
    src1 = [wire[k][1] for k in _BIG]
    lands1 = _own_slots(src1, False, name="own_weights_l1", after=gathered[0])
    gsend, grecv, src1, lands1, tok_w1 = _exchange_start(src1, lands1, False, name="gather_start_l1")

    def kernel_layout(parts, l):
        return dict(w_in=_win_assemble(parts["w_in"], name=f"w_in_assemble_l{l}"),
                    w_q_up=_qup_permute(_shards_to_cols(parts["w_q_up"])),
                    w_kv_up=_shards_to_cols(parts["w_kv_up"]),
                    w_pw=parts["w_pw"].reshape(D_CONV, D_CONV),
                    w_out=parts["w_out"].reshape(D_MLA + D_CONV, d))

    small_in = dict(norm_g=norm_g, q_lat_g=q_lat_g, kv_lat_g=kv_lat_g, q_norm_g=q_norm_g, k_norm_g=k_norm_g,
                    glu_b=glu_b, dw_b=dw_b, conv_ln_g=conv_ln_g, conv_ln_b=conv_ln_b, b_pw=b_pw)
    dw_full = [_shards_to_cols(dw_all[:, l])[:CONV_K] for l in range(nl)]
    rope = _rope_tiles(positions[0])

    def layer_params(l, parts, mod_l):
        full = {k: {l: a} for k, a in kernel_layout(parts, l).items()}
        full["dw_w"] = dw_full
        return _layer_params(l, full, mod_l, small_in)

    params, saved = [None] * nl, [None] * nl
    params[0] = layer_params(0, dict(zip(_BIG, gathered[:-1])), mod[0] + tok_w1[0, 0])
    xs, saved[0] = _layer_fwd(x2, params[0], rope, 0)
    parts1 = _exchange_wait(gsend, grecv, src1, lands1, xs, False, name="gather_wait_l1")
    params[1] = layer_params(1, dict(zip(_BIG, parts1)), mod[1])
    xs, saved[1] = _layer_fwd(xs, params[1], rope, 1)
    gx, loss_part = _loss_head(xs, tgt, name="loss_head")
    loss = lax.psum(loss_part[0, 0], ("x", "y", "c"))

    def shard_major(k, g):
        if k == "w_q_up":
            g = _qup_unpermute(g)
        if k in _COL_SHARDED:
            return _cols_to_shards(g)
        return g.reshape((N_DEV, g.shape[0] // N_DEV, g.shape[1]))

    def scatter_start(big, l):
        send = [_win_split(big["w_in"], name=f"w_in_split_l{l}")]
        send += [shard_major(k, big[k]).astype(WIRE_DTYPE) for k in _BIG[1:]]
        lands = _own_slots(send, True, name=f"own_grads_l{l}")
        return _exchange_start(send, lands, True, name=f"scatter_start_l{l}")

    big_g, small_g, flying = [None] * nl, [None] * nl, [None] * nl
    gx, big_g[1], small_g[1] = _layer_bwd(gx, params[1], saved[1], rope, 1)
    flying[1] = scatter_start(big_g[1], 1)
    p0 = dict(params[0])
    p0["gate"] = p0["gate"] + flying[1][4][0, 0]

    def start_l0(big):
        flying[0] = scatter_start(big, 0)
        return flying[0][4]

    gx, big_g[0], small_g[0] = _layer_bwd(gx, p0, saved[0], rope, 0, hook=start_l0)

    tile = 8 * LANE
    padded = [(k, nn, -(-nn // tile) * tile) for k, nn in _SMALL]
    spk = jnp.concatenate([jnp.pad(small_g[l][k].reshape(-1), (0, np_ - nn)).reshape(-1, LANE)
                           for l in range(nl) for k, nn, np_ in padded], axis=0)
    s_all = _all_gather([spk], name="gather_small_grads")[0]
    s_rows = sum(np_ for _, _, np_ in padded) // LANE
    s_all = s_all.reshape(N_DEV, nl, s_rows, LANE)
    s_parts = {k: a[..., :nn] for (k, nn, _), a in
               zip(padded, _unpack_rows(s_all, [(np_,) for _, _, np_ in padded]))}

    dmod_all = s_parts["dmod"]
    dmod_cols = lax.dynamic_slice_in_dim(dmod_all, me * ada_cols, ada_cols, axis=2).transpose(1, 0, 2)
    g_ada_w = _ada_bwd(c_all.T, dmod_cols, name="ada_bwd")
    gp = {}
    gp["ada_w"] = g_ada_w[None]
    gp["ada_b"] = dmod_all
    for k in ("norm_g", "q_lat_g", "kv_lat_g", "glu_b", "dw_b", "conv_ln_g", "conv_ln_b", "b_pw"):
        gp[k] = s_parts[k]
    for k in ("q_norm_g", "k_norm_g"):
        t = s_parts[k]
        gp[k] = jnp.concatenate([t[..., :NOPE], t[..., LANE:LANE + ROPE]], axis=-1)
    dw_g = s_parts["dw_w"].reshape(N_DEV, nl, HALO, D_CONV)[:, :, :CONV_K]
    gp["dw_w"] = lax.dynamic_slice_in_dim(dw_g, me * LANE, LANE, axis=3)

    res = {k: _adamw(gp[k], w_loc[k], m_loc[k], v_loc[k], name=f"adamw_{k}") for k in names if k not in _BIG}
    arrived = [None] * nl
    arrived[1] = _exchange_wait(*flying[1][:4], gx, True, name="scatter_wait_l1")
    arrived[0] = _exchange_wait(*flying[0][:4], res["ada_w"][1], True, name="scatter_wait_l0")
    for i, k in enumerate(_BIG):
        res[k] = _adamw([arrived[l][i] for l in range(nl)], w_loc[k], m_loc[k], v_loc[k], name=f"adamw_{k}")
    out = [loss, gx[None]]
    for idx in range(4):
        out += [res[k][idx] for k in names]
    return tuple(out)
```

```python
import functools
import math

import jax
import jax.numpy as jnp
from jax import lax
from jax.experimental import pallas as pl
from jax.experimental.pallas import tpu as pltpu

F32 = jnp.float32
MXU_DTYPE = jnp.bfloat16
WIRE_DTYPE = jnp.bfloat16

D_MODEL = 2048
N_LAYERS = 2
N_DEV = 8
N_HEADS = 8
NOPE = 128
ROPE = 64
V_DIM = 128
QK_DIM = NOPE + ROPE
Q_LORA = 512
KV_LORA = 256
D_MLA = N_HEADS * V_DIM
D_CONV = 1024
CONV_K = 31
ROPE_THETA = 10000.0
EPS = 1e-6
LANE = 128
HEAD_PAD = 2 * LANE
HALO = 32

SEG_CI = (0, 2 * D_CONV)
SEG_MG = (2 * D_CONV, D_MLA)
SEG_CG = (2 * D_CONV + D_MLA, D_CONV)
SEG_QL = (2 * D_CONV + D_MLA + D_CONV, Q_LORA)
SEG_KVL = (SEG_QL[0] + Q_LORA, KV_LORA)
SEG_KR = (SEG_KVL[0] + KV_LORA, LANE)
IN_PAD = SEG_KR[0] + LANE
IN_COLS = Q_LORA + KV_LORA + ROPE + D_MLA + 2 * D_CONV + D_CONV

ADAM_LR = 0.001
ADAM_B1 = 0.9
ADAM_B2 = 0.999
ADAM_EPS = 1e-08
ADAM_WD = 0.01
ADAM_STEP = 10

VMEM_LIMIT = 56 * 1024 * 1024
ATT_T = 512
ROW_T = 256
CONV_T = 128
MESH_ID = pl.DeviceIdType.MESH


def _cp(sem=None):
    kw = dict(vmem_limit_bytes=VMEM_LIMIT)
    if sem is not None:
        kw["dimension_semantics"] = sem
    return pltpu.CompilerParams(**kw)


def _sds(shape, dtype):
    return jax.ShapeDtypeStruct(shape, dtype)


def _silu(x):
    return x * jax.nn.sigmoid(x)


def _dsilu(x):
    s = jax.nn.sigmoid(x)
    return s * (1.0 + x * (1.0 - s))


def _rowspec(t, width, col=0):
    return pl.BlockSpec((t, width), lambda i: (i, col))


def _vecspec(width):
    return pl.BlockSpec((1, width), lambda i: (0, 0))


def _colsum(v):
    return jnp.sum(v, axis=0, keepdims=True)


def _mm(a, b, *, name, ta=False, tb=False, out_dtype=F32, tm=512, tn=512, tk=None, n_outer=False, after=None):
    if ta:
        kdim, m = a.shape
    else:
        m, kdim = a.shape
    if tb:
        n, k2 = b.shape
    else:
        k2, n = b.shape
    assert kdim == k2, (a.shape, b.shape)
    tm, tn = min(tm, m), min(tn, n)
    tk = kdim if tk is None else min(tk, kdim)
    assert m % tm == 0 and n % tn == 0 and kdim % tk == 0, (m, n, kdim, tm, tn, tk)
    nk = kdim // tk
    dims = (((0 if ta else 1,), (1 if tb else 0,)), ((), ()))

    n_extra = 0 if after is None else 1

    def body(a_ref, b_ref, *rest):
        o_ref, scratch = rest[n_extra], rest[n_extra + 1:]
        prod = lax.dot_general(a_ref[...].astype(MXU_DTYPE), b_ref[...].astype(MXU_DTYPE), dims,
                               preferred_element_type=F32)
        if nk == 1:
            o_ref[...] = prod.astype(o_ref.dtype)
        else:
            acc = scratch[0]
            k = pl.program_id(2)

            @pl.when(k == 0)
            def _():
                acc[...] = prod

            @pl.when(k > 0)
            def _():
                acc[...] += prod

            @pl.when(k == nk - 1)
            def _():
                o_ref[...] = acc[...].astype(o_ref.dtype)

    if n_outer:
        ij = lambda g0, g1: (g1, g0)
        grid = (n // tn, m // tm, nk)
    else:
        ij = lambda g0, g1: (g0, g1)
        grid = (m // tm, n // tn, nk)

    def a_map(g0, g1, k):
        i, _ = ij(g0, g1)
        return (k, i) if ta else (i, k)

    def b_map(g0, g1, k):
        _, j = ij(g0, g1)
        return (j, k) if tb else (k, j)

    def o_map(g0, g1, k):
        return ij(g0, g1)

    return pl.pallas_call(
        body, name=name, grid=grid,
        in_specs=[pl.BlockSpec((tk, tm) if ta else (tm, tk), a_map),
                  pl.BlockSpec((tn, tk) if tb else (tk, tn), b_map)] + [_ANY] * n_extra,
        out_specs=pl.BlockSpec((tm, tn), o_map),
        out_shape=_sds((m, n), out_dtype),
        scratch_shapes=[pltpu.VMEM((tm, tn), F32)] if nk > 1 else [],
        compiler_params=_cp(("parallel", "parallel", "arbitrary")),
    )(a, b, *([] if after is None else [after]))


def _prenorm(x, g, shift, sc1p, *, name):
    s, d = x.shape
    t = min(ROW_T, s)

    def body(x_ref, g_ref, sh_ref, sc_ref, h_ref):
        xv = x_ref[...]
        r = lax.rsqrt(jnp.mean(xv * xv, axis=-1, keepdims=True) + EPS)
        h_ref[...] = ((xv * r) * g_ref[...] * sc_ref[...] + sh_ref[...]).astype(h_ref.dtype)

    return pl.pallas_call(
        body, name=name, grid=(s // t,),
        in_specs=[_rowspec(t, d), _vecspec(d), _vecspec(d), _vecspec(d)],
        out_specs=_rowspec(t, d), out_shape=_sds((s, d), MXU_DTYPE),
        compiler_params=_cp(("parallel",)),
    )(x, g, shift, sc1p)


def _lat_norm(z, g_ql, g_kvl, *, name):
    s = z.shape[0]
    t = min(ROW_T, s)

    def body(ql_ref, kvl_ref, gq_ref, gk_ref, qn_ref, kn_ref):
        for src, g_ref, dst in ((ql_ref, gq_ref, qn_ref), (kvl_ref, gk_ref, kn_ref)):
            v = src[...]
            r = lax.rsqrt(jnp.mean(v * v, axis=-1, keepdims=True) + EPS)
            dst[...] = ((v * r) * g_ref[...]).astype(dst.dtype)

    return pl.pallas_call(
        body, name=name, grid=(s // t,),
        in_specs=[_rowspec(t, Q_LORA, SEG_QL[0] // Q_LORA), _rowspec(t, KV_LORA, SEG_KVL[0] // KV_LORA),
                  _vecspec(Q_LORA), _vecspec(KV_LORA)],
        out_specs=[_rowspec(t, Q_LORA), _rowspec(t, KV_LORA)],
        out_shape=[_sds((s, Q_LORA), MXU_DTYPE), _sds((s, KV_LORA), MXU_DTYPE)],
        compiler_params=_cp(("parallel",)),
    )(z, z, g_ql, g_kvl)


def _rope_fwd(r, c_t, s1_t, s2_t):
    return r * c_t + pltpu.roll(r, LANE - ROPE // 2, 1) * s1_t + pltpu.roll(r, ROPE // 2, 1) * s2_t


def _rope_bwd(d, c_t, s1_t, s2_t):
    return d * c_t + pltpu.roll(d * s1_t, ROPE // 2, 1) + pltpu.roll(d * s2_t, LANE - ROPE // 2, 1)


def _lanesum(v):
    return jnp.sum(v, axis=-1, keepdims=True)


def _qk_prep(q_raw, kv, z, c_t, s1_t, s2_t, gqn, gqr, gkn, gkr, *, name):
    s = q_raw.shape[0]
    t = min(ROW_T, s)
    scale = 1.0 / math.sqrt(QK_DIM)

    def body(q_ref, kv_ref, kr_ref, c_ref, s1_ref, s2_ref, gqn_ref, gqr_ref, gkn_ref, gkr_ref,
             qf_ref, kf_ref, vf_ref):
        c_v, s1_v, s2_v = c_ref[...], s1_ref[...], s2_ref[...]
        kr = kr_ref[...]
        kr_ss = _lanesum(kr * kr)
        for h in range(N_HEADS):
            n = q_ref[:, h * LANE:(h + 1) * LANE]
            r = q_ref[:, N_HEADS * LANE + h * LANE:N_HEADS * LANE + (h + 1) * LANE]
            rs = lax.rsqrt((_lanesum(n * n) + _lanesum(r * r)) * (1.0 / QK_DIM) + EPS)
            qf_ref[h, :, 0:LANE] = (((n * rs) * gqn_ref[...]) * scale).astype(qf_ref.dtype)
            rr = _rope_fwd((r * rs) * gqr_ref[...], c_v, s1_v, s2_v)
            qf_ref[h, :, LANE:HEAD_PAD] = (rr * scale).astype(qf_ref.dtype)

            n = kv_ref[:, h * 2 * LANE:h * 2 * LANE + LANE]
            rs = lax.rsqrt((_lanesum(n * n) + kr_ss) * (1.0 / QK_DIM) + EPS)
            kf_ref[h, :, 0:LANE] = ((n * rs) * gkn_ref[...]).astype(kf_ref.dtype)
            kf_ref[h, :, LANE:HEAD_PAD] = _rope_fwd((kr * rs) * gkr_ref[...], c_v, s1_v, s2_v).astype(kf_ref.dtype)
            vf_ref[h, :, 0:V_DIM] = kv_ref[:, h * 2 * LANE + LANE:(h + 1) * 2 * LANE].astype(vf_ref.dtype)
            vf_ref[h, :, V_DIM:] = jnp.ones((t, V_DIM), vf_ref.dtype)

    hspec = lambda w: pl.BlockSpec((N_HEADS, t, w), lambda i: (0, i, 0))
    return pl.pallas_call(
        body, name=name, grid=(s // t,),
        in_specs=[_rowspec(t, 2 * N_HEADS * LANE), _rowspec(t, 2 * N_HEADS * LANE),
                  _rowspec(t, LANE, SEG_KR[0] // LANE),
                  _rowspec(t, LANE), _rowspec(t, LANE), _rowspec(t, LANE),
                  _vecspec(LANE), _vecspec(LANE), _vecspec(LANE), _vecspec(LANE)],
        out_specs=[hspec(HEAD_PAD), hspec(HEAD_PAD), hspec(2 * V_DIM)],
        out_shape=[_sds((N_HEADS, s, HEAD_PAD), MXU_DTYPE), _sds((N_HEADS, s, HEAD_PAD), MXU_DTYPE),
                   _sds((N_HEADS, s, 2 * V_DIM), MXU_DTYPE)],
        compiler_params=_cp(("parallel",)),
    )(q_raw, kv, z, c_t, s1_t, s2_t, gqn, gqr, gkn, gkr)


def _causal_mask(t):
    row = lax.broadcasted_iota(jnp.int32, (t, t), 0)
    col = lax.broadcasted_iota(jnp.int32, (t, t), 1)
    return col <= row


NEG = -1e30


def _flash_fwd(qf, kf, va, *, name):
    nh, s, dk = qf.shape
    dv = va.shape[-1] // 2
    t = min(ATT_T, s)
    n = s // t
    assert dv == LANE and t % LANE == 0

    def body(q_ref, k_ref, v_ref, o_ref, lse_ref, m_s, acc_s, s_buf):
        i = pl.program_id(1)
        m_s[...] = jnp.full(m_s.shape, NEG, F32)
        acc_s[...] = jnp.zeros(acc_s.shape, F32)
        q = q_ref[0]

        def rows_of(j):
            return pl.ds(pl.multiple_of(j * t, t), t)

        def scores(j):
            return lax.dot_general(q, k_ref[0, rows_of(j), :], (((1,), (1,)), ((), ())), preferred_element_type=F32)

        def consume(j, slot, masked):
            sc = s_buf[slot]
            if masked:
                sc = jnp.where(_causal_mask(t), sc, NEG)
            m_prev = m_s[...]
            m_new = jnp.maximum(m_prev, jnp.max(sc, axis=-1, keepdims=True))
            alpha = jnp.exp(m_prev - m_new)
            p = jnp.exp(sc - jnp.tile(m_new, (1, t // LANE)))
            acc_s[...] = jnp.tile(alpha, (1, 2)) * acc_s[...] + jnp.dot(
                p.astype(MXU_DTYPE), v_ref[0, rows_of(j), :], preferred_element_type=F32)
            m_s[...] = m_new

        s_buf[0] = scores(0)

        def pair(a, carry):
            s_buf[1] = scores(2 * a + 1)
            consume(2 * a, 0, False)
            s_buf[0] = scores(2 * a + 2)
            consume(2 * a + 1, 1, False)
            return carry

        lax.fori_loop(0, i // 2, pair, 0)

        @pl.when(i % 2 == 1)
        def _():
            s_buf[1] = scores(i)
            consume(i - 1, 0, False)
            consume(i, 1, True)

        @pl.when(i % 2 == 0)
        def _():
            consume(i, 0, True)

        den = acc_s[:, dv:]
        o_ref[...] = acc_s[:, :dv] / den
        lse_ref[0] = m_s[...] + jnp.log(den)

    return pl.pallas_call(
        body, name=name, grid=(nh, n),
        in_specs=[pl.BlockSpec((1, t, dk), lambda h, i: (h, i, 0)),
                  pl.BlockSpec((1, s, dk), lambda h, i: (h, 0, 0)),
                  pl.BlockSpec((1, s, 2 * dv), lambda h, i: (h, 0, 0))],
        out_specs=[pl.BlockSpec((t, dv), lambda h, i: (i, h)),
                   pl.BlockSpec((1, t, LANE), lambda h, i: (h, i, 0))],
        out_shape=[_sds((s, nh * dv), F32), _sds((nh, s, LANE), F32)],
        scratch_shapes=[pltpu.VMEM((t, LANE), F32), pltpu.VMEM((t, 2 * dv), F32), pltpu.VMEM((2, t, t), F32)],
        compiler_params=_cp(("parallel", "arbitrary")),
    )(qf, kf, va)


def _dw_taps(ext_ref, w_ref, row0, t_rows, lane0, lanes, first_off):
    acc = None
    for k in range(CONV_K):
        term = w_ref[k:k + 1, lane0:lane0 + lanes] * ext_ref[pl.ds(row0 + first_off + k, t_rows), lane0:lane0 + lanes]
        acc = term if acc is None else acc + term
    return acc


CONV_RC = 32
CONV_LC = 256


def _conv_fwd(z, glu_b, dw_w, dw_b, ln_g, ln_b, *, name):
    s = z.shape[0]
    t = min(CONV_T, s)
    c2 = 2 * D_CONV
    hb = t // HALO

    def body(zm_ref, zh_ref, gb_ref, w_ref, wb_ref, g_ref, b_ref, u1_ref, u3_ref, ext):
        i = pl.program_id(0)

        def glu(zv):
            ci = zv + gb_ref[...]
            return ci[:, :D_CONV] * jax.nn.sigmoid(ci[:, D_CONV:])

        ext[HALO:, :] = glu(zm_ref[...])
        ext[0:HALO, :] = jnp.where(i > 0, glu(zh_ref[...]), 0.0)
        for rc in range(0, t, CONV_RC):
            for lc in range(0, D_CONV, CONV_LC):
                acc = _dw_taps(ext, w_ref, rc, CONV_RC, lc, CONV_LC, HALO - (CONV_K - 1))
                u1_ref[rc:rc + CONV_RC, lc:lc + CONV_LC] = acc + wb_ref[:, lc:lc + CONV_LC]
        u1 = u1_ref[...]
        mu = jnp.mean(u1, axis=-1, keepdims=True)
        cen = u1 - mu
        var = jnp.mean(cen * cen, axis=-1, keepdims=True)
        u2 = (cen * lax.rsqrt(var + EPS)) * g_ref[...] + b_ref[...]
        u3_ref[...] = _silu(u2).astype(u3_ref.dtype)

    return pl.pallas_call(
        body, name=name, grid=(s // t,),
        in_specs=[_rowspec(t, c2), pl.BlockSpec((HALO, c2), lambda i: (jnp.maximum(i * hb - 1, 0), 0)),
                  _vecspec(c2), pl.BlockSpec((HALO, D_CONV), lambda i: (0, 0)), _vecspec(D_CONV),
                  _vecspec(D_CONV), _vecspec(D_CONV)],
        out_specs=[_rowspec(t, D_CONV), _rowspec(t, D_CONV)],
        out_shape=[_sds((s, D_CONV), F32), _sds((s, D_CONV), MXU_DTYPE)],
        scratch_shapes=[pltpu.VMEM((t + HALO, D_CONV), F32)],
        compiler_params=_cp(("parallel",)),
    )(z, z, glu_b, dw_w, dw_b, ln_g, ln_b)


def _gate_cat(o, z, u4m, b_pw, *, name):
    s = o.shape[0]
    t = min(ROW_T, s)

    def body(o_ref, mg_ref, u4_ref, cg_ref, b_ref, cat_ref):
        cat_ref[:, :D_MLA] = (o_ref[...] * _silu(mg_ref[...])).astype(cat_ref.dtype)
        cat_ref[:, D_MLA:] = ((u4_ref[...] + b_ref[...]) * _silu(cg_ref[...])).astype(cat_ref.dtype)

    return pl.pallas_call(
        body, name=name, grid=(s // t,),
        in_specs=[_rowspec(t, D_MLA), _rowspec(t, D_MLA, SEG_MG[0] // D_MLA), _rowspec(t, D_CONV),
                  _rowspec(t, D_CONV, SEG_CG[0] // D_CONV), _vecspec(D_CONV)],
        out_specs=_rowspec(t, D_MLA + D_CONV), out_shape=_sds((s, D_MLA + D_CONV), MXU_DTYPE),
        compiler_params=_cp(("parallel",)),
    )(o, z, u4m, z, b_pw)


def _residual(x, y, gate, *, name):
    s, d = x.shape
    t = min(ROW_T, s)

    def body(x_ref, y_ref, g_ref, o_ref):
        o_ref[...] = x_ref[...] + g_ref[...] * y_ref[...]

    return pl.pallas_call(
        body, name=name, grid=(s // t,),
        in_specs=[_rowspec(t, d), _rowspec(t, d), _vecspec(d)],
        out_specs=_rowspec(t, d), out_shape=_sds((s, d), F32),
        compiler_params=_cp(("parallel",)),
    )(x, y, gate)


def _loss_head(xf, target, *, name):
    s, d = xf.shape
    t = min(ROW_T, s)

    def body(x_ref, t_ref, gx_ref, loss_ref):
        @pl.when(pl.program_id(0) == 0)
        def _():
            loss_ref[...] = jnp.zeros(loss_ref.shape, F32)

        err = x_ref[...] - t_ref[...]
        gx_ref[...] = err * (1.0 / d)
        loss_ref[...] += 0.5 * jnp.sum(_lanesum(err * err) * (1.0 / d), axis=0, keepdims=True)

    return pl.pallas_call(
        body, name=name, grid=(s // t,),
        in_specs=[_rowspec(t, d), _rowspec(t, d)],
        out_specs=[_rowspec(t, d), pl.BlockSpec((1, 1), lambda i: (0, 0))],
        out_shape=[_sds((s, d), F32), _sds((1, 1), F32)],
        compiler_params=_cp(("arbitrary",)),
    )(xf, target)


def _acc_init(refs):
    @pl.when(pl.program_id(0) == 0)
    def _():
        for r in refs:
            r[...] = jnp.zeros(r.shape, r.dtype)


def _out_bwd(gxo, y, gate, *, name):
    s, d = gxo.shape
    t = min(ROW_T, s)

    def body(g_ref, y_ref, gate_ref, dy_ref, dgate_ref):
        _acc_init([dgate_ref])
        gv = g_ref[...]
        dy_ref[...] = (gv * gate_ref[...]).astype(dy_ref.dtype)
        dgate_ref[...] += _colsum(gv * y_ref[...])

    return pl.pallas_call(
        body, name=name, grid=(s // t,),
        in_specs=[_rowspec(t, d), _rowspec(t, d), _vecspec(d)],
        out_specs=[_rowspec(t, d), _vecspec(d)],
        out_shape=[_sds((s, d), MXU_DTYPE), _sds((1, d), F32)],
        compiler_params=_cp(("arbitrary",)),
    )(gxo, y, gate)


def _gate_bwd(dcat, o, z, u4m, b_pw, *, name):
    s = o.shape[0]
    t = min(ROW_T, s)

    def body(dm_ref, dc_ref, o_ref, mg_ref, u4_ref, cg_ref, b_ref,
             do_ref, delta_ref, dmg_ref, du4_ref, dcg_ref, gb_ref):
        _acc_init([gb_ref])
        dm, ov, mg = dm_ref[...], o_ref[...], mg_ref[...]
        do = dm * _silu(mg)
        do_ref[...] = do.astype(do_ref.dtype)
        dmg_ref[...] = (dm * ov * _dsilu(mg)).astype(dmg_ref.dtype)
        prod = do * ov
        for h in range(N_HEADS):
            delta_ref[h] = _lanesum(prod[:, h * V_DIM:(h + 1) * V_DIM])
        dc, cg = dc_ref[...], cg_ref[...]
        du4 = dc * _silu(cg)
        du4_ref[...] = du4.astype(du4_ref.dtype)
        dcg_ref[...] = (dc * (u4_ref[...] + b_ref[...]) * _dsilu(cg)).astype(dcg_ref.dtype)
        gb_ref[...] += _colsum(du4)

    return pl.pallas_call(
        body, name=name, grid=(s // t,),
        in_specs=[_rowspec(t, D_MLA, 0), _rowspec(t, D_CONV, 1), _rowspec(t, D_MLA),
                  _rowspec(t, D_MLA, SEG_MG[0] // D_MLA), _rowspec(t, D_CONV),
                  _rowspec(t, D_CONV, SEG_CG[0] // D_CONV), _vecspec(D_CONV)],
        out_specs=[_rowspec(t, D_MLA), pl.BlockSpec((N_HEADS, t, 1), lambda i: (0, i, 0)),
                   _rowspec(t, D_MLA), _rowspec(t, D_CONV), _rowspec(t, D_CONV), _vecspec(D_CONV)],
        out_shape=[_sds((s, D_MLA), MXU_DTYPE), _sds((N_HEADS, s, 1), F32), _sds((s, D_MLA), MXU_DTYPE),
                   _sds((s, D_CONV), MXU_DTYPE), _sds((s, D_CONV), MXU_DTYPE), _sds((1, D_CONV), F32)],
        compiler_params=_cp(("arbitrary",)),
    )(dcat, dcat, o, z, u4m, z, b_pw)


def _conv_bwd(du3, u1, z, glu_b, dw_w, ln_g, ln_b, *, name):
    s = z.shape[0]
    t = min(CONV_T, s)
    c2 = 2 * D_CONV
    hb = t // HALO
    n_blk = s // t
    last_halo = s // HALO - 1

    def body(d3m_ref, d3h_ref, u1m_ref, u1h_ref, zm_ref, zh_ref, gb_ref, w_ref, g_ref, b_ref,
             dci_ref, gg_ref, gbn_ref, gwb_ref, ggb_ref, gw_ref, dext, uext, du0_s, gw_acc):
        i = pl.program_id(0)
        _acc_init([gg_ref, gbn_ref, gwb_ref, ggb_ref, gw_acc])

        def ln_bwd(d3, u1v):
            mu = jnp.mean(u1v, axis=-1, keepdims=True)
            cen = u1v - mu
            rstd = lax.rsqrt(jnp.mean(cen * cen, axis=-1, keepdims=True) + EPS)
            uh = cen * rstd
            d2 = d3 * _dsilu(uh * g_ref[...] + b_ref[...])
            dh = d2 * g_ref[...]
            d1 = rstd * (dh - jnp.mean(dh, axis=-1, keepdims=True) - uh * jnp.mean(dh * uh, axis=-1, keepdims=True))
            return d1, d2, uh

        d1, d2, uh = ln_bwd(d3m_ref[...], u1m_ref[...])
        gg_ref[...] += _colsum(d2 * uh)
        gbn_ref[...] += _colsum(d2)
        gwb_ref[...] += _colsum(d1)
        dext[0:t, :] = d1
        d1h, _, _ = ln_bwd(d3h_ref[...], u1h_ref[...])
        dext[t:, :] = jnp.where(i < n_blk - 1, d1h, 0.0)

        def glu_parts(zv):
            ci = zv + gb_ref[...]
            return ci[:, :D_CONV], jax.nn.sigmoid(ci[:, D_CONV:])

        val, sg = glu_parts(zm_ref[...])
        uext[HALO:, :] = val * sg
        valh, sgh = glu_parts(zh_ref[...])
        uext[0:HALO, :] = jnp.where(i > 0, valh * sgh, 0.0)

        for rc in range(0, t, CONV_RC):
            for lc in range(0, D_CONV, CONV_LC):
                acc = None
                dchunk = dext[rc:rc + CONV_RC, lc:lc + CONV_LC]
                for k in range(CONV_K):
                    term = w_ref[k:k + 1, lc:lc + CONV_LC] * dext[pl.ds(rc + (CONV_K - 1) - k, CONV_RC), lc:lc + CONV_LC]
                    acc = term if acc is None else acc + term
                    pr = dchunk * uext[pl.ds(rc + HALO - (CONV_K - 1) + k, CONV_RC), lc:lc + CONV_LC]
                    part = pr[0:8]
                    for r8 in range(8, CONV_RC, 8):
                        part = part + pr[r8:r8 + 8]
                    gw_acc[k, :, lc:lc + CONV_LC] += part
                du0_s[rc:rc + CONV_RC, lc:lc + CONV_LC] = acc

        du0 = du0_s[...]
        dval = du0 * sg
        dgt = du0 * val * sg * (1.0 - sg)
        dci_ref[:, :D_CONV] = dval.astype(dci_ref.dtype)
        dci_ref[:, D_CONV:] = dgt.astype(dci_ref.dtype)
        ggb_ref[:, :D_CONV] += _colsum(dval)
        ggb_ref[:, D_CONV:] += _colsum(dgt)

        @pl.when(i == n_blk - 1)
        def _():
            gw_ref[...] = jnp.sum(gw_acc[...], axis=1)

    halo_next = lambda w: pl.BlockSpec((HALO, w), lambda i: (jnp.minimum((i + 1) * hb, last_halo), 0))
    return pl.pallas_call(
        body, name=name, grid=(n_blk,),
        in_specs=[_rowspec(t, D_CONV), halo_next(D_CONV), _rowspec(t, D_CONV), halo_next(D_CONV),
                  _rowspec(t, c2), pl.BlockSpec((HALO, c2), lambda i: (jnp.maximum(i * hb - 1, 0), 0)),
                  _vecspec(c2), pl.BlockSpec((HALO, D_CONV), lambda i: (0, 0)), _vecspec(D_CONV), _vecspec(D_CONV)],
        out_specs=[_rowspec(t, c2), _vecspec(D_CONV), _vecspec(D_CONV), _vecspec(D_CONV), _vecspec(c2),
                   pl.BlockSpec((HALO, D_CONV), lambda i: (0, 0))],
        out_shape=[_sds((s, c2), MXU_DTYPE), _sds((1, D_CONV), F32), _sds((1, D_CONV), F32), _sds((1, D_CONV), F32),
                   _sds((1, c2), F32), _sds((HALO, D_CONV), F32)],
        scratch_shapes=[pltpu.VMEM((t + HALO, D_CONV), F32), pltpu.VMEM((t + HALO, D_CONV), F32),
                        pltpu.VMEM((t, D_CONV), F32), pltpu.VMEM((HALO, 8, D_CONV), F32)],
        compiler_params=_cp(("arbitrary",)),
    )(du3, du3, u1, u1, z, z, glu_b, dw_w, ln_g, ln_b)


def _flash_bwd(qf, kf, kft, va, do, lse_t, delta_t, *, name):
    nh, s, dk = qf.shape
    dv = va.shape[-1] // 2
    t = min(ATT_T, s)
    n = s // t
    nt = (((1,), (1,)), ((), ()))

    def body(q_ref, do_ref, lse_ref, dl_ref, k_ref, kt_ref, v_ref, dqt_ref, dk_ref, dv_ref,
             dk_s, dv_s, st_buf, dpt_buf):
        j = pl.program_id(1)

        @pl.when(j == 0)
        def _():
            dqt_ref[...] = jnp.zeros(dqt_ref.shape, F32)

        dk_s[...] = jnp.zeros(dk_s.shape, F32)
        dv_s[...] = jnp.zeros(dv_s.shape, F32)
        k, kt, v = k_ref[0], kt_ref[0], v_ref[0]
        n_un = n - 1 - j

        def rows_of(b):
            return pl.ds(pl.multiple_of((n - 1 - b) * t, t), t)

        def produce(b, slot):
            rows = rows_of(b)
            st_buf[slot] = lax.dot_general(k, q_ref[0, rows, :], nt, preferred_element_type=F32)
            dpt_buf[slot] = lax.dot_general(v, do_ref[rows, :], nt, preferred_element_type=F32)

        def consume(b, slot, masked):
            i = n - 1 - b
            rows = rows_of(b)
            q, dov = q_ref[0, rows, :], do_ref[rows, :]
            pt = jnp.exp(st_buf[slot] - lse_ref[0, i])
            if masked:
                key = lax.broadcasted_iota(jnp.int32, (t, t), 0)
                qry = lax.broadcasted_iota(jnp.int32, (t, t), 1)
                pt = jnp.where(key <= qry, pt, 0.0)
            dv_s[...] += jnp.dot(pt.astype(MXU_DTYPE), dov, preferred_element_type=F32)
            dst = (pt * (dpt_buf[slot] - dl_ref[0, i])).astype(MXU_DTYPE)
            dk_s[...] += jnp.dot(dst, q, preferred_element_type=F32)
            dqt_ref[0, i] += jnp.dot(kt, dst, preferred_element_type=F32)

        produce(0, 0)

        def pair(a, carry):
            produce(2 * a + 1, 1)
            consume(2 * a, 0, False)
            produce(2 * a + 2, 0)
            consume(2 * a + 1, 1, False)
            return carry

        lax.fori_loop(0, n_un // 2, pair, 0)

        @pl.when(n_un % 2 == 1)
        def _():
            produce(n_un, 1)
            consume(n_un - 1, 0, False)
            consume(n_un, 1, True)

        @pl.when(n_un % 2 == 0)
        def _():
            consume(n_un, 0, True)

        dk_ref[0] = dk_s[...]
        dv_ref[0] = dv_s[...]

    head = lambda h, j: (h, 0, 0)
    rowv = pl.BlockSpec((1, n, 1, t), lambda h, j: (h, 0, 0, 0))
    return pl.pallas_call(
        body, name=name, grid=(nh, n),
        in_specs=[pl.BlockSpec((1, s, dk), head),
                  pl.BlockSpec((s, dv), lambda h, j: (0, h)),
                  rowv, rowv,
                  pl.BlockSpec((1, t, dk), lambda h, j: (h, j, 0)),
                  pl.BlockSpec((1, dk, t), lambda h, j: (h, 0, j)),
                  pl.BlockSpec((1, t, dv), lambda h, j: (h, j, 0))],
        out_specs=[pl.BlockSpec((1, n, dk, t), lambda h, j: (h, 0, 0, 0)),
                   pl.BlockSpec((1, t, dk), lambda h, j: (h, j, 0)),
                   pl.BlockSpec((1, t, dv), lambda h, j: (h, j, 0))],
        out_shape=[_sds((nh, n, dk, t), F32), _sds((nh, s, dk), F32), _sds((nh, s, dv), F32)],
        scratch_shapes=[pltpu.VMEM((t, dk), F32), pltpu.VMEM((t, dv), F32),
                        pltpu.VMEM((2, t, t), F32), pltpu.VMEM((2, t, t), F32)],
        compiler_params=_cp(("parallel", "arbitrary")),
    )(qf, do, lse_t, delta_t, kf, kft, va)


def _qk_bwd(dqf, dkf, dvf, q_raw, kv, z, c_t, s1_t, s2_t, gqn, gqr, gkn, gkr, *, name):
    s = q_raw.shape[0]
    t = min(ROW_T, s)
    scale = 1.0 / math.sqrt(QK_DIM)

    def body(dq_ref, dk_ref, dv_ref, q_ref, kv_ref, kr_ref, c_ref, s1_ref, s2_ref,
             gqn_ref, gqr_ref, gkn_ref, gkr_ref, dqr_ref, dkv_ref, dkr_ref, ggq_ref, ggk_ref):
        _acc_init([ggq_ref, ggk_ref])
        c_v, s1_v, s2_v = c_ref[...], s1_ref[...], s2_ref[...]
        kr = kr_ref[...]
        kr_ss = _lanesum(kr * kr)
        dkr = jnp.zeros(kr.shape, F32)
        ggq_n = ggq_r = ggk_n = ggk_r = jnp.zeros((1, LANE), F32)

        def norm_bwd(n, r, rs, dyn, dyr, gn, gr):
            nh_, rh_ = n * rs, r * rs
            dnh, drh = dyn * gn, dyr * gr
            dot = (_lanesum(dnh * nh_) + _lanesum(drh * rh_)) * (1.0 / QK_DIM)
            return rs * (dnh - nh_ * dot), rs * (drh - rh_ * dot), _colsum(dyn * nh_), _colsum(dyr * rh_)

        for h in range(N_HEADS):
            n = q_ref[:, h * LANE:(h + 1) * LANE]
            r = q_ref[:, N_HEADS * LANE + h * LANE:N_HEADS * LANE + (h + 1) * LANE]
            rs = lax.rsqrt((_lanesum(n * n) + _lanesum(r * r)) * (1.0 / QK_DIM) + EPS)
            dyn = dq_ref[h, :, 0:LANE] * scale
            dyr = _rope_bwd(dq_ref[h, :, LANE:HEAD_PAD] * scale, c_v, s1_v, s2_v)
            dn, dr, g_n, g_r = norm_bwd(n, r, rs, dyn, dyr, gqn_ref[...], gqr_ref[...])
            dqr_ref[:, h * LANE:(h + 1) * LANE] = dn.astype(dqr_ref.dtype)
            dqr_ref[:, N_HEADS * LANE + h * LANE:N_HEADS * LANE + (h + 1) * LANE] = dr.astype(dqr_ref.dtype)
            ggq_n, ggq_r = ggq_n + g_n, ggq_r + g_r

            n = kv_ref[:, h * 2 * LANE:h * 2 * LANE + LANE]
            rs = lax.rsqrt((_lanesum(n * n) + kr_ss) * (1.0 / QK_DIM) + EPS)
            dyn = dk_ref[h, :, 0:LANE]
            dyr = _rope_bwd(dk_ref[h, :, LANE:HEAD_PAD], c_v, s1_v, s2_v)
            dn, dr, g_n, g_r = norm_bwd(n, kr, rs, dyn, dyr, gkn_ref[...], gkr_ref[...])
            dkv_ref[:, h * 2 * LANE:h * 2 * LANE + LANE] = dn.astype(dkv_ref.dtype)
            dkv_ref[:, h * 2 * LANE + LANE:(h + 1) * 2 * LANE] = dv_ref[h].astype(dkv_ref.dtype)
            dkr = dkr + dr
            ggk_n, ggk_r = ggk_n + g_n, ggk_r + g_r

        dkr_ref[...] = dkr.astype(dkr_ref.dtype)
        ggq_ref[:, 0:LANE] += ggq_n
        ggq_ref[:, LANE:] += ggq_r
        ggk_ref[:, 0:LANE] += ggk_n
        ggk_ref[:, LANE:] += ggk_r

    hspec = lambda w: pl.BlockSpec((N_HEADS, t, w), lambda i: (0, i, 0))
    wide = 2 * N_HEADS * LANE
    return pl.pallas_call(
        body, name=name, grid=(s // t,),
        in_specs=[hspec(HEAD_PAD), hspec(HEAD_PAD), hspec(V_DIM), _rowspec(t, wide), _rowspec(t, wide),
                  _rowspec(t, LANE, SEG_KR[0] // LANE), _rowspec(t, LANE), _rowspec(t, LANE), _rowspec(t, LANE),
                  _vecspec(LANE), _vecspec(LANE), _vecspec(LANE), _vecspec(LANE)],
        out_specs=[_rowspec(t, wide), _rowspec(t, wide), _rowspec(t, LANE), _vecspec(2 * LANE), _vecspec(2 * LANE)],
        out_shape=[_sds((s, wide), MXU_DTYPE), _sds((s, wide), MXU_DTYPE), _sds((s, LANE), MXU_DTYPE),
                   _sds((1, 2 * LANE), F32), _sds((1, 2 * LANE), F32)],
        compiler_params=_cp(("arbitrary",)),
    )(dqf, dkf, dvf, q_raw, kv, z, c_t, s1_t, s2_t, gqn, gqr, gkn, gkr)


def _lat_bwd(dqn, dkn, z, g_ql, g_kvl, *, name):
    s = z.shape[0]
    t = min(ROW_T, s)

    def body(dq_ref, dk_ref, ql_ref, kvl_ref, gq_ref, gk_ref, dql_ref, dkvl_ref, ggq_ref, ggk_ref):
        _acc_init([ggq_ref, ggk_ref])
        for d_ref, src, g_ref, dst, gg_ref in ((dq_ref, ql_ref, gq_ref, dql_ref, ggq_ref),
                                               (dk_ref, kvl_ref, gk_ref, dkvl_ref, ggk_ref)):
            v, dy = src[...], d_ref[...]
            r = lax.rsqrt(jnp.mean(v * v, axis=-1, keepdims=True) + EPS)
            vh = v * r
            dvh = dy * g_ref[...]
            dst[...] = (r * (dvh - vh * jnp.mean(dvh * vh, axis=-1, keepdims=True))).astype(dst.dtype)
            gg_ref[...] += _colsum(dy * vh)

    return pl.pallas_call(
        body, name=name, grid=(s // t,),
        in_specs=[_rowspec(t, Q_LORA), _rowspec(t, KV_LORA),
                  _rowspec(t, Q_LORA, SEG_QL[0] // Q_LORA), _rowspec(t, KV_LORA, SEG_KVL[0] // KV_LORA),
                  _vecspec(Q_LORA), _vecspec(KV_LORA)],
        out_specs=[_rowspec(t, Q_LORA), _rowspec(t, KV_LORA), _vecspec(Q_LORA), _vecspec(KV_LORA)],
        out_shape=[_sds((s, Q_LORA), MXU_DTYPE), _sds((s, KV_LORA), MXU_DTYPE),
                   _sds((1, Q_LORA), F32), _sds((1, KV_LORA), F32)],
        compiler_params=_cp(("arbitrary",)),
    )(dqn, dkn, z, z, g_ql, g_kvl)


def _prenorm_bwd(dh, x, gxo, g, sc1p, *, name):
    s, d = x.shape
    t = min(ROW_T, s)

    def body(dh_ref, x_ref, gx_ref, g_ref, sc_ref, dx_ref, dsh_ref, dsc_ref, gg_ref):
        _acc_init([dsh_ref, dsc_ref, gg_ref])
        xv, dhv = x_ref[...], dh_ref[...]
        r = lax.rsqrt(jnp.mean(xv * xv, axis=-1, keepdims=True) + EPS)
        xn = xv * r
        dsh_ref[...] += _colsum(dhv)
        dsc_ref[...] += _colsum(dhv * (xn * g_ref[...]))
        dm = dhv * sc_ref[...]
        gg_ref[...] += _colsum(dm * xn)
        dxn = dm * g_ref[...]
        dx_ref[...] = gx_ref[...] + r * (dxn - xn * jnp.mean(dxn * xn, axis=-1, keepdims=True))

    return pl.pallas_call(
        body, name=name, grid=(s // t,),
        in_specs=[_rowspec(t, d), _rowspec(t, d), _rowspec(t, d), _vecspec(d), _vecspec(d)],
        out_specs=[_rowspec(t, d), _vecspec(d), _vecspec(d), _vecspec(d)],
        out_shape=[_sds((s, d), F32), _sds((1, d), F32), _sds((1, d), F32), _sds((1, d), F32)],
        compiler_params=_cp(("arbitrary",)),
    )(dh, x, gxo, g, sc1p)


def _ada_fwd(c_all, ada_w, ada_b_cols, *, name):
    nl, d, cols = ada_w.shape

    def body(c_ref, w_ref, b_ref, o_ref):
        ca = _silu(c_ref[...]).astype(MXU_DTYPE)
        o_ref[0] = jnp.dot(ca, w_ref[0].astype(MXU_DTYPE), preferred_element_type=F32) + b_ref[0]

    return pl.pallas_call(
        body, name=name, grid=(nl,),
        in_specs=[pl.BlockSpec((N_DEV, d), lambda l: (0, 0)), pl.BlockSpec((1, d, cols), lambda l: (l, 0, 0)),
                  pl.BlockSpec((1, 1, cols), lambda l: (l, 0, 0))],
        out_specs=pl.BlockSpec((1, N_DEV, cols), lambda l: (l, 0, 0)),
        out_shape=_sds((nl, N_DEV, cols), F32),
        compiler_params=_cp(("parallel",)),
    )(c_all, ada_w, ada_b_cols)


def _ada_bwd(c_all_t, dmod_cols, *, name):
    nl, _, cols = dmod_cols.shape
    d = c_all_t.shape[0]

    def body(c_ref, dm_ref, o_ref):
        ca = _silu(c_ref[...]).astype(MXU_DTYPE)
        o_ref[0] = jnp.dot(ca, dm_ref[0].astype(MXU_DTYPE), preferred_element_type=F32)

    return pl.pallas_call(
        body, name=name, grid=(nl,),
        in_specs=[pl.BlockSpec((d, N_DEV), lambda l: (0, 0)), pl.BlockSpec((1, N_DEV, cols), lambda l: (l, 0, 0))],
        out_specs=pl.BlockSpec((1, d, cols), lambda l: (l, 0, 0)),
        out_shape=_sds((nl, d, cols), F32),
        compiler_params=_cp(("parallel",)),
    )(c_all_t, dmod_cols)


def _adamw(gparts, w, m, v, *, name):
    shape = w.shape
    cols = shape[-1]
    per_layer = isinstance(gparts, (list, tuple))
    nl = shape[0] if per_layer else 1
    rows = w.size // cols // nl
    glist = list(gparts) if per_layer else [gparts]
    npart = glist[0].shape[0]
    glist = [g.reshape(npart, rows, cols) for g in glist]
    w3, m3, v3 = (a.reshape(nl, rows, cols) for a in (w, m, v))
    budget = 2 * 1024 * 1024
    fits = [t for t in (256, 128, 64, 32, 16, 8)
            if rows % t == 0 and npart * t * cols * glist[0].dtype.itemsize <= budget]
    t = fits[0] if fits else rows
    nb = rows // t

    def body(*refs):
        g_refs = refs[:nl]
        w_ref, m_ref, v_ref, go_ref, d_ref, mo_ref, vo_ref, g_s = refs[nl:]
        layer = pl.program_id(0)
        for l in range(nl):
            @pl.when(layer == l)
            def _(l=l):
                g = g_refs[l][0].astype(F32)
                for p in range(1, npart):
                    g = g + g_refs[l][p].astype(F32)
                g_s[...] = g

        g = g_s[...]
        mn = ADAM_B1 * m_ref[0] + (1.0 - ADAM_B1) * g
        vn = ADAM_B2 * v_ref[0] + (1.0 - ADAM_B2) * (g * g)
        m_hat = mn / (1.0 - ADAM_B1 ** ADAM_STEP)
        v_hat = vn / (1.0 - ADAM_B2 ** ADAM_STEP)
        go_ref[0] = g
        d_ref[0] = -ADAM_LR * (m_hat / (jnp.sqrt(v_hat) + ADAM_EPS) + ADAM_WD * w_ref[0])
        mo_ref[0] = mn
        vo_ref[0] = vn

    def g_map(l):
        return lambda layer, i: (0, jnp.where(layer == l, i, jnp.where(layer < l, 0, nb - 1)), 0)

    spec = pl.BlockSpec((1, t, cols), lambda layer, i: (layer, i, 0))
    outs = pl.pallas_call(
        body, name=name, grid=(nl, nb),
        in_specs=[pl.BlockSpec((npart, t, cols), g_map(l)) for l in range(nl)] + [spec, spec, spec],
        out_specs=[spec] * 4, out_shape=[_sds((nl, rows, cols), F32)] * 4,
        scratch_shapes=[pltpu.VMEM((t, cols), F32)],
        compiler_params=_cp(("arbitrary", "arbitrary")),
    )(*glist, w3, m3, v3)
    return tuple(o.reshape(shape) for o in outs)


_ANY = pl.BlockSpec(memory_space=pl.ANY)


def _all_gather(blocks, *, name):
    na = len(blocks)

    def body(*refs):
        x_refs, out_refs = refs[:na], refs[na:2 * na]
        send_sems, recv_sems, local_sems = refs[2 * na:]
        x, y, c = lax.axis_index("x"), lax.axis_index("y"), lax.axis_index("c")
        me, sibling = (x, y, c), (x, y, 1 - c)
        chips = [(1 - x, y), (x, 1 - y), (1 - x, 1 - y)]

        def slot(a, px, py, pc):
            return out_refs[a].at[4 * px + 2 * py + pc]

        def copy(a, k, blk, to, src=None):
            return pltpu.make_async_remote_copy(
                src_ref=slot(a, *blk) if src is None else src, dst_ref=slot(a, *blk),
                send_sem=send_sems.at[7 * a + k], recv_sem=recv_sems.at[7 * a + k],
                device_id=to, device_id_type=MESH_ID)

        mine = [pltpu.make_async_copy(x_refs[a], slot(a, *me), local_sems.at[a]) for a in range(na)]
        for cp in mine:
            cp.start()
        first = []
        for a in range(na):
            first.append(copy(a, 0, me, sibling, src=x_refs[a]))
            first += [copy(a, 1 + j, me, (*chip, c), src=x_refs[a]) for j, chip in enumerate(chips)]
        for cp in first:
            cp.start()
        passed = []
        for a in range(na):
            for j, chip in enumerate(chips):
                copy(a, 1 + j, (*chip, c), me).wait_recv()
                fwd = copy(a, 4 + j, (*chip, c), sibling)
                fwd.start()
                passed.append(fwd)
        for a in range(na):
            copy(a, 0, sibling, me).wait_recv()
            for j, chip in enumerate(chips):
                copy(a, 4 + j, (*chip, 1 - c), me).wait_recv()
        for cp in first + passed:
            cp.wait_send()
        for cp in mine:
            cp.wait()

    outs = pl.pallas_call(
        body, name=name, in_specs=[_ANY] * na, out_specs=[_ANY] * na,
        out_shape=[_sds((N_DEV,) + b.shape, b.dtype) for b in blocks],
        scratch_shapes=[pltpu.SemaphoreType.DMA((7 * na,)), pltpu.SemaphoreType.DMA((7 * na,)),
                        pltpu.SemaphoreType.DMA((na,))],
    )(*blocks)
    return list(outs)


def _all_to_all(parts, *, name):
    na = len(parts)

    def body(*refs):
        in_refs, out_refs = refs[:na], refs[na:2 * na]
        send_sems, recv_sems, local_sems = refs[2 * na:]
        x, y, c = lax.axis_index("x"), lax.axis_index("y"), lax.axis_index("c")
        me = 4 * x + 2 * y + c
        mine = [pltpu.make_async_copy(in_refs[a].at[me], out_refs[a].at[me], local_sems.at[a]) for a in range(na)]
        for cp in mine:
            cp.start()
        copies = []
        for a in range(na):
            for k in range(1, N_DEV):
                px = 1 - x if k & 4 else x
                py = 1 - y if k & 2 else y
                pc = 1 - c if k & 1 else c
                cp = pltpu.make_async_remote_copy(
                    src_ref=in_refs[a].at[4 * px + 2 * py + pc], dst_ref=out_refs[a].at[me],
                    send_sem=send_sems.at[7 * a + k - 1], recv_sem=recv_sems.at[7 * a + k - 1],
                    device_id=(px, py, pc), device_id_type=MESH_ID)
                cp.start()
                copies.append(cp)
        for cp in copies:
            cp.wait()
        for cp in mine:
            cp.wait()

    outs = pl.pallas_call(
        body, name=name, in_specs=[_ANY] * na, out_specs=[_ANY] * na,
        out_shape=[_sds(p.shape, p.dtype) for p in parts],
        scratch_shapes=[pltpu.SemaphoreType.DMA((7 * na,)), pltpu.SemaphoreType.DMA((7 * na,)),
                        pltpu.SemaphoreType.DMA((na,))],
    )(*parts)
    return list(outs)


_HBM = pl.BlockSpec(memory_space=pltpu.HBM)
_SEM = pl.BlockSpec(memory_space=pltpu.SEMAPHORE)
_EFFECT = pltpu.SideEffectType.DATAFLOW_SIDE_EFFECTING


def _peers(x, y, c):
    out = []
    for k in range(1, N_DEV):
        out.append((1 - x if k & 4 else x, 1 - y if k & 2 else y, 1 - c if k & 1 else c))
    return out


def _own_slots(srcs, scatter, *, name, after=None):
    na = len(srcs)
    n_extra = 0 if after is None else 1
    me = (4 * lax.axis_index("x") + 2 * lax.axis_index("y") + lax.axis_index("c")).astype(jnp.int32).reshape(1)

    def body(me_ref, *refs):
        in_refs, out_refs = refs[:na], refs[na + n_extra:]
        for a in range(na):
            out_refs[a][0] = in_refs[a][0] if scatter else in_refs[a][...]

    def slot_spec(shard):
        zeros = (0,) * len(shard)
        return pl.BlockSpec((1,) + tuple(shard), lambda i, me_ref: (me_ref[0],) + zeros)

    def whole_spec(shape):
        zeros = (0,) * len(shape)
        return pl.BlockSpec(tuple(shape), lambda i, me_ref: zeros)

    shards = [s.shape[1:] if scatter else s.shape for s in srcs]
    in_specs = [slot_spec(sh) if scatter else whole_spec(sh) for sh in shards] + [_ANY] * n_extra
    outs = pl.pallas_call(
        body, name=name,
        grid_spec=pltpu.PrefetchScalarGridSpec(
            num_scalar_prefetch=1, grid=(1,), in_specs=in_specs, out_specs=[slot_spec(sh) for sh in shards]),
        out_shape=[_sds((N_DEV,) + tuple(sh), s.dtype) for sh, s in zip(shards, srcs)],
        compiler_params=_cp(("arbitrary",)),
    )(me, *srcs, *([] if after is None else [after]))
    return list(outs)


def _exchange_copies(src_refs, land_refs, send_sems, recv_sems, scatter):
    x, y, c = lax.axis_index("x"), lax.axis_index("y"), lax.axis_index("c")
    me = 4 * x + 2 * y + c
    cps = []
    for a in range(len(src_refs)):
        for k, (px, py, pc) in enumerate(_peers(x, y, c)):
            src = src_refs[a].at[4 * px + 2 * py + pc] if scatter else src_refs[a]
            cps.append(pltpu.make_async_remote_copy(
                src_ref=src, dst_ref=land_refs[a].at[me], send_sem=send_sems.at[7 * a + k],
                recv_sem=recv_sems.at[7 * a + k], device_id=(px, py, pc), device_id_type=MESH_ID))
    return cps


def _exchange_start(srcs, lands, scatter, *, name):
    na = len(srcs)

    def body(*refs):
        src_refs, land_refs = refs[:na], refs[na:2 * na]
        send_sems, recv_sems = refs[2 * na], refs[2 * na + 1]
        token = refs[-1]
        for cp in _exchange_copies(src_refs, land_refs, send_sems, recv_sems, scatter):
            cp.start()
        token[...] = jnp.zeros(token.shape, token.dtype)

    hbm = lambda a: pltpu.HBM(a.shape, a.dtype)
    outs = pl.pallas_call(
        body, name=name,
        out_shape=(pltpu.SemaphoreType.DMA((7 * na,)), pltpu.SemaphoreType.DMA((7 * na,)),
                   *[hbm(a) for a in srcs], *[hbm(a) for a in lands], _sds((8, LANE), F32)),
        in_specs=[_HBM] * (2 * na),
        out_specs=(_SEM, _SEM, *[_HBM] * (2 * na), pl.BlockSpec(memory_space=pltpu.VMEM)),
        input_output_aliases={i: 2 + i for i in range(2 * na)},
        compiler_params=pltpu.CompilerParams(has_side_effects=_EFFECT),
    )(*[pltpu.with_memory_space_constraint(a, pltpu.HBM) for a in list(srcs) + list(lands)])
    return outs[0], outs[1], list(outs[2:2 + na]), list(outs[2 + na:2 + 2 * na]), outs[-1]


def _exchange_wait(send_sems, recv_sems, srcs, lands, after, scatter, *, name):
    na = len(srcs)

    def body(*refs):
        src_refs, land_refs = refs[:na], refs[na:2 * na]
        s_sems, r_sems = refs[2 * na], refs[2 * na + 1]
        for cp in _exchange_copies(src_refs, land_refs, s_sems, r_sems, scatter):
            cp.wait_send()
            cp.wait_recv()

    hbm = lambda a: pltpu.HBM(a.shape, a.dtype)
    outs = pl.pallas_call(
        body, name=name,
        out_shape=(*[hbm(a) for a in srcs], *[hbm(a) for a in lands]),
        in_specs=[_HBM] * (2 * na) + [_SEM, _SEM, _ANY],
        out_specs=tuple([_HBM] * (2 * na)),
        input_output_aliases={i: i for i in range(2 * na)},
        compiler_params=pltpu.CompilerParams(has_side_effects=_EFFECT),
    )(*srcs, *lands, send_sems, recv_sems, after)
    return list(outs[na:])


_WIN_SEGS = (("ql", 0, Q_LORA, SEG_QL[0]), ("kvl", Q_LORA, KV_LORA, SEG_KVL[0]),
             ("kr", Q_LORA + KV_LORA, ROPE, SEG_KR[0]), ("mg", Q_LORA + KV_LORA + ROPE, D_MLA, SEG_MG[0]),
             ("ci", Q_LORA + KV_LORA + ROPE + D_MLA, 2 * D_CONV, SEG_CI[0]),
             ("cg", Q_LORA + KV_LORA + ROPE + D_MLA + 2 * D_CONV, D_CONV, SEG_CG[0]))
_WIN_SHARD = IN_COLS // N_DEV


def _win_pieces():
    out = []
    for _, o, n, new in _WIN_SEGS:
        for j in range(N_DEV):
            lo, hi = max(o, j * _WIN_SHARD), min(o + n, (j + 1) * _WIN_SHARD)
            if lo < hi:
                out.append((j, lo - j * _WIN_SHARD, new + lo - o, hi - lo))
    return out


def _win_assemble(w_all, *, name):
    d = w_all.shape[1]
    t = min(ROW_T, d)
    pieces = sorted(_win_pieces(), key=lambda p: p[2])

    def body(w_ref, o_ref):
        cols = [w_ref[j, :, lo:lo + n].astype(F32) for j, lo, _, n in pieces]
        cols.append(jnp.zeros((t, IN_PAD - (SEG_KR[0] + ROPE)), F32))
        o_ref[...] = jnp.concatenate(cols, axis=1).astype(o_ref.dtype)

    return pl.pallas_call(
        body, name=name, grid=(d // t,),
        in_specs=[pl.BlockSpec((N_DEV, t, _WIN_SHARD), lambda i: (0, i, 0))],
        out_specs=_rowspec(t, IN_PAD), out_shape=_sds((d, IN_PAD), w_all.dtype),
        compiler_params=_cp(("parallel",)),
    )(w_all)


def _win_split(grad, *, name):
    d = grad.shape[0]
    t = min(ROW_T, d)
    by_shard = [sorted([p for p in _win_pieces() if p[0] == j], key=lambda p: p[1]) for j in range(N_DEV)]

    def body(g_ref, o_ref):
        g = g_ref[...]
        for j in range(N_DEV):
            cols = [g[:, new:new + n] for _, _, new, n in by_shard[j]]
            o_ref[j] = jnp.concatenate(cols, axis=1).astype(o_ref.dtype)

    return pl.pallas_call(
        body, name=name, grid=(d // t,),
        in_specs=[_rowspec(t, IN_PAD)],
        out_specs=pl.BlockSpec((N_DEV, t, _WIN_SHARD), lambda i: (0, i, 0)),
        out_shape=_sds((N_DEV, d, _WIN_SHARD), WIRE_DTYPE),
        compiler_params=_cp(("parallel",)),
    )(grad)


def _cols_to_shards(a):
    r, n = a.shape
    return a.reshape(r, N_DEV, n // N_DEV).transpose(1, 0, 2)


def _shards_to_cols(a):
    nd, r, w = a.shape
    return a.transpose(1, 0, 2).reshape(r, nd * w)


def _win_permute(w_in):
    o_ql, o_kvl, o_kr, o_mg = 0, Q_LORA, Q_LORA + KV_LORA, Q_LORA + KV_LORA + ROPE
    o_ci = o_mg + D_MLA
    o_cg = o_ci + 2 * D_CONV
    seg = lambda o, n: w_in[:, o:o + n]
    pad = jnp.zeros((w_in.shape[0], LANE - ROPE), w_in.dtype)
    return jnp.concatenate([seg(o_ci, 2 * D_CONV), seg(o_mg, D_MLA), seg(o_cg, D_CONV), seg(o_ql, Q_LORA),
                            seg(o_kvl, KV_LORA), seg(o_kr, ROPE), pad], axis=1)


def _win_unpermute(g):
    seg = lambda s, n=None: g[:, s[0]:s[0] + (s[1] if n is None else n)]
    return jnp.concatenate([seg(SEG_QL), seg(SEG_KVL), seg(SEG_KR, ROPE), seg(SEG_MG), seg(SEG_CI), seg(SEG_CG)], axis=1)


def _qup_permute(w):
    w3 = w.reshape(w.shape[0], N_HEADS, QK_DIM)
    nope = w3[:, :, :NOPE].reshape(w.shape[0], N_HEADS * NOPE)
    rope = jnp.pad(w3[:, :, NOPE:], ((0, 0), (0, 0), (0, LANE - ROPE))).reshape(w.shape[0], N_HEADS * LANE)
    return jnp.concatenate([nope, rope], axis=1)


def _qup_unpermute(g):
    r = g.shape[0]
    nope = g[:, :N_HEADS * NOPE].reshape(r, N_HEADS, NOPE)
    rope = g[:, N_HEADS * NOPE:].reshape(r, N_HEADS, LANE)[:, :, :ROPE]
    return jnp.concatenate([nope, rope], axis=2).reshape(r, N_HEADS * QK_DIM)


def _norm_tiles(g):
    return g[:NOPE].reshape(1, LANE), jnp.pad(g[NOPE:], (0, LANE - ROPE)).reshape(1, LANE)


def _norm_untile(gt):
    return jnp.concatenate([gt[0, :NOPE], gt[0, LANE:LANE + ROPE]])


def _rope_tiles(positions):
    inv_freq = 1.0 / (ROPE_THETA ** (jnp.arange(0, ROPE, 2, dtype=F32) / ROPE))
    ang = positions.astype(F32)[:, None] * inv_freq
    cos, sin = jnp.cos(ang), jnp.sin(ang)
    zq = jnp.zeros_like(cos)
    c_t = jnp.concatenate([cos, cos, zq, zq], axis=1)
    s1_t = jnp.concatenate([-sin, zq, zq, zq], axis=1)
    s2_t = jnp.concatenate([zq, sin, zq, zq], axis=1)
    return c_t, s1_t, s2_t


_BIG = ("w_in", "w_q_up", "w_kv_up", "w_pw", "w_out")
_COL_SHARDED = ("w_in", "w_q_up", "w_kv_up")


def _pack_rows(arrs):
    return jnp.concatenate([a.reshape(-1, LANE) for a in arrs], axis=0)


def _unpack_rows(buf, shapes):
    out, r0 = [], 0
    lead = buf.shape[:-2]
    for shp in shapes:
        n = math.prod(shp) // LANE
        out.append(buf[..., r0:r0 + n, :].reshape(lead + tuple(shp)))
        r0 += n
    return out


_SMALL = (("dmod", 3 * D_MODEL), ("norm_g", D_MODEL), ("q_lat_g", Q_LORA), ("kv_lat_g", KV_LORA),
          ("q_norm_g", 2 * LANE), ("k_norm_g", 2 * LANE), ("glu_b", 2 * D_CONV), ("dw_w", HALO * D_CONV),
          ("dw_b", D_CONV), ("conv_ln_g", D_CONV), ("conv_ln_b", D_CONV), ("b_pw", D_CONV))


def _layer_fwd(x, p, rope, l):
    n = lambda s: f"{s}_l{l}"
    c_t, s1_t, s2_t = rope
    h = _prenorm(x, p["norm_g"], p["shift"], p["sc1p"], name=n("prenorm"))
    z = _mm(h, p["w_in"], name=n("in_proj"), tn=IN_PAD // 3, n_outer=True)
    qn, kn = _lat_norm(z, p["q_lat_g"], p["kv_lat_g"], name=n("lat_norm"))
    q_raw = _mm(qn, p["w_q_up"], name=n("q_up"), tn=1024)
    kv = _mm(kn, p["w_kv_up"], name=n("kv_up"), tn=1024)
    qf, kf, vf = _qk_prep(q_raw, kv, z, c_t, s1_t, s2_t, *p["qk_tiles"], name=n("qk_prep"))
    o, lse = _flash_fwd(qf, kf, vf, name=n("flash_fwd"))
    u1, u3 = _conv_fwd(z, p["glu_b"], p["dw_w"], p["dw_b"], p["conv_ln_g"], p["conv_ln_b"], name=n("conv_fwd"))
    u4m = _mm(u3, p["w_pw"], name=n("pw"), tn=1024)
    cat = _gate_cat(o, z, u4m, p["b_pw"], name=n("gate_cat"))
    y = _mm(cat, p["w_out"], name=n("out_proj"), tn=1024)
    x_next = _residual(x, y, p["gate"], name=n("residual"))
    saved = dict(x=x, h=h, z=z, qn=qn, kn=kn, q_raw=q_raw, kv=kv, qf=qf, kf=kf, vf=vf, o=o, lse=lse,
                 u1=u1, u3=u3, u4m=u4m, cat=cat, y=y)
    return x_next, saved


def _layer_bwd(gxo, p, sv, rope, l, hook_rest=None, hook_w_in=None):
    n = lambda s: f"{s}_l{l}"
    c_t, s1_t, s2_t = rope
    z = sv["z"]
    dy, dgate = _out_bwd(gxo, sv["y"], p["gate"], name=n("out_bwd"))
    g_w_out = _mm(sv["cat"], dy, ta=True, name=n("g_w_out"), tm=1024, tn=1024, tk=512)
    dcat = _mm(dy, p["w_out"], tb=True, name=n("d_cat"), tn=1024)
    do, delta, dmg, du4, dcg, g_b_pw = _gate_bwd(dcat, sv["o"], z, sv["u4m"], p["b_pw"], name=n("gate_bwd"))
    g_w_pw = _mm(sv["u3"], du4, ta=True, name=n("g_w_pw"), tm=1024, tn=1024, tk=512)
    du3 = _mm(du4, p["w_pw"], tb=True, name=n("d_u3"), tn=1024)
    dci, g_ln_g, g_ln_b, g_dw_b, g_glu_b, g_dw_w = _conv_bwd(
        du3, sv["u1"], z, p["glu_b"], p["dw_w"], p["conv_ln_g"], p["conv_ln_b"], name=n("conv_bwd"))
    t_att = min(ATT_T, z.shape[0])
    to_lanes = lambda a: a.reshape(N_HEADS, z.shape[0] // t_att, 1, t_att)
    dqt, dkf, dvf = _flash_bwd(sv["qf"], sv["kf"], jnp.swapaxes(sv["kf"], 1, 2), sv["vf"], do,
                               to_lanes(sv["lse"][:, :, 0]), to_lanes(delta), name=n("flash_bwd"))
    dqf = jnp.swapaxes(dqt, 2, 3).reshape(N_HEADS, z.shape[0], HEAD_PAD)
    dq_raw, dkv, dkr, g_qn, g_kn = _qk_bwd(dqf, dkf, dvf, sv["q_raw"], sv["kv"], z, c_t, s1_t, s2_t,
                                            *p["qk_tiles"], name=n("qk_bwd"))
    g_w_q_up = _mm(sv["qn"], dq_raw, ta=True, name=n("g_w_q_up"), tm=512, tn=1024, tk=512)
    dqn = _mm(dq_raw, p["w_q_up"], tb=True, name=n("d_qn"))
    g_w_kv_up = _mm(sv["kn"], dkv, ta=True, name=n("g_w_kv_up"), tm=256, tn=1024, tk=512)
    dkn = _mm(dkv, p["w_kv_up"], tb=True, name=n("d_kn"))
    dql, dkvl, g_ql, g_kvl = _lat_bwd(dqn, dkn, z, p["q_lat_g"], p["kv_lat_g"], name=n("lat_bwd"))
    dz = jnp.concatenate([dci, dmg, dcg, dql, dkvl, dkr], axis=1)
    big = dict(w_q_up=g_w_q_up, w_kv_up=g_w_kv_up, w_pw=g_w_pw, w_out=g_w_out)
    after = None if hook_rest is None else hook_rest(big)
    g_w_in = _mm(sv["h"], dz, ta=True, name=n("g_w_in"), tm=1024, tn=IN_PAD // 3, tk=512, after=after)
    big["w_in"] = g_w_in
    after = None if hook_w_in is None else hook_w_in(g_w_in)
    dh = _mm(dz, p["w_in"], tb=True, name=n("d_h"), tn=1024, tk=IN_PAD // 3, after=after)
    dx, dshift, dscale, g_norm = _prenorm_bwd(dh, sv["x"], gxo, p["norm_g"], p["sc1p"], name=n("prenorm_bwd"))
    small = dict(dmod=jnp.concatenate([dshift, dscale, dgate], axis=1), norm_g=g_norm, q_lat_g=g_ql, kv_lat_g=g_kvl,
                 q_norm_g=g_qn, k_norm_g=g_kn, glu_b=g_glu_b, dw_w=g_dw_w, dw_b=g_dw_b,
                 conv_ln_g=g_ln_g, conv_ln_b=g_ln_b, b_pw=g_b_pw)
    return dx, big, small


def _layer_params(l, full, mod_l, small):
    d = D_MODEL
    row = lambda a: a.reshape(1, -1)
    shift, scale, gate = mod_l[:, :d], mod_l[:, d:2 * d], mod_l[:, 2 * d:]
    dw_w = jnp.pad(full["dw_w"][l], ((0, HALO - CONV_K), (0, 0)))
    return dict(
        shift=shift, sc1p=1.0 + scale, gate=gate, norm_g=row(small["norm_g"][l]),
        w_in=full["w_in"][l], w_q_up=full["w_q_up"][l], w_kv_up=full["w_kv_up"][l],
        w_pw=full["w_pw"][l], w_out=full["w_out"][l], dw_w=dw_w,
        q_lat_g=row(small["q_lat_g"][l]), kv_lat_g=row(small["kv_lat_g"][l]),
        qk_tiles=_norm_tiles(small["q_norm_g"][l]) + _norm_tiles(small["k_norm_g"][l]),
        glu_b=row(small["glu_b"][l]), dw_b=row(small["dw_b"][l]), conv_ln_g=row(small["conv_ln_g"][l]),
        conv_ln_b=row(small["conv_ln_b"][l]), b_pw=row(small["b_pw"][l]))


def kernel(x, c, positions, ada_w, ada_b, norm_g, w_in, q_lat_g, w_q_up, kv_lat_g, w_kv_up, q_norm_g, k_norm_g, glu_b, dw_w, dw_b, conv_ln_g, conv_ln_b, w_pw, b_pw, w_out, loss_target, m_ada_w, m_ada_b, m_norm_g, m_w_in, m_q_lat_g, m_w_q_up, m_kv_lat_g, m_w_kv_up, m_q_norm_g, m_k_norm_g, m_glu_b, m_dw_w, m_dw_b, m_conv_ln_g, m_conv_ln_b, m_w_pw, m_b_pw, m_w_out, v_ada_w, v_ada_b, v_norm_g, v_w_in, v_q_lat_g, v_w_q_up, v_kv_lat_g, v_w_kv_up, v_q_norm_g, v_k_norm_g, v_glu_b, v_dw_w, v_dw_b, v_conv_ln_g, v_conv_ln_b, v_w_pw, v_b_pw, v_w_out):
    names = ("ada_w", "ada_b", "norm_g", "w_in", "q_lat_g", "w_q_up", "kv_lat_g", "w_kv_up", "q_norm_g",
             "k_norm_g", "glu_b", "dw_w", "dw_b", "conv_ln_g", "conv_ln_b", "w_pw", "b_pw", "w_out")
    w_loc = dict(zip(names, (ada_w, ada_b, norm_g, w_in, q_lat_g, w_q_up, kv_lat_g, w_kv_up, q_norm_g, k_norm_g,
                             glu_b, dw_w, dw_b, conv_ln_g, conv_ln_b, w_pw, b_pw, w_out)))
    m_loc = dict(zip(names, (m_ada_w, m_ada_b, m_norm_g, m_w_in, m_q_lat_g, m_w_q_up, m_kv_lat_g, m_w_kv_up,
                             m_q_norm_g, m_k_norm_g, m_glu_b, m_dw_w, m_dw_b, m_conv_ln_g, m_conv_ln_b, m_w_pw,
                             m_b_pw, m_w_out)))
    v_loc = dict(zip(names, (v_ada_w, v_ada_b, v_norm_g, v_w_in, v_q_lat_g, v_w_q_up, v_kv_lat_g, v_w_kv_up,
                             v_q_norm_g, v_k_norm_g, v_glu_b, v_dw_w, v_dw_b, v_conv_ln_g, v_conv_ln_b, v_w_pw,
                             v_b_pw, v_w_out)))
    nl, d = N_LAYERS, D_MODEL
    me = 4 * lax.axis_index("x") + 2 * lax.axis_index("y") + lax.axis_index("c")
    x2, tgt = x[0], loss_target[0]
    ada_cols = ada_w.shape[-1]

    c_all = _all_gather([c.reshape(d // LANE, LANE)], name="gather_c")[0].reshape(N_DEV, d)
    ada_b_cols = lax.dynamic_slice_in_dim(ada_b, me * ada_cols, ada_cols, axis=1).reshape(nl, 1, ada_cols)
    mod_cols = _ada_fwd(c_all, ada_w, ada_b_cols, name="ada_fwd")
    mod_all = _all_gather([mod_cols], name="gather_mod")[0]
    mod_me = lax.dynamic_index_in_dim(mod_all, me, axis=2, keepdims=False)
    mod = mod_me.transpose(1, 0, 2).reshape(nl, 1, N_DEV * ada_cols)

    dw_pad = jnp.pad(dw_w, ((0, 0), (0, HALO - CONV_K), (0, 0)))
    wire = {k: w_loc[k].astype(WIRE_DTYPE) for k in _BIG}
    gathered = _all_gather([wire[k][0] for k in _BIG] + [dw_pad], name="gather_weights_l0")
    dw_all = gathered[-1]
    src1 = [wire[k][1] for k in _BIG]
    lands1 = _own_slots(src1, False, name="own_weights_l1", after=mod)
    gsend, grecv, src1, lands1, tok_w1 = _exchange_start(src1, lands1, False, name="gather_start_l1")

    def kernel_layout(parts, l):
        return dict(w_in=_win_assemble(parts["w_in"], name=f"w_in_assemble_l{l}"),
                    w_q_up=_qup_permute(_shards_to_cols(parts["w_q_up"])),
                    w_kv_up=_shards_to_cols(parts["w_kv_up"]),
                    w_pw=parts["w_pw"].reshape(D_CONV, D_CONV),
                    w_out=parts["w_out"].reshape(D_MLA + D_CONV, d))

    small_in = dict(norm_g=norm_g, q_lat_g=q_lat_g, kv_lat_g=kv_lat_g, q_norm_g=q_norm_g, k_norm_g=k_norm_g,
                    glu_b=glu_b, dw_b=dw_b, conv_ln_g=conv_ln_g, conv_ln_b=conv_ln_b, b_pw=b_pw)
    dw_full = [_shards_to_cols(dw_all[:, l])[:CONV_K] for l in range(nl)]
    rope = _rope_tiles(positions[0])

    def layer_params(l, parts, mod_l):
        full = {k: {l: a} for k, a in kernel_layout(parts, l).items()}
        full["dw_w"] = dw_full
        return _layer_params(l, full, mod_l, small_in)

    params, saved = [None] * nl, [None] * nl
    params[0] = layer_params(0, dict(zip(_BIG, gathered[:-1])), mod[0] + tok_w1[0, 0])
    xs, saved[0] = _layer_fwd(x2, params[0], rope, 0)
    parts1 = _exchange_wait(gsend, grecv, src1, lands1, xs, False, name="gather_wait_l1")
    params[1] = layer_params(1, dict(zip(_BIG, parts1)), mod[1])
    xs, saved[1] = _layer_fwd(xs, params[1], rope, 1)
    gx, loss_part = _loss_head(xs, tgt, name="loss_head")
    loss = lax.psum(loss_part[0, 0], ("x", "y", "c"))

    def shard_major(k, g):
        if k == "w_q_up":
            g = _qup_unpermute(g)
        if k in _COL_SHARDED:
            return _cols_to_shards(g)
        return g.reshape((N_DEV, g.shape[0] // N_DEV, g.shape[1]))

    def scatter_start(send, tag):
        lands = _own_slots(send, True, name=f"own_grads_{tag}")
        return _exchange_start(send, lands, True, name=f"scatter_start_{tag}")

    def wire_rest(big):
        return [shard_major(k, big[k]).astype(WIRE_DTYPE) for k in _BIG[1:]]

    big_g, small_g, flying = [None] * nl, [None] * nl, {}
    gx, big_g[1], small_g[1] = _layer_bwd(gx, params[1], saved[1], rope, 1)
    flying["l1"] = scatter_start([_win_split(big_g[1]["w_in"], name="w_in_split_l1")] + wire_rest(big_g[1]), "l1")
    p0 = dict(params[0])
    p0["gate"] = p0["gate"] + flying["l1"][4][0, 0]

    def start_rest_l0(big):
        flying["l0_rest"] = scatter_start(wire_rest(big), "l0_rest")
        return flying["l0_rest"][4]

    def start_w_in_l0(g_w_in):
        flying["l0_w_in"] = scatter_start([_win_split(g_w_in, name="w_in_split_l0")], "l0_w_in")
        return flying["l0_w_in"][4]

    gx, big_g[0], small_g[0] = _layer_bwd(gx, p0, saved[0], rope, 0, hook_rest=start_rest_l0,
                                          hook_w_in=start_w_in_l0)

    tile = 8 * LANE
    padded = [(k, nn, -(-nn // tile) * tile) for k, nn in _SMALL]
    spk = jnp.concatenate([jnp.pad(small_g[l][k].reshape(-1), (0, np_ - nn)).reshape(-1, LANE)
                           for l in range(nl) for k, nn, np_ in padded], axis=0)
    s_all = _all_gather([spk], name="gather_small_grads")[0]
    s_rows = sum(np_ for _, _, np_ in padded) // LANE
    s_all = s_all.reshape(N_DEV, nl, s_rows, LANE)
    s_parts = {k: a[..., :nn] for (k, nn, _), a in
               zip(padded, _unpack_rows(s_all, [(np_,) for _, _, np_ in padded]))}

    dmod_all = s_parts["dmod"]
    dmod_cols = lax.dynamic_slice_in_dim(dmod_all, me * ada_cols, ada_cols, axis=2).transpose(1, 0, 2)
    g_ada_w = _ada_bwd(c_all.T, dmod_cols, name="ada_bwd")
    gp = {}
    gp["ada_w"] = g_ada_w[None]
    gp["ada_b"] = dmod_all
    for k in ("norm_g", "q_lat_g", "kv_lat_g", "glu_b", "dw_b", "conv_ln_g", "conv_ln_b", "b_pw"):
        gp[k] = s_parts[k]
    for k in ("q_norm_g", "k_norm_g"):
        t = s_parts[k]
        gp[k] = jnp.concatenate([t[..., :NOPE], t[..., LANE:LANE + ROPE]], axis=-1)
    dw_g = s_parts["dw_w"].reshape(N_DEV, nl, HALO, D_CONV)[:, :, :CONV_K]
    gp["dw_w"] = lax.dynamic_slice_in_dim(dw_g, me * LANE, LANE, axis=3)

    res = {k: _adamw(gp[k], w_loc[k], m_loc[k], v_loc[k], name=f"adamw_{k}") for k in names if k not in _BIG}
    arrived = [None] * nl
    arrived[1] = _exchange_wait(*flying["l1"][:4], gx, True, name="scatter_wait_l1")
    rest0 = _exchange_wait(*flying["l0_rest"][:4], gx, True, name="scatter_wait_l0_rest")
    arrived[0] = _exchange_wait(*flying["l0_w_in"][:4], res["ada_w"][1], True, name="scatter_wait_l0_w_in") + rest0
    for i, k in enumerate(_BIG):
        res[k] = _adamw([arrived[l][i] for l in range(nl)], w_loc[k], m_loc[k], v_loc[k], name=f"adamw_{k}")
    out = [loss, gx[None]]
    for idx in range(4):
        out += [res[k][idx] for k in names]
    return tuple(out)
```

```python
import functools
import math

import jax
import jax.numpy as jnp
from jax import lax
from jax.experimental import pallas as pl
from jax.experimental.pallas import tpu as pltpu

F32 = jnp.float32
MXU_DTYPE = jnp.bfloat16
WIRE_DTYPE = jnp.bfloat16

D_MODEL = 2048
N_LAYERS = 2
N_DEV = 8
N_HEADS = 8
NOPE = 128
ROPE = 64
V_DIM = 128
QK_DIM = NOPE + ROPE
Q_LORA = 512
KV_LORA = 256
D_MLA = N_HEADS * V_DIM
D_CONV = 1024
CONV_K = 31
ROPE_THETA = 10000.0
EPS = 1e-6
LANE = 128
HEAD_PAD = 2 * LANE
HALO = 32

SEG_CI = (0, 2 * D_CONV)
SEG_MG = (2 * D_CONV, D_MLA)
SEG_CG = (2 * D_CONV + D_MLA, D_CONV)
SEG_QL = (2 * D_CONV + D_MLA + D_CONV, Q_LORA)
SEG_KVL = (SEG_QL[0] + Q_LORA, KV_LORA)
SEG_KR = (SEG_KVL[0] + KV_LORA, LANE)
SEG_LAT = (SEG_QL[0], 1024)
IN_PAD = SEG_LAT[0] + SEG_LAT[1]
IN_TILE = IN_PAD // 4
assert SEG_KR[0] + LANE <= IN_PAD and SEG_LAT[0] % SEG_LAT[1] == 0
IN_COLS = Q_LORA + KV_LORA + ROPE + D_MLA + 2 * D_CONV + D_CONV

ADAM_LR = 0.001
ADAM_B1 = 0.9
ADAM_B2 = 0.999
ADAM_EPS = 1e-08
ADAM_WD = 0.01
ADAM_STEP = 10

VMEM_LIMIT = 56 * 1024 * 1024
ATT_T = 512
ROW_T = 256
CONV_T = 128
MESH_ID = pl.DeviceIdType.MESH


def _cp(sem=None):
    kw = dict(vmem_limit_bytes=VMEM_LIMIT)
    if sem is not None:
        kw["dimension_semantics"] = sem
    return pltpu.CompilerParams(**kw)


def _sds(shape, dtype):
    return jax.ShapeDtypeStruct(shape, dtype)


def _silu(x):
    return x * jax.nn.sigmoid(x)


def _dsilu(x):
    s = jax.nn.sigmoid(x)
    return s * (1.0 + x * (1.0 - s))


def _rowspec(t, width, col=0):
    return pl.BlockSpec((t, width), lambda i: (i, col))


def _vecspec(width):
    return pl.BlockSpec((1, width), lambda i: (0, 0))


def _colsum(v):
    return jnp.sum(v, axis=0, keepdims=True)


def _mm(a, b, *, name, ta=False, tb=False, out_dtype=F32, tm=512, tn=512, tk=None, n_outer=False, after=None):
    if ta:
        kdim, m = a.shape
    else:
        m, kdim = a.shape
    if tb:
        n, k2 = b.shape
    else:
        k2, n = b.shape
    assert kdim == k2, (a.shape, b.shape)
    tm, tn = min(tm, m), min(tn, n)
    tk = kdim if tk is None else min(tk, kdim)
    assert m % tm == 0 and n % tn == 0 and kdim % tk == 0, (m, n, kdim, tm, tn, tk)
    nk = kdim // tk
    dims = (((0 if ta else 1,), (1 if tb else 0,)), ((), ()))

    n_extra = 0 if after is None else 1

    def body(a_ref, b_ref, *rest):
        o_ref, scratch = rest[n_extra], rest[n_extra + 1:]
        prod = lax.dot_general(a_ref[...].astype(MXU_DTYPE), b_ref[...].astype(MXU_DTYPE), dims,
                               preferred_element_type=F32)
        if nk == 1:
            o_ref[...] = prod.astype(o_ref.dtype)
        else:
            acc = scratch[0]
            k = pl.program_id(2)

            @pl.when(k == 0)
            def _():
                acc[...] = prod

            @pl.when(k > 0)
            def _():
                acc[...] += prod

            @pl.when(k == nk - 1)
            def _():
                o_ref[...] = acc[...].astype(o_ref.dtype)

    if n_outer:
        ij = lambda g0, g1: (g1, g0)
        grid = (n // tn, m // tm, nk)
    else:
        ij = lambda g0, g1: (g0, g1)
        grid = (m // tm, n // tn, nk)

    def a_map(g0, g1, k):
        i, _ = ij(g0, g1)
        return (k, i) if ta else (i, k)

    def b_map(g0, g1, k):
        _, j = ij(g0, g1)
        return (j, k) if tb else (k, j)

    def o_map(g0, g1, k):
        return ij(g0, g1)

    return pl.pallas_call(
        body, name=name, grid=grid,
        in_specs=[pl.BlockSpec((tk, tm) if ta else (tm, tk), a_map),
                  pl.BlockSpec((tn, tk) if tb else (tk, tn), b_map)] + [_ANY] * n_extra,
        out_specs=pl.BlockSpec((tm, tn), o_map),
        out_shape=_sds((m, n), out_dtype),
        scratch_shapes=[pltpu.VMEM((tm, tn), F32)] if nk > 1 else [],
        compiler_params=_cp(("parallel", "parallel", "arbitrary")),
    )(a, b, *([] if after is None else [after]))


def _prenorm(x, g, shift, sc1p, *, name):
    s, d = x.shape
    t = min(ROW_T, s)

    def body(x_ref, g_ref, sh_ref, sc_ref, h_ref):
        xv = x_ref[...]
        r = lax.rsqrt(jnp.mean(xv * xv, axis=-1, keepdims=True) + EPS)
        h_ref[...] = ((xv * r) * g_ref[...] * sc_ref[...] + sh_ref[...]).astype(h_ref.dtype)

    return pl.pallas_call(
        body, name=name, grid=(s // t,),
        in_specs=[_rowspec(t, d), _vecspec(d), _vecspec(d), _vecspec(d)],
        out_specs=_rowspec(t, d), out_shape=_sds((s, d), MXU_DTYPE),
        compiler_params=_cp(("parallel",)),
    )(x, g, shift, sc1p)


def _lat_norm(z, g_ql, g_kvl, *, name):
    s = z.shape[0]
    t = min(ROW_T, s)

    def body(ql_ref, kvl_ref, gq_ref, gk_ref, qn_ref, kn_ref):
        for src, g_ref, dst in ((ql_ref, gq_ref, qn_ref), (kvl_ref, gk_ref, kn_ref)):
            v = src[...]
            r = lax.rsqrt(jnp.mean(v * v, axis=-1, keepdims=True) + EPS)
            dst[...] = ((v * r) * g_ref[...]).astype(dst.dtype)

    return pl.pallas_call(
        body, name=name, grid=(s // t,),
        in_specs=[_rowspec(t, Q_LORA, SEG_QL[0] // Q_LORA), _rowspec(t, KV_LORA, SEG_KVL[0] // KV_LORA),
                  _vecspec(Q_LORA), _vecspec(KV_LORA)],
        out_specs=[_rowspec(t, Q_LORA), _rowspec(t, KV_LORA)],
        out_shape=[_sds((s, Q_LORA), MXU_DTYPE), _sds((s, KV_LORA), MXU_DTYPE)],
        compiler_params=_cp(("parallel",)),
    )(z, z, g_ql, g_kvl)


def _rope_fwd(r, c_t, s1_t, s2_t):
    return r * c_t + pltpu.roll(r, LANE - ROPE // 2, 1) * s1_t + pltpu.roll(r, ROPE // 2, 1) * s2_t


def _rope_bwd(d, c_t, s1_t, s2_t):
    return d * c_t + pltpu.roll(d * s1_t, ROPE // 2, 1) + pltpu.roll(d * s2_t, LANE - ROPE // 2, 1)


def _lanesum(v):
    return jnp.sum(v, axis=-1, keepdims=True)


def _qk_prep(q_raw, kv, z, c_t, s1_t, s2_t, gqn, gqr, gkn, gkr, *, name):
    s = q_raw.shape[0]
    t = min(ROW_T, s)
    scale = 1.0 / math.sqrt(QK_DIM)

    def body(q_ref, kv_ref, kr_ref, c_ref, s1_ref, s2_ref, gqn_ref, gqr_ref, gkn_ref, gkr_ref,
             qf_ref, kf_ref, vf_ref):
        c_v, s1_v, s2_v = c_ref[...], s1_ref[...], s2_ref[...]
        kr = kr_ref[...]
        kr_ss = _lanesum(kr * kr)
        for h in range(N_HEADS):
            n = q_ref[:, h * LANE:(h + 1) * LANE]
            r = q_ref[:, N_HEADS * LANE + h * LANE:N_HEADS * LANE + (h + 1) * LANE]
            rs = lax.rsqrt((_lanesum(n * n) + _lanesum(r * r)) * (1.0 / QK_DIM) + EPS)
            qf_ref[h, :, 0:LANE] = (((n * rs) * gqn_ref[...]) * scale).astype(qf_ref.dtype)
            rr = _rope_fwd((r * rs) * gqr_ref[...], c_v, s1_v, s2_v)
            qf_ref[h, :, LANE:HEAD_PAD] = (rr * scale).astype(qf_ref.dtype)

            n = kv_ref[:, h * 2 * LANE:h * 2 * LANE + LANE]
            rs = lax.rsqrt((_lanesum(n * n) + kr_ss) * (1.0 / QK_DIM) + EPS)
            kf_ref[h, :, 0:LANE] = ((n * rs) * gkn_ref[...]).astype(kf_ref.dtype)
            kf_ref[h, :, LANE:HEAD_PAD] = _rope_fwd((kr * rs) * gkr_ref[...], c_v, s1_v, s2_v).astype(kf_ref.dtype)
            vf_ref[h, :, 0:V_DIM] = kv_ref[:, h * 2 * LANE + LANE:(h + 1) * 2 * LANE].astype(vf_ref.dtype)
            vf_ref[h, :, V_DIM:] = jnp.ones((t, V_DIM), vf_ref.dtype)

    hspec = lambda w: pl.BlockSpec((N_HEADS, t, w), lambda i: (0, i, 0))
    return pl.pallas_call(
        body, name=name, grid=(s // t,),
        in_specs=[_rowspec(t, 2 * N_HEADS * LANE), _rowspec(t, 2 * N_HEADS * LANE),
                  _rowspec(t, LANE, SEG_KR[0] // LANE),
                  _rowspec(t, LANE), _rowspec(t, LANE), _rowspec(t, LANE),
                  _vecspec(LANE), _vecspec(LANE), _vecspec(LANE), _vecspec(LANE)],
        out_specs=[hspec(HEAD_PAD), hspec(HEAD_PAD), hspec(2 * V_DIM)],
        out_shape=[_sds((N_HEADS, s, HEAD_PAD), MXU_DTYPE), _sds((N_HEADS, s, HEAD_PAD), MXU_DTYPE),
                   _sds((N_HEADS, s, 2 * V_DIM), MXU_DTYPE)],
        compiler_params=_cp(("parallel",)),
    )(q_raw, kv, z, c_t, s1_t, s2_t, gqn, gqr, gkn, gkr)


def _causal_mask(t):
    row = lax.broadcasted_iota(jnp.int32, (t, t), 0)
    col = lax.broadcasted_iota(jnp.int32, (t, t), 1)
    return col <= row


NEG = -1e30


def _flash_fwd(qf, kf, va, *, name):
    nh, s, dk = qf.shape
    dv = va.shape[-1] // 2
    t = min(ATT_T, s)
    n = s // t
    assert dv == LANE and t % LANE == 0

    def body(q_ref, k_ref, v_ref, o_ref, lse_ref, m_s, acc_s, s_buf):
        i = pl.program_id(1)
        m_s[...] = jnp.full(m_s.shape, NEG, F32)
        acc_s[...] = jnp.zeros(acc_s.shape, F32)
        q = q_ref[0]

        def rows_of(j):
            return pl.ds(pl.multiple_of(j * t, t), t)

        def scores(j):
            return lax.dot_general(q, k_ref[0, rows_of(j), :], (((1,), (1,)), ((), ())), preferred_element_type=F32)

        def consume(j, slot, masked):
            sc = s_buf[slot]
            if masked:
                sc = jnp.where(_causal_mask(t), sc, NEG)
            m_prev = m_s[...]
            m_new = jnp.maximum(m_prev, jnp.max(sc, axis=-1, keepdims=True))
            alpha = jnp.exp(m_prev - m_new)
            p = jnp.exp(sc - jnp.tile(m_new, (1, t // LANE)))
            acc_s[...] = jnp.tile(alpha, (1, 2)) * acc_s[...] + jnp.dot(
                p.astype(MXU_DTYPE), v_ref[0, rows_of(j), :], preferred_element_type=F32)
            m_s[...] = m_new

        s_buf[0] = scores(0)

        def pair(a, carry):
            s_buf[1] = scores(2 * a + 1)
            consume(2 * a, 0, False)
            s_buf[0] = scores(2 * a + 2)
            consume(2 * a + 1, 1, False)
            return carry

        lax.fori_loop(0, i // 2, pair, 0)

        @pl.when(i % 2 == 1)
        def _():
            s_buf[1] = scores(i)
            consume(i - 1, 0, False)
            consume(i, 1, True)

        @pl.when(i % 2 == 0)
        def _():
            consume(i, 0, True)

        den = acc_s[:, dv:]
        o_ref[...] = acc_s[:, :dv] / den
        lse_ref[0] = m_s[...] + jnp.log(den)

    return pl.pallas_call(
        body, name=name, grid=(nh, n),
        in_specs=[pl.BlockSpec((1, t, dk), lambda h, i: (h, i, 0)),
                  pl.BlockSpec((1, s, dk), lambda h, i: (h, 0, 0)),
                  pl.BlockSpec((1, s, 2 * dv), lambda h, i: (h, 0, 0))],
        out_specs=[pl.BlockSpec((t, dv), lambda h, i: (i, h)),
                   pl.BlockSpec((1, t, LANE), lambda h, i: (h, i, 0))],
        out_shape=[_sds((s, nh * dv), F32), _sds((nh, s, LANE), F32)],
        scratch_shapes=[pltpu.VMEM((t, LANE), F32), pltpu.VMEM((t, 2 * dv), F32), pltpu.VMEM((2, t, t), F32)],
        compiler_params=_cp(("parallel", "arbitrary")),
    )(qf, kf, va)


def _shifted_copies(ext_ref):
    rows = ext_ref.shape[1] - 8
    for s in range(1, 8):
        ext_ref[s, 0:rows, :] = ext_ref[0, s:s + rows, :]


def _window(ext_ref, off, t_rows, lane0, lanes):
    return ext_ref[off % 8, pl.ds(off - off % 8, t_rows), lane0:lane0 + lanes]


def _dw_taps(ext_ref, w_ref, row0, t_rows, lane0, lanes, first_off):
    acc = None
    for k in range(CONV_K):
        term = w_ref[k:k + 1, lane0:lane0 + lanes] * _window(ext_ref, row0 + first_off + k, t_rows, lane0, lanes)
        acc = term if acc is None else acc + term
    return acc


CONV_RC = 32
CONV_LC = 256


def _conv_fwd(z, glu_b, dw_w, dw_b, ln_g, ln_b, *, name):
    s = z.shape[0]
    t = min(CONV_T, s)
    c2 = 2 * D_CONV
    hb = t // HALO

    def body(zm_ref, zh_ref, gb_ref, w_ref, wb_ref, g_ref, b_ref, u1_ref, u3_ref, ext):
        i = pl.program_id(0)

        def glu(zv):
            ci = zv + gb_ref[...]
            return ci[:, :D_CONV] * jax.nn.sigmoid(ci[:, D_CONV:])

        ext[0, HALO:, :] = glu(zm_ref[...])
        ext[0, 0:HALO, :] = jnp.where(i > 0, glu(zh_ref[...]), 0.0)
        _shifted_copies(ext)
        for rc in range(0, t, CONV_RC):
            for lc in range(0, D_CONV, CONV_LC):
                acc = _dw_taps(ext, w_ref, rc, CONV_RC, lc, CONV_LC, HALO - (CONV_K - 1))
                u1_ref[rc:rc + CONV_RC, lc:lc + CONV_LC] = acc + wb_ref[:, lc:lc + CONV_LC]
        u1 = u1_ref[...]
        mu = jnp.mean(u1, axis=-1, keepdims=True)
        cen = u1 - mu
        var = jnp.mean(cen * cen, axis=-1, keepdims=True)
        u2 = (cen * lax.rsqrt(var + EPS)) * g_ref[...] + b_ref[...]
        u3_ref[...] = _silu(u2).astype(u3_ref.dtype)

    return pl.pallas_call(
        body, name=name, grid=(s // t,),
        in_specs=[_rowspec(t, c2), pl.BlockSpec((HALO, c2), lambda i: (jnp.maximum(i * hb - 1, 0), 0)),
                  _vecspec(c2), pl.BlockSpec((HALO, D_CONV), lambda i: (0, 0)), _vecspec(D_CONV),
                  _vecspec(D_CONV), _vecspec(D_CONV)],
        out_specs=[_rowspec(t, D_CONV), _rowspec(t, D_CONV)],
        out_shape=[_sds((s, D_CONV), F32), _sds((s, D_CONV), MXU_DTYPE)],
        scratch_shapes=[pltpu.VMEM((8, t + HALO, D_CONV), F32)],
        compiler_params=_cp(("parallel",)),
    )(z, z, glu_b, dw_w, dw_b, ln_g, ln_b)


def _gate_cat(o, z, u4m, b_pw, *, name):
    s = o.shape[0]
    t = min(ROW_T, s)

    def body(o_ref, mg_ref, u4_ref, cg_ref, b_ref, cat_ref):
        cat_ref[:, :D_MLA] = (o_ref[...] * _silu(mg_ref[...])).astype(cat_ref.dtype)
        cat_ref[:, D_MLA:] = ((u4_ref[...] + b_ref[...]) * _silu(cg_ref[...])).astype(cat_ref.dtype)

    return pl.pallas_call(
        body, name=name, grid=(s // t,),
        in_specs=[_rowspec(t, D_MLA), _rowspec(t, D_MLA, SEG_MG[0] // D_MLA), _rowspec(t, D_CONV),
                  _rowspec(t, D_CONV, SEG_CG[0] // D_CONV), _vecspec(D_CONV)],
        out_specs=_rowspec(t, D_MLA + D_CONV), out_shape=_sds((s, D_MLA + D_CONV), MXU_DTYPE),
        compiler_params=_cp(("parallel",)),
    )(o, z, u4m, z, b_pw)


def _residual(x, y, gate, *, name):
    s, d = x.shape
    t = min(ROW_T, s)

    def body(x_ref, y_ref, g_ref, o_ref):
        o_ref[...] = x_ref[...] + g_ref[...] * y_ref[...]

    return pl.pallas_call(
        body, name=name, grid=(s // t,),
        in_specs=[_rowspec(t, d), _rowspec(t, d), _vecspec(d)],
        out_specs=_rowspec(t, d), out_shape=_sds((s, d), F32),
        compiler_params=_cp(("parallel",)),
    )(x, y, gate)


def _loss_head(xf, target, *, name):
    s, d = xf.shape
    t = min(ROW_T, s)

    def body(x_ref, t_ref, gx_ref, loss_ref):
        @pl.when(pl.program_id(0) == 0)
        def _():
            loss_ref[...] = jnp.zeros(loss_ref.shape, F32)

        err = x_ref[...] - t_ref[...]
        gx_ref[...] = err * (1.0 / d)
        loss_ref[...] += 0.5 * jnp.sum(_lanesum(err * err) * (1.0 / d), axis=0, keepdims=True)

    return pl.pallas_call(
        body, name=name, grid=(s // t,),
        in_specs=[_rowspec(t, d), _rowspec(t, d)],
        out_specs=[_rowspec(t, d), pl.BlockSpec((1, 1), lambda i: (0, 0))],
        out_shape=[_sds((s, d), F32), _sds((1, 1), F32)],
        compiler_params=_cp(("arbitrary",)),
    )(xf, target)


def _acc_init(refs):
    @pl.when(pl.program_id(0) == 0)
    def _():
        for r in refs:
            r[...] = jnp.zeros(r.shape, r.dtype)


def _out_bwd(gxo, y, gate, *, name):
    s, d = gxo.shape
    t = min(ROW_T, s)

    def body(g_ref, y_ref, gate_ref, dy_ref, dgate_ref):
        _acc_init([dgate_ref])
        gv = g_ref[...]
        dy_ref[...] = (gv * gate_ref[...]).astype(dy_ref.dtype)
        dgate_ref[...] += _colsum(gv * y_ref[...])

    return pl.pallas_call(
        body, name=name, grid=(s // t,),
        in_specs=[_rowspec(t, d), _rowspec(t, d), _vecspec(d)],
        out_specs=[_rowspec(t, d), _vecspec(d)],
        out_shape=[_sds((s, d), MXU_DTYPE), _sds((1, d), F32)],
        compiler_params=_cp(("arbitrary",)),
    )(gxo, y, gate)


def _gate_bwd(dcat, o, z, u4m, b_pw, *, name):
    s = o.shape[0]
    t = min(ROW_T, s)
    gates = D_MLA + D_CONV
    assert SEG_CG[0] == SEG_MG[0] + D_MLA and SEG_MG[0] % gates == 0

    def body(dm_ref, dc_ref, o_ref, mg_ref, u4_ref, cg_ref, b_ref,
             do_ref, delta_ref, du4_ref, gb_ref, dz_ref):
        _acc_init([gb_ref])
        dm, ov, mg = dm_ref[...], o_ref[...], mg_ref[...]
        do = dm * _silu(mg)
        do_ref[...] = do.astype(do_ref.dtype)
        dz_ref[:, :D_MLA] = (dm * ov * _dsilu(mg)).astype(dz_ref.dtype)
        prod = do * ov
        for h in range(N_HEADS):
            delta_ref[h] = _lanesum(prod[:, h * V_DIM:(h + 1) * V_DIM])
        dc, cg = dc_ref[...], cg_ref[...]
        du4 = dc * _silu(cg)
        du4_ref[...] = du4.astype(du4_ref.dtype)
        dz_ref[:, D_MLA:] = (dc * (u4_ref[...] + b_ref[...]) * _dsilu(cg)).astype(dz_ref.dtype)
        gb_ref[...] += _colsum(du4)

    return pl.pallas_call(
        body, name=name, grid=(s // t,),
        in_specs=[_rowspec(t, D_MLA, 0), _rowspec(t, D_CONV, 1), _rowspec(t, D_MLA),
                  _rowspec(t, D_MLA, SEG_MG[0] // D_MLA), _rowspec(t, D_CONV),
                  _rowspec(t, D_CONV, SEG_CG[0] // D_CONV), _vecspec(D_CONV)],
        out_specs=[_rowspec(t, D_MLA), pl.BlockSpec((N_HEADS, t, 1), lambda i: (0, i, 0)),
                   _rowspec(t, D_CONV), _vecspec(D_CONV), _rowspec(t, gates, SEG_MG[0] // gates)],
        out_shape=[_sds((s, D_MLA), MXU_DTYPE), _sds((N_HEADS, s, 1), F32),
                   _sds((s, D_CONV), MXU_DTYPE), _sds((1, D_CONV), F32), _sds((s, IN_PAD), MXU_DTYPE)],
        compiler_params=_cp(("arbitrary",)),
    )(dcat, dcat, o, z, u4m, z, b_pw)


def _conv_bwd(du3, u1, z, dz, glu_b, dw_w, ln_g, ln_b, *, name):
    s = z.shape[0]
    t = min(CONV_T, s)
    c2 = 2 * D_CONV
    hb = t // HALO
    n_blk = s // t
    last_halo = s // HALO - 1

    def body(d3m_ref, d3h_ref, u1m_ref, u1h_ref, zm_ref, zh_ref, gb_ref, w_ref, g_ref, b_ref, dz_in_ref,
             dci_ref, gg_ref, gbn_ref, gwb_ref, ggb_ref, gw_ref, dext, uext, du0_s, gw_acc):
        i = pl.program_id(0)
        _acc_init([gg_ref, gbn_ref, gwb_ref, ggb_ref, gw_acc])

        def ln_bwd(d3, u1v):
            mu = jnp.mean(u1v, axis=-1, keepdims=True)
            cen = u1v - mu
            rstd = lax.rsqrt(jnp.mean(cen * cen, axis=-1, keepdims=True) + EPS)
            uh = cen * rstd
            d2 = d3 * _dsilu(uh * g_ref[...] + b_ref[...])
            dh = d2 * g_ref[...]
            d1 = rstd * (dh - jnp.mean(dh, axis=-1, keepdims=True) - uh * jnp.mean(dh * uh, axis=-1, keepdims=True))
            return d1, d2, uh

        d1, d2, uh = ln_bwd(d3m_ref[...], u1m_ref[...])
        gg_ref[...] += _colsum(d2 * uh)
        gbn_ref[...] += _colsum(d2)
        gwb_ref[...] += _colsum(d1)
        dext[0, 0:t, :] = d1
        d1h, _, _ = ln_bwd(d3h_ref[...], u1h_ref[...])
        dext[0, t:, :] = jnp.where(i < n_blk - 1, d1h, 0.0)
        _shifted_copies(dext)

        def glu_parts(zv):
            ci = zv + gb_ref[...]
            return ci[:, :D_CONV], jax.nn.sigmoid(ci[:, D_CONV:])

        val, sg = glu_parts(zm_ref[...])
        uext[0, HALO:, :] = val * sg
        valh, sgh = glu_parts(zh_ref[...])
        uext[0, 0:HALO, :] = jnp.where(i > 0, valh * sgh, 0.0)
        _shifted_copies(uext)

        for rc in range(0, t, CONV_RC):
            for lc in range(0, D_CONV, CONV_LC):
                acc = None
                dchunk = dext[0, rc:rc + CONV_RC, lc:lc + CONV_LC]
                for k in range(CONV_K):
                    term = w_ref[k:k + 1, lc:lc + CONV_LC] * _window(dext, rc + (CONV_K - 1) - k, CONV_RC, lc, CONV_LC)
                    acc = term if acc is None else acc + term
                    pr = dchunk * _window(uext, rc + HALO - (CONV_K - 1) + k, CONV_RC, lc, CONV_LC)
                    part = pr[0:8]
                    for r8 in range(8, CONV_RC, 8):
                        part = part + pr[r8:r8 + 8]
                    gw_acc[k, :, lc:lc + CONV_LC] += part
                du0_s[rc:rc + CONV_RC, lc:lc + CONV_LC] = acc

        du0 = du0_s[...]
        dval = du0 * sg
        dgt = du0 * val * sg * (1.0 - sg)
        dci_ref[:, :D_CONV] = dval.astype(dci_ref.dtype)
        dci_ref[:, D_CONV:] = dgt.astype(dci_ref.dtype)
        ggb_ref[:, :D_CONV] += _colsum(dval)
        ggb_ref[:, D_CONV:] += _colsum(dgt)

        @pl.when(i == n_blk - 1)
        def _():
            gw_ref[...] = jnp.sum(gw_acc[...], axis=1)

    halo_next = lambda w: pl.BlockSpec((HALO, w), lambda i: (jnp.minimum((i + 1) * hb, last_halo), 0))
    return pl.pallas_call(
        body, name=name, grid=(n_blk,),
        in_specs=[_rowspec(t, D_CONV), halo_next(D_CONV), _rowspec(t, D_CONV), halo_next(D_CONV),
                  _rowspec(t, c2), pl.BlockSpec((HALO, c2), lambda i: (jnp.maximum(i * hb - 1, 0), 0)),
                  _vecspec(c2), pl.BlockSpec((HALO, D_CONV), lambda i: (0, 0)), _vecspec(D_CONV), _vecspec(D_CONV),
                  _ANY],
        out_specs=[_rowspec(t, c2, SEG_CI[0] // c2), _vecspec(D_CONV), _vecspec(D_CONV), _vecspec(D_CONV),
                   _vecspec(c2), pl.BlockSpec((HALO, D_CONV), lambda i: (0, 0))],
        out_shape=[_sds(dz.shape, dz.dtype), _sds((1, D_CONV), F32), _sds((1, D_CONV), F32), _sds((1, D_CONV), F32),
                   _sds((1, c2), F32), _sds((HALO, D_CONV), F32)],
        scratch_shapes=[pltpu.VMEM((8, t + HALO, D_CONV), F32), pltpu.VMEM((8, t + HALO, D_CONV), F32),
                        pltpu.VMEM((t, D_CONV), F32), pltpu.VMEM((HALO, 8, D_CONV), F32)],
        input_output_aliases={10: 0},
        compiler_params=_cp(("arbitrary",)),
    )(du3, du3, u1, u1, z, z, glu_b, dw_w, ln_g, ln_b, dz)


def _flash_bwd(qf, kf, kft, va, do, lse_t, delta_t, *, name):
    nh, s, dk = qf.shape
    dv = va.shape[-1] // 2
    t = min(ATT_T, s)
    n = s // t
    nt = (((1,), (1,)), ((), ()))

    def body(q_ref, do_ref, lse_ref, dl_ref, k_ref, kt_ref, v_ref, dqt_ref, dk_ref, dv_ref,
             dk_s, dv_s, st_buf, dpt_buf):
        j = pl.program_id(1)

        @pl.when(j == 0)
        def _():
            dqt_ref[...] = jnp.zeros(dqt_ref.shape, F32)

        dk_s[...] = jnp.zeros(dk_s.shape, F32)
        dv_s[...] = jnp.zeros(dv_s.shape, F32)
        k, kt, v = k_ref[0], kt_ref[0], v_ref[0]
        n_un = n - 1 - j

        def rows_of(b):
            return pl.ds(pl.multiple_of((n - 1 - b) * t, t), t)

        def produce(b, slot):
            rows = rows_of(b)
            st_buf[slot] = lax.dot_general(k, q_ref[0, rows, :], nt, preferred_element_type=F32)
            dpt_buf[slot] = lax.dot_general(v, do_ref[rows, :], nt, preferred_element_type=F32)

        def consume(b, slot, masked):
            i = n - 1 - b
            rows = rows_of(b)
            q, dov = q_ref[0, rows, :], do_ref[rows, :]
            pt = jnp.exp(st_buf[slot] - lse_ref[0, i])
            if masked:
                key = lax.broadcasted_iota(jnp.int32, (t, t), 0)
                qry = lax.broadcasted_iota(jnp.int32, (t, t), 1)
                pt = jnp.where(key <= qry, pt, 0.0)
            dv_s[...] += jnp.dot(pt.astype(MXU_DTYPE), dov, preferred_element_type=F32)
            dst = (pt * (dpt_buf[slot] - dl_ref[0, i])).astype(MXU_DTYPE)
            dk_s[...] += jnp.dot(dst, q, preferred_element_type=F32)
            dqt_ref[0, i] += jnp.dot(kt, dst, preferred_element_type=F32)

        produce(0, 0)

        def pair(a, carry):
            produce(2 * a + 1, 1)
            consume(2 * a, 0, False)
            produce(2 * a + 2, 0)
            consume(2 * a + 1, 1, False)
            return carry

        lax.fori_loop(0, n_un // 2, pair, 0)

        @pl.when(n_un % 2 == 1)
        def _():
            produce(n_un, 1)
            consume(n_un - 1, 0, False)
            consume(n_un, 1, True)

        @pl.when(n_un % 2 == 0)
        def _():
            consume(n_un, 0, True)

        dk_ref[0] = dk_s[...]
        dv_ref[0] = dv_s[...]

    head = lambda h, j: (h, 0, 0)
    rowv = pl.BlockSpec((1, n, 1, t), lambda h, j: (h, 0, 0, 0))
    return pl.pallas_call(
        body, name=name, grid=(nh, n),
        in_specs=[pl.BlockSpec((1, s, dk), head),
                  pl.BlockSpec((s, dv), lambda h, j: (0, h)),
                  rowv, rowv,
                  pl.BlockSpec((1, t, dk), lambda h, j: (h, j, 0)),
                  pl.BlockSpec((1, dk, t), lambda h, j: (h, 0, j)),
                  pl.BlockSpec((1, t, dv), lambda h, j: (h, j, 0))],
        out_specs=[pl.BlockSpec((1, n, dk, t), lambda h, j: (h, 0, 0, 0)),
                   pl.BlockSpec((1, t, dk), lambda h, j: (h, j, 0)),
                   pl.BlockSpec((1, t, dv), lambda h, j: (h, j, 0))],
        out_shape=[_sds((nh, n, dk, t), F32), _sds((nh, s, dk), F32), _sds((nh, s, dv), F32)],
        scratch_shapes=[pltpu.VMEM((t, dk), F32), pltpu.VMEM((t, dv), F32),
                        pltpu.VMEM((2, t, t), F32), pltpu.VMEM((2, t, t), F32)],
        compiler_params=_cp(("parallel", "arbitrary")),
    )(qf, do, lse_t, delta_t, kf, kft, va)


def _qk_bwd(dqf, dkf, dvf, q_raw, kv, z, c_t, s1_t, s2_t, gqn, gqr, gkn, gkr, *, name):
    s = q_raw.shape[0]
    t = min(ROW_T, s)
    scale = 1.0 / math.sqrt(QK_DIM)

    def body(dq_ref, dk_ref, dv_ref, q_ref, kv_ref, kr_ref, c_ref, s1_ref, s2_ref,
             gqn_ref, gqr_ref, gkn_ref, gkr_ref, dqr_ref, dkv_ref, dkr_ref, ggq_ref, ggk_ref):
        _acc_init([ggq_ref, ggk_ref])
        c_v, s1_v, s2_v = c_ref[...], s1_ref[...], s2_ref[...]
        kr = kr_ref[...]
        kr_ss = _lanesum(kr * kr)
        dkr = jnp.zeros(kr.shape, F32)
        ggq_n = ggq_r = ggk_n = ggk_r = jnp.zeros((1, LANE), F32)

        def norm_bwd(n, r, rs, dyn, dyr, gn, gr):
            nh_, rh_ = n * rs, r * rs
            dnh, drh = dyn * gn, dyr * gr
            dot = (_lanesum(dnh * nh_) + _lanesum(drh * rh_)) * (1.0 / QK_DIM)
            return rs * (dnh - nh_ * dot), rs * (drh - rh_ * dot), _colsum(dyn * nh_), _colsum(dyr * rh_)

        for h in range(N_HEADS):
            n = q_ref[:, h * LANE:(h + 1) * LANE]
            r = q_ref[:, N_HEADS * LANE + h * LANE:N_HEADS * LANE + (h + 1) * LANE]
            rs = lax.rsqrt((_lanesum(n * n) + _lanesum(r * r)) * (1.0 / QK_DIM) + EPS)
            dyn = dq_ref[h, :, 0:LANE] * scale
            dyr = _rope_bwd(dq_ref[h, :, LANE:HEAD_PAD] * scale, c_v, s1_v, s2_v)
            dn, dr, g_n, g_r = norm_bwd(n, r, rs, dyn, dyr, gqn_ref[...], gqr_ref[...])
            dqr_ref[:, h * LANE:(h + 1) * LANE] = dn.astype(dqr_ref.dtype)
            dqr_ref[:, N_HEADS * LANE + h * LANE:N_HEADS * LANE + (h + 1) * LANE] = dr.astype(dqr_ref.dtype)
            ggq_n, ggq_r = ggq_n + g_n, ggq_r + g_r

            n = kv_ref[:, h * 2 * LANE:h * 2 * LANE + LANE]
            rs = lax.rsqrt((_lanesum(n * n) + kr_ss) * (1.0 / QK_DIM) + EPS)
            dyn = dk_ref[h, :, 0:LANE]
            dyr = _rope_bwd(dk_ref[h, :, LANE:HEAD_PAD], c_v, s1_v, s2_v)
            dn, dr, g_n, g_r = norm_bwd(n, kr, rs, dyn, dyr, gkn_ref[...], gkr_ref[...])
            dkv_ref[:, h * 2 * LANE:h * 2 * LANE + LANE] = dn.astype(dkv_ref.dtype)
            dkv_ref[:, h * 2 * LANE + LANE:(h + 1) * 2 * LANE] = dv_ref[h].astype(dkv_ref.dtype)
            dkr = dkr + dr
            ggk_n, ggk_r = ggk_n + g_n, ggk_r + g_r

        dkr_ref[...] = dkr.astype(dkr_ref.dtype)
        ggq_ref[:, 0:LANE] += ggq_n
        ggq_ref[:, LANE:] += ggq_r
        ggk_ref[:, 0:LANE] += ggk_n
        ggk_ref[:, LANE:] += ggk_r

    hspec = lambda w: pl.BlockSpec((N_HEADS, t, w), lambda i: (0, i, 0))
    wide = 2 * N_HEADS * LANE
    return pl.pallas_call(
        body, name=name, grid=(s // t,),
        in_specs=[hspec(HEAD_PAD), hspec(HEAD_PAD), hspec(V_DIM), _rowspec(t, wide), _rowspec(t, wide),
                  _rowspec(t, LANE, SEG_KR[0] // LANE), _rowspec(t, LANE), _rowspec(t, LANE), _rowspec(t, LANE),
                  _vecspec(LANE), _vecspec(LANE), _vecspec(LANE), _vecspec(LANE)],
        out_specs=[_rowspec(t, wide), _rowspec(t, wide), _rowspec(t, LANE), _vecspec(2 * LANE), _vecspec(2 * LANE)],
        out_shape=[_sds((s, wide), MXU_DTYPE), _sds((s, wide), MXU_DTYPE), _sds((s, LANE), MXU_DTYPE),
                   _sds((1, 2 * LANE), F32), _sds((1, 2 * LANE), F32)],
        compiler_params=_cp(("arbitrary",)),
    )(dqf, dkf, dvf, q_raw, kv, z, c_t, s1_t, s2_t, gqn, gqr, gkn, gkr)


def _lat_bwd(dqn, dkn, dkr, z, dz, g_ql, g_kvl, *, name):
    s = z.shape[0]
    t = min(ROW_T, s)
    o_ql, o_kvl, o_kr = (seg[0] - SEG_LAT[0] for seg in (SEG_QL, SEG_KVL, SEG_KR))

    def body(dq_ref, dk_ref, dkr_ref, ql_ref, kvl_ref, gq_ref, gk_ref, dz_in_ref, dz_ref, ggq_ref, ggk_ref):
        _acc_init([ggq_ref, ggk_ref])
        for d_ref, src, g_ref, off, gg_ref in ((dq_ref, ql_ref, gq_ref, o_ql, ggq_ref),
                                               (dk_ref, kvl_ref, gk_ref, o_kvl, ggk_ref)):
            v, dy = src[...], d_ref[...]
            r = lax.rsqrt(jnp.mean(v * v, axis=-1, keepdims=True) + EPS)
            vh = v * r
            dvh = dy * g_ref[...]
            dz_ref[:, off:off + v.shape[1]] = (
                r * (dvh - vh * jnp.mean(dvh * vh, axis=-1, keepdims=True))).astype(dz_ref.dtype)
            gg_ref[...] += _colsum(dy * vh)
        dz_ref[:, o_kr:o_kr + LANE] = dkr_ref[...]
        dz_ref[:, o_kr + LANE:] = jnp.zeros((t, SEG_LAT[1] - o_kr - LANE), dz_ref.dtype)

    return pl.pallas_call(
        body, name=name, grid=(s // t,),
        in_specs=[_rowspec(t, Q_LORA), _rowspec(t, KV_LORA), _rowspec(t, LANE),
                  _rowspec(t, Q_LORA, SEG_QL[0] // Q_LORA), _rowspec(t, KV_LORA, SEG_KVL[0] // KV_LORA),
                  _vecspec(Q_LORA), _vecspec(KV_LORA), _ANY],
        out_specs=[_rowspec(t, SEG_LAT[1], SEG_LAT[0] // SEG_LAT[1]), _vecspec(Q_LORA), _vecspec(KV_LORA)],
        out_shape=[_sds(dz.shape, dz.dtype), _sds((1, Q_LORA), F32), _sds((1, KV_LORA), F32)],
        input_output_aliases={7: 0},
        compiler_params=_cp(("arbitrary",)),
    )(dqn, dkn, dkr, z, z, g_ql, g_kvl, dz)


def _prenorm_bwd(dh, x, gxo, g, sc1p, *, name):
    s, d = x.shape
    t = min(ROW_T, s)

    def body(dh_ref, x_ref, gx_ref, g_ref, sc_ref, dx_ref, dsh_ref, dsc_ref, gg_ref):
        _acc_init([dsh_ref, dsc_ref, gg_ref])
        xv, dhv = x_ref[...], dh_ref[...]
        r = lax.rsqrt(jnp.mean(xv * xv, axis=-1, keepdims=True) + EPS)
        xn = xv * r
        dsh_ref[...] += _colsum(dhv)
        dsc_ref[...] += _colsum(dhv * (xn * g_ref[...]))
        dm = dhv * sc_ref[...]
        gg_ref[...] += _colsum(dm * xn)
        dxn = dm * g_ref[...]
        dx_ref[...] = gx_ref[...] + r * (dxn - xn * jnp.mean(dxn * xn, axis=-1, keepdims=True))

    return pl.pallas_call(
        body, name=name, grid=(s // t,),
        in_specs=[_rowspec(t, d), _rowspec(t, d), _rowspec(t, d), _vecspec(d), _vecspec(d)],
        out_specs=[_rowspec(t, d), _vecspec(d), _vecspec(d), _vecspec(d)],
        out_shape=[_sds((s, d), F32), _sds((1, d), F32), _sds((1, d), F32), _sds((1, d), F32)],
        compiler_params=_cp(("arbitrary",)),
    )(dh, x, gxo, g, sc1p)


def _ada_fwd(c_all, ada_w, ada_b_cols, *, name):
    nl, d, cols = ada_w.shape

    def body(c_ref, w_ref, b_ref, o_ref):
        ca = _silu(c_ref[...]).astype(MXU_DTYPE)
        o_ref[0] = jnp.dot(ca, w_ref[0].astype(MXU_DTYPE), preferred_element_type=F32) + b_ref[0]

    return pl.pallas_call(
        body, name=name, grid=(nl,),
        in_specs=[pl.BlockSpec((N_DEV, d), lambda l: (0, 0)), pl.BlockSpec((1, d, cols), lambda l: (l, 0, 0)),
                  pl.BlockSpec((1, 1, cols), lambda l: (l, 0, 0))],
        out_specs=pl.BlockSpec((1, N_DEV, cols), lambda l: (l, 0, 0)),
        out_shape=_sds((nl, N_DEV, cols), F32),
        compiler_params=_cp(("parallel",)),
    )(c_all, ada_w, ada_b_cols)


def _ada_bwd(c_all_t, dmod_cols, *, name):
    nl, _, cols = dmod_cols.shape
    d = c_all_t.shape[0]

    def body(c_ref, dm_ref, o_ref):
        ca = _silu(c_ref[...]).astype(MXU_DTYPE)
        o_ref[0] = jnp.dot(ca, dm_ref[0].astype(MXU_DTYPE), preferred_element_type=F32)

    return pl.pallas_call(
        body, name=name, grid=(nl,),
        in_specs=[pl.BlockSpec((d, N_DEV), lambda l: (0, 0)), pl.BlockSpec((1, N_DEV, cols), lambda l: (l, 0, 0))],
        out_specs=pl.BlockSpec((1, d, cols), lambda l: (l, 0, 0)),
        out_shape=_sds((nl, d, cols), F32),
        compiler_params=_cp(("parallel",)),
    )(c_all_t, dmod_cols)


def _adamw(gparts, w, m, v, *, name):
    shape = w.shape
    cols = shape[-1]
    per_layer = isinstance(gparts, (list, tuple))
    nl = shape[0] if per_layer else 1
    rows = w.size // cols // nl
    glist = list(gparts) if per_layer else [gparts]
    npart = glist[0].shape[0]
    glist = [g.reshape(npart, rows, cols) for g in glist]
    w3, m3, v3 = (a.reshape(nl, rows, cols) for a in (w, m, v))
    budget = 2 * 1024 * 1024
    fits = [t for t in (256, 128, 64, 32, 16, 8)
            if rows % t == 0 and npart * t * cols * glist[0].dtype.itemsize <= budget]
    t = fits[0] if fits else rows
    nb = rows // t

    def body(*refs):
        g_refs = refs[:nl]
        w_ref, m_ref, v_ref, go_ref, d_ref, mo_ref, vo_ref, g_s = refs[nl:]
        layer = pl.program_id(0)
        for l in range(nl):
            @pl.when(layer == l)
            def _(l=l):
                g = g_refs[l][0].astype(F32)
                for p in range(1, npart):
                    g = g + g_refs[l][p].astype(F32)
                g_s[...] = g

        g = g_s[...]
        mn = ADAM_B1 * m_ref[0] + (1.0 - ADAM_B1) * g
        vn = ADAM_B2 * v_ref[0] + (1.0 - ADAM_B2) * (g * g)
        m_hat = mn / (1.0 - ADAM_B1 ** ADAM_STEP)
        v_hat = vn / (1.0 - ADAM_B2 ** ADAM_STEP)
        go_ref[0] = g
        d_ref[0] = -ADAM_LR * (m_hat / (jnp.sqrt(v_hat) + ADAM_EPS) + ADAM_WD * w_ref[0])
        mo_ref[0] = mn
        vo_ref[0] = vn

    def g_map(l):
        return lambda layer, i: (0, jnp.where(layer == l, i, jnp.where(layer < l, 0, nb - 1)), 0)

    spec = pl.BlockSpec((1, t, cols), lambda layer, i: (layer, i, 0))
    outs = pl.pallas_call(
        body, name=name, grid=(nl, nb),
        in_specs=[pl.BlockSpec((npart, t, cols), g_map(l)) for l in range(nl)] + [spec, spec, spec],
        out_specs=[spec] * 4, out_shape=[_sds((nl, rows, cols), F32)] * 4,
        scratch_shapes=[pltpu.VMEM((t, cols), F32)],
        compiler_params=_cp(("arbitrary", "arbitrary")),
    )(*glist, w3, m3, v3)
    return tuple(o.reshape(shape) for o in outs)


_ANY = pl.BlockSpec(memory_space=pl.ANY)


def _all_gather(blocks, *, name):
    na = len(blocks)

    def body(*refs):
        x_refs, out_refs = refs[:na], refs[na:2 * na]
        send_sems, recv_sems, local_sems = refs[2 * na:]
        x, y, c = lax.axis_index("x"), lax.axis_index("y"), lax.axis_index("c")
        me, sibling = (x, y, c), (x, y, 1 - c)
        chips = [(1 - x, y), (x, 1 - y), (1 - x, 1 - y)]

        def slot(a, px, py, pc):
            return out_refs[a].at[4 * px + 2 * py + pc]

        def copy(a, k, blk, to, src=None):
            return pltpu.make_async_remote_copy(
                src_ref=slot(a, *blk) if src is None else src, dst_ref=slot(a, *blk),
                send_sem=send_sems.at[7 * a + k], recv_sem=recv_sems.at[7 * a + k],
                device_id=to, device_id_type=MESH_ID)

        mine = [pltpu.make_async_copy(x_refs[a], slot(a, *me), local_sems.at[a]) for a in range(na)]
        for cp in mine:
            cp.start()
        first = []
        for a in range(na):
            first.append(copy(a, 0, me, sibling, src=x_refs[a]))
            first += [copy(a, 1 + j, me, (*chip, c), src=x_refs[a]) for j, chip in enumerate(chips)]
        for cp in first:
            cp.start()
        passed = []
        for a in range(na):
            for j, chip in enumerate(chips):
                copy(a, 1 + j, (*chip, c), me).wait_recv()
                fwd = copy(a, 4 + j, (*chip, c), sibling)
                fwd.start()
                passed.append(fwd)
        for a in range(na):
            copy(a, 0, sibling, me).wait_recv()
            for j, chip in enumerate(chips):
                copy(a, 4 + j, (*chip, 1 - c), me).wait_recv()
        for cp in first + passed:
            cp.wait_send()
        for cp in mine:
            cp.wait()

    outs = pl.pallas_call(
        body, name=name, in_specs=[_ANY] * na, out_specs=[_ANY] * na,
        out_shape=[_sds((N_DEV,) + b.shape, b.dtype) for b in blocks],
        scratch_shapes=[pltpu.SemaphoreType.DMA((7 * na,)), pltpu.SemaphoreType.DMA((7 * na,)),
                        pltpu.SemaphoreType.DMA((na,))],
    )(*blocks)
    return list(outs)


def _all_to_all(parts, *, name):
    na = len(parts)

    def body(*refs):
        in_refs, out_refs = refs[:na], refs[na:2 * na]
        send_sems, recv_sems, local_sems = refs[2 * na:]
        x, y, c = lax.axis_index("x"), lax.axis_index("y"), lax.axis_index("c")
        me = 4 * x + 2 * y + c
        mine = [pltpu.make_async_copy(in_refs[a].at[me], out_refs[a].at[me], local_sems.at[a]) for a in range(na)]
        for cp in mine:
            cp.start()
        copies = []
        for a in range(na):
            for k in range(1, N_DEV):
                px = 1 - x if k & 4 else x
                py = 1 - y if k & 2 else y
                pc = 1 - c if k & 1 else c
                cp = pltpu.make_async_remote_copy(
                    src_ref=in_refs[a].at[4 * px + 2 * py + pc], dst_ref=out_refs[a].at[me],
                    send_sem=send_sems.at[7 * a + k - 1], recv_sem=recv_sems.at[7 * a + k - 1],
                    device_id=(px, py, pc), device_id_type=MESH_ID)
                cp.start()
                copies.append(cp)
        for cp in copies:
            cp.wait()
        for cp in mine:
            cp.wait()

    outs = pl.pallas_call(
        body, name=name, in_specs=[_ANY] * na, out_specs=[_ANY] * na,
        out_shape=[_sds(p.shape, p.dtype) for p in parts],
        scratch_shapes=[pltpu.SemaphoreType.DMA((7 * na,)), pltpu.SemaphoreType.DMA((7 * na,)),
                        pltpu.SemaphoreType.DMA((na,))],
    )(*parts)
    return list(outs)


_HBM = pl.BlockSpec(memory_space=pltpu.HBM)
_SEM = pl.BlockSpec(memory_space=pltpu.SEMAPHORE)
_EFFECT = pltpu.SideEffectType.DATAFLOW_SIDE_EFFECTING


def _peers(x, y, c):
    out = []
    for k in range(1, N_DEV):
        out.append((1 - x if k & 4 else x, 1 - y if k & 2 else y, 1 - c if k & 1 else c))
    return out


def _own_slots(srcs, scatter, *, name, after=None):
    na = len(srcs)
    n_extra = 0 if after is None else 1
    me = (4 * lax.axis_index("x") + 2 * lax.axis_index("y") + lax.axis_index("c")).astype(jnp.int32).reshape(1)

    def body(me_ref, *refs):
        in_refs, out_refs = refs[:na], refs[na + n_extra:]
        for a in range(na):
            out_refs[a][0] = in_refs[a][0] if scatter else in_refs[a][...]

    def slot_spec(shard):
        zeros = (0,) * len(shard)
        return pl.BlockSpec((1,) + tuple(shard), lambda i, me_ref: (me_ref[0],) + zeros)

    def whole_spec(shape):
        zeros = (0,) * len(shape)
        return pl.BlockSpec(tuple(shape), lambda i, me_ref: zeros)

    shards = [s.shape[1:] if scatter else s.shape for s in srcs]
    in_specs = [slot_spec(sh) if scatter else whole_spec(sh) for sh in shards] + [_ANY] * n_extra
    outs = pl.pallas_call(
        body, name=name,
        grid_spec=pltpu.PrefetchScalarGridSpec(
            num_scalar_prefetch=1, grid=(1,), in_specs=in_specs, out_specs=[slot_spec(sh) for sh in shards]),
        out_shape=[_sds((N_DEV,) + tuple(sh), s.dtype) for sh, s in zip(shards, srcs)],
        compiler_params=_cp(("arbitrary",)),
    )(me, *srcs, *([] if after is None else [after]))
    return list(outs)


def _exchange_copies(src_refs, land_refs, send_sems, recv_sems, scatter):
    x, y, c = lax.axis_index("x"), lax.axis_index("y"), lax.axis_index("c")
    me = 4 * x + 2 * y + c
    cps = []
    for a in range(len(src_refs)):
        for k, (px, py, pc) in enumerate(_peers(x, y, c)):
            src = src_refs[a].at[4 * px + 2 * py + pc] if scatter else src_refs[a]
            cps.append(pltpu.make_async_remote_copy(
                src_ref=src, dst_ref=land_refs[a].at[me], send_sem=send_sems.at[7 * a + k],
                recv_sem=recv_sems.at[7 * a + k], device_id=(px, py, pc), device_id_type=MESH_ID))
    return cps


def _exchange_start(srcs, lands, scatter, *, name):
    na = len(srcs)

    def body(*refs):
        src_refs, land_refs = refs[:na], refs[na:2 * na]
        send_sems, recv_sems = refs[2 * na], refs[2 * na + 1]
        token = refs[-1]
        for cp in _exchange_copies(src_refs, land_refs, send_sems, recv_sems, scatter):
            cp.start()
        token[...] = jnp.zeros(token.shape, token.dtype)

    hbm = lambda a: pltpu.HBM(a.shape, a.dtype)
    outs = pl.pallas_call(
        body, name=name,
        out_shape=(pltpu.SemaphoreType.DMA((7 * na,)), pltpu.SemaphoreType.DMA((7 * na,)),
                   *[hbm(a) for a in srcs], *[hbm(a) for a in lands], _sds((8, LANE), F32)),
        in_specs=[_HBM] * (2 * na),
        out_specs=(_SEM, _SEM, *[_HBM] * (2 * na), pl.BlockSpec(memory_space=pltpu.VMEM)),
        input_output_aliases={i: 2 + i for i in range(2 * na)},
        compiler_params=pltpu.CompilerParams(has_side_effects=_EFFECT),
    )(*[pltpu.with_memory_space_constraint(a, pltpu.HBM) for a in list(srcs) + list(lands)])
    return outs[0], outs[1], list(outs[2:2 + na]), list(outs[2 + na:2 + 2 * na]), outs[-1]


def _exchange_wait(send_sems, recv_sems, srcs, lands, after, scatter, *, name):
    na = len(srcs)

    def body(*refs):
        src_refs, land_refs = refs[:na], refs[na:2 * na]
        s_sems, r_sems = refs[2 * na], refs[2 * na + 1]
        for cp in _exchange_copies(src_refs, land_refs, s_sems, r_sems, scatter):
            cp.wait_send()
            cp.wait_recv()

    hbm = lambda a: pltpu.HBM(a.shape, a.dtype)
    outs = pl.pallas_call(
        body, name=name,
        out_shape=(*[hbm(a) for a in srcs], *[hbm(a) for a in lands]),
        in_specs=[_HBM] * (2 * na) + [_SEM, _SEM, _ANY],
        out_specs=tuple([_HBM] * (2 * na)),
        input_output_aliases={i: i for i in range(2 * na)},
        compiler_params=pltpu.CompilerParams(has_side_effects=_EFFECT),
    )(*srcs, *lands, send_sems, recv_sems, after)
    return list(outs[na:])


_WIN_SEGS = (("ql", 0, Q_LORA, SEG_QL[0]), ("kvl", Q_LORA, KV_LORA, SEG_KVL[0]),
             ("kr", Q_LORA + KV_LORA, ROPE, SEG_KR[0]), ("mg", Q_LORA + KV_LORA + ROPE, D_MLA, SEG_MG[0]),
             ("ci", Q_LORA + KV_LORA + ROPE + D_MLA, 2 * D_CONV, SEG_CI[0]),
             ("cg", Q_LORA + KV_LORA + ROPE + D_MLA + 2 * D_CONV, D_CONV, SEG_CG[0]))
_WIN_SHARD = IN_COLS // N_DEV


def _win_pieces():
    out = []
    for _, o, n, new in _WIN_SEGS:
        for j in range(N_DEV):
            lo, hi = max(o, j * _WIN_SHARD), min(o + n, (j + 1) * _WIN_SHARD)
            if lo < hi:
                out.append((j, lo - j * _WIN_SHARD, new + lo - o, hi - lo))
    return out


def _win_assemble(w_all, *, name):
    d = w_all.shape[1]
    t = min(ROW_T, d)
    pieces = sorted(_win_pieces(), key=lambda p: p[2])

    def body(w_ref, o_ref):
        cols = [w_ref[j, :, lo:lo + n].astype(F32) for j, lo, _, n in pieces]
        cols.append(jnp.zeros((t, IN_PAD - (SEG_KR[0] + ROPE)), F32))
        o_ref[...] = jnp.concatenate(cols, axis=1).astype(o_ref.dtype)

    return pl.pallas_call(
        body, name=name, grid=(d // t,),
        in_specs=[pl.BlockSpec((N_DEV, t, _WIN_SHARD), lambda i: (0, i, 0))],
        out_specs=_rowspec(t, IN_PAD), out_shape=_sds((d, IN_PAD), w_all.dtype),
        compiler_params=_cp(("parallel",)),
    )(w_all)


def _win_split(grad, *, name):
    d = grad.shape[0]
    t = min(ROW_T, d)
    by_shard = [sorted([p for p in _win_pieces() if p[0] == j], key=lambda p: p[1]) for j in range(N_DEV)]

    def body(g_ref, o_ref):
        g = g_ref[...]
        for j in range(N_DEV):
            cols = [g[:, new:new + n] for _, _, new, n in by_shard[j]]
            o_ref[j] = jnp.concatenate(cols, axis=1).astype(o_ref.dtype)

    return pl.pallas_call(
        body, name=name, grid=(d // t,),
        in_specs=[_rowspec(t, IN_PAD)],
        out_specs=pl.BlockSpec((N_DEV, t, _WIN_SHARD), lambda i: (0, i, 0)),
        out_shape=_sds((N_DEV, d, _WIN_SHARD), WIRE_DTYPE),
        compiler_params=_cp(("parallel",)),
    )(grad)


def _cols_to_shards(a):
    r, n = a.shape
    return a.reshape(r, N_DEV, n // N_DEV).transpose(1, 0, 2)


def _shards_to_cols(a):
    nd, r, w = a.shape
    return a.transpose(1, 0, 2).reshape(r, nd * w)


def _win_permute(w_in):
    o_ql, o_kvl, o_kr, o_mg = 0, Q_LORA, Q_LORA + KV_LORA, Q_LORA + KV_LORA + ROPE
    o_ci = o_mg + D_MLA
    o_cg = o_ci + 2 * D_CONV
    seg = lambda o, n: w_in[:, o:o + n]
    pad = jnp.zeros((w_in.shape[0], IN_PAD - (SEG_KR[0] + ROPE)), w_in.dtype)
    return jnp.concatenate([seg(o_ci, 2 * D_CONV), seg(o_mg, D_MLA), seg(o_cg, D_CONV), seg(o_ql, Q_LORA),
                            seg(o_kvl, KV_LORA), seg(o_kr, ROPE), pad], axis=1)


def _win_unpermute(g):
    seg = lambda s, n=None: g[:, s[0]:s[0] + (s[1] if n is None else n)]
    return jnp.concatenate([seg(SEG_QL), seg(SEG_KVL), seg(SEG_KR, ROPE), seg(SEG_MG), seg(SEG_CI), seg(SEG_CG)], axis=1)


def _qup_permute(w):
    w3 = w.reshape(w.shape[0], N_HEADS, QK_DIM)
    nope = w3[:, :, :NOPE].reshape(w.shape[0], N_HEADS * NOPE)
    rope = jnp.pad(w3[:, :, NOPE:], ((0, 0), (0, 0), (0, LANE - ROPE))).reshape(w.shape[0], N_HEADS * LANE)
    return jnp.concatenate([nope, rope], axis=1)


def _qup_unpermute(g):
    r = g.shape[0]
    nope = g[:, :N_HEADS * NOPE].reshape(r, N_HEADS, NOPE)
    rope = g[:, N_HEADS * NOPE:].reshape(r, N_HEADS, LANE)[:, :, :ROPE]
    return jnp.concatenate([nope, rope], axis=2).reshape(r, N_HEADS * QK_DIM)


def _norm_tiles(g):
    return g[:NOPE].reshape(1, LANE), jnp.pad(g[NOPE:], (0, LANE - ROPE)).reshape(1, LANE)


def _norm_untile(gt):
    return jnp.concatenate([gt[0, :NOPE], gt[0, LANE:LANE + ROPE]])


def _rope_tiles(positions):
    inv_freq = 1.0 / (ROPE_THETA ** (jnp.arange(0, ROPE, 2, dtype=F32) / ROPE))
    ang = positions.astype(F32)[:, None] * inv_freq
    cos, sin = jnp.cos(ang), jnp.sin(ang)
    zq = jnp.zeros_like(cos)
    c_t = jnp.concatenate([cos, cos, zq, zq], axis=1)
    s1_t = jnp.concatenate([-sin, zq, zq, zq], axis=1)
    s2_t = jnp.concatenate([zq, sin, zq, zq], axis=1)
    return c_t, s1_t, s2_t


_BIG = ("w_in", "w_q_up", "w_kv_up", "w_pw", "w_out")
_COL_SHARDED = ("w_in", "w_q_up", "w_kv_up")


def _pack_rows(arrs):
    return jnp.concatenate([a.reshape(-1, LANE) for a in arrs], axis=0)


def _unpack_rows(buf, shapes):
    out, r0 = [], 0
    lead = buf.shape[:-2]
    for shp in shapes:
        n = math.prod(shp) // LANE
        out.append(buf[..., r0:r0 + n, :].reshape(lead + tuple(shp)))
        r0 += n
    return out


_SMALL = (("dmod", 3 * D_MODEL), ("norm_g", D_MODEL), ("q_lat_g", Q_LORA), ("kv_lat_g", KV_LORA),
          ("q_norm_g", 2 * LANE), ("k_norm_g", 2 * LANE), ("glu_b", 2 * D_CONV), ("dw_w", HALO * D_CONV),
          ("dw_b", D_CONV), ("conv_ln_g", D_CONV), ("conv_ln_b", D_CONV), ("b_pw", D_CONV))


def _layer_fwd(x, p, rope, l):
    n = lambda s: f"{s}_l{l}"
    c_t, s1_t, s2_t = rope
    h = _prenorm(x, p["norm_g"], p["shift"], p["sc1p"], name=n("prenorm"))
    z = _mm(h, p["w_in"], name=n("in_proj"), tn=IN_TILE, n_outer=True)
    qn, kn = _lat_norm(z, p["q_lat_g"], p["kv_lat_g"], name=n("lat_norm"))
    q_raw = _mm(qn, p["w_q_up"], name=n("q_up"), tn=1024)
    kv = _mm(kn, p["w_kv_up"], name=n("kv_up"), tn=1024)
    qf, kf, vf = _qk_prep(q_raw, kv, z, c_t, s1_t, s2_t, *p["qk_tiles"], name=n("qk_prep"))
    o, lse = _flash_fwd(qf, kf, vf, name=n("flash_fwd"))
    u1, u3 = _conv_fwd(z, p["glu_b"], p["dw_w"], p["dw_b"], p["conv_ln_g"], p["conv_ln_b"], name=n("conv_fwd"))
    u4m = _mm(u3, p["w_pw"], name=n("pw"), tn=1024)
    cat = _gate_cat(o, z, u4m, p["b_pw"], name=n("gate_cat"))
    y = _mm(cat, p["w_out"], name=n("out_proj"), tn=1024)
    x_next = _residual(x, y, p["gate"], name=n("residual"))
    saved = dict(x=x, h=h, z=z, qn=qn, kn=kn, q_raw=q_raw, kv=kv, qf=qf, kf=kf, vf=vf, o=o, lse=lse,
                 u1=u1, u3=u3, u4m=u4m, cat=cat, y=y)
    return x_next, saved


def _layer_bwd(gxo, p, sv, rope, l, hook_rest=None, hook_w_in=None):
    n = lambda s: f"{s}_l{l}"
    c_t, s1_t, s2_t = rope
    z = sv["z"]
    dy, dgate = _out_bwd(gxo, sv["y"], p["gate"], name=n("out_bwd"))
    g_w_out = _mm(sv["cat"], dy, ta=True, name=n("g_w_out"), tm=1024, tn=1024, tk=512)
    dcat = _mm(dy, p["w_out"], tb=True, name=n("d_cat"), tn=1024)
    do, delta, du4, g_b_pw, dz = _gate_bwd(dcat, sv["o"], z, sv["u4m"], p["b_pw"], name=n("gate_bwd"))
    g_w_pw = _mm(sv["u3"], du4, ta=True, name=n("g_w_pw"), tm=1024, tn=1024, tk=512)
    du3 = _mm(du4, p["w_pw"], tb=True, name=n("d_u3"), tn=1024)
    dz, g_ln_g, g_ln_b, g_dw_b, g_glu_b, g_dw_w = _conv_bwd(
        du3, sv["u1"], z, dz, p["glu_b"], p["dw_w"], p["conv_ln_g"], p["conv_ln_b"], name=n("conv_bwd"))
    t_att = min(ATT_T, z.shape[0])
    to_lanes = lambda a: a.reshape(N_HEADS, z.shape[0] // t_att, 1, t_att)
    dqt, dkf, dvf = _flash_bwd(sv["qf"], sv["kf"], jnp.swapaxes(sv["kf"], 1, 2), sv["vf"], do,
                               to_lanes(sv["lse"][:, :, 0]), to_lanes(delta), name=n("flash_bwd"))
    dqf = jnp.swapaxes(dqt, 2, 3).reshape(N_HEADS, z.shape[0], HEAD_PAD)
    dq_raw, dkv, dkr, g_qn, g_kn = _qk_bwd(dqf, dkf, dvf, sv["q_raw"], sv["kv"], z, c_t, s1_t, s2_t,
                                            *p["qk_tiles"], name=n("qk_bwd"))
    g_w_q_up = _mm(sv["qn"], dq_raw, ta=True, name=n("g_w_q_up"), tm=512, tn=1024, tk=512)
    dqn = _mm(dq_raw, p["w_q_up"], tb=True, name=n("d_qn"))
    g_w_kv_up = _mm(sv["kn"], dkv, ta=True, name=n("g_w_kv_up"), tm=256, tn=1024, tk=512)
    dkn = _mm(dkv, p["w_kv_up"], tb=True, name=n("d_kn"))
    dz, g_ql, g_kvl = _lat_bwd(dqn, dkn, dkr, z, dz, p["q_lat_g"], p["kv_lat_g"], name=n("lat_bwd"))
    big = dict(w_q_up=g_w_q_up, w_kv_up=g_w_kv_up, w_pw=g_w_pw, w_out=g_w_out)
    after = None if hook_rest is None else hook_rest(big)
    g_w_in = _mm(sv["h"], dz, ta=True, name=n("g_w_in"), tm=1024, tn=IN_TILE, tk=512, after=after)
    big["w_in"] = g_w_in
    after = None if hook_w_in is None else hook_w_in(g_w_in)
    dh = _mm(dz, p["w_in"], tb=True, name=n("d_h"), tn=1024, tk=IN_TILE, after=after)
    dx, dshift, dscale, g_norm = _prenorm_bwd(dh, sv["x"], gxo, p["norm_g"], p["sc1p"], name=n("prenorm_bwd"))
    small = dict(dmod=jnp.concatenate([dshift, dscale, dgate], axis=1), norm_g=g_norm, q_lat_g=g_ql, kv_lat_g=g_kvl,
                 q_norm_g=g_qn, k_norm_g=g_kn, glu_b=g_glu_b, dw_w=g_dw_w, dw_b=g_dw_b,
                 conv_ln_g=g_ln_g, conv_ln_b=g_ln_b, b_pw=g_b_pw)
    return dx, big, small


def _layer_params(l, full, mod_l, small):
    d = D_MODEL
    row = lambda a: a.reshape(1, -1)
    shift, scale, gate = mod_l[:, :d], mod_l[:, d:2 * d], mod_l[:, 2 * d:]
    dw_w = jnp.pad(full["dw_w"][l], ((0, HALO - CONV_K), (0, 0)))
    return dict(
        shift=shift, sc1p=1.0 + scale, gate=gate, norm_g=row(small["norm_g"][l]),
        w_in=full["w_in"][l], w_q_up=full["w_q_up"][l], w_kv_up=full["w_kv_up"][l],
        w_pw=full["w_pw"][l], w_out=full["w_out"][l], dw_w=dw_w,
        q_lat_g=row(small["q_lat_g"][l]), kv_lat_g=row(small["kv_lat_g"][l]),
        qk_tiles=_norm_tiles(small["q_norm_g"][l]) + _norm_tiles(small["k_norm_g"][l]),
        glu_b=row(small["glu_b"][l]), dw_b=row(small["dw_b"][l]), conv_ln_g=row(small["conv_ln_g"][l]),
        conv_ln_b=row(small["conv_ln_b"][l]), b_pw=row(small["b_pw"][l]))


def kernel(x, c, positions, ada_w, ada_b, norm_g, w_in, q_lat_g, w_q_up, kv_lat_g, w_kv_up, q_norm_g, k_norm_g, glu_b, dw_w, dw_b, conv_ln_g, conv_ln_b, w_pw, b_pw, w_out, loss_target, m_ada_w, m_ada_b, m_norm_g, m_w_in, m_q_lat_g, m_w_q_up, m_kv_lat_g, m_w_kv_up, m_q_norm_g, m_k_norm_g, m_glu_b, m_dw_w, m_dw_b, m_conv_ln_g, m_conv_ln_b, m_w_pw, m_b_pw, m_w_out, v_ada_w, v_ada_b, v_norm_g, v_w_in, v_q_lat_g, v_w_q_up, v_kv_lat_g, v_w_kv_up, v_q_norm_g, v_k_norm_g, v_glu_b, v_dw_w, v_dw_b, v_conv_ln_g, v_conv_ln_b, v_w_pw, v_b_pw, v_w_out):
    names = ("ada_w", "ada_b", "norm_g", "w_in", "q_lat_g", "w_q_up", "kv_lat_g", "w_kv_up", "q_norm_g",
             "k_norm_g", "glu_b", "dw_w", "dw_b", "conv_ln_g", "conv_ln_b", "w_pw", "b_pw", "w_out")
    w_loc = dict(zip(names, (ada_w, ada_b, norm_g, w_in, q_lat_g, w_q_up, kv_lat_g, w_kv_up, q_norm_g, k_norm_g,
                             glu_b, dw_w, dw_b, conv_ln_g, conv_ln_b, w_pw, b_pw, w_out)))
    m_loc = dict(zip(names, (m_ada_w, m_ada_b, m_norm_g, m_w_in, m_q_lat_g, m_w_q_up, m_kv_lat_g, m_w_kv_up,
                             m_q_norm_g, m_k_norm_g, m_glu_b, m_dw_w, m_dw_b, m_conv_ln_g, m_conv_ln_b, m_w_pw,
                             m_b_pw, m_w_out)))
    v_loc = dict(zip(names, (v_ada_w, v_ada_b, v_norm_g, v_w_in, v_q_lat_g, v_w_q_up, v_kv_lat_g, v_w_kv_up,
                             v_q_norm_g, v_k_norm_g, v_glu_b, v_dw_w, v_dw_b, v_conv_ln_g, v_conv_ln_b, v_w_pw,
                             v_b_pw, v_w_out)))
    nl, d = N_LAYERS, D_MODEL
    me = 4 * lax.axis_index("x") + 2 * lax.axis_index("y") + lax.axis_index("c")
    x2, tgt = x[0], loss_target[0]
    ada_cols = ada_w.shape[-1]

    c_all = _all_gather([c.reshape(d // LANE, LANE)], name="gather_c")[0].reshape(N_DEV, d)
    ada_b_cols = lax.dynamic_slice_in_dim(ada_b, me * ada_cols, ada_cols, axis=1).reshape(nl, 1, ada_cols)
    mod_cols = _ada_fwd(c_all, ada_w, ada_b_cols, name="ada_fwd")
    mod_all = _all_gather([mod_cols], name="gather_mod")[0]
    mod_me = lax.dynamic_index_in_dim(mod_all, me, axis=2, keepdims=False)
    mod = mod_me.transpose(1, 0, 2).reshape(nl, 1, N_DEV * ada_cols)

    dw_pad = jnp.pad(dw_w, ((0, 0), (0, HALO - CONV_K), (0, 0)))
    wire = {k: w_loc[k].astype(WIRE_DTYPE) for k in _BIG}
    gathered = _all_gather([wire[k][0] for k in _BIG] + [dw_pad], name="gather_weights_l0")
    dw_all = gathered[-1]
    src1 = [wire[k][1] for k in _BIG]
    lands1 = _own_slots(src1, False, name="own_weights_l1", after=mod)
    gsend, grecv, src1, lands1, tok_w1 = _exchange_start(src1, lands1, False, name="gather_start_l1")

    def kernel_layout(parts, l):
        return dict(w_in=_win_assemble(parts["w_in"], name=f"w_in_assemble_l{l}"),
                    w_q_up=_qup_permute(_shards_to_cols(parts["w_q_up"])),
                    w_kv_up=_shards_to_cols(parts["w_kv_up"]),
                    w_pw=parts["w_pw"].reshape(D_CONV, D_CONV),
                    w_out=parts["w_out"].reshape(D_MLA + D_CONV, d))

    small_in = dict(norm_g=norm_g, q_lat_g=q_lat_g, kv_lat_g=kv_lat_g, q_norm_g=q_norm_g, k_norm_g=k_norm_g,
                    glu_b=glu_b, dw_b=dw_b, conv_ln_g=conv_ln_g, conv_ln_b=conv_ln_b, b_pw=b_pw)
    dw_full = [_shards_to_cols(dw_all[:, l])[:CONV_K] for l in range(nl)]
    rope = _rope_tiles(positions[0])

    def layer_params(l, parts, mod_l):
        full = {k: {l: a} for k, a in kernel_layout(parts, l).items()}
        full["dw_w"] = dw_full
        return _layer_params(l, full, mod_l, small_in)

    params, saved = [None] * nl, [None] * nl
    params[0] = layer_params(0, dict(zip(_BIG, gathered[:-1])), mod[0] + tok_w1[0, 0])
    xs, saved[0] = _layer_fwd(x2, params[0], rope, 0)
    parts1 = _exchange_wait(gsend, grecv, src1, lands1, xs, False, name="gather_wait_l1")
    params[1] = layer_params(1, dict(zip(_BIG, parts1)), mod[1])
    xs, saved[1] = _layer_fwd(xs, params[1], rope, 1)
    gx, loss_part = _loss_head(xs, tgt, name="loss_head")
    loss = lax.psum(loss_part[0, 0], ("x", "y", "c"))

    def shard_major(k, g):
        if k == "w_q_up":
            g = _qup_unpermute(g)
        if k in _COL_SHARDED:
            return _cols_to_shards(g)
        return g.reshape((N_DEV, g.shape[0] // N_DEV, g.shape[1]))

    def scatter_start(send, tag):
        lands = _own_slots(send, True, name=f"own_grads_{tag}")
        return _exchange_start(send, lands, True, name=f"scatter_start_{tag}")

    def wire_rest(big):
        return [shard_major(k, big[k]).astype(WIRE_DTYPE) for k in _BIG[1:]]

    big_g, small_g, flying = [None] * nl, [None] * nl, {}
    gx, big_g[1], small_g[1] = _layer_bwd(gx, params[1], saved[1], rope, 1)
    flying["l1"] = scatter_start([_win_split(big_g[1]["w_in"], name="w_in_split_l1")] + wire_rest(big_g[1]), "l1")
    p0 = dict(params[0])
    p0["gate"] = p0["gate"] + flying["l1"][4][0, 0]

    def start_rest_l0(big):
        flying["l0_rest"] = scatter_start(wire_rest(big), "l0_rest")
        return flying["l0_rest"][4]

    def start_w_in_l0(g_w_in):
        flying["l0_w_in"] = scatter_start([_win_split(g_w_in, name="w_in_split_l0")], "l0_w_in")
        return flying["l0_w_in"][4]

    gx, big_g[0], small_g[0] = _layer_bwd(gx, p0, saved[0], rope, 0, hook_rest=start_rest_l0,
                                          hook_w_in=start_w_in_l0)

    tile = 8 * LANE
    padded = [(k, nn, -(-nn // tile) * tile) for k, nn in _SMALL]
    spk = jnp.concatenate([jnp.pad(small_g[l][k].reshape(-1), (0, np_ - nn)).reshape(-1, LANE)
                           for l in range(nl) for k, nn, np_ in padded], axis=0)
    s_all = _all_gather([spk], name="gather_small_grads")[0]
    s_rows = sum(np_ for _, _, np_ in padded) // LANE
    s_all = s_all.reshape(N_DEV, nl, s_rows, LANE)
    s_parts = {k: a[..., :nn] for (k, nn, _), a in
               zip(padded, _unpack_rows(s_all, [(np_,) for _, _, np_ in padded]))}

    dmod_all = s_parts["dmod"]
    dmod_cols = lax.dynamic_slice_in_dim(dmod_all, me * ada_cols, ada_cols, axis=2).transpose(1, 0, 2)
    g_ada_w = _ada_bwd(c_all.T, dmod_cols, name="ada_bwd")
    gp = {}
    gp["ada_w"] = g_ada_w[None]
    gp["ada_b"] = dmod_all
    for k in ("norm_g", "q_lat_g", "kv_lat_g", "glu_b", "dw_b", "conv_ln_g", "conv_ln_b", "b_pw"):
        gp[k] = s_parts[k]
    for k in ("q_norm_g", "k_norm_g"):
        t = s_parts[k]
        gp[k] = jnp.concatenate([t[..., :NOPE], t[..., LANE:LANE + ROPE]], axis=-1)
    dw_g = s_parts["dw_w"].reshape(N_DEV, nl, HALO, D_CONV)[:, :, :CONV_K]
    gp["dw_w"] = lax.dynamic_slice_in_dim(dw_g, me * LANE, LANE, axis=3)

    res = {k: _adamw(gp[k], w_loc[k], m_loc[k], v_loc[k], name=f"adamw_{k}") for k in names if k not in _BIG}
    arrived = [None] * nl
    arrived[1] = _exchange_wait(*flying["l1"][:4], gx, True, name="scatter_wait_l1")
    rest0 = _exchange_wait(*flying["l0_rest"][:4], gx, True, name="scatter_wait_l0_rest")
    arrived[0] = _exchange_wait(*flying["l0_w_in"][:4], res["ada_w"][1], True, name="scatter_wait_l0_w_in") + rest0
    for i, k in enumerate(_BIG):
        res[k] = _adamw([arrived[l][i] for l in range(nl)], w_loc[k], m_loc[k], v_loc[k], name=f"adamw_{k}")
    out = [loss, gx[None]]
    for idx in range(4):
        out += [res[k][idx] for k in names]
    return tuple(out)
```

```python
import functools
import math

import jax
import jax.numpy as jnp
from jax import lax
from jax.experimental import pallas as pl
from jax.experimental.pallas import tpu as pltpu

F32 = jnp.float32
MXU_DTYPE = jnp.bfloat16
WIRE_DTYPE = jnp.bfloat16

D_MODEL = 2048
N_LAYERS = 2
N_DEV = 8
N_HEADS = 8
NOPE = 128
ROPE = 64
V_DIM = 128
QK_DIM = NOPE + ROPE
Q_LORA = 512
KV_LORA = 256
D_MLA = N_HEADS * V_DIM
D_CONV = 1024
CONV_K = 31
ROPE_THETA = 10000.0
EPS = 1e-6
LANE = 128
HEAD_PAD = 2 * LANE
HALO = 32

SEG_CI = (0, 2 * D_CONV)
SEG_MG = (2 * D_CONV, D_MLA)
SEG_CG = (2 * D_CONV + D_MLA, D_CONV)
SEG_QL = (2 * D_CONV + D_MLA + D_CONV, Q_LORA)
SEG_KVL = (SEG_QL[0] + Q_LORA, KV_LORA)
SEG_KR = (SEG_KVL[0] + KV_LORA, LANE)
SEG_LAT = (SEG_QL[0], 1024)
IN_PAD = SEG_LAT[0] + SEG_LAT[1]
IN_TILE = IN_PAD // 4
assert SEG_KR[0] + LANE <= IN_PAD and SEG_LAT[0] % SEG_LAT[1] == 0
IN_COLS = Q_LORA + KV_LORA + ROPE + D_MLA + 2 * D_CONV + D_CONV

ADAM_LR = 0.001
ADAM_B1 = 0.9
ADAM_B2 = 0.999
ADAM_EPS = 1e-08
ADAM_WD = 0.01
ADAM_STEP = 10

VMEM_LIMIT = 56 * 1024 * 1024
ATT_T = 512
ROW_T = 256
CONV_T = 128
MESH_ID = pl.DeviceIdType.MESH


def _cp(sem=None):
    kw = dict(vmem_limit_bytes=VMEM_LIMIT)
    if sem is not None:
        kw["dimension_semantics"] = sem
    return pltpu.CompilerParams(**kw)


def _sds(shape, dtype):
    return jax.ShapeDtypeStruct(shape, dtype)


def _silu(x):
    return x * jax.nn.sigmoid(x)


def _dsilu(x):
    s = jax.nn.sigmoid(x)
    return s * (1.0 + x * (1.0 - s))


def _rowspec(t, width, col=0):
    return pl.BlockSpec((t, width), lambda i: (i, col))


def _vecspec(width):
    return pl.BlockSpec((1, width), lambda i: (0, 0))


def _colsum(v):
    return jnp.sum(v, axis=0, keepdims=True)


def _mm(a, b, *, name, ta=False, tb=False, out_dtype=F32, tm=512, tn=512, tk=None, n_outer=False, after=None,
        residual=None):
    if ta:
        kdim, m = a.shape
    else:
        m, kdim = a.shape
    if tb:
        n, k2 = b.shape
    else:
        k2, n = b.shape
    assert kdim == k2, (a.shape, b.shape)
    tm, tn = min(tm, m), min(tn, n)
    tk = kdim if tk is None else min(tk, kdim)
    assert m % tm == 0 and n % tn == 0 and kdim % tk == 0, (m, n, kdim, tm, tn, tk)
    nk = kdim // tk
    dims = (((0 if ta else 1,), (1 if tb else 0,)), ((), ()))

    n_extra = 0 if after is None else 1
    assert residual is None or nk == 1

    def body(a_ref, b_ref, *rest):
        if residual is not None:
            x_ref, gate_ref = rest[:2]
            rest = rest[2:]
        o_ref, scratch = rest[n_extra], rest[n_extra + 1:]
        prod = lax.dot_general(a_ref[...].astype(MXU_DTYPE), b_ref[...].astype(MXU_DTYPE), dims,
                               preferred_element_type=F32)
        if residual is not None:
            o_ref[...] = prod.astype(o_ref.dtype)
            scratch[0][...] = x_ref[...] + gate_ref[...] * prod
        elif nk == 1:
            o_ref[...] = prod.astype(o_ref.dtype)
        else:
            acc = scratch[0]
            k = pl.program_id(2)

            @pl.when(k == 0)
            def _():
                acc[...] = prod

            @pl.when(k > 0)
            def _():
                acc[...] += prod

            @pl.when(k == nk - 1)
            def _():
                o_ref[...] = acc[...].astype(o_ref.dtype)

    if n_outer:
        ij = lambda g0, g1: (g1, g0)
        grid = (n // tn, m // tm, nk)
    else:
        ij = lambda g0, g1: (g0, g1)
        grid = (m // tm, n // tn, nk)

    def a_map(g0, g1, k):
        i, _ = ij(g0, g1)
        return (k, i) if ta else (i, k)

    def b_map(g0, g1, k):
        _, j = ij(g0, g1)
        return (j, k) if tb else (k, j)

    def o_map(g0, g1, k):
        return ij(g0, g1)

    in_specs = [pl.BlockSpec((tk, tm) if ta else (tm, tk), a_map), pl.BlockSpec((tn, tk) if tb else (tk, tn), b_map)]
    operands = [a, b]
    out_specs, out_shape = pl.BlockSpec((tm, tn), o_map), _sds((m, n), out_dtype)
    if residual is not None:
        in_specs += [pl.BlockSpec((tm, tn), o_map), pl.BlockSpec((1, tn), lambda g0, g1, k: (0, ij(g0, g1)[1]))]
        operands += list(residual)
        out_specs, out_shape = [out_specs, pl.BlockSpec((tm, tn), o_map)], [out_shape, _sds((m, n), F32)]
    if after is not None:
        in_specs.append(_ANY)
        operands.append(after)
    return pl.pallas_call(
        body, name=name, grid=grid, in_specs=in_specs, out_specs=out_specs, out_shape=out_shape,
        scratch_shapes=[pltpu.VMEM((tm, tn), F32)] if nk > 1 else [],
        compiler_params=_cp(("parallel", "parallel", "arbitrary")),
    )(*operands)


def _prenorm(x, g, shift, sc1p, *, name):
    s, d = x.shape
    t = min(ROW_T, s)

    def body(x_ref, g_ref, sh_ref, sc_ref, h_ref):
        xv = x_ref[...]
        r = lax.rsqrt(jnp.mean(xv * xv, axis=-1, keepdims=True) + EPS)
        h_ref[...] = ((xv * r) * g_ref[...] * sc_ref[...] + sh_ref[...]).astype(h_ref.dtype)

    return pl.pallas_call(
        body, name=name, grid=(s // t,),
        in_specs=[_rowspec(t, d), _vecspec(d), _vecspec(d), _vecspec(d)],
        out_specs=_rowspec(t, d), out_shape=_sds((s, d), MXU_DTYPE),
        compiler_params=_cp(("parallel",)),
    )(x, g, shift, sc1p)


def _lat_norm(z, g_ql, g_kvl, *, name):
    s = z.shape[0]
    t = min(ROW_T, s)

    def body(ql_ref, kvl_ref, gq_ref, gk_ref, qn_ref, kn_ref):
        for src, g_ref, dst in ((ql_ref, gq_ref, qn_ref), (kvl_ref, gk_ref, kn_ref)):
            v = src[...]
            r = lax.rsqrt(jnp.mean(v * v, axis=-1, keepdims=True) + EPS)
            dst[...] = ((v * r) * g_ref[...]).astype(dst.dtype)

    return pl.pallas_call(
        body, name=name, grid=(s // t,),
        in_specs=[_rowspec(t, Q_LORA, SEG_QL[0] // Q_LORA), _rowspec(t, KV_LORA, SEG_KVL[0] // KV_LORA),
                  _vecspec(Q_LORA), _vecspec(KV_LORA)],
        out_specs=[_rowspec(t, Q_LORA), _rowspec(t, KV_LORA)],
        out_shape=[_sds((s, Q_LORA), MXU_DTYPE), _sds((s, KV_LORA), MXU_DTYPE)],
        compiler_params=_cp(("parallel",)),
    )(z, z, g_ql, g_kvl)


def _rope_fwd(r, c_t, s1_t, s2_t):
    return r * c_t + pltpu.roll(r, LANE - ROPE // 2, 1) * s1_t + pltpu.roll(r, ROPE // 2, 1) * s2_t


def _rope_bwd(d, c_t, s1_t, s2_t):
    return d * c_t + pltpu.roll(d * s1_t, ROPE // 2, 1) + pltpu.roll(d * s2_t, LANE - ROPE // 2, 1)


def _lanesum(v):
    return jnp.sum(v, axis=-1, keepdims=True)


def _qk_prep(q_raw, kv, z, c_t, s1_t, s2_t, gqn, gqr, gkn, gkr, *, name):
    s = q_raw.shape[0]
    t = min(ROW_T, s)
    scale = 1.0 / math.sqrt(QK_DIM)

    def body(q_ref, kv_ref, kr_ref, c_ref, s1_ref, s2_ref, gqn_ref, gqr_ref, gkn_ref, gkr_ref,
             qf_ref, kf_ref, vf_ref):
        c_v, s1_v, s2_v = c_ref[...], s1_ref[...], s2_ref[...]
        kr = kr_ref[...]
        kr_ss = _lanesum(kr * kr)
        for h in range(N_HEADS):
            n = q_ref[:, h * LANE:(h + 1) * LANE]
            r = q_ref[:, N_HEADS * LANE + h * LANE:N_HEADS * LANE + (h + 1) * LANE]
            rs = lax.rsqrt((_lanesum(n * n) + _lanesum(r * r)) * (1.0 / QK_DIM) + EPS)
            qf_ref[h, :, 0:LANE] = (((n * rs) * gqn_ref[...]) * scale).astype(qf_ref.dtype)
            rr = _rope_fwd((r * rs) * gqr_ref[...], c_v, s1_v, s2_v)
            qf_ref[h, :, LANE:HEAD_PAD] = (rr * scale).astype(qf_ref.dtype)

            n = kv_ref[:, h * 2 * LANE:h * 2 * LANE + LANE]
            rs = lax.rsqrt((_lanesum(n * n) + kr_ss) * (1.0 / QK_DIM) + EPS)
            kf_ref[h, :, 0:LANE] = ((n * rs) * gkn_ref[...]).astype(kf_ref.dtype)
            kf_ref[h, :, LANE:HEAD_PAD] = _rope_fwd((kr * rs) * gkr_ref[...], c_v, s1_v, s2_v).astype(kf_ref.dtype)
            vf_ref[h, :, 0:V_DIM] = kv_ref[:, h * 2 * LANE + LANE:(h + 1) * 2 * LANE].astype(vf_ref.dtype)
            vf_ref[h, :, V_DIM:] = jnp.ones((t, V_DIM), vf_ref.dtype)

    hspec = lambda w: pl.BlockSpec((N_HEADS, t, w), lambda i: (0, i, 0))
    return pl.pallas_call(
        body, name=name, grid=(s // t,),
        in_specs=[_rowspec(t, 2 * N_HEADS * LANE), _rowspec(t, 2 * N_HEADS * LANE),
                  _rowspec(t, LANE, SEG_KR[0] // LANE),
                  _rowspec(t, LANE), _rowspec(t, LANE), _rowspec(t, LANE),
                  _vecspec(LANE), _vecspec(LANE), _vecspec(LANE), _vecspec(LANE)],
        out_specs=[hspec(HEAD_PAD), hspec(HEAD_PAD), hspec(2 * V_DIM)],
        out_shape=[_sds((N_HEADS, s, HEAD_PAD), MXU_DTYPE), _sds((N_HEADS, s, HEAD_PAD), MXU_DTYPE),
                   _sds((N_HEADS, s, 2 * V_DIM), MXU_DTYPE)],
        compiler_params=_cp(("parallel",)),
    )(q_raw, kv, z, c_t, s1_t, s2_t, gqn, gqr, gkn, gkr)


def _causal_mask(t):
    row = lax.broadcasted_iota(jnp.int32, (t, t), 0)
    col = lax.broadcasted_iota(jnp.int32, (t, t), 1)
    return col <= row


NEG = -1e30


def _flash_fwd(qf, kf, va, *, name):
    nh, s, dk = qf.shape
    dv = va.shape[-1] // 2
    t = min(ATT_T, s)
    n = s // t
    assert dv == LANE and t % LANE == 0

    def body(q_ref, k_ref, v_ref, o_ref, lse_ref, m_s, acc_s, s_buf):
        i = pl.program_id(1)
        m_s[...] = jnp.full(m_s.shape, NEG, F32)
        acc_s[...] = jnp.zeros(acc_s.shape, F32)
        q = q_ref[0]

        def rows_of(j):
            return pl.ds(pl.multiple_of(j * t, t), t)

        def scores(j):
            return lax.dot_general(q, k_ref[0, rows_of(j), :], (((1,), (1,)), ((), ())), preferred_element_type=F32)

        def consume(j, slot, masked):
            sc = s_buf[slot]
            if masked:
                sc = jnp.where(_causal_mask(t), sc, NEG)
            m_prev = m_s[...]
            m_new = jnp.maximum(m_prev, jnp.max(sc, axis=-1, keepdims=True))
            alpha = jnp.exp(m_prev - m_new)
            p = jnp.exp(sc - jnp.tile(m_new, (1, t // LANE)))
            acc_s[...] = jnp.tile(alpha, (1, 2)) * acc_s[...] + jnp.dot(
                p.astype(MXU_DTYPE), v_ref[0, rows_of(j), :], preferred_element_type=F32)
            m_s[...] = m_new

        s_buf[0] = scores(0)

        def pair(a, carry):
            s_buf[1] = scores(2 * a + 1)
            consume(2 * a, 0, False)
            s_buf[0] = scores(2 * a + 2)
            consume(2 * a + 1, 1, False)
            return carry

        lax.fori_loop(0, i // 2, pair, 0)

        @pl.when(i % 2 == 1)
        def _():
            s_buf[1] = scores(i)
            consume(i - 1, 0, False)
            consume(i, 1, True)

        @pl.when(i % 2 == 0)
        def _():
            consume(i, 0, True)

        den = acc_s[:, dv:]
        o_ref[...] = acc_s[:, :dv] / den
        lse_ref[0] = m_s[...] + jnp.log(den)

    return pl.pallas_call(
        body, name=name, grid=(nh, n),
        in_specs=[pl.BlockSpec((1, t, dk), lambda h, i: (h, i, 0)),
                  pl.BlockSpec((1, s, dk), lambda h, i: (h, 0, 0)),
                  pl.BlockSpec((1, s, 2 * dv), lambda h, i: (h, 0, 0))],
        out_specs=[pl.BlockSpec((t, dv), lambda h, i: (i, h)),
                   pl.BlockSpec((1, t, LANE), lambda h, i: (h, i, 0))],
        out_shape=[_sds((s, nh * dv), F32), _sds((nh, s, LANE), F32)],
        scratch_shapes=[pltpu.VMEM((t, LANE), F32), pltpu.VMEM((t, 2 * dv), F32), pltpu.VMEM((2, t, t), F32)],
        compiler_params=_cp(("parallel", "arbitrary")),
    )(qf, kf, va)


def _shifted_copies(ext_ref):
    rows = ext_ref.shape[1] - 8
    for s in range(1, 8):
        ext_ref[s, 0:rows, :] = ext_ref[0, s:s + rows, :]


def _window(ext_ref, off, t_rows, lane0, lanes):
    return ext_ref[off % 8, pl.ds(off - off % 8, t_rows), lane0:lane0 + lanes]


def _dw_taps(ext_ref, w_ref, row0, t_rows, lane0, lanes, first_off):
    acc = None
    for k in range(CONV_K):
        term = w_ref[k:k + 1, lane0:lane0 + lanes] * _window(ext_ref, row0 + first_off + k, t_rows, lane0, lanes)
        acc = term if acc is None else acc + term
    return acc


CONV_RC = 32
CONV_LC = 256


def _conv_fwd(z, glu_b, dw_w, dw_b, ln_g, ln_b, *, name):
    s = z.shape[0]
    t = min(CONV_T, s)
    c2 = 2 * D_CONV
    hb = t // HALO

    def body(zm_ref, zh_ref, gb_ref, w_ref, wb_ref, g_ref, b_ref, u1_ref, u3_ref, ext):
        i = pl.program_id(0)

        def glu(zv):
            ci = zv + gb_ref[...]
            return ci[:, :D_CONV] * jax.nn.sigmoid(ci[:, D_CONV:])

        ext[0, HALO:, :] = glu(zm_ref[...])
        ext[0, 0:HALO, :] = jnp.where(i > 0, glu(zh_ref[...]), 0.0)
        _shifted_copies(ext)
        for rc in range(0, t, CONV_RC):
            for lc in range(0, D_CONV, CONV_LC):
                acc = _dw_taps(ext, w_ref, rc, CONV_RC, lc, CONV_LC, HALO - (CONV_K - 1))
                u1_ref[rc:rc + CONV_RC, lc:lc + CONV_LC] = acc + wb_ref[:, lc:lc + CONV_LC]
        u1 = u1_ref[...]
        mu = jnp.mean(u1, axis=-1, keepdims=True)
        cen = u1 - mu
        var = jnp.mean(cen * cen, axis=-1, keepdims=True)
        u2 = (cen * lax.rsqrt(var + EPS)) * g_ref[...] + b_ref[...]
        u3_ref[...] = _silu(u2).astype(u3_ref.dtype)

    return pl.pallas_call(
        body, name=name, grid=(s // t,),
        in_specs=[_rowspec(t, c2), pl.BlockSpec((HALO, c2), lambda i: (jnp.maximum(i * hb - 1, 0), 0)),
                  _vecspec(c2), pl.BlockSpec((HALO, D_CONV), lambda i: (0, 0)), _vecspec(D_CONV),
                  _vecspec(D_CONV), _vecspec(D_CONV)],
        out_specs=[_rowspec(t, D_CONV), _rowspec(t, D_CONV)],
        out_shape=[_sds((s, D_CONV), F32), _sds((s, D_CONV), MXU_DTYPE)],
        scratch_shapes=[pltpu.VMEM((8, t + HALO, D_CONV), F32)],
        compiler_params=_cp(("parallel",)),
    )(z, z, glu_b, dw_w, dw_b, ln_g, ln_b)


def _gate_cat(o, z, u4m, b_pw, *, name):
    s = o.shape[0]
    t = min(ROW_T, s)

    def body(o_ref, mg_ref, u4_ref, cg_ref, b_ref, cat_ref):
        cat_ref[:, :D_MLA] = (o_ref[...] * _silu(mg_ref[...])).astype(cat_ref.dtype)
        cat_ref[:, D_MLA:] = ((u4_ref[...] + b_ref[...]) * _silu(cg_ref[...])).astype(cat_ref.dtype)

    return pl.pallas_call(
        body, name=name, grid=(s // t,),
        in_specs=[_rowspec(t, D_MLA), _rowspec(t, D_MLA, SEG_MG[0] // D_MLA), _rowspec(t, D_CONV),
                  _rowspec(t, D_CONV, SEG_CG[0] // D_CONV), _vecspec(D_CONV)],
        out_specs=_rowspec(t, D_MLA + D_CONV), out_shape=_sds((s, D_MLA + D_CONV), MXU_DTYPE),
        compiler_params=_cp(("parallel",)),
    )(o, z, u4m, z, b_pw)


def _loss_head(xf, target, *, name):
    s, d = xf.shape
    t = min(ROW_T, s)

    def body(x_ref, t_ref, gx_ref, loss_ref):
        @pl.when(pl.program_id(0) == 0)
        def _():
            loss_ref[...] = jnp.zeros(loss_ref.shape, F32)

        err = x_ref[...] - t_ref[...]
        gx_ref[...] = err * (1.0 / d)
        loss_ref[...] += 0.5 * jnp.sum(_lanesum(err * err) * (1.0 / d), axis=0, keepdims=True)

    return pl.pallas_call(
        body, name=name, grid=(s // t,),
        in_specs=[_rowspec(t, d), _rowspec(t, d)],
        out_specs=[_rowspec(t, d), pl.BlockSpec((1, 1), lambda i: (0, 0))],
        out_shape=[_sds((s, d), F32), _sds((1, 1), F32)],
        compiler_params=_cp(("arbitrary",)),
    )(xf, target)


def _acc_init(refs):
    @pl.when(pl.program_id(0) == 0)
    def _():
        for r in refs:
            r[...] = jnp.zeros(r.shape, r.dtype)


def _out_bwd(gxo, y, gate, *, name):
    s, d = gxo.shape
    t = min(ROW_T, s)

    def body(g_ref, y_ref, gate_ref, dy_ref, dgate_ref):
        _acc_init([dgate_ref])
        gv = g_ref[...]
        dy_ref[...] = (gv * gate_ref[...]).astype(dy_ref.dtype)
        dgate_ref[...] += _colsum(gv * y_ref[...])

    return pl.pallas_call(
        body, name=name, grid=(s // t,),
        in_specs=[_rowspec(t, d), _rowspec(t, d), _vecspec(d)],
        out_specs=[_rowspec(t, d), _vecspec(d)],
        out_shape=[_sds((s, d), MXU_DTYPE), _sds((1, d), F32)],
        compiler_params=_cp(("arbitrary",)),
    )(gxo, y, gate)


def _gate_bwd(dcat, o, z, u4m, b_pw, *, name):
    s = o.shape[0]
    t = min(ROW_T, s)
    gates = D_MLA + D_CONV
    assert SEG_CG[0] == SEG_MG[0] + D_MLA and SEG_MG[0] % gates == 0

    def body(dm_ref, dc_ref, o_ref, mg_ref, u4_ref, cg_ref, b_ref,
             do_ref, delta_ref, du4_ref, gb_ref, dz_ref):
        _acc_init([gb_ref])
        dm, ov, mg = dm_ref[...], o_ref[...], mg_ref[...]
        do = dm * _silu(mg)
        do_ref[...] = do.astype(do_ref.dtype)
        dz_ref[:, :D_MLA] = (dm * ov * _dsilu(mg)).astype(dz_ref.dtype)
        prod = do * ov
        for h in range(N_HEADS):
            delta_ref[h] = _lanesum(prod[:, h * V_DIM:(h + 1) * V_DIM])
        dc, cg = dc_ref[...], cg_ref[...]
        du4 = dc * _silu(cg)
        du4_ref[...] = du4.astype(du4_ref.dtype)
        dz_ref[:, D_MLA:] = (dc * (u4_ref[...] + b_ref[...]) * _dsilu(cg)).astype(dz_ref.dtype)
        gb_ref[...] += _colsum(du4)

    return pl.pallas_call(
        body, name=name, grid=(s // t,),
        in_specs=[_rowspec(t, D_MLA, 0), _rowspec(t, D_CONV, 1), _rowspec(t, D_MLA),
                  _rowspec(t, D_MLA, SEG_MG[0] // D_MLA), _rowspec(t, D_CONV),
                  _rowspec(t, D_CONV, SEG_CG[0] // D_CONV), _vecspec(D_CONV)],
        out_specs=[_rowspec(t, D_MLA), pl.BlockSpec((N_HEADS, t, 1), lambda i: (0, i, 0)),
                   _rowspec(t, D_CONV), _vecspec(D_CONV), _rowspec(t, gates, SEG_MG[0] // gates)],
        out_shape=[_sds((s, D_MLA), MXU_DTYPE), _sds((N_HEADS, s, 1), F32),
                   _sds((s, D_CONV), MXU_DTYPE), _sds((1, D_CONV), F32), _sds((s, IN_PAD), MXU_DTYPE)],
        compiler_params=_cp(("arbitrary",)),
    )(dcat, dcat, o, z, u4m, z, b_pw)


def _conv_bwd(du3, u1, z, dz, glu_b, dw_w, ln_g, ln_b, *, name):
    s = z.shape[0]
    t = min(CONV_T, s)
    c2 = 2 * D_CONV
    hb = t // HALO
    n_blk = s // t
    last_halo = s // HALO - 1

    def body(d3m_ref, d3h_ref, u1m_ref, u1h_ref, zm_ref, zh_ref, gb_ref, w_ref, g_ref, b_ref, dz_in_ref,
             dci_ref, gg_ref, gbn_ref, gwb_ref, ggb_ref, gw_ref, dext, uext, du0_s, gw_acc):
        i = pl.program_id(0)
        _acc_init([gg_ref, gbn_ref, gwb_ref, ggb_ref, gw_acc])

        def ln_bwd(d3, u1v):
            mu = jnp.mean(u1v, axis=-1, keepdims=True)
            cen = u1v - mu
            rstd = lax.rsqrt(jnp.mean(cen * cen, axis=-1, keepdims=True) + EPS)
            uh = cen * rstd
            d2 = d3 * _dsilu(uh * g_ref[...] + b_ref[...])
            dh = d2 * g_ref[...]
            d1 = rstd * (dh - jnp.mean(dh, axis=-1, keepdims=True) - uh * jnp.mean(dh * uh, axis=-1, keepdims=True))
            return d1, d2, uh

        d1, d2, uh = ln_bwd(d3m_ref[...], u1m_ref[...])
        gg_ref[...] += _colsum(d2 * uh)
        gbn_ref[...] += _colsum(d2)
        gwb_ref[...] += _colsum(d1)
        dext[0, 0:t, :] = d1
        d1h, _, _ = ln_bwd(d3h_ref[...], u1h_ref[...])
        dext[0, t:, :] = jnp.where(i < n_blk - 1, d1h, 0.0)
        _shifted_copies(dext)

        def glu_parts(zv):
            ci = zv + gb_ref[...]
            return ci[:, :D_CONV], jax.nn.sigmoid(ci[:, D_CONV:])

        val, sg = glu_parts(zm_ref[...])
        uext[0, HALO:, :] = val * sg
        valh, sgh = glu_parts(zh_ref[...])
        uext[0, 0:HALO, :] = jnp.where(i > 0, valh * sgh, 0.0)
        _shifted_copies(uext)

        for rc in range(0, t, CONV_RC):
            for lc in range(0, D_CONV, CONV_LC):
                acc = None
                dchunk = dext[0, rc:rc + CONV_RC, lc:lc + CONV_LC]
                for k in range(CONV_K):
                    term = w_ref[k:k + 1, lc:lc + CONV_LC] * _window(dext, rc + (CONV_K - 1) - k, CONV_RC, lc, CONV_LC)
                    acc = term if acc is None else acc + term
                    pr = dchunk * _window(uext, rc + HALO - (CONV_K - 1) + k, CONV_RC, lc, CONV_LC)
                    part = pr[0:8]
                    for r8 in range(8, CONV_RC, 8):
                        part = part + pr[r8:r8 + 8]
                    gw_acc[k, :, lc:lc + CONV_LC] += part
                du0_s[rc:rc + CONV_RC, lc:lc + CONV_LC] = acc

        du0 = du0_s[...]
        dval = du0 * sg
        dgt = du0 * val * sg * (1.0 - sg)
        dci_ref[:, :D_CONV] = dval.astype(dci_ref.dtype)
        dci_ref[:, D_CONV:] = dgt.astype(dci_ref.dtype)
        ggb_ref[:, :D_CONV] += _colsum(dval)
        ggb_ref[:, D_CONV:] += _colsum(dgt)

        @pl.when(i == n_blk - 1)
        def _():
            gw_ref[...] = jnp.sum(gw_acc[...], axis=1)

    halo_next = lambda w: pl.BlockSpec((HALO, w), lambda i: (jnp.minimum((i + 1) * hb, last_halo), 0))
    return pl.pallas_call(
        body, name=name, grid=(n_blk,),
        in_specs=[_rowspec(t, D_CONV), halo_next(D_CONV), _rowspec(t, D_CONV), halo_next(D_CONV),
                  _rowspec(t, c2), pl.BlockSpec((HALO, c2), lambda i: (jnp.maximum(i * hb - 1, 0), 0)),
                  _vecspec(c2), pl.BlockSpec((HALO, D_CONV), lambda i: (0, 0)), _vecspec(D_CONV), _vecspec(D_CONV),
                  _ANY],
        out_specs=[_rowspec(t, c2, SEG_CI[0] // c2), _vecspec(D_CONV), _vecspec(D_CONV), _vecspec(D_CONV),
                   _vecspec(c2), pl.BlockSpec((HALO, D_CONV), lambda i: (0, 0))],
        out_shape=[_sds(dz.shape, dz.dtype), _sds((1, D_CONV), F32), _sds((1, D_CONV), F32), _sds((1, D_CONV), F32),
                   _sds((1, c2), F32), _sds((HALO, D_CONV), F32)],
        scratch_shapes=[pltpu.VMEM((8, t + HALO, D_CONV), F32), pltpu.VMEM((8, t + HALO, D_CONV), F32),
                        pltpu.VMEM((t, D_CONV), F32), pltpu.VMEM((HALO, 8, D_CONV), F32)],
        input_output_aliases={10: 0},
        compiler_params=_cp(("arbitrary",)),
    )(du3, du3, u1, u1, z, z, glu_b, dw_w, ln_g, ln_b, dz)


def _flash_bwd(qf, kf, va, do, lse_t, delta_t, *, name):
    nh, s, dk = qf.shape
    dv = va.shape[-1] // 2
    t = min(ATT_T, s)
    n = s // t
    nt = (((1,), (1,)), ((), ()))
    tn = (((0,), (0,)), ((), ()))

    def body(q_ref, do_ref, lse_ref, dl_ref, k_ref, v_ref, dq_ref, dk_ref, dv_ref,
             dk_s, dv_s, st_buf, dpt_buf):
        j = pl.program_id(1)

        @pl.when(j == 0)
        def _():
            dq_ref[...] = jnp.zeros(dq_ref.shape, F32)

        dk_s[...] = jnp.zeros(dk_s.shape, F32)
        dv_s[...] = jnp.zeros(dv_s.shape, F32)
        k, v = k_ref[0], v_ref[0]
        n_un = n - 1 - j

        def rows_of(b):
            return pl.ds(pl.multiple_of((n - 1 - b) * t, t), t)

        def produce(b, slot):
            rows = rows_of(b)
            st_buf[slot] = lax.dot_general(k, q_ref[0, rows, :], nt, preferred_element_type=F32)
            dpt_buf[slot] = lax.dot_general(v, do_ref[rows, :], nt, preferred_element_type=F32)

        def consume(b, slot, masked):
            i = n - 1 - b
            rows = rows_of(b)
            q, dov = q_ref[0, rows, :], do_ref[rows, :]
            pt = jnp.exp(st_buf[slot] - lse_ref[0, i])
            if masked:
                key = lax.broadcasted_iota(jnp.int32, (t, t), 0)
                qry = lax.broadcasted_iota(jnp.int32, (t, t), 1)
                pt = jnp.where(key <= qry, pt, 0.0)
            dv_s[...] += jnp.dot(pt.astype(MXU_DTYPE), dov, preferred_element_type=F32)
            dst = (pt * (dpt_buf[slot] - dl_ref[0, i])).astype(MXU_DTYPE)
            dk_s[...] += jnp.dot(dst, q, preferred_element_type=F32)
            dq_ref[0, rows, :] += lax.dot_general(dst, k, tn, preferred_element_type=F32)

        produce(0, 0)

        def pair(a, carry):
            produce(2 * a + 1, 1)
            consume(2 * a, 0, False)
            produce(2 * a + 2, 0)
            consume(2 * a + 1, 1, False)
            return carry

        lax.fori_loop(0, n_un // 2, pair, 0)

        @pl.when(n_un % 2 == 1)
        def _():
            produce(n_un, 1)
            consume(n_un - 1, 0, False)
            consume(n_un, 1, True)

        @pl.when(n_un % 2 == 0)
        def _():
            consume(n_un, 0, True)

        dk_ref[0] = dk_s[...]
        dv_ref[0] = dv_s[...]

    head = lambda h, j: (h, 0, 0)
    rowv = pl.BlockSpec((1, n, 1, t), lambda h, j: (h, 0, 0, 0))
    return pl.pallas_call(
        body, name=name, grid=(nh, n),
        in_specs=[pl.BlockSpec((1, s, dk), head),
                  pl.BlockSpec((s, dv), lambda h, j: (0, h)),
                  rowv, rowv,
                  pl.BlockSpec((1, t, dk), lambda h, j: (h, j, 0)),
                  pl.BlockSpec((1, t, dv), lambda h, j: (h, j, 0))],
        out_specs=[pl.BlockSpec((1, s, dk), head),
                   pl.BlockSpec((1, t, dk), lambda h, j: (h, j, 0)),
                   pl.BlockSpec((1, t, dv), lambda h, j: (h, j, 0))],
        out_shape=[_sds((nh, s, dk), F32), _sds((nh, s, dk), F32), _sds((nh, s, dv), F32)],
        scratch_shapes=[pltpu.VMEM((t, dk), F32), pltpu.VMEM((t, dv), F32),
                        pltpu.VMEM((2, t, t), F32), pltpu.VMEM((2, t, t), F32)],
        compiler_params=_cp(("parallel", "arbitrary")),
    )(qf, do, lse_t, delta_t, kf, va)


def _qk_bwd(dqf, dkf, dvf, q_raw, kv, z, c_t, s1_t, s2_t, gqn, gqr, gkn, gkr, *, name):
    s = q_raw.shape[0]
    t = min(ROW_T, s)
    scale = 1.0 / math.sqrt(QK_DIM)

    def body(dq_ref, dk_ref, dv_ref, q_ref, kv_ref, kr_ref, c_ref, s1_ref, s2_ref,
             gqn_ref, gqr_ref, gkn_ref, gkr_ref, dqr_ref, dkv_ref, dkr_ref, ggq_ref, ggk_ref):
        _acc_init([ggq_ref, ggk_ref])
        c_v, s1_v, s2_v = c_ref[...], s1_ref[...], s2_ref[...]
        kr = kr_ref[...]
        kr_ss = _lanesum(kr * kr)
        dkr = jnp.zeros(kr.shape, F32)
        ggq_n = ggq_r = ggk_n = ggk_r = jnp.zeros((1, LANE), F32)

        def norm_bwd(n, r, rs, dyn, dyr, gn, gr):
            nh_, rh_ = n * rs, r * rs
            dnh, drh = dyn * gn, dyr * gr
            dot = (_lanesum(dnh * nh_) + _lanesum(drh * rh_)) * (1.0 / QK_DIM)
            return rs * (dnh - nh_ * dot), rs * (drh - rh_ * dot), _colsum(dyn * nh_), _colsum(dyr * rh_)

        for h in range(N_HEADS):
            n = q_ref[:, h * LANE:(h + 1) * LANE]
            r = q_ref[:, N_HEADS * LANE + h * LANE:N_HEADS * LANE + (h + 1) * LANE]
            rs = lax.rsqrt((_lanesum(n * n) + _lanesum(r * r)) * (1.0 / QK_DIM) + EPS)
            dyn = dq_ref[h, :, 0:LANE] * scale
            dyr = _rope_bwd(dq_ref[h, :, LANE:HEAD_PAD] * scale, c_v, s1_v, s2_v)
            dn, dr, g_n, g_r = norm_bwd(n, r, rs, dyn, dyr, gqn_ref[...], gqr_ref[...])
            dqr_ref[:, h * LANE:(h + 1) * LANE] = dn.astype(dqr_ref.dtype)
            dqr_ref[:, N_HEADS * LANE + h * LANE:N_HEADS * LANE + (h + 1) * LANE] = dr.astype(dqr_ref.dtype)
            ggq_n, ggq_r = ggq_n + g_n, ggq_r + g_r

            n = kv_ref[:, h * 2 * LANE:h * 2 * LANE + LANE]
            rs = lax.rsqrt((_lanesum(n * n) + kr_ss) * (1.0 / QK_DIM) + EPS)
            dyn = dk_ref[h, :, 0:LANE]
            dyr = _rope_bwd(dk_ref[h, :, LANE:HEAD_PAD], c_v, s1_v, s2_v)
            dn, dr, g_n, g_r = norm_bwd(n, kr, rs, dyn, dyr, gkn_ref[...], gkr_ref[...])
            dkv_ref[:, h * 2 * LANE:h * 2 * LANE + LANE] = dn.astype(dkv_ref.dtype)
            dkv_ref[:, h * 2 * LANE + LANE:(h + 1) * 2 * LANE] = dv_ref[h].astype(dkv_ref.dtype)
            dkr = dkr + dr
            ggk_n, ggk_r = ggk_n + g_n, ggk_r + g_r

        dkr_ref[...] = dkr.astype(dkr_ref.dtype)
        ggq_ref[:, 0:LANE] += ggq_n
        ggq_ref[:, LANE:] += ggq_r
        ggk_ref[:, 0:LANE] += ggk_n
        ggk_ref[:, LANE:] += ggk_r

    hspec = lambda w: pl.BlockSpec((N_HEADS, t, w), lambda i: (0, i, 0))
    wide = 2 * N_HEADS * LANE
    return pl.pallas_call(
        body, name=name, grid=(s // t,),
        in_specs=[hspec(HEAD_PAD), hspec(HEAD_PAD), hspec(V_DIM), _rowspec(t, wide), _rowspec(t, wide),
                  _rowspec(t, LANE, SEG_KR[0] // LANE), _rowspec(t, LANE), _rowspec(t, LANE), _rowspec(t, LANE),
                  _vecspec(LANE), _vecspec(LANE), _vecspec(LANE), _vecspec(LANE)],
        out_specs=[_rowspec(t, wide), _rowspec(t, wide), _rowspec(t, LANE), _vecspec(2 * LANE), _vecspec(2 * LANE)],
        out_shape=[_sds((s, wide), MXU_DTYPE), _sds((s, wide), MXU_DTYPE), _sds((s, LANE), MXU_DTYPE),
                   _sds((1, 2 * LANE), F32), _sds((1, 2 * LANE), F32)],
        compiler_params=_cp(("arbitrary",)),
    )(dqf, dkf, dvf, q_raw, kv, z, c_t, s1_t, s2_t, gqn, gqr, gkn, gkr)


def _lat_bwd(dqn, dkn, dkr, z, dz, g_ql, g_kvl, *, name):
    s = z.shape[0]
    t = min(ROW_T, s)
    o_ql, o_kvl, o_kr = (seg[0] - SEG_LAT[0] for seg in (SEG_QL, SEG_KVL, SEG_KR))

    def body(dq_ref, dk_ref, dkr_ref, ql_ref, kvl_ref, gq_ref, gk_ref, dz_in_ref, dz_ref, ggq_ref, ggk_ref):
        _acc_init([ggq_ref, ggk_ref])
        for d_ref, src, g_ref, off, gg_ref in ((dq_ref, ql_ref, gq_ref, o_ql, ggq_ref),
                                               (dk_ref, kvl_ref, gk_ref, o_kvl, ggk_ref)):
            v, dy = src[...], d_ref[...]
            r = lax.rsqrt(jnp.mean(v * v, axis=-1, keepdims=True) + EPS)
            vh = v * r
            dvh = dy * g_ref[...]
            dz_ref[:, off:off + v.shape[1]] = (
                r * (dvh - vh * jnp.mean(dvh * vh, axis=-1, keepdims=True))).astype(dz_ref.dtype)
            gg_ref[...] += _colsum(dy * vh)
        dz_ref[:, o_kr:o_kr + LANE] = dkr_ref[...]
        dz_ref[:, o_kr + LANE:] = jnp.zeros((t, SEG_LAT[1] - o_kr - LANE), dz_ref.dtype)

    return pl.pallas_call(
        body, name=name, grid=(s // t,),
        in_specs=[_rowspec(t, Q_LORA), _rowspec(t, KV_LORA), _rowspec(t, LANE),
                  _rowspec(t, Q_LORA, SEG_QL[0] // Q_LORA), _rowspec(t, KV_LORA, SEG_KVL[0] // KV_LORA),
                  _vecspec(Q_LORA), _vecspec(KV_LORA), _ANY],
        out_specs=[_rowspec(t, SEG_LAT[1], SEG_LAT[0] // SEG_LAT[1]), _vecspec(Q_LORA), _vecspec(KV_LORA)],
        out_shape=[_sds(dz.shape, dz.dtype), _sds((1, Q_LORA), F32), _sds((1, KV_LORA), F32)],
        input_output_aliases={7: 0},
        compiler_params=_cp(("arbitrary",)),
    )(dqn, dkn, dkr, z, z, g_ql, g_kvl, dz)


def _prenorm_bwd(dh, x, gxo, g, sc1p, *, name):
    s, d = x.shape
    t = min(ROW_T, s)

    def body(dh_ref, x_ref, gx_ref, g_ref, sc_ref, dx_ref, dsh_ref, dsc_ref, gg_ref):
        _acc_init([dsh_ref, dsc_ref, gg_ref])
        xv, dhv = x_ref[...], dh_ref[...]
        r = lax.rsqrt(jnp.mean(xv * xv, axis=-1, keepdims=True) + EPS)
        xn = xv * r
        dsh_ref[...] += _colsum(dhv)
        dsc_ref[...] += _colsum(dhv * (xn * g_ref[...]))
        dm = dhv * sc_ref[...]
        gg_ref[...] += _colsum(dm * xn)
        dxn = dm * g_ref[...]
        dx_ref[...] = gx_ref[...] + r * (dxn - xn * jnp.mean(dxn * xn, axis=-1, keepdims=True))

    return pl.pallas_call(
        body, name=name, grid=(s // t,),
        in_specs=[_rowspec(t, d), _rowspec(t, d), _rowspec(t, d), _vecspec(d), _vecspec(d)],
        out_specs=[_rowspec(t, d), _vecspec(d), _vecspec(d), _vecspec(d)],
        out_shape=[_sds((s, d), F32), _sds((1, d), F32), _sds((1, d), F32), _sds((1, d), F32)],
        compiler_params=_cp(("arbitrary",)),
    )(dh, x, gxo, g, sc1p)


def _ada_fwd(c_all, ada_w, ada_b_cols, *, name):
    nl, d, cols = ada_w.shape

    def body(c_ref, w_ref, b_ref, o_ref):
        ca = _silu(c_ref[...]).astype(MXU_DTYPE)
        o_ref[0] = jnp.dot(ca, w_ref[0].astype(MXU_DTYPE), preferred_element_type=F32) + b_ref[0]

    return pl.pallas_call(
        body, name=name, grid=(nl,),
        in_specs=[pl.BlockSpec((N_DEV, d), lambda l: (0, 0)), pl.BlockSpec((1, d, cols), lambda l: (l, 0, 0)),
                  pl.BlockSpec((1, 1, cols), lambda l: (l, 0, 0))],
        out_specs=pl.BlockSpec((1, N_DEV, cols), lambda l: (l, 0, 0)),
        out_shape=_sds((nl, N_DEV, cols), F32),
        compiler_params=_cp(("parallel",)),
    )(c_all, ada_w, ada_b_cols)


def _ada_bwd(c_all_t, dmod_cols, *, name):
    nl, _, cols = dmod_cols.shape
    d = c_all_t.shape[0]

    def body(c_ref, dm_ref, o_ref):
        ca = _silu(c_ref[...]).astype(MXU_DTYPE)
        o_ref[0] = jnp.dot(ca, dm_ref[0].astype(MXU_DTYPE), preferred_element_type=F32)

    return pl.pallas_call(
        body, name=name, grid=(nl,),
        in_specs=[pl.BlockSpec((d, N_DEV), lambda l: (0, 0)), pl.BlockSpec((1, N_DEV, cols), lambda l: (l, 0, 0))],
        out_specs=pl.BlockSpec((1, d, cols), lambda l: (l, 0, 0)),
        out_shape=_sds((nl, d, cols), F32),
        compiler_params=_cp(("parallel",)),
    )(c_all_t, dmod_cols)


def _adamw(gparts, w, m, v, *, name):
    shape = w.shape
    cols = shape[-1]
    per_layer = isinstance(gparts, (list, tuple))
    nl = shape[0] if per_layer else 1
    rows = w.size // cols // nl
    glist = list(gparts) if per_layer else [gparts]
    npart = glist[0].shape[0]
    glist = [g.reshape(npart, rows, cols) for g in glist]
    w3, m3, v3 = (a.reshape(nl, rows, cols) for a in (w, m, v))
    budget = 2 * 1024 * 1024
    fits = [t for t in (256, 128, 64, 32, 16, 8)
            if rows % t == 0 and npart * t * cols * glist[0].dtype.itemsize <= budget]
    t = fits[0] if fits else rows
    nb = rows // t

    def body(*refs):
        g_refs = refs[:nl]
        w_ref, m_ref, v_ref, go_ref, d_ref, mo_ref, vo_ref, g_s = refs[nl:]
        layer = pl.program_id(0)
        for l in range(nl):
            @pl.when(layer == l)
            def _(l=l):
                g = g_refs[l][0].astype(F32)
                for p in range(1, npart):
                    g = g + g_refs[l][p].astype(F32)
                g_s[...] = g

        g = g_s[...]
        mn = ADAM_B1 * m_ref[0] + (1.0 - ADAM_B1) * g
        vn = ADAM_B2 * v_ref[0] + (1.0 - ADAM_B2) * (g * g)
        m_hat = mn / (1.0 - ADAM_B1 ** ADAM_STEP)
        v_hat = vn / (1.0 - ADAM_B2 ** ADAM_STEP)
        go_ref[0] = g
        d_ref[0] = -ADAM_LR * (m_hat / (jnp.sqrt(v_hat) + ADAM_EPS) + ADAM_WD * w_ref[0])
        mo_ref[0] = mn
        vo_ref[0] = vn

    def g_map(l):
        return lambda layer, i: (0, jnp.where(layer == l, i, jnp.where(layer < l, 0, nb - 1)), 0)

    spec = pl.BlockSpec((1, t, cols), lambda layer, i: (layer, i, 0))
    outs = pl.pallas_call(
        body, name=name, grid=(nl, nb),
        in_specs=[pl.BlockSpec((npart, t, cols), g_map(l)) for l in range(nl)] + [spec, spec, spec],
        out_specs=[spec] * 4, out_shape=[_sds((nl, rows, cols), F32)] * 4,
        scratch_shapes=[pltpu.VMEM((t, cols), F32)],
        compiler_params=_cp(("arbitrary", "arbitrary")),
    )(*glist, w3, m3, v3)
    return tuple(o.reshape(shape) for o in outs)


_ANY = pl.BlockSpec(memory_space=pl.ANY)


def _all_gather(blocks, *, name):
    na = len(blocks)

    def body(*refs):
        x_refs, out_refs = refs[:na], refs[na:2 * na]
        send_sems, recv_sems, local_sems = refs[2 * na:]
        x, y, c = lax.axis_index("x"), lax.axis_index("y"), lax.axis_index("c")
        me, sibling = (x, y, c), (x, y, 1 - c)
        chips = [(1 - x, y), (x, 1 - y), (1 - x, 1 - y)]

        def slot(a, px, py, pc):
            return out_refs[a].at[4 * px + 2 * py + pc]

        def copy(a, k, blk, to, src=None):
            return pltpu.make_async_remote_copy(
                src_ref=slot(a, *blk) if src is None else src, dst_ref=slot(a, *blk),
                send_sem=send_sems.at[7 * a + k], recv_sem=recv_sems.at[7 * a + k],
                device_id=to, device_id_type=MESH_ID)

        mine = [pltpu.make_async_copy(x_refs[a], slot(a, *me), local_sems.at[a]) for a in range(na)]
        for cp in mine:
            cp.start()
        first = []
        for a in range(na):
            first.append(copy(a, 0, me, sibling, src=x_refs[a]))
            first += [copy(a, 1 + j, me, (*chip, c), src=x_refs[a]) for j, chip in enumerate(chips)]
        for cp in first:
            cp.start()
        passed = []
        for a in range(na):
            for j, chip in enumerate(chips):
                copy(a, 1 + j, (*chip, c), me).wait_recv()
                fwd = copy(a, 4 + j, (*chip, c), sibling)
                fwd.start()
                passed.append(fwd)
        for a in range(na):
            copy(a, 0, sibling, me).wait_recv()
            for j, chip in enumerate(chips):
                copy(a, 4 + j, (*chip, 1 - c), me).wait_recv()
        for cp in first + passed:
            cp.wait_send()
        for cp in mine:
            cp.wait()

    outs = pl.pallas_call(
        body, name=name, in_specs=[_ANY] * na, out_specs=[_ANY] * na,
        out_shape=[_sds((N_DEV,) + b.shape, b.dtype) for b in blocks],
        scratch_shapes=[pltpu.SemaphoreType.DMA((7 * na,)), pltpu.SemaphoreType.DMA((7 * na,)),
                        pltpu.SemaphoreType.DMA((na,))],
    )(*blocks)
    return list(outs)


_HBM = pl.BlockSpec(memory_space=pltpu.HBM)
_SEM = pl.BlockSpec(memory_space=pltpu.SEMAPHORE)
_EFFECT = pltpu.SideEffectType.DATAFLOW_SIDE_EFFECTING


def _peers(x, y, c):
    out = []
    for k in range(1, N_DEV):
        out.append((1 - x if k & 4 else x, 1 - y if k & 2 else y, 1 - c if k & 1 else c))
    return out


def _own_slots(srcs, scatter, *, name, after=None):
    na = len(srcs)
    n_extra = 0 if after is None else 1
    me = (4 * lax.axis_index("x") + 2 * lax.axis_index("y") + lax.axis_index("c")).astype(jnp.int32).reshape(1)

    def body(me_ref, *refs):
        in_refs, out_refs = refs[:na], refs[na + n_extra:]
        for a in range(na):
            out_refs[a][0] = in_refs[a][0] if scatter else in_refs[a][...]

    def slot_spec(shard):
        zeros = (0,) * len(shard)
        return pl.BlockSpec((1,) + tuple(shard), lambda i, me_ref: (me_ref[0],) + zeros)

    def whole_spec(shape):
        zeros = (0,) * len(shape)
        return pl.BlockSpec(tuple(shape), lambda i, me_ref: zeros)

    shards = [s.shape[1:] if scatter else s.shape for s in srcs]
    in_specs = [slot_spec(sh) if scatter else whole_spec(sh) for sh in shards] + [_ANY] * n_extra
    outs = pl.pallas_call(
        body, name=name,
        grid_spec=pltpu.PrefetchScalarGridSpec(
            num_scalar_prefetch=1, grid=(1,), in_specs=in_specs, out_specs=[slot_spec(sh) for sh in shards]),
        out_shape=[_sds((N_DEV,) + tuple(sh), s.dtype) for sh, s in zip(shards, srcs)],
        compiler_params=_cp(("arbitrary",)),
    )(me, *srcs, *([] if after is None else [after]))
    return list(outs)


def _exchange_copies(src_refs, land_refs, send_sems, recv_sems, scatter):
    x, y, c = lax.axis_index("x"), lax.axis_index("y"), lax.axis_index("c")
    me = 4 * x + 2 * y + c
    cps = []
    for a in range(len(src_refs)):
        for k, (px, py, pc) in enumerate(_peers(x, y, c)):
            src = src_refs[a].at[4 * px + 2 * py + pc] if scatter else src_refs[a]
            cps.append(pltpu.make_async_remote_copy(
                src_ref=src, dst_ref=land_refs[a].at[me], send_sem=send_sems.at[7 * a + k],
                recv_sem=recv_sems.at[7 * a + k], device_id=(px, py, pc), device_id_type=MESH_ID))
    return cps


def _exchange_start(srcs, lands, scatter, *, name):
    na = len(srcs)

    def body(*refs):
        src_refs, land_refs = refs[:na], refs[na:2 * na]
        send_sems, recv_sems = refs[2 * na], refs[2 * na + 1]
        token = refs[-1]
        for cp in _exchange_copies(src_refs, land_refs, send_sems, recv_sems, scatter):
            cp.start()
        token[...] = jnp.zeros(token.shape, token.dtype)

    hbm = lambda a: pltpu.HBM(a.shape, a.dtype)
    outs = pl.pallas_call(
        body, name=name,
        out_shape=(pltpu.SemaphoreType.DMA((7 * na,)), pltpu.SemaphoreType.DMA((7 * na,)),
                   *[hbm(a) for a in srcs], *[hbm(a) for a in lands], _sds((8, LANE), F32)),
        in_specs=[_HBM] * (2 * na),
        out_specs=(_SEM, _SEM, *[_HBM] * (2 * na), pl.BlockSpec(memory_space=pltpu.VMEM)),
        input_output_aliases={i: 2 + i for i in range(2 * na)},
        compiler_params=pltpu.CompilerParams(has_side_effects=_EFFECT),
    )(*[pltpu.with_memory_space_constraint(a, pltpu.HBM) for a in list(srcs) + list(lands)])
    return outs[0], outs[1], list(outs[2:2 + na]), list(outs[2 + na:2 + 2 * na]), outs[-1]


def _exchange_wait(send_sems, recv_sems, srcs, lands, after, scatter, *, name):
    na = len(srcs)

    def body(*refs):
        src_refs, land_refs = refs[:na], refs[na:2 * na]
        s_sems, r_sems = refs[2 * na], refs[2 * na + 1]
        for cp in _exchange_copies(src_refs, land_refs, s_sems, r_sems, scatter):
            cp.wait_send()
            cp.wait_recv()

    hbm = lambda a: pltpu.HBM(a.shape, a.dtype)
    outs = pl.pallas_call(
        body, name=name,
        out_shape=(*[hbm(a) for a in srcs], *[hbm(a) for a in lands]),
        in_specs=[_HBM] * (2 * na) + [_SEM, _SEM, _ANY],
        out_specs=tuple([_HBM] * (2 * na)),
        input_output_aliases={i: i for i in range(2 * na)},
        compiler_params=pltpu.CompilerParams(has_side_effects=_EFFECT),
    )(*srcs, *lands, send_sems, recv_sems, after)
    return list(outs[na:])


_WIN_SEGS = (("ql", 0, Q_LORA, SEG_QL[0]), ("kvl", Q_LORA, KV_LORA, SEG_KVL[0]),
             ("kr", Q_LORA + KV_LORA, ROPE, SEG_KR[0]), ("mg", Q_LORA + KV_LORA + ROPE, D_MLA, SEG_MG[0]),
             ("ci", Q_LORA + KV_LORA + ROPE + D_MLA, 2 * D_CONV, SEG_CI[0]),
             ("cg", Q_LORA + KV_LORA + ROPE + D_MLA + 2 * D_CONV, D_CONV, SEG_CG[0]))
_WIN_SHARD = IN_COLS // N_DEV


def _win_pieces():
    out = []
    for _, o, n, new in _WIN_SEGS:
        for j in range(N_DEV):
            lo, hi = max(o, j * _WIN_SHARD), min(o + n, (j + 1) * _WIN_SHARD)
            if lo < hi:
                out.append((j, lo - j * _WIN_SHARD, new + lo - o, hi - lo))
    return out


def _win_assemble(w_all, *, name):
    d = w_all.shape[1]
    t = min(ROW_T, d)
    pieces = sorted(_win_pieces(), key=lambda p: p[2])

    def body(w_ref, o_ref):
        cols = [w_ref[j, :, lo:lo + n].astype(F32) for j, lo, _, n in pieces]
        cols.append(jnp.zeros((t, IN_PAD - (SEG_KR[0] + ROPE)), F32))
        o_ref[...] = jnp.concatenate(cols, axis=1).astype(o_ref.dtype)

    return pl.pallas_call(
        body, name=name, grid=(d // t,),
        in_specs=[pl.BlockSpec((N_DEV, t, _WIN_SHARD), lambda i: (0, i, 0))],
        out_specs=_rowspec(t, IN_PAD), out_shape=_sds((d, IN_PAD), w_all.dtype),
        compiler_params=_cp(("parallel",)),
    )(w_all)


def _win_split(grad, *, name):
    d = grad.shape[0]
    t = min(ROW_T, d)
    by_shard = [sorted([p for p in _win_pieces() if p[0] == j], key=lambda p: p[1]) for j in range(N_DEV)]

    def body(g_ref, o_ref):
        g = g_ref[...]
        for j in range(N_DEV):
            cols = [g[:, new:new + n] for _, _, new, n in by_shard[j]]
            o_ref[j] = jnp.concatenate(cols, axis=1).astype(o_ref.dtype)

    return pl.pallas_call(
        body, name=name, grid=(d // t,),
        in_specs=[_rowspec(t, IN_PAD)],
        out_specs=pl.BlockSpec((N_DEV, t, _WIN_SHARD), lambda i: (0, i, 0)),
        out_shape=_sds((N_DEV, d, _WIN_SHARD), WIRE_DTYPE),
        compiler_params=_cp(("parallel",)),
    )(grad)


def _cols_to_shards(a):
    r, n = a.shape
    return a.reshape(r, N_DEV, n // N_DEV).transpose(1, 0, 2)


def _shards_to_cols(a):
    nd, r, w = a.shape
    return a.transpose(1, 0, 2).reshape(r, nd * w)


def _win_permute(w_in):
    o_ql, o_kvl, o_kr, o_mg = 0, Q_LORA, Q_LORA + KV_LORA, Q_LORA + KV_LORA + ROPE
    o_ci = o_mg + D_MLA
    o_cg = o_ci + 2 * D_CONV
    seg = lambda o, n: w_in[:, o:o + n]
    pad = jnp.zeros((w_in.shape[0], IN_PAD - (SEG_KR[0] + ROPE)), w_in.dtype)
    return jnp.concatenate([seg(o_ci, 2 * D_CONV), seg(o_mg, D_MLA), seg(o_cg, D_CONV), seg(o_ql, Q_LORA),
                            seg(o_kvl, KV_LORA), seg(o_kr, ROPE), pad], axis=1)


def _win_unpermute(g):
    seg = lambda s, n=None: g[:, s[0]:s[0] + (s[1] if n is None else n)]
    return jnp.concatenate([seg(SEG_QL), seg(SEG_KVL), seg(SEG_KR, ROPE), seg(SEG_MG), seg(SEG_CI), seg(SEG_CG)], axis=1)


def _qup_permute(w):
    w3 = w.reshape(w.shape[0], N_HEADS, QK_DIM)
    nope = w3[:, :, :NOPE].reshape(w.shape[0], N_HEADS * NOPE)
    rope = jnp.pad(w3[:, :, NOPE:], ((0, 0), (0, 0), (0, LANE - ROPE))).reshape(w.shape[0], N_HEADS * LANE)
    return jnp.concatenate([nope, rope], axis=1)


def _qup_unpermute(g):
    r = g.shape[0]
    nope = g[:, :N_HEADS * NOPE].reshape(r, N_HEADS, NOPE)
    rope = g[:, N_HEADS * NOPE:].reshape(r, N_HEADS, LANE)[:, :, :ROPE]
    return jnp.concatenate([nope, rope], axis=2).reshape(r, N_HEADS * QK_DIM)


def _norm_tiles(g):
    return g[:NOPE].reshape(1, LANE), jnp.pad(g[NOPE:], (0, LANE - ROPE)).reshape(1, LANE)


def _norm_untile(gt):
    return jnp.concatenate([gt[0, :NOPE], gt[0, LANE:LANE + ROPE]])


def _rope_tiles(positions):
    inv_freq = 1.0 / (ROPE_THETA ** (jnp.arange(0, ROPE, 2, dtype=F32) / ROPE))
    ang = positions.astype(F32)[:, None] * inv_freq
    cos, sin = jnp.cos(ang), jnp.sin(ang)
    zq = jnp.zeros_like(cos)
    c_t = jnp.concatenate([cos, cos, zq, zq], axis=1)
    s1_t = jnp.concatenate([-sin, zq, zq, zq], axis=1)
    s2_t = jnp.concatenate([zq, sin, zq, zq], axis=1)
    return c_t, s1_t, s2_t


_BIG = ("w_in", "w_q_up", "w_kv_up", "w_pw", "w_out")
_COL_SHARDED = ("w_in", "w_q_up", "w_kv_up")


def _pack_rows(arrs):
    return jnp.concatenate([a.reshape(-1, LANE) for a in arrs], axis=0)


def _unpack_rows(buf, shapes):
    out, r0 = [], 0
    lead = buf.shape[:-2]
    for shp in shapes:
        n = math.prod(shp) // LANE
        out.append(buf[..., r0:r0 + n, :].reshape(lead + tuple(shp)))
        r0 += n
    return out


_SMALL = (("dmod", 3 * D_MODEL), ("norm_g", D_MODEL), ("q_lat_g", Q_LORA), ("kv_lat_g", KV_LORA),
          ("q_norm_g", 2 * LANE), ("k_norm_g", 2 * LANE), ("glu_b", 2 * D_CONV), ("dw_w", HALO * D_CONV),
          ("dw_b", D_CONV), ("conv_ln_g", D_CONV), ("conv_ln_b", D_CONV), ("b_pw", D_CONV))


def _layer_fwd(x, p, rope, l, late=None):
    n = lambda s: f"{s}_l{l}"
    c_t, s1_t, s2_t = rope
    h = _prenorm(x, p["norm_g"], p["shift"], p["sc1p"], name=n("prenorm"))
    z = _mm(h, p["w_in"], name=n("in_proj"), tn=IN_TILE, n_outer=True)
    if late is not None:
        p = {**p, **late(z)}
    qn, kn = _lat_norm(z, p["q_lat_g"], p["kv_lat_g"], name=n("lat_norm"))
    q_raw = _mm(qn, p["w_q_up"], name=n("q_up"), tn=1024)
    kv = _mm(kn, p["w_kv_up"], name=n("kv_up"), tn=1024)
    qf, kf, vf = _qk_prep(q_raw, kv, z, c_t, s1_t, s2_t, *p["qk_tiles"], name=n("qk_prep"))
    o, lse = _flash_fwd(qf, kf, vf, name=n("flash_fwd"))
    u1, u3 = _conv_fwd(z, p["glu_b"], p["dw_w"], p["dw_b"], p["conv_ln_g"], p["conv_ln_b"], name=n("conv_fwd"))
    u4m = _mm(u3, p["w_pw"], name=n("pw"), tn=1024)
    cat = _gate_cat(o, z, u4m, p["b_pw"], name=n("gate_cat"))
    y, x_next = _mm(cat, p["w_out"], name=n("out_proj"), tn=1024, residual=(x, p["gate"]))
    saved = dict(x=x, h=h, z=z, qn=qn, kn=kn, q_raw=q_raw, kv=kv, qf=qf, kf=kf, vf=vf, o=o, lse=lse,
                 u1=u1, u3=u3, u4m=u4m, cat=cat, y=y)
    return x_next, saved, p


def _layer_bwd(gxo, p, sv, rope, l, hook_rest=None, hook_w_in=None):
    n = lambda s: f"{s}_l{l}"
    c_t, s1_t, s2_t = rope
    z = sv["z"]
    dy, dgate = _out_bwd(gxo, sv["y"], p["gate"], name=n("out_bwd"))
    g_w_out = _mm(sv["cat"], dy, ta=True, name=n("g_w_out"), tm=1024, tn=1024)
    dcat = _mm(dy, p["w_out"], tb=True, name=n("d_cat"), tn=1024)
    do, delta, du4, g_b_pw, dz = _gate_bwd(dcat, sv["o"], z, sv["u4m"], p["b_pw"], name=n("gate_bwd"))
    g_w_pw = _mm(sv["u3"], du4, ta=True, name=n("g_w_pw"), tm=1024, tn=1024, tk=512)
    du3 = _mm(du4, p["w_pw"], tb=True, name=n("d_u3"), tn=1024)
    dz, g_ln_g, g_ln_b, g_dw_b, g_glu_b, g_dw_w = _conv_bwd(
        du3, sv["u1"], z, dz, p["glu_b"], p["dw_w"], p["conv_ln_g"], p["conv_ln_b"], name=n("conv_bwd"))
    t_att = min(ATT_T, z.shape[0])
    to_lanes = lambda a: a.reshape(N_HEADS, z.shape[0] // t_att, 1, t_att)
    dqf, dkf, dvf = _flash_bwd(sv["qf"], sv["kf"], sv["vf"], do,
                               to_lanes(sv["lse"][:, :, 0]), to_lanes(delta), name=n("flash_bwd"))
    dq_raw, dkv, dkr, g_qn, g_kn = _qk_bwd(dqf, dkf, dvf, sv["q_raw"], sv["kv"], z, c_t, s1_t, s2_t,
                                            *p["qk_tiles"], name=n("qk_bwd"))
    g_w_q_up = _mm(sv["qn"], dq_raw, ta=True, name=n("g_w_q_up"), tm=512, tn=1024, tk=512)
    dqn = _mm(dq_raw, p["w_q_up"], tb=True, name=n("d_qn"))
    g_w_kv_up = _mm(sv["kn"], dkv, ta=True, name=n("g_w_kv_up"), tm=256, tn=1024, tk=512)
    dkn = _mm(dkv, p["w_kv_up"], tb=True, name=n("d_kn"))
    dz, g_ql, g_kvl = _lat_bwd(dqn, dkn, dkr, z, dz, p["q_lat_g"], p["kv_lat_g"], name=n("lat_bwd"))
    big = dict(w_q_up=g_w_q_up, w_kv_up=g_w_kv_up, w_pw=g_w_pw, w_out=g_w_out)
    after = None if hook_rest is None else hook_rest(big)
    g_w_in = _mm(sv["h"], dz, ta=True, name=n("g_w_in"), tm=512, tn=IN_TILE, after=after)
    big["w_in"] = g_w_in
    after = None if hook_w_in is None else hook_w_in(g_w_in)
    dh = _mm(dz, p["w_in"], tb=True, name=n("d_h"), tn=1024, after=after)
    dx, dshift, dscale, g_norm = _prenorm_bwd(dh, sv["x"], gxo, p["norm_g"], p["sc1p"], name=n("prenorm_bwd"))
    small = dict(dmod=jnp.concatenate([dshift, dscale, dgate], axis=1), norm_g=g_norm, q_lat_g=g_ql, kv_lat_g=g_kvl,
                 q_norm_g=g_qn, k_norm_g=g_kn, glu_b=g_glu_b, dw_w=g_dw_w, dw_b=g_dw_b,
                 conv_ln_g=g_ln_g, conv_ln_b=g_ln_b, b_pw=g_b_pw)
    return dx, big, small


def _layer_params(l, full, mod_l, small):
    d = D_MODEL
    row = lambda a: a.reshape(1, -1)
    shift, scale, gate = mod_l[:, :d], mod_l[:, d:2 * d], mod_l[:, 2 * d:]
    dw_w = jnp.pad(full["dw_w"][l], ((0, HALO - CONV_K), (0, 0)))
    return dict(
        shift=shift, sc1p=1.0 + scale, gate=gate, norm_g=row(small["norm_g"][l]),
        **{k: full[k][l] for k in _BIG if k in full}, dw_w=dw_w,
        q_lat_g=row(small["q_lat_g"][l]), kv_lat_g=row(small["kv_lat_g"][l]),
        qk_tiles=_norm_tiles(small["q_norm_g"][l]) + _norm_tiles(small["k_norm_g"][l]),
        glu_b=row(small["glu_b"][l]), dw_b=row(small["dw_b"][l]), conv_ln_g=row(small["conv_ln_g"][l]),
        conv_ln_b=row(small["conv_ln_b"][l]), b_pw=row(small["b_pw"][l]))


def kernel(x, c, positions, ada_w, ada_b, norm_g, w_in, q_lat_g, w_q_up, kv_lat_g, w_kv_up, q_norm_g, k_norm_g, glu_b, dw_w, dw_b, conv_ln_g, conv_ln_b, w_pw, b_pw, w_out, loss_target, m_ada_w, m_ada_b, m_norm_g, m_w_in, m_q_lat_g, m_w_q_up, m_kv_lat_g, m_w_kv_up, m_q_norm_g, m_k_norm_g, m_glu_b, m_dw_w, m_dw_b, m_conv_ln_g, m_conv_ln_b, m_w_pw, m_b_pw, m_w_out, v_ada_w, v_ada_b, v_norm_g, v_w_in, v_q_lat_g, v_w_q_up, v_kv_lat_g, v_w_kv_up, v_q_norm_g, v_k_norm_g, v_glu_b, v_dw_w, v_dw_b, v_conv_ln_g, v_conv_ln_b, v_w_pw, v_b_pw, v_w_out):
    names = ("ada_w", "ada_b", "norm_g", "w_in", "q_lat_g", "w_q_up", "kv_lat_g", "w_kv_up", "q_norm_g",
             "k_norm_g", "glu_b", "dw_w", "dw_b", "conv_ln_g", "conv_ln_b", "w_pw", "b_pw", "w_out")
    w_loc = dict(zip(names, (ada_w, ada_b, norm_g, w_in, q_lat_g, w_q_up, kv_lat_g, w_kv_up, q_norm_g, k_norm_g,
                             glu_b, dw_w, dw_b, conv_ln_g, conv_ln_b, w_pw, b_pw, w_out)))
    m_loc = dict(zip(names, (m_ada_w, m_ada_b, m_norm_g, m_w_in, m_q_lat_g, m_w_q_up, m_kv_lat_g, m_w_kv_up,
                             m_q_norm_g, m_k_norm_g, m_glu_b, m_dw_w, m_dw_b, m_conv_ln_g, m_conv_ln_b, m_w_pw,
                             m_b_pw, m_w_out)))
    v_loc = dict(zip(names, (v_ada_w, v_ada_b, v_norm_g, v_w_in, v_q_lat_g, v_w_q_up, v_kv_lat_g, v_w_kv_up,
                             v_q_norm_g, v_k_norm_g, v_glu_b, v_dw_w, v_dw_b, v_conv_ln_g, v_conv_ln_b, v_w_pw,
                             v_b_pw, v_w_out)))
    nl, d = N_LAYERS, D_MODEL
    me = 4 * lax.axis_index("x") + 2 * lax.axis_index("y") + lax.axis_index("c")
    x2, tgt = x[0], loss_target[0]
    ada_cols = ada_w.shape[-1]

    c_all = _all_gather([c.reshape(d // LANE, LANE)], name="gather_c")[0].reshape(N_DEV, d)
    ada_b_cols = lax.dynamic_slice_in_dim(ada_b, me * ada_cols, ada_cols, axis=1).reshape(nl, 1, ada_cols)
    mod_cols = _ada_fwd(c_all, ada_w, ada_b_cols, name="ada_fwd")
    mod_all = _all_gather([mod_cols], name="gather_mod")[0]
    mod_me = lax.dynamic_index_in_dim(mod_all, me, axis=2, keepdims=False)
    mod = mod_me.transpose(1, 0, 2).reshape(nl, 1, N_DEV * ada_cols)

    dw_pad = jnp.pad(dw_w, ((0, 0), (0, HALO - CONV_K), (0, 0)))
    wire = {k: w_loc[k].astype(WIRE_DTYPE) for k in _BIG}
    w_in_all0, dw_all = _all_gather([wire["w_in"][0], dw_pad], name="gather_w_in_l0")
    rest0 = [wire[k][0] for k in _BIG[1:]]
    fly_r0 = _exchange_start(rest0, _own_slots(rest0, False, name="own_weights_l0_rest", after=w_in_all0), False,
                             name="gather_start_l0_rest")
    fly_w1 = {}

    def layout_rest(parts):
        return dict(w_q_up=_qup_permute(_shards_to_cols(parts[0])), w_kv_up=_shards_to_cols(parts[1]),
                    w_pw=parts[2].reshape(D_CONV, D_CONV), w_out=parts[3].reshape(D_MLA + D_CONV, d))

    small_in = dict(norm_g=norm_g, q_lat_g=q_lat_g, kv_lat_g=kv_lat_g, q_norm_g=q_norm_g, k_norm_g=k_norm_g,
                    glu_b=glu_b, dw_b=dw_b, conv_ln_g=conv_ln_g, conv_ln_b=conv_ln_b, b_pw=b_pw)
    dw_full = [_shards_to_cols(dw_all[:, l])[:CONV_K] for l in range(nl)]
    rope = _rope_tiles(positions[0])

    def layer_params(l, w_in_all, rest, mod_l):
        full = dict(w_in={l: _win_assemble(w_in_all, name=f"w_in_assemble_l{l}")}, dw_w=dw_full)
        if rest is not None:
            full.update({k: {l: a} for k, a in layout_rest(rest).items()})
        return _layer_params(l, full, mod_l, small_in)

    def late_l0(z):
        parts = _exchange_wait(*fly_r0[:4], z, False, name="gather_wait_l0_rest")
        src1 = [wire[k][1] for k in _BIG]
        fly_w1["x"] = _exchange_start(src1, _own_slots(src1, False, name="own_weights_l1", after=parts[0]), False,
                                      name="gather_start_l1")
        late = layout_rest(parts)
        late["q_lat_g"] = small_in["q_lat_g"][0].reshape(1, -1) + fly_w1["x"][4][0, 0]
        return late

    params, saved = [None] * nl, [None] * nl
    p0 = layer_params(0, w_in_all0, None, mod[0] + fly_r0[4][0, 0])
    xs, saved[0], params[0] = _layer_fwd(x2, p0, rope, 0, late=late_l0)
    parts1 = _exchange_wait(*fly_w1["x"][:4], xs, False, name="gather_wait_l1")
    params[1] = layer_params(1, parts1[0], parts1[1:], mod[1])
    xs, saved[1], _ = _layer_fwd(xs, params[1], rope, 1)
    gx, loss_part = _loss_head(xs, tgt, name="loss_head")
    loss = lax.psum(loss_part[0, 0], ("x", "y", "c"))

    def shard_major(k, g):
        if k == "w_q_up":
            g = _qup_unpermute(g)
        if k in _COL_SHARDED:
            return _cols_to_shards(g)
        return g.reshape((N_DEV, g.shape[0] // N_DEV, g.shape[1]))

    def scatter_start(send, tag):
        lands = _own_slots(send, True, name=f"own_grads_{tag}")
        return _exchange_start(send, lands, True, name=f"scatter_start_{tag}")

    def wire_rest(big):
        return [shard_major(k, big[k]).astype(WIRE_DTYPE) for k in _BIG[1:]]

    big_g, small_g, flying = [None] * nl, [None] * nl, {}
    gx, big_g[1], small_g[1] = _layer_bwd(gx, params[1], saved[1], rope, 1)
    flying["l1"] = scatter_start([_win_split(big_g[1]["w_in"], name="w_in_split_l1")] + wire_rest(big_g[1]), "l1")
    p0 = dict(params[0])
    p0["gate"] = p0["gate"] + flying["l1"][4][0, 0]

    def start_rest_l0(big):
        flying["l0_rest"] = scatter_start(wire_rest(big), "l0_rest")
        return flying["l0_rest"][4]

    def start_w_in_l0(g_w_in):
        flying["l0_w_in"] = scatter_start([_win_split(g_w_in, name="w_in_split_l0")], "l0_w_in")
        return flying["l0_w_in"][4]

    gx, big_g[0], small_g[0] = _layer_bwd(gx, p0, saved[0], rope, 0, hook_rest=start_rest_l0,
                                          hook_w_in=start_w_in_l0)

    tile = 8 * LANE
    padded = [(k, nn, -(-nn // tile) * tile) for k, nn in _SMALL]
    spk = jnp.concatenate([jnp.pad(small_g[l][k].reshape(-1), (0, np_ - nn)).reshape(-1, LANE)
                           for l in range(nl) for k, nn, np_ in padded], axis=0)
    s_all = _all_gather([spk], name="gather_small_grads")[0]
    s_rows = sum(np_ for _, _, np_ in padded) // LANE
    s_all = s_all.reshape(N_DEV, nl, s_rows, LANE)
    s_parts = {k: a[..., :nn] for (k, nn, _), a in
               zip(padded, _unpack_rows(s_all, [(np_,) for _, _, np_ in padded]))}

    dmod_all = s_parts["dmod"]
    dmod_cols = lax.dynamic_slice_in_dim(dmod_all, me * ada_cols, ada_cols, axis=2).transpose(1, 0, 2)
    g_ada_w = _ada_bwd(c_all.T, dmod_cols, name="ada_bwd")
    gp = {}
    gp["ada_w"] = g_ada_w[None]
    gp["ada_b"] = dmod_all
    for k in ("norm_g", "q_lat_g", "kv_lat_g", "glu_b", "dw_b", "conv_ln_g", "conv_ln_b", "b_pw"):
        gp[k] = s_parts[k]
    for k in ("q_norm_g", "k_norm_g"):
        t = s_parts[k]
        gp[k] = jnp.concatenate([t[..., :NOPE], t[..., LANE:LANE + ROPE]], axis=-1)
    dw_g = s_parts["dw_w"].reshape(N_DEV, nl, HALO, D_CONV)[:, :, :CONV_K]
    gp["dw_w"] = lax.dynamic_slice_in_dim(dw_g, me * LANE, LANE, axis=3)

    res = {k: _adamw(gp[k], w_loc[k], m_loc[k], v_loc[k], name=f"adamw_{k}") for k in names if k not in _BIG}
    arrived = [None] * nl
    arrived[1] = _exchange_wait(*flying["l1"][:4], gx, True, name="scatter_wait_l1")
    rest0 = _exchange_wait(*flying["l0_rest"][:4], gx, True, name="scatter_wait_l0_rest")
    arrived[0] = _exchange_wait(*flying["l0_w_in"][:4], res["ada_w"][1], True, name="scatter_wait_l0_w_in") + rest0
    for i, k in enumerate(_BIG):
        res[k] = _adamw([arrived[l][i] for l in range(nl)], w_loc[k], m_loc[k], v_loc[k], name=f"adamw_{k}")
    out = [loss, gx[None]]
    for idx in range(4):
        out += [res[k][idx] for k in names]
    return tuple(out)
```

```python
import functools
import math

import jax
import jax.numpy as jnp
from jax import lax
from jax.experimental import pallas as pl
from jax.experimental.pallas import tpu as pltpu

F32 = jnp.float32
MXU_DTYPE = jnp.bfloat16
WIRE_DTYPE = jnp.bfloat16

D_MODEL = 2048
N_LAYERS = 2
N_DEV = 8
N_HEADS = 8
NOPE = 128
ROPE = 64
V_DIM = 128
QK_DIM = NOPE + ROPE
Q_LORA = 512
KV_LORA = 256
D_MLA = N_HEADS * V_DIM
D_CONV = 1024
CONV_K = 31
ROPE_THETA = 10000.0
EPS = 1e-6
LANE = 128
HEAD_PAD = 2 * LANE
HALO = 32

SEG_CI = (0, 2 * D_CONV)
SEG_MG = (2 * D_CONV, D_MLA)
SEG_CG = (2 * D_CONV + D_MLA, D_CONV)
SEG_QL = (2 * D_CONV + D_MLA + D_CONV, Q_LORA)
SEG_KVL = (SEG_QL[0] + Q_LORA, KV_LORA)
SEG_KR = (SEG_KVL[0] + KV_LORA, LANE)
SEG_LAT = (SEG_QL[0], 1024)
IN_PAD = SEG_LAT[0] + SEG_LAT[1]
IN_TILE = IN_PAD // 4
assert SEG_KR[0] + LANE <= IN_PAD and SEG_LAT[0] % SEG_LAT[1] == 0
IN_COLS = Q_LORA + KV_LORA + ROPE + D_MLA + 2 * D_CONV + D_CONV

ADAM_LR = 0.001
ADAM_B1 = 0.9
ADAM_B2 = 0.999
ADAM_EPS = 1e-08
ADAM_WD = 0.01
ADAM_STEP = 10

VMEM_LIMIT = 56 * 1024 * 1024
ATT_T = 512
ROW_T = 256
CONV_T = 128
MESH_ID = pl.DeviceIdType.MESH


def _cp(sem=None):
    kw = dict(vmem_limit_bytes=VMEM_LIMIT)
    if sem is not None:
        kw["dimension_semantics"] = sem
    return pltpu.CompilerParams(**kw)


def _sds(shape, dtype):
    return jax.ShapeDtypeStruct(shape, dtype)


def _silu(x):
    return x * jax.nn.sigmoid(x)


def _dsilu(x):
    s = jax.nn.sigmoid(x)
    return s * (1.0 + x * (1.0 - s))


def _rowspec(t, width, col=0):
    return pl.BlockSpec((t, width), lambda i: (i, col))


def _vecspec(width):
    return pl.BlockSpec((1, width), lambda i: (0, 0))


def _colsum(v):
    return jnp.sum(v, axis=0, keepdims=True)


def _mm(a, b, *, name, ta=False, tb=False, out_dtype=F32, tm=512, tn=512, tk=None, n_outer=False, after=None,
        residual=None):
    if ta:
        kdim, m = a.shape
    else:
        m, kdim = a.shape
    if tb:
        n, k2 = b.shape
    else:
        k2, n = b.shape
    assert kdim == k2, (a.shape, b.shape)
    tm, tn = min(tm, m), min(tn, n)
    tk = kdim if tk is None else min(tk, kdim)
    assert m % tm == 0 and n % tn == 0 and kdim % tk == 0, (m, n, kdim, tm, tn, tk)
    nk = kdim // tk
    dims = (((0 if ta else 1,), (1 if tb else 0,)), ((), ()))

    n_extra = 0 if after is None else 1
    assert residual is None or nk == 1

    def body(a_ref, b_ref, *rest):
        if residual is not None:
            x_ref, gate_ref = rest[:2]
            rest = rest[2:]
        o_ref, scratch = rest[n_extra], rest[n_extra + 1:]
        prod = lax.dot_general(a_ref[...].astype(MXU_DTYPE), b_ref[...].astype(MXU_DTYPE), dims,
                               preferred_element_type=F32)
        if residual is not None:
            o_ref[...] = prod.astype(o_ref.dtype)
            scratch[0][...] = x_ref[...] + gate_ref[...] * prod
        elif nk == 1:
            o_ref[...] = prod.astype(o_ref.dtype)
        else:
            acc = scratch[0]
            k = pl.program_id(2)

            @pl.when(k == 0)
            def _():
                acc[...] = prod

            @pl.when(k > 0)
            def _():
                acc[...] += prod

            @pl.when(k == nk - 1)
            def _():
                o_ref[...] = acc[...].astype(o_ref.dtype)

    if n_outer:
        ij = lambda g0, g1: (g1, g0)
        grid = (n // tn, m // tm, nk)
    else:
        ij = lambda g0, g1: (g0, g1)
        grid = (m // tm, n // tn, nk)

    def a_map(g0, g1, k):
        i, _ = ij(g0, g1)
        return (k, i) if ta else (i, k)

    def b_map(g0, g1, k):
        _, j = ij(g0, g1)
        return (j, k) if tb else (k, j)

    def o_map(g0, g1, k):
        return ij(g0, g1)

    in_specs = [pl.BlockSpec((tk, tm) if ta else (tm, tk), a_map), pl.BlockSpec((tn, tk) if tb else (tk, tn), b_map)]
    operands = [a, b]
    out_specs, out_shape = pl.BlockSpec((tm, tn), o_map), _sds((m, n), out_dtype)
    if residual is not None:
        in_specs += [pl.BlockSpec((tm, tn), o_map), pl.BlockSpec((1, tn), lambda g0, g1, k: (0, ij(g0, g1)[1]))]
        operands += list(residual)
        out_specs, out_shape = [out_specs, pl.BlockSpec((tm, tn), o_map)], [out_shape, _sds((m, n), F32)]
    if after is not None:
        in_specs.append(_ANY)
        operands.append(after)
    return pl.pallas_call(
        body, name=name, grid=grid, in_specs=in_specs, out_specs=out_specs, out_shape=out_shape,
        scratch_shapes=[pltpu.VMEM((tm, tn), F32)] if nk > 1 else [],
        compiler_params=_cp(("parallel", "parallel", "arbitrary")),
    )(*operands)


def _prenorm(x, g, shift, sc1p, *, name):
    s, d = x.shape
    t = min(ROW_T, s)

    def body(x_ref, g_ref, sh_ref, sc_ref, h_ref):
        xv = x_ref[...]
        r = lax.rsqrt(jnp.mean(xv * xv, axis=-1, keepdims=True) + EPS)
        h_ref[...] = ((xv * r) * g_ref[...] * sc_ref[...] + sh_ref[...]).astype(h_ref.dtype)

    return pl.pallas_call(
        body, name=name, grid=(s // t,),
        in_specs=[_rowspec(t, d), _vecspec(d), _vecspec(d), _vecspec(d)],
        out_specs=_rowspec(t, d), out_shape=_sds((s, d), MXU_DTYPE),
        compiler_params=_cp(("parallel",)),
    )(x, g, shift, sc1p)


def _lat_norm(z, g_ql, g_kvl, *, name):
    s = z.shape[0]
    t = min(ROW_T, s)

    def body(ql_ref, kvl_ref, gq_ref, gk_ref, qn_ref, kn_ref):
        for src, g_ref, dst in ((ql_ref, gq_ref, qn_ref), (kvl_ref, gk_ref, kn_ref)):
            v = src[...]
            r = lax.rsqrt(jnp.mean(v * v, axis=-1, keepdims=True) + EPS)
            dst[...] = ((v * r) * g_ref[...]).astype(dst.dtype)

    return pl.pallas_call(
        body, name=name, grid=(s // t,),
        in_specs=[_rowspec(t, Q_LORA, SEG_QL[0] // Q_LORA), _rowspec(t, KV_LORA, SEG_KVL[0] // KV_LORA),
                  _vecspec(Q_LORA), _vecspec(KV_LORA)],
        out_specs=[_rowspec(t, Q_LORA), _rowspec(t, KV_LORA)],
        out_shape=[_sds((s, Q_LORA), MXU_DTYPE), _sds((s, KV_LORA), MXU_DTYPE)],
        compiler_params=_cp(("parallel",)),
    )(z, z, g_ql, g_kvl)


def _rope_fwd(r, c_t, s1_t, s2_t):
    return r * c_t + pltpu.roll(r, LANE - ROPE // 2, 1) * s1_t + pltpu.roll(r, ROPE // 2, 1) * s2_t


def _rope_bwd(d, c_t, s1_t, s2_t):
    return d * c_t + pltpu.roll(d * s1_t, ROPE // 2, 1) + pltpu.roll(d * s2_t, LANE - ROPE // 2, 1)


def _lanesum(v):
    return jnp.sum(v, axis=-1, keepdims=True)


def _qk_prep(q_raw, kv, z, c_t, s1_t, s2_t, gqn, gqr, gkn, gkr, *, name):
    s = q_raw.shape[0]
    t = min(ROW_T, s)
    scale = 1.0 / math.sqrt(QK_DIM)

    def body(q_ref, kv_ref, kr_ref, c_ref, s1_ref, s2_ref, gqn_ref, gqr_ref, gkn_ref, gkr_ref,
             qf_ref, kf_ref, vf_ref):
        c_v, s1_v, s2_v = c_ref[...], s1_ref[...], s2_ref[...]
        kr = kr_ref[...]
        kr_ss = _lanesum(kr * kr)
        for h in range(N_HEADS):
            n = q_ref[:, h * LANE:(h + 1) * LANE]
            r = q_ref[:, N_HEADS * LANE + h * LANE:N_HEADS * LANE + (h + 1) * LANE]
            rs = lax.rsqrt((_lanesum(n * n) + _lanesum(r * r)) * (1.0 / QK_DIM) + EPS)
            qf_ref[h, :, 0:LANE] = (((n * rs) * gqn_ref[...]) * scale).astype(qf_ref.dtype)
            rr = _rope_fwd((r * rs) * gqr_ref[...], c_v, s1_v, s2_v)
            qf_ref[h, :, LANE:HEAD_PAD] = (rr * scale).astype(qf_ref.dtype)

            n = kv_ref[:, h * 2 * LANE:h * 2 * LANE + LANE]
            rs = lax.rsqrt((_lanesum(n * n) + kr_ss) * (1.0 / QK_DIM) + EPS)
            kf_ref[h, :, 0:LANE] = ((n * rs) * gkn_ref[...]).astype(kf_ref.dtype)
            kf_ref[h, :, LANE:HEAD_PAD] = _rope_fwd((kr * rs) * gkr_ref[...], c_v, s1_v, s2_v).astype(kf_ref.dtype)
            vf_ref[h, :, 0:V_DIM] = kv_ref[:, h * 2 * LANE + LANE:(h + 1) * 2 * LANE].astype(vf_ref.dtype)
            vf_ref[h, :, V_DIM:] = jnp.ones((t, V_DIM), vf_ref.dtype)

    hspec = lambda w: pl.BlockSpec((N_HEADS, t, w), lambda i: (0, i, 0))
    return pl.pallas_call(
        body, name=name, grid=(s // t,),
        in_specs=[_rowspec(t, 2 * N_HEADS * LANE), _rowspec(t, 2 * N_HEADS * LANE),
                  _rowspec(t, LANE, SEG_KR[0] // LANE),
                  _rowspec(t, LANE), _rowspec(t, LANE), _rowspec(t, LANE),
                  _vecspec(LANE), _vecspec(LANE), _vecspec(LANE), _vecspec(LANE)],
        out_specs=[hspec(HEAD_PAD), hspec(HEAD_PAD), hspec(2 * V_DIM)],
        out_shape=[_sds((N_HEADS, s, HEAD_PAD), MXU_DTYPE), _sds((N_HEADS, s, HEAD_PAD), MXU_DTYPE),
                   _sds((N_HEADS, s, 2 * V_DIM), MXU_DTYPE)],
        compiler_params=_cp(("parallel",)),
    )(q_raw, kv, z, c_t, s1_t, s2_t, gqn, gqr, gkn, gkr)


def _causal_mask(t):
    row = lax.broadcasted_iota(jnp.int32, (t, t), 0)
    col = lax.broadcasted_iota(jnp.int32, (t, t), 1)
    return col <= row


NEG = -1e30


def _flash_fwd(qf, kf, va, *, name):
    nh, s, dk = qf.shape
    dv = va.shape[-1] // 2
    t = min(ATT_T, s)
    n = s // t
    assert dv == LANE and t % LANE == 0

    def body(q_ref, k_ref, v_ref, o_ref, lse_ref, m_s, acc_s, s_buf):
        i = pl.program_id(1)
        m_s[...] = jnp.full(m_s.shape, NEG, F32)
        acc_s[...] = jnp.zeros(acc_s.shape, F32)
        q = q_ref[0]

        def rows_of(j):
            return pl.ds(pl.multiple_of(j * t, t), t)

        def scores(j):
            return lax.dot_general(q, k_ref[0, rows_of(j), :], (((1,), (1,)), ((), ())), preferred_element_type=F32)

        def consume(j, slot, masked):
            sc = s_buf[slot]
            if masked:
                sc = jnp.where(_causal_mask(t), sc, NEG)
            m_prev = m_s[...]
            m_new = jnp.maximum(m_prev, jnp.max(sc, axis=-1, keepdims=True))
            alpha = jnp.exp(m_prev - m_new)
            p = jnp.exp(sc - jnp.tile(m_new, (1, t // LANE)))
            acc_s[...] = jnp.tile(alpha, (1, 2)) * acc_s[...] + jnp.dot(
                p.astype(MXU_DTYPE), v_ref[0, rows_of(j), :], preferred_element_type=F32)
            m_s[...] = m_new

        s_buf[0] = scores(0)

        def pair(a, carry):
            s_buf[1] = scores(2 * a + 1)
            consume(2 * a, 0, False)
            s_buf[0] = scores(2 * a + 2)
            consume(2 * a + 1, 1, False)
            return carry

        lax.fori_loop(0, i // 2, pair, 0)

        @pl.when(i % 2 == 1)
        def _():
            s_buf[1] = scores(i)
            consume(i - 1, 0, False)
            consume(i, 1, True)

        @pl.when(i % 2 == 0)
        def _():
            consume(i, 0, True)

        den = acc_s[:, dv:]
        o_ref[...] = acc_s[:, :dv] / den
        lse_ref[0] = m_s[...] + jnp.log(den)

    return pl.pallas_call(
        body, name=name, grid=(nh, n),
        in_specs=[pl.BlockSpec((1, t, dk), lambda h, i: (h, i, 0)),
                  pl.BlockSpec((1, s, dk), lambda h, i: (h, 0, 0)),
                  pl.BlockSpec((1, s, 2 * dv), lambda h, i: (h, 0, 0))],
        out_specs=[pl.BlockSpec((t, dv), lambda h, i: (i, h)),
                   pl.BlockSpec((1, t, LANE), lambda h, i: (h, i, 0))],
        out_shape=[_sds((s, nh * dv), F32), _sds((nh, s, LANE), F32)],
        scratch_shapes=[pltpu.VMEM((t, LANE), F32), pltpu.VMEM((t, 2 * dv), F32), pltpu.VMEM((2, t, t), F32)],
        compiler_params=_cp(("parallel", "arbitrary")),
    )(qf, kf, va)


def _shifted_copies(ext_ref):
    rows = ext_ref.shape[1] - 8
    for s in range(1, 8):
        ext_ref[s, 0:rows, :] = ext_ref[0, s:s + rows, :]


def _window(ext_ref, off, t_rows, lane0, lanes):
    return ext_ref[off % 8, pl.ds(off - off % 8, t_rows), lane0:lane0 + lanes]


def _dw_taps(ext_ref, w_ref, row0, t_rows, lane0, lanes, first_off):
    acc = None
    for k in range(CONV_K):
        term = w_ref[k:k + 1, lane0:lane0 + lanes] * _window(ext_ref, row0 + first_off + k, t_rows, lane0, lanes)
        acc = term if acc is None else acc + term
    return acc


CONV_RC = 32
CONV_LC = 256


def _conv_fwd(z, glu_b, dw_w, dw_b, ln_g, ln_b, *, name):
    s = z.shape[0]
    t = min(CONV_T, s)
    c2 = 2 * D_CONV
    hb = t // HALO

    def body(zm_ref, zh_ref, gb_ref, w_ref, wb_ref, g_ref, b_ref, u1_ref, u3_ref, ext):
        i = pl.program_id(0)

        def glu(zv):
            ci = zv + gb_ref[...]
            return ci[:, :D_CONV] * jax.nn.sigmoid(ci[:, D_CONV:])

        ext[0, HALO:, :] = glu(zm_ref[...])
        ext[0, 0:HALO, :] = jnp.where(i > 0, glu(zh_ref[...]), 0.0)
        _shifted_copies(ext)
        for rc in range(0, t, CONV_RC):
            for lc in range(0, D_CONV, CONV_LC):
                acc = _dw_taps(ext, w_ref, rc, CONV_RC, lc, CONV_LC, HALO - (CONV_K - 1))
                u1_ref[rc:rc + CONV_RC, lc:lc + CONV_LC] = acc + wb_ref[:, lc:lc + CONV_LC]
        u1 = u1_ref[...]
        mu = jnp.mean(u1, axis=-1, keepdims=True)
        cen = u1 - mu
        var = jnp.mean(cen * cen, axis=-1, keepdims=True)
        u2 = (cen * lax.rsqrt(var + EPS)) * g_ref[...] + b_ref[...]
        u3_ref[...] = _silu(u2).astype(u3_ref.dtype)

    return pl.pallas_call(
        body, name=name, grid=(s // t,),
        in_specs=[_rowspec(t, c2), pl.BlockSpec((HALO, c2), lambda i: (jnp.maximum(i * hb - 1, 0), 0)),
                  _vecspec(c2), pl.BlockSpec((HALO, D_CONV), lambda i: (0, 0)), _vecspec(D_CONV),
                  _vecspec(D_CONV), _vecspec(D_CONV)],
        out_specs=[_rowspec(t, D_CONV), _rowspec(t, D_CONV)],
        out_shape=[_sds((s, D_CONV), F32), _sds((s, D_CONV), MXU_DTYPE)],
        scratch_shapes=[pltpu.VMEM((8, t + HALO, D_CONV), F32)],
        compiler_params=_cp(("parallel",)),
    )(z, z, glu_b, dw_w, dw_b, ln_g, ln_b)


def _gate_cat(o, z, u4m, b_pw, *, name):
    s = o.shape[0]
    t = min(ROW_T, s)

    def body(o_ref, mg_ref, u4_ref, cg_ref, b_ref, cat_ref):
        cat_ref[:, :D_MLA] = (o_ref[...] * _silu(mg_ref[...])).astype(cat_ref.dtype)
        cat_ref[:, D_MLA:] = ((u4_ref[...] + b_ref[...]) * _silu(cg_ref[...])).astype(cat_ref.dtype)

    return pl.pallas_call(
        body, name=name, grid=(s // t,),
        in_specs=[_rowspec(t, D_MLA), _rowspec(t, D_MLA, SEG_MG[0] // D_MLA), _rowspec(t, D_CONV),
                  _rowspec(t, D_CONV, SEG_CG[0] // D_CONV), _vecspec(D_CONV)],
        out_specs=_rowspec(t, D_MLA + D_CONV), out_shape=_sds((s, D_MLA + D_CONV), MXU_DTYPE),
        compiler_params=_cp(("parallel",)),
    )(o, z, u4m, z, b_pw)


def _loss_head(xf, target, *, name):
    s, d = xf.shape
    t = min(ROW_T, s)

    def body(x_ref, t_ref, gx_ref, loss_ref):
        @pl.when(pl.program_id(0) == 0)
        def _():
            loss_ref[...] = jnp.zeros(loss_ref.shape, F32)

        err = x_ref[...] - t_ref[...]
        gx_ref[...] = err * (1.0 / d)
        loss_ref[...] += 0.5 * jnp.sum(_lanesum(err * err) * (1.0 / d), axis=0, keepdims=True)

    return pl.pallas_call(
        body, name=name, grid=(s // t,),
        in_specs=[_rowspec(t, d), _rowspec(t, d)],
        out_specs=[_rowspec(t, d), pl.BlockSpec((1, 1), lambda i: (0, 0))],
        out_shape=[_sds((s, d), F32), _sds((1, 1), F32)],
        compiler_params=_cp(("arbitrary",)),
    )(xf, target)


def _acc_init(refs):
    @pl.when(pl.program_id(0) == 0)
    def _():
        for r in refs:
            r[...] = jnp.zeros(r.shape, r.dtype)


def _out_bwd(gxo, y, gate, *, name):
    s, d = gxo.shape
    t = min(ROW_T, s)

    def body(g_ref, y_ref, gate_ref, dy_ref, dgate_ref):
        _acc_init([dgate_ref])
        gv = g_ref[...]
        dy_ref[...] = (gv * gate_ref[...]).astype(dy_ref.dtype)
        dgate_ref[...] += _colsum(gv * y_ref[...])

    return pl.pallas_call(
        body, name=name, grid=(s // t,),
        in_specs=[_rowspec(t, d), _rowspec(t, d), _vecspec(d)],
        out_specs=[_rowspec(t, d), _vecspec(d)],
        out_shape=[_sds((s, d), MXU_DTYPE), _sds((1, d), F32)],
        compiler_params=_cp(("arbitrary",)),
    )(gxo, y, gate)


def _gate_bwd(dcat, o, z, u4m, b_pw, *, name):
    s = o.shape[0]
    t = min(ROW_T, s)
    gates = D_MLA + D_CONV
    assert SEG_CG[0] == SEG_MG[0] + D_MLA and SEG_MG[0] % gates == 0

    def body(dm_ref, dc_ref, o_ref, mg_ref, u4_ref, cg_ref, b_ref,
             do_ref, delta_ref, du4_ref, gb_ref, dz_ref):
        _acc_init([gb_ref])
        dm, ov, mg = dm_ref[...], o_ref[...], mg_ref[...]
        do = dm * _silu(mg)
        do_ref[...] = do.astype(do_ref.dtype)
        dz_ref[:, :D_MLA] = (dm * ov * _dsilu(mg)).astype(dz_ref.dtype)
        prod = do * ov
        for h in range(N_HEADS):
            delta_ref[h] = _lanesum(prod[:, h * V_DIM:(h + 1) * V_DIM])
        dc, cg = dc_ref[...], cg_ref[...]
        du4 = dc * _silu(cg)
        du4_ref[...] = du4.astype(du4_ref.dtype)
        dz_ref[:, D_MLA:] = (dc * (u4_ref[...] + b_ref[...]) * _dsilu(cg)).astype(dz_ref.dtype)
        gb_ref[...] += _colsum(du4)

    return pl.pallas_call(
        body, name=name, grid=(s // t,),
        in_specs=[_rowspec(t, D_MLA, 0), _rowspec(t, D_CONV, 1), _rowspec(t, D_MLA),
                  _rowspec(t, D_MLA, SEG_MG[0] // D_MLA), _rowspec(t, D_CONV),
                  _rowspec(t, D_CONV, SEG_CG[0] // D_CONV), _vecspec(D_CONV)],
        out_specs=[_rowspec(t, D_MLA), pl.BlockSpec((N_HEADS, t, 1), lambda i: (0, i, 0)),
                   _rowspec(t, D_CONV), _vecspec(D_CONV), _rowspec(t, gates, SEG_MG[0] // gates)],
        out_shape=[_sds((s, D_MLA), MXU_DTYPE), _sds((N_HEADS, s, 1), F32),
                   _sds((s, D_CONV), MXU_DTYPE), _sds((1, D_CONV), F32), _sds((s, IN_PAD), MXU_DTYPE)],
        compiler_params=_cp(("arbitrary",)),
    )(dcat, dcat, o, z, u4m, z, b_pw)


def _conv_bwd(du3, u1, z, dz, glu_b, dw_w, ln_g, ln_b, *, name):
    s = z.shape[0]
    t = min(CONV_T, s)
    c2 = 2 * D_CONV
    hb = t // HALO
    n_blk = s // t
    last_halo = s // HALO - 1

    def body(d3m_ref, d3h_ref, u1m_ref, u1h_ref, zm_ref, zh_ref, gb_ref, w_ref, g_ref, b_ref, dz_in_ref,
             dci_ref, gg_ref, gbn_ref, gwb_ref, ggb_ref, gw_ref, dext, uext, du0_s, gw_acc):
        i = pl.program_id(0)
        _acc_init([gg_ref, gbn_ref, gwb_ref, ggb_ref, gw_acc])

        def ln_bwd(d3, u1v):
            mu = jnp.mean(u1v, axis=-1, keepdims=True)
            cen = u1v - mu
            rstd = lax.rsqrt(jnp.mean(cen * cen, axis=-1, keepdims=True) + EPS)
            uh = cen * rstd
            d2 = d3 * _dsilu(uh * g_ref[...] + b_ref[...])
            dh = d2 * g_ref[...]
            d1 = rstd * (dh - jnp.mean(dh, axis=-1, keepdims=True) - uh * jnp.mean(dh * uh, axis=-1, keepdims=True))
            return d1, d2, uh

        d1, d2, uh = ln_bwd(d3m_ref[...], u1m_ref[...])
        gg_ref[...] += _colsum(d2 * uh)
        gbn_ref[...] += _colsum(d2)
        gwb_ref[...] += _colsum(d1)
        dext[0, 0:t, :] = d1
        d1h, _, _ = ln_bwd(d3h_ref[...], u1h_ref[...])
        dext[0, t:, :] = jnp.where(i < n_blk - 1, d1h, 0.0)
        _shifted_copies(dext)

        def glu_parts(zv):
            ci = zv + gb_ref[...]
            return ci[:, :D_CONV], jax.nn.sigmoid(ci[:, D_CONV:])

        val, sg = glu_parts(zm_ref[...])
        uext[0, HALO:, :] = val * sg
        valh, sgh = glu_parts(zh_ref[...])
        uext[0, 0:HALO, :] = jnp.where(i > 0, valh * sgh, 0.0)
        _shifted_copies(uext)

        for rc in range(0, t, CONV_RC):
            for lc in range(0, D_CONV, CONV_LC):
                acc = None
                dchunk = dext[0, rc:rc + CONV_RC, lc:lc + CONV_LC]
                for k in range(CONV_K):
                    term = w_ref[k:k + 1, lc:lc + CONV_LC] * _window(dext, rc + (CONV_K - 1) - k, CONV_RC, lc, CONV_LC)
                    acc = term if acc is None else acc + term
                    pr = dchunk * _window(uext, rc + HALO - (CONV_K - 1) + k, CONV_RC, lc, CONV_LC)
                    part = pr[0:8]
                    for r8 in range(8, CONV_RC, 8):
                        part = part + pr[r8:r8 + 8]
                    gw_acc[k, :, lc:lc + CONV_LC] += part
                du0_s[rc:rc + CONV_RC, lc:lc + CONV_LC] = acc

        du0 = du0_s[...]
        dval = du0 * sg
        dgt = du0 * val * sg * (1.0 - sg)
        dci_ref[:, :D_CONV] = dval.astype(dci_ref.dtype)
        dci_ref[:, D_CONV:] = dgt.astype(dci_ref.dtype)
        ggb_ref[:, :D_CONV] += _colsum(dval)
        ggb_ref[:, D_CONV:] += _colsum(dgt)

        @pl.when(i == n_blk - 1)
        def _():
            gw_ref[...] = jnp.sum(gw_acc[...], axis=1)

    halo_next = lambda w: pl.BlockSpec((HALO, w), lambda i: (jnp.minimum((i + 1) * hb, last_halo), 0))
    return pl.pallas_call(
        body, name=name, grid=(n_blk,),
        in_specs=[_rowspec(t, D_CONV), halo_next(D_CONV), _rowspec(t, D_CONV), halo_next(D_CONV),
                  _rowspec(t, c2), pl.BlockSpec((HALO, c2), lambda i: (jnp.maximum(i * hb - 1, 0), 0)),
                  _vecspec(c2), pl.BlockSpec((HALO, D_CONV), lambda i: (0, 0)), _vecspec(D_CONV), _vecspec(D_CONV),
                  _ANY],
        out_specs=[_rowspec(t, c2, SEG_CI[0] // c2), _vecspec(D_CONV), _vecspec(D_CONV), _vecspec(D_CONV),
                   _vecspec(c2), pl.BlockSpec((HALO, D_CONV), lambda i: (0, 0))],
        out_shape=[_sds(dz.shape, dz.dtype), _sds((1, D_CONV), F32), _sds((1, D_CONV), F32), _sds((1, D_CONV), F32),
                   _sds((1, c2), F32), _sds((HALO, D_CONV), F32)],
        scratch_shapes=[pltpu.VMEM((8, t + HALO, D_CONV), F32), pltpu.VMEM((8, t + HALO, D_CONV), F32),
                        pltpu.VMEM((t, D_CONV), F32), pltpu.VMEM((HALO, 8, D_CONV), F32)],
        input_output_aliases={10: 0},
        compiler_params=_cp(("arbitrary",)),
    )(du3, du3, u1, u1, z, z, glu_b, dw_w, ln_g, ln_b, dz)


def _flash_bwd(qf, kf, va, do, lse_t, delta_t, *, name):
    nh, s, dk = qf.shape
    dv = va.shape[-1] // 2
    t = min(ATT_T, s)
    n = s // t
    nt = (((1,), (1,)), ((), ()))
    tn = (((0,), (0,)), ((), ()))

    def body(q_ref, do_ref, lse_ref, dl_ref, k_ref, v_ref, dq_ref, dk_ref, dv_ref,
             dk_s, dv_s, st_buf, dpt_buf):
        j = pl.program_id(1)

        @pl.when(j == 0)
        def _():
            dq_ref[...] = jnp.zeros(dq_ref.shape, F32)

        dk_s[...] = jnp.zeros(dk_s.shape, F32)
        dv_s[...] = jnp.zeros(dv_s.shape, F32)
        k, v = k_ref[0], v_ref[0]
        n_un = n - 1 - j

        def rows_of(b):
            return pl.ds(pl.multiple_of((n - 1 - b) * t, t), t)

        def produce(b, slot):
            rows = rows_of(b)
            st_buf[slot] = lax.dot_general(k, q_ref[0, rows, :], nt, preferred_element_type=F32)
            dpt_buf[slot] = lax.dot_general(v, do_ref[rows, :], nt, preferred_element_type=F32)

        def consume(b, slot, masked):
            i = n - 1 - b
            rows = rows_of(b)
            q, dov = q_ref[0, rows, :], do_ref[rows, :]
            pt = jnp.exp(st_buf[slot] - lse_ref[0, i])
            if masked:
                key = lax.broadcasted_iota(jnp.int32, (t, t), 0)
                qry = lax.broadcasted_iota(jnp.int32, (t, t), 1)
                pt = jnp.where(key <= qry, pt, 0.0)
            dv_s[...] += jnp.dot(pt.astype(MXU_DTYPE), dov, preferred_element_type=F32)
            dst = (pt * (dpt_buf[slot] - dl_ref[0, i])).astype(MXU_DTYPE)
            dk_s[...] += jnp.dot(dst, q, preferred_element_type=F32)
            dq_ref[0, rows, :] += lax.dot_general(dst, k, tn, preferred_element_type=F32)

        produce(0, 0)

        def pair(a, carry):
            produce(2 * a + 1, 1)
            consume(2 * a, 0, False)
            produce(2 * a + 2, 0)
            consume(2 * a + 1, 1, False)
            return carry

        lax.fori_loop(0, n_un // 2, pair, 0)

        @pl.when(n_un % 2 == 1)
        def _():
            produce(n_un, 1)
            consume(n_un - 1, 0, False)
            consume(n_un, 1, True)

        @pl.when(n_un % 2 == 0)
        def _():
            consume(n_un, 0, True)

        dk_ref[0] = dk_s[...]
        dv_ref[0] = dv_s[...]

    head = lambda h, j: (h, 0, 0)
    rowv = pl.BlockSpec((1, n, 1, t), lambda h, j: (h, 0, 0, 0))
    return pl.pallas_call(
        body, name=name, grid=(nh, n),
        in_specs=[pl.BlockSpec((1, s, dk), head),
                  pl.BlockSpec((s, dv), lambda h, j: (0, h)),
                  rowv, rowv,
                  pl.BlockSpec((1, t, dk), lambda h, j: (h, j, 0)),
                  pl.BlockSpec((1, t, dv), lambda h, j: (h, j, 0))],
        out_specs=[pl.BlockSpec((1, s, dk), head),
                   pl.BlockSpec((1, t, dk), lambda h, j: (h, j, 0)),
                   pl.BlockSpec((1, t, dv), lambda h, j: (h, j, 0))],
        out_shape=[_sds((nh, s, dk), F32), _sds((nh, s, dk), F32), _sds((nh, s, dv), F32)],
        scratch_shapes=[pltpu.VMEM((t, dk), F32), pltpu.VMEM((t, dv), F32),
                        pltpu.VMEM((2, t, t), F32), pltpu.VMEM((2, t, t), F32)],
        compiler_params=_cp(("parallel", "arbitrary")),
    )(qf, do, lse_t, delta_t, kf, va)


def _qk_bwd(dqf, dkf, dvf, q_raw, kv, z, c_t, s1_t, s2_t, gqn, gqr, gkn, gkr, *, name):
    s = q_raw.shape[0]
    t = min(ROW_T, s)
    scale = 1.0 / math.sqrt(QK_DIM)

    def body(dq_ref, dk_ref, dv_ref, q_ref, kv_ref, kr_ref, c_ref, s1_ref, s2_ref,
             gqn_ref, gqr_ref, gkn_ref, gkr_ref, dqr_ref, dkv_ref, dkr_ref, ggq_ref, ggk_ref):
        _acc_init([ggq_ref, ggk_ref])
        c_v, s1_v, s2_v = c_ref[...], s1_ref[...], s2_ref[...]
        kr = kr_ref[...]
        kr_ss = _lanesum(kr * kr)
        dkr = jnp.zeros(kr.shape, F32)
        ggq_n = ggq_r = ggk_n = ggk_r = jnp.zeros((1, LANE), F32)

        def norm_bwd(n, r, rs, dyn, dyr, gn, gr):
            nh_, rh_ = n * rs, r * rs
            dnh, drh = dyn * gn, dyr * gr
            dot = (_lanesum(dnh * nh_) + _lanesum(drh * rh_)) * (1.0 / QK_DIM)
            return rs * (dnh - nh_ * dot), rs * (drh - rh_ * dot), _colsum(dyn * nh_), _colsum(dyr * rh_)

        for h in range(N_HEADS):
            n = q_ref[:, h * LANE:(h + 1) * LANE]
            r = q_ref[:, N_HEADS * LANE + h * LANE:N_HEADS * LANE + (h + 1) * LANE]
            rs = lax.rsqrt((_lanesum(n * n) + _lanesum(r * r)) * (1.0 / QK_DIM) + EPS)
            dyn = dq_ref[h, :, 0:LANE] * scale
            dyr = _rope_bwd(dq_ref[h, :, LANE:HEAD_PAD] * scale, c_v, s1_v, s2_v)
            dn, dr, g_n, g_r = norm_bwd(n, r, rs, dyn, dyr, gqn_ref[...], gqr_ref[...])
            dqr_ref[:, h * LANE:(h + 1) * LANE] = dn.astype(dqr_ref.dtype)
            dqr_ref[:, N_HEADS * LANE + h * LANE:N_HEADS * LANE + (h + 1) * LANE] = dr.astype(dqr_ref.dtype)
            ggq_n, ggq_r = ggq_n + g_n, ggq_r + g_r

            n = kv_ref[:, h * 2 * LANE:h * 2 * LANE + LANE]
            rs = lax.rsqrt((_lanesum(n * n) + kr_ss) * (1.0 / QK_DIM) + EPS)
            dyn = dk_ref[h, :, 0:LANE]
            dyr = _rope_bwd(dk_ref[h, :, LANE:HEAD_PAD], c_v, s1_v, s2_v)
            dn, dr, g_n, g_r = norm_bwd(n, kr, rs, dyn, dyr, gkn_ref[...], gkr_ref[...])
            dkv_ref[:, h * 2 * LANE:h * 2 * LANE + LANE] = dn.astype(dkv_ref.dtype)
            dkv_ref[:, h * 2 * LANE + LANE:(h + 1) * 2 * LANE] = dv_ref[h].astype(dkv_ref.dtype)
            dkr = dkr + dr
            ggk_n, ggk_r = ggk_n + g_n, ggk_r + g_r

        dkr_ref[...] = dkr.astype(dkr_ref.dtype)
        ggq_ref[:, 0:LANE] += ggq_n
        ggq_ref[:, LANE:] += ggq_r
        ggk_ref[:, 0:LANE] += ggk_n
        ggk_ref[:, LANE:] += ggk_r

    hspec = lambda w: pl.BlockSpec((N_HEADS, t, w), lambda i: (0, i, 0))
    wide = 2 * N_HEADS * LANE
    return pl.pallas_call(
        body, name=name, grid=(s // t,),
        in_specs=[hspec(HEAD_PAD), hspec(HEAD_PAD), hspec(V_DIM), _rowspec(t, wide), _rowspec(t, wide),
                  _rowspec(t, LANE, SEG_KR[0] // LANE), _rowspec(t, LANE), _rowspec(t, LANE), _rowspec(t, LANE),
                  _vecspec(LANE), _vecspec(LANE), _vecspec(LANE), _vecspec(LANE)],
        out_specs=[_rowspec(t, wide), _rowspec(t, wide), _rowspec(t, LANE), _vecspec(2 * LANE), _vecspec(2 * LANE)],
        out_shape=[_sds((s, wide), MXU_DTYPE), _sds((s, wide), MXU_DTYPE), _sds((s, LANE), MXU_DTYPE),
                   _sds((1, 2 * LANE), F32), _sds((1, 2 * LANE), F32)],
        compiler_params=_cp(("arbitrary",)),
    )(dqf, dkf, dvf, q_raw, kv, z, c_t, s1_t, s2_t, gqn, gqr, gkn, gkr)


def _lat_bwd(dqn, dkn, dkr, z, dz, g_ql, g_kvl, *, name):
    s = z.shape[0]
    t = min(ROW_T, s)
    o_ql, o_kvl, o_kr = (seg[0] - SEG_LAT[0] for seg in (SEG_QL, SEG_KVL, SEG_KR))

    def body(dq_ref, dk_ref, dkr_ref, ql_ref, kvl_ref, gq_ref, gk_ref, dz_in_ref, dz_ref, ggq_ref, ggk_ref):
        _acc_init([ggq_ref, ggk_ref])
        for d_ref, src, g_ref, off, gg_ref in ((dq_ref, ql_ref, gq_ref, o_ql, ggq_ref),
                                               (dk_ref, kvl_ref, gk_ref, o_kvl, ggk_ref)):
            v, dy = src[...], d_ref[...]
            r = lax.rsqrt(jnp.mean(v * v, axis=-1, keepdims=True) + EPS)
            vh = v * r
            dvh = dy * g_ref[...]
            dz_ref[:, off:off + v.shape[1]] = (
                r * (dvh - vh * jnp.mean(dvh * vh, axis=-1, keepdims=True))).astype(dz_ref.dtype)
            gg_ref[...] += _colsum(dy * vh)
        dz_ref[:, o_kr:o_kr + LANE] = dkr_ref[...]
        dz_ref[:, o_kr + LANE:] = jnp.zeros((t, SEG_LAT[1] - o_kr - LANE), dz_ref.dtype)

    return pl.pallas_call(
        body, name=name, grid=(s // t,),
        in_specs=[_rowspec(t, Q_LORA), _rowspec(t, KV_LORA), _rowspec(t, LANE),
                  _rowspec(t, Q_LORA, SEG_QL[0] // Q_LORA), _rowspec(t, KV_LORA, SEG_KVL[0] // KV_LORA),
                  _vecspec(Q_LORA), _vecspec(KV_LORA), _ANY],
        out_specs=[_rowspec(t, SEG_LAT[1], SEG_LAT[0] // SEG_LAT[1]), _vecspec(Q_LORA), _vecspec(KV_LORA)],
        out_shape=[_sds(dz.shape, dz.dtype), _sds((1, Q_LORA), F32), _sds((1, KV_LORA), F32)],
        input_output_aliases={7: 0},
        compiler_params=_cp(("arbitrary",)),
    )(dqn, dkn, dkr, z, z, g_ql, g_kvl, dz)


def _prenorm_bwd(dh, x, gxo, g, sc1p, *, name):
    s, d = x.shape
    t = min(ROW_T, s)

    def body(dh_ref, x_ref, gx_ref, g_ref, sc_ref, dx_ref, dsh_ref, dsc_ref, gg_ref):
        _acc_init([dsh_ref, dsc_ref, gg_ref])
        xv, dhv = x_ref[...], dh_ref[...]
        r = lax.rsqrt(jnp.mean(xv * xv, axis=-1, keepdims=True) + EPS)
        xn = xv * r
        dsh_ref[...] += _colsum(dhv)
        dsc_ref[...] += _colsum(dhv * (xn * g_ref[...]))
        dm = dhv * sc_ref[...]
        gg_ref[...] += _colsum(dm * xn)
        dxn = dm * g_ref[...]
        dx_ref[...] = gx_ref[...] + r * (dxn - xn * jnp.mean(dxn * xn, axis=-1, keepdims=True))

    return pl.pallas_call(
        body, name=name, grid=(s // t,),
        in_specs=[_rowspec(t, d), _rowspec(t, d), _rowspec(t, d), _vecspec(d), _vecspec(d)],
        out_specs=[_rowspec(t, d), _vecspec(d), _vecspec(d), _vecspec(d)],
        out_shape=[_sds((s, d), F32), _sds((1, d), F32), _sds((1, d), F32), _sds((1, d), F32)],
        compiler_params=_cp(("arbitrary",)),
    )(dh, x, gxo, g, sc1p)


def _ada_fwd(c_all, ada_w, ada_b_cols, *, name):
    nl, d, cols = ada_w.shape

    def body(c_ref, w_ref, b_ref, o_ref):
        ca = _silu(c_ref[...]).astype(MXU_DTYPE)
        o_ref[0] = jnp.dot(ca, w_ref[0].astype(MXU_DTYPE), preferred_element_type=F32) + b_ref[0]

    return pl.pallas_call(
        body, name=name, grid=(nl,),
        in_specs=[pl.BlockSpec((N_DEV, d), lambda l: (0, 0)), pl.BlockSpec((1, d, cols), lambda l: (l, 0, 0)),
                  pl.BlockSpec((1, 1, cols), lambda l: (l, 0, 0))],
        out_specs=pl.BlockSpec((1, N_DEV, cols), lambda l: (l, 0, 0)),
        out_shape=_sds((nl, N_DEV, cols), F32),
        compiler_params=_cp(("parallel",)),
    )(c_all, ada_w, ada_b_cols)


def _ada_bwd(c_all_t, dmod_cols, *, name):
    nl, _, cols = dmod_cols.shape
    d = c_all_t.shape[0]

    def body(c_ref, dm_ref, o_ref):
        ca = _silu(c_ref[...]).astype(MXU_DTYPE)
        o_ref[0] = jnp.dot(ca, dm_ref[0].astype(MXU_DTYPE), preferred_element_type=F32)

    return pl.pallas_call(
        body, name=name, grid=(nl,),
        in_specs=[pl.BlockSpec((d, N_DEV), lambda l: (0, 0)), pl.BlockSpec((1, N_DEV, cols), lambda l: (l, 0, 0))],
        out_specs=pl.BlockSpec((1, d, cols), lambda l: (l, 0, 0)),
        out_shape=_sds((nl, d, cols), F32),
        compiler_params=_cp(("parallel",)),
    )(c_all_t, dmod_cols)


def _adamw(gparts, w, m, v, *, name):
    shape = w.shape
    cols = shape[-1]
    per_layer = isinstance(gparts, (list, tuple))
    nl = shape[0] if per_layer else 1
    rows = w.size // cols // nl
    glist = list(gparts) if per_layer else [gparts]
    npart = glist[0].shape[0]
    glist = [g.reshape(npart, rows, cols) for g in glist]
    w3, m3, v3 = (a.reshape(nl, rows, cols) for a in (w, m, v))
    budget = 2 * 1024 * 1024
    fits = [t for t in range(min(rows, 256) // 8 * 8, 7, -8)
            if rows % t == 0 and npart * t * cols * glist[0].dtype.itemsize <= budget]
    t = fits[0] if fits else rows
    nb = rows // t

    def body(*refs):
        g_refs = refs[:nl]
        w_ref, m_ref, v_ref, go_ref, d_ref, mo_ref, vo_ref, g_s = refs[nl:]
        layer = pl.program_id(0)
        for l in range(nl):
            @pl.when(layer == l)
            def _(l=l):
                g = g_refs[l][0].astype(F32)
                for p in range(1, npart):
                    g = g + g_refs[l][p].astype(F32)
                g_s[...] = g

        g = g_s[...]
        mn = ADAM_B1 * m_ref[0] + (1.0 - ADAM_B1) * g
        vn = ADAM_B2 * v_ref[0] + (1.0 - ADAM_B2) * (g * g)
        m_hat = mn / (1.0 - ADAM_B1 ** ADAM_STEP)
        v_hat = vn / (1.0 - ADAM_B2 ** ADAM_STEP)
        go_ref[0] = g
        d_ref[0] = -ADAM_LR * (m_hat / (jnp.sqrt(v_hat) + ADAM_EPS) + ADAM_WD * w_ref[0])
        mo_ref[0] = mn
        vo_ref[0] = vn

    def g_map(l):
        return lambda layer, i: (0, jnp.where(layer == l, i, jnp.where(layer < l, 0, nb - 1)), 0)

    spec = pl.BlockSpec((1, t, cols), lambda layer, i: (layer, i, 0))
    outs = pl.pallas_call(
        body, name=name, grid=(nl, nb),
        in_specs=[pl.BlockSpec((npart, t, cols), g_map(l)) for l in range(nl)] + [spec, spec, spec],
        out_specs=[spec] * 4, out_shape=[_sds((nl, rows, cols), F32)] * 4,
        scratch_shapes=[pltpu.VMEM((t, cols), F32)],
        compiler_params=_cp(("arbitrary", "arbitrary")),
    )(*glist, w3, m3, v3)
    return tuple(o.reshape(shape) for o in outs)


_ANY = pl.BlockSpec(memory_space=pl.ANY)


def _all_gather(blocks, *, name):
    na = len(blocks)

    def body(*refs):
        x_refs, out_refs = refs[:na], refs[na:2 * na]
        send_sems, recv_sems, local_sems = refs[2 * na:]
        x, y, c = lax.axis_index("x"), lax.axis_index("y"), lax.axis_index("c")
        me, sibling = (x, y, c), (x, y, 1 - c)
        chips = [(1 - x, y), (x, 1 - y), (1 - x, 1 - y)]

        def slot(a, px, py, pc):
            return out_refs[a].at[4 * px + 2 * py + pc]

        def copy(a, k, blk, to, src=None):
            return pltpu.make_async_remote_copy(
                src_ref=slot(a, *blk) if src is None else src, dst_ref=slot(a, *blk),
                send_sem=send_sems.at[7 * a + k], recv_sem=recv_sems.at[7 * a + k],
                device_id=to, device_id_type=MESH_ID)

        mine = [pltpu.make_async_copy(x_refs[a], slot(a, *me), local_sems.at[a]) for a in range(na)]
        for cp in mine:
            cp.start()
        first = []
        for a in range(na):
            first.append(copy(a, 0, me, sibling, src=x_refs[a]))
            first += [copy(a, 1 + j, me, (*chip, c), src=x_refs[a]) for j, chip in enumerate(chips)]
        for cp in first:
            cp.start()
        passed = []
        for a in range(na):
            for j, chip in enumerate(chips):
                copy(a, 1 + j, (*chip, c), me).wait_recv()
                fwd = copy(a, 4 + j, (*chip, c), sibling)
                fwd.start()
                passed.append(fwd)
        for a in range(na):
            copy(a, 0, sibling, me).wait_recv()
            for j, chip in enumerate(chips):
                copy(a, 4 + j, (*chip, 1 - c), me).wait_recv()
        for cp in first + passed:
            cp.wait_send()
        for cp in mine:
            cp.wait()

    outs = pl.pallas_call(
        body, name=name, in_specs=[_ANY] * na, out_specs=[_ANY] * na,
        out_shape=[_sds((N_DEV,) + b.shape, b.dtype) for b in blocks],
        scratch_shapes=[pltpu.SemaphoreType.DMA((7 * na,)), pltpu.SemaphoreType.DMA((7 * na,)),
                        pltpu.SemaphoreType.DMA((na,))],
    )(*blocks)
    return list(outs)


_HBM = pl.BlockSpec(memory_space=pltpu.HBM)
_SEM = pl.BlockSpec(memory_space=pltpu.SEMAPHORE)
_EFFECT = pltpu.SideEffectType.DATAFLOW_SIDE_EFFECTING


def _peers(x, y, c):
    out = []
    for k in range(1, N_DEV):
        out.append((1 - x if k & 4 else x, 1 - y if k & 2 else y, 1 - c if k & 1 else c))
    return out


def _own_slots(srcs, scatter, *, name, after=None):
    na = len(srcs)
    n_extra = 0 if after is None else 1
    me = (4 * lax.axis_index("x") + 2 * lax.axis_index("y") + lax.axis_index("c")).astype(jnp.int32).reshape(1)

    def body(me_ref, *refs):
        in_refs, out_refs = refs[:na], refs[na + n_extra:]
        for a in range(na):
            out_refs[a][0] = in_refs[a][0] if scatter else in_refs[a][...]

    def slot_spec(shard):
        zeros = (0,) * len(shard)
        return pl.BlockSpec((1,) + tuple(shard), lambda i, me_ref: (me_ref[0],) + zeros)

    def whole_spec(shape):
        zeros = (0,) * len(shape)
        return pl.BlockSpec(tuple(shape), lambda i, me_ref: zeros)

    shards = [s.shape[1:] if scatter else s.shape for s in srcs]
    in_specs = [slot_spec(sh) if scatter else whole_spec(sh) for sh in shards] + [_ANY] * n_extra
    outs = pl.pallas_call(
        body, name=name,
        grid_spec=pltpu.PrefetchScalarGridSpec(
            num_scalar_prefetch=1, grid=(1,), in_specs=in_specs, out_specs=[slot_spec(sh) for sh in shards]),
        out_shape=[_sds((N_DEV,) + tuple(sh), s.dtype) for sh, s in zip(shards, srcs)],
        compiler_params=_cp(("arbitrary",)),
    )(me, *srcs, *([] if after is None else [after]))
    return list(outs)


def _exchange_copies(src_refs, land_refs, send_sems, recv_sems, scatter):
    x, y, c = lax.axis_index("x"), lax.axis_index("y"), lax.axis_index("c")
    me = 4 * x + 2 * y + c
    cps = []
    for a in range(len(src_refs)):
        for k, (px, py, pc) in enumerate(_peers(x, y, c)):
            src = src_refs[a].at[4 * px + 2 * py + pc] if scatter else src_refs[a]
            cps.append(pltpu.make_async_remote_copy(
                src_ref=src, dst_ref=land_refs[a].at[me], send_sem=send_sems.at[7 * a + k],
                recv_sem=recv_sems.at[7 * a + k], device_id=(px, py, pc), device_id_type=MESH_ID))
    return cps


def _exchange_start(srcs, lands, scatter, *, name):
    na = len(srcs)

    def body(*refs):
        src_refs, land_refs = refs[:na], refs[na:2 * na]
        send_sems, recv_sems = refs[2 * na], refs[2 * na + 1]
        token = refs[-1]
        for cp in _exchange_copies(src_refs, land_refs, send_sems, recv_sems, scatter):
            cp.start()
        token[...] = jnp.zeros(token.shape, token.dtype)

    hbm = lambda a: pltpu.HBM(a.shape, a.dtype)
    outs = pl.pallas_call(
        body, name=name,
        out_shape=(pltpu.SemaphoreType.DMA((7 * na,)), pltpu.SemaphoreType.DMA((7 * na,)),
                   *[hbm(a) for a in srcs], *[hbm(a) for a in lands], _sds((8, LANE), F32)),
        in_specs=[_HBM] * (2 * na),
        out_specs=(_SEM, _SEM, *[_HBM] * (2 * na), pl.BlockSpec(memory_space=pltpu.VMEM)),
        input_output_aliases={i: 2 + i for i in range(2 * na)},
        compiler_params=pltpu.CompilerParams(has_side_effects=_EFFECT),
    )(*[pltpu.with_memory_space_constraint(a, pltpu.HBM) for a in list(srcs) + list(lands)])
    return outs[0], outs[1], list(outs[2:2 + na]), list(outs[2 + na:2 + 2 * na]), outs[-1]


def _exchange_wait(send_sems, recv_sems, srcs, lands, after, scatter, *, name):
    na = len(srcs)

    def body(*refs):
        src_refs, land_refs = refs[:na], refs[na:2 * na]
        s_sems, r_sems = refs[2 * na], refs[2 * na + 1]
        for cp in _exchange_copies(src_refs, land_refs, s_sems, r_sems, scatter):
            cp.wait_send()
            cp.wait_recv()

    hbm = lambda a: pltpu.HBM(a.shape, a.dtype)
    outs = pl.pallas_call(
        body, name=name,
        out_shape=(*[hbm(a) for a in srcs], *[hbm(a) for a in lands]),
        in_specs=[_HBM] * (2 * na) + [_SEM, _SEM, _ANY],
        out_specs=tuple([_HBM] * (2 * na)),
        input_output_aliases={i: i for i in range(2 * na)},
        compiler_params=pltpu.CompilerParams(has_side_effects=_EFFECT),
    )(*srcs, *lands, send_sems, recv_sems, after)
    return list(outs[na:])


_WIN_SEGS = (("ql", 0, Q_LORA, SEG_QL[0]), ("kvl", Q_LORA, KV_LORA, SEG_KVL[0]),
             ("kr", Q_LORA + KV_LORA, ROPE, SEG_KR[0]), ("mg", Q_LORA + KV_LORA + ROPE, D_MLA, SEG_MG[0]),
             ("ci", Q_LORA + KV_LORA + ROPE + D_MLA, 2 * D_CONV, SEG_CI[0]),
             ("cg", Q_LORA + KV_LORA + ROPE + D_MLA + 2 * D_CONV, D_CONV, SEG_CG[0]))
_WIN_SHARD = IN_COLS // N_DEV


def _win_pieces():
    out = []
    for _, o, n, new in _WIN_SEGS:
        for j in range(N_DEV):
            lo, hi = max(o, j * _WIN_SHARD), min(o + n, (j + 1) * _WIN_SHARD)
            if lo < hi:
                out.append((j, lo - j * _WIN_SHARD, new + lo - o, hi - lo))
    return out


WIN_T = 512


def _win_assemble(w_all, *, name):
    d = w_all.shape[2]
    t = min(WIN_T, d)
    pieces = sorted(_win_pieces(), key=lambda p: p[2])
    assert all(lo % 8 == 0 and n % 8 == 0 for _, lo, _, n in pieces)

    def body(w_ref, o_ref):
        rows = [w_ref[j].astype(F32)[lo:lo + n, :] for j, lo, _, n in pieces]
        rows.append(jnp.zeros((IN_PAD - (SEG_KR[0] + ROPE), t), F32))
        o_ref[...] = jnp.concatenate(rows, axis=0).astype(o_ref.dtype)

    return pl.pallas_call(
        body, name=name, grid=(d // t,),
        in_specs=[pl.BlockSpec((N_DEV, _WIN_SHARD, t), lambda i: (0, 0, i))],
        out_specs=pl.BlockSpec((IN_PAD, t), lambda i: (0, i)), out_shape=_sds((IN_PAD, d), w_all.dtype),
        compiler_params=_cp(("parallel",)),
    )(w_all)


def _win_split(grad, *, name):
    d = grad.shape[1]
    t = min(WIN_T, d)
    by_shard = [sorted([p for p in _win_pieces() if p[0] == j], key=lambda p: p[1]) for j in range(N_DEV)]

    def body(g_ref, o_ref):
        for j in range(N_DEV):
            rows = [g_ref[new:new + n, :] for _, _, new, n in by_shard[j]]
            o_ref[j] = jnp.concatenate(rows, axis=0).astype(o_ref.dtype)

    return pl.pallas_call(
        body, name=name, grid=(d // t,),
        in_specs=[pl.BlockSpec((IN_PAD, t), lambda i: (0, i))],
        out_specs=pl.BlockSpec((N_DEV, _WIN_SHARD, t), lambda i: (0, 0, i)),
        out_shape=_sds((N_DEV, _WIN_SHARD, d), WIRE_DTYPE),
        compiler_params=_cp(("parallel",)),
    )(grad)


def _cols_to_shards(a):
    r, n = a.shape
    return a.reshape(r, N_DEV, n // N_DEV).transpose(1, 0, 2)


def _shards_to_cols(a):
    nd, r, w = a.shape
    return a.transpose(1, 0, 2).reshape(r, nd * w)


def _win_permute(w_in):
    o_ql, o_kvl, o_kr, o_mg = 0, Q_LORA, Q_LORA + KV_LORA, Q_LORA + KV_LORA + ROPE
    o_ci = o_mg + D_MLA
    o_cg = o_ci + 2 * D_CONV
    seg = lambda o, n: w_in[:, o:o + n]
    pad = jnp.zeros((w_in.shape[0], IN_PAD - (SEG_KR[0] + ROPE)), w_in.dtype)
    return jnp.concatenate([seg(o_ci, 2 * D_CONV), seg(o_mg, D_MLA), seg(o_cg, D_CONV), seg(o_ql, Q_LORA),
                            seg(o_kvl, KV_LORA), seg(o_kr, ROPE), pad], axis=1)


def _win_unpermute(g):
    seg = lambda s, n=None: g[:, s[0]:s[0] + (s[1] if n is None else n)]
    return jnp.concatenate([seg(SEG_QL), seg(SEG_KVL), seg(SEG_KR, ROPE), seg(SEG_MG), seg(SEG_CI), seg(SEG_CG)], axis=1)


def _qup_permute(w):
    w3 = w.reshape(w.shape[0], N_HEADS, QK_DIM)
    nope = w3[:, :, :NOPE].reshape(w.shape[0], N_HEADS * NOPE)
    rope = jnp.pad(w3[:, :, NOPE:], ((0, 0), (0, 0), (0, LANE - ROPE))).reshape(w.shape[0], N_HEADS * LANE)
    return jnp.concatenate([nope, rope], axis=1)


def _qup_unpermute(g):
    r = g.shape[0]
    nope = g[:, :N_HEADS * NOPE].reshape(r, N_HEADS, NOPE)
    rope = g[:, N_HEADS * NOPE:].reshape(r, N_HEADS, LANE)[:, :, :ROPE]
    return jnp.concatenate([nope, rope], axis=2).reshape(r, N_HEADS * QK_DIM)


def _norm_tiles(g):
    return g[:NOPE].reshape(1, LANE), jnp.pad(g[NOPE:], (0, LANE - ROPE)).reshape(1, LANE)


def _norm_untile(gt):
    return jnp.concatenate([gt[0, :NOPE], gt[0, LANE:LANE + ROPE]])


def _rope_tiles(positions):
    inv_freq = 1.0 / (ROPE_THETA ** (jnp.arange(0, ROPE, 2, dtype=F32) / ROPE))
    ang = positions.astype(F32)[:, None] * inv_freq
    cos, sin = jnp.cos(ang), jnp.sin(ang)
    zq = jnp.zeros_like(cos)
    c_t = jnp.concatenate([cos, cos, zq, zq], axis=1)
    s1_t = jnp.concatenate([-sin, zq, zq, zq], axis=1)
    s2_t = jnp.concatenate([zq, sin, zq, zq], axis=1)
    return c_t, s1_t, s2_t


_BIG = ("w_in", "w_q_up", "w_kv_up", "w_pw", "w_out")
_COL_SHARDED = ("w_in", "w_q_up", "w_kv_up")


def _pack_rows(arrs):
    return jnp.concatenate([a.reshape(-1, LANE) for a in arrs], axis=0)


def _unpack_rows(buf, shapes):
    out, r0 = [], 0
    lead = buf.shape[:-2]
    for shp in shapes:
        n = math.prod(shp) // LANE
        out.append(buf[..., r0:r0 + n, :].reshape(lead + tuple(shp)))
        r0 += n
    return out


_SMALL = (("dmod", 3 * D_MODEL), ("norm_g", D_MODEL), ("q_lat_g", Q_LORA), ("kv_lat_g", KV_LORA),
          ("q_norm_g", 2 * LANE), ("k_norm_g", 2 * LANE), ("glu_b", 2 * D_CONV), ("dw_w", HALO * D_CONV),
          ("dw_b", D_CONV), ("conv_ln_g", D_CONV), ("conv_ln_b", D_CONV), ("b_pw", D_CONV))


def _layer_fwd(x, p, rope, l, late=None):
    n = lambda s: f"{s}_l{l}"
    c_t, s1_t, s2_t = rope
    h = _prenorm(x, p["norm_g"], p["shift"], p["sc1p"], name=n("prenorm"))
    z = _mm(h, p["w_in"], tb=True, name=n("in_proj"), tn=IN_TILE, n_outer=True)
    if late is not None:
        p = {**p, **late(z)}
    qn, kn = _lat_norm(z, p["q_lat_g"], p["kv_lat_g"], name=n("lat_norm"))
    q_raw = _mm(qn, p["w_q_up"], name=n("q_up"), tn=1024)
    kv = _mm(kn, p["w_kv_up"], name=n("kv_up"), tn=1024)
    qf, kf, vf = _qk_prep(q_raw, kv, z, c_t, s1_t, s2_t, *p["qk_tiles"], name=n("qk_prep"))
    o, lse = _flash_fwd(qf, kf, vf, name=n("flash_fwd"))
    u1, u3 = _conv_fwd(z, p["glu_b"], p["dw_w"], p["dw_b"], p["conv_ln_g"], p["conv_ln_b"], name=n("conv_fwd"))
    u4m = _mm(u3, p["w_pw"], name=n("pw"), tn=1024)
    cat = _gate_cat(o, z, u4m, p["b_pw"], name=n("gate_cat"))
    y, x_next = _mm(cat, p["w_out"], name=n("out_proj"), tn=1024, residual=(x, p["gate"]))
    saved = dict(x=x, h=h, z=z, qn=qn, kn=kn, q_raw=q_raw, kv=kv, qf=qf, kf=kf, vf=vf, o=o, lse=lse,
                 u1=u1, u3=u3, u4m=u4m, cat=cat, y=y)
    return x_next, saved, p


def _layer_bwd(gxo, p, sv, rope, l, hook_rest=None, hook_w_in=None):
    n = lambda s: f"{s}_l{l}"
    c_t, s1_t, s2_t = rope
    z = sv["z"]
    dy, dgate = _out_bwd(gxo, sv["y"], p["gate"], name=n("out_bwd"))
    g_w_out = _mm(sv["cat"], dy, ta=True, name=n("g_w_out"), tm=1024, tn=1024)
    dcat = _mm(dy, p["w_out"], tb=True, name=n("d_cat"), tn=1024)
    do, delta, du4, g_b_pw, dz = _gate_bwd(dcat, sv["o"], z, sv["u4m"], p["b_pw"], name=n("gate_bwd"))
    g_w_pw = _mm(sv["u3"], du4, ta=True, name=n("g_w_pw"), tm=1024, tn=1024, tk=512)
    du3 = _mm(du4, p["w_pw"], tb=True, name=n("d_u3"), tn=1024)
    dz, g_ln_g, g_ln_b, g_dw_b, g_glu_b, g_dw_w = _conv_bwd(
        du3, sv["u1"], z, dz, p["glu_b"], p["dw_w"], p["conv_ln_g"], p["conv_ln_b"], name=n("conv_bwd"))
    t_att = min(ATT_T, z.shape[0])
    to_lanes = lambda a: a.reshape(N_HEADS, z.shape[0] // t_att, 1, t_att)
    dqf, dkf, dvf = _flash_bwd(sv["qf"], sv["kf"], sv["vf"], do,
                               to_lanes(sv["lse"][:, :, 0]), to_lanes(delta), name=n("flash_bwd"))
    dq_raw, dkv, dkr, g_qn, g_kn = _qk_bwd(dqf, dkf, dvf, sv["q_raw"], sv["kv"], z, c_t, s1_t, s2_t,
                                            *p["qk_tiles"], name=n("qk_bwd"))
    g_w_q_up = _mm(sv["qn"], dq_raw, ta=True, name=n("g_w_q_up"), tm=512, tn=1024, tk=512)
    dqn = _mm(dq_raw, p["w_q_up"], tb=True, name=n("d_qn"))
    g_w_kv_up = _mm(sv["kn"], dkv, ta=True, name=n("g_w_kv_up"), tm=256, tn=1024, tk=512)
    dkn = _mm(dkv, p["w_kv_up"], tb=True, name=n("d_kn"))
    dz, g_ql, g_kvl = _lat_bwd(dqn, dkn, dkr, z, dz, p["q_lat_g"], p["kv_lat_g"], name=n("lat_bwd"))
    big = dict(w_q_up=g_w_q_up, w_kv_up=g_w_kv_up, w_pw=g_w_pw, w_out=g_w_out)
    after = None if hook_rest is None else hook_rest(big)
    g_w_in = _mm(dz, sv["h"], ta=True, name=n("g_w_in"), tm=512, tn=1024, after=after)
    big["w_in"] = g_w_in
    after = None if hook_w_in is None else hook_w_in(g_w_in)
    dh = _mm(dz, p["w_in"], name=n("d_h"), tn=1024, after=after)
    dx, dshift, dscale, g_norm = _prenorm_bwd(dh, sv["x"], gxo, p["norm_g"], p["sc1p"], name=n("prenorm_bwd"))
    small = dict(dmod=jnp.concatenate([dshift, dscale, dgate], axis=1), norm_g=g_norm, q_lat_g=g_ql, kv_lat_g=g_kvl,
                 q_norm_g=g_qn, k_norm_g=g_kn, glu_b=g_glu_b, dw_w=g_dw_w, dw_b=g_dw_b,
                 conv_ln_g=g_ln_g, conv_ln_b=g_ln_b, b_pw=g_b_pw)
    return dx, big, small


def _layer_params(l, full, mod_l, small):
    d = D_MODEL
    row = lambda a: a.reshape(1, -1)
    shift, scale, gate = mod_l[:, :d], mod_l[:, d:2 * d], mod_l[:, 2 * d:]
    dw_w = jnp.pad(full["dw_w"][l], ((0, HALO - CONV_K), (0, 0)))
    return dict(
        shift=shift, sc1p=1.0 + scale, gate=gate, norm_g=row(small["norm_g"][l]),
        **{k: full[k][l] for k in _BIG if k in full}, dw_w=dw_w,
        q_lat_g=row(small["q_lat_g"][l]), kv_lat_g=row(small["kv_lat_g"][l]),
        qk_tiles=_norm_tiles(small["q_norm_g"][l]) + _norm_tiles(small["k_norm_g"][l]),
        glu_b=row(small["glu_b"][l]), dw_b=row(small["dw_b"][l]), conv_ln_g=row(small["conv_ln_g"][l]),
        conv_ln_b=row(small["conv_ln_b"][l]), b_pw=row(small["b_pw"][l]))


def kernel(x, c, positions, ada_w, ada_b, norm_g, w_in, q_lat_g, w_q_up, kv_lat_g, w_kv_up, q_norm_g, k_norm_g, glu_b, dw_w, dw_b, conv_ln_g, conv_ln_b, w_pw, b_pw, w_out, loss_target, m_ada_w, m_ada_b, m_norm_g, m_w_in, m_q_lat_g, m_w_q_up, m_kv_lat_g, m_w_kv_up, m_q_norm_g, m_k_norm_g, m_glu_b, m_dw_w, m_dw_b, m_conv_ln_g, m_conv_ln_b, m_w_pw, m_b_pw, m_w_out, v_ada_w, v_ada_b, v_norm_g, v_w_in, v_q_lat_g, v_w_q_up, v_kv_lat_g, v_w_kv_up, v_q_norm_g, v_k_norm_g, v_glu_b, v_dw_w, v_dw_b, v_conv_ln_g, v_conv_ln_b, v_w_pw, v_b_pw, v_w_out):
    names = ("ada_w", "ada_b", "norm_g", "w_in", "q_lat_g", "w_q_up", "kv_lat_g", "w_kv_up", "q_norm_g",
             "k_norm_g", "glu_b", "dw_w", "dw_b", "conv_ln_g", "conv_ln_b", "w_pw", "b_pw", "w_out")
    w_loc = dict(zip(names, (ada_w, ada_b, norm_g, w_in, q_lat_g, w_q_up, kv_lat_g, w_kv_up, q_norm_g, k_norm_g,
                             glu_b, dw_w, dw_b, conv_ln_g, conv_ln_b, w_pw, b_pw, w_out)))
    m_loc = dict(zip(names, (m_ada_w, m_ada_b, m_norm_g, m_w_in, m_q_lat_g, m_w_q_up, m_kv_lat_g, m_w_kv_up,
                             m_q_norm_g, m_k_norm_g, m_glu_b, m_dw_w, m_dw_b, m_conv_ln_g, m_conv_ln_b, m_w_pw,
                             m_b_pw, m_w_out)))
    v_loc = dict(zip(names, (v_ada_w, v_ada_b, v_norm_g, v_w_in, v_q_lat_g, v_w_q_up, v_kv_lat_g, v_w_kv_up,
                             v_q_norm_g, v_k_norm_g, v_glu_b, v_dw_w, v_dw_b, v_conv_ln_g, v_conv_ln_b, v_w_pw,
                             v_b_pw, v_w_out)))
    nl, d = N_LAYERS, D_MODEL
    me = 4 * lax.axis_index("x") + 2 * lax.axis_index("y") + lax.axis_index("c")
    x2, tgt = x[0], loss_target[0]
    ada_cols = ada_w.shape[-1]

    c_all = _all_gather([c.reshape(d // LANE, LANE)], name="gather_c")[0].reshape(N_DEV, d)
    ada_b_cols = lax.dynamic_slice_in_dim(ada_b, me * ada_cols, ada_cols, axis=1).reshape(nl, 1, ada_cols)
    mod_cols = _ada_fwd(c_all, ada_w, ada_b_cols, name="ada_fwd")
    mod_all = _all_gather([mod_cols], name="gather_mod")[0]
    mod_me = lax.dynamic_index_in_dim(mod_all, me, axis=2, keepdims=False)
    mod = mod_me.transpose(1, 0, 2).reshape(nl, 1, N_DEV * ada_cols)

    dw_pad = jnp.pad(dw_w, ((0, 0), (0, HALO - CONV_K), (0, 0)))
    tr = lambda a: jnp.swapaxes(a, 1, 2)
    w_loc, m_loc, v_loc = ({**dd, "w_in": tr(dd["w_in"])} for dd in (w_loc, m_loc, v_loc))
    wire = {k: w_loc[k].astype(WIRE_DTYPE) for k in _BIG}
    w_in_all0, dw_all = _all_gather([wire["w_in"][0], dw_pad], name="gather_w_in_l0")
    rest0 = [wire[k][0] for k in _BIG[1:]]
    fly_r0 = _exchange_start(rest0, _own_slots(rest0, False, name="own_weights_l0_rest", after=w_in_all0), False,
                             name="gather_start_l0_rest")
    fly_w1 = {}

    def layout_rest(parts):
        return dict(w_q_up=_qup_permute(_shards_to_cols(parts[0])), w_kv_up=_shards_to_cols(parts[1]),
                    w_pw=parts[2].reshape(D_CONV, D_CONV), w_out=parts[3].reshape(D_MLA + D_CONV, d))

    small_in = dict(norm_g=norm_g, q_lat_g=q_lat_g, kv_lat_g=kv_lat_g, q_norm_g=q_norm_g, k_norm_g=k_norm_g,
                    glu_b=glu_b, dw_b=dw_b, conv_ln_g=conv_ln_g, conv_ln_b=conv_ln_b, b_pw=b_pw)
    dw_full = [_shards_to_cols(dw_all[:, l])[:CONV_K] for l in range(nl)]
    rope = _rope_tiles(positions[0])

    def layer_params(l, w_in_all, rest, mod_l):
        full = dict(w_in={l: _win_assemble(w_in_all, name=f"w_in_assemble_l{l}")}, dw_w=dw_full)
        if rest is not None:
            full.update({k: {l: a} for k, a in layout_rest(rest).items()})
        return _layer_params(l, full, mod_l, small_in)

    def late_l0(z):
        parts = _exchange_wait(*fly_r0[:4], z, False, name="gather_wait_l0_rest")
        src1 = [wire[k][1] for k in _BIG]
        fly_w1["x"] = _exchange_start(src1, _own_slots(src1, False, name="own_weights_l1", after=parts[0]), False,
                                      name="gather_start_l1")
        late = layout_rest(parts)
        late["q_lat_g"] = small_in["q_lat_g"][0].reshape(1, -1) + fly_w1["x"][4][0, 0]
        return late

    params, saved = [None] * nl, [None] * nl
    p0 = layer_params(0, w_in_all0, None, mod[0] + fly_r0[4][0, 0])
    xs, saved[0], params[0] = _layer_fwd(x2, p0, rope, 0, late=late_l0)
    parts1 = _exchange_wait(*fly_w1["x"][:4], xs, False, name="gather_wait_l1")
    params[1] = layer_params(1, parts1[0], parts1[1:], mod[1])
    xs, saved[1], _ = _layer_fwd(xs, params[1], rope, 1)
    gx, loss_part = _loss_head(xs, tgt, name="loss_head")
    loss = lax.psum(loss_part[0, 0], ("x", "y", "c"))

    def shard_major(k, g):
        if k == "w_q_up":
            g = _qup_unpermute(g)
        if k in _COL_SHARDED:
            return _cols_to_shards(g)
        return g.reshape((N_DEV, g.shape[0] // N_DEV, g.shape[1]))

    def scatter_start(send, tag):
        lands = _own_slots(send, True, name=f"own_grads_{tag}")
        return _exchange_start(send, lands, True, name=f"scatter_start_{tag}")

    def wire_rest(big):
        return [shard_major(k, big[k]).astype(WIRE_DTYPE) for k in _BIG[1:]]

    big_g, small_g, flying = [None] * nl, [None] * nl, {}
    gx, big_g[1], small_g[1] = _layer_bwd(gx, params[1], saved[1], rope, 1)
    flying["l1"] = scatter_start([_win_split(big_g[1]["w_in"], name="w_in_split_l1")] + wire_rest(big_g[1]), "l1")
    p0 = dict(params[0])
    p0["gate"] = p0["gate"] + flying["l1"][4][0, 0]

    def start_rest_l0(big):
        flying["l0_rest"] = scatter_start(wire_rest(big), "l0_rest")
        return flying["l0_rest"][4]

    def start_w_in_l0(g_w_in):
        flying["l0_w_in"] = scatter_start([_win_split(g_w_in, name="w_in_split_l0")], "l0_w_in")
        return flying["l0_w_in"][4]

    gx, big_g[0], small_g[0] = _layer_bwd(gx, p0, saved[0], rope, 0, hook_rest=start_rest_l0,
                                          hook_w_in=start_w_in_l0)

    tile = 8 * LANE
    padded = [(k, nn, -(-nn // tile) * tile) for k, nn in _SMALL]
    spk = jnp.concatenate([jnp.pad(small_g[l][k].reshape(-1), (0, np_ - nn)).reshape(-1, LANE)
                           for l in range(nl) for k, nn, np_ in padded], axis=0)
    s_all = _all_gather([spk], name="gather_small_grads")[0]
    s_rows = sum(np_ for _, _, np_ in padded) // LANE
    s_all = s_all.reshape(N_DEV, nl, s_rows, LANE)
    s_parts = {k: a[..., :nn] for (k, nn, _), a in
               zip(padded, _unpack_rows(s_all, [(np_,) for _, _, np_ in padded]))}

    dmod_all = s_parts["dmod"]
    dmod_cols = lax.dynamic_slice_in_dim(dmod_all, me * ada_cols, ada_cols, axis=2).transpose(1, 0, 2)
    g_ada_w = _ada_bwd(c_all.T, dmod_cols, name="ada_bwd")
    gp = {}
    gp["ada_w"] = g_ada_w[None]
    gp["ada_b"] = dmod_all
    for k in ("norm_g", "q_lat_g", "kv_lat_g", "glu_b", "dw_b", "conv_ln_g", "conv_ln_b", "b_pw"):
        gp[k] = s_parts[k]
    for k in ("q_norm_g", "k_norm_g"):
        t = s_parts[k]
        gp[k] = jnp.concatenate([t[..., :NOPE], t[..., LANE:LANE + ROPE]], axis=-1)
    dw_g = s_parts["dw_w"].reshape(N_DEV, nl, HALO, D_CONV)[:, :, :CONV_K]
    gp["dw_w"] = lax.dynamic_slice_in_dim(dw_g, me * LANE, LANE, axis=3)

    res = {k: _adamw(gp[k], w_loc[k], m_loc[k], v_loc[k], name=f"adamw_{k}") for k in names if k not in _BIG}
    arrived = [None] * nl
    arrived[1] = _exchange_wait(*flying["l1"][:4], gx, True, name="scatter_wait_l1")
    rest0 = _exchange_wait(*flying["l0_rest"][:4], gx, True, name="scatter_wait_l0_rest")
    arrived[0] = _exchange_wait(*flying["l0_w_in"][:4], res["ada_w"][1], True, name="scatter_wait_l0_w_in") + rest0
    for i, k in enumerate(_BIG):
        res[k] = _adamw([arrived[l][i] for l in range(nl)], w_loc[k], m_loc[k], v_loc[k], name=f"adamw_{k}")
    res["w_in"] = tuple(tr(a) for a in res["w_in"])
    out = [loss, gx[None]]
    for idx in range(4):
        out += [res[k][idx] for k in names]
    return tuple(out)
```

```python
import functools
import math

import jax
import jax.numpy as jnp
from jax import lax
from jax.experimental import pallas as pl
from jax.experimental.pallas import tpu as pltpu

F32 = jnp.float32
MXU_DTYPE = jnp.bfloat16
WIRE_DTYPE = jnp.bfloat16

D_MODEL = 2048
N_LAYERS = 2
N_DEV = 8
N_HEADS = 8
NOPE = 128
ROPE = 64
V_DIM = 128
QK_DIM = NOPE + ROPE
Q_LORA = 512
KV_LORA = 256
D_MLA = N_HEADS * V_DIM
D_CONV = 1024
CONV_K = 31
ROPE_THETA = 10000.0
EPS = 1e-6
LANE = 128
HEAD_PAD = 2 * LANE
HALO = 32

SEG_CI = (0, 2 * D_CONV)
SEG_MG = (2 * D_CONV, D_MLA)
SEG_CG = (2 * D_CONV + D_MLA, D_CONV)
SEG_QL = (2 * D_CONV + D_MLA + D_CONV, Q_LORA)
SEG_KVL = (SEG_QL[0] + Q_LORA, KV_LORA)
SEG_KR = (SEG_KVL[0] + KV_LORA, LANE)
SEG_LAT = (SEG_QL[0], 1024)
IN_PAD = SEG_LAT[0] + SEG_LAT[1]
IN_TILE = IN_PAD // 4
assert SEG_KR[0] + LANE <= IN_PAD and SEG_LAT[0] % SEG_LAT[1] == 0
IN_COLS = Q_LORA + KV_LORA + ROPE + D_MLA + 2 * D_CONV + D_CONV

ADAM_LR = 0.001
ADAM_B1 = 0.9
ADAM_B2 = 0.999
ADAM_EPS = 1e-08
ADAM_WD = 0.01
ADAM_STEP = 10

VMEM_LIMIT = 56 * 1024 * 1024
ATT_T = 512
ROW_T = 256
CONV_T = 128
MESH_ID = pl.DeviceIdType.MESH


def _cp(sem=None):
    kw = dict(vmem_limit_bytes=VMEM_LIMIT)
    if sem is not None:
        kw["dimension_semantics"] = sem
    return pltpu.CompilerParams(**kw)


def _sds(shape, dtype):
    return jax.ShapeDtypeStruct(shape, dtype)


def _silu(x):
    return x * jax.nn.sigmoid(x)


def _dsilu(x):
    s = jax.nn.sigmoid(x)
    return s * (1.0 + x * (1.0 - s))


def _rowspec(t, width, col=0):
    return pl.BlockSpec((t, width), lambda i: (i, col))


def _vecspec(width):
    return pl.BlockSpec((1, width), lambda i: (0, 0))


def _colsum(v):
    return jnp.sum(v, axis=0, keepdims=True)


def _mm(a, b, *, name, ta=False, tb=False, out_dtype=F32, tm=512, tn=512, tk=None, n_outer=False, after=None,
        residual=None):
    if ta:
        kdim, m = a.shape
    else:
        m, kdim = a.shape
    if tb:
        n, k2 = b.shape
    else:
        k2, n = b.shape
    assert kdim == k2, (a.shape, b.shape)
    tm, tn = min(tm, m), min(tn, n)
    tk = kdim if tk is None else min(tk, kdim)
    assert m % tm == 0 and n % tn == 0 and kdim % tk == 0, (m, n, kdim, tm, tn, tk)
    nk = kdim // tk
    dims = (((0 if ta else 1,), (1 if tb else 0,)), ((), ()))

    n_extra = 0 if after is None else 1
    assert residual is None or nk == 1

    def body(a_ref, b_ref, *rest):
        if residual is not None:
            x_ref, gate_ref = rest[:2]
            rest = rest[2:]
        o_ref, scratch = rest[n_extra], rest[n_extra + 1:]
        prod = lax.dot_general(a_ref[...].astype(MXU_DTYPE), b_ref[...].astype(MXU_DTYPE), dims,
                               preferred_element_type=F32)
        if residual is not None:
            o_ref[...] = prod.astype(o_ref.dtype)
            scratch[0][...] = x_ref[...] + gate_ref[...] * prod
        elif nk == 1:
            o_ref[...] = prod.astype(o_ref.dtype)
        else:
            acc = scratch[0]
            k = pl.program_id(2)

            @pl.when(k == 0)
            def _():
                acc[...] = prod

            @pl.when(k > 0)
            def _():
                acc[...] += prod

            @pl.when(k == nk - 1)
            def _():
                o_ref[...] = acc[...].astype(o_ref.dtype)

    if n_outer:
        ij = lambda g0, g1: (g1, g0)
        grid = (n // tn, m // tm, nk)
    else:
        ij = lambda g0, g1: (g0, g1)
        grid = (m // tm, n // tn, nk)

    def a_map(g0, g1, k):
        i, _ = ij(g0, g1)
        return (k, i) if ta else (i, k)

    def b_map(g0, g1, k):
        _, j = ij(g0, g1)
        return (j, k) if tb else (k, j)

    def o_map(g0, g1, k):
        return ij(g0, g1)

    in_specs = [pl.BlockSpec((tk, tm) if ta else (tm, tk), a_map), pl.BlockSpec((tn, tk) if tb else (tk, tn), b_map)]
    operands = [a, b]
    out_specs, out_shape = pl.BlockSpec((tm, tn), o_map), _sds((m, n), out_dtype)
    if residual is not None:
        in_specs += [pl.BlockSpec((tm, tn), o_map), pl.BlockSpec((1, tn), lambda g0, g1, k: (0, ij(g0, g1)[1]))]
        operands += list(residual)
        out_specs, out_shape = [out_specs, pl.BlockSpec((tm, tn), o_map)], [out_shape, _sds((m, n), F32)]
    if after is not None:
        in_specs.append(_ANY)
        operands.append(after)
    return pl.pallas_call(
        body, name=name, grid=grid, in_specs=in_specs, out_specs=out_specs, out_shape=out_shape,
        scratch_shapes=[pltpu.VMEM((tm, tn), F32)] if nk > 1 else [],
        compiler_params=_cp(("parallel", "parallel", "arbitrary")),
    )(*operands)


def _prenorm(x, g, shift, sc1p, *, name):
    s, d = x.shape
    t = min(ROW_T, s)

    def body(x_ref, g_ref, sh_ref, sc_ref, h_ref):
        xv = x_ref[...]
        r = lax.rsqrt(jnp.mean(xv * xv, axis=-1, keepdims=True) + EPS)
        h_ref[...] = ((xv * r) * g_ref[...] * sc_ref[...] + sh_ref[...]).astype(h_ref.dtype)

    return pl.pallas_call(
        body, name=name, grid=(s // t,),
        in_specs=[_rowspec(t, d), _vecspec(d), _vecspec(d), _vecspec(d)],
        out_specs=_rowspec(t, d), out_shape=_sds((s, d), MXU_DTYPE),
        compiler_params=_cp(("parallel",)),
    )(x, g, shift, sc1p)


def _lat_norm(z, g_ql, g_kvl, *, name):
    s = z.shape[0]
    t = min(ROW_T, s)

    def body(ql_ref, kvl_ref, gq_ref, gk_ref, qn_ref, kn_ref):
        for src, g_ref, dst in ((ql_ref, gq_ref, qn_ref), (kvl_ref, gk_ref, kn_ref)):
            v = src[...]
            r = lax.rsqrt(jnp.mean(v * v, axis=-1, keepdims=True) + EPS)
            dst[...] = ((v * r) * g_ref[...]).astype(dst.dtype)

    return pl.pallas_call(
        body, name=name, grid=(s // t,),
        in_specs=[_rowspec(t, Q_LORA, SEG_QL[0] // Q_LORA), _rowspec(t, KV_LORA, SEG_KVL[0] // KV_LORA),
                  _vecspec(Q_LORA), _vecspec(KV_LORA)],
        out_specs=[_rowspec(t, Q_LORA), _rowspec(t, KV_LORA)],
        out_shape=[_sds((s, Q_LORA), MXU_DTYPE), _sds((s, KV_LORA), MXU_DTYPE)],
        compiler_params=_cp(("parallel",)),
    )(z, z, g_ql, g_kvl)


def _rope_fwd(r, c_t, s1_t, s2_t):
    return r * c_t + pltpu.roll(r, LANE - ROPE // 2, 1) * s1_t + pltpu.roll(r, ROPE // 2, 1) * s2_t


def _rope_bwd(d, c_t, s1_t, s2_t):
    return d * c_t + pltpu.roll(d * s1_t, ROPE // 2, 1) + pltpu.roll(d * s2_t, LANE - ROPE // 2, 1)


def _lanesum(v):
    return jnp.sum(v, axis=-1, keepdims=True)


def _qk_prep(q_raw, kv, z, c_t, s1_t, s2_t, gqn, gqr, gkn, gkr, *, name):
    s = q_raw.shape[0]
    t = min(ROW_T, s)
    scale = 1.0 / math.sqrt(QK_DIM)

    def body(q_ref, kv_ref, kr_ref, c_ref, s1_ref, s2_ref, gqn_ref, gqr_ref, gkn_ref, gkr_ref,
             qf_ref, kf_ref, vf_ref):
        c_v, s1_v, s2_v = c_ref[...], s1_ref[...], s2_ref[...]
        kr = kr_ref[...]
        kr_ss = _lanesum(kr * kr)
        for h in range(N_HEADS):
            n = q_ref[:, h * LANE:(h + 1) * LANE]
            r = q_ref[:, N_HEADS * LANE + h * LANE:N_HEADS * LANE + (h + 1) * LANE]
            rs = lax.rsqrt((_lanesum(n * n) + _lanesum(r * r)) * (1.0 / QK_DIM) + EPS)
            qf_ref[h, :, 0:LANE] = (((n * rs) * gqn_ref[...]) * scale).astype(qf_ref.dtype)
            rr = _rope_fwd((r * rs) * gqr_ref[...], c_v, s1_v, s2_v)
            qf_ref[h, :, LANE:HEAD_PAD] = (rr * scale).astype(qf_ref.dtype)

            n = kv_ref[:, h * 2 * LANE:h * 2 * LANE + LANE]
            rs = lax.rsqrt((_lanesum(n * n) + kr_ss) * (1.0 / QK_DIM) + EPS)
            kf_ref[h, :, 0:LANE] = ((n * rs) * gkn_ref[...]).astype(kf_ref.dtype)
            kf_ref[h, :, LANE:HEAD_PAD] = _rope_fwd((kr * rs) * gkr_ref[...], c_v, s1_v, s2_v).astype(kf_ref.dtype)
            vf_ref[h, :, 0:V_DIM] = kv_ref[:, h * 2 * LANE + LANE:(h + 1) * 2 * LANE].astype(vf_ref.dtype)
            vf_ref[h, :, V_DIM:] = jnp.ones((t, V_DIM), vf_ref.dtype)

    hspec = lambda w: pl.BlockSpec((N_HEADS, t, w), lambda i: (0, i, 0))
    return pl.pallas_call(
        body, name=name, grid=(s // t,),
        in_specs=[_rowspec(t, 2 * N_HEADS * LANE), _rowspec(t, 2 * N_HEADS * LANE),
                  _rowspec(t, LANE, SEG_KR[0] // LANE),
                  _rowspec(t, LANE), _rowspec(t, LANE), _rowspec(t, LANE),
                  _vecspec(LANE), _vecspec(LANE), _vecspec(LANE), _vecspec(LANE)],
        out_specs=[hspec(HEAD_PAD), hspec(HEAD_PAD), hspec(2 * V_DIM)],
        out_shape=[_sds((N_HEADS, s, HEAD_PAD), MXU_DTYPE), _sds((N_HEADS, s, HEAD_PAD), MXU_DTYPE),
                   _sds((N_HEADS, s, 2 * V_DIM), MXU_DTYPE)],
        compiler_params=_cp(("parallel",)),
    )(q_raw, kv, z, c_t, s1_t, s2_t, gqn, gqr, gkn, gkr)


def _causal_mask(t):
    row = lax.broadcasted_iota(jnp.int32, (t, t), 0)
    col = lax.broadcasted_iota(jnp.int32, (t, t), 1)
    return col <= row


NEG = -1e30


def _flash_fwd(qf, kf, va, *, name):
    nh, s, dk = qf.shape
    dv = va.shape[-1] // 2
    t = min(ATT_T, s)
    n = s // t
    assert dv == LANE and t % LANE == 0

    def body(q_ref, k_ref, v_ref, o_ref, lse_ref, m_s, acc_s, s_buf):
        i = pl.program_id(1)
        m_s[...] = jnp.full(m_s.shape, NEG, F32)
        acc_s[...] = jnp.zeros(acc_s.shape, F32)

        def rows_of(j):
            return pl.ds(pl.multiple_of(j * t, t), t)

        def scores(qi, j):
            return lax.dot_general(q_ref[0, rows_of(qi), :], k_ref[0, rows_of(j), :], (((1,), (1,)), ((), ())),
                                   preferred_element_type=F32)

        def consume(j, slot, masked):
            sc = s_buf[slot]
            if masked:
                sc = jnp.where(_causal_mask(t), sc, NEG)
            m_prev = m_s[...]
            m_new = jnp.maximum(m_prev, jnp.max(sc, axis=-1, keepdims=True))
            alpha = jnp.exp(m_prev - m_new)
            p = jnp.exp(sc - jnp.tile(m_new, (1, t // LANE)))
            acc_s[...] = jnp.tile(alpha, (1, 2)) * acc_s[...] + jnp.dot(
                p.astype(MXU_DTYPE), v_ref[0, rows_of(j), :], preferred_element_type=F32)
            m_s[...] = m_new

        nxt = jnp.minimum(i + 1, n - 1)

        @pl.when(i == 0)
        def _():
            s_buf[2] = scores(0, 0)
            consume(0, 2, True)
            s_buf[2] = scores(nxt, 0)

        @pl.when(i > 0)
        def _():
            s_buf[1] = scores(i, 1)
            consume(0, 2, False)

            def pair(a, carry):
                s_buf[0] = scores(i, 2 * a + 2)
                consume(2 * a + 1, 1, False)
                s_buf[1] = scores(i, 2 * a + 3)
                consume(2 * a + 2, 0, False)
                return carry

            lax.fori_loop(0, (i - 1) // 2, pair, 0)

            @pl.when(i % 2 == 1)
            def _():
                s_buf[2] = scores(nxt, 0)
                consume(i, 1, True)

            @pl.when(i % 2 == 0)
            def _():
                s_buf[0] = scores(i, i)
                consume(i - 1, 1, False)
                s_buf[2] = scores(nxt, 0)
                consume(i, 0, True)

        den = acc_s[:, dv:]
        o_ref[...] = acc_s[:, :dv] / den
        lse_ref[0] = m_s[...] + jnp.log(den)

    head = lambda h, i: (h, 0, 0)
    return pl.pallas_call(
        body, name=name, grid=(nh, n),
        in_specs=[pl.BlockSpec((1, s, dk), head), pl.BlockSpec((1, s, dk), head), pl.BlockSpec((1, s, 2 * dv), head)],
        out_specs=[pl.BlockSpec((t, dv), lambda h, i: (i, h)),
                   pl.BlockSpec((1, t, LANE), lambda h, i: (h, i, 0))],
        out_shape=[_sds((s, nh * dv), F32), _sds((nh, s, LANE), F32)],
        scratch_shapes=[pltpu.VMEM((t, LANE), F32), pltpu.VMEM((t, 2 * dv), F32), pltpu.VMEM((3, t, t), F32)],
        compiler_params=_cp(("arbitrary", "arbitrary")),
    )(qf, kf, va)


def _shifted_copies(ext_ref):
    rows = ext_ref.shape[1] - 8
    for s in range(1, 8):
        ext_ref[s, 0:rows, :] = ext_ref[0, s:s + rows, :]


def _windows(ext_ref, offsets, t_rows, lane0, lanes):
    for s in range(8):
        group = [o for o in offsets if o % 8 == s]
        if not group:
            continue
        lo, hi = min(group) - s, max(group) - s
        wide = ext_ref[s, pl.ds(lo, hi - lo + t_rows), lane0:lane0 + lanes]
        for o in group:
            yield o, wide[o - s - lo:o - s - lo + t_rows]


def _dw_taps(ext_ref, w_ref, row0, t_rows, lane0, lanes, first_off):
    acc = None
    for off, win in _windows(ext_ref, [row0 + first_off + k for k in range(CONV_K)], t_rows, lane0, lanes):
        k = off - row0 - first_off
        term = w_ref[k:k + 1, lane0:lane0 + lanes] * win
        acc = term if acc is None else acc + term
    return acc


CONV_RC = 32
CONV_LC = 256


def _conv_fwd(z, glu_b, dw_w, dw_b, ln_g, ln_b, *, name):
    s = z.shape[0]
    t = min(CONV_T, s)
    c2 = 2 * D_CONV
    hb = t // HALO

    def body(zm_ref, zh_ref, gb_ref, w_ref, wb_ref, g_ref, b_ref, u1_ref, u3_ref, ext):
        i = pl.program_id(0)

        def glu(zv):
            ci = zv + gb_ref[...]
            return ci[:, :D_CONV] * jax.nn.sigmoid(ci[:, D_CONV:])

        ext[0, HALO:, :] = glu(zm_ref[...])
        ext[0, 0:HALO, :] = jnp.where(i > 0, glu(zh_ref[...]), 0.0)
        _shifted_copies(ext)
        for rc in range(0, t, CONV_RC):
            for lc in range(0, D_CONV, CONV_LC):
                acc = _dw_taps(ext, w_ref, rc, CONV_RC, lc, CONV_LC, HALO - (CONV_K - 1))
                u1_ref[rc:rc + CONV_RC, lc:lc + CONV_LC] = acc + wb_ref[:, lc:lc + CONV_LC]
        u1 = u1_ref[...]
        mu = jnp.mean(u1, axis=-1, keepdims=True)
        cen = u1 - mu
        var = jnp.mean(cen * cen, axis=-1, keepdims=True)
        u2 = (cen * lax.rsqrt(var + EPS)) * g_ref[...] + b_ref[...]
        u3_ref[...] = _silu(u2).astype(u3_ref.dtype)

    return pl.pallas_call(
        body, name=name, grid=(s // t,),
        in_specs=[_rowspec(t, c2), pl.BlockSpec((HALO, c2), lambda i: (jnp.maximum(i * hb - 1, 0), 0)),
                  _vecspec(c2), pl.BlockSpec((HALO, D_CONV), lambda i: (0, 0)), _vecspec(D_CONV),
                  _vecspec(D_CONV), _vecspec(D_CONV)],
        out_specs=[_rowspec(t, D_CONV), _rowspec(t, D_CONV)],
        out_shape=[_sds((s, D_CONV), F32), _sds((s, D_CONV), MXU_DTYPE)],
        scratch_shapes=[pltpu.VMEM((8, t + HALO, D_CONV), F32)],
        compiler_params=_cp(("parallel",)),
    )(z, z, glu_b, dw_w, dw_b, ln_g, ln_b)


def _gate_cat(o, z, u4m, b_pw, *, name):
    s = o.shape[0]
    t = min(ROW_T, s)

    def body(o_ref, mg_ref, u4_ref, cg_ref, b_ref, cat_ref):
        cat_ref[:, :D_MLA] = (o_ref[...] * _silu(mg_ref[...])).astype(cat_ref.dtype)
        cat_ref[:, D_MLA:] = ((u4_ref[...] + b_ref[...]) * _silu(cg_ref[...])).astype(cat_ref.dtype)

    return pl.pallas_call(
        body, name=name, grid=(s // t,),
        in_specs=[_rowspec(t, D_MLA), _rowspec(t, D_MLA, SEG_MG[0] // D_MLA), _rowspec(t, D_CONV),
                  _rowspec(t, D_CONV, SEG_CG[0] // D_CONV), _vecspec(D_CONV)],
        out_specs=_rowspec(t, D_MLA + D_CONV), out_shape=_sds((s, D_MLA + D_CONV), MXU_DTYPE),
        compiler_params=_cp(("parallel",)),
    )(o, z, u4m, z, b_pw)


def _loss_head(xf, target, *, name):
    s, d = xf.shape
    t = min(ROW_T, s)

    def body(x_ref, t_ref, gx_ref, loss_ref):
        @pl.when(pl.program_id(0) == 0)
        def _():
            loss_ref[...] = jnp.zeros(loss_ref.shape, F32)

        err = x_ref[...] - t_ref[...]
        gx_ref[...] = err * (1.0 / d)
        loss_ref[...] += 0.5 * jnp.sum(_lanesum(err * err) * (1.0 / d), axis=0, keepdims=True)

    return pl.pallas_call(
        body, name=name, grid=(s // t,),
        in_specs=[_rowspec(t, d), _rowspec(t, d)],
        out_specs=[_rowspec(t, d), pl.BlockSpec((1, 1), lambda i: (0, 0))],
        out_shape=[_sds((s, d), F32), _sds((1, 1), F32)],
        compiler_params=_cp(("arbitrary",)),
    )(xf, target)


def _acc_init(refs):
    @pl.when(pl.program_id(0) == 0)
    def _():
        for r in refs:
            r[...] = jnp.zeros(r.shape, r.dtype)


def _out_bwd(gxo, y, gate, *, name):
    s, d = gxo.shape
    t = min(ROW_T, s)

    def body(g_ref, y_ref, gate_ref, dy_ref, dgate_ref):
        _acc_init([dgate_ref])
        gv = g_ref[...]
        dy_ref[...] = (gv * gate_ref[...]).astype(dy_ref.dtype)
        dgate_ref[...] += _colsum(gv * y_ref[...])

    return pl.pallas_call(
        body, name=name, grid=(s // t,),
        in_specs=[_rowspec(t, d), _rowspec(t, d), _vecspec(d)],
        out_specs=[_rowspec(t, d), _vecspec(d)],
        out_shape=[_sds((s, d), MXU_DTYPE), _sds((1, d), F32)],
        compiler_params=_cp(("arbitrary",)),
    )(gxo, y, gate)


def _gate_bwd(dcat, o, z, u4m, b_pw, *, name):
    s = o.shape[0]
    t = min(ROW_T, s)
    gates = D_MLA + D_CONV
    assert SEG_CG[0] == SEG_MG[0] + D_MLA and SEG_MG[0] % gates == 0

    def body(dm_ref, dc_ref, o_ref, mg_ref, u4_ref, cg_ref, b_ref,
             do_ref, delta_ref, du4_ref, gb_ref, dz_ref):
        _acc_init([gb_ref])
        dm, ov, mg = dm_ref[...], o_ref[...], mg_ref[...]
        do = dm * _silu(mg)
        do_ref[...] = do.astype(do_ref.dtype)
        dz_ref[:, :D_MLA] = (dm * ov * _dsilu(mg)).astype(dz_ref.dtype)
        prod = do * ov
        for h in range(N_HEADS):
            delta_ref[h] = _lanesum(prod[:, h * V_DIM:(h + 1) * V_DIM])
        dc, cg = dc_ref[...], cg_ref[...]
        du4 = dc * _silu(cg)
        du4_ref[...] = du4.astype(du4_ref.dtype)
        dz_ref[:, D_MLA:] = (dc * (u4_ref[...] + b_ref[...]) * _dsilu(cg)).astype(dz_ref.dtype)
        gb_ref[...] += _colsum(du4)

    return pl.pallas_call(
        body, name=name, grid=(s // t,),
        in_specs=[_rowspec(t, D_MLA, 0), _rowspec(t, D_CONV, 1), _rowspec(t, D_MLA),
                  _rowspec(t, D_MLA, SEG_MG[0] // D_MLA), _rowspec(t, D_CONV),
                  _rowspec(t, D_CONV, SEG_CG[0] // D_CONV), _vecspec(D_CONV)],
        out_specs=[_rowspec(t, D_MLA), pl.BlockSpec((N_HEADS, t, 1), lambda i: (0, i, 0)),
                   _rowspec(t, D_CONV), _vecspec(D_CONV), _rowspec(t, gates, SEG_MG[0] // gates)],
        out_shape=[_sds((s, D_MLA), MXU_DTYPE), _sds((N_HEADS, s, 1), F32),
                   _sds((s, D_CONV), MXU_DTYPE), _sds((1, D_CONV), F32), _sds((s, IN_PAD), MXU_DTYPE)],
        compiler_params=_cp(("arbitrary",)),
    )(dcat, dcat, o, z, u4m, z, b_pw)


def _conv_bwd(du3, u1, z, dz, glu_b, dw_w, ln_g, ln_b, *, name):
    s = z.shape[0]
    t = min(CONV_T, s)
    c2 = 2 * D_CONV
    hb = t // HALO
    n_blk = s // t
    last_halo = s // HALO - 1

    def body(d3m_ref, d3h_ref, u1m_ref, u1h_ref, zm_ref, zh_ref, gb_ref, w_ref, g_ref, b_ref, dz_in_ref,
             dci_ref, gg_ref, gbn_ref, gwb_ref, ggb_ref, gw_ref, dext, uext, du0_s, gw_acc):
        i = pl.program_id(0)
        _acc_init([gg_ref, gbn_ref, gwb_ref, ggb_ref, gw_acc])

        def ln_bwd(d3, u1v):
            mu = jnp.mean(u1v, axis=-1, keepdims=True)
            cen = u1v - mu
            rstd = lax.rsqrt(jnp.mean(cen * cen, axis=-1, keepdims=True) + EPS)
            uh = cen * rstd
            d2 = d3 * _dsilu(uh * g_ref[...] + b_ref[...])
            dh = d2 * g_ref[...]
            d1 = rstd * (dh - jnp.mean(dh, axis=-1, keepdims=True) - uh * jnp.mean(dh * uh, axis=-1, keepdims=True))
            return d1, d2, uh

        d1, d2, uh = ln_bwd(d3m_ref[...], u1m_ref[...])
        gg_ref[...] += _colsum(d2 * uh)
        gbn_ref[...] += _colsum(d2)
        gwb_ref[...] += _colsum(d1)
        dext[0, 0:t, :] = d1
        d1h, _, _ = ln_bwd(d3h_ref[...], u1h_ref[...])
        dext[0, t:, :] = jnp.where(i < n_blk - 1, d1h, 0.0)
        _shifted_copies(dext)

        def glu_parts(zv):
            ci = zv + gb_ref[...]
            return ci[:, :D_CONV], jax.nn.sigmoid(ci[:, D_CONV:])

        val, sg = glu_parts(zm_ref[...])
        uext[0, HALO:, :] = val * sg
        valh, sgh = glu_parts(zh_ref[...])
        uext[0, 0:HALO, :] = jnp.where(i > 0, valh * sgh, 0.0)
        _shifted_copies(uext)

        for rc in range(0, t, CONV_RC):
            for lc in range(0, D_CONV, CONV_LC):
                acc = None
                for off, win in _windows(dext, [rc + k for k in range(CONV_K)], CONV_RC, lc, CONV_LC):
                    k = (CONV_K - 1) - (off - rc)
                    term = w_ref[k:k + 1, lc:lc + CONV_LC] * win
                    acc = term if acc is None else acc + term
                du0_s[rc:rc + CONV_RC, lc:lc + CONV_LC] = acc
                dchunk = dext[0, rc:rc + CONV_RC, lc:lc + CONV_LC]
                first = rc + HALO - (CONV_K - 1)
                for off, win in _windows(uext, [first + k for k in range(CONV_K)], CONV_RC, lc, CONV_LC):
                    k = off - first
                    pr = dchunk * win
                    part = pr[0:8]
                    for r8 in range(8, CONV_RC, 8):
                        part = part + pr[r8:r8 + 8]
                    gw_acc[k, :, lc:lc + CONV_LC] += part

        du0 = du0_s[...]
        dval = du0 * sg
        dgt = du0 * val * sg * (1.0 - sg)
        dci_ref[:, :D_CONV] = dval.astype(dci_ref.dtype)
        dci_ref[:, D_CONV:] = dgt.astype(dci_ref.dtype)
        ggb_ref[:, :D_CONV] += _colsum(dval)
        ggb_ref[:, D_CONV:] += _colsum(dgt)

        @pl.when(i == n_blk - 1)
        def _():
            gw_ref[...] = jnp.sum(gw_acc[...], axis=1)

    halo_next = lambda w: pl.BlockSpec((HALO, w), lambda i: (jnp.minimum((i + 1) * hb, last_halo), 0))
    return pl.pallas_call(
        body, name=name, grid=(n_blk,),
        in_specs=[_rowspec(t, D_CONV), halo_next(D_CONV), _rowspec(t, D_CONV), halo_next(D_CONV),
                  _rowspec(t, c2), pl.BlockSpec((HALO, c2), lambda i: (jnp.maximum(i * hb - 1, 0), 0)),
                  _vecspec(c2), pl.BlockSpec((HALO, D_CONV), lambda i: (0, 0)), _vecspec(D_CONV), _vecspec(D_CONV),
                  _ANY],
        out_specs=[_rowspec(t, c2, SEG_CI[0] // c2), _vecspec(D_CONV), _vecspec(D_CONV), _vecspec(D_CONV),
                   _vecspec(c2), pl.BlockSpec((HALO, D_CONV), lambda i: (0, 0))],
        out_shape=[_sds(dz.shape, dz.dtype), _sds((1, D_CONV), F32), _sds((1, D_CONV), F32), _sds((1, D_CONV), F32),
                   _sds((1, c2), F32), _sds((HALO, D_CONV), F32)],
        scratch_shapes=[pltpu.VMEM((8, t + HALO, D_CONV), F32), pltpu.VMEM((8, t + HALO, D_CONV), F32),
                        pltpu.VMEM((t, D_CONV), F32), pltpu.VMEM((HALO, 8, D_CONV), F32)],
        input_output_aliases={10: 0},
        compiler_params=_cp(("arbitrary",)),
    )(du3, du3, u1, u1, z, z, glu_b, dw_w, ln_g, ln_b, dz)


def _flash_bwd(qf, kf, va, do, lse_t, delta_t, *, name):
    nh, s, dk = qf.shape
    dv = va.shape[-1] // 2
    t = min(ATT_T, s)
    n = s // t
    nt = (((1,), (1,)), ((), ()))
    tn = (((0,), (0,)), ((), ()))

    def body(q_ref, do_ref, lse_ref, dl_ref, k_ref, v_ref, dq_ref, dk_ref, dv_ref,
             dk_s, dv_s, st_buf, dpt_buf):
        j = pl.program_id(1)

        @pl.when(j == 0)
        def _():
            dq_ref[...] = jnp.zeros(dq_ref.shape, F32)

        dk_s[...] = jnp.zeros(dk_s.shape, F32)
        dv_s[...] = jnp.zeros(dv_s.shape, F32)
        n_un = n - 1 - j
        nxt = jnp.minimum(j + 1, n - 1)

        def rows_at(blk):
            return pl.ds(pl.multiple_of(blk * t, t), t)

        def rows_of(b):
            return rows_at(n - 1 - b)

        k = k_ref[0, rows_at(j), :]

        def produce(kj, b, slot):
            rows = rows_of(b)
            st_buf[slot] = lax.dot_general(k_ref[0, rows_at(kj), :], q_ref[0, rows, :], nt,
                                           preferred_element_type=F32)
            dpt_buf[slot] = lax.dot_general(v_ref[0, rows_at(kj), 0:dv], do_ref[rows, :], nt,
                                            preferred_element_type=F32)

        def consume(b, slot, masked):
            i = n - 1 - b
            rows = rows_of(b)
            q, dov = q_ref[0, rows, :], do_ref[rows, :]
            pt = jnp.exp(st_buf[slot] - lse_ref[0, i])
            if masked:
                key = lax.broadcasted_iota(jnp.int32, (t, t), 0)
                qry = lax.broadcasted_iota(jnp.int32, (t, t), 1)
                pt = jnp.where(key <= qry, pt, 0.0)
            dv_s[...] += jnp.dot(pt.astype(MXU_DTYPE), dov, preferred_element_type=F32)
            dst = (pt * (dpt_buf[slot] - dl_ref[0, i])).astype(MXU_DTYPE)
            dk_s[...] += jnp.dot(dst, q, preferred_element_type=F32)
            dq_ref[0, rows, :] += lax.dot_general(dst, k, tn, preferred_element_type=F32)

        @pl.when(j == 0)
        def _():
            produce(0, 0, 2)

        @pl.when(n_un == 0)
        def _():
            consume(0, 2, True)

        @pl.when(n_un > 0)
        def _():
            produce(j, 1, 1)
            consume(0, 2, False)

            def pair(a, carry):
                produce(j, 2 * a + 2, 0)
                consume(2 * a + 1, 1, False)
                produce(j, 2 * a + 3, 1)
                consume(2 * a + 2, 0, False)
                return carry

            lax.fori_loop(0, (n_un - 1) // 2, pair, 0)

            @pl.when(n_un % 2 == 1)
            def _():
                produce(nxt, 0, 2)
                consume(n_un, 1, True)

            @pl.when(n_un % 2 == 0)
            def _():
                produce(j, n_un, 0)
                consume(n_un - 1, 1, False)
                produce(nxt, 0, 2)
                consume(n_un, 0, True)

        dk_ref[0] = dk_s[...]
        dv_ref[0] = dv_s[...]

    head = lambda h, j: (h, 0, 0)
    rowv = pl.BlockSpec((1, n, 1, t), lambda h, j: (h, 0, 0, 0))
    return pl.pallas_call(
        body, name=name, grid=(nh, n),
        in_specs=[pl.BlockSpec((1, s, dk), head),
                  pl.BlockSpec((s, dv), lambda h, j: (0, h)),
                  rowv, rowv,
                  pl.BlockSpec((1, s, dk), head),
                  pl.BlockSpec((1, s, 2 * dv), head)],
        out_specs=[pl.BlockSpec((1, s, dk), head),
                   pl.BlockSpec((1, t, dk), lambda h, j: (h, j, 0)),
                   pl.BlockSpec((1, t, dv), lambda h, j: (h, j, 0))],
        out_shape=[_sds((nh, s, dk), F32), _sds((nh, s, dk), F32), _sds((nh, s, dv), F32)],
        scratch_shapes=[pltpu.VMEM((t, dk), F32), pltpu.VMEM((t, dv), F32),
                        pltpu.VMEM((3, t, t), F32), pltpu.VMEM((3, t, t), F32)],
        compiler_params=_cp(("arbitrary", "arbitrary")),
    )(qf, do, lse_t, delta_t, kf, va)


def _qk_bwd(dqf, dkf, dvf, q_raw, kv, z, c_t, s1_t, s2_t, gqn, gqr, gkn, gkr, *, name):
    s = q_raw.shape[0]
    t = min(ROW_T, s)
    scale = 1.0 / math.sqrt(QK_DIM)

    def body(dq_ref, dk_ref, dv_ref, q_ref, kv_ref, kr_ref, c_ref, s1_ref, s2_ref,
             gqn_ref, gqr_ref, gkn_ref, gkr_ref, dqr_ref, dkv_ref, dkr_ref, ggq_ref, ggk_ref):
        _acc_init([ggq_ref, ggk_ref])
        c_v, s1_v, s2_v = c_ref[...], s1_ref[...], s2_ref[...]
        kr = kr_ref[...]
        kr_ss = _lanesum(kr * kr)
        dkr = jnp.zeros(kr.shape, F32)
        ggq_n = ggq_r = ggk_n = ggk_r = jnp.zeros((1, LANE), F32)

        def norm_bwd(n, r, rs, dyn, dyr, gn, gr):
            nh_, rh_ = n * rs, r * rs
            dnh, drh = dyn * gn, dyr * gr
            dot = (_lanesum(dnh * nh_) + _lanesum(drh * rh_)) * (1.0 / QK_DIM)
            return rs * (dnh - nh_ * dot), rs * (drh - rh_ * dot), _colsum(dyn * nh_), _colsum(dyr * rh_)

        for h in range(N_HEADS):
            n = q_ref[:, h * LANE:(h + 1) * LANE]
            r = q_ref[:, N_HEADS * LANE + h * LANE:N_HEADS * LANE + (h + 1) * LANE]
            rs = lax.rsqrt((_lanesum(n * n) + _lanesum(r * r)) * (1.0 / QK_DIM) + EPS)
            dyn = dq_ref[h, :, 0:LANE] * scale
            dyr = _rope_bwd(dq_ref[h, :, LANE:HEAD_PAD] * scale, c_v, s1_v, s2_v)
            dn, dr, g_n, g_r = norm_bwd(n, r, rs, dyn, dyr, gqn_ref[...], gqr_ref[...])
            dqr_ref[:, h * LANE:(h + 1) * LANE] = dn.astype(dqr_ref.dtype)
            dqr_ref[:, N_HEADS * LANE + h * LANE:N_HEADS * LANE + (h + 1) * LANE] = dr.astype(dqr_ref.dtype)
            ggq_n, ggq_r = ggq_n + g_n, ggq_r + g_r

            n = kv_ref[:, h * 2 * LANE:h * 2 * LANE + LANE]
            rs = lax.rsqrt((_lanesum(n * n) + kr_ss) * (1.0 / QK_DIM) + EPS)
            dyn = dk_ref[h, :, 0:LANE]
            dyr = _rope_bwd(dk_ref[h, :, LANE:HEAD_PAD], c_v, s1_v, s2_v)
            dn, dr, g_n, g_r = norm_bwd(n, kr, rs, dyn, dyr, gkn_ref[...], gkr_ref[...])
            dkv_ref[:, h * 2 * LANE:h * 2 * LANE + LANE] = dn.astype(dkv_ref.dtype)
            dkv_ref[:, h * 2 * LANE + LANE:(h + 1) * 2 * LANE] = dv_ref[h].astype(dkv_ref.dtype)
            dkr = dkr + dr
            ggk_n, ggk_r = ggk_n + g_n, ggk_r + g_r

        dkr_ref[...] = dkr.astype(dkr_ref.dtype)
        ggq_ref[:, 0:LANE] += ggq_n
        ggq_ref[:, LANE:] += ggq_r
        ggk_ref[:, 0:LANE] += ggk_n
        ggk_ref[:, LANE:] += ggk_r

    hspec = lambda w: pl.BlockSpec((N_HEADS, t, w), lambda i: (0, i, 0))
    wide = 2 * N_HEADS * LANE
    return pl.pallas_call(
        body, name=name, grid=(s // t,),
        in_specs=[hspec(HEAD_PAD), hspec(HEAD_PAD), hspec(V_DIM), _rowspec(t, wide), _rowspec(t, wide),
                  _rowspec(t, LANE, SEG_KR[0] // LANE), _rowspec(t, LANE), _rowspec(t, LANE), _rowspec(t, LANE),
                  _vecspec(LANE), _vecspec(LANE), _vecspec(LANE), _vecspec(LANE)],
        out_specs=[_rowspec(t, wide), _rowspec(t, wide), _rowspec(t, LANE), _vecspec(2 * LANE), _vecspec(2 * LANE)],
        out_shape=[_sds((s, wide), MXU_DTYPE), _sds((s, wide), MXU_DTYPE), _sds((s, LANE), MXU_DTYPE),
                   _sds((1, 2 * LANE), F32), _sds((1, 2 * LANE), F32)],
        compiler_params=_cp(("arbitrary",)),
    )(dqf, dkf, dvf, q_raw, kv, z, c_t, s1_t, s2_t, gqn, gqr, gkn, gkr)


def _lat_bwd(dqn, dkn, dkr, z, dz, g_ql, g_kvl, *, name):
    s = z.shape[0]
    t = min(ROW_T, s)
    o_ql, o_kvl, o_kr = (seg[0] - SEG_LAT[0] for seg in (SEG_QL, SEG_KVL, SEG_KR))

    def body(dq_ref, dk_ref, dkr_ref, ql_ref, kvl_ref, gq_ref, gk_ref, dz_in_ref, dz_ref, ggq_ref, ggk_ref):
        _acc_init([ggq_ref, ggk_ref])
        for d_ref, src, g_ref, off, gg_ref in ((dq_ref, ql_ref, gq_ref, o_ql, ggq_ref),
                                               (dk_ref, kvl_ref, gk_ref, o_kvl, ggk_ref)):
            v, dy = src[...], d_ref[...]
            r = lax.rsqrt(jnp.mean(v * v, axis=-1, keepdims=True) + EPS)
            vh = v * r
            dvh = dy * g_ref[...]
            dz_ref[:, off:off + v.shape[1]] = (
                r * (dvh - vh * jnp.mean(dvh * vh, axis=-1, keepdims=True))).astype(dz_ref.dtype)
            gg_ref[...] += _colsum(dy * vh)
        dz_ref[:, o_kr:o_kr + LANE] = dkr_ref[...]
        dz_ref[:, o_kr + LANE:] = jnp.zeros((t, SEG_LAT[1] - o_kr - LANE), dz_ref.dtype)

    return pl.pallas_call(
        body, name=name, grid=(s // t,),
        in_specs=[_rowspec(t, Q_LORA), _rowspec(t, KV_LORA), _rowspec(t, LANE),
                  _rowspec(t, Q_LORA, SEG_QL[0] // Q_LORA), _rowspec(t, KV_LORA, SEG_KVL[0] // KV_LORA),
                  _vecspec(Q_LORA), _vecspec(KV_LORA), _ANY],
        out_specs=[_rowspec(t, SEG_LAT[1], SEG_LAT[0] // SEG_LAT[1]), _vecspec(Q_LORA), _vecspec(KV_LORA)],
        out_shape=[_sds(dz.shape, dz.dtype), _sds((1, Q_LORA), F32), _sds((1, KV_LORA), F32)],
        input_output_aliases={7: 0},
        compiler_params=_cp(("arbitrary",)),
    )(dqn, dkn, dkr, z, z, g_ql, g_kvl, dz)


def _prenorm_bwd(dh, x, gxo, g, sc1p, *, name):
    s, d = x.shape
    t = min(ROW_T, s)

    def body(dh_ref, x_ref, gx_ref, g_ref, sc_ref, dx_ref, dsh_ref, dsc_ref, gg_ref):
        _acc_init([dsh_ref, dsc_ref, gg_ref])
        xv, dhv = x_ref[...], dh_ref[...]
        r = lax.rsqrt(jnp.mean(xv * xv, axis=-1, keepdims=True) + EPS)
        xn = xv * r
        dsh_ref[...] += _colsum(dhv)
        dsc_ref[...] += _colsum(dhv * (xn * g_ref[...]))
        dm = dhv * sc_ref[...]
        gg_ref[...] += _colsum(dm * xn)
        dxn = dm * g_ref[...]
        dx_ref[...] = gx_ref[...] + r * (dxn - xn * jnp.mean(dxn * xn, axis=-1, keepdims=True))

    return pl.pallas_call(
        body, name=name, grid=(s // t,),
        in_specs=[_rowspec(t, d), _rowspec(t, d), _rowspec(t, d), _vecspec(d), _vecspec(d)],
        out_specs=[_rowspec(t, d), _vecspec(d), _vecspec(d), _vecspec(d)],
        out_shape=[_sds((s, d), F32), _sds((1, d), F32), _sds((1, d), F32), _sds((1, d), F32)],
        compiler_params=_cp(("arbitrary",)),
    )(dh, x, gxo, g, sc1p)


def _ada_fwd(c_all, ada_w, ada_b_cols, *, name):
    nl, d, cols = ada_w.shape

    def body(c_ref, w_ref, b_ref, o_ref):
        ca = _silu(c_ref[...]).astype(MXU_DTYPE)
        o_ref[0] = jnp.dot(ca, w_ref[0].astype(MXU_DTYPE), preferred_element_type=F32) + b_ref[0]

    return pl.pallas_call(
        body, name=name, grid=(nl,),
        in_specs=[pl.BlockSpec((N_DEV, d), lambda l: (0, 0)), pl.BlockSpec((1, d, cols), lambda l: (l, 0, 0)),
                  pl.BlockSpec((1, 1, cols), lambda l: (l, 0, 0))],
        out_specs=pl.BlockSpec((1, N_DEV, cols), lambda l: (l, 0, 0)),
        out_shape=_sds((nl, N_DEV, cols), F32),
        compiler_params=_cp(("parallel",)),
    )(c_all, ada_w, ada_b_cols)


def _ada_bwd(c_all_t, dmod_cols, *, name):
    nl, _, cols = dmod_cols.shape
    d = c_all_t.shape[0]

    def body(c_ref, dm_ref, o_ref):
        ca = _silu(c_ref[...]).astype(MXU_DTYPE)
        o_ref[0] = jnp.dot(ca, dm_ref[0].astype(MXU_DTYPE), preferred_element_type=F32)

    return pl.pallas_call(
        body, name=name, grid=(nl,),
        in_specs=[pl.BlockSpec((d, N_DEV), lambda l: (0, 0)), pl.BlockSpec((1, N_DEV, cols), lambda l: (l, 0, 0))],
        out_specs=pl.BlockSpec((1, d, cols), lambda l: (l, 0, 0)),
        out_shape=_sds((nl, d, cols), F32),
        compiler_params=_cp(("parallel",)),
    )(c_all_t, dmod_cols)


def _adamw(gparts, w, m, v, *, name):
    shape = w.shape
    cols = shape[-1]
    per_layer = isinstance(gparts, (list, tuple))
    nl = shape[0] if per_layer else 1
    rows = w.size // cols // nl
    glist = list(gparts) if per_layer else [gparts]
    npart = glist[0].shape[0]
    glist = [g.reshape(npart, rows, cols) for g in glist]
    w3, m3, v3 = (a.reshape(nl, rows, cols) for a in (w, m, v))
    budget = 2 * 1024 * 1024
    fits = [t for t in range(min(rows, 256) // 8 * 8, 7, -8)
            if rows % t == 0 and npart * t * cols * glist[0].dtype.itemsize <= budget]
    t = fits[0] if fits else rows
    nb = rows // t

    def body(*refs):
        g_refs = refs[:nl]
        w_ref, m_ref, v_ref, go_ref, d_ref, mo_ref, vo_ref, g_s = refs[nl:]
        layer = pl.program_id(0)
        for l in range(nl):
            @pl.when(layer == l)
            def _(l=l):
                g = g_refs[l][0].astype(F32)
                for p in range(1, npart):
                    g = g + g_refs[l][p].astype(F32)
                g_s[...] = g

        g = g_s[...]
        mn = ADAM_B1 * m_ref[0] + (1.0 - ADAM_B1) * g
        vn = ADAM_B2 * v_ref[0] + (1.0 - ADAM_B2) * (g * g)
        m_hat = mn / (1.0 - ADAM_B1 ** ADAM_STEP)
        v_hat = vn / (1.0 - ADAM_B2 ** ADAM_STEP)
        go_ref[0] = g
        d_ref[0] = -ADAM_LR * (m_hat / (jnp.sqrt(v_hat) + ADAM_EPS) + ADAM_WD * w_ref[0])
        mo_ref[0] = mn
        vo_ref[0] = vn

    def g_map(l):
        return lambda layer, i: (0, jnp.where(layer == l, i, jnp.where(layer < l, 0, nb - 1)), 0)

    spec = pl.BlockSpec((1, t, cols), lambda layer, i: (layer, i, 0))
    outs = pl.pallas_call(
        body, name=name, grid=(nl, nb),
        in_specs=[pl.BlockSpec((npart, t, cols), g_map(l)) for l in range(nl)] + [spec, spec, spec],
        out_specs=[spec] * 4, out_shape=[_sds((nl, rows, cols), F32)] * 4,
        scratch_shapes=[pltpu.VMEM((t, cols), F32)],
        compiler_params=_cp(("arbitrary", "arbitrary")),
    )(*glist, w3, m3, v3)
    return tuple(o.reshape(shape) for o in outs)


_ANY = pl.BlockSpec(memory_space=pl.ANY)


def _all_gather(blocks, *, name):
    na = len(blocks)

    def body(*refs):
        x_refs, out_refs = refs[:na], refs[na:2 * na]
        send_sems, recv_sems, local_sems = refs[2 * na:]
        x, y, c = lax.axis_index("x"), lax.axis_index("y"), lax.axis_index("c")
        me, sibling = (x, y, c), (x, y, 1 - c)
        chips = [(1 - x, y), (x, 1 - y), (1 - x, 1 - y)]

        def slot(a, px, py, pc):
            return out_refs[a].at[4 * px + 2 * py + pc]

        def copy(a, k, blk, to, src=None):
            return pltpu.make_async_remote_copy(
                src_ref=slot(a, *blk) if src is None else src, dst_ref=slot(a, *blk),
                send_sem=send_sems.at[7 * a + k], recv_sem=recv_sems.at[7 * a + k],
                device_id=to, device_id_type=MESH_ID)

        mine = [pltpu.make_async_copy(x_refs[a], slot(a, *me), local_sems.at[a]) for a in range(na)]
        for cp in mine:
            cp.start()
        first = []
        for a in range(na):
            first.append(copy(a, 0, me, sibling, src=x_refs[a]))
            first += [copy(a, 1 + j, me, (*chip, c), src=x_refs[a]) for j, chip in enumerate(chips)]
        for cp in first:
            cp.start()
        passed = []
        for a in range(na):
            for j, chip in enumerate(chips):
                copy(a, 1 + j, (*chip, c), me).wait_recv()
                fwd = copy(a, 4 + j, (*chip, c), sibling)
                fwd.start()
                passed.append(fwd)
        for a in range(na):
            copy(a, 0, sibling, me).wait_recv()
            for j, chip in enumerate(chips):
                copy(a, 4 + j, (*chip, 1 - c), me).wait_recv()
        for cp in first + passed:
            cp.wait_send()
        for cp in mine:
            cp.wait()

    outs = pl.pallas_call(
        body, name=name, in_specs=[_ANY] * na, out_specs=[_ANY] * na,
        out_shape=[_sds((N_DEV,) + b.shape, b.dtype) for b in blocks],
        scratch_shapes=[pltpu.SemaphoreType.DMA((7 * na,)), pltpu.SemaphoreType.DMA((7 * na,)),
                        pltpu.SemaphoreType.DMA((na,))],
    )(*blocks)
    return list(outs)


_HBM = pl.BlockSpec(memory_space=pltpu.HBM)
_SEM = pl.BlockSpec(memory_space=pltpu.SEMAPHORE)
_EFFECT = pltpu.SideEffectType.DATAFLOW_SIDE_EFFECTING


def _peers(x, y, c):
    out = []
    for k in range(1, N_DEV):
        out.append((1 - x if k & 4 else x, 1 - y if k & 2 else y, 1 - c if k & 1 else c))
    return out


def _own_slots(srcs, scatter, *, name, after=None):
    na = len(srcs)
    n_extra = 0 if after is None else 1
    me = (4 * lax.axis_index("x") + 2 * lax.axis_index("y") + lax.axis_index("c")).astype(jnp.int32).reshape(1)

    def body(me_ref, *refs):
        in_refs, out_refs = refs[:na], refs[na + n_extra:]
        for a in range(na):
            out_refs[a][0] = in_refs[a][0] if scatter else in_refs[a][...]

    def slot_spec(shard):
        zeros = (0,) * len(shard)
        return pl.BlockSpec((1,) + tuple(shard), lambda i, me_ref: (me_ref[0],) + zeros)

    def whole_spec(shape):
        zeros = (0,) * len(shape)
        return pl.BlockSpec(tuple(shape), lambda i, me_ref: zeros)

    shards = [s.shape[1:] if scatter else s.shape for s in srcs]
    in_specs = [slot_spec(sh) if scatter else whole_spec(sh) for sh in shards] + [_ANY] * n_extra
    outs = pl.pallas_call(
        body, name=name,
        grid_spec=pltpu.PrefetchScalarGridSpec(
            num_scalar_prefetch=1, grid=(1,), in_specs=in_specs, out_specs=[slot_spec(sh) for sh in shards]),
        out_shape=[_sds((N_DEV,) + tuple(sh), s.dtype) for sh, s in zip(shards, srcs)],
        compiler_params=_cp(("arbitrary",)),
    )(me, *srcs, *([] if after is None else [after]))
    return list(outs)


def _exchange_copies(src_refs, land_refs, send_sems, recv_sems, scatter):
    x, y, c = lax.axis_index("x"), lax.axis_index("y"), lax.axis_index("c")
    me = 4 * x + 2 * y + c
    cps = []
    for a in range(len(src_refs)):
        for k, (px, py, pc) in enumerate(_peers(x, y, c)):
            src = src_refs[a].at[4 * px + 2 * py + pc] if scatter else src_refs[a]
            cps.append(pltpu.make_async_remote_copy(
                src_ref=src, dst_ref=land_refs[a].at[me], send_sem=send_sems.at[7 * a + k],
                recv_sem=recv_sems.at[7 * a + k], device_id=(px, py, pc), device_id_type=MESH_ID))
    return cps


def _exchange_start(srcs, lands, scatter, *, name):
    na = len(srcs)

    def body(*refs):
        src_refs, land_refs = refs[:na], refs[na:2 * na]
        send_sems, recv_sems = refs[2 * na], refs[2 * na + 1]
        token = refs[-1]
        for cp in _exchange_copies(src_refs, land_refs, send_sems, recv_sems, scatter):
            cp.start()
        token[...] = jnp.zeros(token.shape, token.dtype)

    hbm = lambda a: pltpu.HBM(a.shape, a.dtype)
    outs = pl.pallas_call(
        body, name=name,
        out_shape=(pltpu.SemaphoreType.DMA((7 * na,)), pltpu.SemaphoreType.DMA((7 * na,)),
                   *[hbm(a) for a in srcs], *[hbm(a) for a in lands], _sds((8, LANE), F32)),
        in_specs=[_HBM] * (2 * na),
        out_specs=(_SEM, _SEM, *[_HBM] * (2 * na), pl.BlockSpec(memory_space=pltpu.VMEM)),
        input_output_aliases={i: 2 + i for i in range(2 * na)},
        compiler_params=pltpu.CompilerParams(has_side_effects=_EFFECT),
    )(*[pltpu.with_memory_space_constraint(a, pltpu.HBM) for a in list(srcs) + list(lands)])
    return outs[0], outs[1], list(outs[2:2 + na]), list(outs[2 + na:2 + 2 * na]), outs[-1]


def _exchange_wait(send_sems, recv_sems, srcs, lands, after, scatter, *, name):
    na = len(srcs)

    def body(*refs):
        src_refs, land_refs = refs[:na], refs[na:2 * na]
        s_sems, r_sems = refs[2 * na], refs[2 * na + 1]
        for cp in _exchange_copies(src_refs, land_refs, s_sems, r_sems, scatter):
            cp.wait_send()
            cp.wait_recv()

    hbm = lambda a: pltpu.HBM(a.shape, a.dtype)
    outs = pl.pallas_call(
        body, name=name,
        out_shape=(*[hbm(a) for a in srcs], *[hbm(a) for a in lands]),
        in_specs=[_HBM] * (2 * na) + [_SEM, _SEM, _ANY],
        out_specs=tuple([_HBM] * (2 * na)),
        input_output_aliases={i: i for i in range(2 * na)},
        compiler_params=pltpu.CompilerParams(has_side_effects=_EFFECT),
    )(*srcs, *lands, send_sems, recv_sems, after)
    return list(outs[na:])


_WIN_SEGS = (("ql", 0, Q_LORA, SEG_QL[0]), ("kvl", Q_LORA, KV_LORA, SEG_KVL[0]),
             ("kr", Q_LORA + KV_LORA, ROPE, SEG_KR[0]), ("mg", Q_LORA + KV_LORA + ROPE, D_MLA, SEG_MG[0]),
             ("ci", Q_LORA + KV_LORA + ROPE + D_MLA, 2 * D_CONV, SEG_CI[0]),
             ("cg", Q_LORA + KV_LORA + ROPE + D_MLA + 2 * D_CONV, D_CONV, SEG_CG[0]))
_WIN_SHARD = IN_COLS // N_DEV


def _win_pieces():
    out = []
    for _, o, n, new in _WIN_SEGS:
        for j in range(N_DEV):
            lo, hi = max(o, j * _WIN_SHARD), min(o + n, (j + 1) * _WIN_SHARD)
            if lo < hi:
                out.append((j, lo - j * _WIN_SHARD, new + lo - o, hi - lo))
    return out


WIN_T = 512


def _win_assemble(w_all, *, name):
    d = w_all.shape[2]
    t = min(WIN_T, d)
    pieces = sorted(_win_pieces(), key=lambda p: p[2])
    assert all(lo % 8 == 0 and n % 8 == 0 for _, lo, _, n in pieces)

    def body(w_ref, o_ref):
        rows = [w_ref[j].astype(F32)[lo:lo + n, :] for j, lo, _, n in pieces]
        rows.append(jnp.zeros((IN_PAD - (SEG_KR[0] + ROPE), t), F32))
        o_ref[...] = jnp.concatenate(rows, axis=0).astype(o_ref.dtype)

    return pl.pallas_call(
        body, name=name, grid=(d // t,),
        in_specs=[pl.BlockSpec((N_DEV, _WIN_SHARD, t), lambda i: (0, 0, i))],
        out_specs=pl.BlockSpec((IN_PAD, t), lambda i: (0, i)), out_shape=_sds((IN_PAD, d), w_all.dtype),
        compiler_params=_cp(("parallel",)),
    )(w_all)


def _win_split(grad, *, name):
    d = grad.shape[1]
    t = min(WIN_T, d)
    by_shard = [sorted([p for p in _win_pieces() if p[0] == j], key=lambda p: p[1]) for j in range(N_DEV)]

    def body(g_ref, o_ref):
        for j in range(N_DEV):
            rows = [g_ref[new:new + n, :] for _, _, new, n in by_shard[j]]
            o_ref[j] = jnp.concatenate(rows, axis=0).astype(o_ref.dtype)

    return pl.pallas_call(
        body, name=name, grid=(d // t,),
        in_specs=[pl.BlockSpec((IN_PAD, t), lambda i: (0, i))],
        out_specs=pl.BlockSpec((N_DEV, _WIN_SHARD, t), lambda i: (0, 0, i)),
        out_shape=_sds((N_DEV, _WIN_SHARD, d), WIRE_DTYPE),
        compiler_params=_cp(("parallel",)),
    )(grad)


def _cols_to_shards(a):
    r, n = a.shape
    return a.reshape(r, N_DEV, n // N_DEV).transpose(1, 0, 2)


def _shards_to_cols(a):
    nd, r, w = a.shape
    return a.transpose(1, 0, 2).reshape(r, nd * w)


def _win_permute(w_in):
    o_ql, o_kvl, o_kr, o_mg = 0, Q_LORA, Q_LORA + KV_LORA, Q_LORA + KV_LORA + ROPE
    o_ci = o_mg + D_MLA
    o_cg = o_ci + 2 * D_CONV
    seg = lambda o, n: w_in[:, o:o + n]
    pad = jnp.zeros((w_in.shape[0], IN_PAD - (SEG_KR[0] + ROPE)), w_in.dtype)
    return jnp.concatenate([seg(o_ci, 2 * D_CONV), seg(o_mg, D_MLA), seg(o_cg, D_CONV), seg(o_ql, Q_LORA),
                            seg(o_kvl, KV_LORA), seg(o_kr, ROPE), pad], axis=1)


def _win_unpermute(g):
    seg = lambda s, n=None: g[:, s[0]:s[0] + (s[1] if n is None else n)]
    return jnp.concatenate([seg(SEG_QL), seg(SEG_KVL), seg(SEG_KR, ROPE), seg(SEG_MG), seg(SEG_CI), seg(SEG_CG)], axis=1)


def _qup_permute(w):
    w3 = w.reshape(w.shape[0], N_HEADS, QK_DIM)
    nope = w3[:, :, :NOPE].reshape(w.shape[0], N_HEADS * NOPE)
    rope = jnp.pad(w3[:, :, NOPE:], ((0, 0), (0, 0), (0, LANE - ROPE))).reshape(w.shape[0], N_HEADS * LANE)
    return jnp.concatenate([nope, rope], axis=1)


def _qup_unpermute(g):
    r = g.shape[0]
    nope = g[:, :N_HEADS * NOPE].reshape(r, N_HEADS, NOPE)
    rope = g[:, N_HEADS * NOPE:].reshape(r, N_HEADS, LANE)[:, :, :ROPE]
    return jnp.concatenate([nope, rope], axis=2).reshape(r, N_HEADS * QK_DIM)


def _norm_tiles(g):
    return g[:NOPE].reshape(1, LANE), jnp.pad(g[NOPE:], (0, LANE - ROPE)).reshape(1, LANE)


def _norm_untile(gt):
    return jnp.concatenate([gt[0, :NOPE], gt[0, LANE:LANE + ROPE]])


def _rope_tiles(positions):
    inv_freq = 1.0 / (ROPE_THETA ** (jnp.arange(0, ROPE, 2, dtype=F32) / ROPE))
    ang = positions.astype(F32)[:, None] * inv_freq
    cos, sin = jnp.cos(ang), jnp.sin(ang)
    zq = jnp.zeros_like(cos)
    c_t = jnp.concatenate([cos, cos, zq, zq], axis=1)
    s1_t = jnp.concatenate([-sin, zq, zq, zq], axis=1)
    s2_t = jnp.concatenate([zq, sin, zq, zq], axis=1)
    return c_t, s1_t, s2_t


_BIG = ("w_in", "w_q_up", "w_kv_up", "w_pw", "w_out")
_COL_SHARDED = ("w_in", "w_q_up", "w_kv_up")


def _pack_rows(arrs):
    return jnp.concatenate([a.reshape(-1, LANE) for a in arrs], axis=0)


def _unpack_rows(buf, shapes):
    out, r0 = [], 0
    lead = buf.shape[:-2]
    for shp in shapes:
        n = math.prod(shp) // LANE
        out.append(buf[..., r0:r0 + n, :].reshape(lead + tuple(shp)))
        r0 += n
    return out


_SMALL = (("dmod", 3 * D_MODEL), ("norm_g", D_MODEL), ("q_lat_g", Q_LORA), ("kv_lat_g", KV_LORA),
          ("q_norm_g", 2 * LANE), ("k_norm_g", 2 * LANE), ("glu_b", 2 * D_CONV), ("dw_w", HALO * D_CONV),
          ("dw_b", D_CONV), ("conv_ln_g", D_CONV), ("conv_ln_b", D_CONV), ("b_pw", D_CONV))


def _layer_fwd(x, p, rope, l, late=None):
    n = lambda s: f"{s}_l{l}"
    c_t, s1_t, s2_t = rope
    h = _prenorm(x, p["norm_g"], p["shift"], p["sc1p"], name=n("prenorm"))
    z = _mm(h, p["w_in"], tb=True, name=n("in_proj"), tn=IN_TILE, n_outer=True)
    if late is not None:
        p = {**p, **late(z)}
    qn, kn = _lat_norm(z, p["q_lat_g"], p["kv_lat_g"], name=n("lat_norm"))
    q_raw = _mm(qn, p["w_q_up"], name=n("q_up"), tn=1024)
    kv = _mm(kn, p["w_kv_up"], name=n("kv_up"), tn=1024)
    qf, kf, vf = _qk_prep(q_raw, kv, z, c_t, s1_t, s2_t, *p["qk_tiles"], name=n("qk_prep"))
    o, lse = _flash_fwd(qf, kf, vf, name=n("flash_fwd"))
    u1, u3 = _conv_fwd(z, p["glu_b"], p["dw_w"], p["dw_b"], p["conv_ln_g"], p["conv_ln_b"], name=n("conv_fwd"))
    u4m = _mm(u3, p["w_pw"], name=n("pw"), tn=1024)
    cat = _gate_cat(o, z, u4m, p["b_pw"], name=n("gate_cat"))
    y, x_next = _mm(cat, p["w_out"], name=n("out_proj"), tn=1024, residual=(x, p["gate"]))
    saved = dict(x=x, h=h, z=z, qn=qn, kn=kn, q_raw=q_raw, kv=kv, qf=qf, kf=kf, vf=vf, o=o, lse=lse,
                 u1=u1, u3=u3, u4m=u4m, cat=cat, y=y)
    return x_next, saved, p


def _layer_bwd(gxo, p, sv, rope, l, hook_rest=None, hook_w_in=None):
    n = lambda s: f"{s}_l{l}"
    c_t, s1_t, s2_t = rope
    z = sv["z"]
    dy, dgate = _out_bwd(gxo, sv["y"], p["gate"], name=n("out_bwd"))
    g_w_out = _mm(sv["cat"], dy, ta=True, name=n("g_w_out"), tm=1024, tn=1024)
    dcat = _mm(dy, p["w_out"], tb=True, name=n("d_cat"), tn=1024)
    do, delta, du4, g_b_pw, dz = _gate_bwd(dcat, sv["o"], z, sv["u4m"], p["b_pw"], name=n("gate_bwd"))
    g_w_pw = _mm(sv["u3"], du4, ta=True, name=n("g_w_pw"), tm=1024, tn=1024, tk=512)
    du3 = _mm(du4, p["w_pw"], tb=True, name=n("d_u3"), tn=1024)
    dz, g_ln_g, g_ln_b, g_dw_b, g_glu_b, g_dw_w = _conv_bwd(
        du3, sv["u1"], z, dz, p["glu_b"], p["dw_w"], p["conv_ln_g"], p["conv_ln_b"], name=n("conv_bwd"))
    t_att = min(ATT_T, z.shape[0])
    to_lanes = lambda a: a.reshape(N_HEADS, z.shape[0] // t_att, 1, t_att)
    dqf, dkf, dvf = _flash_bwd(sv["qf"], sv["kf"], sv["vf"], do,
                               to_lanes(sv["lse"][:, :, 0]), to_lanes(delta), name=n("flash_bwd"))
    dq_raw, dkv, dkr, g_qn, g_kn = _qk_bwd(dqf, dkf, dvf, sv["q_raw"], sv["kv"], z, c_t, s1_t, s2_t,
                                            *p["qk_tiles"], name=n("qk_bwd"))
    g_w_q_up = _mm(sv["qn"], dq_raw, ta=True, name=n("g_w_q_up"), tm=512, tn=1024, tk=512)
    dqn = _mm(dq_raw, p["w_q_up"], tb=True, name=n("d_qn"))
    g_w_kv_up = _mm(sv["kn"], dkv, ta=True, name=n("g_w_kv_up"), tm=256, tn=1024, tk=512)
    dkn = _mm(dkv, p["w_kv_up"], tb=True, name=n("d_kn"))
    dz, g_ql, g_kvl = _lat_bwd(dqn, dkn, dkr, z, dz, p["q_lat_g"], p["kv_lat_g"], name=n("lat_bwd"))
    big = dict(w_q_up=g_w_q_up, w_kv_up=g_w_kv_up, w_pw=g_w_pw, w_out=g_w_out)
    after = None if hook_rest is None else hook_rest(big)
    g_w_in = _mm(dz, sv["h"], ta=True, name=n("g_w_in"), tm=512, tn=1024, after=after)
    big["w_in"] = g_w_in
    after = None if hook_w_in is None else hook_w_in(g_w_in)
    dh = _mm(dz, p["w_in"], name=n("d_h"), tn=1024, after=after)
    dx, dshift, dscale, g_norm = _prenorm_bwd(dh, sv["x"], gxo, p["norm_g"], p["sc1p"], name=n("prenorm_bwd"))
    small = dict(dmod=jnp.concatenate([dshift, dscale, dgate], axis=1), norm_g=g_norm, q_lat_g=g_ql, kv_lat_g=g_kvl,
                 q_norm_g=g_qn, k_norm_g=g_kn, glu_b=g_glu_b, dw_w=g_dw_w, dw_b=g_dw_b,
                 conv_ln_g=g_ln_g, conv_ln_b=g_ln_b, b_pw=g_b_pw)
    return dx, big, small


def _layer_params(l, full, mod_l, small):
    d = D_MODEL
    row = lambda a: a.reshape(1, -1)
    shift, scale, gate = mod_l[:, :d], mod_l[:, d:2 * d], mod_l[:, 2 * d:]
    dw_w = jnp.pad(full["dw_w"][l], ((0, HALO - CONV_K), (0, 0)))
    return dict(
        shift=shift, sc1p=1.0 + scale, gate=gate, norm_g=row(small["norm_g"][l]),
        **{k: full[k][l] for k in _BIG if k in full}, dw_w=dw_w,
        q_lat_g=row(small["q_lat_g"][l]), kv_lat_g=row(small["kv_lat_g"][l]),
        qk_tiles=_norm_tiles(small["q_norm_g"][l]) + _norm_tiles(small["k_norm_g"][l]),
        glu_b=row(small["glu_b"][l]), dw_b=row(small["dw_b"][l]), conv_ln_g=row(small["conv_ln_g"][l]),
        conv_ln_b=row(small["conv_ln_b"][l]), b_pw=row(small["b_pw"][l]))


def kernel(x, c, positions, ada_w, ada_b, norm_g, w_in, q_lat_g, w_q_up, kv_lat_g, w_kv_up, q_norm_g, k_norm_g, glu_b, dw_w, dw_b, conv_ln_g, conv_ln_b, w_pw, b_pw, w_out, loss_target, m_ada_w, m_ada_b, m_norm_g, m_w_in, m_q_lat_g, m_w_q_up, m_kv_lat_g, m_w_kv_up, m_q_norm_g, m_k_norm_g, m_glu_b, m_dw_w, m_dw_b, m_conv_ln_g, m_conv_ln_b, m_w_pw, m_b_pw, m_w_out, v_ada_w, v_ada_b, v_norm_g, v_w_in, v_q_lat_g, v_w_q_up, v_kv_lat_g, v_w_kv_up, v_q_norm_g, v_k_norm_g, v_glu_b, v_dw_w, v_dw_b, v_conv_ln_g, v_conv_ln_b, v_w_pw, v_b_pw, v_w_out):
    names = ("ada_w", "ada_b", "norm_g", "w_in", "q_lat_g", "w_q_up", "kv_lat_g", "w_kv_up", "q_norm_g",
             "k_norm_g", "glu_b", "dw_w", "dw_b", "conv_ln_g", "conv_ln_b", "w_pw", "b_pw", "w_out")
    w_loc = dict(zip(names, (ada_w, ada_b, norm_g, w_in, q_lat_g, w_q_up, kv_lat_g, w_kv_up, q_norm_g, k_norm_g,
                             glu_b, dw_w, dw_b, conv_ln_g, conv_ln_b, w_pw, b_pw, w_out)))
    m_loc = dict(zip(names, (m_ada_w, m_ada_b, m_norm_g, m_w_in, m_q_lat_g, m_w_q_up, m_kv_lat_g, m_w_kv_up,
                             m_q_norm_g, m_k_norm_g, m_glu_b, m_dw_w, m_dw_b, m_conv_ln_g, m_conv_ln_b, m_w_pw,
                             m_b_pw, m_w_out)))
    v_loc = dict(zip(names, (v_ada_w, v_ada_b, v_norm_g, v_w_in, v_q_lat_g, v_w_q_up, v_kv_lat_g, v_w_kv_up,
                             v_q_norm_g, v_k_norm_g, v_glu_b, v_dw_w, v_dw_b, v_conv_ln_g, v_conv_ln_b, v_w_pw,
                             v_b_pw, v_w_out)))
    nl, d = N_LAYERS, D_MODEL
    me = 4 * lax.axis_index("x") + 2 * lax.axis_index("y") + lax.axis_index("c")
    x2, tgt = x[0], loss_target[0]
    ada_cols = ada_w.shape[-1]

    c_all = _all_gather([c.reshape(d // LANE, LANE)], name="gather_c")[0].reshape(N_DEV, d)
    ada_b_cols = lax.dynamic_slice_in_dim(ada_b, me * ada_cols, ada_cols, axis=1).reshape(nl, 1, ada_cols)
    mod_cols = _ada_fwd(c_all, ada_w, ada_b_cols, name="ada_fwd")
    mod_all = _all_gather([mod_cols], name="gather_mod")[0]
    mod_me = lax.dynamic_index_in_dim(mod_all, me, axis=2, keepdims=False)
    mod = mod_me.transpose(1, 0, 2).reshape(nl, 1, N_DEV * ada_cols)

    dw_pad = jnp.pad(dw_w, ((0, 0), (0, HALO - CONV_K), (0, 0)))
    tr = lambda a: jnp.swapaxes(a, 1, 2)
    w_loc, m_loc, v_loc = ({**dd, "w_in": tr(dd["w_in"])} for dd in (w_loc, m_loc, v_loc))
    wire = {k: w_loc[k].astype(WIRE_DTYPE) for k in _BIG}
    w_in_all0, dw_all = _all_gather([wire["w_in"][0], dw_pad], name="gather_w_in_l0")
    rest0 = [wire[k][0] for k in _BIG[1:]]
    fly_r0 = _exchange_start(rest0, _own_slots(rest0, False, name="own_weights_l0_rest", after=w_in_all0), False,
                             name="gather_start_l0_rest")
    fly_w1 = {}

    def layout_rest(parts):
        return dict(w_q_up=_qup_permute(_shards_to_cols(parts[0])), w_kv_up=_shards_to_cols(parts[1]),
                    w_pw=parts[2].reshape(D_CONV, D_CONV), w_out=parts[3].reshape(D_MLA + D_CONV, d))

    small_in = dict(norm_g=norm_g, q_lat_g=q_lat_g, kv_lat_g=kv_lat_g, q_norm_g=q_norm_g, k_norm_g=k_norm_g,
                    glu_b=glu_b, dw_b=dw_b, conv_ln_g=conv_ln_g, conv_ln_b=conv_ln_b, b_pw=b_pw)
    dw_full = [_shards_to_cols(dw_all[:, l])[:CONV_K] for l in range(nl)]
    rope = _rope_tiles(positions[0])

    def layer_params(l, w_in_all, rest, mod_l):
        full = dict(w_in={l: _win_assemble(w_in_all, name=f"w_in_assemble_l{l}")}, dw_w=dw_full)
        if rest is not None:
            full.update({k: {l: a} for k, a in layout_rest(rest).items()})
        return _layer_params(l, full, mod_l, small_in)

    def late_l0(z):
        parts = _exchange_wait(*fly_r0[:4], z, False, name="gather_wait_l0_rest")
        src1 = [wire[k][1] for k in _BIG]
        fly_w1["x"] = _exchange_start(src1, _own_slots(src1, False, name="own_weights_l1", after=parts[0]), False,
                                      name="gather_start_l1")
        late = layout_rest(parts)
        late["q_lat_g"] = small_in["q_lat_g"][0].reshape(1, -1) + fly_w1["x"][4][0, 0]
        return late

    params, saved = [None] * nl, [None] * nl
    p0 = layer_params(0, w_in_all0, None, mod[0] + fly_r0[4][0, 0])
    xs, saved[0], params[0] = _layer_fwd(x2, p0, rope, 0, late=late_l0)
    parts1 = _exchange_wait(*fly_w1["x"][:4], xs, False, name="gather_wait_l1")
    params[1] = layer_params(1, parts1[0], parts1[1:], mod[1])
    xs, saved[1], _ = _layer_fwd(xs, params[1], rope, 1)
    gx, loss_part = _loss_head(xs, tgt, name="loss_head")
    loss = lax.psum(loss_part[0, 0], ("x", "y", "c"))

    def shard_major(k, g):
        if k == "w_q_up":
            g = _qup_unpermute(g)
        if k in _COL_SHARDED:
            return _cols_to_shards(g)
        return g.reshape((N_DEV, g.shape[0] // N_DEV, g.shape[1]))

    def scatter_start(send, tag):
        lands = _own_slots(send, True, name=f"own_grads_{tag}")
        return _exchange_start(send, lands, True, name=f"scatter_start_{tag}")

    def wire_rest(big):
        return [shard_major(k, big[k]).astype(WIRE_DTYPE) for k in _BIG[1:]]

    big_g, small_g, flying = [None] * nl, [None] * nl, {}
    gx, big_g[1], small_g[1] = _layer_bwd(gx, params[1], saved[1], rope, 1)
    flying["l1"] = scatter_start([_win_split(big_g[1]["w_in"], name="w_in_split_l1")] + wire_rest(big_g[1]), "l1")
    p0 = dict(params[0])
    p0["gate"] = p0["gate"] + flying["l1"][4][0, 0]

    def start_rest_l0(big):
        flying["l0_rest"] = scatter_start(wire_rest(big), "l0_rest")
        return flying["l0_rest"][4]

    def start_w_in_l0(g_w_in):
        flying["l0_w_in"] = scatter_start([_win_split(g_w_in, name="w_in_split_l0")], "l0_w_in")
        return flying["l0_w_in"][4]

    gx, big_g[0], small_g[0] = _layer_bwd(gx, p0, saved[0], rope, 0, hook_rest=start_rest_l0,
                                          hook_w_in=start_w_in_l0)

    tile = 8 * LANE
    padded = [(k, nn, -(-nn // tile) * tile) for k, nn in _SMALL]
    spk = jnp.concatenate([jnp.pad(small_g[l][k].reshape(-1), (0, np_ - nn)).reshape(-1, LANE)
                           for l in range(nl) for k, nn, np_ in padded], axis=0)
    s_all = _all_gather([spk], name="gather_small_grads")[0]
    s_rows = sum(np_ for _, _, np_ in padded) // LANE
    s_all = s_all.reshape(N_DEV, nl, s_rows, LANE)
    s_parts = {k: a[..., :nn] for (k, nn, _), a in
               zip(padded, _unpack_rows(s_all, [(np_,) for _, _, np_ in padded]))}

    dmod_all = s_parts["dmod"]
    dmod_cols = lax.dynamic_slice_in_dim(dmod_all, me * ada_cols, ada_cols, axis=2).transpose(1, 0, 2)
    g_ada_w = _ada_bwd(c_all.T, dmod_cols, name="ada_bwd")
    gp = {}
    gp["ada_w"] = g_ada_w[None]
    gp["ada_b"] = dmod_all
    for k in ("norm_g", "q_lat_g", "kv_lat_g", "glu_b", "dw_b", "conv_ln_g", "conv_ln_b", "b_pw"):
        gp[k] = s_parts[k]
    for k in ("q_norm_g", "k_norm_g"):
        t = s_parts[k]
        gp[k] = jnp.concatenate([t[..., :NOPE], t[..., LANE:LANE + ROPE]], axis=-1)
    dw_g = s_parts["dw_w"].reshape(N_DEV, nl, HALO, D_CONV)[:, :, :CONV_K]
    gp["dw_w"] = lax.dynamic_slice_in_dim(dw_g, me * LANE, LANE, axis=3)

    res = {k: _adamw(gp[k], w_loc[k], m_loc[k], v_loc[k], name=f"adamw_{k}") for k in names if k not in _BIG}
    arrived = [None] * nl
    arrived[1] = _exchange_wait(*flying["l1"][:4], gx, True, name="scatter_wait_l1")
    rest0 = _exchange_wait(*flying["l0_rest"][:4], gx, True, name="scatter_wait_l0_rest")
    arrived[0] = _exchange_wait(*flying["l0_w_in"][:4], res["ada_w"][1], True, name="scatter_wait_l0_w_in") + rest0
    for i, k in enumerate(_BIG):
        res[k] = _adamw([arrived[l][i] for l in range(nl)], w_loc[k], m_loc[k], v_loc[k], name=f"adamw_{k}")
    res["w_in"] = tuple(tr(a) for a in res["w_in"])
    out = [loss, gx[None]]
    for idx in range(4):
        out += [res[k][idx] for k in names]
    return tuple(out)
```

```python
import functools
import math

import jax
import jax.numpy as jnp
from jax import lax
from jax.experimental import pallas as pl
from jax.experimental.pallas import tpu as pltpu

F32 = jnp.float32
MXU_DTYPE = jnp.bfloat16
WIRE_DTYPE = jnp.bfloat16

D_MODEL = 2048
N_LAYERS = 2
N_DEV = 8
N_HEADS = 8
NOPE = 128
ROPE = 64
V_DIM = 128
QK_DIM = NOPE + ROPE
Q_LORA = 512
KV_LORA = 256
D_MLA = N_HEADS * V_DIM
D_CONV = 1024
CONV_K = 31
ROPE_THETA = 10000.0
EPS = 1e-6
LANE = 128
HEAD_PAD = 2 * LANE
HALO = 32

SEG_CI = (0, 2 * D_CONV)
SEG_MG = (2 * D_CONV, D_MLA)
SEG_CG = (2 * D_CONV + D_MLA, D_CONV)
SEG_QL = (2 * D_CONV + D_MLA + D_CONV, Q_LORA)
SEG_KVL = (SEG_QL[0] + Q_LORA, KV_LORA)
SEG_KR = (SEG_KVL[0] + KV_LORA, LANE)
SEG_LAT = (SEG_QL[0], 1024)
IN_PAD = SEG_LAT[0] + SEG_LAT[1]
IN_TILE = IN_PAD // 4
assert SEG_KR[0] + LANE <= IN_PAD and SEG_LAT[0] % SEG_LAT[1] == 0
IN_COLS = Q_LORA + KV_LORA + ROPE + D_MLA + 2 * D_CONV + D_CONV

ADAM_LR = 0.001
ADAM_B1 = 0.9
ADAM_B2 = 0.999
ADAM_EPS = 1e-08
ADAM_WD = 0.01
ADAM_STEP = 10

VMEM_LIMIT = 56 * 1024 * 1024
ATT_T = 512
ROW_T = 256
CONV_T = 128
MESH_ID = pl.DeviceIdType.MESH


def _cp(sem=None):
    kw = dict(vmem_limit_bytes=VMEM_LIMIT)
    if sem is not None:
        kw["dimension_semantics"] = sem
    return pltpu.CompilerParams(**kw)


def _sds(shape, dtype):
    return jax.ShapeDtypeStruct(shape, dtype)


def _silu(x):
    return x * jax.nn.sigmoid(x)


def _dsilu(x):
    s = jax.nn.sigmoid(x)
    return s * (1.0 + x * (1.0 - s))


def _rowspec(t, width, col=0):
    return pl.BlockSpec((t, width), lambda i: (i, col))


def _vecspec(width):
    return pl.BlockSpec((1, width), lambda i: (0, 0))


def _colsum(v):
    return jnp.sum(v, axis=0, keepdims=True)


def _mm(a, b, *, name, ta=False, tb=False, out_dtype=F32, tm=512, tn=512, tk=None, n_outer=False, after=None,
        residual=None):
    if ta:
        kdim, m = a.shape
    else:
        m, kdim = a.shape
    if tb:
        n, k2 = b.shape
    else:
        k2, n = b.shape
    assert kdim == k2, (a.shape, b.shape)
    tm, tn = min(tm, m), min(tn, n)
    tk = kdim if tk is None else min(tk, kdim)
    assert m % tm == 0 and n % tn == 0 and kdim % tk == 0, (m, n, kdim, tm, tn, tk)
    nk = kdim // tk
    dims = (((0 if ta else 1,), (1 if tb else 0,)), ((), ()))

    n_extra = 0 if after is None else 1
    assert residual is None or nk == 1

    def body(a_ref, b_ref, *rest):
        if residual is not None:
            x_ref, gate_ref = rest[:2]
            rest = rest[2:]
        o_ref, scratch = rest[n_extra], rest[n_extra + 1:]
        prod = lax.dot_general(a_ref[...].astype(MXU_DTYPE), b_ref[...].astype(MXU_DTYPE), dims,
                               preferred_element_type=F32)
        if residual is not None:
            o_ref[...] = prod.astype(o_ref.dtype)
            scratch[0][...] = x_ref[...] + gate_ref[...] * prod
        elif nk == 1:
            o_ref[...] = prod.astype(o_ref.dtype)
        else:
            acc = scratch[0]
            k = pl.program_id(2)

            @pl.when(k == 0)
            def _():
                acc[...] = prod

            @pl.when(k > 0)
            def _():
                acc[...] += prod

            @pl.when(k == nk - 1)
            def _():
                o_ref[...] = acc[...].astype(o_ref.dtype)

    if n_outer:
        ij = lambda g0, g1: (g1, g0)
        grid = (n // tn, m // tm, nk)
    else:
        ij = lambda g0, g1: (g0, g1)
        grid = (m // tm, n // tn, nk)

    def a_map(g0, g1, k):
        i, _ = ij(g0, g1)
        return (k, i) if ta else (i, k)

    def b_map(g0, g1, k):
        _, j = ij(g0, g1)
        return (j, k) if tb else (k, j)

    def o_map(g0, g1, k):
        return ij(g0, g1)

    in_specs = [pl.BlockSpec((tk, tm) if ta else (tm, tk), a_map), pl.BlockSpec((tn, tk) if tb else (tk, tn), b_map)]
    operands = [a, b]
    out_specs, out_shape = pl.BlockSpec((tm, tn), o_map), _sds((m, n), out_dtype)
    if residual is not None:
        in_specs += [pl.BlockSpec((tm, tn), o_map), pl.BlockSpec((1, tn), lambda g0, g1, k: (0, ij(g0, g1)[1]))]
        operands += list(residual)
        out_specs, out_shape = [out_specs, pl.BlockSpec((tm, tn), o_map)], [out_shape, _sds((m, n), F32)]
    if after is not None:
        in_specs.append(_ANY)
        operands.append(after)
    return pl.pallas_call(
        body, name=name, grid=grid, in_specs=in_specs, out_specs=out_specs, out_shape=out_shape,
        scratch_shapes=[pltpu.VMEM((tm, tn), F32)] if nk > 1 else [],
        compiler_params=_cp(("parallel", "parallel", "arbitrary")),
    )(*operands)


def _prenorm(x, g, shift, sc1p, *, name):
    s, d = x.shape
    t = min(ROW_T, s)

    def body(x_ref, g_ref, sh_ref, sc_ref, h_ref):
        xv = x_ref[...]
        r = lax.rsqrt(jnp.mean(xv * xv, axis=-1, keepdims=True) + EPS)
        h_ref[...] = ((xv * r) * g_ref[...] * sc_ref[...] + sh_ref[...]).astype(h_ref.dtype)

    return pl.pallas_call(
        body, name=name, grid=(s // t,),
        in_specs=[_rowspec(t, d), _vecspec(d), _vecspec(d), _vecspec(d)],
        out_specs=_rowspec(t, d), out_shape=_sds((s, d), MXU_DTYPE),
        compiler_params=_cp(("parallel",)),
    )(x, g, shift, sc1p)


def _lat_norm(z, g_ql, g_kvl, *, name):
    s = z.shape[0]
    t = min(ROW_T, s)

    def body(ql_ref, kvl_ref, gq_ref, gk_ref, qn_ref, kn_ref):
        for src, g_ref, dst in ((ql_ref, gq_ref, qn_ref), (kvl_ref, gk_ref, kn_ref)):
            v = src[...]
            r = lax.rsqrt(jnp.mean(v * v, axis=-1, keepdims=True) + EPS)
            dst[...] = ((v * r) * g_ref[...]).astype(dst.dtype)

    return pl.pallas_call(
        body, name=name, grid=(s // t,),
        in_specs=[_rowspec(t, Q_LORA, SEG_QL[0] // Q_LORA), _rowspec(t, KV_LORA, SEG_KVL[0] // KV_LORA),
                  _vecspec(Q_LORA), _vecspec(KV_LORA)],
        out_specs=[_rowspec(t, Q_LORA), _rowspec(t, KV_LORA)],
        out_shape=[_sds((s, Q_LORA), MXU_DTYPE), _sds((s, KV_LORA), MXU_DTYPE)],
        compiler_params=_cp(("parallel",)),
    )(z, z, g_ql, g_kvl)


def _rope_fwd(r, c_t, s1_t, s2_t):
    return r * c_t + pltpu.roll(r, LANE - ROPE // 2, 1) * s1_t + pltpu.roll(r, ROPE // 2, 1) * s2_t


def _rope_bwd(d, c_t, s1_t, s2_t):
    return d * c_t + pltpu.roll(d * s1_t, ROPE // 2, 1) + pltpu.roll(d * s2_t, LANE - ROPE // 2, 1)


def _lanesum(v):
    return jnp.sum(v, axis=-1, keepdims=True)


def _qk_prep(q_raw, kv, z, c_t, s1_t, s2_t, gqn, gqr, gkn, gkr, *, name):
    s = q_raw.shape[0]
    t = min(ROW_T, s)
    scale = 1.0 / math.sqrt(QK_DIM)

    def body(q_ref, kv_ref, kr_ref, c_ref, s1_ref, s2_ref, gqn_ref, gqr_ref, gkn_ref, gkr_ref,
             qf_ref, kf_ref, vf_ref):
        c_v, s1_v, s2_v = c_ref[...], s1_ref[...], s2_ref[...]
        kr = kr_ref[...]
        kr_ss = _lanesum(kr * kr)
        for h in range(N_HEADS):
            n = q_ref[:, h * LANE:(h + 1) * LANE]
            r = q_ref[:, N_HEADS * LANE + h * LANE:N_HEADS * LANE + (h + 1) * LANE]
            rs = lax.rsqrt((_lanesum(n * n) + _lanesum(r * r)) * (1.0 / QK_DIM) + EPS)
            qf_ref[h, :, 0:LANE] = (((n * rs) * gqn_ref[...]) * scale).astype(qf_ref.dtype)
            rr = _rope_fwd((r * rs) * gqr_ref[...], c_v, s1_v, s2_v)
            qf_ref[h, :, LANE:HEAD_PAD] = (rr * scale).astype(qf_ref.dtype)

            n = kv_ref[:, h * 2 * LANE:h * 2 * LANE + LANE]
            rs = lax.rsqrt((_lanesum(n * n) + kr_ss) * (1.0 / QK_DIM) + EPS)
            kf_ref[h, :, 0:LANE] = ((n * rs) * gkn_ref[...]).astype(kf_ref.dtype)
            kf_ref[h, :, LANE:HEAD_PAD] = _rope_fwd((kr * rs) * gkr_ref[...], c_v, s1_v, s2_v).astype(kf_ref.dtype)
            vf_ref[h, :, 0:V_DIM] = kv_ref[:, h * 2 * LANE + LANE:(h + 1) * 2 * LANE].astype(vf_ref.dtype)
            vf_ref[h, :, V_DIM:] = jnp.ones((t, V_DIM), vf_ref.dtype)

    hspec = lambda w: pl.BlockSpec((N_HEADS, t, w), lambda i: (0, i, 0))
    return pl.pallas_call(
        body, name=name, grid=(s // t,),
        in_specs=[_rowspec(t, 2 * N_HEADS * LANE), _rowspec(t, 2 * N_HEADS * LANE),
                  _rowspec(t, LANE, SEG_KR[0] // LANE),
                  _rowspec(t, LANE), _rowspec(t, LANE), _rowspec(t, LANE),
                  _vecspec(LANE), _vecspec(LANE), _vecspec(LANE), _vecspec(LANE)],
        out_specs=[hspec(HEAD_PAD), hspec(HEAD_PAD), hspec(2 * V_DIM)],
        out_shape=[_sds((N_HEADS, s, HEAD_PAD), MXU_DTYPE), _sds((N_HEADS, s, HEAD_PAD), MXU_DTYPE),
                   _sds((N_HEADS, s, 2 * V_DIM), MXU_DTYPE)],
        compiler_params=_cp(("parallel",)),
    )(q_raw, kv, z, c_t, s1_t, s2_t, gqn, gqr, gkn, gkr)


def _causal_mask(t):
    row = lax.broadcasted_iota(jnp.int32, (t, t), 0)
    col = lax.broadcasted_iota(jnp.int32, (t, t), 1)
    return col <= row


NEG = -1e30


def _flash_fwd(qf, kf, va, *, name):
    nh, s, dk = qf.shape
    dv = va.shape[-1] // 2
    t = min(ATT_T, s)
    n = s // t
    assert dv == LANE and t % LANE == 0

    def body(q_ref, k_ref, v_ref, o_ref, lse_ref, m_s, acc_s, s_buf):
        i = pl.program_id(1)
        m_s[...] = jnp.full(m_s.shape, NEG, F32)
        acc_s[...] = jnp.zeros(acc_s.shape, F32)

        def rows_of(j):
            return pl.ds(pl.multiple_of(j * t, t), t)

        def scores(qi, j):
            return lax.dot_general(q_ref[0, rows_of(qi), :], k_ref[0, rows_of(j), :], (((1,), (1,)), ((), ())),
                                   preferred_element_type=F32)

        def consume(j, slot, masked):
            sc = s_buf[slot]
            if masked:
                sc = jnp.where(_causal_mask(t), sc, NEG)
            m_prev = m_s[...]
            m_new = jnp.maximum(m_prev, jnp.max(sc, axis=-1, keepdims=True))
            alpha = jnp.exp(m_prev - m_new)
            p = jnp.exp(sc - jnp.tile(m_new, (1, t // LANE)))
            acc_s[...] = jnp.tile(alpha, (1, 2)) * acc_s[...] + jnp.dot(
                p.astype(MXU_DTYPE), v_ref[0, rows_of(j), :], preferred_element_type=F32)
            m_s[...] = m_new

        nxt = jnp.minimum(i + 1, n - 1)

        @pl.when(i == 0)
        def _():
            s_buf[2] = scores(0, 0)
            consume(0, 2, True)
            s_buf[2] = scores(nxt, 0)

        @pl.when(i > 0)
        def _():
            s_buf[1] = scores(i, 1)
            consume(0, 2, False)

            def pair(a, carry):
                s_buf[0] = scores(i, 2 * a + 2)
                consume(2 * a + 1, 1, False)
                s_buf[1] = scores(i, 2 * a + 3)
                consume(2 * a + 2, 0, False)
                return carry

            lax.fori_loop(0, (i - 1) // 2, pair, 0)

            @pl.when(i % 2 == 1)
            def _():
                s_buf[2] = scores(nxt, 0)
                consume(i, 1, True)

            @pl.when(i % 2 == 0)
            def _():
                s_buf[0] = scores(i, i)
                consume(i - 1, 1, False)
                s_buf[2] = scores(nxt, 0)
                consume(i, 0, True)

        den = acc_s[:, dv:]
        o_ref[...] = acc_s[:, :dv] / den
        lse_ref[0] = m_s[...] + jnp.log(den)

    head = lambda h, i: (h, 0, 0)
    return pl.pallas_call(
        body, name=name, grid=(nh, n),
        in_specs=[pl.BlockSpec((1, s, dk), head), pl.BlockSpec((1, s, dk), head), pl.BlockSpec((1, s, 2 * dv), head)],
        out_specs=[pl.BlockSpec((t, dv), lambda h, i: (i, h)),
                   pl.BlockSpec((1, t, LANE), lambda h, i: (h, i, 0))],
        out_shape=[_sds((s, nh * dv), F32), _sds((nh, s, LANE), F32)],
        scratch_shapes=[pltpu.VMEM((t, LANE), F32), pltpu.VMEM((t, 2 * dv), F32), pltpu.VMEM((3, t, t), F32)],
        compiler_params=_cp(("arbitrary", "arbitrary")),
    )(qf, kf, va)


def _shifted_copies(ext_ref):
    rows = ext_ref.shape[1] - 8
    for s in range(1, 8):
        ext_ref[s, 0:rows, :] = ext_ref[0, s:s + rows, :]


def _windows(ext_ref, offsets, t_rows, lane0, lanes):
    for s in range(8):
        group = [o for o in offsets if o % 8 == s]
        if not group:
            continue
        lo, hi = min(group) - s, max(group) - s
        wide = ext_ref[s, pl.ds(lo, hi - lo + t_rows), lane0:lane0 + lanes]
        for o in group:
            yield o, wide[o - s - lo:o - s - lo + t_rows]


def _dw_taps(ext_ref, w_ref, row0, t_rows, lane0, lanes, first_off):
    acc = None
    for off, win in _windows(ext_ref, [row0 + first_off + k for k in range(CONV_K)], t_rows, lane0, lanes):
        k = off - row0 - first_off
        term = w_ref[k:k + 1, lane0:lane0 + lanes] * win
        acc = term if acc is None else acc + term
    return acc


CONV_RC = 32
CONV_LC = 256


def _conv_fwd(z, glu_b, dw_w, dw_b, ln_g, ln_b, *, name):
    s = z.shape[0]
    t = min(CONV_T, s)
    c2 = 2 * D_CONV
    hb = t // HALO

    def body(zm_ref, zh_ref, gb_ref, w_ref, wb_ref, g_ref, b_ref, u1_ref, u3_ref, ext):
        i = pl.program_id(0)

        def glu(zv):
            ci = zv + gb_ref[...]
            return ci[:, :D_CONV] * jax.nn.sigmoid(ci[:, D_CONV:])

        ext[0, HALO:, :] = glu(zm_ref[...])
        ext[0, 0:HALO, :] = jnp.where(i > 0, glu(zh_ref[...]), 0.0)
        _shifted_copies(ext)
        for rc in range(0, t, CONV_RC):
            for lc in range(0, D_CONV, CONV_LC):
                acc = _dw_taps(ext, w_ref, rc, CONV_RC, lc, CONV_LC, HALO - (CONV_K - 1))
                u1_ref[rc:rc + CONV_RC, lc:lc + CONV_LC] = acc + wb_ref[:, lc:lc + CONV_LC]
        u1 = u1_ref[...]
        mu = jnp.mean(u1, axis=-1, keepdims=True)
        cen = u1 - mu
        var = jnp.mean(cen * cen, axis=-1, keepdims=True)
        u2 = (cen * lax.rsqrt(var + EPS)) * g_ref[...] + b_ref[...]
        u3_ref[...] = _silu(u2).astype(u3_ref.dtype)

    return pl.pallas_call(
        body, name=name, grid=(s // t,),
        in_specs=[_rowspec(t, c2), pl.BlockSpec((HALO, c2), lambda i: (jnp.maximum(i * hb - 1, 0), 0)),
                  _vecspec(c2), pl.BlockSpec((HALO, D_CONV), lambda i: (0, 0)), _vecspec(D_CONV),
                  _vecspec(D_CONV), _vecspec(D_CONV)],
        out_specs=[_rowspec(t, D_CONV), _rowspec(t, D_CONV)],
        out_shape=[_sds((s, D_CONV), F32), _sds((s, D_CONV), MXU_DTYPE)],
        scratch_shapes=[pltpu.VMEM((8, t + HALO, D_CONV), F32)],
        compiler_params=_cp(("parallel",)),
    )(z, z, glu_b, dw_w, dw_b, ln_g, ln_b)


def _gate_cat(o, z, u4m, b_pw, *, name):
    s = o.shape[0]
    t = min(ROW_T, s)

    def body(o_ref, mg_ref, u4_ref, cg_ref, b_ref, cat_ref):
        cat_ref[:, :D_MLA] = (o_ref[...] * _silu(mg_ref[...])).astype(cat_ref.dtype)
        cat_ref[:, D_MLA:] = ((u4_ref[...] + b_ref[...]) * _silu(cg_ref[...])).astype(cat_ref.dtype)

    return pl.pallas_call(
        body, name=name, grid=(s // t,),
        in_specs=[_rowspec(t, D_MLA), _rowspec(t, D_MLA, SEG_MG[0] // D_MLA), _rowspec(t, D_CONV),
                  _rowspec(t, D_CONV, SEG_CG[0] // D_CONV), _vecspec(D_CONV)],
        out_specs=_rowspec(t, D_MLA + D_CONV), out_shape=_sds((s, D_MLA + D_CONV), MXU_DTYPE),
        compiler_params=_cp(("parallel",)),
    )(o, z, u4m, z, b_pw)


def _loss_head(xf, target, *, name):
    s, d = xf.shape
    t = min(ROW_T, s)

    def body(x_ref, t_ref, gx_ref, loss_ref):
        @pl.when(pl.program_id(0) == 0)
        def _():
            loss_ref[...] = jnp.zeros(loss_ref.shape, F32)

        err = x_ref[...] - t_ref[...]
        gx_ref[...] = err * (1.0 / d)
        loss_ref[...] += 0.5 * jnp.sum(_lanesum(err * err) * (1.0 / d), axis=0, keepdims=True)

    return pl.pallas_call(
        body, name=name, grid=(s // t,),
        in_specs=[_rowspec(t, d), _rowspec(t, d)],
        out_specs=[_rowspec(t, d), pl.BlockSpec((1, 1), lambda i: (0, 0))],
        out_shape=[_sds((s, d), F32), _sds((1, 1), F32)],
        compiler_params=_cp(("arbitrary",)),
    )(xf, target)


def _acc_init(refs):
    @pl.when(pl.program_id(0) == 0)
    def _():
        for r in refs:
            r[...] = jnp.zeros(r.shape, r.dtype)


def _out_bwd(gxo, y, gate, *, name):
    s, d = gxo.shape
    t = min(ROW_T, s)

    def body(g_ref, y_ref, gate_ref, dy_ref, dgate_ref):
        _acc_init([dgate_ref])
        gv = g_ref[...]
        dy_ref[...] = (gv * gate_ref[...]).astype(dy_ref.dtype)
        dgate_ref[...] += _colsum(gv * y_ref[...])

    return pl.pallas_call(
        body, name=name, grid=(s // t,),
        in_specs=[_rowspec(t, d), _rowspec(t, d), _vecspec(d)],
        out_specs=[_rowspec(t, d), _vecspec(d)],
        out_shape=[_sds((s, d), MXU_DTYPE), _sds((1, d), F32)],
        compiler_params=_cp(("arbitrary",)),
    )(gxo, y, gate)


def _gate_bwd(dcat, o, z, u4m, b_pw, *, name):
    s = o.shape[0]
    t = min(ROW_T, s)
    gates = D_MLA + D_CONV
    assert SEG_CG[0] == SEG_MG[0] + D_MLA and SEG_MG[0] % gates == 0

    def body(dm_ref, dc_ref, o_ref, mg_ref, u4_ref, cg_ref, b_ref,
             do_ref, delta_ref, du4_ref, gb_ref, dz_ref):
        _acc_init([gb_ref])
        dm, ov, mg = dm_ref[...], o_ref[...], mg_ref[...]
        do = dm * _silu(mg)
        do_ref[...] = do.astype(do_ref.dtype)
        dz_ref[:, :D_MLA] = (dm * ov * _dsilu(mg)).astype(dz_ref.dtype)
        prod = do * ov
        for h in range(N_HEADS):
            delta_ref[h] = _lanesum(prod[:, h * V_DIM:(h + 1) * V_DIM])
        dc, cg = dc_ref[...], cg_ref[...]
        du4 = dc * _silu(cg)
        du4_ref[...] = du4.astype(du4_ref.dtype)
        dz_ref[:, D_MLA:] = (dc * (u4_ref[...] + b_ref[...]) * _dsilu(cg)).astype(dz_ref.dtype)
        gb_ref[...] += _colsum(du4)

    return pl.pallas_call(
        body, name=name, grid=(s // t,),
        in_specs=[_rowspec(t, D_MLA, 0), _rowspec(t, D_CONV, 1), _rowspec(t, D_MLA),
                  _rowspec(t, D_MLA, SEG_MG[0] // D_MLA), _rowspec(t, D_CONV),
                  _rowspec(t, D_CONV, SEG_CG[0] // D_CONV), _vecspec(D_CONV)],
        out_specs=[_rowspec(t, D_MLA), pl.BlockSpec((N_HEADS, t, 1), lambda i: (0, i, 0)),
                   _rowspec(t, D_CONV), _vecspec(D_CONV), _rowspec(t, gates, SEG_MG[0] // gates)],
        out_shape=[_sds((s, D_MLA), MXU_DTYPE), _sds((N_HEADS, s, 1), F32),
                   _sds((s, D_CONV), MXU_DTYPE), _sds((1, D_CONV), F32), _sds((s, IN_PAD), MXU_DTYPE)],
        compiler_params=_cp(("arbitrary",)),
    )(dcat, dcat, o, z, u4m, z, b_pw)


def _conv_bwd(du3, u1, z, dz, glu_b, dw_w, ln_g, ln_b, *, name):
    s = z.shape[0]
    t = min(CONV_T, s)
    c2 = 2 * D_CONV
    hb = t // HALO
    n_blk = s // t
    last_halo = s // HALO - 1

    def body(d3m_ref, d3h_ref, u1m_ref, u1h_ref, zm_ref, zh_ref, gb_ref, w_ref, g_ref, b_ref, dz_in_ref,
             dci_ref, gg_ref, gbn_ref, gwb_ref, ggb_ref, gw_ref, dext, uext, du0_s, gw_acc):
        i = pl.program_id(0)
        _acc_init([gg_ref, gbn_ref, gwb_ref, ggb_ref, gw_acc])

        def ln_bwd(d3, u1v):
            mu = jnp.mean(u1v, axis=-1, keepdims=True)
            cen = u1v - mu
            rstd = lax.rsqrt(jnp.mean(cen * cen, axis=-1, keepdims=True) + EPS)
            uh = cen * rstd
            d2 = d3 * _dsilu(uh * g_ref[...] + b_ref[...])
            dh = d2 * g_ref[...]
            d1 = rstd * (dh - jnp.mean(dh, axis=-1, keepdims=True) - uh * jnp.mean(dh * uh, axis=-1, keepdims=True))
            return d1, d2, uh

        d1, d2, uh = ln_bwd(d3m_ref[...], u1m_ref[...])
        gg_ref[...] += _colsum(d2 * uh)
        gbn_ref[...] += _colsum(d2)
        gwb_ref[...] += _colsum(d1)
        dext[0, 0:t, :] = d1
        d1h, _, _ = ln_bwd(d3h_ref[...], u1h_ref[...])
        dext[0, t:, :] = jnp.where(i < n_blk - 1, d1h, 0.0)
        _shifted_copies(dext)

        def glu_parts(zv):
            ci = zv + gb_ref[...]
            return ci[:, :D_CONV], jax.nn.sigmoid(ci[:, D_CONV:])

        val, sg = glu_parts(zm_ref[...])
        uext[0, HALO:, :] = val * sg
        valh, sgh = glu_parts(zh_ref[...])
        uext[0, 0:HALO, :] = jnp.where(i > 0, valh * sgh, 0.0)
        _shifted_copies(uext)

        for rc in range(0, t, CONV_RC):
            for lc in range(0, D_CONV, CONV_LC):
                acc = None
                for off, win in _windows(dext, [rc + k for k in range(CONV_K)], CONV_RC, lc, CONV_LC):
                    k = (CONV_K - 1) - (off - rc)
                    term = w_ref[k:k + 1, lc:lc + CONV_LC] * win
                    acc = term if acc is None else acc + term
                du0_s[rc:rc + CONV_RC, lc:lc + CONV_LC] = acc
                dchunk = dext[0, rc:rc + CONV_RC, lc:lc + CONV_LC]
                first = rc + HALO - (CONV_K - 1)
                for off, win in _windows(uext, [first + k for k in range(CONV_K)], CONV_RC, lc, CONV_LC):
                    k = off - first
                    pr = dchunk * win
                    part = pr[0:8]
                    for r8 in range(8, CONV_RC, 8):
                        part = part + pr[r8:r8 + 8]
                    gw_acc[k, :, lc:lc + CONV_LC] += part

        du0 = du0_s[...]
        dval = du0 * sg
        dgt = du0 * val * sg * (1.0 - sg)
        dci_ref[:, :D_CONV] = dval.astype(dci_ref.dtype)
        dci_ref[:, D_CONV:] = dgt.astype(dci_ref.dtype)
        ggb_ref[:, :D_CONV] += _colsum(dval)
        ggb_ref[:, D_CONV:] += _colsum(dgt)

        @pl.when(i == n_blk - 1)
        def _():
            gw_ref[...] = jnp.sum(gw_acc[...], axis=1)

    halo_next = lambda w: pl.BlockSpec((HALO, w), lambda i: (jnp.minimum((i + 1) * hb, last_halo), 0))
    return pl.pallas_call(
        body, name=name, grid=(n_blk,),
        in_specs=[_rowspec(t, D_CONV), halo_next(D_CONV), _rowspec(t, D_CONV), halo_next(D_CONV),
                  _rowspec(t, c2), pl.BlockSpec((HALO, c2), lambda i: (jnp.maximum(i * hb - 1, 0), 0)),
                  _vecspec(c2), pl.BlockSpec((HALO, D_CONV), lambda i: (0, 0)), _vecspec(D_CONV), _vecspec(D_CONV),
                  _ANY],
        out_specs=[_rowspec(t, c2, SEG_CI[0] // c2), _vecspec(D_CONV), _vecspec(D_CONV), _vecspec(D_CONV),
                   _vecspec(c2), pl.BlockSpec((HALO, D_CONV), lambda i: (0, 0))],
        out_shape=[_sds(dz.shape, dz.dtype), _sds((1, D_CONV), F32), _sds((1, D_CONV), F32), _sds((1, D_CONV), F32),
                   _sds((1, c2), F32), _sds((HALO, D_CONV), F32)],
        scratch_shapes=[pltpu.VMEM((8, t + HALO, D_CONV), F32), pltpu.VMEM((8, t + HALO, D_CONV), F32),
                        pltpu.VMEM((t, D_CONV), F32), pltpu.VMEM((HALO, 8, D_CONV), F32)],
        input_output_aliases={10: 0},
        compiler_params=_cp(("arbitrary",)),
    )(du3, du3, u1, u1, z, z, glu_b, dw_w, ln_g, ln_b, dz)


def _flash_bwd(qf, kf, va, do, lse_t, delta_t, *, name):
    nh, s, dk = qf.shape
    dv = va.shape[-1] // 2
    t = min(ATT_T, s)
    n = s // t
    nt = (((1,), (1,)), ((), ()))
    tn = (((0,), (0,)), ((), ()))

    def body(q_ref, do_ref, lse_ref, dl_ref, k_ref, v_ref, dq_ref, dk_ref, dv_ref,
             dk_s, dv_s, st_buf, dpt_buf):
        n_un = pl.program_id(1)
        j = n - 1 - n_un
        nxt = jnp.maximum(j - 1, 0)

        @pl.when(n_un == 0)
        def _():
            dq_ref[...] = jnp.zeros(dq_ref.shape, F32)

        dk_s[...] = jnp.zeros(dk_s.shape, F32)
        dv_s[...] = jnp.zeros(dv_s.shape, F32)

        def rows_at(blk):
            return pl.ds(pl.multiple_of(blk * t, t), t)

        def rows_of(b):
            return rows_at(n - 1 - b)

        k = k_ref[0, rows_at(j), :]

        def produce(kj, b, slot):
            rows = rows_of(b)
            st_buf[slot] = lax.dot_general(k_ref[0, rows_at(kj), :], q_ref[0, rows, :], nt,
                                           preferred_element_type=F32)
            dpt_buf[slot] = lax.dot_general(v_ref[0, rows_at(kj), 0:dv], do_ref[rows, :], nt,
                                            preferred_element_type=F32)

        def consume(b, slot, masked):
            i = n - 1 - b
            rows = rows_of(b)
            q, dov = q_ref[0, rows, :], do_ref[rows, :]
            pt = jnp.exp(st_buf[slot] - lse_ref[0, i])
            if masked:
                key = lax.broadcasted_iota(jnp.int32, (t, t), 0)
                qry = lax.broadcasted_iota(jnp.int32, (t, t), 1)
                pt = jnp.where(key <= qry, pt, 0.0)
            dv_s[...] += jnp.dot(pt.astype(MXU_DTYPE), dov, preferred_element_type=F32)
            dst = (pt * (dpt_buf[slot] - dl_ref[0, i])).astype(MXU_DTYPE)
            dk_s[...] += jnp.dot(dst, q, preferred_element_type=F32)
            dq_ref[0, rows, :] += lax.dot_general(dst, k, tn, preferred_element_type=F32)

        @pl.when(n_un == 0)
        def _():
            produce(j, 0, 2)
            consume(0, 2, True)
            produce(nxt, 0, 2)

        @pl.when(n_un > 0)
        def _():
            produce(j, 1, 1)
            consume(0, 2, False)

            def pair(a, carry):
                produce(j, 2 * a + 2, 0)
                consume(2 * a + 1, 1, False)
                produce(j, 2 * a + 3, 1)
                consume(2 * a + 2, 0, False)
                return carry

            lax.fori_loop(0, (n_un - 1) // 2, pair, 0)

            @pl.when(n_un % 2 == 1)
            def _():
                produce(nxt, 0, 2)
                consume(n_un, 1, True)

            @pl.when(n_un % 2 == 0)
            def _():
                produce(j, n_un, 0)
                consume(n_un - 1, 1, False)
                produce(nxt, 0, 2)
                consume(n_un, 0, True)

        dk_ref[0] = dk_s[...]
        dv_ref[0] = dv_s[...]

    head = lambda h, j: (h, 0, 0)
    rowv = pl.BlockSpec((1, n, 1, t), lambda h, j: (h, 0, 0, 0))
    return pl.pallas_call(
        body, name=name, grid=(nh, n),
        in_specs=[pl.BlockSpec((1, s, dk), head),
                  pl.BlockSpec((s, dv), lambda h, j: (0, h)),
                  rowv, rowv,
                  pl.BlockSpec((1, s, dk), head),
                  pl.BlockSpec((1, s, 2 * dv), head)],
        out_specs=[pl.BlockSpec((1, s, dk), head),
                   pl.BlockSpec((1, t, dk), lambda h, g: (h, n - 1 - g, 0)),
                   pl.BlockSpec((1, t, dv), lambda h, g: (h, n - 1 - g, 0))],
        out_shape=[_sds((nh, s, dk), F32), _sds((nh, s, dk), F32), _sds((nh, s, dv), F32)],
        scratch_shapes=[pltpu.VMEM((t, dk), F32), pltpu.VMEM((t, dv), F32),
                        pltpu.VMEM((3, t, t), F32), pltpu.VMEM((3, t, t), F32)],
        compiler_params=_cp(("arbitrary", "arbitrary")),
    )(qf, do, lse_t, delta_t, kf, va)


def _qk_bwd(dqf, dkf, dvf, q_raw, kv, z, c_t, s1_t, s2_t, gqn, gqr, gkn, gkr, *, name):
    s = q_raw.shape[0]
    t = min(ROW_T, s)
    scale = 1.0 / math.sqrt(QK_DIM)

    def body(dq_ref, dk_ref, dv_ref, q_ref, kv_ref, kr_ref, c_ref, s1_ref, s2_ref,
             gqn_ref, gqr_ref, gkn_ref, gkr_ref, dqr_ref, dkv_ref, dkr_ref, ggq_ref, ggk_ref):
        _acc_init([ggq_ref, ggk_ref])
        c_v, s1_v, s2_v = c_ref[...], s1_ref[...], s2_ref[...]
        kr = kr_ref[...]
        kr_ss = _lanesum(kr * kr)
        dkr = jnp.zeros(kr.shape, F32)
        ggq_n = ggq_r = ggk_n = ggk_r = jnp.zeros((1, LANE), F32)

        def norm_bwd(n, r, rs, dyn, dyr, gn, gr):
            nh_, rh_ = n * rs, r * rs
            dnh, drh = dyn * gn, dyr * gr
            dot = (_lanesum(dnh * nh_) + _lanesum(drh * rh_)) * (1.0 / QK_DIM)
            return rs * (dnh - nh_ * dot), rs * (drh - rh_ * dot), _colsum(dyn * nh_), _colsum(dyr * rh_)

        for h in range(N_HEADS):
            n = q_ref[:, h * LANE:(h + 1) * LANE]
            r = q_ref[:, N_HEADS * LANE + h * LANE:N_HEADS * LANE + (h + 1) * LANE]
            rs = lax.rsqrt((_lanesum(n * n) + _lanesum(r * r)) * (1.0 / QK_DIM) + EPS)
            dyn = dq_ref[h, :, 0:LANE] * scale
            dyr = _rope_bwd(dq_ref[h, :, LANE:HEAD_PAD] * scale, c_v, s1_v, s2_v)
            dn, dr, g_n, g_r = norm_bwd(n, r, rs, dyn, dyr, gqn_ref[...], gqr_ref[...])
            dqr_ref[:, h * LANE:(h + 1) * LANE] = dn.astype(dqr_ref.dtype)
            dqr_ref[:, N_HEADS * LANE + h * LANE:N_HEADS * LANE + (h + 1) * LANE] = dr.astype(dqr_ref.dtype)
            ggq_n, ggq_r = ggq_n + g_n, ggq_r + g_r

            n = kv_ref[:, h * 2 * LANE:h * 2 * LANE + LANE]
            rs = lax.rsqrt((_lanesum(n * n) + kr_ss) * (1.0 / QK_DIM) + EPS)
            dyn = dk_ref[h, :, 0:LANE]
            dyr = _rope_bwd(dk_ref[h, :, LANE:HEAD_PAD], c_v, s1_v, s2_v)
            dn, dr, g_n, g_r = norm_bwd(n, kr, rs, dyn, dyr, gkn_ref[...], gkr_ref[...])
            dkv_ref[:, h * 2 * LANE:h * 2 * LANE + LANE] = dn.astype(dkv_ref.dtype)
            dkv_ref[:, h * 2 * LANE + LANE:(h + 1) * 2 * LANE] = dv_ref[h].astype(dkv_ref.dtype)
            dkr = dkr + dr
            ggk_n, ggk_r = ggk_n + g_n, ggk_r + g_r

        dkr_ref[...] = dkr.astype(dkr_ref.dtype)
        ggq_ref[:, 0:LANE] += ggq_n
        ggq_ref[:, LANE:] += ggq_r
        ggk_ref[:, 0:LANE] += ggk_n
        ggk_ref[:, LANE:] += ggk_r

    hspec = lambda w: pl.BlockSpec((N_HEADS, t, w), lambda i: (0, i, 0))
    wide = 2 * N_HEADS * LANE
    return pl.pallas_call(
        body, name=name, grid=(s // t,),
        in_specs=[hspec(HEAD_PAD), hspec(HEAD_PAD), hspec(V_DIM), _rowspec(t, wide), _rowspec(t, wide),
                  _rowspec(t, LANE, SEG_KR[0] // LANE), _rowspec(t, LANE), _rowspec(t, LANE), _rowspec(t, LANE),
                  _vecspec(LANE), _vecspec(LANE), _vecspec(LANE), _vecspec(LANE)],
        out_specs=[_rowspec(t, wide), _rowspec(t, wide), _rowspec(t, LANE), _vecspec(2 * LANE), _vecspec(2 * LANE)],
        out_shape=[_sds((s, wide), MXU_DTYPE), _sds((s, wide), MXU_DTYPE), _sds((s, LANE), MXU_DTYPE),
                   _sds((1, 2 * LANE), F32), _sds((1, 2 * LANE), F32)],
        compiler_params=_cp(("arbitrary",)),
    )(dqf, dkf, dvf, q_raw, kv, z, c_t, s1_t, s2_t, gqn, gqr, gkn, gkr)


def _lat_bwd(dqn, dkn, dkr, z, dz, g_ql, g_kvl, *, name):
    s = z.shape[0]
    t = min(ROW_T, s)
    o_ql, o_kvl, o_kr = (seg[0] - SEG_LAT[0] for seg in (SEG_QL, SEG_KVL, SEG_KR))

    def body(dq_ref, dk_ref, dkr_ref, ql_ref, kvl_ref, gq_ref, gk_ref, dz_in_ref, dz_ref, ggq_ref, ggk_ref):
        _acc_init([ggq_ref, ggk_ref])
        for d_ref, src, g_ref, off, gg_ref in ((dq_ref, ql_ref, gq_ref, o_ql, ggq_ref),
                                               (dk_ref, kvl_ref, gk_ref, o_kvl, ggk_ref)):
            v, dy = src[...], d_ref[...]
            r = lax.rsqrt(jnp.mean(v * v, axis=-1, keepdims=True) + EPS)
            vh = v * r
            dvh = dy * g_ref[...]
            dz_ref[:, off:off + v.shape[1]] = (
                r * (dvh - vh * jnp.mean(dvh * vh, axis=-1, keepdims=True))).astype(dz_ref.dtype)
            gg_ref[...] += _colsum(dy * vh)
        dz_ref[:, o_kr:o_kr + LANE] = dkr_ref[...]
        dz_ref[:, o_kr + LANE:] = jnp.zeros((t, SEG_LAT[1] - o_kr - LANE), dz_ref.dtype)

    return pl.pallas_call(
        body, name=name, grid=(s // t,),
        in_specs=[_rowspec(t, Q_LORA), _rowspec(t, KV_LORA), _rowspec(t, LANE),
                  _rowspec(t, Q_LORA, SEG_QL[0] // Q_LORA), _rowspec(t, KV_LORA, SEG_KVL[0] // KV_LORA),
                  _vecspec(Q_LORA), _vecspec(KV_LORA), _ANY],
        out_specs=[_rowspec(t, SEG_LAT[1], SEG_LAT[0] // SEG_LAT[1]), _vecspec(Q_LORA), _vecspec(KV_LORA)],
        out_shape=[_sds(dz.shape, dz.dtype), _sds((1, Q_LORA), F32), _sds((1, KV_LORA), F32)],
        input_output_aliases={7: 0},
        compiler_params=_cp(("arbitrary",)),
    )(dqn, dkn, dkr, z, z, g_ql, g_kvl, dz)


def _prenorm_bwd(dh, x, gxo, g, sc1p, *, name):
    s, d = x.shape
    t = min(ROW_T, s)

    def body(dh_ref, x_ref, gx_ref, g_ref, sc_ref, dx_ref, dsh_ref, dsc_ref, gg_ref):
        _acc_init([dsh_ref, dsc_ref, gg_ref])
        xv, dhv = x_ref[...], dh_ref[...]
        r = lax.rsqrt(jnp.mean(xv * xv, axis=-1, keepdims=True) + EPS)
        xn = xv * r
        dsh_ref[...] += _colsum(dhv)
        dsc_ref[...] += _colsum(dhv * (xn * g_ref[...]))
        dm = dhv * sc_ref[...]
        gg_ref[...] += _colsum(dm * xn)
        dxn = dm * g_ref[...]
        dx_ref[...] = gx_ref[...] + r * (dxn - xn * jnp.mean(dxn * xn, axis=-1, keepdims=True))

    return pl.pallas_call(
        body, name=name, grid=(s // t,),
        in_specs=[_rowspec(t, d), _rowspec(t, d), _rowspec(t, d), _vecspec(d), _vecspec(d)],
        out_specs=[_rowspec(t, d), _vecspec(d), _vecspec(d), _vecspec(d)],
        out_shape=[_sds((s, d), F32), _sds((1, d), F32), _sds((1, d), F32), _sds((1, d), F32)],
        compiler_params=_cp(("arbitrary",)),
    )(dh, x, gxo, g, sc1p)


def _ada_fwd(c_all, ada_w, ada_b_cols, *, name):
    nl, d, cols = ada_w.shape

    def body(c_ref, w_ref, b_ref, o_ref):
        ca = _silu(c_ref[...]).astype(MXU_DTYPE)
        o_ref[0] = jnp.dot(ca, w_ref[0].astype(MXU_DTYPE), preferred_element_type=F32) + b_ref[0]

    return pl.pallas_call(
        body, name=name, grid=(nl,),
        in_specs=[pl.BlockSpec((N_DEV, d), lambda l: (0, 0)), pl.BlockSpec((1, d, cols), lambda l: (l, 0, 0)),
                  pl.BlockSpec((1, 1, cols), lambda l: (l, 0, 0))],
        out_specs=pl.BlockSpec((1, N_DEV, cols), lambda l: (l, 0, 0)),
        out_shape=_sds((nl, N_DEV, cols), F32),
        compiler_params=_cp(("parallel",)),
    )(c_all, ada_w, ada_b_cols)


def _ada_bwd(c_all_t, dmod_cols, *, name):
    nl, _, cols = dmod_cols.shape
    d = c_all_t.shape[0]

    def body(c_ref, dm_ref, o_ref):
        ca = _silu(c_ref[...]).astype(MXU_DTYPE)
        o_ref[0] = jnp.dot(ca, dm_ref[0].astype(MXU_DTYPE), preferred_element_type=F32)

    return pl.pallas_call(
        body, name=name, grid=(nl,),
        in_specs=[pl.BlockSpec((d, N_DEV), lambda l: (0, 0)), pl.BlockSpec((1, N_DEV, cols), lambda l: (l, 0, 0))],
        out_specs=pl.BlockSpec((1, d, cols), lambda l: (l, 0, 0)),
        out_shape=_sds((nl, d, cols), F32),
        compiler_params=_cp(("parallel",)),
    )(c_all_t, dmod_cols)


def _adamw(gparts, w, m, v, *, name):
    shape = w.shape
    cols = shape[-1]
    per_layer = isinstance(gparts, (list, tuple))
    nl = shape[0] if per_layer else 1
    rows = w.size // cols // nl
    glist = list(gparts) if per_layer else [gparts]
    npart = glist[0].shape[0]
    glist = [g.reshape(npart, rows, cols) for g in glist]
    w3, m3, v3 = (a.reshape(nl, rows, cols) for a in (w, m, v))
    budget = 2 * 1024 * 1024
    fits = [t for t in range(min(rows, 256) // 8 * 8, 7, -8)
            if rows % t == 0 and npart * t * cols * glist[0].dtype.itemsize <= budget]
    t = fits[0] if fits else rows
    nb = rows // t

    def body(*refs):
        g_refs = refs[:nl]
        w_ref, m_ref, v_ref, go_ref, d_ref, mo_ref, vo_ref, g_s = refs[nl:]
        layer = pl.program_id(0)
        for l in range(nl):
            @pl.when(layer == l)
            def _(l=l):
                g = g_refs[l][0].astype(F32)
                for p in range(1, npart):
                    g = g + g_refs[l][p].astype(F32)
                g_s[...] = g

        g = g_s[...]
        mn = ADAM_B1 * m_ref[0] + (1.0 - ADAM_B1) * g
        vn = ADAM_B2 * v_ref[0] + (1.0 - ADAM_B2) * (g * g)
        m_hat = mn / (1.0 - ADAM_B1 ** ADAM_STEP)
        v_hat = vn / (1.0 - ADAM_B2 ** ADAM_STEP)
        go_ref[0] = g
        d_ref[0] = -ADAM_LR * (m_hat / (jnp.sqrt(v_hat) + ADAM_EPS) + ADAM_WD * w_ref[0])
        mo_ref[0] = mn
        vo_ref[0] = vn

    def g_map(l):
        return lambda layer, i: (0, jnp.where(layer == l, i, jnp.where(layer < l, 0, nb - 1)), 0)

    spec = pl.BlockSpec((1, t, cols), lambda layer, i: (layer, i, 0))
    outs = pl.pallas_call(
        body, name=name, grid=(nl, nb),
        in_specs=[pl.BlockSpec((npart, t, cols), g_map(l)) for l in range(nl)] + [spec, spec, spec],
        out_specs=[spec] * 4, out_shape=[_sds((nl, rows, cols), F32)] * 4,
        scratch_shapes=[pltpu.VMEM((t, cols), F32)],
        compiler_params=_cp(("arbitrary", "arbitrary")),
    )(*glist, w3, m3, v3)
    return tuple(o.reshape(shape) for o in outs)


_ANY = pl.BlockSpec(memory_space=pl.ANY)


def _all_gather(blocks, *, name):
    na = len(blocks)

    def body(*refs):
        x_refs, out_refs = refs[:na], refs[na:2 * na]
        send_sems, recv_sems, local_sems = refs[2 * na:]
        x, y, c = lax.axis_index("x"), lax.axis_index("y"), lax.axis_index("c")
        me, sibling = (x, y, c), (x, y, 1 - c)
        chips = [(1 - x, y), (x, 1 - y), (1 - x, 1 - y)]

        def slot(a, px, py, pc):
            return out_refs[a].at[4 * px + 2 * py + pc]

        def copy(a, k, blk, to, src=None):
            return pltpu.make_async_remote_copy(
                src_ref=slot(a, *blk) if src is None else src, dst_ref=slot(a, *blk),
                send_sem=send_sems.at[7 * a + k], recv_sem=recv_sems.at[7 * a + k],
                device_id=to, device_id_type=MESH_ID)

        mine = [pltpu.make_async_copy(x_refs[a], slot(a, *me), local_sems.at[a]) for a in range(na)]
        for cp in mine:
            cp.start()
        first = []
        for a in range(na):
            first.append(copy(a, 0, me, sibling, src=x_refs[a]))
            first += [copy(a, 1 + j, me, (*chip, c), src=x_refs[a]) for j, chip in enumerate(chips)]
        for cp in first:
            cp.start()
        passed = []
        for a in range(na):
            for j, chip in enumerate(chips):
                copy(a, 1 + j, (*chip, c), me).wait_recv()
                fwd = copy(a, 4 + j, (*chip, c), sibling)
                fwd.start()
                passed.append(fwd)
        for a in range(na):
            copy(a, 0, sibling, me).wait_recv()
            for j, chip in enumerate(chips):
                copy(a, 4 + j, (*chip, 1 - c), me).wait_recv()
        for cp in first + passed:
            cp.wait_send()
        for cp in mine:
            cp.wait()

    outs = pl.pallas_call(
        body, name=name, in_specs=[_ANY] * na, out_specs=[_ANY] * na,
        out_shape=[_sds((N_DEV,) + b.shape, b.dtype) for b in blocks],
        scratch_shapes=[pltpu.SemaphoreType.DMA((7 * na,)), pltpu.SemaphoreType.DMA((7 * na,)),
                        pltpu.SemaphoreType.DMA((na,))],
    )(*blocks)
    return list(outs)


_HBM = pl.BlockSpec(memory_space=pltpu.HBM)
_SEM = pl.BlockSpec(memory_space=pltpu.SEMAPHORE)
_EFFECT = pltpu.SideEffectType.DATAFLOW_SIDE_EFFECTING


def _peers(x, y, c):
    out = []
    for k in range(1, N_DEV):
        out.append((1 - x if k & 4 else x, 1 - y if k & 2 else y, 1 - c if k & 1 else c))
    return out


def _own_slots(srcs, scatter, *, name, after=None):
    na = len(srcs)
    n_extra = 0 if after is None else 1
    me = (4 * lax.axis_index("x") + 2 * lax.axis_index("y") + lax.axis_index("c")).astype(jnp.int32).reshape(1)

    def body(me_ref, *refs):
        in_refs, out_refs = refs[:na], refs[na + n_extra:]
        for a in range(na):
            out_refs[a][0] = in_refs[a][0] if scatter else in_refs[a][...]

    def slot_spec(shard):
        zeros = (0,) * len(shard)
        return pl.BlockSpec((1,) + tuple(shard), lambda i, me_ref: (me_ref[0],) + zeros)

    def whole_spec(shape):
        zeros = (0,) * len(shape)
        return pl.BlockSpec(tuple(shape), lambda i, me_ref: zeros)

    shards = [s.shape[1:] if scatter else s.shape for s in srcs]
    in_specs = [slot_spec(sh) if scatter else whole_spec(sh) for sh in shards] + [_ANY] * n_extra
    outs = pl.pallas_call(
        body, name=name,
        grid_spec=pltpu.PrefetchScalarGridSpec(
            num_scalar_prefetch=1, grid=(1,), in_specs=in_specs, out_specs=[slot_spec(sh) for sh in shards]),
        out_shape=[_sds((N_DEV,) + tuple(sh), s.dtype) for sh, s in zip(shards, srcs)],
        compiler_params=_cp(("arbitrary",)),
    )(me, *srcs, *([] if after is None else [after]))
    return list(outs)


_N_COPIES = dict(scatter=7, gather=7, chips=4, forward=3)


def _exchange_copies(src_refs, land_refs, send_sems, recv_sems, mode):
    x, y, c = lax.axis_index("x"), lax.axis_index("y"), lax.axis_index("c")
    me = 4 * x + 2 * y + c
    nc = _N_COPIES[mode]
    chips = [(1 - x, y), (x, 1 - y), (1 - x, 1 - y)]
    cps = []
    for a in range(len(land_refs)):
        if mode in ("scatter", "gather"):
            plan = [((src_refs[a].at[4 * px + 2 * py + pc] if mode == "scatter" else src_refs[a]),
                     land_refs[a].at[me], (px, py, pc)) for px, py, pc in _peers(x, y, c)]
        elif mode == "chips":
            plan = [(src_refs[a], land_refs[a].at[me], to) for to in [(x, y, 1 - c)] + [(*ch, c) for ch in chips]]
        else:
            plan = [(land_refs[a].at[4 * px + 2 * py + c], land_refs[a].at[4 * px + 2 * py + c], (x, y, 1 - c))
                    for px, py in chips]
        for k, (src, dst, to) in enumerate(plan):
            cps.append(pltpu.make_async_remote_copy(
                src_ref=src, dst_ref=dst, send_sem=send_sems.at[nc * a + k], recv_sem=recv_sems.at[nc * a + k],
                device_id=to, device_id_type=MESH_ID))
    return cps


def _exchange_start(srcs, lands, mode, *, name):
    ns, nz = len(srcs), len(lands)
    nsem = _N_COPIES[mode] * nz

    def body(*refs):
        src_refs, land_refs = refs[:ns], refs[ns:ns + nz]
        send_sems, recv_sems = refs[ns + nz], refs[ns + nz + 1]
        token = refs[-1]
        for cp in _exchange_copies(src_refs, land_refs, send_sems, recv_sems, mode):
            cp.start()
        token[...] = jnp.zeros(token.shape, token.dtype)

    hbm = lambda a: pltpu.HBM(a.shape, a.dtype)
    outs = pl.pallas_call(
        body, name=name,
        out_shape=(pltpu.SemaphoreType.DMA((nsem,)), pltpu.SemaphoreType.DMA((nsem,)),
                   *[hbm(a) for a in srcs], *[hbm(a) for a in lands], _sds((8, LANE), F32)),
        in_specs=[_HBM] * (ns + nz),
        out_specs=(_SEM, _SEM, *[_HBM] * (ns + nz), pl.BlockSpec(memory_space=pltpu.VMEM)),
        input_output_aliases={i: 2 + i for i in range(ns + nz)},
        compiler_params=pltpu.CompilerParams(has_side_effects=_EFFECT),
    )(*[pltpu.with_memory_space_constraint(a, pltpu.HBM) for a in list(srcs) + list(lands)])
    return outs[0], outs[1], list(outs[2:2 + ns]), list(outs[2 + ns:2 + ns + nz]), outs[-1]


def _exchange_wait(send_sems, recv_sems, srcs, lands, after, mode, *, name):
    ns, nz = len(srcs), len(lands)

    def body(*refs):
        src_refs, land_refs = refs[:ns], refs[ns:ns + nz]
        s_sems, r_sems = refs[ns + nz], refs[ns + nz + 1]
        for cp in _exchange_copies(src_refs, land_refs, s_sems, r_sems, mode):
            cp.wait_send()
            cp.wait_recv()

    hbm = lambda a: pltpu.HBM(a.shape, a.dtype)
    outs = pl.pallas_call(
        body, name=name,
        out_shape=(*[hbm(a) for a in srcs], *[hbm(a) for a in lands]),
        in_specs=[_HBM] * (ns + nz) + [_SEM, _SEM, _ANY],
        out_specs=tuple([_HBM] * (ns + nz)),
        input_output_aliases={i: i for i in range(ns + nz)},
        compiler_params=pltpu.CompilerParams(has_side_effects=_EFFECT),
    )(*srcs, *lands, send_sems, recv_sems, after)
    return list(outs[ns:])


_WIN_SEGS = (("ql", 0, Q_LORA, SEG_QL[0]), ("kvl", Q_LORA, KV_LORA, SEG_KVL[0]),
             ("kr", Q_LORA + KV_LORA, ROPE, SEG_KR[0]), ("mg", Q_LORA + KV_LORA + ROPE, D_MLA, SEG_MG[0]),
             ("ci", Q_LORA + KV_LORA + ROPE + D_MLA, 2 * D_CONV, SEG_CI[0]),
             ("cg", Q_LORA + KV_LORA + ROPE + D_MLA + 2 * D_CONV, D_CONV, SEG_CG[0]))
_WIN_SHARD = IN_COLS // N_DEV


def _win_pieces():
    out = []
    for _, o, n, new in _WIN_SEGS:
        for j in range(N_DEV):
            lo, hi = max(o, j * _WIN_SHARD), min(o + n, (j + 1) * _WIN_SHARD)
            if lo < hi:
                out.append((j, lo - j * _WIN_SHARD, new + lo - o, hi - lo))
    return out


WIN_T = 512


def _win_assemble(w_all, *, name):
    d = w_all.shape[2]
    t = min(WIN_T, d)
    pieces = sorted(_win_pieces(), key=lambda p: p[2])
    assert all(lo % 8 == 0 and n % 8 == 0 for _, lo, _, n in pieces)

    def body(w_ref, o_ref):
        rows = [w_ref[j].astype(F32)[lo:lo + n, :] for j, lo, _, n in pieces]
        rows.append(jnp.zeros((IN_PAD - (SEG_KR[0] + ROPE), t), F32))
        o_ref[...] = jnp.concatenate(rows, axis=0).astype(o_ref.dtype)

    return pl.pallas_call(
        body, name=name, grid=(d // t,),
        in_specs=[pl.BlockSpec((N_DEV, _WIN_SHARD, t), lambda i: (0, 0, i))],
        out_specs=pl.BlockSpec((IN_PAD, t), lambda i: (0, i)), out_shape=_sds((IN_PAD, d), w_all.dtype),
        compiler_params=_cp(("parallel",)),
    )(w_all)


def _win_split(grad, *, name):
    d = grad.shape[1]
    t = min(WIN_T, d)
    by_shard = [sorted([p for p in _win_pieces() if p[0] == j], key=lambda p: p[1]) for j in range(N_DEV)]

    def body(g_ref, o_ref):
        for j in range(N_DEV):
            rows = [g_ref[new:new + n, :] for _, _, new, n in by_shard[j]]
            o_ref[j] = jnp.concatenate(rows, axis=0).astype(o_ref.dtype)

    return pl.pallas_call(
        body, name=name, grid=(d // t,),
        in_specs=[pl.BlockSpec((IN_PAD, t), lambda i: (0, i))],
        out_specs=pl.BlockSpec((N_DEV, _WIN_SHARD, t), lambda i: (0, 0, i)),
        out_shape=_sds((N_DEV, _WIN_SHARD, d), WIRE_DTYPE),
        compiler_params=_cp(("parallel",)),
    )(grad)


def _cols_to_shards(a):
    r, n = a.shape
    return a.reshape(r, N_DEV, n // N_DEV).transpose(1, 0, 2)


def _shards_to_cols(a):
    nd, r, w = a.shape
    return a.transpose(1, 0, 2).reshape(r, nd * w)


def _win_permute(w_in):
    o_ql, o_kvl, o_kr, o_mg = 0, Q_LORA, Q_LORA + KV_LORA, Q_LORA + KV_LORA + ROPE
    o_ci = o_mg + D_MLA
    o_cg = o_ci + 2 * D_CONV
    seg = lambda o, n: w_in[:, o:o + n]
    pad = jnp.zeros((w_in.shape[0], IN_PAD - (SEG_KR[0] + ROPE)), w_in.dtype)
    return jnp.concatenate([seg(o_ci, 2 * D_CONV), seg(o_mg, D_MLA), seg(o_cg, D_CONV), seg(o_ql, Q_LORA),
                            seg(o_kvl, KV_LORA), seg(o_kr, ROPE), pad], axis=1)


def _win_unpermute(g):
    seg = lambda s, n=None: g[:, s[0]:s[0] + (s[1] if n is None else n)]
    return jnp.concatenate([seg(SEG_QL), seg(SEG_KVL), seg(SEG_KR, ROPE), seg(SEG_MG), seg(SEG_CI), seg(SEG_CG)], axis=1)


def _qup_permute(w):
    w3 = w.reshape(w.shape[0], N_HEADS, QK_DIM)
    nope = w3[:, :, :NOPE].reshape(w.shape[0], N_HEADS * NOPE)
    rope = jnp.pad(w3[:, :, NOPE:], ((0, 0), (0, 0), (0, LANE - ROPE))).reshape(w.shape[0], N_HEADS * LANE)
    return jnp.concatenate([nope, rope], axis=1)


def _qup_unpermute(g):
    r = g.shape[0]
    nope = g[:, :N_HEADS * NOPE].reshape(r, N_HEADS, NOPE)
    rope = g[:, N_HEADS * NOPE:].reshape(r, N_HEADS, LANE)[:, :, :ROPE]
    return jnp.concatenate([nope, rope], axis=2).reshape(r, N_HEADS * QK_DIM)


def _norm_tiles(g):
    return g[:NOPE].reshape(1, LANE), jnp.pad(g[NOPE:], (0, LANE - ROPE)).reshape(1, LANE)


def _norm_untile(gt):
    return jnp.concatenate([gt[0, :NOPE], gt[0, LANE:LANE + ROPE]])


def _rope_tiles(positions):
    inv_freq = 1.0 / (ROPE_THETA ** (jnp.arange(0, ROPE, 2, dtype=F32) / ROPE))
    ang = positions.astype(F32)[:, None] * inv_freq
    cos, sin = jnp.cos(ang), jnp.sin(ang)
    zq = jnp.zeros_like(cos)
    c_t = jnp.concatenate([cos, cos, zq, zq], axis=1)
    s1_t = jnp.concatenate([-sin, zq, zq, zq], axis=1)
    s2_t = jnp.concatenate([zq, sin, zq, zq], axis=1)
    return c_t, s1_t, s2_t


_BIG = ("w_in", "w_q_up", "w_kv_up", "w_pw", "w_out")
_COL_SHARDED = ("w_in", "w_q_up", "w_kv_up")


def _pack_rows(arrs):
    return jnp.concatenate([a.reshape(-1, LANE) for a in arrs], axis=0)


def _unpack_rows(buf, shapes):
    out, r0 = [], 0
    lead = buf.shape[:-2]
    for shp in shapes:
        n = math.prod(shp) // LANE
        out.append(buf[..., r0:r0 + n, :].reshape(lead + tuple(shp)))
        r0 += n
    return out


_SMALL = (("dmod", 3 * D_MODEL), ("norm_g", D_MODEL), ("q_lat_g", Q_LORA), ("kv_lat_g", KV_LORA),
          ("q_norm_g", 2 * LANE), ("k_norm_g", 2 * LANE), ("glu_b", 2 * D_CONV), ("dw_w", HALO * D_CONV),
          ("dw_b", D_CONV), ("conv_ln_g", D_CONV), ("conv_ln_b", D_CONV), ("b_pw", D_CONV))


def _layer_fwd(x, p, rope, l, late=None):
    n = lambda s: f"{s}_l{l}"
    c_t, s1_t, s2_t = rope
    h = _prenorm(x, p["norm_g"], p["shift"], p["sc1p"], name=n("prenorm"))
    z = _mm(h, p["w_in"], tb=True, name=n("in_proj"), tn=IN_TILE, n_outer=True)
    if late is not None:
        p = {**p, **late(z)}
    qn, kn = _lat_norm(z, p["q_lat_g"], p["kv_lat_g"], name=n("lat_norm"))
    q_raw = _mm(qn, p["w_q_up"], name=n("q_up"), tn=1024)
    kv = _mm(kn, p["w_kv_up"], name=n("kv_up"), tn=1024)
    qf, kf, vf = _qk_prep(q_raw, kv, z, c_t, s1_t, s2_t, *p["qk_tiles"], name=n("qk_prep"))
    o, lse = _flash_fwd(qf, kf, vf, name=n("flash_fwd"))
    u1, u3 = _conv_fwd(z, p["glu_b"], p["dw_w"], p["dw_b"], p["conv_ln_g"], p["conv_ln_b"], name=n("conv_fwd"))
    u4m = _mm(u3, p["w_pw"], name=n("pw"), tn=1024)
    cat = _gate_cat(o, z, u4m, p["b_pw"], name=n("gate_cat"))
    y, x_next = _mm(cat, p["w_out"], name=n("out_proj"), tn=1024, residual=(x, p["gate"]))
    saved = dict(x=x, h=h, z=z, qn=qn, kn=kn, q_raw=q_raw, kv=kv, qf=qf, kf=kf, vf=vf, o=o, lse=lse,
                 u1=u1, u3=u3, u4m=u4m, cat=cat, y=y)
    return x_next, saved, p


def _layer_bwd(gxo, p, sv, rope, l, hook_rest=None, hook_w_in=None):
    n = lambda s: f"{s}_l{l}"
    c_t, s1_t, s2_t = rope
    z = sv["z"]
    dy, dgate = _out_bwd(gxo, sv["y"], p["gate"], name=n("out_bwd"))
    g_w_out = _mm(sv["cat"], dy, ta=True, name=n("g_w_out"), tm=1024, tn=1024)
    dcat = _mm(dy, p["w_out"], tb=True, name=n("d_cat"), tn=1024)
    do, delta, du4, g_b_pw, dz = _gate_bwd(dcat, sv["o"], z, sv["u4m"], p["b_pw"], name=n("gate_bwd"))
    g_w_pw = _mm(sv["u3"], du4, ta=True, name=n("g_w_pw"), tm=1024, tn=1024, tk=512)
    du3 = _mm(du4, p["w_pw"], tb=True, name=n("d_u3"), tn=1024)
    dz, g_ln_g, g_ln_b, g_dw_b, g_glu_b, g_dw_w = _conv_bwd(
        du3, sv["u1"], z, dz, p["glu_b"], p["dw_w"], p["conv_ln_g"], p["conv_ln_b"], name=n("conv_bwd"))
    t_att = min(ATT_T, z.shape[0])
    to_lanes = lambda a: a.reshape(N_HEADS, z.shape[0] // t_att, 1, t_att)
    dqf, dkf, dvf = _flash_bwd(sv["qf"], sv["kf"], sv["vf"], do,
                               to_lanes(sv["lse"][:, :, 0]), to_lanes(delta), name=n("flash_bwd"))
    dq_raw, dkv, dkr, g_qn, g_kn = _qk_bwd(dqf, dkf, dvf, sv["q_raw"], sv["kv"], z, c_t, s1_t, s2_t,
                                            *p["qk_tiles"], name=n("qk_bwd"))
    g_w_q_up = _mm(sv["qn"], dq_raw, ta=True, name=n("g_w_q_up"), tm=512, tn=1024, tk=512)
    dqn = _mm(dq_raw, p["w_q_up"], tb=True, name=n("d_qn"))
    g_w_kv_up = _mm(sv["kn"], dkv, ta=True, name=n("g_w_kv_up"), tm=256, tn=1024, tk=512)
    dkn = _mm(dkv, p["w_kv_up"], tb=True, name=n("d_kn"))
    dz, g_ql, g_kvl = _lat_bwd(dqn, dkn, dkr, z, dz, p["q_lat_g"], p["kv_lat_g"], name=n("lat_bwd"))
    big = dict(w_q_up=g_w_q_up, w_kv_up=g_w_kv_up, w_pw=g_w_pw, w_out=g_w_out)
    after = None if hook_rest is None else hook_rest(big)
    g_w_in = _mm(dz, sv["h"], ta=True, name=n("g_w_in"), tm=512, tn=1024, after=after)
    big["w_in"] = g_w_in
    after = None if hook_w_in is None else hook_w_in(g_w_in)
    dh = _mm(dz, p["w_in"], name=n("d_h"), tn=1024, after=after)
    dx, dshift, dscale, g_norm = _prenorm_bwd(dh, sv["x"], gxo, p["norm_g"], p["sc1p"], name=n("prenorm_bwd"))
    small = dict(dmod=jnp.concatenate([dshift, dscale, dgate], axis=1), norm_g=g_norm, q_lat_g=g_ql, kv_lat_g=g_kvl,
                 q_norm_g=g_qn, k_norm_g=g_kn, glu_b=g_glu_b, dw_w=g_dw_w, dw_b=g_dw_b,
                 conv_ln_g=g_ln_g, conv_ln_b=g_ln_b, b_pw=g_b_pw)
    return dx, big, small


def _layer_params(l, full, mod_l, small):
    d = D_MODEL
    row = lambda a: a.reshape(1, -1)
    shift, scale, gate = mod_l[:, :d], mod_l[:, d:2 * d], mod_l[:, 2 * d:]
    dw_w = jnp.pad(full["dw_w"][l], ((0, HALO - CONV_K), (0, 0)))
    return dict(
        shift=shift, sc1p=1.0 + scale, gate=gate, norm_g=row(small["norm_g"][l]),
        **{k: full[k][l] for k in _BIG if k in full}, dw_w=dw_w,
        q_lat_g=row(small["q_lat_g"][l]), kv_lat_g=row(small["kv_lat_g"][l]),
        qk_tiles=_norm_tiles(small["q_norm_g"][l]) + _norm_tiles(small["k_norm_g"][l]),
        glu_b=row(small["glu_b"][l]), dw_b=row(small["dw_b"][l]), conv_ln_g=row(small["conv_ln_g"][l]),
        conv_ln_b=row(small["conv_ln_b"][l]), b_pw=row(small["b_pw"][l]))


def kernel(x, c, positions, ada_w, ada_b, norm_g, w_in, q_lat_g, w_q_up, kv_lat_g, w_kv_up, q_norm_g, k_norm_g, glu_b, dw_w, dw_b, conv_ln_g, conv_ln_b, w_pw, b_pw, w_out, loss_target, m_ada_w, m_ada_b, m_norm_g, m_w_in, m_q_lat_g, m_w_q_up, m_kv_lat_g, m_w_kv_up, m_q_norm_g, m_k_norm_g, m_glu_b, m_dw_w, m_dw_b, m_conv_ln_g, m_conv_ln_b, m_w_pw, m_b_pw, m_w_out, v_ada_w, v_ada_b, v_norm_g, v_w_in, v_q_lat_g, v_w_q_up, v_kv_lat_g, v_w_kv_up, v_q_norm_g, v_k_norm_g, v_glu_b, v_dw_w, v_dw_b, v_conv_ln_g, v_conv_ln_b, v_w_pw, v_b_pw, v_w_out):
    names = ("ada_w", "ada_b", "norm_g", "w_in", "q_lat_g", "w_q_up", "kv_lat_g", "w_kv_up", "q_norm_g",
             "k_norm_g", "glu_b", "dw_w", "dw_b", "conv_ln_g", "conv_ln_b", "w_pw", "b_pw", "w_out")
    w_loc = dict(zip(names, (ada_w, ada_b, norm_g, w_in, q_lat_g, w_q_up, kv_lat_g, w_kv_up, q_norm_g, k_norm_g,
                             glu_b, dw_w, dw_b, conv_ln_g, conv_ln_b, w_pw, b_pw, w_out)))
    m_loc = dict(zip(names, (m_ada_w, m_ada_b, m_norm_g, m_w_in, m_q_lat_g, m_w_q_up, m_kv_lat_g, m_w_kv_up,
                             m_q_norm_g, m_k_norm_g, m_glu_b, m_dw_w, m_dw_b, m_conv_ln_g, m_conv_ln_b, m_w_pw,
                             m_b_pw, m_w_out)))
    v_loc = dict(zip(names, (v_ada_w, v_ada_b, v_norm_g, v_w_in, v_q_lat_g, v_w_q_up, v_kv_lat_g, v_w_kv_up,
                             v_q_norm_g, v_k_norm_g, v_glu_b, v_dw_w, v_dw_b, v_conv_ln_g, v_conv_ln_b, v_w_pw,
                             v_b_pw, v_w_out)))
    nl, d = N_LAYERS, D_MODEL
    me = 4 * lax.axis_index("x") + 2 * lax.axis_index("y") + lax.axis_index("c")
    x2, tgt = x[0], loss_target[0]
    ada_cols = ada_w.shape[-1]

    tr = lambda a: jnp.swapaxes(a, 1, 2)
    w_loc, m_loc, v_loc = ({**dd, "w_in": tr(dd["w_in"])} for dd in (w_loc, m_loc, v_loc))
    wire = {k: w_loc[k].astype(WIRE_DTYPE) for k in _BIG}
    w_in0 = [wire["w_in"][0]]
    fly_c = _exchange_start(w_in0, _own_slots(w_in0, False, name="own_w_in_l0"), "chips", name="gather_start_w_in_l0")

    dw_pad = jnp.pad(dw_w, ((0, 0), (0, HALO - CONV_K), (0, 0)))
    c_rows = c.reshape(d // LANE, LANE) + fly_c[4][0:1, :]
    c_all, dw_all = _all_gather([c_rows, dw_pad], name="gather_c")
    c_all = c_all.reshape(N_DEV, d)
    ada_b_cols = lax.dynamic_slice_in_dim(ada_b, me * ada_cols, ada_cols, axis=1).reshape(nl, 1, ada_cols)
    mod_cols = _ada_fwd(c_all, ada_w, ada_b_cols, name="ada_fwd")
    mod_all = _all_gather([mod_cols], name="gather_mod")[0]
    mod_me = lax.dynamic_index_in_dim(mod_all, me, axis=2, keepdims=False)
    mod = mod_me.transpose(1, 0, 2).reshape(nl, 1, N_DEV * ada_cols)

    from_chips = _exchange_wait(*fly_c[:4], mod, "chips", name="gather_wait_w_in_l0")
    fly_f = _exchange_start([], from_chips, "forward", name="forward_start_w_in_l0")
    w_in_all0 = _exchange_wait(*fly_f[:4], fly_f[4], "forward", name="forward_wait_w_in_l0")[0]
    rest0 = [wire[k][0] for k in _BIG[1:]]
    fly_r0 = _exchange_start(rest0, _own_slots(rest0, False, name="own_weights_l0_rest", after=w_in_all0), "gather",
                             name="gather_start_l0_rest")
    fly_w1 = {}

    def layout_rest(parts):
        return dict(w_q_up=_qup_permute(_shards_to_cols(parts[0])), w_kv_up=_shards_to_cols(parts[1]),
                    w_pw=parts[2].reshape(D_CONV, D_CONV), w_out=parts[3].reshape(D_MLA + D_CONV, d))

    small_in = dict(norm_g=norm_g, q_lat_g=q_lat_g, kv_lat_g=kv_lat_g, q_norm_g=q_norm_g, k_norm_g=k_norm_g,
                    glu_b=glu_b, dw_b=dw_b, conv_ln_g=conv_ln_g, conv_ln_b=conv_ln_b, b_pw=b_pw)
    dw_full = [_shards_to_cols(dw_all[:, l])[:CONV_K] for l in range(nl)]
    rope = _rope_tiles(positions[0])

    def layer_params(l, w_in_all, rest, mod_l):
        full = dict(w_in={l: _win_assemble(w_in_all, name=f"w_in_assemble_l{l}")}, dw_w=dw_full)
        if rest is not None:
            full.update({k: {l: a} for k, a in layout_rest(rest).items()})
        return _layer_params(l, full, mod_l, small_in)

    def late_l0(z):
        parts = _exchange_wait(*fly_r0[:4], z, "gather", name="gather_wait_l0_rest")
        src1 = [wire[k][1] for k in _BIG]
        fly_w1["x"] = _exchange_start(src1, _own_slots(src1, False, name="own_weights_l1", after=parts[0]), "gather",
                                      name="gather_start_l1")
        late = layout_rest(parts)
        late["q_lat_g"] = small_in["q_lat_g"][0].reshape(1, -1) + fly_w1["x"][4][0, 0]
        return late

    params, saved = [None] * nl, [None] * nl
    p0 = layer_params(0, w_in_all0, None, mod[0] + fly_r0[4][0, 0])
    xs, saved[0], params[0] = _layer_fwd(x2, p0, rope, 0, late=late_l0)
    parts1 = _exchange_wait(*fly_w1["x"][:4], xs, "gather", name="gather_wait_l1")
    params[1] = layer_params(1, parts1[0], parts1[1:], mod[1])
    xs, saved[1], _ = _layer_fwd(xs, params[1], rope, 1)
    gx, loss_part = _loss_head(xs, tgt, name="loss_head")
    loss = lax.psum(loss_part[0, 0], ("x", "y", "c"))

    def shard_major(k, g):
        if k == "w_q_up":
            g = _qup_unpermute(g)
        if k in _COL_SHARDED:
            return _cols_to_shards(g)
        return g.reshape((N_DEV, g.shape[0] // N_DEV, g.shape[1]))

    def scatter_start(send, tag):
        lands = _own_slots(send, True, name=f"own_grads_{tag}")
        return _exchange_start(send, lands, "scatter", name=f"scatter_start_{tag}")

    def wire_rest(big):
        return [shard_major(k, big[k]).astype(WIRE_DTYPE) for k in _BIG[1:]]

    big_g, small_g, flying = [None] * nl, [None] * nl, {}
    gx, big_g[1], small_g[1] = _layer_bwd(gx, params[1], saved[1], rope, 1)
    flying["l1"] = scatter_start([_win_split(big_g[1]["w_in"], name="w_in_split_l1")] + wire_rest(big_g[1]), "l1")
    p0 = dict(params[0])
    p0["gate"] = p0["gate"] + flying["l1"][4][0, 0]

    def start_rest_l0(big):
        flying["l0_rest"] = scatter_start(wire_rest(big), "l0_rest")
        return flying["l0_rest"][4]

    def start_w_in_l0(g_w_in):
        flying["l0_w_in"] = scatter_start([_win_split(g_w_in, name="w_in_split_l0")], "l0_w_in")
        return flying["l0_w_in"][4]

    gx, big_g[0], small_g[0] = _layer_bwd(gx, p0, saved[0], rope, 0, hook_rest=start_rest_l0,
                                          hook_w_in=start_w_in_l0)

    tile = 8 * LANE
    padded = [(k, nn, -(-nn // tile) * tile) for k, nn in _SMALL]
    spk = jnp.concatenate([jnp.pad(small_g[l][k].reshape(-1), (0, np_ - nn)).reshape(-1, LANE)
                           for l in range(nl) for k, nn, np_ in padded], axis=0)
    s_all = _all_gather([spk], name="gather_small_grads")[0]
    s_rows = sum(np_ for _, _, np_ in padded) // LANE
    s_all = s_all.reshape(N_DEV, nl, s_rows, LANE)
    s_parts = {k: a[..., :nn] for (k, nn, _), a in
               zip(padded, _unpack_rows(s_all, [(np_,) for _, _, np_ in padded]))}

    dmod_all = s_parts["dmod"]
    dmod_cols = lax.dynamic_slice_in_dim(dmod_all, me * ada_cols, ada_cols, axis=2).transpose(1, 0, 2)
    g_ada_w = _ada_bwd(c_all.T, dmod_cols, name="ada_bwd")
    gp = {}
    gp["ada_w"] = g_ada_w[None]
    gp["ada_b"] = dmod_all
    for k in ("norm_g", "q_lat_g", "kv_lat_g", "glu_b", "dw_b", "conv_ln_g", "conv_ln_b", "b_pw"):
        gp[k] = s_parts[k]
    for k in ("q_norm_g", "k_norm_g"):
        t = s_parts[k]
        gp[k] = jnp.concatenate([t[..., :NOPE], t[..., LANE:LANE + ROPE]], axis=-1)
    dw_g = s_parts["dw_w"].reshape(N_DEV, nl, HALO, D_CONV)[:, :, :CONV_K]
    gp["dw_w"] = lax.dynamic_slice_in_dim(dw_g, me * LANE, LANE, axis=3)

    res = {k: _adamw(gp[k], w_loc[k], m_loc[k], v_loc[k], name=f"adamw_{k}") for k in names if k not in _BIG}
    arrived = [None] * nl
    arrived[1] = _exchange_wait(*flying["l1"][:4], gx, "scatter", name="scatter_wait_l1")
    rest0 = _exchange_wait(*flying["l0_rest"][:4], gx, "scatter", name="scatter_wait_l0_rest")
    arrived[0] = _exchange_wait(*flying["l0_w_in"][:4], res["ada_w"][1], "scatter",
                                name="scatter_wait_l0_w_in") + rest0
    for i, k in enumerate(_BIG):
        res[k] = _adamw([arrived[l][i] for l in range(nl)], w_loc[k], m_loc[k], v_loc[k], name=f"adamw_{k}")
    res["w_in"] = tuple(tr(a) for a in res["w_in"])
    out = [loss, gx[None]]
    for idx in range(4):
        out += [res[k][idx] for k in names]
    return tuple(out)
```

```python
import functools
import math

import jax
import jax.numpy as jnp
from jax import lax
from jax.experimental import pallas as pl
from jax.experimental.pallas import tpu as pltpu

F32 = jnp.float32
MXU_DTYPE = jnp.bfloat16
WIRE_DTYPE = jnp.bfloat16

D_MODEL = 2048
N_LAYERS = 2
N_DEV = 8
N_HEADS = 8
NOPE = 128
ROPE = 64
V_DIM = 128
QK_DIM = NOPE + ROPE
Q_LORA = 512
KV_LORA = 256
D_MLA = N_HEADS * V_DIM
D_CONV = 1024
CONV_K = 31
ROPE_THETA = 10000.0
EPS = 1e-6
LANE = 128
HEAD_PAD = 2 * LANE
HALO = 32

SEG_CI = (0, 2 * D_CONV)
SEG_MG = (2 * D_CONV, D_MLA)
SEG_CG = (2 * D_CONV + D_MLA, D_CONV)
SEG_QL = (2 * D_CONV + D_MLA + D_CONV, Q_LORA)
SEG_KVL = (SEG_QL[0] + Q_LORA, KV_LORA)
SEG_KR = (SEG_KVL[0] + KV_LORA, LANE)
SEG_LAT = (SEG_QL[0], 1024)
IN_PAD = SEG_LAT[0] + SEG_LAT[1]
IN_TILE = IN_PAD // 4
assert SEG_KR[0] + LANE <= IN_PAD and SEG_LAT[0] % SEG_LAT[1] == 0
IN_COLS = Q_LORA + KV_LORA + ROPE + D_MLA + 2 * D_CONV + D_CONV

ADAM_LR = 0.001
ADAM_B1 = 0.9
ADAM_B2 = 0.999
ADAM_EPS = 1e-08
ADAM_WD = 0.01
ADAM_STEP = 10

VMEM_LIMIT = 56 * 1024 * 1024
ATT_T = 512
ROW_T = 256
CONV_T = 128
MESH_ID = pl.DeviceIdType.MESH


def _cp(sem=None):
    kw = dict(vmem_limit_bytes=VMEM_LIMIT)
    if sem is not None:
        kw["dimension_semantics"] = sem
    return pltpu.CompilerParams(**kw)


def _sds(shape, dtype):
    return jax.ShapeDtypeStruct(shape, dtype)


def _silu(x):
    return x * jax.nn.sigmoid(x)


def _dsilu(x):
    s = jax.nn.sigmoid(x)
    return s * (1.0 + x * (1.0 - s))


def _rowspec(t, width, col=0):
    return pl.BlockSpec((t, width), lambda i: (i, col))


def _vecspec(width):
    return pl.BlockSpec((1, width), lambda i: (0, 0))


def _colsum(v):
    return jnp.sum(v, axis=0, keepdims=True)


def _mm(a, b, *, name, ta=False, tb=False, out_dtype=F32, tm=512, tn=512, tk=None, n_outer=False, after=None,
        residual=None):
    if ta:
        kdim, m = a.shape
    else:
        m, kdim = a.shape
    if tb:
        n, k2 = b.shape
    else:
        k2, n = b.shape
    assert kdim == k2, (a.shape, b.shape)
    tm, tn = min(tm, m), min(tn, n)
    tk = kdim if tk is None else min(tk, kdim)
    assert m % tm == 0 and n % tn == 0 and kdim % tk == 0, (m, n, kdim, tm, tn, tk)
    nk = kdim // tk
    dims = (((0 if ta else 1,), (1 if tb else 0,)), ((), ()))

    n_extra = 0 if after is None else 1
    assert residual is None or nk == 1

    def body(a_ref, b_ref, *rest):
        if residual is not None:
            x_ref, gate_ref = rest[:2]
            rest = rest[2:]
        o_ref, scratch = rest[n_extra], rest[n_extra + 1:]
        prod = lax.dot_general(a_ref[...].astype(MXU_DTYPE), b_ref[...].astype(MXU_DTYPE), dims,
                               preferred_element_type=F32)
        if residual is not None:
            o_ref[...] = prod.astype(o_ref.dtype)
            scratch[0][...] = x_ref[...] + gate_ref[...] * prod
        elif nk == 1:
            o_ref[...] = prod.astype(o_ref.dtype)
        else:
            acc = scratch[0]
            k = pl.program_id(2)

            @pl.when(k == 0)
            def _():
                acc[...] = prod

            @pl.when(k > 0)
            def _():
                acc[...] += prod

            @pl.when(k == nk - 1)
            def _():
                o_ref[...] = acc[...].astype(o_ref.dtype)

    if n_outer:
        ij = lambda g0, g1: (g1, g0)
        grid = (n // tn, m // tm, nk)
    else:
        ij = lambda g0, g1: (g0, g1)
        grid = (m // tm, n // tn, nk)

    def a_map(g0, g1, k):
        i, _ = ij(g0, g1)
        return (k, i) if ta else (i, k)

    def b_map(g0, g1, k):
        _, j = ij(g0, g1)
        return (j, k) if tb else (k, j)

    def o_map(g0, g1, k):
        return ij(g0, g1)

    in_specs = [pl.BlockSpec((tk, tm) if ta else (tm, tk), a_map), pl.BlockSpec((tn, tk) if tb else (tk, tn), b_map)]
    operands = [a, b]
    out_specs, out_shape = pl.BlockSpec((tm, tn), o_map), _sds((m, n), out_dtype)
    if residual is not None:
        in_specs += [pl.BlockSpec((tm, tn), o_map), pl.BlockSpec((1, tn), lambda g0, g1, k: (0, ij(g0, g1)[1]))]
        operands += list(residual)
        out_specs, out_shape = [out_specs, pl.BlockSpec((tm, tn), o_map)], [out_shape, _sds((m, n), F32)]
    if after is not None:
        in_specs.append(_ANY)
        operands.append(after)
    return pl.pallas_call(
        body, name=name, grid=grid, in_specs=in_specs, out_specs=out_specs, out_shape=out_shape,
        scratch_shapes=[pltpu.VMEM((tm, tn), F32)] if nk > 1 else [],
        compiler_params=_cp(("parallel", "parallel", "arbitrary")),
    )(*operands)


def _prenorm(x, g, shift, sc1p, *, name):
    s, d = x.shape
    t = min(ROW_T, s)

    def body(x_ref, g_ref, sh_ref, sc_ref, h_ref):
        xv = x_ref[...]
        r = lax.rsqrt(jnp.mean(xv * xv, axis=-1, keepdims=True) + EPS)
        h_ref[...] = ((xv * r) * g_ref[...] * sc_ref[...] + sh_ref[...]).astype(h_ref.dtype)

    return pl.pallas_call(
        body, name=name, grid=(s // t,),
        in_specs=[_rowspec(t, d), _vecspec(d), _vecspec(d), _vecspec(d)],
        out_specs=_rowspec(t, d), out_shape=_sds((s, d), MXU_DTYPE),
        compiler_params=_cp(("parallel",)),
    )(x, g, shift, sc1p)


def _lat_norm(z, g_ql, g_kvl, *, name):
    s = z.shape[0]
    t = min(ROW_T, s)

    def body(ql_ref, kvl_ref, gq_ref, gk_ref, qn_ref, kn_ref):
        for src, g_ref, dst in ((ql_ref, gq_ref, qn_ref), (kvl_ref, gk_ref, kn_ref)):
            v = src[...]
            r = lax.rsqrt(jnp.mean(v * v, axis=-1, keepdims=True) + EPS)
            dst[...] = ((v * r) * g_ref[...]).astype(dst.dtype)

    return pl.pallas_call(
        body, name=name, grid=(s // t,),
        in_specs=[_rowspec(t, Q_LORA, SEG_QL[0] // Q_LORA), _rowspec(t, KV_LORA, SEG_KVL[0] // KV_LORA),
                  _vecspec(Q_LORA), _vecspec(KV_LORA)],
        out_specs=[_rowspec(t, Q_LORA), _rowspec(t, KV_LORA)],
        out_shape=[_sds((s, Q_LORA), MXU_DTYPE), _sds((s, KV_LORA), MXU_DTYPE)],
        compiler_params=_cp(("parallel",)),
    )(z, z, g_ql, g_kvl)


def _rope_fwd(r, c_t, s1_t, s2_t):
    return r * c_t + pltpu.roll(r, LANE - ROPE // 2, 1) * s1_t + pltpu.roll(r, ROPE // 2, 1) * s2_t


def _rope_bwd(d, c_t, s1_t, s2_t):
    return d * c_t + pltpu.roll(d * s1_t, ROPE // 2, 1) + pltpu.roll(d * s2_t, LANE - ROPE // 2, 1)


def _lanesum(v):
    return jnp.sum(v, axis=-1, keepdims=True)


def _qk_prep(q_raw, kv, z, c_t, s1_t, s2_t, gqn, gqr, gkn, gkr, *, name):
    s = q_raw.shape[0]
    t = min(ROW_T, s)
    scale = 1.0 / math.sqrt(QK_DIM)

    def body(q_ref, kv_ref, kr_ref, c_ref, s1_ref, s2_ref, gqn_ref, gqr_ref, gkn_ref, gkr_ref,
             qf_ref, kf_ref, vf_ref):
        c_v, s1_v, s2_v = c_ref[...], s1_ref[...], s2_ref[...]
        kr = kr_ref[...]
        kr_ss = _lanesum(kr * kr)
        for h in range(N_HEADS):
            n = q_ref[:, h * LANE:(h + 1) * LANE]
            r = q_ref[:, N_HEADS * LANE + h * LANE:N_HEADS * LANE + (h + 1) * LANE]
            rs = lax.rsqrt((_lanesum(n * n) + _lanesum(r * r)) * (1.0 / QK_DIM) + EPS)
            qf_ref[h, :, 0:LANE] = (((n * rs) * gqn_ref[...]) * scale).astype(qf_ref.dtype)
            rr = _rope_fwd((r * rs) * gqr_ref[...], c_v, s1_v, s2_v)
            qf_ref[h, :, LANE:HEAD_PAD] = (rr * scale).astype(qf_ref.dtype)

            n = kv_ref[:, h * 2 * LANE:h * 2 * LANE + LANE]
            rs = lax.rsqrt((_lanesum(n * n) + kr_ss) * (1.0 / QK_DIM) + EPS)
            kf_ref[h, :, 0:LANE] = ((n * rs) * gkn_ref[...]).astype(kf_ref.dtype)
            kf_ref[h, :, LANE:HEAD_PAD] = _rope_fwd((kr * rs) * gkr_ref[...], c_v, s1_v, s2_v).astype(kf_ref.dtype)
            vf_ref[h, :, 0:V_DIM] = kv_ref[:, h * 2 * LANE + LANE:(h + 1) * 2 * LANE].astype(vf_ref.dtype)
            vf_ref[h, :, V_DIM:] = jnp.ones((t, V_DIM), vf_ref.dtype)

    hspec = lambda w: pl.BlockSpec((N_HEADS, t, w), lambda i: (0, i, 0))
    return pl.pallas_call(
        body, name=name, grid=(s // t,),
        in_specs=[_rowspec(t, 2 * N_HEADS * LANE), _rowspec(t, 2 * N_HEADS * LANE),
                  _rowspec(t, LANE, SEG_KR[0] // LANE),
                  _rowspec(t, LANE), _rowspec(t, LANE), _rowspec(t, LANE),
                  _vecspec(LANE), _vecspec(LANE), _vecspec(LANE), _vecspec(LANE)],
        out_specs=[hspec(HEAD_PAD), hspec(HEAD_PAD), hspec(2 * V_DIM)],
        out_shape=[_sds((N_HEADS, s, HEAD_PAD), MXU_DTYPE), _sds((N_HEADS, s, HEAD_PAD), MXU_DTYPE),
                   _sds((N_HEADS, s, 2 * V_DIM), MXU_DTYPE)],
        compiler_params=_cp(("parallel",)),
    )(q_raw, kv, z, c_t, s1_t, s2_t, gqn, gqr, gkn, gkr)


def _causal_mask(t):
    row = lax.broadcasted_iota(jnp.int32, (t, t), 0)
    col = lax.broadcasted_iota(jnp.int32, (t, t), 1)
    return col <= row


NEG = -1e30


def _flash_fwd(qf, kf, va, *, name):
    nh, s, dk = qf.shape
    dv = va.shape[-1] // 2
    t = min(ATT_T, s)
    n = s // t
    assert dv == LANE and t % LANE == 0

    def body(q_ref, k_ref, v_ref, o_ref, lse_ref, m_s, acc_s, s_buf):
        i = pl.program_id(1)
        m_s[...] = jnp.full(m_s.shape, NEG, F32)
        acc_s[...] = jnp.zeros(acc_s.shape, F32)

        def rows_of(j):
            return pl.ds(pl.multiple_of(j * t, t), t)

        def scores(qi, j):
            return lax.dot_general(q_ref[0, rows_of(qi), :], k_ref[0, rows_of(j), :], (((1,), (1,)), ((), ())),
                                   preferred_element_type=F32)

        def consume(j, slot, masked):
            sc = s_buf[slot]
            if masked:
                sc = jnp.where(_causal_mask(t), sc, NEG)
            m_prev = m_s[...]
            m_new = jnp.maximum(m_prev, jnp.max(sc, axis=-1, keepdims=True))
            alpha = jnp.exp(m_prev - m_new)
            p = jnp.exp(sc - jnp.tile(m_new, (1, t // LANE)))
            acc_s[...] = jnp.tile(alpha, (1, 2)) * acc_s[...] + jnp.dot(
                p.astype(MXU_DTYPE), v_ref[0, rows_of(j), :], preferred_element_type=F32)
            m_s[...] = m_new

        nxt = jnp.minimum(i + 1, n - 1)

        @pl.when(i == 0)
        def _():
            s_buf[2] = scores(0, 0)
            consume(0, 2, True)
            s_buf[2] = scores(nxt, 0)

        @pl.when(i > 0)
        def _():
            s_buf[1] = scores(i, 1)
            consume(0, 2, False)

            def pair(a, carry):
                s_buf[0] = scores(i, 2 * a + 2)
                consume(2 * a + 1, 1, False)
                s_buf[1] = scores(i, 2 * a + 3)
                consume(2 * a + 2, 0, False)
                return carry

            lax.fori_loop(0, (i - 1) // 2, pair, 0)

            @pl.when(i % 2 == 1)
            def _():
                s_buf[2] = scores(nxt, 0)
                consume(i, 1, True)

            @pl.when(i % 2 == 0)
            def _():
                s_buf[0] = scores(i, i)
                consume(i - 1, 1, False)
                s_buf[2] = scores(nxt, 0)
                consume(i, 0, True)

        den = acc_s[:, dv:]
        o_ref[...] = acc_s[:, :dv] / den
        lse_ref[0] = m_s[...] + jnp.log(den)

    head = lambda h, i: (h, 0, 0)
    return pl.pallas_call(
        body, name=name, grid=(nh, n),
        in_specs=[pl.BlockSpec((1, s, dk), head), pl.BlockSpec((1, s, dk), head), pl.BlockSpec((1, s, 2 * dv), head)],
        out_specs=[pl.BlockSpec((t, dv), lambda h, i: (i, h)),
                   pl.BlockSpec((1, t, LANE), lambda h, i: (h, i, 0))],
        out_shape=[_sds((s, nh * dv), F32), _sds((nh, s, LANE), F32)],
        scratch_shapes=[pltpu.VMEM((t, LANE), F32), pltpu.VMEM((t, 2 * dv), F32), pltpu.VMEM((3, t, t), F32)],
        compiler_params=_cp(("arbitrary", "arbitrary")),
    )(qf, kf, va)


def _shifted_copies(ext_ref):
    rows = ext_ref.shape[1] - 8
    for s in range(1, 8):
        ext_ref[s, 0:rows, :] = ext_ref[0, s:s + rows, :]


def _windows(ext_ref, offsets, t_rows, lane0, lanes):
    for s in range(8):
        group = [o for o in offsets if o % 8 == s]
        if not group:
            continue
        lo, hi = min(group) - s, max(group) - s
        wide = ext_ref[s, pl.ds(lo, hi - lo + t_rows), lane0:lane0 + lanes]
        for o in group:
            yield o, wide[o - s - lo:o - s - lo + t_rows]


def _dw_taps(ext_ref, w_ref, row0, t_rows, lane0, lanes, first_off):
    acc = None
    for off, win in _windows(ext_ref, [row0 + first_off + k for k in range(CONV_K)], t_rows, lane0, lanes):
        k = off - row0 - first_off
        term = w_ref[k:k + 1, lane0:lane0 + lanes] * win
        acc = term if acc is None else acc + term
    return acc


CONV_RC = 32
CONV_LC = 256


def _conv_fwd(z, glu_b, dw_w, dw_b, ln_g, ln_b, *, name):
    s = z.shape[0]
    t = min(CONV_T, s)
    c2 = 2 * D_CONV
    hb = t // HALO

    def body(zm_ref, zh_ref, gb_ref, w_ref, wb_ref, g_ref, b_ref, u1_ref, u3_ref, ext):
        i = pl.program_id(0)

        def glu(zv):
            ci = zv + gb_ref[...]
            return ci[:, :D_CONV] * jax.nn.sigmoid(ci[:, D_CONV:])

        ext[0, HALO:, :] = glu(zm_ref[...])
        ext[0, 0:HALO, :] = jnp.where(i > 0, glu(zh_ref[...]), 0.0)
        _shifted_copies(ext)
        for rc in range(0, t, CONV_RC):
            for lc in range(0, D_CONV, CONV_LC):
                acc = _dw_taps(ext, w_ref, rc, CONV_RC, lc, CONV_LC, HALO - (CONV_K - 1))
                u1_ref[rc:rc + CONV_RC, lc:lc + CONV_LC] = acc + wb_ref[:, lc:lc + CONV_LC]
        u1 = u1_ref[...]
        mu = jnp.mean(u1, axis=-1, keepdims=True)
        cen = u1 - mu
        var = jnp.mean(cen * cen, axis=-1, keepdims=True)
        u2 = (cen * lax.rsqrt(var + EPS)) * g_ref[...] + b_ref[...]
        u3_ref[...] = _silu(u2).astype(u3_ref.dtype)

    return pl.pallas_call(
        body, name=name, grid=(s // t,),
        in_specs=[_rowspec(t, c2), pl.BlockSpec((HALO, c2), lambda i: (jnp.maximum(i * hb - 1, 0), 0)),
                  _vecspec(c2), pl.BlockSpec((HALO, D_CONV), lambda i: (0, 0)), _vecspec(D_CONV),
                  _vecspec(D_CONV), _vecspec(D_CONV)],
        out_specs=[_rowspec(t, D_CONV), _rowspec(t, D_CONV)],
        out_shape=[_sds((s, D_CONV), F32), _sds((s, D_CONV), MXU_DTYPE)],
        scratch_shapes=[pltpu.VMEM((8, t + HALO, D_CONV), F32)],
        compiler_params=_cp(("parallel",)),
    )(z, z, glu_b, dw_w, dw_b, ln_g, ln_b)


def _gate_cat(o, z, u4m, b_pw, *, name):
    s = o.shape[0]
    t = min(ROW_T, s)

    def body(o_ref, mg_ref, u4_ref, cg_ref, b_ref, cat_ref):
        cat_ref[:, :D_MLA] = (o_ref[...] * _silu(mg_ref[...])).astype(cat_ref.dtype)
        cat_ref[:, D_MLA:] = ((u4_ref[...] + b_ref[...]) * _silu(cg_ref[...])).astype(cat_ref.dtype)

    return pl.pallas_call(
        body, name=name, grid=(s // t,),
        in_specs=[_rowspec(t, D_MLA), _rowspec(t, D_MLA, SEG_MG[0] // D_MLA), _rowspec(t, D_CONV),
                  _rowspec(t, D_CONV, SEG_CG[0] // D_CONV), _vecspec(D_CONV)],
        out_specs=_rowspec(t, D_MLA + D_CONV), out_shape=_sds((s, D_MLA + D_CONV), MXU_DTYPE),
        compiler_params=_cp(("parallel",)),
    )(o, z, u4m, z, b_pw)


def _loss_head(xf, target, *, name):
    s, d = xf.shape
    t = min(ROW_T, s)

    def body(x_ref, t_ref, gx_ref, loss_ref):
        @pl.when(pl.program_id(0) == 0)
        def _():
            loss_ref[...] = jnp.zeros(loss_ref.shape, F32)

        err = x_ref[...] - t_ref[...]
        gx_ref[...] = err * (1.0 / d)
        loss_ref[...] += 0.5 * jnp.sum(_lanesum(err * err) * (1.0 / d), axis=0, keepdims=True)

    return pl.pallas_call(
        body, name=name, grid=(s // t,),
        in_specs=[_rowspec(t, d), _rowspec(t, d)],
        out_specs=[_rowspec(t, d), pl.BlockSpec((1, 1), lambda i: (0, 0))],
        out_shape=[_sds((s, d), F32), _sds((1, 1), F32)],
        compiler_params=_cp(("arbitrary",)),
    )(xf, target)


def _acc_init(refs):
    @pl.when(pl.program_id(0) == 0)
    def _():
        for r in refs:
            r[...] = jnp.zeros(r.shape, r.dtype)


def _out_bwd(gxo, y, gate, *, name):
    s, d = gxo.shape
    t = min(ROW_T, s)

    def body(g_ref, y_ref, gate_ref, dy_ref, dgate_ref):
        _acc_init([dgate_ref])
        gv = g_ref[...]
        dy_ref[...] = (gv * gate_ref[...]).astype(dy_ref.dtype)
        dgate_ref[...] += _colsum(gv * y_ref[...])

    return pl.pallas_call(
        body, name=name, grid=(s // t,),
        in_specs=[_rowspec(t, d), _rowspec(t, d), _vecspec(d)],
        out_specs=[_rowspec(t, d), _vecspec(d)],
        out_shape=[_sds((s, d), MXU_DTYPE), _sds((1, d), F32)],
        compiler_params=_cp(("arbitrary",)),
    )(gxo, y, gate)


def _gate_bwd(dcat, o, z, u4m, b_pw, *, name):
    s = o.shape[0]
    t = min(ROW_T, s)
    gates = D_MLA + D_CONV
    assert SEG_CG[0] == SEG_MG[0] + D_MLA and SEG_MG[0] % gates == 0

    def body(dm_ref, dc_ref, o_ref, mg_ref, u4_ref, cg_ref, b_ref,
             do_ref, delta_ref, du4_ref, gb_ref, dz_ref):
        _acc_init([gb_ref])
        dm, ov, mg = dm_ref[...], o_ref[...], mg_ref[...]
        do = dm * _silu(mg)
        do_ref[...] = do.astype(do_ref.dtype)
        dz_ref[:, :D_MLA] = (dm * ov * _dsilu(mg)).astype(dz_ref.dtype)
        prod = do * ov
        for h in range(N_HEADS):
            delta_ref[h] = _lanesum(prod[:, h * V_DIM:(h + 1) * V_DIM])
        dc, cg = dc_ref[...], cg_ref[...]
        du4 = dc * _silu(cg)
        du4_ref[...] = du4.astype(du4_ref.dtype)
        dz_ref[:, D_MLA:] = (dc * (u4_ref[...] + b_ref[...]) * _dsilu(cg)).astype(dz_ref.dtype)
        gb_ref[...] += _colsum(du4)

    return pl.pallas_call(
        body, name=name, grid=(s // t,),
        in_specs=[_rowspec(t, D_MLA, 0), _rowspec(t, D_CONV, 1), _rowspec(t, D_MLA),
                  _rowspec(t, D_MLA, SEG_MG[0] // D_MLA), _rowspec(t, D_CONV),
                  _rowspec(t, D_CONV, SEG_CG[0] // D_CONV), _vecspec(D_CONV)],
        out_specs=[_rowspec(t, D_MLA), pl.BlockSpec((N_HEADS, t, 1), lambda i: (0, i, 0)),
                   _rowspec(t, D_CONV), _vecspec(D_CONV), _rowspec(t, gates, SEG_MG[0] // gates)],
        out_shape=[_sds((s, D_MLA), MXU_DTYPE), _sds((N_HEADS, s, 1), F32),
                   _sds((s, D_CONV), MXU_DTYPE), _sds((1, D_CONV), F32), _sds((s, IN_PAD), MXU_DTYPE)],
        compiler_params=_cp(("arbitrary",)),
    )(dcat, dcat, o, z, u4m, z, b_pw)


def _conv_bwd(du3, u1, z, dz, glu_b, dw_w, ln_g, ln_b, *, name):
    s = z.shape[0]
    t = min(CONV_T, s)
    c2 = 2 * D_CONV
    hb = t // HALO
    n_blk = s // t
    last_halo = s // HALO - 1

    def body(d3m_ref, d3h_ref, u1m_ref, u1h_ref, zm_ref, zh_ref, gb_ref, w_ref, g_ref, b_ref, dz_in_ref,
             dci_ref, gg_ref, gbn_ref, gwb_ref, ggb_ref, gw_ref, dext, uext, du0_s, gw_acc):
        i = pl.program_id(0)
        _acc_init([gg_ref, gbn_ref, gwb_ref, ggb_ref, gw_acc])

        def ln_bwd(d3, u1v):
            mu = jnp.mean(u1v, axis=-1, keepdims=True)
            cen = u1v - mu
            rstd = lax.rsqrt(jnp.mean(cen * cen, axis=-1, keepdims=True) + EPS)
            uh = cen * rstd
            d2 = d3 * _dsilu(uh * g_ref[...] + b_ref[...])
            dh = d2 * g_ref[...]
            d1 = rstd * (dh - jnp.mean(dh, axis=-1, keepdims=True) - uh * jnp.mean(dh * uh, axis=-1, keepdims=True))
            return d1, d2, uh

        d1, d2, uh = ln_bwd(d3m_ref[...], u1m_ref[...])
        gg_ref[...] += _colsum(d2 * uh)
        gbn_ref[...] += _colsum(d2)
        gwb_ref[...] += _colsum(d1)
        dext[0, 0:t, :] = d1
        d1h, _, _ = ln_bwd(d3h_ref[...], u1h_ref[...])
        dext[0, t:, :] = jnp.where(i < n_blk - 1, d1h, 0.0)
        _shifted_copies(dext)

        def glu_parts(zv):
            ci = zv + gb_ref[...]
            return ci[:, :D_CONV], jax.nn.sigmoid(ci[:, D_CONV:])

        val, sg = glu_parts(zm_ref[...])
        uext[0, HALO:, :] = val * sg
        valh, sgh = glu_parts(zh_ref[...])
        uext[0, 0:HALO, :] = jnp.where(i > 0, valh * sgh, 0.0)
        _shifted_copies(uext)

        for rc in range(0, t, CONV_RC):
            for lc in range(0, D_CONV, CONV_LC):
                acc = None
                for off, win in _windows(dext, [rc + k for k in range(CONV_K)], CONV_RC, lc, CONV_LC):
                    k = (CONV_K - 1) - (off - rc)
                    term = w_ref[k:k + 1, lc:lc + CONV_LC] * win
                    acc = term if acc is None else acc + term
                du0_s[rc:rc + CONV_RC, lc:lc + CONV_LC] = acc
                dchunk = dext[0, rc:rc + CONV_RC, lc:lc + CONV_LC]
                first = rc + HALO - (CONV_K - 1)
                for off, win in _windows(uext, [first + k for k in range(CONV_K)], CONV_RC, lc, CONV_LC):
                    k = off - first
                    pr = dchunk * win
                    part = pr[0:8]
                    for r8 in range(8, CONV_RC, 8):
                        part = part + pr[r8:r8 + 8]
                    gw_acc[k, :, lc:lc + CONV_LC] += part

        du0 = du0_s[...]
        dval = du0 * sg
        dgt = du0 * val * sg * (1.0 - sg)
        dci_ref[:, :D_CONV] = dval.astype(dci_ref.dtype)
        dci_ref[:, D_CONV:] = dgt.astype(dci_ref.dtype)
        ggb_ref[:, :D_CONV] += _colsum(dval)
        ggb_ref[:, D_CONV:] += _colsum(dgt)

        @pl.when(i == n_blk - 1)
        def _():
            gw_ref[...] = jnp.sum(gw_acc[...], axis=1)

    halo_next = lambda w: pl.BlockSpec((HALO, w), lambda i: (jnp.minimum((i + 1) * hb, last_halo), 0))
    return pl.pallas_call(
        body, name=name, grid=(n_blk,),
        in_specs=[_rowspec(t, D_CONV), halo_next(D_CONV), _rowspec(t, D_CONV), halo_next(D_CONV),
                  _rowspec(t, c2), pl.BlockSpec((HALO, c2), lambda i: (jnp.maximum(i * hb - 1, 0), 0)),
                  _vecspec(c2), pl.BlockSpec((HALO, D_CONV), lambda i: (0, 0)), _vecspec(D_CONV), _vecspec(D_CONV),
                  _ANY],
        out_specs=[_rowspec(t, c2, SEG_CI[0] // c2), _vecspec(D_CONV), _vecspec(D_CONV), _vecspec(D_CONV),
                   _vecspec(c2), pl.BlockSpec((HALO, D_CONV), lambda i: (0, 0))],
        out_shape=[_sds(dz.shape, dz.dtype), _sds((1, D_CONV), F32), _sds((1, D_CONV), F32), _sds((1, D_CONV), F32),
                   _sds((1, c2), F32), _sds((HALO, D_CONV), F32)],
        scratch_shapes=[pltpu.VMEM((8, t + HALO, D_CONV), F32), pltpu.VMEM((8, t + HALO, D_CONV), F32),
                        pltpu.VMEM((t, D_CONV), F32), pltpu.VMEM((HALO, 8, D_CONV), F32)],
        input_output_aliases={10: 0},
        compiler_params=_cp(("arbitrary",)),
    )(du3, du3, u1, u1, z, z, glu_b, dw_w, ln_g, ln_b, dz)


def _flash_bwd(qf, kf, va, do, lse_t, delta_t, *, name):
    nh, s, dk = qf.shape
    dv = va.shape[-1] // 2
    t = min(ATT_T, s)
    n = s // t
    nt = (((1,), (1,)), ((), ()))
    tn = (((0,), (0,)), ((), ()))

    def body(q_ref, do_ref, lse_ref, dl_ref, k_ref, v_ref, dq_ref, dk_ref, dv_ref,
             dk_s, dv_s, st_buf, dpt_buf):
        n_un = pl.program_id(1)
        j = n - 1 - n_un
        nxt = jnp.maximum(j - 1, 0)

        @pl.when(n_un == 0)
        def _():
            dq_ref[...] = jnp.zeros(dq_ref.shape, F32)

        dk_s[...] = jnp.zeros(dk_s.shape, F32)
        dv_s[...] = jnp.zeros(dv_s.shape, F32)

        def rows_at(blk):
            return pl.ds(pl.multiple_of(blk * t, t), t)

        def rows_of(b):
            return rows_at(n - 1 - b)

        k = k_ref[0, rows_at(j), :]

        def produce(kj, b, slot):
            rows = rows_of(b)
            st_buf[slot] = lax.dot_general(k_ref[0, rows_at(kj), :], q_ref[0, rows, :], nt,
                                           preferred_element_type=F32)
            dpt_buf[slot] = lax.dot_general(v_ref[0, rows_at(kj), 0:dv], do_ref[rows, :], nt,
                                            preferred_element_type=F32)

        def consume(b, slot, masked):
            i = n - 1 - b
            rows = rows_of(b)
            q, dov = q_ref[0, rows, :], do_ref[rows, :]
            pt = jnp.exp(st_buf[slot] - lse_ref[0, i])
            if masked:
                key = lax.broadcasted_iota(jnp.int32, (t, t), 0)
                qry = lax.broadcasted_iota(jnp.int32, (t, t), 1)
                pt = jnp.where(key <= qry, pt, 0.0)
            dv_s[...] += jnp.dot(pt.astype(MXU_DTYPE), dov, preferred_element_type=F32)
            dst = (pt * (dpt_buf[slot] - dl_ref[0, i])).astype(MXU_DTYPE)
            dk_s[...] += jnp.dot(dst, q, preferred_element_type=F32)
            dq_ref[0, rows, :] += lax.dot_general(dst, k, tn, preferred_element_type=F32)

        @pl.when(n_un == 0)
        def _():
            produce(j, 0, 2)
            consume(0, 2, True)
            produce(nxt, 0, 2)

        @pl.when(n_un > 0)
        def _():
            produce(j, 1, 1)
            consume(0, 2, False)

            def pair(a, carry):
                produce(j, 2 * a + 2, 0)
                consume(2 * a + 1, 1, False)
                produce(j, 2 * a + 3, 1)
                consume(2 * a + 2, 0, False)
                return carry

            lax.fori_loop(0, (n_un - 1) // 2, pair, 0)

            @pl.when(n_un % 2 == 1)
            def _():
                produce(nxt, 0, 2)
                consume(n_un, 1, True)

            @pl.when(n_un % 2 == 0)
            def _():
                produce(j, n_un, 0)
                consume(n_un - 1, 1, False)
                produce(nxt, 0, 2)
                consume(n_un, 0, True)

        dk_ref[0] = dk_s[...]
        dv_ref[0] = dv_s[...]

    head = lambda h, j: (h, 0, 0)
    rowv = pl.BlockSpec((1, n, 1, t), lambda h, j: (h, 0, 0, 0))
    return pl.pallas_call(
        body, name=name, grid=(nh, n),
        in_specs=[pl.BlockSpec((1, s, dk), head),
                  pl.BlockSpec((s, dv), lambda h, j: (0, h)),
                  rowv, rowv,
                  pl.BlockSpec((1, s, dk), head),
                  pl.BlockSpec((1, s, 2 * dv), head)],
        out_specs=[pl.BlockSpec((1, s, dk), head),
                   pl.BlockSpec((1, t, dk), lambda h, g: (h, n - 1 - g, 0)),
                   pl.BlockSpec((1, t, dv), lambda h, g: (h, n - 1 - g, 0))],
        out_shape=[_sds((nh, s, dk), F32), _sds((nh, s, dk), F32), _sds((nh, s, dv), F32)],
        scratch_shapes=[pltpu.VMEM((t, dk), F32), pltpu.VMEM((t, dv), F32),
                        pltpu.VMEM((3, t, t), F32), pltpu.VMEM((3, t, t), F32)],
        compiler_params=_cp(("arbitrary", "arbitrary")),
    )(qf, do, lse_t, delta_t, kf, va)


def _qk_bwd(dqf, dkf, dvf, q_raw, kv, z, c_t, s1_t, s2_t, gqn, gqr, gkn, gkr, *, name):
    s = q_raw.shape[0]
    t = min(ROW_T, s)
    scale = 1.0 / math.sqrt(QK_DIM)

    def body(dq_ref, dk_ref, dv_ref, q_ref, kv_ref, kr_ref, c_ref, s1_ref, s2_ref,
             gqn_ref, gqr_ref, gkn_ref, gkr_ref, dqr_ref, dkv_ref, dkr_ref, ggq_ref, ggk_ref):
        _acc_init([ggq_ref, ggk_ref])
        c_v, s1_v, s2_v = c_ref[...], s1_ref[...], s2_ref[...]
        kr = kr_ref[...]
        kr_ss = _lanesum(kr * kr)
        dkr = jnp.zeros(kr.shape, F32)
        ggq_n = ggq_r = ggk_n = ggk_r = jnp.zeros((1, LANE), F32)

        def norm_bwd(n, r, rs, dyn, dyr, gn, gr):
            nh_, rh_ = n * rs, r * rs
            dnh, drh = dyn * gn, dyr * gr
            dot = (_lanesum(dnh * nh_) + _lanesum(drh * rh_)) * (1.0 / QK_DIM)
            return rs * (dnh - nh_ * dot), rs * (drh - rh_ * dot), _colsum(dyn * nh_), _colsum(dyr * rh_)

        for h in range(N_HEADS):
            n = q_ref[:, h * LANE:(h + 1) * LANE]
            r = q_ref[:, N_HEADS * LANE + h * LANE:N_HEADS * LANE + (h + 1) * LANE]
            rs = lax.rsqrt((_lanesum(n * n) + _lanesum(r * r)) * (1.0 / QK_DIM) + EPS)
            dyn = dq_ref[h, :, 0:LANE] * scale
            dyr = _rope_bwd(dq_ref[h, :, LANE:HEAD_PAD] * scale, c_v, s1_v, s2_v)
            dn, dr, g_n, g_r = norm_bwd(n, r, rs, dyn, dyr, gqn_ref[...], gqr_ref[...])
            dqr_ref[:, h * LANE:(h + 1) * LANE] = dn.astype(dqr_ref.dtype)
            dqr_ref[:, N_HEADS * LANE + h * LANE:N_HEADS * LANE + (h + 1) * LANE] = dr.astype(dqr_ref.dtype)
            ggq_n, ggq_r = ggq_n + g_n, ggq_r + g_r

            n = kv_ref[:, h * 2 * LANE:h * 2 * LANE + LANE]
            rs = lax.rsqrt((_lanesum(n * n) + kr_ss) * (1.0 / QK_DIM) + EPS)
            dyn = dk_ref[h, :, 0:LANE]
            dyr = _rope_bwd(dk_ref[h, :, LANE:HEAD_PAD], c_v, s1_v, s2_v)
            dn, dr, g_n, g_r = norm_bwd(n, kr, rs, dyn, dyr, gkn_ref[...], gkr_ref[...])
            dkv_ref[:, h * 2 * LANE:h * 2 * LANE + LANE] = dn.astype(dkv_ref.dtype)
            dkv_ref[:, h * 2 * LANE + LANE:(h + 1) * 2 * LANE] = dv_ref[h].astype(dkv_ref.dtype)
            dkr = dkr + dr
            ggk_n, ggk_r = ggk_n + g_n, ggk_r + g_r

        dkr_ref[...] = dkr.astype(dkr_ref.dtype)
        ggq_ref[:, 0:LANE] += ggq_n
        ggq_ref[:, LANE:] += ggq_r
        ggk_ref[:, 0:LANE] += ggk_n
        ggk_ref[:, LANE:] += ggk_r

    hspec = lambda w: pl.BlockSpec((N_HEADS, t, w), lambda i: (0, i, 0))
    wide = 2 * N_HEADS * LANE
    return pl.pallas_call(
        body, name=name, grid=(s // t,),
        in_specs=[hspec(HEAD_PAD), hspec(HEAD_PAD), hspec(V_DIM), _rowspec(t, wide), _rowspec(t, wide),
                  _rowspec(t, LANE, SEG_KR[0] // LANE), _rowspec(t, LANE), _rowspec(t, LANE), _rowspec(t, LANE),
                  _vecspec(LANE), _vecspec(LANE), _vecspec(LANE), _vecspec(LANE)],
        out_specs=[_rowspec(t, wide), _rowspec(t, wide), _rowspec(t, LANE), _vecspec(2 * LANE), _vecspec(2 * LANE)],
        out_shape=[_sds((s, wide), MXU_DTYPE), _sds((s, wide), MXU_DTYPE), _sds((s, LANE), MXU_DTYPE),
                   _sds((1, 2 * LANE), F32), _sds((1, 2 * LANE), F32)],
        compiler_params=_cp(("arbitrary",)),
    )(dqf, dkf, dvf, q_raw, kv, z, c_t, s1_t, s2_t, gqn, gqr, gkn, gkr)


def _lat_bwd(dqn, dkn, dkr, z, dz, g_ql, g_kvl, *, name):
    s = z.shape[0]
    t = min(ROW_T, s)
    o_ql, o_kvl, o_kr = (seg[0] - SEG_LAT[0] for seg in (SEG_QL, SEG_KVL, SEG_KR))

    def body(dq_ref, dk_ref, dkr_ref, ql_ref, kvl_ref, gq_ref, gk_ref, dz_in_ref, dz_ref, ggq_ref, ggk_ref):
        _acc_init([ggq_ref, ggk_ref])
        for d_ref, src, g_ref, off, gg_ref in ((dq_ref, ql_ref, gq_ref, o_ql, ggq_ref),
                                               (dk_ref, kvl_ref, gk_ref, o_kvl, ggk_ref)):
            v, dy = src[...], d_ref[...]
            r = lax.rsqrt(jnp.mean(v * v, axis=-1, keepdims=True) + EPS)
            vh = v * r
            dvh = dy * g_ref[...]
            dz_ref[:, off:off + v.shape[1]] = (
                r * (dvh - vh * jnp.mean(dvh * vh, axis=-1, keepdims=True))).astype(dz_ref.dtype)
            gg_ref[...] += _colsum(dy * vh)
        dz_ref[:, o_kr:o_kr + LANE] = dkr_ref[...]
        dz_ref[:, o_kr + LANE:] = jnp.zeros((t, SEG_LAT[1] - o_kr - LANE), dz_ref.dtype)

    return pl.pallas_call(
        body, name=name, grid=(s // t,),
        in_specs=[_rowspec(t, Q_LORA), _rowspec(t, KV_LORA), _rowspec(t, LANE),
                  _rowspec(t, Q_LORA, SEG_QL[0] // Q_LORA), _rowspec(t, KV_LORA, SEG_KVL[0] // KV_LORA),
                  _vecspec(Q_LORA), _vecspec(KV_LORA), _ANY],
        out_specs=[_rowspec(t, SEG_LAT[1], SEG_LAT[0] // SEG_LAT[1]), _vecspec(Q_LORA), _vecspec(KV_LORA)],
        out_shape=[_sds(dz.shape, dz.dtype), _sds((1, Q_LORA), F32), _sds((1, KV_LORA), F32)],
        input_output_aliases={7: 0},
        compiler_params=_cp(("arbitrary",)),
    )(dqn, dkn, dkr, z, z, g_ql, g_kvl, dz)


def _prenorm_bwd(dh, x, gxo, g, sc1p, *, name):
    s, d = x.shape
    t = min(ROW_T, s)

    def body(dh_ref, x_ref, gx_ref, g_ref, sc_ref, dx_ref, dsh_ref, dsc_ref, gg_ref):
        _acc_init([dsh_ref, dsc_ref, gg_ref])
        xv, dhv = x_ref[...], dh_ref[...]
        r = lax.rsqrt(jnp.mean(xv * xv, axis=-1, keepdims=True) + EPS)
        xn = xv * r
        dsh_ref[...] += _colsum(dhv)
        dsc_ref[...] += _colsum(dhv * (xn * g_ref[...]))
        dm = dhv * sc_ref[...]
        gg_ref[...] += _colsum(dm * xn)
        dxn = dm * g_ref[...]
        dx_ref[...] = gx_ref[...] + r * (dxn - xn * jnp.mean(dxn * xn, axis=-1, keepdims=True))

    return pl.pallas_call(
        body, name=name, grid=(s // t,),
        in_specs=[_rowspec(t, d), _rowspec(t, d), _rowspec(t, d), _vecspec(d), _vecspec(d)],
        out_specs=[_rowspec(t, d), _vecspec(d), _vecspec(d), _vecspec(d)],
        out_shape=[_sds((s, d), F32), _sds((1, d), F32), _sds((1, d), F32), _sds((1, d), F32)],
        compiler_params=_cp(("arbitrary",)),
    )(dh, x, gxo, g, sc1p)


def _ada_fwd(c_all, ada_w, ada_b_cols, *, name):
    nl, d, cols = ada_w.shape

    def body(c_ref, w_ref, b_ref, o_ref):
        ca = _silu(c_ref[...]).astype(MXU_DTYPE)
        o_ref[0] = jnp.dot(ca, w_ref[0].astype(MXU_DTYPE), preferred_element_type=F32) + b_ref[0]

    return pl.pallas_call(
        body, name=name, grid=(nl,),
        in_specs=[pl.BlockSpec((N_DEV, d), lambda l: (0, 0)), pl.BlockSpec((1, d, cols), lambda l: (l, 0, 0)),
                  pl.BlockSpec((1, 1, cols), lambda l: (l, 0, 0))],
        out_specs=pl.BlockSpec((1, N_DEV, cols), lambda l: (l, 0, 0)),
        out_shape=_sds((nl, N_DEV, cols), F32),
        compiler_params=_cp(("parallel",)),
    )(c_all, ada_w, ada_b_cols)


def _ada_bwd(c_all_t, dmod_cols, *, name):
    nl, _, cols = dmod_cols.shape
    d = c_all_t.shape[0]

    def body(c_ref, dm_ref, o_ref):
        ca = _silu(c_ref[...]).astype(MXU_DTYPE)
        o_ref[0] = jnp.dot(ca, dm_ref[0].astype(MXU_DTYPE), preferred_element_type=F32)

    return pl.pallas_call(
        body, name=name, grid=(nl,),
        in_specs=[pl.BlockSpec((d, N_DEV), lambda l: (0, 0)), pl.BlockSpec((1, N_DEV, cols), lambda l: (l, 0, 0))],
        out_specs=pl.BlockSpec((1, d, cols), lambda l: (l, 0, 0)),
        out_shape=_sds((nl, d, cols), F32),
        compiler_params=_cp(("parallel",)),
    )(c_all_t, dmod_cols)


def _adamw(gparts, w, m, v, *, name):
    shape = w.shape
    cols = shape[-1]
    per_layer = isinstance(gparts, (list, tuple))
    nl = shape[0] if per_layer else 1
    rows = w.size // cols // nl
    glist = list(gparts) if per_layer else [gparts]
    npart = glist[0].shape[0]
    glist = [g.reshape(npart, rows, cols) for g in glist]
    w3, m3, v3 = (a.reshape(nl, rows, cols) for a in (w, m, v))
    budget = 2 * 1024 * 1024
    fits = [t for t in range(min(rows, 256) // 8 * 8, 7, -8)
            if rows % t == 0 and npart * t * cols * glist[0].dtype.itemsize <= budget]
    t = fits[0] if fits else rows
    nb = rows // t

    def body(*refs):
        g_refs = refs[:nl]
        w_ref, m_ref, v_ref, go_ref, d_ref, mo_ref, vo_ref, g_s = refs[nl:]
        layer = pl.program_id(0)
        for l in range(nl):
            @pl.when(layer == l)
            def _(l=l):
                g = g_refs[l][0].astype(F32)
                for p in range(1, npart):
                    g = g + g_refs[l][p].astype(F32)
                g_s[...] = g

        g = g_s[...]
        mn = ADAM_B1 * m_ref[0] + (1.0 - ADAM_B1) * g
        vn = ADAM_B2 * v_ref[0] + (1.0 - ADAM_B2) * (g * g)
        m_hat = mn / (1.0 - ADAM_B1 ** ADAM_STEP)
        v_hat = vn / (1.0 - ADAM_B2 ** ADAM_STEP)
        go_ref[0] = g
        d_ref[0] = -ADAM_LR * (m_hat / (jnp.sqrt(v_hat) + ADAM_EPS) + ADAM_WD * w_ref[0])
        mo_ref[0] = mn
        vo_ref[0] = vn

    def g_map(l):
        return lambda layer, i: (0, jnp.where(layer == l, i, jnp.where(layer < l, 0, nb - 1)), 0)

    spec = pl.BlockSpec((1, t, cols), lambda layer, i: (layer, i, 0))
    outs = pl.pallas_call(
        body, name=name, grid=(nl, nb),
        in_specs=[pl.BlockSpec((npart, t, cols), g_map(l)) for l in range(nl)] + [spec, spec, spec],
        out_specs=[spec] * 4, out_shape=[_sds((nl, rows, cols), F32)] * 4,
        scratch_shapes=[pltpu.VMEM((t, cols), F32)],
        compiler_params=_cp(("arbitrary", "arbitrary")),
    )(*glist, w3, m3, v3)
    return tuple(o.reshape(shape) for o in outs)


_ANY = pl.BlockSpec(memory_space=pl.ANY)


def _all_gather(blocks, *, name):
    na = len(blocks)

    def body(*refs):
        x_refs, out_refs = refs[:na], refs[na:2 * na]
        send_sems, recv_sems, local_sems = refs[2 * na:]
        x, y, c = lax.axis_index("x"), lax.axis_index("y"), lax.axis_index("c")
        me, sibling = (x, y, c), (x, y, 1 - c)
        chips = [(1 - x, y), (x, 1 - y), (1 - x, 1 - y)]

        def slot(a, px, py, pc):
            return out_refs[a].at[4 * px + 2 * py + pc]

        def copy(a, k, blk, to, src=None):
            return pltpu.make_async_remote_copy(
                src_ref=slot(a, *blk) if src is None else src, dst_ref=slot(a, *blk),
                send_sem=send_sems.at[7 * a + k], recv_sem=recv_sems.at[7 * a + k],
                device_id=to, device_id_type=MESH_ID)

        mine = [pltpu.make_async_copy(x_refs[a], slot(a, *me), local_sems.at[a]) for a in range(na)]
        for cp in mine:
            cp.start()
        first = []
        for a in range(na):
            first.append(copy(a, 0, me, sibling, src=x_refs[a]))
            first += [copy(a, 1 + j, me, (*chip, c), src=x_refs[a]) for j, chip in enumerate(chips)]
        for cp in first:
            cp.start()
        passed = []
        for a in range(na):
            for j, chip in enumerate(chips):
                copy(a, 1 + j, (*chip, c), me).wait_recv()
                fwd = copy(a, 4 + j, (*chip, c), sibling)
                fwd.start()
                passed.append(fwd)
        for a in range(na):
            copy(a, 0, sibling, me).wait_recv()
            for j, chip in enumerate(chips):
                copy(a, 4 + j, (*chip, 1 - c), me).wait_recv()
        for cp in first + passed:
            cp.wait_send()
        for cp in mine:
            cp.wait()

    outs = pl.pallas_call(
        body, name=name, in_specs=[_ANY] * na, out_specs=[_ANY] * na,
        out_shape=[_sds((N_DEV,) + b.shape, b.dtype) for b in blocks],
        scratch_shapes=[pltpu.SemaphoreType.DMA((7 * na,)), pltpu.SemaphoreType.DMA((7 * na,)),
                        pltpu.SemaphoreType.DMA((na,))],
    )(*blocks)
    return list(outs)


_HBM = pl.BlockSpec(memory_space=pltpu.HBM)
_SEM = pl.BlockSpec(memory_space=pltpu.SEMAPHORE)
_EFFECT = pltpu.SideEffectType.DATAFLOW_SIDE_EFFECTING


def _peers(x, y, c):
    out = []
    for k in range(1, N_DEV):
        out.append((1 - x if k & 4 else x, 1 - y if k & 2 else y, 1 - c if k & 1 else c))
    return out


def _own_slots(srcs, scatter, *, name, after=None):
    na = len(srcs)
    n_extra = 0 if after is None else 1
    me = (4 * lax.axis_index("x") + 2 * lax.axis_index("y") + lax.axis_index("c")).astype(jnp.int32).reshape(1)

    def body(me_ref, *refs):
        in_refs, out_refs = refs[:na], refs[na + n_extra:]
        for a in range(na):
            out_refs[a][0] = in_refs[a][0] if scatter else in_refs[a][...]

    def slot_spec(shard):
        zeros = (0,) * len(shard)
        return pl.BlockSpec((1,) + tuple(shard), lambda i, me_ref: (me_ref[0],) + zeros)

    def whole_spec(shape):
        zeros = (0,) * len(shape)
        return pl.BlockSpec(tuple(shape), lambda i, me_ref: zeros)

    shards = [s.shape[1:] if scatter else s.shape for s in srcs]
    in_specs = [slot_spec(sh) if scatter else whole_spec(sh) for sh in shards] + [_ANY] * n_extra
    outs = pl.pallas_call(
        body, name=name,
        grid_spec=pltpu.PrefetchScalarGridSpec(
            num_scalar_prefetch=1, grid=(1,), in_specs=in_specs, out_specs=[slot_spec(sh) for sh in shards]),
        out_shape=[_sds((N_DEV,) + tuple(sh), s.dtype) for sh, s in zip(shards, srcs)],
        compiler_params=_cp(("arbitrary",)),
    )(me, *srcs, *([] if after is None else [after]))
    return list(outs)


_N_COPIES = dict(scatter=7, gather=7, chips=4, forward=3)


def _exchange_copies(src_refs, land_refs, send_sems, recv_sems, mode):
    x, y, c = lax.axis_index("x"), lax.axis_index("y"), lax.axis_index("c")
    me = 4 * x + 2 * y + c
    nc = _N_COPIES[mode]
    chips = [(1 - x, y), (x, 1 - y), (1 - x, 1 - y)]
    cps = []
    for a in range(len(land_refs)):
        if mode in ("scatter", "gather"):
            plan = [((src_refs[a].at[4 * px + 2 * py + pc] if mode == "scatter" else src_refs[a]),
                     land_refs[a].at[me], (px, py, pc)) for px, py, pc in _peers(x, y, c)]
        elif mode == "chips":
            plan = [(src_refs[a], land_refs[a].at[me], to) for to in [(x, y, 1 - c)] + [(*ch, c) for ch in chips]]
        else:
            plan = [(land_refs[a].at[4 * px + 2 * py + c], land_refs[a].at[4 * px + 2 * py + c], (x, y, 1 - c))
                    for px, py in chips]
        for k, (src, dst, to) in enumerate(plan):
            cps.append(pltpu.make_async_remote_copy(
                src_ref=src, dst_ref=dst, send_sem=send_sems.at[nc * a + k], recv_sem=recv_sems.at[nc * a + k],
                device_id=to, device_id_type=MESH_ID))
    return cps


def _exchange_start(srcs, lands, mode, *, name):
    ns, nz = len(srcs), len(lands)
    nsem = _N_COPIES[mode] * nz

    def body(*refs):
        src_refs, land_refs = refs[:ns], refs[ns:ns + nz]
        send_sems, recv_sems = refs[ns + nz], refs[ns + nz + 1]
        token = refs[-1]
        for cp in _exchange_copies(src_refs, land_refs, send_sems, recv_sems, mode):
            cp.start()
        token[...] = jnp.zeros(token.shape, token.dtype)

    hbm = lambda a: pltpu.HBM(a.shape, a.dtype)
    outs = pl.pallas_call(
        body, name=name,
        out_shape=(pltpu.SemaphoreType.DMA((nsem,)), pltpu.SemaphoreType.DMA((nsem,)),
                   *[hbm(a) for a in srcs], *[hbm(a) for a in lands], _sds((8, LANE), F32)),
        in_specs=[_HBM] * (ns + nz),
        out_specs=(_SEM, _SEM, *[_HBM] * (ns + nz), pl.BlockSpec(memory_space=pltpu.VMEM)),
        input_output_aliases={i: 2 + i for i in range(ns + nz)},
        compiler_params=pltpu.CompilerParams(has_side_effects=_EFFECT),
    )(*[pltpu.with_memory_space_constraint(a, pltpu.HBM) for a in list(srcs) + list(lands)])
    return outs[0], outs[1], list(outs[2:2 + ns]), list(outs[2 + ns:2 + ns + nz]), outs[-1]


def _exchange_wait(send_sems, recv_sems, srcs, lands, after, mode, *, name):
    ns, nz = len(srcs), len(lands)

    def body(*refs):
        src_refs, land_refs = refs[:ns], refs[ns:ns + nz]
        s_sems, r_sems = refs[ns + nz], refs[ns + nz + 1]
        for cp in _exchange_copies(src_refs, land_refs, s_sems, r_sems, mode):
            cp.wait_send()
            cp.wait_recv()

    hbm = lambda a: pltpu.HBM(a.shape, a.dtype)
    outs = pl.pallas_call(
        body, name=name,
        out_shape=(*[hbm(a) for a in srcs], *[hbm(a) for a in lands]),
        in_specs=[_HBM] * (ns + nz) + [_SEM, _SEM, _ANY],
        out_specs=tuple([_HBM] * (ns + nz)),
        input_output_aliases={i: i for i in range(ns + nz)},
        compiler_params=pltpu.CompilerParams(has_side_effects=_EFFECT),
    )(*srcs, *lands, send_sems, recv_sems, after)
    return list(outs[ns:])


_WIN_SEGS = (("ql", 0, Q_LORA, SEG_QL[0]), ("kvl", Q_LORA, KV_LORA, SEG_KVL[0]),
             ("kr", Q_LORA + KV_LORA, ROPE, SEG_KR[0]), ("mg", Q_LORA + KV_LORA + ROPE, D_MLA, SEG_MG[0]),
             ("ci", Q_LORA + KV_LORA + ROPE + D_MLA, 2 * D_CONV, SEG_CI[0]),
             ("cg", Q_LORA + KV_LORA + ROPE + D_MLA + 2 * D_CONV, D_CONV, SEG_CG[0]))
_WIN_SHARD = IN_COLS // N_DEV


def _win_pieces():
    out = []
    for _, o, n, new in _WIN_SEGS:
        for j in range(N_DEV):
            lo, hi = max(o, j * _WIN_SHARD), min(o + n, (j + 1) * _WIN_SHARD)
            if lo < hi:
                out.append((j, lo - j * _WIN_SHARD, new + lo - o, hi - lo))
    return out


WIN_T = 512


def _win_assemble(w_all, *, name):
    d = w_all.shape[2]
    t = min(WIN_T, d)
    pieces = sorted(_win_pieces(), key=lambda p: p[2])
    assert all(lo % 8 == 0 and n % 8 == 0 for _, lo, _, n in pieces)

    def body(w_ref, o_ref):
        rows = [w_ref[j].astype(F32)[lo:lo + n, :] for j, lo, _, n in pieces]
        rows.append(jnp.zeros((IN_PAD - (SEG_KR[0] + ROPE), t), F32))
        o_ref[...] = jnp.concatenate(rows, axis=0).astype(o_ref.dtype)

    return pl.pallas_call(
        body, name=name, grid=(d // t,),
        in_specs=[pl.BlockSpec((N_DEV, _WIN_SHARD, t), lambda i: (0, 0, i))],
        out_specs=pl.BlockSpec((IN_PAD, t), lambda i: (0, i)), out_shape=_sds((IN_PAD, d), w_all.dtype),
        compiler_params=_cp(("parallel",)),
    )(w_all)


def _win_split(grad, *, name):
    d = grad.shape[1]
    t = min(WIN_T, d)
    by_shard = [sorted([p for p in _win_pieces() if p[0] == j], key=lambda p: p[1]) for j in range(N_DEV)]

    def body(g_ref, o_ref):
        for j in range(N_DEV):
            rows = [g_ref[new:new + n, :] for _, _, new, n in by_shard[j]]
            o_ref[j] = jnp.concatenate(rows, axis=0).astype(o_ref.dtype)

    return pl.pallas_call(
        body, name=name, grid=(d // t,),
        in_specs=[pl.BlockSpec((IN_PAD, t), lambda i: (0, i))],
        out_specs=pl.BlockSpec((N_DEV, _WIN_SHARD, t), lambda i: (0, 0, i)),
        out_shape=_sds((N_DEV, _WIN_SHARD, d), WIRE_DTYPE),
        compiler_params=_cp(("parallel",)),
    )(grad)


def _cols_to_shards(a):
    r, n = a.shape
    return a.reshape(r, N_DEV, n // N_DEV).transpose(1, 0, 2)


def _shards_to_cols(a):
    nd, r, w = a.shape
    return a.transpose(1, 0, 2).reshape(r, nd * w)


def _win_permute(w_in):
    o_ql, o_kvl, o_kr, o_mg = 0, Q_LORA, Q_LORA + KV_LORA, Q_LORA + KV_LORA + ROPE
    o_ci = o_mg + D_MLA
    o_cg = o_ci + 2 * D_CONV
    seg = lambda o, n: w_in[:, o:o + n]
    pad = jnp.zeros((w_in.shape[0], IN_PAD - (SEG_KR[0] + ROPE)), w_in.dtype)
    return jnp.concatenate([seg(o_ci, 2 * D_CONV), seg(o_mg, D_MLA), seg(o_cg, D_CONV), seg(o_ql, Q_LORA),
                            seg(o_kvl, KV_LORA), seg(o_kr, ROPE), pad], axis=1)


def _win_unpermute(g):
    seg = lambda s, n=None: g[:, s[0]:s[0] + (s[1] if n is None else n)]
    return jnp.concatenate([seg(SEG_QL), seg(SEG_KVL), seg(SEG_KR, ROPE), seg(SEG_MG), seg(SEG_CI), seg(SEG_CG)], axis=1)


def _qup_permute(w):
    w3 = w.reshape(w.shape[0], N_HEADS, QK_DIM)
    nope = w3[:, :, :NOPE].reshape(w.shape[0], N_HEADS * NOPE)
    rope = jnp.pad(w3[:, :, NOPE:], ((0, 0), (0, 0), (0, LANE - ROPE))).reshape(w.shape[0], N_HEADS * LANE)
    return jnp.concatenate([nope, rope], axis=1)


def _qup_unpermute(g):
    r = g.shape[0]
    nope = g[:, :N_HEADS * NOPE].reshape(r, N_HEADS, NOPE)
    rope = g[:, N_HEADS * NOPE:].reshape(r, N_HEADS, LANE)[:, :, :ROPE]
    return jnp.concatenate([nope, rope], axis=2).reshape(r, N_HEADS * QK_DIM)


def _norm_tiles(g):
    return g[:NOPE].reshape(1, LANE), jnp.pad(g[NOPE:], (0, LANE - ROPE)).reshape(1, LANE)


def _norm_untile(gt):
    return jnp.concatenate([gt[0, :NOPE], gt[0, LANE:LANE + ROPE]])


def _rope_tiles(positions):
    inv_freq = 1.0 / (ROPE_THETA ** (jnp.arange(0, ROPE, 2, dtype=F32) / ROPE))
    ang = positions.astype(F32)[:, None] * inv_freq
    cos, sin = jnp.cos(ang), jnp.sin(ang)
    zq = jnp.zeros_like(cos)
    c_t = jnp.concatenate([cos, cos, zq, zq], axis=1)
    s1_t = jnp.concatenate([-sin, zq, zq, zq], axis=1)
    s2_t = jnp.concatenate([zq, sin, zq, zq], axis=1)
    return c_t, s1_t, s2_t


_BIG = ("w_in", "w_q_up", "w_kv_up", "w_pw", "w_out")
_COL_SHARDED = ("w_in", "w_q_up", "w_kv_up")


def _pack_rows(arrs):
    return jnp.concatenate([a.reshape(-1, LANE) for a in arrs], axis=0)


def _unpack_rows(buf, shapes):
    out, r0 = [], 0
    lead = buf.shape[:-2]
    for shp in shapes:
        n = math.prod(shp) // LANE
        out.append(buf[..., r0:r0 + n, :].reshape(lead + tuple(shp)))
        r0 += n
    return out


_SMALL = (("dmod", 3 * D_MODEL), ("norm_g", D_MODEL), ("q_lat_g", Q_LORA), ("kv_lat_g", KV_LORA),
          ("q_norm_g", 2 * LANE), ("k_norm_g", 2 * LANE), ("glu_b", 2 * D_CONV), ("dw_w", HALO * D_CONV),
          ("dw_b", D_CONV), ("conv_ln_g", D_CONV), ("conv_ln_b", D_CONV), ("b_pw", D_CONV))


def _layer_fwd(x, p, rope, l, early=None, late=None):
    n = lambda s: f"{s}_l{l}"
    c_t, s1_t, s2_t = rope
    h = _prenorm(x, p["norm_g"], p["shift"], p["sc1p"], name=n("prenorm"))
    if early is not None:
        p = {**p, **early(h)}
    z = _mm(h, p["w_in"], tb=True, name=n("in_proj"), tn=IN_TILE, n_outer=True,
            after=p.get("in_proj_after"))
    if late is not None:
        p = {**p, **late(z)}
    qn, kn = _lat_norm(z, p["q_lat_g"], p["kv_lat_g"], name=n("lat_norm"))
    q_raw = _mm(qn, p["w_q_up"], name=n("q_up"), tn=1024)
    kv = _mm(kn, p["w_kv_up"], name=n("kv_up"), tn=1024)
    qf, kf, vf = _qk_prep(q_raw, kv, z, c_t, s1_t, s2_t, *p["qk_tiles"], name=n("qk_prep"))
    o, lse = _flash_fwd(qf, kf, vf, name=n("flash_fwd"))
    u1, u3 = _conv_fwd(z, p["glu_b"], p["dw_w"], p["dw_b"], p["conv_ln_g"], p["conv_ln_b"], name=n("conv_fwd"))
    u4m = _mm(u3, p["w_pw"], name=n("pw"), tn=1024)
    cat = _gate_cat(o, z, u4m, p["b_pw"], name=n("gate_cat"))
    y, x_next = _mm(cat, p["w_out"], name=n("out_proj"), tn=1024, residual=(x, p["gate"]))
    saved = dict(x=x, h=h, z=z, qn=qn, kn=kn, q_raw=q_raw, kv=kv, qf=qf, kf=kf, vf=vf, o=o, lse=lse,
                 u1=u1, u3=u3, u4m=u4m, cat=cat, y=y)
    return x_next, saved, p


def _layer_bwd(gxo, p, sv, rope, l, hook_rest=None, hook_w_in=None):
    n = lambda s: f"{s}_l{l}"
    c_t, s1_t, s2_t = rope
    z = sv["z"]
    dy, dgate = _out_bwd(gxo, sv["y"], p["gate"], name=n("out_bwd"))
    g_w_out = _mm(sv["cat"], dy, ta=True, name=n("g_w_out"), tm=1024, tn=1024)
    dcat = _mm(dy, p["w_out"], tb=True, name=n("d_cat"), tn=1024)
    do, delta, du4, g_b_pw, dz = _gate_bwd(dcat, sv["o"], z, sv["u4m"], p["b_pw"], name=n("gate_bwd"))
    g_w_pw = _mm(sv["u3"], du4, ta=True, name=n("g_w_pw"), tm=1024, tn=1024, tk=512)
    du3 = _mm(du4, p["w_pw"], tb=True, name=n("d_u3"), tn=1024)
    dz, g_ln_g, g_ln_b, g_dw_b, g_glu_b, g_dw_w = _conv_bwd(
        du3, sv["u1"], z, dz, p["glu_b"], p["dw_w"], p["conv_ln_g"], p["conv_ln_b"], name=n("conv_bwd"))
    t_att = min(ATT_T, z.shape[0])
    to_lanes = lambda a: a.reshape(N_HEADS, z.shape[0] // t_att, 1, t_att)
    dqf, dkf, dvf = _flash_bwd(sv["qf"], sv["kf"], sv["vf"], do,
                               to_lanes(sv["lse"][:, :, 0]), to_lanes(delta), name=n("flash_bwd"))
    dq_raw, dkv, dkr, g_qn, g_kn = _qk_bwd(dqf, dkf, dvf, sv["q_raw"], sv["kv"], z, c_t, s1_t, s2_t,
                                            *p["qk_tiles"], name=n("qk_bwd"))
    g_w_q_up = _mm(sv["qn"], dq_raw, ta=True, name=n("g_w_q_up"), tm=512, tn=1024, tk=512)
    dqn = _mm(dq_raw, p["w_q_up"], tb=True, name=n("d_qn"))
    g_w_kv_up = _mm(sv["kn"], dkv, ta=True, name=n("g_w_kv_up"), tm=256, tn=1024, tk=512)
    dkn = _mm(dkv, p["w_kv_up"], tb=True, name=n("d_kn"))
    dz, g_ql, g_kvl = _lat_bwd(dqn, dkn, dkr, z, dz, p["q_lat_g"], p["kv_lat_g"], name=n("lat_bwd"))
    big = dict(w_q_up=g_w_q_up, w_kv_up=g_w_kv_up, w_pw=g_w_pw, w_out=g_w_out)
    after = None if hook_rest is None else hook_rest(big)
    g_w_in = _mm(dz, sv["h"], ta=True, name=n("g_w_in"), tm=512, tn=1024, after=after)
    big["w_in"] = g_w_in
    after = None if hook_w_in is None else hook_w_in(g_w_in)
    dh = _mm(dz, p["w_in"], name=n("d_h"), tn=1024, after=after)
    dx, dshift, dscale, g_norm = _prenorm_bwd(dh, sv["x"], gxo, p["norm_g"], p["sc1p"], name=n("prenorm_bwd"))
    small = dict(dmod=jnp.concatenate([dshift, dscale, dgate], axis=1), norm_g=g_norm, q_lat_g=g_ql, kv_lat_g=g_kvl,
                 q_norm_g=g_qn, k_norm_g=g_kn, glu_b=g_glu_b, dw_w=g_dw_w, dw_b=g_dw_b,
                 conv_ln_g=g_ln_g, conv_ln_b=g_ln_b, b_pw=g_b_pw)
    return dx, big, small


def _layer_params(l, full, mod_l, small):
    d = D_MODEL
    row = lambda a: a.reshape(1, -1)
    shift, scale, gate = mod_l[:, :d], mod_l[:, d:2 * d], mod_l[:, 2 * d:]
    dw_w = jnp.pad(full["dw_w"][l], ((0, HALO - CONV_K), (0, 0)))
    return dict(
        shift=shift, sc1p=1.0 + scale, gate=gate, norm_g=row(small["norm_g"][l]),
        **{k: full[k][l] for k in _BIG if k in full}, dw_w=dw_w,
        q_lat_g=row(small["q_lat_g"][l]), kv_lat_g=row(small["kv_lat_g"][l]),
        qk_tiles=_norm_tiles(small["q_norm_g"][l]) + _norm_tiles(small["k_norm_g"][l]),
        glu_b=row(small["glu_b"][l]), dw_b=row(small["dw_b"][l]), conv_ln_g=row(small["conv_ln_g"][l]),
        conv_ln_b=row(small["conv_ln_b"][l]), b_pw=row(small["b_pw"][l]))


def kernel(x, c, positions, ada_w, ada_b, norm_g, w_in, q_lat_g, w_q_up, kv_lat_g, w_kv_up, q_norm_g, k_norm_g, glu_b, dw_w, dw_b, conv_ln_g, conv_ln_b, w_pw, b_pw, w_out, loss_target, m_ada_w, m_ada_b, m_norm_g, m_w_in, m_q_lat_g, m_w_q_up, m_kv_lat_g, m_w_kv_up, m_q_norm_g, m_k_norm_g, m_glu_b, m_dw_w, m_dw_b, m_conv_ln_g, m_conv_ln_b, m_w_pw, m_b_pw, m_w_out, v_ada_w, v_ada_b, v_norm_g, v_w_in, v_q_lat_g, v_w_q_up, v_kv_lat_g, v_w_kv_up, v_q_norm_g, v_k_norm_g, v_glu_b, v_dw_w, v_dw_b, v_conv_ln_g, v_conv_ln_b, v_w_pw, v_b_pw, v_w_out):
    names = ("ada_w", "ada_b", "norm_g", "w_in", "q_lat_g", "w_q_up", "kv_lat_g", "w_kv_up", "q_norm_g",
             "k_norm_g", "glu_b", "dw_w", "dw_b", "conv_ln_g", "conv_ln_b", "w_pw", "b_pw", "w_out")
    w_loc = dict(zip(names, (ada_w, ada_b, norm_g, w_in, q_lat_g, w_q_up, kv_lat_g, w_kv_up, q_norm_g, k_norm_g,
                             glu_b, dw_w, dw_b, conv_ln_g, conv_ln_b, w_pw, b_pw, w_out)))
    m_loc = dict(zip(names, (m_ada_w, m_ada_b, m_norm_g, m_w_in, m_q_lat_g, m_w_q_up, m_kv_lat_g, m_w_kv_up,
                             m_q_norm_g, m_k_norm_g, m_glu_b, m_dw_w, m_dw_b, m_conv_ln_g, m_conv_ln_b, m_w_pw,
                             m_b_pw, m_w_out)))
    v_loc = dict(zip(names, (v_ada_w, v_ada_b, v_norm_g, v_w_in, v_q_lat_g, v_w_q_up, v_kv_lat_g, v_w_kv_up,
                             v_q_norm_g, v_k_norm_g, v_glu_b, v_dw_w, v_dw_b, v_conv_ln_g, v_conv_ln_b, v_w_pw,
                             v_b_pw, v_w_out)))
    nl, d = N_LAYERS, D_MODEL
    me = 4 * lax.axis_index("x") + 2 * lax.axis_index("y") + lax.axis_index("c")
    x2, tgt = x[0], loss_target[0]
    ada_cols = ada_w.shape[-1]

    tr = lambda a: jnp.swapaxes(a, 1, 2)
    w_loc, m_loc, v_loc = ({**dd, "w_in": tr(dd["w_in"])} for dd in (w_loc, m_loc, v_loc))
    wire = {k: w_loc[k].astype(WIRE_DTYPE) for k in _BIG}
    dw_pad = jnp.pad(dw_w, ((0, 0), (0, HALO - CONV_K), (0, 0)))
    c_all, dw_all = _all_gather([c.reshape(d // LANE, LANE), dw_pad], name="gather_c")
    c_all = c_all.reshape(N_DEV, d)
    ada_b_cols = lax.dynamic_slice_in_dim(ada_b, me * ada_cols, ada_cols, axis=1).reshape(nl, 1, ada_cols)
    mod_cols = _ada_fwd(c_all, ada_w, ada_b_cols, name="ada_fwd")
    mod_all = _all_gather([mod_cols], name="gather_mod")[0]
    mod_me = lax.dynamic_index_in_dim(mod_all, me, axis=2, keepdims=False)
    mod = mod_me.transpose(1, 0, 2).reshape(nl, 1, N_DEV * ada_cols)

    w_in0 = [wire["w_in"][0]]
    fly_c = _exchange_start(w_in0, _own_slots(w_in0, False, name="own_w_in_l0", after=mod), "chips",
                            name="gather_start_w_in_l0")
    fly_r0, fly_w1 = {}, {}

    def early_l0(h):
        from_chips = _exchange_wait(*fly_c[:4], h, "chips", name="gather_wait_w_in_l0")
        fly_f = _exchange_start([], from_chips, "forward", name="forward_start_w_in_l0")
        w_in_all0 = _exchange_wait(*fly_f[:4], fly_f[4], "forward", name="forward_wait_w_in_l0")[0]
        rest0 = [wire[k][0] for k in _BIG[1:]]
        fly_r0["x"] = _exchange_start(rest0, _own_slots(rest0, False, name="own_weights_l0_rest", after=w_in_all0),
                                      "gather", name="gather_start_l0_rest")
        return dict(w_in=_win_assemble(w_in_all0, name="w_in_assemble_l0"), in_proj_after=fly_r0["x"][4])

    def layout_rest(parts):
        return dict(w_q_up=_qup_permute(_shards_to_cols(parts[0])), w_kv_up=_shards_to_cols(parts[1]),
                    w_pw=parts[2].reshape(D_CONV, D_CONV), w_out=parts[3].reshape(D_MLA + D_CONV, d))

    small_in = dict(norm_g=norm_g, q_lat_g=q_lat_g, kv_lat_g=kv_lat_g, q_norm_g=q_norm_g, k_norm_g=k_norm_g,
                    glu_b=glu_b, dw_b=dw_b, conv_ln_g=conv_ln_g, conv_ln_b=conv_ln_b, b_pw=b_pw)
    dw_full = [_shards_to_cols(dw_all[:, l])[:CONV_K] for l in range(nl)]
    rope = _rope_tiles(positions[0])

    def layer_params(l, w_in_all, rest, mod_l):
        full = dict(dw_w=dw_full)
        if w_in_all is not None:
            full["w_in"] = {l: _win_assemble(w_in_all, name=f"w_in_assemble_l{l}")}
        if rest is not None:
            full.update({k: {l: a} for k, a in layout_rest(rest).items()})
        return _layer_params(l, full, mod_l, small_in)

    def late_l0(z):
        parts = _exchange_wait(*fly_r0["x"][:4], z, "gather", name="gather_wait_l0_rest")
        src1 = [wire[k][1] for k in _BIG]
        fly_w1["x"] = _exchange_start(src1, _own_slots(src1, False, name="own_weights_l1", after=parts[0]), "gather",
                                      name="gather_start_l1")
        late = layout_rest(parts)
        late["q_lat_g"] = small_in["q_lat_g"][0].reshape(1, -1) + fly_w1["x"][4][0, 0]
        return late

    params, saved = [None] * nl, [None] * nl
    p0 = layer_params(0, None, None, mod[0] + fly_c[4][0, 0])
    xs, saved[0], params[0] = _layer_fwd(x2, p0, rope, 0, early=early_l0, late=late_l0)
    parts1 = _exchange_wait(*fly_w1["x"][:4], xs, "gather", name="gather_wait_l1")
    params[1] = layer_params(1, parts1[0], parts1[1:], mod[1])
    xs, saved[1], _ = _layer_fwd(xs, params[1], rope, 1)
    gx, loss_part = _loss_head(xs, tgt, name="loss_head")
    loss = lax.psum(loss_part[0, 0], ("x", "y", "c"))

    def shard_major(k, g):
        if k == "w_q_up":
            g = _qup_unpermute(g)
        if k in _COL_SHARDED:
            return _cols_to_shards(g)
        return g.reshape((N_DEV, g.shape[0] // N_DEV, g.shape[1]))

    def scatter_start(send, tag):
        lands = _own_slots(send, True, name=f"own_grads_{tag}")
        return _exchange_start(send, lands, "scatter", name=f"scatter_start_{tag}")

    def wire_rest(big):
        return [shard_major(k, big[k]).astype(WIRE_DTYPE) for k in _BIG[1:]]

    big_g, small_g, flying = [None] * nl, [None] * nl, {}
    gx, big_g[1], small_g[1] = _layer_bwd(gx, params[1], saved[1], rope, 1)
    flying["l1"] = scatter_start([_win_split(big_g[1]["w_in"], name="w_in_split_l1")] + wire_rest(big_g[1]), "l1")
    p0 = dict(params[0])
    p0["gate"] = p0["gate"] + flying["l1"][4][0, 0]

    def start_rest_l0(big):
        flying["l0_rest"] = scatter_start(wire_rest(big), "l0_rest")
        return flying["l0_rest"][4]

    def start_w_in_l0(g_w_in):
        flying["l0_w_in"] = scatter_start([_win_split(g_w_in, name="w_in_split_l0")], "l0_w_in")
        return flying["l0_w_in"][4]

    gx, big_g[0], small_g[0] = _layer_bwd(gx, p0, saved[0], rope, 0, hook_rest=start_rest_l0,
                                          hook_w_in=start_w_in_l0)

    tile = 8 * LANE
    padded = [(k, nn, -(-nn // tile) * tile) for k, nn in _SMALL]
    spk = jnp.concatenate([jnp.pad(small_g[l][k].reshape(-1), (0, np_ - nn)).reshape(-1, LANE)
                           for l in range(nl) for k, nn, np_ in padded], axis=0)
    s_all = _all_gather([spk], name="gather_small_grads")[0]
    s_rows = sum(np_ for _, _, np_ in padded) // LANE
    s_all = s_all.reshape(N_DEV, nl, s_rows, LANE)
    s_parts = {k: a[..., :nn] for (k, nn, _), a in
               zip(padded, _unpack_rows(s_all, [(np_,) for _, _, np_ in padded]))}

    dmod_all = s_parts["dmod"]
    dmod_cols = lax.dynamic_slice_in_dim(dmod_all, me * ada_cols, ada_cols, axis=2).transpose(1, 0, 2)
    g_ada_w = _ada_bwd(c_all.T, dmod_cols, name="ada_bwd")
    gp = {}
    gp["ada_w"] = g_ada_w[None]
    gp["ada_b"] = dmod_all
    for k in ("norm_g", "q_lat_g", "kv_lat_g", "glu_b", "dw_b", "conv_ln_g", "conv_ln_b", "b_pw"):
        gp[k] = s_parts[k]
    for k in ("q_norm_g", "k_norm_g"):
        t = s_parts[k]
        gp[k] = jnp.concatenate([t[..., :NOPE], t[..., LANE:LANE + ROPE]], axis=-1)
    dw_g = s_parts["dw_w"].reshape(N_DEV, nl, HALO, D_CONV)[:, :, :CONV_K]
    gp["dw_w"] = lax.dynamic_slice_in_dim(dw_g, me * LANE, LANE, axis=3)

    res = {"ada_w": _adamw(gp["ada_w"], w_loc["ada_w"], m_loc["ada_w"], v_loc["ada_w"], name="adamw_ada_w")}
    small_names = [k for k in names if k not in _BIG and k != "ada_w"]
    sizes = [w_loc[k].size for k in small_names]
    rows = [-(-sz // tile) * (tile // LANE) for sz in sizes]

    def pack(arrs, lead):
        return jnp.concatenate([jnp.pad(a.reshape(lead + (sz,)), [(0, 0)] * len(lead) + [(0, r * LANE - sz)])
                                .reshape(lead + (r, LANE)) for a, sz, r in zip(arrs, sizes, rows)], axis=len(lead))

    packed = _adamw(pack([gp[k] for k in small_names], (N_DEV,)), *(pack([dd[k] for k in small_names], ())
                                                                    for dd in (w_loc, m_loc, v_loc)),
                    name="adamw_small")
    r0 = 0
    for k, sz, r in zip(small_names, sizes, rows):
        res[k] = tuple(o[r0:r0 + r].reshape(-1)[:sz].reshape(w_loc[k].shape) for o in packed)
        r0 += r
    arrived = [None] * nl
    arrived[1] = _exchange_wait(*flying["l1"][:4], gx, "scatter", name="scatter_wait_l1")
    rest0 = _exchange_wait(*flying["l0_rest"][:4], gx, "scatter", name="scatter_wait_l0_rest")
    arrived[0] = _exchange_wait(*flying["l0_w_in"][:4], res["ada_w"][1], "scatter",
                                name="scatter_wait_l0_w_in") + rest0
    for i, k in enumerate(_BIG):
        res[k] = _adamw([arrived[l][i] for l in range(nl)], w_loc[k], m_loc[k], v_loc[k], name=f"adamw_{k}")
    res["w_in"] = tuple(tr(a) for a in res["w_in"])
    out = [loss, gx[None]]
    for idx in range(4):
        out += [res[k][idx] for k in names]
    return tuple(out)
```

```python
import functools
import math

import jax
import jax.numpy as jnp
from jax import lax
from jax.experimental import pallas as pl
from jax.experimental.pallas import tpu as pltpu

F32 = jnp.float32
MXU_DTYPE = jnp.bfloat16
WIRE_DTYPE = jnp.bfloat16

D_MODEL = 2048
N_LAYERS = 2
N_DEV = 8
N_HEADS = 8
NOPE = 128
ROPE = 64
V_DIM = 128
QK_DIM = NOPE + ROPE
Q_LORA = 512
KV_LORA = 256
D_MLA = N_HEADS * V_DIM
D_CONV = 1024
CONV_K = 31
ROPE_THETA = 10000.0
EPS = 1e-6
LANE = 128
HEAD_PAD = 2 * LANE
HALO = 32

SEG_CI = (0, 2 * D_CONV)
SEG_MG = (2 * D_CONV, D_MLA)
SEG_CG = (2 * D_CONV + D_MLA, D_CONV)
SEG_QL = (2 * D_CONV + D_MLA + D_CONV, Q_LORA)
SEG_KVL = (SEG_QL[0] + Q_LORA, KV_LORA)
SEG_KR = (SEG_KVL[0] + KV_LORA, LANE)
SEG_LAT = (SEG_QL[0], 1024)
IN_PAD = SEG_LAT[0] + SEG_LAT[1]
IN_TILE = IN_PAD // 4
assert SEG_KR[0] + LANE <= IN_PAD and SEG_LAT[0] % SEG_LAT[1] == 0
IN_COLS = Q_LORA + KV_LORA + ROPE + D_MLA + 2 * D_CONV + D_CONV

ADAM_LR = 0.001
ADAM_B1 = 0.9
ADAM_B2 = 0.999
ADAM_EPS = 1e-08
ADAM_WD = 0.01
ADAM_STEP = 10

VMEM_LIMIT = 56 * 1024 * 1024
ATT_T = 512
ROW_T = 256
CONV_T = 128
MESH_ID = pl.DeviceIdType.MESH


def _cp(sem=None):
    kw = dict(vmem_limit_bytes=VMEM_LIMIT)
    if sem is not None:
        kw["dimension_semantics"] = sem
    return pltpu.CompilerParams(**kw)


def _sds(shape, dtype):
    return jax.ShapeDtypeStruct(shape, dtype)


def _silu(x):
    return x * jax.nn.sigmoid(x)


def _dsilu(x):
    s = jax.nn.sigmoid(x)
    return s * (1.0 + x * (1.0 - s))


def _rowspec(t, width, col=0):
    return pl.BlockSpec((t, width), lambda i: (i, col))


def _vecspec(width):
    return pl.BlockSpec((1, width), lambda i: (0, 0))


def _colsum(v):
    return jnp.sum(v, axis=0, keepdims=True)


def _mm(a, b, *, name, ta=False, tb=False, out_dtype=F32, tm=512, tn=512, tk=None, n_outer=False, after=None,
        residual=None):
    if ta:
        kdim, m = a.shape
    else:
        m, kdim = a.shape
    if tb:
        n, k2 = b.shape
    else:
        k2, n = b.shape
    assert kdim == k2, (a.shape, b.shape)
    tm, tn = min(tm, m), min(tn, n)
    tk = kdim if tk is None else min(tk, kdim)
    assert m % tm == 0 and n % tn == 0 and kdim % tk == 0, (m, n, kdim, tm, tn, tk)
    nk = kdim // tk
    dims = (((0 if ta else 1,), (1 if tb else 0,)), ((), ()))

    n_extra = 0 if after is None else 1
    assert residual is None or nk == 1

    def body(a_ref, b_ref, *rest):
        if residual is not None:
            x_ref, gate_ref = rest[:2]
            rest = rest[2:]
        o_ref, scratch = rest[n_extra], rest[n_extra + 1:]
        prod = lax.dot_general(a_ref[...].astype(MXU_DTYPE), b_ref[...].astype(MXU_DTYPE), dims,
                               preferred_element_type=F32)
        if residual is not None:
            o_ref[...] = prod.astype(o_ref.dtype)
            scratch[0][...] = x_ref[...] + gate_ref[...] * prod
        elif nk == 1:
            o_ref[...] = prod.astype(o_ref.dtype)
        else:
            acc = scratch[0]
            k = pl.program_id(2)

            @pl.when(k == 0)
            def _():
                acc[...] = prod

            @pl.when(k > 0)
            def _():
                acc[...] += prod

            @pl.when(k == nk - 1)
            def _():
                o_ref[...] = acc[...].astype(o_ref.dtype)

    if n_outer:
        ij = lambda g0, g1: (g1, g0)
        grid = (n // tn, m // tm, nk)
    else:
        ij = lambda g0, g1: (g0, g1)
        grid = (m // tm, n // tn, nk)

    def a_map(g0, g1, k):
        i, _ = ij(g0, g1)
        return (k, i) if ta else (i, k)

    def b_map(g0, g1, k):
        _, j = ij(g0, g1)
        return (j, k) if tb else (k, j)

    def o_map(g0, g1, k):
        return ij(g0, g1)

    in_specs = [pl.BlockSpec((tk, tm) if ta else (tm, tk), a_map), pl.BlockSpec((tn, tk) if tb else (tk, tn), b_map)]
    operands = [a, b]
    out_specs, out_shape = pl.BlockSpec((tm, tn), o_map), _sds((m, n), out_dtype)
    if residual is not None:
        in_specs += [pl.BlockSpec((tm, tn), o_map), pl.BlockSpec((1, tn), lambda g0, g1, k: (0, ij(g0, g1)[1]))]
        operands += list(residual)
        out_specs, out_shape = [out_specs, pl.BlockSpec((tm, tn), o_map)], [out_shape, _sds((m, n), F32)]
    if after is not None:
        in_specs.append(_ANY)
        operands.append(after)
    return pl.pallas_call(
        body, name=name, grid=grid, in_specs=in_specs, out_specs=out_specs, out_shape=out_shape,
        scratch_shapes=[pltpu.VMEM((tm, tn), F32)] if nk > 1 else [],
        compiler_params=_cp(("parallel", "parallel", "arbitrary")),
    )(*operands)


def _prenorm(x, g, shift, sc1p, *, name):
    s, d = x.shape
    t = min(ROW_T, s)

    def body(x_ref, g_ref, sh_ref, sc_ref, h_ref):
        xv = x_ref[...]
        r = lax.rsqrt(jnp.mean(xv * xv, axis=-1, keepdims=True) + EPS)
        h_ref[...] = ((xv * r) * g_ref[...] * sc_ref[...] + sh_ref[...]).astype(h_ref.dtype)

    return pl.pallas_call(
        body, name=name, grid=(s // t,),
        in_specs=[_rowspec(t, d), _vecspec(d), _vecspec(d), _vecspec(d)],
        out_specs=_rowspec(t, d), out_shape=_sds((s, d), MXU_DTYPE),
        compiler_params=_cp(("parallel",)),
    )(x, g, shift, sc1p)


def _lat_norm(z, g_ql, g_kvl, *, name):
    s = z.shape[0]
    t = min(ROW_T, s)

    def body(ql_ref, kvl_ref, gq_ref, gk_ref, qn_ref, kn_ref):
        for src, g_ref, dst in ((ql_ref, gq_ref, qn_ref), (kvl_ref, gk_ref, kn_ref)):
            v = src[...]
            r = lax.rsqrt(jnp.mean(v * v, axis=-1, keepdims=True) + EPS)
            dst[...] = ((v * r) * g_ref[...]).astype(dst.dtype)

    return pl.pallas_call(
        body, name=name, grid=(s // t,),
        in_specs=[_rowspec(t, Q_LORA, SEG_QL[0] // Q_LORA), _rowspec(t, KV_LORA, SEG_KVL[0] // KV_LORA),
                  _vecspec(Q_LORA), _vecspec(KV_LORA)],
        out_specs=[_rowspec(t, Q_LORA), _rowspec(t, KV_LORA)],
        out_shape=[_sds((s, Q_LORA), MXU_DTYPE), _sds((s, KV_LORA), MXU_DTYPE)],
        compiler_params=_cp(("parallel",)),
    )(z, z, g_ql, g_kvl)


def _rope_fwd(r, c_t, s1_t, s2_t):
    return r * c_t + pltpu.roll(r, LANE - ROPE // 2, 1) * s1_t + pltpu.roll(r, ROPE // 2, 1) * s2_t


def _rope_bwd(d, c_t, s1_t, s2_t):
    return d * c_t + pltpu.roll(d * s1_t, ROPE // 2, 1) + pltpu.roll(d * s2_t, LANE - ROPE // 2, 1)


def _lanesum(v):
    return jnp.sum(v, axis=-1, keepdims=True)


def _qk_prep(q_raw, kv, z, c_t, s1_t, s2_t, gqn, gqr, gkn, gkr, *, name):
    s = q_raw.shape[0]
    t = min(ROW_T, s)
    scale = 1.0 / math.sqrt(QK_DIM)

    def body(q_ref, kv_ref, kr_ref, c_ref, s1_ref, s2_ref, gqn_ref, gqr_ref, gkn_ref, gkr_ref,
             qf_ref, kf_ref, vf_ref):
        c_v, s1_v, s2_v = c_ref[...], s1_ref[...], s2_ref[...]
        kr = kr_ref[...]
        kr_ss = _lanesum(kr * kr)
        for h in range(N_HEADS):
            n = q_ref[:, h * LANE:(h + 1) * LANE]
            r = q_ref[:, N_HEADS * LANE + h * LANE:N_HEADS * LANE + (h + 1) * LANE]
            rs = lax.rsqrt((_lanesum(n * n) + _lanesum(r * r)) * (1.0 / QK_DIM) + EPS)
            qf_ref[h, :, 0:LANE] = (((n * rs) * gqn_ref[...]) * scale).astype(qf_ref.dtype)
            rr = _rope_fwd((r * rs) * gqr_ref[...], c_v, s1_v, s2_v)
            qf_ref[h, :, LANE:HEAD_PAD] = (rr * scale).astype(qf_ref.dtype)

            n = kv_ref[:, h * 2 * LANE:h * 2 * LANE + LANE]
            rs = lax.rsqrt((_lanesum(n * n) + kr_ss) * (1.0 / QK_DIM) + EPS)
            kf_ref[h, :, 0:LANE] = ((n * rs) * gkn_ref[...]).astype(kf_ref.dtype)
            kf_ref[h, :, LANE:HEAD_PAD] = _rope_fwd((kr * rs) * gkr_ref[...], c_v, s1_v, s2_v).astype(kf_ref.dtype)
            vf_ref[h, :, 0:V_DIM] = kv_ref[:, h * 2 * LANE + LANE:(h + 1) * 2 * LANE].astype(vf_ref.dtype)
            vf_ref[h, :, V_DIM:] = jnp.ones((t, V_DIM), vf_ref.dtype)

    hspec = lambda w: pl.BlockSpec((N_HEADS, t, w), lambda i: (0, i, 0))
    return pl.pallas_call(
        body, name=name, grid=(s // t,),
        in_specs=[_rowspec(t, 2 * N_HEADS * LANE), _rowspec(t, 2 * N_HEADS * LANE),
                  _rowspec(t, LANE, SEG_KR[0] // LANE),
                  _rowspec(t, LANE), _rowspec(t, LANE), _rowspec(t, LANE),
                  _vecspec(LANE), _vecspec(LANE), _vecspec(LANE), _vecspec(LANE)],
        out_specs=[hspec(HEAD_PAD), hspec(HEAD_PAD), hspec(2 * V_DIM)],
        out_shape=[_sds((N_HEADS, s, HEAD_PAD), MXU_DTYPE), _sds((N_HEADS, s, HEAD_PAD), MXU_DTYPE),
                   _sds((N_HEADS, s, 2 * V_DIM), MXU_DTYPE)],
        compiler_params=_cp(("parallel",)),
    )(q_raw, kv, z, c_t, s1_t, s2_t, gqn, gqr, gkn, gkr)


def _causal_mask(t):
    row = lax.broadcasted_iota(jnp.int32, (t, t), 0)
    col = lax.broadcasted_iota(jnp.int32, (t, t), 1)
    return col <= row


NEG = -1e30


def _flash_fwd(qf, kf, va, *, name):
    nh, s, dk = qf.shape
    dv = va.shape[-1] // 2
    t = min(ATT_T, s)
    n = s // t
    assert dv == LANE and t % LANE == 0

    def body(q_ref, k_ref, v_ref, o_ref, lse_ref, m_s, acc_s, s_buf):
        i = pl.program_id(1)
        m_s[...] = jnp.full(m_s.shape, NEG, F32)
        acc_s[...] = jnp.zeros(acc_s.shape, F32)

        def rows_of(j):
            return pl.ds(pl.multiple_of(j * t, t), t)

        def scores(qi, j):
            return lax.dot_general(q_ref[0, rows_of(qi), :], k_ref[0, rows_of(j), :], (((1,), (1,)), ((), ())),
                                   preferred_element_type=F32)

        def consume(j, slot, masked):
            sc = s_buf[slot]
            if masked:
                sc = jnp.where(_causal_mask(t), sc, NEG)
            m_prev = m_s[...]
            m_new = jnp.maximum(m_prev, jnp.max(sc, axis=-1, keepdims=True))
            alpha = jnp.exp(m_prev - m_new)
            p = jnp.exp(sc - jnp.tile(m_new, (1, t // LANE)))
            acc_s[...] = jnp.tile(alpha, (1, 2)) * acc_s[...] + jnp.dot(
                p.astype(MXU_DTYPE), v_ref[0, rows_of(j), :], preferred_element_type=F32)
            m_s[...] = m_new

        nxt = jnp.minimum(i + 1, n - 1)

        @pl.when(i == 0)
        def _():
            s_buf[2] = scores(0, 0)
            consume(0, 2, True)
            s_buf[2] = scores(nxt, 0)

        @pl.when(i > 0)
        def _():
            s_buf[1] = scores(i, 1)
            consume(0, 2, False)

            def pair(a, carry):
                s_buf[0] = scores(i, 2 * a + 2)
                consume(2 * a + 1, 1, False)
                s_buf[1] = scores(i, 2 * a + 3)
                consume(2 * a + 2, 0, False)
                return carry

            lax.fori_loop(0, (i - 1) // 2, pair, 0)

            @pl.when(i % 2 == 1)
            def _():
                s_buf[2] = scores(nxt, 0)
                consume(i, 1, True)

            @pl.when(i % 2 == 0)
            def _():
                s_buf[0] = scores(i, i)
                consume(i - 1, 1, False)
                s_buf[2] = scores(nxt, 0)
                consume(i, 0, True)

        den = acc_s[:, dv:]
        o_ref[...] = acc_s[:, :dv] / den
        lse_ref[0] = m_s[...] + jnp.log(den)

    head = lambda h, i: (h, 0, 0)
    return pl.pallas_call(
        body, name=name, grid=(nh, n),
        in_specs=[pl.BlockSpec((1, s, dk), head), pl.BlockSpec((1, s, dk), head), pl.BlockSpec((1, s, 2 * dv), head)],
        out_specs=[pl.BlockSpec((t, dv), lambda h, i: (i, h)),
                   pl.BlockSpec((1, t, LANE), lambda h, i: (h, i, 0))],
        out_shape=[_sds((s, nh * dv), F32), _sds((nh, s, LANE), F32)],
        scratch_shapes=[pltpu.VMEM((t, LANE), F32), pltpu.VMEM((t, 2 * dv), F32), pltpu.VMEM((3, t, t), F32)],
        compiler_params=_cp(("arbitrary", "arbitrary")),
    )(qf, kf, va)


def _shifted_copies(ext_ref):
    rows = ext_ref.shape[1] - 8
    for s in range(1, 8):
        ext_ref[s, 0:rows, :] = ext_ref[0, s:s + rows, :]


def _windows(ext_ref, offsets, t_rows, lane0, lanes):
    for s in range(8):
        group = [o for o in offsets if o % 8 == s]
        if not group:
            continue
        lo, hi = min(group) - s, max(group) - s
        wide = ext_ref[s, pl.ds(lo, hi - lo + t_rows), lane0:lane0 + lanes]
        for o in group:
            yield o, wide[o - s - lo:o - s - lo + t_rows]


def _dw_taps(ext_ref, w_ref, row0, t_rows, lane0, lanes, first_off):
    acc = None
    for off, win in _windows(ext_ref, [row0 + first_off + k for k in range(CONV_K)], t_rows, lane0, lanes):
        k = off - row0 - first_off
        term = w_ref[k:k + 1, lane0:lane0 + lanes] * win
        acc = term if acc is None else acc + term
    return acc


CONV_RC = 32
CONV_LC = 256


def _conv_fwd(z, glu_b, dw_w, dw_b, ln_g, ln_b, *, name):
    s = z.shape[0]
    t = min(CONV_T, s)
    c2 = 2 * D_CONV
    hb = t // HALO

    def body(zm_ref, zh_ref, gb_ref, w_ref, wb_ref, g_ref, b_ref, u1_ref, u3_ref, ext):
        i = pl.program_id(0)

        def glu(zv):
            ci = zv + gb_ref[...]
            return ci[:, :D_CONV] * jax.nn.sigmoid(ci[:, D_CONV:])

        ext[0, HALO:, :] = glu(zm_ref[...])
        ext[0, 0:HALO, :] = jnp.where(i > 0, glu(zh_ref[...]), 0.0)
        _shifted_copies(ext)
        for rc in range(0, t, CONV_RC):
            for lc in range(0, D_CONV, CONV_LC):
                acc = _dw_taps(ext, w_ref, rc, CONV_RC, lc, CONV_LC, HALO - (CONV_K - 1))
                u1_ref[rc:rc + CONV_RC, lc:lc + CONV_LC] = acc + wb_ref[:, lc:lc + CONV_LC]
        u1 = u1_ref[...]
        mu = jnp.mean(u1, axis=-1, keepdims=True)
        cen = u1 - mu
        var = jnp.mean(cen * cen, axis=-1, keepdims=True)
        u2 = (cen * lax.rsqrt(var + EPS)) * g_ref[...] + b_ref[...]
        u3_ref[...] = _silu(u2).astype(u3_ref.dtype)

    return pl.pallas_call(
        body, name=name, grid=(s // t,),
        in_specs=[_rowspec(t, c2), pl.BlockSpec((HALO, c2), lambda i: (jnp.maximum(i * hb - 1, 0), 0)),
                  _vecspec(c2), pl.BlockSpec((HALO, D_CONV), lambda i: (0, 0)), _vecspec(D_CONV),
                  _vecspec(D_CONV), _vecspec(D_CONV)],
        out_specs=[_rowspec(t, D_CONV), _rowspec(t, D_CONV)],
        out_shape=[_sds((s, D_CONV), F32), _sds((s, D_CONV), MXU_DTYPE)],
        scratch_shapes=[pltpu.VMEM((8, t + HALO, D_CONV), F32)],
        compiler_params=_cp(("parallel",)),
    )(z, z, glu_b, dw_w, dw_b, ln_g, ln_b)


def _gate_cat(o, z, u4m, b_pw, *, name):
    s = o.shape[0]
    t = min(ROW_T, s)

    def body(o_ref, mg_ref, u4_ref, cg_ref, b_ref, cat_ref):
        cat_ref[:, :D_MLA] = (o_ref[...] * _silu(mg_ref[...])).astype(cat_ref.dtype)
        cat_ref[:, D_MLA:] = ((u4_ref[...] + b_ref[...]) * _silu(cg_ref[...])).astype(cat_ref.dtype)

    return pl.pallas_call(
        body, name=name, grid=(s // t,),
        in_specs=[_rowspec(t, D_MLA), _rowspec(t, D_MLA, SEG_MG[0] // D_MLA), _rowspec(t, D_CONV),
                  _rowspec(t, D_CONV, SEG_CG[0] // D_CONV), _vecspec(D_CONV)],
        out_specs=_rowspec(t, D_MLA + D_CONV), out_shape=_sds((s, D_MLA + D_CONV), MXU_DTYPE),
        compiler_params=_cp(("parallel",)),
    )(o, z, u4m, z, b_pw)


def _loss_head(xf, target, *, name):
    s, d = xf.shape
    t = min(ROW_T, s)

    def body(x_ref, t_ref, gx_ref, loss_ref):
        @pl.when(pl.program_id(0) == 0)
        def _():
            loss_ref[...] = jnp.zeros(loss_ref.shape, F32)

        err = x_ref[...] - t_ref[...]
        gx_ref[...] = err * (1.0 / d)
        loss_ref[...] += 0.5 * jnp.sum(_lanesum(err * err) * (1.0 / d), axis=0, keepdims=True)

    return pl.pallas_call(
        body, name=name, grid=(s // t,),
        in_specs=[_rowspec(t, d), _rowspec(t, d)],
        out_specs=[_rowspec(t, d), pl.BlockSpec((1, 1), lambda i: (0, 0))],
        out_shape=[_sds((s, d), F32), _sds((1, 1), F32)],
        compiler_params=_cp(("arbitrary",)),
    )(xf, target)


def _acc_init(refs):
    @pl.when(pl.program_id(0) == 0)
    def _():
        for r in refs:
            r[...] = jnp.zeros(r.shape, r.dtype)


def _out_bwd(gxo, y, gate, *, name):
    s, d = gxo.shape
    t = min(ROW_T, s)

    def body(g_ref, y_ref, gate_ref, dy_ref, dgate_ref):
        _acc_init([dgate_ref])
        gv = g_ref[...]
        dy_ref[...] = (gv * gate_ref[...]).astype(dy_ref.dtype)
        dgate_ref[...] += _colsum(gv * y_ref[...])

    return pl.pallas_call(
        body, name=name, grid=(s // t,),
        in_specs=[_rowspec(t, d), _rowspec(t, d), _vecspec(d)],
        out_specs=[_rowspec(t, d), _vecspec(d)],
        out_shape=[_sds((s, d), MXU_DTYPE), _sds((1, d), F32)],
        compiler_params=_cp(("arbitrary",)),
    )(gxo, y, gate)


def _gate_bwd(dy, w_out, o, z, u4m, b_pw, *, name):
    s, d = dy.shape
    t = min(2 * ROW_T, s)
    gates = D_MLA + D_CONV
    assert SEG_CG[0] == SEG_MG[0] + D_MLA and SEG_MG[0] % gates == 0

    def body(dy_ref, w_ref, o_ref, mg_ref, u4_ref, cg_ref, b_ref,
             do_ref, delta_ref, du4_ref, gb_ref, dz_ref):
        _acc_init([gb_ref])
        dcat = lax.dot_general(dy_ref[...], w_ref[...], (((1,), (1,)), ((), ())), preferred_element_type=F32)
        dm, ov, mg = dcat[:, :D_MLA], o_ref[...], mg_ref[...]
        do = dm * _silu(mg)
        do_ref[...] = do.astype(do_ref.dtype)
        dz_ref[:, :D_MLA] = (dm * ov * _dsilu(mg)).astype(dz_ref.dtype)
        prod = do * ov
        for h in range(N_HEADS):
            delta_ref[h] = _lanesum(prod[:, h * V_DIM:(h + 1) * V_DIM])
        dc, cg = dcat[:, D_MLA:], cg_ref[...]
        du4 = dc * _silu(cg)
        du4_ref[...] = du4.astype(du4_ref.dtype)
        dz_ref[:, D_MLA:] = (dc * (u4_ref[...] + b_ref[...]) * _dsilu(cg)).astype(dz_ref.dtype)
        gb_ref[...] += _colsum(du4)

    return pl.pallas_call(
        body, name=name, grid=(s // t,),
        in_specs=[_rowspec(t, d), pl.BlockSpec((gates, d), lambda i: (0, 0)), _rowspec(t, D_MLA),
                  _rowspec(t, D_MLA, SEG_MG[0] // D_MLA), _rowspec(t, D_CONV),
                  _rowspec(t, D_CONV, SEG_CG[0] // D_CONV), _vecspec(D_CONV)],
        out_specs=[_rowspec(t, D_MLA), pl.BlockSpec((N_HEADS, t, 1), lambda i: (0, i, 0)),
                   _rowspec(t, D_CONV), _vecspec(D_CONV), _rowspec(t, gates, SEG_MG[0] // gates)],
        out_shape=[_sds((s, D_MLA), MXU_DTYPE), _sds((N_HEADS, s, 1), F32),
                   _sds((s, D_CONV), MXU_DTYPE), _sds((1, D_CONV), F32), _sds((s, IN_PAD), MXU_DTYPE)],
        compiler_params=_cp(("arbitrary",)),
    )(dy, w_out, o, z, u4m, z, b_pw)


def _conv_bwd(du3, u1, z, dz, glu_b, dw_w, ln_g, ln_b, *, name):
    s = z.shape[0]
    t = min(CONV_T, s)
    c2 = 2 * D_CONV
    hb = t // HALO
    n_blk = s // t
    last_halo = s // HALO - 1

    def body(d3m_ref, d3h_ref, u1m_ref, u1h_ref, zm_ref, zh_ref, gb_ref, w_ref, g_ref, b_ref, dz_in_ref,
             dci_ref, gg_ref, gbn_ref, gwb_ref, ggb_ref, gw_ref, dext, uext, du0_s, gw_acc):
        i = pl.program_id(0)
        _acc_init([gg_ref, gbn_ref, gwb_ref, ggb_ref, gw_acc])

        def ln_bwd(d3, u1v):
            mu = jnp.mean(u1v, axis=-1, keepdims=True)
            cen = u1v - mu
            rstd = lax.rsqrt(jnp.mean(cen * cen, axis=-1, keepdims=True) + EPS)
            uh = cen * rstd
            d2 = d3 * _dsilu(uh * g_ref[...] + b_ref[...])
            dh = d2 * g_ref[...]
            d1 = rstd * (dh - jnp.mean(dh, axis=-1, keepdims=True) - uh * jnp.mean(dh * uh, axis=-1, keepdims=True))
            return d1, d2, uh

        d1, d2, uh = ln_bwd(d3m_ref[...], u1m_ref[...])
        gg_ref[...] += _colsum(d2 * uh)
        gbn_ref[...] += _colsum(d2)
        gwb_ref[...] += _colsum(d1)
        dext[0, 0:t, :] = d1
        d1h, _, _ = ln_bwd(d3h_ref[...], u1h_ref[...])
        dext[0, t:, :] = jnp.where(i < n_blk - 1, d1h, 0.0)
        _shifted_copies(dext)

        def glu_parts(zv):
            ci = zv + gb_ref[...]
            return ci[:, :D_CONV], jax.nn.sigmoid(ci[:, D_CONV:])

        val, sg = glu_parts(zm_ref[...])
        uext[0, HALO:, :] = val * sg
        valh, sgh = glu_parts(zh_ref[...])
        uext[0, 0:HALO, :] = jnp.where(i > 0, valh * sgh, 0.0)
        _shifted_copies(uext)

        for rc in range(0, t, CONV_RC):
            for lc in range(0, D_CONV, CONV_LC):
                acc = None
                for off, win in _windows(dext, [rc + k for k in range(CONV_K)], CONV_RC, lc, CONV_LC):
                    k = (CONV_K - 1) - (off - rc)
                    term = w_ref[k:k + 1, lc:lc + CONV_LC] * win
                    acc = term if acc is None else acc + term
                du0_s[rc:rc + CONV_RC, lc:lc + CONV_LC] = acc
                dchunk = dext[0, rc:rc + CONV_RC, lc:lc + CONV_LC]
                first = rc + HALO - (CONV_K - 1)
                for off, win in _windows(uext, [first + k for k in range(CONV_K)], CONV_RC, lc, CONV_LC):
                    k = off - first
                    pr = dchunk * win
                    part = pr[0:8]
                    for r8 in range(8, CONV_RC, 8):
                        part = part + pr[r8:r8 + 8]
                    gw_acc[k, :, lc:lc + CONV_LC] += part

        du0 = du0_s[...]
        dval = du0 * sg
        dgt = du0 * val * sg * (1.0 - sg)
        dci_ref[:, :D_CONV] = dval.astype(dci_ref.dtype)
        dci_ref[:, D_CONV:] = dgt.astype(dci_ref.dtype)
        ggb_ref[:, :D_CONV] += _colsum(dval)
        ggb_ref[:, D_CONV:] += _colsum(dgt)

        @pl.when(i == n_blk - 1)
        def _():
            gw_ref[...] = jnp.sum(gw_acc[...], axis=1)

    halo_next = lambda w: pl.BlockSpec((HALO, w), lambda i: (jnp.minimum((i + 1) * hb, last_halo), 0))
    return pl.pallas_call(
        body, name=name, grid=(n_blk,),
        in_specs=[_rowspec(t, D_CONV), halo_next(D_CONV), _rowspec(t, D_CONV), halo_next(D_CONV),
                  _rowspec(t, c2), pl.BlockSpec((HALO, c2), lambda i: (jnp.maximum(i * hb - 1, 0), 0)),
                  _vecspec(c2), pl.BlockSpec((HALO, D_CONV), lambda i: (0, 0)), _vecspec(D_CONV), _vecspec(D_CONV),
                  _ANY],
        out_specs=[_rowspec(t, c2, SEG_CI[0] // c2), _vecspec(D_CONV), _vecspec(D_CONV), _vecspec(D_CONV),
                   _vecspec(c2), pl.BlockSpec((HALO, D_CONV), lambda i: (0, 0))],
        out_shape=[_sds(dz.shape, dz.dtype), _sds((1, D_CONV), F32), _sds((1, D_CONV), F32), _sds((1, D_CONV), F32),
                   _sds((1, c2), F32), _sds((HALO, D_CONV), F32)],
        scratch_shapes=[pltpu.VMEM((8, t + HALO, D_CONV), F32), pltpu.VMEM((8, t + HALO, D_CONV), F32),
                        pltpu.VMEM((t, D_CONV), F32), pltpu.VMEM((HALO, 8, D_CONV), F32)],
        input_output_aliases={10: 0},
        compiler_params=_cp(("arbitrary",)),
    )(du3, du3, u1, u1, z, z, glu_b, dw_w, ln_g, ln_b, dz)


def _flash_bwd(qf, kf, va, do, lse_t, delta_t, *, name):
    nh, s, dk = qf.shape
    dv = va.shape[-1] // 2
    t = min(ATT_T, s)
    n = s // t
    nt = (((1,), (1,)), ((), ()))
    tn = (((0,), (0,)), ((), ()))

    def body(q_ref, do_ref, lse_ref, dl_ref, k_ref, v_ref, dq_ref, dk_ref, dv_ref,
             dk_s, dv_s, st_buf, dpt_buf):
        n_un = pl.program_id(1)
        j = n - 1 - n_un
        nxt = jnp.maximum(j - 1, 0)

        @pl.when(n_un == 0)
        def _():
            dq_ref[...] = jnp.zeros(dq_ref.shape, F32)

        dk_s[...] = jnp.zeros(dk_s.shape, F32)
        dv_s[...] = jnp.zeros(dv_s.shape, F32)

        def rows_at(blk):
            return pl.ds(pl.multiple_of(blk * t, t), t)

        def rows_of(b):
            return rows_at(n - 1 - b)

        k = k_ref[0, rows_at(j), :]

        def produce(kj, b, slot):
            rows = rows_of(b)
            st_buf[slot] = lax.dot_general(k_ref[0, rows_at(kj), :], q_ref[0, rows, :], nt,
                                           preferred_element_type=F32)
            dpt_buf[slot] = lax.dot_general(v_ref[0, rows_at(kj), 0:dv], do_ref[rows, :], nt,
                                            preferred_element_type=F32)

        def consume(b, slot, masked):
            i = n - 1 - b
            rows = rows_of(b)
            q, dov = q_ref[0, rows, :], do_ref[rows, :]
            pt = jnp.exp(st_buf[slot] - lse_ref[0, i])
            if masked:
                key = lax.broadcasted_iota(jnp.int32, (t, t), 0)
                qry = lax.broadcasted_iota(jnp.int32, (t, t), 1)
                pt = jnp.where(key <= qry, pt, 0.0)
            dv_s[...] += jnp.dot(pt.astype(MXU_DTYPE), dov, preferred_element_type=F32)
            dst = (pt * (dpt_buf[slot] - dl_ref[0, i])).astype(MXU_DTYPE)
            dk_s[...] += jnp.dot(dst, q, preferred_element_type=F32)
            dq_ref[0, rows, :] += lax.dot_general(dst, k, tn, preferred_element_type=F32)

        @pl.when(n_un == 0)
        def _():
            produce(j, 0, 2)
            consume(0, 2, True)
            produce(nxt, 0, 2)

        @pl.when(n_un > 0)
        def _():
            produce(j, 1, 1)
            consume(0, 2, False)

            def pair(a, carry):
                produce(j, 2 * a + 2, 0)
                consume(2 * a + 1, 1, False)
                produce(j, 2 * a + 3, 1)
                consume(2 * a + 2, 0, False)
                return carry

            lax.fori_loop(0, (n_un - 1) // 2, pair, 0)

            @pl.when(n_un % 2 == 1)
            def _():
                produce(nxt, 0, 2)
                consume(n_un, 1, True)

            @pl.when(n_un % 2 == 0)
            def _():
                produce(j, n_un, 0)
                consume(n_un - 1, 1, False)
                produce(nxt, 0, 2)
                consume(n_un, 0, True)

        dk_ref[0] = dk_s[...]
        dv_ref[0] = dv_s[...]

    head = lambda h, j: (h, 0, 0)
    rowv = pl.BlockSpec((1, n, 1, t), lambda h, j: (h, 0, 0, 0))
    return pl.pallas_call(
        body, name=name, grid=(nh, n),
        in_specs=[pl.BlockSpec((1, s, dk), head),
                  pl.BlockSpec((s, dv), lambda h, j: (0, h)),
                  rowv, rowv,
                  pl.BlockSpec((1, s, dk), head),
                  pl.BlockSpec((1, s, 2 * dv), head)],
        out_specs=[pl.BlockSpec((1, s, dk), head),
                   pl.BlockSpec((1, t, dk), lambda h, g: (h, n - 1 - g, 0)),
                   pl.BlockSpec((1, t, dv), lambda h, g: (h, n - 1 - g, 0))],
        out_shape=[_sds((nh, s, dk), F32), _sds((nh, s, dk), F32), _sds((nh, s, dv), F32)],
        scratch_shapes=[pltpu.VMEM((t, dk), F32), pltpu.VMEM((t, dv), F32),
                        pltpu.VMEM((3, t, t), F32), pltpu.VMEM((3, t, t), F32)],
        compiler_params=_cp(("arbitrary", "arbitrary")),
    )(qf, do, lse_t, delta_t, kf, va)


def _qk_bwd(dqf, dkf, dvf, q_raw, kv, z, c_t, s1_t, s2_t, gqn, gqr, gkn, gkr, *, name):
    s = q_raw.shape[0]
    t = min(ROW_T, s)
    scale = 1.0 / math.sqrt(QK_DIM)

    def body(dq_ref, dk_ref, dv_ref, q_ref, kv_ref, kr_ref, c_ref, s1_ref, s2_ref,
             gqn_ref, gqr_ref, gkn_ref, gkr_ref, dqr_ref, dkv_ref, dkr_ref, ggq_ref, ggk_ref):
        _acc_init([ggq_ref, ggk_ref])
        c_v, s1_v, s2_v = c_ref[...], s1_ref[...], s2_ref[...]
        kr = kr_ref[...]
        kr_ss = _lanesum(kr * kr)
        dkr = jnp.zeros(kr.shape, F32)
        ggq_n = ggq_r = ggk_n = ggk_r = jnp.zeros((1, LANE), F32)

        def norm_bwd(n, r, rs, dyn, dyr, gn, gr):
            nh_, rh_ = n * rs, r * rs
            dnh, drh = dyn * gn, dyr * gr
            dot = (_lanesum(dnh * nh_) + _lanesum(drh * rh_)) * (1.0 / QK_DIM)
            return rs * (dnh - nh_ * dot), rs * (drh - rh_ * dot), _colsum(dyn * nh_), _colsum(dyr * rh_)

        for h in range(N_HEADS):
            n = q_ref[:, h * LANE:(h + 1) * LANE]
            r = q_ref[:, N_HEADS * LANE + h * LANE:N_HEADS * LANE + (h + 1) * LANE]
            rs = lax.rsqrt((_lanesum(n * n) + _lanesum(r * r)) * (1.0 / QK_DIM) + EPS)
            dyn = dq_ref[h, :, 0:LANE] * scale
            dyr = _rope_bwd(dq_ref[h, :, LANE:HEAD_PAD] * scale, c_v, s1_v, s2_v)
            dn, dr, g_n, g_r = norm_bwd(n, r, rs, dyn, dyr, gqn_ref[...], gqr_ref[...])
            dqr_ref[:, h * LANE:(h + 1) * LANE] = dn.astype(dqr_ref.dtype)
            dqr_ref[:, N_HEADS * LANE + h * LANE:N_HEADS * LANE + (h + 1) * LANE] = dr.astype(dqr_ref.dtype)
            ggq_n, ggq_r = ggq_n + g_n, ggq_r + g_r

            n = kv_ref[:, h * 2 * LANE:h * 2 * LANE + LANE]
            rs = lax.rsqrt((_lanesum(n * n) + kr_ss) * (1.0 / QK_DIM) + EPS)
            dyn = dk_ref[h, :, 0:LANE]
            dyr = _rope_bwd(dk_ref[h, :, LANE:HEAD_PAD], c_v, s1_v, s2_v)
            dn, dr, g_n, g_r = norm_bwd(n, kr, rs, dyn, dyr, gkn_ref[...], gkr_ref[...])
            dkv_ref[:, h * 2 * LANE:h * 2 * LANE + LANE] = dn.astype(dkv_ref.dtype)
            dkv_ref[:, h * 2 * LANE + LANE:(h + 1) * 2 * LANE] = dv_ref[h].astype(dkv_ref.dtype)
            dkr = dkr + dr
            ggk_n, ggk_r = ggk_n + g_n, ggk_r + g_r

        dkr_ref[...] = dkr.astype(dkr_ref.dtype)
        ggq_ref[:, 0:LANE] += ggq_n
        ggq_ref[:, LANE:] += ggq_r
        ggk_ref[:, 0:LANE] += ggk_n
        ggk_ref[:, LANE:] += ggk_r

    hspec = lambda w: pl.BlockSpec((N_HEADS, t, w), lambda i: (0, i, 0))
    wide = 2 * N_HEADS * LANE
    return pl.pallas_call(
        body, name=name, grid=(s // t,),
        in_specs=[hspec(HEAD_PAD), hspec(HEAD_PAD), hspec(V_DIM), _rowspec(t, wide), _rowspec(t, wide),
                  _rowspec(t, LANE, SEG_KR[0] // LANE), _rowspec(t, LANE), _rowspec(t, LANE), _rowspec(t, LANE),
                  _vecspec(LANE), _vecspec(LANE), _vecspec(LANE), _vecspec(LANE)],
        out_specs=[_rowspec(t, wide), _rowspec(t, wide), _rowspec(t, LANE), _vecspec(2 * LANE), _vecspec(2 * LANE)],
        out_shape=[_sds((s, wide), MXU_DTYPE), _sds((s, wide), MXU_DTYPE), _sds((s, LANE), MXU_DTYPE),
                   _sds((1, 2 * LANE), F32), _sds((1, 2 * LANE), F32)],
        compiler_params=_cp(("arbitrary",)),
    )(dqf, dkf, dvf, q_raw, kv, z, c_t, s1_t, s2_t, gqn, gqr, gkn, gkr)


def _lat_bwd(dqn, dkn, dkr, z, dz, g_ql, g_kvl, *, name):
    s = z.shape[0]
    t = min(ROW_T, s)
    o_ql, o_kvl, o_kr = (seg[0] - SEG_LAT[0] for seg in (SEG_QL, SEG_KVL, SEG_KR))

    def body(dq_ref, dk_ref, dkr_ref, ql_ref, kvl_ref, gq_ref, gk_ref, dz_in_ref, dz_ref, ggq_ref, ggk_ref):
        _acc_init([ggq_ref, ggk_ref])
        for d_ref, src, g_ref, off, gg_ref in ((dq_ref, ql_ref, gq_ref, o_ql, ggq_ref),
                                               (dk_ref, kvl_ref, gk_ref, o_kvl, ggk_ref)):
            v, dy = src[...], d_ref[...]
            r = lax.rsqrt(jnp.mean(v * v, axis=-1, keepdims=True) + EPS)
            vh = v * r
            dvh = dy * g_ref[...]
            dz_ref[:, off:off + v.shape[1]] = (
                r * (dvh - vh * jnp.mean(dvh * vh, axis=-1, keepdims=True))).astype(dz_ref.dtype)
            gg_ref[...] += _colsum(dy * vh)
        dz_ref[:, o_kr:o_kr + LANE] = dkr_ref[...]
        dz_ref[:, o_kr + LANE:] = jnp.zeros((t, SEG_LAT[1] - o_kr - LANE), dz_ref.dtype)

    return pl.pallas_call(
        body, name=name, grid=(s // t,),
        in_specs=[_rowspec(t, Q_LORA), _rowspec(t, KV_LORA), _rowspec(t, LANE),
                  _rowspec(t, Q_LORA, SEG_QL[0] // Q_LORA), _rowspec(t, KV_LORA, SEG_KVL[0] // KV_LORA),
                  _vecspec(Q_LORA), _vecspec(KV_LORA), _ANY],
        out_specs=[_rowspec(t, SEG_LAT[1], SEG_LAT[0] // SEG_LAT[1]), _vecspec(Q_LORA), _vecspec(KV_LORA)],
        out_shape=[_sds(dz.shape, dz.dtype), _sds((1, Q_LORA), F32), _sds((1, KV_LORA), F32)],
        input_output_aliases={7: 0},
        compiler_params=_cp(("arbitrary",)),
    )(dqn, dkn, dkr, z, z, g_ql, g_kvl, dz)


def _prenorm_bwd(dh, x, gxo, g, sc1p, *, name):
    s, d = x.shape
    t = min(ROW_T, s)

    def body(dh_ref, x_ref, gx_ref, g_ref, sc_ref, dx_ref, dsh_ref, dsc_ref, gg_ref):
        _acc_init([dsh_ref, dsc_ref, gg_ref])
        xv, dhv = x_ref[...], dh_ref[...]
        r = lax.rsqrt(jnp.mean(xv * xv, axis=-1, keepdims=True) + EPS)
        xn = xv * r
        dsh_ref[...] += _colsum(dhv)
        dsc_ref[...] += _colsum(dhv * (xn * g_ref[...]))
        dm = dhv * sc_ref[...]
        gg_ref[...] += _colsum(dm * xn)
        dxn = dm * g_ref[...]
        dx_ref[...] = gx_ref[...] + r * (dxn - xn * jnp.mean(dxn * xn, axis=-1, keepdims=True))

    return pl.pallas_call(
        body, name=name, grid=(s // t,),
        in_specs=[_rowspec(t, d), _rowspec(t, d), _rowspec(t, d), _vecspec(d), _vecspec(d)],
        out_specs=[_rowspec(t, d), _vecspec(d), _vecspec(d), _vecspec(d)],
        out_shape=[_sds((s, d), F32), _sds((1, d), F32), _sds((1, d), F32), _sds((1, d), F32)],
        compiler_params=_cp(("arbitrary",)),
    )(dh, x, gxo, g, sc1p)


def _ada_fwd(c_all, ada_w, ada_b_cols, *, name):
    nl, d, cols = ada_w.shape

    def body(c_ref, w_ref, b_ref, o_ref):
        ca = _silu(c_ref[...]).astype(MXU_DTYPE)
        o_ref[0] = jnp.dot(ca, w_ref[0].astype(MXU_DTYPE), preferred_element_type=F32) + b_ref[0]

    return pl.pallas_call(
        body, name=name, grid=(nl,),
        in_specs=[pl.BlockSpec((N_DEV, d), lambda l: (0, 0)), pl.BlockSpec((1, d, cols), lambda l: (l, 0, 0)),
                  pl.BlockSpec((1, 1, cols), lambda l: (l, 0, 0))],
        out_specs=pl.BlockSpec((1, N_DEV, cols), lambda l: (l, 0, 0)),
        out_shape=_sds((nl, N_DEV, cols), F32),
        compiler_params=_cp(("parallel",)),
    )(c_all, ada_w, ada_b_cols)


def _ada_bwd(c_all_t, dmod_cols, *, name):
    nl, _, cols = dmod_cols.shape
    d = c_all_t.shape[0]

    def body(c_ref, dm_ref, o_ref):
        ca = _silu(c_ref[...]).astype(MXU_DTYPE)
        o_ref[0] = jnp.dot(ca, dm_ref[0].astype(MXU_DTYPE), preferred_element_type=F32)

    return pl.pallas_call(
        body, name=name, grid=(nl,),
        in_specs=[pl.BlockSpec((d, N_DEV), lambda l: (0, 0)), pl.BlockSpec((1, N_DEV, cols), lambda l: (l, 0, 0))],
        out_specs=pl.BlockSpec((1, d, cols), lambda l: (l, 0, 0)),
        out_shape=_sds((nl, d, cols), F32),
        compiler_params=_cp(("parallel",)),
    )(c_all_t, dmod_cols)


def _adamw(gparts, w, m, v, *, name):
    shape = w.shape
    cols = shape[-1]
    per_layer = isinstance(gparts, (list, tuple))
    nl = shape[0] if per_layer else 1
    rows = w.size // cols // nl
    glist = list(gparts) if per_layer else [gparts]
    npart = glist[0].shape[0]
    glist = [g.reshape(npart, rows, cols) for g in glist]
    w3, m3, v3 = (a.reshape(nl, rows, cols) for a in (w, m, v))
    budget = 2 * 1024 * 1024
    fits = [t for t in range(min(rows, 256) // 8 * 8, 7, -8)
            if rows % t == 0 and npart * t * cols * glist[0].dtype.itemsize <= budget]
    t = fits[0] if fits else rows
    nb = rows // t

    def body(*refs):
        g_refs = refs[:nl]
        w_ref, m_ref, v_ref, go_ref, d_ref, mo_ref, vo_ref, g_s = refs[nl:]
        layer = pl.program_id(0)
        for l in range(nl):
            @pl.when(layer == l)
            def _(l=l):
                g = g_refs[l][0].astype(F32)
                for p in range(1, npart):
                    g = g + g_refs[l][p].astype(F32)
                g_s[...] = g

        g = g_s[...]
        mn = ADAM_B1 * m_ref[0] + (1.0 - ADAM_B1) * g
        vn = ADAM_B2 * v_ref[0] + (1.0 - ADAM_B2) * (g * g)
        m_hat = mn / (1.0 - ADAM_B1 ** ADAM_STEP)
        v_hat = vn / (1.0 - ADAM_B2 ** ADAM_STEP)
        go_ref[0] = g
        d_ref[0] = -ADAM_LR * (m_hat / (jnp.sqrt(v_hat) + ADAM_EPS) + ADAM_WD * w_ref[0])
        mo_ref[0] = mn
        vo_ref[0] = vn

    def g_map(l):
        return lambda layer, i: (0, jnp.where(layer == l, i, jnp.where(layer < l, 0, nb - 1)), 0)

    spec = pl.BlockSpec((1, t, cols), lambda layer, i: (layer, i, 0))
    outs = pl.pallas_call(
        body, name=name, grid=(nl, nb),
        in_specs=[pl.BlockSpec((npart, t, cols), g_map(l)) for l in range(nl)] + [spec, spec, spec],
        out_specs=[spec] * 4, out_shape=[_sds((nl, rows, cols), F32)] * 4,
        scratch_shapes=[pltpu.VMEM((t, cols), F32)],
        compiler_params=_cp(("arbitrary", "arbitrary")),
    )(*glist, w3, m3, v3)
    return tuple(o.reshape(shape) for o in outs)


_ANY = pl.BlockSpec(memory_space=pl.ANY)


def _all_gather(blocks, *, name):
    na = len(blocks)

    def body(*refs):
        x_refs, out_refs = refs[:na], refs[na:2 * na]
        send_sems, recv_sems, local_sems = refs[2 * na:]
        x, y, c = lax.axis_index("x"), lax.axis_index("y"), lax.axis_index("c")
        me, sibling = (x, y, c), (x, y, 1 - c)
        chips = [(1 - x, y), (x, 1 - y), (1 - x, 1 - y)]

        def slot(a, px, py, pc):
            return out_refs[a].at[4 * px + 2 * py + pc]

        def copy(a, k, blk, to, src=None):
            return pltpu.make_async_remote_copy(
                src_ref=slot(a, *blk) if src is None else src, dst_ref=slot(a, *blk),
                send_sem=send_sems.at[7 * a + k], recv_sem=recv_sems.at[7 * a + k],
                device_id=to, device_id_type=MESH_ID)

        mine = [pltpu.make_async_copy(x_refs[a], slot(a, *me), local_sems.at[a]) for a in range(na)]
        for cp in mine:
            cp.start()
        first = []
        for a in range(na):
            first.append(copy(a, 0, me, sibling, src=x_refs[a]))
            first += [copy(a, 1 + j, me, (*chip, c), src=x_refs[a]) for j, chip in enumerate(chips)]
        for cp in first:
            cp.start()
        passed = []
        for a in range(na):
            for j, chip in enumerate(chips):
                copy(a, 1 + j, (*chip, c), me).wait_recv()
                fwd = copy(a, 4 + j, (*chip, c), sibling)
                fwd.start()
                passed.append(fwd)
        for a in range(na):
            copy(a, 0, sibling, me).wait_recv()
            for j, chip in enumerate(chips):
                copy(a, 4 + j, (*chip, 1 - c), me).wait_recv()
        for cp in first + passed:
            cp.wait_send()
        for cp in mine:
            cp.wait()

    outs = pl.pallas_call(
        body, name=name, in_specs=[_ANY] * na, out_specs=[_ANY] * na,
        out_shape=[_sds((N_DEV,) + b.shape, b.dtype) for b in blocks],
        scratch_shapes=[pltpu.SemaphoreType.DMA((7 * na,)), pltpu.SemaphoreType.DMA((7 * na,)),
                        pltpu.SemaphoreType.DMA((na,))],
    )(*blocks)
    return list(outs)


_HBM = pl.BlockSpec(memory_space=pltpu.HBM)
_SEM = pl.BlockSpec(memory_space=pltpu.SEMAPHORE)
_EFFECT = pltpu.SideEffectType.DATAFLOW_SIDE_EFFECTING


def _peers(x, y, c):
    out = []
    for k in range(1, N_DEV):
        out.append((1 - x if k & 4 else x, 1 - y if k & 2 else y, 1 - c if k & 1 else c))
    return out


def _own_slots(srcs, scatter, *, name, after=None):
    na = len(srcs)
    n_extra = 0 if after is None else 1
    me = (4 * lax.axis_index("x") + 2 * lax.axis_index("y") + lax.axis_index("c")).astype(jnp.int32).reshape(1)

    def body(me_ref, *refs):
        in_refs, out_refs = refs[:na], refs[na + n_extra:]
        for a in range(na):
            out_refs[a][0] = in_refs[a][0] if scatter else in_refs[a][...]

    def slot_spec(shard):
        zeros = (0,) * len(shard)
        return pl.BlockSpec((1,) + tuple(shard), lambda i, me_ref: (me_ref[0],) + zeros)

    def whole_spec(shape):
        zeros = (0,) * len(shape)
        return pl.BlockSpec(tuple(shape), lambda i, me_ref: zeros)

    shards = [s.shape[1:] if scatter else s.shape for s in srcs]
    in_specs = [slot_spec(sh) if scatter else whole_spec(sh) for sh in shards] + [_ANY] * n_extra
    outs = pl.pallas_call(
        body, name=name,
        grid_spec=pltpu.PrefetchScalarGridSpec(
            num_scalar_prefetch=1, grid=(1,), in_specs=in_specs, out_specs=[slot_spec(sh) for sh in shards]),
        out_shape=[_sds((N_DEV,) + tuple(sh), s.dtype) for sh, s in zip(shards, srcs)],
        compiler_params=_cp(("arbitrary",)),
    )(me, *srcs, *([] if after is None else [after]))
    return list(outs)


_N_COPIES = dict(scatter=7, gather=7, chips=4, forward=3)


def _exchange_copies(src_refs, land_refs, send_sems, recv_sems, mode):
    x, y, c = lax.axis_index("x"), lax.axis_index("y"), lax.axis_index("c")
    me = 4 * x + 2 * y + c
    nc = _N_COPIES[mode]
    chips = [(1 - x, y), (x, 1 - y), (1 - x, 1 - y)]
    cps = []
    for a in range(len(land_refs)):
        if mode in ("scatter", "gather"):
            plan = [((src_refs[a].at[4 * px + 2 * py + pc] if mode == "scatter" else src_refs[a]),
                     land_refs[a].at[me], (px, py, pc)) for px, py, pc in _peers(x, y, c)]
        elif mode == "chips":
            plan = [(src_refs[a], land_refs[a].at[me], to) for to in [(x, y, 1 - c)] + [(*ch, c) for ch in chips]]
        else:
            plan = [(land_refs[a].at[4 * px + 2 * py + c], land_refs[a].at[4 * px + 2 * py + c], (x, y, 1 - c))
                    for px, py in chips]
        for k, (src, dst, to) in enumerate(plan):
            cps.append(pltpu.make_async_remote_copy(
                src_ref=src, dst_ref=dst, send_sem=send_sems.at[nc * a + k], recv_sem=recv_sems.at[nc * a + k],
                device_id=to, device_id_type=MESH_ID))
    return cps


def _exchange_start(srcs, lands, mode, *, name):
    ns, nz = len(srcs), len(lands)
    nsem = _N_COPIES[mode] * nz

    def body(*refs):
        src_refs, land_refs = refs[:ns], refs[ns:ns + nz]
        send_sems, recv_sems = refs[ns + nz], refs[ns + nz + 1]
        token = refs[-1]
        for cp in _exchange_copies(src_refs, land_refs, send_sems, recv_sems, mode):
            cp.start()
        token[...] = jnp.zeros(token.shape, token.dtype)

    hbm = lambda a: pltpu.HBM(a.shape, a.dtype)
    outs = pl.pallas_call(
        body, name=name,
        out_shape=(pltpu.SemaphoreType.DMA((nsem,)), pltpu.SemaphoreType.DMA((nsem,)),
                   *[hbm(a) for a in srcs], *[hbm(a) for a in lands], _sds((8, LANE), F32)),
        in_specs=[_HBM] * (ns + nz),
        out_specs=(_SEM, _SEM, *[_HBM] * (ns + nz), pl.BlockSpec(memory_space=pltpu.VMEM)),
        input_output_aliases={i: 2 + i for i in range(ns + nz)},
        compiler_params=pltpu.CompilerParams(has_side_effects=_EFFECT),
    )(*[pltpu.with_memory_space_constraint(a, pltpu.HBM) for a in list(srcs) + list(lands)])
    return outs[0], outs[1], list(outs[2:2 + ns]), list(outs[2 + ns:2 + ns + nz]), outs[-1]


def _exchange_wait(send_sems, recv_sems, srcs, lands, after, mode, *, name):
    ns, nz = len(srcs), len(lands)

    def body(*refs):
        src_refs, land_refs = refs[:ns], refs[ns:ns + nz]
        s_sems, r_sems = refs[ns + nz], refs[ns + nz + 1]
        for cp in _exchange_copies(src_refs, land_refs, s_sems, r_sems, mode):
            cp.wait_send()
            cp.wait_recv()

    hbm = lambda a: pltpu.HBM(a.shape, a.dtype)
    outs = pl.pallas_call(
        body, name=name,
        out_shape=(*[hbm(a) for a in srcs], *[hbm(a) for a in lands]),
        in_specs=[_HBM] * (ns + nz) + [_SEM, _SEM, _ANY],
        out_specs=tuple([_HBM] * (ns + nz)),
        input_output_aliases={i: i for i in range(ns + nz)},
        compiler_params=pltpu.CompilerParams(has_side_effects=_EFFECT),
    )(*srcs, *lands, send_sems, recv_sems, after)
    return list(outs[ns:])


_WIN_SEGS = (("ql", 0, Q_LORA, SEG_QL[0]), ("kvl", Q_LORA, KV_LORA, SEG_KVL[0]),
             ("kr", Q_LORA + KV_LORA, ROPE, SEG_KR[0]), ("mg", Q_LORA + KV_LORA + ROPE, D_MLA, SEG_MG[0]),
             ("ci", Q_LORA + KV_LORA + ROPE + D_MLA, 2 * D_CONV, SEG_CI[0]),
             ("cg", Q_LORA + KV_LORA + ROPE + D_MLA + 2 * D_CONV, D_CONV, SEG_CG[0]))
_WIN_SHARD = IN_COLS // N_DEV


def _win_pieces():
    out = []
    for _, o, n, new in _WIN_SEGS:
        for j in range(N_DEV):
            lo, hi = max(o, j * _WIN_SHARD), min(o + n, (j + 1) * _WIN_SHARD)
            if lo < hi:
                out.append((j, lo - j * _WIN_SHARD, new + lo - o, hi - lo))
    return out


WIN_T = 512


def _win_assemble(w_all, *, name):
    d = w_all.shape[2]
    t = min(WIN_T, d)
    pieces = sorted(_win_pieces(), key=lambda p: p[2])
    assert all(lo % 8 == 0 and n % 8 == 0 for _, lo, _, n in pieces)

    def body(w_ref, o_ref):
        rows = [w_ref[j].astype(F32)[lo:lo + n, :] for j, lo, _, n in pieces]
        rows.append(jnp.zeros((IN_PAD - (SEG_KR[0] + ROPE), t), F32))
        o_ref[...] = jnp.concatenate(rows, axis=0).astype(o_ref.dtype)

    return pl.pallas_call(
        body, name=name, grid=(d // t,),
        in_specs=[pl.BlockSpec((N_DEV, _WIN_SHARD, t), lambda i: (0, 0, i))],
        out_specs=pl.BlockSpec((IN_PAD, t), lambda i: (0, i)), out_shape=_sds((IN_PAD, d), w_all.dtype),
        compiler_params=_cp(("parallel",)),
    )(w_all)


def _win_split(grad, *, name):
    d = grad.shape[1]
    t = min(WIN_T, d)
    by_shard = [sorted([p for p in _win_pieces() if p[0] == j], key=lambda p: p[1]) for j in range(N_DEV)]

    def body(g_ref, o_ref):
        for j in range(N_DEV):
            rows = [g_ref[new:new + n, :] for _, _, new, n in by_shard[j]]
            o_ref[j] = jnp.concatenate(rows, axis=0).astype(o_ref.dtype)

    return pl.pallas_call(
        body, name=name, grid=(d // t,),
        in_specs=[pl.BlockSpec((IN_PAD, t), lambda i: (0, i))],
        out_specs=pl.BlockSpec((N_DEV, _WIN_SHARD, t), lambda i: (0, 0, i)),
        out_shape=_sds((N_DEV, _WIN_SHARD, d), WIRE_DTYPE),
        compiler_params=_cp(("parallel",)),
    )(grad)


def _cols_to_shards(a):
    r, n = a.shape
    return a.reshape(r, N_DEV, n // N_DEV).transpose(1, 0, 2)


def _shards_to_cols(a):
    nd, r, w = a.shape
    return a.transpose(1, 0, 2).reshape(r, nd * w)


def _win_permute(w_in):
    o_ql, o_kvl, o_kr, o_mg = 0, Q_LORA, Q_LORA + KV_LORA, Q_LORA + KV_LORA + ROPE
    o_ci = o_mg + D_MLA
    o_cg = o_ci + 2 * D_CONV
    seg = lambda o, n: w_in[:, o:o + n]
    pad = jnp.zeros((w_in.shape[0], IN_PAD - (SEG_KR[0] + ROPE)), w_in.dtype)
    return jnp.concatenate([seg(o_ci, 2 * D_CONV), seg(o_mg, D_MLA), seg(o_cg, D_CONV), seg(o_ql, Q_LORA),
                            seg(o_kvl, KV_LORA), seg(o_kr, ROPE), pad], axis=1)


def _win_unpermute(g):
    seg = lambda s, n=None: g[:, s[0]:s[0] + (s[1] if n is None else n)]
    return jnp.concatenate([seg(SEG_QL), seg(SEG_KVL), seg(SEG_KR, ROPE), seg(SEG_MG), seg(SEG_CI), seg(SEG_CG)], axis=1)


def _qup_permute(w):
    w3 = w.reshape(w.shape[0], N_HEADS, QK_DIM)
    nope = w3[:, :, :NOPE].reshape(w.shape[0], N_HEADS * NOPE)
    rope = jnp.pad(w3[:, :, NOPE:], ((0, 0), (0, 0), (0, LANE - ROPE))).reshape(w.shape[0], N_HEADS * LANE)
    return jnp.concatenate([nope, rope], axis=1)


def _qup_unpermute(g):
    r = g.shape[0]
    nope = g[:, :N_HEADS * NOPE].reshape(r, N_HEADS, NOPE)
    rope = g[:, N_HEADS * NOPE:].reshape(r, N_HEADS, LANE)[:, :, :ROPE]
    return jnp.concatenate([nope, rope], axis=2).reshape(r, N_HEADS * QK_DIM)


def _norm_tiles(g):
    return g[:NOPE].reshape(1, LANE), jnp.pad(g[NOPE:], (0, LANE - ROPE)).reshape(1, LANE)


def _norm_untile(gt):
    return jnp.concatenate([gt[0, :NOPE], gt[0, LANE:LANE + ROPE]])


def _rope_tiles(positions):
    inv_freq = 1.0 / (ROPE_THETA ** (jnp.arange(0, ROPE, 2, dtype=F32) / ROPE))
    ang = positions.astype(F32)[:, None] * inv_freq
    cos, sin = jnp.cos(ang), jnp.sin(ang)
    zq = jnp.zeros_like(cos)
    c_t = jnp.concatenate([cos, cos, zq, zq], axis=1)
    s1_t = jnp.concatenate([-sin, zq, zq, zq], axis=1)
    s2_t = jnp.concatenate([zq, sin, zq, zq], axis=1)
    return c_t, s1_t, s2_t


_BIG = ("w_in", "w_q_up", "w_kv_up", "w_pw", "w_out")
_COL_SHARDED = ("w_in", "w_q_up", "w_kv_up")


def _pack_rows(arrs):
    return jnp.concatenate([a.reshape(-1, LANE) for a in arrs], axis=0)


def _unpack_rows(buf, shapes):
    out, r0 = [], 0
    lead = buf.shape[:-2]
    for shp in shapes:
        n = math.prod(shp) // LANE
        out.append(buf[..., r0:r0 + n, :].reshape(lead + tuple(shp)))
        r0 += n
    return out


_SMALL = (("dmod", 3 * D_MODEL), ("norm_g", D_MODEL), ("q_lat_g", Q_LORA), ("kv_lat_g", KV_LORA),
          ("q_norm_g", 2 * LANE), ("k_norm_g", 2 * LANE), ("glu_b", 2 * D_CONV), ("dw_w", HALO * D_CONV),
          ("dw_b", D_CONV), ("conv_ln_g", D_CONV), ("conv_ln_b", D_CONV), ("b_pw", D_CONV))


def _layer_fwd(x, p, rope, l, early=None, late=None):
    n = lambda s: f"{s}_l{l}"
    c_t, s1_t, s2_t = rope
    h = _prenorm(x, p["norm_g"], p["shift"], p["sc1p"], name=n("prenorm"))
    if early is not None:
        p = {**p, **early(h)}
    z = _mm(h, p["w_in"], tb=True, name=n("in_proj"), tn=IN_TILE, n_outer=True,
            after=p.get("in_proj_after"))
    if late is not None:
        p = {**p, **late(z)}
    qn, kn = _lat_norm(z, p["q_lat_g"], p["kv_lat_g"], name=n("lat_norm"))
    q_raw = _mm(qn, p["w_q_up"], name=n("q_up"), tn=1024)
    kv = _mm(kn, p["w_kv_up"], name=n("kv_up"), tn=1024)
    qf, kf, vf = _qk_prep(q_raw, kv, z, c_t, s1_t, s2_t, *p["qk_tiles"], name=n("qk_prep"))
    o, lse = _flash_fwd(qf, kf, vf, name=n("flash_fwd"))
    u1, u3 = _conv_fwd(z, p["glu_b"], p["dw_w"], p["dw_b"], p["conv_ln_g"], p["conv_ln_b"], name=n("conv_fwd"))
    u4m = _mm(u3, p["w_pw"], name=n("pw"), tn=1024)
    cat = _gate_cat(o, z, u4m, p["b_pw"], name=n("gate_cat"))
    y, x_next = _mm(cat, p["w_out"], name=n("out_proj"), tn=1024, residual=(x, p["gate"]))
    saved = dict(x=x, h=h, z=z, qn=qn, kn=kn, q_raw=q_raw, kv=kv, qf=qf, kf=kf, vf=vf, o=o, lse=lse,
                 u1=u1, u3=u3, u4m=u4m, cat=cat, y=y)
    return x_next, saved, p


def _layer_bwd(gxo, p, sv, rope, l, hook_rest=None, hook_w_in=None):
    n = lambda s: f"{s}_l{l}"
    c_t, s1_t, s2_t = rope
    z = sv["z"]
    dy, dgate = _out_bwd(gxo, sv["y"], p["gate"], name=n("out_bwd"))
    g_w_out = _mm(sv["cat"], dy, ta=True, name=n("g_w_out"), tm=1024, tn=1024)
    do, delta, du4, g_b_pw, dz = _gate_bwd(dy, p["w_out"], sv["o"], z, sv["u4m"], p["b_pw"], name=n("gate_bwd"))
    g_w_pw = _mm(sv["u3"], du4, ta=True, name=n("g_w_pw"), tm=1024, tn=1024, tk=512)
    du3 = _mm(du4, p["w_pw"], tb=True, name=n("d_u3"), tn=1024)
    dz, g_ln_g, g_ln_b, g_dw_b, g_glu_b, g_dw_w = _conv_bwd(
        du3, sv["u1"], z, dz, p["glu_b"], p["dw_w"], p["conv_ln_g"], p["conv_ln_b"], name=n("conv_bwd"))
    t_att = min(ATT_T, z.shape[0])
    to_lanes = lambda a: a.reshape(N_HEADS, z.shape[0] // t_att, 1, t_att)
    dqf, dkf, dvf = _flash_bwd(sv["qf"], sv["kf"], sv["vf"], do,
                               to_lanes(sv["lse"][:, :, 0]), to_lanes(delta), name=n("flash_bwd"))
    dq_raw, dkv, dkr, g_qn, g_kn = _qk_bwd(dqf, dkf, dvf, sv["q_raw"], sv["kv"], z, c_t, s1_t, s2_t,
                                            *p["qk_tiles"], name=n("qk_bwd"))
    g_w_q_up = _mm(sv["qn"], dq_raw, ta=True, name=n("g_w_q_up"), tm=512, tn=1024, tk=512)
    dqn = _mm(dq_raw, p["w_q_up"], tb=True, name=n("d_qn"))
    g_w_kv_up = _mm(sv["kn"], dkv, ta=True, name=n("g_w_kv_up"), tm=256, tn=1024, tk=512)
    dkn = _mm(dkv, p["w_kv_up"], tb=True, name=n("d_kn"))
    dz, g_ql, g_kvl = _lat_bwd(dqn, dkn, dkr, z, dz, p["q_lat_g"], p["kv_lat_g"], name=n("lat_bwd"))
    big = dict(w_q_up=g_w_q_up, w_kv_up=g_w_kv_up, w_pw=g_w_pw, w_out=g_w_out)
    after = None if hook_rest is None else hook_rest(big)
    g_w_in = _mm(dz, sv["h"], ta=True, name=n("g_w_in"), tm=512, tn=1024, after=after)
    big["w_in"] = g_w_in
    after = None if hook_w_in is None else hook_w_in(g_w_in)
    dh = _mm(dz, p["w_in"], name=n("d_h"), tn=1024, after=after)
    dx, dshift, dscale, g_norm = _prenorm_bwd(dh, sv["x"], gxo, p["norm_g"], p["sc1p"], name=n("prenorm_bwd"))
    small = dict(dmod=jnp.concatenate([dshift, dscale, dgate], axis=1), norm_g=g_norm, q_lat_g=g_ql, kv_lat_g=g_kvl,
                 q_norm_g=g_qn, k_norm_g=g_kn, glu_b=g_glu_b, dw_w=g_dw_w, dw_b=g_dw_b,
                 conv_ln_g=g_ln_g, conv_ln_b=g_ln_b, b_pw=g_b_pw)
    return dx, big, small


def _layer_params(l, full, mod_l, small):
    d = D_MODEL
    row = lambda a: a.reshape(1, -1)
    shift, scale, gate = mod_l[:, :d], mod_l[:, d:2 * d], mod_l[:, 2 * d:]
    dw_w = jnp.pad(full["dw_w"][l], ((0, HALO - CONV_K), (0, 0)))
    return dict(
        shift=shift, sc1p=1.0 + scale, gate=gate, norm_g=row(small["norm_g"][l]),
        **{k: full[k][l] for k in _BIG if k in full}, dw_w=dw_w,
        q_lat_g=row(small["q_lat_g"][l]), kv_lat_g=row(small["kv_lat_g"][l]),
        qk_tiles=_norm_tiles(small["q_norm_g"][l]) + _norm_tiles(small["k_norm_g"][l]),
        glu_b=row(small["glu_b"][l]), dw_b=row(small["dw_b"][l]), conv_ln_g=row(small["conv_ln_g"][l]),
        conv_ln_b=row(small["conv_ln_b"][l]), b_pw=row(small["b_pw"][l]))


def kernel(x, c, positions, ada_w, ada_b, norm_g, w_in, q_lat_g, w_q_up, kv_lat_g, w_kv_up, q_norm_g, k_norm_g, glu_b, dw_w, dw_b, conv_ln_g, conv_ln_b, w_pw, b_pw, w_out, loss_target, m_ada_w, m_ada_b, m_norm_g, m_w_in, m_q_lat_g, m_w_q_up, m_kv_lat_g, m_w_kv_up, m_q_norm_g, m_k_norm_g, m_glu_b, m_dw_w, m_dw_b, m_conv_ln_g, m_conv_ln_b, m_w_pw, m_b_pw, m_w_out, v_ada_w, v_ada_b, v_norm_g, v_w_in, v_q_lat_g, v_w_q_up, v_kv_lat_g, v_w_kv_up, v_q_norm_g, v_k_norm_g, v_glu_b, v_dw_w, v_dw_b, v_conv_ln_g, v_conv_ln_b, v_w_pw, v_b_pw, v_w_out):
    names = ("ada_w", "ada_b", "norm_g", "w_in", "q_lat_g", "w_q_up", "kv_lat_g", "w_kv_up", "q_norm_g",
             "k_norm_g", "glu_b", "dw_w", "dw_b", "conv_ln_g", "conv_ln_b", "w_pw", "b_pw", "w_out")
    w_loc = dict(zip(names, (ada_w, ada_b, norm_g, w_in, q_lat_g, w_q_up, kv_lat_g, w_kv_up, q_norm_g, k_norm_g,
                             glu_b, dw_w, dw_b, conv_ln_g, conv_ln_b, w_pw, b_pw, w_out)))
    m_loc = dict(zip(names, (m_ada_w, m_ada_b, m_norm_g, m_w_in, m_q_lat_g, m_w_q_up, m_kv_lat_g, m_w_kv_up,
                             m_q_norm_g, m_k_norm_g, m_glu_b, m_dw_w, m_dw_b, m_conv_ln_g, m_conv_ln_b, m_w_pw,
                             m_b_pw, m_w_out)))
    v_loc = dict(zip(names, (v_ada_w, v_ada_b, v_norm_g, v_w_in, v_q_lat_g, v_w_q_up, v_kv_lat_g, v_w_kv_up,
                             v_q_norm_g, v_k_norm_g, v_glu_b, v_dw_w, v_dw_b, v_conv_ln_g, v_conv_ln_b, v_w_pw,
                             v_b_pw, v_w_out)))
    nl, d = N_LAYERS, D_MODEL
    me = 4 * lax.axis_index("x") + 2 * lax.axis_index("y") + lax.axis_index("c")
    x2, tgt = x[0], loss_target[0]
    ada_cols = ada_w.shape[-1]

    tr = lambda a: jnp.swapaxes(a, 1, 2)
    w_loc, m_loc, v_loc = ({**dd, "w_in": tr(dd["w_in"])} for dd in (w_loc, m_loc, v_loc))
    wire = {k: w_loc[k].astype(WIRE_DTYPE) for k in _BIG}
    w_in0 = [wire["w_in"][0]]
    fly_c = _exchange_start(w_in0, _own_slots(w_in0, False, name="own_w_in_l0"), "chips", name="gather_start_w_in_l0")

    dw_pad = jnp.pad(dw_w, ((0, 0), (0, HALO - CONV_K), (0, 0)))
    c_rows = c.reshape(d // LANE, LANE) + fly_c[4][0:1, :]
    c_all, dw_all = _all_gather([c_rows, dw_pad], name="gather_c")
    c_all = c_all.reshape(N_DEV, d)
    ada_b_cols = lax.dynamic_slice_in_dim(ada_b, me * ada_cols, ada_cols, axis=1).reshape(nl, 1, ada_cols)
    mod_cols = _ada_fwd(c_all, ada_w, ada_b_cols, name="ada_fwd")
    mod_all = _all_gather([mod_cols], name="gather_mod")[0]
    mod_me = lax.dynamic_index_in_dim(mod_all, me, axis=2, keepdims=False)
    mod = mod_me.transpose(1, 0, 2).reshape(nl, 1, N_DEV * ada_cols)

    from_chips = _exchange_wait(*fly_c[:4], mod, "chips", name="gather_wait_w_in_l0")
    fly_f = _exchange_start([], from_chips, "forward", name="forward_start_w_in_l0")
    w_in_all0 = _exchange_wait(*fly_f[:4], fly_f[4], "forward", name="forward_wait_w_in_l0")[0]
    rest0 = [wire[k][0] for k in _BIG[1:]]
    fly_r0, fly_w1 = {}, {}
    fly_r0["x"] = _exchange_start(rest0, _own_slots(rest0, False, name="own_weights_l0_rest", after=w_in_all0),
                                  "gather", name="gather_start_l0_rest")

    def layout_rest(parts):
        return dict(w_q_up=_qup_permute(_shards_to_cols(parts[0])), w_kv_up=_shards_to_cols(parts[1]),
                    w_pw=parts[2].reshape(D_CONV, D_CONV), w_out=parts[3].reshape(D_MLA + D_CONV, d))

    small_in = dict(norm_g=norm_g, q_lat_g=q_lat_g, kv_lat_g=kv_lat_g, q_norm_g=q_norm_g, k_norm_g=k_norm_g,
                    glu_b=glu_b, dw_b=dw_b, conv_ln_g=conv_ln_g, conv_ln_b=conv_ln_b, b_pw=b_pw)
    dw_full = [_shards_to_cols(dw_all[:, l])[:CONV_K] for l in range(nl)]
    rope = _rope_tiles(positions[0])

    def layer_params(l, w_in_all, rest, mod_l):
        full = dict(dw_w=dw_full)
        if w_in_all is not None:
            full["w_in"] = {l: _win_assemble(w_in_all, name=f"w_in_assemble_l{l}")}
        if rest is not None:
            full.update({k: {l: a} for k, a in layout_rest(rest).items()})
        return _layer_params(l, full, mod_l, small_in)

    def late_l0(z):
        parts = _exchange_wait(*fly_r0["x"][:4], z, "gather", name="gather_wait_l0_rest")
        src1 = [wire[k][1] for k in _BIG]
        fly_w1["x"] = _exchange_start(src1, _own_slots(src1, False, name="own_weights_l1", after=parts[0]), "gather",
                                      name="gather_start_l1")
        late = layout_rest(parts)
        late["q_lat_g"] = small_in["q_lat_g"][0].reshape(1, -1) + fly_w1["x"][4][0, 0]
        return late

    params, saved = [None] * nl, [None] * nl
    p0 = layer_params(0, w_in_all0, None, mod[0] + fly_r0["x"][4][0, 0])
    xs, saved[0], params[0] = _layer_fwd(x2, p0, rope, 0, late=late_l0)
    parts1 = _exchange_wait(*fly_w1["x"][:4], xs, "gather", name="gather_wait_l1")
    params[1] = layer_params(1, parts1[0], parts1[1:], mod[1])
    xs, saved[1], _ = _layer_fwd(xs, params[1], rope, 1)
    gx, loss_part = _loss_head(xs, tgt, name="loss_head")
    loss = lax.psum(loss_part[0, 0], ("x", "y", "c"))

    def shard_major(k, g):
        if k == "w_q_up":
            g = _qup_unpermute(g)
        if k in _COL_SHARDED:
            return _cols_to_shards(g)
        return g.reshape((N_DEV, g.shape[0] // N_DEV, g.shape[1]))

    def scatter_start(send, tag):
        lands = _own_slots(send, True, name=f"own_grads_{tag}")
        return _exchange_start(send, lands, "scatter", name=f"scatter_start_{tag}")

    def wire_rest(big):
        return [shard_major(k, big[k]).astype(WIRE_DTYPE) for k in _BIG[1:]]

    big_g, small_g, flying = [None] * nl, [None] * nl, {}
    gx, big_g[1], small_g[1] = _layer_bwd(gx, params[1], saved[1], rope, 1)
    flying["l1"] = scatter_start([_win_split(big_g[1]["w_in"], name="w_in_split_l1")] + wire_rest(big_g[1]), "l1")
    p0 = dict(params[0])
    p0["gate"] = p0["gate"] + flying["l1"][4][0, 0]

    def start_rest_l0(big):
        flying["l0_rest"] = scatter_start(wire_rest(big), "l0_rest")
        return flying["l0_rest"][4]

    res, arrived = {}, [None] * nl

    def start_w_in_l0(g_w_in):
        flying["l0_w_in"] = scatter_start([_win_split(g_w_in, name="w_in_split_l0")], "l0_w_in")
        tok = flying["l0_w_in"][4]
        arrived[1] = _exchange_wait(*flying["l1"][:4], tok, "scatter", name="scatter_wait_l1")
        arrived[0] = [None] + _exchange_wait(*flying["l0_rest"][:4], tok, "scatter", name="scatter_wait_l0_rest")
        for i, k in enumerate(_BIG):
            if i > 0:
                res[k] = _adamw([arrived[l][i] for l in range(nl)], w_loc[k], m_loc[k], v_loc[k], name=f"adamw_{k}")
        return res["w_out"][0]

    gx, big_g[0], small_g[0] = _layer_bwd(gx, p0, saved[0], rope, 0, hook_rest=start_rest_l0,
                                          hook_w_in=start_w_in_l0)

    tile = 8 * LANE
    padded = [(k, nn, -(-nn // tile) * tile) for k, nn in _SMALL]
    spk = jnp.concatenate([jnp.pad(small_g[l][k].reshape(-1), (0, np_ - nn)).reshape(-1, LANE)
                           for l in range(nl) for k, nn, np_ in padded], axis=0)
    s_all = _all_gather([spk], name="gather_small_grads")[0]
    s_rows = sum(np_ for _, _, np_ in padded) // LANE
    s_all = s_all.reshape(N_DEV, nl, s_rows, LANE)
    s_parts = {k: a[..., :nn] for (k, nn, _), a in
               zip(padded, _unpack_rows(s_all, [(np_,) for _, _, np_ in padded]))}

    dmod_all = s_parts["dmod"]
    dmod_cols = lax.dynamic_slice_in_dim(dmod_all, me * ada_cols, ada_cols, axis=2).transpose(1, 0, 2)
    g_ada_w = _ada_bwd(c_all.T, dmod_cols, name="ada_bwd")
    gp = {}
    gp["ada_w"] = g_ada_w[None]
    gp["ada_b"] = dmod_all
    for k in ("norm_g", "q_lat_g", "kv_lat_g", "glu_b", "dw_b", "conv_ln_g", "conv_ln_b", "b_pw"):
        gp[k] = s_parts[k]
    for k in ("q_norm_g", "k_norm_g"):
        t = s_parts[k]
        gp[k] = jnp.concatenate([t[..., :NOPE], t[..., LANE:LANE + ROPE]], axis=-1)
    dw_g = s_parts["dw_w"].reshape(N_DEV, nl, HALO, D_CONV)[:, :, :CONV_K]
    gp["dw_w"] = lax.dynamic_slice_in_dim(dw_g, me * LANE, LANE, axis=3)

    res.update({k: _adamw(gp[k], w_loc[k], m_loc[k], v_loc[k], name=f"adamw_{k}") for k in names if k not in _BIG})
    arrived[0][0] = _exchange_wait(*flying["l0_w_in"][:4], res["ada_w"][1], "scatter", name="scatter_wait_l0_w_in")[0]
    w_in_res = _adamw([arrived[l][0] for l in range(nl)], w_loc["w_in"], m_loc["w_in"], v_loc["w_in"],
                      name="adamw_w_in")
    res["w_in"] = tuple(tr(a) for a in w_in_res)
    out = [loss, gx[None]]
    for idx in range(4):
        out += [res[k][idx] for k in names]
    return tuple(out)
```

```python
import functools
import math

import jax
import jax.numpy as jnp
from jax import lax
from jax.experimental import pallas as pl
from jax.experimental.pallas import tpu as pltpu

F32 = jnp.float32
MXU_DTYPE = jnp.bfloat16
WIRE_DTYPE = jnp.bfloat16

D_MODEL = 2048
N_LAYERS = 2
N_DEV = 8
N_HEADS = 8
NOPE = 128
ROPE = 64
V_DIM = 128
QK_DIM = NOPE + ROPE
Q_LORA = 512
KV_LORA = 256
D_MLA = N_HEADS * V_DIM
D_CONV = 1024
CONV_K = 31
ROPE_THETA = 10000.0
EPS = 1e-6
LANE = 128
HEAD_PAD = 2 * LANE
HALO = 32

SEG_CI = (0, 2 * D_CONV)
SEG_MG = (2 * D_CONV, D_MLA)
SEG_CG = (2 * D_CONV + D_MLA, D_CONV)
SEG_QL = (2 * D_CONV + D_MLA + D_CONV, Q_LORA)
SEG_KVL = (SEG_QL[0] + Q_LORA, KV_LORA)
SEG_KR = (SEG_KVL[0] + KV_LORA, LANE)
SEG_LAT = (SEG_QL[0], 1024)
IN_PAD = SEG_LAT[0] + SEG_LAT[1]
IN_TILE = IN_PAD // 4
assert SEG_KR[0] + LANE <= IN_PAD and SEG_LAT[0] % SEG_LAT[1] == 0
IN_COLS = Q_LORA + KV_LORA + ROPE + D_MLA + 2 * D_CONV + D_CONV

ADAM_LR = 0.001
ADAM_B1 = 0.9
ADAM_B2 = 0.999
ADAM_EPS = 1e-08
ADAM_WD = 0.01
ADAM_STEP = 10

VMEM_LIMIT = 56 * 1024 * 1024
ATT_T = 512
ROW_T = 256
CONV_T = 128
MESH_ID = pl.DeviceIdType.MESH


def _cp(sem=None):
    kw = dict(vmem_limit_bytes=VMEM_LIMIT)
    if sem is not None:
        kw["dimension_semantics"] = sem
    return pltpu.CompilerParams(**kw)


def _sds(shape, dtype):
    return jax.ShapeDtypeStruct(shape, dtype)


def _silu(x):
    return x * jax.nn.sigmoid(x)


def _dsilu(x):
    s = jax.nn.sigmoid(x)
    return s * (1.0 + x * (1.0 - s))


def _rowspec(t, width, col=0):
    return pl.BlockSpec((t, width), lambda i: (i, col))


def _vecspec(width):
    return pl.BlockSpec((1, width), lambda i: (0, 0))


def _colsum(v):
    return jnp.sum(v, axis=0, keepdims=True)


def _mm(a, b, *, name, ta=False, tb=False, out_dtype=F32, tm=512, tn=512, tk=None, n_outer=False, after=None,
        residual=None):
    if ta:
        kdim, m = a.shape
    else:
        m, kdim = a.shape
    if tb:
        n, k2 = b.shape
    else:
        k2, n = b.shape
    assert kdim == k2, (a.shape, b.shape)
    tm, tn = min(tm, m), min(tn, n)
    tk = kdim if tk is None else min(tk, kdim)
    assert m % tm == 0 and n % tn == 0 and kdim % tk == 0, (m, n, kdim, tm, tn, tk)
    nk = kdim // tk
    dims = (((0 if ta else 1,), (1 if tb else 0,)), ((), ()))

    n_extra = 0 if after is None else 1
    assert residual is None or nk == 1

    def body(a_ref, b_ref, *rest):
        if residual is not None:
            x_ref, gate_ref = rest[:2]
            rest = rest[2:]
        o_ref, scratch = rest[n_extra], rest[n_extra + 1:]
        prod = lax.dot_general(a_ref[...].astype(MXU_DTYPE), b_ref[...].astype(MXU_DTYPE), dims,
                               preferred_element_type=F32)
        if residual is not None:
            o_ref[...] = prod.astype(o_ref.dtype)
            scratch[0][...] = x_ref[...] + gate_ref[...] * prod
        elif nk == 1:
            o_ref[...] = prod.astype(o_ref.dtype)
        else:
            acc = scratch[0]
            k = pl.program_id(2)

            @pl.when(k == 0)
            def _():
                acc[...] = prod

            @pl.when(k > 0)
            def _():
                acc[...] += prod

            @pl.when(k == nk - 1)
            def _():
                o_ref[...] = acc[...].astype(o_ref.dtype)

    if n_outer:
        ij = lambda g0, g1: (g1, g0)
        grid = (n // tn, m // tm, nk)
    else:
        ij = lambda g0, g1: (g0, g1)
        grid = (m // tm, n // tn, nk)

    def a_map(g0, g1, k):
        i, _ = ij(g0, g1)
        return (k, i) if ta else (i, k)

    def b_map(g0, g1, k):
        _, j = ij(g0, g1)
        return (j, k) if tb else (k, j)

    def o_map(g0, g1, k):
        return ij(g0, g1)

    in_specs = [pl.BlockSpec((tk, tm) if ta else (tm, tk), a_map), pl.BlockSpec((tn, tk) if tb else (tk, tn), b_map)]
    operands = [a, b]
    out_specs, out_shape = pl.BlockSpec((tm, tn), o_map), _sds((m, n), out_dtype)
    if residual is not None:
        in_specs += [pl.BlockSpec((tm, tn), o_map), pl.BlockSpec((1, tn), lambda g0, g1, k: (0, ij(g0, g1)[1]))]
        operands += list(residual)
        out_specs, out_shape = [out_specs, pl.BlockSpec((tm, tn), o_map)], [out_shape, _sds((m, n), F32)]
    if after is not None:
        in_specs.append(_ANY)
        operands.append(after)
    return pl.pallas_call(
        body, name=name, grid=grid, in_specs=in_specs, out_specs=out_specs, out_shape=out_shape,
        scratch_shapes=[pltpu.VMEM((tm, tn), F32)] if nk > 1 else [],
        compiler_params=_cp(("parallel", "parallel", "arbitrary")),
    )(*operands)


def _prenorm(x, g, shift, sc1p, *, name):
    s, d = x.shape
    t = min(ROW_T, s)

    def body(x_ref, g_ref, sh_ref, sc_ref, h_ref):
        xv = x_ref[...]
        r = lax.rsqrt(jnp.mean(xv * xv, axis=-1, keepdims=True) + EPS)
        h_ref[...] = ((xv * r) * g_ref[...] * sc_ref[...] + sh_ref[...]).astype(h_ref.dtype)

    return pl.pallas_call(
        body, name=name, grid=(s // t,),
        in_specs=[_rowspec(t, d), _vecspec(d), _vecspec(d), _vecspec(d)],
        out_specs=_rowspec(t, d), out_shape=_sds((s, d), MXU_DTYPE),
        compiler_params=_cp(("parallel",)),
    )(x, g, shift, sc1p)


def _lat_norm(z, g_ql, g_kvl, *, name):
    s = z.shape[0]
    t = min(ROW_T, s)

    def body(ql_ref, kvl_ref, gq_ref, gk_ref, qn_ref, kn_ref):
        for src, g_ref, dst in ((ql_ref, gq_ref, qn_ref), (kvl_ref, gk_ref, kn_ref)):
            v = src[...]
            r = lax.rsqrt(jnp.mean(v * v, axis=-1, keepdims=True) + EPS)
            dst[...] = ((v * r) * g_ref[...]).astype(dst.dtype)

    return pl.pallas_call(
        body, name=name, grid=(s // t,),
        in_specs=[_rowspec(t, Q_LORA, SEG_QL[0] // Q_LORA), _rowspec(t, KV_LORA, SEG_KVL[0] // KV_LORA),
                  _vecspec(Q_LORA), _vecspec(KV_LORA)],
        out_specs=[_rowspec(t, Q_LORA), _rowspec(t, KV_LORA)],
        out_shape=[_sds((s, Q_LORA), MXU_DTYPE), _sds((s, KV_LORA), MXU_DTYPE)],
        compiler_params=_cp(("parallel",)),
    )(z, z, g_ql, g_kvl)


def _rope_fwd(r, c_t, s1_t, s2_t):
    return r * c_t + pltpu.roll(r, LANE - ROPE // 2, 1) * s1_t + pltpu.roll(r, ROPE // 2, 1) * s2_t


def _rope_bwd(d, c_t, s1_t, s2_t):
    return d * c_t + pltpu.roll(d * s1_t, ROPE // 2, 1) + pltpu.roll(d * s2_t, LANE - ROPE // 2, 1)


def _lanesum(v):
    return jnp.sum(v, axis=-1, keepdims=True)


def _qk_prep(q_raw, kv, z, c_t, s1_t, s2_t, gqn, gqr, gkn, gkr, *, name):
    s = q_raw.shape[0]
    t = min(ROW_T, s)
    scale = 1.0 / math.sqrt(QK_DIM)

    def body(q_ref, kv_ref, kr_ref, c_ref, s1_ref, s2_ref, gqn_ref, gqr_ref, gkn_ref, gkr_ref,
             qf_ref, kf_ref, vf_ref):
        c_v, s1_v, s2_v = c_ref[...], s1_ref[...], s2_ref[...]
        kr = kr_ref[...]
        kr_ss = _lanesum(kr * kr)
        for h in range(N_HEADS):
            n = q_ref[:, h * LANE:(h + 1) * LANE]
            r = q_ref[:, N_HEADS * LANE + h * LANE:N_HEADS * LANE + (h + 1) * LANE]
            rs = lax.rsqrt((_lanesum(n * n) + _lanesum(r * r)) * (1.0 / QK_DIM) + EPS)
            qf_ref[h, :, 0:LANE] = (((n * rs) * gqn_ref[...]) * scale).astype(qf_ref.dtype)
            rr = _rope_fwd((r * rs) * gqr_ref[...], c_v, s1_v, s2_v)
            qf_ref[h, :, LANE:HEAD_PAD] = (rr * scale).astype(qf_ref.dtype)

            n = kv_ref[:, h * 2 * LANE:h * 2 * LANE + LANE]
            rs = lax.rsqrt((_lanesum(n * n) + kr_ss) * (1.0 / QK_DIM) + EPS)
            kf_ref[h, :, 0:LANE] = ((n * rs) * gkn_ref[...]).astype(kf_ref.dtype)
            kf_ref[h, :, LANE:HEAD_PAD] = _rope_fwd((kr * rs) * gkr_ref[...], c_v, s1_v, s2_v).astype(kf_ref.dtype)
            vf_ref[h, :, 0:V_DIM] = kv_ref[:, h * 2 * LANE + LANE:(h + 1) * 2 * LANE].astype(vf_ref.dtype)
            vf_ref[h, :, V_DIM:] = jnp.ones((t, V_DIM), vf_ref.dtype)

    hspec = lambda w: pl.BlockSpec((N_HEADS, t, w), lambda i: (0, i, 0))
    return pl.pallas_call(
        body, name=name, grid=(s // t,),
        in_specs=[_rowspec(t, 2 * N_HEADS * LANE), _rowspec(t, 2 * N_HEADS * LANE),
                  _rowspec(t, LANE, SEG_KR[0] // LANE),
                  _rowspec(t, LANE), _rowspec(t, LANE), _rowspec(t, LANE),
                  _vecspec(LANE), _vecspec(LANE), _vecspec(LANE), _vecspec(LANE)],
        out_specs=[hspec(HEAD_PAD), hspec(HEAD_PAD), hspec(2 * V_DIM)],
        out_shape=[_sds((N_HEADS, s, HEAD_PAD), MXU_DTYPE), _sds((N_HEADS, s, HEAD_PAD), MXU_DTYPE),
                   _sds((N_HEADS, s, 2 * V_DIM), MXU_DTYPE)],
        compiler_params=_cp(("parallel",)),
    )(q_raw, kv, z, c_t, s1_t, s2_t, gqn, gqr, gkn, gkr)


def _causal_mask(t):
    row = lax.broadcasted_iota(jnp.int32, (t, t), 0)
    col = lax.broadcasted_iota(jnp.int32, (t, t), 1)
    return col <= row


NEG = -1e30


def _flash_fwd(qf, kf, va, *, name):
    nh, s, dk = qf.shape
    dv = va.shape[-1] // 2
    t = min(ATT_T, s)
    n = s // t
    assert dv == LANE and t % LANE == 0

    def body(q_ref, k_ref, v_ref, o_ref, lse_ref, m_s, acc_s, s_buf):
        i = pl.program_id(1)
        m_s[...] = jnp.full(m_s.shape, NEG, F32)
        acc_s[...] = jnp.zeros(acc_s.shape, F32)

        def rows_of(j):
            return pl.ds(pl.multiple_of(j * t, t), t)

        def scores(qi, j):
            return lax.dot_general(q_ref[0, rows_of(qi), :], k_ref[0, rows_of(j), :], (((1,), (1,)), ((), ())),
                                   preferred_element_type=F32)

        def consume(j, slot, masked):
            sc = s_buf[slot]
            if masked:
                sc = jnp.where(_causal_mask(t), sc, NEG)
            m_prev = m_s[...]
            m_new = jnp.maximum(m_prev, jnp.max(sc, axis=-1, keepdims=True))
            alpha = jnp.exp(m_prev - m_new)
            p = jnp.exp(sc - jnp.tile(m_new, (1, t // LANE)))
            acc_s[...] = jnp.tile(alpha, (1, 2)) * acc_s[...] + jnp.dot(
                p.astype(MXU_DTYPE), v_ref[0, rows_of(j), :], preferred_element_type=F32)
            m_s[...] = m_new

        nxt = jnp.minimum(i + 1, n - 1)

        @pl.when(i == 0)
        def _():
            s_buf[2] = scores(0, 0)
            consume(0, 2, True)
            s_buf[2] = scores(nxt, 0)

        @pl.when(i > 0)
        def _():
            s_buf[1] = scores(i, 1)
            consume(0, 2, False)

            def pair(a, carry):
                s_buf[0] = scores(i, 2 * a + 2)
                consume(2 * a + 1, 1, False)
                s_buf[1] = scores(i, 2 * a + 3)
                consume(2 * a + 2, 0, False)
                return carry

            lax.fori_loop(0, (i - 1) // 2, pair, 0)

            @pl.when(i % 2 == 1)
            def _():
                s_buf[2] = scores(nxt, 0)
                consume(i, 1, True)

            @pl.when(i % 2 == 0)
            def _():
                s_buf[0] = scores(i, i)
                consume(i - 1, 1, False)
                s_buf[2] = scores(nxt, 0)
                consume(i, 0, True)

        den = acc_s[:, dv:]
        o_ref[...] = acc_s[:, :dv] / den
        lse_ref[0] = m_s[...] + jnp.log(den)

    head = lambda h, i: (h, 0, 0)
    return pl.pallas_call(
        body, name=name, grid=(nh, n),
        in_specs=[pl.BlockSpec((1, s, dk), head), pl.BlockSpec((1, s, dk), head), pl.BlockSpec((1, s, 2 * dv), head)],
        out_specs=[pl.BlockSpec((t, dv), lambda h, i: (i, h)),
                   pl.BlockSpec((1, t, LANE), lambda h, i: (h, i, 0))],
        out_shape=[_sds((s, nh * dv), F32), _sds((nh, s, LANE), F32)],
        scratch_shapes=[pltpu.VMEM((t, LANE), F32), pltpu.VMEM((t, 2 * dv), F32), pltpu.VMEM((3, t, t), F32)],
        compiler_params=_cp(("arbitrary", "arbitrary")),
    )(qf, kf, va)


def _shifted_copies(ext_ref):
    rows = ext_ref.shape[1] - 8
    for s in range(1, 8):
        ext_ref[s, 0:rows, :] = ext_ref[0, s:s + rows, :]


def _windows(ext_ref, offsets, t_rows, lane0, lanes):
    for s in range(8):
        group = [o for o in offsets if o % 8 == s]
        if not group:
            continue
        lo, hi = min(group) - s, max(group) - s
        wide = ext_ref[s, pl.ds(lo, hi - lo + t_rows), lane0:lane0 + lanes]
        for o in group:
            yield o, wide[o - s - lo:o - s - lo + t_rows]


def _dw_taps(ext_ref, w_ref, row0, t_rows, lane0, lanes, first_off):
    acc = None
    for off, win in _windows(ext_ref, [row0 + first_off + k for k in range(CONV_K)], t_rows, lane0, lanes):
        k = off - row0 - first_off
        term = w_ref[k:k + 1, lane0:lane0 + lanes] * win
        acc = term if acc is None else acc + term
    return acc


CONV_RC = 32
CONV_LC = 256


def _conv_fwd(z, glu_b, dw_w, dw_b, ln_g, ln_b, *, name):
    s = z.shape[0]
    t = min(CONV_T, s)
    c2 = 2 * D_CONV
    hb = t // HALO

    def body(zm_ref, zh_ref, gb_ref, w_ref, wb_ref, g_ref, b_ref, u1_ref, u3_ref, ext):
        i = pl.program_id(0)

        def glu(zv):
            ci = zv + gb_ref[...]
            return ci[:, :D_CONV] * jax.nn.sigmoid(ci[:, D_CONV:])

        ext[0, HALO:, :] = glu(zm_ref[...])
        ext[0, 0:HALO, :] = jnp.where(i > 0, glu(zh_ref[...]), 0.0)
        _shifted_copies(ext)
        for rc in range(0, t, CONV_RC):
            for lc in range(0, D_CONV, CONV_LC):
                acc = _dw_taps(ext, w_ref, rc, CONV_RC, lc, CONV_LC, HALO - (CONV_K - 1))
                u1_ref[rc:rc + CONV_RC, lc:lc + CONV_LC] = acc + wb_ref[:, lc:lc + CONV_LC]
        u1 = u1_ref[...]
        mu = jnp.mean(u1, axis=-1, keepdims=True)
        cen = u1 - mu
        var = jnp.mean(cen * cen, axis=-1, keepdims=True)
        u2 = (cen * lax.rsqrt(var + EPS)) * g_ref[...] + b_ref[...]
        u3_ref[...] = _silu(u2).astype(u3_ref.dtype)

    return pl.pallas_call(
        body, name=name, grid=(s // t,),
        in_specs=[_rowspec(t, c2), pl.BlockSpec((HALO, c2), lambda i: (jnp.maximum(i * hb - 1, 0), 0)),
                  _vecspec(c2), pl.BlockSpec((HALO, D_CONV), lambda i: (0, 0)), _vecspec(D_CONV),
                  _vecspec(D_CONV), _vecspec(D_CONV)],
        out_specs=[_rowspec(t, D_CONV), _rowspec(t, D_CONV)],
        out_shape=[_sds((s, D_CONV), F32), _sds((s, D_CONV), MXU_DTYPE)],
        scratch_shapes=[pltpu.VMEM((8, t + HALO, D_CONV), F32)],
        compiler_params=_cp(("parallel",)),
    )(z, z, glu_b, dw_w, dw_b, ln_g, ln_b)


def _gate_cat(o, z, u4m, b_pw, *, name):
    s = o.shape[0]
    t = min(ROW_T, s)

    def body(o_ref, mg_ref, u4_ref, cg_ref, b_ref, cat_ref):
        cat_ref[:, :D_MLA] = (o_ref[...] * _silu(mg_ref[...])).astype(cat_ref.dtype)
        cat_ref[:, D_MLA:] = ((u4_ref[...] + b_ref[...]) * _silu(cg_ref[...])).astype(cat_ref.dtype)

    return pl.pallas_call(
        body, name=name, grid=(s // t,),
        in_specs=[_rowspec(t, D_MLA), _rowspec(t, D_MLA, SEG_MG[0] // D_MLA), _rowspec(t, D_CONV),
                  _rowspec(t, D_CONV, SEG_CG[0] // D_CONV), _vecspec(D_CONV)],
        out_specs=_rowspec(t, D_MLA + D_CONV), out_shape=_sds((s, D_MLA + D_CONV), MXU_DTYPE),
        compiler_params=_cp(("parallel",)),
    )(o, z, u4m, z, b_pw)


def _loss_head(xf, target, *, name):
    s, d = xf.shape
    t = min(ROW_T, s)

    def body(x_ref, t_ref, gx_ref, loss_ref):
        @pl.when(pl.program_id(0) == 0)
        def _():
            loss_ref[...] = jnp.zeros(loss_ref.shape, F32)

        err = x_ref[...] - t_ref[...]
        gx_ref[...] = err * (1.0 / d)
        loss_ref[...] += 0.5 * jnp.sum(_lanesum(err * err) * (1.0 / d), axis=0, keepdims=True)

    return pl.pallas_call(
        body, name=name, grid=(s // t,),
        in_specs=[_rowspec(t, d), _rowspec(t, d)],
        out_specs=[_rowspec(t, d), pl.BlockSpec((1, 1), lambda i: (0, 0))],
        out_shape=[_sds((s, d), F32), _sds((1, 1), F32)],
        compiler_params=_cp(("arbitrary",)),
    )(xf, target)


def _acc_init(refs):
    @pl.when(pl.program_id(0) == 0)
    def _():
        for r in refs:
            r[...] = jnp.zeros(r.shape, r.dtype)


def _out_bwd(gxo, y, gate, *, name):
    s, d = gxo.shape
    t = min(ROW_T, s)

    def body(g_ref, y_ref, gate_ref, dy_ref, dgate_ref):
        _acc_init([dgate_ref])
        gv = g_ref[...]
        dy_ref[...] = (gv * gate_ref[...]).astype(dy_ref.dtype)
        dgate_ref[...] += _colsum(gv * y_ref[...])

    return pl.pallas_call(
        body, name=name, grid=(s // t,),
        in_specs=[_rowspec(t, d), _rowspec(t, d), _vecspec(d)],
        out_specs=[_rowspec(t, d), _vecspec(d)],
        out_shape=[_sds((s, d), MXU_DTYPE), _sds((1, d), F32)],
        compiler_params=_cp(("arbitrary",)),
    )(gxo, y, gate)


def _gate_bwd(dy, w_out, o, z, u4m, b_pw, *, name):
    s, d = dy.shape
    t = min(2 * ROW_T, s)
    gates = D_MLA + D_CONV
    assert SEG_CG[0] == SEG_MG[0] + D_MLA and SEG_MG[0] % gates == 0

    def body(dy_ref, w_ref, o_ref, mg_ref, u4_ref, cg_ref, b_ref,
             do_ref, delta_ref, du4_ref, gb_ref, dz_ref):
        _acc_init([gb_ref])
        dcat = lax.dot_general(dy_ref[...], w_ref[...], (((1,), (1,)), ((), ())), preferred_element_type=F32)
        dm, ov, mg = dcat[:, :D_MLA], o_ref[...], mg_ref[...]
        do = dm * _silu(mg)
        do_ref[...] = do.astype(do_ref.dtype)
        dz_ref[:, :D_MLA] = (dm * ov * _dsilu(mg)).astype(dz_ref.dtype)
        prod = do * ov
        for h in range(N_HEADS):
            delta_ref[h] = _lanesum(prod[:, h * V_DIM:(h + 1) * V_DIM])
        dc, cg = dcat[:, D_MLA:], cg_ref[...]
        du4 = dc * _silu(cg)
        du4_ref[...] = du4.astype(du4_ref.dtype)
        dz_ref[:, D_MLA:] = (dc * (u4_ref[...] + b_ref[...]) * _dsilu(cg)).astype(dz_ref.dtype)
        gb_ref[...] += _colsum(du4)

    return pl.pallas_call(
        body, name=name, grid=(s // t,),
        in_specs=[_rowspec(t, d), pl.BlockSpec((gates, d), lambda i: (0, 0)), _rowspec(t, D_MLA),
                  _rowspec(t, D_MLA, SEG_MG[0] // D_MLA), _rowspec(t, D_CONV),
                  _rowspec(t, D_CONV, SEG_CG[0] // D_CONV), _vecspec(D_CONV)],
        out_specs=[_rowspec(t, D_MLA), pl.BlockSpec((N_HEADS, t, 1), lambda i: (0, i, 0)),
                   _rowspec(t, D_CONV), _vecspec(D_CONV), _rowspec(t, gates, SEG_MG[0] // gates)],
        out_shape=[_sds((s, D_MLA), MXU_DTYPE), _sds((N_HEADS, s, 1), F32),
                   _sds((s, D_CONV), MXU_DTYPE), _sds((1, D_CONV), F32), _sds((s, IN_PAD), MXU_DTYPE)],
        compiler_params=_cp(("arbitrary",)),
    )(dy, w_out, o, z, u4m, z, b_pw)


def _conv_bwd(du3, u1, z, dz, glu_b, dw_w, ln_g, ln_b, *, name):
    s = z.shape[0]
    t = min(CONV_T, s)
    c2 = 2 * D_CONV
    hb = t // HALO
    n_blk = s // t
    last_halo = s // HALO - 1

    def body(d3m_ref, d3h_ref, u1m_ref, u1h_ref, zm_ref, zh_ref, gb_ref, w_ref, g_ref, b_ref, dz_in_ref,
             dci_ref, gg_ref, gbn_ref, gwb_ref, ggb_ref, gw_ref, dext, uext, du0_s, gw_acc):
        i = pl.program_id(0)
        _acc_init([gg_ref, gbn_ref, gwb_ref, ggb_ref, gw_acc])

        def ln_bwd(d3, u1v):
            mu = jnp.mean(u1v, axis=-1, keepdims=True)
            cen = u1v - mu
            rstd = lax.rsqrt(jnp.mean(cen * cen, axis=-1, keepdims=True) + EPS)
            uh = cen * rstd
            d2 = d3 * _dsilu(uh * g_ref[...] + b_ref[...])
            dh = d2 * g_ref[...]
            d1 = rstd * (dh - jnp.mean(dh, axis=-1, keepdims=True) - uh * jnp.mean(dh * uh, axis=-1, keepdims=True))
            return d1, d2, uh

        d1, d2, uh = ln_bwd(d3m_ref[...], u1m_ref[...])
        gg_ref[...] += _colsum(d2 * uh)
        gbn_ref[...] += _colsum(d2)
        gwb_ref[...] += _colsum(d1)
        dext[0, 0:t, :] = d1
        d1h, _, _ = ln_bwd(d3h_ref[...], u1h_ref[...])
        dext[0, t:, :] = jnp.where(i < n_blk - 1, d1h, 0.0)
        _shifted_copies(dext)

        def glu_parts(zv):
            ci = zv + gb_ref[...]
            return ci[:, :D_CONV], jax.nn.sigmoid(ci[:, D_CONV:])

        val, sg = glu_parts(zm_ref[...])
        uext[0, HALO:, :] = val * sg
        valh, sgh = glu_parts(zh_ref[...])
        uext[0, 0:HALO, :] = jnp.where(i > 0, valh * sgh, 0.0)
        _shifted_copies(uext)

        for rc in range(0, t, CONV_RC):
            for lc in range(0, D_CONV, CONV_LC):
                acc = None
                for off, win in _windows(dext, [rc + k for k in range(CONV_K)], CONV_RC, lc, CONV_LC):
                    k = (CONV_K - 1) - (off - rc)
                    term = w_ref[k:k + 1, lc:lc + CONV_LC] * win
                    acc = term if acc is None else acc + term
                du0_s[rc:rc + CONV_RC, lc:lc + CONV_LC] = acc
                dchunk = dext[0, rc:rc + CONV_RC, lc:lc + CONV_LC]
                first = rc + HALO - (CONV_K - 1)
                for off, win in _windows(uext, [first + k for k in range(CONV_K)], CONV_RC, lc, CONV_LC):
                    k = off - first
                    pr = dchunk * win
                    part = pr[0:8]
                    for r8 in range(8, CONV_RC, 8):
                        part = part + pr[r8:r8 + 8]
                    gw_acc[k, :, lc:lc + CONV_LC] += part

        du0 = du0_s[...]
        dval = du0 * sg
        dgt = du0 * val * sg * (1.0 - sg)
        dci_ref[:, :D_CONV] = dval.astype(dci_ref.dtype)
        dci_ref[:, D_CONV:] = dgt.astype(dci_ref.dtype)
        ggb_ref[:, :D_CONV] += _colsum(dval)
        ggb_ref[:, D_CONV:] += _colsum(dgt)

        @pl.when(i == n_blk - 1)
        def _():
            gw_ref[...] = jnp.sum(gw_acc[...], axis=1)

    halo_next = lambda w: pl.BlockSpec((HALO, w), lambda i: (jnp.minimum((i + 1) * hb, last_halo), 0))
    return pl.pallas_call(
        body, name=name, grid=(n_blk,),
        in_specs=[_rowspec(t, D_CONV), halo_next(D_CONV), _rowspec(t, D_CONV), halo_next(D_CONV),
                  _rowspec(t, c2), pl.BlockSpec((HALO, c2), lambda i: (jnp.maximum(i * hb - 1, 0), 0)),
                  _vecspec(c2), pl.BlockSpec((HALO, D_CONV), lambda i: (0, 0)), _vecspec(D_CONV), _vecspec(D_CONV),
                  _ANY],
        out_specs=[_rowspec(t, c2, SEG_CI[0] // c2), _vecspec(D_CONV), _vecspec(D_CONV), _vecspec(D_CONV),
                   _vecspec(c2), pl.BlockSpec((HALO, D_CONV), lambda i: (0, 0))],
        out_shape=[_sds(dz.shape, dz.dtype), _sds((1, D_CONV), F32), _sds((1, D_CONV), F32), _sds((1, D_CONV), F32),
                   _sds((1, c2), F32), _sds((HALO, D_CONV), F32)],
        scratch_shapes=[pltpu.VMEM((8, t + HALO, D_CONV), F32), pltpu.VMEM((8, t + HALO, D_CONV), F32),
                        pltpu.VMEM((t, D_CONV), F32), pltpu.VMEM((HALO, 8, D_CONV), F32)],
        input_output_aliases={10: 0},
        compiler_params=_cp(("arbitrary",)),
    )(du3, du3, u1, u1, z, z, glu_b, dw_w, ln_g, ln_b, dz)


def _flash_bwd(qf, kf, va, do, lse_t, delta_t, *, name):
    nh, s, dk = qf.shape
    dv = va.shape[-1] // 2
    t = min(ATT_T, s)
    n = s // t
    nt = (((1,), (1,)), ((), ()))
    tn = (((0,), (0,)), ((), ()))

    def body(q_ref, do_ref, lse_ref, dl_ref, k_ref, v_ref, dq_ref, dk_ref, dv_ref,
             dk_s, dv_s, st_buf, dpt_buf):
        n_un = pl.program_id(1)
        j = n - 1 - n_un
        nxt = jnp.maximum(j - 1, 0)

        @pl.when(n_un == 0)
        def _():
            dq_ref[...] = jnp.zeros(dq_ref.shape, F32)

        dk_s[...] = jnp.zeros(dk_s.shape, F32)
        dv_s[...] = jnp.zeros(dv_s.shape, F32)

        def rows_at(blk):
            return pl.ds(pl.multiple_of(blk * t, t), t)

        def rows_of(b):
            return rows_at(n - 1 - b)

        k = k_ref[0, rows_at(j), :]

        def produce(kj, b, slot):
            rows = rows_of(b)
            st_buf[slot] = lax.dot_general(k_ref[0, rows_at(kj), :], q_ref[0, rows, :], nt,
                                           preferred_element_type=F32)
            dpt_buf[slot] = lax.dot_general(v_ref[0, rows_at(kj), 0:dv], do_ref[rows, :], nt,
                                            preferred_element_type=F32)

        def consume(b, slot, masked):
            i = n - 1 - b
            rows = rows_of(b)
            q, dov = q_ref[0, rows, :], do_ref[rows, :]
            pt = jnp.exp(st_buf[slot] - lse_ref[0, i])
            if masked:
                key = lax.broadcasted_iota(jnp.int32, (t, t), 0)
                qry = lax.broadcasted_iota(jnp.int32, (t, t), 1)
                pt = jnp.where(key <= qry, pt, 0.0)
            dv_s[...] += jnp.dot(pt.astype(MXU_DTYPE), dov, preferred_element_type=F32)
            dst = (pt * (dpt_buf[slot] - dl_ref[0, i])).astype(MXU_DTYPE)
            dk_s[...] += jnp.dot(dst, q, preferred_element_type=F32)
            dq_ref[0, rows, :] += lax.dot_general(dst, k, tn, preferred_element_type=F32)

        @pl.when(n_un == 0)
        def _():
            produce(j, 0, 2)
            consume(0, 2, True)
            produce(nxt, 0, 2)

        @pl.when(n_un > 0)
        def _():
            produce(j, 1, 1)
            consume(0, 2, False)

            def pair(a, carry):
                produce(j, 2 * a + 2, 0)
                consume(2 * a + 1, 1, False)
                produce(j, 2 * a + 3, 1)
                consume(2 * a + 2, 0, False)
                return carry

            lax.fori_loop(0, (n_un - 1) // 2, pair, 0)

            @pl.when(n_un % 2 == 1)
            def _():
                produce(nxt, 0, 2)
                consume(n_un, 1, True)

            @pl.when(n_un % 2 == 0)
            def _():
                produce(j, n_un, 0)
                consume(n_un - 1, 1, False)
                produce(nxt, 0, 2)
                consume(n_un, 0, True)

        dk_ref[0] = dk_s[...]
        dv_ref[0] = dv_s[...]

    head = lambda h, j: (h, 0, 0)
    rowv = pl.BlockSpec((1, n, 1, t), lambda h, j: (h, 0, 0, 0))
    return pl.pallas_call(
        body, name=name, grid=(nh, n),
        in_specs=[pl.BlockSpec((1, s, dk), head),
                  pl.BlockSpec((s, dv), lambda h, j: (0, h)),
                  rowv, rowv,
                  pl.BlockSpec((1, s, dk), head),
                  pl.BlockSpec((1, s, 2 * dv), head)],
        out_specs=[pl.BlockSpec((1, s, dk), head),
                   pl.BlockSpec((1, t, dk), lambda h, g: (h, n - 1 - g, 0)),
                   pl.BlockSpec((1, t, dv), lambda h, g: (h, n - 1 - g, 0))],
        out_shape=[_sds((nh, s, dk), F32), _sds((nh, s, dk), F32), _sds((nh, s, dv), F32)],
        scratch_shapes=[pltpu.VMEM((t, dk), F32), pltpu.VMEM((t, dv), F32),
                        pltpu.VMEM((3, t, t), F32), pltpu.VMEM((3, t, t), F32)],
        compiler_params=_cp(("arbitrary", "arbitrary")),
    )(qf, do, lse_t, delta_t, kf, va)


def _qk_bwd(dqf, dkf, dvf, q_raw, kv, z, c_t, s1_t, s2_t, gqn, gqr, gkn, gkr, *, name):
    s = q_raw.shape[0]
    t = min(ROW_T, s)
    scale = 1.0 / math.sqrt(QK_DIM)

    def body(dq_ref, dk_ref, dv_ref, q_ref, kv_ref, kr_ref, c_ref, s1_ref, s2_ref,
             gqn_ref, gqr_ref, gkn_ref, gkr_ref, dqr_ref, dkv_ref, dkr_ref, ggq_ref, ggk_ref):
        _acc_init([ggq_ref, ggk_ref])
        c_v, s1_v, s2_v = c_ref[...], s1_ref[...], s2_ref[...]
        kr = kr_ref[...]
        kr_ss = _lanesum(kr * kr)
        dkr = jnp.zeros(kr.shape, F32)
        ggq_n = ggq_r = ggk_n = ggk_r = jnp.zeros((1, LANE), F32)

        def norm_bwd(n, r, rs, dyn, dyr, gn, gr):
            nh_, rh_ = n * rs, r * rs
            dnh, drh = dyn * gn, dyr * gr
            dot = (_lanesum(dnh * nh_) + _lanesum(drh * rh_)) * (1.0 / QK_DIM)
            return rs * (dnh - nh_ * dot), rs * (drh - rh_ * dot), _colsum(dyn * nh_), _colsum(dyr * rh_)

        for h in range(N_HEADS):
            n = q_ref[:, h * LANE:(h + 1) * LANE]
            r = q_ref[:, N_HEADS * LANE + h * LANE:N_HEADS * LANE + (h + 1) * LANE]
            rs = lax.rsqrt((_lanesum(n * n) + _lanesum(r * r)) * (1.0 / QK_DIM) + EPS)
            dyn = dq_ref[h, :, 0:LANE] * scale
            dyr = _rope_bwd(dq_ref[h, :, LANE:HEAD_PAD] * scale, c_v, s1_v, s2_v)
            dn, dr, g_n, g_r = norm_bwd(n, r, rs, dyn, dyr, gqn_ref[...], gqr_ref[...])
            dqr_ref[:, h * LANE:(h + 1) * LANE] = dn.astype(dqr_ref.dtype)
            dqr_ref[:, N_HEADS * LANE + h * LANE:N_HEADS * LANE + (h + 1) * LANE] = dr.astype(dqr_ref.dtype)
            ggq_n, ggq_r = ggq_n + g_n, ggq_r + g_r

            n = kv_ref[:, h * 2 * LANE:h * 2 * LANE + LANE]
            rs = lax.rsqrt((_lanesum(n * n) + kr_ss) * (1.0 / QK_DIM) + EPS)
            dyn = dk_ref[h, :, 0:LANE]
            dyr = _rope_bwd(dk_ref[h, :, LANE:HEAD_PAD], c_v, s1_v, s2_v)
            dn, dr, g_n, g_r = norm_bwd(n, kr, rs, dyn, dyr, gkn_ref[...], gkr_ref[...])
            dkv_ref[:, h * 2 * LANE:h * 2 * LANE + LANE] = dn.astype(dkv_ref.dtype)
            dkv_ref[:, h * 2 * LANE + LANE:(h + 1) * 2 * LANE] = dv_ref[h].astype(dkv_ref.dtype)
            dkr = dkr + dr
            ggk_n, ggk_r = ggk_n + g_n, ggk_r + g_r

        dkr_ref[...] = dkr.astype(dkr_ref.dtype)
        ggq_ref[:, 0:LANE] += ggq_n
        ggq_ref[:, LANE:] += ggq_r
        ggk_ref[:, 0:LANE] += ggk_n
        ggk_ref[:, LANE:] += ggk_r

    hspec = lambda w: pl.BlockSpec((N_HEADS, t, w), lambda i: (0, i, 0))
    wide = 2 * N_HEADS * LANE
    return pl.pallas_call(
        body, name=name, grid=(s // t,),
        in_specs=[hspec(HEAD_PAD), hspec(HEAD_PAD), hspec(V_DIM), _rowspec(t, wide), _rowspec(t, wide),
                  _rowspec(t, LANE, SEG_KR[0] // LANE), _rowspec(t, LANE), _rowspec(t, LANE), _rowspec(t, LANE),
                  _vecspec(LANE), _vecspec(LANE), _vecspec(LANE), _vecspec(LANE)],
        out_specs=[_rowspec(t, wide), _rowspec(t, wide), _rowspec(t, LANE), _vecspec(2 * LANE), _vecspec(2 * LANE)],
        out_shape=[_sds((s, wide), MXU_DTYPE), _sds((s, wide), MXU_DTYPE), _sds((s, LANE), MXU_DTYPE),
                   _sds((1, 2 * LANE), F32), _sds((1, 2 * LANE), F32)],
        compiler_params=_cp(("arbitrary",)),
    )(dqf, dkf, dvf, q_raw, kv, z, c_t, s1_t, s2_t, gqn, gqr, gkn, gkr)


def _lat_bwd(dqn, dkn, dkr, z, dz, g_ql, g_kvl, *, name):
    s = z.shape[0]
    t = min(ROW_T, s)
    o_ql, o_kvl, o_kr = (seg[0] - SEG_LAT[0] for seg in (SEG_QL, SEG_KVL, SEG_KR))

    def body(dq_ref, dk_ref, dkr_ref, ql_ref, kvl_ref, gq_ref, gk_ref, dz_in_ref, dz_ref, ggq_ref, ggk_ref):
        _acc_init([ggq_ref, ggk_ref])
        for d_ref, src, g_ref, off, gg_ref in ((dq_ref, ql_ref, gq_ref, o_ql, ggq_ref),
                                               (dk_ref, kvl_ref, gk_ref, o_kvl, ggk_ref)):
            v, dy = src[...], d_ref[...]
            r = lax.rsqrt(jnp.mean(v * v, axis=-1, keepdims=True) + EPS)
            vh = v * r
            dvh = dy * g_ref[...]
            dz_ref[:, off:off + v.shape[1]] = (
                r * (dvh - vh * jnp.mean(dvh * vh, axis=-1, keepdims=True))).astype(dz_ref.dtype)
            gg_ref[...] += _colsum(dy * vh)
        dz_ref[:, o_kr:o_kr + LANE] = dkr_ref[...]
        dz_ref[:, o_kr + LANE:] = jnp.zeros((t, SEG_LAT[1] - o_kr - LANE), dz_ref.dtype)

    return pl.pallas_call(
        body, name=name, grid=(s // t,),
        in_specs=[_rowspec(t, Q_LORA), _rowspec(t, KV_LORA), _rowspec(t, LANE),
                  _rowspec(t, Q_LORA, SEG_QL[0] // Q_LORA), _rowspec(t, KV_LORA, SEG_KVL[0] // KV_LORA),
                  _vecspec(Q_LORA), _vecspec(KV_LORA), _ANY],
        out_specs=[_rowspec(t, SEG_LAT[1], SEG_LAT[0] // SEG_LAT[1]), _vecspec(Q_LORA), _vecspec(KV_LORA)],
        out_shape=[_sds(dz.shape, dz.dtype), _sds((1, Q_LORA), F32), _sds((1, KV_LORA), F32)],
        input_output_aliases={7: 0},
        compiler_params=_cp(("arbitrary",)),
    )(dqn, dkn, dkr, z, z, g_ql, g_kvl, dz)


def _prenorm_bwd(dh, x, gxo, g, sc1p, *, name):
    s, d = x.shape
    t = min(ROW_T, s)

    def body(dh_ref, x_ref, gx_ref, g_ref, sc_ref, dx_ref, dsh_ref, dsc_ref, gg_ref):
        _acc_init([dsh_ref, dsc_ref, gg_ref])
        xv, dhv = x_ref[...], dh_ref[...]
        r = lax.rsqrt(jnp.mean(xv * xv, axis=-1, keepdims=True) + EPS)
        xn = xv * r
        dsh_ref[...] += _colsum(dhv)
        dsc_ref[...] += _colsum(dhv * (xn * g_ref[...]))
        dm = dhv * sc_ref[...]
        gg_ref[...] += _colsum(dm * xn)
        dxn = dm * g_ref[...]
        dx_ref[...] = gx_ref[...] + r * (dxn - xn * jnp.mean(dxn * xn, axis=-1, keepdims=True))

    return pl.pallas_call(
        body, name=name, grid=(s // t,),
        in_specs=[_rowspec(t, d), _rowspec(t, d), _rowspec(t, d), _vecspec(d), _vecspec(d)],
        out_specs=[_rowspec(t, d), _vecspec(d), _vecspec(d), _vecspec(d)],
        out_shape=[_sds((s, d), F32), _sds((1, d), F32), _sds((1, d), F32), _sds((1, d), F32)],
        compiler_params=_cp(("arbitrary",)),
    )(dh, x, gxo, g, sc1p)


def _ada_fwd(c_all, ada_w, ada_b_cols, *, name):
    nl, d, cols = ada_w.shape

    def body(c_ref, w_ref, b_ref, o_ref):
        ca = _silu(c_ref[...]).astype(MXU_DTYPE)
        o_ref[0] = jnp.dot(ca, w_ref[0].astype(MXU_DTYPE), preferred_element_type=F32) + b_ref[0]

    return pl.pallas_call(
        body, name=name, grid=(nl,),
        in_specs=[pl.BlockSpec((N_DEV, d), lambda l: (0, 0)), pl.BlockSpec((1, d, cols), lambda l: (l, 0, 0)),
                  pl.BlockSpec((1, 1, cols), lambda l: (l, 0, 0))],
        out_specs=pl.BlockSpec((1, N_DEV, cols), lambda l: (l, 0, 0)),
        out_shape=_sds((nl, N_DEV, cols), F32),
        compiler_params=_cp(("parallel",)),
    )(c_all, ada_w, ada_b_cols)


def _ada_bwd(c_all_t, dmod_cols, *, name):
    nl, _, cols = dmod_cols.shape
    d = c_all_t.shape[0]

    def body(c_ref, dm_ref, o_ref):
        ca = _silu(c_ref[...]).astype(MXU_DTYPE)
        o_ref[0] = jnp.dot(ca, dm_ref[0].astype(MXU_DTYPE), preferred_element_type=F32)

    return pl.pallas_call(
        body, name=name, grid=(nl,),
        in_specs=[pl.BlockSpec((d, N_DEV), lambda l: (0, 0)), pl.BlockSpec((1, N_DEV, cols), lambda l: (l, 0, 0))],
        out_specs=pl.BlockSpec((1, d, cols), lambda l: (l, 0, 0)),
        out_shape=_sds((nl, d, cols), F32),
        compiler_params=_cp(("parallel",)),
    )(c_all_t, dmod_cols)


def _adamw_math(g, w, m, v):
    mn = ADAM_B1 * m + (1.0 - ADAM_B1) * g
    vn = ADAM_B2 * v + (1.0 - ADAM_B2) * (g * g)
    m_hat = mn / (1.0 - ADAM_B1 ** ADAM_STEP)
    v_hat = vn / (1.0 - ADAM_B2 ** ADAM_STEP)
    return -ADAM_LR * (m_hat / (jnp.sqrt(v_hat) + ADAM_EPS) + ADAM_WD * w), mn, vn


def _adamw_small(items, *, name):
    n = len(items)
    shapes = [it[1].shape for it in items]
    flat = lambda a, lead: a.reshape(lead + (-1, a.shape[-1]))
    operands = []
    for gp, w, m, v in items:
        operands += [flat(gp, (gp.shape[0],)), flat(w, ()), flat(m, ()), flat(v, ())]
    nparts = [it[0].shape[0] for it in items]

    def body(*refs):
        ins, outs = refs[:4 * n], refs[4 * n:]
        for i in range(n):
            g_ref, w_ref, m_ref, v_ref = ins[4 * i:4 * i + 4]
            g = g_ref[0].astype(F32)
            for p in range(1, nparts[i]):
                g = g + g_ref[p].astype(F32)
            outs[4 * i][...] = g
            outs[4 * i + 1][...], outs[4 * i + 2][...], outs[4 * i + 3][...] = _adamw_math(
                g, w_ref[...], m_ref[...], v_ref[...])

    out_shape = []
    for it in items:
        out_shape += [_sds(flat(it[1], ()).shape, F32)] * 4
    outs = pl.pallas_call(body, name=name, out_shape=out_shape, compiler_params=_cp())(*operands)
    return [tuple(o.reshape(shp) for o in outs[4 * i:4 * i + 4]) for i, shp in enumerate(shapes)]


def _adamw(gparts, w, m, v, *, name):
    shape = w.shape
    cols = shape[-1]
    per_layer = isinstance(gparts, (list, tuple))
    nl = shape[0] if per_layer else 1
    rows = w.size // cols // nl
    glist = list(gparts) if per_layer else [gparts]
    npart = glist[0].shape[0]
    glist = [g.reshape(npart, rows, cols) for g in glist]
    w3, m3, v3 = (a.reshape(nl, rows, cols) for a in (w, m, v))
    budget = 2 * 1024 * 1024
    fits = [t for t in range(min(rows, 256) // 8 * 8, 7, -8)
            if rows % t == 0 and npart * t * cols * glist[0].dtype.itemsize <= budget]
    t = fits[0] if fits else rows
    nb = rows // t

    def body(*refs):
        g_refs = refs[:nl]
        w_ref, m_ref, v_ref, go_ref, d_ref, mo_ref, vo_ref, g_s = refs[nl:]
        layer = pl.program_id(0)
        for l in range(nl):
            @pl.when(layer == l)
            def _(l=l):
                g = g_refs[l][0].astype(F32)
                for p in range(1, npart):
                    g = g + g_refs[l][p].astype(F32)
                g_s[...] = g

        g = g_s[...]
        go_ref[0] = g
        d_ref[0], mo_ref[0], vo_ref[0] = _adamw_math(g, w_ref[0], m_ref[0], v_ref[0])

    def g_map(l):
        return lambda layer, i: (0, jnp.where(layer == l, i, jnp.where(layer < l, 0, nb - 1)), 0)

    spec = pl.BlockSpec((1, t, cols), lambda layer, i: (layer, i, 0))
    outs = pl.pallas_call(
        body, name=name, grid=(nl, nb),
        in_specs=[pl.BlockSpec((npart, t, cols), g_map(l)) for l in range(nl)] + [spec, spec, spec],
        out_specs=[spec] * 4, out_shape=[_sds((nl, rows, cols), F32)] * 4,
        scratch_shapes=[pltpu.VMEM((t, cols), F32)],
        compiler_params=_cp(("arbitrary", "arbitrary")),
    )(*glist, w3, m3, v3)
    return tuple(o.reshape(shape) for o in outs)


_ANY = pl.BlockSpec(memory_space=pl.ANY)


def _all_gather(blocks, *, name):
    na = len(blocks)

    def body(*refs):
        x_refs, out_refs = refs[:na], refs[na:2 * na]
        send_sems, recv_sems, local_sems = refs[2 * na:]
        x, y, c = lax.axis_index("x"), lax.axis_index("y"), lax.axis_index("c")
        me, sibling = (x, y, c), (x, y, 1 - c)
        chips = [(1 - x, y), (x, 1 - y), (1 - x, 1 - y)]

        def slot(a, px, py, pc):
            return out_refs[a].at[4 * px + 2 * py + pc]

        def copy(a, k, blk, to, src=None):
            return pltpu.make_async_remote_copy(
                src_ref=slot(a, *blk) if src is None else src, dst_ref=slot(a, *blk),
                send_sem=send_sems.at[7 * a + k], recv_sem=recv_sems.at[7 * a + k],
                device_id=to, device_id_type=MESH_ID)

        mine = [pltpu.make_async_copy(x_refs[a], slot(a, *me), local_sems.at[a]) for a in range(na)]
        for cp in mine:
            cp.start()
        first = []
        for a in range(na):
            first.append(copy(a, 0, me, sibling, src=x_refs[a]))
            first += [copy(a, 1 + j, me, (*chip, c), src=x_refs[a]) for j, chip in enumerate(chips)]
        for cp in first:
            cp.start()
        passed = []
        for a in range(na):
            for j, chip in enumerate(chips):
                copy(a, 1 + j, (*chip, c), me).wait_recv()
                fwd = copy(a, 4 + j, (*chip, c), sibling)
                fwd.start()
                passed.append(fwd)
        for a in range(na):
            copy(a, 0, sibling, me).wait_recv()
            for j, chip in enumerate(chips):
                copy(a, 4 + j, (*chip, 1 - c), me).wait_recv()
        for cp in first + passed:
            cp.wait_send()
        for cp in mine:
            cp.wait()

    outs = pl.pallas_call(
        body, name=name, in_specs=[_ANY] * na, out_specs=[_ANY] * na,
        out_shape=[_sds((N_DEV,) + b.shape, b.dtype) for b in blocks],
        scratch_shapes=[pltpu.SemaphoreType.DMA((7 * na,)), pltpu.SemaphoreType.DMA((7 * na,)),
                        pltpu.SemaphoreType.DMA((na,))],
    )(*blocks)
    return list(outs)


_HBM = pl.BlockSpec(memory_space=pltpu.HBM)
_SEM = pl.BlockSpec(memory_space=pltpu.SEMAPHORE)
_EFFECT = pltpu.SideEffectType.DATAFLOW_SIDE_EFFECTING


def _peers(x, y, c):
    out = []
    for k in range(1, N_DEV):
        out.append((1 - x if k & 4 else x, 1 - y if k & 2 else y, 1 - c if k & 1 else c))
    return out


def _own_slots(srcs, scatter, *, name, after=None):
    na = len(srcs)
    n_extra = 0 if after is None else 1
    me = (4 * lax.axis_index("x") + 2 * lax.axis_index("y") + lax.axis_index("c")).astype(jnp.int32).reshape(1)

    def body(me_ref, *refs):
        in_refs, out_refs = refs[:na], refs[na + n_extra:]
        for a in range(na):
            out_refs[a][0] = in_refs[a][0] if scatter else in_refs[a][...]

    def slot_spec(shard):
        zeros = (0,) * len(shard)
        return pl.BlockSpec((1,) + tuple(shard), lambda i, me_ref: (me_ref[0],) + zeros)

    def whole_spec(shape):
        zeros = (0,) * len(shape)
        return pl.BlockSpec(tuple(shape), lambda i, me_ref: zeros)

    shards = [s.shape[1:] if scatter else s.shape for s in srcs]
    in_specs = [slot_spec(sh) if scatter else whole_spec(sh) for sh in shards] + [_ANY] * n_extra
    outs = pl.pallas_call(
        body, name=name,
        grid_spec=pltpu.PrefetchScalarGridSpec(
            num_scalar_prefetch=1, grid=(1,), in_specs=in_specs, out_specs=[slot_spec(sh) for sh in shards]),
        out_shape=[_sds((N_DEV,) + tuple(sh), s.dtype) for sh, s in zip(shards, srcs)],
        compiler_params=_cp(("arbitrary",)),
    )(me, *srcs, *([] if after is None else [after]))
    return list(outs)


_N_COPIES = dict(scatter=7, gather=7, chips=4, forward=3)


def _exchange_copies(src_refs, land_refs, send_sems, recv_sems, mode):
    x, y, c = lax.axis_index("x"), lax.axis_index("y"), lax.axis_index("c")
    me = 4 * x + 2 * y + c
    nc = _N_COPIES[mode]
    chips = [(1 - x, y), (x, 1 - y), (1 - x, 1 - y)]
    cps = []
    for a in range(len(land_refs)):
        if mode in ("scatter", "gather"):
            plan = [((src_refs[a].at[4 * px + 2 * py + pc] if mode == "scatter" else src_refs[a]),
                     land_refs[a].at[me], (px, py, pc)) for px, py, pc in _peers(x, y, c)]
        elif mode == "chips":
            plan = [(src_refs[a], land_refs[a].at[me], to) for to in [(x, y, 1 - c)] + [(*ch, c) for ch in chips]]
        else:
            plan = [(land_refs[a].at[4 * px + 2 * py + c], land_refs[a].at[4 * px + 2 * py + c], (x, y, 1 - c))
                    for px, py in chips]
        for k, (src, dst, to) in enumerate(plan):
            cps.append(pltpu.make_async_remote_copy(
                src_ref=src, dst_ref=dst, send_sem=send_sems.at[nc * a + k], recv_sem=recv_sems.at[nc * a + k],
                device_id=to, device_id_type=MESH_ID))
    return cps


def _exchange_start(srcs, lands, mode, *, name):
    ns, nz = len(srcs), len(lands)
    nsem = _N_COPIES[mode] * nz

    def body(*refs):
        src_refs, land_refs = refs[:ns], refs[ns:ns + nz]
        send_sems, recv_sems = refs[ns + nz], refs[ns + nz + 1]
        token = refs[-1]
        for cp in _exchange_copies(src_refs, land_refs, send_sems, recv_sems, mode):
            cp.start()
        token[...] = jnp.zeros(token.shape, token.dtype)

    hbm = lambda a: pltpu.HBM(a.shape, a.dtype)
    outs = pl.pallas_call(
        body, name=name,
        out_shape=(pltpu.SemaphoreType.DMA((nsem,)), pltpu.SemaphoreType.DMA((nsem,)),
                   *[hbm(a) for a in srcs], *[hbm(a) for a in lands], _sds((8, LANE), F32)),
        in_specs=[_HBM] * (ns + nz),
        out_specs=(_SEM, _SEM, *[_HBM] * (ns + nz), pl.BlockSpec(memory_space=pltpu.VMEM)),
        input_output_aliases={i: 2 + i for i in range(ns + nz)},
        compiler_params=pltpu.CompilerParams(has_side_effects=_EFFECT),
    )(*[pltpu.with_memory_space_constraint(a, pltpu.HBM) for a in list(srcs) + list(lands)])
    return outs[0], outs[1], list(outs[2:2 + ns]), list(outs[2 + ns:2 + ns + nz]), outs[-1]


def _exchange_wait(send_sems, recv_sems, srcs, lands, after, mode, *, name):
    ns, nz = len(srcs), len(lands)

    def body(*refs):
        src_refs, land_refs = refs[:ns], refs[ns:ns + nz]
        s_sems, r_sems = refs[ns + nz], refs[ns + nz + 1]
        for cp in _exchange_copies(src_refs, land_refs, s_sems, r_sems, mode):
            cp.wait_send()
            cp.wait_recv()

    hbm = lambda a: pltpu.HBM(a.shape, a.dtype)
    outs = pl.pallas_call(
        body, name=name,
        out_shape=(*[hbm(a) for a in srcs], *[hbm(a) for a in lands]),
        in_specs=[_HBM] * (ns + nz) + [_SEM, _SEM, _ANY],
        out_specs=tuple([_HBM] * (ns + nz)),
        input_output_aliases={i: i for i in range(ns + nz)},
        compiler_params=pltpu.CompilerParams(has_side_effects=_EFFECT),
    )(*srcs, *lands, send_sems, recv_sems, after)
    return list(outs[ns:])


_WIN_SEGS = (("ql", 0, Q_LORA, SEG_QL[0]), ("kvl", Q_LORA, KV_LORA, SEG_KVL[0]),
             ("kr", Q_LORA + KV_LORA, ROPE, SEG_KR[0]), ("mg", Q_LORA + KV_LORA + ROPE, D_MLA, SEG_MG[0]),
             ("ci", Q_LORA + KV_LORA + ROPE + D_MLA, 2 * D_CONV, SEG_CI[0]),
             ("cg", Q_LORA + KV_LORA + ROPE + D_MLA + 2 * D_CONV, D_CONV, SEG_CG[0]))
_WIN_SHARD = IN_COLS // N_DEV


def _win_pieces():
    out = []
    for _, o, n, new in _WIN_SEGS:
        for j in range(N_DEV):
            lo, hi = max(o, j * _WIN_SHARD), min(o + n, (j + 1) * _WIN_SHARD)
            if lo < hi:
                out.append((j, lo - j * _WIN_SHARD, new + lo - o, hi - lo))
    return out


WIN_T = 512


def _win_assemble(w_all, *, name):
    d = w_all.shape[2]
    t = min(WIN_T, d)
    pieces = sorted(_win_pieces(), key=lambda p: p[2])
    assert all(lo % 8 == 0 and n % 8 == 0 for _, lo, _, n in pieces)

    def body(w_ref, o_ref):
        rows = [w_ref[j].astype(F32)[lo:lo + n, :] for j, lo, _, n in pieces]
        rows.append(jnp.zeros((IN_PAD - (SEG_KR[0] + ROPE), t), F32))
        o_ref[...] = jnp.concatenate(rows, axis=0).astype(o_ref.dtype)

    return pl.pallas_call(
        body, name=name, grid=(d // t,),
        in_specs=[pl.BlockSpec((N_DEV, _WIN_SHARD, t), lambda i: (0, 0, i))],
        out_specs=pl.BlockSpec((IN_PAD, t), lambda i: (0, i)), out_shape=_sds((IN_PAD, d), w_all.dtype),
        compiler_params=_cp(("parallel",)),
    )(w_all)


def _win_split(grad, *, name):
    d = grad.shape[1]
    t = min(WIN_T, d)
    by_shard = [sorted([p for p in _win_pieces() if p[0] == j], key=lambda p: p[1]) for j in range(N_DEV)]

    def body(g_ref, o_ref):
        for j in range(N_DEV):
            rows = [g_ref[new:new + n, :] for _, _, new, n in by_shard[j]]
            o_ref[j] = jnp.concatenate(rows, axis=0).astype(o_ref.dtype)

    return pl.pallas_call(
        body, name=name, grid=(d // t,),
        in_specs=[pl.BlockSpec((IN_PAD, t), lambda i: (0, i))],
        out_specs=pl.BlockSpec((N_DEV, _WIN_SHARD, t), lambda i: (0, 0, i)),
        out_shape=_sds((N_DEV, _WIN_SHARD, d), WIRE_DTYPE),
        compiler_params=_cp(("parallel",)),
    )(grad)


def _cols_to_shards(a):
    r, n = a.shape
    return a.reshape(r, N_DEV, n // N_DEV).transpose(1, 0, 2)


def _shards_to_cols(a):
    nd, r, w = a.shape
    return a.transpose(1, 0, 2).reshape(r, nd * w)


def _win_permute(w_in):
    o_ql, o_kvl, o_kr, o_mg = 0, Q_LORA, Q_LORA + KV_LORA, Q_LORA + KV_LORA + ROPE
    o_ci = o_mg + D_MLA
    o_cg = o_ci + 2 * D_CONV
    seg = lambda o, n: w_in[:, o:o + n]
    pad = jnp.zeros((w_in.shape[0], IN_PAD - (SEG_KR[0] + ROPE)), w_in.dtype)
    return jnp.concatenate([seg(o_ci, 2 * D_CONV), seg(o_mg, D_MLA), seg(o_cg, D_CONV), seg(o_ql, Q_LORA),
                            seg(o_kvl, KV_LORA), seg(o_kr, ROPE), pad], axis=1)


def _win_unpermute(g):
    seg = lambda s, n=None: g[:, s[0]:s[0] + (s[1] if n is None else n)]
    return jnp.concatenate([seg(SEG_QL), seg(SEG_KVL), seg(SEG_KR, ROPE), seg(SEG_MG), seg(SEG_CI), seg(SEG_CG)], axis=1)


def _qup_permute(w):
    w3 = w.reshape(w.shape[0], N_HEADS, QK_DIM)
    nope = w3[:, :, :NOPE].reshape(w.shape[0], N_HEADS * NOPE)
    rope = jnp.pad(w3[:, :, NOPE:], ((0, 0), (0, 0), (0, LANE - ROPE))).reshape(w.shape[0], N_HEADS * LANE)
    return jnp.concatenate([nope, rope], axis=1)


def _qup_unpermute(g):
    r = g.shape[0]
    nope = g[:, :N_HEADS * NOPE].reshape(r, N_HEADS, NOPE)
    rope = g[:, N_HEADS * NOPE:].reshape(r, N_HEADS, LANE)[:, :, :ROPE]
    return jnp.concatenate([nope, rope], axis=2).reshape(r, N_HEADS * QK_DIM)


def _norm_tiles(g):
    return g[:NOPE].reshape(1, LANE), jnp.pad(g[NOPE:], (0, LANE - ROPE)).reshape(1, LANE)


def _norm_untile(gt):
    return jnp.concatenate([gt[0, :NOPE], gt[0, LANE:LANE + ROPE]])


def _rope_tiles(positions):
    inv_freq = 1.0 / (ROPE_THETA ** (jnp.arange(0, ROPE, 2, dtype=F32) / ROPE))
    ang = positions.astype(F32)[:, None] * inv_freq
    cos, sin = jnp.cos(ang), jnp.sin(ang)
    zq = jnp.zeros_like(cos)
    c_t = jnp.concatenate([cos, cos, zq, zq], axis=1)
    s1_t = jnp.concatenate([-sin, zq, zq, zq], axis=1)
    s2_t = jnp.concatenate([zq, sin, zq, zq], axis=1)
    return c_t, s1_t, s2_t


_BIG = ("w_in", "w_q_up", "w_kv_up", "w_pw", "w_out")
_COL_SHARDED = ("w_in", "w_q_up", "w_kv_up")


def _pack_rows(arrs):
    return jnp.concatenate([a.reshape(-1, LANE) for a in arrs], axis=0)


def _unpack_rows(buf, shapes):
    out, r0 = [], 0
    lead = buf.shape[:-2]
    for shp in shapes:
        n = math.prod(shp) // LANE
        out.append(buf[..., r0:r0 + n, :].reshape(lead + tuple(shp)))
        r0 += n
    return out


_SMALL = (("dmod", 3 * D_MODEL), ("norm_g", D_MODEL), ("q_lat_g", Q_LORA), ("kv_lat_g", KV_LORA),
          ("q_norm_g", 2 * LANE), ("k_norm_g", 2 * LANE), ("glu_b", 2 * D_CONV), ("dw_w", HALO * D_CONV),
          ("dw_b", D_CONV), ("conv_ln_g", D_CONV), ("conv_ln_b", D_CONV), ("b_pw", D_CONV))


def _layer_fwd(x, p, rope, l, early=None, late=None):
    n = lambda s: f"{s}_l{l}"
    c_t, s1_t, s2_t = rope
    h = _prenorm(x, p["norm_g"], p["shift"], p["sc1p"], name=n("prenorm"))
    if early is not None:
        p = {**p, **early(h)}
    z = _mm(h, p["w_in"], tb=True, name=n("in_proj"), tn=IN_TILE, n_outer=True,
            after=p.get("in_proj_after"))
    if late is not None:
        p = {**p, **late(z)}
    qn, kn = _lat_norm(z, p["q_lat_g"], p["kv_lat_g"], name=n("lat_norm"))
    q_raw = _mm(qn, p["w_q_up"], name=n("q_up"), tn=1024)
    kv = _mm(kn, p["w_kv_up"], name=n("kv_up"), tn=1024)
    qf, kf, vf = _qk_prep(q_raw, kv, z, c_t, s1_t, s2_t, *p["qk_tiles"], name=n("qk_prep"))
    o, lse = _flash_fwd(qf, kf, vf, name=n("flash_fwd"))
    u1, u3 = _conv_fwd(z, p["glu_b"], p["dw_w"], p["dw_b"], p["conv_ln_g"], p["conv_ln_b"], name=n("conv_fwd"))
    u4m = _mm(u3, p["w_pw"], name=n("pw"), tn=1024)
    cat = _gate_cat(o, z, u4m, p["b_pw"], name=n("gate_cat"))
    y, x_next = _mm(cat, p["w_out"], name=n("out_proj"), tn=1024, residual=(x, p["gate"]))
    saved = dict(x=x, h=h, z=z, qn=qn, kn=kn, q_raw=q_raw, kv=kv, qf=qf, kf=kf, vf=vf, o=o, lse=lse,
                 u1=u1, u3=u3, u4m=u4m, cat=cat, y=y)
    return x_next, saved, p


def _layer_bwd(gxo, p, sv, rope, l, hook_rest=None, hook_w_in=None):
    n = lambda s: f"{s}_l{l}"
    c_t, s1_t, s2_t = rope
    z = sv["z"]
    dy, dgate = _out_bwd(gxo, sv["y"], p["gate"], name=n("out_bwd"))
    g_w_out = _mm(sv["cat"], dy, ta=True, name=n("g_w_out"), tm=1024, tn=1024)
    do, delta, du4, g_b_pw, dz = _gate_bwd(dy, p["w_out"], sv["o"], z, sv["u4m"], p["b_pw"], name=n("gate_bwd"))
    g_w_pw = _mm(sv["u3"], du4, ta=True, name=n("g_w_pw"), tm=1024, tn=1024, tk=512)
    du3 = _mm(du4, p["w_pw"], tb=True, name=n("d_u3"), tn=1024)
    dz, g_ln_g, g_ln_b, g_dw_b, g_glu_b, g_dw_w = _conv_bwd(
        du3, sv["u1"], z, dz, p["glu_b"], p["dw_w"], p["conv_ln_g"], p["conv_ln_b"], name=n("conv_bwd"))
    t_att = min(ATT_T, z.shape[0])
    to_lanes = lambda a: a.reshape(N_HEADS, z.shape[0] // t_att, 1, t_att)
    dqf, dkf, dvf = _flash_bwd(sv["qf"], sv["kf"], sv["vf"], do,
                               to_lanes(sv["lse"][:, :, 0]), to_lanes(delta), name=n("flash_bwd"))
    dq_raw, dkv, dkr, g_qn, g_kn = _qk_bwd(dqf, dkf, dvf, sv["q_raw"], sv["kv"], z, c_t, s1_t, s2_t,
                                            *p["qk_tiles"], name=n("qk_bwd"))
    g_w_q_up = _mm(sv["qn"], dq_raw, ta=True, name=n("g_w_q_up"), tm=512, tn=1024, tk=512)
    dqn = _mm(dq_raw, p["w_q_up"], tb=True, name=n("d_qn"))
    g_w_kv_up = _mm(sv["kn"], dkv, ta=True, name=n("g_w_kv_up"), tm=256, tn=1024, tk=512)
    dkn = _mm(dkv, p["w_kv_up"], tb=True, name=n("d_kn"))
    dz, g_ql, g_kvl = _lat_bwd(dqn, dkn, dkr, z, dz, p["q_lat_g"], p["kv_lat_g"], name=n("lat_bwd"))
    big = dict(w_q_up=g_w_q_up, w_kv_up=g_w_kv_up, w_pw=g_w_pw, w_out=g_w_out)
    after = None if hook_rest is None else hook_rest(big)
    g_w_in = _mm(dz, sv["h"], ta=True, name=n("g_w_in"), tm=512, tn=1024, after=after)
    big["w_in"] = g_w_in
    after = None if hook_w_in is None else hook_w_in(g_w_in)
    dh = _mm(dz, p["w_in"], name=n("d_h"), tn=1024, after=after)
    dx, dshift, dscale, g_norm = _prenorm_bwd(dh, sv["x"], gxo, p["norm_g"], p["sc1p"], name=n("prenorm_bwd"))
    small = dict(dmod=jnp.concatenate([dshift, dscale, dgate], axis=1), norm_g=g_norm, q_lat_g=g_ql, kv_lat_g=g_kvl,
                 q_norm_g=g_qn, k_norm_g=g_kn, glu_b=g_glu_b, dw_w=g_dw_w, dw_b=g_dw_b,
                 conv_ln_g=g_ln_g, conv_ln_b=g_ln_b, b_pw=g_b_pw)
    return dx, big, small


def _layer_params(l, full, mod_l, small):
    d = D_MODEL
    row = lambda a: a.reshape(1, -1)
    shift, scale, gate = mod_l[:, :d], mod_l[:, d:2 * d], mod_l[:, 2 * d:]
    dw_w = jnp.pad(full["dw_w"][l], ((0, HALO - CONV_K), (0, 0)))
    return dict(
        shift=shift, sc1p=1.0 + scale, gate=gate, norm_g=row(small["norm_g"][l]),
        **{k: full[k][l] for k in _BIG if k in full}, dw_w=dw_w,
        q_lat_g=row(small["q_lat_g"][l]), kv_lat_g=row(small["kv_lat_g"][l]),
        qk_tiles=_norm_tiles(small["q_norm_g"][l]) + _norm_tiles(small["k_norm_g"][l]),
        glu_b=row(small["glu_b"][l]), dw_b=row(small["dw_b"][l]), conv_ln_g=row(small["conv_ln_g"][l]),
        conv_ln_b=row(small["conv_ln_b"][l]), b_pw=row(small["b_pw"][l]))


def kernel(x, c, positions, ada_w, ada_b, norm_g, w_in, q_lat_g, w_q_up, kv_lat_g, w_kv_up, q_norm_g, k_norm_g, glu_b, dw_w, dw_b, conv_ln_g, conv_ln_b, w_pw, b_pw, w_out, loss_target, m_ada_w, m_ada_b, m_norm_g, m_w_in, m_q_lat_g, m_w_q_up, m_kv_lat_g, m_w_kv_up, m_q_norm_g, m_k_norm_g, m_glu_b, m_dw_w, m_dw_b, m_conv_ln_g, m_conv_ln_b, m_w_pw, m_b_pw, m_w_out, v_ada_w, v_ada_b, v_norm_g, v_w_in, v_q_lat_g, v_w_q_up, v_kv_lat_g, v_w_kv_up, v_q_norm_g, v_k_norm_g, v_glu_b, v_dw_w, v_dw_b, v_conv_ln_g, v_conv_ln_b, v_w_pw, v_b_pw, v_w_out):
    names = ("ada_w", "ada_b", "norm_g", "w_in", "q_lat_g", "w_q_up", "kv_lat_g", "w_kv_up", "q_norm_g",
             "k_norm_g", "glu_b", "dw_w", "dw_b", "conv_ln_g", "conv_ln_b", "w_pw", "b_pw", "w_out")
    w_loc = dict(zip(names, (ada_w, ada_b, norm_g, w_in, q_lat_g, w_q_up, kv_lat_g, w_kv_up, q_norm_g, k_norm_g,
                             glu_b, dw_w, dw_b, conv_ln_g, conv_ln_b, w_pw, b_pw, w_out)))
    m_loc = dict(zip(names, (m_ada_w, m_ada_b, m_norm_g, m_w_in, m_q_lat_g, m_w_q_up, m_kv_lat_g, m_w_kv_up,
                             m_q_norm_g, m_k_norm_g, m_glu_b, m_dw_w, m_dw_b, m_conv_ln_g, m_conv_ln_b, m_w_pw,
                             m_b_pw, m_w_out)))
    v_loc = dict(zip(names, (v_ada_w, v_ada_b, v_norm_g, v_w_in, v_q_lat_g, v_w_q_up, v_kv_lat_g, v_w_kv_up,
                             v_q_norm_g, v_k_norm_g, v_glu_b, v_dw_w, v_dw_b, v_conv_ln_g, v_conv_ln_b, v_w_pw,
                             v_b_pw, v_w_out)))
    nl, d = N_LAYERS, D_MODEL
    me = 4 * lax.axis_index("x") + 2 * lax.axis_index("y") + lax.axis_index("c")
    x2, tgt = x[0], loss_target[0]
    ada_cols = ada_w.shape[-1]

    tr = lambda a: jnp.swapaxes(a, 1, 2)
    w_loc, m_loc, v_loc = ({**dd, "w_in": tr(dd["w_in"])} for dd in (w_loc, m_loc, v_loc))
    w_in0 = [w_loc["w_in"][0].astype(WIRE_DTYPE)]
    fly_c = _exchange_start(w_in0, _own_slots(w_in0, False, name="own_w_in_l0"), "chips", name="gather_start_w_in_l0")
    held = dict(c=c, positions=positions, ada_b=ada_b, norm_g=norm_g, q_lat_g=q_lat_g, kv_lat_g=kv_lat_g,
                q_norm_g=q_norm_g, k_norm_g=k_norm_g, glu_b=glu_b, dw_w=dw_w, dw_b=dw_b, conv_ln_g=conv_ln_g,
                conv_ln_b=conv_ln_b, b_pw=b_pw, big={k: w_loc[k] for k in _BIG})
    tok_c, held = lax.optimization_barrier((fly_c[4], held))
    c, positions, ada_b, norm_g, q_lat_g, kv_lat_g, q_norm_g, k_norm_g, glu_b, dw_w, dw_b, conv_ln_g, conv_ln_b, b_pw = (
        held[k] for k in ("c", "positions", "ada_b", "norm_g", "q_lat_g", "kv_lat_g", "q_norm_g", "k_norm_g", "glu_b",
                          "dw_w", "dw_b", "conv_ln_g", "conv_ln_b", "b_pw"))
    wire = {k: held["big"][k].astype(WIRE_DTYPE) for k in _BIG}

    dw_pad = jnp.pad(dw_w, ((0, 0), (0, HALO - CONV_K), (0, 0)))
    c_rows = c.reshape(d // LANE, LANE) + tok_c[0:1, :]
    c_all, dw_all = _all_gather([c_rows, dw_pad], name="gather_c")
    c_all = c_all.reshape(N_DEV, d)
    ada_b_cols = lax.dynamic_slice_in_dim(ada_b, me * ada_cols, ada_cols, axis=1).reshape(nl, 1, ada_cols)
    mod_cols = _ada_fwd(c_all, ada_w, ada_b_cols, name="ada_fwd")
    mod_all = _all_gather([mod_cols], name="gather_mod")[0]
    mod_me = lax.dynamic_index_in_dim(mod_all, me, axis=2, keepdims=False)
    mod = mod_me.transpose(1, 0, 2).reshape(nl, 1, N_DEV * ada_cols)

    from_chips = _exchange_wait(*fly_c[:4], mod, "chips", name="gather_wait_w_in_l0")
    fly_f = _exchange_start([], from_chips, "forward", name="forward_start_w_in_l0")
    w_in_all0 = _exchange_wait(*fly_f[:4], fly_f[4], "forward", name="forward_wait_w_in_l0")[0]
    rest0 = [wire[k][0] for k in _BIG[1:]]
    fly_r0, fly_w1 = {}, {}
    fly_r0["x"] = _exchange_start(rest0, _own_slots(rest0, False, name="own_weights_l0_rest", after=w_in_all0),
                                  "gather", name="gather_start_l0_rest")

    def layout_rest(parts):
        return dict(w_q_up=_qup_permute(_shards_to_cols(parts[0])), w_kv_up=_shards_to_cols(parts[1]),
                    w_pw=parts[2].reshape(D_CONV, D_CONV), w_out=parts[3].reshape(D_MLA + D_CONV, d))

    small_in = dict(norm_g=norm_g, q_lat_g=q_lat_g, kv_lat_g=kv_lat_g, q_norm_g=q_norm_g, k_norm_g=k_norm_g,
                    glu_b=glu_b, dw_b=dw_b, conv_ln_g=conv_ln_g, conv_ln_b=conv_ln_b, b_pw=b_pw)
    dw_full = [_shards_to_cols(dw_all[:, l])[:CONV_K] for l in range(nl)]
    rope = _rope_tiles(positions[0])

    def layer_params(l, w_in_all, rest, mod_l):
        full = dict(dw_w=dw_full)
        if w_in_all is not None:
            full["w_in"] = {l: _win_assemble(w_in_all, name=f"w_in_assemble_l{l}")}
        if rest is not None:
            full.update({k: {l: a} for k, a in layout_rest(rest).items()})
        return _layer_params(l, full, mod_l, small_in)

    def late_l0(z):
        parts = _exchange_wait(*fly_r0["x"][:4], z, "gather", name="gather_wait_l0_rest")
        src1 = [wire[k][1] for k in _BIG]
        fly_w1["x"] = _exchange_start(src1, _own_slots(src1, False, name="own_weights_l1", after=parts[0]), "gather",
                                      name="gather_start_l1")
        late = layout_rest(parts)
        late["q_lat_g"] = small_in["q_lat_g"][0].reshape(1, -1) + fly_w1["x"][4][0, 0]
        return late

    params, saved = [None] * nl, [None] * nl
    p0 = layer_params(0, w_in_all0, None, mod[0] + fly_r0["x"][4][0, 0])
    xs, saved[0], params[0] = _layer_fwd(x2, p0, rope, 0, late=late_l0)
    parts1 = _exchange_wait(*fly_w1["x"][:4], xs, "gather", name="gather_wait_l1")
    params[1] = layer_params(1, parts1[0], parts1[1:], mod[1])
    xs, saved[1], _ = _layer_fwd(xs, params[1], rope, 1)
    gx, loss_part = _loss_head(xs, tgt, name="loss_head")
    loss = lax.psum(loss_part[0, 0], ("x", "y", "c"))

    def shard_major(k, g):
        if k == "w_q_up":
            g = _qup_unpermute(g)
        if k in _COL_SHARDED:
            return _cols_to_shards(g)
        return g.reshape((N_DEV, g.shape[0] // N_DEV, g.shape[1]))

    def scatter_start(send, tag):
        lands = _own_slots(send, True, name=f"own_grads_{tag}")
        return _exchange_start(send, lands, "scatter", name=f"scatter_start_{tag}")

    def wire_rest(big):
        return [shard_major(k, big[k]).astype(WIRE_DTYPE) for k in _BIG[1:]]

    big_g, small_g, flying = [None] * nl, [None] * nl, {}
    gx, big_g[1], small_g[1] = _layer_bwd(gx, params[1], saved[1], rope, 1)
    flying["l1"] = scatter_start([_win_split(big_g[1]["w_in"], name="w_in_split_l1")] + wire_rest(big_g[1]), "l1")
    p0 = dict(params[0])
    p0["gate"] = p0["gate"] + flying["l1"][4][0, 0]

    def start_rest_l0(big):
        flying["l0_rest"] = scatter_start(wire_rest(big), "l0_rest")
        return flying["l0_rest"][4]

    res, arrived = {}, [None] * nl

    def start_w_in_l0(g_w_in):
        flying["l0_w_in"] = scatter_start([_win_split(g_w_in, name="w_in_split_l0")], "l0_w_in")
        tok = flying["l0_w_in"][4]
        arrived[1] = _exchange_wait(*flying["l1"][:4], tok, "scatter", name="scatter_wait_l1")
        arrived[0] = [None] + _exchange_wait(*flying["l0_rest"][:4], tok, "scatter", name="scatter_wait_l0_rest")
        for i, k in enumerate(_BIG):
            if i > 0:
                res[k] = _adamw([arrived[l][i] for l in range(nl)], w_loc[k], m_loc[k], v_loc[k], name=f"adamw_{k}")
        return res["w_out"][0]

    gx, big_g[0], small_g[0] = _layer_bwd(gx, p0, saved[0], rope, 0, hook_rest=start_rest_l0,
                                          hook_w_in=start_w_in_l0)

    tile = 8 * LANE
    padded = [(k, nn, -(-nn // tile) * tile) for k, nn in _SMALL]
    spk = jnp.concatenate([jnp.pad(small_g[l][k].reshape(-1), (0, np_ - nn)).reshape(-1, LANE)
                           for l in range(nl) for k, nn, np_ in padded], axis=0)
    s_all = _all_gather([spk], name="gather_small_grads")[0]
    s_rows = sum(np_ for _, _, np_ in padded) // LANE
    s_all = s_all.reshape(N_DEV, nl, s_rows, LANE)
    s_parts = {k: a[..., :nn] for (k, nn, _), a in
               zip(padded, _unpack_rows(s_all, [(np_,) for _, _, np_ in padded]))}

    dmod_all = s_parts["dmod"]
    dmod_cols = lax.dynamic_slice_in_dim(dmod_all, me * ada_cols, ada_cols, axis=2).transpose(1, 0, 2)
    g_ada_w = _ada_bwd(c_all.T, dmod_cols, name="ada_bwd")
    gp = {}
    gp["ada_w"] = g_ada_w[None]
    gp["ada_b"] = dmod_all
    for k in ("norm_g", "q_lat_g", "kv_lat_g", "glu_b", "dw_b", "conv_ln_g", "conv_ln_b", "b_pw"):
        gp[k] = s_parts[k]
    for k in ("q_norm_g", "k_norm_g"):
        t = s_parts[k]
        gp[k] = jnp.concatenate([t[..., :NOPE], t[..., LANE:LANE + ROPE]], axis=-1)
    dw_g = s_parts["dw_w"].reshape(N_DEV, nl, HALO, D_CONV)[:, :, :CONV_K]
    gp["dw_w"] = lax.dynamic_slice_in_dim(dw_g, me * LANE, LANE, axis=3)

    res["ada_w"] = _adamw(gp["ada_w"], w_loc["ada_w"], m_loc["ada_w"], v_loc["ada_w"], name="adamw_ada_w")
    small_names = [k for k in names if k not in _BIG and k != "ada_w"]
    res.update(zip(small_names, _adamw_small([(gp[k], w_loc[k], m_loc[k], v_loc[k]) for k in small_names],
                                             name="adamw_small")))
    arrived[0][0] = _exchange_wait(*flying["l0_w_in"][:4], res["ada_w"][1], "scatter", name="scatter_wait_l0_w_in")[0]
    w_in_res = _adamw([arrived[l][0] for l in range(nl)], w_loc["w_in"], m_loc["w_in"], v_loc["w_in"],
                      name="adamw_w_in")
    res["w_in"] = tuple(tr(a) for a in w_in_res)
    out = [loss, gx[None]]
    for idx in range(4):
        out += [res[k][idx] for k in names]
    return tuple(out)
```

```python
import functools
import math

import jax
import jax.numpy as jnp
from jax import lax
from jax.experimental import pallas as pl
from jax.experimental.pallas import tpu as pltpu

F32 = jnp.float32
MXU_DTYPE = jnp.bfloat16
WIRE_DTYPE = jnp.bfloat16

D_MODEL = 2048
N_LAYERS = 2
N_DEV = 8
N_HEADS = 8
NOPE = 128
ROPE = 64
V_DIM = 128
QK_DIM = NOPE + ROPE
Q_LORA = 512
KV_LORA = 256
D_MLA = N_HEADS * V_DIM
D_CONV = 1024
CONV_K = 31
ROPE_THETA = 10000.0
EPS = 1e-6
LANE = 128
HEAD_PAD = 2 * LANE
HALO = 32

SEG_CI = (0, 2 * D_CONV)
SEG_MG = (2 * D_CONV, D_MLA)
SEG_CG = (2 * D_CONV + D_MLA, D_CONV)
SEG_QL = (2 * D_CONV + D_MLA + D_CONV, Q_LORA)
SEG_KVL = (SEG_QL[0] + Q_LORA, KV_LORA)
SEG_KR = (SEG_KVL[0] + KV_LORA, LANE)
SEG_LAT = (SEG_QL[0], 1024)
IN_PAD = SEG_LAT[0] + SEG_LAT[1]
IN_TILE = IN_PAD // 4
assert SEG_KR[0] + LANE <= IN_PAD and SEG_LAT[0] % SEG_LAT[1] == 0
IN_COLS = Q_LORA + KV_LORA + ROPE + D_MLA + 2 * D_CONV + D_CONV

ADAM_LR = 0.001
ADAM_B1 = 0.9
ADAM_B2 = 0.999
ADAM_EPS = 1e-08
ADAM_WD = 0.01
ADAM_STEP = 10

VMEM_LIMIT = 56 * 1024 * 1024
ATT_T = 512
ROW_T = 256
CONV_T = 128
MESH_ID = pl.DeviceIdType.MESH


def _cp(sem=None):
    kw = dict(vmem_limit_bytes=VMEM_LIMIT)
    if sem is not None:
        kw["dimension_semantics"] = sem
    return pltpu.CompilerParams(**kw)


def _sds(shape, dtype):
    return jax.ShapeDtypeStruct(shape, dtype)


def _silu(x):
    return x * jax.nn.sigmoid(x)


def _dsilu(x):
    s = jax.nn.sigmoid(x)
    return s * (1.0 + x * (1.0 - s))


def _rowspec(t, width, col=0):
    return pl.BlockSpec((t, width), lambda i: (i, col))


def _vecspec(width):
    return pl.BlockSpec((1, width), lambda i: (0, 0))


def _colsum(v):
    return jnp.sum(v, axis=0, keepdims=True)


def _mm(a, b, *, name, ta=False, tb=False, out_dtype=F32, tm=512, tn=512, tk=None, n_outer=False, after=None,
        residual=None):
    if ta:
        kdim, m = a.shape
    else:
        m, kdim = a.shape
    if tb:
        n, k2 = b.shape
    else:
        k2, n = b.shape
    assert kdim == k2, (a.shape, b.shape)
    tm, tn = min(tm, m), min(tn, n)
    tk = kdim if tk is None else min(tk, kdim)
    assert m % tm == 0 and n % tn == 0 and kdim % tk == 0, (m, n, kdim, tm, tn, tk)
    nk = kdim // tk
    dims = (((0 if ta else 1,), (1 if tb else 0,)), ((), ()))

    n_extra = 0 if after is None else 1
    assert residual is None or nk == 1

    def body(a_ref, b_ref, *rest):
        if residual is not None:
            x_ref, gate_ref = rest[:2]
            rest = rest[2:]
        o_ref, scratch = rest[n_extra], rest[n_extra + 1:]
        prod = lax.dot_general(a_ref[...].astype(MXU_DTYPE), b_ref[...].astype(MXU_DTYPE), dims,
                               preferred_element_type=F32)
        if residual is not None:
            o_ref[...] = prod.astype(o_ref.dtype)
            scratch[0][...] = x_ref[...] + gate_ref[...] * prod
        elif nk == 1:
            o_ref[...] = prod.astype(o_ref.dtype)
        else:
            acc = scratch[0]
            k = pl.program_id(2)

            @pl.when(k == 0)
            def _():
                acc[...] = prod

            @pl.when(k > 0)
            def _():
                acc[...] += prod

            @pl.when(k == nk - 1)
            def _():
                o_ref[...] = acc[...].astype(o_ref.dtype)

    if n_outer:
        ij = lambda g0, g1: (g1, g0)
        grid = (n // tn, m // tm, nk)
    else:
        ij = lambda g0, g1: (g0, g1)
        grid = (m // tm, n // tn, nk)

    def a_map(g0, g1, k):
        i, _ = ij(g0, g1)
        return (k, i) if ta else (i, k)

    def b_map(g0, g1, k):
        _, j = ij(g0, g1)
        return (j, k) if tb else (k, j)

    def o_map(g0, g1, k):
        return ij(g0, g1)

    in_specs = [pl.BlockSpec((tk, tm) if ta else (tm, tk), a_map), pl.BlockSpec((tn, tk) if tb else (tk, tn), b_map)]
    operands = [a, b]
    out_specs, out_shape = pl.BlockSpec((tm, tn), o_map), _sds((m, n), out_dtype)
    if residual is not None:
        in_specs += [pl.BlockSpec((tm, tn), o_map), pl.BlockSpec((1, tn), lambda g0, g1, k: (0, ij(g0, g1)[1]))]
        operands += list(residual)
        out_specs, out_shape = [out_specs, pl.BlockSpec((tm, tn), o_map)], [out_shape, _sds((m, n), F32)]
    if after is not None:
        in_specs.append(_ANY)
        operands.append(after)
    return pl.pallas_call(
        body, name=name, grid=grid, in_specs=in_specs, out_specs=out_specs, out_shape=out_shape,
        scratch_shapes=[pltpu.VMEM((tm, tn), F32)] if nk > 1 else [],
        compiler_params=_cp(("parallel", "parallel", "arbitrary")),
    )(*operands)


def _prenorm(x, g, shift, sc1p, *, name):
    s, d = x.shape
    t = min(ROW_T, s)

    def body(x_ref, g_ref, sh_ref, sc_ref, h_ref):
        xv = x_ref[...]
        r = lax.rsqrt(jnp.mean(xv * xv, axis=-1, keepdims=True) + EPS)
        h_ref[...] = ((xv * r) * g_ref[...] * sc_ref[...] + sh_ref[...]).astype(h_ref.dtype)

    return pl.pallas_call(
        body, name=name, grid=(s // t,),
        in_specs=[_rowspec(t, d), _vecspec(d), _vecspec(d), _vecspec(d)],
        out_specs=_rowspec(t, d), out_shape=_sds((s, d), MXU_DTYPE),
        compiler_params=_cp(("parallel",)),
    )(x, g, shift, sc1p)


def _lat_norm(z, g_ql, g_kvl, *, name):
    s = z.shape[0]
    t = min(ROW_T, s)

    def body(ql_ref, kvl_ref, gq_ref, gk_ref, qn_ref, kn_ref):
        for src, g_ref, dst in ((ql_ref, gq_ref, qn_ref), (kvl_ref, gk_ref, kn_ref)):
            v = src[...]
            r = lax.rsqrt(jnp.mean(v * v, axis=-1, keepdims=True) + EPS)
            dst[...] = ((v * r) * g_ref[...]).astype(dst.dtype)

    return pl.pallas_call(
        body, name=name, grid=(s // t,),
        in_specs=[_rowspec(t, Q_LORA, SEG_QL[0] // Q_LORA), _rowspec(t, KV_LORA, SEG_KVL[0] // KV_LORA),
                  _vecspec(Q_LORA), _vecspec(KV_LORA)],
        out_specs=[_rowspec(t, Q_LORA), _rowspec(t, KV_LORA)],
        out_shape=[_sds((s, Q_LORA), MXU_DTYPE), _sds((s, KV_LORA), MXU_DTYPE)],
        compiler_params=_cp(("parallel",)),
    )(z, z, g_ql, g_kvl)


def _rope_fwd(r, c_t, s1_t, s2_t):
    return r * c_t + pltpu.roll(r, LANE - ROPE // 2, 1) * s1_t + pltpu.roll(r, ROPE // 2, 1) * s2_t


def _rope_bwd(d, c_t, s1_t, s2_t):
    return d * c_t + pltpu.roll(d * s1_t, ROPE // 2, 1) + pltpu.roll(d * s2_t, LANE - ROPE // 2, 1)


def _lanesum(v):
    return jnp.sum(v, axis=-1, keepdims=True)


def _qk_prep(q_raw, kv, z, c_t, s1_t, s2_t, gqn, gqr, gkn, gkr, *, name):
    s = q_raw.shape[0]
    t = min(ROW_T, s)
    scale = 1.0 / math.sqrt(QK_DIM)

    def body(q_ref, kv_ref, kr_ref, c_ref, s1_ref, s2_ref, gqn_ref, gqr_ref, gkn_ref, gkr_ref,
             qf_ref, kf_ref, vf_ref):
        c_v, s1_v, s2_v = c_ref[...], s1_ref[...], s2_ref[...]
        kr = kr_ref[...]
        kr_ss = _lanesum(kr * kr)
        for h in range(N_HEADS):
            n = q_ref[:, h * LANE:(h + 1) * LANE]
            r = q_ref[:, N_HEADS * LANE + h * LANE:N_HEADS * LANE + (h + 1) * LANE]
            rs = lax.rsqrt((_lanesum(n * n) + _lanesum(r * r)) * (1.0 / QK_DIM) + EPS)
            qf_ref[h, :, 0:LANE] = (((n * rs) * gqn_ref[...]) * scale).astype(qf_ref.dtype)
            rr = _rope_fwd((r * rs) * gqr_ref[...], c_v, s1_v, s2_v)
            qf_ref[h, :, LANE:HEAD_PAD] = (rr * scale).astype(qf_ref.dtype)

            n = kv_ref[:, h * 2 * LANE:h * 2 * LANE + LANE]
            rs = lax.rsqrt((_lanesum(n * n) + kr_ss) * (1.0 / QK_DIM) + EPS)
            kf_ref[h, :, 0:LANE] = ((n * rs) * gkn_ref[...]).astype(kf_ref.dtype)
            kf_ref[h, :, LANE:HEAD_PAD] = _rope_fwd((kr * rs) * gkr_ref[...], c_v, s1_v, s2_v).astype(kf_ref.dtype)
            vf_ref[h, :, 0:V_DIM] = kv_ref[:, h * 2 * LANE + LANE:(h + 1) * 2 * LANE].astype(vf_ref.dtype)
            vf_ref[h, :, V_DIM:] = jnp.ones((t, V_DIM), vf_ref.dtype)

    hspec = lambda w: pl.BlockSpec((N_HEADS, t, w), lambda i: (0, i, 0))
    return pl.pallas_call(
        body, name=name, grid=(s // t,),
        in_specs=[_rowspec(t, 2 * N_HEADS * LANE), _rowspec(t, 2 * N_HEADS * LANE),
                  _rowspec(t, LANE, SEG_KR[0] // LANE),
                  _rowspec(t, LANE), _rowspec(t, LANE), _rowspec(t, LANE),
                  _vecspec(LANE), _vecspec(LANE), _vecspec(LANE), _vecspec(LANE)],
        out_specs=[hspec(HEAD_PAD), hspec(HEAD_PAD), hspec(2 * V_DIM)],
        out_shape=[_sds((N_HEADS, s, HEAD_PAD), MXU_DTYPE), _sds((N_HEADS, s, HEAD_PAD), MXU_DTYPE),
                   _sds((N_HEADS, s, 2 * V_DIM), MXU_DTYPE)],
        compiler_params=_cp(("parallel",)),
    )(q_raw, kv, z, c_t, s1_t, s2_t, gqn, gqr, gkn, gkr)


def _causal_mask(t):
    row = lax.broadcasted_iota(jnp.int32, (t, t), 0)
    col = lax.broadcasted_iota(jnp.int32, (t, t), 1)
    return col <= row


NEG = -1e30


def _flash_fwd(qf, kf, va, *, name):
    nh, s, dk = qf.shape
    dv = va.shape[-1] // 2
    t = min(ATT_T, s)
    n = s // t
    assert dv == LANE and t % LANE == 0

    def body(q_ref, k_ref, v_ref, o_ref, lse_ref, m_s, acc_s, s_buf):
        i = pl.program_id(1)
        m_s[...] = jnp.full(m_s.shape, NEG, F32)
        acc_s[...] = jnp.zeros(acc_s.shape, F32)

        def rows_of(j):
            return pl.ds(pl.multiple_of(j * t, t), t)

        def scores(qi, j):
            return lax.dot_general(q_ref[0, rows_of(qi), :], k_ref[0, rows_of(j), :], (((1,), (1,)), ((), ())),
                                   preferred_element_type=F32)

        def consume(j, slot, masked):
            sc = s_buf[slot]
            if masked:
                sc = jnp.where(_causal_mask(t), sc, NEG)
            m_prev = m_s[...]
            m_new = jnp.maximum(m_prev, jnp.max(sc, axis=-1, keepdims=True))
            alpha = jnp.exp(m_prev - m_new)
            p = jnp.exp(sc - jnp.tile(m_new, (1, t // LANE)))
            acc_s[...] = jnp.tile(alpha, (1, 2)) * acc_s[...] + jnp.dot(
                p.astype(MXU_DTYPE), v_ref[0, rows_of(j), :], preferred_element_type=F32)
            m_s[...] = m_new

        nxt = jnp.minimum(i + 1, n - 1)

        @pl.when(i == 0)
        def _():
            s_buf[2] = scores(0, 0)
            consume(0, 2, True)
            s_buf[2] = scores(nxt, 0)

        @pl.when(i > 0)
        def _():
            s_buf[1] = scores(i, 1)
            consume(0, 2, False)

            def pair(a, carry):
                s_buf[0] = scores(i, 2 * a + 2)
                consume(2 * a + 1, 1, False)
                s_buf[1] = scores(i, 2 * a + 3)
                consume(2 * a + 2, 0, False)
                return carry

            lax.fori_loop(0, (i - 1) // 2, pair, 0)

            @pl.when(i % 2 == 1)
            def _():
                s_buf[2] = scores(nxt, 0)
                consume(i, 1, True)

            @pl.when(i % 2 == 0)
            def _():
                s_buf[0] = scores(i, i)
                consume(i - 1, 1, False)
                s_buf[2] = scores(nxt, 0)
                consume(i, 0, True)

        den = acc_s[:, dv:]
        o_ref[...] = acc_s[:, :dv] / den
        lse_ref[0] = m_s[...] + jnp.log(den)

    head = lambda h, i: (h, 0, 0)
    return pl.pallas_call(
        body, name=name, grid=(nh, n),
        in_specs=[pl.BlockSpec((1, s, dk), head), pl.BlockSpec((1, s, dk), head), pl.BlockSpec((1, s, 2 * dv), head)],
        out_specs=[pl.BlockSpec((t, dv), lambda h, i: (i, h)),
                   pl.BlockSpec((1, t, LANE), lambda h, i: (h, i, 0))],
        out_shape=[_sds((s, nh * dv), F32), _sds((nh, s, LANE), F32)],
        scratch_shapes=[pltpu.VMEM((t, LANE), F32), pltpu.VMEM((t, 2 * dv), F32), pltpu.VMEM((3, t, t), F32)],
        compiler_params=_cp(("arbitrary", "arbitrary")),
    )(qf, kf, va)


def _shifted_copies(ext_ref):
    rows = ext_ref.shape[1] - 8
    for s in range(1, 8):
        ext_ref[s, 0:rows, :] = ext_ref[0, s:s + rows, :]


def _windows(ext_ref, offsets, t_rows, lane0, lanes):
    for s in range(8):
        group = [o for o in offsets if o % 8 == s]
        if not group:
            continue
        lo, hi = min(group) - s, max(group) - s
        wide = ext_ref[s, pl.ds(lo, hi - lo + t_rows), lane0:lane0 + lanes]
        for o in group:
            yield o, wide[o - s - lo:o - s - lo + t_rows]


def _dw_taps(ext_ref, w_ref, row0, t_rows, lane0, lanes, first_off):
    acc = None
    for off, win in _windows(ext_ref, [row0 + first_off + k for k in range(CONV_K)], t_rows, lane0, lanes):
        k = off - row0 - first_off
        term = w_ref[k:k + 1, lane0:lane0 + lanes] * win
        acc = term if acc is None else acc + term
    return acc


CONV_RC = 32
CONV_LC = 256


def _conv_fwd(z, glu_b, dw_w, dw_b, ln_g, ln_b, *, name):
    s = z.shape[0]
    t = min(CONV_T, s)
    c2 = 2 * D_CONV
    hb = t // HALO

    def body(zm_ref, zh_ref, gb_ref, w_ref, wb_ref, g_ref, b_ref, u1_ref, u3_ref, ext):
        i = pl.program_id(0)

        def glu(zv):
            ci = zv + gb_ref[...]
            return ci[:, :D_CONV] * jax.nn.sigmoid(ci[:, D_CONV:])

        ext[0, HALO:, :] = glu(zm_ref[...])
        ext[0, 0:HALO, :] = jnp.where(i > 0, glu(zh_ref[...]), 0.0)
        _shifted_copies(ext)
        for rc in range(0, t, CONV_RC):
            for lc in range(0, D_CONV, CONV_LC):
                acc = _dw_taps(ext, w_ref, rc, CONV_RC, lc, CONV_LC, HALO - (CONV_K - 1))
                u1_ref[rc:rc + CONV_RC, lc:lc + CONV_LC] = acc + wb_ref[:, lc:lc + CONV_LC]
        u1 = u1_ref[...]
        mu = jnp.mean(u1, axis=-1, keepdims=True)
        cen = u1 - mu
        var = jnp.mean(cen * cen, axis=-1, keepdims=True)
        u2 = (cen * lax.rsqrt(var + EPS)) * g_ref[...] + b_ref[...]
        u3_ref[...] = _silu(u2).astype(u3_ref.dtype)

    return pl.pallas_call(
        body, name=name, grid=(s // t,),
        in_specs=[_rowspec(t, c2), pl.BlockSpec((HALO, c2), lambda i: (jnp.maximum(i * hb - 1, 0), 0)),
                  _vecspec(c2), pl.BlockSpec((HALO, D_CONV), lambda i: (0, 0)), _vecspec(D_CONV),
                  _vecspec(D_CONV), _vecspec(D_CONV)],
        out_specs=[_rowspec(t, D_CONV), _rowspec(t, D_CONV)],
        out_shape=[_sds((s, D_CONV), F32), _sds((s, D_CONV), MXU_DTYPE)],
        scratch_shapes=[pltpu.VMEM((8, t + HALO, D_CONV), F32)],
        compiler_params=_cp(("parallel",)),
    )(z, z, glu_b, dw_w, dw_b, ln_g, ln_b)


def _gate_cat(o, z, u4m, b_pw, *, name):
    s = o.shape[0]
    t = min(ROW_T, s)

    def body(o_ref, mg_ref, u4_ref, cg_ref, b_ref, cat_ref):
        cat_ref[:, :D_MLA] = (o_ref[...] * _silu(mg_ref[...])).astype(cat_ref.dtype)
        cat_ref[:, D_MLA:] = ((u4_ref[...] + b_ref[...]) * _silu(cg_ref[...])).astype(cat_ref.dtype)

    return pl.pallas_call(
        body, name=name, grid=(s // t,),
        in_specs=[_rowspec(t, D_MLA), _rowspec(t, D_MLA, SEG_MG[0] // D_MLA), _rowspec(t, D_CONV),
                  _rowspec(t, D_CONV, SEG_CG[0] // D_CONV), _vecspec(D_CONV)],
        out_specs=_rowspec(t, D_MLA + D_CONV), out_shape=_sds((s, D_MLA + D_CONV), MXU_DTYPE),
        compiler_params=_cp(("parallel",)),
    )(o, z, u4m, z, b_pw)


def _loss_head(xf, target, *, name):
    s, d = xf.shape
    t = min(ROW_T, s)

    def body(x_ref, t_ref, gx_ref, loss_ref):
        @pl.when(pl.program_id(0) == 0)
        def _():
            loss_ref[...] = jnp.zeros(loss_ref.shape, F32)

        err = x_ref[...] - t_ref[...]
        gx_ref[...] = err * (1.0 / d)
        loss_ref[...] += 0.5 * jnp.sum(_lanesum(err * err) * (1.0 / d), axis=0, keepdims=True)

    return pl.pallas_call(
        body, name=name, grid=(s // t,),
        in_specs=[_rowspec(t, d), _rowspec(t, d)],
        out_specs=[_rowspec(t, d), pl.BlockSpec((1, 1), lambda i: (0, 0))],
        out_shape=[_sds((s, d), F32), _sds((1, 1), F32)],
        compiler_params=_cp(("arbitrary",)),
    )(xf, target)


def _acc_init(refs):
    @pl.when(pl.program_id(0) == 0)
    def _():
        for r in refs:
            r[...] = jnp.zeros(r.shape, r.dtype)


def _out_bwd(gxo, y, gate, *, name):
    s, d = gxo.shape
    t = min(ROW_T, s)

    def body(g_ref, y_ref, gate_ref, dy_ref, dgate_ref):
        _acc_init([dgate_ref])
        gv = g_ref[...]
        dy_ref[...] = (gv * gate_ref[...]).astype(dy_ref.dtype)
        dgate_ref[...] += _colsum(gv * y_ref[...])

    return pl.pallas_call(
        body, name=name, grid=(s // t,),
        in_specs=[_rowspec(t, d), _rowspec(t, d), _vecspec(d)],
        out_specs=[_rowspec(t, d), _vecspec(d)],
        out_shape=[_sds((s, d), MXU_DTYPE), _sds((1, d), F32)],
        compiler_params=_cp(("arbitrary",)),
    )(gxo, y, gate)


def _gate_bwd(dy, w_out, o, z, u4m, b_pw, *, name):
    s, d = dy.shape
    t = min(2 * ROW_T, s)
    gates = D_MLA + D_CONV
    assert SEG_CG[0] == SEG_MG[0] + D_MLA and SEG_MG[0] % gates == 0

    def body(dy_ref, w_ref, o_ref, mg_ref, u4_ref, cg_ref, b_ref,
             do_ref, delta_ref, du4_ref, gb_ref, dz_ref):
        _acc_init([gb_ref])
        dcat = lax.dot_general(dy_ref[...], w_ref[...], (((1,), (1,)), ((), ())), preferred_element_type=F32)
        dm, ov, mg = dcat[:, :D_MLA], o_ref[...], mg_ref[...]
        do = dm * _silu(mg)
        do_ref[...] = do.astype(do_ref.dtype)
        dz_ref[:, :D_MLA] = (dm * ov * _dsilu(mg)).astype(dz_ref.dtype)
        prod = do * ov
        for h in range(N_HEADS):
            delta_ref[h] = _lanesum(prod[:, h * V_DIM:(h + 1) * V_DIM])
        dc, cg = dcat[:, D_MLA:], cg_ref[...]
        du4 = dc * _silu(cg)
        du4_ref[...] = du4.astype(du4_ref.dtype)
        dz_ref[:, D_MLA:] = (dc * (u4_ref[...] + b_ref[...]) * _dsilu(cg)).astype(dz_ref.dtype)
        gb_ref[...] += _colsum(du4)

    return pl.pallas_call(
        body, name=name, grid=(s // t,),
        in_specs=[_rowspec(t, d), pl.BlockSpec((gates, d), lambda i: (0, 0)), _rowspec(t, D_MLA),
                  _rowspec(t, D_MLA, SEG_MG[0] // D_MLA), _rowspec(t, D_CONV),
                  _rowspec(t, D_CONV, SEG_CG[0] // D_CONV), _vecspec(D_CONV)],
        out_specs=[_rowspec(t, D_MLA), pl.BlockSpec((N_HEADS, t, 1), lambda i: (0, i, 0)),
                   _rowspec(t, D_CONV), _vecspec(D_CONV), _rowspec(t, gates, SEG_MG[0] // gates)],
        out_shape=[_sds((s, D_MLA), MXU_DTYPE), _sds((N_HEADS, s, 1), F32),
                   _sds((s, D_CONV), MXU_DTYPE), _sds((1, D_CONV), F32), _sds((s, IN_PAD), MXU_DTYPE)],
        compiler_params=_cp(("arbitrary",)),
    )(dy, w_out, o, z, u4m, z, b_pw)


def _conv_bwd(du3, u1, z, dz, glu_b, dw_w, ln_g, ln_b, *, name):
    s = z.shape[0]
    t = min(CONV_T, s)
    c2 = 2 * D_CONV
    hb = t // HALO
    n_blk = s // t
    last_halo = s // HALO - 1

    def body(d3m_ref, d3h_ref, u1m_ref, u1h_ref, zm_ref, zh_ref, gb_ref, w_ref, g_ref, b_ref, dz_in_ref,
             dci_ref, gg_ref, gbn_ref, gwb_ref, ggb_ref, gw_ref, dext, uext, du0_s, gw_acc):
        i = pl.program_id(0)
        _acc_init([gg_ref, gbn_ref, gwb_ref, ggb_ref, gw_acc])

        def ln_bwd(d3, u1v):
            mu = jnp.mean(u1v, axis=-1, keepdims=True)
            cen = u1v - mu
            rstd = lax.rsqrt(jnp.mean(cen * cen, axis=-1, keepdims=True) + EPS)
            uh = cen * rstd
            d2 = d3 * _dsilu(uh * g_ref[...] + b_ref[...])
            dh = d2 * g_ref[...]
            d1 = rstd * (dh - jnp.mean(dh, axis=-1, keepdims=True) - uh * jnp.mean(dh * uh, axis=-1, keepdims=True))
            return d1, d2, uh

        d1, d2, uh = ln_bwd(d3m_ref[...], u1m_ref[...])
        gg_ref[...] += _colsum(d2 * uh)
        gbn_ref[...] += _colsum(d2)
        gwb_ref[...] += _colsum(d1)
        dext[0, 0:t, :] = d1
        d1h, _, _ = ln_bwd(d3h_ref[...], u1h_ref[...])
        dext[0, t:, :] = jnp.where(i < n_blk - 1, d1h, 0.0)
        _shifted_copies(dext)

        def glu_parts(zv):
            ci = zv + gb_ref[...]
            return ci[:, :D_CONV], jax.nn.sigmoid(ci[:, D_CONV:])

        val, sg = glu_parts(zm_ref[...])
        uext[0, HALO:, :] = val * sg
        valh, sgh = glu_parts(zh_ref[...])
        uext[0, 0:HALO, :] = jnp.where(i > 0, valh * sgh, 0.0)
        _shifted_copies(uext)

        for rc in range(0, t, CONV_RC):
            for lc in range(0, D_CONV, CONV_LC):
                acc = None
                for off, win in _windows(dext, [rc + k for k in range(CONV_K)], CONV_RC, lc, CONV_LC):
                    k = (CONV_K - 1) - (off - rc)
                    term = w_ref[k:k + 1, lc:lc + CONV_LC] * win
                    acc = term if acc is None else acc + term
                du0_s[rc:rc + CONV_RC, lc:lc + CONV_LC] = acc
                dchunk = dext[0, rc:rc + CONV_RC, lc:lc + CONV_LC]
                first = rc + HALO - (CONV_K - 1)
                for off, win in _windows(uext, [first + k for k in range(CONV_K)], CONV_RC, lc, CONV_LC):
                    k = off - first
                    pr = dchunk * win
                    part = pr[0:8]
                    for r8 in range(8, CONV_RC, 8):
                        part = part + pr[r8:r8 + 8]
                    gw_acc[k, :, lc:lc + CONV_LC] += part

        du0 = du0_s[...]
        dval = du0 * sg
        dgt = du0 * val * sg * (1.0 - sg)
        dci_ref[:, :D_CONV] = dval.astype(dci_ref.dtype)
        dci_ref[:, D_CONV:] = dgt.astype(dci_ref.dtype)
        ggb_ref[:, :D_CONV] += _colsum(dval)
        ggb_ref[:, D_CONV:] += _colsum(dgt)

        @pl.when(i == n_blk - 1)
        def _():
            gw_ref[...] = jnp.sum(gw_acc[...], axis=1)

    halo_next = lambda w: pl.BlockSpec((HALO, w), lambda i: (jnp.minimum((i + 1) * hb, last_halo), 0))
    return pl.pallas_call(
        body, name=name, grid=(n_blk,),
        in_specs=[_rowspec(t, D_CONV), halo_next(D_CONV), _rowspec(t, D_CONV), halo_next(D_CONV),
                  _rowspec(t, c2), pl.BlockSpec((HALO, c2), lambda i: (jnp.maximum(i * hb - 1, 0), 0)),
                  _vecspec(c2), pl.BlockSpec((HALO, D_CONV), lambda i: (0, 0)), _vecspec(D_CONV), _vecspec(D_CONV),
                  _ANY],
        out_specs=[_rowspec(t, c2, SEG_CI[0] // c2), _vecspec(D_CONV), _vecspec(D_CONV), _vecspec(D_CONV),
                   _vecspec(c2), pl.BlockSpec((HALO, D_CONV), lambda i: (0, 0))],
        out_shape=[_sds(dz.shape, dz.dtype), _sds((1, D_CONV), F32), _sds((1, D_CONV), F32), _sds((1, D_CONV), F32),
                   _sds((1, c2), F32), _sds((HALO, D_CONV), F32)],
        scratch_shapes=[pltpu.VMEM((8, t + HALO, D_CONV), F32), pltpu.VMEM((8, t + HALO, D_CONV), F32),
                        pltpu.VMEM((t, D_CONV), F32), pltpu.VMEM((HALO, 8, D_CONV), F32)],
        input_output_aliases={10: 0},
        compiler_params=_cp(("arbitrary",)),
    )(du3, du3, u1, u1, z, z, glu_b, dw_w, ln_g, ln_b, dz)


def _flash_bwd(qf, kf, va, do, lse_t, delta_t, *, name):
    nh, s, dk = qf.shape
    dv = va.shape[-1] // 2
    t = min(ATT_T, s)
    n = s // t
    nt = (((1,), (1,)), ((), ()))
    tn = (((0,), (0,)), ((), ()))

    def body(q_ref, do_ref, lse_ref, dl_ref, k_ref, v_ref, dq_ref, dk_ref, dv_ref,
             dk_s, dv_s, st_buf, dpt_buf):
        n_un = pl.program_id(1)
        j = n - 1 - n_un
        nxt = jnp.maximum(j - 1, 0)

        @pl.when(n_un == 0)
        def _():
            dq_ref[...] = jnp.zeros(dq_ref.shape, F32)

        dk_s[...] = jnp.zeros(dk_s.shape, F32)
        dv_s[...] = jnp.zeros(dv_s.shape, F32)

        def rows_at(blk):
            return pl.ds(pl.multiple_of(blk * t, t), t)

        def rows_of(b):
            return rows_at(n - 1 - b)

        k = k_ref[0, rows_at(j), :]

        def produce(kj, b, slot):
            rows = rows_of(b)
            st_buf[slot] = lax.dot_general(k_ref[0, rows_at(kj), :], q_ref[0, rows, :], nt,
                                           preferred_element_type=F32)
            dpt_buf[slot] = lax.dot_general(v_ref[0, rows_at(kj), 0:dv], do_ref[rows, :], nt,
                                            preferred_element_type=F32)

        def consume(b, slot, masked):
            i = n - 1 - b
            rows = rows_of(b)
            q, dov = q_ref[0, rows, :], do_ref[rows, :]
            pt = jnp.exp(st_buf[slot] - lse_ref[0, i])
            if masked:
                key = lax.broadcasted_iota(jnp.int32, (t, t), 0)
                qry = lax.broadcasted_iota(jnp.int32, (t, t), 1)
                pt = jnp.where(key <= qry, pt, 0.0)
            dv_s[...] += jnp.dot(pt.astype(MXU_DTYPE), dov, preferred_element_type=F32)
            dst = (pt * (dpt_buf[slot] - dl_ref[0, i])).astype(MXU_DTYPE)
            dk_s[...] += jnp.dot(dst, q, preferred_element_type=F32)
            dq_ref[0, rows, :] += lax.dot_general(dst, k, tn, preferred_element_type=F32)

        @pl.when(n_un == 0)
        def _():
            produce(j, 0, 2)
            consume(0, 2, True)
            produce(nxt, 0, 2)

        @pl.when(n_un > 0)
        def _():
            produce(j, 1, 1)
            consume(0, 2, False)

            def pair(a, carry):
                produce(j, 2 * a + 2, 0)
                consume(2 * a + 1, 1, False)
                produce(j, 2 * a + 3, 1)
                consume(2 * a + 2, 0, False)
                return carry

            lax.fori_loop(0, (n_un - 1) // 2, pair, 0)

            @pl.when(n_un % 2 == 1)
            def _():
                produce(nxt, 0, 2)
                consume(n_un, 1, True)

            @pl.when(n_un % 2 == 0)
            def _():
                produce(j, n_un, 0)
                consume(n_un - 1, 1, False)
                produce(nxt, 0, 2)
                consume(n_un, 0, True)

        dk_ref[0] = dk_s[...]
        dv_ref[0] = dv_s[...]

    head = lambda h, j: (h, 0, 0)
    rowv = pl.BlockSpec((1, n, 1, t), lambda h, j: (h, 0, 0, 0))
    return pl.pallas_call(
        body, name=name, grid=(nh, n),
        in_specs=[pl.BlockSpec((1, s, dk), head),
                  pl.BlockSpec((s, dv), lambda h, j: (0, h)),
                  rowv, rowv,
                  pl.BlockSpec((1, s, dk), head),
                  pl.BlockSpec((1, s, 2 * dv), head)],
        out_specs=[pl.BlockSpec((1, s, dk), head),
                   pl.BlockSpec((1, t, dk), lambda h, g: (h, n - 1 - g, 0)),
                   pl.BlockSpec((1, t, dv), lambda h, g: (h, n - 1 - g, 0))],
        out_shape=[_sds((nh, s, dk), F32), _sds((nh, s, dk), F32), _sds((nh, s, dv), F32)],
        scratch_shapes=[pltpu.VMEM((t, dk), F32), pltpu.VMEM((t, dv), F32),
                        pltpu.VMEM((3, t, t), F32), pltpu.VMEM((3, t, t), F32)],
        compiler_params=_cp(("arbitrary", "arbitrary")),
    )(qf, do, lse_t, delta_t, kf, va)


def _qk_bwd(dqf, dkf, dvf, q_raw, kv, z, c_t, s1_t, s2_t, gqn, gqr, gkn, gkr, *, name):
    s = q_raw.shape[0]
    t = min(ROW_T, s)
    scale = 1.0 / math.sqrt(QK_DIM)

    def body(dq_ref, dk_ref, dv_ref, q_ref, kv_ref, kr_ref, c_ref, s1_ref, s2_ref,
             gqn_ref, gqr_ref, gkn_ref, gkr_ref, dqr_ref, dkv_ref, dkr_ref, ggq_ref, ggk_ref):
        _acc_init([ggq_ref, ggk_ref])
        c_v, s1_v, s2_v = c_ref[...], s1_ref[...], s2_ref[...]
        kr = kr_ref[...]
        kr_ss = _lanesum(kr * kr)
        dkr = jnp.zeros(kr.shape, F32)
        ggq_n = ggq_r = ggk_n = ggk_r = jnp.zeros((1, LANE), F32)

        def norm_bwd(n, r, rs, dyn, dyr, gn, gr):
            nh_, rh_ = n * rs, r * rs
            dnh, drh = dyn * gn, dyr * gr
            dot = (_lanesum(dnh * nh_) + _lanesum(drh * rh_)) * (1.0 / QK_DIM)
            return rs * (dnh - nh_ * dot), rs * (drh - rh_ * dot), _colsum(dyn * nh_), _colsum(dyr * rh_)

        for h in range(N_HEADS):
            n = q_ref[:, h * LANE:(h + 1) * LANE]
            r = q_ref[:, N_HEADS * LANE + h * LANE:N_HEADS * LANE + (h + 1) * LANE]
            rs = lax.rsqrt((_lanesum(n * n) + _lanesum(r * r)) * (1.0 / QK_DIM) + EPS)
            dyn = dq_ref[h, :, 0:LANE] * scale
            dyr = _rope_bwd(dq_ref[h, :, LANE:HEAD_PAD] * scale, c_v, s1_v, s2_v)
            dn, dr, g_n, g_r = norm_bwd(n, r, rs, dyn, dyr, gqn_ref[...], gqr_ref[...])
            dqr_ref[:, h * LANE:(h + 1) * LANE] = dn.astype(dqr_ref.dtype)
            dqr_ref[:, N_HEADS * LANE + h * LANE:N_HEADS * LANE + (h + 1) * LANE] = dr.astype(dqr_ref.dtype)
            ggq_n, ggq_r = ggq_n + g_n, ggq_r + g_r

            n = kv_ref[:, h * 2 * LANE:h * 2 * LANE + LANE]
            rs = lax.rsqrt((_lanesum(n * n) + kr_ss) * (1.0 / QK_DIM) + EPS)
            dyn = dk_ref[h, :, 0:LANE]
            dyr = _rope_bwd(dk_ref[h, :, LANE:HEAD_PAD], c_v, s1_v, s2_v)
            dn, dr, g_n, g_r = norm_bwd(n, kr, rs, dyn, dyr, gkn_ref[...], gkr_ref[...])
            dkv_ref[:, h * 2 * LANE:h * 2 * LANE + LANE] = dn.astype(dkv_ref.dtype)
            dkv_ref[:, h * 2 * LANE + LANE:(h + 1) * 2 * LANE] = dv_ref[h].astype(dkv_ref.dtype)
            dkr = dkr + dr
            ggk_n, ggk_r = ggk_n + g_n, ggk_r + g_r

        dkr_ref[...] = dkr.astype(dkr_ref.dtype)
        ggq_ref[:, 0:LANE] += ggq_n
        ggq_ref[:, LANE:] += ggq_r
        ggk_ref[:, 0:LANE] += ggk_n
        ggk_ref[:, LANE:] += ggk_r

    hspec = lambda w: pl.BlockSpec((N_HEADS, t, w), lambda i: (0, i, 0))
    wide = 2 * N_HEADS * LANE
    return pl.pallas_call(
        body, name=name, grid=(s // t,),
        in_specs=[hspec(HEAD_PAD), hspec(HEAD_PAD), hspec(V_DIM), _rowspec(t, wide), _rowspec(t, wide),
                  _rowspec(t, LANE, SEG_KR[0] // LANE), _rowspec(t, LANE), _rowspec(t, LANE), _rowspec(t, LANE),
                  _vecspec(LANE), _vecspec(LANE), _vecspec(LANE), _vecspec(LANE)],
        out_specs=[_rowspec(t, wide), _rowspec(t, wide), _rowspec(t, LANE), _vecspec(2 * LANE), _vecspec(2 * LANE)],
        out_shape=[_sds((s, wide), MXU_DTYPE), _sds((s, wide), MXU_DTYPE), _sds((s, LANE), MXU_DTYPE),
                   _sds((1, 2 * LANE), F32), _sds((1, 2 * LANE), F32)],
        compiler_params=_cp(("arbitrary",)),
    )(dqf, dkf, dvf, q_raw, kv, z, c_t, s1_t, s2_t, gqn, gqr, gkn, gkr)


def _lat_bwd(dqn, dkn, dkr, z, dz, g_ql, g_kvl, *, name):
    s = z.shape[0]
    t = min(ROW_T, s)
    o_ql, o_kvl, o_kr = (seg[0] - SEG_LAT[0] for seg in (SEG_QL, SEG_KVL, SEG_KR))

    def body(dq_ref, dk_ref, dkr_ref, ql_ref, kvl_ref, gq_ref, gk_ref, dz_in_ref, dz_ref, ggq_ref, ggk_ref):
        _acc_init([ggq_ref, ggk_ref])
        for d_ref, src, g_ref, off, gg_ref in ((dq_ref, ql_ref, gq_ref, o_ql, ggq_ref),
                                               (dk_ref, kvl_ref, gk_ref, o_kvl, ggk_ref)):
            v, dy = src[...], d_ref[...]
            r = lax.rsqrt(jnp.mean(v * v, axis=-1, keepdims=True) + EPS)
            vh = v * r
            dvh = dy * g_ref[...]
            dz_ref[:, off:off + v.shape[1]] = (
                r * (dvh - vh * jnp.mean(dvh * vh, axis=-1, keepdims=True))).astype(dz_ref.dtype)
            gg_ref[...] += _colsum(dy * vh)
        dz_ref[:, o_kr:o_kr + LANE] = dkr_ref[...]
        dz_ref[:, o_kr + LANE:] = jnp.zeros((t, SEG_LAT[1] - o_kr - LANE), dz_ref.dtype)

    return pl.pallas_call(
        body, name=name, grid=(s // t,),
        in_specs=[_rowspec(t, Q_LORA), _rowspec(t, KV_LORA), _rowspec(t, LANE),
                  _rowspec(t, Q_LORA, SEG_QL[0] // Q_LORA), _rowspec(t, KV_LORA, SEG_KVL[0] // KV_LORA),
                  _vecspec(Q_LORA), _vecspec(KV_LORA), _ANY],
        out_specs=[_rowspec(t, SEG_LAT[1], SEG_LAT[0] // SEG_LAT[1]), _vecspec(Q_LORA), _vecspec(KV_LORA)],
        out_shape=[_sds(dz.shape, dz.dtype), _sds((1, Q_LORA), F32), _sds((1, KV_LORA), F32)],
        input_output_aliases={7: 0},
        compiler_params=_cp(("arbitrary",)),
    )(dqn, dkn, dkr, z, z, g_ql, g_kvl, dz)


def _prenorm_bwd(dh, x, gxo, g, sc1p, *, name):
    s, d = x.shape
    t = min(ROW_T, s)

    def body(dh_ref, x_ref, gx_ref, g_ref, sc_ref, dx_ref, dsh_ref, dsc_ref, gg_ref):
        _acc_init([dsh_ref, dsc_ref, gg_ref])
        xv, dhv = x_ref[...], dh_ref[...]
        r = lax.rsqrt(jnp.mean(xv * xv, axis=-1, keepdims=True) + EPS)
        xn = xv * r
        dsh_ref[...] += _colsum(dhv)
        dsc_ref[...] += _colsum(dhv * (xn * g_ref[...]))
        dm = dhv * sc_ref[...]
        gg_ref[...] += _colsum(dm * xn)
        dxn = dm * g_ref[...]
        dx_ref[...] = gx_ref[...] + r * (dxn - xn * jnp.mean(dxn * xn, axis=-1, keepdims=True))

    return pl.pallas_call(
        body, name=name, grid=(s // t,),
        in_specs=[_rowspec(t, d), _rowspec(t, d), _rowspec(t, d), _vecspec(d), _vecspec(d)],
        out_specs=[_rowspec(t, d), _vecspec(d), _vecspec(d), _vecspec(d)],
        out_shape=[_sds((s, d), F32), _sds((1, d), F32), _sds((1, d), F32), _sds((1, d), F32)],
        compiler_params=_cp(("arbitrary",)),
    )(dh, x, gxo, g, sc1p)


def _ada_fwd(c_all, ada_w, ada_b_cols, *, name):
    nl, d, cols = ada_w.shape

    def body(c_ref, w_ref, b_ref, o_ref):
        ca = _silu(c_ref[...]).astype(MXU_DTYPE)
        o_ref[0] = jnp.dot(ca, w_ref[0].astype(MXU_DTYPE), preferred_element_type=F32) + b_ref[0]

    return pl.pallas_call(
        body, name=name, grid=(nl,),
        in_specs=[pl.BlockSpec((N_DEV, d), lambda l: (0, 0)), pl.BlockSpec((1, d, cols), lambda l: (l, 0, 0)),
                  pl.BlockSpec((1, 1, cols), lambda l: (l, 0, 0))],
        out_specs=pl.BlockSpec((1, N_DEV, cols), lambda l: (l, 0, 0)),
        out_shape=_sds((nl, N_DEV, cols), F32),
        compiler_params=_cp(("parallel",)),
    )(c_all, ada_w, ada_b_cols)


def _ada_bwd(c_all_t, dmod_cols, *, name):
    nl, _, cols = dmod_cols.shape
    d = c_all_t.shape[0]

    def body(c_ref, dm_ref, o_ref):
        ca = _silu(c_ref[...]).astype(MXU_DTYPE)
        o_ref[0] = jnp.dot(ca, dm_ref[0].astype(MXU_DTYPE), preferred_element_type=F32)

    return pl.pallas_call(
        body, name=name, grid=(nl,),
        in_specs=[pl.BlockSpec((d, N_DEV), lambda l: (0, 0)), pl.BlockSpec((1, N_DEV, cols), lambda l: (l, 0, 0))],
        out_specs=pl.BlockSpec((1, d, cols), lambda l: (l, 0, 0)),
        out_shape=_sds((nl, d, cols), F32),
        compiler_params=_cp(("parallel",)),
    )(c_all_t, dmod_cols)


def _adamw_math(g, w, m, v):
    mn = ADAM_B1 * m + (1.0 - ADAM_B1) * g
    vn = ADAM_B2 * v + (1.0 - ADAM_B2) * (g * g)
    m_hat = mn / (1.0 - ADAM_B1 ** ADAM_STEP)
    v_hat = vn / (1.0 - ADAM_B2 ** ADAM_STEP)
    return -ADAM_LR * (m_hat / (jnp.sqrt(v_hat) + ADAM_EPS) + ADAM_WD * w), mn, vn


def _adamw_small(items, *, name):
    n = len(items)
    shapes = [it[1].shape for it in items]
    flat = lambda a, lead: a.reshape(lead + (-1, a.shape[-1]))
    operands = []
    for gp, w, m, v in items:
        operands += [flat(gp, (gp.shape[0],)), flat(w, ()), flat(m, ()), flat(v, ())]
    nparts = [it[0].shape[0] for it in items]

    def body(*refs):
        ins, outs = refs[:4 * n], refs[4 * n:]
        for i in range(n):
            g_ref, w_ref, m_ref, v_ref = ins[4 * i:4 * i + 4]
            g = g_ref[0].astype(F32)
            for p in range(1, nparts[i]):
                g = g + g_ref[p].astype(F32)
            outs[4 * i][...] = g
            outs[4 * i + 1][...], outs[4 * i + 2][...], outs[4 * i + 3][...] = _adamw_math(
                g, w_ref[...], m_ref[...], v_ref[...])

    out_shape = []
    for it in items:
        out_shape += [_sds(flat(it[1], ()).shape, F32)] * 4
    outs = pl.pallas_call(body, name=name, out_shape=out_shape, compiler_params=_cp())(*operands)
    return [tuple(o.reshape(shp) for o in outs[4 * i:4 * i + 4]) for i, shp in enumerate(shapes)]


def _adamw(gparts, w, m, v, *, name):
    shape = w.shape
    cols = shape[-1]
    per_layer = isinstance(gparts, (list, tuple))
    nl = shape[0] if per_layer else 1
    rows = w.size // cols // nl
    glist = list(gparts) if per_layer else [gparts]
    npart = glist[0].shape[0]
    glist = [g.reshape(npart, rows, cols) for g in glist]
    w3, m3, v3 = (a.reshape(nl, rows, cols) for a in (w, m, v))
    budget = 2 * 1024 * 1024
    fits = [t for t in range(min(rows, 256) // 8 * 8, 7, -8)
            if rows % t == 0 and npart * t * cols * glist[0].dtype.itemsize <= budget]
    t = fits[0] if fits else rows
    nb = rows // t

    def body(*refs):
        g_refs = refs[:nl]
        w_ref, m_ref, v_ref, go_ref, d_ref, mo_ref, vo_ref, g_s = refs[nl:]
        layer = pl.program_id(0)
        for l in range(nl):
            @pl.when(layer == l)
            def _(l=l):
                g = g_refs[l][0].astype(F32)
                for p in range(1, npart):
                    g = g + g_refs[l][p].astype(F32)
                g_s[...] = g

        g = g_s[...]
        go_ref[0] = g
        d_ref[0], mo_ref[0], vo_ref[0] = _adamw_math(g, w_ref[0], m_ref[0], v_ref[0])

    def g_map(l):
        return lambda layer, i: (0, jnp.where(layer == l, i, jnp.where(layer < l, 0, nb - 1)), 0)

    spec = pl.BlockSpec((1, t, cols), lambda layer, i: (layer, i, 0))
    outs = pl.pallas_call(
        body, name=name, grid=(nl, nb),
        in_specs=[pl.BlockSpec((npart, t, cols), g_map(l)) for l in range(nl)] + [spec, spec, spec],
        out_specs=[spec] * 4, out_shape=[_sds((nl, rows, cols), F32)] * 4,
        scratch_shapes=[pltpu.VMEM((t, cols), F32)],
        compiler_params=_cp(("arbitrary", "arbitrary")),
    )(*glist, w3, m3, v3)
    return tuple(o.reshape(shape) for o in outs)


_ANY = pl.BlockSpec(memory_space=pl.ANY)


def _all_gather(blocks, *, name):
    na = len(blocks)

    def body(*refs):
        x_refs, out_refs = refs[:na], refs[na:2 * na]
        send_sems, recv_sems, local_sems = refs[2 * na:]
        x, y, c = lax.axis_index("x"), lax.axis_index("y"), lax.axis_index("c")
        me, sibling = (x, y, c), (x, y, 1 - c)
        chips = [(1 - x, y), (x, 1 - y), (1 - x, 1 - y)]

        def slot(a, px, py, pc):
            return out_refs[a].at[4 * px + 2 * py + pc]

        def copy(a, k, blk, to, src=None):
            return pltpu.make_async_remote_copy(
                src_ref=slot(a, *blk) if src is None else src, dst_ref=slot(a, *blk),
                send_sem=send_sems.at[7 * a + k], recv_sem=recv_sems.at[7 * a + k],
                device_id=to, device_id_type=MESH_ID)

        mine = [pltpu.make_async_copy(x_refs[a], slot(a, *me), local_sems.at[a]) for a in range(na)]
        for cp in mine:
            cp.start()
        first = []
        for a in range(na):
            first.append(copy(a, 0, me, sibling, src=x_refs[a]))
            first += [copy(a, 1 + j, me, (*chip, c), src=x_refs[a]) for j, chip in enumerate(chips)]
        for cp in first:
            cp.start()
        passed = []
        for a in range(na):
            for j, chip in enumerate(chips):
                copy(a, 1 + j, (*chip, c), me).wait_recv()
                fwd = copy(a, 4 + j, (*chip, c), sibling)
                fwd.start()
                passed.append(fwd)
        for a in range(na):
            copy(a, 0, sibling, me).wait_recv()
            for j, chip in enumerate(chips):
                copy(a, 4 + j, (*chip, 1 - c), me).wait_recv()
        for cp in first + passed:
            cp.wait_send()
        for cp in mine:
            cp.wait()

    outs = pl.pallas_call(
        body, name=name, in_specs=[_ANY] * na, out_specs=[_ANY] * na,
        out_shape=[_sds((N_DEV,) + b.shape, b.dtype) for b in blocks],
        scratch_shapes=[pltpu.SemaphoreType.DMA((7 * na,)), pltpu.SemaphoreType.DMA((7 * na,)),
                        pltpu.SemaphoreType.DMA((na,))],
    )(*blocks)
    return list(outs)


_HBM = pl.BlockSpec(memory_space=pltpu.HBM)
_SEM = pl.BlockSpec(memory_space=pltpu.SEMAPHORE)
_EFFECT = pltpu.SideEffectType.DATAFLOW_SIDE_EFFECTING


def _peers(x, y, c):
    out = []
    for k in range(1, N_DEV):
        out.append((1 - x if k & 4 else x, 1 - y if k & 2 else y, 1 - c if k & 1 else c))
    return out


def _own_slots(srcs, scatter, *, name, after=None):
    na = len(srcs)
    n_extra = 0 if after is None else 1
    me = (4 * lax.axis_index("x") + 2 * lax.axis_index("y") + lax.axis_index("c")).astype(jnp.int32).reshape(1)

    def body(me_ref, *refs):
        in_refs, out_refs = refs[:na], refs[na + n_extra:]
        for a in range(na):
            out_refs[a][0] = in_refs[a][0] if scatter else in_refs[a][...]

    def slot_spec(shard):
        zeros = (0,) * len(shard)
        return pl.BlockSpec((1,) + tuple(shard), lambda i, me_ref: (me_ref[0],) + zeros)

    def whole_spec(shape):
        zeros = (0,) * len(shape)
        return pl.BlockSpec(tuple(shape), lambda i, me_ref: zeros)

    shards = [s.shape[1:] if scatter else s.shape for s in srcs]
    in_specs = [slot_spec(sh) if scatter else whole_spec(sh) for sh in shards] + [_ANY] * n_extra
    outs = pl.pallas_call(
        body, name=name,
        grid_spec=pltpu.PrefetchScalarGridSpec(
            num_scalar_prefetch=1, grid=(1,), in_specs=in_specs, out_specs=[slot_spec(sh) for sh in shards]),
        out_shape=[_sds((N_DEV,) + tuple(sh), s.dtype) for sh, s in zip(shards, srcs)],
        compiler_params=_cp(("arbitrary",)),
    )(me, *srcs, *([] if after is None else [after]))
    return list(outs)


_N_COPIES = dict(scatter=7, gather=7, chips=4, forward=3)


def _exchange_copies(src_refs, land_refs, send_sems, recv_sems, mode):
    x, y, c = lax.axis_index("x"), lax.axis_index("y"), lax.axis_index("c")
    me = 4 * x + 2 * y + c
    nc = _N_COPIES[mode]
    chips = [(1 - x, y), (x, 1 - y), (1 - x, 1 - y)]
    cps = []
    for a in range(len(land_refs)):
        if mode in ("scatter", "gather"):
            plan = [((src_refs[a].at[4 * px + 2 * py + pc] if mode == "scatter" else src_refs[a]),
                     land_refs[a].at[me], (px, py, pc)) for px, py, pc in _peers(x, y, c)]
        elif mode == "chips":
            plan = [(src_refs[a], land_refs[a].at[me], to) for to in [(x, y, 1 - c)] + [(*ch, c) for ch in chips]]
        else:
            plan = [(land_refs[a].at[4 * px + 2 * py + c], land_refs[a].at[4 * px + 2 * py + c], (x, y, 1 - c))
                    for px, py in chips]
        for k, (src, dst, to) in enumerate(plan):
            cps.append(pltpu.make_async_remote_copy(
                src_ref=src, dst_ref=dst, send_sem=send_sems.at[nc * a + k], recv_sem=recv_sems.at[nc * a + k],
                device_id=to, device_id_type=MESH_ID))
    return cps


def _exchange_start(srcs, lands, mode, *, name):
    ns, nz = len(srcs), len(lands)
    nsem = _N_COPIES[mode] * nz

    def body(*refs):
        src_refs, land_refs = refs[:ns], refs[ns:ns + nz]
        send_sems, recv_sems = refs[ns + nz], refs[ns + nz + 1]
        token = refs[-1]
        for cp in _exchange_copies(src_refs, land_refs, send_sems, recv_sems, mode):
            cp.start()
        token[...] = jnp.zeros(token.shape, token.dtype)

    hbm = lambda a: pltpu.HBM(a.shape, a.dtype)
    outs = pl.pallas_call(
        body, name=name,
        out_shape=(pltpu.SemaphoreType.DMA((nsem,)), pltpu.SemaphoreType.DMA((nsem,)),
                   *[hbm(a) for a in srcs], *[hbm(a) for a in lands], _sds((8, LANE), F32)),
        in_specs=[_HBM] * (ns + nz),
        out_specs=(_SEM, _SEM, *[_HBM] * (ns + nz), pl.BlockSpec(memory_space=pltpu.VMEM)),
        input_output_aliases={i: 2 + i for i in range(ns + nz)},
        compiler_params=pltpu.CompilerParams(has_side_effects=_EFFECT),
    )(*[pltpu.with_memory_space_constraint(a, pltpu.HBM) for a in list(srcs) + list(lands)])
    return outs[0], outs[1], list(outs[2:2 + ns]), list(outs[2 + ns:2 + ns + nz]), outs[-1]


def _exchange_wait(send_sems, recv_sems, srcs, lands, after, mode, *, name):
    ns, nz = len(srcs), len(lands)

    def body(*refs):
        src_refs, land_refs = refs[:ns], refs[ns:ns + nz]
        s_sems, r_sems = refs[ns + nz], refs[ns + nz + 1]
        for cp in _exchange_copies(src_refs, land_refs, s_sems, r_sems, mode):
            cp.wait_send()
            cp.wait_recv()

    hbm = lambda a: pltpu.HBM(a.shape, a.dtype)
    outs = pl.pallas_call(
        body, name=name,
        out_shape=(*[hbm(a) for a in srcs], *[hbm(a) for a in lands]),
        in_specs=[_HBM] * (ns + nz) + [_SEM, _SEM, _ANY],
        out_specs=tuple([_HBM] * (ns + nz)),
        input_output_aliases={i: i for i in range(ns + nz)},
        compiler_params=pltpu.CompilerParams(has_side_effects=_EFFECT),
    )(*srcs, *lands, send_sems, recv_sems, after)
    return list(outs[ns:])


_WIN_SEGS = (("ql", 0, Q_LORA, SEG_QL[0]), ("kvl", Q_LORA, KV_LORA, SEG_KVL[0]),
             ("kr", Q_LORA + KV_LORA, ROPE, SEG_KR[0]), ("mg", Q_LORA + KV_LORA + ROPE, D_MLA, SEG_MG[0]),
             ("ci", Q_LORA + KV_LORA + ROPE + D_MLA, 2 * D_CONV, SEG_CI[0]),
             ("cg", Q_LORA + KV_LORA + ROPE + D_MLA + 2 * D_CONV, D_CONV, SEG_CG[0]))
_WIN_SHARD = IN_COLS // N_DEV


def _win_pieces():
    out = []
    for _, o, n, new in _WIN_SEGS:
        for j in range(N_DEV):
            lo, hi = max(o, j * _WIN_SHARD), min(o + n, (j + 1) * _WIN_SHARD)
            if lo < hi:
                out.append((j, lo - j * _WIN_SHARD, new + lo - o, hi - lo))
    return out


WIN_T = 512


def _win_assemble(w_all, *, name):
    d = w_all.shape[2]
    t = min(WIN_T, d)
    pieces = sorted(_win_pieces(), key=lambda p: p[2])
    assert all(lo % 8 == 0 and n % 8 == 0 for _, lo, _, n in pieces)

    def body(w_ref, o_ref):
        rows = [w_ref[j].astype(F32)[lo:lo + n, :] for j, lo, _, n in pieces]
        rows.append(jnp.zeros((IN_PAD - (SEG_KR[0] + ROPE), t), F32))
        o_ref[...] = jnp.concatenate(rows, axis=0).astype(o_ref.dtype)

    return pl.pallas_call(
        body, name=name, grid=(d // t,),
        in_specs=[pl.BlockSpec((N_DEV, _WIN_SHARD, t), lambda i: (0, 0, i))],
        out_specs=pl.BlockSpec((IN_PAD, t), lambda i: (0, i)), out_shape=_sds((IN_PAD, d), w_all.dtype),
        compiler_params=_cp(("parallel",)),
    )(w_all)


def _win_split(grad, *, name):
    d = grad.shape[1]
    t = min(WIN_T, d)
    by_shard = [sorted([p for p in _win_pieces() if p[0] == j], key=lambda p: p[1]) for j in range(N_DEV)]

    def body(g_ref, o_ref):
        for j in range(N_DEV):
            rows = [g_ref[new:new + n, :] for _, _, new, n in by_shard[j]]
            o_ref[j] = jnp.concatenate(rows, axis=0).astype(o_ref.dtype)

    return pl.pallas_call(
        body, name=name, grid=(d // t,),
        in_specs=[pl.BlockSpec((IN_PAD, t), lambda i: (0, i))],
        out_specs=pl.BlockSpec((N_DEV, _WIN_SHARD, t), lambda i: (0, 0, i)),
        out_shape=_sds((N_DEV, _WIN_SHARD, d), WIRE_DTYPE),
        compiler_params=_cp(("parallel",)),
    )(grad)


def _cols_to_shards(a):
    r, n = a.shape
    return a.reshape(r, N_DEV, n // N_DEV).transpose(1, 0, 2)


def _shards_to_cols(a):
    nd, r, w = a.shape
    return a.transpose(1, 0, 2).reshape(r, nd * w)


def _win_permute(w_in):
    o_ql, o_kvl, o_kr, o_mg = 0, Q_LORA, Q_LORA + KV_LORA, Q_LORA + KV_LORA + ROPE
    o_ci = o_mg + D_MLA
    o_cg = o_ci + 2 * D_CONV
    seg = lambda o, n: w_in[:, o:o + n]
    pad = jnp.zeros((w_in.shape[0], IN_PAD - (SEG_KR[0] + ROPE)), w_in.dtype)
    return jnp.concatenate([seg(o_ci, 2 * D_CONV), seg(o_mg, D_MLA), seg(o_cg, D_CONV), seg(o_ql, Q_LORA),
                            seg(o_kvl, KV_LORA), seg(o_kr, ROPE), pad], axis=1)


def _win_unpermute(g):
    seg = lambda s, n=None: g[:, s[0]:s[0] + (s[1] if n is None else n)]
    return jnp.concatenate([seg(SEG_QL), seg(SEG_KVL), seg(SEG_KR, ROPE), seg(SEG_MG), seg(SEG_CI), seg(SEG_CG)], axis=1)


def _qup_permute(w):
    w3 = w.reshape(w.shape[0], N_HEADS, QK_DIM)
    nope = w3[:, :, :NOPE].reshape(w.shape[0], N_HEADS * NOPE)
    rope = jnp.pad(w3[:, :, NOPE:], ((0, 0), (0, 0), (0, LANE - ROPE))).reshape(w.shape[0], N_HEADS * LANE)
    return jnp.concatenate([nope, rope], axis=1)


def _qup_unpermute(g):
    r = g.shape[0]
    nope = g[:, :N_HEADS * NOPE].reshape(r, N_HEADS, NOPE)
    rope = g[:, N_HEADS * NOPE:].reshape(r, N_HEADS, LANE)[:, :, :ROPE]
    return jnp.concatenate([nope, rope], axis=2).reshape(r, N_HEADS * QK_DIM)


def _norm_tiles(g):
    return g[:NOPE].reshape(1, LANE), jnp.pad(g[NOPE:], (0, LANE - ROPE)).reshape(1, LANE)


def _norm_untile(gt):
    return jnp.concatenate([gt[0, :NOPE], gt[0, LANE:LANE + ROPE]])


def _rope_tiles(positions):
    inv_freq = 1.0 / (ROPE_THETA ** (jnp.arange(0, ROPE, 2, dtype=F32) / ROPE))
    ang = positions.astype(F32)[:, None] * inv_freq
    cos, sin = jnp.cos(ang), jnp.sin(ang)
    zq = jnp.zeros_like(cos)
    c_t = jnp.concatenate([cos, cos, zq, zq], axis=1)
    s1_t = jnp.concatenate([-sin, zq, zq, zq], axis=1)
    s2_t = jnp.concatenate([zq, sin, zq, zq], axis=1)
    return c_t, s1_t, s2_t


_BIG = ("w_in", "w_q_up", "w_kv_up", "w_pw", "w_out")
_COL_SHARDED = ("w_in", "w_q_up", "w_kv_up")


def _pack_rows(arrs):
    return jnp.concatenate([a.reshape(-1, LANE) for a in arrs], axis=0)


def _unpack_rows(buf, shapes):
    out, r0 = [], 0
    lead = buf.shape[:-2]
    for shp in shapes:
        n = math.prod(shp) // LANE
        out.append(buf[..., r0:r0 + n, :].reshape(lead + tuple(shp)))
        r0 += n
    return out


_SMALL = (("dmod", 3 * D_MODEL), ("norm_g", D_MODEL), ("q_lat_g", Q_LORA), ("kv_lat_g", KV_LORA),
          ("q_norm_g", 2 * LANE), ("k_norm_g", 2 * LANE), ("glu_b", 2 * D_CONV), ("dw_w", HALO * D_CONV),
          ("dw_b", D_CONV), ("conv_ln_g", D_CONV), ("conv_ln_b", D_CONV), ("b_pw", D_CONV))


def _layer_fwd(x, p, rope, l, early=None, late=None):
    n = lambda s: f"{s}_l{l}"
    c_t, s1_t, s2_t = rope
    h = _prenorm(x, p["norm_g"], p["shift"], p["sc1p"], name=n("prenorm"))
    if early is not None:
        p = {**p, **early(h)}
    z = _mm(h, p["w_in"], tb=True, name=n("in_proj"), tn=IN_TILE, n_outer=True,
            after=p.get("in_proj_after"))
    if late is not None:
        p = {**p, **late(z)}
    qn, kn = _lat_norm(z, p["q_lat_g"], p["kv_lat_g"], name=n("lat_norm"))
    q_raw = _mm(qn, p["w_q_up"], name=n("q_up"), tn=1024)
    kv = _mm(kn, p["w_kv_up"], name=n("kv_up"), tn=1024)
    qf, kf, vf = _qk_prep(q_raw, kv, z, c_t, s1_t, s2_t, *p["qk_tiles"], name=n("qk_prep"))
    o, lse = _flash_fwd(qf, kf, vf, name=n("flash_fwd"))
    u1, u3 = _conv_fwd(z, p["glu_b"], p["dw_w"], p["dw_b"], p["conv_ln_g"], p["conv_ln_b"], name=n("conv_fwd"))
    u4m = _mm(u3, p["w_pw"], name=n("pw"), tn=1024)
    cat = _gate_cat(o, z, u4m, p["b_pw"], name=n("gate_cat"))
    y, x_next = _mm(cat, p["w_out"], name=n("out_proj"), tn=1024, residual=(x, p["gate"]))
    saved = dict(x=x, h=h, z=z, qn=qn, kn=kn, q_raw=q_raw, kv=kv, qf=qf, kf=kf, vf=vf, o=o, lse=lse,
                 u1=u1, u3=u3, u4m=u4m, cat=cat, y=y)
    return x_next, saved, p


def _layer_bwd(gxo, p, sv, rope, l, hook_rest=None, hook_w_in=None):
    n = lambda s: f"{s}_l{l}"
    c_t, s1_t, s2_t = rope
    z = sv["z"]
    dy, dgate = _out_bwd(gxo, sv["y"], p["gate"], name=n("out_bwd"))
    g_w_out = _mm(sv["cat"], dy, ta=True, name=n("g_w_out"), tm=1024, tn=1024)
    do, delta, du4, g_b_pw, dz = _gate_bwd(dy, p["w_out"], sv["o"], z, sv["u4m"], p["b_pw"], name=n("gate_bwd"))
    g_w_pw = _mm(sv["u3"], du4, ta=True, name=n("g_w_pw"), tm=1024, tn=1024, tk=512)
    du3 = _mm(du4, p["w_pw"], tb=True, name=n("d_u3"), tn=1024)
    dz, g_ln_g, g_ln_b, g_dw_b, g_glu_b, g_dw_w = _conv_bwd(
        du3, sv["u1"], z, dz, p["glu_b"], p["dw_w"], p["conv_ln_g"], p["conv_ln_b"], name=n("conv_bwd"))
    t_att = min(ATT_T, z.shape[0])
    to_lanes = lambda a: a.reshape(N_HEADS, z.shape[0] // t_att, 1, t_att)
    dqf, dkf, dvf = _flash_bwd(sv["qf"], sv["kf"], sv["vf"], do,
                               to_lanes(sv["lse"][:, :, 0]), to_lanes(delta), name=n("flash_bwd"))
    dq_raw, dkv, dkr, g_qn, g_kn = _qk_bwd(dqf, dkf, dvf, sv["q_raw"], sv["kv"], z, c_t, s1_t, s2_t,
                                            *p["qk_tiles"], name=n("qk_bwd"))
    g_w_q_up = _mm(sv["qn"], dq_raw, ta=True, name=n("g_w_q_up"), tm=512, tn=1024, tk=512)
    dqn = _mm(dq_raw, p["w_q_up"], tb=True, name=n("d_qn"))
    g_w_kv_up = _mm(sv["kn"], dkv, ta=True, name=n("g_w_kv_up"), tm=256, tn=1024, tk=512)
    dkn = _mm(dkv, p["w_kv_up"], tb=True, name=n("d_kn"))
    dz, g_ql, g_kvl = _lat_bwd(dqn, dkn, dkr, z, dz, p["q_lat_g"], p["kv_lat_g"], name=n("lat_bwd"))
    big = dict(w_q_up=g_w_q_up, w_kv_up=g_w_kv_up, w_pw=g_w_pw, w_out=g_w_out)
    after = None if hook_rest is None else hook_rest(big)
    g_w_in = _mm(dz, sv["h"], ta=True, name=n("g_w_in"), tm=512, tn=1024, after=after)
    big["w_in"] = g_w_in
    after = None if hook_w_in is None else hook_w_in(g_w_in)
    dh = _mm(dz, p["w_in"], name=n("d_h"), tn=1024, after=after)
    dx, dshift, dscale, g_norm = _prenorm_bwd(dh, sv["x"], gxo, p["norm_g"], p["sc1p"], name=n("prenorm_bwd"))
    small = dict(dmod=jnp.concatenate([dshift, dscale, dgate], axis=1), norm_g=g_norm, q_lat_g=g_ql, kv_lat_g=g_kvl,
                 q_norm_g=g_qn, k_norm_g=g_kn, glu_b=g_glu_b, dw_w=g_dw_w, dw_b=g_dw_b,
                 conv_ln_g=g_ln_g, conv_ln_b=g_ln_b, b_pw=g_b_pw)
    return dx, big, small


def _layer_params(l, full, mod_l, small):
    d = D_MODEL
    row = lambda a: a.reshape(1, -1)
    shift, scale, gate = mod_l[:, :d], mod_l[:, d:2 * d], mod_l[:, 2 * d:]
    dw_w = jnp.pad(full["dw_w"][l], ((0, HALO - CONV_K), (0, 0)))
    return dict(
        shift=shift, sc1p=1.0 + scale, gate=gate, norm_g=row(small["norm_g"][l]),
        **{k: full[k][l] for k in _BIG if k in full}, dw_w=dw_w,
        q_lat_g=row(small["q_lat_g"][l]), kv_lat_g=row(small["kv_lat_g"][l]),
        qk_tiles=_norm_tiles(small["q_norm_g"][l]) + _norm_tiles(small["k_norm_g"][l]),
        glu_b=row(small["glu_b"][l]), dw_b=row(small["dw_b"][l]), conv_ln_g=row(small["conv_ln_g"][l]),
        conv_ln_b=row(small["conv_ln_b"][l]), b_pw=row(small["b_pw"][l]))


def kernel(x, c, positions, ada_w, ada_b, norm_g, w_in, q_lat_g, w_q_up, kv_lat_g, w_kv_up, q_norm_g, k_norm_g, glu_b, dw_w, dw_b, conv_ln_g, conv_ln_b, w_pw, b_pw, w_out, loss_target, m_ada_w, m_ada_b, m_norm_g, m_w_in, m_q_lat_g, m_w_q_up, m_kv_lat_g, m_w_kv_up, m_q_norm_g, m_k_norm_g, m_glu_b, m_dw_w, m_dw_b, m_conv_ln_g, m_conv_ln_b, m_w_pw, m_b_pw, m_w_out, v_ada_w, v_ada_b, v_norm_g, v_w_in, v_q_lat_g, v_w_q_up, v_kv_lat_g, v_w_kv_up, v_q_norm_g, v_k_norm_g, v_glu_b, v_dw_w, v_dw_b, v_conv_ln_g, v_conv_ln_b, v_w_pw, v_b_pw, v_w_out):
    names = ("ada_w", "ada_b", "norm_g", "w_in", "q_lat_g", "w_q_up", "kv_lat_g", "w_kv_up", "q_norm_g",
             "k_norm_g", "glu_b", "dw_w", "dw_b", "conv_ln_g", "conv_ln_b", "w_pw", "b_pw", "w_out")
    w_loc = dict(zip(names, (ada_w, ada_b, norm_g, w_in, q_lat_g, w_q_up, kv_lat_g, w_kv_up, q_norm_g, k_norm_g,
                             glu_b, dw_w, dw_b, conv_ln_g, conv_ln_b, w_pw, b_pw, w_out)))
    m_loc = dict(zip(names, (m_ada_w, m_ada_b, m_norm_g, m_w_in, m_q_lat_g, m_w_q_up, m_kv_lat_g, m_w_kv_up,
                             m_q_norm_g, m_k_norm_g, m_glu_b, m_dw_w, m_dw_b, m_conv_ln_g, m_conv_ln_b, m_w_pw,
                             m_b_pw, m_w_out)))
    v_loc = dict(zip(names, (v_ada_w, v_ada_b, v_norm_g, v_w_in, v_q_lat_g, v_w_q_up, v_kv_lat_g, v_w_kv_up,
                             v_q_norm_g, v_k_norm_g, v_glu_b, v_dw_w, v_dw_b, v_conv_ln_g, v_conv_ln_b, v_w_pw,
                             v_b_pw, v_w_out)))
    nl, d = N_LAYERS, D_MODEL
    me = 4 * lax.axis_index("x") + 2 * lax.axis_index("y") + lax.axis_index("c")
    x2, tgt = x[0], loss_target[0]
    ada_cols = ada_w.shape[-1]

    tr = lambda a: jnp.swapaxes(a, 1, 2)
    w_loc, m_loc, v_loc = ({**dd, "w_in": tr(dd["w_in"])} for dd in (w_loc, m_loc, v_loc))
    w_in0 = [w_loc["w_in"][0].astype(WIRE_DTYPE)]

    dw_pad = jnp.pad(dw_w, ((0, 0), (0, HALO - CONV_K), (0, 0)))
    c_all, dw_all = _all_gather([c.reshape(d // LANE, LANE), dw_pad], name="gather_c")
    c_all = c_all.reshape(N_DEV, d)
    ada_b_cols = lax.dynamic_slice_in_dim(ada_b, me * ada_cols, ada_cols, axis=1).reshape(nl, 1, ada_cols)
    mod_cols = _ada_fwd(c_all, ada_w, ada_b_cols, name="ada_fwd")
    mod_all = _all_gather([mod_cols], name="gather_mod")[0]

    fly_c = _exchange_start(w_in0, _own_slots(w_in0, False, name="own_w_in_l0", after=mod_all), "chips",
                            name="gather_start_w_in_l0")
    held = dict(positions=positions, norm_g=norm_g, q_lat_g=q_lat_g, kv_lat_g=kv_lat_g, mod_all=mod_all,
                q_norm_g=q_norm_g, k_norm_g=k_norm_g, glu_b=glu_b, dw_b=dw_b, conv_ln_g=conv_ln_g,
                conv_ln_b=conv_ln_b, b_pw=b_pw, big={k: w_loc[k] for k in _BIG})
    _, held = lax.optimization_barrier((fly_c[4], held))
    positions, norm_g, q_lat_g, kv_lat_g, q_norm_g, k_norm_g, glu_b, dw_b, conv_ln_g, conv_ln_b, b_pw, mod_all = (
        held[k] for k in ("positions", "norm_g", "q_lat_g", "kv_lat_g", "q_norm_g", "k_norm_g", "glu_b",
                          "dw_b", "conv_ln_g", "conv_ln_b", "b_pw", "mod_all"))
    wire = {k: held["big"][k].astype(WIRE_DTYPE) for k in _BIG}
    mod_me = lax.dynamic_index_in_dim(mod_all, me, axis=2, keepdims=False)
    mod = mod_me.transpose(1, 0, 2).reshape(nl, 1, N_DEV * ada_cols)
    fly_r0, fly_w1 = {}, {}

    def early_l0(h):
        from_chips = _exchange_wait(*fly_c[:4], h, "chips", name="gather_wait_w_in_l0")
        fly_f = _exchange_start([], from_chips, "forward", name="forward_start_w_in_l0")
        w_in_all0 = _exchange_wait(*fly_f[:4], fly_f[4], "forward", name="forward_wait_w_in_l0")[0]
        rest0 = [wire[k][0] for k in _BIG[1:]]
        fly_r0["x"] = _exchange_start(rest0, _own_slots(rest0, False, name="own_weights_l0_rest", after=w_in_all0),
                                      "gather", name="gather_start_l0_rest")
        return dict(w_in=_win_assemble(w_in_all0, name="w_in_assemble_l0"), in_proj_after=fly_r0["x"][4])

    def layout_rest(parts):
        return dict(w_q_up=_qup_permute(_shards_to_cols(parts[0])), w_kv_up=_shards_to_cols(parts[1]),
                    w_pw=parts[2].reshape(D_CONV, D_CONV), w_out=parts[3].reshape(D_MLA + D_CONV, d))

    small_in = dict(norm_g=norm_g, q_lat_g=q_lat_g, kv_lat_g=kv_lat_g, q_norm_g=q_norm_g, k_norm_g=k_norm_g,
                    glu_b=glu_b, dw_b=dw_b, conv_ln_g=conv_ln_g, conv_ln_b=conv_ln_b, b_pw=b_pw)
    dw_full = [_shards_to_cols(dw_all[:, l])[:CONV_K] for l in range(nl)]
    rope = _rope_tiles(positions[0])

    def layer_params(l, w_in_all, rest, mod_l):
        full = dict(dw_w=dw_full)
        if w_in_all is not None:
            full["w_in"] = {l: _win_assemble(w_in_all, name=f"w_in_assemble_l{l}")}
        if rest is not None:
            full.update({k: {l: a} for k, a in layout_rest(rest).items()})
        return _layer_params(l, full, mod_l, small_in)

    def late_l0(z):
        parts = _exchange_wait(*fly_r0["x"][:4], z, "gather", name="gather_wait_l0_rest")
        src1 = [wire[k][1] for k in _BIG]
        fly_w1["x"] = _exchange_start(src1, _own_slots(src1, False, name="own_weights_l1", after=parts[0]), "gather",
                                      name="gather_start_l1")
        late = layout_rest(parts)
        late["q_lat_g"] = small_in["q_lat_g"][0].reshape(1, -1) + fly_w1["x"][4][0, 0]
        return late

    params, saved = [None] * nl, [None] * nl
    p0 = layer_params(0, None, None, mod[0])
    xs, saved[0], params[0] = _layer_fwd(x2, p0, rope, 0, early=early_l0, late=late_l0)
    parts1 = _exchange_wait(*fly_w1["x"][:4], xs, "gather", name="gather_wait_l1")
    params[1] = layer_params(1, parts1[0], parts1[1:], mod[1])
    xs, saved[1], _ = _layer_fwd(xs, params[1], rope, 1)
    gx, loss_part = _loss_head(xs, tgt, name="loss_head")
    loss = lax.psum(loss_part[0, 0], ("x", "y", "c"))

    def shard_major(k, g):
        if k == "w_q_up":
            g = _qup_unpermute(g)
        if k in _COL_SHARDED:
            return _cols_to_shards(g)
        return g.reshape((N_DEV, g.shape[0] // N_DEV, g.shape[1]))

    def scatter_start(send, tag):
        lands = _own_slots(send, True, name=f"own_grads_{tag}")
        return _exchange_start(send, lands, "scatter", name=f"scatter_start_{tag}")

    def wire_rest(big):
        return [shard_major(k, big[k]).astype(WIRE_DTYPE) for k in _BIG[1:]]

    big_g, small_g, flying = [None] * nl, [None] * nl, {}
    gx, big_g[1], small_g[1] = _layer_bwd(gx, params[1], saved[1], rope, 1)
    flying["l1"] = scatter_start([_win_split(big_g[1]["w_in"], name="w_in_split_l1")] + wire_rest(big_g[1]), "l1")
    p0 = dict(params[0])
    p0["gate"] = p0["gate"] + flying["l1"][4][0, 0]

    def start_rest_l0(big):
        flying["l0_rest"] = scatter_start(wire_rest(big), "l0_rest")
        return flying["l0_rest"][4]

    res, arrived = {}, [None] * nl

    def start_w_in_l0(g_w_in):
        flying["l0_w_in"] = scatter_start([_win_split(g_w_in, name="w_in_split_l0")], "l0_w_in")
        tok = flying["l0_w_in"][4]
        arrived[1] = _exchange_wait(*flying["l1"][:4], tok, "scatter", name="scatter_wait_l1")
        arrived[0] = [None] + _exchange_wait(*flying["l0_rest"][:4], tok, "scatter", name="scatter_wait_l0_rest")
        for i, k in enumerate(_BIG):
            if i > 0:
                res[k] = _adamw([arrived[l][i] for l in range(nl)], w_loc[k], m_loc[k], v_loc[k], name=f"adamw_{k}")
        return res["w_out"][0]

    gx, big_g[0], small_g[0] = _layer_bwd(gx, p0, saved[0], rope, 0, hook_rest=start_rest_l0,
                                          hook_w_in=start_w_in_l0)

    tile = 8 * LANE
    padded = [(k, nn, -(-nn // tile) * tile) for k, nn in _SMALL]
    spk = jnp.concatenate([jnp.pad(small_g[l][k].reshape(-1), (0, np_ - nn)).reshape(-1, LANE)
                           for l in range(nl) for k, nn, np_ in padded], axis=0)
    s_all = _all_gather([spk], name="gather_small_grads")[0]
    s_rows = sum(np_ for _, _, np_ in padded) // LANE
    s_all = s_all.reshape(N_DEV, nl, s_rows, LANE)
    s_parts = {k: a[..., :nn] for (k, nn, _), a in
               zip(padded, _unpack_rows(s_all, [(np_,) for _, _, np_ in padded]))}

    dmod_all = s_parts["dmod"]
    dmod_cols = lax.dynamic_slice_in_dim(dmod_all, me * ada_cols, ada_cols, axis=2).transpose(1, 0, 2)
    g_ada_w = _ada_bwd(c_all.T, dmod_cols, name="ada_bwd")
    gp = {}
    gp["ada_w"] = g_ada_w[None]
    gp["ada_b"] = dmod_all
    for k in ("norm_g", "q_lat_g", "kv_lat_g", "glu_b", "dw_b", "conv_ln_g", "conv_ln_b", "b_pw"):
        gp[k] = s_parts[k]
    for k in ("q_norm_g", "k_norm_g"):
        t = s_parts[k]
        gp[k] = jnp.concatenate([t[..., :NOPE], t[..., LANE:LANE + ROPE]], axis=-1)
    dw_g = s_parts["dw_w"].reshape(N_DEV, nl, HALO, D_CONV)[:, :, :CONV_K]
    gp["dw_w"] = lax.dynamic_slice_in_dim(dw_g, me * LANE, LANE, axis=3)

    res["ada_w"] = _adamw(gp["ada_w"], w_loc["ada_w"], m_loc["ada_w"], v_loc["ada_w"], name="adamw_ada_w")
    small_names = [k for k in names if k not in _BIG and k != "ada_w"]
    res.update(zip(small_names, _adamw_small([(gp[k], w_loc[k], m_loc[k], v_loc[k]) for k in small_names],
                                             name="adamw_small")))
    arrived[0][0] = _exchange_wait(*flying["l0_w_in"][:4], res["ada_w"][1], "scatter", name="scatter_wait_l0_w_in")[0]
    w_in_res = _adamw([arrived[l][0] for l in range(nl)], w_loc["w_in"], m_loc["w_in"], v_loc["w_in"],
                      name="adamw_w_in")
    res["w_in"] = tuple(tr(a) for a in w_in_res)
    out = [loss, gx[None]]
    for idx in range(4):
        out += [res[k][idx] for k in names]
    return tuple(out)
```

```python
import functools
import math

import jax
import jax.numpy as jnp
from jax import lax
from jax.experimental import pallas as pl
from jax.experimental.pallas import tpu as pltpu

F32 = jnp.float32
MXU_DTYPE = jnp.bfloat16
WIRE_DTYPE = jnp.bfloat16

D_MODEL = 2048
N_LAYERS = 2
N_DEV = 8
N_HEADS = 8
NOPE = 128
ROPE = 64
V_DIM = 128
QK_DIM = NOPE + ROPE
Q_LORA = 512
KV_LORA = 256
D_MLA = N_HEADS * V_DIM
D_CONV = 1024
CONV_K = 31
ROPE_THETA = 10000.0
EPS = 1e-6
LANE = 128
HEAD_PAD = 2 * LANE
HALO = 32

SEG_CI = (0, 2 * D_CONV)
SEG_MG = (2 * D_CONV, D_MLA)
SEG_CG = (2 * D_CONV + D_MLA, D_CONV)
SEG_QL = (2 * D_CONV + D_MLA + D_CONV, Q_LORA)
SEG_KVL = (SEG_QL[0] + Q_LORA, KV_LORA)
SEG_KR = (SEG_KVL[0] + KV_LORA, LANE)
SEG_LAT = (SEG_QL[0], 1024)
IN_PAD = SEG_LAT[0] + SEG_LAT[1]
IN_TILE = IN_PAD // 4
assert SEG_KR[0] + LANE <= IN_PAD and SEG_LAT[0] % SEG_LAT[1] == 0
IN_COLS = Q_LORA + KV_LORA + ROPE + D_MLA + 2 * D_CONV + D_CONV

ADAM_LR = 0.001
ADAM_B1 = 0.9
ADAM_B2 = 0.999
ADAM_EPS = 1e-08
ADAM_WD = 0.01
ADAM_STEP = 10

VMEM_LIMIT = 56 * 1024 * 1024
ATT_T = 512
ROW_T = 256
CONV_T = 128
MESH_ID = pl.DeviceIdType.MESH


def _cp(sem=None):
    kw = dict(vmem_limit_bytes=VMEM_LIMIT)
    if sem is not None:
        kw["dimension_semantics"] = sem
    return pltpu.CompilerParams(**kw)


def _sds(shape, dtype):
    return jax.ShapeDtypeStruct(shape, dtype)


def _silu(x):
    return x * jax.nn.sigmoid(x)


def _dsilu(x):
    s = jax.nn.sigmoid(x)
    return s * (1.0 + x * (1.0 - s))


def _rowspec(t, width, col=0):
    return pl.BlockSpec((t, width), lambda i: (i, col))


def _vecspec(width):
    return pl.BlockSpec((1, width), lambda i: (0, 0))


def _colsum(v):
    return jnp.sum(v, axis=0, keepdims=True)


def _mm(a, b, *, name, ta=False, tb=False, out_dtype=F32, tm=512, tn=512, tk=None, n_outer=False, after=None,
        residual=None):
    if ta:
        kdim, m = a.shape
    else:
        m, kdim = a.shape
    if tb:
        n, k2 = b.shape
    else:
        k2, n = b.shape
    assert kdim == k2, (a.shape, b.shape)
    tm, tn = min(tm, m), min(tn, n)
    tk = kdim if tk is None else min(tk, kdim)
    assert m % tm == 0 and n % tn == 0 and kdim % tk == 0, (m, n, kdim, tm, tn, tk)
    nk = kdim // tk
    dims = (((0 if ta else 1,), (1 if tb else 0,)), ((), ()))

    n_extra = 0 if after is None else 1
    assert residual is None or nk == 1

    def body(a_ref, b_ref, *rest):
        if residual is not None:
            x_ref, gate_ref = rest[:2]
            rest = rest[2:]
        o_ref, scratch = rest[n_extra], rest[n_extra + 1:]
        prod = lax.dot_general(a_ref[...].astype(MXU_DTYPE), b_ref[...].astype(MXU_DTYPE), dims,
                               preferred_element_type=F32)
        if residual is not None:
            o_ref[...] = prod.astype(o_ref.dtype)
            scratch[0][...] = x_ref[...] + gate_ref[...] * prod
        elif nk == 1:
            o_ref[...] = prod.astype(o_ref.dtype)
        else:
            acc = scratch[0]
            k = pl.program_id(2)

            @pl.when(k == 0)
            def _():
                acc[...] = prod

            @pl.when(k > 0)
            def _():
                acc[...] += prod

            @pl.when(k == nk - 1)
            def _():
                o_ref[...] = acc[...].astype(o_ref.dtype)

    if n_outer:
        ij = lambda g0, g1: (g1, g0)
        grid = (n // tn, m // tm, nk)
    else:
        ij = lambda g0, g1: (g0, g1)
        grid = (m // tm, n // tn, nk)

    def a_map(g0, g1, k):
        i, _ = ij(g0, g1)
        return (k, i) if ta else (i, k)

    def b_map(g0, g1, k):
        _, j = ij(g0, g1)
        return (j, k) if tb else (k, j)

    def o_map(g0, g1, k):
        return ij(g0, g1)

    in_specs = [pl.BlockSpec((tk, tm) if ta else (tm, tk), a_map), pl.BlockSpec((tn, tk) if tb else (tk, tn), b_map)]
    operands = [a, b]
    out_specs, out_shape = pl.BlockSpec((tm, tn), o_map), _sds((m, n), out_dtype)
    if residual is not None:
        in_specs += [pl.BlockSpec((tm, tn), o_map), pl.BlockSpec((1, tn), lambda g0, g1, k: (0, ij(g0, g1)[1]))]
        operands += list(residual)
        out_specs, out_shape = [out_specs, pl.BlockSpec((tm, tn), o_map)], [out_shape, _sds((m, n), F32)]
    if after is not None:
        in_specs.append(_ANY)
        operands.append(after)
    return pl.pallas_call(
        body, name=name, grid=grid, in_specs=in_specs, out_specs=out_specs, out_shape=out_shape,
        scratch_shapes=[pltpu.VMEM((tm, tn), F32)] if nk > 1 else [],
        compiler_params=_cp(("parallel", "parallel", "arbitrary")),
    )(*operands)


def _prenorm(x, g, shift, sc1p, *, name):
    s, d = x.shape
    t = min(2 * ROW_T, s)

    def body(x_ref, g_ref, sh_ref, sc_ref, h_ref):
        xv = x_ref[...]
        r = lax.rsqrt(jnp.mean(xv * xv, axis=-1, keepdims=True) + EPS)
        h_ref[...] = ((xv * r) * g_ref[...] * sc_ref[...] + sh_ref[...]).astype(h_ref.dtype)

    return pl.pallas_call(
        body, name=name, grid=(s // t,),
        in_specs=[_rowspec(t, d), _vecspec(d), _vecspec(d), _vecspec(d)],
        out_specs=_rowspec(t, d), out_shape=_sds((s, d), MXU_DTYPE),
        compiler_params=_cp(("parallel",)),
    )(x, g, shift, sc1p)


def _rope_fwd(r, c_t, s1_t, s2_t):
    return r * c_t + pltpu.roll(r, LANE - ROPE // 2, 1) * s1_t + pltpu.roll(r, ROPE // 2, 1) * s2_t


def _rope_bwd(d, c_t, s1_t, s2_t):
    return d * c_t + pltpu.roll(d * s1_t, ROPE // 2, 1) + pltpu.roll(d * s2_t, LANE - ROPE // 2, 1)


def _lanesum(v):
    return jnp.sum(v, axis=-1, keepdims=True)


def _mla_pre(z, w_q_up, w_kv_up, g_ql, g_kvl, c_t, s1_t, s2_t, gqn, gqr, gkn, gkr, *, name):
    s = z.shape[0]
    t = min(2 * ROW_T, s)
    scale = 1.0 / math.sqrt(QK_DIM)
    wide = 2 * N_HEADS * LANE

    def body(ql_ref, kvl_ref, kr_ref, wq_ref, wkv_ref, gq_ref, gk_ref, c_ref, s1_ref, s2_ref,
             gqn_ref, gqr_ref, gkn_ref, gkr_ref, qn_ref, kn_ref, q_ref, kv_ref, qf_ref, kf_ref, vf_ref):
        for src, g_ref, dst, w_ref, up in ((ql_ref, gq_ref, qn_ref, wq_ref, q_ref),
                                           (kvl_ref, gk_ref, kn_ref, wkv_ref, kv_ref)):
            v = src[...]
            r = lax.rsqrt(jnp.mean(v * v, axis=-1, keepdims=True) + EPS)
            dst[...] = ((v * r) * g_ref[...]).astype(dst.dtype)
            up[...] = jnp.dot(dst[...], w_ref[...], preferred_element_type=F32)
        c_v, s1_v, s2_v = c_ref[...], s1_ref[...], s2_ref[...]
        kr = kr_ref[...]
        kr_ss = _lanesum(kr * kr)
        for h in range(N_HEADS):
            n = q_ref[:, h * LANE:(h + 1) * LANE]
            r = q_ref[:, N_HEADS * LANE + h * LANE:N_HEADS * LANE + (h + 1) * LANE]
            rs = lax.rsqrt((_lanesum(n * n) + _lanesum(r * r)) * (1.0 / QK_DIM) + EPS)
            qf_ref[h, :, 0:LANE] = (((n * rs) * gqn_ref[...]) * scale).astype(qf_ref.dtype)
            rr = _rope_fwd((r * rs) * gqr_ref[...], c_v, s1_v, s2_v)
            qf_ref[h, :, LANE:HEAD_PAD] = (rr * scale).astype(qf_ref.dtype)

            n = kv_ref[:, h * 2 * LANE:h * 2 * LANE + LANE]
            rs = lax.rsqrt((_lanesum(n * n) + kr_ss) * (1.0 / QK_DIM) + EPS)
            kf_ref[h, :, 0:LANE] = ((n * rs) * gkn_ref[...]).astype(kf_ref.dtype)
            kf_ref[h, :, LANE:HEAD_PAD] = _rope_fwd((kr * rs) * gkr_ref[...], c_v, s1_v, s2_v).astype(kf_ref.dtype)
            vf_ref[h, :, 0:V_DIM] = kv_ref[:, h * 2 * LANE + LANE:(h + 1) * 2 * LANE].astype(vf_ref.dtype)
            vf_ref[h, :, V_DIM:] = jnp.ones((t, V_DIM), vf_ref.dtype)

    hspec = lambda w: pl.BlockSpec((N_HEADS, t, w), lambda i: (0, i, 0))
    whole = lambda a: pl.BlockSpec(a.shape, lambda i: (0, 0))
    return pl.pallas_call(
        body, name=name, grid=(s // t,),
        in_specs=[_rowspec(t, Q_LORA, SEG_QL[0] // Q_LORA), _rowspec(t, KV_LORA, SEG_KVL[0] // KV_LORA),
                  _rowspec(t, LANE, SEG_KR[0] // LANE), whole(w_q_up), whole(w_kv_up),
                  _vecspec(Q_LORA), _vecspec(KV_LORA),
                  _rowspec(t, LANE), _rowspec(t, LANE), _rowspec(t, LANE),
                  _vecspec(LANE), _vecspec(LANE), _vecspec(LANE), _vecspec(LANE)],
        out_specs=[_rowspec(t, Q_LORA), _rowspec(t, KV_LORA), _rowspec(t, wide), _rowspec(t, wide),
                   hspec(HEAD_PAD), hspec(HEAD_PAD), hspec(2 * V_DIM)],
        out_shape=[_sds((s, Q_LORA), MXU_DTYPE), _sds((s, KV_LORA), MXU_DTYPE), _sds((s, wide), F32),
                   _sds((s, wide), F32), _sds((N_HEADS, s, HEAD_PAD), MXU_DTYPE),
                   _sds((N_HEADS, s, HEAD_PAD), MXU_DTYPE), _sds((N_HEADS, s, 2 * V_DIM), MXU_DTYPE)],
        compiler_params=_cp(("parallel",)),
    )(z, z, z, w_q_up, w_kv_up, g_ql, g_kvl, c_t, s1_t, s2_t, gqn, gqr, gkn, gkr)


def _causal_mask(t):
    row = lax.broadcasted_iota(jnp.int32, (t, t), 0)
    col = lax.broadcasted_iota(jnp.int32, (t, t), 1)
    return col <= row


NEG = -1e30


def _flash_fwd(qf, kf, va, *, name):
    nh, s, dk = qf.shape
    dv = va.shape[-1] // 2
    t = min(ATT_T, s)
    n = s // t
    assert dv == LANE and t % LANE == 0

    def body(q_ref, k_ref, v_ref, o_ref, lse_ref, m_s, acc_s, s_buf):
        i = pl.program_id(1)
        m_s[...] = jnp.full(m_s.shape, NEG, F32)
        acc_s[...] = jnp.zeros(acc_s.shape, F32)

        def rows_of(j):
            return pl.ds(pl.multiple_of(j * t, t), t)

        def scores(qi, j):
            return lax.dot_general(q_ref[0, rows_of(qi), :], k_ref[0, rows_of(j), :], (((1,), (1,)), ((), ())),
                                   preferred_element_type=F32)

        def consume(j, slot, masked):
            sc = s_buf[slot]
            if masked:
                sc = jnp.where(_causal_mask(t), sc, NEG)
            m_prev = m_s[...]
            m_new = jnp.maximum(m_prev, jnp.max(sc, axis=-1, keepdims=True))
            alpha = jnp.exp(m_prev - m_new)
            p = jnp.exp(sc - jnp.tile(m_new, (1, t // LANE)))
            acc_s[...] = jnp.tile(alpha, (1, 2)) * acc_s[...] + jnp.dot(
                p.astype(MXU_DTYPE), v_ref[0, rows_of(j), :], preferred_element_type=F32)
            m_s[...] = m_new

        nxt = jnp.minimum(i + 1, n - 1)

        @pl.when(i == 0)
        def _():
            s_buf[2] = scores(0, 0)
            consume(0, 2, True)
            s_buf[2] = scores(nxt, 0)

        @pl.when(i > 0)
        def _():
            s_buf[1] = scores(i, 1)
            consume(0, 2, False)

            def pair(a, carry):
                s_buf[0] = scores(i, 2 * a + 2)
                consume(2 * a + 1, 1, False)
                s_buf[1] = scores(i, 2 * a + 3)
                consume(2 * a + 2, 0, False)
                return carry

            lax.fori_loop(0, (i - 1) // 2, pair, 0)

            @pl.when(i % 2 == 1)
            def _():
                s_buf[2] = scores(nxt, 0)
                consume(i, 1, True)

            @pl.when(i % 2 == 0)
            def _():
                s_buf[0] = scores(i, i)
                consume(i - 1, 1, False)
                s_buf[2] = scores(nxt, 0)
                consume(i, 0, True)

        den = acc_s[:, dv:]
        o_ref[...] = acc_s[:, :dv] / den
        lse_ref[0] = m_s[...] + jnp.log(den)

    head = lambda h, i: (h, 0, 0)
    return pl.pallas_call(
        body, name=name, grid=(nh, n),
        in_specs=[pl.BlockSpec((1, s, dk), head), pl.BlockSpec((1, s, dk), head), pl.BlockSpec((1, s, 2 * dv), head)],
        out_specs=[pl.BlockSpec((t, dv), lambda h, i: (i, h)),
                   pl.BlockSpec((1, t, LANE), lambda h, i: (h, i, 0))],
        out_shape=[_sds((s, nh * dv), F32), _sds((nh, s, LANE), F32)],
        scratch_shapes=[pltpu.VMEM((t, LANE), F32), pltpu.VMEM((t, 2 * dv), F32), pltpu.VMEM((3, t, t), F32)],
        compiler_params=_cp(("arbitrary", "arbitrary")),
    )(qf, kf, va)


def _shifted_copies(ext_ref):
    rows = ext_ref.shape[1] - 8
    for s in range(1, 8):
        ext_ref[s, 0:rows, :] = ext_ref[0, s:s + rows, :]


def _windows(ext_ref, offsets, t_rows, lane0, lanes):
    for s in range(8):
        group = [o for o in offsets if o % 8 == s]
        if not group:
            continue
        lo, hi = min(group) - s, max(group) - s
        wide = ext_ref[s, pl.ds(lo, hi - lo + t_rows), lane0:lane0 + lanes]
        for o in group:
            yield o, wide[o - s - lo:o - s - lo + t_rows]


def _dw_taps(ext_ref, w_ref, row0, t_rows, lane0, lanes, first_off):
    acc = None
    for off, win in _windows(ext_ref, [row0 + first_off + k for k in range(CONV_K)], t_rows, lane0, lanes):
        k = off - row0 - first_off
        term = w_ref[k:k + 1, lane0:lane0 + lanes] * win
        acc = term if acc is None else acc + term
    return acc


CONV_RC = 32
CONV_LC = 256


def _conv_fwd(z, glu_b, dw_w, dw_b, ln_g, ln_b, *, name):
    s = z.shape[0]
    t = min(CONV_T, s)
    c2 = 2 * D_CONV
    hb = t // HALO

    def body(zm_ref, zh_ref, gb_ref, w_ref, wb_ref, g_ref, b_ref, u1_ref, u3_ref, ext):
        i = pl.program_id(0)

        def glu(zv):
            ci = zv + gb_ref[...]
            return ci[:, :D_CONV] * jax.nn.sigmoid(ci[:, D_CONV:])

        ext[0, HALO:, :] = glu(zm_ref[...])
        ext[0, 0:HALO, :] = jnp.where(i > 0, glu(zh_ref[...]), 0.0)
        _shifted_copies(ext)
        for rc in range(0, t, CONV_RC):
            for lc in range(0, D_CONV, CONV_LC):
                acc = _dw_taps(ext, w_ref, rc, CONV_RC, lc, CONV_LC, HALO - (CONV_K - 1))
                u1_ref[rc:rc + CONV_RC, lc:lc + CONV_LC] = acc + wb_ref[:, lc:lc + CONV_LC]
        u1 = u1_ref[...]
        mu = jnp.mean(u1, axis=-1, keepdims=True)
        cen = u1 - mu
        var = jnp.mean(cen * cen, axis=-1, keepdims=True)
        u2 = (cen * lax.rsqrt(var + EPS)) * g_ref[...] + b_ref[...]
        u3_ref[...] = _silu(u2).astype(u3_ref.dtype)

    return pl.pallas_call(
        body, name=name, grid=(s // t,),
        in_specs=[_rowspec(t, c2), pl.BlockSpec((HALO, c2), lambda i: (jnp.maximum(i * hb - 1, 0), 0)),
                  _vecspec(c2), pl.BlockSpec((HALO, D_CONV), lambda i: (0, 0)), _vecspec(D_CONV),
                  _vecspec(D_CONV), _vecspec(D_CONV)],
        out_specs=[_rowspec(t, D_CONV), _rowspec(t, D_CONV)],
        out_shape=[_sds((s, D_CONV), F32), _sds((s, D_CONV), MXU_DTYPE)],
        scratch_shapes=[pltpu.VMEM((8, t + HALO, D_CONV), F32)],
        compiler_params=_cp(("parallel",)),
    )(z, z, glu_b, dw_w, dw_b, ln_g, ln_b)


def _gate_cat(o, z, u4m, b_pw, *, name):
    s = o.shape[0]
    t = min(2 * ROW_T, s)

    def body(o_ref, mg_ref, u4_ref, cg_ref, b_ref, cat_ref):
        cat_ref[:, :D_MLA] = (o_ref[...] * _silu(mg_ref[...])).astype(cat_ref.dtype)
        cat_ref[:, D_MLA:] = ((u4_ref[...] + b_ref[...]) * _silu(cg_ref[...])).astype(cat_ref.dtype)

    return pl.pallas_call(
        body, name=name, grid=(s // t,),
        in_specs=[_rowspec(t, D_MLA), _rowspec(t, D_MLA, SEG_MG[0] // D_MLA), _rowspec(t, D_CONV),
                  _rowspec(t, D_CONV, SEG_CG[0] // D_CONV), _vecspec(D_CONV)],
        out_specs=_rowspec(t, D_MLA + D_CONV), out_shape=_sds((s, D_MLA + D_CONV), MXU_DTYPE),
        compiler_params=_cp(("parallel",)),
    )(o, z, u4m, z, b_pw)


def _loss_head(xf, target, *, name):
    s, d = xf.shape
    t = min(2 * ROW_T, s)

    def body(x_ref, t_ref, gx_ref, loss_ref):
        @pl.when(pl.program_id(0) == 0)
        def _():
            loss_ref[...] = jnp.zeros(loss_ref.shape, F32)

        err = x_ref[...] - t_ref[...]
        gx_ref[...] = err * (1.0 / d)
        loss_ref[...] += 0.5 * jnp.sum(_lanesum(err * err) * (1.0 / d), axis=0, keepdims=True)

    return pl.pallas_call(
        body, name=name, grid=(s // t,),
        in_specs=[_rowspec(t, d), _rowspec(t, d)],
        out_specs=[_rowspec(t, d), pl.BlockSpec((1, 1), lambda i: (0, 0))],
        out_shape=[_sds((s, d), F32), _sds((1, 1), F32)],
        compiler_params=_cp(("arbitrary",)),
    )(xf, target)


def _acc_init(refs):
    @pl.when(pl.program_id(0) == 0)
    def _():
        for r in refs:
            r[...] = jnp.zeros(r.shape, r.dtype)


def _out_bwd(gxo, y, gate, *, name):
    s, d = gxo.shape
    t = min(2 * ROW_T, s)

    def body(g_ref, y_ref, gate_ref, dy_ref, dgate_ref):
        _acc_init([dgate_ref])
        gv = g_ref[...]
        dy_ref[...] = (gv * gate_ref[...]).astype(dy_ref.dtype)
        dgate_ref[...] += _colsum(gv * y_ref[...])

    return pl.pallas_call(
        body, name=name, grid=(s // t,),
        in_specs=[_rowspec(t, d), _rowspec(t, d), _vecspec(d)],
        out_specs=[_rowspec(t, d), _vecspec(d)],
        out_shape=[_sds((s, d), MXU_DTYPE), _sds((1, d), F32)],
        compiler_params=_cp(("arbitrary",)),
    )(gxo, y, gate)


def _gate_bwd(dy, w_out, o, z, u4m, b_pw, *, name):
    s, d = dy.shape
    t = min(2 * ROW_T, s)
    gates = D_MLA + D_CONV
    assert SEG_CG[0] == SEG_MG[0] + D_MLA and SEG_MG[0] % gates == 0

    def body(dy_ref, w_ref, o_ref, mg_ref, u4_ref, cg_ref, b_ref,
             do_ref, delta_ref, du4_ref, gb_ref, dz_ref):
        _acc_init([gb_ref])
        dcat = lax.dot_general(dy_ref[...], w_ref[...], (((1,), (1,)), ((), ())), preferred_element_type=F32)
        dm, ov, mg = dcat[:, :D_MLA], o_ref[...], mg_ref[...]
        do = dm * _silu(mg)
        do_ref[...] = do.astype(do_ref.dtype)
        dz_ref[:, :D_MLA] = (dm * ov * _dsilu(mg)).astype(dz_ref.dtype)
        prod = do * ov
        for h in range(N_HEADS):
            delta_ref[h] = _lanesum(prod[:, h * V_DIM:(h + 1) * V_DIM])
        dc, cg = dcat[:, D_MLA:], cg_ref[...]
        du4 = dc * _silu(cg)
        du4_ref[...] = du4.astype(du4_ref.dtype)
        dz_ref[:, D_MLA:] = (dc * (u4_ref[...] + b_ref[...]) * _dsilu(cg)).astype(dz_ref.dtype)
        gb_ref[...] += _colsum(du4)

    return pl.pallas_call(
        body, name=name, grid=(s // t,),
        in_specs=[_rowspec(t, d), pl.BlockSpec((gates, d), lambda i: (0, 0)), _rowspec(t, D_MLA),
                  _rowspec(t, D_MLA, SEG_MG[0] // D_MLA), _rowspec(t, D_CONV),
                  _rowspec(t, D_CONV, SEG_CG[0] // D_CONV), _vecspec(D_CONV)],
        out_specs=[_rowspec(t, D_MLA), pl.BlockSpec((N_HEADS, t, 1), lambda i: (0, i, 0)),
                   _rowspec(t, D_CONV), _vecspec(D_CONV), _rowspec(t, gates, SEG_MG[0] // gates)],
        out_shape=[_sds((s, D_MLA), MXU_DTYPE), _sds((N_HEADS, s, 1), F32),
                   _sds((s, D_CONV), MXU_DTYPE), _sds((1, D_CONV), F32), _sds((s, IN_PAD), MXU_DTYPE)],
        compiler_params=_cp(("arbitrary",)),
    )(dy, w_out, o, z, u4m, z, b_pw)


def _conv_bwd(du3, u1, z, dz, glu_b, dw_w, ln_g, ln_b, *, name):
    s = z.shape[0]
    t = min(CONV_T, s)
    c2 = 2 * D_CONV
    hb = t // HALO
    n_blk = s // t
    last_halo = s // HALO - 1

    def body(d3m_ref, d3h_ref, u1m_ref, u1h_ref, zm_ref, zh_ref, gb_ref, w_ref, g_ref, b_ref, dz_in_ref,
             dci_ref, gg_ref, gbn_ref, gwb_ref, ggb_ref, gw_ref, dext, uext, du0_s, gw_acc):
        i = pl.program_id(0)
        _acc_init([gg_ref, gbn_ref, gwb_ref, ggb_ref, gw_acc])

        def ln_bwd(d3, u1v):
            mu = jnp.mean(u1v, axis=-1, keepdims=True)
            cen = u1v - mu
            rstd = lax.rsqrt(jnp.mean(cen * cen, axis=-1, keepdims=True) + EPS)
            uh = cen * rstd
            d2 = d3 * _dsilu(uh * g_ref[...] + b_ref[...])
            dh = d2 * g_ref[...]
            d1 = rstd * (dh - jnp.mean(dh, axis=-1, keepdims=True) - uh * jnp.mean(dh * uh, axis=-1, keepdims=True))
            return d1, d2, uh

        d1, d2, uh = ln_bwd(d3m_ref[...], u1m_ref[...])
        gg_ref[...] += _colsum(d2 * uh)
        gbn_ref[...] += _colsum(d2)
        gwb_ref[...] += _colsum(d1)
        dext[0, 0:t, :] = d1
        d1h, _, _ = ln_bwd(d3h_ref[...], u1h_ref[...])
        dext[0, t:, :] = jnp.where(i < n_blk - 1, d1h, 0.0)
        _shifted_copies(dext)

        def glu_parts(zv):
            ci = zv + gb_ref[...]
            return ci[:, :D_CONV], jax.nn.sigmoid(ci[:, D_CONV:])

        val, sg = glu_parts(zm_ref[...])
        uext[0, HALO:, :] = val * sg
        valh, sgh = glu_parts(zh_ref[...])
        uext[0, 0:HALO, :] = jnp.where(i > 0, valh * sgh, 0.0)
        _shifted_copies(uext)

        for rc in range(0, t, CONV_RC):
            for lc in range(0, D_CONV, CONV_LC):
                acc = None
                for off, win in _windows(dext, [rc + k for k in range(CONV_K)], CONV_RC, lc, CONV_LC):
                    k = (CONV_K - 1) - (off - rc)
                    term = w_ref[k:k + 1, lc:lc + CONV_LC] * win
                    acc = term if acc is None else acc + term
                du0_s[rc:rc + CONV_RC, lc:lc + CONV_LC] = acc
                dchunk = dext[0, rc:rc + CONV_RC, lc:lc + CONV_LC]
                first = rc + HALO - (CONV_K - 1)
                for off, win in _windows(uext, [first + k for k in range(CONV_K)], CONV_RC, lc, CONV_LC):
                    k = off - first
                    pr = dchunk * win
                    part = pr[0:8]
                    for r8 in range(8, CONV_RC, 8):
                        part = part + pr[r8:r8 + 8]
                    gw_acc[k, :, lc:lc + CONV_LC] += part

        du0 = du0_s[...]
        dval = du0 * sg
        dgt = du0 * val * sg * (1.0 - sg)
        dci_ref[:, :D_CONV] = dval.astype(dci_ref.dtype)
        dci_ref[:, D_CONV:] = dgt.astype(dci_ref.dtype)
        ggb_ref[:, :D_CONV] += _colsum(dval)
        ggb_ref[:, D_CONV:] += _colsum(dgt)

        @pl.when(i == n_blk - 1)
        def _():
            gw_ref[...] = jnp.sum(gw_acc[...], axis=1)

    halo_next = lambda w: pl.BlockSpec((HALO, w), lambda i: (jnp.minimum((i + 1) * hb, last_halo), 0))
    return pl.pallas_call(
        body, name=name, grid=(n_blk,),
        in_specs=[_rowspec(t, D_CONV), halo_next(D_CONV), _rowspec(t, D_CONV), halo_next(D_CONV),
                  _rowspec(t, c2), pl.BlockSpec((HALO, c2), lambda i: (jnp.maximum(i * hb - 1, 0), 0)),
                  _vecspec(c2), pl.BlockSpec((HALO, D_CONV), lambda i: (0, 0)), _vecspec(D_CONV), _vecspec(D_CONV),
                  _ANY],
        out_specs=[_rowspec(t, c2, SEG_CI[0] // c2), _vecspec(D_CONV), _vecspec(D_CONV), _vecspec(D_CONV),
                   _vecspec(c2), pl.BlockSpec((HALO, D_CONV), lambda i: (0, 0))],
        out_shape=[_sds(dz.shape, dz.dtype), _sds((1, D_CONV), F32), _sds((1, D_CONV), F32), _sds((1, D_CONV), F32),
                   _sds((1, c2), F32), _sds((HALO, D_CONV), F32)],
        scratch_shapes=[pltpu.VMEM((8, t + HALO, D_CONV), F32), pltpu.VMEM((8, t + HALO, D_CONV), F32),
                        pltpu.VMEM((t, D_CONV), F32), pltpu.VMEM((HALO, 8, D_CONV), F32)],
        input_output_aliases={10: 0},
        compiler_params=_cp(("arbitrary",)),
    )(du3, du3, u1, u1, z, z, glu_b, dw_w, ln_g, ln_b, dz)


def _flash_bwd(qf, kf, va, do, lse_t, delta_t, *, name):
    nh, s, dk = qf.shape
    dv = va.shape[-1] // 2
    t = min(ATT_T, s)
    n = s // t
    nt = (((1,), (1,)), ((), ()))
    tn = (((0,), (0,)), ((), ()))

    def body(q_ref, do_ref, lse_ref, dl_ref, k_ref, v_ref, dq_ref, dk_ref, dv_ref,
             dk_s, dv_s, st_buf, dpt_buf):
        n_un = pl.program_id(1)
        j = n - 1 - n_un
        nxt = jnp.maximum(j - 1, 0)

        @pl.when(n_un == 0)
        def _():
            dq_ref[...] = jnp.zeros(dq_ref.shape, F32)

        dk_s[...] = jnp.zeros(dk_s.shape, F32)
        dv_s[...] = jnp.zeros(dv_s.shape, F32)

        def rows_at(blk):
            return pl.ds(pl.multiple_of(blk * t, t), t)

        def rows_of(b):
            return rows_at(n - 1 - b)

        k = k_ref[0, rows_at(j), :]

        def produce(kj, b, slot):
            rows = rows_of(b)
            st_buf[slot] = lax.dot_general(k_ref[0, rows_at(kj), :], q_ref[0, rows, :], nt,
                                           preferred_element_type=F32)
            dpt_buf[slot] = lax.dot_general(v_ref[0, rows_at(kj), 0:dv], do_ref[rows, :], nt,
                                            preferred_element_type=F32)

        def consume(b, slot, masked):
            i = n - 1 - b
            rows = rows_of(b)
            q, dov = q_ref[0, rows, :], do_ref[rows, :]
            pt = jnp.exp(st_buf[slot] - lse_ref[0, i])
            if masked:
                key = lax.broadcasted_iota(jnp.int32, (t, t), 0)
                qry = lax.broadcasted_iota(jnp.int32, (t, t), 1)
                pt = jnp.where(key <= qry, pt, 0.0)
            dv_s[...] += jnp.dot(pt.astype(MXU_DTYPE), dov, preferred_element_type=F32)
            dst = (pt * (dpt_buf[slot] - dl_ref[0, i])).astype(MXU_DTYPE)
            dk_s[...] += jnp.dot(dst, q, preferred_element_type=F32)
            dq_ref[0, rows, :] += lax.dot_general(dst, k, tn, preferred_element_type=F32)

        @pl.when(n_un == 0)
        def _():
            produce(j, 0, 2)
            consume(0, 2, True)
            produce(nxt, 0, 2)

        @pl.when(n_un > 0)
        def _():
            produce(j, 1, 1)
            consume(0, 2, False)

            def pair(a, carry):
                produce(j, 2 * a + 2, 0)
                consume(2 * a + 1, 1, False)
                produce(j, 2 * a + 3, 1)
                consume(2 * a + 2, 0, False)
                return carry

            lax.fori_loop(0, (n_un - 1) // 2, pair, 0)

            @pl.when(n_un % 2 == 1)
            def _():
                produce(nxt, 0, 2)
                consume(n_un, 1, True)

            @pl.when(n_un % 2 == 0)
            def _():
                produce(j, n_un, 0)
                consume(n_un - 1, 1, False)
                produce(nxt, 0, 2)
                consume(n_un, 0, True)

        dk_ref[0] = dk_s[...]
        dv_ref[0] = dv_s[...]

    head = lambda h, j: (h, 0, 0)
    rowv = pl.BlockSpec((1, n, 1, t), lambda h, j: (h, 0, 0, 0))
    return pl.pallas_call(
        body, name=name, grid=(nh, n),
        in_specs=[pl.BlockSpec((1, s, dk), head),
                  pl.BlockSpec((s, dv), lambda h, j: (0, h)),
                  rowv, rowv,
                  pl.BlockSpec((1, s, dk), head),
                  pl.BlockSpec((1, s, 2 * dv), head)],
        out_specs=[pl.BlockSpec((1, s, dk), head),
                   pl.BlockSpec((1, t, dk), lambda h, g: (h, n - 1 - g, 0)),
                   pl.BlockSpec((1, t, dv), lambda h, g: (h, n - 1 - g, 0))],
        out_shape=[_sds((nh, s, dk), F32), _sds((nh, s, dk), F32), _sds((nh, s, dv), F32)],
        scratch_shapes=[pltpu.VMEM((t, dk), F32), pltpu.VMEM((t, dv), F32),
                        pltpu.VMEM((3, t, t), F32), pltpu.VMEM((3, t, t), F32)],
        compiler_params=_cp(("arbitrary", "arbitrary")),
    )(qf, do, lse_t, delta_t, kf, va)


def _qk_bwd(dqf, dkf, dvf, q_raw, kv, z, c_t, s1_t, s2_t, gqn, gqr, gkn, gkr, *, name):
    s = q_raw.shape[0]
    t = min(ROW_T, s)
    scale = 1.0 / math.sqrt(QK_DIM)

    def body(dq_ref, dk_ref, dv_ref, q_ref, kv_ref, kr_ref, c_ref, s1_ref, s2_ref,
             gqn_ref, gqr_ref, gkn_ref, gkr_ref, dqr_ref, dkv_ref, dkr_ref, ggq_ref, ggk_ref):
        _acc_init([ggq_ref, ggk_ref])
        c_v, s1_v, s2_v = c_ref[...], s1_ref[...], s2_ref[...]
        kr = kr_ref[...]
        kr_ss = _lanesum(kr * kr)
        dkr = jnp.zeros(kr.shape, F32)
        ggq_n = ggq_r = ggk_n = ggk_r = jnp.zeros((1, LANE), F32)

        def norm_bwd(n, r, rs, dyn, dyr, gn, gr):
            nh_, rh_ = n * rs, r * rs
            dnh, drh = dyn * gn, dyr * gr
            dot = (_lanesum(dnh * nh_) + _lanesum(drh * rh_)) * (1.0 / QK_DIM)
            return rs * (dnh - nh_ * dot), rs * (drh - rh_ * dot), _colsum(dyn * nh_), _colsum(dyr * rh_)

        for h in range(N_HEADS):
            n = q_ref[:, h * LANE:(h + 1) * LANE]
            r = q_ref[:, N_HEADS * LANE + h * LANE:N_HEADS * LANE + (h + 1) * LANE]
            rs = lax.rsqrt((_lanesum(n * n) + _lanesum(r * r)) * (1.0 / QK_DIM) + EPS)
            dyn = dq_ref[h, :, 0:LANE] * scale
            dyr = _rope_bwd(dq_ref[h, :, LANE:HEAD_PAD] * scale, c_v, s1_v, s2_v)
            dn, dr, g_n, g_r = norm_bwd(n, r, rs, dyn, dyr, gqn_ref[...], gqr_ref[...])
            dqr_ref[:, h * LANE:(h + 1) * LANE] = dn.astype(dqr_ref.dtype)
            dqr_ref[:, N_HEADS * LANE + h * LANE:N_HEADS * LANE + (h + 1) * LANE] = dr.astype(dqr_ref.dtype)
            ggq_n, ggq_r = ggq_n + g_n, ggq_r + g_r

            n = kv_ref[:, h * 2 * LANE:h * 2 * LANE + LANE]
            rs = lax.rsqrt((_lanesum(n * n) + kr_ss) * (1.0 / QK_DIM) + EPS)
            dyn = dk_ref[h, :, 0:LANE]
            dyr = _rope_bwd(dk_ref[h, :, LANE:HEAD_PAD], c_v, s1_v, s2_v)
            dn, dr, g_n, g_r = norm_bwd(n, kr, rs, dyn, dyr, gkn_ref[...], gkr_ref[...])
            dkv_ref[:, h * 2 * LANE:h * 2 * LANE + LANE] = dn.astype(dkv_ref.dtype)
            dkv_ref[:, h * 2 * LANE + LANE:(h + 1) * 2 * LANE] = dv_ref[h].astype(dkv_ref.dtype)
            dkr = dkr + dr
            ggk_n, ggk_r = ggk_n + g_n, ggk_r + g_r

        dkr_ref[...] = dkr.astype(dkr_ref.dtype)
        ggq_ref[:, 0:LANE] += ggq_n
        ggq_ref[:, LANE:] += ggq_r
        ggk_ref[:, 0:LANE] += ggk_n
        ggk_ref[:, LANE:] += ggk_r

    hspec = lambda w: pl.BlockSpec((N_HEADS, t, w), lambda i: (0, i, 0))
    wide = 2 * N_HEADS * LANE
    return pl.pallas_call(
        body, name=name, grid=(s // t,),
        in_specs=[hspec(HEAD_PAD), hspec(HEAD_PAD), hspec(V_DIM), _rowspec(t, wide), _rowspec(t, wide),
                  _rowspec(t, LANE, SEG_KR[0] // LANE), _rowspec(t, LANE), _rowspec(t, LANE), _rowspec(t, LANE),
                  _vecspec(LANE), _vecspec(LANE), _vecspec(LANE), _vecspec(LANE)],
        out_specs=[_rowspec(t, wide), _rowspec(t, wide), _rowspec(t, LANE), _vecspec(2 * LANE), _vecspec(2 * LANE)],
        out_shape=[_sds((s, wide), MXU_DTYPE), _sds((s, wide), MXU_DTYPE), _sds((s, LANE), MXU_DTYPE),
                   _sds((1, 2 * LANE), F32), _sds((1, 2 * LANE), F32)],
        compiler_params=_cp(("arbitrary",)),
    )(dqf, dkf, dvf, q_raw, kv, z, c_t, s1_t, s2_t, gqn, gqr, gkn, gkr)


def _lat_bwd(dqn, dkn, dkr, z, dz, g_ql, g_kvl, *, name):
    s = z.shape[0]
    t = min(ROW_T, s)
    o_ql, o_kvl, o_kr = (seg[0] - SEG_LAT[0] for seg in (SEG_QL, SEG_KVL, SEG_KR))

    def body(dq_ref, dk_ref, dkr_ref, ql_ref, kvl_ref, gq_ref, gk_ref, dz_in_ref, dz_ref, ggq_ref, ggk_ref):
        _acc_init([ggq_ref, ggk_ref])
        for d_ref, src, g_ref, off, gg_ref in ((dq_ref, ql_ref, gq_ref, o_ql, ggq_ref),
                                               (dk_ref, kvl_ref, gk_ref, o_kvl, ggk_ref)):
            v, dy = src[...], d_ref[...]
            r = lax.rsqrt(jnp.mean(v * v, axis=-1, keepdims=True) + EPS)
            vh = v * r
            dvh = dy * g_ref[...]
            dz_ref[:, off:off + v.shape[1]] = (
                r * (dvh - vh * jnp.mean(dvh * vh, axis=-1, keepdims=True))).astype(dz_ref.dtype)
            gg_ref[...] += _colsum(dy * vh)
        dz_ref[:, o_kr:o_kr + LANE] = dkr_ref[...]
        dz_ref[:, o_kr + LANE:] = jnp.zeros((t, SEG_LAT[1] - o_kr - LANE), dz_ref.dtype)

    return pl.pallas_call(
        body, name=name, grid=(s // t,),
        in_specs=[_rowspec(t, Q_LORA), _rowspec(t, KV_LORA), _rowspec(t, LANE),
                  _rowspec(t, Q_LORA, SEG_QL[0] // Q_LORA), _rowspec(t, KV_LORA, SEG_KVL[0] // KV_LORA),
                  _vecspec(Q_LORA), _vecspec(KV_LORA), _ANY],
        out_specs=[_rowspec(t, SEG_LAT[1], SEG_LAT[0] // SEG_LAT[1]), _vecspec(Q_LORA), _vecspec(KV_LORA)],
        out_shape=[_sds(dz.shape, dz.dtype), _sds((1, Q_LORA), F32), _sds((1, KV_LORA), F32)],
        input_output_aliases={7: 0},
        compiler_params=_cp(("arbitrary",)),
    )(dqn, dkn, dkr, z, z, g_ql, g_kvl, dz)


def _prenorm_bwd(dh, x, gxo, g, sc1p, *, name):
    s, d = x.shape
    t = min(2 * ROW_T, s)

    def body(dh_ref, x_ref, gx_ref, g_ref, sc_ref, dx_ref, dsh_ref, dsc_ref, gg_ref):
        _acc_init([dsh_ref, dsc_ref, gg_ref])
        xv, dhv = x_ref[...], dh_ref[...]
        r = lax.rsqrt(jnp.mean(xv * xv, axis=-1, keepdims=True) + EPS)
        xn = xv * r
        dsh_ref[...] += _colsum(dhv)
        dsc_ref[...] += _colsum(dhv * (xn * g_ref[...]))
        dm = dhv * sc_ref[...]
        gg_ref[...] += _colsum(dm * xn)
        dxn = dm * g_ref[...]
        dx_ref[...] = gx_ref[...] + r * (dxn - xn * jnp.mean(dxn * xn, axis=-1, keepdims=True))

    return pl.pallas_call(
        body, name=name, grid=(s // t,),
        in_specs=[_rowspec(t, d), _rowspec(t, d), _rowspec(t, d), _vecspec(d), _vecspec(d)],
        out_specs=[_rowspec(t, d), _vecspec(d), _vecspec(d), _vecspec(d)],
        out_shape=[_sds((s, d), F32), _sds((1, d), F32), _sds((1, d), F32), _sds((1, d), F32)],
        compiler_params=_cp(("arbitrary",)),
    )(dh, x, gxo, g, sc1p)


def _ada_fwd(c_all, ada_w, ada_b_cols, *, name):
    nl, d, cols = ada_w.shape

    def body(c_ref, w_ref, b_ref, o_ref):
        ca = _silu(c_ref[...]).astype(MXU_DTYPE)
        o_ref[0] = jnp.dot(ca, w_ref[0].astype(MXU_DTYPE), preferred_element_type=F32) + b_ref[0]

    return pl.pallas_call(
        body, name=name, grid=(nl,),
        in_specs=[pl.BlockSpec((N_DEV, d), lambda l: (0, 0)), pl.BlockSpec((1, d, cols), lambda l: (l, 0, 0)),
                  pl.BlockSpec((1, 1, cols), lambda l: (l, 0, 0))],
        out_specs=pl.BlockSpec((1, N_DEV, cols), lambda l: (l, 0, 0)),
        out_shape=_sds((nl, N_DEV, cols), F32),
        compiler_params=_cp(("parallel",)),
    )(c_all, ada_w, ada_b_cols)


def _ada_bwd(c_all_t, dmod_cols, *, name):
    nl, _, cols = dmod_cols.shape
    d = c_all_t.shape[0]

    def body(c_ref, dm_ref, o_ref):
        ca = _silu(c_ref[...]).astype(MXU_DTYPE)
        o_ref[0] = jnp.dot(ca, dm_ref[0].astype(MXU_DTYPE), preferred_element_type=F32)

    return pl.pallas_call(
        body, name=name, grid=(nl,),
        in_specs=[pl.BlockSpec((d, N_DEV), lambda l: (0, 0)), pl.BlockSpec((1, N_DEV, cols), lambda l: (l, 0, 0))],
        out_specs=pl.BlockSpec((1, d, cols), lambda l: (l, 0, 0)),
        out_shape=_sds((nl, d, cols), F32),
        compiler_params=_cp(("parallel",)),
    )(c_all_t, dmod_cols)


def _adamw_math(g, w, m, v):
    mn = ADAM_B1 * m + (1.0 - ADAM_B1) * g
    vn = ADAM_B2 * v + (1.0 - ADAM_B2) * (g * g)
    m_hat = mn / (1.0 - ADAM_B1 ** ADAM_STEP)
    v_hat = vn / (1.0 - ADAM_B2 ** ADAM_STEP)
    return -ADAM_LR * (m_hat / (jnp.sqrt(v_hat) + ADAM_EPS) + ADAM_WD * w), mn, vn


def _adamw_small(items, *, name):
    n = len(items)
    shapes = [it[1].shape for it in items]
    flat = lambda a, lead: a.reshape(lead + (-1, a.shape[-1]))
    operands = []
    for gp, w, m, v in items:
        operands += [flat(gp, (gp.shape[0],)), flat(w, ()), flat(m, ()), flat(v, ())]
    nparts = [it[0].shape[0] for it in items]

    def body(*refs):
        ins, outs = refs[:4 * n], refs[4 * n:]
        for i in range(n):
            g_ref, w_ref, m_ref, v_ref = ins[4 * i:4 * i + 4]
            g = g_ref[0].astype(F32)
            for p in range(1, nparts[i]):
                g = g + g_ref[p].astype(F32)
            outs[4 * i][...] = g
            outs[4 * i + 1][...], outs[4 * i + 2][...], outs[4 * i + 3][...] = _adamw_math(
                g, w_ref[...], m_ref[...], v_ref[...])

    out_shape = []
    for it in items:
        out_shape += [_sds(flat(it[1], ()).shape, F32)] * 4
    outs = pl.pallas_call(body, name=name, out_shape=out_shape, compiler_params=_cp())(*operands)
    return [tuple(o.reshape(shp) for o in outs[4 * i:4 * i + 4]) for i, shp in enumerate(shapes)]


def _adamw(gparts, w, m, v, *, name):
    shape = w.shape
    cols = shape[-1]
    per_layer = isinstance(gparts, (list, tuple))
    nl = shape[0] if per_layer else 1
    rows = w.size // cols // nl
    glist = list(gparts) if per_layer else [gparts]
    npart = glist[0].shape[0]
    glist = [g.reshape(npart, rows, cols) for g in glist]
    w3, m3, v3 = (a.reshape(nl, rows, cols) for a in (w, m, v))
    budget = 2 * 1024 * 1024
    fits = [t for t in range(min(rows, 256) // 8 * 8, 7, -8)
            if rows % t == 0 and npart * t * cols * glist[0].dtype.itemsize <= budget]
    t = fits[0] if fits else rows
    nb = rows // t

    def body(*refs):
        g_refs = refs[:nl]
        w_ref, m_ref, v_ref, go_ref, d_ref, mo_ref, vo_ref, g_s = refs[nl:]
        layer = pl.program_id(0)
        for l in range(nl):
            @pl.when(layer == l)
            def _(l=l):
                g = g_refs[l][0].astype(F32)
                for p in range(1, npart):
                    g = g + g_refs[l][p].astype(F32)
                g_s[...] = g

        g = g_s[...]
        go_ref[0] = g
        d_ref[0], mo_ref[0], vo_ref[0] = _adamw_math(g, w_ref[0], m_ref[0], v_ref[0])

    def g_map(l):
        return lambda layer, i: (0, jnp.where(layer == l, i, jnp.where(layer < l, 0, nb - 1)), 0)

    spec = pl.BlockSpec((1, t, cols), lambda layer, i: (layer, i, 0))
    outs = pl.pallas_call(
        body, name=name, grid=(nl, nb),
        in_specs=[pl.BlockSpec((npart, t, cols), g_map(l)) for l in range(nl)] + [spec, spec, spec],
        out_specs=[spec] * 4, out_shape=[_sds((nl, rows, cols), F32)] * 4,
        scratch_shapes=[pltpu.VMEM((t, cols), F32)],
        compiler_params=_cp(("arbitrary", "arbitrary")),
    )(*glist, w3, m3, v3)
    return tuple(o.reshape(shape) for o in outs)


_ANY = pl.BlockSpec(memory_space=pl.ANY)


def _all_gather(blocks, *, name):
    na = len(blocks)

    def body(*refs):
        x_refs, out_refs = refs[:na], refs[na:2 * na]
        send_sems, recv_sems, local_sems = refs[2 * na:]
        x, y, c = lax.axis_index("x"), lax.axis_index("y"), lax.axis_index("c")
        me, sibling = (x, y, c), (x, y, 1 - c)
        chips = [(1 - x, y), (x, 1 - y), (1 - x, 1 - y)]

        def slot(a, px, py, pc):
            return out_refs[a].at[4 * px + 2 * py + pc]

        def copy(a, k, blk, to, src=None):
            return pltpu.make_async_remote_copy(
                src_ref=slot(a, *blk) if src is None else src, dst_ref=slot(a, *blk),
                send_sem=send_sems.at[7 * a + k], recv_sem=recv_sems.at[7 * a + k],
                device_id=to, device_id_type=MESH_ID)

        mine = [pltpu.make_async_copy(x_refs[a], slot(a, *me), local_sems.at[a]) for a in range(na)]
        for cp in mine:
            cp.start()
        first = []
        for a in range(na):
            first.append(copy(a, 0, me, sibling, src=x_refs[a]))
            first += [copy(a, 1 + j, me, (*chip, c), src=x_refs[a]) for j, chip in enumerate(chips)]
        for cp in first:
            cp.start()
        passed = []
        for a in range(na):
            for j, chip in enumerate(chips):
                copy(a, 1 + j, (*chip, c), me).wait_recv()
                fwd = copy(a, 4 + j, (*chip, c), sibling)
                fwd.start()
                passed.append(fwd)
        for a in range(na):
            copy(a, 0, sibling, me).wait_recv()
            for j, chip in enumerate(chips):
                copy(a, 4 + j, (*chip, 1 - c), me).wait_recv()
        for cp in first + passed:
            cp.wait_send()
        for cp in mine:
            cp.wait()

    outs = pl.pallas_call(
        body, name=name, in_specs=[_ANY] * na, out_specs=[_ANY] * na,
        out_shape=[_sds((N_DEV,) + b.shape, b.dtype) for b in blocks],
        scratch_shapes=[pltpu.SemaphoreType.DMA((7 * na,)), pltpu.SemaphoreType.DMA((7 * na,)),
                        pltpu.SemaphoreType.DMA((na,))],
    )(*blocks)
    return list(outs)


_HBM = pl.BlockSpec(memory_space=pltpu.HBM)
_SEM = pl.BlockSpec(memory_space=pltpu.SEMAPHORE)
_EFFECT = pltpu.SideEffectType.DATAFLOW_SIDE_EFFECTING


def _peers(x, y, c):
    out = []
    for k in range(1, N_DEV):
        out.append((1 - x if k & 4 else x, 1 - y if k & 2 else y, 1 - c if k & 1 else c))
    return out


def _own_slots(srcs, scatter, *, name, after=None):
    na = len(srcs)
    n_extra = 0 if after is None else 1
    me = (4 * lax.axis_index("x") + 2 * lax.axis_index("y") + lax.axis_index("c")).astype(jnp.int32).reshape(1)

    def body(me_ref, *refs):
        in_refs, out_refs = refs[:na], refs[na + n_extra:]
        for a in range(na):
            out_refs[a][0] = in_refs[a][0] if scatter else in_refs[a][...]

    def slot_spec(shard):
        zeros = (0,) * len(shard)
        return pl.BlockSpec((1,) + tuple(shard), lambda i, me_ref: (me_ref[0],) + zeros)

    def whole_spec(shape):
        zeros = (0,) * len(shape)
        return pl.BlockSpec(tuple(shape), lambda i, me_ref: zeros)

    shards = [s.shape[1:] if scatter else s.shape for s in srcs]
    in_specs = [slot_spec(sh) if scatter else whole_spec(sh) for sh in shards] + [_ANY] * n_extra
    outs = pl.pallas_call(
        body, name=name,
        grid_spec=pltpu.PrefetchScalarGridSpec(
            num_scalar_prefetch=1, grid=(1,), in_specs=in_specs, out_specs=[slot_spec(sh) for sh in shards]),
        out_shape=[_sds((N_DEV,) + tuple(sh), s.dtype) for sh, s in zip(shards, srcs)],
        compiler_params=_cp(("arbitrary",)),
    )(me, *srcs, *([] if after is None else [after]))
    return list(outs)


_N_COPIES = dict(scatter=7, gather=7, chips=4, forward=3)


def _exchange_copies(src_refs, land_refs, send_sems, recv_sems, mode):
    x, y, c = lax.axis_index("x"), lax.axis_index("y"), lax.axis_index("c")
    me = 4 * x + 2 * y + c
    nc = _N_COPIES[mode]
    chips = [(1 - x, y), (x, 1 - y), (1 - x, 1 - y)]
    cps = []
    for a in range(len(land_refs)):
        if mode in ("scatter", "gather"):
            plan = [((src_refs[a].at[4 * px + 2 * py + pc] if mode == "scatter" else src_refs[a]),
                     land_refs[a].at[me], (px, py, pc)) for px, py, pc in _peers(x, y, c)]
        elif mode == "chips":
            plan = [(src_refs[a], land_refs[a].at[me], to) for to in [(x, y, 1 - c)] + [(*ch, c) for ch in chips]]
        else:
            plan = [(land_refs[a].at[4 * px + 2 * py + c], land_refs[a].at[4 * px + 2 * py + c], (x, y, 1 - c))
                    for px, py in chips]
        for k, (src, dst, to) in enumerate(plan):
            cps.append(pltpu.make_async_remote_copy(
                src_ref=src, dst_ref=dst, send_sem=send_sems.at[nc * a + k], recv_sem=recv_sems.at[nc * a + k],
                device_id=to, device_id_type=MESH_ID))
    return cps


def _exchange_start(srcs, lands, mode, *, name):
    ns, nz = len(srcs), len(lands)
    nsem = _N_COPIES[mode] * nz

    def body(*refs):
        src_refs, land_refs = refs[:ns], refs[ns:ns + nz]
        send_sems, recv_sems = refs[ns + nz], refs[ns + nz + 1]
        token = refs[-1]
        for cp in _exchange_copies(src_refs, land_refs, send_sems, recv_sems, mode):
            cp.start()
        token[...] = jnp.zeros(token.shape, token.dtype)

    hbm = lambda a: pltpu.HBM(a.shape, a.dtype)
    outs = pl.pallas_call(
        body, name=name,
        out_shape=(pltpu.SemaphoreType.DMA((nsem,)), pltpu.SemaphoreType.DMA((nsem,)),
                   *[hbm(a) for a in srcs], *[hbm(a) for a in lands], _sds((8, LANE), F32)),
        in_specs=[_HBM] * (ns + nz),
        out_specs=(_SEM, _SEM, *[_HBM] * (ns + nz), pl.BlockSpec(memory_space=pltpu.VMEM)),
        input_output_aliases={i: 2 + i for i in range(ns + nz)},
        compiler_params=pltpu.CompilerParams(has_side_effects=_EFFECT),
    )(*[pltpu.with_memory_space_constraint(a, pltpu.HBM) for a in list(srcs) + list(lands)])
    return outs[0], outs[1], list(outs[2:2 + ns]), list(outs[2 + ns:2 + ns + nz]), outs[-1]


def _exchange_wait(send_sems, recv_sems, srcs, lands, after, mode, *, name):
    ns, nz = len(srcs), len(lands)

    def body(*refs):
        src_refs, land_refs = refs[:ns], refs[ns:ns + nz]
        s_sems, r_sems = refs[ns + nz], refs[ns + nz + 1]
        for cp in _exchange_copies(src_refs, land_refs, s_sems, r_sems, mode):
            cp.wait_send()
            cp.wait_recv()

    hbm = lambda a: pltpu.HBM(a.shape, a.dtype)
    outs = pl.pallas_call(
        body, name=name,
        out_shape=(*[hbm(a) for a in srcs], *[hbm(a) for a in lands]),
        in_specs=[_HBM] * (ns + nz) + [_SEM, _SEM, _ANY],
        out_specs=tuple([_HBM] * (ns + nz)),
        input_output_aliases={i: i for i in range(ns + nz)},
        compiler_params=pltpu.CompilerParams(has_side_effects=_EFFECT),
    )(*srcs, *lands, send_sems, recv_sems, after)
    return list(outs[ns:])


_WIN_SEGS = (("ql", 0, Q_LORA, SEG_QL[0]), ("kvl", Q_LORA, KV_LORA, SEG_KVL[0]),
             ("kr", Q_LORA + KV_LORA, ROPE, SEG_KR[0]), ("mg", Q_LORA + KV_LORA + ROPE, D_MLA, SEG_MG[0]),
             ("ci", Q_LORA + KV_LORA + ROPE + D_MLA, 2 * D_CONV, SEG_CI[0]),
             ("cg", Q_LORA + KV_LORA + ROPE + D_MLA + 2 * D_CONV, D_CONV, SEG_CG[0]))
_WIN_SHARD = IN_COLS // N_DEV


def _win_pieces():
    out = []
    for _, o, n, new in _WIN_SEGS:
        for j in range(N_DEV):
            lo, hi = max(o, j * _WIN_SHARD), min(o + n, (j + 1) * _WIN_SHARD)
            if lo < hi:
                out.append((j, lo - j * _WIN_SHARD, new + lo - o, hi - lo))
    return out


WIN_T = 512


def _win_assemble(w_all, *, name):
    d = w_all.shape[2]
    t = min(WIN_T, d)
    pieces = sorted(_win_pieces(), key=lambda p: p[2])
    assert all(lo % 8 == 0 and n % 8 == 0 for _, lo, _, n in pieces)

    def body(w_ref, o_ref):
        rows = [w_ref[j].astype(F32)[lo:lo + n, :] for j, lo, _, n in pieces]
        rows.append(jnp.zeros((IN_PAD - (SEG_KR[0] + ROPE), t), F32))
        o_ref[...] = jnp.concatenate(rows, axis=0).astype(o_ref.dtype)

    return pl.pallas_call(
        body, name=name, grid=(d // t,),
        in_specs=[pl.BlockSpec((N_DEV, _WIN_SHARD, t), lambda i: (0, 0, i))],
        out_specs=pl.BlockSpec((IN_PAD, t), lambda i: (0, i)), out_shape=_sds((IN_PAD, d), w_all.dtype),
        compiler_params=_cp(("parallel",)),
    )(w_all)


def _win_split(grad, *, name):
    d = grad.shape[1]
    t = min(WIN_T, d)
    by_shard = [sorted([p for p in _win_pieces() if p[0] == j], key=lambda p: p[1]) for j in range(N_DEV)]

    def body(g_ref, o_ref):
        for j in range(N_DEV):
            rows = [g_ref[new:new + n, :] for _, _, new, n in by_shard[j]]
            o_ref[j] = jnp.concatenate(rows, axis=0).astype(o_ref.dtype)

    return pl.pallas_call(
        body, name=name, grid=(d // t,),
        in_specs=[pl.BlockSpec((IN_PAD, t), lambda i: (0, i))],
        out_specs=pl.BlockSpec((N_DEV, _WIN_SHARD, t), lambda i: (0, 0, i)),
        out_shape=_sds((N_DEV, _WIN_SHARD, d), WIRE_DTYPE),
        compiler_params=_cp(("parallel",)),
    )(grad)


def _cols_to_shards(a):
    r, n = a.shape
    return a.reshape(r, N_DEV, n // N_DEV).transpose(1, 0, 2)


def _shards_to_cols(a):
    nd, r, w = a.shape
    return a.transpose(1, 0, 2).reshape(r, nd * w)


def _win_permute(w_in):
    o_ql, o_kvl, o_kr, o_mg = 0, Q_LORA, Q_LORA + KV_LORA, Q_LORA + KV_LORA + ROPE
    o_ci = o_mg + D_MLA
    o_cg = o_ci + 2 * D_CONV
    seg = lambda o, n: w_in[:, o:o + n]
    pad = jnp.zeros((w_in.shape[0], IN_PAD - (SEG_KR[0] + ROPE)), w_in.dtype)
    return jnp.concatenate([seg(o_ci, 2 * D_CONV), seg(o_mg, D_MLA), seg(o_cg, D_CONV), seg(o_ql, Q_LORA),
                            seg(o_kvl, KV_LORA), seg(o_kr, ROPE), pad], axis=1)


def _win_unpermute(g):
    seg = lambda s, n=None: g[:, s[0]:s[0] + (s[1] if n is None else n)]
    return jnp.concatenate([seg(SEG_QL), seg(SEG_KVL), seg(SEG_KR, ROPE), seg(SEG_MG), seg(SEG_CI), seg(SEG_CG)], axis=1)


def _qup_permute(w):
    w3 = w.reshape(w.shape[0], N_HEADS, QK_DIM)
    nope = w3[:, :, :NOPE].reshape(w.shape[0], N_HEADS * NOPE)
    rope = jnp.pad(w3[:, :, NOPE:], ((0, 0), (0, 0), (0, LANE - ROPE))).reshape(w.shape[0], N_HEADS * LANE)
    return jnp.concatenate([nope, rope], axis=1)


def _qup_unpermute(g):
    r = g.shape[0]
    nope = g[:, :N_HEADS * NOPE].reshape(r, N_HEADS, NOPE)
    rope = g[:, N_HEADS * NOPE:].reshape(r, N_HEADS, LANE)[:, :, :ROPE]
    return jnp.concatenate([nope, rope], axis=2).reshape(r, N_HEADS * QK_DIM)


def _norm_tiles(g):
    return g[:NOPE].reshape(1, LANE), jnp.pad(g[NOPE:], (0, LANE - ROPE)).reshape(1, LANE)


def _norm_untile(gt):
    return jnp.concatenate([gt[0, :NOPE], gt[0, LANE:LANE + ROPE]])


def _rope_tiles(positions):
    inv_freq = 1.0 / (ROPE_THETA ** (jnp.arange(0, ROPE, 2, dtype=F32) / ROPE))
    ang = positions.astype(F32)[:, None] * inv_freq
    cos, sin = jnp.cos(ang), jnp.sin(ang)
    zq = jnp.zeros_like(cos)
    c_t = jnp.concatenate([cos, cos, zq, zq], axis=1)
    s1_t = jnp.concatenate([-sin, zq, zq, zq], axis=1)
    s2_t = jnp.concatenate([zq, sin, zq, zq], axis=1)
    return c_t, s1_t, s2_t


_BIG = ("w_in", "w_q_up", "w_kv_up", "w_pw", "w_out")
_COL_SHARDED = ("w_in", "w_q_up", "w_kv_up")


def _pack_rows(arrs):
    return jnp.concatenate([a.reshape(-1, LANE) for a in arrs], axis=0)


def _unpack_rows(buf, shapes):
    out, r0 = [], 0
    lead = buf.shape[:-2]
    for shp in shapes:
        n = math.prod(shp) // LANE
        out.append(buf[..., r0:r0 + n, :].reshape(lead + tuple(shp)))
        r0 += n
    return out


_SMALL = (("dmod", 3 * D_MODEL), ("norm_g", D_MODEL), ("q_lat_g", Q_LORA), ("kv_lat_g", KV_LORA),
          ("q_norm_g", 2 * LANE), ("k_norm_g", 2 * LANE), ("glu_b", 2 * D_CONV), ("dw_w", HALO * D_CONV),
          ("dw_b", D_CONV), ("conv_ln_g", D_CONV), ("conv_ln_b", D_CONV), ("b_pw", D_CONV))


def _layer_fwd(x, p, rope, l, early=None, late=None):
    n = lambda s: f"{s}_l{l}"
    c_t, s1_t, s2_t = rope
    h = _prenorm(x, p["norm_g"], p["shift"], p["sc1p"], name=n("prenorm"))
    if early is not None:
        p = {**p, **early(h)}
    z = _mm(h, p["w_in"], tb=True, name=n("in_proj"), tn=IN_TILE, n_outer=True,
            after=p.get("in_proj_after"))
    if late is not None:
        p = {**p, **late(z)}
    qn, kn, q_raw, kv, qf, kf, vf = _mla_pre(z, p["w_q_up"], p["w_kv_up"], p["q_lat_g"], p["kv_lat_g"],
                                             c_t, s1_t, s2_t, *p["qk_tiles"], name=n("mla_pre"))
    o, lse = _flash_fwd(qf, kf, vf, name=n("flash_fwd"))
    u1, u3 = _conv_fwd(z, p["glu_b"], p["dw_w"], p["dw_b"], p["conv_ln_g"], p["conv_ln_b"], name=n("conv_fwd"))
    u4m = _mm(u3, p["w_pw"], name=n("pw"), tn=1024)
    cat = _gate_cat(o, z, u4m, p["b_pw"], name=n("gate_cat"))
    y, x_next = _mm(cat, p["w_out"], name=n("out_proj"), tn=1024, residual=(x, p["gate"]))
    saved = dict(x=x, h=h, z=z, qn=qn, kn=kn, q_raw=q_raw, kv=kv, qf=qf, kf=kf, vf=vf, o=o, lse=lse,
                 u1=u1, u3=u3, u4m=u4m, cat=cat, y=y)
    return x_next, saved, p


def _layer_bwd(gxo, p, sv, rope, l, hook_rest=None, hook_w_in=None):
    n = lambda s: f"{s}_l{l}"
    c_t, s1_t, s2_t = rope
    z = sv["z"]
    dy, dgate = _out_bwd(gxo, sv["y"], p["gate"], name=n("out_bwd"))
    g_w_out = _mm(sv["cat"], dy, ta=True, name=n("g_w_out"), tm=1024, tn=1024)
    do, delta, du4, g_b_pw, dz = _gate_bwd(dy, p["w_out"], sv["o"], z, sv["u4m"], p["b_pw"], name=n("gate_bwd"))
    g_w_pw = _mm(sv["u3"], du4, ta=True, name=n("g_w_pw"), tm=1024, tn=1024, tk=512)
    du3 = _mm(du4, p["w_pw"], tb=True, name=n("d_u3"), tn=1024)
    dz, g_ln_g, g_ln_b, g_dw_b, g_glu_b, g_dw_w = _conv_bwd(
        du3, sv["u1"], z, dz, p["glu_b"], p["dw_w"], p["conv_ln_g"], p["conv_ln_b"], name=n("conv_bwd"))
    t_att = min(ATT_T, z.shape[0])
    to_lanes = lambda a: a.reshape(N_HEADS, z.shape[0] // t_att, 1, t_att)
    dqf, dkf, dvf = _flash_bwd(sv["qf"], sv["kf"], sv["vf"], do,
                               to_lanes(sv["lse"][:, :, 0]), to_lanes(delta), name=n("flash_bwd"))
    dq_raw, dkv, dkr, g_qn, g_kn = _qk_bwd(dqf, dkf, dvf, sv["q_raw"], sv["kv"], z, c_t, s1_t, s2_t,
                                            *p["qk_tiles"], name=n("qk_bwd"))
    g_w_q_up = _mm(sv["qn"], dq_raw, ta=True, name=n("g_w_q_up"), tm=512, tn=1024, tk=512)
    dqn = _mm(dq_raw, p["w_q_up"], tb=True, name=n("d_qn"))
    g_w_kv_up = _mm(sv["kn"], dkv, ta=True, name=n("g_w_kv_up"), tm=256, tn=1024, tk=512)
    dkn = _mm(dkv, p["w_kv_up"], tb=True, name=n("d_kn"))
    dz, g_ql, g_kvl = _lat_bwd(dqn, dkn, dkr, z, dz, p["q_lat_g"], p["kv_lat_g"], name=n("lat_bwd"))
    big = dict(w_q_up=g_w_q_up, w_kv_up=g_w_kv_up, w_pw=g_w_pw, w_out=g_w_out)
    after = None if hook_rest is None else hook_rest(big)
    g_w_in = _mm(dz, sv["h"], ta=True, name=n("g_w_in"), tm=512, tn=1024, after=after)
    big["w_in"] = g_w_in
    after = None if hook_w_in is None else hook_w_in(g_w_in)
    dh = _mm(dz, p["w_in"], name=n("d_h"), tn=1024, after=after)
    dx, dshift, dscale, g_norm = _prenorm_bwd(dh, sv["x"], gxo, p["norm_g"], p["sc1p"], name=n("prenorm_bwd"))
    small = dict(dmod=jnp.concatenate([dshift, dscale, dgate], axis=1), norm_g=g_norm, q_lat_g=g_ql, kv_lat_g=g_kvl,
                 q_norm_g=g_qn, k_norm_g=g_kn, glu_b=g_glu_b, dw_w=g_dw_w, dw_b=g_dw_b,
                 conv_ln_g=g_ln_g, conv_ln_b=g_ln_b, b_pw=g_b_pw)
    return dx, big, small


def _layer_params(l, full, mod_l, small):
    d = D_MODEL
    row = lambda a: a.reshape(1, -1)
    shift, scale, gate = mod_l[:, :d], mod_l[:, d:2 * d], mod_l[:, 2 * d:]
    dw_w = jnp.pad(full["dw_w"][l], ((0, HALO - CONV_K), (0, 0)))
    return dict(
        shift=shift, sc1p=1.0 + scale, gate=gate, norm_g=row(small["norm_g"][l]),
        **{k: full[k][l] for k in _BIG if k in full}, dw_w=dw_w,
        q_lat_g=row(small["q_lat_g"][l]), kv_lat_g=row(small["kv_lat_g"][l]),
        qk_tiles=_norm_tiles(small["q_norm_g"][l]) + _norm_tiles(small["k_norm_g"][l]),
        glu_b=row(small["glu_b"][l]), dw_b=row(small["dw_b"][l]), conv_ln_g=row(small["conv_ln_g"][l]),
        conv_ln_b=row(small["conv_ln_b"][l]), b_pw=row(small["b_pw"][l]))


def kernel(x, c, positions, ada_w, ada_b, norm_g, w_in, q_lat_g, w_q_up, kv_lat_g, w_kv_up, q_norm_g, k_norm_g, glu_b, dw_w, dw_b, conv_ln_g, conv_ln_b, w_pw, b_pw, w_out, loss_target, m_ada_w, m_ada_b, m_norm_g, m_w_in, m_q_lat_g, m_w_q_up, m_kv_lat_g, m_w_kv_up, m_q_norm_g, m_k_norm_g, m_glu_b, m_dw_w, m_dw_b, m_conv_ln_g, m_conv_ln_b, m_w_pw, m_b_pw, m_w_out, v_ada_w, v_ada_b, v_norm_g, v_w_in, v_q_lat_g, v_w_q_up, v_kv_lat_g, v_w_kv_up, v_q_norm_g, v_k_norm_g, v_glu_b, v_dw_w, v_dw_b, v_conv_ln_g, v_conv_ln_b, v_w_pw, v_b_pw, v_w_out):
    names = ("ada_w", "ada_b", "norm_g", "w_in", "q_lat_g", "w_q_up", "kv_lat_g", "w_kv_up", "q_norm_g",
             "k_norm_g", "glu_b", "dw_w", "dw_b", "conv_ln_g", "conv_ln_b", "w_pw", "b_pw", "w_out")
    w_loc = dict(zip(names, (ada_w, ada_b, norm_g, w_in, q_lat_g, w_q_up, kv_lat_g, w_kv_up, q_norm_g, k_norm_g,
                             glu_b, dw_w, dw_b, conv_ln_g, conv_ln_b, w_pw, b_pw, w_out)))
    m_loc = dict(zip(names, (m_ada_w, m_ada_b, m_norm_g, m_w_in, m_q_lat_g, m_w_q_up, m_kv_lat_g, m_w_kv_up,
                             m_q_norm_g, m_k_norm_g, m_glu_b, m_dw_w, m_dw_b, m_conv_ln_g, m_conv_ln_b, m_w_pw,
                             m_b_pw, m_w_out)))
    v_loc = dict(zip(names, (v_ada_w, v_ada_b, v_norm_g, v_w_in, v_q_lat_g, v_w_q_up, v_kv_lat_g, v_w_kv_up,
                             v_q_norm_g, v_k_norm_g, v_glu_b, v_dw_w, v_dw_b, v_conv_ln_g, v_conv_ln_b, v_w_pw,
                             v_b_pw, v_w_out)))
    nl, d = N_LAYERS, D_MODEL
    me = 4 * lax.axis_index("x") + 2 * lax.axis_index("y") + lax.axis_index("c")
    x2, tgt = x[0], loss_target[0]
    ada_cols = ada_w.shape[-1]

    tr = lambda a: jnp.swapaxes(a, 1, 2)
    w_loc, m_loc, v_loc = ({**dd, "w_in": tr(dd["w_in"])} for dd in (w_loc, m_loc, v_loc))
    w_in0 = [w_loc["w_in"][0].astype(WIRE_DTYPE)]
    fly_c = _exchange_start(w_in0, _own_slots(w_in0, False, name="own_w_in_l0"), "chips", name="gather_start_w_in_l0")
    held = dict(c=c, positions=positions, ada_b=ada_b, norm_g=norm_g, q_lat_g=q_lat_g, kv_lat_g=kv_lat_g,
                q_norm_g=q_norm_g, k_norm_g=k_norm_g, glu_b=glu_b, dw_w=dw_w, dw_b=dw_b, conv_ln_g=conv_ln_g,
                conv_ln_b=conv_ln_b, b_pw=b_pw, big={k: w_loc[k] for k in _BIG})
    tok_c, held = lax.optimization_barrier((fly_c[4], held))
    c, positions, ada_b, norm_g, q_lat_g, kv_lat_g, q_norm_g, k_norm_g, glu_b, dw_w, dw_b, conv_ln_g, conv_ln_b, b_pw = (
        held[k] for k in ("c", "positions", "ada_b", "norm_g", "q_lat_g", "kv_lat_g", "q_norm_g", "k_norm_g", "glu_b",
                          "dw_w", "dw_b", "conv_ln_g", "conv_ln_b", "b_pw"))
    wire = {k: held["big"][k].astype(WIRE_DTYPE) for k in _BIG}

    dw_pad = jnp.pad(dw_w, ((0, 0), (0, HALO - CONV_K), (0, 0)))
    c_rows = c.reshape(d // LANE, LANE) + tok_c[0:1, :]
    c_all, dw_all = _all_gather([c_rows, dw_pad], name="gather_c")
    c_all = c_all.reshape(N_DEV, d)
    ada_b_cols = lax.dynamic_slice_in_dim(ada_b, me * ada_cols, ada_cols, axis=1).reshape(nl, 1, ada_cols)
    mod_cols = _ada_fwd(c_all, ada_w, ada_b_cols, name="ada_fwd")
    mod_all = _all_gather([mod_cols], name="gather_mod")[0]
    mod_me = lax.dynamic_index_in_dim(mod_all, me, axis=2, keepdims=False)
    mod = mod_me.transpose(1, 0, 2).reshape(nl, 1, N_DEV * ada_cols)

    from_chips = _exchange_wait(*fly_c[:4], mod, "chips", name="gather_wait_w_in_l0")
    fly_f = _exchange_start([], from_chips, "forward", name="forward_start_w_in_l0")
    w_in_all0 = _exchange_wait(*fly_f[:4], fly_f[4], "forward", name="forward_wait_w_in_l0")[0]
    rest0 = [wire[k][0] for k in _BIG[1:]]
    fly_r0, fly_w1 = {}, {}
    fly_r0["x"] = _exchange_start(rest0, _own_slots(rest0, False, name="own_weights_l0_rest", after=w_in_all0),
                                  "gather", name="gather_start_l0_rest")

    def layout_rest(parts):
        return dict(w_q_up=_qup_permute(_shards_to_cols(parts[0])), w_kv_up=_shards_to_cols(parts[1]),
                    w_pw=parts[2].reshape(D_CONV, D_CONV), w_out=parts[3].reshape(D_MLA + D_CONV, d))

    small_in = dict(norm_g=norm_g, q_lat_g=q_lat_g, kv_lat_g=kv_lat_g, q_norm_g=q_norm_g, k_norm_g=k_norm_g,
                    glu_b=glu_b, dw_b=dw_b, conv_ln_g=conv_ln_g, conv_ln_b=conv_ln_b, b_pw=b_pw)
    dw_full = [_shards_to_cols(dw_all[:, l])[:CONV_K] for l in range(nl)]
    rope = _rope_tiles(positions[0])

    def layer_params(l, w_in_all, rest, mod_l):
        full = dict(dw_w=dw_full)
        if w_in_all is not None:
            full["w_in"] = {l: _win_assemble(w_in_all, name=f"w_in_assemble_l{l}")}
        if rest is not None:
            full.update({k: {l: a} for k, a in layout_rest(rest).items()})
        return _layer_params(l, full, mod_l, small_in)

    def late_l0(z):
        parts = _exchange_wait(*fly_r0["x"][:4], z, "gather", name="gather_wait_l0_rest")
        src1 = [wire[k][1] for k in _BIG]
        fly_w1["x"] = _exchange_start(src1, _own_slots(src1, False, name="own_weights_l1", after=parts[0]), "gather",
                                      name="gather_start_l1")
        late = layout_rest(parts)
        late["q_lat_g"] = small_in["q_lat_g"][0].reshape(1, -1) + fly_w1["x"][4][0, 0]
        return late

    params, saved = [None] * nl, [None] * nl
    p0 = layer_params(0, w_in_all0, None, mod[0] + fly_r0["x"][4][0, 0])
    xs, saved[0], params[0] = _layer_fwd(x2, p0, rope, 0, late=late_l0)
    parts1 = _exchange_wait(*fly_w1["x"][:4], xs, "gather", name="gather_wait_l1")
    params[1] = layer_params(1, parts1[0], parts1[1:], mod[1])
    xs, saved[1], _ = _layer_fwd(xs, params[1], rope, 1)
    gx, loss_part = _loss_head(xs, tgt, name="loss_head")
    loss = lax.psum(loss_part[0, 0], ("x", "y", "c"))

    def shard_major(k, g):
        if k == "w_q_up":
            g = _qup_unpermute(g)
        if k in _COL_SHARDED:
            return _cols_to_shards(g)
        return g.reshape((N_DEV, g.shape[0] // N_DEV, g.shape[1]))

    def scatter_start(send, tag):
        lands = _own_slots(send, True, name=f"own_grads_{tag}")
        return _exchange_start(send, lands, "scatter", name=f"scatter_start_{tag}")

    def wire_rest(big):
        return [shard_major(k, big[k]).astype(WIRE_DTYPE) for k in _BIG[1:]]

    big_g, small_g, flying = [None] * nl, [None] * nl, {}
    gx, big_g[1], small_g[1] = _layer_bwd(gx, params[1], saved[1], rope, 1)
    flying["l1"] = scatter_start([_win_split(big_g[1]["w_in"], name="w_in_split_l1")] + wire_rest(big_g[1]), "l1")
    p0 = dict(params[0])
    p0["gate"] = p0["gate"] + flying["l1"][4][0, 0]

    def start_rest_l0(big):
        flying["l0_rest"] = scatter_start(wire_rest(big), "l0_rest")
        return flying["l0_rest"][4]

    res, arrived = {}, [None] * nl

    def start_w_in_l0(g_w_in):
        flying["l0_w_in"] = scatter_start([_win_split(g_w_in, name="w_in_split_l0")], "l0_w_in")
        tok = flying["l0_w_in"][4]
        arrived[1] = _exchange_wait(*flying["l1"][:4], tok, "scatter", name="scatter_wait_l1")
        arrived[0] = [None] + _exchange_wait(*flying["l0_rest"][:4], tok, "scatter", name="scatter_wait_l0_rest")
        for i, k in enumerate(_BIG):
            if i > 0:
                res[k] = _adamw([arrived[l][i] for l in range(nl)], w_loc[k], m_loc[k], v_loc[k], name=f"adamw_{k}")
        return res["w_out"][0]

    gx, big_g[0], small_g[0] = _layer_bwd(gx, p0, saved[0], rope, 0, hook_rest=start_rest_l0,
                                          hook_w_in=start_w_in_l0)

    tile = 8 * LANE
    padded = [(k, nn, -(-nn // tile) * tile) for k, nn in _SMALL]
    spk = jnp.concatenate([jnp.pad(small_g[l][k].reshape(-1), (0, np_ - nn)).reshape(-1, LANE)
                           for l in range(nl) for k, nn, np_ in padded], axis=0)
    s_all = _all_gather([spk], name="gather_small_grads")[0]
    s_rows = sum(np_ for _, _, np_ in padded) // LANE
    s_all = s_all.reshape(N_DEV, nl, s_rows, LANE)
    s_parts = {k: a[..., :nn] for (k, nn, _), a in
               zip(padded, _unpack_rows(s_all, [(np_,) for _, _, np_ in padded]))}

    dmod_all = s_parts["dmod"]
    dmod_cols = lax.dynamic_slice_in_dim(dmod_all, me * ada_cols, ada_cols, axis=2).transpose(1, 0, 2)
    g_ada_w = _ada_bwd(c_all.T, dmod_cols, name="ada_bwd")
    gp = {}
    gp["ada_w"] = g_ada_w[None]
    gp["ada_b"] = dmod_all
    for k in ("norm_g", "q_lat_g", "kv_lat_g", "glu_b", "dw_b", "conv_ln_g", "conv_ln_b", "b_pw"):
        gp[k] = s_parts[k]
    for k in ("q_norm_g", "k_norm_g"):
        t = s_parts[k]
        gp[k] = jnp.concatenate([t[..., :NOPE], t[..., LANE:LANE + ROPE]], axis=-1)
    dw_g = s_parts["dw_w"].reshape(N_DEV, nl, HALO, D_CONV)[:, :, :CONV_K]
    gp["dw_w"] = lax.dynamic_slice_in_dim(dw_g, me * LANE, LANE, axis=3)

    res["ada_w"] = _adamw(gp["ada_w"], w_loc["ada_w"], m_loc["ada_w"], v_loc["ada_w"], name="adamw_ada_w")
    small_names = [k for k in names if k not in _BIG and k != "ada_w"]
    res.update(zip(small_names, _adamw_small([(gp[k], w_loc[k], m_loc[k], v_loc[k]) for k in small_names],
                                             name="adamw_small")))
    arrived[0][0] = _exchange_wait(*flying["l0_w_in"][:4], res["ada_w"][1], "scatter", name="scatter_wait_l0_w_in")[0]
    w_in_res = _adamw([arrived[l][0] for l in range(nl)], w_loc["w_in"], m_loc["w_in"], v_loc["w_in"],
                      name="adamw_w_in")
    res["w_in"] = tuple(tr(a) for a in w_in_res)
    out = [loss, gx[None]]
    for idx in range(4):
        out += [res[k][idx] for k in names]
    return tuple(out)
```

```python
import functools
import math

import jax
import jax.numpy as jnp
from jax import lax
from jax.experimental import pallas as pl
from jax.experimental.pallas import tpu as pltpu

F32 = jnp.float32
MXU_DTYPE = jnp.bfloat16
WIRE_DTYPE = jnp.bfloat16

D_MODEL = 2048
N_LAYERS = 2
N_DEV = 8
N_HEADS = 8
NOPE = 128
ROPE = 64
V_DIM = 128
QK_DIM = NOPE + ROPE
Q_LORA = 512
KV_LORA = 256
D_MLA = N_HEADS * V_DIM
D_CONV = 1024
CONV_K = 31
ROPE_THETA = 10000.0
EPS = 1e-6
LANE = 128
HEAD_PAD = 2 * LANE
HALO = 32

SEG_CI = (0, 2 * D_CONV)
SEG_MG = (2 * D_CONV, D_MLA)
SEG_CG = (2 * D_CONV + D_MLA, D_CONV)
SEG_QL = (2 * D_CONV + D_MLA + D_CONV, Q_LORA)
SEG_KVL = (SEG_QL[0] + Q_LORA, KV_LORA)
SEG_KR = (SEG_KVL[0] + KV_LORA, LANE)
SEG_LAT = (SEG_QL[0], 1024)
IN_PAD = SEG_LAT[0] + SEG_LAT[1]
IN_TILE = IN_PAD // 4
assert SEG_KR[0] + LANE <= IN_PAD and SEG_LAT[0] % SEG_LAT[1] == 0
IN_COLS = Q_LORA + KV_LORA + ROPE + D_MLA + 2 * D_CONV + D_CONV

ADAM_LR = 0.001
ADAM_B1 = 0.9
ADAM_B2 = 0.999
ADAM_EPS = 1e-08
ADAM_WD = 0.01
ADAM_STEP = 10

VMEM_LIMIT = 56 * 1024 * 1024
ATT_T = 512
ROW_T = 256
CONV_T = 128
MESH_ID = pl.DeviceIdType.MESH


def _cp(sem=None):
    kw = dict(vmem_limit_bytes=VMEM_LIMIT)
    if sem is not None:
        kw["dimension_semantics"] = sem
    return pltpu.CompilerParams(**kw)


def _sds(shape, dtype):
    return jax.ShapeDtypeStruct(shape, dtype)


def _silu(x):
    return x * jax.nn.sigmoid(x)


def _dsilu(x):
    s = jax.nn.sigmoid(x)
    return s * (1.0 + x * (1.0 - s))


def _rowspec(t, width, col=0):
    return pl.BlockSpec((t, width), lambda i: (i, col))


def _vecspec(width):
    return pl.BlockSpec((1, width), lambda i: (0, 0))


def _colsum(v):
    return jnp.sum(v, axis=0, keepdims=True)


def _mm(a, b, *, name, ta=False, tb=False, out_dtype=F32, tm=512, tn=512, tk=None, n_outer=False, after=None,
        residual=None):
    if ta:
        kdim, m = a.shape
    else:
        m, kdim = a.shape
    if tb:
        n, k2 = b.shape
    else:
        k2, n = b.shape
    assert kdim == k2, (a.shape, b.shape)
    tm, tn = min(tm, m), min(tn, n)
    tk = kdim if tk is None else min(tk, kdim)
    assert m % tm == 0 and n % tn == 0 and kdim % tk == 0, (m, n, kdim, tm, tn, tk)
    nk = kdim // tk
    dims = (((0 if ta else 1,), (1 if tb else 0,)), ((), ()))

    n_extra = 0 if after is None else 1
    assert residual is None or nk == 1

    def body(a_ref, b_ref, *rest):
        if residual is not None:
            x_ref, gate_ref = rest[:2]
            rest = rest[2:]
        o_ref, scratch = rest[n_extra], rest[n_extra + 1:]
        prod = lax.dot_general(a_ref[...].astype(MXU_DTYPE), b_ref[...].astype(MXU_DTYPE), dims,
                               preferred_element_type=F32)
        if residual is not None:
            o_ref[...] = prod.astype(o_ref.dtype)
            scratch[0][...] = x_ref[...] + gate_ref[...] * prod
        elif nk == 1:
            o_ref[...] = prod.astype(o_ref.dtype)
        else:
            acc = scratch[0]
            k = pl.program_id(2)

            @pl.when(k == 0)
            def _():
                acc[...] = prod

            @pl.when(k > 0)
            def _():
                acc[...] += prod

            @pl.when(k == nk - 1)
            def _():
                o_ref[...] = acc[...].astype(o_ref.dtype)

    if n_outer:
        ij = lambda g0, g1: (g1, g0)
        grid = (n // tn, m // tm, nk)
    else:
        ij = lambda g0, g1: (g0, g1)
        grid = (m // tm, n // tn, nk)

    def a_map(g0, g1, k):
        i, _ = ij(g0, g1)
        return (k, i) if ta else (i, k)

    def b_map(g0, g1, k):
        _, j = ij(g0, g1)
        return (j, k) if tb else (k, j)

    def o_map(g0, g1, k):
        return ij(g0, g1)

    in_specs = [pl.BlockSpec((tk, tm) if ta else (tm, tk), a_map), pl.BlockSpec((tn, tk) if tb else (tk, tn), b_map)]
    operands = [a, b]
    out_specs, out_shape = pl.BlockSpec((tm, tn), o_map), _sds((m, n), out_dtype)
    if residual is not None:
        in_specs += [pl.BlockSpec((tm, tn), o_map), pl.BlockSpec((1, tn), lambda g0, g1, k: (0, ij(g0, g1)[1]))]
        operands += list(residual)
        out_specs, out_shape = [out_specs, pl.BlockSpec((tm, tn), o_map)], [out_shape, _sds((m, n), F32)]
    if after is not None:
        in_specs.append(_ANY)
        operands.append(after)
    return pl.pallas_call(
        body, name=name, grid=grid, in_specs=in_specs, out_specs=out_specs, out_shape=out_shape,
        scratch_shapes=[pltpu.VMEM((tm, tn), F32)] if nk > 1 else [],
        compiler_params=_cp(("parallel", "parallel", "arbitrary")),
    )(*operands)


def _prenorm(x, g, shift, sc1p, *, name):
    s, d = x.shape
    t = min(2 * ROW_T, s)

    def body(x_ref, g_ref, sh_ref, sc_ref, h_ref):
        xv = x_ref[...]
        r = lax.rsqrt(jnp.mean(xv * xv, axis=-1, keepdims=True) + EPS)
        h_ref[...] = ((xv * r) * g_ref[...] * sc_ref[...] + sh_ref[...]).astype(h_ref.dtype)

    return pl.pallas_call(
        body, name=name, grid=(s // t,),
        in_specs=[_rowspec(t, d), _vecspec(d), _vecspec(d), _vecspec(d)],
        out_specs=_rowspec(t, d), out_shape=_sds((s, d), MXU_DTYPE),
        compiler_params=_cp(("parallel",)),
    )(x, g, shift, sc1p)


def _rope_fwd(r, c_t, s1_t, s2_t):
    return r * c_t + pltpu.roll(r, LANE - ROPE // 2, 1) * s1_t + pltpu.roll(r, ROPE // 2, 1) * s2_t


def _rope_bwd(d, c_t, s1_t, s2_t):
    return d * c_t + pltpu.roll(d * s1_t, ROPE // 2, 1) + pltpu.roll(d * s2_t, LANE - ROPE // 2, 1)


def _lanesum(v):
    return jnp.sum(v, axis=-1, keepdims=True)


def _mla_pre(z, w_q_up, w_kv_up, g_ql, g_kvl, c_t, s1_t, s2_t, gqn, gqr, gkn, gkr, *, name):
    s = z.shape[0]
    t = min(2 * ROW_T, s)
    scale = 1.0 / math.sqrt(QK_DIM)
    wide = 2 * N_HEADS * LANE

    def body(ql_ref, kvl_ref, kr_ref, wq_ref, wkv_ref, gq_ref, gk_ref, c_ref, s1_ref, s2_ref,
             gqn_ref, gqr_ref, gkn_ref, gkr_ref, qn_ref, kn_ref, q_ref, kv_ref, qf_ref, kf_ref, vf_ref):
        for src, g_ref, dst, w_ref, up in ((ql_ref, gq_ref, qn_ref, wq_ref, q_ref),
                                           (kvl_ref, gk_ref, kn_ref, wkv_ref, kv_ref)):
            v = src[...]
            r = lax.rsqrt(jnp.mean(v * v, axis=-1, keepdims=True) + EPS)
            dst[...] = ((v * r) * g_ref[...]).astype(dst.dtype)
            up[...] = jnp.dot(dst[...], w_ref[...], preferred_element_type=F32)
        c_v, s1_v, s2_v = c_ref[...], s1_ref[...], s2_ref[...]
        kr = kr_ref[...]
        kr_ss = _lanesum(kr * kr)
        for h in range(N_HEADS):
            n = q_ref[:, h * LANE:(h + 1) * LANE]
            r = q_ref[:, N_HEADS * LANE + h * LANE:N_HEADS * LANE + (h + 1) * LANE]
            rs = lax.rsqrt((_lanesum(n * n) + _lanesum(r * r)) * (1.0 / QK_DIM) + EPS)
            qf_ref[h, :, 0:LANE] = (((n * rs) * gqn_ref[...]) * scale).astype(qf_ref.dtype)
            rr = _rope_fwd((r * rs) * gqr_ref[...], c_v, s1_v, s2_v)
            qf_ref[h, :, LANE:HEAD_PAD] = (rr * scale).astype(qf_ref.dtype)

            n = kv_ref[:, h * 2 * LANE:h * 2 * LANE + LANE]
            rs = lax.rsqrt((_lanesum(n * n) + kr_ss) * (1.0 / QK_DIM) + EPS)
            kf_ref[h, :, 0:LANE] = ((n * rs) * gkn_ref[...]).astype(kf_ref.dtype)
            kf_ref[h, :, LANE:HEAD_PAD] = _rope_fwd((kr * rs) * gkr_ref[...], c_v, s1_v, s2_v).astype(kf_ref.dtype)
            vf_ref[h, :, 0:V_DIM] = kv_ref[:, h * 2 * LANE + LANE:(h + 1) * 2 * LANE].astype(vf_ref.dtype)
            vf_ref[h, :, V_DIM:] = jnp.ones((t, V_DIM), vf_ref.dtype)

    hspec = lambda w: pl.BlockSpec((N_HEADS, t, w), lambda i: (0, i, 0))
    whole = lambda a: pl.BlockSpec(a.shape, lambda i: (0, 0))
    return pl.pallas_call(
        body, name=name, grid=(s // t,),
        in_specs=[_rowspec(t, Q_LORA, SEG_QL[0] // Q_LORA), _rowspec(t, KV_LORA, SEG_KVL[0] // KV_LORA),
                  _rowspec(t, LANE, SEG_KR[0] // LANE), whole(w_q_up), whole(w_kv_up),
                  _vecspec(Q_LORA), _vecspec(KV_LORA),
                  _rowspec(t, LANE), _rowspec(t, LANE), _rowspec(t, LANE),
                  _vecspec(LANE), _vecspec(LANE), _vecspec(LANE), _vecspec(LANE)],
        out_specs=[_rowspec(t, Q_LORA), _rowspec(t, KV_LORA), _rowspec(t, wide), _rowspec(t, wide),
                   hspec(HEAD_PAD), hspec(HEAD_PAD), hspec(2 * V_DIM)],
        out_shape=[_sds((s, Q_LORA), MXU_DTYPE), _sds((s, KV_LORA), MXU_DTYPE), _sds((s, wide), F32),
                   _sds((s, wide), F32), _sds((N_HEADS, s, HEAD_PAD), MXU_DTYPE),
                   _sds((N_HEADS, s, HEAD_PAD), MXU_DTYPE), _sds((N_HEADS, s, 2 * V_DIM), MXU_DTYPE)],
        compiler_params=_cp(("parallel",)),
    )(z, z, z, w_q_up, w_kv_up, g_ql, g_kvl, c_t, s1_t, s2_t, gqn, gqr, gkn, gkr)


def _causal_mask(t):
    row = lax.broadcasted_iota(jnp.int32, (t, t), 0)
    col = lax.broadcasted_iota(jnp.int32, (t, t), 1)
    return col <= row


NEG = -1e30


def _flash_fwd(qf, kf, va, *, name):
    nh, s, dk = qf.shape
    dv = va.shape[-1] // 2
    t = min(ATT_T, s)
    n = s // t
    assert dv == LANE and t % LANE == 0

    def body(q_ref, k_ref, v_ref, o_ref, lse_ref, m_s, acc_s, s_buf):
        i = pl.program_id(1)
        m_s[...] = jnp.full(m_s.shape, NEG, F32)
        acc_s[...] = jnp.zeros(acc_s.shape, F32)

        def rows_of(j):
            return pl.ds(pl.multiple_of(j * t, t), t)

        def scores(qi, j):
            return lax.dot_general(q_ref[0, rows_of(qi), :], k_ref[0, rows_of(j), :], (((1,), (1,)), ((), ())),
                                   preferred_element_type=F32)

        def consume(j, slot, masked):
            sc = s_buf[slot]
            if masked:
                sc = jnp.where(_causal_mask(t), sc, NEG)
            m_prev = m_s[...]
            m_new = jnp.maximum(m_prev, jnp.max(sc, axis=-1, keepdims=True))
            alpha = jnp.exp(m_prev - m_new)
            p = jnp.exp(sc - jnp.tile(m_new, (1, t // LANE)))
            acc_s[...] = jnp.tile(alpha, (1, 2)) * acc_s[...] + jnp.dot(
                p.astype(MXU_DTYPE), v_ref[0, rows_of(j), :], preferred_element_type=F32)
            m_s[...] = m_new

        nxt = jnp.minimum(i + 1, n - 1)

        @pl.when(i == 0)
        def _():
            s_buf[2] = scores(0, 0)
            consume(0, 2, True)
            s_buf[2] = scores(nxt, 0)

        @pl.when(i > 0)
        def _():
            s_buf[1] = scores(i, 1)
            consume(0, 2, False)

            def pair(a, carry):
                s_buf[0] = scores(i, 2 * a + 2)
                consume(2 * a + 1, 1, False)
                s_buf[1] = scores(i, 2 * a + 3)
                consume(2 * a + 2, 0, False)
                return carry

            lax.fori_loop(0, (i - 1) // 2, pair, 0)

            @pl.when(i % 2 == 1)
            def _():
                s_buf[2] = scores(nxt, 0)
                consume(i, 1, True)

            @pl.when(i % 2 == 0)
            def _():
                s_buf[0] = scores(i, i)
                consume(i - 1, 1, False)
                s_buf[2] = scores(nxt, 0)
                consume(i, 0, True)

        den = acc_s[:, dv:]
        o_ref[...] = acc_s[:, :dv] / den
        lse_ref[0] = m_s[...] + jnp.log(den)

    head = lambda h, i: (h, 0, 0)
    return pl.pallas_call(
        body, name=name, grid=(nh, n),
        in_specs=[pl.BlockSpec((1, s, dk), head), pl.BlockSpec((1, s, dk), head), pl.BlockSpec((1, s, 2 * dv), head)],
        out_specs=[pl.BlockSpec((t, dv), lambda h, i: (i, h)),
                   pl.BlockSpec((1, t, LANE), lambda h, i: (h, i, 0))],
        out_shape=[_sds((s, nh * dv), F32), _sds((nh, s, LANE), F32)],
        scratch_shapes=[pltpu.VMEM((t, LANE), F32), pltpu.VMEM((t, 2 * dv), F32), pltpu.VMEM((3, t, t), F32)],
        compiler_params=_cp(("arbitrary", "arbitrary")),
    )(qf, kf, va)


def _shifted_copies(ext_ref):
    rows = ext_ref.shape[1] - 8
    for s in range(1, 8):
        ext_ref[s, 0:rows, :] = ext_ref[0, s:s + rows, :]


def _windows(ext_ref, offsets, t_rows, lane0, lanes):
    for s in range(8):
        group = [o for o in offsets if o % 8 == s]
        if not group:
            continue
        lo, hi = min(group) - s, max(group) - s
        wide = ext_ref[s, pl.ds(lo, hi - lo + t_rows), lane0:lane0 + lanes]
        for o in group:
            yield o, wide[o - s - lo:o - s - lo + t_rows]


def _dw_taps(ext_ref, w_ref, row0, t_rows, lane0, lanes, first_off):
    acc = None
    for off, win in _windows(ext_ref, [row0 + first_off + k for k in range(CONV_K)], t_rows, lane0, lanes):
        k = off - row0 - first_off
        term = w_ref[k:k + 1, lane0:lane0 + lanes] * win
        acc = term if acc is None else acc + term
    return acc


CONV_RC = 32
CONV_LC = 256


def _conv_fwd(z, glu_b, dw_w, dw_b, ln_g, ln_b, w_pw, *, name):
    s = z.shape[0]
    t = min(CONV_T, s)
    c2 = 2 * D_CONV
    hb = t // HALO

    def body(zm_ref, zh_ref, gb_ref, w_ref, wb_ref, g_ref, b_ref, wpw_ref, u1_ref, u3_ref, u4_ref, ext):
        i = pl.program_id(0)

        def glu(zv):
            ci = zv + gb_ref[...]
            return ci[:, :D_CONV] * jax.nn.sigmoid(ci[:, D_CONV:])

        ext[0, HALO:, :] = glu(zm_ref[...])
        ext[0, 0:HALO, :] = jnp.where(i > 0, glu(zh_ref[...]), 0.0)
        _shifted_copies(ext)
        for rc in range(0, t, CONV_RC):
            for lc in range(0, D_CONV, CONV_LC):
                acc = _dw_taps(ext, w_ref, rc, CONV_RC, lc, CONV_LC, HALO - (CONV_K - 1))
                u1_ref[rc:rc + CONV_RC, lc:lc + CONV_LC] = acc + wb_ref[:, lc:lc + CONV_LC]
        u1 = u1_ref[...]
        mu = jnp.mean(u1, axis=-1, keepdims=True)
        cen = u1 - mu
        var = jnp.mean(cen * cen, axis=-1, keepdims=True)
        u2 = (cen * lax.rsqrt(var + EPS)) * g_ref[...] + b_ref[...]
        u3_ref[...] = _silu(u2).astype(u3_ref.dtype)
        u4_ref[...] = jnp.dot(u3_ref[...], wpw_ref[...], preferred_element_type=F32)

    return pl.pallas_call(
        body, name=name, grid=(s // t,),
        in_specs=[_rowspec(t, c2), pl.BlockSpec((HALO, c2), lambda i: (jnp.maximum(i * hb - 1, 0), 0)),
                  _vecspec(c2), pl.BlockSpec((HALO, D_CONV), lambda i: (0, 0)), _vecspec(D_CONV),
                  _vecspec(D_CONV), _vecspec(D_CONV), pl.BlockSpec((D_CONV, D_CONV), lambda i: (0, 0))],
        out_specs=[_rowspec(t, D_CONV), _rowspec(t, D_CONV), _rowspec(t, D_CONV)],
        out_shape=[_sds((s, D_CONV), F32), _sds((s, D_CONV), MXU_DTYPE), _sds((s, D_CONV), F32)],
        scratch_shapes=[pltpu.VMEM((8, t + HALO, D_CONV), F32)],
        compiler_params=_cp(("parallel",)),
    )(z, z, glu_b, dw_w, dw_b, ln_g, ln_b, w_pw)


def _gate_cat(o, z, u4m, b_pw, *, name):
    s = o.shape[0]
    t = min(2 * ROW_T, s)

    def body(o_ref, mg_ref, u4_ref, cg_ref, b_ref, cat_ref):
        cat_ref[:, :D_MLA] = (o_ref[...] * _silu(mg_ref[...])).astype(cat_ref.dtype)
        cat_ref[:, D_MLA:] = ((u4_ref[...] + b_ref[...]) * _silu(cg_ref[...])).astype(cat_ref.dtype)

    return pl.pallas_call(
        body, name=name, grid=(s // t,),
        in_specs=[_rowspec(t, D_MLA), _rowspec(t, D_MLA, SEG_MG[0] // D_MLA), _rowspec(t, D_CONV),
                  _rowspec(t, D_CONV, SEG_CG[0] // D_CONV), _vecspec(D_CONV)],
        out_specs=_rowspec(t, D_MLA + D_CONV), out_shape=_sds((s, D_MLA + D_CONV), MXU_DTYPE),
        compiler_params=_cp(("parallel",)),
    )(o, z, u4m, z, b_pw)


def _loss_head(xf, target, *, name):
    s, d = xf.shape
    t = min(2 * ROW_T, s)

    def body(x_ref, t_ref, gx_ref, loss_ref):
        @pl.when(pl.program_id(0) == 0)
        def _():
            loss_ref[...] = jnp.zeros(loss_ref.shape, F32)

        err = x_ref[...] - t_ref[...]
        gx_ref[...] = err * (1.0 / d)
        loss_ref[...] += 0.5 * jnp.sum(_lanesum(err * err) * (1.0 / d), axis=0, keepdims=True)

    return pl.pallas_call(
        body, name=name, grid=(s // t,),
        in_specs=[_rowspec(t, d), _rowspec(t, d)],
        out_specs=[_rowspec(t, d), pl.BlockSpec((1, 1), lambda i: (0, 0))],
        out_shape=[_sds((s, d), F32), _sds((1, 1), F32)],
        compiler_params=_cp(("arbitrary",)),
    )(xf, target)


def _acc_init(refs):
    @pl.when(pl.program_id(0) == 0)
    def _():
        for r in refs:
            r[...] = jnp.zeros(r.shape, r.dtype)


def _out_bwd(gxo, y, gate, *, name):
    s, d = gxo.shape
    t = min(2 * ROW_T, s)

    def body(g_ref, y_ref, gate_ref, dy_ref, dgate_ref):
        _acc_init([dgate_ref])
        gv = g_ref[...]
        dy_ref[...] = (gv * gate_ref[...]).astype(dy_ref.dtype)
        dgate_ref[...] += _colsum(gv * y_ref[...])

    return pl.pallas_call(
        body, name=name, grid=(s // t,),
        in_specs=[_rowspec(t, d), _rowspec(t, d), _vecspec(d)],
        out_specs=[_rowspec(t, d), _vecspec(d)],
        out_shape=[_sds((s, d), MXU_DTYPE), _sds((1, d), F32)],
        compiler_params=_cp(("arbitrary",)),
    )(gxo, y, gate)


def _gate_bwd(dy, w_out, o, z, u4m, b_pw, *, name):
    s, d = dy.shape
    t = min(2 * ROW_T, s)
    gates = D_MLA + D_CONV
    assert SEG_CG[0] == SEG_MG[0] + D_MLA and SEG_MG[0] % gates == 0

    def body(dy_ref, w_ref, o_ref, mg_ref, u4_ref, cg_ref, b_ref,
             do_ref, delta_ref, du4_ref, gb_ref, dz_ref):
        _acc_init([gb_ref])
        dcat = lax.dot_general(dy_ref[...], w_ref[...], (((1,), (1,)), ((), ())), preferred_element_type=F32)
        dm, ov, mg = dcat[:, :D_MLA], o_ref[...], mg_ref[...]
        do = dm * _silu(mg)
        do_ref[...] = do.astype(do_ref.dtype)
        dz_ref[:, :D_MLA] = (dm * ov * _dsilu(mg)).astype(dz_ref.dtype)
        prod = do * ov
        for h in range(N_HEADS):
            delta_ref[h] = _lanesum(prod[:, h * V_DIM:(h + 1) * V_DIM])
        dc, cg = dcat[:, D_MLA:], cg_ref[...]
        du4 = dc * _silu(cg)
        du4_ref[...] = du4.astype(du4_ref.dtype)
        dz_ref[:, D_MLA:] = (dc * (u4_ref[...] + b_ref[...]) * _dsilu(cg)).astype(dz_ref.dtype)
        gb_ref[...] += _colsum(du4)

    return pl.pallas_call(
        body, name=name, grid=(s // t,),
        in_specs=[_rowspec(t, d), pl.BlockSpec((gates, d), lambda i: (0, 0)), _rowspec(t, D_MLA),
                  _rowspec(t, D_MLA, SEG_MG[0] // D_MLA), _rowspec(t, D_CONV),
                  _rowspec(t, D_CONV, SEG_CG[0] // D_CONV), _vecspec(D_CONV)],
        out_specs=[_rowspec(t, D_MLA), pl.BlockSpec((N_HEADS, t, 1), lambda i: (0, i, 0)),
                   _rowspec(t, D_CONV), _vecspec(D_CONV), _rowspec(t, gates, SEG_MG[0] // gates)],
        out_shape=[_sds((s, D_MLA), MXU_DTYPE), _sds((N_HEADS, s, 1), F32),
                   _sds((s, D_CONV), MXU_DTYPE), _sds((1, D_CONV), F32), _sds((s, IN_PAD), MXU_DTYPE)],
        compiler_params=_cp(("arbitrary",)),
    )(dy, w_out, o, z, u4m, z, b_pw)


def _conv_bwd(du4, w_pw, u1, z, dz, glu_b, dw_w, ln_g, ln_b, *, name):
    s = z.shape[0]
    t = min(CONV_T, s)
    c2 = 2 * D_CONV
    hb = t // HALO
    n_blk = s // t
    last_halo = s // HALO - 1

    def body(d4m_ref, d4h_ref, wpw_ref, u1m_ref, u1h_ref, zm_ref, zh_ref, gb_ref, w_ref, g_ref, b_ref, dz_in_ref,
             dci_ref, gg_ref, gbn_ref, gwb_ref, ggb_ref, gw_ref, dext, uext, du0_s, gw_acc):
        i = pl.program_id(0)
        _acc_init([gg_ref, gbn_ref, gwb_ref, ggb_ref, gw_acc])

        def ln_bwd(d4, u1v):
            d3 = lax.dot_general(d4, wpw_ref[...], (((1,), (1,)), ((), ())), preferred_element_type=F32)
            mu = jnp.mean(u1v, axis=-1, keepdims=True)
            cen = u1v - mu
            rstd = lax.rsqrt(jnp.mean(cen * cen, axis=-1, keepdims=True) + EPS)
            uh = cen * rstd
            d2 = d3 * _dsilu(uh * g_ref[...] + b_ref[...])
            dh = d2 * g_ref[...]
            d1 = rstd * (dh - jnp.mean(dh, axis=-1, keepdims=True) - uh * jnp.mean(dh * uh, axis=-1, keepdims=True))
            return d1, d2, uh

        d1, d2, uh = ln_bwd(d4m_ref[...], u1m_ref[...])
        gg_ref[...] += _colsum(d2 * uh)
        gbn_ref[...] += _colsum(d2)
        gwb_ref[...] += _colsum(d1)
        dext[0, 0:t, :] = d1
        d1h, _, _ = ln_bwd(d4h_ref[...], u1h_ref[...])
        dext[0, t:, :] = jnp.where(i < n_blk - 1, d1h, 0.0)
        _shifted_copies(dext)

        def glu_parts(zv):
            ci = zv + gb_ref[...]
            return ci[:, :D_CONV], jax.nn.sigmoid(ci[:, D_CONV:])

        val, sg = glu_parts(zm_ref[...])
        uext[0, HALO:, :] = val * sg
        valh, sgh = glu_parts(zh_ref[...])
        uext[0, 0:HALO, :] = jnp.where(i > 0, valh * sgh, 0.0)
        _shifted_copies(uext)

        for rc in range(0, t, CONV_RC):
            for lc in range(0, D_CONV, CONV_LC):
                acc = None
                for off, win in _windows(dext, [rc + k for k in range(CONV_K)], CONV_RC, lc, CONV_LC):
                    k = (CONV_K - 1) - (off - rc)
                    term = w_ref[k:k + 1, lc:lc + CONV_LC] * win
                    acc = term if acc is None else acc + term
                du0_s[rc:rc + CONV_RC, lc:lc + CONV_LC] = acc
                dchunk = dext[0, rc:rc + CONV_RC, lc:lc + CONV_LC]
                first = rc + HALO - (CONV_K - 1)
                for off, win in _windows(uext, [first + k for k in range(CONV_K)], CONV_RC, lc, CONV_LC):
                    k = off - first
                    pr = dchunk * win
                    part = pr[0:8]
                    for r8 in range(8, CONV_RC, 8):
                        part = part + pr[r8:r8 + 8]
                    gw_acc[k, :, lc:lc + CONV_LC] += part

        du0 = du0_s[...]
        dval = du0 * sg
        dgt = du0 * val * sg * (1.0 - sg)
        dci_ref[:, :D_CONV] = dval.astype(dci_ref.dtype)
        dci_ref[:, D_CONV:] = dgt.astype(dci_ref.dtype)
        ggb_ref[:, :D_CONV] += _colsum(dval)
        ggb_ref[:, D_CONV:] += _colsum(dgt)

        @pl.when(i == n_blk - 1)
        def _():
            gw_ref[...] = jnp.sum(gw_acc[...], axis=1)

    halo_next = lambda w: pl.BlockSpec((HALO, w), lambda i: (jnp.minimum((i + 1) * hb, last_halo), 0))
    return pl.pallas_call(
        body, name=name, grid=(n_blk,),
        in_specs=[_rowspec(t, D_CONV), halo_next(D_CONV), pl.BlockSpec((D_CONV, D_CONV), lambda i: (0, 0)),
                  _rowspec(t, D_CONV), halo_next(D_CONV),
                  _rowspec(t, c2), pl.BlockSpec((HALO, c2), lambda i: (jnp.maximum(i * hb - 1, 0), 0)),
                  _vecspec(c2), pl.BlockSpec((HALO, D_CONV), lambda i: (0, 0)), _vecspec(D_CONV), _vecspec(D_CONV),
                  _ANY],
        out_specs=[_rowspec(t, c2, SEG_CI[0] // c2), _vecspec(D_CONV), _vecspec(D_CONV), _vecspec(D_CONV),
                   _vecspec(c2), pl.BlockSpec((HALO, D_CONV), lambda i: (0, 0))],
        out_shape=[_sds(dz.shape, dz.dtype), _sds((1, D_CONV), F32), _sds((1, D_CONV), F32), _sds((1, D_CONV), F32),
                   _sds((1, c2), F32), _sds((HALO, D_CONV), F32)],
        scratch_shapes=[pltpu.VMEM((8, t + HALO, D_CONV), F32), pltpu.VMEM((8, t + HALO, D_CONV), F32),
                        pltpu.VMEM((t, D_CONV), F32), pltpu.VMEM((HALO, 8, D_CONV), F32)],
        input_output_aliases={11: 0},
        compiler_params=_cp(("arbitrary",)),
    )(du4, du4, w_pw, u1, u1, z, z, glu_b, dw_w, ln_g, ln_b, dz)


def _flash_bwd(qf, kf, va, do, lse_t, delta_t, *, name):
    nh, s, dk = qf.shape
    dv = va.shape[-1] // 2
    t = min(ATT_T, s)
    n = s // t
    nt = (((1,), (1,)), ((), ()))
    tn = (((0,), (0,)), ((), ()))

    def body(q_ref, do_ref, lse_ref, dl_ref, k_ref, v_ref, dq_ref, dk_ref, dv_ref,
             dk_s, dv_s, st_buf, dpt_buf):
        n_un = pl.program_id(1)
        j = n - 1 - n_un
        nxt = jnp.maximum(j - 1, 0)

        @pl.when(n_un == 0)
        def _():
            dq_ref[...] = jnp.zeros(dq_ref.shape, F32)

        dk_s[...] = jnp.zeros(dk_s.shape, F32)
        dv_s[...] = jnp.zeros(dv_s.shape, F32)

        def rows_at(blk):
            return pl.ds(pl.multiple_of(blk * t, t), t)

        def rows_of(b):
            return rows_at(n - 1 - b)

        k = k_ref[0, rows_at(j), :]

        def produce(kj, b, slot):
            rows = rows_of(b)
            st_buf[slot] = lax.dot_general(k_ref[0, rows_at(kj), :], q_ref[0, rows, :], nt,
                                           preferred_element_type=F32)
            dpt_buf[slot] = lax.dot_general(v_ref[0, rows_at(kj), 0:dv], do_ref[rows, :], nt,
                                            preferred_element_type=F32)

        def consume(b, slot, masked):
            i = n - 1 - b
            rows = rows_of(b)
            q, dov = q_ref[0, rows, :], do_ref[rows, :]
            pt = jnp.exp(st_buf[slot] - lse_ref[0, i])
            if masked:
                key = lax.broadcasted_iota(jnp.int32, (t, t), 0)
                qry = lax.broadcasted_iota(jnp.int32, (t, t), 1)
                pt = jnp.where(key <= qry, pt, 0.0)
            dv_s[...] += jnp.dot(pt.astype(MXU_DTYPE), dov, preferred_element_type=F32)
            dst = (pt * (dpt_buf[slot] - dl_ref[0, i])).astype(MXU_DTYPE)
            dk_s[...] += jnp.dot(dst, q, preferred_element_type=F32)
            dq_ref[0, rows, :] += lax.dot_general(dst, k, tn, preferred_element_type=F32)

        @pl.when(n_un == 0)
        def _():
            produce(j, 0, 2)
            consume(0, 2, True)
            produce(nxt, 0, 2)

        @pl.when(n_un > 0)
        def _():
            produce(j, 1, 1)
            consume(0, 2, False)

            def pair(a, carry):
                produce(j, 2 * a + 2, 0)
                consume(2 * a + 1, 1, False)
                produce(j, 2 * a + 3, 1)
                consume(2 * a + 2, 0, False)
                return carry

            lax.fori_loop(0, (n_un - 1) // 2, pair, 0)

            @pl.when(n_un % 2 == 1)
            def _():
                produce(nxt, 0, 2)
                consume(n_un, 1, True)

            @pl.when(n_un % 2 == 0)
            def _():
                produce(j, n_un, 0)
                consume(n_un - 1, 1, False)
                produce(nxt, 0, 2)
                consume(n_un, 0, True)

        dk_ref[0] = dk_s[...]
        dv_ref[0] = dv_s[...]

    head = lambda h, j: (h, 0, 0)
    rowv = pl.BlockSpec((1, n, 1, t), lambda h, j: (h, 0, 0, 0))
    return pl.pallas_call(
        body, name=name, grid=(nh, n),
        in_specs=[pl.BlockSpec((1, s, dk), head),
                  pl.BlockSpec((s, dv), lambda h, j: (0, h)),
                  rowv, rowv,
                  pl.BlockSpec((1, s, dk), head),
                  pl.BlockSpec((1, s, 2 * dv), head)],
        out_specs=[pl.BlockSpec((1, s, dk), head),
                   pl.BlockSpec((1, t, dk), lambda h, g: (h, n - 1 - g, 0)),
                   pl.BlockSpec((1, t, dv), lambda h, g: (h, n - 1 - g, 0))],
        out_shape=[_sds((nh, s, dk), F32), _sds((nh, s, dk), F32), _sds((nh, s, dv), F32)],
        scratch_shapes=[pltpu.VMEM((t, dk), F32), pltpu.VMEM((t, dv), F32),
                        pltpu.VMEM((3, t, t), F32), pltpu.VMEM((3, t, t), F32)],
        compiler_params=_cp(("arbitrary", "arbitrary")),
    )(qf, do, lse_t, delta_t, kf, va)


def _mla_bwd(dqf, dkf, dvf, q_raw, kv, z, dz, qn, kn, w_q_up, w_kv_up, g_ql, g_kvl, c_t, s1_t, s2_t,
             gqn, gqr, gkn, gkr, *, name):
    s = q_raw.shape[0]
    t = min(ROW_T, s)
    scale = 1.0 / math.sqrt(QK_DIM)
    o_ql, o_kvl, o_kr = (seg[0] - SEG_LAT[0] for seg in (SEG_QL, SEG_KVL, SEG_KR))
    tn = (((0,), (0,)), ((), ()))
    nt = (((1,), (1,)), ((), ()))

    def body(dq_ref, dk_ref, dv_ref, q_ref, kv_ref, kr_ref, ql_ref, kvl_ref, qn_ref, kn_ref, wq_ref, wkv_ref,
             gq_ref, gk_ref, c_ref, s1_ref, s2_ref, gqn_ref, gqr_ref, gkn_ref, gkr_ref, dz_in_ref,
             dz_ref, gwq_ref, gwkv_ref, ggq_ref, ggk_ref, gql_ref, gkvl_ref, dqr_ref, dkv_ref):
        _acc_init([gwq_ref, gwkv_ref, ggq_ref, ggk_ref, gql_ref, gkvl_ref])
        c_v, s1_v, s2_v = c_ref[...], s1_ref[...], s2_ref[...]
        kr = kr_ref[...]
        kr_ss = _lanesum(kr * kr)
        dkr = jnp.zeros(kr.shape, F32)
        ggq_n = ggq_r = ggk_n = ggk_r = jnp.zeros((1, LANE), F32)

        def norm_bwd(n, r, rs, dyn, dyr, gn, gr):
            nh_, rh_ = n * rs, r * rs
            dnh, drh = dyn * gn, dyr * gr
            dot = (_lanesum(dnh * nh_) + _lanesum(drh * rh_)) * (1.0 / QK_DIM)
            return rs * (dnh - nh_ * dot), rs * (drh - rh_ * dot), _colsum(dyn * nh_), _colsum(dyr * rh_)

        for h in range(N_HEADS):
            n = q_ref[:, h * LANE:(h + 1) * LANE]
            r = q_ref[:, N_HEADS * LANE + h * LANE:N_HEADS * LANE + (h + 1) * LANE]
            rs = lax.rsqrt((_lanesum(n * n) + _lanesum(r * r)) * (1.0 / QK_DIM) + EPS)
            dyn = dq_ref[h, :, 0:LANE] * scale
            dyr = _rope_bwd(dq_ref[h, :, LANE:HEAD_PAD] * scale, c_v, s1_v, s2_v)
            dn, dr, g_n, g_r = norm_bwd(n, r, rs, dyn, dyr, gqn_ref[...], gqr_ref[...])
            dqr_ref[:, h * LANE:(h + 1) * LANE] = dn.astype(dqr_ref.dtype)
            dqr_ref[:, N_HEADS * LANE + h * LANE:N_HEADS * LANE + (h + 1) * LANE] = dr.astype(dqr_ref.dtype)
            ggq_n, ggq_r = ggq_n + g_n, ggq_r + g_r

            n = kv_ref[:, h * 2 * LANE:h * 2 * LANE + LANE]
            rs = lax.rsqrt((_lanesum(n * n) + kr_ss) * (1.0 / QK_DIM) + EPS)
            dyn = dk_ref[h, :, 0:LANE]
            dyr = _rope_bwd(dk_ref[h, :, LANE:HEAD_PAD], c_v, s1_v, s2_v)
            dn, dr, g_n, g_r = norm_bwd(n, kr, rs, dyn, dyr, gkn_ref[...], gkr_ref[...])
            dkv_ref[:, h * 2 * LANE:h * 2 * LANE + LANE] = dn.astype(dkv_ref.dtype)
            dkv_ref[:, h * 2 * LANE + LANE:(h + 1) * 2 * LANE] = dv_ref[h].astype(dkv_ref.dtype)
            dkr = dkr + dr
            ggk_n, ggk_r = ggk_n + g_n, ggk_r + g_r

        ggq_ref[:, 0:LANE] += ggq_n
        ggq_ref[:, LANE:] += ggq_r
        ggk_ref[:, 0:LANE] += ggk_n
        ggk_ref[:, LANE:] += ggk_r

        for d_ref, x_ref, w_ref, gw_ref, src, g_ref, off, gg_ref in (
                (dqr_ref, qn_ref, wq_ref, gwq_ref, ql_ref, gq_ref, o_ql, gql_ref),
                (dkv_ref, kn_ref, wkv_ref, gwkv_ref, kvl_ref, gk_ref, o_kvl, gkvl_ref)):
            dup = d_ref[...]
            gw_ref[...] += lax.dot_general(x_ref[...], dup, tn, preferred_element_type=F32)
            dy = lax.dot_general(dup, w_ref[...], nt, preferred_element_type=F32)
            v = src[...]
            r = lax.rsqrt(jnp.mean(v * v, axis=-1, keepdims=True) + EPS)
            vh = v * r
            dvh = dy * g_ref[...]
            dz_ref[:, off:off + v.shape[1]] = (
                r * (dvh - vh * jnp.mean(dvh * vh, axis=-1, keepdims=True))).astype(dz_ref.dtype)
            gg_ref[...] += _colsum(dy * vh)
        dz_ref[:, o_kr:o_kr + LANE] = dkr.astype(dz_ref.dtype)
        dz_ref[:, o_kr + LANE:] = jnp.zeros((t, SEG_LAT[1] - o_kr - LANE), dz_ref.dtype)

    hspec = lambda w: pl.BlockSpec((N_HEADS, t, w), lambda i: (0, i, 0))
    whole = lambda a: pl.BlockSpec(a.shape, lambda i: (0, 0))
    wide = 2 * N_HEADS * LANE
    return pl.pallas_call(
        body, name=name, grid=(s // t,),
        in_specs=[hspec(HEAD_PAD), hspec(HEAD_PAD), hspec(V_DIM), _rowspec(t, wide), _rowspec(t, wide),
                  _rowspec(t, LANE, SEG_KR[0] // LANE), _rowspec(t, Q_LORA, SEG_QL[0] // Q_LORA),
                  _rowspec(t, KV_LORA, SEG_KVL[0] // KV_LORA), _rowspec(t, Q_LORA), _rowspec(t, KV_LORA),
                  whole(w_q_up), whole(w_kv_up), _vecspec(Q_LORA), _vecspec(KV_LORA),
                  _rowspec(t, LANE), _rowspec(t, LANE), _rowspec(t, LANE),
                  _vecspec(LANE), _vecspec(LANE), _vecspec(LANE), _vecspec(LANE), _ANY],
        out_specs=[_rowspec(t, SEG_LAT[1], SEG_LAT[0] // SEG_LAT[1]), whole(w_q_up), whole(w_kv_up),
                   _vecspec(2 * LANE), _vecspec(2 * LANE), _vecspec(Q_LORA), _vecspec(KV_LORA)],
        out_shape=[_sds(dz.shape, dz.dtype), _sds(w_q_up.shape, F32), _sds(w_kv_up.shape, F32),
                   _sds((1, 2 * LANE), F32), _sds((1, 2 * LANE), F32), _sds((1, Q_LORA), F32),
                   _sds((1, KV_LORA), F32)],
        scratch_shapes=[pltpu.VMEM((t, wide), MXU_DTYPE), pltpu.VMEM((t, wide), MXU_DTYPE)],
        input_output_aliases={21: 0},
        compiler_params=_cp(("arbitrary",)),
    )(dqf, dkf, dvf, q_raw, kv, z, z, z, qn, kn, w_q_up, w_kv_up, g_ql, g_kvl, c_t, s1_t, s2_t,
      gqn, gqr, gkn, gkr, dz)


def _prenorm_bwd(dh, x, gxo, g, sc1p, *, name):
    s, d = x.shape
    t = min(2 * ROW_T, s)

    def body(dh_ref, x_ref, gx_ref, g_ref, sc_ref, dx_ref, dsh_ref, dsc_ref, gg_ref):
        _acc_init([dsh_ref, dsc_ref, gg_ref])
        xv, dhv = x_ref[...], dh_ref[...]
        r = lax.rsqrt(jnp.mean(xv * xv, axis=-1, keepdims=True) + EPS)
        xn = xv * r
        dsh_ref[...] += _colsum(dhv)
        dsc_ref[...] += _colsum(dhv * (xn * g_ref[...]))
        dm = dhv * sc_ref[...]
        gg_ref[...] += _colsum(dm * xn)
        dxn = dm * g_ref[...]
        dx_ref[...] = gx_ref[...] + r * (dxn - xn * jnp.mean(dxn * xn, axis=-1, keepdims=True))

    return pl.pallas_call(
        body, name=name, grid=(s // t,),
        in_specs=[_rowspec(t, d), _rowspec(t, d), _rowspec(t, d), _vecspec(d), _vecspec(d)],
        out_specs=[_rowspec(t, d), _vecspec(d), _vecspec(d), _vecspec(d)],
        out_shape=[_sds((s, d), F32), _sds((1, d), F32), _sds((1, d), F32), _sds((1, d), F32)],
        compiler_params=_cp(("arbitrary",)),
    )(dh, x, gxo, g, sc1p)


def _ada_fwd(c_all, ada_w, ada_b_cols, *, name):
    nl, d, cols = ada_w.shape

    def body(c_ref, w_ref, b_ref, o_ref):
        ca = _silu(c_ref[...]).astype(MXU_DTYPE)
        o_ref[0] = jnp.dot(ca, w_ref[0].astype(MXU_DTYPE), preferred_element_type=F32) + b_ref[0]

    return pl.pallas_call(
        body, name=name, grid=(nl,),
        in_specs=[pl.BlockSpec((N_DEV, d), lambda l: (0, 0)), pl.BlockSpec((1, d, cols), lambda l: (l, 0, 0)),
                  pl.BlockSpec((1, 1, cols), lambda l: (l, 0, 0))],
        out_specs=pl.BlockSpec((1, N_DEV, cols), lambda l: (l, 0, 0)),
        out_shape=_sds((nl, N_DEV, cols), F32),
        compiler_params=_cp(("parallel",)),
    )(c_all, ada_w, ada_b_cols)


def _ada_bwd(c_all_t, dmod_cols, *, name):
    nl, _, cols = dmod_cols.shape
    d = c_all_t.shape[0]

    def body(c_ref, dm_ref, o_ref):
        ca = _silu(c_ref[...]).astype(MXU_DTYPE)
        o_ref[0] = jnp.dot(ca, dm_ref[0].astype(MXU_DTYPE), preferred_element_type=F32)

    return pl.pallas_call(
        body, name=name, grid=(nl,),
        in_specs=[pl.BlockSpec((d, N_DEV), lambda l: (0, 0)), pl.BlockSpec((1, N_DEV, cols), lambda l: (l, 0, 0))],
        out_specs=pl.BlockSpec((1, d, cols), lambda l: (l, 0, 0)),
        out_shape=_sds((nl, d, cols), F32),
        compiler_params=_cp(("parallel",)),
    )(c_all_t, dmod_cols)


def _adamw_math(g, w, m, v):
    mn = ADAM_B1 * m + (1.0 - ADAM_B1) * g
    vn = ADAM_B2 * v + (1.0 - ADAM_B2) * (g * g)
    m_hat = mn / (1.0 - ADAM_B1 ** ADAM_STEP)
    v_hat = vn / (1.0 - ADAM_B2 ** ADAM_STEP)
    return -ADAM_LR * (m_hat / (jnp.sqrt(v_hat) + ADAM_EPS) + ADAM_WD * w), mn, vn


def _adamw_small(items, *, name):
    n = len(items)
    shapes = [it[1].shape for it in items]
    flat = lambda a, lead: a.reshape(lead + (-1, a.shape[-1]))
    operands = []
    for gp, w, m, v in items:
        operands += [flat(gp, (gp.shape[0],)), flat(w, ()), flat(m, ()), flat(v, ())]
    nparts = [it[0].shape[0] for it in items]

    def body(*refs):
        ins, outs = refs[:4 * n], refs[4 * n:]
        for i in range(n):
            g_ref, w_ref, m_ref, v_ref = ins[4 * i:4 * i + 4]
            g = g_ref[0].astype(F32)
            for p in range(1, nparts[i]):
                g = g + g_ref[p].astype(F32)
            outs[4 * i][...] = g
            outs[4 * i + 1][...], outs[4 * i + 2][...], outs[4 * i + 3][...] = _adamw_math(
                g, w_ref[...], m_ref[...], v_ref[...])

    out_shape = []
    for it in items:
        out_shape += [_sds(flat(it[1], ()).shape, F32)] * 4
    outs = pl.pallas_call(body, name=name, out_shape=out_shape, compiler_params=_cp())(*operands)
    return [tuple(o.reshape(shp) for o in outs[4 * i:4 * i + 4]) for i, shp in enumerate(shapes)]


def _adamw(gparts, w, m, v, *, name):
    shape = w.shape
    cols = shape[-1]
    per_layer = isinstance(gparts, (list, tuple))
    nl = shape[0] if per_layer else 1
    rows = w.size // cols // nl
    glist = list(gparts) if per_layer else [gparts]
    npart = glist[0].shape[0]
    glist = [g.reshape(npart, rows, cols) for g in glist]
    w3, m3, v3 = (a.reshape(nl, rows, cols) for a in (w, m, v))
    budget = 2 * 1024 * 1024
    fits = [t for t in range(min(rows, 256) // 8 * 8, 7, -8)
            if rows % t == 0 and npart * t * cols * glist[0].dtype.itemsize <= budget]
    t = fits[0] if fits else rows
    nb = rows // t

    def body(*refs):
        g_refs = refs[:nl]
        w_ref, m_ref, v_ref, go_ref, d_ref, mo_ref, vo_ref, g_s = refs[nl:]
        layer = pl.program_id(0)
        for l in range(nl):
            @pl.when(layer == l)
            def _(l=l):
                g = g_refs[l][0].astype(F32)
                for p in range(1, npart):
                    g = g + g_refs[l][p].astype(F32)
                g_s[...] = g

        g = g_s[...]
        go_ref[0] = g
        d_ref[0], mo_ref[0], vo_ref[0] = _adamw_math(g, w_ref[0], m_ref[0], v_ref[0])

    def g_map(l):
        return lambda layer, i: (0, jnp.where(layer == l, i, jnp.where(layer < l, 0, nb - 1)), 0)

    spec = pl.BlockSpec((1, t, cols), lambda layer, i: (layer, i, 0))
    outs = pl.pallas_call(
        body, name=name, grid=(nl, nb),
        in_specs=[pl.BlockSpec((npart, t, cols), g_map(l)) for l in range(nl)] + [spec, spec, spec],
        out_specs=[spec] * 4, out_shape=[_sds((nl, rows, cols), F32)] * 4,
        scratch_shapes=[pltpu.VMEM((t, cols), F32)],
        compiler_params=_cp(("arbitrary", "arbitrary")),
    )(*glist, w3, m3, v3)
    return tuple(o.reshape(shape) for o in outs)


_ANY = pl.BlockSpec(memory_space=pl.ANY)


def _all_gather(blocks, *, name):
    na = len(blocks)

    def body(*refs):
        x_refs, out_refs = refs[:na], refs[na:2 * na]
        send_sems, recv_sems, local_sems = refs[2 * na:]
        x, y, c = lax.axis_index("x"), lax.axis_index("y"), lax.axis_index("c")
        me, sibling = (x, y, c), (x, y, 1 - c)
        chips = [(1 - x, y), (x, 1 - y), (1 - x, 1 - y)]

        def slot(a, px, py, pc):
            return out_refs[a].at[4 * px + 2 * py + pc]

        def copy(a, k, blk, to, src=None):
            return pltpu.make_async_remote_copy(
                src_ref=slot(a, *blk) if src is None else src, dst_ref=slot(a, *blk),
                send_sem=send_sems.at[7 * a + k], recv_sem=recv_sems.at[7 * a + k],
                device_id=to, device_id_type=MESH_ID)

        mine = [pltpu.make_async_copy(x_refs[a], slot(a, *me), local_sems.at[a]) for a in range(na)]
        for cp in mine:
            cp.start()
        first = []
        for a in range(na):
            first.append(copy(a, 0, me, sibling, src=x_refs[a]))
            first += [copy(a, 1 + j, me, (*chip, c), src=x_refs[a]) for j, chip in enumerate(chips)]
        for cp in first:
            cp.start()
        passed = []
        for a in range(na):
            for j, chip in enumerate(chips):
                copy(a, 1 + j, (*chip, c), me).wait_recv()
                fwd = copy(a, 4 + j, (*chip, c), sibling)
                fwd.start()
                passed.append(fwd)
        for a in range(na):
            copy(a, 0, sibling, me).wait_recv()
            for j, chip in enumerate(chips):
                copy(a, 4 + j, (*chip, 1 - c), me).wait_recv()
        for cp in first + passed:
            cp.wait_send()
        for cp in mine:
            cp.wait()

    outs = pl.pallas_call(
        body, name=name, in_specs=[_ANY] * na, out_specs=[_ANY] * na,
        out_shape=[_sds((N_DEV,) + b.shape, b.dtype) for b in blocks],
        scratch_shapes=[pltpu.SemaphoreType.DMA((7 * na,)), pltpu.SemaphoreType.DMA((7 * na,)),
                        pltpu.SemaphoreType.DMA((na,))],
    )(*blocks)
    return list(outs)


_HBM = pl.BlockSpec(memory_space=pltpu.HBM)
_SEM = pl.BlockSpec(memory_space=pltpu.SEMAPHORE)
_EFFECT = pltpu.SideEffectType.DATAFLOW_SIDE_EFFECTING


def _peers(x, y, c):
    out = []
    for k in range(1, N_DEV):
        out.append((1 - x if k & 4 else x, 1 - y if k & 2 else y, 1 - c if k & 1 else c))
    return out


def _own_slots(srcs, scatter, *, name, after=None):
    na = len(srcs)
    n_extra = 0 if after is None else 1
    me = (4 * lax.axis_index("x") + 2 * lax.axis_index("y") + lax.axis_index("c")).astype(jnp.int32).reshape(1)

    def body(me_ref, *refs):
        in_refs, out_refs = refs[:na], refs[na + n_extra:]
        for a in range(na):
            out_refs[a][0] = in_refs[a][0] if scatter else in_refs[a][...]

    def slot_spec(shard):
        zeros = (0,) * len(shard)
        return pl.BlockSpec((1,) + tuple(shard), lambda i, me_ref: (me_ref[0],) + zeros)

    def whole_spec(shape):
        zeros = (0,) * len(shape)
        return pl.BlockSpec(tuple(shape), lambda i, me_ref: zeros)

    shards = [s.shape[1:] if scatter else s.shape for s in srcs]
    in_specs = [slot_spec(sh) if scatter else whole_spec(sh) for sh in shards] + [_ANY] * n_extra
    outs = pl.pallas_call(
        body, name=name,
        grid_spec=pltpu.PrefetchScalarGridSpec(
            num_scalar_prefetch=1, grid=(1,), in_specs=in_specs, out_specs=[slot_spec(sh) for sh in shards]),
        out_shape=[_sds((N_DEV,) + tuple(sh), s.dtype) for sh, s in zip(shards, srcs)],
        compiler_params=_cp(("arbitrary",)),
    )(me, *srcs, *([] if after is None else [after]))
    return list(outs)


_N_COPIES = dict(scatter=7, gather=7, chips=4, forward=3)


def _exchange_copies(src_refs, land_refs, send_sems, recv_sems, mode):
    x, y, c = lax.axis_index("x"), lax.axis_index("y"), lax.axis_index("c")
    me = 4 * x + 2 * y + c
    nc = _N_COPIES[mode]
    chips = [(1 - x, y), (x, 1 - y), (1 - x, 1 - y)]
    cps = []
    for a in range(len(land_refs)):
        if mode in ("scatter", "gather"):
            plan = [((src_refs[a].at[4 * px + 2 * py + pc] if mode == "scatter" else src_refs[a]),
                     land_refs[a].at[me], (px, py, pc)) for px, py, pc in _peers(x, y, c)]
        elif mode == "chips":
            plan = [(src_refs[a], land_refs[a].at[me], to) for to in [(x, y, 1 - c)] + [(*ch, c) for ch in chips]]
        else:
            plan = [(land_refs[a].at[4 * px + 2 * py + c], land_refs[a].at[4 * px + 2 * py + c], (x, y, 1 - c))
                    for px, py in chips]
        for k, (src, dst, to) in enumerate(plan):
            cps.append(pltpu.make_async_remote_copy(
                src_ref=src, dst_ref=dst, send_sem=send_sems.at[nc * a + k], recv_sem=recv_sems.at[nc * a + k],
                device_id=to, device_id_type=MESH_ID))
    return cps


def _exchange_start(srcs, lands, mode, *, name):
    ns, nz = len(srcs), len(lands)
    nsem = _N_COPIES[mode] * nz

    def body(*refs):
        src_refs, land_refs = refs[:ns], refs[ns:ns + nz]
        send_sems, recv_sems = refs[ns + nz], refs[ns + nz + 1]
        token = refs[-1]
        for cp in _exchange_copies(src_refs, land_refs, send_sems, recv_sems, mode):
            cp.start()
        token[...] = jnp.zeros(token.shape, token.dtype)

    hbm = lambda a: pltpu.HBM(a.shape, a.dtype)
    outs = pl.pallas_call(
        body, name=name,
        out_shape=(pltpu.SemaphoreType.DMA((nsem,)), pltpu.SemaphoreType.DMA((nsem,)),
                   *[hbm(a) for a in srcs], *[hbm(a) for a in lands], _sds((8, LANE), F32)),
        in_specs=[_HBM] * (ns + nz),
        out_specs=(_SEM, _SEM, *[_HBM] * (ns + nz), pl.BlockSpec(memory_space=pltpu.VMEM)),
        input_output_aliases={i: 2 + i for i in range(ns + nz)},
        compiler_params=pltpu.CompilerParams(has_side_effects=_EFFECT),
    )(*[pltpu.with_memory_space_constraint(a, pltpu.HBM) for a in list(srcs) + list(lands)])
    return outs[0], outs[1], list(outs[2:2 + ns]), list(outs[2 + ns:2 + ns + nz]), outs[-1]


def _exchange_wait(send_sems, recv_sems, srcs, lands, after, mode, *, name):
    ns, nz = len(srcs), len(lands)

    def body(*refs):
        src_refs, land_refs = refs[:ns], refs[ns:ns + nz]
        s_sems, r_sems = refs[ns + nz], refs[ns + nz + 1]
        for cp in _exchange_copies(src_refs, land_refs, s_sems, r_sems, mode):
            cp.wait_send()
            cp.wait_recv()

    hbm = lambda a: pltpu.HBM(a.shape, a.dtype)
    outs = pl.pallas_call(
        body, name=name,
        out_shape=(*[hbm(a) for a in srcs], *[hbm(a) for a in lands]),
        in_specs=[_HBM] * (ns + nz) + [_SEM, _SEM, _ANY],
        out_specs=tuple([_HBM] * (ns + nz)),
        input_output_aliases={i: i for i in range(ns + nz)},
        compiler_params=pltpu.CompilerParams(has_side_effects=_EFFECT),
    )(*srcs, *lands, send_sems, recv_sems, after)
    return list(outs[ns:])


_WIN_SEGS = (("ql", 0, Q_LORA, SEG_QL[0]), ("kvl", Q_LORA, KV_LORA, SEG_KVL[0]),
             ("kr", Q_LORA + KV_LORA, ROPE, SEG_KR[0]), ("mg", Q_LORA + KV_LORA + ROPE, D_MLA, SEG_MG[0]),
             ("ci", Q_LORA + KV_LORA + ROPE + D_MLA, 2 * D_CONV, SEG_CI[0]),
             ("cg", Q_LORA + KV_LORA + ROPE + D_MLA + 2 * D_CONV, D_CONV, SEG_CG[0]))
_WIN_SHARD = IN_COLS // N_DEV


def _win_pieces():
    out = []
    for _, o, n, new in _WIN_SEGS:
        for j in range(N_DEV):
            lo, hi = max(o, j * _WIN_SHARD), min(o + n, (j + 1) * _WIN_SHARD)
            if lo < hi:
                out.append((j, lo - j * _WIN_SHARD, new + lo - o, hi - lo))
    return out


WIN_T = 512


def _win_assemble(w_all, *, name):
    d = w_all.shape[2]
    t = min(WIN_T, d)
    pieces = sorted(_win_pieces(), key=lambda p: p[2])
    assert all(lo % 8 == 0 and n % 8 == 0 for _, lo, _, n in pieces)

    def body(w_ref, o_ref):
        rows = [w_ref[j].astype(F32)[lo:lo + n, :] for j, lo, _, n in pieces]
        rows.append(jnp.zeros((IN_PAD - (SEG_KR[0] + ROPE), t), F32))
        o_ref[...] = jnp.concatenate(rows, axis=0).astype(o_ref.dtype)

    return pl.pallas_call(
        body, name=name, grid=(d // t,),
        in_specs=[pl.BlockSpec((N_DEV, _WIN_SHARD, t), lambda i: (0, 0, i))],
        out_specs=pl.BlockSpec((IN_PAD, t), lambda i: (0, i)), out_shape=_sds((IN_PAD, d), w_all.dtype),
        compiler_params=_cp(("parallel",)),
    )(w_all)


def _win_split(grad, *, name):
    d = grad.shape[1]
    t = min(WIN_T, d)
    by_shard = [sorted([p for p in _win_pieces() if p[0] == j], key=lambda p: p[1]) for j in range(N_DEV)]

    def body(g_ref, o_ref):
        for j in range(N_DEV):
            rows = [g_ref[new:new + n, :] for _, _, new, n in by_shard[j]]
            o_ref[j] = jnp.concatenate(rows, axis=0).astype(o_ref.dtype)

    return pl.pallas_call(
        body, name=name, grid=(d // t,),
        in_specs=[pl.BlockSpec((IN_PAD, t), lambda i: (0, i))],
        out_specs=pl.BlockSpec((N_DEV, _WIN_SHARD, t), lambda i: (0, 0, i)),
        out_shape=_sds((N_DEV, _WIN_SHARD, d), WIRE_DTYPE),
        compiler_params=_cp(("parallel",)),
    )(grad)


def _cols_to_shards(a):
    r, n = a.shape
    return a.reshape(r, N_DEV, n // N_DEV).transpose(1, 0, 2)


def _shards_to_cols(a):
    nd, r, w = a.shape
    return a.transpose(1, 0, 2).reshape(r, nd * w)


def _win_permute(w_in):
    o_ql, o_kvl, o_kr, o_mg = 0, Q_LORA, Q_LORA + KV_LORA, Q_LORA + KV_LORA + ROPE
    o_ci = o_mg + D_MLA
    o_cg = o_ci + 2 * D_CONV
    seg = lambda o, n: w_in[:, o:o + n]
    pad = jnp.zeros((w_in.shape[0], IN_PAD - (SEG_KR[0] + ROPE)), w_in.dtype)
    return jnp.concatenate([seg(o_ci, 2 * D_CONV), seg(o_mg, D_MLA), seg(o_cg, D_CONV), seg(o_ql, Q_LORA),
                            seg(o_kvl, KV_LORA), seg(o_kr, ROPE), pad], axis=1)


def _win_unpermute(g):
    seg = lambda s, n=None: g[:, s[0]:s[0] + (s[1] if n is None else n)]
    return jnp.concatenate([seg(SEG_QL), seg(SEG_KVL), seg(SEG_KR, ROPE), seg(SEG_MG), seg(SEG_CI), seg(SEG_CG)], axis=1)


def _qup_permute(w):
    w3 = w.reshape(w.shape[0], N_HEADS, QK_DIM)
    nope = w3[:, :, :NOPE].reshape(w.shape[0], N_HEADS * NOPE)
    rope = jnp.pad(w3[:, :, NOPE:], ((0, 0), (0, 0), (0, LANE - ROPE))).reshape(w.shape[0], N_HEADS * LANE)
    return jnp.concatenate([nope, rope], axis=1)


def _qup_unpermute(g):
    r = g.shape[0]
    nope = g[:, :N_HEADS * NOPE].reshape(r, N_HEADS, NOPE)
    rope = g[:, N_HEADS * NOPE:].reshape(r, N_HEADS, LANE)[:, :, :ROPE]
    return jnp.concatenate([nope, rope], axis=2).reshape(r, N_HEADS * QK_DIM)


def _norm_tiles(g):
    return g[:NOPE].reshape(1, LANE), jnp.pad(g[NOPE:], (0, LANE - ROPE)).reshape(1, LANE)


def _norm_untile(gt):
    return jnp.concatenate([gt[0, :NOPE], gt[0, LANE:LANE + ROPE]])


def _rope_tiles(positions):
    inv_freq = 1.0 / (ROPE_THETA ** (jnp.arange(0, ROPE, 2, dtype=F32) / ROPE))
    ang = positions.astype(F32)[:, None] * inv_freq
    cos, sin = jnp.cos(ang), jnp.sin(ang)
    zq = jnp.zeros_like(cos)
    c_t = jnp.concatenate([cos, cos, zq, zq], axis=1)
    s1_t = jnp.concatenate([-sin, zq, zq, zq], axis=1)
    s2_t = jnp.concatenate([zq, sin, zq, zq], axis=1)
    return c_t, s1_t, s2_t


_BIG = ("w_in", "w_q_up", "w_kv_up", "w_pw", "w_out")
_COL_SHARDED = ("w_in", "w_q_up", "w_kv_up")


def _pack_rows(arrs):
    return jnp.concatenate([a.reshape(-1, LANE) for a in arrs], axis=0)


def _unpack_rows(buf, shapes):
    out, r0 = [], 0
    lead = buf.shape[:-2]
    for shp in shapes:
        n = math.prod(shp) // LANE
        out.append(buf[..., r0:r0 + n, :].reshape(lead + tuple(shp)))
        r0 += n
    return out


_SMALL = (("dmod", 3 * D_MODEL), ("norm_g", D_MODEL), ("q_lat_g", Q_LORA), ("kv_lat_g", KV_LORA),
          ("q_norm_g", 2 * LANE), ("k_norm_g", 2 * LANE), ("glu_b", 2 * D_CONV), ("dw_w", HALO * D_CONV),
          ("dw_b", D_CONV), ("conv_ln_g", D_CONV), ("conv_ln_b", D_CONV), ("b_pw", D_CONV))


def _layer_fwd(x, p, rope, l, early=None, late=None):
    n = lambda s: f"{s}_l{l}"
    c_t, s1_t, s2_t = rope
    h = _prenorm(x, p["norm_g"], p["shift"], p["sc1p"], name=n("prenorm"))
    if early is not None:
        p = {**p, **early(h)}
    z = _mm(h, p["w_in"], tb=True, name=n("in_proj"), tn=IN_TILE, n_outer=True,
            after=p.get("in_proj_after"))
    if late is not None:
        p = {**p, **late(z)}
    qn, kn, q_raw, kv, qf, kf, vf = _mla_pre(z, p["w_q_up"], p["w_kv_up"], p["q_lat_g"], p["kv_lat_g"],
                                             c_t, s1_t, s2_t, *p["qk_tiles"], name=n("mla_pre"))
    o, lse = _flash_fwd(qf, kf, vf, name=n("flash_fwd"))
    u1, u3, u4m = _conv_fwd(z, p["glu_b"], p["dw_w"], p["dw_b"], p["conv_ln_g"], p["conv_ln_b"], p["w_pw"],
                            name=n("conv_fwd"))
    cat = _gate_cat(o, z, u4m, p["b_pw"], name=n("gate_cat"))
    y, x_next = _mm(cat, p["w_out"], name=n("out_proj"), tn=1024, residual=(x, p["gate"]))
    saved = dict(x=x, h=h, z=z, qn=qn, kn=kn, q_raw=q_raw, kv=kv, qf=qf, kf=kf, vf=vf, o=o, lse=lse,
                 u1=u1, u3=u3, u4m=u4m, cat=cat, y=y)
    return x_next, saved, p


def _layer_bwd(gxo, p, sv, rope, l, hook_rest=None, hook_w_in=None):
    n = lambda s: f"{s}_l{l}"
    c_t, s1_t, s2_t = rope
    z = sv["z"]
    dy, dgate = _out_bwd(gxo, sv["y"], p["gate"], name=n("out_bwd"))
    g_w_out = _mm(sv["cat"], dy, ta=True, name=n("g_w_out"), tm=1024, tn=1024)
    do, delta, du4, g_b_pw, dz = _gate_bwd(dy, p["w_out"], sv["o"], z, sv["u4m"], p["b_pw"], name=n("gate_bwd"))
    g_w_pw = _mm(sv["u3"], du4, ta=True, name=n("g_w_pw"), tm=1024, tn=1024, tk=512)
    dz, g_ln_g, g_ln_b, g_dw_b, g_glu_b, g_dw_w = _conv_bwd(
        du4, p["w_pw"], sv["u1"], z, dz, p["glu_b"], p["dw_w"], p["conv_ln_g"], p["conv_ln_b"], name=n("conv_bwd"))
    t_att = min(ATT_T, z.shape[0])
    to_lanes = lambda a: a.reshape(N_HEADS, z.shape[0] // t_att, 1, t_att)
    dqf, dkf, dvf = _flash_bwd(sv["qf"], sv["kf"], sv["vf"], do,
                               to_lanes(sv["lse"][:, :, 0]), to_lanes(delta), name=n("flash_bwd"))
    dz, g_w_q_up, g_w_kv_up, g_qn, g_kn, g_ql, g_kvl = _mla_bwd(
        dqf, dkf, dvf, sv["q_raw"], sv["kv"], z, dz, sv["qn"], sv["kn"], p["w_q_up"], p["w_kv_up"],
        p["q_lat_g"], p["kv_lat_g"], c_t, s1_t, s2_t, *p["qk_tiles"], name=n("mla_bwd"))
    big = dict(w_q_up=g_w_q_up, w_kv_up=g_w_kv_up, w_pw=g_w_pw, w_out=g_w_out)
    after = None if hook_rest is None else hook_rest(big)
    g_w_in = _mm(dz, sv["h"], ta=True, name=n("g_w_in"), tm=512, tn=1024, after=after)
    big["w_in"] = g_w_in
    after = None if hook_w_in is None else hook_w_in(g_w_in)
    dh = _mm(dz, p["w_in"], name=n("d_h"), tn=1024, after=after)
    dx, dshift, dscale, g_norm = _prenorm_bwd(dh, sv["x"], gxo, p["norm_g"], p["sc1p"], name=n("prenorm_bwd"))
    small = dict(dmod=jnp.concatenate([dshift, dscale, dgate], axis=1), norm_g=g_norm, q_lat_g=g_ql, kv_lat_g=g_kvl,
                 q_norm_g=g_qn, k_norm_g=g_kn, glu_b=g_glu_b, dw_w=g_dw_w, dw_b=g_dw_b,
                 conv_ln_g=g_ln_g, conv_ln_b=g_ln_b, b_pw=g_b_pw)
    return dx, big, small


def _layer_params(l, full, mod_l, small):
    d = D_MODEL
    row = lambda a: a.reshape(1, -1)
    shift, scale, gate = mod_l[:, :d], mod_l[:, d:2 * d], mod_l[:, 2 * d:]
    dw_w = jnp.pad(full["dw_w"][l], ((0, HALO - CONV_K), (0, 0)))
    return dict(
        shift=shift, sc1p=1.0 + scale, gate=gate, norm_g=row(small["norm_g"][l]),
        **{k: full[k][l] for k in _BIG if k in full}, dw_w=dw_w,
        q_lat_g=row(small["q_lat_g"][l]), kv_lat_g=row(small["kv_lat_g"][l]),
        qk_tiles=_norm_tiles(small["q_norm_g"][l]) + _norm_tiles(small["k_norm_g"][l]),
        glu_b=row(small["glu_b"][l]), dw_b=row(small["dw_b"][l]), conv_ln_g=row(small["conv_ln_g"][l]),
        conv_ln_b=row(small["conv_ln_b"][l]), b_pw=row(small["b_pw"][l]))


def kernel(x, c, positions, ada_w, ada_b, norm_g, w_in, q_lat_g, w_q_up, kv_lat_g, w_kv_up, q_norm_g, k_norm_g, glu_b, dw_w, dw_b, conv_ln_g, conv_ln_b, w_pw, b_pw, w_out, loss_target, m_ada_w, m_ada_b, m_norm_g, m_w_in, m_q_lat_g, m_w_q_up, m_kv_lat_g, m_w_kv_up, m_q_norm_g, m_k_norm_g, m_glu_b, m_dw_w, m_dw_b, m_conv_ln_g, m_conv_ln_b, m_w_pw, m_b_pw, m_w_out, v_ada_w, v_ada_b, v_norm_g, v_w_in, v_q_lat_g, v_w_q_up, v_kv_lat_g, v_w_kv_up, v_q_norm_g, v_k_norm_g, v_glu_b, v_dw_w, v_dw_b, v_conv_ln_g, v_conv_ln_b, v_w_pw, v_b_pw, v_w_out):
    names = ("ada_w", "ada_b", "norm_g", "w_in", "q_lat_g", "w_q_up", "kv_lat_g", "w_kv_up", "q_norm_g",
             "k_norm_g", "glu_b", "dw_w", "dw_b", "conv_ln_g", "conv_ln_b", "w_pw", "b_pw", "w_out")
    w_loc = dict(zip(names, (ada_w, ada_b, norm_g, w_in, q_lat_g, w_q_up, kv_lat_g, w_kv_up, q_norm_g, k_norm_g,
                             glu_b, dw_w, dw_b, conv_ln_g, conv_ln_b, w_pw, b_pw, w_out)))
    m_loc = dict(zip(names, (m_ada_w, m_ada_b, m_norm_g, m_w_in, m_q_lat_g, m_w_q_up, m_kv_lat_g, m_w_kv_up,
                             m_q_norm_g, m_k_norm_g, m_glu_b, m_dw_w, m_dw_b, m_conv_ln_g, m_conv_ln_b, m_w_pw,
                             m_b_pw, m_w_out)))
    v_loc = dict(zip(names, (v_ada_w, v_ada_b, v_norm_g, v_w_in, v_q_lat_g, v_w_q_up, v_kv_lat_g, v_w_kv_up,
                             v_q_norm_g, v_k_norm_g, v_glu_b, v_dw_w, v_dw_b, v_conv_ln_g, v_conv_ln_b, v_w_pw,
                             v_b_pw, v_w_out)))
    nl, d = N_LAYERS, D_MODEL
    me = 4 * lax.axis_index("x") + 2 * lax.axis_index("y") + lax.axis_index("c")
    x2, tgt = x[0], loss_target[0]
    ada_cols = ada_w.shape[-1]

    tr = lambda a: jnp.swapaxes(a, 1, 2)
    w_loc, m_loc, v_loc = ({**dd, "w_in": tr(dd["w_in"])} for dd in (w_loc, m_loc, v_loc))
    w_in0 = [w_loc["w_in"][0].astype(WIRE_DTYPE)]
    fly_c = _exchange_start(w_in0, _own_slots(w_in0, False, name="own_w_in_l0"), "chips", name="gather_start_w_in_l0")
    held = dict(c=c, positions=positions, ada_b=ada_b, norm_g=norm_g, q_lat_g=q_lat_g, kv_lat_g=kv_lat_g,
                q_norm_g=q_norm_g, k_norm_g=k_norm_g, glu_b=glu_b, dw_w=dw_w, dw_b=dw_b, conv_ln_g=conv_ln_g,
                conv_ln_b=conv_ln_b, b_pw=b_pw, big={k: w_loc[k] for k in _BIG})
    tok_c, held = lax.optimization_barrier((fly_c[4], held))
    c, positions, ada_b, norm_g, q_lat_g, kv_lat_g, q_norm_g, k_norm_g, glu_b, dw_w, dw_b, conv_ln_g, conv_ln_b, b_pw = (
        held[k] for k in ("c", "positions", "ada_b", "norm_g", "q_lat_g", "kv_lat_g", "q_norm_g", "k_norm_g", "glu_b",
                          "dw_w", "dw_b", "conv_ln_g", "conv_ln_b", "b_pw"))
    wire = {k: held["big"][k].astype(WIRE_DTYPE) for k in _BIG}

    dw_pad = jnp.pad(dw_w, ((0, 0), (0, HALO - CONV_K), (0, 0)))
    c_rows = c.reshape(d // LANE, LANE) + tok_c[0:1, :]
    c_all, dw_all = _all_gather([c_rows, dw_pad], name="gather_c")
    c_all = c_all.reshape(N_DEV, d)
    ada_b_cols = lax.dynamic_slice_in_dim(ada_b, me * ada_cols, ada_cols, axis=1).reshape(nl, 1, ada_cols)
    mod_cols = _ada_fwd(c_all, ada_w, ada_b_cols, name="ada_fwd")
    mod_all = _all_gather([mod_cols], name="gather_mod")[0]
    mod_me = lax.dynamic_index_in_dim(mod_all, me, axis=2, keepdims=False)
    mod = mod_me.transpose(1, 0, 2).reshape(nl, 1, N_DEV * ada_cols)

    from_chips = _exchange_wait(*fly_c[:4], mod, "chips", name="gather_wait_w_in_l0")
    fly_f = _exchange_start([], from_chips, "forward", name="forward_start_w_in_l0")
    w_in_all0 = _exchange_wait(*fly_f[:4], fly_f[4], "forward", name="forward_wait_w_in_l0")[0]
    rest0 = [wire[k][0] for k in _BIG[1:]]
    fly_r0, fly_w1 = {}, {}
    fly_r0["x"] = _exchange_start(rest0, _own_slots(rest0, False, name="own_weights_l0_rest", after=w_in_all0),
                                  "gather", name="gather_start_l0_rest")

    def layout_rest(parts):
        return dict(w_q_up=_qup_permute(_shards_to_cols(parts[0])), w_kv_up=_shards_to_cols(parts[1]),
                    w_pw=parts[2].reshape(D_CONV, D_CONV), w_out=parts[3].reshape(D_MLA + D_CONV, d))

    small_in = dict(norm_g=norm_g, q_lat_g=q_lat_g, kv_lat_g=kv_lat_g, q_norm_g=q_norm_g, k_norm_g=k_norm_g,
                    glu_b=glu_b, dw_b=dw_b, conv_ln_g=conv_ln_g, conv_ln_b=conv_ln_b, b_pw=b_pw)
    dw_full = [_shards_to_cols(dw_all[:, l])[:CONV_K] for l in range(nl)]
    rope = _rope_tiles(positions[0])

    def layer_params(l, w_in_all, rest, mod_l):
        full = dict(dw_w=dw_full)
        if w_in_all is not None:
            full["w_in"] = {l: _win_assemble(w_in_all, name=f"w_in_assemble_l{l}")}
        if rest is not None:
            full.update({k: {l: a} for k, a in layout_rest(rest).items()})
        return _layer_params(l, full, mod_l, small_in)

    def late_l0(z):
        parts = _exchange_wait(*fly_r0["x"][:4], z, "gather", name="gather_wait_l0_rest")
        src1 = [wire[k][1] for k in _BIG]
        fly_w1["x"] = _exchange_start(src1, _own_slots(src1, False, name="own_weights_l1", after=parts[0]), "gather",
                                      name="gather_start_l1")
        late = layout_rest(parts)
        late["q_lat_g"] = small_in["q_lat_g"][0].reshape(1, -1) + fly_w1["x"][4][0, 0]
        return late

    params, saved = [None] * nl, [None] * nl
    p0 = layer_params(0, w_in_all0, None, mod[0] + fly_r0["x"][4][0, 0])
    xs, saved[0], params[0] = _layer_fwd(x2, p0, rope, 0, late=late_l0)
    parts1 = _exchange_wait(*fly_w1["x"][:4], xs, "gather", name="gather_wait_l1")
    params[1] = layer_params(1, parts1[0], parts1[1:], mod[1])
    xs, saved[1], _ = _layer_fwd(xs, params[1], rope, 1)
    gx, loss_part = _loss_head(xs, tgt, name="loss_head")
    loss = lax.psum(loss_part[0, 0], ("x", "y", "c"))

    def shard_major(k, g):
        if k == "w_q_up":
            g = _qup_unpermute(g)
        if k in _COL_SHARDED:
            return _cols_to_shards(g)
        return g.reshape((N_DEV, g.shape[0] // N_DEV, g.shape[1]))

    def scatter_start(send, tag):
        lands = _own_slots(send, True, name=f"own_grads_{tag}")
        return _exchange_start(send, lands, "scatter", name=f"scatter_start_{tag}")

    def wire_rest(big):
        return [shard_major(k, big[k]).astype(WIRE_DTYPE) for k in _BIG[1:]]

    big_g, small_g, flying = [None] * nl, [None] * nl, {}
    gx, big_g[1], small_g[1] = _layer_bwd(gx, params[1], saved[1], rope, 1)
    flying["l1"] = scatter_start([_win_split(big_g[1]["w_in"], name="w_in_split_l1")] + wire_rest(big_g[1]), "l1")
    p0 = dict(params[0])
    p0["gate"] = p0["gate"] + flying["l1"][4][0, 0]

    def start_rest_l0(big):
        flying["l0_rest"] = scatter_start(wire_rest(big), "l0_rest")
        return flying["l0_rest"][4]

    res, arrived = {}, [None] * nl

    def start_w_in_l0(g_w_in):
        flying["l0_w_in"] = scatter_start([_win_split(g_w_in, name="w_in_split_l0")], "l0_w_in")
        tok = flying["l0_w_in"][4]
        arrived[1] = _exchange_wait(*flying["l1"][:4], tok, "scatter", name="scatter_wait_l1")
        arrived[0] = [None] + _exchange_wait(*flying["l0_rest"][:4], tok, "scatter", name="scatter_wait_l0_rest")
        for i, k in enumerate(_BIG):
            if i > 0:
                res[k] = _adamw([arrived[l][i] for l in range(nl)], w_loc[k], m_loc[k], v_loc[k], name=f"adamw_{k}")
        return res["w_out"][0]

    gx, big_g[0], small_g[0] = _layer_bwd(gx, p0, saved[0], rope, 0, hook_rest=start_rest_l0,
                                          hook_w_in=start_w_in_l0)

    tile = 8 * LANE
    padded = [(k, nn, -(-nn // tile) * tile) for k, nn in _SMALL]
    spk = jnp.concatenate([jnp.pad(small_g[l][k].reshape(-1), (0, np_ - nn)).reshape(-1, LANE)
                           for l in range(nl) for k, nn, np_ in padded], axis=0)
    s_all = _all_gather([spk], name="gather_small_grads")[0]
    s_rows = sum(np_ for _, _, np_ in padded) // LANE
    s_all = s_all.reshape(N_DEV, nl, s_rows, LANE)
    s_parts = {k: a[..., :nn] for (k, nn, _), a in
               zip(padded, _unpack_rows(s_all, [(np_,) for _, _, np_ in padded]))}

    dmod_all = s_parts["dmod"]
    dmod_cols = lax.dynamic_slice_in_dim(dmod_all, me * ada_cols, ada_cols, axis=2).transpose(1, 0, 2)
    g_ada_w = _ada_bwd(c_all.T, dmod_cols, name="ada_bwd")
    gp = {}
    gp["ada_w"] = g_ada_w[None]
    gp["ada_b"] = dmod_all
    for k in ("norm_g", "q_lat_g", "kv_lat_g", "glu_b", "dw_b", "conv_ln_g", "conv_ln_b", "b_pw"):
        gp[k] = s_parts[k]
    for k in ("q_norm_g", "k_norm_g"):
        t = s_parts[k]
        gp[k] = jnp.concatenate([t[..., :NOPE], t[..., LANE:LANE + ROPE]], axis=-1)
    dw_g = s_parts["dw_w"].reshape(N_DEV, nl, HALO, D_CONV)[:, :, :CONV_K]
    gp["dw_w"] = lax.dynamic_slice_in_dim(dw_g, me * LANE, LANE, axis=3)

    res["ada_w"] = _adamw(gp["ada_w"], w_loc["ada_w"], m_loc["ada_w"], v_loc["ada_w"], name="adamw_ada_w")
    small_names = [k for k in names if k not in _BIG and k != "ada_w"]
    res.update(zip(small_names, _adamw_small([(gp[k], w_loc[k], m_loc[k], v_loc[k]) for k in small_names],
                                             name="adamw_small")))
    arrived[0][0] = _exchange_wait(*flying["l0_w_in"][:4], res["ada_w"][1], "scatter", name="scatter_wait_l0_w_in")[0]
    w_in_res = _adamw([arrived[l][0] for l in range(nl)], w_loc["w_in"], m_loc["w_in"], v_loc["w_in"],
                      name="adamw_w_in")
    res["w_in"] = tuple(tr(a) for a in w_in_res)
    out = [loss, gx[None]]
    for idx in range(4):
        out += [res[k][idx] for k in names]
    return tuple(out)
```

```python
import functools
import math

import jax
import jax.numpy as jnp
from jax import lax
from jax.experimental import pallas as pl
from jax.experimental.pallas import tpu as pltpu

F32 = jnp.float32
MXU_DTYPE = jnp.bfloat16
WIRE_DTYPE = jnp.bfloat16

D_MODEL = 2048
N_LAYERS = 2
N_DEV = 8
N_HEADS = 8
NOPE = 128
ROPE = 64
V_DIM = 128
QK_DIM = NOPE + ROPE
Q_LORA = 512
KV_LORA = 256
D_MLA = N_HEADS * V_DIM
D_CONV = 1024
CONV_K = 31
ROPE_THETA = 10000.0
EPS = 1e-6
LANE = 128
HEAD_PAD = 2 * LANE
HALO = 32

SEG_CI = (0, 2 * D_CONV)
SEG_MG = (2 * D_CONV, D_MLA)
SEG_CG = (2 * D_CONV + D_MLA, D_CONV)
SEG_QL = (2 * D_CONV + D_MLA + D_CONV, Q_LORA)
SEG_KVL = (SEG_QL[0] + Q_LORA, KV_LORA)
SEG_KR = (SEG_KVL[0] + KV_LORA, LANE)
SEG_LAT = (SEG_QL[0], 1024)
IN_PAD = SEG_LAT[0] + SEG_LAT[1]
IN_TILE = IN_PAD // 4
assert SEG_KR[0] + LANE <= IN_PAD and SEG_LAT[0] % SEG_LAT[1] == 0
IN_COLS = Q_LORA + KV_LORA + ROPE + D_MLA + 2 * D_CONV + D_CONV

ADAM_LR = 0.001
ADAM_B1 = 0.9
ADAM_B2 = 0.999
ADAM_EPS = 1e-08
ADAM_WD = 0.01
ADAM_STEP = 10

VMEM_LIMIT = 56 * 1024 * 1024
ATT_T = 512
ROW_T = 256
CONV_T = 128
MESH_ID = pl.DeviceIdType.MESH


def _cp(sem=None):
    kw = dict(vmem_limit_bytes=VMEM_LIMIT)
    if sem is not None:
        kw["dimension_semantics"] = sem
    return pltpu.CompilerParams(**kw)


def _sds(shape, dtype):
    return jax.ShapeDtypeStruct(shape, dtype)


def _silu(x):
    return x * jax.nn.sigmoid(x)


def _dsilu(x):
    s = jax.nn.sigmoid(x)
    return s * (1.0 + x * (1.0 - s))


def _rowspec(t, width, col=0):
    return pl.BlockSpec((t, width), lambda i: (i, col))


def _vecspec(width):
    return pl.BlockSpec((1, width), lambda i: (0, 0))


def _colsum(v):
    return jnp.sum(v, axis=0, keepdims=True)


def _mm(a, b, *, name, ta=False, tb=False, out_dtype=F32, tm=512, tn=512, tk=None, n_outer=False, after=None,
        residual=None):
    if ta:
        kdim, m = a.shape
    else:
        m, kdim = a.shape
    if tb:
        n, k2 = b.shape
    else:
        k2, n = b.shape
    assert kdim == k2, (a.shape, b.shape)
    tm, tn = min(tm, m), min(tn, n)
    tk = kdim if tk is None else min(tk, kdim)
    assert m % tm == 0 and n % tn == 0 and kdim % tk == 0, (m, n, kdim, tm, tn, tk)
    nk = kdim // tk
    dims = (((0 if ta else 1,), (1 if tb else 0,)), ((), ()))

    n_extra = 0 if after is None else 1
    assert residual is None or nk == 1

    def body(a_ref, b_ref, *rest):
        if residual is not None:
            x_ref, gate_ref = rest[:2]
            rest = rest[2:]
        o_ref, scratch = rest[n_extra], rest[n_extra + 1:]
        prod = lax.dot_general(a_ref[...].astype(MXU_DTYPE), b_ref[...].astype(MXU_DTYPE), dims,
                               preferred_element_type=F32)
        if residual is not None:
            o_ref[...] = prod.astype(o_ref.dtype)
            scratch[0][...] = x_ref[...] + gate_ref[...] * prod
        elif nk == 1:
            o_ref[...] = prod.astype(o_ref.dtype)
        else:
            acc = scratch[0]
            k = pl.program_id(2)

            @pl.when(k == 0)
            def _():
                acc[...] = prod

            @pl.when(k > 0)
            def _():
                acc[...] += prod

            @pl.when(k == nk - 1)
            def _():
                o_ref[...] = acc[...].astype(o_ref.dtype)

    if n_outer:
        ij = lambda g0, g1: (g1, g0)
        grid = (n // tn, m // tm, nk)
    else:
        ij = lambda g0, g1: (g0, g1)
        grid = (m // tm, n // tn, nk)

    def a_map(g0, g1, k):
        i, _ = ij(g0, g1)
        return (k, i) if ta else (i, k)

    def b_map(g0, g1, k):
        _, j = ij(g0, g1)
        return (j, k) if tb else (k, j)

    def o_map(g0, g1, k):
        return ij(g0, g1)

    in_specs = [pl.BlockSpec((tk, tm) if ta else (tm, tk), a_map), pl.BlockSpec((tn, tk) if tb else (tk, tn), b_map)]
    operands = [a, b]
    out_specs, out_shape = pl.BlockSpec((tm, tn), o_map), _sds((m, n), out_dtype)
    if residual is not None:
        in_specs += [pl.BlockSpec((tm, tn), o_map), pl.BlockSpec((1, tn), lambda g0, g1, k: (0, ij(g0, g1)[1]))]
        operands += list(residual)
        out_specs, out_shape = [out_specs, pl.BlockSpec((tm, tn), o_map)], [out_shape, _sds((m, n), F32)]
    if after is not None:
        in_specs.append(_ANY)
        operands.append(after)
    return pl.pallas_call(
        body, name=name, grid=grid, in_specs=in_specs, out_specs=out_specs, out_shape=out_shape,
        scratch_shapes=[pltpu.VMEM((tm, tn), F32)] if nk > 1 else [],
        compiler_params=_cp(("parallel", "parallel", "arbitrary")),
    )(*operands)


def _prenorm(x, g, shift, sc1p, *, name):
    s, d = x.shape
    t = min(2 * ROW_T, s)

    def body(x_ref, g_ref, sh_ref, sc_ref, h_ref):
        xv = x_ref[...]
        r = lax.rsqrt(jnp.mean(xv * xv, axis=-1, keepdims=True) + EPS)
        h_ref[...] = ((xv * r) * g_ref[...] * sc_ref[...] + sh_ref[...]).astype(h_ref.dtype)

    return pl.pallas_call(
        body, name=name, grid=(s // t,),
        in_specs=[_rowspec(t, d), _vecspec(d), _vecspec(d), _vecspec(d)],
        out_specs=_rowspec(t, d), out_shape=_sds((s, d), MXU_DTYPE),
        compiler_params=_cp(("parallel",)),
    )(x, g, shift, sc1p)


def _rope_fwd(r, c_t, s1_t, s2_t):
    return r * c_t + pltpu.roll(r, LANE - ROPE // 2, 1) * s1_t + pltpu.roll(r, ROPE // 2, 1) * s2_t


def _rope_bwd(d, c_t, s1_t, s2_t):
    return d * c_t + pltpu.roll(d * s1_t, ROPE // 2, 1) + pltpu.roll(d * s2_t, LANE - ROPE // 2, 1)


def _lanesum(v):
    return jnp.sum(v, axis=-1, keepdims=True)


def _mla_pre(z, w_q_up, w_kv_up, g_ql, g_kvl, c_t, s1_t, s2_t, gqn, gqr, gkn, gkr, *, name):
    s = z.shape[0]
    t = min(2 * ROW_T, s)
    scale = 1.0 / math.sqrt(QK_DIM)
    wide = 2 * N_HEADS * LANE

    def body(ql_ref, kvl_ref, kr_ref, wq_ref, wkv_ref, gq_ref, gk_ref, c_ref, s1_ref, s2_ref,
             gqn_ref, gqr_ref, gkn_ref, gkr_ref, qn_ref, kn_ref, q_ref, kv_ref, qf_ref, kf_ref, vf_ref):
        for src, g_ref, dst, w_ref, up in ((ql_ref, gq_ref, qn_ref, wq_ref, q_ref),
                                           (kvl_ref, gk_ref, kn_ref, wkv_ref, kv_ref)):
            v = src[...]
            r = lax.rsqrt(jnp.mean(v * v, axis=-1, keepdims=True) + EPS)
            dst[...] = ((v * r) * g_ref[...]).astype(dst.dtype)
            up[...] = jnp.dot(dst[...], w_ref[...], preferred_element_type=F32)
        c_v, s1_v, s2_v = c_ref[...], s1_ref[...], s2_ref[...]
        kr = kr_ref[...]
        kr_ss = _lanesum(kr * kr)
        for h in range(N_HEADS):
            n = q_ref[:, h * LANE:(h + 1) * LANE]
            r = q_ref[:, N_HEADS * LANE + h * LANE:N_HEADS * LANE + (h + 1) * LANE]
            rs = lax.rsqrt((_lanesum(n * n) + _lanesum(r * r)) * (1.0 / QK_DIM) + EPS)
            qf_ref[h, :, 0:LANE] = (((n * rs) * gqn_ref[...]) * scale).astype(qf_ref.dtype)
            rr = _rope_fwd((r * rs) * gqr_ref[...], c_v, s1_v, s2_v)
            qf_ref[h, :, LANE:HEAD_PAD] = (rr * scale).astype(qf_ref.dtype)

            n = kv_ref[:, h * 2 * LANE:h * 2 * LANE + LANE]
            rs = lax.rsqrt((_lanesum(n * n) + kr_ss) * (1.0 / QK_DIM) + EPS)
            kf_ref[h, :, 0:LANE] = ((n * rs) * gkn_ref[...]).astype(kf_ref.dtype)
            kf_ref[h, :, LANE:HEAD_PAD] = _rope_fwd((kr * rs) * gkr_ref[...], c_v, s1_v, s2_v).astype(kf_ref.dtype)
            vf_ref[h, :, 0:V_DIM] = kv_ref[:, h * 2 * LANE + LANE:(h + 1) * 2 * LANE].astype(vf_ref.dtype)
            vf_ref[h, :, V_DIM:] = jnp.ones((t, V_DIM), vf_ref.dtype)

    hspec = lambda w: pl.BlockSpec((N_HEADS, t, w), lambda i: (0, i, 0))
    whole = lambda a: pl.BlockSpec(a.shape, lambda i: (0, 0))
    return pl.pallas_call(
        body, name=name, grid=(s // t,),
        in_specs=[_rowspec(t, Q_LORA, SEG_QL[0] // Q_LORA), _rowspec(t, KV_LORA, SEG_KVL[0] // KV_LORA),
                  _rowspec(t, LANE, SEG_KR[0] // LANE), whole(w_q_up), whole(w_kv_up),
                  _vecspec(Q_LORA), _vecspec(KV_LORA),
                  _rowspec(t, LANE), _rowspec(t, LANE), _rowspec(t, LANE),
                  _vecspec(LANE), _vecspec(LANE), _vecspec(LANE), _vecspec(LANE)],
        out_specs=[_rowspec(t, Q_LORA), _rowspec(t, KV_LORA), _rowspec(t, wide), _rowspec(t, wide),
                   hspec(HEAD_PAD), hspec(HEAD_PAD), hspec(2 * V_DIM)],
        out_shape=[_sds((s, Q_LORA), MXU_DTYPE), _sds((s, KV_LORA), MXU_DTYPE), _sds((s, wide), F32),
                   _sds((s, wide), F32), _sds((N_HEADS, s, HEAD_PAD), MXU_DTYPE),
                   _sds((N_HEADS, s, HEAD_PAD), MXU_DTYPE), _sds((N_HEADS, s, 2 * V_DIM), MXU_DTYPE)],
        compiler_params=_cp(("parallel",)),
    )(z, z, z, w_q_up, w_kv_up, g_ql, g_kvl, c_t, s1_t, s2_t, gqn, gqr, gkn, gkr)


def _causal_mask(t):
    row = lax.broadcasted_iota(jnp.int32, (t, t), 0)
    col = lax.broadcasted_iota(jnp.int32, (t, t), 1)
    return col <= row


NEG = -1e30


def _flash_fwd(qf, kf, va, *, name):
    nh, s, dk = qf.shape
    dv = va.shape[-1] // 2
    t = min(ATT_T, s)
    n = s // t
    assert dv == LANE and t % LANE == 0

    def body(q_ref, k_ref, v_ref, o_ref, lse_ref, m_s, acc_s, s_buf):
        i = pl.program_id(1)
        m_s[...] = jnp.full(m_s.shape, NEG, F32)
        acc_s[...] = jnp.zeros(acc_s.shape, F32)

        def rows_of(j):
            return pl.ds(pl.multiple_of(j * t, t), t)

        def scores(qi, j):
            return lax.dot_general(q_ref[0, rows_of(qi), :], k_ref[0, rows_of(j), :], (((1,), (1,)), ((), ())),
                                   preferred_element_type=F32)

        def consume(j, slot, masked):
            sc = s_buf[slot]
            if masked:
                sc = jnp.where(_causal_mask(t), sc, NEG)
            m_prev = m_s[...]
            m_new = jnp.maximum(m_prev, jnp.max(sc, axis=-1, keepdims=True))
            alpha = jnp.exp(m_prev - m_new)
            p = jnp.exp(sc - jnp.tile(m_new, (1, t // LANE)))
            acc_s[...] = jnp.tile(alpha, (1, 2)) * acc_s[...] + jnp.dot(
                p.astype(MXU_DTYPE), v_ref[0, rows_of(j), :], preferred_element_type=F32)
            m_s[...] = m_new

        nxt = jnp.minimum(i + 1, n - 1)

        @pl.when(i == 0)
        def _():
            s_buf[2] = scores(0, 0)
            consume(0, 2, True)
            s_buf[2] = scores(nxt, 0)

        @pl.when(i > 0)
        def _():
            s_buf[1] = scores(i, 1)
            consume(0, 2, False)

            def pair(a, carry):
                s_buf[0] = scores(i, 2 * a + 2)
                consume(2 * a + 1, 1, False)
                s_buf[1] = scores(i, 2 * a + 3)
                consume(2 * a + 2, 0, False)
                return carry

            lax.fori_loop(0, (i - 1) // 2, pair, 0)

            @pl.when(i % 2 == 1)
            def _():
                s_buf[2] = scores(nxt, 0)
                consume(i, 1, True)

            @pl.when(i % 2 == 0)
            def _():
                s_buf[0] = scores(i, i)
                consume(i - 1, 1, False)
                s_buf[2] = scores(nxt, 0)
                consume(i, 0, True)

        den = acc_s[:, dv:]
        o_ref[...] = acc_s[:, :dv] / den
        lse_ref[0] = m_s[...] + jnp.log(den)

    head = lambda h, i: (h, 0, 0)
    return pl.pallas_call(
        body, name=name, grid=(nh, n),
        in_specs=[pl.BlockSpec((1, s, dk), head), pl.BlockSpec((1, s, dk), head), pl.BlockSpec((1, s, 2 * dv), head)],
        out_specs=[pl.BlockSpec((t, dv), lambda h, i: (i, h)),
                   pl.BlockSpec((1, t, LANE), lambda h, i: (h, i, 0))],
        out_shape=[_sds((s, nh * dv), F32), _sds((nh, s, LANE), F32)],
        scratch_shapes=[pltpu.VMEM((t, LANE), F32), pltpu.VMEM((t, 2 * dv), F32), pltpu.VMEM((3, t, t), F32)],
        compiler_params=_cp(("arbitrary", "arbitrary")),
    )(qf, kf, va)


def _shifted_copies(ext_ref):
    rows = ext_ref.shape[1] - 8
    for s in range(1, 8):
        ext_ref[s, 0:rows, :] = ext_ref[0, s:s + rows, :]


def _windows(ext_ref, offsets, t_rows, lane0, lanes):
    for s in range(8):
        group = [o for o in offsets if o % 8 == s]
        if not group:
            continue
        lo, hi = min(group) - s, max(group) - s
        wide = ext_ref[s, pl.ds(lo, hi - lo + t_rows), lane0:lane0 + lanes]
        for o in group:
            yield o, wide[o - s - lo:o - s - lo + t_rows]


def _dw_taps(ext_ref, w_ref, row0, t_rows, lane0, lanes, first_off):
    acc = None
    for off, win in _windows(ext_ref, [row0 + first_off + k for k in range(CONV_K)], t_rows, lane0, lanes):
        k = off - row0 - first_off
        term = w_ref[k:k + 1, lane0:lane0 + lanes] * win
        acc = term if acc is None else acc + term
    return acc


CONV_RC = 32
CONV_LC = 256


def _conv_fwd(z, glu_b, dw_w, dw_b, ln_g, ln_b, w_pw, *, name):
    s = z.shape[0]
    t = min(CONV_T, s)
    c2 = 2 * D_CONV
    hb = t // HALO

    def body(zm_ref, zh_ref, gb_ref, w_ref, wb_ref, g_ref, b_ref, wpw_ref, u1_ref, u3_ref, u4_ref, ext):
        i = pl.program_id(0)

        def glu(zv):
            ci = zv + gb_ref[...]
            return ci[:, :D_CONV] * jax.nn.sigmoid(ci[:, D_CONV:])

        ext[0, HALO:, :] = glu(zm_ref[...])
        ext[0, 0:HALO, :] = jnp.where(i > 0, glu(zh_ref[...]), 0.0)
        _shifted_copies(ext)
        for rc in range(0, t, CONV_RC):
            for lc in range(0, D_CONV, CONV_LC):
                acc = _dw_taps(ext, w_ref, rc, CONV_RC, lc, CONV_LC, HALO - (CONV_K - 1))
                u1_ref[rc:rc + CONV_RC, lc:lc + CONV_LC] = acc + wb_ref[:, lc:lc + CONV_LC]
        u1 = u1_ref[...]
        mu = jnp.mean(u1, axis=-1, keepdims=True)
        cen = u1 - mu
        var = jnp.mean(cen * cen, axis=-1, keepdims=True)
        u2 = (cen * lax.rsqrt(var + EPS)) * g_ref[...] + b_ref[...]
        u3_ref[...] = _silu(u2).astype(u3_ref.dtype)
        u4_ref[...] = jnp.dot(u3_ref[...], wpw_ref[...], preferred_element_type=F32)

    return pl.pallas_call(
        body, name=name, grid=(s // t,),
        in_specs=[_rowspec(t, c2), pl.BlockSpec((HALO, c2), lambda i: (jnp.maximum(i * hb - 1, 0), 0)),
                  _vecspec(c2), pl.BlockSpec((HALO, D_CONV), lambda i: (0, 0)), _vecspec(D_CONV),
                  _vecspec(D_CONV), _vecspec(D_CONV), pl.BlockSpec((D_CONV, D_CONV), lambda i: (0, 0))],
        out_specs=[_rowspec(t, D_CONV), _rowspec(t, D_CONV), _rowspec(t, D_CONV)],
        out_shape=[_sds((s, D_CONV), F32), _sds((s, D_CONV), MXU_DTYPE), _sds((s, D_CONV), F32)],
        scratch_shapes=[pltpu.VMEM((8, t + HALO, D_CONV), F32)],
        compiler_params=_cp(("parallel",)),
    )(z, z, glu_b, dw_w, dw_b, ln_g, ln_b, w_pw)


def _gate_cat(o, z, u4m, b_pw, *, name):
    s = o.shape[0]
    t = min(2 * ROW_T, s)

    def body(o_ref, mg_ref, u4_ref, cg_ref, b_ref, cat_ref):
        cat_ref[:, :D_MLA] = (o_ref[...] * _silu(mg_ref[...])).astype(cat_ref.dtype)
        cat_ref[:, D_MLA:] = ((u4_ref[...] + b_ref[...]) * _silu(cg_ref[...])).astype(cat_ref.dtype)

    return pl.pallas_call(
        body, name=name, grid=(s // t,),
        in_specs=[_rowspec(t, D_MLA), _rowspec(t, D_MLA, SEG_MG[0] // D_MLA), _rowspec(t, D_CONV),
                  _rowspec(t, D_CONV, SEG_CG[0] // D_CONV), _vecspec(D_CONV)],
        out_specs=_rowspec(t, D_MLA + D_CONV), out_shape=_sds((s, D_MLA + D_CONV), MXU_DTYPE),
        compiler_params=_cp(("parallel",)),
    )(o, z, u4m, z, b_pw)


def _loss_head(xf, target, *, name):
    s, d = xf.shape
    t = min(2 * ROW_T, s)

    def body(x_ref, t_ref, gx_ref, loss_ref):
        @pl.when(pl.program_id(0) == 0)
        def _():
            loss_ref[...] = jnp.zeros(loss_ref.shape, F32)

        err = x_ref[...] - t_ref[...]
        gx_ref[...] = err * (1.0 / d)
        loss_ref[...] += 0.5 * jnp.sum(_lanesum(err * err) * (1.0 / d), axis=0, keepdims=True)

    return pl.pallas_call(
        body, name=name, grid=(s // t,),
        in_specs=[_rowspec(t, d), _rowspec(t, d)],
        out_specs=[_rowspec(t, d), pl.BlockSpec((1, 1), lambda i: (0, 0))],
        out_shape=[_sds((s, d), F32), _sds((1, 1), F32)],
        compiler_params=_cp(("arbitrary",)),
    )(xf, target)


def _acc_init(refs):
    @pl.when(pl.program_id(0) == 0)
    def _():
        for r in refs:
            r[...] = jnp.zeros(r.shape, r.dtype)


def _out_bwd(gxo, y, gate, *, name):
    s, d = gxo.shape
    t = min(2 * ROW_T, s)

    def body(g_ref, y_ref, gate_ref, dy_ref, dgate_ref):
        _acc_init([dgate_ref])
        gv = g_ref[...]
        dy_ref[...] = (gv * gate_ref[...]).astype(dy_ref.dtype)
        dgate_ref[...] += _colsum(gv * y_ref[...])

    return pl.pallas_call(
        body, name=name, grid=(s // t,),
        in_specs=[_rowspec(t, d), _rowspec(t, d), _vecspec(d)],
        out_specs=[_rowspec(t, d), _vecspec(d)],
        out_shape=[_sds((s, d), MXU_DTYPE), _sds((1, d), F32)],
        compiler_params=_cp(("arbitrary",)),
    )(gxo, y, gate)


def _gate_bwd(dy, w_out, o, z, u4m, b_pw, *, name):
    s, d = dy.shape
    t = min(2 * ROW_T, s)
    gates = D_MLA + D_CONV
    assert SEG_CG[0] == SEG_MG[0] + D_MLA and SEG_MG[0] % gates == 0

    def body(dy_ref, w_ref, o_ref, mg_ref, u4_ref, cg_ref, b_ref,
             do_ref, delta_ref, du4_ref, gb_ref, dz_ref):
        _acc_init([gb_ref])
        dcat = lax.dot_general(dy_ref[...], w_ref[...], (((1,), (1,)), ((), ())), preferred_element_type=F32)
        dm, ov, mg = dcat[:, :D_MLA], o_ref[...], mg_ref[...]
        do = dm * _silu(mg)
        do_ref[...] = do.astype(do_ref.dtype)
        dz_ref[:, :D_MLA] = (dm * ov * _dsilu(mg)).astype(dz_ref.dtype)
        prod = do * ov
        for h in range(N_HEADS):
            delta_ref[h] = _lanesum(prod[:, h * V_DIM:(h + 1) * V_DIM])
        dc, cg = dcat[:, D_MLA:], cg_ref[...]
        du4 = dc * _silu(cg)
        du4_ref[...] = du4.astype(du4_ref.dtype)
        dz_ref[:, D_MLA:] = (dc * (u4_ref[...] + b_ref[...]) * _dsilu(cg)).astype(dz_ref.dtype)
        gb_ref[...] += _colsum(du4)

    return pl.pallas_call(
        body, name=name, grid=(s // t,),
        in_specs=[_rowspec(t, d), pl.BlockSpec((gates, d), lambda i: (0, 0)), _rowspec(t, D_MLA),
                  _rowspec(t, D_MLA, SEG_MG[0] // D_MLA), _rowspec(t, D_CONV),
                  _rowspec(t, D_CONV, SEG_CG[0] // D_CONV), _vecspec(D_CONV)],
        out_specs=[_rowspec(t, D_MLA), pl.BlockSpec((N_HEADS, t, 1), lambda i: (0, i, 0)),
                   _rowspec(t, D_CONV), _vecspec(D_CONV), _rowspec(t, gates, SEG_MG[0] // gates)],
        out_shape=[_sds((s, D_MLA), MXU_DTYPE), _sds((N_HEADS, s, 1), F32),
                   _sds((s, D_CONV), MXU_DTYPE), _sds((1, D_CONV), F32), _sds((s, IN_PAD), MXU_DTYPE)],
        compiler_params=_cp(("arbitrary",)),
    )(dy, w_out, o, z, u4m, z, b_pw)


def _conv_bwd(du3, u1, z, dz, glu_b, dw_w, ln_g, ln_b, *, name):
    s = z.shape[0]
    t = min(CONV_T, s)
    c2 = 2 * D_CONV
    hb = t // HALO
    n_blk = s // t
    last_halo = s // HALO - 1

    def body(d3m_ref, d3h_ref, u1m_ref, u1h_ref, zm_ref, zh_ref, gb_ref, w_ref, g_ref, b_ref, dz_in_ref,
             dci_ref, gg_ref, gbn_ref, gwb_ref, ggb_ref, gw_ref, dext, uext, du0_s, gw_acc):
        i = pl.program_id(0)
        _acc_init([gg_ref, gbn_ref, gwb_ref, ggb_ref, gw_acc])

        def ln_bwd(d3, u1v):
            mu = jnp.mean(u1v, axis=-1, keepdims=True)
            cen = u1v - mu
            rstd = lax.rsqrt(jnp.mean(cen * cen, axis=-1, keepdims=True) + EPS)
            uh = cen * rstd
            d2 = d3 * _dsilu(uh * g_ref[...] + b_ref[...])
            dh = d2 * g_ref[...]
            d1 = rstd * (dh - jnp.mean(dh, axis=-1, keepdims=True) - uh * jnp.mean(dh * uh, axis=-1, keepdims=True))
            return d1, d2, uh

        d1, d2, uh = ln_bwd(d3m_ref[...], u1m_ref[...])
        gg_ref[...] += _colsum(d2 * uh)
        gbn_ref[...] += _colsum(d2)
        gwb_ref[...] += _colsum(d1)
        dext[0, 0:t, :] = d1
        d1h, _, _ = ln_bwd(d3h_ref[...], u1h_ref[...])
        dext[0, t:, :] = jnp.where(i < n_blk - 1, d1h, 0.0)
        _shifted_copies(dext)

        def glu_parts(zv):
            ci = zv + gb_ref[...]
            return ci[:, :D_CONV], jax.nn.sigmoid(ci[:, D_CONV:])

        val, sg = glu_parts(zm_ref[...])
        uext[0, HALO:, :] = val * sg
        valh, sgh = glu_parts(zh_ref[...])
        uext[0, 0:HALO, :] = jnp.where(i > 0, valh * sgh, 0.0)
        _shifted_copies(uext)

        for rc in range(0, t, CONV_RC):
            for lc in range(0, D_CONV, CONV_LC):
                acc = None
                for off, win in _windows(dext, [rc + k for k in range(CONV_K)], CONV_RC, lc, CONV_LC):
                    k = (CONV_K - 1) - (off - rc)
                    term = w_ref[k:k + 1, lc:lc + CONV_LC] * win
                    acc = term if acc is None else acc + term
                du0_s[rc:rc + CONV_RC, lc:lc + CONV_LC] = acc
                dchunk = dext[0, rc:rc + CONV_RC, lc:lc + CONV_LC]
                first = rc + HALO - (CONV_K - 1)
                for off, win in _windows(uext, [first + k for k in range(CONV_K)], CONV_RC, lc, CONV_LC):
                    k = off - first
                    pr = dchunk * win
                    part = pr[0:8]
                    for r8 in range(8, CONV_RC, 8):
                        part = part + pr[r8:r8 + 8]
                    gw_acc[k, :, lc:lc + CONV_LC] += part

        du0 = du0_s[...]
        dval = du0 * sg
        dgt = du0 * val * sg * (1.0 - sg)
        dci_ref[:, :D_CONV] = dval.astype(dci_ref.dtype)
        dci_ref[:, D_CONV:] = dgt.astype(dci_ref.dtype)
        ggb_ref[:, :D_CONV] += _colsum(dval)
        ggb_ref[:, D_CONV:] += _colsum(dgt)

        @pl.when(i == n_blk - 1)
        def _():
            gw_ref[...] = jnp.sum(gw_acc[...], axis=1)

    halo_next = lambda w: pl.BlockSpec((HALO, w), lambda i: (jnp.minimum((i + 1) * hb, last_halo), 0))
    return pl.pallas_call(
        body, name=name, grid=(n_blk,),
        in_specs=[_rowspec(t, D_CONV), halo_next(D_CONV), _rowspec(t, D_CONV), halo_next(D_CONV),
                  _rowspec(t, c2), pl.BlockSpec((HALO, c2), lambda i: (jnp.maximum(i * hb - 1, 0), 0)),
                  _vecspec(c2), pl.BlockSpec((HALO, D_CONV), lambda i: (0, 0)), _vecspec(D_CONV), _vecspec(D_CONV),
                  _ANY],
        out_specs=[_rowspec(t, c2, SEG_CI[0] // c2), _vecspec(D_CONV), _vecspec(D_CONV), _vecspec(D_CONV),
                   _vecspec(c2), pl.BlockSpec((HALO, D_CONV), lambda i: (0, 0))],
        out_shape=[_sds(dz.shape, dz.dtype), _sds((1, D_CONV), F32), _sds((1, D_CONV), F32), _sds((1, D_CONV), F32),
                   _sds((1, c2), F32), _sds((HALO, D_CONV), F32)],
        scratch_shapes=[pltpu.VMEM((8, t + HALO, D_CONV), F32), pltpu.VMEM((8, t + HALO, D_CONV), F32),
                        pltpu.VMEM((t, D_CONV), F32), pltpu.VMEM((HALO, 8, D_CONV), F32)],
        input_output_aliases={10: 0},
        compiler_params=_cp(("arbitrary",)),
    )(du3, du3, u1, u1, z, z, glu_b, dw_w, ln_g, ln_b, dz)


def _flash_bwd(qf, kf, va, do, lse_t, delta_t, *, name):
    nh, s, dk = qf.shape
    dv = va.shape[-1] // 2
    t = min(ATT_T, s)
    n = s // t
    nt = (((1,), (1,)), ((), ()))
    tn = (((0,), (0,)), ((), ()))

    def body(q_ref, do_ref, lse_ref, dl_ref, k_ref, v_ref, dq_ref, dk_ref, dv_ref,
             dk_s, dv_s, st_buf, dpt_buf):
        n_un = pl.program_id(1)
        j = n - 1 - n_un
        nxt = jnp.maximum(j - 1, 0)

        @pl.when(n_un == 0)
        def _():
            dq_ref[...] = jnp.zeros(dq_ref.shape, F32)

        dk_s[...] = jnp.zeros(dk_s.shape, F32)
        dv_s[...] = jnp.zeros(dv_s.shape, F32)

        def rows_at(blk):
            return pl.ds(pl.multiple_of(blk * t, t), t)

        def rows_of(b):
            return rows_at(n - 1 - b)

        k = k_ref[0, rows_at(j), :]

        def produce(kj, b, slot):
            rows = rows_of(b)
            st_buf[slot] = lax.dot_general(k_ref[0, rows_at(kj), :], q_ref[0, rows, :], nt,
                                           preferred_element_type=F32)
            dpt_buf[slot] = lax.dot_general(v_ref[0, rows_at(kj), 0:dv], do_ref[rows, :], nt,
                                            preferred_element_type=F32)

        def consume(b, slot, masked):
            i = n - 1 - b
            rows = rows_of(b)
            q, dov = q_ref[0, rows, :], do_ref[rows, :]
            pt = jnp.exp(st_buf[slot] - lse_ref[0, i])
            if masked:
                key = lax.broadcasted_iota(jnp.int32, (t, t), 0)
                qry = lax.broadcasted_iota(jnp.int32, (t, t), 1)
                pt = jnp.where(key <= qry, pt, 0.0)
            dv_s[...] += jnp.dot(pt.astype(MXU_DTYPE), dov, preferred_element_type=F32)
            dst = (pt * (dpt_buf[slot] - dl_ref[0, i])).astype(MXU_DTYPE)
            dk_s[...] += jnp.dot(dst, q, preferred_element_type=F32)
            dq_ref[0, rows, :] += lax.dot_general(dst, k, tn, preferred_element_type=F32)

        @pl.when(n_un == 0)
        def _():
            produce(j, 0, 2)
            consume(0, 2, True)
            produce(nxt, 0, 2)

        @pl.when(n_un > 0)
        def _():
            produce(j, 1, 1)
            consume(0, 2, False)

            def pair(a, carry):
                produce(j, 2 * a + 2, 0)
                consume(2 * a + 1, 1, False)
                produce(j, 2 * a + 3, 1)
                consume(2 * a + 2, 0, False)
                return carry

            lax.fori_loop(0, (n_un - 1) // 2, pair, 0)

            @pl.when(n_un % 2 == 1)
            def _():
                produce(nxt, 0, 2)
                consume(n_un, 1, True)

            @pl.when(n_un % 2 == 0)
            def _():
                produce(j, n_un, 0)
                consume(n_un - 1, 1, False)
                produce(nxt, 0, 2)
                consume(n_un, 0, True)

        dk_ref[0] = dk_s[...]
        dv_ref[0] = dv_s[...]

    head = lambda h, j: (h, 0, 0)
    rowv = pl.BlockSpec((1, n, 1, t), lambda h, j: (h, 0, 0, 0))
    return pl.pallas_call(
        body, name=name, grid=(nh, n),
        in_specs=[pl.BlockSpec((1, s, dk), head),
                  pl.BlockSpec((s, dv), lambda h, j: (0, h)),
                  rowv, rowv,
                  pl.BlockSpec((1, s, dk), head),
                  pl.BlockSpec((1, s, 2 * dv), head)],
        out_specs=[pl.BlockSpec((1, s, dk), head),
                   pl.BlockSpec((1, t, dk), lambda h, g: (h, n - 1 - g, 0)),
                   pl.BlockSpec((1, t, dv), lambda h, g: (h, n - 1 - g, 0))],
        out_shape=[_sds((nh, s, dk), F32), _sds((nh, s, dk), F32), _sds((nh, s, dv), F32)],
        scratch_shapes=[pltpu.VMEM((t, dk), F32), pltpu.VMEM((t, dv), F32),
                        pltpu.VMEM((3, t, t), F32), pltpu.VMEM((3, t, t), F32)],
        compiler_params=_cp(("arbitrary", "arbitrary")),
    )(qf, do, lse_t, delta_t, kf, va)


def _mla_bwd(dqf, dkf, dvf, q_raw, kv, z, dz, qn, kn, w_q_up, w_kv_up, g_ql, g_kvl, c_t, s1_t, s2_t,
             gqn, gqr, gkn, gkr, *, name):
    s = q_raw.shape[0]
    t = min(ROW_T, s)
    scale = 1.0 / math.sqrt(QK_DIM)
    o_ql, o_kvl, o_kr = (seg[0] - SEG_LAT[0] for seg in (SEG_QL, SEG_KVL, SEG_KR))
    tn = (((0,), (0,)), ((), ()))
    nt = (((1,), (1,)), ((), ()))

    def body(dq_ref, dk_ref, dv_ref, q_ref, kv_ref, kr_ref, ql_ref, kvl_ref, qn_ref, kn_ref, wq_ref, wkv_ref,
             gq_ref, gk_ref, c_ref, s1_ref, s2_ref, gqn_ref, gqr_ref, gkn_ref, gkr_ref, dz_in_ref,
             dz_ref, gwq_ref, gwkv_ref, ggq_ref, ggk_ref, gql_ref, gkvl_ref, dqr_ref, dkv_ref):
        _acc_init([gwq_ref, gwkv_ref, ggq_ref, ggk_ref, gql_ref, gkvl_ref])
        c_v, s1_v, s2_v = c_ref[...], s1_ref[...], s2_ref[...]
        kr = kr_ref[...]
        kr_ss = _lanesum(kr * kr)
        dkr = jnp.zeros(kr.shape, F32)
        ggq_n = ggq_r = ggk_n = ggk_r = jnp.zeros((1, LANE), F32)

        def norm_bwd(n, r, rs, dyn, dyr, gn, gr):
            nh_, rh_ = n * rs, r * rs
            dnh, drh = dyn * gn, dyr * gr
            dot = (_lanesum(dnh * nh_) + _lanesum(drh * rh_)) * (1.0 / QK_DIM)
            return rs * (dnh - nh_ * dot), rs * (drh - rh_ * dot), _colsum(dyn * nh_), _colsum(dyr * rh_)

        for h in range(N_HEADS):
            n = q_ref[:, h * LANE:(h + 1) * LANE]
            r = q_ref[:, N_HEADS * LANE + h * LANE:N_HEADS * LANE + (h + 1) * LANE]
            rs = lax.rsqrt((_lanesum(n * n) + _lanesum(r * r)) * (1.0 / QK_DIM) + EPS)
            dyn = dq_ref[h, :, 0:LANE] * scale
            dyr = _rope_bwd(dq_ref[h, :, LANE:HEAD_PAD] * scale, c_v, s1_v, s2_v)
            dn, dr, g_n, g_r = norm_bwd(n, r, rs, dyn, dyr, gqn_ref[...], gqr_ref[...])
            dqr_ref[:, h * LANE:(h + 1) * LANE] = dn.astype(dqr_ref.dtype)
            dqr_ref[:, N_HEADS * LANE + h * LANE:N_HEADS * LANE + (h + 1) * LANE] = dr.astype(dqr_ref.dtype)
            ggq_n, ggq_r = ggq_n + g_n, ggq_r + g_r

            n = kv_ref[:, h * 2 * LANE:h * 2 * LANE + LANE]
            rs = lax.rsqrt((_lanesum(n * n) + kr_ss) * (1.0 / QK_DIM) + EPS)
            dyn = dk_ref[h, :, 0:LANE]
            dyr = _rope_bwd(dk_ref[h, :, LANE:HEAD_PAD], c_v, s1_v, s2_v)
            dn, dr, g_n, g_r = norm_bwd(n, kr, rs, dyn, dyr, gkn_ref[...], gkr_ref[...])
            dkv_ref[:, h * 2 * LANE:h * 2 * LANE + LANE] = dn.astype(dkv_ref.dtype)
            dkv_ref[:, h * 2 * LANE + LANE:(h + 1) * 2 * LANE] = dv_ref[h].astype(dkv_ref.dtype)
            dkr = dkr + dr
            ggk_n, ggk_r = ggk_n + g_n, ggk_r + g_r

        ggq_ref[:, 0:LANE] += ggq_n
        ggq_ref[:, LANE:] += ggq_r
        ggk_ref[:, 0:LANE] += ggk_n
        ggk_ref[:, LANE:] += ggk_r

        for d_ref, x_ref, w_ref, gw_ref, src, g_ref, off, gg_ref in (
                (dqr_ref, qn_ref, wq_ref, gwq_ref, ql_ref, gq_ref, o_ql, gql_ref),
                (dkv_ref, kn_ref, wkv_ref, gwkv_ref, kvl_ref, gk_ref, o_kvl, gkvl_ref)):
            dup = d_ref[...]
            gw_ref[...] += lax.dot_general(x_ref[...], dup, tn, preferred_element_type=F32)
            dy = lax.dot_general(dup, w_ref[...], nt, preferred_element_type=F32)
            v = src[...]
            r = lax.rsqrt(jnp.mean(v * v, axis=-1, keepdims=True) + EPS)
            vh = v * r
            dvh = dy * g_ref[...]
            dz_ref[:, off:off + v.shape[1]] = (
                r * (dvh - vh * jnp.mean(dvh * vh, axis=-1, keepdims=True))).astype(dz_ref.dtype)
            gg_ref[...] += _colsum(dy * vh)
        dz_ref[:, o_kr:o_kr + LANE] = dkr.astype(dz_ref.dtype)
        dz_ref[:, o_kr + LANE:] = jnp.zeros((t, SEG_LAT[1] - o_kr - LANE), dz_ref.dtype)

    hspec = lambda w: pl.BlockSpec((N_HEADS, t, w), lambda i: (0, i, 0))
    whole = lambda a: pl.BlockSpec(a.shape, lambda i: (0, 0))
    wide = 2 * N_HEADS * LANE
    return pl.pallas_call(
        body, name=name, grid=(s // t,),
        in_specs=[hspec(HEAD_PAD), hspec(HEAD_PAD), hspec(V_DIM), _rowspec(t, wide), _rowspec(t, wide),
                  _rowspec(t, LANE, SEG_KR[0] // LANE), _rowspec(t, Q_LORA, SEG_QL[0] // Q_LORA),
                  _rowspec(t, KV_LORA, SEG_KVL[0] // KV_LORA), _rowspec(t, Q_LORA), _rowspec(t, KV_LORA),
                  whole(w_q_up), whole(w_kv_up), _vecspec(Q_LORA), _vecspec(KV_LORA),
                  _rowspec(t, LANE), _rowspec(t, LANE), _rowspec(t, LANE),
                  _vecspec(LANE), _vecspec(LANE), _vecspec(LANE), _vecspec(LANE), _ANY],
        out_specs=[_rowspec(t, SEG_LAT[1], SEG_LAT[0] // SEG_LAT[1]), whole(w_q_up), whole(w_kv_up),
                   _vecspec(2 * LANE), _vecspec(2 * LANE), _vecspec(Q_LORA), _vecspec(KV_LORA)],
        out_shape=[_sds(dz.shape, dz.dtype), _sds(w_q_up.shape, F32), _sds(w_kv_up.shape, F32),
                   _sds((1, 2 * LANE), F32), _sds((1, 2 * LANE), F32), _sds((1, Q_LORA), F32),
                   _sds((1, KV_LORA), F32)],
        scratch_shapes=[pltpu.VMEM((t, wide), MXU_DTYPE), pltpu.VMEM((t, wide), MXU_DTYPE)],
        input_output_aliases={21: 0},
        compiler_params=_cp(("arbitrary",)),
    )(dqf, dkf, dvf, q_raw, kv, z, z, z, qn, kn, w_q_up, w_kv_up, g_ql, g_kvl, c_t, s1_t, s2_t,
      gqn, gqr, gkn, gkr, dz)


def _prenorm_bwd(dh, x, gxo, g, sc1p, *, name):
    s, d = x.shape
    t = min(2 * ROW_T, s)

    def body(dh_ref, x_ref, gx_ref, g_ref, sc_ref, dx_ref, dsh_ref, dsc_ref, gg_ref):
        _acc_init([dsh_ref, dsc_ref, gg_ref])
        xv, dhv = x_ref[...], dh_ref[...]
        r = lax.rsqrt(jnp.mean(xv * xv, axis=-1, keepdims=True) + EPS)
        xn = xv * r
        dsh_ref[...] += _colsum(dhv)
        dsc_ref[...] += _colsum(dhv * (xn * g_ref[...]))
        dm = dhv * sc_ref[...]
        gg_ref[...] += _colsum(dm * xn)
        dxn = dm * g_ref[...]
        dx_ref[...] = gx_ref[...] + r * (dxn - xn * jnp.mean(dxn * xn, axis=-1, keepdims=True))

    return pl.pallas_call(
        body, name=name, grid=(s // t,),
        in_specs=[_rowspec(t, d), _rowspec(t, d), _rowspec(t, d), _vecspec(d), _vecspec(d)],
        out_specs=[_rowspec(t, d), _vecspec(d), _vecspec(d), _vecspec(d)],
        out_shape=[_sds((s, d), F32), _sds((1, d), F32), _sds((1, d), F32), _sds((1, d), F32)],
        compiler_params=_cp(("arbitrary",)),
    )(dh, x, gxo, g, sc1p)


def _ada_fwd(c_all, ada_w, ada_b_cols, *, name):
    nl, d, cols = ada_w.shape

    def body(c_ref, w_ref, b_ref, o_ref):
        ca = _silu(c_ref[...]).astype(MXU_DTYPE)
        o_ref[0] = jnp.dot(ca, w_ref[0].astype(MXU_DTYPE), preferred_element_type=F32) + b_ref[0]

    return pl.pallas_call(
        body, name=name, grid=(nl,),
        in_specs=[pl.BlockSpec((N_DEV, d), lambda l: (0, 0)), pl.BlockSpec((1, d, cols), lambda l: (l, 0, 0)),
                  pl.BlockSpec((1, 1, cols), lambda l: (l, 0, 0))],
        out_specs=pl.BlockSpec((1, N_DEV, cols), lambda l: (l, 0, 0)),
        out_shape=_sds((nl, N_DEV, cols), F32),
        compiler_params=_cp(("parallel",)),
    )(c_all, ada_w, ada_b_cols)


def _ada_bwd(c_all_t, dmod_cols, *, name):
    nl, _, cols = dmod_cols.shape
    d = c_all_t.shape[0]

    def body(c_ref, dm_ref, o_ref):
        ca = _silu(c_ref[...]).astype(MXU_DTYPE)
        o_ref[0] = jnp.dot(ca, dm_ref[0].astype(MXU_DTYPE), preferred_element_type=F32)

    return pl.pallas_call(
        body, name=name, grid=(nl,),
        in_specs=[pl.BlockSpec((d, N_DEV), lambda l: (0, 0)), pl.BlockSpec((1, N_DEV, cols), lambda l: (l, 0, 0))],
        out_specs=pl.BlockSpec((1, d, cols), lambda l: (l, 0, 0)),
        out_shape=_sds((nl, d, cols), F32),
        compiler_params=_cp(("parallel",)),
    )(c_all_t, dmod_cols)


def _adamw_math(g, w, m, v):
    mn = ADAM_B1 * m + (1.0 - ADAM_B1) * g
    vn = ADAM_B2 * v + (1.0 - ADAM_B2) * (g * g)
    m_hat = mn / (1.0 - ADAM_B1 ** ADAM_STEP)
    v_hat = vn / (1.0 - ADAM_B2 ** ADAM_STEP)
    return -ADAM_LR * (m_hat / (jnp.sqrt(v_hat) + ADAM_EPS) + ADAM_WD * w), mn, vn


def _adamw_small(items, *, name):
    n = len(items)
    shapes = [it[1].shape for it in items]
    flat = lambda a, lead: a.reshape(lead + (-1, a.shape[-1]))
    operands = []
    for gp, w, m, v in items:
        operands += [flat(gp, (gp.shape[0],)), flat(w, ()), flat(m, ()), flat(v, ())]
    nparts = [it[0].shape[0] for it in items]

    def body(*refs):
        ins, outs = refs[:4 * n], refs[4 * n:]
        for i in range(n):
            g_ref, w_ref, m_ref, v_ref = ins[4 * i:4 * i + 4]
            g = g_ref[0].astype(F32)
            for p in range(1, nparts[i]):
                g = g + g_ref[p].astype(F32)
            outs[4 * i][...] = g
            outs[4 * i + 1][...], outs[4 * i + 2][...], outs[4 * i + 3][...] = _adamw_math(
                g, w_ref[...], m_ref[...], v_ref[...])

    out_shape = []
    for it in items:
        out_shape += [_sds(flat(it[1], ()).shape, F32)] * 4
    outs = pl.pallas_call(body, name=name, out_shape=out_shape, compiler_params=_cp())(*operands)
    return [tuple(o.reshape(shp) for o in outs[4 * i:4 * i + 4]) for i, shp in enumerate(shapes)]


def _adamw(gparts, w, m, v, *, name):
    shape = w.shape
    cols = shape[-1]
    per_layer = isinstance(gparts, (list, tuple))
    nl = shape[0] if per_layer else 1
    rows = w.size // cols // nl
    glist = list(gparts) if per_layer else [gparts]
    npart = glist[0].shape[0]
    glist = [g.reshape(npart, rows, cols) for g in glist]
    w3, m3, v3 = (a.reshape(nl, rows, cols) for a in (w, m, v))
    budget = 2 * 1024 * 1024
    fits = [t for t in range(min(rows, 256) // 8 * 8, 7, -8)
            if rows % t == 0 and npart * t * cols * glist[0].dtype.itemsize <= budget]
    t = fits[0] if fits else rows
    nb = rows // t

    def body(*refs):
        g_refs = refs[:nl]
        w_ref, m_ref, v_ref, go_ref, d_ref, mo_ref, vo_ref, g_s = refs[nl:]
        layer = pl.program_id(0)
        for l in range(nl):
            @pl.when(layer == l)
            def _(l=l):
                g = g_refs[l][0].astype(F32)
                for p in range(1, npart):
                    g = g + g_refs[l][p].astype(F32)
                g_s[...] = g

        g = g_s[...]
        go_ref[0] = g
        d_ref[0], mo_ref[0], vo_ref[0] = _adamw_math(g, w_ref[0], m_ref[0], v_ref[0])

    def g_map(l):
        return lambda layer, i: (0, jnp.where(layer == l, i, jnp.where(layer < l, 0, nb - 1)), 0)

    spec = pl.BlockSpec((1, t, cols), lambda layer, i: (layer, i, 0))
    outs = pl.pallas_call(
        body, name=name, grid=(nl, nb),
        in_specs=[pl.BlockSpec((npart, t, cols), g_map(l)) for l in range(nl)] + [spec, spec, spec],
        out_specs=[spec] * 4, out_shape=[_sds((nl, rows, cols), F32)] * 4,
        scratch_shapes=[pltpu.VMEM((t, cols), F32)],
        compiler_params=_cp(("arbitrary", "arbitrary")),
    )(*glist, w3, m3, v3)
    return tuple(o.reshape(shape) for o in outs)


_ANY = pl.BlockSpec(memory_space=pl.ANY)


def _all_gather(blocks, *, name):
    na = len(blocks)

    def body(*refs):
        x_refs, out_refs = refs[:na], refs[na:2 * na]
        send_sems, recv_sems, local_sems = refs[2 * na:]
        x, y, c = lax.axis_index("x"), lax.axis_index("y"), lax.axis_index("c")
        me, sibling = (x, y, c), (x, y, 1 - c)
        chips = [(1 - x, y), (x, 1 - y), (1 - x, 1 - y)]

        def slot(a, px, py, pc):
            return out_refs[a].at[4 * px + 2 * py + pc]

        def copy(a, k, blk, to, src=None):
            return pltpu.make_async_remote_copy(
                src_ref=slot(a, *blk) if src is None else src, dst_ref=slot(a, *blk),
                send_sem=send_sems.at[7 * a + k], recv_sem=recv_sems.at[7 * a + k],
                device_id=to, device_id_type=MESH_ID)

        mine = [pltpu.make_async_copy(x_refs[a], slot(a, *me), local_sems.at[a]) for a in range(na)]
        for cp in mine:
            cp.start()
        first = []
        for a in range(na):
            first.append(copy(a, 0, me, sibling, src=x_refs[a]))
            first += [copy(a, 1 + j, me, (*chip, c), src=x_refs[a]) for j, chip in enumerate(chips)]
        for cp in first:
            cp.start()
        passed = []
        for a in range(na):
            for j, chip in enumerate(chips):
                copy(a, 1 + j, (*chip, c), me).wait_recv()
                fwd = copy(a, 4 + j, (*chip, c), sibling)
                fwd.start()
                passed.append(fwd)
        for a in range(na):
            copy(a, 0, sibling, me).wait_recv()
            for j, chip in enumerate(chips):
                copy(a, 4 + j, (*chip, 1 - c), me).wait_recv()
        for cp in first + passed:
            cp.wait_send()
        for cp in mine:
            cp.wait()

    outs = pl.pallas_call(
        body, name=name, in_specs=[_ANY] * na, out_specs=[_ANY] * na,
        out_shape=[_sds((N_DEV,) + b.shape, b.dtype) for b in blocks],
        scratch_shapes=[pltpu.SemaphoreType.DMA((7 * na,)), pltpu.SemaphoreType.DMA((7 * na,)),
                        pltpu.SemaphoreType.DMA((na,))],
    )(*blocks)
    return list(outs)


_HBM = pl.BlockSpec(memory_space=pltpu.HBM)
_SEM = pl.BlockSpec(memory_space=pltpu.SEMAPHORE)
_EFFECT = pltpu.SideEffectType.DATAFLOW_SIDE_EFFECTING


def _peers(x, y, c):
    out = []
    for k in range(1, N_DEV):
        out.append((1 - x if k & 4 else x, 1 - y if k & 2 else y, 1 - c if k & 1 else c))
    return out


def _own_slots(srcs, scatter, *, name, after=None):
    na = len(srcs)
    n_extra = 0 if after is None else 1
    me = (4 * lax.axis_index("x") + 2 * lax.axis_index("y") + lax.axis_index("c")).astype(jnp.int32).reshape(1)

    def body(me_ref, *refs):
        in_refs, out_refs = refs[:na], refs[na + n_extra:]
        for a in range(na):
            out_refs[a][0] = in_refs[a][0] if scatter else in_refs[a][...]

    def slot_spec(shard):
        zeros = (0,) * len(shard)
        return pl.BlockSpec((1,) + tuple(shard), lambda i, me_ref: (me_ref[0],) + zeros)

    def whole_spec(shape):
        zeros = (0,) * len(shape)
        return pl.BlockSpec(tuple(shape), lambda i, me_ref: zeros)

    shards = [s.shape[1:] if scatter else s.shape for s in srcs]
    in_specs = [slot_spec(sh) if scatter else whole_spec(sh) for sh in shards] + [_ANY] * n_extra
    outs = pl.pallas_call(
        body, name=name,
        grid_spec=pltpu.PrefetchScalarGridSpec(
            num_scalar_prefetch=1, grid=(1,), in_specs=in_specs, out_specs=[slot_spec(sh) for sh in shards]),
        out_shape=[_sds((N_DEV,) + tuple(sh), s.dtype) for sh, s in zip(shards, srcs)],
        compiler_params=_cp(("arbitrary",)),
    )(me, *srcs, *([] if after is None else [after]))
    return list(outs)


_N_COPIES = dict(scatter=7, gather=7, chips=4, forward=3)


def _exchange_copies(src_refs, land_refs, send_sems, recv_sems, mode):
    x, y, c = lax.axis_index("x"), lax.axis_index("y"), lax.axis_index("c")
    me = 4 * x + 2 * y + c
    nc = _N_COPIES[mode]
    chips = [(1 - x, y), (x, 1 - y), (1 - x, 1 - y)]
    cps = []
    for a in range(len(land_refs)):
        if mode in ("scatter", "gather"):
            plan = [((src_refs[a].at[4 * px + 2 * py + pc] if mode == "scatter" else src_refs[a]),
                     land_refs[a].at[me], (px, py, pc)) for px, py, pc in _peers(x, y, c)]
        elif mode == "chips":
            plan = [(src_refs[a], land_refs[a].at[me], to) for to in [(x, y, 1 - c)] + [(*ch, c) for ch in chips]]
        else:
            plan = [(land_refs[a].at[4 * px + 2 * py + c], land_refs[a].at[4 * px + 2 * py + c], (x, y, 1 - c))
                    for px, py in chips]
        for k, (src, dst, to) in enumerate(plan):
            cps.append(pltpu.make_async_remote_copy(
                src_ref=src, dst_ref=dst, send_sem=send_sems.at[nc * a + k], recv_sem=recv_sems.at[nc * a + k],
                device_id=to, device_id_type=MESH_ID))
    return cps


def _exchange_start(srcs, lands, mode, *, name):
    ns, nz = len(srcs), len(lands)
    nsem = _N_COPIES[mode] * nz

    def body(*refs):
        src_refs, land_refs = refs[:ns], refs[ns:ns + nz]
        send_sems, recv_sems = refs[ns + nz], refs[ns + nz + 1]
        token = refs[-1]
        for cp in _exchange_copies(src_refs, land_refs, send_sems, recv_sems, mode):
            cp.start()
        token[...] = jnp.zeros(token.shape, token.dtype)

    hbm = lambda a: pltpu.HBM(a.shape, a.dtype)
    outs = pl.pallas_call(
        body, name=name,
        out_shape=(pltpu.SemaphoreType.DMA((nsem,)), pltpu.SemaphoreType.DMA((nsem,)),
                   *[hbm(a) for a in srcs], *[hbm(a) for a in lands], _sds((8, LANE), F32)),
        in_specs=[_HBM] * (ns + nz),
        out_specs=(_SEM, _SEM, *[_HBM] * (ns + nz), pl.BlockSpec(memory_space=pltpu.VMEM)),
        input_output_aliases={i: 2 + i for i in range(ns + nz)},
        compiler_params=pltpu.CompilerParams(has_side_effects=_EFFECT),
    )(*[pltpu.with_memory_space_constraint(a, pltpu.HBM) for a in list(srcs) + list(lands)])
    return outs[0], outs[1], list(outs[2:2 + ns]), list(outs[2 + ns:2 + ns + nz]), outs[-1]


def _exchange_wait(send_sems, recv_sems, srcs, lands, after, mode, *, name):
    ns, nz = len(srcs), len(lands)

    def body(*refs):
        src_refs, land_refs = refs[:ns], refs[ns:ns + nz]
        s_sems, r_sems = refs[ns + nz], refs[ns + nz + 1]
        for cp in _exchange_copies(src_refs, land_refs, s_sems, r_sems, mode):
            cp.wait_send()
            cp.wait_recv()

    hbm = lambda a: pltpu.HBM(a.shape, a.dtype)
    outs = pl.pallas_call(
        body, name=name,
        out_shape=(*[hbm(a) for a in srcs], *[hbm(a) for a in lands]),
        in_specs=[_HBM] * (ns + nz) + [_SEM, _SEM, _ANY],
        out_specs=tuple([_HBM] * (ns + nz)),
        input_output_aliases={i: i for i in range(ns + nz)},
        compiler_params=pltpu.CompilerParams(has_side_effects=_EFFECT),
    )(*srcs, *lands, send_sems, recv_sems, after)
    return list(outs[ns:])


_WIN_SEGS = (("ql", 0, Q_LORA, SEG_QL[0]), ("kvl", Q_LORA, KV_LORA, SEG_KVL[0]),
             ("kr", Q_LORA + KV_LORA, ROPE, SEG_KR[0]), ("mg", Q_LORA + KV_LORA + ROPE, D_MLA, SEG_MG[0]),
             ("ci", Q_LORA + KV_LORA + ROPE + D_MLA, 2 * D_CONV, SEG_CI[0]),
             ("cg", Q_LORA + KV_LORA + ROPE + D_MLA + 2 * D_CONV, D_CONV, SEG_CG[0]))
_WIN_SHARD = IN_COLS // N_DEV


def _win_pieces():
    out = []
    for _, o, n, new in _WIN_SEGS:
        for j in range(N_DEV):
            lo, hi = max(o, j * _WIN_SHARD), min(o + n, (j + 1) * _WIN_SHARD)
            if lo < hi:
                out.append((j, lo - j * _WIN_SHARD, new + lo - o, hi - lo))
    return out


WIN_T = 512


def _win_assemble(w_all, *, name):
    d = w_all.shape[2]
    t = min(WIN_T, d)
    pieces = sorted(_win_pieces(), key=lambda p: p[2])
    assert all(lo % 8 == 0 and n % 8 == 0 for _, lo, _, n in pieces)

    def body(w_ref, o_ref):
        rows = [w_ref[j].astype(F32)[lo:lo + n, :] for j, lo, _, n in pieces]
        rows.append(jnp.zeros((IN_PAD - (SEG_KR[0] + ROPE), t), F32))
        o_ref[...] = jnp.concatenate(rows, axis=0).astype(o_ref.dtype)

    return pl.pallas_call(
        body, name=name, grid=(d // t,),
        in_specs=[pl.BlockSpec((N_DEV, _WIN_SHARD, t), lambda i: (0, 0, i))],
        out_specs=pl.BlockSpec((IN_PAD, t), lambda i: (0, i)), out_shape=_sds((IN_PAD, d), w_all.dtype),
        compiler_params=_cp(("parallel",)),
    )(w_all)


def _win_split(grad, *, name):
    d = grad.shape[1]
    t = min(WIN_T, d)
    by_shard = [sorted([p for p in _win_pieces() if p[0] == j], key=lambda p: p[1]) for j in range(N_DEV)]

    def body(g_ref, o_ref):
        for j in range(N_DEV):
            rows = [g_ref[new:new + n, :] for _, _, new, n in by_shard[j]]
            o_ref[j] = jnp.concatenate(rows, axis=0).astype(o_ref.dtype)

    return pl.pallas_call(
        body, name=name, grid=(d // t,),
        in_specs=[pl.BlockSpec((IN_PAD, t), lambda i: (0, i))],
        out_specs=pl.BlockSpec((N_DEV, _WIN_SHARD, t), lambda i: (0, 0, i)),
        out_shape=_sds((N_DEV, _WIN_SHARD, d), WIRE_DTYPE),
        compiler_params=_cp(("parallel",)),
    )(grad)


def _cols_to_shards(a):
    r, n = a.shape
    return a.reshape(r, N_DEV, n // N_DEV).transpose(1, 0, 2)


def _shards_to_cols(a):
    nd, r, w = a.shape
    return a.transpose(1, 0, 2).reshape(r, nd * w)


def _qup_permute(w):
    w3 = w.reshape(w.shape[0], N_HEADS, QK_DIM)
    nope = w3[:, :, :NOPE].reshape(w.shape[0], N_HEADS * NOPE)
    rope = jnp.pad(w3[:, :, NOPE:], ((0, 0), (0, 0), (0, LANE - ROPE))).reshape(w.shape[0], N_HEADS * LANE)
    return jnp.concatenate([nope, rope], axis=1)


def _qup_unpermute(g):
    r = g.shape[0]
    nope = g[:, :N_HEADS * NOPE].reshape(r, N_HEADS, NOPE)
    rope = g[:, N_HEADS * NOPE:].reshape(r, N_HEADS, LANE)[:, :, :ROPE]
    return jnp.concatenate([nope, rope], axis=2).reshape(r, N_HEADS * QK_DIM)


def _norm_tiles(g):
    return g[:NOPE].reshape(1, LANE), jnp.pad(g[NOPE:], (0, LANE - ROPE)).reshape(1, LANE)


def _rope_tiles(positions):
    inv_freq = 1.0 / (ROPE_THETA ** (jnp.arange(0, ROPE, 2, dtype=F32) / ROPE))
    ang = positions.astype(F32)[:, None] * inv_freq
    cos, sin = jnp.cos(ang), jnp.sin(ang)
    zq = jnp.zeros_like(cos)
    c_t = jnp.concatenate([cos, cos, zq, zq], axis=1)
    s1_t = jnp.concatenate([-sin, zq, zq, zq], axis=1)
    s2_t = jnp.concatenate([zq, sin, zq, zq], axis=1)
    return c_t, s1_t, s2_t


_BIG = ("w_in", "w_q_up", "w_kv_up", "w_pw", "w_out")
_COL_SHARDED = ("w_q_up", "w_kv_up")


def _unpack_rows(buf, shapes):
    out, r0 = [], 0
    lead = buf.shape[:-2]
    for shp in shapes:
        n = math.prod(shp) // LANE
        out.append(buf[..., r0:r0 + n, :].reshape(lead + tuple(shp)))
        r0 += n
    return out


_SMALL = (("dmod", 3 * D_MODEL), ("norm_g", D_MODEL), ("q_lat_g", Q_LORA), ("kv_lat_g", KV_LORA),
          ("q_norm_g", 2 * LANE), ("k_norm_g", 2 * LANE), ("glu_b", 2 * D_CONV), ("dw_w", HALO * D_CONV),
          ("dw_b", D_CONV), ("conv_ln_g", D_CONV), ("conv_ln_b", D_CONV), ("b_pw", D_CONV))


def _layer_fwd(x, p, rope, l, late=None):
    n = lambda s: f"{s}_l{l}"
    c_t, s1_t, s2_t = rope
    h = _prenorm(x, p["norm_g"], p["shift"], p["sc1p"], name=n("prenorm"))
    z = _mm(h, p["w_in"], tb=True, name=n("in_proj"), tn=IN_TILE, n_outer=True)
    if late is not None:
        p = {**p, **late(z)}
    qn, kn, q_raw, kv, qf, kf, vf = _mla_pre(z, p["w_q_up"], p["w_kv_up"], p["q_lat_g"], p["kv_lat_g"],
                                             c_t, s1_t, s2_t, *p["qk_tiles"], name=n("mla_pre"))
    o, lse = _flash_fwd(qf, kf, vf, name=n("flash_fwd"))
    u1, u3, u4m = _conv_fwd(z, p["glu_b"], p["dw_w"], p["dw_b"], p["conv_ln_g"], p["conv_ln_b"], p["w_pw"],
                            name=n("conv_fwd"))
    cat = _gate_cat(o, z, u4m, p["b_pw"], name=n("gate_cat"))
    y, x_next = _mm(cat, p["w_out"], name=n("out_proj"), tn=1024, residual=(x, p["gate"]))
    saved = dict(x=x, h=h, z=z, qn=qn, kn=kn, q_raw=q_raw, kv=kv, qf=qf, kf=kf, vf=vf, o=o, lse=lse,
                 u1=u1, u3=u3, u4m=u4m, cat=cat, y=y)
    return x_next, saved, p


def _layer_bwd(gxo, p, sv, rope, l, hook_rest=None, hook_w_in=None):
    n = lambda s: f"{s}_l{l}"
    c_t, s1_t, s2_t = rope
    z = sv["z"]
    dy, dgate = _out_bwd(gxo, sv["y"], p["gate"], name=n("out_bwd"))
    g_w_out = _mm(sv["cat"], dy, ta=True, name=n("g_w_out"), tm=1024, tn=1024)
    do, delta, du4, g_b_pw, dz = _gate_bwd(dy, p["w_out"], sv["o"], z, sv["u4m"], p["b_pw"], name=n("gate_bwd"))
    g_w_pw = _mm(sv["u3"], du4, ta=True, name=n("g_w_pw"), tm=1024, tn=1024, tk=512)
    du3 = _mm(du4, p["w_pw"], tb=True, name=n("d_u3"), tn=1024)
    dz, g_ln_g, g_ln_b, g_dw_b, g_glu_b, g_dw_w = _conv_bwd(
        du3, sv["u1"], z, dz, p["glu_b"], p["dw_w"], p["conv_ln_g"], p["conv_ln_b"], name=n("conv_bwd"))
    t_att = min(ATT_T, z.shape[0])
    to_lanes = lambda a: a.reshape(N_HEADS, z.shape[0] // t_att, 1, t_att)
    dqf, dkf, dvf = _flash_bwd(sv["qf"], sv["kf"], sv["vf"], do,
                               to_lanes(sv["lse"][:, :, 0]), to_lanes(delta), name=n("flash_bwd"))
    dz, g_w_q_up, g_w_kv_up, g_qn, g_kn, g_ql, g_kvl = _mla_bwd(
        dqf, dkf, dvf, sv["q_raw"], sv["kv"], z, dz, sv["qn"], sv["kn"], p["w_q_up"], p["w_kv_up"],
        p["q_lat_g"], p["kv_lat_g"], c_t, s1_t, s2_t, *p["qk_tiles"], name=n("mla_bwd"))
    big = dict(w_q_up=g_w_q_up, w_kv_up=g_w_kv_up, w_pw=g_w_pw, w_out=g_w_out)
    after = None if hook_rest is None else hook_rest(big)
    g_w_in = _mm(dz, sv["h"], ta=True, name=n("g_w_in"), tm=512, tn=1024, after=after)
    big["w_in"] = g_w_in
    after = None if hook_w_in is None else hook_w_in(g_w_in)
    dh = _mm(dz, p["w_in"], name=n("d_h"), tn=1024, after=after)
    dx, dshift, dscale, g_norm = _prenorm_bwd(dh, sv["x"], gxo, p["norm_g"], p["sc1p"], name=n("prenorm_bwd"))
    small = dict(dmod=jnp.concatenate([dshift, dscale, dgate], axis=1), norm_g=g_norm, q_lat_g=g_ql, kv_lat_g=g_kvl,
                 q_norm_g=g_qn, k_norm_g=g_kn, glu_b=g_glu_b, dw_w=g_dw_w, dw_b=g_dw_b,
                 conv_ln_g=g_ln_g, conv_ln_b=g_ln_b, b_pw=g_b_pw)
    return dx, big, small


def _layer_params(l, full, mod_l, small):
    d = D_MODEL
    row = lambda a: a.reshape(1, -1)
    shift, scale, gate = mod_l[:, :d], mod_l[:, d:2 * d], mod_l[:, 2 * d:]
    dw_w = jnp.pad(full["dw_w"][l], ((0, HALO - CONV_K), (0, 0)))
    return dict(
        shift=shift, sc1p=1.0 + scale, gate=gate, norm_g=row(small["norm_g"][l]),
        **{k: full[k][l] for k in _BIG if k in full}, dw_w=dw_w,
        q_lat_g=row(small["q_lat_g"][l]), kv_lat_g=row(small["kv_lat_g"][l]),
        qk_tiles=_norm_tiles(small["q_norm_g"][l]) + _norm_tiles(small["k_norm_g"][l]),
        glu_b=row(small["glu_b"][l]), dw_b=row(small["dw_b"][l]), conv_ln_g=row(small["conv_ln_g"][l]),
        conv_ln_b=row(small["conv_ln_b"][l]), b_pw=row(small["b_pw"][l]))


def kernel(x, c, positions, ada_w, ada_b, norm_g, w_in, q_lat_g, w_q_up, kv_lat_g, w_kv_up, q_norm_g, k_norm_g, glu_b, dw_w, dw_b, conv_ln_g, conv_ln_b, w_pw, b_pw, w_out, loss_target, m_ada_w, m_ada_b, m_norm_g, m_w_in, m_q_lat_g, m_w_q_up, m_kv_lat_g, m_w_kv_up, m_q_norm_g, m_k_norm_g, m_glu_b, m_dw_w, m_dw_b, m_conv_ln_g, m_conv_ln_b, m_w_pw, m_b_pw, m_w_out, v_ada_w, v_ada_b, v_norm_g, v_w_in, v_q_lat_g, v_w_q_up, v_kv_lat_g, v_w_kv_up, v_q_norm_g, v_k_norm_g, v_glu_b, v_dw_w, v_dw_b, v_conv_ln_g, v_conv_ln_b, v_w_pw, v_b_pw, v_w_out):
    names = ("ada_w", "ada_b", "norm_g", "w_in", "q_lat_g", "w_q_up", "kv_lat_g", "w_kv_up", "q_norm_g",
             "k_norm_g", "glu_b", "dw_w", "dw_b", "conv_ln_g", "conv_ln_b", "w_pw", "b_pw", "w_out")
    w_loc = dict(zip(names, (ada_w, ada_b, norm_g, w_in, q_lat_g, w_q_up, kv_lat_g, w_kv_up, q_norm_g, k_norm_g,
                             glu_b, dw_w, dw_b, conv_ln_g, conv_ln_b, w_pw, b_pw, w_out)))
    m_loc = dict(zip(names, (m_ada_w, m_ada_b, m_norm_g, m_w_in, m_q_lat_g, m_w_q_up, m_kv_lat_g, m_w_kv_up,
                             m_q_norm_g, m_k_norm_g, m_glu_b, m_dw_w, m_dw_b, m_conv_ln_g, m_conv_ln_b, m_w_pw,
                             m_b_pw, m_w_out)))
    v_loc = dict(zip(names, (v_ada_w, v_ada_b, v_norm_g, v_w_in, v_q_lat_g, v_w_q_up, v_kv_lat_g, v_w_kv_up,
                             v_q_norm_g, v_k_norm_g, v_glu_b, v_dw_w, v_dw_b, v_conv_ln_g, v_conv_ln_b, v_w_pw,
                             v_b_pw, v_w_out)))
    nl, d = N_LAYERS, D_MODEL
    me = 4 * lax.axis_index("x") + 2 * lax.axis_index("y") + lax.axis_index("c")
    x2, tgt = x[0], loss_target[0]
    ada_cols = ada_w.shape[-1]

    tr = lambda a: jnp.swapaxes(a, 1, 2)
    w_loc, m_loc, v_loc = ({**dd, "w_in": tr(dd["w_in"])} for dd in (w_loc, m_loc, v_loc))
    w_in0 = [w_loc["w_in"][0].astype(WIRE_DTYPE)]
    fly_c = _exchange_start(w_in0, _own_slots(w_in0, False, name="own_w_in_l0"), "chips", name="gather_start_w_in_l0")
    held = dict(c=c, positions=positions, ada_b=ada_b, norm_g=norm_g, q_lat_g=q_lat_g, kv_lat_g=kv_lat_g,
                q_norm_g=q_norm_g, k_norm_g=k_norm_g, glu_b=glu_b, dw_w=dw_w, dw_b=dw_b, conv_ln_g=conv_ln_g,
                conv_ln_b=conv_ln_b, b_pw=b_pw, big={k: w_loc[k] for k in _BIG})
    tok_c, held = lax.optimization_barrier((fly_c[4], held))
    c, positions, ada_b, norm_g, q_lat_g, kv_lat_g, q_norm_g, k_norm_g, glu_b, dw_w, dw_b, conv_ln_g, conv_ln_b, b_pw = (
        held[k] for k in ("c", "positions", "ada_b", "norm_g", "q_lat_g", "kv_lat_g", "q_norm_g", "k_norm_g", "glu_b",
                          "dw_w", "dw_b", "conv_ln_g", "conv_ln_b", "b_pw"))
    wire = {k: held["big"][k].astype(WIRE_DTYPE) for k in _BIG}

    dw_pad = jnp.pad(dw_w, ((0, 0), (0, HALO - CONV_K), (0, 0)))
    c_rows = c.reshape(d // LANE, LANE) + tok_c[0:1, :]
    c_all, dw_all = _all_gather([c_rows, dw_pad], name="gather_c")
    c_all = c_all.reshape(N_DEV, d)
    ada_b_cols = lax.dynamic_slice_in_dim(ada_b, me * ada_cols, ada_cols, axis=1).reshape(nl, 1, ada_cols)
    mod_cols = _ada_fwd(c_all, ada_w, ada_b_cols, name="ada_fwd")
    mod_all = _all_gather([mod_cols], name="gather_mod")[0]
    mod_me = lax.dynamic_index_in_dim(mod_all, me, axis=2, keepdims=False)
    mod = mod_me.transpose(1, 0, 2).reshape(nl, 1, N_DEV * ada_cols)

    from_chips = _exchange_wait(*fly_c[:4], mod, "chips", name="gather_wait_w_in_l0")
    fly_f = _exchange_start([], from_chips, "forward", name="forward_start_w_in_l0")
    w_in_all0 = _exchange_wait(*fly_f[:4], fly_f[4], "forward", name="forward_wait_w_in_l0")[0]
    rest0 = [wire[k][0] for k in _BIG[1:]]
    fly_r0, fly_w1 = {}, {}
    fly_r0["x"] = _exchange_start(rest0, _own_slots(rest0, False, name="own_weights_l0_rest", after=w_in_all0),
                                  "gather", name="gather_start_l0_rest")

    def layout_rest(parts):
        return dict(w_q_up=_qup_permute(_shards_to_cols(parts[0])), w_kv_up=_shards_to_cols(parts[1]),
                    w_pw=parts[2].reshape(D_CONV, D_CONV), w_out=parts[3].reshape(D_MLA + D_CONV, d))

    small_in = dict(norm_g=norm_g, q_lat_g=q_lat_g, kv_lat_g=kv_lat_g, q_norm_g=q_norm_g, k_norm_g=k_norm_g,
                    glu_b=glu_b, dw_b=dw_b, conv_ln_g=conv_ln_g, conv_ln_b=conv_ln_b, b_pw=b_pw)
    dw_full = [_shards_to_cols(dw_all[:, l])[:CONV_K] for l in range(nl)]
    rope = _rope_tiles(positions[0])

    def layer_params(l, w_in_all, rest, mod_l):
        full = dict(dw_w=dw_full)
        if w_in_all is not None:
            full["w_in"] = {l: _win_assemble(w_in_all, name=f"w_in_assemble_l{l}")}
        if rest is not None:
            full.update({k: {l: a} for k, a in layout_rest(rest).items()})
        return _layer_params(l, full, mod_l, small_in)

    def late_l0(z):
        parts = _exchange_wait(*fly_r0["x"][:4], z, "gather", name="gather_wait_l0_rest")
        src1 = [wire[k][1] for k in _BIG]
        fly_w1["x"] = _exchange_start(src1, _own_slots(src1, False, name="own_weights_l1", after=parts[0]), "gather",
                                      name="gather_start_l1")
        late = layout_rest(parts)
        late["q_lat_g"] = small_in["q_lat_g"][0].reshape(1, -1) + fly_w1["x"][4][0, 0]
        return late

    params, saved = [None] * nl, [None] * nl
    p0 = layer_params(0, w_in_all0, None, mod[0] + fly_r0["x"][4][0, 0])
    xs, saved[0], params[0] = _layer_fwd(x2, p0, rope, 0, late=late_l0)
    parts1 = _exchange_wait(*fly_w1["x"][:4], xs, "gather", name="gather_wait_l1")
    params[1] = layer_params(1, parts1[0], parts1[1:], mod[1])
    xs, saved[1], _ = _layer_fwd(xs, params[1], rope, 1)
    gx, loss_part = _loss_head(xs, tgt, name="loss_head")
    loss = lax.psum(loss_part[0, 0], ("x", "y", "c"))

    def shard_major(k, g):
        if k == "w_q_up":
            g = _qup_unpermute(g)
        if k in _COL_SHARDED:
            return _cols_to_shards(g)
        return g.reshape((N_DEV, g.shape[0] // N_DEV, g.shape[1]))

    def scatter_start(send, tag):
        lands = _own_slots(send, True, name=f"own_grads_{tag}")
        return _exchange_start(send, lands, "scatter", name=f"scatter_start_{tag}")

    def wire_rest(big):
        return [shard_major(k, big[k]).astype(WIRE_DTYPE) for k in _BIG[1:]]

    big_g, small_g, flying = [None] * nl, [None] * nl, {}
    gx, big_g[1], small_g[1] = _layer_bwd(gx, params[1], saved[1], rope, 1)
    flying["l1"] = scatter_start([_win_split(big_g[1]["w_in"], name="w_in_split_l1")] + wire_rest(big_g[1]), "l1")
    p0 = dict(params[0])
    p0["gate"] = p0["gate"] + flying["l1"][4][0, 0]

    def start_rest_l0(big):
        flying["l0_rest"] = scatter_start(wire_rest(big), "l0_rest")
        return flying["l0_rest"][4]

    res, arrived = {}, [None] * nl

    def start_w_in_l0(g_w_in):
        flying["l0_w_in"] = scatter_start([_win_split(g_w_in, name="w_in_split_l0")], "l0_w_in")
        tok = flying["l0_w_in"][4]
        arrived[1] = _exchange_wait(*flying["l1"][:4], tok, "scatter", name="scatter_wait_l1")
        arrived[0] = [None] + _exchange_wait(*flying["l0_rest"][:4], tok, "scatter", name="scatter_wait_l0_rest")
        for i, k in enumerate(_BIG):
            if i > 0:
                res[k] = _adamw([arrived[l][i] for l in range(nl)], w_loc[k], m_loc[k], v_loc[k], name=f"adamw_{k}")
        return res["w_out"][0]

    gx, big_g[0], small_g[0] = _layer_bwd(gx, p0, saved[0], rope, 0, hook_rest=start_rest_l0,
                                          hook_w_in=start_w_in_l0)

    tile = 8 * LANE
    padded = [(k, nn, -(-nn // tile) * tile) for k, nn in _SMALL]
    spk = jnp.concatenate([jnp.pad(small_g[l][k].reshape(-1), (0, np_ - nn)).reshape(-1, LANE)
                           for l in range(nl) for k, nn, np_ in padded], axis=0)
    s_all = _all_gather([spk], name="gather_small_grads")[0]
    s_rows = sum(np_ for _, _, np_ in padded) // LANE
    s_all = s_all.reshape(N_DEV, nl, s_rows, LANE)
    s_parts = {k: a[..., :nn] for (k, nn, _), a in
               zip(padded, _unpack_rows(s_all, [(np_,) for _, _, np_ in padded]))}

    dmod_all = s_parts["dmod"]
    dmod_cols = lax.dynamic_slice_in_dim(dmod_all, me * ada_cols, ada_cols, axis=2).transpose(1, 0, 2)
    g_ada_w = _ada_bwd(c_all.T, dmod_cols, name="ada_bwd")
    gp = {}
    gp["ada_w"] = g_ada_w[None]
    gp["ada_b"] = dmod_all
    for k in ("norm_g", "q_lat_g", "kv_lat_g", "glu_b", "dw_b", "conv_ln_g", "conv_ln_b", "b_pw"):
        gp[k] = s_parts[k]
    for k in ("q_norm_g", "k_norm_g"):
        t = s_parts[k]
        gp[k] = jnp.concatenate([t[..., :NOPE], t[..., LANE:LANE + ROPE]], axis=-1)
    dw_g = s_parts["dw_w"].reshape(N_DEV, nl, HALO, D_CONV)[:, :, :CONV_K]
    gp["dw_w"] = lax.dynamic_slice_in_dim(dw_g, me * LANE, LANE, axis=3)

    res["ada_w"] = _adamw(gp["ada_w"], w_loc["ada_w"], m_loc["ada_w"], v_loc["ada_w"], name="adamw_ada_w")
    small_names = [k for k in names if k not in _BIG and k != "ada_w"]
    res.update(zip(small_names, _adamw_small([(gp[k], w_loc[k], m_loc[k], v_loc[k]) for k in small_names],
                                             name="adamw_small")))
    arrived[0][0] = _exchange_wait(*flying["l0_w_in"][:4], res["ada_w"][1], "scatter", name="scatter_wait_l0_w_in")[0]
    w_in_res = _adamw([arrived[l][0] for l in range(nl)], w_loc["w_in"], m_loc["w_in"], v_loc["w_in"],
                      name="adamw_w_in")
    res["w_in"] = tuple(tr(a) for a in w_in_res)
    out = [loss, gx[None]]
    for idx in range(4):
        out += [res[k][idx] for k in names]
    return tuple(out)
```

```python
import functools
import math

import jax
import jax.numpy as jnp
from jax import lax
from jax.experimental import pallas as pl
from jax.experimental.pallas import tpu as pltpu

F32 = jnp.float32
MXU_DTYPE = jnp.bfloat16
WIRE_DTYPE = jnp.bfloat16

D_MODEL = 2048
N_LAYERS = 2
N_DEV = 8
N_HEADS = 8
NOPE = 128
ROPE = 64
V_DIM = 128
QK_DIM = NOPE + ROPE
Q_LORA = 512
KV_LORA = 256
D_MLA = N_HEADS * V_DIM
D_CONV = 1024
CONV_K = 31
ROPE_THETA = 10000.0
EPS = 1e-6
LANE = 128
HEAD_PAD = 2 * LANE
HALO = 32

SEG_CI = (0, 2 * D_CONV)
SEG_MG = (2 * D_CONV, D_MLA)
SEG_CG = (2 * D_CONV + D_MLA, D_CONV)
SEG_QL = (2 * D_CONV + D_MLA + D_CONV, Q_LORA)
SEG_KVL = (SEG_QL[0] + Q_LORA, KV_LORA)
SEG_KR = (SEG_KVL[0] + KV_LORA, LANE)
SEG_LAT = (SEG_QL[0], 1024)
IN_PAD = SEG_LAT[0] + SEG_LAT[1]
IN_TILE = IN_PAD // 4
assert SEG_KR[0] + LANE <= IN_PAD and SEG_LAT[0] % SEG_LAT[1] == 0
IN_COLS = Q_LORA + KV_LORA + ROPE + D_MLA + 2 * D_CONV + D_CONV

ADAM_LR = 0.001
ADAM_B1 = 0.9
ADAM_B2 = 0.999
ADAM_EPS = 1e-08
ADAM_WD = 0.01
ADAM_STEP = 10

VMEM_LIMIT = 56 * 1024 * 1024
ATT_T = 512
ROW_T = 256
CONV_T = 128
MESH_ID = pl.DeviceIdType.MESH


def _cp(sem=None):
    kw = dict(vmem_limit_bytes=VMEM_LIMIT)
    if sem is not None:
        kw["dimension_semantics"] = sem
    return pltpu.CompilerParams(**kw)


def _sds(shape, dtype):
    return jax.ShapeDtypeStruct(shape, dtype)


def _silu(x):
    return x * jax.nn.sigmoid(x)


def _dsilu(x):
    s = jax.nn.sigmoid(x)
    return s * (1.0 + x * (1.0 - s))


def _rowspec(t, width, col=0):
    return pl.BlockSpec((t, width), lambda i: (i, col))


def _vecspec(width):
    return pl.BlockSpec((1, width), lambda i: (0, 0))


def _colsum(v):
    return jnp.sum(v, axis=0, keepdims=True)


def _mm(a, b, *, name, ta=False, tb=False, out_dtype=F32, tm=512, tn=512, tk=None, n_outer=False, after=None,
        residual=None):
    if ta:
        kdim, m = a.shape
    else:
        m, kdim = a.shape
    if tb:
        n, k2 = b.shape
    else:
        k2, n = b.shape
    assert kdim == k2, (a.shape, b.shape)
    tm, tn = min(tm, m), min(tn, n)
    tk = kdim if tk is None else min(tk, kdim)
    assert m % tm == 0 and n % tn == 0 and kdim % tk == 0, (m, n, kdim, tm, tn, tk)
    nk = kdim // tk
    dims = (((0 if ta else 1,), (1 if tb else 0,)), ((), ()))

    n_extra = 0 if after is None else 1
    assert residual is None or nk == 1

    def body(a_ref, b_ref, *rest):
        if residual is not None:
            x_ref, gate_ref = rest[:2]
            rest = rest[2:]
        o_ref, scratch = rest[n_extra], rest[n_extra + 1:]
        prod = lax.dot_general(a_ref[...].astype(MXU_DTYPE), b_ref[...].astype(MXU_DTYPE), dims,
                               preferred_element_type=F32)
        if residual is not None:
            o_ref[...] = prod.astype(o_ref.dtype)
            scratch[0][...] = x_ref[...] + gate_ref[...] * prod
        elif nk == 1:
            o_ref[...] = prod.astype(o_ref.dtype)
        else:
            acc = scratch[0]
            k = pl.program_id(2)

            @pl.when(k == 0)
            def _():
                acc[...] = prod

            @pl.when(k > 0)
            def _():
                acc[...] += prod

            @pl.when(k == nk - 1)
            def _():
                o_ref[...] = acc[...].astype(o_ref.dtype)

    if n_outer:
        ij = lambda g0, g1: (g1, g0)
        grid = (n // tn, m // tm, nk)
    else:
        ij = lambda g0, g1: (g0, g1)
        grid = (m // tm, n // tn, nk)

    def a_map(g0, g1, k):
        i, _ = ij(g0, g1)
        return (k, i) if ta else (i, k)

    def b_map(g0, g1, k):
        _, j = ij(g0, g1)
        return (j, k) if tb else (k, j)

    def o_map(g0, g1, k):
        return ij(g0, g1)

    in_specs = [pl.BlockSpec((tk, tm) if ta else (tm, tk), a_map), pl.BlockSpec((tn, tk) if tb else (tk, tn), b_map)]
    operands = [a, b]
    out_specs, out_shape = pl.BlockSpec((tm, tn), o_map), _sds((m, n), out_dtype)
    if residual is not None:
        in_specs += [pl.BlockSpec((tm, tn), o_map), pl.BlockSpec((1, tn), lambda g0, g1, k: (0, ij(g0, g1)[1]))]
        operands += list(residual)
        out_specs, out_shape = [out_specs, pl.BlockSpec((tm, tn), o_map)], [out_shape, _sds((m, n), F32)]
    if after is not None:
        in_specs.append(_ANY)
        operands.append(after)
    return pl.pallas_call(
        body, name=name, grid=grid, in_specs=in_specs, out_specs=out_specs, out_shape=out_shape,
        scratch_shapes=[pltpu.VMEM((tm, tn), F32)] if nk > 1 else [],
        compiler_params=_cp(("parallel", "parallel", "arbitrary")),
    )(*operands)


def _prenorm(x, g, shift, sc1p, *, name):
    s, d = x.shape
    t = min(2 * ROW_T, s)

    def body(x_ref, g_ref, sh_ref, sc_ref, h_ref):
        xv = x_ref[...]
        r = lax.rsqrt(jnp.mean(xv * xv, axis=-1, keepdims=True) + EPS)
        h_ref[...] = ((xv * r) * g_ref[...] * sc_ref[...] + sh_ref[...]).astype(h_ref.dtype)

    return pl.pallas_call(
        body, name=name, grid=(s // t,),
        in_specs=[_rowspec(t, d), _vecspec(d), _vecspec(d), _vecspec(d)],
        out_specs=_rowspec(t, d), out_shape=_sds((s, d), MXU_DTYPE),
        compiler_params=_cp(("parallel",)),
    )(x, g, shift, sc1p)


def _rope_fwd(r, c_t, s1_t, s2_t):
    return r * c_t + pltpu.roll(r, LANE - ROPE // 2, 1) * s1_t + pltpu.roll(r, ROPE // 2, 1) * s2_t


def _rope_bwd(d, c_t, s1_t, s2_t):
    return d * c_t + pltpu.roll(d * s1_t, ROPE // 2, 1) + pltpu.roll(d * s2_t, LANE - ROPE // 2, 1)


def _lanesum(v):
    return jnp.sum(v, axis=-1, keepdims=True)


def _mla_pre(z, w_q_up, w_kv_up, g_ql, g_kvl, c_t, s1_t, s2_t, gqn, gqr, gkn, gkr, *, name):
    s = z.shape[0]
    t = min(2 * ROW_T, s)
    scale = 1.0 / math.sqrt(QK_DIM)
    wide = 2 * N_HEADS * LANE

    def body(ql_ref, kvl_ref, kr_ref, wq_ref, wkv_ref, gq_ref, gk_ref, c_ref, s1_ref, s2_ref,
             gqn_ref, gqr_ref, gkn_ref, gkr_ref, qn_ref, kn_ref, q_ref, kv_ref, qf_ref, kf_ref, vf_ref):
        for src, g_ref, dst, w_ref, up in ((ql_ref, gq_ref, qn_ref, wq_ref, q_ref),
                                           (kvl_ref, gk_ref, kn_ref, wkv_ref, kv_ref)):
            v = src[...]
            r = lax.rsqrt(jnp.mean(v * v, axis=-1, keepdims=True) + EPS)
            dst[...] = ((v * r) * g_ref[...]).astype(dst.dtype)
            up[...] = jnp.dot(dst[...], w_ref[...], preferred_element_type=F32)
        c_v, s1_v, s2_v = c_ref[...], s1_ref[...], s2_ref[...]
        kr = kr_ref[...]
        kr_ss = _lanesum(kr * kr)
        for h in range(N_HEADS):
            n = q_ref[:, h * LANE:(h + 1) * LANE]
            r = q_ref[:, N_HEADS * LANE + h * LANE:N_HEADS * LANE + (h + 1) * LANE]
            rs = lax.rsqrt((_lanesum(n * n) + _lanesum(r * r)) * (1.0 / QK_DIM) + EPS)
            qf_ref[h, :, 0:LANE] = (((n * rs) * gqn_ref[...]) * scale).astype(qf_ref.dtype)
            rr = _rope_fwd((r * rs) * gqr_ref[...], c_v, s1_v, s2_v)
            qf_ref[h, :, LANE:HEAD_PAD] = (rr * scale).astype(qf_ref.dtype)

            n = kv_ref[:, h * 2 * LANE:h * 2 * LANE + LANE]
            rs = lax.rsqrt((_lanesum(n * n) + kr_ss) * (1.0 / QK_DIM) + EPS)
            kf_ref[h, :, 0:LANE] = ((n * rs) * gkn_ref[...]).astype(kf_ref.dtype)
            kf_ref[h, :, LANE:HEAD_PAD] = _rope_fwd((kr * rs) * gkr_ref[...], c_v, s1_v, s2_v).astype(kf_ref.dtype)
            vf_ref[h, :, 0:V_DIM] = kv_ref[:, h * 2 * LANE + LANE:(h + 1) * 2 * LANE].astype(vf_ref.dtype)
            vf_ref[h, :, V_DIM:] = jnp.ones((t, V_DIM), vf_ref.dtype)

    hspec = lambda w: pl.BlockSpec((N_HEADS, t, w), lambda i: (0, i, 0))
    whole = lambda a: pl.BlockSpec(a.shape, lambda i: (0, 0))
    return pl.pallas_call(
        body, name=name, grid=(s // t,),
        in_specs=[_rowspec(t, Q_LORA, SEG_QL[0] // Q_LORA), _rowspec(t, KV_LORA, SEG_KVL[0] // KV_LORA),
                  _rowspec(t, LANE, SEG_KR[0] // LANE), whole(w_q_up), whole(w_kv_up),
                  _vecspec(Q_LORA), _vecspec(KV_LORA),
                  _rowspec(t, LANE), _rowspec(t, LANE), _rowspec(t, LANE),
                  _vecspec(LANE), _vecspec(LANE), _vecspec(LANE), _vecspec(LANE)],
        out_specs=[_rowspec(t, Q_LORA), _rowspec(t, KV_LORA), _rowspec(t, wide), _rowspec(t, wide),
                   hspec(HEAD_PAD), hspec(HEAD_PAD), hspec(2 * V_DIM)],
        out_shape=[_sds((s, Q_LORA), MXU_DTYPE), _sds((s, KV_LORA), MXU_DTYPE), _sds((s, wide), F32),
                   _sds((s, wide), F32), _sds((N_HEADS, s, HEAD_PAD), MXU_DTYPE),
                   _sds((N_HEADS, s, HEAD_PAD), MXU_DTYPE), _sds((N_HEADS, s, 2 * V_DIM), MXU_DTYPE)],
        compiler_params=_cp(("parallel",)),
    )(z, z, z, w_q_up, w_kv_up, g_ql, g_kvl, c_t, s1_t, s2_t, gqn, gqr, gkn, gkr)


def _causal_mask(t):
    row = lax.broadcasted_iota(jnp.int32, (t, t), 0)
    col = lax.broadcasted_iota(jnp.int32, (t, t), 1)
    return col <= row


NEG = -1e30


def _flash_fwd(qf, kf, va, *, name):
    nh, s, dk = qf.shape
    dv = va.shape[-1] // 2
    t = min(ATT_T, s)
    n = s // t
    assert dv == LANE and t % LANE == 0

    def body(q_ref, k_ref, v_ref, o_ref, lse_ref, m_s, acc_s, s_buf):
        i = pl.program_id(1)
        m_s[...] = jnp.full(m_s.shape, NEG, F32)
        acc_s[...] = jnp.zeros(acc_s.shape, F32)

        def rows_of(j):
            return pl.ds(pl.multiple_of(j * t, t), t)

        def scores(qi, j):
            return lax.dot_general(q_ref[0, rows_of(qi), :], k_ref[0, rows_of(j), :], (((1,), (1,)), ((), ())),
                                   preferred_element_type=F32)

        def consume(j, slot, masked):
            sc = s_buf[slot]
            if masked:
                sc = jnp.where(_causal_mask(t), sc, NEG)
            m_prev = m_s[...]
            m_new = jnp.maximum(m_prev, jnp.max(sc, axis=-1, keepdims=True))
            alpha = jnp.exp(m_prev - m_new)
            p = jnp.exp(sc - jnp.tile(m_new, (1, t // LANE)))
            acc_s[...] = jnp.tile(alpha, (1, 2)) * acc_s[...] + jnp.dot(
                p.astype(MXU_DTYPE), v_ref[0, rows_of(j), :], preferred_element_type=F32)
            m_s[...] = m_new

        nxt = jnp.minimum(i + 1, n - 1)

        @pl.when(i == 0)
        def _():
            s_buf[2] = scores(0, 0)
            consume(0, 2, True)
            s_buf[2] = scores(nxt, 0)

        @pl.when(i > 0)
        def _():
            s_buf[1] = scores(i, 1)
            consume(0, 2, False)

            def pair(a, carry):
                s_buf[0] = scores(i, 2 * a + 2)
                consume(2 * a + 1, 1, False)
                s_buf[1] = scores(i, 2 * a + 3)
                consume(2 * a + 2, 0, False)
                return carry

            lax.fori_loop(0, (i - 1) // 2, pair, 0)

            @pl.when(i % 2 == 1)
            def _():
                s_buf[2] = scores(nxt, 0)
                consume(i, 1, True)

            @pl.when(i % 2 == 0)
            def _():
                s_buf[0] = scores(i, i)
                consume(i - 1, 1, False)
                s_buf[2] = scores(nxt, 0)
                consume(i, 0, True)

        den = acc_s[:, dv:]
        o_ref[...] = acc_s[:, :dv] / den
        lse_ref[0] = m_s[...] + jnp.log(den)

    head = lambda h, i: (h, 0, 0)
    return pl.pallas_call(
        body, name=name, grid=(nh, n),
        in_specs=[pl.BlockSpec((1, s, dk), head), pl.BlockSpec((1, s, dk), head), pl.BlockSpec((1, s, 2 * dv), head)],
        out_specs=[pl.BlockSpec((t, dv), lambda h, i: (i, h)),
                   pl.BlockSpec((1, t, LANE), lambda h, i: (h, i, 0))],
        out_shape=[_sds((s, nh * dv), F32), _sds((nh, s, LANE), F32)],
        scratch_shapes=[pltpu.VMEM((t, LANE), F32), pltpu.VMEM((t, 2 * dv), F32), pltpu.VMEM((3, t, t), F32)],
        compiler_params=_cp(("arbitrary", "arbitrary")),
    )(qf, kf, va)


def _shifted_copies(ext_ref):
    rows = ext_ref.shape[1] - 8
    for s in range(1, 8):
        ext_ref[s, 0:rows, :] = ext_ref[0, s:s + rows, :]


def _windows(ext_ref, offsets, t_rows, lane0, lanes):
    for s in range(8):
        group = [o for o in offsets if o % 8 == s]
        if not group:
            continue
        lo, hi = min(group) - s, max(group) - s
        wide = ext_ref[s, pl.ds(lo, hi - lo + t_rows), lane0:lane0 + lanes]
        for o in group:
            yield o, wide[o - s - lo:o - s - lo + t_rows]


def _dw_taps(ext_ref, w_ref, row0, t_rows, lane0, lanes, first_off):
    acc = None
    for off, win in _windows(ext_ref, [row0 + first_off + k for k in range(CONV_K)], t_rows, lane0, lanes):
        k = off - row0 - first_off
        term = w_ref[k:k + 1, lane0:lane0 + lanes] * win
        acc = term if acc is None else acc + term
    return acc


CONV_RC = 32
CONV_LC = 256


def _conv_fwd(z, glu_b, dw_w, dw_b, ln_g, ln_b, w_pw, *, name):
    s = z.shape[0]
    t = min(CONV_T, s)
    c2 = 2 * D_CONV
    hb = t // HALO

    def body(zm_ref, zh_ref, gb_ref, w_ref, wb_ref, g_ref, b_ref, wpw_ref, u1_ref, u3_ref, u4_ref, ext):
        i = pl.program_id(0)

        def glu(zv):
            ci = zv + gb_ref[...]
            return ci[:, :D_CONV] * jax.nn.sigmoid(ci[:, D_CONV:])

        ext[0, HALO:, :] = glu(zm_ref[...])
        ext[0, 0:HALO, :] = jnp.where(i > 0, glu(zh_ref[...]), 0.0)
        _shifted_copies(ext)
        for rc in range(0, t, CONV_RC):
            for lc in range(0, D_CONV, CONV_LC):
                acc = _dw_taps(ext, w_ref, rc, CONV_RC, lc, CONV_LC, HALO - (CONV_K - 1))
                u1_ref[rc:rc + CONV_RC, lc:lc + CONV_LC] = acc + wb_ref[:, lc:lc + CONV_LC]
        u1 = u1_ref[...]
        mu = jnp.mean(u1, axis=-1, keepdims=True)
        cen = u1 - mu
        var = jnp.mean(cen * cen, axis=-1, keepdims=True)
        u2 = (cen * lax.rsqrt(var + EPS)) * g_ref[...] + b_ref[...]
        u3_ref[...] = _silu(u2).astype(u3_ref.dtype)
        u4_ref[...] = jnp.dot(u3_ref[...], wpw_ref[...], preferred_element_type=F32)

    return pl.pallas_call(
        body, name=name, grid=(s // t,),
        in_specs=[_rowspec(t, c2), pl.BlockSpec((HALO, c2), lambda i: (jnp.maximum(i * hb - 1, 0), 0)),
                  _vecspec(c2), pl.BlockSpec((HALO, D_CONV), lambda i: (0, 0)), _vecspec(D_CONV),
                  _vecspec(D_CONV), _vecspec(D_CONV), pl.BlockSpec((D_CONV, D_CONV), lambda i: (0, 0))],
        out_specs=[_rowspec(t, D_CONV), _rowspec(t, D_CONV), _rowspec(t, D_CONV)],
        out_shape=[_sds((s, D_CONV), F32), _sds((s, D_CONV), MXU_DTYPE), _sds((s, D_CONV), F32)],
        scratch_shapes=[pltpu.VMEM((8, t + HALO, D_CONV), F32)],
        compiler_params=_cp(("parallel",)),
    )(z, z, glu_b, dw_w, dw_b, ln_g, ln_b, w_pw)


def _gate_cat(o, z, u4m, b_pw, *, name):
    s = o.shape[0]
    t = min(2 * ROW_T, s)

    def body(o_ref, mg_ref, u4_ref, cg_ref, b_ref, cat_ref):
        cat_ref[:, :D_MLA] = (o_ref[...] * _silu(mg_ref[...])).astype(cat_ref.dtype)
        cat_ref[:, D_MLA:] = ((u4_ref[...] + b_ref[...]) * _silu(cg_ref[...])).astype(cat_ref.dtype)

    return pl.pallas_call(
        body, name=name, grid=(s // t,),
        in_specs=[_rowspec(t, D_MLA), _rowspec(t, D_MLA, SEG_MG[0] // D_MLA), _rowspec(t, D_CONV),
                  _rowspec(t, D_CONV, SEG_CG[0] // D_CONV), _vecspec(D_CONV)],
        out_specs=_rowspec(t, D_MLA + D_CONV), out_shape=_sds((s, D_MLA + D_CONV), MXU_DTYPE),
        compiler_params=_cp(("parallel",)),
    )(o, z, u4m, z, b_pw)


def _gated_residual_bwd(gx, y_ref, gate_ref, dy_ref, dgate_ref):
    dy_ref[...] = (gx * gate_ref[...]).astype(dy_ref.dtype)
    dgate_ref[...] += _colsum(gx * y_ref[...])


def _loss_head(xf, target, y, gate, *, name):
    s, d = xf.shape
    t = min(2 * ROW_T, s)

    def body(x_ref, t_ref, y_ref, gate_ref, gx_ref, loss_ref, dy_ref, dgate_ref):
        @pl.when(pl.program_id(0) == 0)
        def _():
            loss_ref[...] = jnp.zeros(loss_ref.shape, F32)
            dgate_ref[...] = jnp.zeros(dgate_ref.shape, F32)

        err = x_ref[...] - t_ref[...]
        gx = err * (1.0 / d)
        gx_ref[...] = gx
        loss_ref[...] += 0.5 * jnp.sum(_lanesum(err * err) * (1.0 / d), axis=0, keepdims=True)
        _gated_residual_bwd(gx, y_ref, gate_ref, dy_ref, dgate_ref)

    return pl.pallas_call(
        body, name=name, grid=(s // t,),
        in_specs=[_rowspec(t, d), _rowspec(t, d), _rowspec(t, d), _vecspec(d)],
        out_specs=[_rowspec(t, d), pl.BlockSpec((1, 1), lambda i: (0, 0)), _rowspec(t, d), _vecspec(d)],
        out_shape=[_sds((s, d), F32), _sds((1, 1), F32), _sds((s, d), MXU_DTYPE), _sds((1, d), F32)],
        compiler_params=_cp(("arbitrary",)),
    )(xf, target, y, gate)


def _acc_init(refs):
    @pl.when(pl.program_id(0) == 0)
    def _():
        for r in refs:
            r[...] = jnp.zeros(r.shape, r.dtype)


def _gate_bwd(dy, w_out, o, z, u4m, b_pw, *, name):
    s, d = dy.shape
    t = min(2 * ROW_T, s)
    gates = D_MLA + D_CONV
    assert SEG_CG[0] == SEG_MG[0] + D_MLA and SEG_MG[0] % gates == 0

    def body(dy_ref, w_ref, o_ref, mg_ref, u4_ref, cg_ref, b_ref,
             do_ref, delta_ref, du4_ref, gb_ref, dz_ref):
        _acc_init([gb_ref])
        dcat = lax.dot_general(dy_ref[...], w_ref[...], (((1,), (1,)), ((), ())), preferred_element_type=F32)
        dm, ov, mg = dcat[:, :D_MLA], o_ref[...], mg_ref[...]
        do = dm * _silu(mg)
        do_ref[...] = do.astype(do_ref.dtype)
        dz_ref[:, :D_MLA] = (dm * ov * _dsilu(mg)).astype(dz_ref.dtype)
        prod = do * ov
        for h in range(N_HEADS):
            delta_ref[h] = _lanesum(prod[:, h * V_DIM:(h + 1) * V_DIM])
        dc, cg = dcat[:, D_MLA:], cg_ref[...]
        du4 = dc * _silu(cg)
        du4_ref[...] = du4.astype(du4_ref.dtype)
        dz_ref[:, D_MLA:] = (dc * (u4_ref[...] + b_ref[...]) * _dsilu(cg)).astype(dz_ref.dtype)
        gb_ref[...] += _colsum(du4)

    return pl.pallas_call(
        body, name=name, grid=(s // t,),
        in_specs=[_rowspec(t, d), pl.BlockSpec((gates, d), lambda i: (0, 0)), _rowspec(t, D_MLA),
                  _rowspec(t, D_MLA, SEG_MG[0] // D_MLA), _rowspec(t, D_CONV),
                  _rowspec(t, D_CONV, SEG_CG[0] // D_CONV), _vecspec(D_CONV)],
        out_specs=[_rowspec(t, D_MLA), pl.BlockSpec((N_HEADS, t, 1), lambda i: (0, i, 0)),
                   _rowspec(t, D_CONV), _vecspec(D_CONV), _rowspec(t, gates, SEG_MG[0] // gates)],
        out_shape=[_sds((s, D_MLA), MXU_DTYPE), _sds((N_HEADS, s, 1), F32),
                   _sds((s, D_CONV), MXU_DTYPE), _sds((1, D_CONV), F32), _sds((s, IN_PAD), MXU_DTYPE)],
        compiler_params=_cp(("arbitrary",)),
    )(dy, w_out, o, z, u4m, z, b_pw)


def _conv_bwd(du3, u1, z, dz, glu_b, dw_w, ln_g, ln_b, *, name):
    s = z.shape[0]
    t = min(CONV_T, s)
    c2 = 2 * D_CONV
    hb = t // HALO
    n_blk = s // t
    last_halo = s // HALO - 1

    def body(d3m_ref, d3h_ref, u1m_ref, u1h_ref, zm_ref, zh_ref, gb_ref, w_ref, g_ref, b_ref, dz_in_ref,
             dci_ref, gg_ref, gbn_ref, gwb_ref, ggb_ref, gw_ref, dext, uext, du0_s, gw_acc):
        i = pl.program_id(0)
        _acc_init([gg_ref, gbn_ref, gwb_ref, ggb_ref, gw_acc])

        def ln_bwd(d3, u1v):
            mu = jnp.mean(u1v, axis=-1, keepdims=True)
            cen = u1v - mu
            rstd = lax.rsqrt(jnp.mean(cen * cen, axis=-1, keepdims=True) + EPS)
            uh = cen * rstd
            d2 = d3 * _dsilu(uh * g_ref[...] + b_ref[...])
            dh = d2 * g_ref[...]
            d1 = rstd * (dh - jnp.mean(dh, axis=-1, keepdims=True) - uh * jnp.mean(dh * uh, axis=-1, keepdims=True))
            return d1, d2, uh

        d1, d2, uh = ln_bwd(d3m_ref[...], u1m_ref[...])
        gg_ref[...] += _colsum(d2 * uh)
        gbn_ref[...] += _colsum(d2)
        gwb_ref[...] += _colsum(d1)
        dext[0, 0:t, :] = d1
        d1h, _, _ = ln_bwd(d3h_ref[...], u1h_ref[...])
        dext[0, t:, :] = jnp.where(i < n_blk - 1, d1h, 0.0)
        _shifted_copies(dext)

        def glu_parts(zv):
            ci = zv + gb_ref[...]
            return ci[:, :D_CONV], jax.nn.sigmoid(ci[:, D_CONV:])

        val, sg = glu_parts(zm_ref[...])
        uext[0, HALO:, :] = val * sg
        valh, sgh = glu_parts(zh_ref[...])
        uext[0, 0:HALO, :] = jnp.where(i > 0, valh * sgh, 0.0)
        _shifted_copies(uext)

        for rc in range(0, t, CONV_RC):
            for lc in range(0, D_CONV, CONV_LC):
                acc = None
                for off, win in _windows(dext, [rc + k for k in range(CONV_K)], CONV_RC, lc, CONV_LC):
                    k = (CONV_K - 1) - (off - rc)
                    term = w_ref[k:k + 1, lc:lc + CONV_LC] * win
                    acc = term if acc is None else acc + term
                du0_s[rc:rc + CONV_RC, lc:lc + CONV_LC] = acc
                dchunk = dext[0, rc:rc + CONV_RC, lc:lc + CONV_LC]
                first = rc + HALO - (CONV_K - 1)
                for off, win in _windows(uext, [first + k for k in range(CONV_K)], CONV_RC, lc, CONV_LC):
                    k = off - first
                    pr = dchunk * win
                    part = pr[0:8]
                    for r8 in range(8, CONV_RC, 8):
                        part = part + pr[r8:r8 + 8]
                    gw_acc[k, :, lc:lc + CONV_LC] += part

        du0 = du0_s[...]
        dval = du0 * sg
        dgt = du0 * val * sg * (1.0 - sg)
        dci_ref[:, :D_CONV] = dval.astype(dci_ref.dtype)
        dci_ref[:, D_CONV:] = dgt.astype(dci_ref.dtype)
        ggb_ref[:, :D_CONV] += _colsum(dval)
        ggb_ref[:, D_CONV:] += _colsum(dgt)

        @pl.when(i == n_blk - 1)
        def _():
            gw_ref[...] = jnp.sum(gw_acc[...], axis=1)

    halo_next = lambda w: pl.BlockSpec((HALO, w), lambda i: (jnp.minimum((i + 1) * hb, last_halo), 0))
    return pl.pallas_call(
        body, name=name, grid=(n_blk,),
        in_specs=[_rowspec(t, D_CONV), halo_next(D_CONV), _rowspec(t, D_CONV), halo_next(D_CONV),
                  _rowspec(t, c2), pl.BlockSpec((HALO, c2), lambda i: (jnp.maximum(i * hb - 1, 0), 0)),
                  _vecspec(c2), pl.BlockSpec((HALO, D_CONV), lambda i: (0, 0)), _vecspec(D_CONV), _vecspec(D_CONV),
                  _ANY],
        out_specs=[_rowspec(t, c2, SEG_CI[0] // c2), _vecspec(D_CONV), _vecspec(D_CONV), _vecspec(D_CONV),
                   _vecspec(c2), pl.BlockSpec((HALO, D_CONV), lambda i: (0, 0))],
        out_shape=[_sds(dz.shape, dz.dtype), _sds((1, D_CONV), F32), _sds((1, D_CONV), F32), _sds((1, D_CONV), F32),
                   _sds((1, c2), F32), _sds((HALO, D_CONV), F32)],
        scratch_shapes=[pltpu.VMEM((8, t + HALO, D_CONV), F32), pltpu.VMEM((8, t + HALO, D_CONV), F32),
                        pltpu.VMEM((t, D_CONV), F32), pltpu.VMEM((HALO, 8, D_CONV), F32)],
        input_output_aliases={10: 0},
        compiler_params=_cp(("arbitrary",)),
    )(du3, du3, u1, u1, z, z, glu_b, dw_w, ln_g, ln_b, dz)


def _flash_bwd(qf, kf, va, do, lse_t, delta_t, *, name):
    nh, s, dk = qf.shape
    dv = va.shape[-1] // 2
    t = min(ATT_T, s)
    n = s // t
    nt = (((1,), (1,)), ((), ()))
    tn = (((0,), (0,)), ((), ()))

    def body(q_ref, do_ref, lse_ref, dl_ref, k_ref, v_ref, dq_ref, dk_ref, dv_ref,
             dk_s, dv_s, st_buf, dpt_buf):
        n_un = pl.program_id(1)
        j = n - 1 - n_un
        nxt = jnp.maximum(j - 1, 0)

        @pl.when(n_un == 0)
        def _():
            dq_ref[...] = jnp.zeros(dq_ref.shape, F32)

        dk_s[...] = jnp.zeros(dk_s.shape, F32)
        dv_s[...] = jnp.zeros(dv_s.shape, F32)

        def rows_at(blk):
            return pl.ds(pl.multiple_of(blk * t, t), t)

        def rows_of(b):
            return rows_at(n - 1 - b)

        k = k_ref[0, rows_at(j), :]

        def produce(kj, b, slot):
            rows = rows_of(b)
            st_buf[slot] = lax.dot_general(k_ref[0, rows_at(kj), :], q_ref[0, rows, :], nt,
                                           preferred_element_type=F32)
            dpt_buf[slot] = lax.dot_general(v_ref[0, rows_at(kj), 0:dv], do_ref[rows, :], nt,
                                            preferred_element_type=F32)

        def consume(b, slot, masked):
            i = n - 1 - b
            rows = rows_of(b)
            q, dov = q_ref[0, rows, :], do_ref[rows, :]
            pt = jnp.exp(st_buf[slot] - lse_ref[0, i])
            if masked:
                key = lax.broadcasted_iota(jnp.int32, (t, t), 0)
                qry = lax.broadcasted_iota(jnp.int32, (t, t), 1)
                pt = jnp.where(key <= qry, pt, 0.0)
            dv_s[...] += jnp.dot(pt.astype(MXU_DTYPE), dov, preferred_element_type=F32)
            dst = (pt * (dpt_buf[slot] - dl_ref[0, i])).astype(MXU_DTYPE)
            dk_s[...] += jnp.dot(dst, q, preferred_element_type=F32)
            dq_ref[0, rows, :] += lax.dot_general(dst, k, tn, preferred_element_type=F32)

        @pl.when(n_un == 0)
        def _():
            produce(j, 0, 2)
            consume(0, 2, True)
            produce(nxt, 0, 2)

        @pl.when(n_un > 0)
        def _():
            produce(j, 1, 1)
            consume(0, 2, False)

            def pair(a, carry):
                produce(j, 2 * a + 2, 0)
                consume(2 * a + 1, 1, False)
                produce(j, 2 * a + 3, 1)
                consume(2 * a + 2, 0, False)
                return carry

            lax.fori_loop(0, (n_un - 1) // 2, pair, 0)

            @pl.when(n_un % 2 == 1)
            def _():
                produce(nxt, 0, 2)
                consume(n_un, 1, True)

            @pl.when(n_un % 2 == 0)
            def _():
                produce(j, n_un, 0)
                consume(n_un - 1, 1, False)
                produce(nxt, 0, 2)
                consume(n_un, 0, True)

        dk_ref[0] = dk_s[...]
        dv_ref[0] = dv_s[...]

    head = lambda h, j: (h, 0, 0)
    rowv = pl.BlockSpec((1, n, 1, t), lambda h, j: (h, 0, 0, 0))
    return pl.pallas_call(
        body, name=name, grid=(nh, n),
        in_specs=[pl.BlockSpec((1, s, dk), head),
                  pl.BlockSpec((s, dv), lambda h, j: (0, h)),
                  rowv, rowv,
                  pl.BlockSpec((1, s, dk), head),
                  pl.BlockSpec((1, s, 2 * dv), head)],
        out_specs=[pl.BlockSpec((1, s, dk), head),
                   pl.BlockSpec((1, t, dk), lambda h, g: (h, n - 1 - g, 0)),
                   pl.BlockSpec((1, t, dv), lambda h, g: (h, n - 1 - g, 0))],
        out_shape=[_sds((nh, s, dk), F32), _sds((nh, s, dk), F32), _sds((nh, s, dv), F32)],
        scratch_shapes=[pltpu.VMEM((t, dk), F32), pltpu.VMEM((t, dv), F32),
                        pltpu.VMEM((3, t, t), F32), pltpu.VMEM((3, t, t), F32)],
        compiler_params=_cp(("arbitrary", "arbitrary")),
    )(qf, do, lse_t, delta_t, kf, va)


def _mla_bwd(dqf, dkf, dvf, q_raw, kv, z, dz, qn, kn, w_q_up, w_kv_up, g_ql, g_kvl, c_t, s1_t, s2_t,
             gqn, gqr, gkn, gkr, *, name):
    s = q_raw.shape[0]
    t = min(ROW_T, s)
    scale = 1.0 / math.sqrt(QK_DIM)
    o_ql, o_kvl, o_kr = (seg[0] - SEG_LAT[0] for seg in (SEG_QL, SEG_KVL, SEG_KR))
    tn = (((0,), (0,)), ((), ()))
    nt = (((1,), (1,)), ((), ()))

    def body(dq_ref, dk_ref, dv_ref, q_ref, kv_ref, kr_ref, ql_ref, kvl_ref, qn_ref, kn_ref, wq_ref, wkv_ref,
             gq_ref, gk_ref, c_ref, s1_ref, s2_ref, gqn_ref, gqr_ref, gkn_ref, gkr_ref, dz_in_ref,
             dz_ref, gwq_ref, gwkv_ref, ggq_ref, ggk_ref, gql_ref, gkvl_ref, dqr_ref, dkv_ref):
        _acc_init([gwq_ref, gwkv_ref, ggq_ref, ggk_ref, gql_ref, gkvl_ref])
        c_v, s1_v, s2_v = c_ref[...], s1_ref[...], s2_ref[...]
        kr = kr_ref[...]
        kr_ss = _lanesum(kr * kr)
        dkr = jnp.zeros(kr.shape, F32)
        ggq_n = ggq_r = ggk_n = ggk_r = jnp.zeros((1, LANE), F32)

        def norm_bwd(n, r, rs, dyn, dyr, gn, gr):
            nh_, rh_ = n * rs, r * rs
            dnh, drh = dyn * gn, dyr * gr
            dot = (_lanesum(dnh * nh_) + _lanesum(drh * rh_)) * (1.0 / QK_DIM)
            return rs * (dnh - nh_ * dot), rs * (drh - rh_ * dot), _colsum(dyn * nh_), _colsum(dyr * rh_)

        for h in range(N_HEADS):
            n = q_ref[:, h * LANE:(h + 1) * LANE]
            r = q_ref[:, N_HEADS * LANE + h * LANE:N_HEADS * LANE + (h + 1) * LANE]
            rs = lax.rsqrt((_lanesum(n * n) + _lanesum(r * r)) * (1.0 / QK_DIM) + EPS)
            dyn = dq_ref[h, :, 0:LANE] * scale
            dyr = _rope_bwd(dq_ref[h, :, LANE:HEAD_PAD] * scale, c_v, s1_v, s2_v)
            dn, dr, g_n, g_r = norm_bwd(n, r, rs, dyn, dyr, gqn_ref[...], gqr_ref[...])
            dqr_ref[:, h * LANE:(h + 1) * LANE] = dn.astype(dqr_ref.dtype)
            dqr_ref[:, N_HEADS * LANE + h * LANE:N_HEADS * LANE + (h + 1) * LANE] = dr.astype(dqr_ref.dtype)
            ggq_n, ggq_r = ggq_n + g_n, ggq_r + g_r

            n = kv_ref[:, h * 2 * LANE:h * 2 * LANE + LANE]
            rs = lax.rsqrt((_lanesum(n * n) + kr_ss) * (1.0 / QK_DIM) + EPS)
            dyn = dk_ref[h, :, 0:LANE]
            dyr = _rope_bwd(dk_ref[h, :, LANE:HEAD_PAD], c_v, s1_v, s2_v)
            dn, dr, g_n, g_r = norm_bwd(n, kr, rs, dyn, dyr, gkn_ref[...], gkr_ref[...])
            dkv_ref[:, h * 2 * LANE:h * 2 * LANE + LANE] = dn.astype(dkv_ref.dtype)
            dkv_ref[:, h * 2 * LANE + LANE:(h + 1) * 2 * LANE] = dv_ref[h].astype(dkv_ref.dtype)
            dkr = dkr + dr
            ggk_n, ggk_r = ggk_n + g_n, ggk_r + g_r

        ggq_ref[:, 0:LANE] += ggq_n
        ggq_ref[:, LANE:] += ggq_r
        ggk_ref[:, 0:LANE] += ggk_n
        ggk_ref[:, LANE:] += ggk_r

        for d_ref, x_ref, w_ref, gw_ref, src, g_ref, off, gg_ref in (
                (dqr_ref, qn_ref, wq_ref, gwq_ref, ql_ref, gq_ref, o_ql, gql_ref),
                (dkv_ref, kn_ref, wkv_ref, gwkv_ref, kvl_ref, gk_ref, o_kvl, gkvl_ref)):
            dup = d_ref[...]
            gw_ref[...] += lax.dot_general(x_ref[...], dup, tn, preferred_element_type=F32)
            dy = lax.dot_general(dup, w_ref[...], nt, preferred_element_type=F32)
            v = src[...]
            r = lax.rsqrt(jnp.mean(v * v, axis=-1, keepdims=True) + EPS)
            vh = v * r
            dvh = dy * g_ref[...]
            dz_ref[:, off:off + v.shape[1]] = (
                r * (dvh - vh * jnp.mean(dvh * vh, axis=-1, keepdims=True))).astype(dz_ref.dtype)
            gg_ref[...] += _colsum(dy * vh)
        dz_ref[:, o_kr:o_kr + LANE] = dkr.astype(dz_ref.dtype)
        dz_ref[:, o_kr + LANE:] = jnp.zeros((t, SEG_LAT[1] - o_kr - LANE), dz_ref.dtype)

    hspec = lambda w: pl.BlockSpec((N_HEADS, t, w), lambda i: (0, i, 0))
    whole = lambda a: pl.BlockSpec(a.shape, lambda i: (0, 0))
    wide = 2 * N_HEADS * LANE
    return pl.pallas_call(
        body, name=name, grid=(s // t,),
        in_specs=[hspec(HEAD_PAD), hspec(HEAD_PAD), hspec(V_DIM), _rowspec(t, wide), _rowspec(t, wide),
                  _rowspec(t, LANE, SEG_KR[0] // LANE), _rowspec(t, Q_LORA, SEG_QL[0] // Q_LORA),
                  _rowspec(t, KV_LORA, SEG_KVL[0] // KV_LORA), _rowspec(t, Q_LORA), _rowspec(t, KV_LORA),
                  whole(w_q_up), whole(w_kv_up), _vecspec(Q_LORA), _vecspec(KV_LORA),
                  _rowspec(t, LANE), _rowspec(t, LANE), _rowspec(t, LANE),
                  _vecspec(LANE), _vecspec(LANE), _vecspec(LANE), _vecspec(LANE), _ANY],
        out_specs=[_rowspec(t, SEG_LAT[1], SEG_LAT[0] // SEG_LAT[1]), whole(w_q_up), whole(w_kv_up),
                   _vecspec(2 * LANE), _vecspec(2 * LANE), _vecspec(Q_LORA), _vecspec(KV_LORA)],
        out_shape=[_sds(dz.shape, dz.dtype), _sds(w_q_up.shape, F32), _sds(w_kv_up.shape, F32),
                   _sds((1, 2 * LANE), F32), _sds((1, 2 * LANE), F32), _sds((1, Q_LORA), F32),
                   _sds((1, KV_LORA), F32)],
        scratch_shapes=[pltpu.VMEM((t, wide), MXU_DTYPE), pltpu.VMEM((t, wide), MXU_DTYPE)],
        input_output_aliases={21: 0},
        compiler_params=_cp(("arbitrary",)),
    )(dqf, dkf, dvf, q_raw, kv, z, z, z, qn, kn, w_q_up, w_kv_up, g_ql, g_kvl, c_t, s1_t, s2_t,
      gqn, gqr, gkn, gkr, dz)


def _prenorm_bwd(dh, x, gxo, g, sc1p, below=None, *, name):
    s, d = x.shape
    t = min(2 * ROW_T if below is None else ROW_T, s)
    nb = 0 if below is None else 2

    def body(dh_ref, x_ref, gx_ref, g_ref, sc_ref, *rest):
        dx_ref, dsh_ref, dsc_ref, gg_ref = rest[nb:nb + 4]
        _acc_init([dsh_ref, dsc_ref, gg_ref])
        xv, dhv = x_ref[...], dh_ref[...]
        r = lax.rsqrt(jnp.mean(xv * xv, axis=-1, keepdims=True) + EPS)
        xn = xv * r
        dsh_ref[...] += _colsum(dhv)
        dsc_ref[...] += _colsum(dhv * (xn * g_ref[...]))
        dm = dhv * sc_ref[...]
        gg_ref[...] += _colsum(dm * xn)
        dxn = dm * g_ref[...]
        dx = gx_ref[...] + r * (dxn - xn * jnp.mean(dxn * xn, axis=-1, keepdims=True))
        dx_ref[...] = dx
        if below is not None:
            _acc_init([rest[nb + 5]])
            _gated_residual_bwd(dx, rest[0], rest[1], rest[nb + 4], rest[nb + 5])

    vec_out = [_vecspec(d), _vecspec(d), _vecspec(d)]
    vec_shape = [_sds((1, d), F32)] * 3
    return pl.pallas_call(
        body, name=name, grid=(s // t,),
        in_specs=[_rowspec(t, d), _rowspec(t, d), _rowspec(t, d), _vecspec(d), _vecspec(d)]
        + ([_rowspec(t, d), _vecspec(d)] if below is not None else []),
        out_specs=[_rowspec(t, d)] + vec_out + ([_rowspec(t, d), _vecspec(d)] if below is not None else []),
        out_shape=[_sds((s, d), F32)] + vec_shape
        + ([_sds((s, d), MXU_DTYPE), _sds((1, d), F32)] if below is not None else []),
        compiler_params=_cp(("arbitrary",)),
    )(dh, x, gxo, g, sc1p, *(below if below is not None else ()))


def _ada_fwd(c_all, ada_w, ada_b_cols, *, name):
    nl, d, cols = ada_w.shape

    def body(c_ref, w_ref, b_ref, o_ref):
        ca = _silu(c_ref[...]).astype(MXU_DTYPE)
        o_ref[0] = jnp.dot(ca, w_ref[0].astype(MXU_DTYPE), preferred_element_type=F32) + b_ref[0]

    return pl.pallas_call(
        body, name=name, grid=(nl,),
        in_specs=[pl.BlockSpec((N_DEV, d), lambda l: (0, 0)), pl.BlockSpec((1, d, cols), lambda l: (l, 0, 0)),
                  pl.BlockSpec((1, 1, cols), lambda l: (l, 0, 0))],
        out_specs=pl.BlockSpec((1, N_DEV, cols), lambda l: (l, 0, 0)),
        out_shape=_sds((nl, N_DEV, cols), F32),
        compiler_params=_cp(("parallel",)),
    )(c_all, ada_w, ada_b_cols)


def _ada_bwd(c_all_t, dmod_cols, *, name):
    nl, _, cols = dmod_cols.shape
    d = c_all_t.shape[0]

    def body(c_ref, dm_ref, o_ref):
        ca = _silu(c_ref[...]).astype(MXU_DTYPE)
        o_ref[0] = jnp.dot(ca, dm_ref[0].astype(MXU_DTYPE), preferred_element_type=F32)

    return pl.pallas_call(
        body, name=name, grid=(nl,),
        in_specs=[pl.BlockSpec((d, N_DEV), lambda l: (0, 0)), pl.BlockSpec((1, N_DEV, cols), lambda l: (l, 0, 0))],
        out_specs=pl.BlockSpec((1, d, cols), lambda l: (l, 0, 0)),
        out_shape=_sds((nl, d, cols), F32),
        compiler_params=_cp(("parallel",)),
    )(c_all_t, dmod_cols)


def _adamw_math(g, w, m, v):
    mn = ADAM_B1 * m + (1.0 - ADAM_B1) * g
    vn = ADAM_B2 * v + (1.0 - ADAM_B2) * (g * g)
    m_hat = mn / (1.0 - ADAM_B1 ** ADAM_STEP)
    v_hat = vn / (1.0 - ADAM_B2 ** ADAM_STEP)
    return -ADAM_LR * (m_hat / (jnp.sqrt(v_hat) + ADAM_EPS) + ADAM_WD * w), mn, vn


def _adamw_small(items, *, name):
    n = len(items)
    shapes = [it[1].shape for it in items]
    flat = lambda a, lead: a.reshape(lead + (-1, a.shape[-1]))
    operands = []
    for gp, w, m, v in items:
        operands += [flat(gp, (gp.shape[0],)), flat(w, ()), flat(m, ()), flat(v, ())]
    nparts = [it[0].shape[0] for it in items]

    def body(*refs):
        ins, outs = refs[:4 * n], refs[4 * n:]
        for i in range(n):
            g_ref, w_ref, m_ref, v_ref = ins[4 * i:4 * i + 4]
            g = g_ref[0].astype(F32)
            for p in range(1, nparts[i]):
                g = g + g_ref[p].astype(F32)
            outs[4 * i][...] = g
            outs[4 * i + 1][...], outs[4 * i + 2][...], outs[4 * i + 3][...] = _adamw_math(
                g, w_ref[...], m_ref[...], v_ref[...])

    out_shape = []
    for it in items:
        out_shape += [_sds(flat(it[1], ()).shape, F32)] * 4
    outs = pl.pallas_call(body, name=name, out_shape=out_shape, compiler_params=_cp())(*operands)
    return [tuple(o.reshape(shp) for o in outs[4 * i:4 * i + 4]) for i, shp in enumerate(shapes)]


def _adamw_layer(gparts, w, m, v, layer, prev, *, name):
    shape = w.shape
    nl, cols = shape[0], shape[-1]
    rows = w.size // cols // nl
    npart = gparts.shape[0]
    g3 = gparts.reshape(npart, rows, cols)
    w3, m3, v3 = (a.reshape(nl, rows, cols) for a in (w, m, v))
    fits = [t for t in range(min(rows, 256) // 8 * 8, 7, -8)
            if rows % t == 0 and npart * t * cols * g3.dtype.itemsize <= 2 * 1024 * 1024]
    t = fits[0] if fits else rows
    n_prev = 0 if prev is None else 4

    def body(g_ref, w_ref, m_ref, v_ref, *rest):
        go_ref, d_ref, mo_ref, vo_ref = rest[n_prev:]
        g = g_ref[0].astype(F32)
        for p in range(1, npart):
            g = g + g_ref[p].astype(F32)
        go_ref[0] = g
        d_ref[0], mo_ref[0], vo_ref[0] = _adamw_math(g, w_ref[0], m_ref[0], v_ref[0])

    spec = pl.BlockSpec((1, t, cols), lambda i: (layer, i, 0))
    outs = pl.pallas_call(
        body, name=name, grid=(rows // t,),
        in_specs=[pl.BlockSpec((npart, t, cols), lambda i: (0, i, 0)), spec, spec, spec] + [_ANY] * n_prev,
        out_specs=[spec] * 4, out_shape=[_sds((nl, rows, cols), F32)] * 4,
        input_output_aliases={4 + k: k for k in range(n_prev)},
        compiler_params=_cp(("parallel",)),
    )(g3, w3, m3, v3, *([] if prev is None else [a.reshape(nl, rows, cols) for a in prev]))
    return tuple(o.reshape(shape) for o in outs)


def _adamw(gparts, w, m, v, *, name):
    shape = w.shape
    cols = shape[-1]
    per_layer = isinstance(gparts, (list, tuple))
    nl = shape[0] if per_layer else 1
    rows = w.size // cols // nl
    glist = list(gparts) if per_layer else [gparts]
    npart = glist[0].shape[0]
    glist = [g.reshape(npart, rows, cols) for g in glist]
    w3, m3, v3 = (a.reshape(nl, rows, cols) for a in (w, m, v))
    budget = 2 * 1024 * 1024
    fits = [t for t in range(min(rows, 256) // 8 * 8, 7, -8)
            if rows % t == 0 and npart * t * cols * glist[0].dtype.itemsize <= budget]
    t = fits[0] if fits else rows
    nb = rows // t

    def body(*refs):
        g_refs = refs[:nl]
        w_ref, m_ref, v_ref, go_ref, d_ref, mo_ref, vo_ref, g_s = refs[nl:]
        layer = pl.program_id(0)
        for l in range(nl):
            @pl.when(layer == l)
            def _(l=l):
                g = g_refs[l][0].astype(F32)
                for p in range(1, npart):
                    g = g + g_refs[l][p].astype(F32)
                g_s[...] = g

        g = g_s[...]
        go_ref[0] = g
        d_ref[0], mo_ref[0], vo_ref[0] = _adamw_math(g, w_ref[0], m_ref[0], v_ref[0])

    def g_map(l):
        return lambda layer, i: (0, jnp.where(layer == l, i, jnp.where(layer < l, 0, nb - 1)), 0)

    spec = pl.BlockSpec((1, t, cols), lambda layer, i: (layer, i, 0))
    outs = pl.pallas_call(
        body, name=name, grid=(nl, nb),
        in_specs=[pl.BlockSpec((npart, t, cols), g_map(l)) for l in range(nl)] + [spec, spec, spec],
        out_specs=[spec] * 4, out_shape=[_sds((nl, rows, cols), F32)] * 4,
        scratch_shapes=[pltpu.VMEM((t, cols), F32)],
        compiler_params=_cp(("arbitrary", "arbitrary")),
    )(*glist, w3, m3, v3)
    return tuple(o.reshape(shape) for o in outs)


_ANY = pl.BlockSpec(memory_space=pl.ANY)


def _all_gather(blocks, *, name):
    na = len(blocks)

    def body(*refs):
        x_refs, out_refs = refs[:na], refs[na:2 * na]
        send_sems, recv_sems, local_sems = refs[2 * na:]
        x, y, c = lax.axis_index("x"), lax.axis_index("y"), lax.axis_index("c")
        me, sibling = (x, y, c), (x, y, 1 - c)
        chips = [(1 - x, y), (x, 1 - y), (1 - x, 1 - y)]

        def slot(a, px, py, pc):
            return out_refs[a].at[4 * px + 2 * py + pc]

        def copy(a, k, blk, to, src=None):
            return pltpu.make_async_remote_copy(
                src_ref=slot(a, *blk) if src is None else src, dst_ref=slot(a, *blk),
                send_sem=send_sems.at[7 * a + k], recv_sem=recv_sems.at[7 * a + k],
                device_id=to, device_id_type=MESH_ID)

        mine = [pltpu.make_async_copy(x_refs[a], slot(a, *me), local_sems.at[a]) for a in range(na)]
        for cp in mine:
            cp.start()
        first = []
        for a in range(na):
            first.append(copy(a, 0, me, sibling, src=x_refs[a]))
            first += [copy(a, 1 + j, me, (*chip, c), src=x_refs[a]) for j, chip in enumerate(chips)]
        for cp in first:
            cp.start()
        passed = []
        for a in range(na):
            for j, chip in enumerate(chips):
                copy(a, 1 + j, (*chip, c), me).wait_recv()
                fwd = copy(a, 4 + j, (*chip, c), sibling)
                fwd.start()
                passed.append(fwd)
        for a in range(na):
            copy(a, 0, sibling, me).wait_recv()
            for j, chip in enumerate(chips):
                copy(a, 4 + j, (*chip, 1 - c), me).wait_recv()
        for cp in first + passed:
            cp.wait_send()
        for cp in mine:
            cp.wait()

    outs = pl.pallas_call(
        body, name=name, in_specs=[_ANY] * na, out_specs=[_ANY] * na,
        out_shape=[_sds((N_DEV,) + b.shape, b.dtype) for b in blocks],
        scratch_shapes=[pltpu.SemaphoreType.DMA((7 * na,)), pltpu.SemaphoreType.DMA((7 * na,)),
                        pltpu.SemaphoreType.DMA((na,))],
    )(*blocks)
    return list(outs)


_HBM = pl.BlockSpec(memory_space=pltpu.HBM)
_SEM = pl.BlockSpec(memory_space=pltpu.SEMAPHORE)
_EFFECT = pltpu.SideEffectType.DATAFLOW_SIDE_EFFECTING


def _peers(x, y, c):
    out = []
    for k in range(1, N_DEV):
        out.append((1 - x if k & 4 else x, 1 - y if k & 2 else y, 1 - c if k & 1 else c))
    return out


def _own_slots(srcs, scatter, *, name, after=None):
    na = len(srcs)
    n_extra = 0 if after is None else 1
    me = (4 * lax.axis_index("x") + 2 * lax.axis_index("y") + lax.axis_index("c")).astype(jnp.int32).reshape(1)

    def body(me_ref, *refs):
        in_refs, out_refs = refs[:na], refs[na + n_extra:]
        for a in range(na):
            out_refs[a][0] = in_refs[a][0] if scatter else in_refs[a][...]

    def slot_spec(shard):
        zeros = (0,) * len(shard)
        return pl.BlockSpec((1,) + tuple(shard), lambda i, me_ref: (me_ref[0],) + zeros)

    def whole_spec(shape):
        zeros = (0,) * len(shape)
        return pl.BlockSpec(tuple(shape), lambda i, me_ref: zeros)

    shards = [s.shape[1:] if scatter else s.shape for s in srcs]
    in_specs = [slot_spec(sh) if scatter else whole_spec(sh) for sh in shards] + [_ANY] * n_extra
    outs = pl.pallas_call(
        body, name=name,
        grid_spec=pltpu.PrefetchScalarGridSpec(
            num_scalar_prefetch=1, grid=(1,), in_specs=in_specs, out_specs=[slot_spec(sh) for sh in shards]),
        out_shape=[_sds((N_DEV,) + tuple(sh), s.dtype) for sh, s in zip(shards, srcs)],
        compiler_params=_cp(("arbitrary",)),
    )(me, *srcs, *([] if after is None else [after]))
    return list(outs)


_N_COPIES = dict(scatter=7, gather=7, chips=4, forward=3)


def _exchange_copies(src_refs, land_refs, send_sems, recv_sems, mode):
    x, y, c = lax.axis_index("x"), lax.axis_index("y"), lax.axis_index("c")
    me = 4 * x + 2 * y + c
    nc = _N_COPIES[mode]
    chips = [(1 - x, y), (x, 1 - y), (1 - x, 1 - y)]
    cps = []
    for a in range(len(land_refs)):
        if mode in ("scatter", "gather"):
            plan = [((src_refs[a].at[4 * px + 2 * py + pc] if mode == "scatter" else src_refs[a]),
                     land_refs[a].at[me], (px, py, pc)) for px, py, pc in _peers(x, y, c)]
        elif mode == "chips":
            plan = [(src_refs[a], land_refs[a].at[me], to) for to in [(x, y, 1 - c)] + [(*ch, c) for ch in chips]]
        else:
            plan = [(land_refs[a].at[4 * px + 2 * py + c], land_refs[a].at[4 * px + 2 * py + c], (x, y, 1 - c))
                    for px, py in chips]
        for k, (src, dst, to) in enumerate(plan):
            cps.append(pltpu.make_async_remote_copy(
                src_ref=src, dst_ref=dst, send_sem=send_sems.at[nc * a + k], recv_sem=recv_sems.at[nc * a + k],
                device_id=to, device_id_type=MESH_ID))
    return cps


def _exchange_start(srcs, lands, mode, *, name):
    ns, nz = len(srcs), len(lands)
    nsem = _N_COPIES[mode] * nz

    def body(*refs):
        src_refs, land_refs = refs[:ns], refs[ns:ns + nz]
        send_sems, recv_sems = refs[ns + nz], refs[ns + nz + 1]
        token = refs[-1]
        for cp in _exchange_copies(src_refs, land_refs, send_sems, recv_sems, mode):
            cp.start()
        token[...] = jnp.zeros(token.shape, token.dtype)

    hbm = lambda a: pltpu.HBM(a.shape, a.dtype)
    outs = pl.pallas_call(
        body, name=name,
        out_shape=(pltpu.SemaphoreType.DMA((nsem,)), pltpu.SemaphoreType.DMA((nsem,)),
                   *[hbm(a) for a in srcs], *[hbm(a) for a in lands], _sds((8, LANE), F32)),
        in_specs=[_HBM] * (ns + nz),
        out_specs=(_SEM, _SEM, *[_HBM] * (ns + nz), pl.BlockSpec(memory_space=pltpu.VMEM)),
        input_output_aliases={i: 2 + i for i in range(ns + nz)},
        compiler_params=pltpu.CompilerParams(has_side_effects=_EFFECT),
    )(*[pltpu.with_memory_space_constraint(a, pltpu.HBM) for a in list(srcs) + list(lands)])
    return outs[0], outs[1], list(outs[2:2 + ns]), list(outs[2 + ns:2 + ns + nz]), outs[-1]


def _exchange_wait(send_sems, recv_sems, srcs, lands, after, mode, *, name):
    ns, nz = len(srcs), len(lands)

    def body(*refs):
        src_refs, land_refs = refs[:ns], refs[ns:ns + nz]
        s_sems, r_sems = refs[ns + nz], refs[ns + nz + 1]
        for cp in _exchange_copies(src_refs, land_refs, s_sems, r_sems, mode):
            cp.wait_send()
            cp.wait_recv()

    hbm = lambda a: pltpu.HBM(a.shape, a.dtype)
    outs = pl.pallas_call(
        body, name=name,
        out_shape=(*[hbm(a) for a in srcs], *[hbm(a) for a in lands]),
        in_specs=[_HBM] * (ns + nz) + [_SEM, _SEM, _ANY],
        out_specs=tuple([_HBM] * (ns + nz)),
        input_output_aliases={i: i for i in range(ns + nz)},
        compiler_params=pltpu.CompilerParams(has_side_effects=_EFFECT),
    )(*srcs, *lands, send_sems, recv_sems, after)
    return list(outs[ns:])


_WIN_SEGS = (("ql", 0, Q_LORA, SEG_QL[0]), ("kvl", Q_LORA, KV_LORA, SEG_KVL[0]),
             ("kr", Q_LORA + KV_LORA, ROPE, SEG_KR[0]), ("mg", Q_LORA + KV_LORA + ROPE, D_MLA, SEG_MG[0]),
             ("ci", Q_LORA + KV_LORA + ROPE + D_MLA, 2 * D_CONV, SEG_CI[0]),
             ("cg", Q_LORA + KV_LORA + ROPE + D_MLA + 2 * D_CONV, D_CONV, SEG_CG[0]))
_WIN_SHARD = IN_COLS // N_DEV


def _win_pieces():
    out = []
    for _, o, n, new in _WIN_SEGS:
        for j in range(N_DEV):
            lo, hi = max(o, j * _WIN_SHARD), min(o + n, (j + 1) * _WIN_SHARD)
            if lo < hi:
                out.append((j, lo - j * _WIN_SHARD, new + lo - o, hi - lo))
    return out


WIN_T = 512


def _win_assemble(w_all, *, name):
    d = w_all.shape[2]
    t = min(WIN_T, d)
    pieces = sorted(_win_pieces(), key=lambda p: p[2])
    assert all(lo % 8 == 0 and n % 8 == 0 for _, lo, _, n in pieces)

    def body(w_ref, o_ref):
        rows = [w_ref[j].astype(F32)[lo:lo + n, :] for j, lo, _, n in pieces]
        rows.append(jnp.zeros((IN_PAD - (SEG_KR[0] + ROPE), t), F32))
        o_ref[...] = jnp.concatenate(rows, axis=0).astype(o_ref.dtype)

    return pl.pallas_call(
        body, name=name, grid=(d // t,),
        in_specs=[pl.BlockSpec((N_DEV, _WIN_SHARD, t), lambda i: (0, 0, i))],
        out_specs=pl.BlockSpec((IN_PAD, t), lambda i: (0, i)), out_shape=_sds((IN_PAD, d), w_all.dtype),
        compiler_params=_cp(("parallel",)),
    )(w_all)


def _win_split(grad, *, name):
    d = grad.shape[1]
    t = min(WIN_T, d)
    by_shard = [sorted([p for p in _win_pieces() if p[0] == j], key=lambda p: p[1]) for j in range(N_DEV)]

    def body(g_ref, o_ref):
        for j in range(N_DEV):
            rows = [g_ref[new:new + n, :] for _, _, new, n in by_shard[j]]
            o_ref[j] = jnp.concatenate(rows, axis=0).astype(o_ref.dtype)

    return pl.pallas_call(
        body, name=name, grid=(d // t,),
        in_specs=[pl.BlockSpec((IN_PAD, t), lambda i: (0, i))],
        out_specs=pl.BlockSpec((N_DEV, _WIN_SHARD, t), lambda i: (0, 0, i)),
        out_shape=_sds((N_DEV, _WIN_SHARD, d), WIRE_DTYPE),
        compiler_params=_cp(("parallel",)),
    )(grad)


def _cols_to_shards(a):
    r, n = a.shape
    return a.reshape(r, N_DEV, n // N_DEV).transpose(1, 0, 2)


def _shards_to_cols(a):
    nd, r, w = a.shape
    return a.transpose(1, 0, 2).reshape(r, nd * w)


def _qup_permute(w):
    w3 = w.reshape(w.shape[0], N_HEADS, QK_DIM)
    nope = w3[:, :, :NOPE].reshape(w.shape[0], N_HEADS * NOPE)
    rope = jnp.pad(w3[:, :, NOPE:], ((0, 0), (0, 0), (0, LANE - ROPE))).reshape(w.shape[0], N_HEADS * LANE)
    return jnp.concatenate([nope, rope], axis=1)


def _qup_unpermute(g):
    r = g.shape[0]
    nope = g[:, :N_HEADS * NOPE].reshape(r, N_HEADS, NOPE)
    rope = g[:, N_HEADS * NOPE:].reshape(r, N_HEADS, LANE)[:, :, :ROPE]
    return jnp.concatenate([nope, rope], axis=2).reshape(r, N_HEADS * QK_DIM)


def _norm_tiles(g):
    return g[:NOPE].reshape(1, LANE), jnp.pad(g[NOPE:], (0, LANE - ROPE)).reshape(1, LANE)


def _rope_tiles(positions):
    inv_freq = 1.0 / (ROPE_THETA ** (jnp.arange(0, ROPE, 2, dtype=F32) / ROPE))
    ang = positions.astype(F32)[:, None] * inv_freq
    cos, sin = jnp.cos(ang), jnp.sin(ang)
    zq = jnp.zeros_like(cos)
    c_t = jnp.concatenate([cos, cos, zq, zq], axis=1)
    s1_t = jnp.concatenate([-sin, zq, zq, zq], axis=1)
    s2_t = jnp.concatenate([zq, sin, zq, zq], axis=1)
    return c_t, s1_t, s2_t


_BIG = ("w_in", "w_q_up", "w_kv_up", "w_pw", "w_out")
_COL_SHARDED = ("w_q_up", "w_kv_up")


def _unpack_rows(buf, shapes):
    out, r0 = [], 0
    lead = buf.shape[:-2]
    for shp in shapes:
        n = math.prod(shp) // LANE
        out.append(buf[..., r0:r0 + n, :].reshape(lead + tuple(shp)))
        r0 += n
    return out


_SMALL = (("dmod", 3 * D_MODEL), ("norm_g", D_MODEL), ("q_lat_g", Q_LORA), ("kv_lat_g", KV_LORA),
          ("q_norm_g", 2 * LANE), ("k_norm_g", 2 * LANE), ("glu_b", 2 * D_CONV), ("dw_w", HALO * D_CONV),
          ("dw_b", D_CONV), ("conv_ln_g", D_CONV), ("conv_ln_b", D_CONV), ("b_pw", D_CONV))


def _layer_fwd(x, p, rope, l, late=None):
    n = lambda s: f"{s}_l{l}"
    c_t, s1_t, s2_t = rope
    h = _prenorm(x, p["norm_g"], p["shift"], p["sc1p"], name=n("prenorm"))
    z = _mm(h, p["w_in"], tb=True, name=n("in_proj"), tn=IN_TILE, n_outer=True)
    if late is not None:
        p = {**p, **late(z)}
    qn, kn, q_raw, kv, qf, kf, vf = _mla_pre(z, p["w_q_up"], p["w_kv_up"], p["q_lat_g"], p["kv_lat_g"],
                                             c_t, s1_t, s2_t, *p["qk_tiles"], name=n("mla_pre"))
    o, lse = _flash_fwd(qf, kf, vf, name=n("flash_fwd"))
    u1, u3, u4m = _conv_fwd(z, p["glu_b"], p["dw_w"], p["dw_b"], p["conv_ln_g"], p["conv_ln_b"], p["w_pw"],
                            name=n("conv_fwd"))
    cat = _gate_cat(o, z, u4m, p["b_pw"], name=n("gate_cat"))
    y, x_next = _mm(cat, p["w_out"], name=n("out_proj"), tn=1024, residual=(x, p["gate"]))
    saved = dict(x=x, h=h, z=z, qn=qn, kn=kn, q_raw=q_raw, kv=kv, qf=qf, kf=kf, vf=vf, o=o, lse=lse,
                 u1=u1, u3=u3, u4m=u4m, cat=cat, y=y)
    return x_next, saved, p


def _layer_bwd(gxo, dy, dgate, p, sv, rope, l, below=None, hook_rest=None, hook_w_in=None):
    n = lambda s: f"{s}_l{l}"
    c_t, s1_t, s2_t = rope
    z = sv["z"]
    g_w_out = _mm(sv["cat"], dy, ta=True, name=n("g_w_out"), tm=1024, tn=1024, after=p.get("after_start"))
    do, delta, du4, g_b_pw, dz = _gate_bwd(dy, p["w_out"], sv["o"], z, sv["u4m"], p["b_pw"], name=n("gate_bwd"))
    g_w_pw = _mm(sv["u3"], du4, ta=True, name=n("g_w_pw"), tm=1024, tn=1024, tk=512)
    du3 = _mm(du4, p["w_pw"], tb=True, name=n("d_u3"), tn=1024)
    dz, g_ln_g, g_ln_b, g_dw_b, g_glu_b, g_dw_w = _conv_bwd(
        du3, sv["u1"], z, dz, p["glu_b"], p["dw_w"], p["conv_ln_g"], p["conv_ln_b"], name=n("conv_bwd"))
    t_att = min(ATT_T, z.shape[0])
    to_lanes = lambda a: a.reshape(N_HEADS, z.shape[0] // t_att, 1, t_att)
    dqf, dkf, dvf = _flash_bwd(sv["qf"], sv["kf"], sv["vf"], do,
                               to_lanes(sv["lse"][:, :, 0]), to_lanes(delta), name=n("flash_bwd"))
    dz, g_w_q_up, g_w_kv_up, g_qn, g_kn, g_ql, g_kvl = _mla_bwd(
        dqf, dkf, dvf, sv["q_raw"], sv["kv"], z, dz, sv["qn"], sv["kn"], p["w_q_up"], p["w_kv_up"],
        p["q_lat_g"], p["kv_lat_g"], c_t, s1_t, s2_t, *p["qk_tiles"], name=n("mla_bwd"))
    big = dict(w_q_up=g_w_q_up, w_kv_up=g_w_kv_up, w_pw=g_w_pw, w_out=g_w_out)
    after = None if hook_rest is None else hook_rest(big)
    g_w_in = _mm(dz, sv["h"], ta=True, name=n("g_w_in"), tm=512, tn=1024, after=after)
    big["w_in"] = g_w_in
    after = None if hook_w_in is None else hook_w_in(g_w_in)
    dh = _mm(dz, p["w_in"], name=n("d_h"), tn=1024, after=after)
    dx, dshift, dscale, g_norm, *down = _prenorm_bwd(dh, sv["x"], gxo, p["norm_g"], p["sc1p"], below,
                                                     name=n("prenorm_bwd"))
    small = dict(dmod=jnp.concatenate([dshift, dscale, dgate], axis=1), norm_g=g_norm, q_lat_g=g_ql, kv_lat_g=g_kvl,
                 q_norm_g=g_qn, k_norm_g=g_kn, glu_b=g_glu_b, dw_w=g_dw_w, dw_b=g_dw_b,
                 conv_ln_g=g_ln_g, conv_ln_b=g_ln_b, b_pw=g_b_pw)
    return (dx, *down), big, small


def _layer_params(l, full, mod_l, small):
    d = D_MODEL
    row = lambda a: a.reshape(1, -1)
    shift, scale, gate = mod_l[:, :d], mod_l[:, d:2 * d], mod_l[:, 2 * d:]
    dw_w = jnp.pad(full["dw_w"][l], ((0, HALO - CONV_K), (0, 0)))
    return dict(
        shift=shift, sc1p=1.0 + scale, gate=gate, norm_g=row(small["norm_g"][l]),
        **{k: full[k][l] for k in _BIG if k in full}, dw_w=dw_w,
        q_lat_g=row(small["q_lat_g"][l]), kv_lat_g=row(small["kv_lat_g"][l]),
        qk_tiles=_norm_tiles(small["q_norm_g"][l]) + _norm_tiles(small["k_norm_g"][l]),
        glu_b=row(small["glu_b"][l]), dw_b=row(small["dw_b"][l]), conv_ln_g=row(small["conv_ln_g"][l]),
        conv_ln_b=row(small["conv_ln_b"][l]), b_pw=row(small["b_pw"][l]))


def kernel(x, c, positions, ada_w, ada_b, norm_g, w_in, q_lat_g, w_q_up, kv_lat_g, w_kv_up, q_norm_g, k_norm_g, glu_b, dw_w, dw_b, conv_ln_g, conv_ln_b, w_pw, b_pw, w_out, loss_target, m_ada_w, m_ada_b, m_norm_g, m_w_in, m_q_lat_g, m_w_q_up, m_kv_lat_g, m_w_kv_up, m_q_norm_g, m_k_norm_g, m_glu_b, m_dw_w, m_dw_b, m_conv_ln_g, m_conv_ln_b, m_w_pw, m_b_pw, m_w_out, v_ada_w, v_ada_b, v_norm_g, v_w_in, v_q_lat_g, v_w_q_up, v_kv_lat_g, v_w_kv_up, v_q_norm_g, v_k_norm_g, v_glu_b, v_dw_w, v_dw_b, v_conv_ln_g, v_conv_ln_b, v_w_pw, v_b_pw, v_w_out):
    names = ("ada_w", "ada_b", "norm_g", "w_in", "q_lat_g", "w_q_up", "kv_lat_g", "w_kv_up", "q_norm_g",
             "k_norm_g", "glu_b", "dw_w", "dw_b", "conv_ln_g", "conv_ln_b", "w_pw", "b_pw", "w_out")
    w_loc = dict(zip(names, (ada_w, ada_b, norm_g, w_in, q_lat_g, w_q_up, kv_lat_g, w_kv_up, q_norm_g, k_norm_g,
                             glu_b, dw_w, dw_b, conv_ln_g, conv_ln_b, w_pw, b_pw, w_out)))
    m_loc = dict(zip(names, (m_ada_w, m_ada_b, m_norm_g, m_w_in, m_q_lat_g, m_w_q_up, m_kv_lat_g, m_w_kv_up,
                             m_q_norm_g, m_k_norm_g, m_glu_b, m_dw_w, m_dw_b, m_conv_ln_g, m_conv_ln_b, m_w_pw,
                             m_b_pw, m_w_out)))
    v_loc = dict(zip(names, (v_ada_w, v_ada_b, v_norm_g, v_w_in, v_q_lat_g, v_w_q_up, v_kv_lat_g, v_w_kv_up,
                             v_q_norm_g, v_k_norm_g, v_glu_b, v_dw_w, v_dw_b, v_conv_ln_g, v_conv_ln_b, v_w_pw,
                             v_b_pw, v_w_out)))
    nl, d = N_LAYERS, D_MODEL
    me = 4 * lax.axis_index("x") + 2 * lax.axis_index("y") + lax.axis_index("c")
    x2, tgt = x[0], loss_target[0]
    ada_cols = ada_w.shape[-1]

    tr = lambda a: jnp.swapaxes(a, 1, 2)
    w_loc, m_loc, v_loc = ({**dd, "w_in": tr(dd["w_in"])} for dd in (w_loc, m_loc, v_loc))
    w_in0 = [w_loc["w_in"][0].astype(WIRE_DTYPE)]
    fly_c = _exchange_start(w_in0, _own_slots(w_in0, False, name="own_w_in_l0"), "chips", name="gather_start_w_in_l0")
    held = dict(c=c, positions=positions, ada_b=ada_b, norm_g=norm_g, q_lat_g=q_lat_g, kv_lat_g=kv_lat_g,
                q_norm_g=q_norm_g, k_norm_g=k_norm_g, glu_b=glu_b, dw_w=dw_w, dw_b=dw_b, conv_ln_g=conv_ln_g,
                conv_ln_b=conv_ln_b, b_pw=b_pw, big={k: w_loc[k] for k in _BIG})
    tok_c, held = lax.optimization_barrier((fly_c[4], held))
    c, positions, ada_b, norm_g, q_lat_g, kv_lat_g, q_norm_g, k_norm_g, glu_b, dw_w, dw_b, conv_ln_g, conv_ln_b, b_pw = (
        held[k] for k in ("c", "positions", "ada_b", "norm_g", "q_lat_g", "kv_lat_g", "q_norm_g", "k_norm_g", "glu_b",
                          "dw_w", "dw_b", "conv_ln_g", "conv_ln_b", "b_pw"))
    wire = {k: held["big"][k].astype(WIRE_DTYPE) for k in _BIG}

    dw_pad = jnp.pad(dw_w, ((0, 0), (0, HALO - CONV_K), (0, 0)))
    c_rows = c.reshape(d // LANE, LANE) + tok_c[0:1, :]
    c_all, dw_all = _all_gather([c_rows, dw_pad], name="gather_c")
    c_all = c_all.reshape(N_DEV, d)
    ada_b_cols = lax.dynamic_slice_in_dim(ada_b, me * ada_cols, ada_cols, axis=1).reshape(nl, 1, ada_cols)
    mod_cols = _ada_fwd(c_all, ada_w, ada_b_cols, name="ada_fwd")
    mod_all = _all_gather([mod_cols], name="gather_mod")[0]
    mod_me = lax.dynamic_index_in_dim(mod_all, me, axis=2, keepdims=False)
    mod = mod_me.transpose(1, 0, 2).reshape(nl, 1, N_DEV * ada_cols)

    from_chips = _exchange_wait(*fly_c[:4], mod, "chips", name="gather_wait_w_in_l0")
    fly_f = _exchange_start([], from_chips, "forward", name="forward_start_w_in_l0")
    w_in_all0 = _exchange_wait(*fly_f[:4], fly_f[4], "forward", name="forward_wait_w_in_l0")[0]
    rest0 = [wire[k][0] for k in _BIG[1:]]
    fly_r0, fly_w1 = {}, {}
    fly_r0["x"] = _exchange_start(rest0, _own_slots(rest0, False, name="own_weights_l0_rest", after=w_in_all0),
                                  "gather", name="gather_start_l0_rest")

    def layout_rest(parts):
        return dict(w_q_up=_qup_permute(_shards_to_cols(parts[0])), w_kv_up=_shards_to_cols(parts[1]),
                    w_pw=parts[2].reshape(D_CONV, D_CONV), w_out=parts[3].reshape(D_MLA + D_CONV, d))

    small_in = dict(norm_g=norm_g, q_lat_g=q_lat_g, kv_lat_g=kv_lat_g, q_norm_g=q_norm_g, k_norm_g=k_norm_g,
                    glu_b=glu_b, dw_b=dw_b, conv_ln_g=conv_ln_g, conv_ln_b=conv_ln_b, b_pw=b_pw)
    dw_full = [_shards_to_cols(dw_all[:, l])[:CONV_K] for l in range(nl)]
    rope = _rope_tiles(positions[0])

    def layer_params(l, w_in_all, rest, mod_l):
        full = dict(dw_w=dw_full)
        if w_in_all is not None:
            full["w_in"] = {l: _win_assemble(w_in_all, name=f"w_in_assemble_l{l}")}
        if rest is not None:
            full.update({k: {l: a} for k, a in layout_rest(rest).items()})
        return _layer_params(l, full, mod_l, small_in)

    def late_l0(z):
        parts = _exchange_wait(*fly_r0["x"][:4], z, "gather", name="gather_wait_l0_rest")
        src1 = [wire[k][1] for k in _BIG]
        fly_w1["x"] = _exchange_start(src1, _own_slots(src1, False, name="own_weights_l1", after=parts[0]), "gather",
                                      name="gather_start_l1")
        late = layout_rest(parts)
        late["q_lat_g"] = small_in["q_lat_g"][0].reshape(1, -1) + fly_w1["x"][4][0, 0]
        return late

    params, saved = [None] * nl, [None] * nl
    p0 = layer_params(0, w_in_all0, None, mod[0] + fly_r0["x"][4][0, 0])
    xs, saved[0], params[0] = _layer_fwd(x2, p0, rope, 0, late=late_l0)
    parts1 = _exchange_wait(*fly_w1["x"][:4], xs, "gather", name="gather_wait_l1")
    params[1] = layer_params(1, parts1[0], parts1[1:], mod[1])
    xs, saved[1], _ = _layer_fwd(xs, params[1], rope, 1)
    gx, loss_part, dy, dgate = _loss_head(xs, tgt, saved[1]["y"], params[1]["gate"], name="loss_head")
    loss = lax.psum(loss_part[0, 0], ("x", "y", "c"))

    def shard_major(k, g):
        if k == "w_q_up":
            g = _qup_unpermute(g)
        if k in _COL_SHARDED:
            return _cols_to_shards(g)
        return g.reshape((N_DEV, g.shape[0] // N_DEV, g.shape[1]))

    def scatter_start(send, tag):
        lands = _own_slots(send, True, name=f"own_grads_{tag}")
        return _exchange_start(send, lands, "scatter", name=f"scatter_start_{tag}")

    def wire_rest(big):
        return [shard_major(k, big[k]).astype(WIRE_DTYPE) for k in _BIG[1:]]

    big_g, small_g, flying = [None] * nl, [None] * nl, {}
    (gx, dy, dgate), big_g[1], small_g[1] = _layer_bwd(gx, dy, dgate, params[1], saved[1], rope, 1,
                                                       below=(saved[0]["y"], params[0]["gate"]))
    flying["l1"] = scatter_start([_win_split(big_g[1]["w_in"], name="w_in_split_l1")] + wire_rest(big_g[1]), "l1")
    p0 = dict(params[0], after_start=flying["l1"][4])

    def start_rest_l0(big):
        flying["l0_rest"] = scatter_start(wire_rest(big), "l0_rest")
        return flying["l0_rest"][4]

    res, arrived = {}, [None] * nl

    def start_w_in_l0(g_w_in):
        flying["l0_w_in"] = scatter_start([_win_split(g_w_in, name="w_in_split_l0")], "l0_w_in")
        tok = flying["l0_w_in"][4]
        arrived[1] = _exchange_wait(*flying["l1"][:4], tok, "scatter", name="scatter_wait_l1")
        arrived[0] = [None] + _exchange_wait(*flying["l0_rest"][:4], tok, "scatter", name="scatter_wait_l0_rest")
        for i, k in enumerate(_BIG):
            if i > 0:
                res[k] = _adamw([arrived[l][i] for l in range(nl)], w_loc[k], m_loc[k], v_loc[k], name=f"adamw_{k}")
        res["w_in_l1"] = _adamw_layer(arrived[1][0], w_loc["w_in"], m_loc["w_in"], v_loc["w_in"], 1, None,
                                      name="adamw_w_in_l1")
        return res["w_in_l1"][0]

    (gx,), big_g[0], small_g[0] = _layer_bwd(gx, dy, dgate, p0, saved[0], rope, 0, hook_rest=start_rest_l0,
                                             hook_w_in=start_w_in_l0)

    tile = 8 * LANE
    padded = [(k, nn, -(-nn // tile) * tile) for k, nn in _SMALL]
    spk = jnp.concatenate([jnp.pad(small_g[l][k].reshape(-1), (0, np_ - nn)).reshape(-1, LANE)
                           for l in range(nl) for k, nn, np_ in padded], axis=0)
    s_all = _all_gather([spk], name="gather_small_grads")[0]
    s_rows = sum(np_ for _, _, np_ in padded) // LANE
    s_all = s_all.reshape(N_DEV, nl, s_rows, LANE)
    s_parts = {k: a[..., :nn] for (k, nn, _), a in
               zip(padded, _unpack_rows(s_all, [(np_,) for _, _, np_ in padded]))}

    dmod_all = s_parts["dmod"]
    dmod_cols = lax.dynamic_slice_in_dim(dmod_all, me * ada_cols, ada_cols, axis=2).transpose(1, 0, 2)
    g_ada_w = _ada_bwd(c_all.T, dmod_cols, name="ada_bwd")
    gp = {}
    gp["ada_w"] = g_ada_w[None]
    gp["ada_b"] = dmod_all
    for k in ("norm_g", "q_lat_g", "kv_lat_g", "glu_b", "dw_b", "conv_ln_g", "conv_ln_b", "b_pw"):
        gp[k] = s_parts[k]
    for k in ("q_norm_g", "k_norm_g"):
        t = s_parts[k]
        gp[k] = jnp.concatenate([t[..., :NOPE], t[..., LANE:LANE + ROPE]], axis=-1)
    dw_g = s_parts["dw_w"].reshape(N_DEV, nl, HALO, D_CONV)[:, :, :CONV_K]
    gp["dw_w"] = lax.dynamic_slice_in_dim(dw_g, me * LANE, LANE, axis=3)

    res["ada_w"] = _adamw(gp["ada_w"], w_loc["ada_w"], m_loc["ada_w"], v_loc["ada_w"], name="adamw_ada_w")
    small_names = [k for k in names if k not in _BIG and k != "ada_w"]
    res.update(zip(small_names, _adamw_small([(gp[k], w_loc[k], m_loc[k], v_loc[k]) for k in small_names],
                                             name="adamw_small")))
    arrived[0][0] = _exchange_wait(*flying["l0_w_in"][:4], res["ada_w"][1], "scatter", name="scatter_wait_l0_w_in")[0]
    w_in_res = _adamw_layer(arrived[0][0], w_loc["w_in"], m_loc["w_in"], v_loc["w_in"], 0, res.pop("w_in_l1"),
                            name="adamw_w_in_l0")
    res["w_in"] = tuple(tr(a) for a in w_in_res)
    out = [loss, gx[None]]
    for idx in range(4):
        out += [res[k][idx] for k in names]
    return tuple(out)
```

```python
import functools
import math

import jax
import jax.numpy as jnp
from jax import lax
from jax.experimental import pallas as pl
from jax.experimental.pallas import tpu as pltpu

F32 = jnp.float32
MXU_DTYPE = jnp.bfloat16
WIRE_DTYPE = jnp.bfloat16

D_MODEL = 2048
N_LAYERS = 2
N_DEV = 8
N_HEADS = 8
NOPE = 128
ROPE = 64
V_DIM = 128
QK_DIM = NOPE + ROPE
Q_LORA = 512
KV_LORA = 256
D_MLA = N_HEADS * V_DIM
D_CONV = 1024
CONV_K = 31
ROPE_THETA = 10000.0
EPS = 1e-6
LANE = 128
HEAD_PAD = 2 * LANE
HALO = 32

SEG_CI = (0, 2 * D_CONV)
SEG_MG = (2 * D_CONV, D_MLA)
SEG_CG = (2 * D_CONV + D_MLA, D_CONV)
SEG_QL = (2 * D_CONV + D_MLA + D_CONV, Q_LORA)
SEG_KVL = (SEG_QL[0] + Q_LORA, KV_LORA)
SEG_KR = (SEG_KVL[0] + KV_LORA, LANE)
SEG_LAT = (SEG_QL[0], 1024)
IN_PAD = SEG_LAT[0] + SEG_LAT[1]
IN_TILE = IN_PAD // 4
assert SEG_KR[0] + LANE <= IN_PAD and SEG_LAT[0] % SEG_LAT[1] == 0
IN_COLS = Q_LORA + KV_LORA + ROPE + D_MLA + 2 * D_CONV + D_CONV

ADAM_LR = 0.001
ADAM_B1 = 0.9
ADAM_B2 = 0.999
ADAM_EPS = 1e-08
ADAM_WD = 0.01
ADAM_STEP = 10

VMEM_LIMIT = 56 * 1024 * 1024
ATT_T = 512
ROW_T = 256
CONV_T = 128
MESH_ID = pl.DeviceIdType.MESH


def _cp(sem=None):
    kw = dict(vmem_limit_bytes=VMEM_LIMIT)
    if sem is not None:
        kw["dimension_semantics"] = sem
    return pltpu.CompilerParams(**kw)


def _sds(shape, dtype):
    return jax.ShapeDtypeStruct(shape, dtype)


def _silu(x):
    return x * jax.nn.sigmoid(x)


def _dsilu(x):
    s = jax.nn.sigmoid(x)
    return s * (1.0 + x * (1.0 - s))


def _rowspec(t, width, col=0):
    return pl.BlockSpec((t, width), lambda i: (i, col))


def _vecspec(width):
    return pl.BlockSpec((1, width), lambda i: (0, 0))


def _colsum(v):
    return jnp.sum(v, axis=0, keepdims=True)


def _mm(a, b, *, name, ta=False, tb=False, out_dtype=F32, tm=512, tn=512, tk=None, n_outer=False, after=None,
        residual=None):
    if ta:
        kdim, m = a.shape
    else:
        m, kdim = a.shape
    if tb:
        n, k2 = b.shape
    else:
        k2, n = b.shape
    assert kdim == k2, (a.shape, b.shape)
    tm, tn = min(tm, m), min(tn, n)
    tk = kdim if tk is None else min(tk, kdim)
    assert m % tm == 0 and n % tn == 0 and kdim % tk == 0, (m, n, kdim, tm, tn, tk)
    nk = kdim // tk
    dims = (((0 if ta else 1,), (1 if tb else 0,)), ((), ()))

    n_extra = 0 if after is None else 1
    assert residual is None or nk == 1

    def body(a_ref, b_ref, *rest):
        if residual is not None:
            x_ref, gate_ref = rest[:2]
            rest = rest[2:]
        o_ref, scratch = rest[n_extra], rest[n_extra + 1:]
        prod = lax.dot_general(a_ref[...].astype(MXU_DTYPE), b_ref[...].astype(MXU_DTYPE), dims,
                               preferred_element_type=F32)
        if residual is not None:
            o_ref[...] = prod.astype(o_ref.dtype)
            scratch[0][...] = x_ref[...] + gate_ref[...] * prod
        elif nk == 1:
            o_ref[...] = prod.astype(o_ref.dtype)
        else:
            acc = scratch[0]
            k = pl.program_id(2)

            @pl.when(k == 0)
            def _():
                acc[...] = prod

            @pl.when(k > 0)
            def _():
                acc[...] += prod

            @pl.when(k == nk - 1)
            def _():
                o_ref[...] = acc[...].astype(o_ref.dtype)

    if n_outer:
        ij = lambda g0, g1: (g1, g0)
        grid = (n // tn, m // tm, nk)
    else:
        ij = lambda g0, g1: (g0, g1)
        grid = (m // tm, n // tn, nk)

    def a_map(g0, g1, k):
        i, _ = ij(g0, g1)
        return (k, i) if ta else (i, k)

    def b_map(g0, g1, k):
        _, j = ij(g0, g1)
        return (j, k) if tb else (k, j)

    def o_map(g0, g1, k):
        return ij(g0, g1)

    in_specs = [pl.BlockSpec((tk, tm) if ta else (tm, tk), a_map), pl.BlockSpec((tn, tk) if tb else (tk, tn), b_map)]
    operands = [a, b]
    out_specs, out_shape = pl.BlockSpec((tm, tn), o_map), _sds((m, n), out_dtype)
    if residual is not None:
        in_specs += [pl.BlockSpec((tm, tn), o_map), pl.BlockSpec((1, tn), lambda g0, g1, k: (0, ij(g0, g1)[1]))]
        operands += list(residual)
        out_specs, out_shape = [out_specs, pl.BlockSpec((tm, tn), o_map)], [out_shape, _sds((m, n), F32)]
    if after is not None:
        in_specs.append(_ANY)
        operands.append(after)
    return pl.pallas_call(
        body, name=name, grid=grid, in_specs=in_specs, out_specs=out_specs, out_shape=out_shape,
        scratch_shapes=[pltpu.VMEM((tm, tn), F32)] if nk > 1 else [],
        compiler_params=_cp(("parallel", "parallel", "arbitrary")),
    )(*operands)


def _prenorm(x, g, shift, sc1p, *, name):
    s, d = x.shape
    t = min(2 * ROW_T, s)

    def body(x_ref, g_ref, sh_ref, sc_ref, h_ref):
        xv = x_ref[...]
        r = lax.rsqrt(jnp.mean(xv * xv, axis=-1, keepdims=True) + EPS)
        h_ref[...] = ((xv * r) * g_ref[...] * sc_ref[...] + sh_ref[...]).astype(h_ref.dtype)

    return pl.pallas_call(
        body, name=name, grid=(s // t,),
        in_specs=[_rowspec(t, d), _vecspec(d), _vecspec(d), _vecspec(d)],
        out_specs=_rowspec(t, d), out_shape=_sds((s, d), MXU_DTYPE),
        compiler_params=_cp(("parallel",)),
    )(x, g, shift, sc1p)


def _rope_fwd(r, c_t, s1_t, s2_t):
    return r * c_t + pltpu.roll(r, LANE - ROPE // 2, 1) * s1_t + pltpu.roll(r, ROPE // 2, 1) * s2_t


def _rope_bwd(d, c_t, s1_t, s2_t):
    return d * c_t + pltpu.roll(d * s1_t, ROPE // 2, 1) + pltpu.roll(d * s2_t, LANE - ROPE // 2, 1)


def _lanesum(v):
    return jnp.sum(v, axis=-1, keepdims=True)


def _mla_pre(z, w_q_up, w_kv_up, g_ql, g_kvl, c_t, s1_t, s2_t, gqn, gqr, gkn, gkr, *, name):
    s = z.shape[0]
    t = min(2 * ROW_T, s)
    scale = 1.0 / math.sqrt(QK_DIM)
    wide = 2 * N_HEADS * LANE

    def body(ql_ref, kvl_ref, kr_ref, wq_ref, wkv_ref, gq_ref, gk_ref, c_ref, s1_ref, s2_ref,
             gqn_ref, gqr_ref, gkn_ref, gkr_ref, qn_ref, kn_ref, q_ref, kv_ref, qf_ref, kf_ref, vf_ref):
        for src, g_ref, dst, w_ref, up in ((ql_ref, gq_ref, qn_ref, wq_ref, q_ref),
                                           (kvl_ref, gk_ref, kn_ref, wkv_ref, kv_ref)):
            v = src[...]
            r = lax.rsqrt(jnp.mean(v * v, axis=-1, keepdims=True) + EPS)
            dst[...] = ((v * r) * g_ref[...]).astype(dst.dtype)
            up[...] = jnp.dot(dst[...], w_ref[...], preferred_element_type=F32)
        c_v, s1_v, s2_v = c_ref[...], s1_ref[...], s2_ref[...]
        kr = kr_ref[...]
        kr_ss = _lanesum(kr * kr)
        for h in range(N_HEADS):
            n = q_ref[:, h * LANE:(h + 1) * LANE]
            r = q_ref[:, N_HEADS * LANE + h * LANE:N_HEADS * LANE + (h + 1) * LANE]
            rs = lax.rsqrt((_lanesum(n * n) + _lanesum(r * r)) * (1.0 / QK_DIM) + EPS)
            qf_ref[h, :, 0:LANE] = (((n * rs) * gqn_ref[...]) * scale).astype(qf_ref.dtype)
            rr = _rope_fwd((r * rs) * gqr_ref[...], c_v, s1_v, s2_v)
            qf_ref[h, :, LANE:HEAD_PAD] = (rr * scale).astype(qf_ref.dtype)

            n = kv_ref[:, h * 2 * LANE:h * 2 * LANE + LANE]
            rs = lax.rsqrt((_lanesum(n * n) + kr_ss) * (1.0 / QK_DIM) + EPS)
            kf_ref[h, :, 0:LANE] = ((n * rs) * gkn_ref[...]).astype(kf_ref.dtype)
            kf_ref[h, :, LANE:HEAD_PAD] = _rope_fwd((kr * rs) * gkr_ref[...], c_v, s1_v, s2_v).astype(kf_ref.dtype)
            vf_ref[h, :, 0:V_DIM] = kv_ref[:, h * 2 * LANE + LANE:(h + 1) * 2 * LANE].astype(vf_ref.dtype)
            vf_ref[h, :, V_DIM:] = jnp.ones((t, V_DIM), vf_ref.dtype)

    hspec = lambda w: pl.BlockSpec((N_HEADS, t, w), lambda i: (0, i, 0))
    whole = lambda a: pl.BlockSpec(a.shape, lambda i: (0, 0))
    return pl.pallas_call(
        body, name=name, grid=(s // t,),
        in_specs=[_rowspec(t, Q_LORA, SEG_QL[0] // Q_LORA), _rowspec(t, KV_LORA, SEG_KVL[0] // KV_LORA),
                  _rowspec(t, LANE, SEG_KR[0] // LANE), whole(w_q_up), whole(w_kv_up),
                  _vecspec(Q_LORA), _vecspec(KV_LORA),
                  _rowspec(t, LANE), _rowspec(t, LANE), _rowspec(t, LANE),
                  _vecspec(LANE), _vecspec(LANE), _vecspec(LANE), _vecspec(LANE)],
        out_specs=[_rowspec(t, Q_LORA), _rowspec(t, KV_LORA), _rowspec(t, wide), _rowspec(t, wide),
                   hspec(HEAD_PAD), hspec(HEAD_PAD), hspec(2 * V_DIM)],
        out_shape=[_sds((s, Q_LORA), MXU_DTYPE), _sds((s, KV_LORA), MXU_DTYPE), _sds((s, wide), F32),
                   _sds((s, wide), F32), _sds((N_HEADS, s, HEAD_PAD), MXU_DTYPE),
                   _sds((N_HEADS, s, HEAD_PAD), MXU_DTYPE), _sds((N_HEADS, s, 2 * V_DIM), MXU_DTYPE)],
        compiler_params=_cp(("parallel",)),
    )(z, z, z, w_q_up, w_kv_up, g_ql, g_kvl, c_t, s1_t, s2_t, gqn, gqr, gkn, gkr)


def _causal_mask(t):
    row = lax.broadcasted_iota(jnp.int32, (t, t), 0)
    col = lax.broadcasted_iota(jnp.int32, (t, t), 1)
    return col <= row


NEG = -1e30


def _flash_fwd(qf, kf, va, *, name):
    nh, s, dk = qf.shape
    dv = va.shape[-1] // 2
    t = min(ATT_T, s)
    n = s // t
    assert dv == LANE and t % LANE == 0

    def body(q_ref, k_ref, v_ref, o_ref, lse_ref, m_s, acc_s, s_buf):
        i = pl.program_id(1)
        m_s[...] = jnp.full(m_s.shape, NEG, F32)
        acc_s[...] = jnp.zeros(acc_s.shape, F32)

        def rows_of(j):
            return pl.ds(pl.multiple_of(j * t, t), t)

        def scores(qi, j):
            return lax.dot_general(q_ref[0, rows_of(qi), :], k_ref[0, rows_of(j), :], (((1,), (1,)), ((), ())),
                                   preferred_element_type=F32)

        def consume(j, slot, masked):
            sc = s_buf[slot]
            if masked:
                sc = jnp.where(_causal_mask(t), sc, NEG)
            m_prev = m_s[...]
            m_new = jnp.maximum(m_prev, jnp.max(sc, axis=-1, keepdims=True))
            alpha = jnp.exp(m_prev - m_new)
            p = jnp.exp(sc - jnp.tile(m_new, (1, t // LANE)))
            acc_s[...] = jnp.tile(alpha, (1, 2)) * acc_s[...] + jnp.dot(
                p.astype(MXU_DTYPE), v_ref[0, rows_of(j), :], preferred_element_type=F32)
            m_s[...] = m_new

        nxt = jnp.minimum(i + 1, n - 1)

        @pl.when(i == 0)
        def _():
            s_buf[2] = scores(0, 0)
            consume(0, 2, True)
            s_buf[2] = scores(nxt, 0)

        @pl.when(i > 0)
        def _():
            s_buf[1] = scores(i, 1)
            consume(0, 2, False)

            def pair(a, carry):
                s_buf[0] = scores(i, 2 * a + 2)
                consume(2 * a + 1, 1, False)
                s_buf[1] = scores(i, 2 * a + 3)
                consume(2 * a + 2, 0, False)
                return carry

            lax.fori_loop(0, (i - 1) // 2, pair, 0)

            @pl.when(i % 2 == 1)
            def _():
                s_buf[2] = scores(nxt, 0)
                consume(i, 1, True)

            @pl.when(i % 2 == 0)
            def _():
                s_buf[0] = scores(i, i)
                consume(i - 1, 1, False)
                s_buf[2] = scores(nxt, 0)
                consume(i, 0, True)

        den = acc_s[:, dv:]
        o_ref[...] = acc_s[:, :dv] / den
        lse_ref[0] = m_s[...] + jnp.log(den)

    head = lambda h, i: (h, 0, 0)
    return pl.pallas_call(
        body, name=name, grid=(nh, n),
        in_specs=[pl.BlockSpec((1, s, dk), head), pl.BlockSpec((1, s, dk), head), pl.BlockSpec((1, s, 2 * dv), head)],
        out_specs=[pl.BlockSpec((t, dv), lambda h, i: (i, h)),
                   pl.BlockSpec((1, t, LANE), lambda h, i: (h, i, 0))],
        out_shape=[_sds((s, nh * dv), F32), _sds((nh, s, LANE), F32)],
        scratch_shapes=[pltpu.VMEM((t, LANE), F32), pltpu.VMEM((t, 2 * dv), F32), pltpu.VMEM((3, t, t), F32)],
        compiler_params=_cp(("arbitrary", "arbitrary")),
    )(qf, kf, va)


def _shifted_copies(ext_ref):
    rows = ext_ref.shape[1] - 8
    for s in range(1, 8):
        ext_ref[s, 0:rows, :] = ext_ref[0, s:s + rows, :]


def _windows(ext_ref, offsets, t_rows, lane0, lanes):
    for s in range(8):
        group = [o for o in offsets if o % 8 == s]
        if not group:
            continue
        lo, hi = min(group) - s, max(group) - s
        wide = ext_ref[s, pl.ds(lo, hi - lo + t_rows), lane0:lane0 + lanes]
        for o in group:
            yield o, wide[o - s - lo:o - s - lo + t_rows]


def _dw_taps(ext_ref, w_ref, row0, t_rows, lane0, lanes, first_off):
    acc = None
    for off, win in _windows(ext_ref, [row0 + first_off + k for k in range(CONV_K)], t_rows, lane0, lanes):
        k = off - row0 - first_off
        term = w_ref[k:k + 1, lane0:lane0 + lanes] * win
        acc = term if acc is None else acc + term
    return acc


CONV_RC = 32
CONV_LC = 256


def _conv_fwd(z, glu_b, dw_w, dw_b, ln_g, ln_b, w_pw, *, name):
    s = z.shape[0]
    t = min(CONV_T, s)
    c2 = 2 * D_CONV
    hb = t // HALO

    def body(zm_ref, zh_ref, gb_ref, w_ref, wb_ref, g_ref, b_ref, wpw_ref, u1_ref, u3_ref, u4_ref, ext):
        i = pl.program_id(0)

        def glu(zv):
            ci = zv + gb_ref[...]
            return ci[:, :D_CONV] * jax.nn.sigmoid(ci[:, D_CONV:])

        ext[0, HALO:, :] = glu(zm_ref[...])
        ext[0, 0:HALO, :] = jnp.where(i > 0, glu(zh_ref[...]), 0.0)
        _shifted_copies(ext)
        for rc in range(0, t, CONV_RC):
            for lc in range(0, D_CONV, CONV_LC):
                acc = _dw_taps(ext, w_ref, rc, CONV_RC, lc, CONV_LC, HALO - (CONV_K - 1))
                u1_ref[rc:rc + CONV_RC, lc:lc + CONV_LC] = acc + wb_ref[:, lc:lc + CONV_LC]
        u1 = u1_ref[...]
        mu = jnp.mean(u1, axis=-1, keepdims=True)
        cen = u1 - mu
        var = jnp.mean(cen * cen, axis=-1, keepdims=True)
        u2 = (cen * lax.rsqrt(var + EPS)) * g_ref[...] + b_ref[...]
        u3_ref[...] = _silu(u2).astype(u3_ref.dtype)
        u4_ref[...] = jnp.dot(u3_ref[...], wpw_ref[...], preferred_element_type=F32)

    return pl.pallas_call(
        body, name=name, grid=(s // t,),
        in_specs=[_rowspec(t, c2), pl.BlockSpec((HALO, c2), lambda i: (jnp.maximum(i * hb - 1, 0), 0)),
                  _vecspec(c2), pl.BlockSpec((HALO, D_CONV), lambda i: (0, 0)), _vecspec(D_CONV),
                  _vecspec(D_CONV), _vecspec(D_CONV), pl.BlockSpec((D_CONV, D_CONV), lambda i: (0, 0))],
        out_specs=[_rowspec(t, D_CONV), _rowspec(t, D_CONV), _rowspec(t, D_CONV)],
        out_shape=[_sds((s, D_CONV), F32), _sds((s, D_CONV), MXU_DTYPE), _sds((s, D_CONV), F32)],
        scratch_shapes=[pltpu.VMEM((8, t + HALO, D_CONV), F32)],
        compiler_params=_cp(("parallel",)),
    )(z, z, glu_b, dw_w, dw_b, ln_g, ln_b, w_pw)


def _gate_cat(o, z, u4m, b_pw, *, name):
    s = o.shape[0]
    t = min(2 * ROW_T, s)

    def body(o_ref, mg_ref, u4_ref, cg_ref, b_ref, cat_ref):
        cat_ref[:, :D_MLA] = (o_ref[...] * _silu(mg_ref[...])).astype(cat_ref.dtype)
        cat_ref[:, D_MLA:] = ((u4_ref[...] + b_ref[...]) * _silu(cg_ref[...])).astype(cat_ref.dtype)

    return pl.pallas_call(
        body, name=name, grid=(s // t,),
        in_specs=[_rowspec(t, D_MLA), _rowspec(t, D_MLA, SEG_MG[0] // D_MLA), _rowspec(t, D_CONV),
                  _rowspec(t, D_CONV, SEG_CG[0] // D_CONV), _vecspec(D_CONV)],
        out_specs=_rowspec(t, D_MLA + D_CONV), out_shape=_sds((s, D_MLA + D_CONV), MXU_DTYPE),
        compiler_params=_cp(("parallel",)),
    )(o, z, u4m, z, b_pw)


def _gated_residual_bwd(gx, y_ref, gate_ref, dy_ref, dgate_ref):
    dy_ref[...] = (gx * gate_ref[...]).astype(dy_ref.dtype)
    dgate_ref[...] += _colsum(gx * y_ref[...])


def _loss_head(xf, target, y, gate, *, name):
    s, d = xf.shape
    t = min(2 * ROW_T, s)

    def body(x_ref, t_ref, y_ref, gate_ref, gx_ref, loss_ref, dy_ref, dgate_ref):
        @pl.when(pl.program_id(0) == 0)
        def _():
            loss_ref[...] = jnp.zeros(loss_ref.shape, F32)
            dgate_ref[...] = jnp.zeros(dgate_ref.shape, F32)

        err = x_ref[...] - t_ref[...]
        gx = err * (1.0 / d)
        gx_ref[...] = gx
        loss_ref[...] += 0.5 * jnp.sum(_lanesum(err * err) * (1.0 / d), axis=0, keepdims=True)
        _gated_residual_bwd(gx, y_ref, gate_ref, dy_ref, dgate_ref)

    return pl.pallas_call(
        body, name=name, grid=(s // t,),
        in_specs=[_rowspec(t, d), _rowspec(t, d), _rowspec(t, d), _vecspec(d)],
        out_specs=[_rowspec(t, d), pl.BlockSpec((1, 1), lambda i: (0, 0)), _rowspec(t, d), _vecspec(d)],
        out_shape=[_sds((s, d), F32), _sds((1, 1), F32), _sds((s, d), MXU_DTYPE), _sds((1, d), F32)],
        compiler_params=_cp(("arbitrary",)),
    )(xf, target, y, gate)


def _acc_init(refs):
    @pl.when(pl.program_id(0) == 0)
    def _():
        for r in refs:
            r[...] = jnp.zeros(r.shape, r.dtype)


def _gate_bwd(dy, w_out, o, z, u4m, b_pw, *, name):
    s, d = dy.shape
    t = min(2 * ROW_T, s)
    gates = D_MLA + D_CONV
    assert SEG_CG[0] == SEG_MG[0] + D_MLA and SEG_MG[0] % gates == 0

    def body(dy_ref, w_ref, o_ref, mg_ref, u4_ref, cg_ref, b_ref,
             do_ref, delta_ref, du4_ref, gb_ref, dz_ref):
        _acc_init([gb_ref])
        dcat = lax.dot_general(dy_ref[...], w_ref[...], (((1,), (1,)), ((), ())), preferred_element_type=F32)
        dm, ov, mg = dcat[:, :D_MLA], o_ref[...], mg_ref[...]
        do = dm * _silu(mg)
        do_ref[...] = do.astype(do_ref.dtype)
        dz_ref[:, :D_MLA] = (dm * ov * _dsilu(mg)).astype(dz_ref.dtype)
        prod = do * ov
        for h in range(N_HEADS):
            delta_ref[h] = _lanesum(prod[:, h * V_DIM:(h + 1) * V_DIM])
        dc, cg = dcat[:, D_MLA:], cg_ref[...]
        du4 = dc * _silu(cg)
        du4_ref[...] = du4.astype(du4_ref.dtype)
        dz_ref[:, D_MLA:] = (dc * (u4_ref[...] + b_ref[...]) * _dsilu(cg)).astype(dz_ref.dtype)
        gb_ref[...] += _colsum(du4)

    return pl.pallas_call(
        body, name=name, grid=(s // t,),
        in_specs=[_rowspec(t, d), pl.BlockSpec((gates, d), lambda i: (0, 0)), _rowspec(t, D_MLA),
                  _rowspec(t, D_MLA, SEG_MG[0] // D_MLA), _rowspec(t, D_CONV),
                  _rowspec(t, D_CONV, SEG_CG[0] // D_CONV), _vecspec(D_CONV)],
        out_specs=[_rowspec(t, D_MLA), pl.BlockSpec((N_HEADS, t, 1), lambda i: (0, i, 0)),
                   _rowspec(t, D_CONV), _vecspec(D_CONV), _rowspec(t, gates, SEG_MG[0] // gates)],
        out_shape=[_sds((s, D_MLA), MXU_DTYPE), _sds((N_HEADS, s, 1), F32),
                   _sds((s, D_CONV), MXU_DTYPE), _sds((1, D_CONV), F32), _sds((s, IN_PAD), MXU_DTYPE)],
        compiler_params=_cp(("arbitrary",)),
    )(dy, w_out, o, z, u4m, z, b_pw)


def _conv_bwd(du3, u1, z, dz, glu_b, dw_w, ln_g, ln_b, *, name):
    s = z.shape[0]
    t = min(CONV_T, s)
    c2 = 2 * D_CONV
    hb = t // HALO
    n_blk = s // t
    last_halo = s // HALO - 1

    def body(d3m_ref, d3h_ref, u1m_ref, u1h_ref, zm_ref, zh_ref, gb_ref, w_ref, g_ref, b_ref, dz_in_ref,
             dci_ref, gg_ref, gbn_ref, gwb_ref, ggb_ref, gw_ref, dext, uext, du0_s, gw_acc):
        i = pl.program_id(0)
        _acc_init([gg_ref, gbn_ref, gwb_ref, ggb_ref, gw_acc])

        def ln_bwd(d3, u1v):
            mu = jnp.mean(u1v, axis=-1, keepdims=True)
            cen = u1v - mu
            rstd = lax.rsqrt(jnp.mean(cen * cen, axis=-1, keepdims=True) + EPS)
            uh = cen * rstd
            d2 = d3 * _dsilu(uh * g_ref[...] + b_ref[...])
            dh = d2 * g_ref[...]
            d1 = rstd * (dh - jnp.mean(dh, axis=-1, keepdims=True) - uh * jnp.mean(dh * uh, axis=-1, keepdims=True))
            return d1, d2, uh

        d1, d2, uh = ln_bwd(d3m_ref[...], u1m_ref[...])
        gg_ref[...] += _colsum(d2 * uh)
        gbn_ref[...] += _colsum(d2)
        gwb_ref[...] += _colsum(d1)
        dext[0, 0:t, :] = d1
        d1h, _, _ = ln_bwd(d3h_ref[...], u1h_ref[...])
        dext[0, t:, :] = jnp.where(i < n_blk - 1, d1h, 0.0)
        _shifted_copies(dext)

        def glu_parts(zv):
            ci = zv + gb_ref[...]
            return ci[:, :D_CONV], jax.nn.sigmoid(ci[:, D_CONV:])

        val, sg = glu_parts(zm_ref[...])
        uext[0, HALO:, :] = val * sg
        valh, sgh = glu_parts(zh_ref[...])
        uext[0, 0:HALO, :] = jnp.where(i > 0, valh * sgh, 0.0)
        _shifted_copies(uext)

        for rc in range(0, t, CONV_RC):
            for lc in range(0, D_CONV, CONV_LC):
                acc = None
                for off, win in _windows(dext, [rc + k for k in range(CONV_K)], CONV_RC, lc, CONV_LC):
                    k = (CONV_K - 1) - (off - rc)
                    term = w_ref[k:k + 1, lc:lc + CONV_LC] * win
                    acc = term if acc is None else acc + term
                du0_s[rc:rc + CONV_RC, lc:lc + CONV_LC] = acc
                dchunk = dext[0, rc:rc + CONV_RC, lc:lc + CONV_LC]
                first = rc + HALO - (CONV_K - 1)
                for off, win in _windows(uext, [first + k for k in range(CONV_K)], CONV_RC, lc, CONV_LC):
                    k = off - first
                    pr = dchunk * win
                    part = pr[0:8]
                    for r8 in range(8, CONV_RC, 8):
                        part = part + pr[r8:r8 + 8]
                    gw_acc[k, :, lc:lc + CONV_LC] += part

        du0 = du0_s[...]
        dval = du0 * sg
        dgt = du0 * val * sg * (1.0 - sg)
        dci_ref[:, :D_CONV] = dval.astype(dci_ref.dtype)
        dci_ref[:, D_CONV:] = dgt.astype(dci_ref.dtype)
        ggb_ref[:, :D_CONV] += _colsum(dval)
        ggb_ref[:, D_CONV:] += _colsum(dgt)

        @pl.when(i == n_blk - 1)
        def _():
            gw_ref[...] = jnp.sum(gw_acc[...], axis=1)

    halo_next = lambda w: pl.BlockSpec((HALO, w), lambda i: (jnp.minimum((i + 1) * hb, last_halo), 0))
    return pl.pallas_call(
        body, name=name, grid=(n_blk,),
        in_specs=[_rowspec(t, D_CONV), halo_next(D_CONV), _rowspec(t, D_CONV), halo_next(D_CONV),
                  _rowspec(t, c2), pl.BlockSpec((HALO, c2), lambda i: (jnp.maximum(i * hb - 1, 0), 0)),
                  _vecspec(c2), pl.BlockSpec((HALO, D_CONV), lambda i: (0, 0)), _vecspec(D_CONV), _vecspec(D_CONV),
                  _ANY],
        out_specs=[_rowspec(t, c2, SEG_CI[0] // c2), _vecspec(D_CONV), _vecspec(D_CONV), _vecspec(D_CONV),
                   _vecspec(c2), pl.BlockSpec((HALO, D_CONV), lambda i: (0, 0))],
        out_shape=[_sds(dz.shape, dz.dtype), _sds((1, D_CONV), F32), _sds((1, D_CONV), F32), _sds((1, D_CONV), F32),
                   _sds((1, c2), F32), _sds((HALO, D_CONV), F32)],
        scratch_shapes=[pltpu.VMEM((8, t + HALO, D_CONV), F32), pltpu.VMEM((8, t + HALO, D_CONV), F32),
                        pltpu.VMEM((t, D_CONV), F32), pltpu.VMEM((HALO, 8, D_CONV), F32)],
        input_output_aliases={10: 0},
        compiler_params=_cp(("arbitrary",)),
    )(du3, du3, u1, u1, z, z, glu_b, dw_w, ln_g, ln_b, dz)


def _flash_bwd(qf, kf, va, do, lse_t, delta_t, *, name):
    nh, s, dk = qf.shape
    dv = va.shape[-1] // 2
    t = min(ATT_T, s)
    n = s // t
    nt = (((1,), (1,)), ((), ()))
    tn = (((0,), (0,)), ((), ()))

    def body(q_ref, do_ref, lse_ref, dl_ref, k_ref, v_ref, dq_ref, dk_ref, dv_ref,
             dk_s, dv_s, st_buf, dpt_buf):
        n_un = pl.program_id(1)
        j = n - 1 - n_un
        nxt = jnp.maximum(j - 1, 0)

        @pl.when(n_un == 0)
        def _():
            dq_ref[...] = jnp.zeros(dq_ref.shape, F32)

        dk_s[...] = jnp.zeros(dk_s.shape, F32)
        dv_s[...] = jnp.zeros(dv_s.shape, F32)

        def rows_at(blk):
            return pl.ds(pl.multiple_of(blk * t, t), t)

        def rows_of(b):
            return rows_at(n - 1 - b)

        k = k_ref[0, rows_at(j), :]

        def produce(kj, b, slot):
            rows = rows_of(b)
            st_buf[slot] = lax.dot_general(k_ref[0, rows_at(kj), :], q_ref[0, rows, :], nt,
                                           preferred_element_type=F32)
            dpt_buf[slot] = lax.dot_general(v_ref[0, rows_at(kj), 0:dv], do_ref[rows, :], nt,
                                            preferred_element_type=F32)

        def consume(b, slot, masked):
            i = n - 1 - b
            rows = rows_of(b)
            q, dov = q_ref[0, rows, :], do_ref[rows, :]
            pt = jnp.exp(st_buf[slot] - lse_ref[0, i])
            if masked:
                key = lax.broadcasted_iota(jnp.int32, (t, t), 0)
                qry = lax.broadcasted_iota(jnp.int32, (t, t), 1)
                pt = jnp.where(key <= qry, pt, 0.0)
            dv_s[...] += jnp.dot(pt.astype(MXU_DTYPE), dov, preferred_element_type=F32)
            dst = (pt * (dpt_buf[slot] - dl_ref[0, i])).astype(MXU_DTYPE)
            dk_s[...] += jnp.dot(dst, q, preferred_element_type=F32)
            dq_ref[0, rows, :] += lax.dot_general(dst, k, tn, preferred_element_type=F32)

        @pl.when(n_un == 0)
        def _():
            produce(j, 0, 2)
            consume(0, 2, True)
            produce(nxt, 0, 2)

        @pl.when(n_un > 0)
        def _():
            produce(j, 1, 1)
            consume(0, 2, False)

            def pair(a, carry):
                produce(j, 2 * a + 2, 0)
                consume(2 * a + 1, 1, False)
                produce(j, 2 * a + 3, 1)
                consume(2 * a + 2, 0, False)
                return carry

            lax.fori_loop(0, (n_un - 1) // 2, pair, 0)

            @pl.when(n_un % 2 == 1)
            def _():
                produce(nxt, 0, 2)
                consume(n_un, 1, True)

            @pl.when(n_un % 2 == 0)
            def _():
                produce(j, n_un, 0)
                consume(n_un - 1, 1, False)
                produce(nxt, 0, 2)
                consume(n_un, 0, True)

        dk_ref[0] = dk_s[...]
        dv_ref[0] = dv_s[...]

    head = lambda h, j: (h, 0, 0)
    rowv = pl.BlockSpec((1, n, 1, t), lambda h, j: (h, 0, 0, 0))
    return pl.pallas_call(
        body, name=name, grid=(nh, n),
        in_specs=[pl.BlockSpec((1, s, dk), head),
                  pl.BlockSpec((s, dv), lambda h, j: (0, h)),
                  rowv, rowv,
                  pl.BlockSpec((1, s, dk), head),
                  pl.BlockSpec((1, s, 2 * dv), head)],
        out_specs=[pl.BlockSpec((1, s, dk), head),
                   pl.BlockSpec((1, t, dk), lambda h, g: (h, n - 1 - g, 0)),
                   pl.BlockSpec((1, t, dv), lambda h, g: (h, n - 1 - g, 0))],
        out_shape=[_sds((nh, s, dk), F32), _sds((nh, s, dk), F32), _sds((nh, s, dv), F32)],
        scratch_shapes=[pltpu.VMEM((t, dk), F32), pltpu.VMEM((t, dv), F32),
                        pltpu.VMEM((3, t, t), F32), pltpu.VMEM((3, t, t), F32)],
        compiler_params=_cp(("arbitrary", "arbitrary")),
    )(qf, do, lse_t, delta_t, kf, va)


def _mla_bwd(dqf, dkf, dvf, q_raw, kv, z, dz, qn, kn, w_q_up, w_kv_up, g_ql, g_kvl, c_t, s1_t, s2_t,
             gqn, gqr, gkn, gkr, *, name):
    s = q_raw.shape[0]
    t = min(ROW_T, s)
    scale = 1.0 / math.sqrt(QK_DIM)
    o_ql, o_kvl, o_kr = (seg[0] - SEG_LAT[0] for seg in (SEG_QL, SEG_KVL, SEG_KR))
    tn = (((0,), (0,)), ((), ()))
    nt = (((1,), (1,)), ((), ()))

    def body(dq_ref, dk_ref, dv_ref, q_ref, kv_ref, kr_ref, ql_ref, kvl_ref, qn_ref, kn_ref, wq_ref, wkv_ref,
             gq_ref, gk_ref, c_ref, s1_ref, s2_ref, gqn_ref, gqr_ref, gkn_ref, gkr_ref, dz_in_ref,
             dz_ref, gwq_ref, gwkv_ref, ggq_ref, ggk_ref, gql_ref, gkvl_ref, dqr_ref, dkv_ref):
        _acc_init([gwq_ref, gwkv_ref, ggq_ref, ggk_ref, gql_ref, gkvl_ref])
        c_v, s1_v, s2_v = c_ref[...], s1_ref[...], s2_ref[...]
        kr = kr_ref[...]
        kr_ss = _lanesum(kr * kr)
        dkr = jnp.zeros(kr.shape, F32)
        ggq_n = ggq_r = ggk_n = ggk_r = jnp.zeros((1, LANE), F32)

        def norm_bwd(n, r, rs, dyn, dyr, gn, gr):
            nh_, rh_ = n * rs, r * rs
            dnh, drh = dyn * gn, dyr * gr
            dot = (_lanesum(dnh * nh_) + _lanesum(drh * rh_)) * (1.0 / QK_DIM)
            return rs * (dnh - nh_ * dot), rs * (drh - rh_ * dot), _colsum(dyn * nh_), _colsum(dyr * rh_)

        for h in range(N_HEADS):
            n = q_ref[:, h * LANE:(h + 1) * LANE]
            r = q_ref[:, N_HEADS * LANE + h * LANE:N_HEADS * LANE + (h + 1) * LANE]
            rs = lax.rsqrt((_lanesum(n * n) + _lanesum(r * r)) * (1.0 / QK_DIM) + EPS)
            dyn = dq_ref[h, :, 0:LANE] * scale
            dyr = _rope_bwd(dq_ref[h, :, LANE:HEAD_PAD] * scale, c_v, s1_v, s2_v)
            dn, dr, g_n, g_r = norm_bwd(n, r, rs, dyn, dyr, gqn_ref[...], gqr_ref[...])
            dqr_ref[:, h * LANE:(h + 1) * LANE] = dn.astype(dqr_ref.dtype)
            dqr_ref[:, N_HEADS * LANE + h * LANE:N_HEADS * LANE + (h + 1) * LANE] = dr.astype(dqr_ref.dtype)
            ggq_n, ggq_r = ggq_n + g_n, ggq_r + g_r

            n = kv_ref[:, h * 2 * LANE:h * 2 * LANE + LANE]
            rs = lax.rsqrt((_lanesum(n * n) + kr_ss) * (1.0 / QK_DIM) + EPS)
            dyn = dk_ref[h, :, 0:LANE]
            dyr = _rope_bwd(dk_ref[h, :, LANE:HEAD_PAD], c_v, s1_v, s2_v)
            dn, dr, g_n, g_r = norm_bwd(n, kr, rs, dyn, dyr, gkn_ref[...], gkr_ref[...])
            dkv_ref[:, h * 2 * LANE:h * 2 * LANE + LANE] = dn.astype(dkv_ref.dtype)
            dkv_ref[:, h * 2 * LANE + LANE:(h + 1) * 2 * LANE] = dv_ref[h].astype(dkv_ref.dtype)
            dkr = dkr + dr
            ggk_n, ggk_r = ggk_n + g_n, ggk_r + g_r

        ggq_ref[:, 0:LANE] += ggq_n
        ggq_ref[:, LANE:] += ggq_r
        ggk_ref[:, 0:LANE] += ggk_n
        ggk_ref[:, LANE:] += ggk_r

        for d_ref, x_ref, w_ref, gw_ref, src, g_ref, off, gg_ref in (
                (dqr_ref, qn_ref, wq_ref, gwq_ref, ql_ref, gq_ref, o_ql, gql_ref),
                (dkv_ref, kn_ref, wkv_ref, gwkv_ref, kvl_ref, gk_ref, o_kvl, gkvl_ref)):
            dup = d_ref[...]
            gw_ref[...] += lax.dot_general(x_ref[...], dup, tn, preferred_element_type=F32)
            dy = lax.dot_general(dup, w_ref[...], nt, preferred_element_type=F32)
            v = src[...]
            r = lax.rsqrt(jnp.mean(v * v, axis=-1, keepdims=True) + EPS)
            vh = v * r
            dvh = dy * g_ref[...]
            dz_ref[:, off:off + v.shape[1]] = (
                r * (dvh - vh * jnp.mean(dvh * vh, axis=-1, keepdims=True))).astype(dz_ref.dtype)
            gg_ref[...] += _colsum(dy * vh)
        dz_ref[:, o_kr:o_kr + LANE] = dkr.astype(dz_ref.dtype)
        dz_ref[:, o_kr + LANE:] = jnp.zeros((t, SEG_LAT[1] - o_kr - LANE), dz_ref.dtype)

    hspec = lambda w: pl.BlockSpec((N_HEADS, t, w), lambda i: (0, i, 0))
    whole = lambda a: pl.BlockSpec(a.shape, lambda i: (0, 0))
    wide = 2 * N_HEADS * LANE
    return pl.pallas_call(
        body, name=name, grid=(s // t,),
        in_specs=[hspec(HEAD_PAD), hspec(HEAD_PAD), hspec(V_DIM), _rowspec(t, wide), _rowspec(t, wide),
                  _rowspec(t, LANE, SEG_KR[0] // LANE), _rowspec(t, Q_LORA, SEG_QL[0] // Q_LORA),
                  _rowspec(t, KV_LORA, SEG_KVL[0] // KV_LORA), _rowspec(t, Q_LORA), _rowspec(t, KV_LORA),
                  whole(w_q_up), whole(w_kv_up), _vecspec(Q_LORA), _vecspec(KV_LORA),
                  _rowspec(t, LANE), _rowspec(t, LANE), _rowspec(t, LANE),
                  _vecspec(LANE), _vecspec(LANE), _vecspec(LANE), _vecspec(LANE), _ANY],
        out_specs=[_rowspec(t, SEG_LAT[1], SEG_LAT[0] // SEG_LAT[1]), whole(w_q_up), whole(w_kv_up),
                   _vecspec(2 * LANE), _vecspec(2 * LANE), _vecspec(Q_LORA), _vecspec(KV_LORA)],
        out_shape=[_sds(dz.shape, dz.dtype), _sds(w_q_up.shape, F32), _sds(w_kv_up.shape, F32),
                   _sds((1, 2 * LANE), F32), _sds((1, 2 * LANE), F32), _sds((1, Q_LORA), F32),
                   _sds((1, KV_LORA), F32)],
        scratch_shapes=[pltpu.VMEM((t, wide), MXU_DTYPE), pltpu.VMEM((t, wide), MXU_DTYPE)],
        input_output_aliases={21: 0},
        compiler_params=_cp(("arbitrary",)),
    )(dqf, dkf, dvf, q_raw, kv, z, z, z, qn, kn, w_q_up, w_kv_up, g_ql, g_kvl, c_t, s1_t, s2_t,
      gqn, gqr, gkn, gkr, dz)


def _prenorm_bwd(dh, x, gxo, g, sc1p, below=None, *, name):
    s, d = x.shape
    t = min(2 * ROW_T if below is None else ROW_T, s)
    nb = 0 if below is None else 2

    def body(dh_ref, x_ref, gx_ref, g_ref, sc_ref, *rest):
        dx_ref, dsh_ref, dsc_ref, gg_ref = rest[nb:nb + 4]
        _acc_init([dsh_ref, dsc_ref, gg_ref])
        xv, dhv = x_ref[...], dh_ref[...]
        r = lax.rsqrt(jnp.mean(xv * xv, axis=-1, keepdims=True) + EPS)
        xn = xv * r
        dsh_ref[...] += _colsum(dhv)
        dsc_ref[...] += _colsum(dhv * (xn * g_ref[...]))
        dm = dhv * sc_ref[...]
        gg_ref[...] += _colsum(dm * xn)
        dxn = dm * g_ref[...]
        dx = gx_ref[...] + r * (dxn - xn * jnp.mean(dxn * xn, axis=-1, keepdims=True))
        dx_ref[...] = dx
        if below is not None:
            _acc_init([rest[nb + 5]])
            _gated_residual_bwd(dx, rest[0], rest[1], rest[nb + 4], rest[nb + 5])

    vec_out = [_vecspec(d), _vecspec(d), _vecspec(d)]
    vec_shape = [_sds((1, d), F32)] * 3
    return pl.pallas_call(
        body, name=name, grid=(s // t,),
        in_specs=[_rowspec(t, d), _rowspec(t, d), _rowspec(t, d), _vecspec(d), _vecspec(d)]
        + ([_rowspec(t, d), _vecspec(d)] if below is not None else []),
        out_specs=[_rowspec(t, d)] + vec_out + ([_rowspec(t, d), _vecspec(d)] if below is not None else []),
        out_shape=[_sds((s, d), F32)] + vec_shape
        + ([_sds((s, d), MXU_DTYPE), _sds((1, d), F32)] if below is not None else []),
        compiler_params=_cp(("arbitrary",)),
    )(dh, x, gxo, g, sc1p, *(below if below is not None else ()))


def _ada_fwd(c_all, ada_w, ada_b_cols, *, name):
    nl, d, cols = ada_w.shape

    def body(c_ref, w_ref, b_ref, o_ref):
        ca = _silu(c_ref[...]).astype(MXU_DTYPE)
        o_ref[0] = jnp.dot(ca, w_ref[0].astype(MXU_DTYPE), preferred_element_type=F32) + b_ref[0]

    return pl.pallas_call(
        body, name=name, grid=(nl,),
        in_specs=[pl.BlockSpec((N_DEV, d), lambda l: (0, 0)), pl.BlockSpec((1, d, cols), lambda l: (l, 0, 0)),
                  pl.BlockSpec((1, 1, cols), lambda l: (l, 0, 0))],
        out_specs=pl.BlockSpec((1, N_DEV, cols), lambda l: (l, 0, 0)),
        out_shape=_sds((nl, N_DEV, cols), F32),
        compiler_params=_cp(("parallel",)),
    )(c_all, ada_w, ada_b_cols)


def _ada_bwd(c_all_t, dmod_cols, *, name):
    nl, _, cols = dmod_cols.shape
    d = c_all_t.shape[0]

    def body(c_ref, dm_ref, o_ref):
        ca = _silu(c_ref[...]).astype(MXU_DTYPE)
        o_ref[0] = jnp.dot(ca, dm_ref[0].astype(MXU_DTYPE), preferred_element_type=F32)

    return pl.pallas_call(
        body, name=name, grid=(nl,),
        in_specs=[pl.BlockSpec((d, N_DEV), lambda l: (0, 0)), pl.BlockSpec((1, N_DEV, cols), lambda l: (l, 0, 0))],
        out_specs=pl.BlockSpec((1, d, cols), lambda l: (l, 0, 0)),
        out_shape=_sds((nl, d, cols), F32),
        compiler_params=_cp(("parallel",)),
    )(c_all_t, dmod_cols)


def _adamw_math(g, w, m, v):
    mn = ADAM_B1 * m + (1.0 - ADAM_B1) * g
    vn = ADAM_B2 * v + (1.0 - ADAM_B2) * (g * g)
    m_hat = mn / (1.0 - ADAM_B1 ** ADAM_STEP)
    v_hat = vn / (1.0 - ADAM_B2 ** ADAM_STEP)
    return -ADAM_LR * (m_hat / (jnp.sqrt(v_hat) + ADAM_EPS) + ADAM_WD * w), mn, vn


def _adamw_small(items, *, name):
    n = len(items)
    shapes = [it[1].shape for it in items]
    flat = lambda a, lead: a.reshape(lead + (-1, a.shape[-1]))
    operands = []
    for gp, w, m, v in items:
        operands += [flat(gp, (gp.shape[0],)), flat(w, ()), flat(m, ()), flat(v, ())]
    nparts = [it[0].shape[0] for it in items]

    def body(*refs):
        ins, outs = refs[:4 * n], refs[4 * n:]
        for i in range(n):
            g_ref, w_ref, m_ref, v_ref = ins[4 * i:4 * i + 4]
            g = g_ref[0].astype(F32)
            for p in range(1, nparts[i]):
                g = g + g_ref[p].astype(F32)
            outs[4 * i][...] = g
            outs[4 * i + 1][...], outs[4 * i + 2][...], outs[4 * i + 3][...] = _adamw_math(
                g, w_ref[...], m_ref[...], v_ref[...])

    out_shape = []
    for it in items:
        out_shape += [_sds(flat(it[1], ()).shape, F32)] * 4
    outs = pl.pallas_call(body, name=name, out_shape=out_shape, compiler_params=_cp())(*operands)
    return [tuple(o.reshape(shp) for o in outs[4 * i:4 * i + 4]) for i, shp in enumerate(shapes)]


def _adamw_layer(gparts, w, m, v, layer, prev, *, name):
    shape = w.shape
    nl, cols = shape[0], shape[-1]
    rows = w.size // cols // nl
    npart = gparts.shape[0]
    g3 = gparts.reshape(npart, rows, cols)
    w3, m3, v3 = (a.reshape(nl, rows, cols) for a in (w, m, v))
    fits = [t for t in range(min(rows, 256) // 8 * 8, 7, -8)
            if rows % t == 0 and npart * t * cols * g3.dtype.itemsize <= 2 * 1024 * 1024]
    t = fits[0] if fits else rows
    n_prev = 0 if prev is None else 4

    def body(g_ref, w_ref, m_ref, v_ref, *rest):
        go_ref, d_ref, mo_ref, vo_ref = rest[n_prev:]
        g = g_ref[0].astype(F32)
        for p in range(1, npart):
            g = g + g_ref[p].astype(F32)
        go_ref[0] = g
        d_ref[0], mo_ref[0], vo_ref[0] = _adamw_math(g, w_ref[0], m_ref[0], v_ref[0])

    spec = pl.BlockSpec((1, t, cols), lambda i: (layer, i, 0))
    outs = pl.pallas_call(
        body, name=name, grid=(rows // t,),
        in_specs=[pl.BlockSpec((npart, t, cols), lambda i: (0, i, 0)), spec, spec, spec] + [_ANY] * n_prev,
        out_specs=[spec] * 4, out_shape=[_sds((nl, rows, cols), F32)] * 4,
        input_output_aliases={4 + k: k for k in range(n_prev)},
        compiler_params=_cp(("parallel",)),
    )(g3, w3, m3, v3, *([] if prev is None else [a.reshape(nl, rows, cols) for a in prev]))
    return tuple(o.reshape(shape) for o in outs)


def _adamw(gparts, w, m, v, *, name):
    shape = w.shape
    cols = shape[-1]
    per_layer = isinstance(gparts, (list, tuple))
    nl = shape[0] if per_layer else 1
    rows = w.size // cols // nl
    glist = list(gparts) if per_layer else [gparts]
    npart = glist[0].shape[0]
    glist = [g.reshape(npart, rows, cols) for g in glist]
    w3, m3, v3 = (a.reshape(nl, rows, cols) for a in (w, m, v))
    budget = 2 * 1024 * 1024
    fits = [t for t in range(min(rows, 256) // 8 * 8, 7, -8)
            if rows % t == 0 and npart * t * cols * glist[0].dtype.itemsize <= budget]
    t = fits[0] if fits else rows
    nb = rows // t

    def body(*refs):
        g_refs = refs[:nl]
        w_ref, m_ref, v_ref, go_ref, d_ref, mo_ref, vo_ref, g_s = refs[nl:]
        layer = pl.program_id(0)
        for l in range(nl):
            @pl.when(layer == l)
            def _(l=l):
                g = g_refs[l][0].astype(F32)
                for p in range(1, npart):
                    g = g + g_refs[l][p].astype(F32)
                g_s[...] = g

        g = g_s[...]
        go_ref[0] = g
        d_ref[0], mo_ref[0], vo_ref[0] = _adamw_math(g, w_ref[0], m_ref[0], v_ref[0])

    def g_map(l):
        return lambda layer, i: (0, jnp.where(layer == l, i, jnp.where(layer < l, 0, nb - 1)), 0)

    spec = pl.BlockSpec((1, t, cols), lambda layer, i: (layer, i, 0))
    outs = pl.pallas_call(
        body, name=name, grid=(nl, nb),
        in_specs=[pl.BlockSpec((npart, t, cols), g_map(l)) for l in range(nl)] + [spec, spec, spec],
        out_specs=[spec] * 4, out_shape=[_sds((nl, rows, cols), F32)] * 4,
        scratch_shapes=[pltpu.VMEM((t, cols), F32)],
        compiler_params=_cp(("arbitrary", "arbitrary")),
    )(*glist, w3, m3, v3)
    return tuple(o.reshape(shape) for o in outs)


_ANY = pl.BlockSpec(memory_space=pl.ANY)


def _all_gather(blocks, *, name):
    na = len(blocks)

    def body(*refs):
        x_refs, out_refs = refs[:na], refs[na:2 * na]
        send_sems, recv_sems, local_sems = refs[2 * na:]
        x, y, c = lax.axis_index("x"), lax.axis_index("y"), lax.axis_index("c")
        me, sibling = (x, y, c), (x, y, 1 - c)
        chips = [(1 - x, y), (x, 1 - y), (1 - x, 1 - y)]

        def slot(a, px, py, pc):
            return out_refs[a].at[4 * px + 2 * py + pc]

        def copy(a, k, blk, to, src=None):
            return pltpu.make_async_remote_copy(
                src_ref=slot(a, *blk) if src is None else src, dst_ref=slot(a, *blk),
                send_sem=send_sems.at[7 * a + k], recv_sem=recv_sems.at[7 * a + k],
                device_id=to, device_id_type=MESH_ID)

        mine = [pltpu.make_async_copy(x_refs[a], slot(a, *me), local_sems.at[a]) for a in range(na)]
        for cp in mine:
            cp.start()
        first = []
        for a in range(na):
            first.append(copy(a, 0, me, sibling, src=x_refs[a]))
            first += [copy(a, 1 + j, me, (*chip, c), src=x_refs[a]) for j, chip in enumerate(chips)]
        for cp in first:
            cp.start()
        passed = []
        for a in range(na):
            for j, chip in enumerate(chips):
                copy(a, 1 + j, (*chip, c), me).wait_recv()
                fwd = copy(a, 4 + j, (*chip, c), sibling)
                fwd.start()
                passed.append(fwd)
        for a in range(na):
            copy(a, 0, sibling, me).wait_recv()
            for j, chip in enumerate(chips):
                copy(a, 4 + j, (*chip, 1 - c), me).wait_recv()
        for cp in first + passed:
            cp.wait_send()
        for cp in mine:
            cp.wait()

    outs = pl.pallas_call(
        body, name=name, in_specs=[_ANY] * na, out_specs=[_ANY] * na,
        out_shape=[_sds((N_DEV,) + b.shape, b.dtype) for b in blocks],
        scratch_shapes=[pltpu.SemaphoreType.DMA((7 * na,)), pltpu.SemaphoreType.DMA((7 * na,)),
                        pltpu.SemaphoreType.DMA((na,))],
    )(*blocks)
    return list(outs)


_HBM = pl.BlockSpec(memory_space=pltpu.HBM)
_SEM = pl.BlockSpec(memory_space=pltpu.SEMAPHORE)
_EFFECT = pltpu.SideEffectType.DATAFLOW_SIDE_EFFECTING


def _peers(x, y, c):
    out = []
    for k in range(1, N_DEV):
        out.append((1 - x if k & 4 else x, 1 - y if k & 2 else y, 1 - c if k & 1 else c))
    return out


def _own_slots(srcs, scatter, *, name, after=None):
    na = len(srcs)
    n_extra = 0 if after is None else 1
    me = (4 * lax.axis_index("x") + 2 * lax.axis_index("y") + lax.axis_index("c")).astype(jnp.int32).reshape(1)

    def body(me_ref, *refs):
        in_refs, out_refs = refs[:na], refs[na + n_extra:]
        for a in range(na):
            out_refs[a][0] = in_refs[a][0] if scatter else in_refs[a][...]

    def slot_spec(shard):
        zeros = (0,) * len(shard)
        return pl.BlockSpec((1,) + tuple(shard), lambda i, me_ref: (me_ref[0],) + zeros)

    def whole_spec(shape):
        zeros = (0,) * len(shape)
        return pl.BlockSpec(tuple(shape), lambda i, me_ref: zeros)

    shards = [s.shape[1:] if scatter else s.shape for s in srcs]
    in_specs = [slot_spec(sh) if scatter else whole_spec(sh) for sh in shards] + [_ANY] * n_extra
    outs = pl.pallas_call(
        body, name=name,
        grid_spec=pltpu.PrefetchScalarGridSpec(
            num_scalar_prefetch=1, grid=(1,), in_specs=in_specs, out_specs=[slot_spec(sh) for sh in shards]),
        out_shape=[_sds((N_DEV,) + tuple(sh), s.dtype) for sh, s in zip(shards, srcs)],
        compiler_params=_cp(("arbitrary",)),
    )(me, *srcs, *([] if after is None else [after]))
    return list(outs)


_N_COPIES = dict(scatter=7, gather=7, chips=4, forward=3)


def _exchange_copies(src_refs, land_refs, send_sems, recv_sems, mode):
    x, y, c = lax.axis_index("x"), lax.axis_index("y"), lax.axis_index("c")
    me = 4 * x + 2 * y + c
    nc = _N_COPIES[mode]
    chips = [(1 - x, y), (x, 1 - y), (1 - x, 1 - y)]
    cps = []
    for a in range(len(land_refs)):
        if mode in ("scatter", "gather"):
            plan = [((src_refs[a].at[4 * px + 2 * py + pc] if mode == "scatter" else src_refs[a]),
                     land_refs[a].at[me], (px, py, pc)) for px, py, pc in _peers(x, y, c)]
        elif mode == "chips":
            plan = [(src_refs[a], land_refs[a].at[me], to) for to in [(x, y, 1 - c)] + [(*ch, c) for ch in chips]]
        else:
            plan = [(land_refs[a].at[4 * px + 2 * py + c], land_refs[a].at[4 * px + 2 * py + c], (x, y, 1 - c))
                    for px, py in chips]
        for k, (src, dst, to) in enumerate(plan):
            cps.append(pltpu.make_async_remote_copy(
                src_ref=src, dst_ref=dst, send_sem=send_sems.at[nc * a + k], recv_sem=recv_sems.at[nc * a + k],
                device_id=to, device_id_type=MESH_ID))
    return cps


def _exchange_start(srcs, lands, mode, *, name):
    ns, nz = len(srcs), len(lands)
    nsem = _N_COPIES[mode] * nz

    def body(*refs):
        src_refs, land_refs = refs[:ns], refs[ns:ns + nz]
        send_sems, recv_sems = refs[ns + nz], refs[ns + nz + 1]
        token = refs[-1]
        for cp in _exchange_copies(src_refs, land_refs, send_sems, recv_sems, mode):
            cp.start()
        token[...] = jnp.zeros(token.shape, token.dtype)

    hbm = lambda a: pltpu.HBM(a.shape, a.dtype)
    outs = pl.pallas_call(
        body, name=name,
        out_shape=(pltpu.SemaphoreType.DMA((nsem,)), pltpu.SemaphoreType.DMA((nsem,)),
                   *[hbm(a) for a in srcs], *[hbm(a) for a in lands], _sds((8, LANE), F32)),
        in_specs=[_HBM] * (ns + nz),
        out_specs=(_SEM, _SEM, *[_HBM] * (ns + nz), pl.BlockSpec(memory_space=pltpu.VMEM)),
        input_output_aliases={i: 2 + i for i in range(ns + nz)},
        compiler_params=pltpu.CompilerParams(has_side_effects=_EFFECT),
    )(*[pltpu.with_memory_space_constraint(a, pltpu.HBM) for a in list(srcs) + list(lands)])
    return outs[0], outs[1], list(outs[2:2 + ns]), list(outs[2 + ns:2 + ns + nz]), outs[-1]


def _exchange_wait(send_sems, recv_sems, srcs, lands, after, mode, *, name):
    ns, nz = len(srcs), len(lands)

    def body(*refs):
        src_refs, land_refs = refs[:ns], refs[ns:ns + nz]
        s_sems, r_sems = refs[ns + nz], refs[ns + nz + 1]
        for cp in _exchange_copies(src_refs, land_refs, s_sems, r_sems, mode):
            cp.wait_send()
            cp.wait_recv()

    hbm = lambda a: pltpu.HBM(a.shape, a.dtype)
    outs = pl.pallas_call(
        body, name=name,
        out_shape=(*[hbm(a) for a in srcs], *[hbm(a) for a in lands]),
        in_specs=[_HBM] * (ns + nz) + [_SEM, _SEM, _ANY],
        out_specs=tuple([_HBM] * (ns + nz)),
        input_output_aliases={i: i for i in range(ns + nz)},
        compiler_params=pltpu.CompilerParams(has_side_effects=_EFFECT),
    )(*srcs, *lands, send_sems, recv_sems, after)
    return list(outs[ns:])


_WIN_SEGS = (("ql", 0, Q_LORA, SEG_QL[0]), ("kvl", Q_LORA, KV_LORA, SEG_KVL[0]),
             ("kr", Q_LORA + KV_LORA, ROPE, SEG_KR[0]), ("mg", Q_LORA + KV_LORA + ROPE, D_MLA, SEG_MG[0]),
             ("ci", Q_LORA + KV_LORA + ROPE + D_MLA, 2 * D_CONV, SEG_CI[0]),
             ("cg", Q_LORA + KV_LORA + ROPE + D_MLA + 2 * D_CONV, D_CONV, SEG_CG[0]))
_WIN_SHARD = IN_COLS // N_DEV


def _win_pieces():
    out = []
    for _, o, n, new in _WIN_SEGS:
        for j in range(N_DEV):
            lo, hi = max(o, j * _WIN_SHARD), min(o + n, (j + 1) * _WIN_SHARD)
            if lo < hi:
                out.append((j, lo - j * _WIN_SHARD, new + lo - o, hi - lo))
    return out


WIN_T = 512


def _win_assemble(w_all, *, name):
    d = w_all.shape[2]
    t = min(WIN_T, d)
    pieces = sorted(_win_pieces(), key=lambda p: p[2])
    assert all(lo % 8 == 0 and n % 8 == 0 for _, lo, _, n in pieces)

    def body(w_ref, o_ref):
        rows = [w_ref[j].astype(F32)[lo:lo + n, :] for j, lo, _, n in pieces]
        rows.append(jnp.zeros((IN_PAD - (SEG_KR[0] + ROPE), t), F32))
        o_ref[...] = jnp.concatenate(rows, axis=0).astype(o_ref.dtype)

    return pl.pallas_call(
        body, name=name, grid=(d // t,),
        in_specs=[pl.BlockSpec((N_DEV, _WIN_SHARD, t), lambda i: (0, 0, i))],
        out_specs=pl.BlockSpec((IN_PAD, t), lambda i: (0, i)), out_shape=_sds((IN_PAD, d), w_all.dtype),
        compiler_params=_cp(("parallel",)),
    )(w_all)


def _win_split(grad, *, name):
    d = grad.shape[1]
    t = min(WIN_T, d)
    by_shard = [sorted([p for p in _win_pieces() if p[0] == j], key=lambda p: p[1]) for j in range(N_DEV)]

    def body(g_ref, o_ref):
        for j in range(N_DEV):
            rows = [g_ref[new:new + n, :] for _, _, new, n in by_shard[j]]
            o_ref[j] = jnp.concatenate(rows, axis=0).astype(o_ref.dtype)

    return pl.pallas_call(
        body, name=name, grid=(d // t,),
        in_specs=[pl.BlockSpec((IN_PAD, t), lambda i: (0, i))],
        out_specs=pl.BlockSpec((N_DEV, _WIN_SHARD, t), lambda i: (0, 0, i)),
        out_shape=_sds((N_DEV, _WIN_SHARD, d), WIRE_DTYPE),
        compiler_params=_cp(("parallel",)),
    )(grad)


def _cols_to_shards(a):
    r, n = a.shape
    return a.reshape(r, N_DEV, n // N_DEV).transpose(1, 0, 2)


def _shards_to_cols(a):
    nd, r, w = a.shape
    return a.transpose(1, 0, 2).reshape(r, nd * w)


def _qup_permute(w):
    w3 = w.reshape(w.shape[0], N_HEADS, QK_DIM)
    nope = w3[:, :, :NOPE].reshape(w.shape[0], N_HEADS * NOPE)
    rope = jnp.pad(w3[:, :, NOPE:], ((0, 0), (0, 0), (0, LANE - ROPE))).reshape(w.shape[0], N_HEADS * LANE)
    return jnp.concatenate([nope, rope], axis=1)


def _qup_unpermute(g):
    r = g.shape[0]
    nope = g[:, :N_HEADS * NOPE].reshape(r, N_HEADS, NOPE)
    rope = g[:, N_HEADS * NOPE:].reshape(r, N_HEADS, LANE)[:, :, :ROPE]
    return jnp.concatenate([nope, rope], axis=2).reshape(r, N_HEADS * QK_DIM)


def _norm_tiles(g):
    return g[:NOPE].reshape(1, LANE), jnp.pad(g[NOPE:], (0, LANE - ROPE)).reshape(1, LANE)


def _rope_tiles(positions):
    inv_freq = 1.0 / (ROPE_THETA ** (jnp.arange(0, ROPE, 2, dtype=F32) / ROPE))
    ang = positions.astype(F32)[:, None] * inv_freq
    cos, sin = jnp.cos(ang), jnp.sin(ang)
    zq = jnp.zeros_like(cos)
    c_t = jnp.concatenate([cos, cos, zq, zq], axis=1)
    s1_t = jnp.concatenate([-sin, zq, zq, zq], axis=1)
    s2_t = jnp.concatenate([zq, sin, zq, zq], axis=1)
    return c_t, s1_t, s2_t


_BIG = ("w_in", "w_q_up", "w_kv_up", "w_pw", "w_out")
_COL_SHARDED = ("w_q_up", "w_kv_up")


def _unpack_rows(buf, shapes):
    out, r0 = [], 0
    lead = buf.shape[:-2]
    for shp in shapes:
        n = math.prod(shp) // LANE
        out.append(buf[..., r0:r0 + n, :].reshape(lead + tuple(shp)))
        r0 += n
    return out


_SMALL = (("dmod", 3 * D_MODEL), ("norm_g", D_MODEL), ("q_lat_g", Q_LORA), ("kv_lat_g", KV_LORA),
          ("q_norm_g", 2 * LANE), ("k_norm_g", 2 * LANE), ("glu_b", 2 * D_CONV), ("dw_w", HALO * D_CONV),
          ("dw_b", D_CONV), ("conv_ln_g", D_CONV), ("conv_ln_b", D_CONV), ("b_pw", D_CONV))


def _layer_fwd(x, p, rope, l, late=None):
    n = lambda s: f"{s}_l{l}"
    c_t, s1_t, s2_t = rope
    h = _prenorm(x, p["norm_g"], p["shift"], p["sc1p"], name=n("prenorm"))
    z = _mm(h, p["w_in"], tb=True, name=n("in_proj"), tn=IN_TILE, n_outer=True)
    if late is not None:
        p = {**p, **late(z)}
    qn, kn, q_raw, kv, qf, kf, vf = _mla_pre(z, p["w_q_up"], p["w_kv_up"], p["q_lat_g"], p["kv_lat_g"],
                                             c_t, s1_t, s2_t, *p["qk_tiles"], name=n("mla_pre"))
    o, lse = _flash_fwd(qf, kf, vf, name=n("flash_fwd"))
    u1, u3, u4m = _conv_fwd(z, p["glu_b"], p["dw_w"], p["dw_b"], p["conv_ln_g"], p["conv_ln_b"], p["w_pw"],
                            name=n("conv_fwd"))
    cat = _gate_cat(o, z, u4m, p["b_pw"], name=n("gate_cat"))
    y, x_next = _mm(cat, p["w_out"], name=n("out_proj"), tn=1024, residual=(x, p["gate"]))
    saved = dict(x=x, h=h, z=z, qn=qn, kn=kn, q_raw=q_raw, kv=kv, qf=qf, kf=kf, vf=vf, o=o, lse=lse,
                 u1=u1, u3=u3, u4m=u4m, cat=cat, y=y)
    return x_next, saved, p


def _layer_bwd(gxo, dy, dgate, p, sv, rope, l, below=None, hook_rest=None, hook_w_in=None):
    n = lambda s: f"{s}_l{l}"
    c_t, s1_t, s2_t = rope
    z = sv["z"]
    g_w_out = _mm(sv["cat"], dy, ta=True, name=n("g_w_out"), tm=1024, tn=1024, after=p.get("after_start"))
    do, delta, du4, g_b_pw, dz = _gate_bwd(dy, p["w_out"], sv["o"], z, sv["u4m"], p["b_pw"], name=n("gate_bwd"))
    g_w_pw = _mm(sv["u3"], du4, ta=True, name=n("g_w_pw"), tm=1024, tn=1024, tk=512)
    du3 = _mm(du4, p["w_pw"], tb=True, name=n("d_u3"), tn=1024)
    dz, g_ln_g, g_ln_b, g_dw_b, g_glu_b, g_dw_w = _conv_bwd(
        du3, sv["u1"], z, dz, p["glu_b"], p["dw_w"], p["conv_ln_g"], p["conv_ln_b"], name=n("conv_bwd"))
    t_att = min(ATT_T, z.shape[0])
    to_lanes = lambda a: a.reshape(N_HEADS, z.shape[0] // t_att, 1, t_att)
    dqf, dkf, dvf = _flash_bwd(sv["qf"], sv["kf"], sv["vf"], do,
                               to_lanes(sv["lse"][:, :, 0]), to_lanes(delta), name=n("flash_bwd"))
    dz, g_w_q_up, g_w_kv_up, g_qn, g_kn, g_ql, g_kvl = _mla_bwd(
        dqf, dkf, dvf, sv["q_raw"], sv["kv"], z, dz, sv["qn"], sv["kn"], p["w_q_up"], p["w_kv_up"],
        p["q_lat_g"], p["kv_lat_g"], c_t, s1_t, s2_t, *p["qk_tiles"], name=n("mla_bwd"))
    big = dict(w_q_up=g_w_q_up, w_kv_up=g_w_kv_up, w_pw=g_w_pw, w_out=g_w_out)
    after = None if hook_rest is None else hook_rest(big)
    g_w_in = _mm(dz, sv["h"], ta=True, name=n("g_w_in"), tm=512, tn=1024, after=after)
    big["w_in"] = g_w_in
    after = None if hook_w_in is None else hook_w_in(g_w_in)
    dh = _mm(dz, p["w_in"], name=n("d_h"), tn=1024, after=after)
    dx, dshift, dscale, g_norm, *down = _prenorm_bwd(dh, sv["x"], gxo, p["norm_g"], p["sc1p"], below,
                                                     name=n("prenorm_bwd"))
    small = dict(dmod=jnp.concatenate([dshift, dscale, dgate], axis=1), norm_g=g_norm, q_lat_g=g_ql, kv_lat_g=g_kvl,
                 q_norm_g=g_qn, k_norm_g=g_kn, glu_b=g_glu_b, dw_w=g_dw_w, dw_b=g_dw_b,
                 conv_ln_g=g_ln_g, conv_ln_b=g_ln_b, b_pw=g_b_pw)
    return (dx, *down), big, small


def _layer_params(l, full, mod_l, small):
    d = D_MODEL
    row = lambda a: a.reshape(1, -1)
    shift, scale, gate = mod_l[:, :d], mod_l[:, d:2 * d], mod_l[:, 2 * d:]
    dw_w = jnp.pad(full["dw_w"][l], ((0, HALO - CONV_K), (0, 0)))
    return dict(
        shift=shift, sc1p=1.0 + scale, gate=gate, norm_g=row(small["norm_g"][l]),
        **{k: full[k][l] for k in _BIG if k in full}, dw_w=dw_w,
        q_lat_g=row(small["q_lat_g"][l]), kv_lat_g=row(small["kv_lat_g"][l]),
        qk_tiles=_norm_tiles(small["q_norm_g"][l]) + _norm_tiles(small["k_norm_g"][l]),
        glu_b=row(small["glu_b"][l]), dw_b=row(small["dw_b"][l]), conv_ln_g=row(small["conv_ln_g"][l]),
        conv_ln_b=row(small["conv_ln_b"][l]), b_pw=row(small["b_pw"][l]))


def kernel(x, c, positions, ada_w, ada_b, norm_g, w_in, q_lat_g, w_q_up, kv_lat_g, w_kv_up, q_norm_g, k_norm_g, glu_b, dw_w, dw_b, conv_ln_g, conv_ln_b, w_pw, b_pw, w_out, loss_target, m_ada_w, m_ada_b, m_norm_g, m_w_in, m_q_lat_g, m_w_q_up, m_kv_lat_g, m_w_kv_up, m_q_norm_g, m_k_norm_g, m_glu_b, m_dw_w, m_dw_b, m_conv_ln_g, m_conv_ln_b, m_w_pw, m_b_pw, m_w_out, v_ada_w, v_ada_b, v_norm_g, v_w_in, v_q_lat_g, v_w_q_up, v_kv_lat_g, v_w_kv_up, v_q_norm_g, v_k_norm_g, v_glu_b, v_dw_w, v_dw_b, v_conv_ln_g, v_conv_ln_b, v_w_pw, v_b_pw, v_w_out):
    names = ("ada_w", "ada_b", "norm_g", "w_in", "q_lat_g", "w_q_up", "kv_lat_g", "w_kv_up", "q_norm_g",
             "k_norm_g", "glu_b", "dw_w", "dw_b", "conv_ln_g", "conv_ln_b", "w_pw", "b_pw", "w_out")
    w_loc = dict(zip(names, (ada_w, ada_b, norm_g, w_in, q_lat_g, w_q_up, kv_lat_g, w_kv_up, q_norm_g, k_norm_g,
                             glu_b, dw_w, dw_b, conv_ln_g, conv_ln_b, w_pw, b_pw, w_out)))
    m_loc = dict(zip(names, (m_ada_w, m_ada_b, m_norm_g, m_w_in, m_q_lat_g, m_w_q_up, m_kv_lat_g, m_w_kv_up,
                             m_q_norm_g, m_k_norm_g, m_glu_b, m_dw_w, m_dw_b, m_conv_ln_g, m_conv_ln_b, m_w_pw,
                             m_b_pw, m_w_out)))
    v_loc = dict(zip(names, (v_ada_w, v_ada_b, v_norm_g, v_w_in, v_q_lat_g, v_w_q_up, v_kv_lat_g, v_w_kv_up,
                             v_q_norm_g, v_k_norm_g, v_glu_b, v_dw_w, v_dw_b, v_conv_ln_g, v_conv_ln_b, v_w_pw,
                             v_b_pw, v_w_out)))
    nl, d = N_LAYERS, D_MODEL
    me = 4 * lax.axis_index("x") + 2 * lax.axis_index("y") + lax.axis_index("c")
    x2, tgt = x[0], loss_target[0]
    ada_cols = ada_w.shape[-1]

    tr = lambda a: jnp.swapaxes(a, 1, 2)
    w_loc, m_loc, v_loc = ({**dd, "w_in": tr(dd["w_in"])} for dd in (w_loc, m_loc, v_loc))
    w_in0 = [w_loc["w_in"][0].astype(WIRE_DTYPE)]
    fly_c = _exchange_start(w_in0, _own_slots(w_in0, False, name="own_w_in_l0"), "chips", name="gather_start_w_in_l0")
    held = dict(c=c, positions=positions, ada_b=ada_b, norm_g=norm_g, q_lat_g=q_lat_g, kv_lat_g=kv_lat_g,
                q_norm_g=q_norm_g, k_norm_g=k_norm_g, glu_b=glu_b, dw_w=dw_w, dw_b=dw_b, conv_ln_g=conv_ln_g,
                conv_ln_b=conv_ln_b, b_pw=b_pw, big={k: w_loc[k] for k in _BIG})
    tok_c, held = lax.optimization_barrier((fly_c[4], held))
    c, positions, ada_b, norm_g, q_lat_g, kv_lat_g, q_norm_g, k_norm_g, glu_b, dw_w, dw_b, conv_ln_g, conv_ln_b, b_pw = (
        held[k] for k in ("c", "positions", "ada_b", "norm_g", "q_lat_g", "kv_lat_g", "q_norm_g", "k_norm_g", "glu_b",
                          "dw_w", "dw_b", "conv_ln_g", "conv_ln_b", "b_pw"))
    wire = {k: held["big"][k].astype(WIRE_DTYPE) for k in _BIG}

    dw_pad = jnp.pad(dw_w, ((0, 0), (0, HALO - CONV_K), (0, 0)))
    c_rows = c.reshape(d // LANE, LANE) + tok_c[0:1, :]
    c_all, dw_all = _all_gather([c_rows, dw_pad], name="gather_c")
    c_all = c_all.reshape(N_DEV, d)
    ada_b_cols = lax.dynamic_slice_in_dim(ada_b, me * ada_cols, ada_cols, axis=1).reshape(nl, 1, ada_cols)
    mod_cols = _ada_fwd(c_all, ada_w, ada_b_cols, name="ada_fwd")
    mod_all = _all_gather([mod_cols], name="gather_mod")[0]
    mod_me = lax.dynamic_index_in_dim(mod_all, me, axis=2, keepdims=False)
    mod = mod_me.transpose(1, 0, 2).reshape(nl, 1, N_DEV * ada_cols)

    from_chips = _exchange_wait(*fly_c[:4], mod, "chips", name="gather_wait_w_in_l0")
    fly_f = _exchange_start([], from_chips, "forward", name="forward_start_w_in_l0")
    w_in_all0 = _exchange_wait(*fly_f[:4], fly_f[4], "forward", name="forward_wait_w_in_l0")[0]
    rest0 = [wire[k][0] for k in _BIG[1:]]
    fly_r0, fly_w1 = {}, {}
    fly_r0["x"] = _exchange_start(rest0, _own_slots(rest0, False, name="own_weights_l0_rest", after=w_in_all0),
                                  "gather", name="gather_start_l0_rest")

    def layout_rest(parts):
        return dict(w_q_up=_qup_permute(_shards_to_cols(parts[0])), w_kv_up=_shards_to_cols(parts[1]),
                    w_pw=parts[2].reshape(D_CONV, D_CONV), w_out=parts[3].reshape(D_MLA + D_CONV, d))

    small_in = dict(norm_g=norm_g, q_lat_g=q_lat_g, kv_lat_g=kv_lat_g, q_norm_g=q_norm_g, k_norm_g=k_norm_g,
                    glu_b=glu_b, dw_b=dw_b, conv_ln_g=conv_ln_g, conv_ln_b=conv_ln_b, b_pw=b_pw)
    dw_full = [_shards_to_cols(dw_all[:, l])[:CONV_K] for l in range(nl)]
    rope = _rope_tiles(positions[0])

    def layer_params(l, w_in_all, rest, mod_l):
        full = dict(dw_w=dw_full)
        if w_in_all is not None:
            full["w_in"] = {l: _win_assemble(w_in_all, name=f"w_in_assemble_l{l}")}
        if rest is not None:
            full.update({k: {l: a} for k, a in layout_rest(rest).items()})
        return _layer_params(l, full, mod_l, small_in)

    def late_l0(z):
        parts = _exchange_wait(*fly_r0["x"][:4], z, "gather", name="gather_wait_l0_rest")
        src1 = [wire[k][1] for k in _BIG]
        fly_w1["x"] = _exchange_start(src1, _own_slots(src1, False, name="own_weights_l1", after=parts[0]), "gather",
                                      name="gather_start_l1")
        late = layout_rest(parts)
        late["q_lat_g"] = small_in["q_lat_g"][0].reshape(1, -1) + fly_w1["x"][4][0, 0]
        return late

    params, saved = [None] * nl, [None] * nl
    p0 = layer_params(0, w_in_all0, None, mod[0] + fly_r0["x"][4][0, 0])
    xs, saved[0], params[0] = _layer_fwd(x2, p0, rope, 0, late=late_l0)
    parts1 = _exchange_wait(*fly_w1["x"][:4], xs, "gather", name="gather_wait_l1")
    params[1] = layer_params(1, parts1[0], parts1[1:], mod[1])
    xs, saved[1], _ = _layer_fwd(xs, params[1], rope, 1)
    gx, loss_part, dy, dgate = _loss_head(xs, tgt, saved[1]["y"], params[1]["gate"], name="loss_head")
    loss = lax.psum(loss_part[0, 0], ("x", "y", "c"))

    def shard_major(k, g):
        if k == "w_q_up":
            g = _qup_unpermute(g)
        if k in _COL_SHARDED:
            return _cols_to_shards(g)
        return g.reshape((N_DEV, g.shape[0] // N_DEV, g.shape[1]))

    def scatter_start(send, tag):
        lands = _own_slots(send, True, name=f"own_grads_{tag}")
        return _exchange_start(send, lands, "scatter", name=f"scatter_start_{tag}")

    def wire_rest(big):
        return [shard_major(k, big[k]).astype(WIRE_DTYPE) for k in _BIG[1:]]

    big_g, small_g, flying = [None] * nl, [None] * nl, {}
    (gx, dy, dgate), big_g[1], small_g[1] = _layer_bwd(gx, dy, dgate, params[1], saved[1], rope, 1,
                                                       below=(saved[0]["y"], params[0]["gate"]))
    flying["l1"] = scatter_start([_win_split(big_g[1]["w_in"], name="w_in_split_l1")] + wire_rest(big_g[1]), "l1")
    p0 = dict(params[0], after_start=flying["l1"][4], b_pw=params[0]["b_pw"] + flying["l1"][4][0, 0])

    def start_rest_l0(big):
        flying["l0_rest"] = scatter_start(wire_rest(big), "l0_rest")
        return flying["l0_rest"][4]

    res, arrived = {}, [None] * nl

    def start_w_in_l0(g_w_in):
        flying["l0_w_in"] = scatter_start([_win_split(g_w_in, name="w_in_split_l0")], "l0_w_in")
        tok = flying["l0_w_in"][4]
        arrived[1] = _exchange_wait(*flying["l1"][:4], tok, "scatter", name="scatter_wait_l1")
        arrived[0] = [None] + _exchange_wait(*flying["l0_rest"][:4], tok, "scatter", name="scatter_wait_l0_rest")
        for i, k in enumerate(_BIG):
            if i > 0:
                res[k] = _adamw([arrived[l][i] for l in range(nl)], w_loc[k], m_loc[k], v_loc[k], name=f"adamw_{k}")
        res["w_in_l1"] = _adamw_layer(arrived[1][0], w_loc["w_in"], m_loc["w_in"], v_loc["w_in"], 1, None,
                                      name="adamw_w_in_l1")
        return res["w_in_l1"][0]

    (gx,), big_g[0], small_g[0] = _layer_bwd(gx, dy, dgate, p0, saved[0], rope, 0, hook_rest=start_rest_l0,
                                             hook_w_in=start_w_in_l0)

    tile = 8 * LANE
    padded = [(k, nn, -(-nn // tile) * tile) for k, nn in _SMALL]
    spk = jnp.concatenate([jnp.pad(small_g[l][k].reshape(-1), (0, np_ - nn)).reshape(-1, LANE)
                           for l in range(nl) for k, nn, np_ in padded], axis=0)
    s_all = _all_gather([spk], name="gather_small_grads")[0]
    s_rows = sum(np_ for _, _, np_ in padded) // LANE
    s_all = s_all.reshape(N_DEV, nl, s_rows, LANE)
    s_parts = {k: a[..., :nn] for (k, nn, _), a in
               zip(padded, _unpack_rows(s_all, [(np_,) for _, _, np_ in padded]))}

    dmod_all = s_parts["dmod"]
    dmod_cols = lax.dynamic_slice_in_dim(dmod_all, me * ada_cols, ada_cols, axis=2).transpose(1, 0, 2)
    g_ada_w = _ada_bwd(c_all.T, dmod_cols, name="ada_bwd")
    gp = {}
    gp["ada_w"] = g_ada_w[None]
    gp["ada_b"] = dmod_all
    for k in ("norm_g", "q_lat_g", "kv_lat_g", "glu_b", "dw_b", "conv_ln_g", "conv_ln_b", "b_pw"):
        gp[k] = s_parts[k]
    for k in ("q_norm_g", "k_norm_g"):
        t = s_parts[k]
        gp[k] = jnp.concatenate([t[..., :NOPE], t[..., LANE:LANE + ROPE]], axis=-1)
    dw_g = s_parts["dw_w"].reshape(N_DEV, nl, HALO, D_CONV)[:, :, :CONV_K]
    gp["dw_w"] = lax.dynamic_slice_in_dim(dw_g, me * LANE, LANE, axis=3)

    res["ada_w"] = _adamw(gp["ada_w"], w_loc["ada_w"], m_loc["ada_w"], v_loc["ada_w"], name="adamw_ada_w")
    small_names = [k for k in names if k not in _BIG and k != "ada_w"]
    res.update(zip(small_names, _adamw_small([(gp[k], w_loc[k], m_loc[k], v_loc[k]) for k in small_names],
                                             name="adamw_small")))
    arrived[0][0] = _exchange_wait(*flying["l0_w_in"][:4], res["ada_w"][1], "scatter", name="scatter_wait_l0_w_in")[0]
    w_in_res = _adamw_layer(arrived[0][0], w_loc["w_in"], m_loc["w_in"], v_loc["w_in"], 0, res.pop("w_in_l1"),
                            name="adamw_w_in_l0")
    res["w_in"] = tuple(tr(a) for a in w_in_res)
    out = [loss, gx[None]]
    for idx in range(4):
        out += [res[k][idx] for k in names]
    return tuple(out)
```

```python
import functools
import math

import jax
import jax.numpy as jnp
from jax import lax
from jax.experimental import pallas as pl
from jax.experimental.pallas import tpu as pltpu

F32 = jnp.float32
MXU_DTYPE = jnp.bfloat16
WIRE_DTYPE = jnp.bfloat16

D_MODEL = 2048
N_LAYERS = 2
N_DEV = 8
N_HEADS = 8
NOPE = 128
ROPE = 64
V_DIM = 128
QK_DIM = NOPE + ROPE
Q_LORA = 512
KV_LORA = 256
D_MLA = N_HEADS * V_DIM
D_CONV = 1024
CONV_K = 31
ROPE_THETA = 10000.0
EPS = 1e-6
LANE = 128
HEAD_PAD = 2 * LANE
HALO = 32

SEG_CI = (0, 2 * D_CONV)
SEG_MG = (2 * D_CONV, D_MLA)
SEG_CG = (2 * D_CONV + D_MLA, D_CONV)
SEG_QL = (2 * D_CONV + D_MLA + D_CONV, Q_LORA)
SEG_KVL = (SEG_QL[0] + Q_LORA, KV_LORA)
SEG_KR = (SEG_KVL[0] + KV_LORA, LANE)
SEG_LAT = (SEG_QL[0], 1024)
IN_PAD = SEG_LAT[0] + SEG_LAT[1]
IN_TILE = IN_PAD // 4
assert SEG_KR[0] + LANE <= IN_PAD and SEG_LAT[0] % SEG_LAT[1] == 0
IN_COLS = Q_LORA + KV_LORA + ROPE + D_MLA + 2 * D_CONV + D_CONV

ADAM_LR = 0.001
ADAM_B1 = 0.9
ADAM_B2 = 0.999
ADAM_EPS = 1e-08
ADAM_WD = 0.01
ADAM_STEP = 10

VMEM_LIMIT = 56 * 1024 * 1024
ATT_T = 512
ROW_T = 256
CONV_T = 256
MESH_ID = pl.DeviceIdType.MESH


def _cp(sem=None):
    kw = dict(vmem_limit_bytes=VMEM_LIMIT)
    if sem is not None:
        kw["dimension_semantics"] = sem
    return pltpu.CompilerParams(**kw)


def _sds(shape, dtype):
    return jax.ShapeDtypeStruct(shape, dtype)


def _silu(x):
    return x * jax.nn.sigmoid(x)


def _dsilu(x):
    s = jax.nn.sigmoid(x)
    return s * (1.0 + x * (1.0 - s))


def _rowspec(t, width, col=0):
    return pl.BlockSpec((t, width), lambda i: (i, col))


def _vecspec(width):
    return pl.BlockSpec((1, width), lambda i: (0, 0))


def _colsum(v):
    return jnp.sum(v, axis=0, keepdims=True)


def _mm(a, b, *, name, ta=False, tb=False, out_dtype=F32, tm=512, tn=512, tk=None, n_outer=False, after=None,
        residual=None):
    if ta:
        kdim, m = a.shape
    else:
        m, kdim = a.shape
    if tb:
        n, k2 = b.shape
    else:
        k2, n = b.shape
    assert kdim == k2, (a.shape, b.shape)
    tm, tn = min(tm, m), min(tn, n)
    tk = kdim if tk is None else min(tk, kdim)
    assert m % tm == 0 and n % tn == 0 and kdim % tk == 0, (m, n, kdim, tm, tn, tk)
    nk = kdim // tk
    dims = (((0 if ta else 1,), (1 if tb else 0,)), ((), ()))

    n_extra = 0 if after is None else 1
    assert residual is None or nk == 1

    def body(a_ref, b_ref, *rest):
        if residual is not None:
            x_ref, gate_ref = rest[:2]
            rest = rest[2:]
        o_ref, scratch = rest[n_extra], rest[n_extra + 1:]
        prod = lax.dot_general(a_ref[...].astype(MXU_DTYPE), b_ref[...].astype(MXU_DTYPE), dims,
                               preferred_element_type=F32)
        if residual is not None:
            o_ref[...] = prod.astype(o_ref.dtype)
            scratch[0][...] = x_ref[...] + gate_ref[...] * prod
        elif nk == 1:
            o_ref[...] = prod.astype(o_ref.dtype)
        else:
            acc = scratch[0]
            k = pl.program_id(2)

            @pl.when(k == 0)
            def _():
                acc[...] = prod

            @pl.when(k > 0)
            def _():
                acc[...] += prod

            @pl.when(k == nk - 1)
            def _():
                o_ref[...] = acc[...].astype(o_ref.dtype)

    if n_outer:
        ij = lambda g0, g1: (g1, g0)
        grid = (n // tn, m // tm, nk)
    else:
        ij = lambda g0, g1: (g0, g1)
        grid = (m // tm, n // tn, nk)

    def a_map(g0, g1, k):
        i, _ = ij(g0, g1)
        return (k, i) if ta else (i, k)

    def b_map(g0, g1, k):
        _, j = ij(g0, g1)
        return (j, k) if tb else (k, j)

    def o_map(g0, g1, k):
        return ij(g0, g1)

    in_specs = [pl.BlockSpec((tk, tm) if ta else (tm, tk), a_map), pl.BlockSpec((tn, tk) if tb else (tk, tn), b_map)]
    operands = [a, b]
    out_specs, out_shape = pl.BlockSpec((tm, tn), o_map), _sds((m, n), out_dtype)
    if residual is not None:
        in_specs += [pl.BlockSpec((tm, tn), o_map), pl.BlockSpec((1, tn), lambda g0, g1, k: (0, ij(g0, g1)[1]))]
        operands += list(residual)
        out_specs, out_shape = [out_specs, pl.BlockSpec((tm, tn), o_map)], [out_shape, _sds((m, n), F32)]
    if after is not None:
        in_specs.append(_ANY)
        operands.append(after)
    return pl.pallas_call(
        body, name=name, grid=grid, in_specs=in_specs, out_specs=out_specs, out_shape=out_shape,
        scratch_shapes=[pltpu.VMEM((tm, tn), F32)] if nk > 1 else [],
        compiler_params=_cp(("parallel", "parallel", "arbitrary")),
    )(*operands)


def _prenorm(x, g, shift, sc1p, *, name):
    s, d = x.shape
    t = min(2 * ROW_T, s)

    def body(x_ref, g_ref, sh_ref, sc_ref, h_ref):
        xv = x_ref[...]
        r = lax.rsqrt(jnp.mean(xv * xv, axis=-1, keepdims=True) + EPS)
        h_ref[...] = ((xv * r) * g_ref[...] * sc_ref[...] + sh_ref[...]).astype(h_ref.dtype)

    return pl.pallas_call(
        body, name=name, grid=(s // t,),
        in_specs=[_rowspec(t, d), _vecspec(d), _vecspec(d), _vecspec(d)],
        out_specs=_rowspec(t, d), out_shape=_sds((s, d), MXU_DTYPE),
        compiler_params=_cp(("parallel",)),
    )(x, g, shift, sc1p)


def _rope_fwd(r, c_t, s1_t, s2_t):
    return r * c_t + pltpu.roll(r, LANE - ROPE // 2, 1) * s1_t + pltpu.roll(r, ROPE // 2, 1) * s2_t


def _rope_bwd(d, c_t, s1_t, s2_t):
    return d * c_t + pltpu.roll(d * s1_t, ROPE // 2, 1) + pltpu.roll(d * s2_t, LANE - ROPE // 2, 1)


def _lanesum(v):
    return jnp.sum(v, axis=-1, keepdims=True)


def _mla_pre(z, w_q_up, w_kv_up, g_ql, g_kvl, c_t, s1_t, s2_t, gqn, gqr, gkn, gkr, *, name):
    s = z.shape[0]
    t = min(2 * ROW_T, s)
    scale = LOG2E / math.sqrt(QK_DIM)
    wide = 2 * N_HEADS * LANE

    def body(ql_ref, kvl_ref, kr_ref, wq_ref, wkv_ref, gq_ref, gk_ref, c_ref, s1_ref, s2_ref,
             gqn_ref, gqr_ref, gkn_ref, gkr_ref, qn_ref, kn_ref, q_ref, kv_ref, qf_ref, kf_ref, vf_ref):
        for src, g_ref, dst, w_ref, up in ((ql_ref, gq_ref, qn_ref, wq_ref, q_ref),
                                           (kvl_ref, gk_ref, kn_ref, wkv_ref, kv_ref)):
            v = src[...]
            r = lax.rsqrt(jnp.mean(v * v, axis=-1, keepdims=True) + EPS)
            dst[...] = ((v * r) * g_ref[...]).astype(dst.dtype)
            up[...] = jnp.dot(dst[...], w_ref[...], preferred_element_type=F32)
        c_v, s1_v, s2_v = c_ref[...], s1_ref[...], s2_ref[...]
        kr = kr_ref[...]
        kr_ss = _lanesum(kr * kr)
        for h in range(N_HEADS):
            n = q_ref[:, h * LANE:(h + 1) * LANE]
            r = q_ref[:, N_HEADS * LANE + h * LANE:N_HEADS * LANE + (h + 1) * LANE]
            rs = lax.rsqrt((_lanesum(n * n) + _lanesum(r * r)) * (1.0 / QK_DIM) + EPS)
            qf_ref[h, :, 0:LANE] = (((n * rs) * gqn_ref[...]) * scale).astype(qf_ref.dtype)
            rr = _rope_fwd((r * rs) * gqr_ref[...], c_v, s1_v, s2_v)
            qf_ref[h, :, LANE:HEAD_PAD] = (rr * scale).astype(qf_ref.dtype)

            n = kv_ref[:, h * 2 * LANE:h * 2 * LANE + LANE]
            rs = lax.rsqrt((_lanesum(n * n) + kr_ss) * (1.0 / QK_DIM) + EPS)
            kf_ref[h, :, 0:LANE] = ((n * rs) * gkn_ref[...]).astype(kf_ref.dtype)
            kf_ref[h, :, LANE:HEAD_PAD] = _rope_fwd((kr * rs) * gkr_ref[...], c_v, s1_v, s2_v).astype(kf_ref.dtype)
            vf_ref[h, :, 0:V_DIM] = kv_ref[:, h * 2 * LANE + LANE:(h + 1) * 2 * LANE].astype(vf_ref.dtype)
            vf_ref[h, :, V_DIM:] = jnp.ones((t, V_DIM), vf_ref.dtype)

    hspec = lambda w: pl.BlockSpec((N_HEADS, t, w), lambda i: (0, i, 0))
    whole = lambda a: pl.BlockSpec(a.shape, lambda i: (0, 0))
    return pl.pallas_call(
        body, name=name, grid=(s // t,),
        in_specs=[_rowspec(t, Q_LORA, SEG_QL[0] // Q_LORA), _rowspec(t, KV_LORA, SEG_KVL[0] // KV_LORA),
                  _rowspec(t, LANE, SEG_KR[0] // LANE), whole(w_q_up), whole(w_kv_up),
                  _vecspec(Q_LORA), _vecspec(KV_LORA),
                  _rowspec(t, LANE), _rowspec(t, LANE), _rowspec(t, LANE),
                  _vecspec(LANE), _vecspec(LANE), _vecspec(LANE), _vecspec(LANE)],
        out_specs=[_rowspec(t, Q_LORA), _rowspec(t, KV_LORA), _rowspec(t, wide), _rowspec(t, wide),
                   hspec(HEAD_PAD), hspec(HEAD_PAD), hspec(2 * V_DIM)],
        out_shape=[_sds((s, Q_LORA), MXU_DTYPE), _sds((s, KV_LORA), MXU_DTYPE), _sds((s, wide), F32),
                   _sds((s, wide), F32), _sds((N_HEADS, s, HEAD_PAD), MXU_DTYPE),
                   _sds((N_HEADS, s, HEAD_PAD), MXU_DTYPE), _sds((N_HEADS, s, 2 * V_DIM), MXU_DTYPE)],
        compiler_params=_cp(("parallel",)),
    )(z, z, z, w_q_up, w_kv_up, g_ql, g_kvl, c_t, s1_t, s2_t, gqn, gqr, gkn, gkr)


def _causal_mask(t):
    row = lax.broadcasted_iota(jnp.int32, (t, t), 0)
    col = lax.broadcasted_iota(jnp.int32, (t, t), 1)
    return col <= row


NEG = -1e30
LOG2E = math.log2(math.e)


def _flash_fwd(qf, kf, va, *, name):
    nh, s, dk = qf.shape
    dv = va.shape[-1] // 2
    t = min(ATT_T, s)
    n = s // t
    assert dv == LANE and t % LANE == 0

    def body(q_ref, k_ref, v_ref, o_ref, lse_ref, m_s, acc_s, s_buf):
        i = pl.program_id(1)
        m_s[...] = jnp.full(m_s.shape, NEG, F32)
        acc_s[...] = jnp.zeros(acc_s.shape, F32)

        def rows_of(j):
            return pl.ds(pl.multiple_of(j * t, t), t)

        def scores(qi, j):
            return lax.dot_general(q_ref[0, rows_of(qi), :], k_ref[0, rows_of(j), :], (((1,), (1,)), ((), ())),
                                   preferred_element_type=F32)

        def consume(j, slot, masked):
            sc = s_buf[slot]
            if masked:
                sc = jnp.where(_causal_mask(t), sc, NEG)
            m_prev = m_s[...]
            m_new = jnp.maximum(m_prev, jnp.max(sc, axis=-1, keepdims=True))
            alpha = jnp.exp2(m_prev - m_new)
            p = jnp.exp2(sc - jnp.tile(m_new, (1, t // LANE)))
            acc_s[...] = jnp.tile(alpha, (1, 2)) * acc_s[...] + jnp.dot(
                p.astype(MXU_DTYPE), v_ref[0, rows_of(j), :], preferred_element_type=F32)
            m_s[...] = m_new

        nxt = jnp.minimum(i + 1, n - 1)

        @pl.when(i == 0)
        def _():
            s_buf[2] = scores(0, 0)
            consume(0, 2, True)
            s_buf[2] = scores(nxt, 0)

        @pl.when(i > 0)
        def _():
            s_buf[1] = scores(i, 1)
            consume(0, 2, False)

            def pair(a, carry):
                s_buf[0] = scores(i, 2 * a + 2)
                consume(2 * a + 1, 1, False)
                s_buf[1] = scores(i, 2 * a + 3)
                consume(2 * a + 2, 0, False)
                return carry

            lax.fori_loop(0, (i - 1) // 2, pair, 0)

            @pl.when(i % 2 == 1)
            def _():
                s_buf[2] = scores(nxt, 0)
                consume(i, 1, True)

            @pl.when(i % 2 == 0)
            def _():
                s_buf[0] = scores(i, i)
                consume(i - 1, 1, False)
                s_buf[2] = scores(nxt, 0)
                consume(i, 0, True)

        den = acc_s[:, dv:]
        o_ref[...] = acc_s[:, :dv] / den
        lse_ref[0] = m_s[...] + jnp.log2(den)

    head = lambda h, i: (h, 0, 0)
    return pl.pallas_call(
        body, name=name, grid=(nh, n),
        in_specs=[pl.BlockSpec((1, s, dk), head), pl.BlockSpec((1, s, dk), head), pl.BlockSpec((1, s, 2 * dv), head)],
        out_specs=[pl.BlockSpec((t, dv), lambda h, i: (i, h)),
                   pl.BlockSpec((1, t, LANE), lambda h, i: (h, i, 0))],
        out_shape=[_sds((s, nh * dv), F32), _sds((nh, s, LANE), F32)],
        scratch_shapes=[pltpu.VMEM((t, LANE), F32), pltpu.VMEM((t, 2 * dv), F32), pltpu.VMEM((3, t, t), F32)],
        compiler_params=_cp(("arbitrary", "arbitrary")),
    )(qf, kf, va)


def _shifted_copies(ext_ref):
    rows = ext_ref.shape[1] - 8
    for s in range(1, 8):
        ext_ref[s, 0:rows, :] = ext_ref[0, s:s + rows, :]


def _windows(ext_ref, offsets, t_rows, lane0, lanes):
    for s in range(8):
        group = [o for o in offsets if o % 8 == s]
        if not group:
            continue
        lo, hi = min(group) - s, max(group) - s
        wide = ext_ref[s, pl.ds(lo, hi - lo + t_rows), lane0:lane0 + lanes]
        for o in group:
            yield o, wide[o - s - lo:o - s - lo + t_rows]


def _dw_taps(ext_ref, w_ref, row0, t_rows, lane0, lanes, first_off):
    acc = None
    for off, win in _windows(ext_ref, [row0 + first_off + k for k in range(CONV_K)], t_rows, lane0, lanes):
        k = off - row0 - first_off
        term = w_ref[k:k + 1, lane0:lane0 + lanes] * win
        acc = term if acc is None else acc + term
    return acc


CONV_RC = 32
CONV_LC = 256


def _conv_fwd(z, glu_b, dw_w, dw_b, ln_g, ln_b, w_pw, *, name):
    s = z.shape[0]
    t = min(CONV_T, s)
    c2 = 2 * D_CONV
    hb = t // HALO

    def body(zm_ref, zh_ref, gb_ref, w_ref, wb_ref, g_ref, b_ref, wpw_ref, u1_ref, u3_ref, u4_ref, ext):
        i = pl.program_id(0)

        def glu(zv):
            ci = zv + gb_ref[...]
            return ci[:, :D_CONV] * jax.nn.sigmoid(ci[:, D_CONV:])

        ext[0, HALO:, :] = glu(zm_ref[...])
        ext[0, 0:HALO, :] = jnp.where(i > 0, glu(zh_ref[...]), 0.0)
        _shifted_copies(ext)
        for rc in range(0, t, CONV_RC):
            for lc in range(0, D_CONV, CONV_LC):
                acc = _dw_taps(ext, w_ref, rc, CONV_RC, lc, CONV_LC, HALO - (CONV_K - 1))
                u1_ref[rc:rc + CONV_RC, lc:lc + CONV_LC] = acc + wb_ref[:, lc:lc + CONV_LC]
        u1 = u1_ref[...]
        mu = jnp.mean(u1, axis=-1, keepdims=True)
        cen = u1 - mu
        var = jnp.mean(cen * cen, axis=-1, keepdims=True)
        u2 = (cen * lax.rsqrt(var + EPS)) * g_ref[...] + b_ref[...]
        u3_ref[...] = _silu(u2).astype(u3_ref.dtype)
        u4_ref[...] = jnp.dot(u3_ref[...], wpw_ref[...], preferred_element_type=F32)

    return pl.pallas_call(
        body, name=name, grid=(s // t,),
        in_specs=[_rowspec(t, c2), pl.BlockSpec((HALO, c2), lambda i: (jnp.maximum(i * hb - 1, 0), 0)),
                  _vecspec(c2), pl.BlockSpec((HALO, D_CONV), lambda i: (0, 0)), _vecspec(D_CONV),
                  _vecspec(D_CONV), _vecspec(D_CONV), pl.BlockSpec((D_CONV, D_CONV), lambda i: (0, 0))],
        out_specs=[_rowspec(t, D_CONV), _rowspec(t, D_CONV), _rowspec(t, D_CONV)],
        out_shape=[_sds((s, D_CONV), F32), _sds((s, D_CONV), MXU_DTYPE), _sds((s, D_CONV), F32)],
        scratch_shapes=[pltpu.VMEM((8, t + HALO, D_CONV), F32)],
        compiler_params=_cp(("parallel",)),
    )(z, z, glu_b, dw_w, dw_b, ln_g, ln_b, w_pw)


def _gate_cat(o, z, u4m, b_pw, *, name):
    s = o.shape[0]
    t = min(2 * ROW_T, s)

    def body(o_ref, mg_ref, u4_ref, cg_ref, b_ref, cat_ref):
        cat_ref[:, :D_MLA] = (o_ref[...] * _silu(mg_ref[...])).astype(cat_ref.dtype)
        cat_ref[:, D_MLA:] = ((u4_ref[...] + b_ref[...]) * _silu(cg_ref[...])).astype(cat_ref.dtype)

    return pl.pallas_call(
        body, name=name, grid=(s // t,),
        in_specs=[_rowspec(t, D_MLA), _rowspec(t, D_MLA, SEG_MG[0] // D_MLA), _rowspec(t, D_CONV),
                  _rowspec(t, D_CONV, SEG_CG[0] // D_CONV), _vecspec(D_CONV)],
        out_specs=_rowspec(t, D_MLA + D_CONV), out_shape=_sds((s, D_MLA + D_CONV), MXU_DTYPE),
        compiler_params=_cp(("parallel",)),
    )(o, z, u4m, z, b_pw)


def _gated_residual_bwd(gx, y_ref, gate_ref, dy_ref, dgate_ref):
    dy_ref[...] = (gx * gate_ref[...]).astype(dy_ref.dtype)
    dgate_ref[...] += _colsum(gx * y_ref[...])


def _loss_head(xf, target, y, gate, *, name):
    s, d = xf.shape
    t = min(2 * ROW_T, s)

    def body(x_ref, t_ref, y_ref, gate_ref, gx_ref, loss_ref, dy_ref, dgate_ref):
        @pl.when(pl.program_id(0) == 0)
        def _():
            loss_ref[...] = jnp.zeros(loss_ref.shape, F32)
            dgate_ref[...] = jnp.zeros(dgate_ref.shape, F32)

        err = x_ref[...] - t_ref[...]
        gx = err * (1.0 / d)
        gx_ref[...] = gx
        loss_ref[...] += 0.5 * jnp.sum(_lanesum(err * err) * (1.0 / d), axis=0, keepdims=True)
        _gated_residual_bwd(gx, y_ref, gate_ref, dy_ref, dgate_ref)

    return pl.pallas_call(
        body, name=name, grid=(s // t,),
        in_specs=[_rowspec(t, d), _rowspec(t, d), _rowspec(t, d), _vecspec(d)],
        out_specs=[_rowspec(t, d), pl.BlockSpec((1, 1), lambda i: (0, 0)), _rowspec(t, d), _vecspec(d)],
        out_shape=[_sds((s, d), F32), _sds((1, 1), F32), _sds((s, d), MXU_DTYPE), _sds((1, d), F32)],
        compiler_params=_cp(("arbitrary",)),
    )(xf, target, y, gate)


def _acc_init(refs):
    @pl.when(pl.program_id(0) == 0)
    def _():
        for r in refs:
            r[...] = jnp.zeros(r.shape, r.dtype)


def _gate_bwd(dy, w_out, o, z, u4m, b_pw, *, name):
    s, d = dy.shape
    t = min(2 * ROW_T, s)
    gates = D_MLA + D_CONV
    assert SEG_CG[0] == SEG_MG[0] + D_MLA and SEG_MG[0] % gates == 0

    def body(dy_ref, w_ref, o_ref, mg_ref, u4_ref, cg_ref, b_ref,
             do_ref, delta_ref, du4_ref, gb_ref, dz_ref):
        _acc_init([gb_ref])
        dcat = lax.dot_general(dy_ref[...], w_ref[...], (((1,), (1,)), ((), ())), preferred_element_type=F32)
        dm, ov, mg = dcat[:, :D_MLA], o_ref[...], mg_ref[...]
        do = dm * _silu(mg)
        do_ref[...] = do.astype(do_ref.dtype)
        dz_ref[:, :D_MLA] = (dm * ov * _dsilu(mg)).astype(dz_ref.dtype)
        prod = do * ov
        for h in range(N_HEADS):
            delta_ref[h] = _lanesum(prod[:, h * V_DIM:(h + 1) * V_DIM])
        dc, cg = dcat[:, D_MLA:], cg_ref[...]
        du4 = dc * _silu(cg)
        du4_ref[...] = du4.astype(du4_ref.dtype)
        dz_ref[:, D_MLA:] = (dc * (u4_ref[...] + b_ref[...]) * _dsilu(cg)).astype(dz_ref.dtype)
        gb_ref[...] += _colsum(du4)

    return pl.pallas_call(
        body, name=name, grid=(s // t,),
        in_specs=[_rowspec(t, d), pl.BlockSpec((gates, d), lambda i: (0, 0)), _rowspec(t, D_MLA),
                  _rowspec(t, D_MLA, SEG_MG[0] // D_MLA), _rowspec(t, D_CONV),
                  _rowspec(t, D_CONV, SEG_CG[0] // D_CONV), _vecspec(D_CONV)],
        out_specs=[_rowspec(t, D_MLA), pl.BlockSpec((N_HEADS, t, 1), lambda i: (0, i, 0)),
                   _rowspec(t, D_CONV), _vecspec(D_CONV), _rowspec(t, gates, SEG_MG[0] // gates)],
        out_shape=[_sds((s, D_MLA), MXU_DTYPE), _sds((N_HEADS, s, 1), F32),
                   _sds((s, D_CONV), MXU_DTYPE), _sds((1, D_CONV), F32), _sds((s, IN_PAD), MXU_DTYPE)],
        compiler_params=_cp(("arbitrary",)),
    )(dy, w_out, o, z, u4m, z, b_pw)


def _conv_bwd(du3, u1, z, dz, glu_b, dw_w, ln_g, ln_b, *, name):
    s = z.shape[0]
    t = min(CONV_T, s)
    c2 = 2 * D_CONV
    hb = t // HALO
    n_blk = s // t
    last_halo = s // HALO - 1

    def body(d3m_ref, d3h_ref, u1m_ref, u1h_ref, zm_ref, zh_ref, gb_ref, w_ref, g_ref, b_ref, dz_in_ref,
             dci_ref, gg_ref, gbn_ref, gwb_ref, ggb_ref, gw_ref, dext, uext, du0_s, gw_acc):
        i = pl.program_id(0)
        _acc_init([gg_ref, gbn_ref, gwb_ref, ggb_ref, gw_acc])

        def ln_bwd(d3, u1v):
            mu = jnp.mean(u1v, axis=-1, keepdims=True)
            cen = u1v - mu
            rstd = lax.rsqrt(jnp.mean(cen * cen, axis=-1, keepdims=True) + EPS)
            uh = cen * rstd
            d2 = d3 * _dsilu(uh * g_ref[...] + b_ref[...])
            dh = d2 * g_ref[...]
            d1 = rstd * (dh - jnp.mean(dh, axis=-1, keepdims=True) - uh * jnp.mean(dh * uh, axis=-1, keepdims=True))
            return d1, d2, uh

        d1, d2, uh = ln_bwd(d3m_ref[...], u1m_ref[...])
        gg_ref[...] += _colsum(d2 * uh)
        gbn_ref[...] += _colsum(d2)
        gwb_ref[...] += _colsum(d1)
        dext[0, 0:t, :] = d1
        d1h, _, _ = ln_bwd(d3h_ref[...], u1h_ref[...])
        dext[0, t:, :] = jnp.where(i < n_blk - 1, d1h, 0.0)
        _shifted_copies(dext)

        def glu_parts(zv):
            ci = zv + gb_ref[...]
            return ci[:, :D_CONV], jax.nn.sigmoid(ci[:, D_CONV:])

        val, sg = glu_parts(zm_ref[...])
        uext[0, HALO:, :] = val * sg
        valh, sgh = glu_parts(zh_ref[...])
        uext[0, 0:HALO, :] = jnp.where(i > 0, valh * sgh, 0.0)
        _shifted_copies(uext)

        for rc in range(0, t, CONV_RC):
            for lc in range(0, D_CONV, CONV_LC):
                acc = None
                for off, win in _windows(dext, [rc + k for k in range(CONV_K)], CONV_RC, lc, CONV_LC):
                    k = (CONV_K - 1) - (off - rc)
                    term = w_ref[k:k + 1, lc:lc + CONV_LC] * win
                    acc = term if acc is None else acc + term
                du0_s[rc:rc + CONV_RC, lc:lc + CONV_LC] = acc
                dchunk = dext[0, rc:rc + CONV_RC, lc:lc + CONV_LC]
                first = rc + HALO - (CONV_K - 1)
                for off, win in _windows(uext, [first + k for k in range(CONV_K)], CONV_RC, lc, CONV_LC):
                    k = off - first
                    pr = dchunk * win
                    part = pr[0:8]
                    for r8 in range(8, CONV_RC, 8):
                        part = part + pr[r8:r8 + 8]
                    gw_acc[k, :, lc:lc + CONV_LC] += part

        du0 = du0_s[...]
        dval = du0 * sg
        dgt = du0 * val * sg * (1.0 - sg)
        dci_ref[:, :D_CONV] = dval.astype(dci_ref.dtype)
        dci_ref[:, D_CONV:] = dgt.astype(dci_ref.dtype)
        ggb_ref[:, :D_CONV] += _colsum(dval)
        ggb_ref[:, D_CONV:] += _colsum(dgt)

        @pl.when(i == n_blk - 1)
        def _():
            gw_ref[...] = jnp.sum(gw_acc[...], axis=1)

    halo_next = lambda w: pl.BlockSpec((HALO, w), lambda i: (jnp.minimum((i + 1) * hb, last_halo), 0))
    return pl.pallas_call(
        body, name=name, grid=(n_blk,),
        in_specs=[_rowspec(t, D_CONV), halo_next(D_CONV), _rowspec(t, D_CONV), halo_next(D_CONV),
                  _rowspec(t, c2), pl.BlockSpec((HALO, c2), lambda i: (jnp.maximum(i * hb - 1, 0), 0)),
                  _vecspec(c2), pl.BlockSpec((HALO, D_CONV), lambda i: (0, 0)), _vecspec(D_CONV), _vecspec(D_CONV),
                  _ANY],
        out_specs=[_rowspec(t, c2, SEG_CI[0] // c2), _vecspec(D_CONV), _vecspec(D_CONV), _vecspec(D_CONV),
                   _vecspec(c2), pl.BlockSpec((HALO, D_CONV), lambda i: (0, 0))],
        out_shape=[_sds(dz.shape, dz.dtype), _sds((1, D_CONV), F32), _sds((1, D_CONV), F32), _sds((1, D_CONV), F32),
                   _sds((1, c2), F32), _sds((HALO, D_CONV), F32)],
        scratch_shapes=[pltpu.VMEM((8, t + HALO, D_CONV), F32), pltpu.VMEM((8, t + HALO, D_CONV), F32),
                        pltpu.VMEM((t, D_CONV), F32), pltpu.VMEM((HALO, 8, D_CONV), F32)],
        input_output_aliases={10: 0},
        compiler_params=_cp(("arbitrary",)),
    )(du3, du3, u1, u1, z, z, glu_b, dw_w, ln_g, ln_b, dz)


def _flash_bwd(qf, kf, va, do, lse_t, delta_t, *, name):
    nh, s, dk = qf.shape
    dv = va.shape[-1] // 2
    t = min(ATT_T, s)
    n = s // t
    nt = (((1,), (1,)), ((), ()))
    tn = (((0,), (0,)), ((), ()))

    def body(q_ref, do_ref, lse_ref, dl_ref, k_ref, v_ref, dq_ref, dk_ref, dv_ref,
             dk_s, dv_s, st_buf, dpt_buf):
        n_un = pl.program_id(1)
        j = n - 1 - n_un
        nxt = jnp.maximum(j - 1, 0)

        @pl.when(n_un == 0)
        def _():
            dq_ref[...] = jnp.zeros(dq_ref.shape, F32)

        dk_s[...] = jnp.zeros(dk_s.shape, F32)
        dv_s[...] = jnp.zeros(dv_s.shape, F32)

        def rows_at(blk):
            return pl.ds(pl.multiple_of(blk * t, t), t)

        def rows_of(b):
            return rows_at(n - 1 - b)

        k = k_ref[0, rows_at(j), :]

        def produce(kj, b, slot):
            rows = rows_of(b)
            st_buf[slot] = lax.dot_general(k_ref[0, rows_at(kj), :], q_ref[0, rows, :], nt,
                                           preferred_element_type=F32)
            dpt_buf[slot] = lax.dot_general(v_ref[0, rows_at(kj), 0:dv], do_ref[rows, :], nt,
                                            preferred_element_type=F32)

        def consume(b, slot, masked):
            i = n - 1 - b
            rows = rows_of(b)
            q, dov = q_ref[0, rows, :], do_ref[rows, :]
            pt = jnp.exp2(st_buf[slot] - lse_ref[0, i])
            if masked:
                key = lax.broadcasted_iota(jnp.int32, (t, t), 0)
                qry = lax.broadcasted_iota(jnp.int32, (t, t), 1)
                pt = jnp.where(key <= qry, pt, 0.0)
            dv_s[...] += jnp.dot(pt.astype(MXU_DTYPE), dov, preferred_element_type=F32)
            dst = (pt * (dpt_buf[slot] - dl_ref[0, i])).astype(MXU_DTYPE)
            dk_s[...] += jnp.dot(dst, q, preferred_element_type=F32)
            dq_ref[0, rows, :] += lax.dot_general(dst, k, tn, preferred_element_type=F32)

        @pl.when(n_un == 0)
        def _():
            produce(j, 0, 2)
            consume(0, 2, True)
            produce(nxt, 0, 2)

        @pl.when(n_un > 0)
        def _():
            produce(j, 1, 1)
            consume(0, 2, False)

            def pair(a, carry):
                produce(j, 2 * a + 2, 0)
                consume(2 * a + 1, 1, False)
                produce(j, 2 * a + 3, 1)
                consume(2 * a + 2, 0, False)
                return carry

            lax.fori_loop(0, (n_un - 1) // 2, pair, 0)

            @pl.when(n_un % 2 == 1)
            def _():
                produce(nxt, 0, 2)
                consume(n_un, 1, True)

            @pl.when(n_un % 2 == 0)
            def _():
                produce(j, n_un, 0)
                consume(n_un - 1, 1, False)
                produce(nxt, 0, 2)
                consume(n_un, 0, True)

        dk_ref[0] = dk_s[...]
        dv_ref[0] = dv_s[...]

    head = lambda h, j: (h, 0, 0)
    rowv = pl.BlockSpec((1, n, 1, t), lambda h, j: (h, 0, 0, 0))
    return pl.pallas_call(
        body, name=name, grid=(nh, n),
        in_specs=[pl.BlockSpec((1, s, dk), head),
                  pl.BlockSpec((s, dv), lambda h, j: (0, h)),
                  rowv, rowv,
                  pl.BlockSpec((1, s, dk), head),
                  pl.BlockSpec((1, s, 2 * dv), head)],
        out_specs=[pl.BlockSpec((1, s, dk), head),
                   pl.BlockSpec((1, t, dk), lambda h, g: (h, n - 1 - g, 0)),
                   pl.BlockSpec((1, t, dv), lambda h, g: (h, n - 1 - g, 0))],
        out_shape=[_sds((nh, s, dk), F32), _sds((nh, s, dk), F32), _sds((nh, s, dv), F32)],
        scratch_shapes=[pltpu.VMEM((t, dk), F32), pltpu.VMEM((t, dv), F32),
                        pltpu.VMEM((3, t, t), F32), pltpu.VMEM((3, t, t), F32)],
        compiler_params=_cp(("arbitrary", "arbitrary")),
    )(qf, do, lse_t, delta_t, kf, va)


def _mla_bwd(dqf, dkf, dvf, q_raw, kv, z, dz, qn, kn, w_q_up, w_kv_up, g_ql, g_kvl, c_t, s1_t, s2_t,
             gqn, gqr, gkn, gkr, *, name):
    s = q_raw.shape[0]
    t = min(ROW_T, s)
    scale = 1.0 / math.sqrt(QK_DIM)
    o_ql, o_kvl, o_kr = (seg[0] - SEG_LAT[0] for seg in (SEG_QL, SEG_KVL, SEG_KR))
    tn = (((0,), (0,)), ((), ()))
    nt = (((1,), (1,)), ((), ()))

    def body(dq_ref, dk_ref, dv_ref, q_ref, kv_ref, kr_ref, ql_ref, kvl_ref, qn_ref, kn_ref, wq_ref, wkv_ref,
             gq_ref, gk_ref, c_ref, s1_ref, s2_ref, gqn_ref, gqr_ref, gkn_ref, gkr_ref, dz_in_ref,
             dz_ref, gwq_ref, gwkv_ref, ggq_ref, ggk_ref, gql_ref, gkvl_ref, dqr_ref, dkv_ref):
        _acc_init([gwq_ref, gwkv_ref, ggq_ref, ggk_ref, gql_ref, gkvl_ref])
        c_v, s1_v, s2_v = c_ref[...], s1_ref[...], s2_ref[...]
        kr = kr_ref[...]
        kr_ss = _lanesum(kr * kr)
        dkr = jnp.zeros(kr.shape, F32)
        ggq_n = ggq_r = ggk_n = ggk_r = jnp.zeros((1, LANE), F32)

        def norm_bwd(n, r, rs, dyn, dyr, gn, gr):
            nh_, rh_ = n * rs, r * rs
            dnh, drh = dyn * gn, dyr * gr
            dot = (_lanesum(dnh * nh_) + _lanesum(drh * rh_)) * (1.0 / QK_DIM)
            return rs * (dnh - nh_ * dot), rs * (drh - rh_ * dot), _colsum(dyn * nh_), _colsum(dyr * rh_)

        for h in range(N_HEADS):
            n = q_ref[:, h * LANE:(h + 1) * LANE]
            r = q_ref[:, N_HEADS * LANE + h * LANE:N_HEADS * LANE + (h + 1) * LANE]
            rs = lax.rsqrt((_lanesum(n * n) + _lanesum(r * r)) * (1.0 / QK_DIM) + EPS)
            dyn = dq_ref[h, :, 0:LANE] * scale
            dyr = _rope_bwd(dq_ref[h, :, LANE:HEAD_PAD] * scale, c_v, s1_v, s2_v)
            dn, dr, g_n, g_r = norm_bwd(n, r, rs, dyn, dyr, gqn_ref[...], gqr_ref[...])
            dqr_ref[:, h * LANE:(h + 1) * LANE] = dn.astype(dqr_ref.dtype)
            dqr_ref[:, N_HEADS * LANE + h * LANE:N_HEADS * LANE + (h + 1) * LANE] = dr.astype(dqr_ref.dtype)
            ggq_n, ggq_r = ggq_n + g_n, ggq_r + g_r

            n = kv_ref[:, h * 2 * LANE:h * 2 * LANE + LANE]
            rs = lax.rsqrt((_lanesum(n * n) + kr_ss) * (1.0 / QK_DIM) + EPS)
            dyn = dk_ref[h, :, 0:LANE] * (1.0 / LOG2E)
            dyr = _rope_bwd(dk_ref[h, :, LANE:HEAD_PAD] * (1.0 / LOG2E), c_v, s1_v, s2_v)
            dn, dr, g_n, g_r = norm_bwd(n, kr, rs, dyn, dyr, gkn_ref[...], gkr_ref[...])
            dkv_ref[:, h * 2 * LANE:h * 2 * LANE + LANE] = dn.astype(dkv_ref.dtype)
            dkv_ref[:, h * 2 * LANE + LANE:(h + 1) * 2 * LANE] = dv_ref[h].astype(dkv_ref.dtype)
            dkr = dkr + dr
            ggk_n, ggk_r = ggk_n + g_n, ggk_r + g_r

        ggq_ref[:, 0:LANE] += ggq_n
        ggq_ref[:, LANE:] += ggq_r
        ggk_ref[:, 0:LANE] += ggk_n
        ggk_ref[:, LANE:] += ggk_r

        for d_ref, x_ref, w_ref, gw_ref, src, g_ref, off, gg_ref in (
                (dqr_ref, qn_ref, wq_ref, gwq_ref, ql_ref, gq_ref, o_ql, gql_ref),
                (dkv_ref, kn_ref, wkv_ref, gwkv_ref, kvl_ref, gk_ref, o_kvl, gkvl_ref)):
            dup = d_ref[...]
            gw_ref[...] += lax.dot_general(x_ref[...], dup, tn, preferred_element_type=F32)
            dy = lax.dot_general(dup, w_ref[...], nt, preferred_element_type=F32)
            v = src[...]
            r = lax.rsqrt(jnp.mean(v * v, axis=-1, keepdims=True) + EPS)
            vh = v * r
            dvh = dy * g_ref[...]
            dz_ref[:, off:off + v.shape[1]] = (
                r * (dvh - vh * jnp.mean(dvh * vh, axis=-1, keepdims=True))).astype(dz_ref.dtype)
            gg_ref[...] += _colsum(dy * vh)
        dz_ref[:, o_kr:o_kr + LANE] = dkr.astype(dz_ref.dtype)
        dz_ref[:, o_kr + LANE:] = jnp.zeros((t, SEG_LAT[1] - o_kr - LANE), dz_ref.dtype)

    hspec = lambda w: pl.BlockSpec((N_HEADS, t, w), lambda i: (0, i, 0))
    whole = lambda a: pl.BlockSpec(a.shape, lambda i: (0, 0))
    wide = 2 * N_HEADS * LANE
    return pl.pallas_call(
        body, name=name, grid=(s // t,),
        in_specs=[hspec(HEAD_PAD), hspec(HEAD_PAD), hspec(V_DIM), _rowspec(t, wide), _rowspec(t, wide),
                  _rowspec(t, LANE, SEG_KR[0] // LANE), _rowspec(t, Q_LORA, SEG_QL[0] // Q_LORA),
                  _rowspec(t, KV_LORA, SEG_KVL[0] // KV_LORA), _rowspec(t, Q_LORA), _rowspec(t, KV_LORA),
                  whole(w_q_up), whole(w_kv_up), _vecspec(Q_LORA), _vecspec(KV_LORA),
                  _rowspec(t, LANE), _rowspec(t, LANE), _rowspec(t, LANE),
                  _vecspec(LANE), _vecspec(LANE), _vecspec(LANE), _vecspec(LANE), _ANY],
        out_specs=[_rowspec(t, SEG_LAT[1], SEG_LAT[0] // SEG_LAT[1]), whole(w_q_up), whole(w_kv_up),
                   _vecspec(2 * LANE), _vecspec(2 * LANE), _vecspec(Q_LORA), _vecspec(KV_LORA)],
        out_shape=[_sds(dz.shape, dz.dtype), _sds(w_q_up.shape, F32), _sds(w_kv_up.shape, F32),
                   _sds((1, 2 * LANE), F32), _sds((1, 2 * LANE), F32), _sds((1, Q_LORA), F32),
                   _sds((1, KV_LORA), F32)],
        scratch_shapes=[pltpu.VMEM((t, wide), MXU_DTYPE), pltpu.VMEM((t, wide), MXU_DTYPE)],
        input_output_aliases={21: 0},
        compiler_params=_cp(("arbitrary",)),
    )(dqf, dkf, dvf, q_raw, kv, z, z, z, qn, kn, w_q_up, w_kv_up, g_ql, g_kvl, c_t, s1_t, s2_t,
      gqn, gqr, gkn, gkr, dz)


def _prenorm_bwd(dh, x, gxo, g, sc1p, below=None, *, name):
    s, d = x.shape
    t = min(2 * ROW_T if below is None else ROW_T, s)
    nb = 0 if below is None else 2

    def body(dh_ref, x_ref, gx_ref, g_ref, sc_ref, *rest):
        dx_ref, dsh_ref, dsc_ref, gg_ref = rest[nb:nb + 4]
        _acc_init([dsh_ref, dsc_ref, gg_ref])
        xv, dhv = x_ref[...], dh_ref[...]
        r = lax.rsqrt(jnp.mean(xv * xv, axis=-1, keepdims=True) + EPS)
        xn = xv * r
        dsh_ref[...] += _colsum(dhv)
        dsc_ref[...] += _colsum(dhv * (xn * g_ref[...]))
        dm = dhv * sc_ref[...]
        gg_ref[...] += _colsum(dm * xn)
        dxn = dm * g_ref[...]
        dx = gx_ref[...] + r * (dxn - xn * jnp.mean(dxn * xn, axis=-1, keepdims=True))
        dx_ref[...] = dx
        if below is not None:
            _acc_init([rest[nb + 5]])
            _gated_residual_bwd(dx, rest[0], rest[1], rest[nb + 4], rest[nb + 5])

    vec_out = [_vecspec(d), _vecspec(d), _vecspec(d)]
    vec_shape = [_sds((1, d), F32)] * 3
    return pl.pallas_call(
        body, name=name, grid=(s // t,),
        in_specs=[_rowspec(t, d), _rowspec(t, d), _rowspec(t, d), _vecspec(d), _vecspec(d)]
        + ([_rowspec(t, d), _vecspec(d)] if below is not None else []),
        out_specs=[_rowspec(t, d)] + vec_out + ([_rowspec(t, d), _vecspec(d)] if below is not None else []),
        out_shape=[_sds((s, d), F32)] + vec_shape
        + ([_sds((s, d), MXU_DTYPE), _sds((1, d), F32)] if below is not None else []),
        compiler_params=_cp(("arbitrary",)),
    )(dh, x, gxo, g, sc1p, *(below if below is not None else ()))


def _ada_fwd(c_all, ada_w, ada_b_cols, *, name):
    nl, d, cols = ada_w.shape

    def body(c_ref, w_ref, b_ref, o_ref):
        ca = _silu(c_ref[...]).astype(MXU_DTYPE)
        o_ref[0] = jnp.dot(ca, w_ref[0].astype(MXU_DTYPE), preferred_element_type=F32) + b_ref[0]

    return pl.pallas_call(
        body, name=name, grid=(nl,),
        in_specs=[pl.BlockSpec((N_DEV, d), lambda l: (0, 0)), pl.BlockSpec((1, d, cols), lambda l: (l, 0, 0)),
                  pl.BlockSpec((1, 1, cols), lambda l: (l, 0, 0))],
        out_specs=pl.BlockSpec((1, N_DEV, cols), lambda l: (l, 0, 0)),
        out_shape=_sds((nl, N_DEV, cols), F32),
        compiler_params=_cp(("parallel",)),
    )(c_all, ada_w, ada_b_cols)


def _ada_bwd(c_all_t, dmod_cols, *, name):
    nl, _, cols = dmod_cols.shape
    d = c_all_t.shape[0]

    def body(c_ref, dm_ref, o_ref):
        ca = _silu(c_ref[...]).astype(MXU_DTYPE)
        o_ref[0] = jnp.dot(ca, dm_ref[0].astype(MXU_DTYPE), preferred_element_type=F32)

    return pl.pallas_call(
        body, name=name, grid=(nl,),
        in_specs=[pl.BlockSpec((d, N_DEV), lambda l: (0, 0)), pl.BlockSpec((1, N_DEV, cols), lambda l: (l, 0, 0))],
        out_specs=pl.BlockSpec((1, d, cols), lambda l: (l, 0, 0)),
        out_shape=_sds((nl, d, cols), F32),
        compiler_params=_cp(("parallel",)),
    )(c_all_t, dmod_cols)


def _adamw_math(g, w, m, v):
    mn = ADAM_B1 * m + (1.0 - ADAM_B1) * g
    vn = ADAM_B2 * v + (1.0 - ADAM_B2) * (g * g)
    m_hat = mn / (1.0 - ADAM_B1 ** ADAM_STEP)
    v_hat = vn / (1.0 - ADAM_B2 ** ADAM_STEP)
    return -ADAM_LR * (m_hat / (jnp.sqrt(v_hat) + ADAM_EPS) + ADAM_WD * w), mn, vn


def _adamw_small(items, *, name):
    n = len(items)
    shapes = [it[1].shape for it in items]
    flat = lambda a, lead: a.reshape(lead + (-1, a.shape[-1]))
    operands = []
    for gp, w, m, v in items:
        operands += [flat(gp, (gp.shape[0],)), flat(w, ()), flat(m, ()), flat(v, ())]
    nparts = [it[0].shape[0] for it in items]

    def body(*refs):
        ins, outs = refs[:4 * n], refs[4 * n:]
        for i in range(n):
            g_ref, w_ref, m_ref, v_ref = ins[4 * i:4 * i + 4]
            g = g_ref[0].astype(F32)
            for p in range(1, nparts[i]):
                g = g + g_ref[p].astype(F32)
            outs[4 * i][...] = g
            outs[4 * i + 1][...], outs[4 * i + 2][...], outs[4 * i + 3][...] = _adamw_math(
                g, w_ref[...], m_ref[...], v_ref[...])

    out_shape = []
    for it in items:
        out_shape += [_sds(flat(it[1], ()).shape, F32)] * 4
    outs = pl.pallas_call(body, name=name, out_shape=out_shape, compiler_params=_cp())(*operands)
    return [tuple(o.reshape(shp) for o in outs[4 * i:4 * i + 4]) for i, shp in enumerate(shapes)]


def _adamw_layer(gparts, w, m, v, layer, prev, *, name):
    shape = w.shape
    nl, cols = shape[0], shape[-1]
    rows = w.size // cols // nl
    npart = gparts.shape[0]
    g3 = gparts.reshape(npart, rows, cols)
    w3, m3, v3 = (a.reshape(nl, rows, cols) for a in (w, m, v))
    fits = [t for t in range(min(rows, 256) // 8 * 8, 7, -8)
            if rows % t == 0 and npart * t * cols * g3.dtype.itemsize <= 2 * 1024 * 1024]
    t = fits[0] if fits else rows
    n_prev = 0 if prev is None else 4

    def body(g_ref, w_ref, m_ref, v_ref, *rest):
        go_ref, d_ref, mo_ref, vo_ref = rest[n_prev:]
        g = g_ref[0].astype(F32)
        for p in range(1, npart):
            g = g + g_ref[p].astype(F32)
        go_ref[0] = g
        d_ref[0], mo_ref[0], vo_ref[0] = _adamw_math(g, w_ref[0], m_ref[0], v_ref[0])

    spec = pl.BlockSpec((1, t, cols), lambda i: (layer, i, 0))
    outs = pl.pallas_call(
        body, name=name, grid=(rows // t,),
        in_specs=[pl.BlockSpec((npart, t, cols), lambda i: (0, i, 0)), spec, spec, spec] + [_ANY] * n_prev,
        out_specs=[spec] * 4, out_shape=[_sds((nl, rows, cols), F32)] * 4,
        input_output_aliases={4 + k: k for k in range(n_prev)},
        compiler_params=_cp(("parallel",)),
    )(g3, w3, m3, v3, *([] if prev is None else [a.reshape(nl, rows, cols) for a in prev]))
    return tuple(o.reshape(shape) for o in outs)


def _adamw(gparts, w, m, v, *, name):
    shape = w.shape
    cols = shape[-1]
    per_layer = isinstance(gparts, (list, tuple))
    nl = shape[0] if per_layer else 1
    rows = w.size // cols // nl
    glist = list(gparts) if per_layer else [gparts]
    npart = glist[0].shape[0]
    glist = [g.reshape(npart, rows, cols) for g in glist]
    w3, m3, v3 = (a.reshape(nl, rows, cols) for a in (w, m, v))
    budget = 2 * 1024 * 1024
    fits = [t for t in range(min(rows, 256) // 8 * 8, 7, -8)
            if rows % t == 0 and npart * t * cols * glist[0].dtype.itemsize <= budget]
    t = fits[0] if fits else rows
    nb = rows // t

    def body(*refs):
        g_refs = refs[:nl]
        w_ref, m_ref, v_ref, go_ref, d_ref, mo_ref, vo_ref, g_s = refs[nl:]
        layer = pl.program_id(0)
        for l in range(nl):
            @pl.when(layer == l)
            def _(l=l):
                g = g_refs[l][0].astype(F32)
                for p in range(1, npart):
                    g = g + g_refs[l][p].astype(F32)
                g_s[...] = g

        g = g_s[...]
        go_ref[0] = g
        d_ref[0], mo_ref[0], vo_ref[0] = _adamw_math(g, w_ref[0], m_ref[0], v_ref[0])

    def g_map(l):
        return lambda layer, i: (0, jnp.where(layer == l, i, jnp.where(layer < l, 0, nb - 1)), 0)

    spec = pl.BlockSpec((1, t, cols), lambda layer, i: (layer, i, 0))
    outs = pl.pallas_call(
        body, name=name, grid=(nl, nb),
        in_specs=[pl.BlockSpec((npart, t, cols), g_map(l)) for l in range(nl)] + [spec, spec, spec],
        out_specs=[spec] * 4, out_shape=[_sds((nl, rows, cols), F32)] * 4,
        scratch_shapes=[pltpu.VMEM((t, cols), F32)],
        compiler_params=_cp(("arbitrary", "arbitrary")),
    )(*glist, w3, m3, v3)
    return tuple(o.reshape(shape) for o in outs)


_ANY = pl.BlockSpec(memory_space=pl.ANY)


def _all_gather(blocks, *, name):
    na = len(blocks)

    def body(*refs):
        x_refs, out_refs = refs[:na], refs[na:2 * na]
        send_sems, recv_sems, local_sems = refs[2 * na:]
        x, y, c = lax.axis_index("x"), lax.axis_index("y"), lax.axis_index("c")
        me, sibling = (x, y, c), (x, y, 1 - c)
        chips = [(1 - x, y), (x, 1 - y), (1 - x, 1 - y)]

        def slot(a, px, py, pc):
            return out_refs[a].at[4 * px + 2 * py + pc]

        def copy(a, k, blk, to, src=None):
            return pltpu.make_async_remote_copy(
                src_ref=slot(a, *blk) if src is None else src, dst_ref=slot(a, *blk),
                send_sem=send_sems.at[7 * a + k], recv_sem=recv_sems.at[7 * a + k],
                device_id=to, device_id_type=MESH_ID)

        mine = [pltpu.make_async_copy(x_refs[a], slot(a, *me), local_sems.at[a]) for a in range(na)]
        for cp in mine:
            cp.start()
        first = []
        for a in range(na):
            first.append(copy(a, 0, me, sibling, src=x_refs[a]))
            first += [copy(a, 1 + j, me, (*chip, c), src=x_refs[a]) for j, chip in enumerate(chips)]
        for cp in first:
            cp.start()
        passed = []
        for a in range(na):
            for j, chip in enumerate(chips):
                copy(a, 1 + j, (*chip, c), me).wait_recv()
                fwd = copy(a, 4 + j, (*chip, c), sibling)
                fwd.start()
                passed.append(fwd)
        for a in range(na):
            copy(a, 0, sibling, me).wait_recv()
            for j, chip in enumerate(chips):
                copy(a, 4 + j, (*chip, 1 - c), me).wait_recv()
        for cp in first + passed:
            cp.wait_send()
        for cp in mine:
            cp.wait()

    outs = pl.pallas_call(
        body, name=name, in_specs=[_ANY] * na, out_specs=[_ANY] * na,
        out_shape=[_sds((N_DEV,) + b.shape, b.dtype) for b in blocks],
        scratch_shapes=[pltpu.SemaphoreType.DMA((7 * na,)), pltpu.SemaphoreType.DMA((7 * na,)),
                        pltpu.SemaphoreType.DMA((na,))],
    )(*blocks)
    return list(outs)


_HBM = pl.BlockSpec(memory_space=pltpu.HBM)
_SEM = pl.BlockSpec(memory_space=pltpu.SEMAPHORE)
_EFFECT = pltpu.SideEffectType.DATAFLOW_SIDE_EFFECTING


def _peers(x, y, c):
    out = []
    for k in range(1, N_DEV):
        out.append((1 - x if k & 4 else x, 1 - y if k & 2 else y, 1 - c if k & 1 else c))
    return out


def _own_slots(srcs, scatter, *, name, after=None):
    na = len(srcs)
    n_extra = 0 if after is None else 1
    me = (4 * lax.axis_index("x") + 2 * lax.axis_index("y") + lax.axis_index("c")).astype(jnp.int32).reshape(1)

    def body(me_ref, *refs):
        in_refs, out_refs = refs[:na], refs[na + n_extra:]
        for a in range(na):
            out_refs[a][0] = in_refs[a][0] if scatter else in_refs[a][...]

    def slot_spec(shard):
        zeros = (0,) * len(shard)
        return pl.BlockSpec((1,) + tuple(shard), lambda i, me_ref: (me_ref[0],) + zeros)

    def whole_spec(shape):
        zeros = (0,) * len(shape)
        return pl.BlockSpec(tuple(shape), lambda i, me_ref: zeros)

    shards = [s.shape[1:] if scatter else s.shape for s in srcs]
    in_specs = [slot_spec(sh) if scatter else whole_spec(sh) for sh in shards] + [_ANY] * n_extra
    outs = pl.pallas_call(
        body, name=name,
        grid_spec=pltpu.PrefetchScalarGridSpec(
            num_scalar_prefetch=1, grid=(1,), in_specs=in_specs, out_specs=[slot_spec(sh) for sh in shards]),
        out_shape=[_sds((N_DEV,) + tuple(sh), s.dtype) for sh, s in zip(shards, srcs)],
        compiler_params=_cp(("arbitrary",)),
    )(me, *srcs, *([] if after is None else [after]))
    return list(outs)


_N_COPIES = dict(scatter=7, gather=7, chips=4, forward=3)


def _exchange_copies(src_refs, land_refs, send_sems, recv_sems, mode):
    x, y, c = lax.axis_index("x"), lax.axis_index("y"), lax.axis_index("c")
    me = 4 * x + 2 * y + c
    nc = _N_COPIES[mode]
    chips = [(1 - x, y), (x, 1 - y), (1 - x, 1 - y)]
    cps = []
    for a in range(len(land_refs)):
        if mode in ("scatter", "gather"):
            plan = [((src_refs[a].at[4 * px + 2 * py + pc] if mode == "scatter" else src_refs[a]),
                     land_refs[a].at[me], (px, py, pc)) for px, py, pc in _peers(x, y, c)]
        elif mode == "chips":
            plan = [(src_refs[a], land_refs[a].at[me], to) for to in [(x, y, 1 - c)] + [(*ch, c) for ch in chips]]
        else:
            plan = [(land_refs[a].at[4 * px + 2 * py + c], land_refs[a].at[4 * px + 2 * py + c], (x, y, 1 - c))
                    for px, py in chips]
        for k, (src, dst, to) in enumerate(plan):
            cps.append(pltpu.make_async_remote_copy(
                src_ref=src, dst_ref=dst, send_sem=send_sems.at[nc * a + k], recv_sem=recv_sems.at[nc * a + k],
                device_id=to, device_id_type=MESH_ID))
    return cps


def _exchange_start(srcs, lands, mode, *, name):
    ns, nz = len(srcs), len(lands)
    nsem = _N_COPIES[mode] * nz

    def body(*refs):
        src_refs, land_refs = refs[:ns], refs[ns:ns + nz]
        send_sems, recv_sems = refs[ns + nz], refs[ns + nz + 1]
        token = refs[-1]
        for cp in _exchange_copies(src_refs, land_refs, send_sems, recv_sems, mode):
            cp.start()
        token[...] = jnp.zeros(token.shape, token.dtype)

    hbm = lambda a: pltpu.HBM(a.shape, a.dtype)
    outs = pl.pallas_call(
        body, name=name,
        out_shape=(pltpu.SemaphoreType.DMA((nsem,)), pltpu.SemaphoreType.DMA((nsem,)),
                   *[hbm(a) for a in srcs], *[hbm(a) for a in lands], _sds((8, LANE), F32)),
        in_specs=[_HBM] * (ns + nz),
        out_specs=(_SEM, _SEM, *[_HBM] * (ns + nz), pl.BlockSpec(memory_space=pltpu.VMEM)),
        input_output_aliases={i: 2 + i for i in range(ns + nz)},
        compiler_params=pltpu.CompilerParams(has_side_effects=_EFFECT),
    )(*[pltpu.with_memory_space_constraint(a, pltpu.HBM) for a in list(srcs) + list(lands)])
    return outs[0], outs[1], list(outs[2:2 + ns]), list(outs[2 + ns:2 + ns + nz]), outs[-1]


def _exchange_wait(send_sems, recv_sems, srcs, lands, after, mode, *, name):
    ns, nz = len(srcs), len(lands)

    def body(*refs):
        src_refs, land_refs = refs[:ns], refs[ns:ns + nz]
        s_sems, r_sems = refs[ns + nz], refs[ns + nz + 1]
        for cp in _exchange_copies(src_refs, land_refs, s_sems, r_sems, mode):
            cp.wait_send()
            cp.wait_recv()

    hbm = lambda a: pltpu.HBM(a.shape, a.dtype)
    outs = pl.pallas_call(
        body, name=name,
        out_shape=(*[hbm(a) for a in srcs], *[hbm(a) for a in lands]),
        in_specs=[_HBM] * (ns + nz) + [_SEM, _SEM, _ANY],
        out_specs=tuple([_HBM] * (ns + nz)),
        input_output_aliases={i: i for i in range(ns + nz)},
        compiler_params=pltpu.CompilerParams(has_side_effects=_EFFECT),
    )(*srcs, *lands, send_sems, recv_sems, after)
    return list(outs[ns:])


_WIN_SEGS = (("ql", 0, Q_LORA, SEG_QL[0]), ("kvl", Q_LORA, KV_LORA, SEG_KVL[0]),
             ("kr", Q_LORA + KV_LORA, ROPE, SEG_KR[0]), ("mg", Q_LORA + KV_LORA + ROPE, D_MLA, SEG_MG[0]),
             ("ci", Q_LORA + KV_LORA + ROPE + D_MLA, 2 * D_CONV, SEG_CI[0]),
             ("cg", Q_LORA + KV_LORA + ROPE + D_MLA + 2 * D_CONV, D_CONV, SEG_CG[0]))
_WIN_SHARD = IN_COLS // N_DEV


def _win_pieces():
    out = []
    for _, o, n, new in _WIN_SEGS:
        for j in range(N_DEV):
            lo, hi = max(o, j * _WIN_SHARD), min(o + n, (j + 1) * _WIN_SHARD)
            if lo < hi:
                out.append((j, lo - j * _WIN_SHARD, new + lo - o, hi - lo))
    return out


WIN_T = 512


def _win_assemble(w_all, *, name):
    d = w_all.shape[2]
    t = min(WIN_T, d)
    pieces = sorted(_win_pieces(), key=lambda p: p[2])
    assert all(lo % 8 == 0 and n % 8 == 0 for _, lo, _, n in pieces)

    def body(w_ref, o_ref):
        rows = [w_ref[j].astype(F32)[lo:lo + n, :] for j, lo, _, n in pieces]
        rows.append(jnp.zeros((IN_PAD - (SEG_KR[0] + ROPE), t), F32))
        o_ref[...] = jnp.concatenate(rows, axis=0).astype(o_ref.dtype)

    return pl.pallas_call(
        body, name=name, grid=(d // t,),
        in_specs=[pl.BlockSpec((N_DEV, _WIN_SHARD, t), lambda i: (0, 0, i))],
        out_specs=pl.BlockSpec((IN_PAD, t), lambda i: (0, i)), out_shape=_sds((IN_PAD, d), w_all.dtype),
        compiler_params=_cp(("parallel",)),
    )(w_all)


def _win_split(grad, *, name):
    d = grad.shape[1]
    t = min(WIN_T, d)
    by_shard = [sorted([p for p in _win_pieces() if p[0] == j], key=lambda p: p[1]) for j in range(N_DEV)]

    def body(g_ref, o_ref):
        for j in range(N_DEV):
            rows = [g_ref[new:new + n, :] for _, _, new, n in by_shard[j]]
            o_ref[j] = jnp.concatenate(rows, axis=0).astype(o_ref.dtype)

    return pl.pallas_call(
        body, name=name, grid=(d // t,),
        in_specs=[pl.BlockSpec((IN_PAD, t), lambda i: (0, i))],
        out_specs=pl.BlockSpec((N_DEV, _WIN_SHARD, t), lambda i: (0, 0, i)),
        out_shape=_sds((N_DEV, _WIN_SHARD, d), WIRE_DTYPE),
        compiler_params=_cp(("parallel",)),
    )(grad)


def _cols_to_shards(a):
    r, n = a.shape
    return a.reshape(r, N_DEV, n // N_DEV).transpose(1, 0, 2)


def _shards_to_cols(a):
    nd, r, w = a.shape
    return a.transpose(1, 0, 2).reshape(r, nd * w)


def _qup_permute(w):
    w3 = w.reshape(w.shape[0], N_HEADS, QK_DIM)
    nope = w3[:, :, :NOPE].reshape(w.shape[0], N_HEADS * NOPE)
    rope = jnp.pad(w3[:, :, NOPE:], ((0, 0), (0, 0), (0, LANE - ROPE))).reshape(w.shape[0], N_HEADS * LANE)
    return jnp.concatenate([nope, rope], axis=1)


def _qup_unpermute(g):
    r = g.shape[0]
    nope = g[:, :N_HEADS * NOPE].reshape(r, N_HEADS, NOPE)
    rope = g[:, N_HEADS * NOPE:].reshape(r, N_HEADS, LANE)[:, :, :ROPE]
    return jnp.concatenate([nope, rope], axis=2).reshape(r, N_HEADS * QK_DIM)


def _norm_tiles(g):
    return g[:NOPE].reshape(1, LANE), jnp.pad(g[NOPE:], (0, LANE - ROPE)).reshape(1, LANE)


def _rope_tiles(positions):
    inv_freq = 1.0 / (ROPE_THETA ** (jnp.arange(0, ROPE, 2, dtype=F32) / ROPE))
    ang = positions.astype(F32)[:, None] * inv_freq
    cos, sin = jnp.cos(ang), jnp.sin(ang)
    zq = jnp.zeros_like(cos)
    c_t = jnp.concatenate([cos, cos, zq, zq], axis=1)
    s1_t = jnp.concatenate([-sin, zq, zq, zq], axis=1)
    s2_t = jnp.concatenate([zq, sin, zq, zq], axis=1)
    return c_t, s1_t, s2_t


_BIG = ("w_in", "w_q_up", "w_kv_up", "w_pw", "w_out")
_COL_SHARDED = ("w_q_up", "w_kv_up")


def _unpack_rows(buf, shapes):
    out, r0 = [], 0
    lead = buf.shape[:-2]
    for shp in shapes:
        n = math.prod(shp) // LANE
        out.append(buf[..., r0:r0 + n, :].reshape(lead + tuple(shp)))
        r0 += n
    return out


_SMALL = (("dmod", 3 * D_MODEL), ("norm_g", D_MODEL), ("q_lat_g", Q_LORA), ("kv_lat_g", KV_LORA),
          ("q_norm_g", 2 * LANE), ("k_norm_g", 2 * LANE), ("glu_b", 2 * D_CONV), ("dw_w", HALO * D_CONV),
          ("dw_b", D_CONV), ("conv_ln_g", D_CONV), ("conv_ln_b", D_CONV), ("b_pw", D_CONV))


def _layer_fwd(x, p, rope, l, late=None):
    n = lambda s: f"{s}_l{l}"
    c_t, s1_t, s2_t = rope
    h = _prenorm(x, p["norm_g"], p["shift"], p["sc1p"], name=n("prenorm"))
    z = _mm(h, p["w_in"], tb=True, name=n("in_proj"), tn=IN_TILE, n_outer=True)
    if late is not None:
        p = {**p, **late(z)}
    qn, kn, q_raw, kv, qf, kf, vf = _mla_pre(z, p["w_q_up"], p["w_kv_up"], p["q_lat_g"], p["kv_lat_g"],
                                             c_t, s1_t, s2_t, *p["qk_tiles"], name=n("mla_pre"))
    o, lse = _flash_fwd(qf, kf, vf, name=n("flash_fwd"))
    u1, u3, u4m = _conv_fwd(z, p["glu_b"], p["dw_w"], p["dw_b"], p["conv_ln_g"], p["conv_ln_b"], p["w_pw"],
                            name=n("conv_fwd"))
    cat = _gate_cat(o, z, u4m, p["b_pw"], name=n("gate_cat"))
    y, x_next = _mm(cat, p["w_out"], name=n("out_proj"), tn=1024, residual=(x, p["gate"]))
    saved = dict(x=x, h=h, z=z, qn=qn, kn=kn, q_raw=q_raw, kv=kv, qf=qf, kf=kf, vf=vf, o=o, lse=lse,
                 u1=u1, u3=u3, u4m=u4m, cat=cat, y=y)
    return x_next, saved, p


def _layer_bwd(gxo, dy, dgate, p, sv, rope, l, below=None, hook_rest=None, hook_w_in=None):
    n = lambda s: f"{s}_l{l}"
    c_t, s1_t, s2_t = rope
    z = sv["z"]
    g_w_out = _mm(sv["cat"], dy, ta=True, name=n("g_w_out"), tm=1024, tn=1024, after=p.get("after_start"))
    do, delta, du4, g_b_pw, dz = _gate_bwd(dy, p["w_out"], sv["o"], z, sv["u4m"], p["b_pw"], name=n("gate_bwd"))
    g_w_pw = _mm(sv["u3"], du4, ta=True, name=n("g_w_pw"), tm=1024, tn=1024, tk=512)
    du3 = _mm(du4, p["w_pw"], tb=True, name=n("d_u3"), tn=1024)
    dz, g_ln_g, g_ln_b, g_dw_b, g_glu_b, g_dw_w = _conv_bwd(
        du3, sv["u1"], z, dz, p["glu_b"], p["dw_w"], p["conv_ln_g"], p["conv_ln_b"], name=n("conv_bwd"))
    t_att = min(ATT_T, z.shape[0])
    to_lanes = lambda a: a.reshape(N_HEADS, z.shape[0] // t_att, 1, t_att)
    dqf, dkf, dvf = _flash_bwd(sv["qf"], sv["kf"], sv["vf"], do,
                               to_lanes(sv["lse"][:, :, 0]), to_lanes(delta), name=n("flash_bwd"))
    dz, g_w_q_up, g_w_kv_up, g_qn, g_kn, g_ql, g_kvl = _mla_bwd(
        dqf, dkf, dvf, sv["q_raw"], sv["kv"], z, dz, sv["qn"], sv["kn"], p["w_q_up"], p["w_kv_up"],
        p["q_lat_g"], p["kv_lat_g"], c_t, s1_t, s2_t, *p["qk_tiles"], name=n("mla_bwd"))
    big = dict(w_q_up=g_w_q_up, w_kv_up=g_w_kv_up, w_pw=g_w_pw, w_out=g_w_out)
    after = None if hook_rest is None else hook_rest(big)
    g_w_in = _mm(dz, sv["h"], ta=True, name=n("g_w_in"), tm=512, tn=1024, after=after)
    big["w_in"] = g_w_in
    after = None if hook_w_in is None else hook_w_in(g_w_in)
    dh = _mm(dz, p["w_in"], name=n("d_h"), tn=1024, after=after)
    dx, dshift, dscale, g_norm, *down = _prenorm_bwd(dh, sv["x"], gxo, p["norm_g"], p["sc1p"], below,
                                                     name=n("prenorm_bwd"))
    small = dict(dmod=jnp.concatenate([dshift, dscale, dgate], axis=1), norm_g=g_norm, q_lat_g=g_ql, kv_lat_g=g_kvl,
                 q_norm_g=g_qn, k_norm_g=g_kn, glu_b=g_glu_b, dw_w=g_dw_w, dw_b=g_dw_b,
                 conv_ln_g=g_ln_g, conv_ln_b=g_ln_b, b_pw=g_b_pw)
    return (dx, *down), big, small


def _layer_params(l, full, mod_l, small):
    d = D_MODEL
    row = lambda a: a.reshape(1, -1)
    shift, scale, gate = mod_l[:, :d], mod_l[:, d:2 * d], mod_l[:, 2 * d:]
    dw_w = jnp.pad(full["dw_w"][l], ((0, HALO - CONV_K), (0, 0)))
    return dict(
        shift=shift, sc1p=1.0 + scale, gate=gate, norm_g=row(small["norm_g"][l]),
        **{k: full[k][l] for k in _BIG if k in full}, dw_w=dw_w,
        q_lat_g=row(small["q_lat_g"][l]), kv_lat_g=row(small["kv_lat_g"][l]),
        qk_tiles=_norm_tiles(small["q_norm_g"][l]) + _norm_tiles(small["k_norm_g"][l]),
        glu_b=row(small["glu_b"][l]), dw_b=row(small["dw_b"][l]), conv_ln_g=row(small["conv_ln_g"][l]),
        conv_ln_b=row(small["conv_ln_b"][l]), b_pw=row(small["b_pw"][l]))


def kernel(x, c, positions, ada_w, ada_b, norm_g, w_in, q_lat_g, w_q_up, kv_lat_g, w_kv_up, q_norm_g, k_norm_g, glu_b, dw_w, dw_b, conv_ln_g, conv_ln_b, w_pw, b_pw, w_out, loss_target, m_ada_w, m_ada_b, m_norm_g, m_w_in, m_q_lat_g, m_w_q_up, m_kv_lat_g, m_w_kv_up, m_q_norm_g, m_k_norm_g, m_glu_b, m_dw_w, m_dw_b, m_conv_ln_g, m_conv_ln_b, m_w_pw, m_b_pw, m_w_out, v_ada_w, v_ada_b, v_norm_g, v_w_in, v_q_lat_g, v_w_q_up, v_kv_lat_g, v_w_kv_up, v_q_norm_g, v_k_norm_g, v_glu_b, v_dw_w, v_dw_b, v_conv_ln_g, v_conv_ln_b, v_w_pw, v_b_pw, v_w_out):
    names = ("ada_w", "ada_b", "norm_g", "w_in", "q_lat_g", "w_q_up", "kv_lat_g", "w_kv_up", "q_norm_g",
             "k_norm_g", "glu_b", "dw_w", "dw_b", "conv_ln_g", "conv_ln_b", "w_pw", "b_pw", "w_out")
    w_loc = dict(zip(names, (ada_w, ada_b, norm_g, w_in, q_lat_g, w_q_up, kv_lat_g, w_kv_up, q_norm_g, k_norm_g,
                             glu_b, dw_w, dw_b, conv_ln_g, conv_ln_b, w_pw, b_pw, w_out)))
    m_loc = dict(zip(names, (m_ada_w, m_ada_b, m_norm_g, m_w_in, m_q_lat_g, m_w_q_up, m_kv_lat_g, m_w_kv_up,
                             m_q_norm_g, m_k_norm_g, m_glu_b, m_dw_w, m_dw_b, m_conv_ln_g, m_conv_ln_b, m_w_pw,
                             m_b_pw, m_w_out)))
    v_loc = dict(zip(names, (v_ada_w, v_ada_b, v_norm_g, v_w_in, v_q_lat_g, v_w_q_up, v_kv_lat_g, v_w_kv_up,
                             v_q_norm_g, v_k_norm_g, v_glu_b, v_dw_w, v_dw_b, v_conv_ln_g, v_conv_ln_b, v_w_pw,
                             v_b_pw, v_w_out)))
    nl, d = N_LAYERS, D_MODEL
    me = 4 * lax.axis_index("x") + 2 * lax.axis_index("y") + lax.axis_index("c")
    x2, tgt = x[0], loss_target[0]
    ada_cols = ada_w.shape[-1]

    tr = lambda a: jnp.swapaxes(a, 1, 2)
    w_loc, m_loc, v_loc = ({**dd, "w_in": tr(dd["w_in"])} for dd in (w_loc, m_loc, v_loc))
    w_in0 = [w_loc["w_in"][0].astype(WIRE_DTYPE)]
    fly_c = _exchange_start(w_in0, _own_slots(w_in0, False, name="own_w_in_l0"), "chips", name="gather_start_w_in_l0")
    held = dict(c=c, positions=positions, ada_b=ada_b, norm_g=norm_g, q_lat_g=q_lat_g, kv_lat_g=kv_lat_g,
                q_norm_g=q_norm_g, k_norm_g=k_norm_g, glu_b=glu_b, dw_w=dw_w, dw_b=dw_b, conv_ln_g=conv_ln_g,
                conv_ln_b=conv_ln_b, b_pw=b_pw, big={k: w_loc[k] for k in _BIG})
    tok_c, held = lax.optimization_barrier((fly_c[4], held))
    c, positions, ada_b, norm_g, q_lat_g, kv_lat_g, q_norm_g, k_norm_g, glu_b, dw_w, dw_b, conv_ln_g, conv_ln_b, b_pw = (
        held[k] for k in ("c", "positions", "ada_b", "norm_g", "q_lat_g", "kv_lat_g", "q_norm_g", "k_norm_g", "glu_b",
                          "dw_w", "dw_b", "conv_ln_g", "conv_ln_b", "b_pw"))
    wire = {k: held["big"][k].astype(WIRE_DTYPE) for k in _BIG}

    dw_pad = jnp.pad(dw_w, ((0, 0), (0, HALO - CONV_K), (0, 0)))
    c_rows = c.reshape(d // LANE, LANE) + tok_c[0:1, :]
    c_all, dw_all = _all_gather([c_rows, dw_pad], name="gather_c")
    c_all = c_all.reshape(N_DEV, d)
    ada_b_cols = lax.dynamic_slice_in_dim(ada_b, me * ada_cols, ada_cols, axis=1).reshape(nl, 1, ada_cols)
    mod_cols = _ada_fwd(c_all, ada_w, ada_b_cols, name="ada_fwd")
    mod_all = _all_gather([mod_cols], name="gather_mod")[0]
    mod_me = lax.dynamic_index_in_dim(mod_all, me, axis=2, keepdims=False)
    mod = mod_me.transpose(1, 0, 2).reshape(nl, 1, N_DEV * ada_cols)

    from_chips = _exchange_wait(*fly_c[:4], mod, "chips", name="gather_wait_w_in_l0")
    fly_f = _exchange_start([], from_chips, "forward", name="forward_start_w_in_l0")
    w_in_all0 = _exchange_wait(*fly_f[:4], fly_f[4], "forward", name="forward_wait_w_in_l0")[0]
    rest0 = [wire[k][0] for k in _BIG[1:]]
    fly_r0, fly_w1 = {}, {}
    fly_r0["x"] = _exchange_start(rest0, _own_slots(rest0, False, name="own_weights_l0_rest", after=w_in_all0),
                                  "gather", name="gather_start_l0_rest")

    def layout_rest(parts):
        return dict(w_q_up=_qup_permute(_shards_to_cols(parts[0])), w_kv_up=_shards_to_cols(parts[1]),
                    w_pw=parts[2].reshape(D_CONV, D_CONV), w_out=parts[3].reshape(D_MLA + D_CONV, d))

    small_in = dict(norm_g=norm_g, q_lat_g=q_lat_g, kv_lat_g=kv_lat_g, q_norm_g=q_norm_g, k_norm_g=k_norm_g,
                    glu_b=glu_b, dw_b=dw_b, conv_ln_g=conv_ln_g, conv_ln_b=conv_ln_b, b_pw=b_pw)
    dw_full = [_shards_to_cols(dw_all[:, l])[:CONV_K] for l in range(nl)]
    rope = _rope_tiles(positions[0])

    def layer_params(l, w_in_all, rest, mod_l):
        full = dict(dw_w=dw_full)
        if w_in_all is not None:
            full["w_in"] = {l: _win_assemble(w_in_all, name=f"w_in_assemble_l{l}")}
        if rest is not None:
            full.update({k: {l: a} for k, a in layout_rest(rest).items()})
        return _layer_params(l, full, mod_l, small_in)

    def late_l0(z):
        parts = _exchange_wait(*fly_r0["x"][:4], z, "gather", name="gather_wait_l0_rest")
        src1 = [wire[k][1] for k in _BIG]
        fly_w1["x"] = _exchange_start(src1, _own_slots(src1, False, name="own_weights_l1", after=parts[0]), "gather",
                                      name="gather_start_l1")
        late = layout_rest(parts)
        late["q_lat_g"] = small_in["q_lat_g"][0].reshape(1, -1) + fly_w1["x"][4][0, 0]
        return late

    params, saved = [None] * nl, [None] * nl
    p0 = layer_params(0, w_in_all0, None, mod[0] + fly_r0["x"][4][0, 0])
    xs, saved[0], params[0] = _layer_fwd(x2, p0, rope, 0, late=late_l0)
    parts1 = _exchange_wait(*fly_w1["x"][:4], xs, "gather", name="gather_wait_l1")
    params[1] = layer_params(1, parts1[0], parts1[1:], mod[1])
    xs, saved[1], _ = _layer_fwd(xs, params[1], rope, 1)
    gx, loss_part, dy, dgate = _loss_head(xs, tgt, saved[1]["y"], params[1]["gate"], name="loss_head")
    loss = lax.psum(loss_part[0, 0], ("x", "y", "c"))

    def shard_major(k, g):
        if k == "w_q_up":
            g = _qup_unpermute(g)
        if k in _COL_SHARDED:
            return _cols_to_shards(g)
        return g.reshape((N_DEV, g.shape[0] // N_DEV, g.shape[1]))

    def scatter_start(send, tag):
        lands = _own_slots(send, True, name=f"own_grads_{tag}")
        return _exchange_start(send, lands, "scatter", name=f"scatter_start_{tag}")

    def wire_rest(big):
        return [shard_major(k, big[k]).astype(WIRE_DTYPE) for k in _BIG[1:]]

    big_g, small_g, flying = [None] * nl, [None] * nl, {}
    (gx, dy, dgate), big_g[1], small_g[1] = _layer_bwd(gx, dy, dgate, params[1], saved[1], rope, 1,
                                                       below=(saved[0]["y"], params[0]["gate"]))
    flying["l1"] = scatter_start([_win_split(big_g[1]["w_in"], name="w_in_split_l1")] + wire_rest(big_g[1]), "l1")
    p0 = dict(params[0], after_start=flying["l1"][4], b_pw=params[0]["b_pw"] + flying["l1"][4][0, 0])

    def start_rest_l0(big):
        flying["l0_rest"] = scatter_start(wire_rest(big), "l0_rest")
        return flying["l0_rest"][4]

    res, arrived = {}, [None] * nl

    def start_w_in_l0(g_w_in):
        flying["l0_w_in"] = scatter_start([_win_split(g_w_in, name="w_in_split_l0")], "l0_w_in")
        tok = flying["l0_w_in"][4]
        arrived[1] = _exchange_wait(*flying["l1"][:4], tok, "scatter", name="scatter_wait_l1")
        arrived[0] = [None] + _exchange_wait(*flying["l0_rest"][:4], tok, "scatter", name="scatter_wait_l0_rest")
        for i, k in enumerate(_BIG):
            if i > 0:
                res[k] = _adamw([arrived[l][i] for l in range(nl)], w_loc[k], m_loc[k], v_loc[k], name=f"adamw_{k}")
        res["w_in_l1"] = _adamw_layer(arrived[1][0], w_loc["w_in"], m_loc["w_in"], v_loc["w_in"], 1, None,
                                      name="adamw_w_in_l1")
        return res["w_in_l1"][0]

    (gx,), big_g[0], small_g[0] = _layer_bwd(gx, dy, dgate, p0, saved[0], rope, 0, hook_rest=start_rest_l0,
                                             hook_w_in=start_w_in_l0)

    tile = 8 * LANE
    padded = [(k, nn, -(-nn // tile) * tile) for k, nn in _SMALL]
    spk = jnp.concatenate([jnp.pad(small_g[l][k].reshape(-1), (0, np_ - nn)).reshape(-1, LANE)
                           for l in range(nl) for k, nn, np_ in padded], axis=0)
    s_all = _all_gather([spk], name="gather_small_grads")[0]
    s_rows = sum(np_ for _, _, np_ in padded) // LANE
    s_all = s_all.reshape(N_DEV, nl, s_rows, LANE)
    s_parts = {k: a[..., :nn] for (k, nn, _), a in
               zip(padded, _unpack_rows(s_all, [(np_,) for _, _, np_ in padded]))}

    dmod_all = s_parts["dmod"]
    dmod_cols = lax.dynamic_slice_in_dim(dmod_all, me * ada_cols, ada_cols, axis=2).transpose(1, 0, 2)
    g_ada_w = _ada_bwd(c_all.T, dmod_cols, name="ada_bwd")
    gp = {}
    gp["ada_w"] = g_ada_w[None]
    gp["ada_b"] = dmod_all
    for k in ("norm_g", "q_lat_g", "kv_lat_g", "glu_b", "dw_b", "conv_ln_g", "conv_ln_b", "b_pw"):
        gp[k] = s_parts[k]
    for k in ("q_norm_g", "k_norm_g"):
        t = s_parts[k]
        gp[k] = jnp.concatenate([t[..., :NOPE], t[..., LANE:LANE + ROPE]], axis=-1)
    dw_g = s_parts["dw_w"].reshape(N_DEV, nl, HALO, D_CONV)[:, :, :CONV_K]
    gp["dw_w"] = lax.dynamic_slice_in_dim(dw_g, me * LANE, LANE, axis=3)

    res["ada_w"] = _adamw(gp["ada_w"], w_loc["ada_w"], m_loc["ada_w"], v_loc["ada_w"], name="adamw_ada_w")
    small_names = [k for k in names if k not in _BIG and k != "ada_w"]
    res.update(zip(small_names, _adamw_small([(gp[k], w_loc[k], m_loc[k], v_loc[k]) for k in small_names],
                                             name="adamw_small")))
    arrived[0][0] = _exchange_wait(*flying["l0_w_in"][:4], res["ada_w"][1], "scatter", name="scatter_wait_l0_w_in")[0]
    w_in_res = _adamw_layer(arrived[0][0], w_loc["w_in"], m_loc["w_in"], v_loc["w_in"], 0, res.pop("w_in_l1"),
                            name="adamw_w_in_l0")
    res["w_in"] = tuple(tr(a) for a in w_in_res)
    out = [loss, gx[None]]
    for idx in range(4):
        out += [res[k][idx] for k in names]
    return tuple(out)
```

```python
import functools
import math

import jax
import jax.numpy as jnp
from jax import lax
from jax.experimental import pallas as pl
from jax.experimental.pallas import tpu as pltpu

F32 = jnp.float32
MXU_DTYPE = jnp.bfloat16
WIRE_DTYPE = jnp.bfloat16

D_MODEL = 2048
N_LAYERS = 2
N_DEV = 8
N_HEADS = 8
NOPE = 128
ROPE = 64
V_DIM = 128
QK_DIM = NOPE + ROPE
Q_LORA = 512
KV_LORA = 256
D_MLA = N_HEADS * V_DIM
D_CONV = 1024
CONV_K = 31
ROPE_THETA = 10000.0
EPS = 1e-6
LANE = 128
HEAD_PAD = 2 * LANE
HALO = 32

SEG_CI = (0, 2 * D_CONV)
SEG_MG = (2 * D_CONV, D_MLA)
SEG_CG = (2 * D_CONV + D_MLA, D_CONV)
SEG_QL = (2 * D_CONV + D_MLA + D_CONV, Q_LORA)
SEG_KVL = (SEG_QL[0] + Q_LORA, KV_LORA)
SEG_KR = (SEG_KVL[0] + KV_LORA, LANE)
SEG_LAT = (SEG_QL[0], 1024)
IN_PAD = SEG_LAT[0] + SEG_LAT[1]
IN_TILE = IN_PAD // 4
assert SEG_KR[0] + LANE <= IN_PAD and SEG_LAT[0] % SEG_LAT[1] == 0
IN_COLS = Q_LORA + KV_LORA + ROPE + D_MLA + 2 * D_CONV + D_CONV

ADAM_LR = 0.001
ADAM_B1 = 0.9
ADAM_B2 = 0.999
ADAM_EPS = 1e-08
ADAM_WD = 0.01
ADAM_STEP = 10

VMEM_LIMIT = 56 * 1024 * 1024
ATT_T = 512
ROW_T = 256
CONV_T = 256
MESH_ID = pl.DeviceIdType.MESH


def _cp(sem=None):
    kw = dict(vmem_limit_bytes=VMEM_LIMIT)
    if sem is not None:
        kw["dimension_semantics"] = sem
    return pltpu.CompilerParams(**kw)


def _sds(shape, dtype):
    return jax.ShapeDtypeStruct(shape, dtype)


def _silu(x):
    return x * jax.nn.sigmoid(x)


def _dsilu(x):
    s = jax.nn.sigmoid(x)
    return s * (1.0 + x * (1.0 - s))


def _rowspec(t, width, col=0):
    return pl.BlockSpec((t, width), lambda i: (i, col))


def _vecspec(width):
    return pl.BlockSpec((1, width), lambda i: (0, 0))


def _colsum(v):
    return jnp.sum(v, axis=0, keepdims=True)


def _mm(a, b, *, name, ta=False, tb=False, out_dtype=F32, tm=512, tn=512, tk=None, n_outer=False, after=None,
        residual=None):
    if ta:
        kdim, m = a.shape
    else:
        m, kdim = a.shape
    if tb:
        n, k2 = b.shape
    else:
        k2, n = b.shape
    assert kdim == k2, (a.shape, b.shape)
    tm, tn = min(tm, m), min(tn, n)
    tk = kdim if tk is None else min(tk, kdim)
    assert m % tm == 0 and n % tn == 0 and kdim % tk == 0, (m, n, kdim, tm, tn, tk)
    nk = kdim // tk
    dims = (((0 if ta else 1,), (1 if tb else 0,)), ((), ()))

    n_extra = 0 if after is None else 1
    assert residual is None or nk == 1

    def body(a_ref, b_ref, *rest):
        if residual is not None:
            x_ref, gate_ref = rest[:2]
            rest = rest[2:]
        o_ref, scratch = rest[n_extra], rest[n_extra + 1:]
        prod = lax.dot_general(a_ref[...].astype(MXU_DTYPE), b_ref[...].astype(MXU_DTYPE), dims,
                               preferred_element_type=F32)
        if residual is not None:
            o_ref[...] = prod.astype(o_ref.dtype)
            scratch[0][...] = x_ref[...] + gate_ref[...] * prod
        elif nk == 1:
            o_ref[...] = prod.astype(o_ref.dtype)
        else:
            acc = scratch[0]
            k = pl.program_id(2)

            @pl.when(k == 0)
            def _():
                acc[...] = prod

            @pl.when(k > 0)
            def _():
                acc[...] += prod

            @pl.when(k == nk - 1)
            def _():
                o_ref[...] = acc[...].astype(o_ref.dtype)

    if n_outer:
        ij = lambda g0, g1: (g1, g0)
        grid = (n // tn, m // tm, nk)
    else:
        ij = lambda g0, g1: (g0, g1)
        grid = (m // tm, n // tn, nk)

    def a_map(g0, g1, k):
        i, _ = ij(g0, g1)
        return (k, i) if ta else (i, k)

    def b_map(g0, g1, k):
        _, j = ij(g0, g1)
        return (j, k) if tb else (k, j)

    def o_map(g0, g1, k):
        return ij(g0, g1)

    in_specs = [pl.BlockSpec((tk, tm) if ta else (tm, tk), a_map), pl.BlockSpec((tn, tk) if tb else (tk, tn), b_map)]
    operands = [a, b]
    out_specs, out_shape = pl.BlockSpec((tm, tn), o_map), _sds((m, n), out_dtype)
    if residual is not None:
        in_specs += [pl.BlockSpec((tm, tn), o_map), pl.BlockSpec((1, tn), lambda g0, g1, k: (0, ij(g0, g1)[1]))]
        operands += list(residual)
        out_specs, out_shape = [out_specs, pl.BlockSpec((tm, tn), o_map)], [out_shape, _sds((m, n), F32)]
    if after is not None:
        in_specs.append(_ANY)
        operands.append(after)
    return pl.pallas_call(
        body, name=name, grid=grid, in_specs=in_specs, out_specs=out_specs, out_shape=out_shape,
        scratch_shapes=[pltpu.VMEM((tm, tn), F32)] if nk > 1 else [],
        compiler_params=_cp(("parallel", "parallel", "arbitrary")),
    )(*operands)


def _prenorm(x, g, shift, sc1p, *, name):
    s, d = x.shape
    t = min(2 * ROW_T, s)

    def body(x_ref, g_ref, sh_ref, sc_ref, h_ref):
        xv = x_ref[...]
        r = lax.rsqrt(jnp.mean(xv * xv, axis=-1, keepdims=True) + EPS)
        h_ref[...] = ((xv * r) * g_ref[...] * sc_ref[...] + sh_ref[...]).astype(h_ref.dtype)

    return pl.pallas_call(
        body, name=name, grid=(s // t,),
        in_specs=[_rowspec(t, d), _vecspec(d), _vecspec(d), _vecspec(d)],
        out_specs=_rowspec(t, d), out_shape=_sds((s, d), MXU_DTYPE),
        compiler_params=_cp(("parallel",)),
    )(x, g, shift, sc1p)


def _rope_fwd(r, c_t, s1_t, s2_t):
    return r * c_t + pltpu.roll(r, LANE - ROPE // 2, 1) * s1_t + pltpu.roll(r, ROPE // 2, 1) * s2_t


def _rope_bwd(d, c_t, s1_t, s2_t):
    return d * c_t + pltpu.roll(d * s1_t, ROPE // 2, 1) + pltpu.roll(d * s2_t, LANE - ROPE // 2, 1)


def _lanesum(v):
    return jnp.sum(v, axis=-1, keepdims=True)


def _mla_pre(z, w_q_up, w_kv_up, g_ql, g_kvl, c_t, s1_t, s2_t, gqn, gqr, gkn, gkr, *, name):
    s = z.shape[0]
    t = min(2 * ROW_T, s)
    scale = LOG2E / math.sqrt(QK_DIM)
    wide = 2 * N_HEADS * LANE

    def body(ql_ref, kvl_ref, kr_ref, wq_ref, wkv_ref, gq_ref, gk_ref, c_ref, s1_ref, s2_ref,
             gqn_ref, gqr_ref, gkn_ref, gkr_ref, qn_ref, kn_ref, q_ref, kv_ref, qf_ref, kf_ref, vf_ref):
        for src, g_ref, dst, w_ref, up in ((ql_ref, gq_ref, qn_ref, wq_ref, q_ref),
                                           (kvl_ref, gk_ref, kn_ref, wkv_ref, kv_ref)):
            v = src[...]
            r = lax.rsqrt(jnp.mean(v * v, axis=-1, keepdims=True) + EPS)
            dst[...] = ((v * r) * g_ref[...]).astype(dst.dtype)
            up[...] = jnp.dot(dst[...], w_ref[...], preferred_element_type=F32)
        c_v, s1_v, s2_v = c_ref[...], s1_ref[...], s2_ref[...]
        kr = kr_ref[...]
        kr_ss = _lanesum(kr * kr)
        for h in range(N_HEADS):
            n = q_ref[:, h * LANE:(h + 1) * LANE]
            r = q_ref[:, N_HEADS * LANE + h * LANE:N_HEADS * LANE + (h + 1) * LANE]
            rs = lax.rsqrt((_lanesum(n * n) + _lanesum(r * r)) * (1.0 / QK_DIM) + EPS)
            qf_ref[h, :, 0:LANE] = (((n * rs) * gqn_ref[...]) * scale).astype(qf_ref.dtype)
            rr = _rope_fwd((r * rs) * gqr_ref[...], c_v, s1_v, s2_v)
            qf_ref[h, :, LANE:HEAD_PAD] = (rr * scale).astype(qf_ref.dtype)

            n = kv_ref[:, h * 2 * LANE:h * 2 * LANE + LANE]
            rs = lax.rsqrt((_lanesum(n * n) + kr_ss) * (1.0 / QK_DIM) + EPS)
            kf_ref[h, :, 0:LANE] = ((n * rs) * gkn_ref[...]).astype(kf_ref.dtype)
            kf_ref[h, :, LANE:HEAD_PAD] = _rope_fwd((kr * rs) * gkr_ref[...], c_v, s1_v, s2_v).astype(kf_ref.dtype)
            vf_ref[h, :, 0:V_DIM] = kv_ref[:, h * 2 * LANE + LANE:(h + 1) * 2 * LANE].astype(vf_ref.dtype)
            vf_ref[h, :, V_DIM:] = jnp.ones((t, V_DIM), vf_ref.dtype)

    hspec = lambda w: pl.BlockSpec((N_HEADS, t, w), lambda i: (0, i, 0))
    whole = lambda a: pl.BlockSpec(a.shape, lambda i: (0, 0))
    return pl.pallas_call(
        body, name=name, grid=(s // t,),
        in_specs=[_rowspec(t, Q_LORA, SEG_QL[0] // Q_LORA), _rowspec(t, KV_LORA, SEG_KVL[0] // KV_LORA),
                  _rowspec(t, LANE, SEG_KR[0] // LANE), whole(w_q_up), whole(w_kv_up),
                  _vecspec(Q_LORA), _vecspec(KV_LORA),
                  _rowspec(t, LANE), _rowspec(t, LANE), _rowspec(t, LANE),
                  _vecspec(LANE), _vecspec(LANE), _vecspec(LANE), _vecspec(LANE)],
        out_specs=[_rowspec(t, Q_LORA), _rowspec(t, KV_LORA), _rowspec(t, wide), _rowspec(t, wide),
                   hspec(HEAD_PAD), hspec(HEAD_PAD), hspec(2 * V_DIM)],
        out_shape=[_sds((s, Q_LORA), MXU_DTYPE), _sds((s, KV_LORA), MXU_DTYPE), _sds((s, wide), F32),
                   _sds((s, wide), F32), _sds((N_HEADS, s, HEAD_PAD), MXU_DTYPE),
                   _sds((N_HEADS, s, HEAD_PAD), MXU_DTYPE), _sds((N_HEADS, s, 2 * V_DIM), MXU_DTYPE)],
        compiler_params=_cp(("parallel",)),
    )(z, z, z, w_q_up, w_kv_up, g_ql, g_kvl, c_t, s1_t, s2_t, gqn, gqr, gkn, gkr)


def _causal_mask(t):
    row = lax.broadcasted_iota(jnp.int32, (t, t), 0)
    col = lax.broadcasted_iota(jnp.int32, (t, t), 1)
    return col <= row


NEG = -1e30
LOG2E = math.log2(math.e)


def _flash_fwd(qf, kf, va, *, name):
    nh, s, dk = qf.shape
    dv = va.shape[-1] // 2
    t = min(ATT_T, s)
    n = s // t
    assert dv == LANE and t % LANE == 0

    def body(q_ref, k_ref, v_ref, o_ref, lse_ref, m_s, acc_s, s_buf):
        i = pl.program_id(1)
        m_s[...] = jnp.full(m_s.shape, NEG, F32)
        acc_s[...] = jnp.zeros(acc_s.shape, F32)

        def rows_of(j):
            return pl.ds(pl.multiple_of(j * t, t), t)

        def scores(qi, j):
            return lax.dot_general(q_ref[0, rows_of(qi), :], k_ref[0, rows_of(j), :], (((1,), (1,)), ((), ())),
                                   preferred_element_type=F32)

        def consume(j, slot, masked):
            sc = s_buf[slot]
            if masked:
                sc = jnp.where(_causal_mask(t), sc, NEG)
            m_prev = m_s[...]
            m_new = jnp.maximum(m_prev, jnp.max(sc, axis=-1, keepdims=True))
            alpha = jnp.exp2(m_prev - m_new)
            p = jnp.exp2(sc - jnp.tile(m_new, (1, t // LANE)))
            acc_s[...] = jnp.tile(alpha, (1, 2)) * acc_s[...] + jnp.dot(
                p.astype(MXU_DTYPE), v_ref[0, rows_of(j), :], preferred_element_type=F32)
            m_s[...] = m_new

        nxt = jnp.minimum(i + 1, n - 1)

        @pl.when(i == 0)
        def _():
            s_buf[2] = scores(0, 0)
            consume(0, 2, True)
            s_buf[2] = scores(nxt, 0)

        @pl.when(i > 0)
        def _():
            s_buf[1] = scores(i, 1)
            consume(0, 2, False)

            def pair(a, carry):
                s_buf[0] = scores(i, 2 * a + 2)
                consume(2 * a + 1, 1, False)
                s_buf[1] = scores(i, 2 * a + 3)
                consume(2 * a + 2, 0, False)
                return carry

            lax.fori_loop(0, (i - 1) // 2, pair, 0)

            @pl.when(i % 2 == 1)
            def _():
                s_buf[2] = scores(nxt, 0)
                consume(i, 1, True)

            @pl.when(i % 2 == 0)
            def _():
                s_buf[0] = scores(i, i)
                consume(i - 1, 1, False)
                s_buf[2] = scores(nxt, 0)
                consume(i, 0, True)

        den = acc_s[:, dv:]
        o_ref[...] = acc_s[:, :dv] / den
        lse_ref[0, 0] = jnp.transpose(m_s[...] + jnp.log2(den))[0:1, :]

    head = lambda h, i: (h, 0, 0)
    return pl.pallas_call(
        body, name=name, grid=(nh, n),
        in_specs=[pl.BlockSpec((1, s, dk), head), pl.BlockSpec((1, s, dk), head), pl.BlockSpec((1, s, 2 * dv), head)],
        out_specs=[pl.BlockSpec((t, dv), lambda h, i: (i, h)),
                   pl.BlockSpec((1, 1, 1, t), lambda h, i: (h, i, 0, 0))],
        out_shape=[_sds((s, nh * dv), F32), _sds((nh, n, 1, t), F32)],
        scratch_shapes=[pltpu.VMEM((t, LANE), F32), pltpu.VMEM((t, 2 * dv), F32), pltpu.VMEM((3, t, t), F32)],
        compiler_params=_cp(("arbitrary", "arbitrary")),
    )(qf, kf, va)


def _shifted_copies(ext_ref):
    rows = ext_ref.shape[1] - 8
    for s in range(1, 8):
        ext_ref[s, 0:rows, :] = ext_ref[0, s:s + rows, :]


def _windows(ext_ref, offsets, t_rows, lane0, lanes):
    for s in range(8):
        group = [o for o in offsets if o % 8 == s]
        if not group:
            continue
        lo, hi = min(group) - s, max(group) - s
        wide = ext_ref[s, pl.ds(lo, hi - lo + t_rows), lane0:lane0 + lanes]
        for o in group:
            yield o, wide[o - s - lo:o - s - lo + t_rows]


def _dw_taps(ext_ref, w_ref, row0, t_rows, lane0, lanes, first_off):
    acc = None
    for off, win in _windows(ext_ref, [row0 + first_off + k for k in range(CONV_K)], t_rows, lane0, lanes):
        k = off - row0 - first_off
        term = w_ref[k:k + 1, lane0:lane0 + lanes] * win
        acc = term if acc is None else acc + term
    return acc


CONV_RC = 32
CONV_LC = 256


def _conv_fwd(z, glu_b, dw_w, dw_b, ln_g, ln_b, w_pw, *, name):
    s = z.shape[0]
    t = min(CONV_T, s)
    c2 = 2 * D_CONV
    hb = t // HALO

    def body(zm_ref, zh_ref, gb_ref, w_ref, wb_ref, g_ref, b_ref, wpw_ref, u1_ref, u3_ref, u4_ref, ext):
        i = pl.program_id(0)

        def glu(zv):
            ci = zv + gb_ref[...]
            return ci[:, :D_CONV] * jax.nn.sigmoid(ci[:, D_CONV:])

        ext[0, HALO:, :] = glu(zm_ref[...])
        ext[0, 0:HALO, :] = jnp.where(i > 0, glu(zh_ref[...]), 0.0)
        _shifted_copies(ext)
        for rc in range(0, t, CONV_RC):
            for lc in range(0, D_CONV, CONV_LC):
                acc = _dw_taps(ext, w_ref, rc, CONV_RC, lc, CONV_LC, HALO - (CONV_K - 1))
                u1_ref[rc:rc + CONV_RC, lc:lc + CONV_LC] = acc + wb_ref[:, lc:lc + CONV_LC]
        u1 = u1_ref[...]
        mu = jnp.mean(u1, axis=-1, keepdims=True)
        cen = u1 - mu
        var = jnp.mean(cen * cen, axis=-1, keepdims=True)
        u2 = (cen * lax.rsqrt(var + EPS)) * g_ref[...] + b_ref[...]
        u3_ref[...] = _silu(u2).astype(u3_ref.dtype)
        u4_ref[...] = jnp.dot(u3_ref[...], wpw_ref[...], preferred_element_type=F32)

    return pl.pallas_call(
        body, name=name, grid=(s // t,),
        in_specs=[_rowspec(t, c2), pl.BlockSpec((HALO, c2), lambda i: (jnp.maximum(i * hb - 1, 0), 0)),
                  _vecspec(c2), pl.BlockSpec((HALO, D_CONV), lambda i: (0, 0)), _vecspec(D_CONV),
                  _vecspec(D_CONV), _vecspec(D_CONV), pl.BlockSpec((D_CONV, D_CONV), lambda i: (0, 0))],
        out_specs=[_rowspec(t, D_CONV), _rowspec(t, D_CONV), _rowspec(t, D_CONV)],
        out_shape=[_sds((s, D_CONV), F32), _sds((s, D_CONV), MXU_DTYPE), _sds((s, D_CONV), F32)],
        scratch_shapes=[pltpu.VMEM((8, t + HALO, D_CONV), F32)],
        compiler_params=_cp(("parallel",)),
    )(z, z, glu_b, dw_w, dw_b, ln_g, ln_b, w_pw)


def _gate_cat(o, z, u4m, b_pw, *, name):
    s = o.shape[0]
    t = min(2 * ROW_T, s)

    def body(o_ref, mg_ref, u4_ref, cg_ref, b_ref, cat_ref):
        cat_ref[:, :D_MLA] = (o_ref[...] * _silu(mg_ref[...])).astype(cat_ref.dtype)
        cat_ref[:, D_MLA:] = ((u4_ref[...] + b_ref[...]) * _silu(cg_ref[...])).astype(cat_ref.dtype)

    return pl.pallas_call(
        body, name=name, grid=(s // t,),
        in_specs=[_rowspec(t, D_MLA), _rowspec(t, D_MLA, SEG_MG[0] // D_MLA), _rowspec(t, D_CONV),
                  _rowspec(t, D_CONV, SEG_CG[0] // D_CONV), _vecspec(D_CONV)],
        out_specs=_rowspec(t, D_MLA + D_CONV), out_shape=_sds((s, D_MLA + D_CONV), MXU_DTYPE),
        compiler_params=_cp(("parallel",)),
    )(o, z, u4m, z, b_pw)


def _gated_residual_bwd(gx, y_ref, gate_ref, dy_ref, dgate_ref):
    dy_ref[...] = (gx * gate_ref[...]).astype(dy_ref.dtype)
    dgate_ref[...] += _colsum(gx * y_ref[...])


def _loss_head(xf, target, y, gate, *, name):
    s, d = xf.shape
    t = min(2 * ROW_T, s)

    def body(x_ref, t_ref, y_ref, gate_ref, gx_ref, loss_ref, dy_ref, dgate_ref):
        @pl.when(pl.program_id(0) == 0)
        def _():
            loss_ref[...] = jnp.zeros(loss_ref.shape, F32)
            dgate_ref[...] = jnp.zeros(dgate_ref.shape, F32)

        err = x_ref[...] - t_ref[...]
        gx = err * (1.0 / d)
        gx_ref[...] = gx
        loss_ref[...] += 0.5 * jnp.sum(_lanesum(err * err) * (1.0 / d), axis=0, keepdims=True)
        _gated_residual_bwd(gx, y_ref, gate_ref, dy_ref, dgate_ref)

    return pl.pallas_call(
        body, name=name, grid=(s // t,),
        in_specs=[_rowspec(t, d), _rowspec(t, d), _rowspec(t, d), _vecspec(d)],
        out_specs=[_rowspec(t, d), pl.BlockSpec((1, 1), lambda i: (0, 0)), _rowspec(t, d), _vecspec(d)],
        out_shape=[_sds((s, d), F32), _sds((1, 1), F32), _sds((s, d), MXU_DTYPE), _sds((1, d), F32)],
        compiler_params=_cp(("arbitrary",)),
    )(xf, target, y, gate)


def _acc_init(refs):
    @pl.when(pl.program_id(0) == 0)
    def _():
        for r in refs:
            r[...] = jnp.zeros(r.shape, r.dtype)


def _gate_bwd(dy, w_out, o, z, u4m, b_pw, *, name):
    s, d = dy.shape
    t = min(2 * ROW_T, s)
    gates = D_MLA + D_CONV
    assert SEG_CG[0] == SEG_MG[0] + D_MLA and SEG_MG[0] % gates == 0

    def body(dy_ref, w_ref, o_ref, mg_ref, u4_ref, cg_ref, b_ref,
             do_ref, delta_ref, du4_ref, gb_ref, dz_ref):
        _acc_init([gb_ref])
        dcat = lax.dot_general(dy_ref[...], w_ref[...], (((1,), (1,)), ((), ())), preferred_element_type=F32)
        dm, ov, mg = dcat[:, :D_MLA], o_ref[...], mg_ref[...]
        do = dm * _silu(mg)
        do_ref[...] = do.astype(do_ref.dtype)
        dz_ref[:, :D_MLA] = (dm * ov * _dsilu(mg)).astype(dz_ref.dtype)
        prod = do * ov
        for h in range(N_HEADS):
            rowsum = jnp.broadcast_to(_lanesum(prod[:, h * V_DIM:(h + 1) * V_DIM]), (t, LANE))
            delta_ref[h, 0] = jnp.transpose(rowsum)[0:1, :]
        dc, cg = dcat[:, D_MLA:], cg_ref[...]
        du4 = dc * _silu(cg)
        du4_ref[...] = du4.astype(du4_ref.dtype)
        dz_ref[:, D_MLA:] = (dc * (u4_ref[...] + b_ref[...]) * _dsilu(cg)).astype(dz_ref.dtype)
        gb_ref[...] += _colsum(du4)

    return pl.pallas_call(
        body, name=name, grid=(s // t,),
        in_specs=[_rowspec(t, d), pl.BlockSpec((gates, d), lambda i: (0, 0)), _rowspec(t, D_MLA),
                  _rowspec(t, D_MLA, SEG_MG[0] // D_MLA), _rowspec(t, D_CONV),
                  _rowspec(t, D_CONV, SEG_CG[0] // D_CONV), _vecspec(D_CONV)],
        out_specs=[_rowspec(t, D_MLA), pl.BlockSpec((N_HEADS, 1, 1, t), lambda i: (0, i, 0, 0)),
                   _rowspec(t, D_CONV), _vecspec(D_CONV), _rowspec(t, gates, SEG_MG[0] // gates)],
        out_shape=[_sds((s, D_MLA), MXU_DTYPE), _sds((N_HEADS, s // t, 1, t), F32),
                   _sds((s, D_CONV), MXU_DTYPE), _sds((1, D_CONV), F32), _sds((s, IN_PAD), MXU_DTYPE)],
        compiler_params=_cp(("arbitrary",)),
    )(dy, w_out, o, z, u4m, z, b_pw)


def _conv_bwd(du3, u1, z, dz, glu_b, dw_w, ln_g, ln_b, *, name):
    s = z.shape[0]
    t = min(CONV_T, s)
    c2 = 2 * D_CONV
    hb = t // HALO
    n_blk = s // t
    last_halo = s // HALO - 1

    def body(d3m_ref, d3h_ref, u1m_ref, u1h_ref, zm_ref, zh_ref, gb_ref, w_ref, g_ref, b_ref, dz_in_ref,
             dci_ref, gg_ref, gbn_ref, gwb_ref, ggb_ref, gw_ref, dext, uext, du0_s, gw_acc):
        i = pl.program_id(0)
        _acc_init([gg_ref, gbn_ref, gwb_ref, ggb_ref, gw_acc])

        def ln_bwd(d3, u1v):
            mu = jnp.mean(u1v, axis=-1, keepdims=True)
            cen = u1v - mu
            rstd = lax.rsqrt(jnp.mean(cen * cen, axis=-1, keepdims=True) + EPS)
            uh = cen * rstd
            d2 = d3 * _dsilu(uh * g_ref[...] + b_ref[...])
            dh = d2 * g_ref[...]
            d1 = rstd * (dh - jnp.mean(dh, axis=-1, keepdims=True) - uh * jnp.mean(dh * uh, axis=-1, keepdims=True))
            return d1, d2, uh

        d1, d2, uh = ln_bwd(d3m_ref[...], u1m_ref[...])
        gg_ref[...] += _colsum(d2 * uh)
        gbn_ref[...] += _colsum(d2)
        gwb_ref[...] += _colsum(d1)
        dext[0, 0:t, :] = d1
        d1h, _, _ = ln_bwd(d3h_ref[...], u1h_ref[...])
        dext[0, t:, :] = jnp.where(i < n_blk - 1, d1h, 0.0)
        _shifted_copies(dext)

        def glu_parts(zv):
            ci = zv + gb_ref[...]
            return ci[:, :D_CONV], jax.nn.sigmoid(ci[:, D_CONV:])

        val, sg = glu_parts(zm_ref[...])
        uext[0, HALO:, :] = val * sg
        valh, sgh = glu_parts(zh_ref[...])
        uext[0, 0:HALO, :] = jnp.where(i > 0, valh * sgh, 0.0)
        _shifted_copies(uext)

        for rc in range(0, t, CONV_RC):
            for lc in range(0, D_CONV, CONV_LC):
                acc = None
                for off, win in _windows(dext, [rc + k for k in range(CONV_K)], CONV_RC, lc, CONV_LC):
                    k = (CONV_K - 1) - (off - rc)
                    term = w_ref[k:k + 1, lc:lc + CONV_LC] * win
                    acc = term if acc is None else acc + term
                du0_s[rc:rc + CONV_RC, lc:lc + CONV_LC] = acc
                dchunk = dext[0, rc:rc + CONV_RC, lc:lc + CONV_LC]
                first = rc + HALO - (CONV_K - 1)
                for off, win in _windows(uext, [first + k for k in range(CONV_K)], CONV_RC, lc, CONV_LC):
                    k = off - first
                    pr = dchunk * win
                    part = pr[0:8]
                    for r8 in range(8, CONV_RC, 8):
                        part = part + pr[r8:r8 + 8]
                    gw_acc[k, :, lc:lc + CONV_LC] += part

        du0 = du0_s[...]
        dval = du0 * sg
        dgt = du0 * val * sg * (1.0 - sg)
        dci_ref[:, :D_CONV] = dval.astype(dci_ref.dtype)
        dci_ref[:, D_CONV:] = dgt.astype(dci_ref.dtype)
        ggb_ref[:, :D_CONV] += _colsum(dval)
        ggb_ref[:, D_CONV:] += _colsum(dgt)

        @pl.when(i == n_blk - 1)
        def _():
            gw_ref[...] = jnp.sum(gw_acc[...], axis=1)

    halo_next = lambda w: pl.BlockSpec((HALO, w), lambda i: (jnp.minimum((i + 1) * hb, last_halo), 0))
    return pl.pallas_call(
        body, name=name, grid=(n_blk,),
        in_specs=[_rowspec(t, D_CONV), halo_next(D_CONV), _rowspec(t, D_CONV), halo_next(D_CONV),
                  _rowspec(t, c2), pl.BlockSpec((HALO, c2), lambda i: (jnp.maximum(i * hb - 1, 0), 0)),
                  _vecspec(c2), pl.BlockSpec((HALO, D_CONV), lambda i: (0, 0)), _vecspec(D_CONV), _vecspec(D_CONV),
                  _ANY],
        out_specs=[_rowspec(t, c2, SEG_CI[0] // c2), _vecspec(D_CONV), _vecspec(D_CONV), _vecspec(D_CONV),
                   _vecspec(c2), pl.BlockSpec((HALO, D_CONV), lambda i: (0, 0))],
        out_shape=[_sds(dz.shape, dz.dtype), _sds((1, D_CONV), F32), _sds((1, D_CONV), F32), _sds((1, D_CONV), F32),
                   _sds((1, c2), F32), _sds((HALO, D_CONV), F32)],
        scratch_shapes=[pltpu.VMEM((8, t + HALO, D_CONV), F32), pltpu.VMEM((8, t + HALO, D_CONV), F32),
                        pltpu.VMEM((t, D_CONV), F32), pltpu.VMEM((HALO, 8, D_CONV), F32)],
        input_output_aliases={10: 0},
        compiler_params=_cp(("arbitrary",)),
    )(du3, du3, u1, u1, z, z, glu_b, dw_w, ln_g, ln_b, dz)


def _flash_bwd(qf, kf, va, do, lse_t, delta_t, *, name):
    nh, s, dk = qf.shape
    dv = va.shape[-1] // 2
    t = min(ATT_T, s)
    n = s // t
    nt = (((1,), (1,)), ((), ()))
    tn = (((0,), (0,)), ((), ()))

    def body(q_ref, do_ref, lse_ref, dl_ref, k_ref, v_ref, dq_ref, dk_ref, dv_ref,
             dk_s, dv_s, st_buf, dpt_buf):
        n_un = pl.program_id(1)
        j = n - 1 - n_un
        nxt = jnp.maximum(j - 1, 0)

        @pl.when(n_un == 0)
        def _():
            dq_ref[...] = jnp.zeros(dq_ref.shape, F32)

        dk_s[...] = jnp.zeros(dk_s.shape, F32)
        dv_s[...] = jnp.zeros(dv_s.shape, F32)

        def rows_at(blk):
            return pl.ds(pl.multiple_of(blk * t, t), t)

        def rows_of(b):
            return rows_at(n - 1 - b)

        k = k_ref[0, rows_at(j), :]

        def produce(kj, b, slot):
            rows = rows_of(b)
            st_buf[slot] = lax.dot_general(k_ref[0, rows_at(kj), :], q_ref[0, rows, :], nt,
                                           preferred_element_type=F32)
            dpt_buf[slot] = lax.dot_general(v_ref[0, rows_at(kj), 0:dv], do_ref[rows, :], nt,
                                            preferred_element_type=F32)

        def consume(b, slot, masked):
            i = n - 1 - b
            rows = rows_of(b)
            q, dov = q_ref[0, rows, :], do_ref[rows, :]
            pt = jnp.exp2(st_buf[slot] - lse_ref[0, i])
            if masked:
                key = lax.broadcasted_iota(jnp.int32, (t, t), 0)
                qry = lax.broadcasted_iota(jnp.int32, (t, t), 1)
                pt = jnp.where(key <= qry, pt, 0.0)
            dv_s[...] += jnp.dot(pt.astype(MXU_DTYPE), dov, preferred_element_type=F32)
            dst = (pt * (dpt_buf[slot] - dl_ref[0, i])).astype(MXU_DTYPE)
            dk_s[...] += jnp.dot(dst, q, preferred_element_type=F32)
            dq_ref[0, rows, :] += lax.dot_general(dst, k, tn, preferred_element_type=F32)

        @pl.when(n_un == 0)
        def _():
            produce(j, 0, 2)
            consume(0, 2, True)
            produce(nxt, 0, 2)

        @pl.when(n_un > 0)
        def _():
            produce(j, 1, 1)
            consume(0, 2, False)

            def pair(a, carry):
                produce(j, 2 * a + 2, 0)
                consume(2 * a + 1, 1, False)
                produce(j, 2 * a + 3, 1)
                consume(2 * a + 2, 0, False)
                return carry

            lax.fori_loop(0, (n_un - 1) // 2, pair, 0)

            @pl.when(n_un % 2 == 1)
            def _():
                produce(nxt, 0, 2)
                consume(n_un, 1, True)

            @pl.when(n_un % 2 == 0)
            def _():
                produce(j, n_un, 0)
                consume(n_un - 1, 1, False)
                produce(nxt, 0, 2)
                consume(n_un, 0, True)

        dk_ref[0] = dk_s[...]
        dv_ref[0] = dv_s[...]

    head = lambda h, j: (h, 0, 0)
    rowv = pl.BlockSpec((1, n, 1, t), lambda h, j: (h, 0, 0, 0))
    return pl.pallas_call(
        body, name=name, grid=(nh, n),
        in_specs=[pl.BlockSpec((1, s, dk), head),
                  pl.BlockSpec((s, dv), lambda h, j: (0, h)),
                  rowv, rowv,
                  pl.BlockSpec((1, s, dk), head),
                  pl.BlockSpec((1, s, 2 * dv), head)],
        out_specs=[pl.BlockSpec((1, s, dk), head),
                   pl.BlockSpec((1, t, dk), lambda h, g: (h, n - 1 - g, 0)),
                   pl.BlockSpec((1, t, dv), lambda h, g: (h, n - 1 - g, 0))],
        out_shape=[_sds((nh, s, dk), F32), _sds((nh, s, dk), F32), _sds((nh, s, dv), F32)],
        scratch_shapes=[pltpu.VMEM((t, dk), F32), pltpu.VMEM((t, dv), F32),
                        pltpu.VMEM((3, t, t), F32), pltpu.VMEM((3, t, t), F32)],
        compiler_params=_cp(("arbitrary", "arbitrary")),
    )(qf, do, lse_t, delta_t, kf, va)


def _mla_bwd(dqf, dkf, dvf, q_raw, kv, z, dz, qn, kn, w_q_up, w_kv_up, g_ql, g_kvl, c_t, s1_t, s2_t,
             gqn, gqr, gkn, gkr, *, name):
    s = q_raw.shape[0]
    t = min(ROW_T, s)
    scale = 1.0 / math.sqrt(QK_DIM)
    o_ql, o_kvl, o_kr = (seg[0] - SEG_LAT[0] for seg in (SEG_QL, SEG_KVL, SEG_KR))
    tn = (((0,), (0,)), ((), ()))
    nt = (((1,), (1,)), ((), ()))

    def body(dq_ref, dk_ref, dv_ref, q_ref, kv_ref, kr_ref, ql_ref, kvl_ref, qn_ref, kn_ref, wq_ref, wkv_ref,
             gq_ref, gk_ref, c_ref, s1_ref, s2_ref, gqn_ref, gqr_ref, gkn_ref, gkr_ref, dz_in_ref,
             dz_ref, gwq_ref, gwkv_ref, ggq_ref, ggk_ref, gql_ref, gkvl_ref, dqr_ref, dkv_ref):
        _acc_init([gwq_ref, gwkv_ref, ggq_ref, ggk_ref, gql_ref, gkvl_ref])
        c_v, s1_v, s2_v = c_ref[...], s1_ref[...], s2_ref[...]
        kr = kr_ref[...]
        kr_ss = _lanesum(kr * kr)
        dkr = jnp.zeros(kr.shape, F32)
        ggq_n = ggq_r = ggk_n = ggk_r = jnp.zeros((1, LANE), F32)

        def norm_bwd(n, r, rs, dyn, dyr, gn, gr):
            nh_, rh_ = n * rs, r * rs
            dnh, drh = dyn * gn, dyr * gr
            dot = (_lanesum(dnh * nh_) + _lanesum(drh * rh_)) * (1.0 / QK_DIM)
            return rs * (dnh - nh_ * dot), rs * (drh - rh_ * dot), _colsum(dyn * nh_), _colsum(dyr * rh_)

        for h in range(N_HEADS):
            n = q_ref[:, h * LANE:(h + 1) * LANE]
            r = q_ref[:, N_HEADS * LANE + h * LANE:N_HEADS * LANE + (h + 1) * LANE]
            rs = lax.rsqrt((_lanesum(n * n) + _lanesum(r * r)) * (1.0 / QK_DIM) + EPS)
            dyn = dq_ref[h, :, 0:LANE] * scale
            dyr = _rope_bwd(dq_ref[h, :, LANE:HEAD_PAD] * scale, c_v, s1_v, s2_v)
            dn, dr, g_n, g_r = norm_bwd(n, r, rs, dyn, dyr, gqn_ref[...], gqr_ref[...])
            dqr_ref[:, h * LANE:(h + 1) * LANE] = dn.astype(dqr_ref.dtype)
            dqr_ref[:, N_HEADS * LANE + h * LANE:N_HEADS * LANE + (h + 1) * LANE] = dr.astype(dqr_ref.dtype)
            ggq_n, ggq_r = ggq_n + g_n, ggq_r + g_r

            n = kv_ref[:, h * 2 * LANE:h * 2 * LANE + LANE]
            rs = lax.rsqrt((_lanesum(n * n) + kr_ss) * (1.0 / QK_DIM) + EPS)
            dyn = dk_ref[h, :, 0:LANE] * (1.0 / LOG2E)
            dyr = _rope_bwd(dk_ref[h, :, LANE:HEAD_PAD] * (1.0 / LOG2E), c_v, s1_v, s2_v)
            dn, dr, g_n, g_r = norm_bwd(n, kr, rs, dyn, dyr, gkn_ref[...], gkr_ref[...])
            dkv_ref[:, h * 2 * LANE:h * 2 * LANE + LANE] = dn.astype(dkv_ref.dtype)
            dkv_ref[:, h * 2 * LANE + LANE:(h + 1) * 2 * LANE] = dv_ref[h].astype(dkv_ref.dtype)
            dkr = dkr + dr
            ggk_n, ggk_r = ggk_n + g_n, ggk_r + g_r

        ggq_ref[:, 0:LANE] += ggq_n
        ggq_ref[:, LANE:] += ggq_r
        ggk_ref[:, 0:LANE] += ggk_n
        ggk_ref[:, LANE:] += ggk_r

        for d_ref, x_ref, w_ref, gw_ref, src, g_ref, off, gg_ref in (
                (dqr_ref, qn_ref, wq_ref, gwq_ref, ql_ref, gq_ref, o_ql, gql_ref),
                (dkv_ref, kn_ref, wkv_ref, gwkv_ref, kvl_ref, gk_ref, o_kvl, gkvl_ref)):
            dup = d_ref[...]
            gw_ref[...] += lax.dot_general(x_ref[...], dup, tn, preferred_element_type=F32)
            dy = lax.dot_general(dup, w_ref[...], nt, preferred_element_type=F32)
            v = src[...]
            r = lax.rsqrt(jnp.mean(v * v, axis=-1, keepdims=True) + EPS)
            vh = v * r
            dvh = dy * g_ref[...]
            dz_ref[:, off:off + v.shape[1]] = (
                r * (dvh - vh * jnp.mean(dvh * vh, axis=-1, keepdims=True))).astype(dz_ref.dtype)
            gg_ref[...] += _colsum(dy * vh)
        dz_ref[:, o_kr:o_kr + LANE] = dkr.astype(dz_ref.dtype)
        dz_ref[:, o_kr + LANE:] = jnp.zeros((t, SEG_LAT[1] - o_kr - LANE), dz_ref.dtype)

    hspec = lambda w: pl.BlockSpec((N_HEADS, t, w), lambda i: (0, i, 0))
    whole = lambda a: pl.BlockSpec(a.shape, lambda i: (0, 0))
    wide = 2 * N_HEADS * LANE
    return pl.pallas_call(
        body, name=name, grid=(s // t,),
        in_specs=[hspec(HEAD_PAD), hspec(HEAD_PAD), hspec(V_DIM), _rowspec(t, wide), _rowspec(t, wide),
                  _rowspec(t, LANE, SEG_KR[0] // LANE), _rowspec(t, Q_LORA, SEG_QL[0] // Q_LORA),
                  _rowspec(t, KV_LORA, SEG_KVL[0] // KV_LORA), _rowspec(t, Q_LORA), _rowspec(t, KV_LORA),
                  whole(w_q_up), whole(w_kv_up), _vecspec(Q_LORA), _vecspec(KV_LORA),
                  _rowspec(t, LANE), _rowspec(t, LANE), _rowspec(t, LANE),
                  _vecspec(LANE), _vecspec(LANE), _vecspec(LANE), _vecspec(LANE), _ANY],
        out_specs=[_rowspec(t, SEG_LAT[1], SEG_LAT[0] // SEG_LAT[1]), whole(w_q_up), whole(w_kv_up),
                   _vecspec(2 * LANE), _vecspec(2 * LANE), _vecspec(Q_LORA), _vecspec(KV_LORA)],
        out_shape=[_sds(dz.shape, dz.dtype), _sds(w_q_up.shape, F32), _sds(w_kv_up.shape, F32),
                   _sds((1, 2 * LANE), F32), _sds((1, 2 * LANE), F32), _sds((1, Q_LORA), F32),
                   _sds((1, KV_LORA), F32)],
        scratch_shapes=[pltpu.VMEM((t, wide), MXU_DTYPE), pltpu.VMEM((t, wide), MXU_DTYPE)],
        input_output_aliases={21: 0},
        compiler_params=_cp(("arbitrary",)),
    )(dqf, dkf, dvf, q_raw, kv, z, z, z, qn, kn, w_q_up, w_kv_up, g_ql, g_kvl, c_t, s1_t, s2_t,
      gqn, gqr, gkn, gkr, dz)


def _prenorm_bwd(dh, x, gxo, g, sc1p, below=None, *, name):
    s, d = x.shape
    t = min(2 * ROW_T if below is None else ROW_T, s)
    nb = 0 if below is None else 2

    def body(dh_ref, x_ref, gx_ref, g_ref, sc_ref, *rest):
        dx_ref, dsh_ref, dsc_ref, gg_ref = rest[nb:nb + 4]
        _acc_init([dsh_ref, dsc_ref, gg_ref])
        xv, dhv = x_ref[...], dh_ref[...]
        r = lax.rsqrt(jnp.mean(xv * xv, axis=-1, keepdims=True) + EPS)
        xn = xv * r
        dsh_ref[...] += _colsum(dhv)
        dsc_ref[...] += _colsum(dhv * (xn * g_ref[...]))
        dm = dhv * sc_ref[...]
        gg_ref[...] += _colsum(dm * xn)
        dxn = dm * g_ref[...]
        dx = gx_ref[...] + r * (dxn - xn * jnp.mean(dxn * xn, axis=-1, keepdims=True))
        dx_ref[...] = dx
        if below is not None:
            _acc_init([rest[nb + 5]])
            _gated_residual_bwd(dx, rest[0], rest[1], rest[nb + 4], rest[nb + 5])

    vec_out = [_vecspec(d), _vecspec(d), _vecspec(d)]
    vec_shape = [_sds((1, d), F32)] * 3
    return pl.pallas_call(
        body, name=name, grid=(s // t,),
        in_specs=[_rowspec(t, d), _rowspec(t, d), _rowspec(t, d), _vecspec(d), _vecspec(d)]
        + ([_rowspec(t, d), _vecspec(d)] if below is not None else []),
        out_specs=[_rowspec(t, d)] + vec_out + ([_rowspec(t, d), _vecspec(d)] if below is not None else []),
        out_shape=[_sds((s, d), F32)] + vec_shape
        + ([_sds((s, d), MXU_DTYPE), _sds((1, d), F32)] if below is not None else []),
        compiler_params=_cp(("arbitrary",)),
    )(dh, x, gxo, g, sc1p, *(below if below is not None else ()))


def _ada_fwd(c_all, ada_w, ada_b_cols, *, name):
    nl, d, cols = ada_w.shape

    def body(c_ref, w_ref, b_ref, o_ref):
        ca = _silu(c_ref[...]).astype(MXU_DTYPE)
        o_ref[0] = jnp.dot(ca, w_ref[0].astype(MXU_DTYPE), preferred_element_type=F32) + b_ref[0]

    return pl.pallas_call(
        body, name=name, grid=(nl,),
        in_specs=[pl.BlockSpec((N_DEV, d), lambda l: (0, 0)), pl.BlockSpec((1, d, cols), lambda l: (l, 0, 0)),
                  pl.BlockSpec((1, 1, cols), lambda l: (l, 0, 0))],
        out_specs=pl.BlockSpec((1, N_DEV, cols), lambda l: (l, 0, 0)),
        out_shape=_sds((nl, N_DEV, cols), F32),
        compiler_params=_cp(("parallel",)),
    )(c_all, ada_w, ada_b_cols)


def _ada_bwd(c_all_t, dmod_cols, *, name):
    nl, _, cols = dmod_cols.shape
    d = c_all_t.shape[0]

    def body(c_ref, dm_ref, o_ref):
        ca = _silu(c_ref[...]).astype(MXU_DTYPE)
        o_ref[0] = jnp.dot(ca, dm_ref[0].astype(MXU_DTYPE), preferred_element_type=F32)

    return pl.pallas_call(
        body, name=name, grid=(nl,),
        in_specs=[pl.BlockSpec((d, N_DEV), lambda l: (0, 0)), pl.BlockSpec((1, N_DEV, cols), lambda l: (l, 0, 0))],
        out_specs=pl.BlockSpec((1, d, cols), lambda l: (l, 0, 0)),
        out_shape=_sds((nl, d, cols), F32),
        compiler_params=_cp(("parallel",)),
    )(c_all_t, dmod_cols)


def _adamw_math(g, w, m, v):
    mn = ADAM_B1 * m + (1.0 - ADAM_B1) * g
    vn = ADAM_B2 * v + (1.0 - ADAM_B2) * (g * g)
    m_hat = mn / (1.0 - ADAM_B1 ** ADAM_STEP)
    v_hat = vn / (1.0 - ADAM_B2 ** ADAM_STEP)
    return -ADAM_LR * (m_hat / (jnp.sqrt(v_hat) + ADAM_EPS) + ADAM_WD * w), mn, vn


def _adamw_small(items, *, name):
    n = len(items)
    shapes = [it[1].shape for it in items]
    flat = lambda a, lead: a.reshape(lead + (-1, a.shape[-1]))
    operands = []
    for gp, w, m, v in items:
        operands += [flat(gp, (gp.shape[0],)), flat(w, ()), flat(m, ()), flat(v, ())]
    nparts = [it[0].shape[0] for it in items]

    def body(*refs):
        ins, outs = refs[:4 * n], refs[4 * n:]
        for i in range(n):
            g_ref, w_ref, m_ref, v_ref = ins[4 * i:4 * i + 4]
            g = g_ref[0].astype(F32)
            for p in range(1, nparts[i]):
                g = g + g_ref[p].astype(F32)
            outs[4 * i][...] = g
            outs[4 * i + 1][...], outs[4 * i + 2][...], outs[4 * i + 3][...] = _adamw_math(
                g, w_ref[...], m_ref[...], v_ref[...])

    out_shape = []
    for it in items:
        out_shape += [_sds(flat(it[1], ()).shape, F32)] * 4
    outs = pl.pallas_call(body, name=name, out_shape=out_shape, compiler_params=_cp())(*operands)
    return [tuple(o.reshape(shp) for o in outs[4 * i:4 * i + 4]) for i, shp in enumerate(shapes)]


def _adamw_layer(gparts, w, m, v, layer, prev, *, name):
    shape = w.shape
    nl, cols = shape[0], shape[-1]
    rows = w.size // cols // nl
    npart = gparts.shape[0]
    g3 = gparts.reshape(npart, rows, cols)
    w3, m3, v3 = (a.reshape(nl, rows, cols) for a in (w, m, v))
    fits = [t for t in range(min(rows, 256) // 8 * 8, 7, -8)
            if rows % t == 0 and npart * t * cols * g3.dtype.itemsize <= 2 * 1024 * 1024]
    t = fits[0] if fits else rows
    n_prev = 0 if prev is None else 4

    def body(g_ref, w_ref, m_ref, v_ref, *rest):
        go_ref, d_ref, mo_ref, vo_ref = rest[n_prev:]
        g = g_ref[0].astype(F32)
        for p in range(1, npart):
            g = g + g_ref[p].astype(F32)
        go_ref[0] = g
        d_ref[0], mo_ref[0], vo_ref[0] = _adamw_math(g, w_ref[0], m_ref[0], v_ref[0])

    spec = pl.BlockSpec((1, t, cols), lambda i: (layer, i, 0))
    outs = pl.pallas_call(
        body, name=name, grid=(rows // t,),
        in_specs=[pl.BlockSpec((npart, t, cols), lambda i: (0, i, 0)), spec, spec, spec] + [_ANY] * n_prev,
        out_specs=[spec] * 4, out_shape=[_sds((nl, rows, cols), F32)] * 4,
        input_output_aliases={4 + k: k for k in range(n_prev)},
        compiler_params=_cp(("parallel",)),
    )(g3, w3, m3, v3, *([] if prev is None else [a.reshape(nl, rows, cols) for a in prev]))
    return tuple(o.reshape(shape) for o in outs)


def _adamw(gparts, w, m, v, *, name):
    shape = w.shape
    cols = shape[-1]
    per_layer = isinstance(gparts, (list, tuple))
    nl = shape[0] if per_layer else 1
    rows = w.size // cols // nl
    glist = list(gparts) if per_layer else [gparts]
    npart = glist[0].shape[0]
    glist = [g.reshape(npart, rows, cols) for g in glist]
    w3, m3, v3 = (a.reshape(nl, rows, cols) for a in (w, m, v))
    budget = 2 * 1024 * 1024
    fits = [t for t in range(min(rows, 256) // 8 * 8, 7, -8)
            if rows % t == 0 and npart * t * cols * glist[0].dtype.itemsize <= budget]
    t = fits[0] if fits else rows
    nb = rows // t

    def body(*refs):
        g_refs = refs[:nl]
        w_ref, m_ref, v_ref, go_ref, d_ref, mo_ref, vo_ref, g_s = refs[nl:]
        layer = pl.program_id(0)
        for l in range(nl):
            @pl.when(layer == l)
            def _(l=l):
                g = g_refs[l][0].astype(F32)
                for p in range(1, npart):
                    g = g + g_refs[l][p].astype(F32)
                g_s[...] = g

        g = g_s[...]
        go_ref[0] = g
        d_ref[0], mo_ref[0], vo_ref[0] = _adamw_math(g, w_ref[0], m_ref[0], v_ref[0])

    def g_map(l):
        return lambda layer, i: (0, jnp.where(layer == l, i, jnp.where(layer < l, 0, nb - 1)), 0)

    spec = pl.BlockSpec((1, t, cols), lambda layer, i: (layer, i, 0))
    outs = pl.pallas_call(
        body, name=name, grid=(nl, nb),
        in_specs=[pl.BlockSpec((npart, t, cols), g_map(l)) for l in range(nl)] + [spec, spec, spec],
        out_specs=[spec] * 4, out_shape=[_sds((nl, rows, cols), F32)] * 4,
        scratch_shapes=[pltpu.VMEM((t, cols), F32)],
        compiler_params=_cp(("arbitrary", "arbitrary")),
    )(*glist, w3, m3, v3)
    return tuple(o.reshape(shape) for o in outs)


_ANY = pl.BlockSpec(memory_space=pl.ANY)


def _all_gather(blocks, *, name):
    na = len(blocks)

    def body(*refs):
        x_refs, out_refs = refs[:na], refs[na:2 * na]
        send_sems, recv_sems, local_sems = refs[2 * na:]
        x, y, c = lax.axis_index("x"), lax.axis_index("y"), lax.axis_index("c")
        me, sibling = (x, y, c), (x, y, 1 - c)
        chips = [(1 - x, y), (x, 1 - y), (1 - x, 1 - y)]

        def slot(a, px, py, pc):
            return out_refs[a].at[4 * px + 2 * py + pc]

        def copy(a, k, blk, to, src=None):
            return pltpu.make_async_remote_copy(
                src_ref=slot(a, *blk) if src is None else src, dst_ref=slot(a, *blk),
                send_sem=send_sems.at[7 * a + k], recv_sem=recv_sems.at[7 * a + k],
                device_id=to, device_id_type=MESH_ID)

        mine = [pltpu.make_async_copy(x_refs[a], slot(a, *me), local_sems.at[a]) for a in range(na)]
        for cp in mine:
            cp.start()
        first = []
        for a in range(na):
            first.append(copy(a, 0, me, sibling, src=x_refs[a]))
            first += [copy(a, 1 + j, me, (*chip, c), src=x_refs[a]) for j, chip in enumerate(chips)]
        for cp in first:
            cp.start()
        passed = []
        for a in range(na):
            for j, chip in enumerate(chips):
                copy(a, 1 + j, (*chip, c), me).wait_recv()
                fwd = copy(a, 4 + j, (*chip, c), sibling)
                fwd.start()
                passed.append(fwd)
        for a in range(na):
            copy(a, 0, sibling, me).wait_recv()
            for j, chip in enumerate(chips):
                copy(a, 4 + j, (*chip, 1 - c), me).wait_recv()
        for cp in first + passed:
            cp.wait_send()
        for cp in mine:
            cp.wait()

    outs = pl.pallas_call(
        body, name=name, in_specs=[_ANY] * na, out_specs=[_ANY] * na,
        out_shape=[_sds((N_DEV,) + b.shape, b.dtype) for b in blocks],
        scratch_shapes=[pltpu.SemaphoreType.DMA((7 * na,)), pltpu.SemaphoreType.DMA((7 * na,)),
                        pltpu.SemaphoreType.DMA((na,))],
    )(*blocks)
    return list(outs)


_HBM = pl.BlockSpec(memory_space=pltpu.HBM)
_SEM = pl.BlockSpec(memory_space=pltpu.SEMAPHORE)
_EFFECT = pltpu.SideEffectType.DATAFLOW_SIDE_EFFECTING


def _peers(x, y, c):
    out = []
    for k in range(1, N_DEV):
        out.append((1 - x if k & 4 else x, 1 - y if k & 2 else y, 1 - c if k & 1 else c))
    return out


def _own_slots(srcs, scatter, *, name, after=None):
    na = len(srcs)
    n_extra = 0 if after is None else 1
    me = (4 * lax.axis_index("x") + 2 * lax.axis_index("y") + lax.axis_index("c")).astype(jnp.int32).reshape(1)

    def body(me_ref, *refs):
        in_refs, out_refs = refs[:na], refs[na + n_extra:]
        for a in range(na):
            out_refs[a][0] = in_refs[a][0] if scatter else in_refs[a][...]

    def slot_spec(shard):
        zeros = (0,) * len(shard)
        return pl.BlockSpec((1,) + tuple(shard), lambda i, me_ref: (me_ref[0],) + zeros)

    def whole_spec(shape):
        zeros = (0,) * len(shape)
        return pl.BlockSpec(tuple(shape), lambda i, me_ref: zeros)

    shards = [s.shape[1:] if scatter else s.shape for s in srcs]
    in_specs = [slot_spec(sh) if scatter else whole_spec(sh) for sh in shards] + [_ANY] * n_extra
    outs = pl.pallas_call(
        body, name=name,
        grid_spec=pltpu.PrefetchScalarGridSpec(
            num_scalar_prefetch=1, grid=(1,), in_specs=in_specs, out_specs=[slot_spec(sh) for sh in shards]),
        out_shape=[_sds((N_DEV,) + tuple(sh), s.dtype) for sh, s in zip(shards, srcs)],
        compiler_params=_cp(("arbitrary",)),
    )(me, *srcs, *([] if after is None else [after]))
    return list(outs)


_N_COPIES = dict(scatter=7, gather=7, chips=4, forward=3)


def _exchange_copies(src_refs, land_refs, send_sems, recv_sems, mode):
    x, y, c = lax.axis_index("x"), lax.axis_index("y"), lax.axis_index("c")
    me = 4 * x + 2 * y + c
    nc = _N_COPIES[mode]
    chips = [(1 - x, y), (x, 1 - y), (1 - x, 1 - y)]
    cps = []
    for a in range(len(land_refs)):
        if mode in ("scatter", "gather"):
            plan = [((src_refs[a].at[4 * px + 2 * py + pc] if mode == "scatter" else src_refs[a]),
                     land_refs[a].at[me], (px, py, pc)) for px, py, pc in _peers(x, y, c)]
        elif mode == "chips":
            plan = [(src_refs[a], land_refs[a].at[me], to) for to in [(x, y, 1 - c)] + [(*ch, c) for ch in chips]]
        else:
            plan = [(land_refs[a].at[4 * px + 2 * py + c], land_refs[a].at[4 * px + 2 * py + c], (x, y, 1 - c))
                    for px, py in chips]
        for k, (src, dst, to) in enumerate(plan):
            cps.append(pltpu.make_async_remote_copy(
                src_ref=src, dst_ref=dst, send_sem=send_sems.at[nc * a + k], recv_sem=recv_sems.at[nc * a + k],
                device_id=to, device_id_type=MESH_ID))
    return cps


def _exchange_start(srcs, lands, mode, *, name):
    ns, nz = len(srcs), len(lands)
    nsem = _N_COPIES[mode] * nz

    def body(*refs):
        src_refs, land_refs = refs[:ns], refs[ns:ns + nz]
        send_sems, recv_sems = refs[ns + nz], refs[ns + nz + 1]
        token = refs[-1]
        for cp in _exchange_copies(src_refs, land_refs, send_sems, recv_sems, mode):
            cp.start()
        token[...] = jnp.zeros(token.shape, token.dtype)

    hbm = lambda a: pltpu.HBM(a.shape, a.dtype)
    outs = pl.pallas_call(
        body, name=name,
        out_shape=(pltpu.SemaphoreType.DMA((nsem,)), pltpu.SemaphoreType.DMA((nsem,)),
                   *[hbm(a) for a in srcs], *[hbm(a) for a in lands], _sds((8, LANE), F32)),
        in_specs=[_HBM] * (ns + nz),
        out_specs=(_SEM, _SEM, *[_HBM] * (ns + nz), pl.BlockSpec(memory_space=pltpu.VMEM)),
        input_output_aliases={i: 2 + i for i in range(ns + nz)},
        compiler_params=pltpu.CompilerParams(has_side_effects=_EFFECT),
    )(*[pltpu.with_memory_space_constraint(a, pltpu.HBM) for a in list(srcs) + list(lands)])
    return outs[0], outs[1], list(outs[2:2 + ns]), list(outs[2 + ns:2 + ns + nz]), outs[-1]


def _exchange_wait(send_sems, recv_sems, srcs, lands, after, mode, *, name):
    ns, nz = len(srcs), len(lands)

    def body(*refs):
        src_refs, land_refs = refs[:ns], refs[ns:ns + nz]
        s_sems, r_sems = refs[ns + nz], refs[ns + nz + 1]
        for cp in _exchange_copies(src_refs, land_refs, s_sems, r_sems, mode):
            cp.wait_send()
            cp.wait_recv()

    hbm = lambda a: pltpu.HBM(a.shape, a.dtype)
    outs = pl.pallas_call(
        body, name=name,
        out_shape=(*[hbm(a) for a in srcs], *[hbm(a) for a in lands]),
        in_specs=[_HBM] * (ns + nz) + [_SEM, _SEM, _ANY],
        out_specs=tuple([_HBM] * (ns + nz)),
        input_output_aliases={i: i for i in range(ns + nz)},
        compiler_params=pltpu.CompilerParams(has_side_effects=_EFFECT),
    )(*srcs, *lands, send_sems, recv_sems, after)
    return list(outs[ns:])


_WIN_SEGS = (("ql", 0, Q_LORA, SEG_QL[0]), ("kvl", Q_LORA, KV_LORA, SEG_KVL[0]),
             ("kr", Q_LORA + KV_LORA, ROPE, SEG_KR[0]), ("mg", Q_LORA + KV_LORA + ROPE, D_MLA, SEG_MG[0]),
             ("ci", Q_LORA + KV_LORA + ROPE + D_MLA, 2 * D_CONV, SEG_CI[0]),
             ("cg", Q_LORA + KV_LORA + ROPE + D_MLA + 2 * D_CONV, D_CONV, SEG_CG[0]))
_WIN_SHARD = IN_COLS // N_DEV


def _win_pieces():
    out = []
    for _, o, n, new in _WIN_SEGS:
        for j in range(N_DEV):
            lo, hi = max(o, j * _WIN_SHARD), min(o + n, (j + 1) * _WIN_SHARD)
            if lo < hi:
                out.append((j, lo - j * _WIN_SHARD, new + lo - o, hi - lo))
    return out


WIN_T = 512


def _win_assemble(w_all, *, name):
    d = w_all.shape[2]
    t = min(WIN_T, d)
    pieces = sorted(_win_pieces(), key=lambda p: p[2])
    assert all(lo % 8 == 0 and n % 8 == 0 for _, lo, _, n in pieces)

    def body(w_ref, o_ref):
        rows = [w_ref[j].astype(F32)[lo:lo + n, :] for j, lo, _, n in pieces]
        rows.append(jnp.zeros((IN_PAD - (SEG_KR[0] + ROPE), t), F32))
        o_ref[...] = jnp.concatenate(rows, axis=0).astype(o_ref.dtype)

    return pl.pallas_call(
        body, name=name, grid=(d // t,),
        in_specs=[pl.BlockSpec((N_DEV, _WIN_SHARD, t), lambda i: (0, 0, i))],
        out_specs=pl.BlockSpec((IN_PAD, t), lambda i: (0, i)), out_shape=_sds((IN_PAD, d), w_all.dtype),
        compiler_params=_cp(("parallel",)),
    )(w_all)


def _win_split(grad, *, name):
    d = grad.shape[1]
    t = min(WIN_T, d)
    by_shard = [sorted([p for p in _win_pieces() if p[0] == j], key=lambda p: p[1]) for j in range(N_DEV)]

    def body(g_ref, o_ref):
        for j in range(N_DEV):
            rows = [g_ref[new:new + n, :] for _, _, new, n in by_shard[j]]
            o_ref[j] = jnp.concatenate(rows, axis=0).astype(o_ref.dtype)

    return pl.pallas_call(
        body, name=name, grid=(d // t,),
        in_specs=[pl.BlockSpec((IN_PAD, t), lambda i: (0, i))],
        out_specs=pl.BlockSpec((N_DEV, _WIN_SHARD, t), lambda i: (0, 0, i)),
        out_shape=_sds((N_DEV, _WIN_SHARD, d), WIRE_DTYPE),
        compiler_params=_cp(("parallel",)),
    )(grad)


def _cols_to_shards(a):
    r, n = a.shape
    return a.reshape(r, N_DEV, n // N_DEV).transpose(1, 0, 2)


def _shards_to_cols(a):
    nd, r, w = a.shape
    return a.transpose(1, 0, 2).reshape(r, nd * w)


def _qup_permute(w):
    w3 = w.reshape(w.shape[0], N_HEADS, QK_DIM)
    nope = w3[:, :, :NOPE].reshape(w.shape[0], N_HEADS * NOPE)
    rope = jnp.pad(w3[:, :, NOPE:], ((0, 0), (0, 0), (0, LANE - ROPE))).reshape(w.shape[0], N_HEADS * LANE)
    return jnp.concatenate([nope, rope], axis=1)


def _qup_unpermute(g):
    r = g.shape[0]
    nope = g[:, :N_HEADS * NOPE].reshape(r, N_HEADS, NOPE)
    rope = g[:, N_HEADS * NOPE:].reshape(r, N_HEADS, LANE)[:, :, :ROPE]
    return jnp.concatenate([nope, rope], axis=2).reshape(r, N_HEADS * QK_DIM)


def _norm_tiles(g):
    return g[:NOPE].reshape(1, LANE), jnp.pad(g[NOPE:], (0, LANE - ROPE)).reshape(1, LANE)


def _rope_tiles(positions):
    inv_freq = 1.0 / (ROPE_THETA ** (jnp.arange(0, ROPE, 2, dtype=F32) / ROPE))
    ang = positions.astype(F32)[:, None] * inv_freq
    cos, sin = jnp.cos(ang), jnp.sin(ang)
    zq = jnp.zeros_like(cos)
    c_t = jnp.concatenate([cos, cos, zq, zq], axis=1)
    s1_t = jnp.concatenate([-sin, zq, zq, zq], axis=1)
    s2_t = jnp.concatenate([zq, sin, zq, zq], axis=1)
    return c_t, s1_t, s2_t


_BIG = ("w_in", "w_q_up", "w_kv_up", "w_pw", "w_out")
_COL_SHARDED = ("w_q_up", "w_kv_up")


def _unpack_rows(buf, shapes):
    out, r0 = [], 0
    lead = buf.shape[:-2]
    for shp in shapes:
        n = math.prod(shp) // LANE
        out.append(buf[..., r0:r0 + n, :].reshape(lead + tuple(shp)))
        r0 += n
    return out


_SMALL = (("dmod", 3 * D_MODEL), ("norm_g", D_MODEL), ("q_lat_g", Q_LORA), ("kv_lat_g", KV_LORA),
          ("q_norm_g", 2 * LANE), ("k_norm_g", 2 * LANE), ("glu_b", 2 * D_CONV), ("dw_w", HALO * D_CONV),
          ("dw_b", D_CONV), ("conv_ln_g", D_CONV), ("conv_ln_b", D_CONV), ("b_pw", D_CONV))


def _layer_fwd(x, p, rope, l, late=None):
    n = lambda s: f"{s}_l{l}"
    c_t, s1_t, s2_t = rope
    h = _prenorm(x, p["norm_g"], p["shift"], p["sc1p"], name=n("prenorm"))
    z = _mm(h, p["w_in"], tb=True, name=n("in_proj"), tn=IN_TILE, n_outer=True)
    if late is not None:
        p = {**p, **late(z)}
    qn, kn, q_raw, kv, qf, kf, vf = _mla_pre(z, p["w_q_up"], p["w_kv_up"], p["q_lat_g"], p["kv_lat_g"],
                                             c_t, s1_t, s2_t, *p["qk_tiles"], name=n("mla_pre"))
    o, lse = _flash_fwd(qf, kf, vf, name=n("flash_fwd"))
    u1, u3, u4m = _conv_fwd(z, p["glu_b"], p["dw_w"], p["dw_b"], p["conv_ln_g"], p["conv_ln_b"], p["w_pw"],
                            name=n("conv_fwd"))
    cat = _gate_cat(o, z, u4m, p["b_pw"], name=n("gate_cat"))
    y, x_next = _mm(cat, p["w_out"], name=n("out_proj"), tn=1024, residual=(x, p["gate"]))
    saved = dict(x=x, h=h, z=z, qn=qn, kn=kn, q_raw=q_raw, kv=kv, qf=qf, kf=kf, vf=vf, o=o, lse=lse,
                 u1=u1, u3=u3, u4m=u4m, cat=cat, y=y)
    return x_next, saved, p


def _layer_bwd(gxo, dy, dgate, p, sv, rope, l, below=None, hook_rest=None, hook_w_in=None):
    n = lambda s: f"{s}_l{l}"
    c_t, s1_t, s2_t = rope
    z = sv["z"]
    g_w_out = _mm(sv["cat"], dy, ta=True, name=n("g_w_out"), tm=1024, tn=1024, after=p.get("after_start"))
    do, delta, du4, g_b_pw, dz = _gate_bwd(dy, p["w_out"], sv["o"], z, sv["u4m"], p["b_pw"], name=n("gate_bwd"))
    g_w_pw = _mm(sv["u3"], du4, ta=True, name=n("g_w_pw"), tm=1024, tn=1024, tk=512)
    du3 = _mm(du4, p["w_pw"], tb=True, name=n("d_u3"), tn=1024)
    dz, g_ln_g, g_ln_b, g_dw_b, g_glu_b, g_dw_w = _conv_bwd(
        du3, sv["u1"], z, dz, p["glu_b"], p["dw_w"], p["conv_ln_g"], p["conv_ln_b"], name=n("conv_bwd"))
    dqf, dkf, dvf = _flash_bwd(sv["qf"], sv["kf"], sv["vf"], do, sv["lse"], delta.reshape(sv["lse"].shape),
                               name=n("flash_bwd"))
    dz, g_w_q_up, g_w_kv_up, g_qn, g_kn, g_ql, g_kvl = _mla_bwd(
        dqf, dkf, dvf, sv["q_raw"], sv["kv"], z, dz, sv["qn"], sv["kn"], p["w_q_up"], p["w_kv_up"],
        p["q_lat_g"], p["kv_lat_g"], c_t, s1_t, s2_t, *p["qk_tiles"], name=n("mla_bwd"))
    big = dict(w_q_up=g_w_q_up, w_kv_up=g_w_kv_up, w_pw=g_w_pw, w_out=g_w_out)
    after = None if hook_rest is None else hook_rest(big)
    g_w_in = _mm(dz, sv["h"], ta=True, name=n("g_w_in"), tm=512, tn=1024, after=after)
    big["w_in"] = g_w_in
    after = None if hook_w_in is None else hook_w_in(g_w_in)
    dh = _mm(dz, p["w_in"], name=n("d_h"), tn=1024, after=after)
    dx, dshift, dscale, g_norm, *down = _prenorm_bwd(dh, sv["x"], gxo, p["norm_g"], p["sc1p"], below,
                                                     name=n("prenorm_bwd"))
    small = dict(dmod=jnp.concatenate([dshift, dscale, dgate], axis=1), norm_g=g_norm, q_lat_g=g_ql, kv_lat_g=g_kvl,
                 q_norm_g=g_qn, k_norm_g=g_kn, glu_b=g_glu_b, dw_w=g_dw_w, dw_b=g_dw_b,
                 conv_ln_g=g_ln_g, conv_ln_b=g_ln_b, b_pw=g_b_pw)
    return (dx, *down), big, small


def _layer_params(l, full, mod_l, small):
    d = D_MODEL
    row = lambda a: a.reshape(1, -1)
    shift, scale, gate = mod_l[:, :d], mod_l[:, d:2 * d], mod_l[:, 2 * d:]
    dw_w = jnp.pad(full["dw_w"][l], ((0, HALO - CONV_K), (0, 0)))
    return dict(
        shift=shift, sc1p=1.0 + scale, gate=gate, norm_g=row(small["norm_g"][l]),
        **{k: full[k][l] for k in _BIG if k in full}, dw_w=dw_w,
        q_lat_g=row(small["q_lat_g"][l]), kv_lat_g=row(small["kv_lat_g"][l]),
        qk_tiles=_norm_tiles(small["q_norm_g"][l]) + _norm_tiles(small["k_norm_g"][l]),
        glu_b=row(small["glu_b"][l]), dw_b=row(small["dw_b"][l]), conv_ln_g=row(small["conv_ln_g"][l]),
        conv_ln_b=row(small["conv_ln_b"][l]), b_pw=row(small["b_pw"][l]))


def kernel(x, c, positions, ada_w, ada_b, norm_g, w_in, q_lat_g, w_q_up, kv_lat_g, w_kv_up, q_norm_g, k_norm_g, glu_b, dw_w, dw_b, conv_ln_g, conv_ln_b, w_pw, b_pw, w_out, loss_target, m_ada_w, m_ada_b, m_norm_g, m_w_in, m_q_lat_g, m_w_q_up, m_kv_lat_g, m_w_kv_up, m_q_norm_g, m_k_norm_g, m_glu_b, m_dw_w, m_dw_b, m_conv_ln_g, m_conv_ln_b, m_w_pw, m_b_pw, m_w_out, v_ada_w, v_ada_b, v_norm_g, v_w_in, v_q_lat_g, v_w_q_up, v_kv_lat_g, v_w_kv_up, v_q_norm_g, v_k_norm_g, v_glu_b, v_dw_w, v_dw_b, v_conv_ln_g, v_conv_ln_b, v_w_pw, v_b_pw, v_w_out):
    names = ("ada_w", "ada_b", "norm_g", "w_in", "q_lat_g", "w_q_up", "kv_lat_g", "w_kv_up", "q_norm_g",
             "k_norm_g", "glu_b", "dw_w", "dw_b", "conv_ln_g", "conv_ln_b", "w_pw", "b_pw", "w_out")
    w_loc = dict(zip(names, (ada_w, ada_b, norm_g, w_in, q_lat_g, w_q_up, kv_lat_g, w_kv_up, q_norm_g, k_norm_g,
                             glu_b, dw_w, dw_b, conv_ln_g, conv_ln_b, w_pw, b_pw, w_out)))
    m_loc = dict(zip(names, (m_ada_w, m_ada_b, m_norm_g, m_w_in, m_q_lat_g, m_w_q_up, m_kv_lat_g, m_w_kv_up,
                             m_q_norm_g, m_k_norm_g, m_glu_b, m_dw_w, m_dw_b, m_conv_ln_g, m_conv_ln_b, m_w_pw,
                             m_b_pw, m_w_out)))
    v_loc = dict(zip(names, (v_ada_w, v_ada_b, v_norm_g, v_w_in, v_q_lat_g, v_w_q_up, v_kv_lat_g, v_w_kv_up,
                             v_q_norm_g, v_k_norm_g, v_glu_b, v_dw_w, v_dw_b, v_conv_ln_g, v_conv_ln_b, v_w_pw,
                             v_b_pw, v_w_out)))
    nl, d = N_LAYERS, D_MODEL
    me = 4 * lax.axis_index("x") + 2 * lax.axis_index("y") + lax.axis_index("c")
    x2, tgt = x[0], loss_target[0]
    ada_cols = ada_w.shape[-1]

    tr = lambda a: jnp.swapaxes(a, 1, 2)
    w_loc, m_loc, v_loc = ({**dd, "w_in": tr(dd["w_in"])} for dd in (w_loc, m_loc, v_loc))
    w_in0 = [w_loc["w_in"][0].astype(WIRE_DTYPE)]
    fly_c = _exchange_start(w_in0, _own_slots(w_in0, False, name="own_w_in_l0"), "chips", name="gather_start_w_in_l0")
    held = dict(c=c, positions=positions, ada_b=ada_b, norm_g=norm_g, q_lat_g=q_lat_g, kv_lat_g=kv_lat_g,
                q_norm_g=q_norm_g, k_norm_g=k_norm_g, glu_b=glu_b, dw_w=dw_w, dw_b=dw_b, conv_ln_g=conv_ln_g,
                conv_ln_b=conv_ln_b, b_pw=b_pw, big={k: w_loc[k] for k in _BIG})
    tok_c, held = lax.optimization_barrier((fly_c[4], held))
    c, positions, ada_b, norm_g, q_lat_g, kv_lat_g, q_norm_g, k_norm_g, glu_b, dw_w, dw_b, conv_ln_g, conv_ln_b, b_pw = (
        held[k] for k in ("c", "positions", "ada_b", "norm_g", "q_lat_g", "kv_lat_g", "q_norm_g", "k_norm_g", "glu_b",
                          "dw_w", "dw_b", "conv_ln_g", "conv_ln_b", "b_pw"))
    wire = {k: held["big"][k].astype(WIRE_DTYPE) for k in _BIG}

    dw_pad = jnp.pad(dw_w, ((0, 0), (0, HALO - CONV_K), (0, 0)))
    c_rows = c.reshape(d // LANE, LANE) + tok_c[0:1, :]
    c_all, dw_all = _all_gather([c_rows, dw_pad], name="gather_c")
    c_all = c_all.reshape(N_DEV, d)
    ada_b_cols = lax.dynamic_slice_in_dim(ada_b, me * ada_cols, ada_cols, axis=1).reshape(nl, 1, ada_cols)
    mod_cols = _ada_fwd(c_all, ada_w, ada_b_cols, name="ada_fwd")
    mod_all = _all_gather([mod_cols], name="gather_mod")[0]
    mod_me = lax.dynamic_index_in_dim(mod_all, me, axis=2, keepdims=False)
    mod = mod_me.transpose(1, 0, 2).reshape(nl, 1, N_DEV * ada_cols)

    from_chips = _exchange_wait(*fly_c[:4], mod, "chips", name="gather_wait_w_in_l0")
    fly_f = _exchange_start([], from_chips, "forward", name="forward_start_w_in_l0")
    w_in_all0 = _exchange_wait(*fly_f[:4], fly_f[4], "forward", name="forward_wait_w_in_l0")[0]
    rest0 = [wire[k][0] for k in _BIG[1:]]
    fly_r0, fly_w1 = {}, {}
    fly_r0["x"] = _exchange_start(rest0, _own_slots(rest0, False, name="own_weights_l0_rest", after=w_in_all0),
                                  "gather", name="gather_start_l0_rest")

    def layout_rest(parts):
        return dict(w_q_up=_qup_permute(_shards_to_cols(parts[0])), w_kv_up=_shards_to_cols(parts[1]),
                    w_pw=parts[2].reshape(D_CONV, D_CONV), w_out=parts[3].reshape(D_MLA + D_CONV, d))

    small_in = dict(norm_g=norm_g, q_lat_g=q_lat_g, kv_lat_g=kv_lat_g, q_norm_g=q_norm_g, k_norm_g=k_norm_g,
                    glu_b=glu_b, dw_b=dw_b, conv_ln_g=conv_ln_g, conv_ln_b=conv_ln_b, b_pw=b_pw)
    dw_full = [_shards_to_cols(dw_all[:, l])[:CONV_K] for l in range(nl)]
    rope = _rope_tiles(positions[0])

    def layer_params(l, w_in_all, rest, mod_l):
        full = dict(dw_w=dw_full)
        if w_in_all is not None:
            full["w_in"] = {l: _win_assemble(w_in_all, name=f"w_in_assemble_l{l}")}
        if rest is not None:
            full.update({k: {l: a} for k, a in layout_rest(rest).items()})
        return _layer_params(l, full, mod_l, small_in)

    def late_l0(z):
        parts = _exchange_wait(*fly_r0["x"][:4], z, "gather", name="gather_wait_l0_rest")
        src1 = [wire[k][1] for k in _BIG]
        fly_w1["x"] = _exchange_start(src1, _own_slots(src1, False, name="own_weights_l1", after=parts[0]), "gather",
                                      name="gather_start_l1")
        late = layout_rest(parts)
        late["q_lat_g"] = small_in["q_lat_g"][0].reshape(1, -1) + fly_w1["x"][4][0, 0]
        return late

    params, saved = [None] * nl, [None] * nl
    p0 = layer_params(0, w_in_all0, None, mod[0] + fly_r0["x"][4][0, 0])
    xs, saved[0], params[0] = _layer_fwd(x2, p0, rope, 0, late=late_l0)
    parts1 = _exchange_wait(*fly_w1["x"][:4], xs, "gather", name="gather_wait_l1")
    params[1] = layer_params(1, parts1[0], parts1[1:], mod[1])
    xs, saved[1], _ = _layer_fwd(xs, params[1], rope, 1)
    gx, loss_part, dy, dgate = _loss_head(xs, tgt, saved[1]["y"], params[1]["gate"], name="loss_head")
    loss = lax.psum(loss_part[0, 0], ("x", "y", "c"))

    def shard_major(k, g):
        if k == "w_q_up":
            g = _qup_unpermute(g)
        if k in _COL_SHARDED:
            return _cols_to_shards(g)
        return g.reshape((N_DEV, g.shape[0] // N_DEV, g.shape[1]))

    def scatter_start(send, tag):
        lands = _own_slots(send, True, name=f"own_grads_{tag}")
        return _exchange_start(send, lands, "scatter", name=f"scatter_start_{tag}")

    def wire_rest(big):
        return [shard_major(k, big[k]).astype(WIRE_DTYPE) for k in _BIG[1:]]

    big_g, small_g, flying = [None] * nl, [None] * nl, {}
    (gx, dy, dgate), big_g[1], small_g[1] = _layer_bwd(gx, dy, dgate, params[1], saved[1], rope, 1,
                                                       below=(saved[0]["y"], params[0]["gate"]))
    flying["l1"] = scatter_start([_win_split(big_g[1]["w_in"], name="w_in_split_l1")] + wire_rest(big_g[1]), "l1")
    p0 = dict(params[0], after_start=flying["l1"][4], b_pw=params[0]["b_pw"] + flying["l1"][4][0, 0])

    def start_rest_l0(big):
        flying["l0_rest"] = scatter_start(wire_rest(big), "l0_rest")
        return flying["l0_rest"][4]

    res, arrived = {}, [None] * nl

    def start_w_in_l0(g_w_in):
        flying["l0_w_in"] = scatter_start([_win_split(g_w_in, name="w_in_split_l0")], "l0_w_in")
        tok = flying["l0_w_in"][4]
        arrived[1] = _exchange_wait(*flying["l1"][:4], tok, "scatter", name="scatter_wait_l1")
        arrived[0] = [None] + _exchange_wait(*flying["l0_rest"][:4], tok, "scatter", name="scatter_wait_l0_rest")
        for i, k in enumerate(_BIG):
            if i > 0:
                res[k] = _adamw([arrived[l][i] for l in range(nl)], w_loc[k], m_loc[k], v_loc[k], name=f"adamw_{k}")
        res["w_in_l1"] = _adamw_layer(arrived[1][0], w_loc["w_in"], m_loc["w_in"], v_loc["w_in"], 1, None,
                                      name="adamw_w_in_l1")
        return res["w_in_l1"][0]

    (gx,), big_g[0], small_g[0] = _layer_bwd(gx, dy, dgate, p0, saved[0], rope, 0, hook_rest=start_rest_l0,
                                             hook_w_in=start_w_in_l0)

    tile = 8 * LANE
    padded = [(k, nn, -(-nn // tile) * tile) for k, nn in _SMALL]
    spk = jnp.concatenate([jnp.pad(small_g[l][k].reshape(-1), (0, np_ - nn)).reshape(-1, LANE)
                           for l in range(nl) for k, nn, np_ in padded], axis=0)
    s_all = _all_gather([spk], name="gather_small_grads")[0]
    s_rows = sum(np_ for _, _, np_ in padded) // LANE
    s_all = s_all.reshape(N_DEV, nl, s_rows, LANE)
    s_parts = {k: a[..., :nn] for (k, nn, _), a in
               zip(padded, _unpack_rows(s_all, [(np_,) for _, _, np_ in padded]))}

    dmod_all = s_parts["dmod"]
    dmod_cols = lax.dynamic_slice_in_dim(dmod_all, me * ada_cols, ada_cols, axis=2).transpose(1, 0, 2)
    g_ada_w = _ada_bwd(c_all.T, dmod_cols, name="ada_bwd")
    gp = {}
    gp["ada_w"] = g_ada_w[None]
    gp["ada_b"] = dmod_all
    for k in ("norm_g", "q_lat_g", "kv_lat_g", "glu_b", "dw_b", "conv_ln_g", "conv_ln_b", "b_pw"):
        gp[k] = s_parts[k]
    for k in ("q_norm_g", "k_norm_g"):
        t = s_parts[k]
        gp[k] = jnp.concatenate([t[..., :NOPE], t[..., LANE:LANE + ROPE]], axis=-1)
    dw_g = s_parts["dw_w"].reshape(N_DEV, nl, HALO, D_CONV)[:, :, :CONV_K]
    gp["dw_w"] = lax.dynamic_slice_in_dim(dw_g, me * LANE, LANE, axis=3)

    res["ada_w"] = _adamw(gp["ada_w"], w_loc["ada_w"], m_loc["ada_w"], v_loc["ada_w"], name="adamw_ada_w")
    small_names = [k for k in names if k not in _BIG and k != "ada_w"]
    res.update(zip(small_names, _adamw_small([(gp[k], w_loc[k], m_loc[k], v_loc[k]) for k in small_names],
                                             name="adamw_small")))
    arrived[0][0] = _exchange_wait(*flying["l0_w_in"][:4], res["ada_w"][1], "scatter", name="scatter_wait_l0_w_in")[0]
    w_in_res = _adamw_layer(arrived[0][0], w_loc["w_in"], m_loc["w_in"], v_loc["w_in"], 0, res.pop("w_in_l1"),
                            name="adamw_w_in_l0")
    res["w_in"] = tuple(tr(a) for a in w_in_res)
    out = [loss, gx[None]]
    for idx in range(4):
        out += [res[k][idx] for k in names]
    return tuple(out)
```

```python
import functools
import math

import jax
import jax.numpy as jnp
from jax import lax
from jax.experimental import pallas as pl
from jax.experimental.pallas import tpu as pltpu

F32 = jnp.float32
MXU_DTYPE = jnp.bfloat16
WIRE_DTYPE = jnp.bfloat16

D_MODEL = 2048
N_LAYERS = 2
N_DEV = 8
N_HEADS = 8
NOPE = 128
ROPE = 64
V_DIM = 128
QK_DIM = NOPE + ROPE
Q_LORA = 512
KV_LORA = 256
D_MLA = N_HEADS * V_DIM
D_CONV = 1024
CONV_K = 31
ROPE_THETA = 10000.0
EPS = 1e-6
LANE = 128
HEAD_PAD = 2 * LANE
HALO = 32

SEG_CI = (0, 2 * D_CONV)
SEG_MG = (2 * D_CONV, D_MLA)
SEG_CG = (2 * D_CONV + D_MLA, D_CONV)
SEG_QL = (2 * D_CONV + D_MLA + D_CONV, Q_LORA)
SEG_KVL = (SEG_QL[0] + Q_LORA, KV_LORA)
SEG_KR = (SEG_KVL[0] + KV_LORA, LANE)
SEG_LAT = (SEG_QL[0], 1024)
IN_PAD = SEG_LAT[0] + SEG_LAT[1]
IN_TILE = IN_PAD // 4
assert SEG_KR[0] + LANE <= IN_PAD and SEG_LAT[0] % SEG_LAT[1] == 0
IN_COLS = Q_LORA + KV_LORA + ROPE + D_MLA + 2 * D_CONV + D_CONV

ADAM_LR = 0.001
ADAM_B1 = 0.9
ADAM_B2 = 0.999
ADAM_EPS = 1e-08
ADAM_WD = 0.01
ADAM_STEP = 10

VMEM_LIMIT = 56 * 1024 * 1024
ATT_T = 512
ROW_T = 256
CONV_T = 256
MESH_ID = pl.DeviceIdType.MESH


def _cp(sem=None):
    kw = dict(vmem_limit_bytes=VMEM_LIMIT)
    if sem is not None:
        kw["dimension_semantics"] = sem
    return pltpu.CompilerParams(**kw)


def _sds(shape, dtype):
    return jax.ShapeDtypeStruct(shape, dtype)


def _silu(x):
    return x * jax.nn.sigmoid(x)


def _dsilu(x):
    s = jax.nn.sigmoid(x)
    return s * (1.0 + x * (1.0 - s))


def _rowspec(t, width, col=0):
    return pl.BlockSpec((t, width), lambda i: (i, col))


def _vecspec(width):
    return pl.BlockSpec((1, width), lambda i: (0, 0))


def _colsum(v):
    return jnp.sum(v, axis=0, keepdims=True)


def _mm(a, b, *, name, ta=False, tb=False, out_dtype=F32, tm=512, tn=512, tk=None, n_outer=False, after=None,
        residual=None):
    if ta:
        kdim, m = a.shape
    else:
        m, kdim = a.shape
    if tb:
        n, k2 = b.shape
    else:
        k2, n = b.shape
    assert kdim == k2, (a.shape, b.shape)
    tm, tn = min(tm, m), min(tn, n)
    tk = kdim if tk is None else min(tk, kdim)
    assert m % tm == 0 and n % tn == 0 and kdim % tk == 0, (m, n, kdim, tm, tn, tk)
    nk = kdim // tk
    dims = (((0 if ta else 1,), (1 if tb else 0,)), ((), ()))

    n_extra = 0 if after is None else 1
    assert residual is None or nk == 1

    def body(a_ref, b_ref, *rest):
        if residual is not None:
            x_ref, gate_ref = rest[:2]
            rest = rest[2:]
        o_ref, scratch = rest[n_extra], rest[n_extra + 1:]
        prod = lax.dot_general(a_ref[...].astype(MXU_DTYPE), b_ref[...].astype(MXU_DTYPE), dims,
                               preferred_element_type=F32)
        if residual is not None:
            o_ref[...] = prod.astype(o_ref.dtype)
            scratch[0][...] = x_ref[...] + gate_ref[...] * prod
        elif nk == 1:
            o_ref[...] = prod.astype(o_ref.dtype)
        else:
            acc = scratch[0]
            k = pl.program_id(2)

            @pl.when(k == 0)
            def _():
                acc[...] = prod

            @pl.when(k > 0)
            def _():
                acc[...] += prod

            @pl.when(k == nk - 1)
            def _():
                o_ref[...] = acc[...].astype(o_ref.dtype)

    if n_outer:
        ij = lambda g0, g1: (g1, g0)
        grid = (n // tn, m // tm, nk)
    else:
        ij = lambda g0, g1: (g0, g1)
        grid = (m // tm, n // tn, nk)

    def a_map(g0, g1, k):
        i, _ = ij(g0, g1)
        return (k, i) if ta else (i, k)

    def b_map(g0, g1, k):
        _, j = ij(g0, g1)
        return (j, k) if tb else (k, j)

    def o_map(g0, g1, k):
        return ij(g0, g1)

    in_specs = [pl.BlockSpec((tk, tm) if ta else (tm, tk), a_map), pl.BlockSpec((tn, tk) if tb else (tk, tn), b_map)]
    operands = [a, b]
    out_specs, out_shape = pl.BlockSpec((tm, tn), o_map), _sds((m, n), out_dtype)
    if residual is not None:
        in_specs += [pl.BlockSpec((tm, tn), o_map), pl.BlockSpec((1, tn), lambda g0, g1, k: (0, ij(g0, g1)[1]))]
        operands += list(residual)
        out_specs, out_shape = [out_specs, pl.BlockSpec((tm, tn), o_map)], [out_shape, _sds((m, n), F32)]
    if after is not None:
        in_specs.append(_ANY)
        operands.append(after)
    return pl.pallas_call(
        body, name=name, grid=grid, in_specs=in_specs, out_specs=out_specs, out_shape=out_shape,
        scratch_shapes=[pltpu.VMEM((tm, tn), F32)] if nk > 1 else [],
        compiler_params=_cp(("parallel", "parallel", "arbitrary")),
    )(*operands)


def _prenorm(x, g, shift, sc1p, *, name):
    s, d = x.shape
    t = min(2 * ROW_T, s)

    def body(x_ref, g_ref, sh_ref, sc_ref, h_ref):
        xv = x_ref[...]
        r = lax.rsqrt(jnp.mean(xv * xv, axis=-1, keepdims=True) + EPS)
        h_ref[...] = ((xv * r) * g_ref[...] * sc_ref[...] + sh_ref[...]).astype(h_ref.dtype)

    return pl.pallas_call(
        body, name=name, grid=(s // t,),
        in_specs=[_rowspec(t, d), _vecspec(d), _vecspec(d), _vecspec(d)],
        out_specs=_rowspec(t, d), out_shape=_sds((s, d), MXU_DTYPE),
        compiler_params=_cp(("parallel",)),
    )(x, g, shift, sc1p)


def _rope_fwd(r, c_t, s1_t, s2_t):
    return r * c_t + pltpu.roll(r, LANE - ROPE // 2, 1) * s1_t + pltpu.roll(r, ROPE // 2, 1) * s2_t


def _rope_bwd(d, c_t, s1_t, s2_t):
    return d * c_t + pltpu.roll(d * s1_t, ROPE // 2, 1) + pltpu.roll(d * s2_t, LANE - ROPE // 2, 1)


def _lanesum(v):
    return jnp.sum(v, axis=-1, keepdims=True)


def _mla_pre(z, w_q_up, w_kv_up, g_ql, g_kvl, c_t, s1_t, s2_t, gqn, gqr, gkn, gkr, *, name):
    s = z.shape[0]
    t = min(2 * ROW_T, s)
    scale = LOG2E / math.sqrt(QK_DIM)
    wide = 2 * N_HEADS * LANE

    def body(ql_ref, kvl_ref, kr_ref, wq_ref, wkv_ref, gq_ref, gk_ref, c_ref, s1_ref, s2_ref,
             gqn_ref, gqr_ref, gkn_ref, gkr_ref, qn_ref, kn_ref, q_ref, kv_ref, qf_ref, kf_ref, vf_ref):
        for src, g_ref, dst, w_ref, up in ((ql_ref, gq_ref, qn_ref, wq_ref, q_ref),
                                           (kvl_ref, gk_ref, kn_ref, wkv_ref, kv_ref)):
            v = src[...]
            r = lax.rsqrt(jnp.mean(v * v, axis=-1, keepdims=True) + EPS)
            dst[...] = ((v * r) * g_ref[...]).astype(dst.dtype)
            up[...] = jnp.dot(dst[...], w_ref[...], preferred_element_type=F32)
        c_v, s1_v, s2_v = c_ref[...], s1_ref[...], s2_ref[...]
        kr = kr_ref[...]
        kr_ss = _lanesum(kr * kr)
        for h in range(N_HEADS):
            n = q_ref[:, h * LANE:(h + 1) * LANE]
            r = q_ref[:, N_HEADS * LANE + h * LANE:N_HEADS * LANE + (h + 1) * LANE]
            rs = lax.rsqrt((_lanesum(n * n) + _lanesum(r * r)) * (1.0 / QK_DIM) + EPS)
            qf_ref[h, :, 0:LANE] = (((n * rs) * gqn_ref[...]) * scale).astype(qf_ref.dtype)
            rr = _rope_fwd((r * rs) * gqr_ref[...], c_v, s1_v, s2_v)
            qf_ref[h, :, LANE:HEAD_PAD] = (rr * scale).astype(qf_ref.dtype)

            n = kv_ref[:, h * 2 * LANE:h * 2 * LANE + LANE]
            rs = lax.rsqrt((_lanesum(n * n) + kr_ss) * (1.0 / QK_DIM) + EPS)
            kf_ref[h, :, 0:LANE] = ((n * rs) * gkn_ref[...]).astype(kf_ref.dtype)
            kf_ref[h, :, LANE:HEAD_PAD] = _rope_fwd((kr * rs) * gkr_ref[...], c_v, s1_v, s2_v).astype(kf_ref.dtype)
            vf_ref[h, :, 0:V_DIM] = kv_ref[:, h * 2 * LANE + LANE:(h + 1) * 2 * LANE].astype(vf_ref.dtype)
            vf_ref[h, :, V_DIM:] = jnp.ones((t, V_DIM), vf_ref.dtype)

    hspec = lambda w: pl.BlockSpec((N_HEADS, t, w), lambda i: (0, i, 0))
    whole = lambda a: pl.BlockSpec(a.shape, lambda i: (0, 0))
    return pl.pallas_call(
        body, name=name, grid=(s // t,),
        in_specs=[_rowspec(t, Q_LORA, SEG_QL[0] // Q_LORA), _rowspec(t, KV_LORA, SEG_KVL[0] // KV_LORA),
                  _rowspec(t, LANE, SEG_KR[0] // LANE), whole(w_q_up), whole(w_kv_up),
                  _vecspec(Q_LORA), _vecspec(KV_LORA),
                  _rowspec(t, LANE), _rowspec(t, LANE), _rowspec(t, LANE),
                  _vecspec(LANE), _vecspec(LANE), _vecspec(LANE), _vecspec(LANE)],
        out_specs=[_rowspec(t, Q_LORA), _rowspec(t, KV_LORA), _rowspec(t, wide), _rowspec(t, wide),
                   hspec(HEAD_PAD), hspec(HEAD_PAD), hspec(2 * V_DIM)],
        out_shape=[_sds((s, Q_LORA), MXU_DTYPE), _sds((s, KV_LORA), MXU_DTYPE), _sds((s, wide), F32),
                   _sds((s, wide), F32), _sds((N_HEADS, s, HEAD_PAD), MXU_DTYPE),
                   _sds((N_HEADS, s, HEAD_PAD), MXU_DTYPE), _sds((N_HEADS, s, 2 * V_DIM), MXU_DTYPE)],
        compiler_params=_cp(("parallel",)),
    )(z, z, z, w_q_up, w_kv_up, g_ql, g_kvl, c_t, s1_t, s2_t, gqn, gqr, gkn, gkr)


def _causal_mask(t):
    row = lax.broadcasted_iota(jnp.int32, (t, t), 0)
    col = lax.broadcasted_iota(jnp.int32, (t, t), 1)
    return col <= row


NEG = -1e30
LOG2E = math.log2(math.e)


def _flash_fwd(qf, kf, va, *, name):
    nh, s, dk = qf.shape
    dv = va.shape[-1] // 2
    t = min(ATT_T, s)
    n = s // t
    assert dv == LANE and t % LANE == 0

    def body(q_ref, k_ref, v_ref, o_ref, lse_ref, m_s, acc_s, s_buf):
        i = pl.program_id(1)
        m_s[...] = jnp.full(m_s.shape, NEG, F32)
        acc_s[...] = jnp.zeros(acc_s.shape, F32)

        def rows_of(j):
            return pl.ds(pl.multiple_of(j * t, t), t)

        def scores(qi, j):
            return lax.dot_general(q_ref[0, rows_of(qi), :], k_ref[0, rows_of(j), :], (((1,), (1,)), ((), ())),
                                   preferred_element_type=F32)

        def consume(j, slot, masked):
            sc = s_buf[slot]
            if masked:
                sc = jnp.where(_causal_mask(t), sc, NEG)
            m_prev = m_s[...]
            m_new = jnp.maximum(m_prev, jnp.max(sc, axis=-1, keepdims=True))
            alpha = jnp.exp2(m_prev - m_new)
            p = jnp.exp2(sc - jnp.tile(m_new, (1, t // LANE)))
            acc_s[...] = jnp.tile(alpha, (1, 2)) * acc_s[...] + jnp.dot(
                p.astype(MXU_DTYPE), v_ref[0, rows_of(j), :], preferred_element_type=F32)
            m_s[...] = m_new

        nxt = jnp.minimum(i + 1, n - 1)

        @pl.when(i == 0)
        def _():
            s_buf[2] = scores(0, 0)
            consume(0, 2, True)
            s_buf[2] = scores(nxt, 0)

        @pl.when(i > 0)
        def _():
            s_buf[1] = scores(i, 1)
            consume(0, 2, False)

            def pair(a, carry):
                s_buf[0] = scores(i, 2 * a + 2)
                consume(2 * a + 1, 1, False)
                s_buf[1] = scores(i, 2 * a + 3)
                consume(2 * a + 2, 0, False)
                return carry

            lax.fori_loop(0, (i - 1) // 2, pair, 0)

            @pl.when(i % 2 == 1)
            def _():
                s_buf[2] = scores(nxt, 0)
                consume(i, 1, True)

            @pl.when(i % 2 == 0)
            def _():
                s_buf[0] = scores(i, i)
                consume(i - 1, 1, False)
                s_buf[2] = scores(nxt, 0)
                consume(i, 0, True)

        den = acc_s[:, dv:]
        o_ref[...] = acc_s[:, :dv] / den
        lse_ref[0, 0] = jnp.transpose(m_s[...] + jnp.log2(den))[0:1, :]

    head = lambda h, i: (h, 0, 0)
    return pl.pallas_call(
        body, name=name, grid=(nh, n),
        in_specs=[pl.BlockSpec((1, s, dk), head), pl.BlockSpec((1, s, dk), head), pl.BlockSpec((1, s, 2 * dv), head)],
        out_specs=[pl.BlockSpec((t, dv), lambda h, i: (i, h)),
                   pl.BlockSpec((1, 1, 1, t), lambda h, i: (h, i, 0, 0))],
        out_shape=[_sds((s, nh * dv), F32), _sds((nh, n, 1, t), F32)],
        scratch_shapes=[pltpu.VMEM((t, LANE), F32), pltpu.VMEM((t, 2 * dv), F32), pltpu.VMEM((3, t, t), F32)],
        compiler_params=_cp(("arbitrary", "arbitrary")),
    )(qf, kf, va)


def _shifted_copies(ext_ref):
    rows = ext_ref.shape[1] - 8
    for s in range(1, 8):
        ext_ref[s, 0:rows, :] = ext_ref[0, s:s + rows, :]


def _windows(ext_ref, offsets, t_rows, lane0, lanes):
    for s in range(8):
        group = [o for o in offsets if o % 8 == s]
        if not group:
            continue
        lo, hi = min(group) - s, max(group) - s
        wide = ext_ref[s, pl.ds(lo, hi - lo + t_rows), lane0:lane0 + lanes]
        for o in group:
            yield o, wide[o - s - lo:o - s - lo + t_rows]


def _dw_taps(ext_ref, w_ref, row0, t_rows, lane0, lanes, first_off):
    acc = None
    for off, win in _windows(ext_ref, [row0 + first_off + k for k in range(CONV_K)], t_rows, lane0, lanes):
        k = off - row0 - first_off
        term = w_ref[k:k + 1, lane0:lane0 + lanes] * win
        acc = term if acc is None else acc + term
    return acc


CONV_RC = 32
CONV_LC = 256


def _conv_fwd(z, glu_b, dw_w, dw_b, ln_g, ln_b, w_pw, *, name):
    s = z.shape[0]
    t = min(CONV_T, s)
    c2 = 2 * D_CONV
    hb = t // HALO

    def body(zm_ref, zh_ref, gb_ref, w_ref, wb_ref, g_ref, b_ref, wpw_ref, u1_ref, u3_ref, u4_ref, ext):
        i = pl.program_id(0)

        def glu(zv):
            ci = zv + gb_ref[...]
            return ci[:, :D_CONV] * jax.nn.sigmoid(ci[:, D_CONV:])

        ext[0, HALO:, :] = glu(zm_ref[...])
        ext[0, 0:HALO, :] = jnp.where(i > 0, glu(zh_ref[...]), 0.0)
        _shifted_copies(ext)
        for rc in range(0, t, CONV_RC):
            for lc in range(0, D_CONV, CONV_LC):
                acc = _dw_taps(ext, w_ref, rc, CONV_RC, lc, CONV_LC, HALO - (CONV_K - 1))
                u1_ref[rc:rc + CONV_RC, lc:lc + CONV_LC] = acc + wb_ref[:, lc:lc + CONV_LC]
        u1 = u1_ref[...]
        mu = jnp.mean(u1, axis=-1, keepdims=True)
        cen = u1 - mu
        var = jnp.mean(cen * cen, axis=-1, keepdims=True)
        u2 = (cen * lax.rsqrt(var + EPS)) * g_ref[...] + b_ref[...]
        u3_ref[...] = _silu(u2).astype(u3_ref.dtype)
        u4_ref[...] = jnp.dot(u3_ref[...], wpw_ref[...], preferred_element_type=F32)

    return pl.pallas_call(
        body, name=name, grid=(s // t,),
        in_specs=[_rowspec(t, c2), pl.BlockSpec((HALO, c2), lambda i: (jnp.maximum(i * hb - 1, 0), 0)),
                  _vecspec(c2), pl.BlockSpec((HALO, D_CONV), lambda i: (0, 0)), _vecspec(D_CONV),
                  _vecspec(D_CONV), _vecspec(D_CONV), pl.BlockSpec((D_CONV, D_CONV), lambda i: (0, 0))],
        out_specs=[_rowspec(t, D_CONV), _rowspec(t, D_CONV), _rowspec(t, D_CONV)],
        out_shape=[_sds((s, D_CONV), F32), _sds((s, D_CONV), MXU_DTYPE), _sds((s, D_CONV), F32)],
        scratch_shapes=[pltpu.VMEM((8, t + HALO, D_CONV), F32)],
        compiler_params=_cp(("parallel",)),
    )(z, z, glu_b, dw_w, dw_b, ln_g, ln_b, w_pw)


def _gate_cat(o, z, u4m, b_pw, *, name):
    s = o.shape[0]
    t = min(2 * ROW_T, s)

    def body(o_ref, mg_ref, u4_ref, cg_ref, b_ref, cat_ref):
        cat_ref[:, :D_MLA] = (o_ref[...] * _silu(mg_ref[...])).astype(cat_ref.dtype)
        cat_ref[:, D_MLA:] = ((u4_ref[...] + b_ref[...]) * _silu(cg_ref[...])).astype(cat_ref.dtype)

    return pl.pallas_call(
        body, name=name, grid=(s // t,),
        in_specs=[_rowspec(t, D_MLA), _rowspec(t, D_MLA, SEG_MG[0] // D_MLA), _rowspec(t, D_CONV),
                  _rowspec(t, D_CONV, SEG_CG[0] // D_CONV), _vecspec(D_CONV)],
        out_specs=_rowspec(t, D_MLA + D_CONV), out_shape=_sds((s, D_MLA + D_CONV), MXU_DTYPE),
        compiler_params=_cp(("parallel",)),
    )(o, z, u4m, z, b_pw)


def _gated_residual_bwd(gx, y_ref, gate_ref, dy_ref, dgate_ref):
    dy_ref[...] = (gx * gate_ref[...]).astype(dy_ref.dtype)
    dgate_ref[...] += _colsum(gx * y_ref[...])


def _loss_head(xf, target, y, gate, *, name):
    s, d = xf.shape
    t = min(2 * ROW_T, s)

    def body(x_ref, t_ref, y_ref, gate_ref, gx_ref, loss_ref, dy_ref, dgate_ref):
        @pl.when(pl.program_id(0) == 0)
        def _():
            loss_ref[...] = jnp.zeros(loss_ref.shape, F32)
            dgate_ref[...] = jnp.zeros(dgate_ref.shape, F32)

        err = x_ref[...] - t_ref[...]
        gx = err * (1.0 / d)
        gx_ref[...] = gx
        loss_ref[...] += 0.5 * jnp.sum(_lanesum(err * err) * (1.0 / d), axis=0, keepdims=True)
        _gated_residual_bwd(gx, y_ref, gate_ref, dy_ref, dgate_ref)

    return pl.pallas_call(
        body, name=name, grid=(s // t,),
        in_specs=[_rowspec(t, d), _rowspec(t, d), _rowspec(t, d), _vecspec(d)],
        out_specs=[_rowspec(t, d), pl.BlockSpec((1, 1), lambda i: (0, 0)), _rowspec(t, d), _vecspec(d)],
        out_shape=[_sds((s, d), F32), _sds((1, 1), F32), _sds((s, d), MXU_DTYPE), _sds((1, d), F32)],
        compiler_params=_cp(("arbitrary",)),
    )(xf, target, y, gate)


def _acc_init(refs):
    @pl.when(pl.program_id(0) == 0)
    def _():
        for r in refs:
            r[...] = jnp.zeros(r.shape, r.dtype)


def _gate_bwd(dy, w_out, o, z, u4m, b_pw, *, name):
    s, d = dy.shape
    t = min(2 * ROW_T, s)
    gates = D_MLA + D_CONV
    assert SEG_CG[0] == SEG_MG[0] + D_MLA and SEG_MG[0] % gates == 0

    def body(dy_ref, w_ref, o_ref, mg_ref, u4_ref, cg_ref, b_ref,
             do_ref, delta_ref, du4_ref, gb_ref, dz_ref):
        _acc_init([gb_ref])
        dcat = lax.dot_general(dy_ref[...], w_ref[...], (((1,), (1,)), ((), ())), preferred_element_type=F32)
        dm, ov, mg = dcat[:, :D_MLA], o_ref[...], mg_ref[...]
        do = dm * _silu(mg)
        do_ref[...] = do.astype(do_ref.dtype)
        dz_ref[:, :D_MLA] = (dm * ov * _dsilu(mg)).astype(dz_ref.dtype)
        prod = do * ov
        for h in range(N_HEADS):
            rowsum = jnp.broadcast_to(_lanesum(prod[:, h * V_DIM:(h + 1) * V_DIM]), (t, LANE))
            delta_ref[h, 0] = jnp.transpose(rowsum)[0:1, :]
        dc, cg = dcat[:, D_MLA:], cg_ref[...]
        du4 = dc * _silu(cg)
        du4_ref[...] = du4.astype(du4_ref.dtype)
        dz_ref[:, D_MLA:] = (dc * (u4_ref[...] + b_ref[...]) * _dsilu(cg)).astype(dz_ref.dtype)
        gb_ref[...] += _colsum(du4)

    return pl.pallas_call(
        body, name=name, grid=(s // t,),
        in_specs=[_rowspec(t, d), pl.BlockSpec((gates, d), lambda i: (0, 0)), _rowspec(t, D_MLA),
                  _rowspec(t, D_MLA, SEG_MG[0] // D_MLA), _rowspec(t, D_CONV),
                  _rowspec(t, D_CONV, SEG_CG[0] // D_CONV), _vecspec(D_CONV)],
        out_specs=[_rowspec(t, D_MLA), pl.BlockSpec((N_HEADS, 1, 1, t), lambda i: (0, i, 0, 0)),
                   _rowspec(t, D_CONV), _vecspec(D_CONV), _rowspec(t, gates, SEG_MG[0] // gates)],
        out_shape=[_sds((s, D_MLA), MXU_DTYPE), _sds((N_HEADS, s // t, 1, t), F32),
                   _sds((s, D_CONV), MXU_DTYPE), _sds((1, D_CONV), F32), _sds((s, IN_PAD), MXU_DTYPE)],
        compiler_params=_cp(("arbitrary",)),
    )(dy, w_out, o, z, u4m, z, b_pw)


def _conv_bwd(du3, u1, z, dz, glu_b, dw_w, ln_g, ln_b, *, name):
    s = z.shape[0]
    t = min(CONV_T, s)
    c2 = 2 * D_CONV
    hb = t // HALO
    n_blk = s // t
    last_halo = s // HALO - 1

    def body(d3m_ref, d3h_ref, u1m_ref, u1h_ref, zm_ref, zh_ref, gb_ref, w_ref, g_ref, b_ref, dz_in_ref,
             dci_ref, gg_ref, gbn_ref, gwb_ref, ggb_ref, gw_ref, dext, uext, du0_s, gw_acc):
        i = pl.program_id(0)
        _acc_init([gg_ref, gbn_ref, gwb_ref, ggb_ref, gw_acc])

        def ln_bwd(d3, u1v):
            mu = jnp.mean(u1v, axis=-1, keepdims=True)
            cen = u1v - mu
            rstd = lax.rsqrt(jnp.mean(cen * cen, axis=-1, keepdims=True) + EPS)
            uh = cen * rstd
            d2 = d3 * _dsilu(uh * g_ref[...] + b_ref[...])
            dh = d2 * g_ref[...]
            d1 = rstd * (dh - jnp.mean(dh, axis=-1, keepdims=True) - uh * jnp.mean(dh * uh, axis=-1, keepdims=True))
            return d1, d2, uh

        d1, d2, uh = ln_bwd(d3m_ref[...], u1m_ref[...])
        gg_ref[...] += _colsum(d2 * uh)
        gbn_ref[...] += _colsum(d2)
        gwb_ref[...] += _colsum(d1)
        dext[0, 0:t, :] = d1
        d1h, _, _ = ln_bwd(d3h_ref[...], u1h_ref[...])
        dext[0, t:, :] = jnp.where(i < n_blk - 1, d1h, 0.0)
        _shifted_copies(dext)

        def glu_parts(zv):
            ci = zv + gb_ref[...]
            return ci[:, :D_CONV], jax.nn.sigmoid(ci[:, D_CONV:])

        val, sg = glu_parts(zm_ref[...])
        uext[0, HALO:, :] = val * sg
        valh, sgh = glu_parts(zh_ref[...])
        uext[0, 0:HALO, :] = jnp.where(i > 0, valh * sgh, 0.0)
        _shifted_copies(uext)

        for rc in range(0, t, CONV_RC):
            for lc in range(0, D_CONV, CONV_LC):
                acc = None
                for off, win in _windows(dext, [rc + k for k in range(CONV_K)], CONV_RC, lc, CONV_LC):
                    k = (CONV_K - 1) - (off - rc)
                    term = w_ref[k:k + 1, lc:lc + CONV_LC] * win
                    acc = term if acc is None else acc + term
                du0_s[rc:rc + CONV_RC, lc:lc + CONV_LC] = acc
                dchunk = dext[0, rc:rc + CONV_RC, lc:lc + CONV_LC]
                first = rc + HALO - (CONV_K - 1)
                for off, win in _windows(uext, [first + k for k in range(CONV_K)], CONV_RC, lc, CONV_LC):
                    k = off - first
                    pr = dchunk * win
                    part = pr[0:8]
                    for r8 in range(8, CONV_RC, 8):
                        part = part + pr[r8:r8 + 8]
                    gw_acc[k, :, lc:lc + CONV_LC] += part

        du0 = du0_s[...]
        dval = du0 * sg
        dgt = du0 * val * sg * (1.0 - sg)
        dci_ref[:, :D_CONV] = dval.astype(dci_ref.dtype)
        dci_ref[:, D_CONV:] = dgt.astype(dci_ref.dtype)
        ggb_ref[:, :D_CONV] += _colsum(dval)
        ggb_ref[:, D_CONV:] += _colsum(dgt)

        @pl.when(i == n_blk - 1)
        def _():
            gw_ref[...] = jnp.sum(gw_acc[...], axis=1)

    halo_next = lambda w: pl.BlockSpec((HALO, w), lambda i: (jnp.minimum((i + 1) * hb, last_halo), 0))
    return pl.pallas_call(
        body, name=name, grid=(n_blk,),
        in_specs=[_rowspec(t, D_CONV), halo_next(D_CONV), _rowspec(t, D_CONV), halo_next(D_CONV),
                  _rowspec(t, c2), pl.BlockSpec((HALO, c2), lambda i: (jnp.maximum(i * hb - 1, 0), 0)),
                  _vecspec(c2), pl.BlockSpec((HALO, D_CONV), lambda i: (0, 0)), _vecspec(D_CONV), _vecspec(D_CONV),
                  _ANY],
        out_specs=[_rowspec(t, c2, SEG_CI[0] // c2), _vecspec(D_CONV), _vecspec(D_CONV), _vecspec(D_CONV),
                   _vecspec(c2), pl.BlockSpec((HALO, D_CONV), lambda i: (0, 0))],
        out_shape=[_sds(dz.shape, dz.dtype), _sds((1, D_CONV), F32), _sds((1, D_CONV), F32), _sds((1, D_CONV), F32),
                   _sds((1, c2), F32), _sds((HALO, D_CONV), F32)],
        scratch_shapes=[pltpu.VMEM((8, t + HALO, D_CONV), F32), pltpu.VMEM((8, t + HALO, D_CONV), F32),
                        pltpu.VMEM((t, D_CONV), F32), pltpu.VMEM((HALO, 8, D_CONV), F32)],
        input_output_aliases={10: 0},
        compiler_params=_cp(("arbitrary",)),
    )(du3, du3, u1, u1, z, z, glu_b, dw_w, ln_g, ln_b, dz)


def _flash_bwd(qf, kf, va, do, lse_t, delta_t, *, name):
    nh, s, dk = qf.shape
    dv = va.shape[-1] // 2
    t = min(ATT_T, s)
    n = s // t
    nt = (((1,), (1,)), ((), ()))
    tn = (((0,), (0,)), ((), ()))

    def body(q_ref, do_ref, lse_ref, dl_ref, k_ref, v_ref, dq_ref, dk_ref, dv_ref,
             dk_s, dv_s, st_buf, dpt_buf):
        n_un = pl.program_id(1)
        j = n - 1 - n_un
        nxt = jnp.maximum(j - 1, 0)

        @pl.when(n_un == 0)
        def _():
            dq_ref[...] = jnp.zeros(dq_ref.shape, F32)

        dk_s[...] = jnp.zeros(dk_s.shape, F32)
        dv_s[...] = jnp.zeros(dv_s.shape, F32)

        def rows_at(blk):
            return pl.ds(pl.multiple_of(blk * t, t), t)

        def rows_of(b):
            return rows_at(n - 1 - b)

        k = k_ref[0, rows_at(j), :]

        def produce(kj, b, slot):
            rows = rows_of(b)
            st_buf[slot] = lax.dot_general(k_ref[0, rows_at(kj), :], q_ref[0, rows, :], nt,
                                           preferred_element_type=F32)
            dpt_buf[slot] = lax.dot_general(v_ref[0, rows_at(kj), 0:dv], do_ref[rows, :], nt,
                                            preferred_element_type=F32)

        def consume(b, slot, masked):
            i = n - 1 - b
            rows = rows_of(b)
            q, dov = q_ref[0, rows, :], do_ref[rows, :]
            pt = jnp.exp2(st_buf[slot] - lse_ref[0, i])
            if masked:
                key = lax.broadcasted_iota(jnp.int32, (t, t), 0)
                qry = lax.broadcasted_iota(jnp.int32, (t, t), 1)
                pt = jnp.where(key <= qry, pt, 0.0)
            dv_s[...] += jnp.dot(pt.astype(MXU_DTYPE), dov, preferred_element_type=F32)
            dst = (pt * (dpt_buf[slot] - dl_ref[0, i])).astype(MXU_DTYPE)
            dk_s[...] += jnp.dot(dst, q, preferred_element_type=F32)
            dq_ref[0, rows, :] += lax.dot_general(dst, k, tn, preferred_element_type=F32)

        @pl.when(n_un == 0)
        def _():
            produce(j, 0, 2)
            consume(0, 2, True)
            produce(nxt, 0, 2)

        @pl.when(n_un > 0)
        def _():
            produce(j, 1, 1)
            consume(0, 2, False)

            def pair(a, carry):
                produce(j, 2 * a + 2, 0)
                consume(2 * a + 1, 1, False)
                produce(j, 2 * a + 3, 1)
                consume(2 * a + 2, 0, False)
                return carry

            lax.fori_loop(0, (n_un - 1) // 2, pair, 0)

            @pl.when(n_un % 2 == 1)
            def _():
                produce(nxt, 0, 2)
                consume(n_un, 1, True)

            @pl.when(n_un % 2 == 0)
            def _():
                produce(j, n_un, 0)
                consume(n_un - 1, 1, False)
                produce(nxt, 0, 2)
                consume(n_un, 0, True)

        dk_ref[0] = dk_s[...]
        dv_ref[0] = dv_s[...]

    head = lambda h, j: (h, 0, 0)
    rowv = pl.BlockSpec((1, n, 1, t), lambda h, j: (h, 0, 0, 0))
    return pl.pallas_call(
        body, name=name, grid=(nh, n),
        in_specs=[pl.BlockSpec((1, s, dk), head),
                  pl.BlockSpec((s, dv), lambda h, j: (0, h)),
                  rowv, rowv,
                  pl.BlockSpec((1, s, dk), head),
                  pl.BlockSpec((1, s, 2 * dv), head)],
        out_specs=[pl.BlockSpec((1, s, dk), head),
                   pl.BlockSpec((1, t, dk), lambda h, g: (h, n - 1 - g, 0)),
                   pl.BlockSpec((1, t, dv), lambda h, g: (h, n - 1 - g, 0))],
        out_shape=[_sds((nh, s, dk), F32), _sds((nh, s, dk), F32), _sds((nh, s, dv), F32)],
        scratch_shapes=[pltpu.VMEM((t, dk), F32), pltpu.VMEM((t, dv), F32),
                        pltpu.VMEM((3, t, t), F32), pltpu.VMEM((3, t, t), F32)],
        compiler_params=_cp(("arbitrary", "arbitrary")),
    )(qf, do, lse_t, delta_t, kf, va)


def _mla_bwd(dqf, dkf, dvf, q_raw, kv, z, dz, qn, kn, w_q_up, w_kv_up, g_ql, g_kvl, c_t, s1_t, s2_t,
             gqn, gqr, gkn, gkr, *, name):
    s = q_raw.shape[0]
    t = min(ROW_T, s)
    scale = 1.0 / math.sqrt(QK_DIM)
    o_ql, o_kvl, o_kr = (seg[0] - SEG_LAT[0] for seg in (SEG_QL, SEG_KVL, SEG_KR))
    tn = (((0,), (0,)), ((), ()))
    nt = (((1,), (1,)), ((), ()))

    def body(dq_ref, dk_ref, dv_ref, q_ref, kv_ref, kr_ref, ql_ref, kvl_ref, qn_ref, kn_ref, wq_ref, wkv_ref,
             gq_ref, gk_ref, c_ref, s1_ref, s2_ref, gqn_ref, gqr_ref, gkn_ref, gkr_ref, dz_in_ref,
             dz_ref, gwq_ref, gwkv_ref, ggq_ref, ggk_ref, gql_ref, gkvl_ref, dqr_ref, dkv_ref):
        _acc_init([gwq_ref, gwkv_ref, ggq_ref, ggk_ref, gql_ref, gkvl_ref])
        c_v, s1_v, s2_v = c_ref[...], s1_ref[...], s2_ref[...]
        kr = kr_ref[...]
        kr_ss = _lanesum(kr * kr)
        dkr = jnp.zeros(kr.shape, F32)
        ggq_n = ggq_r = ggk_n = ggk_r = jnp.zeros((1, LANE), F32)

        def norm_bwd(n, r, rs, dyn, dyr, gn, gr):
            nh_, rh_ = n * rs, r * rs
            dnh, drh = dyn * gn, dyr * gr
            dot = (_lanesum(dnh * nh_) + _lanesum(drh * rh_)) * (1.0 / QK_DIM)
            return rs * (dnh - nh_ * dot), rs * (drh - rh_ * dot), _colsum(dyn * nh_), _colsum(dyr * rh_)

        for h in range(N_HEADS):
            n = q_ref[:, h * LANE:(h + 1) * LANE]
            r = q_ref[:, N_HEADS * LANE + h * LANE:N_HEADS * LANE + (h + 1) * LANE]
            rs = lax.rsqrt((_lanesum(n * n) + _lanesum(r * r)) * (1.0 / QK_DIM) + EPS)
            dyn = dq_ref[h, :, 0:LANE] * scale
            dyr = _rope_bwd(dq_ref[h, :, LANE:HEAD_PAD] * scale, c_v, s1_v, s2_v)
            dn, dr, g_n, g_r = norm_bwd(n, r, rs, dyn, dyr, gqn_ref[...], gqr_ref[...])
            dqr_ref[:, h * LANE:(h + 1) * LANE] = dn.astype(dqr_ref.dtype)
            dqr_ref[:, N_HEADS * LANE + h * LANE:N_HEADS * LANE + (h + 1) * LANE] = dr.astype(dqr_ref.dtype)
            ggq_n, ggq_r = ggq_n + g_n, ggq_r + g_r

            n = kv_ref[:, h * 2 * LANE:h * 2 * LANE + LANE]
            rs = lax.rsqrt((_lanesum(n * n) + kr_ss) * (1.0 / QK_DIM) + EPS)
            dyn = dk_ref[h, :, 0:LANE] * (1.0 / LOG2E)
            dyr = _rope_bwd(dk_ref[h, :, LANE:HEAD_PAD] * (1.0 / LOG2E), c_v, s1_v, s2_v)
            dn, dr, g_n, g_r = norm_bwd(n, kr, rs, dyn, dyr, gkn_ref[...], gkr_ref[...])
            dkv_ref[:, h * 2 * LANE:h * 2 * LANE + LANE] = dn.astype(dkv_ref.dtype)
            dkv_ref[:, h * 2 * LANE + LANE:(h + 1) * 2 * LANE] = dv_ref[h].astype(dkv_ref.dtype)
            dkr = dkr + dr
            ggk_n, ggk_r = ggk_n + g_n, ggk_r + g_r

        ggq_ref[:, 0:LANE] += ggq_n
        ggq_ref[:, LANE:] += ggq_r
        ggk_ref[:, 0:LANE] += ggk_n
        ggk_ref[:, LANE:] += ggk_r

        for d_ref, x_ref, w_ref, gw_ref, src, g_ref, off, gg_ref in (
                (dqr_ref, qn_ref, wq_ref, gwq_ref, ql_ref, gq_ref, o_ql, gql_ref),
                (dkv_ref, kn_ref, wkv_ref, gwkv_ref, kvl_ref, gk_ref, o_kvl, gkvl_ref)):
            dup = d_ref[...]
            gw_ref[...] += lax.dot_general(x_ref[...], dup, tn, preferred_element_type=F32)
            dy = lax.dot_general(dup, w_ref[...], nt, preferred_element_type=F32)
            v = src[...]
            r = lax.rsqrt(jnp.mean(v * v, axis=-1, keepdims=True) + EPS)
            vh = v * r
            dvh = dy * g_ref[...]
            dz_ref[:, off:off + v.shape[1]] = (
                r * (dvh - vh * jnp.mean(dvh * vh, axis=-1, keepdims=True))).astype(dz_ref.dtype)
            gg_ref[...] += _colsum(dy * vh)
        dz_ref[:, o_kr:o_kr + LANE] = dkr.astype(dz_ref.dtype)
        dz_ref[:, o_kr + LANE:] = jnp.zeros((t, SEG_LAT[1] - o_kr - LANE), dz_ref.dtype)

    hspec = lambda w: pl.BlockSpec((N_HEADS, t, w), lambda i: (0, i, 0))
    whole = lambda a: pl.BlockSpec(a.shape, lambda i: (0, 0))
    wide = 2 * N_HEADS * LANE
    return pl.pallas_call(
        body, name=name, grid=(s // t,),
        in_specs=[hspec(HEAD_PAD), hspec(HEAD_PAD), hspec(V_DIM), _rowspec(t, wide), _rowspec(t, wide),
                  _rowspec(t, LANE, SEG_KR[0] // LANE), _rowspec(t, Q_LORA, SEG_QL[0] // Q_LORA),
                  _rowspec(t, KV_LORA, SEG_KVL[0] // KV_LORA), _rowspec(t, Q_LORA), _rowspec(t, KV_LORA),
                  whole(w_q_up), whole(w_kv_up), _vecspec(Q_LORA), _vecspec(KV_LORA),
                  _rowspec(t, LANE), _rowspec(t, LANE), _rowspec(t, LANE),
                  _vecspec(LANE), _vecspec(LANE), _vecspec(LANE), _vecspec(LANE), _ANY],
        out_specs=[_rowspec(t, SEG_LAT[1], SEG_LAT[0] // SEG_LAT[1]), whole(w_q_up), whole(w_kv_up),
                   _vecspec(2 * LANE), _vecspec(2 * LANE), _vecspec(Q_LORA), _vecspec(KV_LORA)],
        out_shape=[_sds(dz.shape, dz.dtype), _sds(w_q_up.shape, F32), _sds(w_kv_up.shape, F32),
                   _sds((1, 2 * LANE), F32), _sds((1, 2 * LANE), F32), _sds((1, Q_LORA), F32),
                   _sds((1, KV_LORA), F32)],
        scratch_shapes=[pltpu.VMEM((t, wide), MXU_DTYPE), pltpu.VMEM((t, wide), MXU_DTYPE)],
        input_output_aliases={21: 0},
        compiler_params=_cp(("arbitrary",)),
    )(dqf, dkf, dvf, q_raw, kv, z, z, z, qn, kn, w_q_up, w_kv_up, g_ql, g_kvl, c_t, s1_t, s2_t,
      gqn, gqr, gkn, gkr, dz)


def _prenorm_bwd(dh, x, gxo, g, sc1p, below=None, *, name):
    s, d = x.shape
    t = min(2 * ROW_T if below is None else ROW_T, s)
    nb = 0 if below is None else 2

    def body(dh_ref, x_ref, gx_ref, g_ref, sc_ref, *rest):
        dx_ref, dsh_ref, dsc_ref, gg_ref = rest[nb:nb + 4]
        _acc_init([dsh_ref, dsc_ref, gg_ref])
        xv, dhv = x_ref[...], dh_ref[...]
        r = lax.rsqrt(jnp.mean(xv * xv, axis=-1, keepdims=True) + EPS)
        xn = xv * r
        dsh_ref[...] += _colsum(dhv)
        dsc_ref[...] += _colsum(dhv * (xn * g_ref[...]))
        dm = dhv * sc_ref[...]
        gg_ref[...] += _colsum(dm * xn)
        dxn = dm * g_ref[...]
        dx = gx_ref[...] + r * (dxn - xn * jnp.mean(dxn * xn, axis=-1, keepdims=True))
        dx_ref[...] = dx
        if below is not None:
            _acc_init([rest[nb + 5]])
            _gated_residual_bwd(dx, rest[0], rest[1], rest[nb + 4], rest[nb + 5])

    vec_out = [_vecspec(d), _vecspec(d), _vecspec(d)]
    vec_shape = [_sds((1, d), F32)] * 3
    return pl.pallas_call(
        body, name=name, grid=(s // t,),
        in_specs=[_rowspec(t, d), _rowspec(t, d), _rowspec(t, d), _vecspec(d), _vecspec(d)]
        + ([_rowspec(t, d), _vecspec(d)] if below is not None else []),
        out_specs=[_rowspec(t, d)] + vec_out + ([_rowspec(t, d), _vecspec(d)] if below is not None else []),
        out_shape=[_sds((s, d), F32)] + vec_shape
        + ([_sds((s, d), MXU_DTYPE), _sds((1, d), F32)] if below is not None else []),
        compiler_params=_cp(("arbitrary",)),
    )(dh, x, gxo, g, sc1p, *(below if below is not None else ()))


def _ada_fwd(c_all, ada_w, ada_b_cols, *, name):
    nl, d, cols = ada_w.shape

    def body(c_ref, w_ref, b_ref, o_ref):
        ca = _silu(c_ref[...]).astype(MXU_DTYPE)
        o_ref[0] = jnp.dot(ca, w_ref[0].astype(MXU_DTYPE), preferred_element_type=F32) + b_ref[0]

    return pl.pallas_call(
        body, name=name, grid=(nl,),
        in_specs=[pl.BlockSpec((N_DEV, d), lambda l: (0, 0)), pl.BlockSpec((1, d, cols), lambda l: (l, 0, 0)),
                  pl.BlockSpec((1, 1, cols), lambda l: (l, 0, 0))],
        out_specs=pl.BlockSpec((1, N_DEV, cols), lambda l: (l, 0, 0)),
        out_shape=_sds((nl, N_DEV, cols), F32),
        compiler_params=_cp(("parallel",)),
    )(c_all, ada_w, ada_b_cols)


def _ada_bwd(c_all_t, dmod_cols, *, name):
    nl, _, cols = dmod_cols.shape
    d = c_all_t.shape[0]

    def body(c_ref, dm_ref, o_ref):
        ca = _silu(c_ref[...]).astype(MXU_DTYPE)
        o_ref[0] = jnp.dot(ca, dm_ref[0].astype(MXU_DTYPE), preferred_element_type=F32)

    return pl.pallas_call(
        body, name=name, grid=(nl,),
        in_specs=[pl.BlockSpec((d, N_DEV), lambda l: (0, 0)), pl.BlockSpec((1, N_DEV, cols), lambda l: (l, 0, 0))],
        out_specs=pl.BlockSpec((1, d, cols), lambda l: (l, 0, 0)),
        out_shape=_sds((nl, d, cols), F32),
        compiler_params=_cp(("parallel",)),
    )(c_all_t, dmod_cols)


def _adamw_math(g, w, m, v):
    mn = ADAM_B1 * m + (1.0 - ADAM_B1) * g
    vn = ADAM_B2 * v + (1.0 - ADAM_B2) * (g * g)
    m_hat = mn / (1.0 - ADAM_B1 ** ADAM_STEP)
    v_hat = vn / (1.0 - ADAM_B2 ** ADAM_STEP)
    return -ADAM_LR * (m_hat / (jnp.sqrt(v_hat) + ADAM_EPS) + ADAM_WD * w), mn, vn


def _adamw_small(items, *, name):
    n = len(items)
    shapes = [it[1].shape for it in items]
    flat = lambda a, lead: a.reshape(lead + (-1, a.shape[-1]))
    operands = []
    for gp, w, m, v in items:
        operands += [flat(gp, (gp.shape[0],)), flat(w, ()), flat(m, ()), flat(v, ())]
    nparts = [it[0].shape[0] for it in items]

    def body(*refs):
        ins, outs = refs[:4 * n], refs[4 * n:]
        for i in range(n):
            g_ref, w_ref, m_ref, v_ref = ins[4 * i:4 * i + 4]
            g = g_ref[0].astype(F32)
            for p in range(1, nparts[i]):
                g = g + g_ref[p].astype(F32)
            outs[4 * i][...] = g
            outs[4 * i + 1][...], outs[4 * i + 2][...], outs[4 * i + 3][...] = _adamw_math(
                g, w_ref[...], m_ref[...], v_ref[...])

    out_shape = []
    for it in items:
        out_shape += [_sds(flat(it[1], ()).shape, F32)] * 4
    outs = pl.pallas_call(body, name=name, out_shape=out_shape, compiler_params=_cp())(*operands)
    return [tuple(o.reshape(shp) for o in outs[4 * i:4 * i + 4]) for i, shp in enumerate(shapes)]


def _adamw_layer(gparts, w, m, v, layer, prev, *, name):
    shape = w.shape
    nl, cols = shape[0], shape[-1]
    rows = w.size // cols // nl
    npart = gparts.shape[0]
    g3 = gparts.reshape(npart, rows, cols)
    w3, m3, v3 = (a.reshape(nl, rows, cols) for a in (w, m, v))
    fits = [t for t in range(min(rows, 256) // 8 * 8, 7, -8)
            if rows % t == 0 and npart * t * cols * g3.dtype.itemsize <= 2 * 1024 * 1024]
    t = fits[0] if fits else rows
    n_prev = 0 if prev is None else 4

    def body(g_ref, w_ref, m_ref, v_ref, *rest):
        go_ref, d_ref, mo_ref, vo_ref = rest[n_prev:]
        g = g_ref[0].astype(F32)
        for p in range(1, npart):
            g = g + g_ref[p].astype(F32)
        go_ref[0] = g
        d_ref[0], mo_ref[0], vo_ref[0] = _adamw_math(g, w_ref[0], m_ref[0], v_ref[0])

    spec = pl.BlockSpec((1, t, cols), lambda i: (layer, i, 0))
    outs = pl.pallas_call(
        body, name=name, grid=(rows // t,),
        in_specs=[pl.BlockSpec((npart, t, cols), lambda i: (0, i, 0)), spec, spec, spec] + [_ANY] * n_prev,
        out_specs=[spec] * 4, out_shape=[_sds((nl, rows, cols), F32)] * 4,
        input_output_aliases={4 + k: k for k in range(n_prev)},
        compiler_params=_cp(("parallel",)),
    )(g3, w3, m3, v3, *([] if prev is None else [a.reshape(nl, rows, cols) for a in prev]))
    return tuple(o.reshape(shape) for o in outs)


def _adamw(gparts, w, m, v, *, name):
    shape = w.shape
    cols = shape[-1]
    per_layer = isinstance(gparts, (list, tuple))
    nl = shape[0] if per_layer else 1
    rows = w.size // cols // nl
    glist = list(gparts) if per_layer else [gparts]
    npart = glist[0].shape[0]
    glist = [g.reshape(npart, rows, cols) for g in glist]
    w3, m3, v3 = (a.reshape(nl, rows, cols) for a in (w, m, v))
    budget = 2 * 1024 * 1024
    fits = [t for t in range(min(rows, 256) // 8 * 8, 7, -8)
            if rows % t == 0 and npart * t * cols * glist[0].dtype.itemsize <= budget]
    t = fits[0] if fits else rows
    nb = rows // t

    def body(*refs):
        g_refs = refs[:nl]
        w_ref, m_ref, v_ref, go_ref, d_ref, mo_ref, vo_ref, g_s = refs[nl:]
        layer = pl.program_id(0)
        for l in range(nl):
            @pl.when(layer == l)
            def _(l=l):
                g = g_refs[l][0].astype(F32)
                for p in range(1, npart):
                    g = g + g_refs[l][p].astype(F32)
                g_s[...] = g

        g = g_s[...]
        go_ref[0] = g
        d_ref[0], mo_ref[0], vo_ref[0] = _adamw_math(g, w_ref[0], m_ref[0], v_ref[0])

    def g_map(l):
        return lambda layer, i: (0, jnp.where(layer == l, i, jnp.where(layer < l, 0, nb - 1)), 0)

    spec = pl.BlockSpec((1, t, cols), lambda layer, i: (layer, i, 0))
    outs = pl.pallas_call(
        body, name=name, grid=(nl, nb),
        in_specs=[pl.BlockSpec((npart, t, cols), g_map(l)) for l in range(nl)] + [spec, spec, spec],
        out_specs=[spec] * 4, out_shape=[_sds((nl, rows, cols), F32)] * 4,
        scratch_shapes=[pltpu.VMEM((t, cols), F32)],
        compiler_params=_cp(("arbitrary", "arbitrary")),
    )(*glist, w3, m3, v3)
    return tuple(o.reshape(shape) for o in outs)


_ANY = pl.BlockSpec(memory_space=pl.ANY)


def _all_gather(blocks, *, name):
    na = len(blocks)

    def body(*refs):
        x_refs, out_refs = refs[:na], refs[na:2 * na]
        send_sems, recv_sems, local_sems = refs[2 * na:]
        x, y, c = lax.axis_index("x"), lax.axis_index("y"), lax.axis_index("c")
        me, sibling = (x, y, c), (x, y, 1 - c)
        chips = [(1 - x, y), (x, 1 - y), (1 - x, 1 - y)]

        def slot(a, px, py, pc):
            return out_refs[a].at[4 * px + 2 * py + pc]

        def copy(a, k, blk, to, src=None):
            return pltpu.make_async_remote_copy(
                src_ref=slot(a, *blk) if src is None else src, dst_ref=slot(a, *blk),
                send_sem=send_sems.at[7 * a + k], recv_sem=recv_sems.at[7 * a + k],
                device_id=to, device_id_type=MESH_ID)

        mine = [pltpu.make_async_copy(x_refs[a], slot(a, *me), local_sems.at[a]) for a in range(na)]
        for cp in mine:
            cp.start()
        first = []
        for a in range(na):
            first.append(copy(a, 0, me, sibling, src=x_refs[a]))
            first += [copy(a, 1 + j, me, (*chip, c), src=x_refs[a]) for j, chip in enumerate(chips)]
        for cp in first:
            cp.start()
        passed = []
        for a in range(na):
            for j, chip in enumerate(chips):
                copy(a, 1 + j, (*chip, c), me).wait_recv()
                fwd = copy(a, 4 + j, (*chip, c), sibling)
                fwd.start()
                passed.append(fwd)
        for a in range(na):
            copy(a, 0, sibling, me).wait_recv()
            for j, chip in enumerate(chips):
                copy(a, 4 + j, (*chip, 1 - c), me).wait_recv()
        for cp in first + passed:
            cp.wait_send()
        for cp in mine:
            cp.wait()

    outs = pl.pallas_call(
        body, name=name, in_specs=[_ANY] * na, out_specs=[_ANY] * na,
        out_shape=[_sds((N_DEV,) + b.shape, b.dtype) for b in blocks],
        scratch_shapes=[pltpu.SemaphoreType.DMA((7 * na,)), pltpu.SemaphoreType.DMA((7 * na,)),
                        pltpu.SemaphoreType.DMA((na,))],
    )(*blocks)
    return list(outs)


_HBM = pl.BlockSpec(memory_space=pltpu.HBM)
_SEM = pl.BlockSpec(memory_space=pltpu.SEMAPHORE)
_EFFECT = pltpu.SideEffectType.DATAFLOW_SIDE_EFFECTING


def _peers(x, y, c):
    out = []
    for k in range(1, N_DEV):
        out.append((1 - x if k & 4 else x, 1 - y if k & 2 else y, 1 - c if k & 1 else c))
    return out


def _own_slots(srcs, scatter, *, name, after=None):
    na = len(srcs)
    n_extra = 0 if after is None else 1
    me = (4 * lax.axis_index("x") + 2 * lax.axis_index("y") + lax.axis_index("c")).astype(jnp.int32).reshape(1)

    def body(me_ref, *refs):
        in_refs, out_refs = refs[:na], refs[na + n_extra:]
        for a in range(na):
            out_refs[a][0] = in_refs[a][0] if scatter else in_refs[a][...]

    def slot_spec(shard):
        zeros = (0,) * len(shard)
        return pl.BlockSpec((1,) + tuple(shard), lambda i, me_ref: (me_ref[0],) + zeros)

    def whole_spec(shape):
        zeros = (0,) * len(shape)
        return pl.BlockSpec(tuple(shape), lambda i, me_ref: zeros)

    shards = [s.shape[1:] if scatter else s.shape for s in srcs]
    in_specs = [slot_spec(sh) if scatter else whole_spec(sh) for sh in shards] + [_ANY] * n_extra
    outs = pl.pallas_call(
        body, name=name,
        grid_spec=pltpu.PrefetchScalarGridSpec(
            num_scalar_prefetch=1, grid=(1,), in_specs=in_specs, out_specs=[slot_spec(sh) for sh in shards]),
        out_shape=[_sds((N_DEV,) + tuple(sh), s.dtype) for sh, s in zip(shards, srcs)],
        compiler_params=_cp(("arbitrary",)),
    )(me, *srcs, *([] if after is None else [after]))
    return list(outs)


_N_COPIES = dict(scatter=7, gather=7, chips=4, forward=3)


def _exchange_copies(src_refs, land_refs, send_sems, recv_sems, mode):
    x, y, c = lax.axis_index("x"), lax.axis_index("y"), lax.axis_index("c")
    me = 4 * x + 2 * y + c
    nc = _N_COPIES[mode]
    chips = [(1 - x, y), (x, 1 - y), (1 - x, 1 - y)]
    cps = []
    for a in range(len(land_refs)):
        if mode in ("scatter", "gather"):
            plan = [((src_refs[a].at[4 * px + 2 * py + pc] if mode == "scatter" else src_refs[a]),
                     land_refs[a].at[me], (px, py, pc)) for px, py, pc in _peers(x, y, c)]
        elif mode == "chips":
            plan = [(src_refs[a], land_refs[a].at[me], to) for to in [(x, y, 1 - c)] + [(*ch, c) for ch in chips]]
        else:
            plan = [(land_refs[a].at[4 * px + 2 * py + c], land_refs[a].at[4 * px + 2 * py + c], (x, y, 1 - c))
                    for px, py in chips]
        for k, (src, dst, to) in enumerate(plan):
            cps.append(pltpu.make_async_remote_copy(
                src_ref=src, dst_ref=dst, send_sem=send_sems.at[nc * a + k], recv_sem=recv_sems.at[nc * a + k],
                device_id=to, device_id_type=MESH_ID))
    return cps


def _exchange_start(srcs, lands, mode, *, name):
    ns, nz = len(srcs), len(lands)
    nsem = _N_COPIES[mode] * nz

    def body(*refs):
        src_refs, land_refs = refs[:ns], refs[ns:ns + nz]
        send_sems, recv_sems = refs[ns + nz], refs[ns + nz + 1]
        token = refs[-1]
        for cp in _exchange_copies(src_refs, land_refs, send_sems, recv_sems, mode):
            cp.start()
        token[...] = jnp.zeros(token.shape, token.dtype)

    hbm = lambda a: pltpu.HBM(a.shape, a.dtype)
    outs = pl.pallas_call(
        body, name=name,
        out_shape=(pltpu.SemaphoreType.DMA((nsem,)), pltpu.SemaphoreType.DMA((nsem,)),
                   *[hbm(a) for a in srcs], *[hbm(a) for a in lands], _sds((8, LANE), F32)),
        in_specs=[_HBM] * (ns + nz),
        out_specs=(_SEM, _SEM, *[_HBM] * (ns + nz), pl.BlockSpec(memory_space=pltpu.VMEM)),
        input_output_aliases={i: 2 + i for i in range(ns + nz)},
        compiler_params=pltpu.CompilerParams(has_side_effects=_EFFECT),
    )(*[pltpu.with_memory_space_constraint(a, pltpu.HBM) for a in list(srcs) + list(lands)])
    return outs[0], outs[1], list(outs[2:2 + ns]), list(outs[2 + ns:2 + ns + nz]), outs[-1]


def _exchange_wait(send_sems, recv_sems, srcs, lands, after, mode, *, name):
    ns, nz = len(srcs), len(lands)

    def body(*refs):
        src_refs, land_refs = refs[:ns], refs[ns:ns + nz]
        s_sems, r_sems = refs[ns + nz], refs[ns + nz + 1]
        for cp in _exchange_copies(src_refs, land_refs, s_sems, r_sems, mode):
            cp.wait_send()
            cp.wait_recv()

    hbm = lambda a: pltpu.HBM(a.shape, a.dtype)
    outs = pl.pallas_call(
        body, name=name,
        out_shape=(*[hbm(a) for a in srcs], *[hbm(a) for a in lands]),
        in_specs=[_HBM] * (ns + nz) + [_SEM, _SEM, _ANY],
        out_specs=tuple([_HBM] * (ns + nz)),
        input_output_aliases={i: i for i in range(ns + nz)},
        compiler_params=pltpu.CompilerParams(has_side_effects=_EFFECT),
    )(*srcs, *lands, send_sems, recv_sems, after)
    return list(outs[ns:])


_WIN_SEGS = (("ql", 0, Q_LORA, SEG_QL[0]), ("kvl", Q_LORA, KV_LORA, SEG_KVL[0]),
             ("kr", Q_LORA + KV_LORA, ROPE, SEG_KR[0]), ("mg", Q_LORA + KV_LORA + ROPE, D_MLA, SEG_MG[0]),
             ("ci", Q_LORA + KV_LORA + ROPE + D_MLA, 2 * D_CONV, SEG_CI[0]),
             ("cg", Q_LORA + KV_LORA + ROPE + D_MLA + 2 * D_CONV, D_CONV, SEG_CG[0]))
_WIN_SHARD = IN_COLS // N_DEV


def _win_pieces():
    out = []
    for _, o, n, new in _WIN_SEGS:
        for j in range(N_DEV):
            lo, hi = max(o, j * _WIN_SHARD), min(o + n, (j + 1) * _WIN_SHARD)
            if lo < hi:
                out.append((j, lo - j * _WIN_SHARD, new + lo - o, hi - lo))
    return out


WIN_T = 512


def _win_assemble(w_all, *, name):
    d = w_all.shape[2]
    t = min(WIN_T, d)
    pieces = sorted(_win_pieces(), key=lambda p: p[2])
    assert all(lo % 8 == 0 and n % 8 == 0 for _, lo, _, n in pieces)

    def body(w_ref, o_ref):
        rows = [w_ref[j].astype(F32)[lo:lo + n, :] for j, lo, _, n in pieces]
        rows.append(jnp.zeros((IN_PAD - (SEG_KR[0] + ROPE), t), F32))
        o_ref[...] = jnp.concatenate(rows, axis=0).astype(o_ref.dtype)

    return pl.pallas_call(
        body, name=name, grid=(d // t,),
        in_specs=[pl.BlockSpec((N_DEV, _WIN_SHARD, t), lambda i: (0, 0, i))],
        out_specs=pl.BlockSpec((IN_PAD, t), lambda i: (0, i)), out_shape=_sds((IN_PAD, d), w_all.dtype),
        compiler_params=_cp(("parallel",)),
    )(w_all)


def _win_split(grad, *, name):
    d = grad.shape[1]
    t = min(WIN_T, d)
    by_shard = [sorted([p for p in _win_pieces() if p[0] == j], key=lambda p: p[1]) for j in range(N_DEV)]

    def body(g_ref, o_ref):
        for j in range(N_DEV):
            rows = [g_ref[new:new + n, :] for _, _, new, n in by_shard[j]]
            o_ref[j] = jnp.concatenate(rows, axis=0).astype(o_ref.dtype)

    return pl.pallas_call(
        body, name=name, grid=(d // t,),
        in_specs=[pl.BlockSpec((IN_PAD, t), lambda i: (0, i))],
        out_specs=pl.BlockSpec((N_DEV, _WIN_SHARD, t), lambda i: (0, 0, i)),
        out_shape=_sds((N_DEV, _WIN_SHARD, d), WIRE_DTYPE),
        compiler_params=_cp(("parallel",)),
    )(grad)


def _cols_to_shards(a):
    r, n = a.shape
    return a.reshape(r, N_DEV, n // N_DEV).transpose(1, 0, 2)


def _shards_to_cols(a):
    nd, r, w = a.shape
    return a.transpose(1, 0, 2).reshape(r, nd * w)


def _qup_permute(w):
    w3 = w.reshape(w.shape[0], N_HEADS, QK_DIM)
    nope = w3[:, :, :NOPE].reshape(w.shape[0], N_HEADS * NOPE)
    rope = jnp.pad(w3[:, :, NOPE:], ((0, 0), (0, 0), (0, LANE - ROPE))).reshape(w.shape[0], N_HEADS * LANE)
    return jnp.concatenate([nope, rope], axis=1)


def _qup_unpermute(g):
    r = g.shape[0]
    nope = g[:, :N_HEADS * NOPE].reshape(r, N_HEADS, NOPE)
    rope = g[:, N_HEADS * NOPE:].reshape(r, N_HEADS, LANE)[:, :, :ROPE]
    return jnp.concatenate([nope, rope], axis=2).reshape(r, N_HEADS * QK_DIM)


def _norm_tiles(g):
    return g[:NOPE].reshape(1, LANE), jnp.pad(g[NOPE:], (0, LANE - ROPE)).reshape(1, LANE)


def _rope_tiles(positions):
    inv_freq = 1.0 / (ROPE_THETA ** (jnp.arange(0, ROPE, 2, dtype=F32) / ROPE))
    ang = positions.astype(F32)[:, None] * inv_freq
    cos, sin = jnp.cos(ang), jnp.sin(ang)
    zq = jnp.zeros_like(cos)
    c_t = jnp.concatenate([cos, cos, zq, zq], axis=1)
    s1_t = jnp.concatenate([-sin, zq, zq, zq], axis=1)
    s2_t = jnp.concatenate([zq, sin, zq, zq], axis=1)
    return c_t, s1_t, s2_t


_BIG = ("w_in", "w_q_up", "w_kv_up", "w_pw", "w_out")
_COL_SHARDED = ("w_q_up", "w_kv_up")


def _unpack_rows(buf, shapes):
    out, r0 = [], 0
    lead = buf.shape[:-2]
    for shp in shapes:
        n = math.prod(shp) // LANE
        out.append(buf[..., r0:r0 + n, :].reshape(lead + tuple(shp)))
        r0 += n
    return out


_SMALL = (("dmod", 3 * D_MODEL), ("norm_g", D_MODEL), ("q_lat_g", Q_LORA), ("kv_lat_g", KV_LORA),
          ("q_norm_g", 2 * LANE), ("k_norm_g", 2 * LANE), ("glu_b", 2 * D_CONV), ("dw_w", HALO * D_CONV),
          ("dw_b", D_CONV), ("conv_ln_g", D_CONV), ("conv_ln_b", D_CONV), ("b_pw", D_CONV))


def _layer_fwd(x, p, rope, l, late=None):
    n = lambda s: f"{s}_l{l}"
    c_t, s1_t, s2_t = rope
    h = _prenorm(x, p["norm_g"], p["shift"], p["sc1p"], name=n("prenorm"))
    z = _mm(h, p["w_in"], tb=True, name=n("in_proj"), tn=IN_TILE, n_outer=True)
    if late is not None:
        p = {**p, **late(z)}
    qn, kn, q_raw, kv, qf, kf, vf = _mla_pre(z, p["w_q_up"], p["w_kv_up"], p["q_lat_g"], p["kv_lat_g"],
                                             c_t, s1_t, s2_t, *p["qk_tiles"], name=n("mla_pre"))
    o, lse = _flash_fwd(qf, kf, vf, name=n("flash_fwd"))
    u1, u3, u4m = _conv_fwd(z, p["glu_b"], p["dw_w"], p["dw_b"], p["conv_ln_g"], p["conv_ln_b"], p["w_pw"],
                            name=n("conv_fwd"))
    cat = _gate_cat(o, z, u4m, p["b_pw"], name=n("gate_cat"))
    y, x_next = _mm(cat, p["w_out"], name=n("out_proj"), tn=1024, residual=(x, p["gate"]))
    saved = dict(x=x, h=h, z=z, qn=qn, kn=kn, q_raw=q_raw, kv=kv, qf=qf, kf=kf, vf=vf, o=o, lse=lse,
                 u1=u1, u3=u3, u4m=u4m, cat=cat, y=y)
    return x_next, saved, p


def _layer_bwd(gxo, dy, dgate, p, sv, rope, l, below=None, hook_rest=None, hook_w_in=None):
    n = lambda s: f"{s}_l{l}"
    c_t, s1_t, s2_t = rope
    z = sv["z"]
    g_w_out = _mm(sv["cat"], dy, ta=True, name=n("g_w_out"), tm=1024, tn=1024, after=p.get("after_start"))
    do, delta, du4, g_b_pw, dz = _gate_bwd(dy, p["w_out"], sv["o"], z, sv["u4m"], p["b_pw"], name=n("gate_bwd"))
    g_w_pw = _mm(sv["u3"], du4, ta=True, name=n("g_w_pw"), tm=1024, tn=1024, tk=512)
    du3 = _mm(du4, p["w_pw"], tb=True, name=n("d_u3"), tn=1024)
    dz, g_ln_g, g_ln_b, g_dw_b, g_glu_b, g_dw_w = _conv_bwd(
        du3, sv["u1"], z, dz, p["glu_b"], p["dw_w"], p["conv_ln_g"], p["conv_ln_b"], name=n("conv_bwd"))
    dqf, dkf, dvf = _flash_bwd(sv["qf"], sv["kf"], sv["vf"], do, sv["lse"], delta.reshape(sv["lse"].shape),
                               name=n("flash_bwd"))
    dz, g_w_q_up, g_w_kv_up, g_qn, g_kn, g_ql, g_kvl = _mla_bwd(
        dqf, dkf, dvf, sv["q_raw"], sv["kv"], z, dz, sv["qn"], sv["kn"], p["w_q_up"], p["w_kv_up"],
        p["q_lat_g"], p["kv_lat_g"], c_t, s1_t, s2_t, *p["qk_tiles"], name=n("mla_bwd"))
    big = dict(w_q_up=g_w_q_up, w_kv_up=g_w_kv_up, w_pw=g_w_pw, w_out=g_w_out)
    after = None if hook_rest is None else hook_rest(big)
    g_w_in = _mm(dz, sv["h"], ta=True, name=n("g_w_in"), tm=512, tn=1024, after=after)
    big["w_in"] = g_w_in
    after = None if hook_w_in is None else hook_w_in(g_w_in)
    dh = _mm(dz, p["w_in"], name=n("d_h"), tn=1024, after=after)
    dx, dshift, dscale, g_norm, *down = _prenorm_bwd(dh, sv["x"], gxo, p["norm_g"], p["sc1p"], below,
                                                     name=n("prenorm_bwd"))
    small = dict(dmod=jnp.concatenate([dshift, dscale, dgate], axis=1), norm_g=g_norm, q_lat_g=g_ql, kv_lat_g=g_kvl,
                 q_norm_g=g_qn, k_norm_g=g_kn, glu_b=g_glu_b, dw_w=g_dw_w, dw_b=g_dw_b,
                 conv_ln_g=g_ln_g, conv_ln_b=g_ln_b, b_pw=g_b_pw)
    return (dx, *down), big, small


def _layer_params(l, full, mod_l, small):
    d = D_MODEL
    row = lambda a: a.reshape(1, -1)
    shift, scale, gate = mod_l[:, :d], mod_l[:, d:2 * d], mod_l[:, 2 * d:]
    dw_w = jnp.pad(full["dw_w"][l], ((0, HALO - CONV_K), (0, 0)))
    return dict(
        shift=shift, sc1p=1.0 + scale, gate=gate, norm_g=row(small["norm_g"][l]),
        **{k: full[k][l] for k in _BIG if k in full}, dw_w=dw_w,
        q_lat_g=row(small["q_lat_g"][l]), kv_lat_g=row(small["kv_lat_g"][l]),
        qk_tiles=_norm_tiles(small["q_norm_g"][l]) + _norm_tiles(small["k_norm_g"][l]),
        glu_b=row(small["glu_b"][l]), dw_b=row(small["dw_b"][l]), conv_ln_g=row(small["conv_ln_g"][l]),
        conv_ln_b=row(small["conv_ln_b"][l]), b_pw=row(small["b_pw"][l]))


def kernel(x, c, positions, ada_w, ada_b, norm_g, w_in, q_lat_g, w_q_up, kv_lat_g, w_kv_up, q_norm_g, k_norm_g, glu_b, dw_w, dw_b, conv_ln_g, conv_ln_b, w_pw, b_pw, w_out, loss_target, m_ada_w, m_ada_b, m_norm_g, m_w_in, m_q_lat_g, m_w_q_up, m_kv_lat_g, m_w_kv_up, m_q_norm_g, m_k_norm_g, m_glu_b, m_dw_w, m_dw_b, m_conv_ln_g, m_conv_ln_b, m_w_pw, m_b_pw, m_w_out, v_ada_w, v_ada_b, v_norm_g, v_w_in, v_q_lat_g, v_w_q_up, v_kv_lat_g, v_w_kv_up, v_q_norm_g, v_k_norm_g, v_glu_b, v_dw_w, v_dw_b, v_conv_ln_g, v_conv_ln_b, v_w_pw, v_b_pw, v_w_out):
    names = ("ada_w", "ada_b", "norm_g", "w_in", "q_lat_g", "w_q_up", "kv_lat_g", "w_kv_up", "q_norm_g",
             "k_norm_g", "glu_b", "dw_w", "dw_b", "conv_ln_g", "conv_ln_b", "w_pw", "b_pw", "w_out")
    w_loc = dict(zip(names, (ada_w, ada_b, norm_g, w_in, q_lat_g, w_q_up, kv_lat_g, w_kv_up, q_norm_g, k_norm_g,
                             glu_b, dw_w, dw_b, conv_ln_g, conv_ln_b, w_pw, b_pw, w_out)))
    m_loc = dict(zip(names, (m_ada_w, m_ada_b, m_norm_g, m_w_in, m_q_lat_g, m_w_q_up, m_kv_lat_g, m_w_kv_up,
                             m_q_norm_g, m_k_norm_g, m_glu_b, m_dw_w, m_dw_b, m_conv_ln_g, m_conv_ln_b, m_w_pw,
                             m_b_pw, m_w_out)))
    v_loc = dict(zip(names, (v_ada_w, v_ada_b, v_norm_g, v_w_in, v_q_lat_g, v_w_q_up, v_kv_lat_g, v_w_kv_up,
                             v_q_norm_g, v_k_norm_g, v_glu_b, v_dw_w, v_dw_b, v_conv_ln_g, v_conv_ln_b, v_w_pw,
                             v_b_pw, v_w_out)))
    nl, d = N_LAYERS, D_MODEL
    me = 4 * lax.axis_index("x") + 2 * lax.axis_index("y") + lax.axis_index("c")
    x2, tgt = x[0], loss_target[0]
    ada_cols = ada_w.shape[-1]

    tr = lambda a: jnp.swapaxes(a, 1, 2)
    w_loc, m_loc, v_loc = ({**dd, "w_in": tr(dd["w_in"])} for dd in (w_loc, m_loc, v_loc))
    w_in0 = [w_loc["w_in"][0].astype(WIRE_DTYPE)]
    fly_c = _exchange_start(w_in0, _own_slots(w_in0, False, name="own_w_in_l0"), "chips", name="gather_start_w_in_l0")
    held = dict(c=c, positions=positions, ada_b=ada_b, norm_g=norm_g, q_lat_g=q_lat_g, kv_lat_g=kv_lat_g,
                q_norm_g=q_norm_g, k_norm_g=k_norm_g, glu_b=glu_b, dw_w=dw_w, dw_b=dw_b, conv_ln_g=conv_ln_g,
                conv_ln_b=conv_ln_b, b_pw=b_pw, big={k: w_loc[k] for k in _BIG})
    tok_c, held = lax.optimization_barrier((fly_c[4], held))
    c, positions, ada_b, norm_g, q_lat_g, kv_lat_g, q_norm_g, k_norm_g, glu_b, dw_w, dw_b, conv_ln_g, conv_ln_b, b_pw = (
        held[k] for k in ("c", "positions", "ada_b", "norm_g", "q_lat_g", "kv_lat_g", "q_norm_g", "k_norm_g", "glu_b",
                          "dw_w", "dw_b", "conv_ln_g", "conv_ln_b", "b_pw"))
    wire = {k: held["big"][k].astype(WIRE_DTYPE) for k in _BIG}

    dw_pad = jnp.pad(dw_w, ((0, 0), (0, HALO - CONV_K), (0, 0)))
    c_rows = c.reshape(d // LANE, LANE) + tok_c[0:1, :]
    c_all, dw_all = _all_gather([c_rows, dw_pad], name="gather_c")
    c_all = c_all.reshape(N_DEV, d)
    ada_b_cols = lax.dynamic_slice_in_dim(ada_b, me * ada_cols, ada_cols, axis=1).reshape(nl, 1, ada_cols)
    mod_cols = _ada_fwd(c_all, ada_w, ada_b_cols, name="ada_fwd")
    mod_all = _all_gather([mod_cols], name="gather_mod")[0]
    mod_me = lax.dynamic_index_in_dim(mod_all, me, axis=2, keepdims=False)
    mod = mod_me.transpose(1, 0, 2).reshape(nl, 1, N_DEV * ada_cols)

    from_chips = _exchange_wait(*fly_c[:4], mod, "chips", name="gather_wait_w_in_l0")
    fly_f = _exchange_start([], from_chips, "forward", name="forward_start_w_in_l0")
    w_in_all0 = _exchange_wait(*fly_f[:4], fly_f[4], "forward", name="forward_wait_w_in_l0")[0]
    rest0 = [wire[k][0] for k in _BIG[1:]]
    fly_r0, fly_w1 = {}, {}
    fly_r0["x"] = _exchange_start(rest0, _own_slots(rest0, False, name="own_weights_l0_rest", after=w_in_all0),
                                  "gather", name="gather_start_l0_rest")

    def layout_rest(parts):
        return dict(w_q_up=_qup_permute(_shards_to_cols(parts[0])), w_kv_up=_shards_to_cols(parts[1]),
                    w_pw=parts[2].reshape(D_CONV, D_CONV), w_out=parts[3].reshape(D_MLA + D_CONV, d))

    small_in = dict(norm_g=norm_g, q_lat_g=q_lat_g, kv_lat_g=kv_lat_g, q_norm_g=q_norm_g, k_norm_g=k_norm_g,
                    glu_b=glu_b, dw_b=dw_b, conv_ln_g=conv_ln_g, conv_ln_b=conv_ln_b, b_pw=b_pw)
    dw_full = [_shards_to_cols(dw_all[:, l])[:CONV_K] for l in range(nl)]
    rope = _rope_tiles(positions[0])

    def layer_params(l, w_in_all, rest, mod_l):
        full = dict(dw_w=dw_full)
        if w_in_all is not None:
            full["w_in"] = {l: _win_assemble(w_in_all, name=f"w_in_assemble_l{l}")}
        if rest is not None:
            full.update({k: {l: a} for k, a in layout_rest(rest).items()})
        return _layer_params(l, full, mod_l, small_in)

    src1 = [wire[k][1] for k in _BIG]
    fly_w1["x"] = _exchange_start(src1, _own_slots(src1, False, name="own_weights_l1", after=fly_r0["x"][4]), "gather",
                                  name="gather_start_l1")

    def late_l0(z):
        return layout_rest(_exchange_wait(*fly_r0["x"][:4], z, "gather", name="gather_wait_l0_rest"))

    params, saved = [None] * nl, [None] * nl
    p0 = layer_params(0, w_in_all0, None, mod[0] + fly_w1["x"][4][0, 0])
    xs, saved[0], params[0] = _layer_fwd(x2, p0, rope, 0, late=late_l0)
    parts1 = _exchange_wait(*fly_w1["x"][:4], xs, "gather", name="gather_wait_l1")
    params[1] = layer_params(1, parts1[0], parts1[1:], mod[1])
    xs, saved[1], _ = _layer_fwd(xs, params[1], rope, 1)
    gx, loss_part, dy, dgate = _loss_head(xs, tgt, saved[1]["y"], params[1]["gate"], name="loss_head")
    loss = lax.psum(loss_part[0, 0], ("x", "y", "c"))

    def shard_major(k, g):
        if k == "w_q_up":
            g = _qup_unpermute(g)
        if k in _COL_SHARDED:
            return _cols_to_shards(g)
        return g.reshape((N_DEV, g.shape[0] // N_DEV, g.shape[1]))

    def scatter_start(send, tag):
        lands = _own_slots(send, True, name=f"own_grads_{tag}")
        return _exchange_start(send, lands, "scatter", name=f"scatter_start_{tag}")

    def wire_rest(big):
        return [shard_major(k, big[k]).astype(WIRE_DTYPE) for k in _BIG[1:]]

    big_g, small_g, flying = [None] * nl, [None] * nl, {}
    (gx, dy, dgate), big_g[1], small_g[1] = _layer_bwd(gx, dy, dgate, params[1], saved[1], rope, 1,
                                                       below=(saved[0]["y"], params[0]["gate"]))
    flying["l1"] = scatter_start([_win_split(big_g[1]["w_in"], name="w_in_split_l1")] + wire_rest(big_g[1]), "l1")
    p0 = dict(params[0], after_start=flying["l1"][4], b_pw=params[0]["b_pw"] + flying["l1"][4][0, 0])

    def start_rest_l0(big):
        flying["l0_rest"] = scatter_start(wire_rest(big), "l0_rest")
        return flying["l0_rest"][4]

    res, arrived = {}, [None] * nl

    def start_w_in_l0(g_w_in):
        flying["l0_w_in"] = scatter_start([_win_split(g_w_in, name="w_in_split_l0")], "l0_w_in")
        tok = flying["l0_w_in"][4]
        arrived[1] = _exchange_wait(*flying["l1"][:4], tok, "scatter", name="scatter_wait_l1")
        arrived[0] = [None] + _exchange_wait(*flying["l0_rest"][:4], tok, "scatter", name="scatter_wait_l0_rest")
        for i, k in enumerate(_BIG):
            if i > 0:
                res[k] = _adamw([arrived[l][i] for l in range(nl)], w_loc[k], m_loc[k], v_loc[k], name=f"adamw_{k}")
        res["w_in_l1"] = _adamw_layer(arrived[1][0], w_loc["w_in"], m_loc["w_in"], v_loc["w_in"], 1, None,
                                      name="adamw_w_in_l1")
        return res["w_in_l1"][0]

    (gx,), big_g[0], small_g[0] = _layer_bwd(gx, dy, dgate, p0, saved[0], rope, 0, hook_rest=start_rest_l0,
                                             hook_w_in=start_w_in_l0)

    tile = 8 * LANE
    padded = [(k, nn, -(-nn // tile) * tile) for k, nn in _SMALL]
    spk = jnp.concatenate([jnp.pad(small_g[l][k].reshape(-1), (0, np_ - nn)).reshape(-1, LANE)
                           for l in range(nl) for k, nn, np_ in padded], axis=0)
    s_all = _all_gather([spk], name="gather_small_grads")[0]
    s_rows = sum(np_ for _, _, np_ in padded) // LANE
    s_all = s_all.reshape(N_DEV, nl, s_rows, LANE)
    s_parts = {k: a[..., :nn] for (k, nn, _), a in
               zip(padded, _unpack_rows(s_all, [(np_,) for _, _, np_ in padded]))}

    dmod_all = s_parts["dmod"]
    dmod_cols = lax.dynamic_slice_in_dim(dmod_all, me * ada_cols, ada_cols, axis=2).transpose(1, 0, 2)
    g_ada_w = _ada_bwd(c_all.T, dmod_cols, name="ada_bwd")
    gp = {}
    gp["ada_w"] = g_ada_w[None]
    gp["ada_b"] = dmod_all
    for k in ("norm_g", "q_lat_g", "kv_lat_g", "glu_b", "dw_b", "conv_ln_g", "conv_ln_b", "b_pw"):
        gp[k] = s_parts[k]
    for k in ("q_norm_g", "k_norm_g"):
        t = s_parts[k]
        gp[k] = jnp.concatenate([t[..., :NOPE], t[..., LANE:LANE + ROPE]], axis=-1)
    dw_g = s_parts["dw_w"].reshape(N_DEV, nl, HALO, D_CONV)[:, :, :CONV_K]
    gp["dw_w"] = lax.dynamic_slice_in_dim(dw_g, me * LANE, LANE, axis=3)

    res["ada_w"] = _adamw(gp["ada_w"], w_loc["ada_w"], m_loc["ada_w"], v_loc["ada_w"], name="adamw_ada_w")
    small_names = [k for k in names if k not in _BIG and k != "ada_w"]
    res.update(zip(small_names, _adamw_small([(gp[k], w_loc[k], m_loc[k], v_loc[k]) for k in small_names],
                                             name="adamw_small")))
    arrived[0][0] = _exchange_wait(*flying["l0_w_in"][:4], res["ada_w"][1], "scatter", name="scatter_wait_l0_w_in")[0]
    w_in_res = _adamw_layer(arrived[0][0], w_loc["w_in"], m_loc["w_in"], v_loc["w_in"], 0, res.pop("w_in_l1"),
                            name="adamw_w_in_l0")
    res["w_in"] = tuple(tr(a) for a in w_in_res)
    out = [loss, gx[None]]
    for idx in range(4):
        out += [res[k][idx] for k in names]
    return tuple(out)
```

```python
import functools
import math

import jax
import jax.numpy as jnp
from jax import lax
from jax.experimental import pallas as pl
from jax.experimental.pallas import tpu as pltpu

F32 = jnp.float32
MXU_DTYPE = jnp.bfloat16
WIRE_DTYPE = jnp.bfloat16

D_MODEL = 2048
N_LAYERS = 2
N_DEV = 8
N_HEADS = 8
NOPE = 128
ROPE = 64
V_DIM = 128
QK_DIM = NOPE + ROPE
Q_LORA = 512
KV_LORA = 256
D_MLA = N_HEADS * V_DIM
D_CONV = 1024
CONV_K = 31
ROPE_THETA = 10000.0
EPS = 1e-6
LANE = 128
HEAD_PAD = 2 * LANE
HALO = 32

SEG_CI = (0, 2 * D_CONV)
SEG_MG = (2 * D_CONV, D_MLA)
SEG_CG = (2 * D_CONV + D_MLA, D_CONV)
SEG_QL = (2 * D_CONV + D_MLA + D_CONV, Q_LORA)
SEG_KVL = (SEG_QL[0] + Q_LORA, KV_LORA)
SEG_KR = (SEG_KVL[0] + KV_LORA, LANE)
SEG_LAT = (SEG_QL[0], 1024)
IN_PAD = SEG_LAT[0] + SEG_LAT[1]
IN_TILE = IN_PAD // 4
assert SEG_KR[0] + LANE <= IN_PAD and SEG_LAT[0] % SEG_LAT[1] == 0
IN_COLS = Q_LORA + KV_LORA + ROPE + D_MLA + 2 * D_CONV + D_CONV

ADAM_LR = 0.001
ADAM_B1 = 0.9
ADAM_B2 = 0.999
ADAM_EPS = 1e-08
ADAM_WD = 0.01
ADAM_STEP = 10

VMEM_LIMIT = 56 * 1024 * 1024
ATT_T = 512
ROW_T = 256
CONV_T = 256
MESH_ID = pl.DeviceIdType.MESH


def _cp(sem=None):
    kw = dict(vmem_limit_bytes=VMEM_LIMIT)
    if sem is not None:
        kw["dimension_semantics"] = sem
    return pltpu.CompilerParams(**kw)


def _sds(shape, dtype):
    return jax.ShapeDtypeStruct(shape, dtype)


def _silu(x):
    return x * jax.nn.sigmoid(x)


def _dsilu(x):
    s = jax.nn.sigmoid(x)
    return s * (1.0 + x * (1.0 - s))


def _rowspec(t, width, col=0):
    return pl.BlockSpec((t, width), lambda i: (i, col))


def _vecspec(width):
    return pl.BlockSpec((1, width), lambda i: (0, 0))


def _colsum(v):
    return jnp.sum(v, axis=0, keepdims=True)


def _mm(a, b, *, name, ta=False, tb=False, out_dtype=F32, tm=512, tn=512, tk=None, n_outer=False, after=None,
        residual=None):
    if ta:
        kdim, m = a.shape
    else:
        m, kdim = a.shape
    if tb:
        n, k2 = b.shape
    else:
        k2, n = b.shape
    assert kdim == k2, (a.shape, b.shape)
    tm, tn = min(tm, m), min(tn, n)
    tk = kdim if tk is None else min(tk, kdim)
    assert m % tm == 0 and n % tn == 0 and kdim % tk == 0, (m, n, kdim, tm, tn, tk)
    nk = kdim // tk
    dims = (((0 if ta else 1,), (1 if tb else 0,)), ((), ()))

    n_extra = 0 if after is None else 1
    assert residual is None or nk == 1

    def body(a_ref, b_ref, *rest):
        if residual is not None:
            x_ref, gate_ref = rest[:2]
            rest = rest[2:]
        o_ref, scratch = rest[n_extra], rest[n_extra + 1:]
        prod = lax.dot_general(a_ref[...].astype(MXU_DTYPE), b_ref[...].astype(MXU_DTYPE), dims,
                               preferred_element_type=F32)
        if residual is not None:
            o_ref[...] = prod.astype(o_ref.dtype)
            scratch[0][...] = x_ref[...] + gate_ref[...] * prod
        elif nk == 1:
            o_ref[...] = prod.astype(o_ref.dtype)
        else:
            acc = scratch[0]
            k = pl.program_id(2)

            @pl.when(k == 0)
            def _():
                acc[...] = prod

            @pl.when(k > 0)
            def _():
                acc[...] += prod

            @pl.when(k == nk - 1)
            def _():
                o_ref[...] = acc[...].astype(o_ref.dtype)

    if n_outer:
        ij = lambda g0, g1: (g1, g0)
        grid = (n // tn, m // tm, nk)
    else:
        ij = lambda g0, g1: (g0, g1)
        grid = (m // tm, n // tn, nk)

    def a_map(g0, g1, k):
        i, _ = ij(g0, g1)
        return (k, i) if ta else (i, k)

    def b_map(g0, g1, k):
        _, j = ij(g0, g1)
        return (j, k) if tb else (k, j)

    def o_map(g0, g1, k):
        return ij(g0, g1)

    in_specs = [pl.BlockSpec((tk, tm) if ta else (tm, tk), a_map), pl.BlockSpec((tn, tk) if tb else (tk, tn), b_map)]
    operands = [a, b]
    out_specs, out_shape = pl.BlockSpec((tm, tn), o_map), _sds((m, n), out_dtype)
    if residual is not None:
        in_specs += [pl.BlockSpec((tm, tn), o_map), pl.BlockSpec((1, tn), lambda g0, g1, k: (0, ij(g0, g1)[1]))]
        operands += list(residual)
        out_specs, out_shape = [out_specs, pl.BlockSpec((tm, tn), o_map)], [out_shape, _sds((m, n), F32)]
    if after is not None:
        in_specs.append(_ANY)
        operands.append(after)
    return pl.pallas_call(
        body, name=name, grid=grid, in_specs=in_specs, out_specs=out_specs, out_shape=out_shape,
        scratch_shapes=[pltpu.VMEM((tm, tn), F32)] if nk > 1 else [],
        compiler_params=_cp(("parallel", "parallel", "arbitrary")),
    )(*operands)


def _prenorm(x, g, shift, sc1p, *, name):
    s, d = x.shape
    t = min(2 * ROW_T, s)

    def body(x_ref, g_ref, sh_ref, sc_ref, h_ref):
        xv = x_ref[...]
        r = lax.rsqrt(jnp.mean(xv * xv, axis=-1, keepdims=True) + EPS)
        h_ref[...] = ((xv * r) * g_ref[...] * sc_ref[...] + sh_ref[...]).astype(h_ref.dtype)

    return pl.pallas_call(
        body, name=name, grid=(s // t,),
        in_specs=[_rowspec(t, d), _vecspec(d), _vecspec(d), _vecspec(d)],
        out_specs=_rowspec(t, d), out_shape=_sds((s, d), MXU_DTYPE),
        compiler_params=_cp(("parallel",)),
    )(x, g, shift, sc1p)


def _rope_fwd(r, c_t, s1_t, s2_t):
    return r * c_t + pltpu.roll(r, LANE - ROPE // 2, 1) * s1_t + pltpu.roll(r, ROPE // 2, 1) * s2_t


def _rope_bwd(d, c_t, s1_t, s2_t):
    return d * c_t + pltpu.roll(d * s1_t, ROPE // 2, 1) + pltpu.roll(d * s2_t, LANE - ROPE // 2, 1)


def _lanesum(v):
    return jnp.sum(v, axis=-1, keepdims=True)


def _mla_pre(z, w_q_up, w_kv_up, g_ql, g_kvl, c_t, s1_t, s2_t, gqn, gqr, gkn, gkr, *, name):
    s = z.shape[0]
    t = min(2 * ROW_T, s)
    scale = LOG2E / math.sqrt(QK_DIM)
    wide = 2 * N_HEADS * LANE

    def body(ql_ref, kvl_ref, kr_ref, wq_ref, wkv_ref, gq_ref, gk_ref, c_ref, s1_ref, s2_ref,
             gqn_ref, gqr_ref, gkn_ref, gkr_ref, qn_ref, kn_ref, q_ref, kv_ref, qf_ref, kf_ref, vf_ref):
        for src, g_ref, dst, w_ref, up in ((ql_ref, gq_ref, qn_ref, wq_ref, q_ref),
                                           (kvl_ref, gk_ref, kn_ref, wkv_ref, kv_ref)):
            v = src[...]
            r = lax.rsqrt(jnp.mean(v * v, axis=-1, keepdims=True) + EPS)
            dst[...] = ((v * r) * g_ref[...]).astype(dst.dtype)
            up[...] = jnp.dot(dst[...], w_ref[...], preferred_element_type=F32)
        c_v, s1_v, s2_v = c_ref[...], s1_ref[...], s2_ref[...]
        kr = kr_ref[...]
        kr_ss = _lanesum(kr * kr)
        for h in range(N_HEADS):
            n = q_ref[:, h * LANE:(h + 1) * LANE]
            r = q_ref[:, N_HEADS * LANE + h * LANE:N_HEADS * LANE + (h + 1) * LANE]
            rs = lax.rsqrt((_lanesum(n * n) + _lanesum(r * r)) * (1.0 / QK_DIM) + EPS)
            qf_ref[h, :, 0:LANE] = (((n * rs) * gqn_ref[...]) * scale).astype(qf_ref.dtype)
            rr = _rope_fwd((r * rs) * gqr_ref[...], c_v, s1_v, s2_v)
            qf_ref[h, :, LANE:HEAD_PAD] = (rr * scale).astype(qf_ref.dtype)

            n = kv_ref[:, h * 2 * LANE:h * 2 * LANE + LANE]
            rs = lax.rsqrt((_lanesum(n * n) + kr_ss) * (1.0 / QK_DIM) + EPS)
            kf_ref[h, :, 0:LANE] = ((n * rs) * gkn_ref[...]).astype(kf_ref.dtype)
            kf_ref[h, :, LANE:HEAD_PAD] = _rope_fwd((kr * rs) * gkr_ref[...], c_v, s1_v, s2_v).astype(kf_ref.dtype)
            vf_ref[h, :, 0:V_DIM] = kv_ref[:, h * 2 * LANE + LANE:(h + 1) * 2 * LANE].astype(vf_ref.dtype)
            vf_ref[h, :, V_DIM:] = jnp.ones((t, V_DIM), vf_ref.dtype)

    hspec = lambda w: pl.BlockSpec((N_HEADS, t, w), lambda i: (0, i, 0))
    whole = lambda a: pl.BlockSpec(a.shape, lambda i: (0, 0))
    return pl.pallas_call(
        body, name=name, grid=(s // t,),
        in_specs=[_rowspec(t, Q_LORA, SEG_QL[0] // Q_LORA), _rowspec(t, KV_LORA, SEG_KVL[0] // KV_LORA),
                  _rowspec(t, LANE, SEG_KR[0] // LANE), whole(w_q_up), whole(w_kv_up),
                  _vecspec(Q_LORA), _vecspec(KV_LORA),
                  _rowspec(t, LANE), _rowspec(t, LANE), _rowspec(t, LANE),
                  _vecspec(LANE), _vecspec(LANE), _vecspec(LANE), _vecspec(LANE)],
        out_specs=[_rowspec(t, Q_LORA), _rowspec(t, KV_LORA), _rowspec(t, wide), _rowspec(t, wide),
                   hspec(HEAD_PAD), hspec(HEAD_PAD), hspec(2 * V_DIM)],
        out_shape=[_sds((s, Q_LORA), MXU_DTYPE), _sds((s, KV_LORA), MXU_DTYPE), _sds((s, wide), F32),
                   _sds((s, wide), F32), _sds((N_HEADS, s, HEAD_PAD), MXU_DTYPE),
                   _sds((N_HEADS, s, HEAD_PAD), MXU_DTYPE), _sds((N_HEADS, s, 2 * V_DIM), MXU_DTYPE)],
        compiler_params=_cp(("parallel",)),
    )(z, z, z, w_q_up, w_kv_up, g_ql, g_kvl, c_t, s1_t, s2_t, gqn, gqr, gkn, gkr)


def _causal_mask(t):
    row = lax.broadcasted_iota(jnp.int32, (t, t), 0)
    col = lax.broadcasted_iota(jnp.int32, (t, t), 1)
    return col <= row


NEG = -1e30
LOG2E = math.log2(math.e)


def _flash_fwd(qf, kf, va, *, name):
    nh, s, dk = qf.shape
    dv = va.shape[-1] // 2
    t = min(ATT_T, s)
    n = s // t
    assert dv == LANE and t % LANE == 0

    def body(q_ref, k_ref, v_ref, o_ref, lse_ref, m_s, acc_s, s_buf):
        i = pl.program_id(1)
        m_s[...] = jnp.full(m_s.shape, NEG, F32)
        acc_s[...] = jnp.zeros(acc_s.shape, F32)

        def rows_of(j):
            return pl.ds(pl.multiple_of(j * t, t), t)

        def scores(qi, j):
            return lax.dot_general(q_ref[0, rows_of(qi), :], k_ref[0, rows_of(j), :], (((1,), (1,)), ((), ())),
                                   preferred_element_type=F32)

        def consume(j, slot, masked):
            sc = s_buf[slot]
            if masked:
                sc = jnp.where(_causal_mask(t), sc, NEG)
            m_prev = m_s[...]
            m_new = jnp.maximum(m_prev, jnp.max(sc, axis=-1, keepdims=True))
            alpha = jnp.exp2(m_prev - m_new)
            p = jnp.exp2(sc - jnp.tile(m_new, (1, t // LANE)))
            acc_s[...] = jnp.tile(alpha, (1, 2)) * acc_s[...] + jnp.dot(
                p.astype(MXU_DTYPE), v_ref[0, rows_of(j), :], preferred_element_type=F32)
            m_s[...] = m_new

        nxt = jnp.minimum(i + 1, n - 1)

        @pl.when(i == 0)
        def _():
            s_buf[2] = scores(0, 0)
            consume(0, 2, True)
            s_buf[2] = scores(nxt, 0)

        @pl.when(i > 0)
        def _():
            s_buf[1] = scores(i, 1)
            consume(0, 2, False)

            def pair(a, carry):
                s_buf[0] = scores(i, 2 * a + 2)
                consume(2 * a + 1, 1, False)
                s_buf[1] = scores(i, 2 * a + 3)
                consume(2 * a + 2, 0, False)
                return carry

            lax.fori_loop(0, (i - 1) // 2, pair, 0)

            @pl.when(i % 2 == 1)
            def _():
                s_buf[2] = scores(nxt, 0)
                consume(i, 1, True)

            @pl.when(i % 2 == 0)
            def _():
                s_buf[0] = scores(i, i)
                consume(i - 1, 1, False)
                s_buf[2] = scores(nxt, 0)
                consume(i, 0, True)

        den = acc_s[:, dv:]
        o_ref[...] = acc_s[:, :dv] / den
        lse_ref[0, 0] = jnp.transpose(m_s[...] + jnp.log2(den))[0:1, :]

    head = lambda h, i: (h, 0, 0)
    return pl.pallas_call(
        body, name=name, grid=(nh, n),
        in_specs=[pl.BlockSpec((1, s, dk), head), pl.BlockSpec((1, s, dk), head), pl.BlockSpec((1, s, 2 * dv), head)],
        out_specs=[pl.BlockSpec((t, dv), lambda h, i: (i, h)),
                   pl.BlockSpec((1, 1, 1, t), lambda h, i: (h, i, 0, 0))],
        out_shape=[_sds((s, nh * dv), F32), _sds((nh, n, 1, t), F32)],
        scratch_shapes=[pltpu.VMEM((t, LANE), F32), pltpu.VMEM((t, 2 * dv), F32), pltpu.VMEM((3, t, t), F32)],
        compiler_params=_cp(("arbitrary", "arbitrary")),
    )(qf, kf, va)


def _shifted_copies(ext_ref):
    rows = ext_ref.shape[1] - 8
    for s in range(1, 8):
        ext_ref[s, 0:rows, :] = ext_ref[0, s:s + rows, :]


def _windows(ext_ref, offsets, t_rows, lane0, lanes):
    for s in range(8):
        group = [o for o in offsets if o % 8 == s]
        if not group:
            continue
        lo, hi = min(group) - s, max(group) - s
        wide = ext_ref[s, pl.ds(lo, hi - lo + t_rows), lane0:lane0 + lanes]
        for o in group:
            yield o, wide[o - s - lo:o - s - lo + t_rows]


def _dw_taps(ext_ref, w_ref, row0, t_rows, lane0, lanes, first_off):
    acc = None
    for off, win in _windows(ext_ref, [row0 + first_off + k for k in range(CONV_K)], t_rows, lane0, lanes):
        k = off - row0 - first_off
        term = w_ref[k:k + 1, lane0:lane0 + lanes] * win
        acc = term if acc is None else acc + term
    return acc


CONV_RC = 32
CONV_LC = 256


def _conv_fwd(z, glu_b, dw_w, dw_b, ln_g, ln_b, w_pw, *, name):
    s = z.shape[0]
    t = min(CONV_T, s)
    c2 = 2 * D_CONV
    hb = t // HALO

    def body(zm_ref, zh_ref, gb_ref, w_ref, wb_ref, g_ref, b_ref, wpw_ref, u1_ref, u3_ref, u4_ref, ext):
        i = pl.program_id(0)

        def glu(zv):
            ci = zv + gb_ref[...]
            return ci[:, :D_CONV] * jax.nn.sigmoid(ci[:, D_CONV:])

        ext[0, HALO:, :] = glu(zm_ref[...])
        ext[0, 0:HALO, :] = jnp.where(i > 0, glu(zh_ref[...]), 0.0)
        _shifted_copies(ext)
        for rc in range(0, t, CONV_RC):
            for lc in range(0, D_CONV, CONV_LC):
                acc = _dw_taps(ext, w_ref, rc, CONV_RC, lc, CONV_LC, HALO - (CONV_K - 1))
                u1_ref[rc:rc + CONV_RC, lc:lc + CONV_LC] = acc + wb_ref[:, lc:lc + CONV_LC]
        u1 = u1_ref[...]
        mu = jnp.mean(u1, axis=-1, keepdims=True)
        cen = u1 - mu
        var = jnp.mean(cen * cen, axis=-1, keepdims=True)
        u2 = (cen * lax.rsqrt(var + EPS)) * g_ref[...] + b_ref[...]
        u3_ref[...] = _silu(u2).astype(u3_ref.dtype)
        u4_ref[...] = jnp.dot(u3_ref[...], wpw_ref[...], preferred_element_type=F32)

    return pl.pallas_call(
        body, name=name, grid=(s // t,),
        in_specs=[_rowspec(t, c2), pl.BlockSpec((HALO, c2), lambda i: (jnp.maximum(i * hb - 1, 0), 0)),
                  _vecspec(c2), pl.BlockSpec((HALO, D_CONV), lambda i: (0, 0)), _vecspec(D_CONV),
                  _vecspec(D_CONV), _vecspec(D_CONV), pl.BlockSpec((D_CONV, D_CONV), lambda i: (0, 0))],
        out_specs=[_rowspec(t, D_CONV), _rowspec(t, D_CONV), _rowspec(t, D_CONV)],
        out_shape=[_sds((s, D_CONV), F32), _sds((s, D_CONV), MXU_DTYPE), _sds((s, D_CONV), F32)],
        scratch_shapes=[pltpu.VMEM((8, t + HALO, D_CONV), F32)],
        compiler_params=_cp(("parallel",)),
    )(z, z, glu_b, dw_w, dw_b, ln_g, ln_b, w_pw)


def _gate_cat(o, z, u4m, b_pw, *, name):
    s = o.shape[0]
    t = min(2 * ROW_T, s)

    def body(o_ref, mg_ref, u4_ref, cg_ref, b_ref, cat_ref):
        cat_ref[:, :D_MLA] = (o_ref[...] * _silu(mg_ref[...])).astype(cat_ref.dtype)
        cat_ref[:, D_MLA:] = ((u4_ref[...] + b_ref[...]) * _silu(cg_ref[...])).astype(cat_ref.dtype)

    return pl.pallas_call(
        body, name=name, grid=(s // t,),
        in_specs=[_rowspec(t, D_MLA), _rowspec(t, D_MLA, SEG_MG[0] // D_MLA), _rowspec(t, D_CONV),
                  _rowspec(t, D_CONV, SEG_CG[0] // D_CONV), _vecspec(D_CONV)],
        out_specs=_rowspec(t, D_MLA + D_CONV), out_shape=_sds((s, D_MLA + D_CONV), MXU_DTYPE),
        compiler_params=_cp(("parallel",)),
    )(o, z, u4m, z, b_pw)


def _gated_residual_bwd(gx, y_ref, gate_ref, dy_ref, dgate_ref):
    dy_ref[...] = (gx * gate_ref[...]).astype(dy_ref.dtype)
    dgate_ref[...] += _colsum(gx * y_ref[...])


def _loss_head(xf, target, y, gate, *, name):
    s, d = xf.shape
    t = min(2 * ROW_T, s)

    def body(x_ref, t_ref, y_ref, gate_ref, gx_ref, loss_ref, dy_ref, dgate_ref):
        @pl.when(pl.program_id(0) == 0)
        def _():
            loss_ref[...] = jnp.zeros(loss_ref.shape, F32)
            dgate_ref[...] = jnp.zeros(dgate_ref.shape, F32)

        err = x_ref[...] - t_ref[...]
        gx = err * (1.0 / d)
        gx_ref[...] = gx
        loss_ref[...] += 0.5 * jnp.sum(_lanesum(err * err) * (1.0 / d), axis=0, keepdims=True)
        _gated_residual_bwd(gx, y_ref, gate_ref, dy_ref, dgate_ref)

    return pl.pallas_call(
        body, name=name, grid=(s // t,),
        in_specs=[_rowspec(t, d), _rowspec(t, d), _rowspec(t, d), _vecspec(d)],
        out_specs=[_rowspec(t, d), pl.BlockSpec((1, 1), lambda i: (0, 0)), _rowspec(t, d), _vecspec(d)],
        out_shape=[_sds((s, d), F32), _sds((1, 1), F32), _sds((s, d), MXU_DTYPE), _sds((1, d), F32)],
        compiler_params=_cp(("arbitrary",)),
    )(xf, target, y, gate)


def _acc_init(refs):
    @pl.when(pl.program_id(0) == 0)
    def _():
        for r in refs:
            r[...] = jnp.zeros(r.shape, r.dtype)


def _gate_bwd(dy, w_out, o, z, u4m, b_pw, *, name):
    s, d = dy.shape
    t = min(2 * ROW_T, s)
    gates = D_MLA + D_CONV
    assert SEG_CG[0] == SEG_MG[0] + D_MLA and SEG_MG[0] % gates == 0

    def body(dy_ref, w_ref, o_ref, mg_ref, u4_ref, cg_ref, b_ref,
             do_ref, delta_ref, du4_ref, gb_ref, dz_ref):
        _acc_init([gb_ref])
        dcat = lax.dot_general(dy_ref[...], w_ref[...], (((1,), (1,)), ((), ())), preferred_element_type=F32)
        dm, ov, mg = dcat[:, :D_MLA], o_ref[...], mg_ref[...]
        do = dm * _silu(mg)
        do_ref[...] = do.astype(do_ref.dtype)
        dz_ref[:, :D_MLA] = (dm * ov * _dsilu(mg)).astype(dz_ref.dtype)
        prod = do * ov
        for h in range(N_HEADS):
            rowsum = jnp.broadcast_to(_lanesum(prod[:, h * V_DIM:(h + 1) * V_DIM]), (t, LANE))
            delta_ref[h, 0] = jnp.transpose(rowsum)[0:1, :]
        dc, cg = dcat[:, D_MLA:], cg_ref[...]
        du4 = dc * _silu(cg)
        du4_ref[...] = du4.astype(du4_ref.dtype)
        dz_ref[:, D_MLA:] = (dc * (u4_ref[...] + b_ref[...]) * _dsilu(cg)).astype(dz_ref.dtype)
        gb_ref[...] += _colsum(du4)

    return pl.pallas_call(
        body, name=name, grid=(s // t,),
        in_specs=[_rowspec(t, d), pl.BlockSpec((gates, d), lambda i: (0, 0)), _rowspec(t, D_MLA),
                  _rowspec(t, D_MLA, SEG_MG[0] // D_MLA), _rowspec(t, D_CONV),
                  _rowspec(t, D_CONV, SEG_CG[0] // D_CONV), _vecspec(D_CONV)],
        out_specs=[_rowspec(t, D_MLA), pl.BlockSpec((N_HEADS, 1, 1, t), lambda i: (0, i, 0, 0)),
                   _rowspec(t, D_CONV), _vecspec(D_CONV), _rowspec(t, gates, SEG_MG[0] // gates)],
        out_shape=[_sds((s, D_MLA), MXU_DTYPE), _sds((N_HEADS, s // t, 1, t), F32),
                   _sds((s, D_CONV), MXU_DTYPE), _sds((1, D_CONV), F32), _sds((s, IN_PAD), MXU_DTYPE)],
        compiler_params=_cp(("arbitrary",)),
    )(dy, w_out, o, z, u4m, z, b_pw)


def _conv_bwd(du3, u1, z, dz, glu_b, dw_w, ln_g, ln_b, *, name):
    s = z.shape[0]
    t = min(CONV_T, s)
    c2 = 2 * D_CONV
    hb = t // HALO
    n_blk = s // t
    last_halo = s // HALO - 1

    def body(d3m_ref, d3h_ref, u1m_ref, u1h_ref, zm_ref, zh_ref, gb_ref, w_ref, g_ref, b_ref, dz_in_ref,
             dci_ref, gg_ref, gbn_ref, gwb_ref, ggb_ref, gw_ref, dext, uext, du0_s, gw_acc):
        i = pl.program_id(0)
        _acc_init([gg_ref, gbn_ref, gwb_ref, ggb_ref, gw_acc])

        def ln_bwd(d3, u1v):
            mu = jnp.mean(u1v, axis=-1, keepdims=True)
            cen = u1v - mu
            rstd = lax.rsqrt(jnp.mean(cen * cen, axis=-1, keepdims=True) + EPS)
            uh = cen * rstd
            d2 = d3 * _dsilu(uh * g_ref[...] + b_ref[...])
            dh = d2 * g_ref[...]
            d1 = rstd * (dh - jnp.mean(dh, axis=-1, keepdims=True) - uh * jnp.mean(dh * uh, axis=-1, keepdims=True))
            return d1, d2, uh

        d1, d2, uh = ln_bwd(d3m_ref[...], u1m_ref[...])
        gg_ref[...] += _colsum(d2 * uh)
        gbn_ref[...] += _colsum(d2)
        gwb_ref[...] += _colsum(d1)
        dext[0, 0:t, :] = d1
        d1h, _, _ = ln_bwd(d3h_ref[...], u1h_ref[...])
        dext[0, t:, :] = jnp.where(i < n_blk - 1, d1h, 0.0)
        _shifted_copies(dext)

        def glu_parts(zv):
            ci = zv + gb_ref[...]
            return ci[:, :D_CONV], jax.nn.sigmoid(ci[:, D_CONV:])

        val, sg = glu_parts(zm_ref[...])
        uext[0, HALO:, :] = val * sg
        valh, sgh = glu_parts(zh_ref[...])
        uext[0, 0:HALO, :] = jnp.where(i > 0, valh * sgh, 0.0)
        _shifted_copies(uext)

        for rc in range(0, t, CONV_RC):
            for lc in range(0, D_CONV, CONV_LC):
                acc = None
                for off, win in _windows(dext, [rc + k for k in range(CONV_K)], CONV_RC, lc, CONV_LC):
                    k = (CONV_K - 1) - (off - rc)
                    term = w_ref[k:k + 1, lc:lc + CONV_LC] * win
                    acc = term if acc is None else acc + term
                du0_s[rc:rc + CONV_RC, lc:lc + CONV_LC] = acc
                dchunk = dext[0, rc:rc + CONV_RC, lc:lc + CONV_LC]
                first = rc + HALO - (CONV_K - 1)
                for off, win in _windows(uext, [first + k for k in range(CONV_K)], CONV_RC, lc, CONV_LC):
                    k = off - first
                    pr = dchunk * win
                    part = pr[0:8]
                    for r8 in range(8, CONV_RC, 8):
                        part = part + pr[r8:r8 + 8]
                    gw_acc[k, :, lc:lc + CONV_LC] += part

        du0 = du0_s[...]
        dval = du0 * sg
        dgt = du0 * val * sg * (1.0 - sg)
        dci_ref[:, :D_CONV] = dval.astype(dci_ref.dtype)
        dci_ref[:, D_CONV:] = dgt.astype(dci_ref.dtype)
        ggb_ref[:, :D_CONV] += _colsum(dval)
        ggb_ref[:, D_CONV:] += _colsum(dgt)

        @pl.when(i == n_blk - 1)
        def _():
            gw_ref[...] = jnp.sum(gw_acc[...], axis=1)

    halo_next = lambda w: pl.BlockSpec((HALO, w), lambda i: (jnp.minimum((i + 1) * hb, last_halo), 0))
    return pl.pallas_call(
        body, name=name, grid=(n_blk,),
        in_specs=[_rowspec(t, D_CONV), halo_next(D_CONV), _rowspec(t, D_CONV), halo_next(D_CONV),
                  _rowspec(t, c2), pl.BlockSpec((HALO, c2), lambda i: (jnp.maximum(i * hb - 1, 0), 0)),
                  _vecspec(c2), pl.BlockSpec((HALO, D_CONV), lambda i: (0, 0)), _vecspec(D_CONV), _vecspec(D_CONV),
                  _ANY],
        out_specs=[_rowspec(t, c2, SEG_CI[0] // c2), _vecspec(D_CONV), _vecspec(D_CONV), _vecspec(D_CONV),
                   _vecspec(c2), pl.BlockSpec((HALO, D_CONV), lambda i: (0, 0))],
        out_shape=[_sds(dz.shape, dz.dtype), _sds((1, D_CONV), F32), _sds((1, D_CONV), F32), _sds((1, D_CONV), F32),
                   _sds((1, c2), F32), _sds((HALO, D_CONV), F32)],
        scratch_shapes=[pltpu.VMEM((8, t + HALO, D_CONV), F32), pltpu.VMEM((8, t + HALO, D_CONV), F32),
                        pltpu.VMEM((t, D_CONV), F32), pltpu.VMEM((HALO, 8, D_CONV), F32)],
        input_output_aliases={10: 0},
        compiler_params=_cp(("arbitrary",)),
    )(du3, du3, u1, u1, z, z, glu_b, dw_w, ln_g, ln_b, dz)


def _flash_bwd(qf, kf, va, do, lse_t, delta_t, *, name):
    nh, s, dk = qf.shape
    dv = va.shape[-1] // 2
    t = min(ATT_T, s)
    n = s // t
    nt = (((1,), (1,)), ((), ()))
    tn = (((0,), (0,)), ((), ()))

    def body(q_ref, do_ref, lse_ref, dl_ref, k_ref, v_ref, dq_ref, dk_ref, dv_ref,
             dk_s, dv_s, st_buf, dpt_buf):
        n_un = pl.program_id(1)
        j = n - 1 - n_un
        nxt = jnp.maximum(j - 1, 0)

        @pl.when(n_un == 0)
        def _():
            dq_ref[...] = jnp.zeros(dq_ref.shape, F32)

        dk_s[...] = jnp.zeros(dk_s.shape, F32)
        dv_s[...] = jnp.zeros(dv_s.shape, F32)

        def rows_at(blk):
            return pl.ds(pl.multiple_of(blk * t, t), t)

        def rows_of(b):
            return rows_at(n - 1 - b)

        k = k_ref[0, rows_at(j), :]

        def produce(kj, b, slot):
            rows = rows_of(b)
            st_buf[slot] = lax.dot_general(k_ref[0, rows_at(kj), :], q_ref[0, rows, :], nt,
                                           preferred_element_type=F32)
            dpt_buf[slot] = lax.dot_general(v_ref[0, rows_at(kj), 0:dv], do_ref[rows, :], nt,
                                            preferred_element_type=F32)

        def consume(b, slot, masked):
            i = n - 1 - b
            rows = rows_of(b)
            q, dov = q_ref[0, rows, :], do_ref[rows, :]
            pt = jnp.exp2(st_buf[slot] - lse_ref[0, i])
            if masked:
                key = lax.broadcasted_iota(jnp.int32, (t, t), 0)
                qry = lax.broadcasted_iota(jnp.int32, (t, t), 1)
                pt = jnp.where(key <= qry, pt, 0.0)
            dv_s[...] += jnp.dot(pt.astype(MXU_DTYPE), dov, preferred_element_type=F32)
            dst = (pt * (dpt_buf[slot] - dl_ref[0, i])).astype(MXU_DTYPE)
            dk_s[...] += jnp.dot(dst, q, preferred_element_type=F32)
            dq_ref[0, rows, :] += lax.dot_general(dst, k, tn, preferred_element_type=F32)

        @pl.when(n_un == 0)
        def _():
            produce(j, 0, 2)
            consume(0, 2, True)
            produce(nxt, 0, 2)

        @pl.when(n_un > 0)
        def _():
            produce(j, 1, 1)
            consume(0, 2, False)

            def pair(a, carry):
                produce(j, 2 * a + 2, 0)
                consume(2 * a + 1, 1, False)
                produce(j, 2 * a + 3, 1)
                consume(2 * a + 2, 0, False)
                return carry

            lax.fori_loop(0, (n_un - 1) // 2, pair, 0)

            @pl.when(n_un % 2 == 1)
            def _():
                produce(nxt, 0, 2)
                consume(n_un, 1, True)

            @pl.when(n_un % 2 == 0)
            def _():
                produce(j, n_un, 0)
                consume(n_un - 1, 1, False)
                produce(nxt, 0, 2)
                consume(n_un, 0, True)

        dk_ref[0] = dk_s[...]
        dv_ref[0] = dv_s[...]

    head = lambda h, j: (h, 0, 0)
    rowv = pl.BlockSpec((1, n, 1, t), lambda h, j: (h, 0, 0, 0))
    return pl.pallas_call(
        body, name=name, grid=(nh, n),
        in_specs=[pl.BlockSpec((1, s, dk), head),
                  pl.BlockSpec((s, dv), lambda h, j: (0, h)),
                  rowv, rowv,
                  pl.BlockSpec((1, s, dk), head),
                  pl.BlockSpec((1, s, 2 * dv), head)],
        out_specs=[pl.BlockSpec((1, s, dk), head),
                   pl.BlockSpec((1, t, dk), lambda h, g: (h, n - 1 - g, 0)),
                   pl.BlockSpec((1, t, dv), lambda h, g: (h, n - 1 - g, 0))],
        out_shape=[_sds((nh, s, dk), F32), _sds((nh, s, dk), F32), _sds((nh, s, dv), F32)],
        scratch_shapes=[pltpu.VMEM((t, dk), F32), pltpu.VMEM((t, dv), F32),
                        pltpu.VMEM((3, t, t), F32), pltpu.VMEM((3, t, t), F32)],
        compiler_params=_cp(("arbitrary", "arbitrary")),
    )(qf, do, lse_t, delta_t, kf, va)


def _mla_bwd(dqf, dkf, dvf, q_raw, kv, z, dz, qn, kn, w_q_up, w_kv_up, g_ql, g_kvl, c_t, s1_t, s2_t,
             gqn, gqr, gkn, gkr, *, name):
    s = q_raw.shape[0]
    t = min(ROW_T, s)
    scale = 1.0 / math.sqrt(QK_DIM)
    o_ql, o_kvl, o_kr = (seg[0] - SEG_LAT[0] for seg in (SEG_QL, SEG_KVL, SEG_KR))
    tn = (((0,), (0,)), ((), ()))
    nt = (((1,), (1,)), ((), ()))

    def body(dq_ref, dk_ref, dv_ref, q_ref, kv_ref, kr_ref, ql_ref, kvl_ref, qn_ref, kn_ref, wq_ref, wkv_ref,
             gq_ref, gk_ref, c_ref, s1_ref, s2_ref, gqn_ref, gqr_ref, gkn_ref, gkr_ref, dz_in_ref,
             dz_ref, gwq_ref, gwkv_ref, ggq_ref, ggk_ref, gql_ref, gkvl_ref, dqr_ref, dkv_ref):
        _acc_init([gwq_ref, gwkv_ref, ggq_ref, ggk_ref, gql_ref, gkvl_ref])
        c_v, s1_v, s2_v = c_ref[...], s1_ref[...], s2_ref[...]
        kr = kr_ref[...]
        kr_ss = _lanesum(kr * kr)
        dkr = jnp.zeros(kr.shape, F32)
        ggq_n = ggq_r = ggk_n = ggk_r = jnp.zeros((1, LANE), F32)

        def norm_bwd(n, r, rs, dyn, dyr, gn, gr):
            nh_, rh_ = n * rs, r * rs
            dnh, drh = dyn * gn, dyr * gr
            dot = (_lanesum(dnh * nh_) + _lanesum(drh * rh_)) * (1.0 / QK_DIM)
            return rs * (dnh - nh_ * dot), rs * (drh - rh_ * dot), _colsum(dyn * nh_), _colsum(dyr * rh_)

        for h in range(N_HEADS):
            n = q_ref[:, h * LANE:(h + 1) * LANE]
            r = q_ref[:, N_HEADS * LANE + h * LANE:N_HEADS * LANE + (h + 1) * LANE]
            rs = lax.rsqrt((_lanesum(n * n) + _lanesum(r * r)) * (1.0 / QK_DIM) + EPS)
            dyn = dq_ref[h, :, 0:LANE] * scale
            dyr = _rope_bwd(dq_ref[h, :, LANE:HEAD_PAD] * scale, c_v, s1_v, s2_v)
            dn, dr, g_n, g_r = norm_bwd(n, r, rs, dyn, dyr, gqn_ref[...], gqr_ref[...])
            dqr_ref[:, h * LANE:(h + 1) * LANE] = dn.astype(dqr_ref.dtype)
            dqr_ref[:, N_HEADS * LANE + h * LANE:N_HEADS * LANE + (h + 1) * LANE] = dr.astype(dqr_ref.dtype)
            ggq_n, ggq_r = ggq_n + g_n, ggq_r + g_r

            n = kv_ref[:, h * 2 * LANE:h * 2 * LANE + LANE]
            rs = lax.rsqrt((_lanesum(n * n) + kr_ss) * (1.0 / QK_DIM) + EPS)
            dyn = dk_ref[h, :, 0:LANE] * (1.0 / LOG2E)
            dyr = _rope_bwd(dk_ref[h, :, LANE:HEAD_PAD] * (1.0 / LOG2E), c_v, s1_v, s2_v)
            dn, dr, g_n, g_r = norm_bwd(n, kr, rs, dyn, dyr, gkn_ref[...], gkr_ref[...])
            dkv_ref[:, h * 2 * LANE:h * 2 * LANE + LANE] = dn.astype(dkv_ref.dtype)
            dkv_ref[:, h * 2 * LANE + LANE:(h + 1) * 2 * LANE] = dv_ref[h].astype(dkv_ref.dtype)
            dkr = dkr + dr
            ggk_n, ggk_r = ggk_n + g_n, ggk_r + g_r

        ggq_ref[:, 0:LANE] += ggq_n
        ggq_ref[:, LANE:] += ggq_r
        ggk_ref[:, 0:LANE] += ggk_n
        ggk_ref[:, LANE:] += ggk_r

        for d_ref, x_ref, w_ref, gw_ref, src, g_ref, off, gg_ref in (
                (dqr_ref, qn_ref, wq_ref, gwq_ref, ql_ref, gq_ref, o_ql, gql_ref),
                (dkv_ref, kn_ref, wkv_ref, gwkv_ref, kvl_ref, gk_ref, o_kvl, gkvl_ref)):
            dup = d_ref[...]
            gw_ref[...] += lax.dot_general(x_ref[...], dup, tn, preferred_element_type=F32)
            dy = lax.dot_general(dup, w_ref[...], nt, preferred_element_type=F32)
            v = src[...]
            r = lax.rsqrt(jnp.mean(v * v, axis=-1, keepdims=True) + EPS)
            vh = v * r
            dvh = dy * g_ref[...]
            dz_ref[:, off:off + v.shape[1]] = (
                r * (dvh - vh * jnp.mean(dvh * vh, axis=-1, keepdims=True))).astype(dz_ref.dtype)
            gg_ref[...] += _colsum(dy * vh)
        dz_ref[:, o_kr:o_kr + LANE] = dkr.astype(dz_ref.dtype)
        dz_ref[:, o_kr + LANE:] = jnp.zeros((t, SEG_LAT[1] - o_kr - LANE), dz_ref.dtype)

    hspec = lambda w: pl.BlockSpec((N_HEADS, t, w), lambda i: (0, i, 0))
    whole = lambda a: pl.BlockSpec(a.shape, lambda i: (0, 0))
    wide = 2 * N_HEADS * LANE
    return pl.pallas_call(
        body, name=name, grid=(s // t,),
        in_specs=[hspec(HEAD_PAD), hspec(HEAD_PAD), hspec(V_DIM), _rowspec(t, wide), _rowspec(t, wide),
                  _rowspec(t, LANE, SEG_KR[0] // LANE), _rowspec(t, Q_LORA, SEG_QL[0] // Q_LORA),
                  _rowspec(t, KV_LORA, SEG_KVL[0] // KV_LORA), _rowspec(t, Q_LORA), _rowspec(t, KV_LORA),
                  whole(w_q_up), whole(w_kv_up), _vecspec(Q_LORA), _vecspec(KV_LORA),
                  _rowspec(t, LANE), _rowspec(t, LANE), _rowspec(t, LANE),
                  _vecspec(LANE), _vecspec(LANE), _vecspec(LANE), _vecspec(LANE), _ANY],
        out_specs=[_rowspec(t, SEG_LAT[1], SEG_LAT[0] // SEG_LAT[1]), whole(w_q_up), whole(w_kv_up),
                   _vecspec(2 * LANE), _vecspec(2 * LANE), _vecspec(Q_LORA), _vecspec(KV_LORA)],
        out_shape=[_sds(dz.shape, dz.dtype), _sds(w_q_up.shape, F32), _sds(w_kv_up.shape, F32),
                   _sds((1, 2 * LANE), F32), _sds((1, 2 * LANE), F32), _sds((1, Q_LORA), F32),
                   _sds((1, KV_LORA), F32)],
        scratch_shapes=[pltpu.VMEM((t, wide), MXU_DTYPE), pltpu.VMEM((t, wide), MXU_DTYPE)],
        input_output_aliases={21: 0},
        compiler_params=_cp(("arbitrary",)),
    )(dqf, dkf, dvf, q_raw, kv, z, z, z, qn, kn, w_q_up, w_kv_up, g_ql, g_kvl, c_t, s1_t, s2_t,
      gqn, gqr, gkn, gkr, dz)


def _prenorm_bwd(dh, x, gxo, g, sc1p, below=None, *, name):
    s, d = x.shape
    t = min(2 * ROW_T if below is None else ROW_T, s)
    nb = 0 if below is None else 2

    def body(dh_ref, x_ref, gx_ref, g_ref, sc_ref, *rest):
        dx_ref, dsh_ref, dsc_ref, gg_ref = rest[nb:nb + 4]
        _acc_init([dsh_ref, dsc_ref, gg_ref])
        xv, dhv = x_ref[...], dh_ref[...]
        r = lax.rsqrt(jnp.mean(xv * xv, axis=-1, keepdims=True) + EPS)
        xn = xv * r
        dsh_ref[...] += _colsum(dhv)
        dsc_ref[...] += _colsum(dhv * (xn * g_ref[...]))
        dm = dhv * sc_ref[...]
        gg_ref[...] += _colsum(dm * xn)
        dxn = dm * g_ref[...]
        dx = gx_ref[...] + r * (dxn - xn * jnp.mean(dxn * xn, axis=-1, keepdims=True))
        dx_ref[...] = dx
        if below is not None:
            _acc_init([rest[nb + 5]])
            _gated_residual_bwd(dx, rest[0], rest[1], rest[nb + 4], rest[nb + 5])

    vec_out = [_vecspec(d), _vecspec(d), _vecspec(d)]
    vec_shape = [_sds((1, d), F32)] * 3
    return pl.pallas_call(
        body, name=name, grid=(s // t,),
        in_specs=[_rowspec(t, d), _rowspec(t, d), _rowspec(t, d), _vecspec(d), _vecspec(d)]
        + ([_rowspec(t, d), _vecspec(d)] if below is not None else []),
        out_specs=[_rowspec(t, d)] + vec_out + ([_rowspec(t, d), _vecspec(d)] if below is not None else []),
        out_shape=[_sds((s, d), F32)] + vec_shape
        + ([_sds((s, d), MXU_DTYPE), _sds((1, d), F32)] if below is not None else []),
        compiler_params=_cp(("arbitrary",)),
    )(dh, x, gxo, g, sc1p, *(below if below is not None else ()))


def _ada_fwd(c_all, ada_w, ada_b_cols, *, name):
    nl, d, cols = ada_w.shape

    def body(c_ref, w_ref, b_ref, o_ref):
        ca = _silu(c_ref[...]).astype(MXU_DTYPE)
        o_ref[0] = jnp.dot(ca, w_ref[0].astype(MXU_DTYPE), preferred_element_type=F32) + b_ref[0]

    return pl.pallas_call(
        body, name=name, grid=(nl,),
        in_specs=[pl.BlockSpec((N_DEV, d), lambda l: (0, 0)), pl.BlockSpec((1, d, cols), lambda l: (l, 0, 0)),
                  pl.BlockSpec((1, 1, cols), lambda l: (l, 0, 0))],
        out_specs=pl.BlockSpec((1, N_DEV, cols), lambda l: (l, 0, 0)),
        out_shape=_sds((nl, N_DEV, cols), F32),
        compiler_params=_cp(("parallel",)),
    )(c_all, ada_w, ada_b_cols)


def _ada_bwd(c_all_t, dmod_cols, *, name):
    nl, _, cols = dmod_cols.shape
    d = c_all_t.shape[0]

    def body(c_ref, dm_ref, o_ref):
        ca = _silu(c_ref[...]).astype(MXU_DTYPE)
        o_ref[0] = jnp.dot(ca, dm_ref[0].astype(MXU_DTYPE), preferred_element_type=F32)

    return pl.pallas_call(
        body, name=name, grid=(nl,),
        in_specs=[pl.BlockSpec((d, N_DEV), lambda l: (0, 0)), pl.BlockSpec((1, N_DEV, cols), lambda l: (l, 0, 0))],
        out_specs=pl.BlockSpec((1, d, cols), lambda l: (l, 0, 0)),
        out_shape=_sds((nl, d, cols), F32),
        compiler_params=_cp(("parallel",)),
    )(c_all_t, dmod_cols)


def _adamw_math(g, w, m, v):
    mn = ADAM_B1 * m + (1.0 - ADAM_B1) * g
    vn = ADAM_B2 * v + (1.0 - ADAM_B2) * (g * g)
    m_hat = mn / (1.0 - ADAM_B1 ** ADAM_STEP)
    v_hat = vn / (1.0 - ADAM_B2 ** ADAM_STEP)
    return -ADAM_LR * (m_hat / (jnp.sqrt(v_hat) + ADAM_EPS) + ADAM_WD * w), mn, vn


def _adamw_small(items, *, name):
    n = len(items)
    shapes = [it[1].shape for it in items]
    flat = lambda a, lead: a.reshape(lead + (-1, a.shape[-1]))
    operands = []
    for gp, w, m, v in items:
        operands += [flat(gp, (gp.shape[0],)), flat(w, ()), flat(m, ()), flat(v, ())]
    nparts = [it[0].shape[0] for it in items]

    def body(*refs):
        ins, outs = refs[:4 * n], refs[4 * n:]
        for i in range(n):
            g_ref, w_ref, m_ref, v_ref = ins[4 * i:4 * i + 4]
            g = g_ref[0].astype(F32)
            for p in range(1, nparts[i]):
                g = g + g_ref[p].astype(F32)
            outs[4 * i][...] = g
            outs[4 * i + 1][...], outs[4 * i + 2][...], outs[4 * i + 3][...] = _adamw_math(
                g, w_ref[...], m_ref[...], v_ref[...])

    out_shape = []
    for it in items:
        out_shape += [_sds(flat(it[1], ()).shape, F32)] * 4
    outs = pl.pallas_call(body, name=name, out_shape=out_shape, compiler_params=_cp())(*operands)
    return [tuple(o.reshape(shp) for o in outs[4 * i:4 * i + 4]) for i, shp in enumerate(shapes)]


def _adamw_layer(gparts, w, m, v, layer, prev, *, name):
    shape = w.shape
    nl, cols = shape[0], shape[-1]
    rows = w.size // cols // nl
    npart = gparts.shape[0]
    g3 = gparts.reshape(npart, rows, cols)
    w3, m3, v3 = (a.reshape(nl, rows, cols) for a in (w, m, v))
    fits = [t for t in range(min(rows, 256) // 8 * 8, 7, -8)
            if rows % t == 0 and npart * t * cols * g3.dtype.itemsize <= 2 * 1024 * 1024]
    t = fits[0] if fits else rows
    n_prev = 0 if prev is None else 4

    def body(g_ref, w_ref, m_ref, v_ref, *rest):
        go_ref, d_ref, mo_ref, vo_ref = rest[n_prev:]
        g = g_ref[0].astype(F32)
        for p in range(1, npart):
            g = g + g_ref[p].astype(F32)
        go_ref[0] = g
        d_ref[0], mo_ref[0], vo_ref[0] = _adamw_math(g, w_ref[0], m_ref[0], v_ref[0])

    spec = pl.BlockSpec((1, t, cols), lambda i: (layer, i, 0))
    outs = pl.pallas_call(
        body, name=name, grid=(rows // t,),
        in_specs=[pl.BlockSpec((npart, t, cols), lambda i: (0, i, 0)), spec, spec, spec] + [_ANY] * n_prev,
        out_specs=[spec] * 4, out_shape=[_sds((nl, rows, cols), F32)] * 4,
        input_output_aliases={4 + k: k for k in range(n_prev)},
        compiler_params=_cp(("parallel",)),
    )(g3, w3, m3, v3, *([] if prev is None else [a.reshape(nl, rows, cols) for a in prev]))
    return tuple(o.reshape(shape) for o in outs)


def _adamw(gparts, w, m, v, *, name):
    shape = w.shape
    cols = shape[-1]
    per_layer = isinstance(gparts, (list, tuple))
    nl = shape[0] if per_layer else 1
    rows = w.size // cols // nl
    glist = list(gparts) if per_layer else [gparts]
    npart = glist[0].shape[0]
    glist = [g.reshape(npart, rows, cols) for g in glist]
    w3, m3, v3 = (a.reshape(nl, rows, cols) for a in (w, m, v))
    budget = 2 * 1024 * 1024
    fits = [t for t in range(min(rows, 256) // 8 * 8, 7, -8)
            if rows % t == 0 and npart * t * cols * glist[0].dtype.itemsize <= budget]
    t = fits[0] if fits else rows
    nb = rows // t

    def body(*refs):
        g_refs = refs[:nl]
        w_ref, m_ref, v_ref, go_ref, d_ref, mo_ref, vo_ref, g_s = refs[nl:]
        layer = pl.program_id(0)
        for l in range(nl):
            @pl.when(layer == l)
            def _(l=l):
                g = g_refs[l][0].astype(F32)
                for p in range(1, npart):
                    g = g + g_refs[l][p].astype(F32)
                g_s[...] = g

        g = g_s[...]
        go_ref[0] = g
        d_ref[0], mo_ref[0], vo_ref[0] = _adamw_math(g, w_ref[0], m_ref[0], v_ref[0])

    def g_map(l):
        return lambda layer, i: (0, jnp.where(layer == l, i, jnp.where(layer < l, 0, nb - 1)), 0)

    spec = pl.BlockSpec((1, t, cols), lambda layer, i: (layer, i, 0))
    outs = pl.pallas_call(
        body, name=name, grid=(nl, nb),
        in_specs=[pl.BlockSpec((npart, t, cols), g_map(l)) for l in range(nl)] + [spec, spec, spec],
        out_specs=[spec] * 4, out_shape=[_sds((nl, rows, cols), F32)] * 4,
        scratch_shapes=[pltpu.VMEM((t, cols), F32)],
        compiler_params=_cp(("arbitrary", "arbitrary")),
    )(*glist, w3, m3, v3)
    return tuple(o.reshape(shape) for o in outs)


_ANY = pl.BlockSpec(memory_space=pl.ANY)


def _all_gather(blocks, *, name):
    na = len(blocks)

    def body(*refs):
        x_refs, out_refs = refs[:na], refs[na:2 * na]
        send_sems, recv_sems, local_sems = refs[2 * na:]
        x, y, c = lax.axis_index("x"), lax.axis_index("y"), lax.axis_index("c")
        me, sibling = (x, y, c), (x, y, 1 - c)
        chips = [(1 - x, y), (x, 1 - y), (1 - x, 1 - y)]

        def slot(a, px, py, pc):
            return out_refs[a].at[4 * px + 2 * py + pc]

        def copy(a, k, blk, to, src=None):
            return pltpu.make_async_remote_copy(
                src_ref=slot(a, *blk) if src is None else src, dst_ref=slot(a, *blk),
                send_sem=send_sems.at[7 * a + k], recv_sem=recv_sems.at[7 * a + k],
                device_id=to, device_id_type=MESH_ID)

        mine = [pltpu.make_async_copy(x_refs[a], slot(a, *me), local_sems.at[a]) for a in range(na)]
        for cp in mine:
            cp.start()
        first = []
        for a in range(na):
            first.append(copy(a, 0, me, sibling, src=x_refs[a]))
            first += [copy(a, 1 + j, me, (*chip, c), src=x_refs[a]) for j, chip in enumerate(chips)]
        for cp in first:
            cp.start()
        passed = []
        for a in range(na):
            for j, chip in enumerate(chips):
                copy(a, 1 + j, (*chip, c), me).wait_recv()
                fwd = copy(a, 4 + j, (*chip, c), sibling)
                fwd.start()
                passed.append(fwd)
        for a in range(na):
            copy(a, 0, sibling, me).wait_recv()
            for j, chip in enumerate(chips):
                copy(a, 4 + j, (*chip, 1 - c), me).wait_recv()
        for cp in first + passed:
            cp.wait_send()
        for cp in mine:
            cp.wait()

    outs = pl.pallas_call(
        body, name=name, in_specs=[_ANY] * na, out_specs=[_ANY] * na,
        out_shape=[_sds((N_DEV,) + b.shape, b.dtype) for b in blocks],
        scratch_shapes=[pltpu.SemaphoreType.DMA((7 * na,)), pltpu.SemaphoreType.DMA((7 * na,)),
                        pltpu.SemaphoreType.DMA((na,))],
    )(*blocks)
    return list(outs)


_HBM = pl.BlockSpec(memory_space=pltpu.HBM)
_SEM = pl.BlockSpec(memory_space=pltpu.SEMAPHORE)
_EFFECT = pltpu.SideEffectType.DATAFLOW_SIDE_EFFECTING


def _peers(x, y, c):
    out = []
    for k in range(1, N_DEV):
        out.append((1 - x if k & 4 else x, 1 - y if k & 2 else y, 1 - c if k & 1 else c))
    return out


def _own_slots(srcs, scatter, *, name, after=None):
    na = len(srcs)
    n_extra = 0 if after is None else 1
    me = (4 * lax.axis_index("x") + 2 * lax.axis_index("y") + lax.axis_index("c")).astype(jnp.int32).reshape(1)

    def body(me_ref, *refs):
        in_refs, out_refs = refs[:na], refs[na + n_extra:]
        for a in range(na):
            out_refs[a][0] = in_refs[a][0] if scatter else in_refs[a][...]

    def slot_spec(shard):
        zeros = (0,) * len(shard)
        return pl.BlockSpec((1,) + tuple(shard), lambda i, me_ref: (me_ref[0],) + zeros)

    def whole_spec(shape):
        zeros = (0,) * len(shape)
        return pl.BlockSpec(tuple(shape), lambda i, me_ref: zeros)

    shards = [s.shape[1:] if scatter else s.shape for s in srcs]
    in_specs = [slot_spec(sh) if scatter else whole_spec(sh) for sh in shards] + [_ANY] * n_extra
    outs = pl.pallas_call(
        body, name=name,
        grid_spec=pltpu.PrefetchScalarGridSpec(
            num_scalar_prefetch=1, grid=(1,), in_specs=in_specs, out_specs=[slot_spec(sh) for sh in shards]),
        out_shape=[_sds((N_DEV,) + tuple(sh), s.dtype) for sh, s in zip(shards, srcs)],
        compiler_params=_cp(("arbitrary",)),
    )(me, *srcs, *([] if after is None else [after]))
    return list(outs)


_N_COPIES = dict(scatter=7, gather=7, chips=4, forward=3)


def _exchange_copies(src_refs, land_refs, send_sems, recv_sems, mode):
    x, y, c = lax.axis_index("x"), lax.axis_index("y"), lax.axis_index("c")
    me = 4 * x + 2 * y + c
    nc = _N_COPIES[mode]
    chips = [(1 - x, y), (x, 1 - y), (1 - x, 1 - y)]
    cps = []
    for a in range(len(land_refs)):
        if mode in ("scatter", "gather"):
            plan = [((src_refs[a].at[4 * px + 2 * py + pc] if mode == "scatter" else src_refs[a]),
                     land_refs[a].at[me], (px, py, pc)) for px, py, pc in _peers(x, y, c)]
        elif mode == "chips":
            plan = [(src_refs[a], land_refs[a].at[me], to) for to in [(x, y, 1 - c)] + [(*ch, c) for ch in chips]]
        else:
            plan = [(land_refs[a].at[4 * px + 2 * py + c], land_refs[a].at[4 * px + 2 * py + c], (x, y, 1 - c))
                    for px, py in chips]
        for k, (src, dst, to) in enumerate(plan):
            cps.append(pltpu.make_async_remote_copy(
                src_ref=src, dst_ref=dst, send_sem=send_sems.at[nc * a + k], recv_sem=recv_sems.at[nc * a + k],
                device_id=to, device_id_type=MESH_ID))
    return cps


def _exchange_start(srcs, lands, mode, *, name):
    ns, nz = len(srcs), len(lands)
    nsem = _N_COPIES[mode] * nz

    def body(*refs):
        src_refs, land_refs = refs[:ns], refs[ns:ns + nz]
        send_sems, recv_sems = refs[ns + nz], refs[ns + nz + 1]
        token = refs[-1]
        for cp in _exchange_copies(src_refs, land_refs, send_sems, recv_sems, mode):
            cp.start()
        token[...] = jnp.zeros(token.shape, token.dtype)

    hbm = lambda a: pltpu.HBM(a.shape, a.dtype)
    outs = pl.pallas_call(
        body, name=name,
        out_shape=(pltpu.SemaphoreType.DMA((nsem,)), pltpu.SemaphoreType.DMA((nsem,)),
                   *[hbm(a) for a in srcs], *[hbm(a) for a in lands], _sds((8, LANE), F32)),
        in_specs=[_HBM] * (ns + nz),
        out_specs=(_SEM, _SEM, *[_HBM] * (ns + nz), pl.BlockSpec(memory_space=pltpu.VMEM)),
        input_output_aliases={i: 2 + i for i in range(ns + nz)},
        compiler_params=pltpu.CompilerParams(has_side_effects=_EFFECT),
    )(*[pltpu.with_memory_space_constraint(a, pltpu.HBM) for a in list(srcs) + list(lands)])
    return outs[0], outs[1], list(outs[2:2 + ns]), list(outs[2 + ns:2 + ns + nz]), outs[-1]


def _exchange_wait(send_sems, recv_sems, srcs, lands, after, mode, *, name):
    ns, nz = len(srcs), len(lands)

    def body(*refs):
        src_refs, land_refs = refs[:ns], refs[ns:ns + nz]
        s_sems, r_sems = refs[ns + nz], refs[ns + nz + 1]
        for cp in _exchange_copies(src_refs, land_refs, s_sems, r_sems, mode):
            cp.wait_send()
            cp.wait_recv()

    hbm = lambda a: pltpu.HBM(a.shape, a.dtype)
    outs = pl.pallas_call(
        body, name=name,
        out_shape=(*[hbm(a) for a in srcs], *[hbm(a) for a in lands]),
        in_specs=[_HBM] * (ns + nz) + [_SEM, _SEM, _ANY],
        out_specs=tuple([_HBM] * (ns + nz)),
        input_output_aliases={i: i for i in range(ns + nz)},
        compiler_params=pltpu.CompilerParams(has_side_effects=_EFFECT),
    )(*srcs, *lands, send_sems, recv_sems, after)
    return list(outs[ns:])


_WIN_SEGS = (("ql", 0, Q_LORA, SEG_QL[0]), ("kvl", Q_LORA, KV_LORA, SEG_KVL[0]),
             ("kr", Q_LORA + KV_LORA, ROPE, SEG_KR[0]), ("mg", Q_LORA + KV_LORA + ROPE, D_MLA, SEG_MG[0]),
             ("ci", Q_LORA + KV_LORA + ROPE + D_MLA, 2 * D_CONV, SEG_CI[0]),
             ("cg", Q_LORA + KV_LORA + ROPE + D_MLA + 2 * D_CONV, D_CONV, SEG_CG[0]))
_WIN_SHARD = IN_COLS // N_DEV


def _win_pieces():
    out = []
    for _, o, n, new in _WIN_SEGS:
        for j in range(N_DEV):
            lo, hi = max(o, j * _WIN_SHARD), min(o + n, (j + 1) * _WIN_SHARD)
            if lo < hi:
                out.append((j, lo - j * _WIN_SHARD, new + lo - o, hi - lo))
    return out


WIN_T = 512


def _win_assemble(w_all, *, name):
    d = w_all.shape[2]
    t = min(WIN_T, d)
    pieces = sorted(_win_pieces(), key=lambda p: p[2])
    assert all(lo % 8 == 0 and n % 8 == 0 for _, lo, _, n in pieces)

    def body(w_ref, o_ref):
        rows = [w_ref[j].astype(F32)[lo:lo + n, :] for j, lo, _, n in pieces]
        rows.append(jnp.zeros((IN_PAD - (SEG_KR[0] + ROPE), t), F32))
        o_ref[...] = jnp.concatenate(rows, axis=0).astype(o_ref.dtype)

    return pl.pallas_call(
        body, name=name, grid=(d // t,),
        in_specs=[pl.BlockSpec((N_DEV, _WIN_SHARD, t), lambda i: (0, 0, i))],
        out_specs=pl.BlockSpec((IN_PAD, t), lambda i: (0, i)), out_shape=_sds((IN_PAD, d), w_all.dtype),
        compiler_params=_cp(("parallel",)),
    )(w_all)


def _win_split(grad, *, name):
    d = grad.shape[1]
    t = min(WIN_T, d)
    by_shard = [sorted([p for p in _win_pieces() if p[0] == j], key=lambda p: p[1]) for j in range(N_DEV)]

    def body(g_ref, o_ref):
        for j in range(N_DEV):
            rows = [g_ref[new:new + n, :] for _, _, new, n in by_shard[j]]
            o_ref[j] = jnp.concatenate(rows, axis=0).astype(o_ref.dtype)

    return pl.pallas_call(
        body, name=name, grid=(d // t,),
        in_specs=[pl.BlockSpec((IN_PAD, t), lambda i: (0, i))],
        out_specs=pl.BlockSpec((N_DEV, _WIN_SHARD, t), lambda i: (0, 0, i)),
        out_shape=_sds((N_DEV, _WIN_SHARD, d), WIRE_DTYPE),
        compiler_params=_cp(("parallel",)),
    )(grad)


def _cols_to_shards(a):
    r, n = a.shape
    return a.reshape(r, N_DEV, n // N_DEV).transpose(1, 0, 2)


def _shards_to_cols(a):
    nd, r, w = a.shape
    return a.transpose(1, 0, 2).reshape(r, nd * w)


def _qup_permute(w):
    w3 = w.reshape(w.shape[0], N_HEADS, QK_DIM)
    nope = w3[:, :, :NOPE].reshape(w.shape[0], N_HEADS * NOPE)
    rope = jnp.pad(w3[:, :, NOPE:], ((0, 0), (0, 0), (0, LANE - ROPE))).reshape(w.shape[0], N_HEADS * LANE)
    return jnp.concatenate([nope, rope], axis=1)


def _qup_unpermute(g):
    r = g.shape[0]
    nope = g[:, :N_HEADS * NOPE].reshape(r, N_HEADS, NOPE)
    rope = g[:, N_HEADS * NOPE:].reshape(r, N_HEADS, LANE)[:, :, :ROPE]
    return jnp.concatenate([nope, rope], axis=2).reshape(r, N_HEADS * QK_DIM)


def _norm_tiles(g):
    return g[:NOPE].reshape(1, LANE), jnp.pad(g[NOPE:], (0, LANE - ROPE)).reshape(1, LANE)


def _rope_tiles(positions):
    inv_freq = 1.0 / (ROPE_THETA ** (jnp.arange(0, ROPE, 2, dtype=F32) / ROPE))
    ang = positions.astype(F32)[:, None] * inv_freq
    cos, sin = jnp.cos(ang), jnp.sin(ang)
    zq = jnp.zeros_like(cos)
    c_t = jnp.concatenate([cos, cos, zq, zq], axis=1)
    s1_t = jnp.concatenate([-sin, zq, zq, zq], axis=1)
    s2_t = jnp.concatenate([zq, sin, zq, zq], axis=1)
    return c_t, s1_t, s2_t


_BIG = ("w_in", "w_q_up", "w_kv_up", "w_pw", "w_out")
_COL_SHARDED = ("w_q_up", "w_kv_up")


def _unpack_rows(buf, shapes):
    out, r0 = [], 0
    lead = buf.shape[:-2]
    for shp in shapes:
        n = math.prod(shp) // LANE
        out.append(buf[..., r0:r0 + n, :].reshape(lead + tuple(shp)))
        r0 += n
    return out


_SMALL = (("dmod", 3 * D_MODEL), ("norm_g", D_MODEL), ("q_lat_g", Q_LORA), ("kv_lat_g", KV_LORA),
          ("q_norm_g", 2 * LANE), ("k_norm_g", 2 * LANE), ("glu_b", 2 * D_CONV), ("dw_w", HALO * D_CONV),
          ("dw_b", D_CONV), ("conv_ln_g", D_CONV), ("conv_ln_b", D_CONV), ("b_pw", D_CONV))


def _layer_fwd(x, p, rope, l, late=None):
    n = lambda s: f"{s}_l{l}"
    c_t, s1_t, s2_t = rope
    h = _prenorm(x, p["norm_g"], p["shift"], p["sc1p"], name=n("prenorm"))
    z = _mm(h, p["w_in"], tb=True, name=n("in_proj"), tn=IN_TILE, n_outer=True)
    if late is not None:
        p = {**p, **late(z)}
    qn, kn, q_raw, kv, qf, kf, vf = _mla_pre(z, p["w_q_up"], p["w_kv_up"], p["q_lat_g"], p["kv_lat_g"],
                                             c_t, s1_t, s2_t, *p["qk_tiles"], name=n("mla_pre"))
    o, lse = _flash_fwd(qf, kf, vf, name=n("flash_fwd"))
    u1, u3, u4m = _conv_fwd(z, p["glu_b"], p["dw_w"], p["dw_b"], p["conv_ln_g"], p["conv_ln_b"], p["w_pw"],
                            name=n("conv_fwd"))
    cat = _gate_cat(o, z, u4m, p["b_pw"], name=n("gate_cat"))
    y, x_next = _mm(cat, p["w_out"], name=n("out_proj"), tn=1024, residual=(x, p["gate"]))
    saved = dict(x=x, h=h, z=z, qn=qn, kn=kn, q_raw=q_raw, kv=kv, qf=qf, kf=kf, vf=vf, o=o, lse=lse,
                 u1=u1, u3=u3, u4m=u4m, cat=cat, y=y)
    return x_next, saved, p


def _layer_bwd(gxo, dy, dgate, p, sv, rope, l, below=None, hook_rest=None, hook_w_in=None):
    n = lambda s: f"{s}_l{l}"
    c_t, s1_t, s2_t = rope
    z = sv["z"]
    g_w_out = _mm(sv["cat"], dy, ta=True, name=n("g_w_out"), tm=1024, tn=1024, after=p.get("after_start"))
    do, delta, du4, g_b_pw, dz = _gate_bwd(dy, p["w_out"], sv["o"], z, sv["u4m"], p["b_pw"], name=n("gate_bwd"))
    g_w_pw = _mm(sv["u3"], du4, ta=True, name=n("g_w_pw"), tm=1024, tn=1024)
    du3 = _mm(du4, p["w_pw"], tb=True, name=n("d_u3"), tn=1024)
    dz, g_ln_g, g_ln_b, g_dw_b, g_glu_b, g_dw_w = _conv_bwd(
        du3, sv["u1"], z, dz, p["glu_b"], p["dw_w"], p["conv_ln_g"], p["conv_ln_b"], name=n("conv_bwd"))
    dqf, dkf, dvf = _flash_bwd(sv["qf"], sv["kf"], sv["vf"], do, sv["lse"], delta.reshape(sv["lse"].shape),
                               name=n("flash_bwd"))
    dz, g_w_q_up, g_w_kv_up, g_qn, g_kn, g_ql, g_kvl = _mla_bwd(
        dqf, dkf, dvf, sv["q_raw"], sv["kv"], z, dz, sv["qn"], sv["kn"], p["w_q_up"], p["w_kv_up"],
        p["q_lat_g"], p["kv_lat_g"], c_t, s1_t, s2_t, *p["qk_tiles"], name=n("mla_bwd"))
    big = dict(w_q_up=g_w_q_up, w_kv_up=g_w_kv_up, w_pw=g_w_pw, w_out=g_w_out)
    after = None if hook_rest is None else hook_rest(big)
    g_w_in = _mm(dz, sv["h"], ta=True, name=n("g_w_in"), tm=512, tn=1024, after=after)
    big["w_in"] = g_w_in
    after = None if hook_w_in is None else hook_w_in(g_w_in)
    dh = _mm(dz, p["w_in"], name=n("d_h"), tn=1024, after=after)
    dx, dshift, dscale, g_norm, *down = _prenorm_bwd(dh, sv["x"], gxo, p["norm_g"], p["sc1p"], below,
                                                     name=n("prenorm_bwd"))
    small = dict(dmod=jnp.concatenate([dshift, dscale, dgate], axis=1), norm_g=g_norm, q_lat_g=g_ql, kv_lat_g=g_kvl,
                 q_norm_g=g_qn, k_norm_g=g_kn, glu_b=g_glu_b, dw_w=g_dw_w, dw_b=g_dw_b,
                 conv_ln_g=g_ln_g, conv_ln_b=g_ln_b, b_pw=g_b_pw)
    return (dx, *down), big, small


def _layer_params(l, full, mod_l, small):
    d = D_MODEL
    row = lambda a: a.reshape(1, -1)
    shift, scale, gate = mod_l[:, :d], mod_l[:, d:2 * d], mod_l[:, 2 * d:]
    dw_w = jnp.pad(full["dw_w"][l], ((0, HALO - CONV_K), (0, 0)))
    return dict(
        shift=shift, sc1p=1.0 + scale, gate=gate, norm_g=row(small["norm_g"][l]),
        **{k: full[k][l] for k in _BIG if k in full}, dw_w=dw_w,
        q_lat_g=row(small["q_lat_g"][l]), kv_lat_g=row(small["kv_lat_g"][l]),
        qk_tiles=_norm_tiles(small["q_norm_g"][l]) + _norm_tiles(small["k_norm_g"][l]),
        glu_b=row(small["glu_b"][l]), dw_b=row(small["dw_b"][l]), conv_ln_g=row(small["conv_ln_g"][l]),
        conv_ln_b=row(small["conv_ln_b"][l]), b_pw=row(small["b_pw"][l]))


def kernel(x, c, positions, ada_w, ada_b, norm_g, w_in, q_lat_g, w_q_up, kv_lat_g, w_kv_up, q_norm_g, k_norm_g, glu_b, dw_w, dw_b, conv_ln_g, conv_ln_b, w_pw, b_pw, w_out, loss_target, m_ada_w, m_ada_b, m_norm_g, m_w_in, m_q_lat_g, m_w_q_up, m_kv_lat_g, m_w_kv_up, m_q_norm_g, m_k_norm_g, m_glu_b, m_dw_w, m_dw_b, m_conv_ln_g, m_conv_ln_b, m_w_pw, m_b_pw, m_w_out, v_ada_w, v_ada_b, v_norm_g, v_w_in, v_q_lat_g, v_w_q_up, v_kv_lat_g, v_w_kv_up, v_q_norm_g, v_k_norm_g, v_glu_b, v_dw_w, v_dw_b, v_conv_ln_g, v_conv_ln_b, v_w_pw, v_b_pw, v_w_out):
    names = ("ada_w", "ada_b", "norm_g", "w_in", "q_lat_g", "w_q_up", "kv_lat_g", "w_kv_up", "q_norm_g",
             "k_norm_g", "glu_b", "dw_w", "dw_b", "conv_ln_g", "conv_ln_b", "w_pw", "b_pw", "w_out")
    w_loc = dict(zip(names, (ada_w, ada_b, norm_g, w_in, q_lat_g, w_q_up, kv_lat_g, w_kv_up, q_norm_g, k_norm_g,
                             glu_b, dw_w, dw_b, conv_ln_g, conv_ln_b, w_pw, b_pw, w_out)))
    m_loc = dict(zip(names, (m_ada_w, m_ada_b, m_norm_g, m_w_in, m_q_lat_g, m_w_q_up, m_kv_lat_g, m_w_kv_up,
                             m_q_norm_g, m_k_norm_g, m_glu_b, m_dw_w, m_dw_b, m_conv_ln_g, m_conv_ln_b, m_w_pw,
                             m_b_pw, m_w_out)))
    v_loc = dict(zip(names, (v_ada_w, v_ada_b, v_norm_g, v_w_in, v_q_lat_g, v_w_q_up, v_kv_lat_g, v_w_kv_up,
                             v_q_norm_g, v_k_norm_g, v_glu_b, v_dw_w, v_dw_b, v_conv_ln_g, v_conv_ln_b, v_w_pw,
                             v_b_pw, v_w_out)))
    nl, d = N_LAYERS, D_MODEL
    me = 4 * lax.axis_index("x") + 2 * lax.axis_index("y") + lax.axis_index("c")
    x2, tgt = x[0], loss_target[0]
    ada_cols = ada_w.shape[-1]

    tr = lambda a: jnp.swapaxes(a, 1, 2)
    w_loc, m_loc, v_loc = ({**dd, "w_in": tr(dd["w_in"])} for dd in (w_loc, m_loc, v_loc))
    w_in0 = [w_loc["w_in"][0].astype(WIRE_DTYPE)]
    fly_c = _exchange_start(w_in0, _own_slots(w_in0, False, name="own_w_in_l0"), "chips", name="gather_start_w_in_l0")
    held = dict(c=c, positions=positions, ada_b=ada_b, norm_g=norm_g, q_lat_g=q_lat_g, kv_lat_g=kv_lat_g,
                q_norm_g=q_norm_g, k_norm_g=k_norm_g, glu_b=glu_b, dw_w=dw_w, dw_b=dw_b, conv_ln_g=conv_ln_g,
                conv_ln_b=conv_ln_b, b_pw=b_pw, big={k: w_loc[k] for k in _BIG})
    tok_c, held = lax.optimization_barrier((fly_c[4], held))
    c, positions, ada_b, norm_g, q_lat_g, kv_lat_g, q_norm_g, k_norm_g, glu_b, dw_w, dw_b, conv_ln_g, conv_ln_b, b_pw = (
        held[k] for k in ("c", "positions", "ada_b", "norm_g", "q_lat_g", "kv_lat_g", "q_norm_g", "k_norm_g", "glu_b",
                          "dw_w", "dw_b", "conv_ln_g", "conv_ln_b", "b_pw"))
    wire = {k: held["big"][k].astype(WIRE_DTYPE) for k in _BIG}

    dw_pad = jnp.pad(dw_w, ((0, 0), (0, HALO - CONV_K), (0, 0)))
    c_rows = c.reshape(d // LANE, LANE) + tok_c[0:1, :]
    c_all, dw_all = _all_gather([c_rows, dw_pad], name="gather_c")
    c_all = c_all.reshape(N_DEV, d)
    ada_b_cols = lax.dynamic_slice_in_dim(ada_b, me * ada_cols, ada_cols, axis=1).reshape(nl, 1, ada_cols)
    mod_cols = _ada_fwd(c_all, ada_w, ada_b_cols, name="ada_fwd")
    mod_all = _all_gather([mod_cols], name="gather_mod")[0]
    mod_me = lax.dynamic_index_in_dim(mod_all, me, axis=2, keepdims=False)
    mod = mod_me.transpose(1, 0, 2).reshape(nl, 1, N_DEV * ada_cols)

    from_chips = _exchange_wait(*fly_c[:4], mod, "chips", name="gather_wait_w_in_l0")
    fly_f = _exchange_start([], from_chips, "forward", name="forward_start_w_in_l0")
    w_in_all0 = _exchange_wait(*fly_f[:4], fly_f[4], "forward", name="forward_wait_w_in_l0")[0]
    rest0 = [wire[k][0] for k in _BIG[1:]]
    fly_r0, fly_w1 = {}, {}
    fly_r0["x"] = _exchange_start(rest0, _own_slots(rest0, False, name="own_weights_l0_rest", after=w_in_all0),
                                  "gather", name="gather_start_l0_rest")

    def layout_rest(parts):
        return dict(w_q_up=_qup_permute(_shards_to_cols(parts[0])), w_kv_up=_shards_to_cols(parts[1]),
                    w_pw=parts[2].reshape(D_CONV, D_CONV), w_out=parts[3].reshape(D_MLA + D_CONV, d))

    small_in = dict(norm_g=norm_g, q_lat_g=q_lat_g, kv_lat_g=kv_lat_g, q_norm_g=q_norm_g, k_norm_g=k_norm_g,
                    glu_b=glu_b, dw_b=dw_b, conv_ln_g=conv_ln_g, conv_ln_b=conv_ln_b, b_pw=b_pw)
    dw_full = [_shards_to_cols(dw_all[:, l])[:CONV_K] for l in range(nl)]
    rope = _rope_tiles(positions[0])

    def layer_params(l, w_in_all, rest, mod_l):
        full = dict(dw_w=dw_full)
        if w_in_all is not None:
            full["w_in"] = {l: _win_assemble(w_in_all, name=f"w_in_assemble_l{l}")}
        if rest is not None:
            full.update({k: {l: a} for k, a in layout_rest(rest).items()})
        return _layer_params(l, full, mod_l, small_in)

    src1 = [wire[k][1] for k in _BIG]
    fly_w1["x"] = _exchange_start(src1, _own_slots(src1, False, name="own_weights_l1", after=fly_r0["x"][4]), "gather",
                                  name="gather_start_l1")

    def late_l0(z):
        return layout_rest(_exchange_wait(*fly_r0["x"][:4], z, "gather", name="gather_wait_l0_rest"))

    params, saved = [None] * nl, [None] * nl
    p0 = layer_params(0, w_in_all0, None, mod[0] + fly_w1["x"][4][0, 0])
    xs, saved[0], params[0] = _layer_fwd(x2, p0, rope, 0, late=late_l0)
    parts1 = _exchange_wait(*fly_w1["x"][:4], xs, "gather", name="gather_wait_l1")
    params[1] = layer_params(1, parts1[0], parts1[1:], mod[1])
    xs, saved[1], _ = _layer_fwd(xs, params[1], rope, 1)
    gx, loss_part, dy, dgate = _loss_head(xs, tgt, saved[1]["y"], params[1]["gate"], name="loss_head")
    loss = lax.psum(loss_part[0, 0], ("x", "y", "c"))

    def shard_major(k, g):
        if k == "w_q_up":
            g = _qup_unpermute(g)
        if k in _COL_SHARDED:
            return _cols_to_shards(g)
        return g.reshape((N_DEV, g.shape[0] // N_DEV, g.shape[1]))

    def scatter_start(send, tag):
        lands = _own_slots(send, True, name=f"own_grads_{tag}")
        return _exchange_start(send, lands, "scatter", name=f"scatter_start_{tag}")

    def wire_rest(big):
        return [shard_major(k, big[k]).astype(WIRE_DTYPE) for k in _BIG[1:]]

    big_g, small_g, flying = [None] * nl, [None] * nl, {}
    (gx, dy, dgate), big_g[1], small_g[1] = _layer_bwd(gx, dy, dgate, params[1], saved[1], rope, 1,
                                                       below=(saved[0]["y"], params[0]["gate"]))
    flying["l1"] = scatter_start([_win_split(big_g[1]["w_in"], name="w_in_split_l1")] + wire_rest(big_g[1]), "l1")
    p0 = dict(params[0], after_start=flying["l1"][4], b_pw=params[0]["b_pw"] + flying["l1"][4][0, 0])

    def start_rest_l0(big):
        flying["l0_rest"] = scatter_start(wire_rest(big), "l0_rest")
        return flying["l0_rest"][4]

    res, arrived = {}, [None] * nl

    def start_w_in_l0(g_w_in):
        flying["l0_w_in"] = scatter_start([_win_split(g_w_in, name="w_in_split_l0")], "l0_w_in")
        tok = flying["l0_w_in"][4]
        arrived[1] = _exchange_wait(*flying["l1"][:4], tok, "scatter", name="scatter_wait_l1")
        arrived[0] = [None] + _exchange_wait(*flying["l0_rest"][:4], tok, "scatter", name="scatter_wait_l0_rest")
        for i, k in enumerate(_BIG):
            if i > 0:
                res[k] = _adamw([arrived[l][i] for l in range(nl)], w_loc[k], m_loc[k], v_loc[k], name=f"adamw_{k}")
        res["w_in_l1"] = _adamw_layer(arrived[1][0], w_loc["w_in"], m_loc["w_in"], v_loc["w_in"], 1, None,
                                      name="adamw_w_in_l1")
        return res["w_in_l1"][0]

    (gx,), big_g[0], small_g[0] = _layer_bwd(gx, dy, dgate, p0, saved[0], rope, 0, hook_rest=start_rest_l0,
                                             hook_w_in=start_w_in_l0)

    tile = 8 * LANE
    padded = [(k, nn, -(-nn // tile) * tile) for k, nn in _SMALL]
    spk = jnp.concatenate([jnp.pad(small_g[l][k].reshape(-1), (0, np_ - nn)).reshape(-1, LANE)
                           for l in range(nl) for k, nn, np_ in padded], axis=0)
    s_all = _all_gather([spk], name="gather_small_grads")[0]
    s_rows = sum(np_ for _, _, np_ in padded) // LANE
    s_all = s_all.reshape(N_DEV, nl, s_rows, LANE)
    s_parts = {k: a[..., :nn] for (k, nn, _), a in
               zip(padded, _unpack_rows(s_all, [(np_,) for _, _, np_ in padded]))}

    dmod_all = s_parts["dmod"]
    dmod_cols = lax.dynamic_slice_in_dim(dmod_all, me * ada_cols, ada_cols, axis=2).transpose(1, 0, 2)
    g_ada_w = _ada_bwd(c_all.T, dmod_cols, name="ada_bwd")
    gp = {}
    gp["ada_w"] = g_ada_w[None]
    gp["ada_b"] = dmod_all
    for k in ("norm_g", "q_lat_g", "kv_lat_g", "glu_b", "dw_b", "conv_ln_g", "conv_ln_b", "b_pw"):
        gp[k] = s_parts[k]
    for k in ("q_norm_g", "k_norm_g"):
        t = s_parts[k]
        gp[k] = jnp.concatenate([t[..., :NOPE], t[..., LANE:LANE + ROPE]], axis=-1)
    dw_g = s_parts["dw_w"].reshape(N_DEV, nl, HALO, D_CONV)[:, :, :CONV_K]
    gp["dw_w"] = lax.dynamic_slice_in_dim(dw_g, me * LANE, LANE, axis=3)

    res["ada_w"] = _adamw(gp["ada_w"], w_loc["ada_w"], m_loc["ada_w"], v_loc["ada_w"], name="adamw_ada_w")
    small_names = [k for k in names if k not in _BIG and k != "ada_w"]
    res.update(zip(small_names, _adamw_small([(gp[k], w_loc[k], m_loc[k], v_loc[k]) for k in small_names],
                                             name="adamw_small")))
    arrived[0][0] = _exchange_wait(*flying["l0_w_in"][:4], res["ada_w"][1], "scatter", name="scatter_wait_l0_w_in")[0]
    w_in_res = _adamw_layer(arrived[0][0], w_loc["w_in"], m_loc["w_in"], v_loc["w_in"], 0, res.pop("w_in_l1"),
                            name="adamw_w_in_l0")
    res["w_in"] = tuple(tr(a) for a in w_in_res)
    out = [loss, gx[None]]
    for idx in range(4):
        out += [res[k][idx] for k in names]
    return tuple(out)
```

```python
import functools
import math

import jax
import jax.numpy as jnp
from jax import lax
from jax.experimental import pallas as pl
from jax.experimental.pallas import tpu as pltpu

F32 = jnp.float32
MXU_DTYPE = jnp.bfloat16
WIRE_DTYPE = jnp.bfloat16

D_MODEL = 2048
N_LAYERS = 2
N_DEV = 8
N_HEADS = 8
NOPE = 128
ROPE = 64
V_DIM = 128
QK_DIM = NOPE + ROPE
Q_LORA = 512
KV_LORA = 256
D_MLA = N_HEADS * V_DIM
D_CONV = 1024
CONV_K = 31
ROPE_THETA = 10000.0
EPS = 1e-6
LANE = 128
HEAD_PAD = 2 * LANE
HALO = 32

SEG_CI = (0, 2 * D_CONV)
SEG_MG = (2 * D_CONV, D_MLA)
SEG_CG = (2 * D_CONV + D_MLA, D_CONV)
SEG_QL = (2 * D_CONV + D_MLA + D_CONV, Q_LORA)
SEG_KVL = (SEG_QL[0] + Q_LORA, KV_LORA)
SEG_KR = (SEG_KVL[0] + KV_LORA, LANE)
SEG_LAT = (SEG_QL[0], 1024)
IN_PAD = SEG_LAT[0] + SEG_LAT[1]
IN_TILE = IN_PAD // 4
assert SEG_KR[0] + LANE <= IN_PAD and SEG_LAT[0] % SEG_LAT[1] == 0
IN_COLS = Q_LORA + KV_LORA + ROPE + D_MLA + 2 * D_CONV + D_CONV

ADAM_LR = 0.001
ADAM_B1 = 0.9
ADAM_B2 = 0.999
ADAM_EPS = 1e-08
ADAM_WD = 0.01
ADAM_STEP = 10

VMEM_LIMIT = 56 * 1024 * 1024
ATT_T = 512
ROW_T = 256
CONV_T = 256
MESH_ID = pl.DeviceIdType.MESH


def _cp(sem=None):
    kw = dict(vmem_limit_bytes=VMEM_LIMIT)
    if sem is not None:
        kw["dimension_semantics"] = sem
    return pltpu.CompilerParams(**kw)


def _sds(shape, dtype):
    return jax.ShapeDtypeStruct(shape, dtype)


def _silu(x):
    return x * jax.nn.sigmoid(x)


def _dsilu(x):
    s = jax.nn.sigmoid(x)
    return s * (1.0 + x * (1.0 - s))


def _rowspec(t, width, col=0):
    return pl.BlockSpec((t, width), lambda i: (i, col))


def _vecspec(width):
    return pl.BlockSpec((1, width), lambda i: (0, 0))


def _colsum(v):
    return jnp.sum(v, axis=0, keepdims=True)


def _mm(a, b, *, name, ta=False, tb=False, out_dtype=F32, tm=512, tn=512, tk=None, n_outer=False, after=None,
        residual=None):
    if ta:
        kdim, m = a.shape
    else:
        m, kdim = a.shape
    if tb:
        n, k2 = b.shape
    else:
        k2, n = b.shape
    assert kdim == k2, (a.shape, b.shape)
    tm, tn = min(tm, m), min(tn, n)
    tk = kdim if tk is None else min(tk, kdim)
    assert m % tm == 0 and n % tn == 0 and kdim % tk == 0, (m, n, kdim, tm, tn, tk)
    nk = kdim // tk
    dims = (((0 if ta else 1,), (1 if tb else 0,)), ((), ()))

    n_extra = 0 if after is None else 1
    assert residual is None or nk == 1

    def body(a_ref, b_ref, *rest):
        if residual is not None:
            x_ref, gate_ref = rest[:2]
            rest = rest[2:]
        o_ref, scratch = rest[n_extra], rest[n_extra + 1:]
        prod = lax.dot_general(a_ref[...].astype(MXU_DTYPE), b_ref[...].astype(MXU_DTYPE), dims,
                               preferred_element_type=F32)
        if residual is not None:
            o_ref[...] = prod.astype(o_ref.dtype)
            scratch[0][...] = x_ref[...] + gate_ref[...] * prod
        elif nk == 1:
            o_ref[...] = prod.astype(o_ref.dtype)
        else:
            acc = scratch[0]
            k = pl.program_id(2)

            @pl.when(k == 0)
            def _():
                acc[...] = prod

            @pl.when(k > 0)
            def _():
                acc[...] += prod

            @pl.when(k == nk - 1)
            def _():
                o_ref[...] = acc[...].astype(o_ref.dtype)

    if n_outer:
        ij = lambda g0, g1: (g1, g0)
        grid = (n // tn, m // tm, nk)
    else:
        ij = lambda g0, g1: (g0, g1)
        grid = (m // tm, n // tn, nk)

    def a_map(g0, g1, k):
        i, _ = ij(g0, g1)
        return (k, i) if ta else (i, k)

    def b_map(g0, g1, k):
        _, j = ij(g0, g1)
        return (j, k) if tb else (k, j)

    def o_map(g0, g1, k):
        return ij(g0, g1)

    in_specs = [pl.BlockSpec((tk, tm) if ta else (tm, tk), a_map), pl.BlockSpec((tn, tk) if tb else (tk, tn), b_map)]
    operands = [a, b]
    out_specs, out_shape = pl.BlockSpec((tm, tn), o_map), _sds((m, n), out_dtype)
    if residual is not None:
        in_specs += [pl.BlockSpec((tm, tn), o_map), pl.BlockSpec((1, tn), lambda g0, g1, k: (0, ij(g0, g1)[1]))]
        operands += list(residual)
        out_specs, out_shape = [out_specs, pl.BlockSpec((tm, tn), o_map)], [out_shape, _sds((m, n), F32)]
    if after is not None:
        in_specs.append(_ANY)
        operands.append(after)
    return pl.pallas_call(
        body, name=name, grid=grid, in_specs=in_specs, out_specs=out_specs, out_shape=out_shape,
        scratch_shapes=[pltpu.VMEM((tm, tn), F32)] if nk > 1 else [],
        compiler_params=_cp(("parallel", "parallel", "arbitrary")),
    )(*operands)


def _prenorm(x, g, shift, sc1p, *, name):
    s, d = x.shape
    t = min(2 * ROW_T, s)

    def body(x_ref, g_ref, sh_ref, sc_ref, h_ref):
        xv = x_ref[...]
        r = lax.rsqrt(jnp.mean(xv * xv, axis=-1, keepdims=True) + EPS)
        h_ref[...] = ((xv * r) * g_ref[...] * sc_ref[...] + sh_ref[...]).astype(h_ref.dtype)

    return pl.pallas_call(
        body, name=name, grid=(s // t,),
        in_specs=[_rowspec(t, d), _vecspec(d), _vecspec(d), _vecspec(d)],
        out_specs=_rowspec(t, d), out_shape=_sds((s, d), MXU_DTYPE),
        compiler_params=_cp(("parallel",)),
    )(x, g, shift, sc1p)


def _rope_fwd(r, c_t, s1_t, s2_t):
    return r * c_t + pltpu.roll(r, LANE - ROPE // 2, 1) * s1_t + pltpu.roll(r, ROPE // 2, 1) * s2_t


def _rope_bwd(d, c_t, s1_t, s2_t):
    return d * c_t + pltpu.roll(d * s1_t, ROPE // 2, 1) + pltpu.roll(d * s2_t, LANE - ROPE // 2, 1)


def _lanesum(v):
    return jnp.sum(v, axis=-1, keepdims=True)


def _mla_pre(z, w_q_up, w_kv_up, g_ql, g_kvl, c_t, s1_t, s2_t, gqn, gqr, gkn, gkr, *, name):
    s = z.shape[0]
    t = min(2 * ROW_T, s)
    scale = LOG2E / math.sqrt(QK_DIM)
    wide = 2 * N_HEADS * LANE

    def body(ql_ref, kvl_ref, kr_ref, wq_ref, wkv_ref, gq_ref, gk_ref, c_ref, s1_ref, s2_ref,
             gqn_ref, gqr_ref, gkn_ref, gkr_ref, qn_ref, kn_ref, q_ref, kv_ref, qf_ref, kf_ref, vf_ref):
        for src, g_ref, dst, w_ref, up in ((ql_ref, gq_ref, qn_ref, wq_ref, q_ref),
                                           (kvl_ref, gk_ref, kn_ref, wkv_ref, kv_ref)):
            v = src[...]
            r = lax.rsqrt(jnp.mean(v * v, axis=-1, keepdims=True) + EPS)
            dst[...] = ((v * r) * g_ref[...]).astype(dst.dtype)
            up[...] = jnp.dot(dst[...], w_ref[...], preferred_element_type=F32)
        c_v, s1_v, s2_v = c_ref[...], s1_ref[...], s2_ref[...]
        kr = kr_ref[...]
        kr_ss = _lanesum(kr * kr)
        for h in range(N_HEADS):
            n = q_ref[:, h * LANE:(h + 1) * LANE]
            r = q_ref[:, N_HEADS * LANE + h * LANE:N_HEADS * LANE + (h + 1) * LANE]
            rs = lax.rsqrt((_lanesum(n * n) + _lanesum(r * r)) * (1.0 / QK_DIM) + EPS)
            qf_ref[h, :, 0:LANE] = (((n * rs) * gqn_ref[...]) * scale).astype(qf_ref.dtype)
            rr = _rope_fwd((r * rs) * gqr_ref[...], c_v, s1_v, s2_v)
            qf_ref[h, :, LANE:HEAD_PAD] = (rr * scale).astype(qf_ref.dtype)

            n = kv_ref[:, h * 2 * LANE:h * 2 * LANE + LANE]
            rs = lax.rsqrt((_lanesum(n * n) + kr_ss) * (1.0 / QK_DIM) + EPS)
            kf_ref[h, :, 0:LANE] = ((n * rs) * gkn_ref[...]).astype(kf_ref.dtype)
            kf_ref[h, :, LANE:HEAD_PAD] = _rope_fwd((kr * rs) * gkr_ref[...], c_v, s1_v, s2_v).astype(kf_ref.dtype)
            vf_ref[h, :, 0:V_DIM] = kv_ref[:, h * 2 * LANE + LANE:(h + 1) * 2 * LANE].astype(vf_ref.dtype)
            vf_ref[h, :, V_DIM:] = jnp.ones((t, V_DIM), vf_ref.dtype)

    hspec = lambda w: pl.BlockSpec((N_HEADS, t, w), lambda i: (0, i, 0))
    whole = lambda a: pl.BlockSpec(a.shape, lambda i: (0, 0))
    return pl.pallas_call(
        body, name=name, grid=(s // t,),
        in_specs=[_rowspec(t, Q_LORA, SEG_QL[0] // Q_LORA), _rowspec(t, KV_LORA, SEG_KVL[0] // KV_LORA),
                  _rowspec(t, LANE, SEG_KR[0] // LANE), whole(w_q_up), whole(w_kv_up),
                  _vecspec(Q_LORA), _vecspec(KV_LORA),
                  _rowspec(t, LANE), _rowspec(t, LANE), _rowspec(t, LANE),
                  _vecspec(LANE), _vecspec(LANE), _vecspec(LANE), _vecspec(LANE)],
        out_specs=[_rowspec(t, Q_LORA), _rowspec(t, KV_LORA), _rowspec(t, wide), _rowspec(t, wide),
                   hspec(HEAD_PAD), hspec(HEAD_PAD), hspec(2 * V_DIM)],
        out_shape=[_sds((s, Q_LORA), MXU_DTYPE), _sds((s, KV_LORA), MXU_DTYPE), _sds((s, wide), F32),
                   _sds((s, wide), F32), _sds((N_HEADS, s, HEAD_PAD), MXU_DTYPE),
                   _sds((N_HEADS, s, HEAD_PAD), MXU_DTYPE), _sds((N_HEADS, s, 2 * V_DIM), MXU_DTYPE)],
        compiler_params=_cp(("parallel",)),
    )(z, z, z, w_q_up, w_kv_up, g_ql, g_kvl, c_t, s1_t, s2_t, gqn, gqr, gkn, gkr)


def _causal_mask(t):
    row = lax.broadcasted_iota(jnp.int32, (t, t), 0)
    col = lax.broadcasted_iota(jnp.int32, (t, t), 1)
    return col <= row


NEG = -1e30
LOG2E = math.log2(math.e)


def _flash_fwd(qf, kf, va, *, name):
    nh, s, dk = qf.shape
    dv = va.shape[-1] // 2
    t = min(ATT_T, s)
    n = s // t
    assert dv == LANE and t % LANE == 0

    def body(q_ref, k_ref, v_ref, o_ref, lse_ref, m_s, acc_s, s_buf):
        i = pl.program_id(1)
        m_s[...] = jnp.full(m_s.shape, NEG, F32)
        acc_s[...] = jnp.zeros(acc_s.shape, F32)

        def rows_of(j):
            return pl.ds(pl.multiple_of(j * t, t), t)

        def scores(qi, j):
            return lax.dot_general(q_ref[0, rows_of(qi), :], k_ref[0, rows_of(j), :], (((1,), (1,)), ((), ())),
                                   preferred_element_type=F32)

        def consume(j, slot, masked):
            sc = s_buf[slot]
            if masked:
                sc = jnp.where(_causal_mask(t), sc, NEG)
            m_prev = m_s[...]
            m_new = jnp.maximum(m_prev, jnp.max(sc, axis=-1, keepdims=True))
            alpha = jnp.exp2(m_prev - m_new)
            p = jnp.exp2(sc - jnp.tile(m_new, (1, t // LANE)))
            acc_s[...] = jnp.tile(alpha, (1, 2)) * acc_s[...] + jnp.dot(
                p.astype(MXU_DTYPE), v_ref[0, rows_of(j), :], preferred_element_type=F32)
            m_s[...] = m_new

        nxt = jnp.minimum(i + 1, n - 1)

        @pl.when(i == 0)
        def _():
            s_buf[2] = scores(0, 0)
            consume(0, 2, True)
            s_buf[2] = scores(nxt, 0)

        @pl.when(i > 0)
        def _():
            s_buf[1] = scores(i, 1)
            consume(0, 2, False)

            def pair(a, carry):
                s_buf[0] = scores(i, 2 * a + 2)
                consume(2 * a + 1, 1, False)
                s_buf[1] = scores(i, 2 * a + 3)
                consume(2 * a + 2, 0, False)
                return carry

            lax.fori_loop(0, (i - 1) // 2, pair, 0)

            @pl.when(i % 2 == 1)
            def _():
                s_buf[2] = scores(nxt, 0)
                consume(i, 1, True)

            @pl.when(i % 2 == 0)
            def _():
                s_buf[0] = scores(i, i)
                consume(i - 1, 1, False)
                s_buf[2] = scores(nxt, 0)
                consume(i, 0, True)

        den = acc_s[:, dv:]
        o_ref[...] = acc_s[:, :dv] / den
        lse_ref[0, 0] = jnp.transpose(m_s[...] + jnp.log2(den))[0:1, :]

    head = lambda h, i: (h, 0, 0)
    return pl.pallas_call(
        body, name=name, grid=(nh, n),
        in_specs=[pl.BlockSpec((1, s, dk), head), pl.BlockSpec((1, s, dk), head), pl.BlockSpec((1, s, 2 * dv), head)],
        out_specs=[pl.BlockSpec((t, dv), lambda h, i: (i, h)),
                   pl.BlockSpec((1, 1, 1, t), lambda h, i: (h, i, 0, 0))],
        out_shape=[_sds((s, nh * dv), F32), _sds((nh, n, 1, t), F32)],
        scratch_shapes=[pltpu.VMEM((t, LANE), F32), pltpu.VMEM((t, 2 * dv), F32), pltpu.VMEM((3, t, t), F32)],
        compiler_params=_cp(("arbitrary", "arbitrary")),
    )(qf, kf, va)


def _shifted_copies(ext_ref):
    rows = ext_ref.shape[1] - 8
    for s in range(1, 8):
        ext_ref[s, 0:rows, :] = ext_ref[0, s:s + rows, :]


def _windows(ext_ref, offsets, t_rows, lane0, lanes):
    for s in range(8):
        group = [o for o in offsets if o % 8 == s]
        if not group:
            continue
        lo, hi = min(group) - s, max(group) - s
        wide = ext_ref[s, pl.ds(lo, hi - lo + t_rows), lane0:lane0 + lanes]
        for o in group:
            yield o, wide[o - s - lo:o - s - lo + t_rows]


def _dw_taps(ext_ref, w_ref, row0, t_rows, lane0, lanes, first_off):
    acc = None
    for off, win in _windows(ext_ref, [row0 + first_off + k for k in range(CONV_K)], t_rows, lane0, lanes):
        k = off - row0 - first_off
        term = w_ref[k:k + 1, lane0:lane0 + lanes] * win
        acc = term if acc is None else acc + term
    return acc


CONV_RC = 32
CONV_LC = 256


def _conv_fwd(z, glu_b, dw_w, dw_b, ln_g, ln_b, w_pw, *, name):
    s = z.shape[0]
    t = min(CONV_T, s)
    c2 = 2 * D_CONV
    hb = t // HALO

    def body(zm_ref, zh_ref, gb_ref, w_ref, wb_ref, g_ref, b_ref, wpw_ref, u1_ref, u3_ref, u4_ref, ext):
        i = pl.program_id(0)

        def glu(zv):
            ci = zv + gb_ref[...]
            return ci[:, :D_CONV] * jax.nn.sigmoid(ci[:, D_CONV:])

        ext[0, HALO:, :] = glu(zm_ref[...])
        ext[0, 0:HALO, :] = jnp.where(i > 0, glu(zh_ref[...]), 0.0)
        _shifted_copies(ext)
        for rc in range(0, t, CONV_RC):
            for lc in range(0, D_CONV, CONV_LC):
                acc = _dw_taps(ext, w_ref, rc, CONV_RC, lc, CONV_LC, HALO - (CONV_K - 1))
                u1_ref[rc:rc + CONV_RC, lc:lc + CONV_LC] = acc + wb_ref[:, lc:lc + CONV_LC]
        u1 = u1_ref[...]
        mu = jnp.mean(u1, axis=-1, keepdims=True)
        cen = u1 - mu
        var = jnp.mean(cen * cen, axis=-1, keepdims=True)
        u2 = (cen * lax.rsqrt(var + EPS)) * g_ref[...] + b_ref[...]
        u3_ref[...] = _silu(u2).astype(u3_ref.dtype)
        u4_ref[...] = jnp.dot(u3_ref[...], wpw_ref[...], preferred_element_type=F32)

    return pl.pallas_call(
        body, name=name, grid=(s // t,),
        in_specs=[_rowspec(t, c2), pl.BlockSpec((HALO, c2), lambda i: (jnp.maximum(i * hb - 1, 0), 0)),
                  _vecspec(c2), pl.BlockSpec((HALO, D_CONV), lambda i: (0, 0)), _vecspec(D_CONV),
                  _vecspec(D_CONV), _vecspec(D_CONV), pl.BlockSpec((D_CONV, D_CONV), lambda i: (0, 0))],
        out_specs=[_rowspec(t, D_CONV), _rowspec(t, D_CONV), _rowspec(t, D_CONV)],
        out_shape=[_sds((s, D_CONV), F32), _sds((s, D_CONV), MXU_DTYPE), _sds((s, D_CONV), F32)],
        scratch_shapes=[pltpu.VMEM((8, t + HALO, D_CONV), F32)],
        compiler_params=_cp(("parallel",)),
    )(z, z, glu_b, dw_w, dw_b, ln_g, ln_b, w_pw)


def _gate_cat(o, z, u4m, b_pw, *, name):
    s = o.shape[0]
    t = min(2 * ROW_T, s)

    def body(o_ref, mg_ref, u4_ref, cg_ref, b_ref, cat_ref):
        cat_ref[:, :D_MLA] = (o_ref[...] * _silu(mg_ref[...])).astype(cat_ref.dtype)
        cat_ref[:, D_MLA:] = ((u4_ref[...] + b_ref[...]) * _silu(cg_ref[...])).astype(cat_ref.dtype)

    return pl.pallas_call(
        body, name=name, grid=(s // t,),
        in_specs=[_rowspec(t, D_MLA), _rowspec(t, D_MLA, SEG_MG[0] // D_MLA), _rowspec(t, D_CONV),
                  _rowspec(t, D_CONV, SEG_CG[0] // D_CONV), _vecspec(D_CONV)],
        out_specs=_rowspec(t, D_MLA + D_CONV), out_shape=_sds((s, D_MLA + D_CONV), MXU_DTYPE),
        compiler_params=_cp(("parallel",)),
    )(o, z, u4m, z, b_pw)


def _gated_residual_bwd(gx, y_ref, gate_ref, dy_ref, dgate_ref):
    dy_ref[...] = (gx * gate_ref[...]).astype(dy_ref.dtype)
    dgate_ref[...] += _colsum(gx * y_ref[...])


def _loss_head(xf, target, y, gate, *, name):
    s, d = xf.shape
    t = min(2 * ROW_T, s)

    def body(x_ref, t_ref, y_ref, gate_ref, gx_ref, loss_ref, dy_ref, dgate_ref):
        @pl.when(pl.program_id(0) == 0)
        def _():
            loss_ref[...] = jnp.zeros(loss_ref.shape, F32)
            dgate_ref[...] = jnp.zeros(dgate_ref.shape, F32)

        err = x_ref[...] - t_ref[...]
        gx = err * (1.0 / d)
        gx_ref[...] = gx
        loss_ref[...] += 0.5 * jnp.sum(_lanesum(err * err) * (1.0 / d), axis=0, keepdims=True)
        _gated_residual_bwd(gx, y_ref, gate_ref, dy_ref, dgate_ref)

    return pl.pallas_call(
        body, name=name, grid=(s // t,),
        in_specs=[_rowspec(t, d), _rowspec(t, d), _rowspec(t, d), _vecspec(d)],
        out_specs=[_rowspec(t, d), pl.BlockSpec((1, 1), lambda i: (0, 0)), _rowspec(t, d), _vecspec(d)],
        out_shape=[_sds((s, d), F32), _sds((1, 1), F32), _sds((s, d), MXU_DTYPE), _sds((1, d), F32)],
        compiler_params=_cp(("arbitrary",)),
    )(xf, target, y, gate)


def _acc_init(refs):
    @pl.when(pl.program_id(0) == 0)
    def _():
        for r in refs:
            r[...] = jnp.zeros(r.shape, r.dtype)


def _gate_bwd(dy, w_out, o, z, u4m, b_pw, *, name):
    s, d = dy.shape
    t = min(2 * ROW_T, s)
    gates = D_MLA + D_CONV
    assert SEG_CG[0] == SEG_MG[0] + D_MLA and SEG_MG[0] % gates == 0

    def body(dy_ref, w_ref, o_ref, mg_ref, u4_ref, cg_ref, b_ref,
             do_ref, delta_ref, du4_ref, gb_ref, dz_ref):
        _acc_init([gb_ref])
        dcat = lax.dot_general(dy_ref[...], w_ref[...], (((1,), (1,)), ((), ())), preferred_element_type=F32)
        dm, ov, mg = dcat[:, :D_MLA], o_ref[...], mg_ref[...]
        do = dm * _silu(mg)
        do_ref[...] = do.astype(do_ref.dtype)
        dz_ref[:, :D_MLA] = (dm * ov * _dsilu(mg)).astype(dz_ref.dtype)
        prod = do * ov
        for h in range(N_HEADS):
            rowsum = jnp.broadcast_to(_lanesum(prod[:, h * V_DIM:(h + 1) * V_DIM]), (t, LANE))
            delta_ref[h, 0] = jnp.transpose(rowsum)[0:1, :]
        dc, cg = dcat[:, D_MLA:], cg_ref[...]
        du4 = dc * _silu(cg)
        du4_ref[...] = du4.astype(du4_ref.dtype)
        dz_ref[:, D_MLA:] = (dc * (u4_ref[...] + b_ref[...]) * _dsilu(cg)).astype(dz_ref.dtype)
        gb_ref[...] += _colsum(du4)

    return pl.pallas_call(
        body, name=name, grid=(s // t,),
        in_specs=[_rowspec(t, d), pl.BlockSpec((gates, d), lambda i: (0, 0)), _rowspec(t, D_MLA),
                  _rowspec(t, D_MLA, SEG_MG[0] // D_MLA), _rowspec(t, D_CONV),
                  _rowspec(t, D_CONV, SEG_CG[0] // D_CONV), _vecspec(D_CONV)],
        out_specs=[_rowspec(t, D_MLA), pl.BlockSpec((N_HEADS, 1, 1, t), lambda i: (0, i, 0, 0)),
                   _rowspec(t, D_CONV), _vecspec(D_CONV), _rowspec(t, gates, SEG_MG[0] // gates)],
        out_shape=[_sds((s, D_MLA), MXU_DTYPE), _sds((N_HEADS, s // t, 1, t), F32),
                   _sds((s, D_CONV), MXU_DTYPE), _sds((1, D_CONV), F32), _sds((s, IN_PAD), MXU_DTYPE)],
        compiler_params=_cp(("arbitrary",)),
    )(dy, w_out, o, z, u4m, z, b_pw)


def _conv_bwd(du3, u1, z, dz, glu_b, dw_w, ln_g, ln_b, *, name):
    s = z.shape[0]
    t = min(CONV_T, s)
    c2 = 2 * D_CONV
    hb = t // HALO
    n_blk = s // t
    last_halo = s // HALO - 1

    def body(d3m_ref, d3h_ref, u1m_ref, u1h_ref, zm_ref, zh_ref, gb_ref, w_ref, g_ref, b_ref, dz_in_ref,
             dci_ref, gg_ref, gbn_ref, gwb_ref, ggb_ref, gw_ref, dext, uext, du0_s, gw_acc):
        i = pl.program_id(0)
        _acc_init([gg_ref, gbn_ref, gwb_ref, ggb_ref, gw_acc])

        def ln_bwd(d3, u1v):
            mu = jnp.mean(u1v, axis=-1, keepdims=True)
            cen = u1v - mu
            rstd = lax.rsqrt(jnp.mean(cen * cen, axis=-1, keepdims=True) + EPS)
            uh = cen * rstd
            d2 = d3 * _dsilu(uh * g_ref[...] + b_ref[...])
            dh = d2 * g_ref[...]
            d1 = rstd * (dh - jnp.mean(dh, axis=-1, keepdims=True) - uh * jnp.mean(dh * uh, axis=-1, keepdims=True))
            return d1, d2, uh

        d1, d2, uh = ln_bwd(d3m_ref[...], u1m_ref[...])
        gg_ref[...] += _colsum(d2 * uh)
        gbn_ref[...] += _colsum(d2)
        gwb_ref[...] += _colsum(d1)
        dext[0, 0:t, :] = d1
        d1h, _, _ = ln_bwd(d3h_ref[...], u1h_ref[...])
        dext[0, t:, :] = jnp.where(i < n_blk - 1, d1h, 0.0)
        _shifted_copies(dext)

        def glu_parts(zv):
            ci = zv + gb_ref[...]
            return ci[:, :D_CONV], jax.nn.sigmoid(ci[:, D_CONV:])

        val, sg = glu_parts(zm_ref[...])
        uext[0, HALO:, :] = val * sg
        valh, sgh = glu_parts(zh_ref[...])
        uext[0, 0:HALO, :] = jnp.where(i > 0, valh * sgh, 0.0)
        _shifted_copies(uext)

        for rc in range(0, t, CONV_RC):
            for lc in range(0, D_CONV, CONV_LC):
                acc = None
                for off, win in _windows(dext, [rc + k for k in range(CONV_K)], CONV_RC, lc, CONV_LC):
                    k = (CONV_K - 1) - (off - rc)
                    term = w_ref[k:k + 1, lc:lc + CONV_LC] * win
                    acc = term if acc is None else acc + term
                du0_s[rc:rc + CONV_RC, lc:lc + CONV_LC] = acc
                dchunk = dext[0, rc:rc + CONV_RC, lc:lc + CONV_LC]
                first = rc + HALO - (CONV_K - 1)
                for off, win in _windows(uext, [first + k for k in range(CONV_K)], CONV_RC, lc, CONV_LC):
                    k = off - first
                    pr = dchunk * win
                    part = pr[0:8]
                    for r8 in range(8, CONV_RC, 8):
                        part = part + pr[r8:r8 + 8]
                    gw_acc[k, :, lc:lc + CONV_LC] += part

        du0 = du0_s[...]
        dval = du0 * sg
        dgt = du0 * val * sg * (1.0 - sg)
        dci_ref[:, :D_CONV] = dval.astype(dci_ref.dtype)
        dci_ref[:, D_CONV:] = dgt.astype(dci_ref.dtype)
        ggb_ref[:, :D_CONV] += _colsum(dval)
        ggb_ref[:, D_CONV:] += _colsum(dgt)

        @pl.when(i == n_blk - 1)
        def _():
            gw_ref[...] = jnp.sum(gw_acc[...], axis=1)

    halo_next = lambda w: pl.BlockSpec((HALO, w), lambda i: (jnp.minimum((i + 1) * hb, last_halo), 0))
    return pl.pallas_call(
        body, name=name, grid=(n_blk,),
        in_specs=[_rowspec(t, D_CONV), halo_next(D_CONV), _rowspec(t, D_CONV), halo_next(D_CONV),
                  _rowspec(t, c2), pl.BlockSpec((HALO, c2), lambda i: (jnp.maximum(i * hb - 1, 0), 0)),
                  _vecspec(c2), pl.BlockSpec((HALO, D_CONV), lambda i: (0, 0)), _vecspec(D_CONV), _vecspec(D_CONV),
                  _ANY],
        out_specs=[_rowspec(t, c2, SEG_CI[0] // c2), _vecspec(D_CONV), _vecspec(D_CONV), _vecspec(D_CONV),
                   _vecspec(c2), pl.BlockSpec((HALO, D_CONV), lambda i: (0, 0))],
        out_shape=[_sds(dz.shape, dz.dtype), _sds((1, D_CONV), F32), _sds((1, D_CONV), F32), _sds((1, D_CONV), F32),
                   _sds((1, c2), F32), _sds((HALO, D_CONV), F32)],
        scratch_shapes=[pltpu.VMEM((8, t + HALO, D_CONV), F32), pltpu.VMEM((8, t + HALO, D_CONV), F32),
                        pltpu.VMEM((t, D_CONV), F32), pltpu.VMEM((HALO, 8, D_CONV), F32)],
        input_output_aliases={10: 0},
        compiler_params=_cp(("arbitrary",)),
    )(du3, du3, u1, u1, z, z, glu_b, dw_w, ln_g, ln_b, dz)


def _flash_bwd(qf, kf, va, do, lse_t, delta_t, *, name):
    nh, s, dk = qf.shape
    dv = va.shape[-1] // 2
    t = min(ATT_T, s)
    n = s // t
    nt = (((1,), (1,)), ((), ()))
    tn = (((0,), (0,)), ((), ()))

    def body(q_ref, do_ref, lse_ref, dl_ref, k_ref, v_ref, dq_ref, dk_ref, dv_ref,
             dk_s, dv_s, st_buf, dpt_buf):
        n_un = pl.program_id(1)
        j = n - 1 - n_un
        nxt = jnp.maximum(j - 1, 0)

        @pl.when(n_un == 0)
        def _():
            dq_ref[...] = jnp.zeros(dq_ref.shape, F32)

        dk_s[...] = jnp.zeros(dk_s.shape, F32)
        dv_s[...] = jnp.zeros(dv_s.shape, F32)

        def rows_at(blk):
            return pl.ds(pl.multiple_of(blk * t, t), t)

        def rows_of(b):
            return rows_at(n - 1 - b)

        k = k_ref[0, rows_at(j), :]

        def produce(kj, b, slot):
            rows = rows_of(b)
            st_buf[slot] = lax.dot_general(k_ref[0, rows_at(kj), :], q_ref[0, rows, :], nt,
                                           preferred_element_type=F32)
            dpt_buf[slot] = lax.dot_general(v_ref[0, rows_at(kj), 0:dv], do_ref[rows, :], nt,
                                            preferred_element_type=F32)

        def consume(b, slot, masked):
            i = n - 1 - b
            rows = rows_of(b)
            q, dov = q_ref[0, rows, :], do_ref[rows, :]
            pt = jnp.exp2(st_buf[slot] - lse_ref[0, i])
            if masked:
                key = lax.broadcasted_iota(jnp.int32, (t, t), 0)
                qry = lax.broadcasted_iota(jnp.int32, (t, t), 1)
                pt = jnp.where(key <= qry, pt, 0.0)
            dv_s[...] += jnp.dot(pt.astype(MXU_DTYPE), dov, preferred_element_type=F32)
            dst = (pt * (dpt_buf[slot] - dl_ref[0, i])).astype(MXU_DTYPE)
            dk_s[...] += jnp.dot(dst, q, preferred_element_type=F32)
            dq_ref[0, rows, :] += lax.dot_general(dst, k, tn, preferred_element_type=F32)

        @pl.when(n_un == 0)
        def _():
            produce(j, 0, 2)
            consume(0, 2, True)
            produce(nxt, 0, 2)

        @pl.when(n_un > 0)
        def _():
            produce(j, 1, 1)
            consume(0, 2, False)

            def pair(a, carry):
                produce(j, 2 * a + 2, 0)
                consume(2 * a + 1, 1, False)
                produce(j, 2 * a + 3, 1)
                consume(2 * a + 2, 0, False)
                return carry

            lax.fori_loop(0, (n_un - 1) // 2, pair, 0)

            @pl.when(n_un % 2 == 1)
            def _():
                produce(nxt, 0, 2)
                consume(n_un, 1, True)

            @pl.when(n_un % 2 == 0)
            def _():
                produce(j, n_un, 0)
                consume(n_un - 1, 1, False)
                produce(nxt, 0, 2)
                consume(n_un, 0, True)

        dk_ref[0] = dk_s[...]
        dv_ref[0] = dv_s[...]

    head = lambda h, j: (h, 0, 0)
    rowv = pl.BlockSpec((1, n, 1, t), lambda h, j: (h, 0, 0, 0))
    return pl.pallas_call(
        body, name=name, grid=(nh, n),
        in_specs=[pl.BlockSpec((1, s, dk), head),
                  pl.BlockSpec((s, dv), lambda h, j: (0, h)),
                  rowv, rowv,
                  pl.BlockSpec((1, s, dk), head),
                  pl.BlockSpec((1, s, 2 * dv), head)],
        out_specs=[pl.BlockSpec((1, s, dk), head),
                   pl.BlockSpec((1, t, dk), lambda h, g: (h, n - 1 - g, 0)),
                   pl.BlockSpec((1, t, dv), lambda h, g: (h, n - 1 - g, 0))],
        out_shape=[_sds((nh, s, dk), F32), _sds((nh, s, dk), F32), _sds((nh, s, dv), F32)],
        scratch_shapes=[pltpu.VMEM((t, dk), F32), pltpu.VMEM((t, dv), F32),
                        pltpu.VMEM((3, t, t), F32), pltpu.VMEM((3, t, t), F32)],
        compiler_params=_cp(("arbitrary", "arbitrary")),
    )(qf, do, lse_t, delta_t, kf, va)


def _mla_bwd(dqf, dkf, dvf, q_raw, kv, z, dz, qn, kn, w_q_up, w_kv_up, g_ql, g_kvl, c_t, s1_t, s2_t,
             gqn, gqr, gkn, gkr, *, name):
    s = q_raw.shape[0]
    t = min(ROW_T, s)
    scale = 1.0 / math.sqrt(QK_DIM)
    o_ql, o_kvl, o_kr = (seg[0] - SEG_LAT[0] for seg in (SEG_QL, SEG_KVL, SEG_KR))
    tn = (((0,), (0,)), ((), ()))
    nt = (((1,), (1,)), ((), ()))

    def body(dq_ref, dk_ref, dv_ref, q_ref, kv_ref, kr_ref, ql_ref, kvl_ref, qn_ref, kn_ref, wq_ref, wkv_ref,
             gq_ref, gk_ref, c_ref, s1_ref, s2_ref, gqn_ref, gqr_ref, gkn_ref, gkr_ref, dz_in_ref,
             dz_ref, gwq_ref, gwkv_ref, ggq_ref, ggk_ref, gql_ref, gkvl_ref, dqr_ref, dkv_ref):
        _acc_init([gwq_ref, gwkv_ref, ggq_ref, ggk_ref, gql_ref, gkvl_ref])
        c_v, s1_v, s2_v = c_ref[...], s1_ref[...], s2_ref[...]
        kr = kr_ref[...]
        kr_ss = _lanesum(kr * kr)
        dkr = jnp.zeros(kr.shape, F32)
        ggq_n = ggq_r = ggk_n = ggk_r = jnp.zeros((1, LANE), F32)

        def norm_bwd(n, r, rs, dyn, dyr, gn, gr):
            nh_, rh_ = n * rs, r * rs
            dnh, drh = dyn * gn, dyr * gr
            dot = (_lanesum(dnh * nh_) + _lanesum(drh * rh_)) * (1.0 / QK_DIM)
            return rs * (dnh - nh_ * dot), rs * (drh - rh_ * dot), _colsum(dyn * nh_), _colsum(dyr * rh_)

        for h in range(N_HEADS):
            n = q_ref[:, h * LANE:(h + 1) * LANE]
            r = q_ref[:, N_HEADS * LANE + h * LANE:N_HEADS * LANE + (h + 1) * LANE]
            rs = lax.rsqrt((_lanesum(n * n) + _lanesum(r * r)) * (1.0 / QK_DIM) + EPS)
            dyn = dq_ref[h, :, 0:LANE] * scale
            dyr = _rope_bwd(dq_ref[h, :, LANE:HEAD_PAD] * scale, c_v, s1_v, s2_v)
            dn, dr, g_n, g_r = norm_bwd(n, r, rs, dyn, dyr, gqn_ref[...], gqr_ref[...])
            dqr_ref[:, h * LANE:(h + 1) * LANE] = dn.astype(dqr_ref.dtype)
            dqr_ref[:, N_HEADS * LANE + h * LANE:N_HEADS * LANE + (h + 1) * LANE] = dr.astype(dqr_ref.dtype)
            ggq_n, ggq_r = ggq_n + g_n, ggq_r + g_r

            n = kv_ref[:, h * 2 * LANE:h * 2 * LANE + LANE]
            rs = lax.rsqrt((_lanesum(n * n) + kr_ss) * (1.0 / QK_DIM) + EPS)
            dyn = dk_ref[h, :, 0:LANE] * (1.0 / LOG2E)
            dyr = _rope_bwd(dk_ref[h, :, LANE:HEAD_PAD] * (1.0 / LOG2E), c_v, s1_v, s2_v)
            dn, dr, g_n, g_r = norm_bwd(n, kr, rs, dyn, dyr, gkn_ref[...], gkr_ref[...])
            dkv_ref[:, h * 2 * LANE:h * 2 * LANE + LANE] = dn.astype(dkv_ref.dtype)
            dkv_ref[:, h * 2 * LANE + LANE:(h + 1) * 2 * LANE] = dv_ref[h].astype(dkv_ref.dtype)
            dkr = dkr + dr
            ggk_n, ggk_r = ggk_n + g_n, ggk_r + g_r

        ggq_ref[:, 0:LANE] += ggq_n
        ggq_ref[:, LANE:] += ggq_r
        ggk_ref[:, 0:LANE] += ggk_n
        ggk_ref[:, LANE:] += ggk_r

        for d_ref, x_ref, w_ref, gw_ref, src, g_ref, off, gg_ref in (
                (dqr_ref, qn_ref, wq_ref, gwq_ref, ql_ref, gq_ref, o_ql, gql_ref),
                (dkv_ref, kn_ref, wkv_ref, gwkv_ref, kvl_ref, gk_ref, o_kvl, gkvl_ref)):
            dup = d_ref[...]
            gw_ref[...] += lax.dot_general(x_ref[...], dup, tn, preferred_element_type=F32)
            dy = lax.dot_general(dup, w_ref[...], nt, preferred_element_type=F32)
            v = src[...]
            r = lax.rsqrt(jnp.mean(v * v, axis=-1, keepdims=True) + EPS)
            vh = v * r
            dvh = dy * g_ref[...]
            dz_ref[:, off:off + v.shape[1]] = (
                r * (dvh - vh * jnp.mean(dvh * vh, axis=-1, keepdims=True))).astype(dz_ref.dtype)
            gg_ref[...] += _colsum(dy * vh)
        dz_ref[:, o_kr:o_kr + LANE] = dkr.astype(dz_ref.dtype)
        dz_ref[:, o_kr + LANE:] = jnp.zeros((t, SEG_LAT[1] - o_kr - LANE), dz_ref.dtype)

    hspec = lambda w: pl.BlockSpec((N_HEADS, t, w), lambda i: (0, i, 0))
    whole = lambda a: pl.BlockSpec(a.shape, lambda i: (0, 0))
    wide = 2 * N_HEADS * LANE
    return pl.pallas_call(
        body, name=name, grid=(s // t,),
        in_specs=[hspec(HEAD_PAD), hspec(HEAD_PAD), hspec(V_DIM), _rowspec(t, wide), _rowspec(t, wide),
                  _rowspec(t, LANE, SEG_KR[0] // LANE), _rowspec(t, Q_LORA, SEG_QL[0] // Q_LORA),
                  _rowspec(t, KV_LORA, SEG_KVL[0] // KV_LORA), _rowspec(t, Q_LORA), _rowspec(t, KV_LORA),
                  whole(w_q_up), whole(w_kv_up), _vecspec(Q_LORA), _vecspec(KV_LORA),
                  _rowspec(t, LANE), _rowspec(t, LANE), _rowspec(t, LANE),
                  _vecspec(LANE), _vecspec(LANE), _vecspec(LANE), _vecspec(LANE), _ANY],
        out_specs=[_rowspec(t, SEG_LAT[1], SEG_LAT[0] // SEG_LAT[1]), whole(w_q_up), whole(w_kv_up),
                   _vecspec(2 * LANE), _vecspec(2 * LANE), _vecspec(Q_LORA), _vecspec(KV_LORA)],
        out_shape=[_sds(dz.shape, dz.dtype), _sds(w_q_up.shape, F32), _sds(w_kv_up.shape, F32),
                   _sds((1, 2 * LANE), F32), _sds((1, 2 * LANE), F32), _sds((1, Q_LORA), F32),
                   _sds((1, KV_LORA), F32)],
        scratch_shapes=[pltpu.VMEM((t, wide), MXU_DTYPE), pltpu.VMEM((t, wide), MXU_DTYPE)],
        input_output_aliases={21: 0},
        compiler_params=_cp(("arbitrary",)),
    )(dqf, dkf, dvf, q_raw, kv, z, z, z, qn, kn, w_q_up, w_kv_up, g_ql, g_kvl, c_t, s1_t, s2_t,
      gqn, gqr, gkn, gkr, dz)


def _prenorm_bwd(dh, x, gxo, g, sc1p, below=None, *, name):
    s, d = x.shape
    t = min(2 * ROW_T if below is None else ROW_T, s)
    nb = 0 if below is None else 2

    def body(dh_ref, x_ref, gx_ref, g_ref, sc_ref, *rest):
        dx_ref, dsh_ref, dsc_ref, gg_ref = rest[nb:nb + 4]
        _acc_init([dsh_ref, dsc_ref, gg_ref])
        xv, dhv = x_ref[...], dh_ref[...]
        r = lax.rsqrt(jnp.mean(xv * xv, axis=-1, keepdims=True) + EPS)
        xn = xv * r
        dsh_ref[...] += _colsum(dhv)
        dsc_ref[...] += _colsum(dhv * (xn * g_ref[...]))
        dm = dhv * sc_ref[...]
        gg_ref[...] += _colsum(dm * xn)
        dxn = dm * g_ref[...]
        dx = gx_ref[...] + r * (dxn - xn * jnp.mean(dxn * xn, axis=-1, keepdims=True))
        dx_ref[...] = dx
        if below is not None:
            _acc_init([rest[nb + 5]])
            _gated_residual_bwd(dx, rest[0], rest[1], rest[nb + 4], rest[nb + 5])

    vec_out = [_vecspec(d), _vecspec(d), _vecspec(d)]
    vec_shape = [_sds((1, d), F32)] * 3
    return pl.pallas_call(
        body, name=name, grid=(s // t,),
        in_specs=[_rowspec(t, d), _rowspec(t, d), _rowspec(t, d), _vecspec(d), _vecspec(d)]
        + ([_rowspec(t, d), _vecspec(d)] if below is not None else []),
        out_specs=[_rowspec(t, d)] + vec_out + ([_rowspec(t, d), _vecspec(d)] if below is not None else []),
        out_shape=[_sds((s, d), F32)] + vec_shape
        + ([_sds((s, d), MXU_DTYPE), _sds((1, d), F32)] if below is not None else []),
        compiler_params=_cp(("arbitrary",)),
    )(dh, x, gxo, g, sc1p, *(below if below is not None else ()))


def _ada_fwd(c_all, ada_w, ada_b_cols, *, name):
    nl, d, cols = ada_w.shape

    def body(c_ref, w_ref, b_ref, o_ref):
        ca = _silu(c_ref[...]).astype(MXU_DTYPE)
        o_ref[0] = jnp.dot(ca, w_ref[0].astype(MXU_DTYPE), preferred_element_type=F32) + b_ref[0]

    return pl.pallas_call(
        body, name=name, grid=(nl,),
        in_specs=[pl.BlockSpec((N_DEV, d), lambda l: (0, 0)), pl.BlockSpec((1, d, cols), lambda l: (l, 0, 0)),
                  pl.BlockSpec((1, 1, cols), lambda l: (l, 0, 0))],
        out_specs=pl.BlockSpec((1, N_DEV, cols), lambda l: (l, 0, 0)),
        out_shape=_sds((nl, N_DEV, cols), F32),
        compiler_params=_cp(("parallel",)),
    )(c_all, ada_w, ada_b_cols)


def _ada_bwd(c_all_t, dmod_cols, *, name):
    nl, _, cols = dmod_cols.shape
    d = c_all_t.shape[0]

    def body(c_ref, dm_ref, o_ref):
        ca = _silu(c_ref[...]).astype(MXU_DTYPE)
        o_ref[0] = jnp.dot(ca, dm_ref[0].astype(MXU_DTYPE), preferred_element_type=F32)

    return pl.pallas_call(
        body, name=name, grid=(nl,),
        in_specs=[pl.BlockSpec((d, N_DEV), lambda l: (0, 0)), pl.BlockSpec((1, N_DEV, cols), lambda l: (l, 0, 0))],
        out_specs=pl.BlockSpec((1, d, cols), lambda l: (l, 0, 0)),
        out_shape=_sds((nl, d, cols), F32),
        compiler_params=_cp(("parallel",)),
    )(c_all_t, dmod_cols)


def _adamw_math(g, w, m, v):
    mn = ADAM_B1 * m + (1.0 - ADAM_B1) * g
    vn = ADAM_B2 * v + (1.0 - ADAM_B2) * (g * g)
    m_hat = mn / (1.0 - ADAM_B1 ** ADAM_STEP)
    v_hat = vn / (1.0 - ADAM_B2 ** ADAM_STEP)
    return -ADAM_LR * (m_hat / (jnp.sqrt(v_hat) + ADAM_EPS) + ADAM_WD * w), mn, vn


def _adamw_small(items, *, name):
    n = len(items)
    shapes = [it[1].shape for it in items]
    flat = lambda a, lead: a.reshape(lead + (-1, a.shape[-1]))
    operands = []
    for gp, w, m, v in items:
        operands += [flat(gp, (gp.shape[0],)), flat(w, ()), flat(m, ()), flat(v, ())]
    nparts = [it[0].shape[0] for it in items]

    def body(*refs):
        ins, outs = refs[:4 * n], refs[4 * n:]
        for i in range(n):
            g_ref, w_ref, m_ref, v_ref = ins[4 * i:4 * i + 4]
            g = g_ref[0].astype(F32)
            for p in range(1, nparts[i]):
                g = g + g_ref[p].astype(F32)
            outs[4 * i][...] = g
            outs[4 * i + 1][...], outs[4 * i + 2][...], outs[4 * i + 3][...] = _adamw_math(
                g, w_ref[...], m_ref[...], v_ref[...])

    out_shape = []
    for it in items:
        out_shape += [_sds(flat(it[1], ()).shape, F32)] * 4
    outs = pl.pallas_call(body, name=name, out_shape=out_shape, compiler_params=_cp())(*operands)
    return [tuple(o.reshape(shp) for o in outs[4 * i:4 * i + 4]) for i, shp in enumerate(shapes)]


def _adamw_layer(gparts, w, m, v, layer, prev, *, name):
    shape = w.shape
    nl, cols = shape[0], shape[-1]
    rows = w.size // cols // nl
    npart = gparts.shape[0]
    g3 = gparts.reshape(npart, rows, cols)
    w3, m3, v3 = (a.reshape(nl, rows, cols) for a in (w, m, v))
    fits = [t for t in range(min(rows, 256) // 8 * 8, 7, -8)
            if rows % t == 0 and npart * t * cols * g3.dtype.itemsize <= 2 * 1024 * 1024]
    t = fits[0] if fits else rows
    n_prev = 0 if prev is None else 4

    def body(g_ref, w_ref, m_ref, v_ref, *rest):
        go_ref, d_ref, mo_ref, vo_ref = rest[n_prev:]
        g = g_ref[0].astype(F32)
        for p in range(1, npart):
            g = g + g_ref[p].astype(F32)
        go_ref[0] = g
        d_ref[0], mo_ref[0], vo_ref[0] = _adamw_math(g, w_ref[0], m_ref[0], v_ref[0])

    spec = pl.BlockSpec((1, t, cols), lambda i: (layer, i, 0))
    outs = pl.pallas_call(
        body, name=name, grid=(rows // t,),
        in_specs=[pl.BlockSpec((npart, t, cols), lambda i: (0, i, 0)), spec, spec, spec] + [_ANY] * n_prev,
        out_specs=[spec] * 4, out_shape=[_sds((nl, rows, cols), F32)] * 4,
        input_output_aliases={4 + k: k for k in range(n_prev)},
        compiler_params=_cp(("parallel",)),
    )(g3, w3, m3, v3, *([] if prev is None else [a.reshape(nl, rows, cols) for a in prev]))
    return tuple(o.reshape(shape) for o in outs)


def _adamw(gparts, w, m, v, *, name):
    shape = w.shape
    cols = shape[-1]
    per_layer = isinstance(gparts, (list, tuple))
    nl = shape[0] if per_layer else 1
    rows = w.size // cols // nl
    glist = list(gparts) if per_layer else [gparts]
    npart = glist[0].shape[0]
    glist = [g.reshape(npart, rows, cols) for g in glist]
    w3, m3, v3 = (a.reshape(nl, rows, cols) for a in (w, m, v))
    budget = 2 * 1024 * 1024
    fits = [t for t in range(min(rows, 256) // 8 * 8, 7, -8)
            if rows % t == 0 and npart * t * cols * glist[0].dtype.itemsize <= budget]
    t = fits[0] if fits else rows
    nb = rows // t

    def body(*refs):
        g_refs = refs[:nl]
        w_ref, m_ref, v_ref, go_ref, d_ref, mo_ref, vo_ref, g_s = refs[nl:]
        layer = pl.program_id(0)
        for l in range(nl):
            @pl.when(layer == l)
            def _(l=l):
                g = g_refs[l][0].astype(F32)
                for p in range(1, npart):
                    g = g + g_refs[l][p].astype(F32)
                g_s[...] = g

        g = g_s[...]
        go_ref[0] = g
        d_ref[0], mo_ref[0], vo_ref[0] = _adamw_math(g, w_ref[0], m_ref[0], v_ref[0])

    def g_map(l):
        return lambda layer, i: (0, jnp.where(layer == l, i, jnp.where(layer < l, 0, nb - 1)), 0)

    spec = pl.BlockSpec((1, t, cols), lambda layer, i: (layer, i, 0))
    outs = pl.pallas_call(
        body, name=name, grid=(nl, nb),
        in_specs=[pl.BlockSpec((npart, t, cols), g_map(l)) for l in range(nl)] + [spec, spec, spec],
        out_specs=[spec] * 4, out_shape=[_sds((nl, rows, cols), F32)] * 4,
        scratch_shapes=[pltpu.VMEM((t, cols), F32)],
        compiler_params=_cp(("arbitrary", "arbitrary")),
    )(*glist, w3, m3, v3)
    return tuple(o.reshape(shape) for o in outs)


_ANY = pl.BlockSpec(memory_space=pl.ANY)


def _all_gather(blocks, *, name):
    na = len(blocks)

    def body(*refs):
        x_refs, out_refs = refs[:na], refs[na:2 * na]
        send_sems, recv_sems, local_sems = refs[2 * na:]
        x, y, c = lax.axis_index("x"), lax.axis_index("y"), lax.axis_index("c")
        me, sibling = (x, y, c), (x, y, 1 - c)
        chips = [(1 - x, y), (x, 1 - y), (1 - x, 1 - y)]

        def slot(a, px, py, pc):
            return out_refs[a].at[4 * px + 2 * py + pc]

        def copy(a, k, blk, to, src=None):
            return pltpu.make_async_remote_copy(
                src_ref=slot(a, *blk) if src is None else src, dst_ref=slot(a, *blk),
                send_sem=send_sems.at[7 * a + k], recv_sem=recv_sems.at[7 * a + k],
                device_id=to, device_id_type=MESH_ID)

        mine = [pltpu.make_async_copy(x_refs[a], slot(a, *me), local_sems.at[a]) for a in range(na)]
        for cp in mine:
            cp.start()
        first = []
        for a in range(na):
            first.append(copy(a, 0, me, sibling, src=x_refs[a]))
            first += [copy(a, 1 + j, me, (*chip, c), src=x_refs[a]) for j, chip in enumerate(chips)]
        for cp in first:
            cp.start()
        passed = []
        for a in range(na):
            for j, chip in enumerate(chips):
                copy(a, 1 + j, (*chip, c), me).wait_recv()
                fwd = copy(a, 4 + j, (*chip, c), sibling)
                fwd.start()
                passed.append(fwd)
        for a in range(na):
            copy(a, 0, sibling, me).wait_recv()
            for j, chip in enumerate(chips):
                copy(a, 4 + j, (*chip, 1 - c), me).wait_recv()
        for cp in first + passed:
            cp.wait_send()
        for cp in mine:
            cp.wait()

    outs = pl.pallas_call(
        body, name=name, in_specs=[_ANY] * na, out_specs=[_ANY] * na,
        out_shape=[_sds((N_DEV,) + b.shape, b.dtype) for b in blocks],
        scratch_shapes=[pltpu.SemaphoreType.DMA((7 * na,)), pltpu.SemaphoreType.DMA((7 * na,)),
                        pltpu.SemaphoreType.DMA((na,))],
    )(*blocks)
    return list(outs)


_HBM = pl.BlockSpec(memory_space=pltpu.HBM)
_SEM = pl.BlockSpec(memory_space=pltpu.SEMAPHORE)
_EFFECT = pltpu.SideEffectType.DATAFLOW_SIDE_EFFECTING


def _peers(x, y, c):
    out = []
    for k in range(1, N_DEV):
        out.append((1 - x if k & 4 else x, 1 - y if k & 2 else y, 1 - c if k & 1 else c))
    return out


def _own_slots(srcs, scatter, *, name, after=None):
    na = len(srcs)
    n_extra = 0 if after is None else 1
    me = (4 * lax.axis_index("x") + 2 * lax.axis_index("y") + lax.axis_index("c")).astype(jnp.int32).reshape(1)

    def body(me_ref, *refs):
        in_refs, out_refs = refs[:na], refs[na + n_extra:]
        for a in range(na):
            out_refs[a][0] = in_refs[a][0] if scatter else in_refs[a][...]

    def slot_spec(shard):
        zeros = (0,) * len(shard)
        return pl.BlockSpec((1,) + tuple(shard), lambda i, me_ref: (me_ref[0],) + zeros)

    def whole_spec(shape):
        zeros = (0,) * len(shape)
        return pl.BlockSpec(tuple(shape), lambda i, me_ref: zeros)

    shards = [s.shape[1:] if scatter else s.shape for s in srcs]
    in_specs = [slot_spec(sh) if scatter else whole_spec(sh) for sh in shards] + [_ANY] * n_extra
    outs = pl.pallas_call(
        body, name=name,
        grid_spec=pltpu.PrefetchScalarGridSpec(
            num_scalar_prefetch=1, grid=(1,), in_specs=in_specs, out_specs=[slot_spec(sh) for sh in shards]),
        out_shape=[_sds((N_DEV,) + tuple(sh), s.dtype) for sh, s in zip(shards, srcs)],
        compiler_params=_cp(("arbitrary",)),
    )(me, *srcs, *([] if after is None else [after]))
    return list(outs)


_N_COPIES = dict(scatter=7, gather=7, chips=4, forward=3)


def _exchange_copies(src_refs, land_refs, send_sems, recv_sems, mode):
    x, y, c = lax.axis_index("x"), lax.axis_index("y"), lax.axis_index("c")
    me = 4 * x + 2 * y + c
    nc = _N_COPIES[mode]
    chips = [(1 - x, y), (x, 1 - y), (1 - x, 1 - y)]
    cps = []
    for a in range(len(land_refs)):
        if mode in ("scatter", "gather"):
            plan = [((src_refs[a].at[4 * px + 2 * py + pc] if mode == "scatter" else src_refs[a]),
                     land_refs[a].at[me], (px, py, pc)) for px, py, pc in _peers(x, y, c)]
        elif mode == "chips":
            plan = [(src_refs[a], land_refs[a].at[me], to) for to in [(x, y, 1 - c)] + [(*ch, c) for ch in chips]]
        else:
            plan = [(land_refs[a].at[4 * px + 2 * py + c], land_refs[a].at[4 * px + 2 * py + c], (x, y, 1 - c))
                    for px, py in chips]
        for k, (src, dst, to) in enumerate(plan):
            cps.append(pltpu.make_async_remote_copy(
                src_ref=src, dst_ref=dst, send_sem=send_sems.at[nc * a + k], recv_sem=recv_sems.at[nc * a + k],
                device_id=to, device_id_type=MESH_ID))
    return cps


def _exchange_start(srcs, lands, mode, *, name):
    ns, nz = len(srcs), len(lands)
    nsem = _N_COPIES[mode] * nz

    def body(*refs):
        src_refs, land_refs = refs[:ns], refs[ns:ns + nz]
        send_sems, recv_sems = refs[ns + nz], refs[ns + nz + 1]
        token = refs[-1]
        for cp in _exchange_copies(src_refs, land_refs, send_sems, recv_sems, mode):
            cp.start()
        token[...] = jnp.zeros(token.shape, token.dtype)

    hbm = lambda a: pltpu.HBM(a.shape, a.dtype)
    outs = pl.pallas_call(
        body, name=name,
        out_shape=(pltpu.SemaphoreType.DMA((nsem,)), pltpu.SemaphoreType.DMA((nsem,)),
                   *[hbm(a) for a in srcs], *[hbm(a) for a in lands], _sds((8, LANE), F32)),
        in_specs=[_HBM] * (ns + nz),
        out_specs=(_SEM, _SEM, *[_HBM] * (ns + nz), pl.BlockSpec(memory_space=pltpu.VMEM)),
        input_output_aliases={i: 2 + i for i in range(ns + nz)},
        compiler_params=pltpu.CompilerParams(has_side_effects=_EFFECT),
    )(*[pltpu.with_memory_space_constraint(a, pltpu.HBM) for a in list(srcs) + list(lands)])
    return outs[0], outs[1], list(outs[2:2 + ns]), list(outs[2 + ns:2 + ns + nz]), outs[-1]


def _exchange_wait(send_sems, recv_sems, srcs, lands, after, mode, *, name):
    ns, nz = len(srcs), len(lands)

    def body(*refs):
        src_refs, land_refs = refs[:ns], refs[ns:ns + nz]
        s_sems, r_sems = refs[ns + nz], refs[ns + nz + 1]
        for cp in _exchange_copies(src_refs, land_refs, s_sems, r_sems, mode):
            cp.wait_send()
            cp.wait_recv()

    hbm = lambda a: pltpu.HBM(a.shape, a.dtype)
    outs = pl.pallas_call(
        body, name=name,
        out_shape=(*[hbm(a) for a in srcs], *[hbm(a) for a in lands]),
        in_specs=[_HBM] * (ns + nz) + [_SEM, _SEM, _ANY],
        out_specs=tuple([_HBM] * (ns + nz)),
        input_output_aliases={i: i for i in range(ns + nz)},
        compiler_params=pltpu.CompilerParams(has_side_effects=_EFFECT),
    )(*srcs, *lands, send_sems, recv_sems, after)
    return list(outs[ns:])


_WIN_SEGS = (("ql", 0, Q_LORA, SEG_QL[0]), ("kvl", Q_LORA, KV_LORA, SEG_KVL[0]),
             ("kr", Q_LORA + KV_LORA, ROPE, SEG_KR[0]), ("mg", Q_LORA + KV_LORA + ROPE, D_MLA, SEG_MG[0]),
             ("ci", Q_LORA + KV_LORA + ROPE + D_MLA, 2 * D_CONV, SEG_CI[0]),
             ("cg", Q_LORA + KV_LORA + ROPE + D_MLA + 2 * D_CONV, D_CONV, SEG_CG[0]))
_WIN_SHARD = IN_COLS // N_DEV


def _win_pieces():
    out = []
    for _, o, n, new in _WIN_SEGS:
        for j in range(N_DEV):
            lo, hi = max(o, j * _WIN_SHARD), min(o + n, (j + 1) * _WIN_SHARD)
            if lo < hi:
                out.append((j, lo - j * _WIN_SHARD, new + lo - o, hi - lo))
    return out


WIN_T = 512


def _win_assemble(w_all, *, name):
    d = w_all.shape[2]
    t = min(WIN_T, d)
    pieces = sorted(_win_pieces(), key=lambda p: p[2])
    assert all(lo % 8 == 0 and n % 8 == 0 for _, lo, _, n in pieces)

    def body(w_ref, o_ref):
        rows = [w_ref[j].astype(F32)[lo:lo + n, :] for j, lo, _, n in pieces]
        rows.append(jnp.zeros((IN_PAD - (SEG_KR[0] + ROPE), t), F32))
        o_ref[...] = jnp.concatenate(rows, axis=0).astype(o_ref.dtype)

    return pl.pallas_call(
        body, name=name, grid=(d // t,),
        in_specs=[pl.BlockSpec((N_DEV, _WIN_SHARD, t), lambda i: (0, 0, i))],
        out_specs=pl.BlockSpec((IN_PAD, t), lambda i: (0, i)), out_shape=_sds((IN_PAD, d), w_all.dtype),
        compiler_params=_cp(("parallel",)),
    )(w_all)


def _win_split(grad, *, name):
    d = grad.shape[1]
    t = min(WIN_T, d)
    by_shard = [sorted([p for p in _win_pieces() if p[0] == j], key=lambda p: p[1]) for j in range(N_DEV)]

    def body(g_ref, o_ref):
        for j in range(N_DEV):
            rows = [g_ref[new:new + n, :] for _, _, new, n in by_shard[j]]
            o_ref[j] = jnp.concatenate(rows, axis=0).astype(o_ref.dtype)

    return pl.pallas_call(
        body, name=name, grid=(d // t,),
        in_specs=[pl.BlockSpec((IN_PAD, t), lambda i: (0, i))],
        out_specs=pl.BlockSpec((N_DEV, _WIN_SHARD, t), lambda i: (0, 0, i)),
        out_shape=_sds((N_DEV, _WIN_SHARD, d), WIRE_DTYPE),
        compiler_params=_cp(("parallel",)),
    )(grad)


def _cols_to_shards(a):
    r, n = a.shape
    return a.reshape(r, N_DEV, n // N_DEV).transpose(1, 0, 2)


def _shards_to_cols(a):
    nd, r, w = a.shape
    return a.transpose(1, 0, 2).reshape(r, nd * w)


def _qup_permute(w):
    w3 = w.reshape(w.shape[0], N_HEADS, QK_DIM)
    nope = w3[:, :, :NOPE].reshape(w.shape[0], N_HEADS * NOPE)
    rope = jnp.pad(w3[:, :, NOPE:], ((0, 0), (0, 0), (0, LANE - ROPE))).reshape(w.shape[0], N_HEADS * LANE)
    return jnp.concatenate([nope, rope], axis=1)


def _qup_unpermute(g):
    r = g.shape[0]
    nope = g[:, :N_HEADS * NOPE].reshape(r, N_HEADS, NOPE)
    rope = g[:, N_HEADS * NOPE:].reshape(r, N_HEADS, LANE)[:, :, :ROPE]
    return jnp.concatenate([nope, rope], axis=2).reshape(r, N_HEADS * QK_DIM)


def _norm_tiles(g):
    return g[:NOPE].reshape(1, LANE), jnp.pad(g[NOPE:], (0, LANE - ROPE)).reshape(1, LANE)


def _rope_tiles(positions):
    inv_freq = 1.0 / (ROPE_THETA ** (jnp.arange(0, ROPE, 2, dtype=F32) / ROPE))
    ang = positions.astype(F32)[:, None] * inv_freq
    cos, sin = jnp.cos(ang), jnp.sin(ang)
    zq = jnp.zeros_like(cos)
    c_t = jnp.concatenate([cos, cos, zq, zq], axis=1)
    s1_t = jnp.concatenate([-sin, zq, zq, zq], axis=1)
    s2_t = jnp.concatenate([zq, sin, zq, zq], axis=1)
    return c_t, s1_t, s2_t


_BIG = ("w_in", "w_q_up", "w_kv_up", "w_pw", "w_out")
_COL_SHARDED = ("w_q_up", "w_kv_up")


def _unpack_rows(buf, shapes):
    out, r0 = [], 0
    lead = buf.shape[:-2]
    for shp in shapes:
        n = math.prod(shp) // LANE
        out.append(buf[..., r0:r0 + n, :].reshape(lead + tuple(shp)))
        r0 += n
    return out


_SMALL = (("dmod", 3 * D_MODEL), ("norm_g", D_MODEL), ("q_lat_g", Q_LORA), ("kv_lat_g", KV_LORA),
          ("q_norm_g", 2 * LANE), ("k_norm_g", 2 * LANE), ("glu_b", 2 * D_CONV), ("dw_w", HALO * D_CONV),
          ("dw_b", D_CONV), ("conv_ln_g", D_CONV), ("conv_ln_b", D_CONV), ("b_pw", D_CONV))


def _layer_fwd(x, p, rope, l, late=None):
    n = lambda s: f"{s}_l{l}"
    c_t, s1_t, s2_t = rope
    h = _prenorm(x, p["norm_g"], p["shift"], p["sc1p"], name=n("prenorm"))
    z = _mm(h, p["w_in"], tb=True, name=n("in_proj"), tm=1024, tn=IN_TILE, n_outer=True)
    if late is not None:
        p = {**p, **late(z)}
    qn, kn, q_raw, kv, qf, kf, vf = _mla_pre(z, p["w_q_up"], p["w_kv_up"], p["q_lat_g"], p["kv_lat_g"],
                                             c_t, s1_t, s2_t, *p["qk_tiles"], name=n("mla_pre"))
    o, lse = _flash_fwd(qf, kf, vf, name=n("flash_fwd"))
    u1, u3, u4m = _conv_fwd(z, p["glu_b"], p["dw_w"], p["dw_b"], p["conv_ln_g"], p["conv_ln_b"], p["w_pw"],
                            name=n("conv_fwd"))
    cat = _gate_cat(o, z, u4m, p["b_pw"], name=n("gate_cat"))
    y, x_next = _mm(cat, p["w_out"], name=n("out_proj"), tn=1024, residual=(x, p["gate"]))
    saved = dict(x=x, h=h, z=z, qn=qn, kn=kn, q_raw=q_raw, kv=kv, qf=qf, kf=kf, vf=vf, o=o, lse=lse,
                 u1=u1, u3=u3, u4m=u4m, cat=cat, y=y)
    return x_next, saved, p


def _layer_bwd(gxo, dy, dgate, p, sv, rope, l, below=None, hook_rest=None, hook_w_in=None):
    n = lambda s: f"{s}_l{l}"
    c_t, s1_t, s2_t = rope
    z = sv["z"]
    g_w_out = _mm(sv["cat"], dy, ta=True, name=n("g_w_out"), tm=1024, tn=1024, after=p.get("after_start"))
    do, delta, du4, g_b_pw, dz = _gate_bwd(dy, p["w_out"], sv["o"], z, sv["u4m"], p["b_pw"], name=n("gate_bwd"))
    g_w_pw = _mm(sv["u3"], du4, ta=True, name=n("g_w_pw"), tm=1024, tn=1024)
    du3 = _mm(du4, p["w_pw"], tb=True, name=n("d_u3"), tn=1024)
    dz, g_ln_g, g_ln_b, g_dw_b, g_glu_b, g_dw_w = _conv_bwd(
        du3, sv["u1"], z, dz, p["glu_b"], p["dw_w"], p["conv_ln_g"], p["conv_ln_b"], name=n("conv_bwd"))
    dqf, dkf, dvf = _flash_bwd(sv["qf"], sv["kf"], sv["vf"], do, sv["lse"], delta.reshape(sv["lse"].shape),
                               name=n("flash_bwd"))
    dz, g_w_q_up, g_w_kv_up, g_qn, g_kn, g_ql, g_kvl = _mla_bwd(
        dqf, dkf, dvf, sv["q_raw"], sv["kv"], z, dz, sv["qn"], sv["kn"], p["w_q_up"], p["w_kv_up"],
        p["q_lat_g"], p["kv_lat_g"], c_t, s1_t, s2_t, *p["qk_tiles"], name=n("mla_bwd"))
    big = dict(w_q_up=g_w_q_up, w_kv_up=g_w_kv_up, w_pw=g_w_pw, w_out=g_w_out)
    after = None if hook_rest is None else hook_rest(big)
    g_w_in = _mm(dz, sv["h"], ta=True, name=n("g_w_in"), tm=1024, tn=1024, after=after)
    big["w_in"] = g_w_in
    after = None if hook_w_in is None else hook_w_in(g_w_in)
    dh = _mm(dz, p["w_in"], name=n("d_h"), tn=1024, after=after)
    dx, dshift, dscale, g_norm, *down = _prenorm_bwd(dh, sv["x"], gxo, p["norm_g"], p["sc1p"], below,
                                                     name=n("prenorm_bwd"))
    small = dict(dmod=jnp.concatenate([dshift, dscale, dgate], axis=1), norm_g=g_norm, q_lat_g=g_ql, kv_lat_g=g_kvl,
                 q_norm_g=g_qn, k_norm_g=g_kn, glu_b=g_glu_b, dw_w=g_dw_w, dw_b=g_dw_b,
                 conv_ln_g=g_ln_g, conv_ln_b=g_ln_b, b_pw=g_b_pw)
    return (dx, *down), big, small


def _layer_params(l, full, mod_l, small):
    d = D_MODEL
    row = lambda a: a.reshape(1, -1)
    shift, scale, gate = mod_l[:, :d], mod_l[:, d:2 * d], mod_l[:, 2 * d:]
    dw_w = jnp.pad(full["dw_w"][l], ((0, HALO - CONV_K), (0, 0)))
    return dict(
        shift=shift, sc1p=1.0 + scale, gate=gate, norm_g=row(small["norm_g"][l]),
        **{k: full[k][l] for k in _BIG if k in full}, dw_w=dw_w,
        q_lat_g=row(small["q_lat_g"][l]), kv_lat_g=row(small["kv_lat_g"][l]),
        qk_tiles=_norm_tiles(small["q_norm_g"][l]) + _norm_tiles(small["k_norm_g"][l]),
        glu_b=row(small["glu_b"][l]), dw_b=row(small["dw_b"][l]), conv_ln_g=row(small["conv_ln_g"][l]),
        conv_ln_b=row(small["conv_ln_b"][l]), b_pw=row(small["b_pw"][l]))


def kernel(x, c, positions, ada_w, ada_b, norm_g, w_in, q_lat_g, w_q_up, kv_lat_g, w_kv_up, q_norm_g, k_norm_g, glu_b, dw_w, dw_b, conv_ln_g, conv_ln_b, w_pw, b_pw, w_out, loss_target, m_ada_w, m_ada_b, m_norm_g, m_w_in, m_q_lat_g, m_w_q_up, m_kv_lat_g, m_w_kv_up, m_q_norm_g, m_k_norm_g, m_glu_b, m_dw_w, m_dw_b, m_conv_ln_g, m_conv_ln_b, m_w_pw, m_b_pw, m_w_out, v_ada_w, v_ada_b, v_norm_g, v_w_in, v_q_lat_g, v_w_q_up, v_kv_lat_g, v_w_kv_up, v_q_norm_g, v_k_norm_g, v_glu_b, v_dw_w, v_dw_b, v_conv_ln_g, v_conv_ln_b, v_w_pw, v_b_pw, v_w_out):
    names = ("ada_w", "ada_b", "norm_g", "w_in", "q_lat_g", "w_q_up", "kv_lat_g", "w_kv_up", "q_norm_g",
             "k_norm_g", "glu_b", "dw_w", "dw_b", "conv_ln_g", "conv_ln_b", "w_pw", "b_pw", "w_out")
    w_loc = dict(zip(names, (ada_w, ada_b, norm_g, w_in, q_lat_g, w_q_up, kv_lat_g, w_kv_up, q_norm_g, k_norm_g,
                             glu_b, dw_w, dw_b, conv_ln_g, conv_ln_b, w_pw, b_pw, w_out)))
    m_loc = dict(zip(names, (m_ada_w, m_ada_b, m_norm_g, m_w_in, m_q_lat_g, m_w_q_up, m_kv_lat_g, m_w_kv_up,
                             m_q_norm_g, m_k_norm_g, m_glu_b, m_dw_w, m_dw_b, m_conv_ln_g, m_conv_ln_b, m_w_pw,
                             m_b_pw, m_w_out)))
    v_loc = dict(zip(names, (v_ada_w, v_ada_b, v_norm_g, v_w_in, v_q_lat_g, v_w_q_up, v_kv_lat_g, v_w_kv_up,
                             v_q_norm_g, v_k_norm_g, v_glu_b, v_dw_w, v_dw_b, v_conv_ln_g, v_conv_ln_b, v_w_pw,
                             v_b_pw, v_w_out)))
    nl, d = N_LAYERS, D_MODEL
    me = 4 * lax.axis_index("x") + 2 * lax.axis_index("y") + lax.axis_index("c")
    x2, tgt = x[0], loss_target[0]
    ada_cols = ada_w.shape[-1]

    tr = lambda a: jnp.swapaxes(a, 1, 2)
    w_loc, m_loc, v_loc = ({**dd, "w_in": tr(dd["w_in"])} for dd in (w_loc, m_loc, v_loc))
    w_in0 = [w_loc["w_in"][0].astype(WIRE_DTYPE)]
    fly_c = _exchange_start(w_in0, _own_slots(w_in0, False, name="own_w_in_l0"), "chips", name="gather_start_w_in_l0")
    held = dict(c=c, positions=positions, ada_b=ada_b, norm_g=norm_g, q_lat_g=q_lat_g, kv_lat_g=kv_lat_g,
                q_norm_g=q_norm_g, k_norm_g=k_norm_g, glu_b=glu_b, dw_w=dw_w, dw_b=dw_b, conv_ln_g=conv_ln_g,
                conv_ln_b=conv_ln_b, b_pw=b_pw, big={k: w_loc[k] for k in _BIG})
    tok_c, held = lax.optimization_barrier((fly_c[4], held))
    c, positions, ada_b, norm_g, q_lat_g, kv_lat_g, q_norm_g, k_norm_g, glu_b, dw_w, dw_b, conv_ln_g, conv_ln_b, b_pw = (
        held[k] for k in ("c", "positions", "ada_b", "norm_g", "q_lat_g", "kv_lat_g", "q_norm_g", "k_norm_g", "glu_b",
                          "dw_w", "dw_b", "conv_ln_g", "conv_ln_b", "b_pw"))
    wire = {k: held["big"][k].astype(WIRE_DTYPE) for k in _BIG}

    dw_pad = jnp.pad(dw_w, ((0, 0), (0, HALO - CONV_K), (0, 0)))
    c_rows = c.reshape(d // LANE, LANE) + tok_c[0:1, :]
    c_all, dw_all = _all_gather([c_rows, dw_pad], name="gather_c")
    c_all = c_all.reshape(N_DEV, d)
    ada_b_cols = lax.dynamic_slice_in_dim(ada_b, me * ada_cols, ada_cols, axis=1).reshape(nl, 1, ada_cols)
    mod_cols = _ada_fwd(c_all, ada_w, ada_b_cols, name="ada_fwd")
    mod_all = _all_gather([mod_cols], name="gather_mod")[0]
    mod_me = lax.dynamic_index_in_dim(mod_all, me, axis=2, keepdims=False)
    mod = mod_me.transpose(1, 0, 2).reshape(nl, 1, N_DEV * ada_cols)

    from_chips = _exchange_wait(*fly_c[:4], mod, "chips", name="gather_wait_w_in_l0")
    fly_f = _exchange_start([], from_chips, "forward", name="forward_start_w_in_l0")
    w_in_all0 = _exchange_wait(*fly_f[:4], fly_f[4], "forward", name="forward_wait_w_in_l0")[0]
    rest0 = [wire[k][0] for k in _BIG[1:]]
    fly_r0, fly_w1 = {}, {}
    fly_r0["x"] = _exchange_start(rest0, _own_slots(rest0, False, name="own_weights_l0_rest", after=w_in_all0),
                                  "gather", name="gather_start_l0_rest")

    def layout_rest(parts):
        return dict(w_q_up=_qup_permute(_shards_to_cols(parts[0])), w_kv_up=_shards_to_cols(parts[1]),
                    w_pw=parts[2].reshape(D_CONV, D_CONV), w_out=parts[3].reshape(D_MLA + D_CONV, d))

    small_in = dict(norm_g=norm_g, q_lat_g=q_lat_g, kv_lat_g=kv_lat_g, q_norm_g=q_norm_g, k_norm_g=k_norm_g,
                    glu_b=glu_b, dw_b=dw_b, conv_ln_g=conv_ln_g, conv_ln_b=conv_ln_b, b_pw=b_pw)
    dw_full = [_shards_to_cols(dw_all[:, l])[:CONV_K] for l in range(nl)]
    rope = _rope_tiles(positions[0])

    def layer_params(l, w_in_all, rest, mod_l):
        full = dict(dw_w=dw_full)
        if w_in_all is not None:
            full["w_in"] = {l: _win_assemble(w_in_all, name=f"w_in_assemble_l{l}")}
        if rest is not None:
            full.update({k: {l: a} for k, a in layout_rest(rest).items()})
        return _layer_params(l, full, mod_l, small_in)

    src1 = [wire[k][1] for k in _BIG]
    fly_w1["x"] = _exchange_start(src1, _own_slots(src1, False, name="own_weights_l1", after=fly_r0["x"][4]), "gather",
                                  name="gather_start_l1")

    def late_l0(z):
        return layout_rest(_exchange_wait(*fly_r0["x"][:4], z, "gather", name="gather_wait_l0_rest"))

    params, saved = [None] * nl, [None] * nl
    p0 = layer_params(0, w_in_all0, None, mod[0] + fly_w1["x"][4][0, 0])
    xs, saved[0], params[0] = _layer_fwd(x2, p0, rope, 0, late=late_l0)
    parts1 = _exchange_wait(*fly_w1["x"][:4], xs, "gather", name="gather_wait_l1")
    params[1] = layer_params(1, parts1[0], parts1[1:], mod[1])
    xs, saved[1], _ = _layer_fwd(xs, params[1], rope, 1)
    gx, loss_part, dy, dgate = _loss_head(xs, tgt, saved[1]["y"], params[1]["gate"], name="loss_head")
    loss = lax.psum(loss_part[0, 0], ("x", "y", "c"))

    def shard_major(k, g):
        if k == "w_q_up":
            g = _qup_unpermute(g)
        if k in _COL_SHARDED:
            return _cols_to_shards(g)
        return g.reshape((N_DEV, g.shape[0] // N_DEV, g.shape[1]))

    def scatter_start(send, tag):
        lands = _own_slots(send, True, name=f"own_grads_{tag}")
        return _exchange_start(send, lands, "scatter", name=f"scatter_start_{tag}")

    def wire_rest(big):
        return [shard_major(k, big[k]).astype(WIRE_DTYPE) for k in _BIG[1:]]

    big_g, small_g, flying = [None] * nl, [None] * nl, {}
    (gx, dy, dgate), big_g[1], small_g[1] = _layer_bwd(gx, dy, dgate, params[1], saved[1], rope, 1,
                                                       below=(saved[0]["y"], params[0]["gate"]))
    flying["l1"] = scatter_start([_win_split(big_g[1]["w_in"], name="w_in_split_l1")] + wire_rest(big_g[1]), "l1")
    p0 = dict(params[0], after_start=flying["l1"][4], b_pw=params[0]["b_pw"] + flying["l1"][4][0, 0])

    def start_rest_l0(big):
        flying["l0_rest"] = scatter_start(wire_rest(big), "l0_rest")
        return flying["l0_rest"][4]

    res, arrived = {}, [None] * nl

    def start_w_in_l0(g_w_in):
        flying["l0_w_in"] = scatter_start([_win_split(g_w_in, name="w_in_split_l0")], "l0_w_in")
        tok = flying["l0_w_in"][4]
        arrived[1] = _exchange_wait(*flying["l1"][:4], tok, "scatter", name="scatter_wait_l1")
        arrived[0] = [None] + _exchange_wait(*flying["l0_rest"][:4], tok, "scatter", name="scatter_wait_l0_rest")
        for i, k in enumerate(_BIG):
            if i > 0:
                res[k] = _adamw([arrived[l][i] for l in range(nl)], w_loc[k], m_loc[k], v_loc[k], name=f"adamw_{k}")
        res["w_in_l1"] = _adamw_layer(arrived[1][0], w_loc["w_in"], m_loc["w_in"], v_loc["w_in"], 1, None,
                                      name="adamw_w_in_l1")
        return res["w_in_l1"][0]

    (gx,), big_g[0], small_g[0] = _layer_bwd(gx, dy, dgate, p0, saved[0], rope, 0, hook_rest=start_rest_l0,
                                             hook_w_in=start_w_in_l0)

    tile = 8 * LANE
    padded = [(k, nn, -(-nn // tile) * tile) for k, nn in _SMALL]
    spk = jnp.concatenate([jnp.pad(small_g[l][k].reshape(-1), (0, np_ - nn)).reshape(-1, LANE)
                           for l in range(nl) for k, nn, np_ in padded], axis=0)
    s_all = _all_gather([spk], name="gather_small_grads")[0]
    s_rows = sum(np_ for _, _, np_ in padded) // LANE
    s_all = s_all.reshape(N_DEV, nl, s_rows, LANE)
    s_parts = {k: a[..., :nn] for (k, nn, _), a in
               zip(padded, _unpack_rows(s_all, [(np_,) for _, _, np_ in padded]))}

    dmod_all = s_parts["dmod"]
    dmod_cols = lax.dynamic_slice_in_dim(dmod_all, me * ada_cols, ada_cols, axis=2).transpose(1, 0, 2)
    g_ada_w = _ada_bwd(c_all.T, dmod_cols, name="ada_bwd")
    gp = {}
    gp["ada_w"] = g_ada_w[None]
    gp["ada_b"] = dmod_all
    for k in ("norm_g", "q_lat_g", "kv_lat_g", "glu_b", "dw_b", "conv_ln_g", "conv_ln_b", "b_pw"):
        gp[k] = s_parts[k]
    for k in ("q_norm_g", "k_norm_g"):
        t = s_parts[k]
        gp[k] = jnp.concatenate([t[..., :NOPE], t[..., LANE:LANE + ROPE]], axis=-1)
    dw_g = s_parts["dw_w"].reshape(N_DEV, nl, HALO, D_CONV)[:, :, :CONV_K]
    gp["dw_w"] = lax.dynamic_slice_in_dim(dw_g, me * LANE, LANE, axis=3)

    res["ada_w"] = _adamw(gp["ada_w"], w_loc["ada_w"], m_loc["ada_w"], v_loc["ada_w"], name="adamw_ada_w")
    small_names = [k for k in names if k not in _BIG and k != "ada_w"]
    res.update(zip(small_names, _adamw_small([(gp[k], w_loc[k], m_loc[k], v_loc[k]) for k in small_names],
                                             name="adamw_small")))
    arrived[0][0] = _exchange_wait(*flying["l0_w_in"][:4], res["ada_w"][1], "scatter", name="scatter_wait_l0_w_in")[0]
    w_in_res = _adamw_layer(arrived[0][0], w_loc["w_in"], m_loc["w_in"], v_loc["w_in"], 0, res.pop("w_in_l1"),
                            name="adamw_w_in_l0")
    res["w_in"] = tuple(tr(a) for a in w_in_res)
    out = [loss, gx[None]]
    for idx in range(4):
        out += [res[k][idx] for k in names]
    return tuple(out)
```

```python
import functools
import math

import jax
import jax.numpy as jnp
from jax import lax
from jax.experimental import pallas as pl
from jax.experimental.pallas import tpu as pltpu

F32 = jnp.float32
MXU_DTYPE = jnp.bfloat16
WIRE_DTYPE = jnp.bfloat16

D_MODEL = 2048
N_LAYERS = 2
N_DEV = 8
N_HEADS = 8
NOPE = 128
ROPE = 64
V_DIM = 128
QK_DIM = NOPE + ROPE
Q_LORA = 512
KV_LORA = 256
D_MLA = N_HEADS * V_DIM
D_CONV = 1024
CONV_K = 31
ROPE_THETA = 10000.0
EPS = 1e-6
LANE = 128
HEAD_PAD = 2 * LANE
HALO = 32

SEG_CI = (0, 2 * D_CONV)
SEG_MG = (2 * D_CONV, D_MLA)
SEG_CG = (2 * D_CONV + D_MLA, D_CONV)
SEG_QL = (2 * D_CONV + D_MLA + D_CONV, Q_LORA)
SEG_KVL = (SEG_QL[0] + Q_LORA, KV_LORA)
SEG_KR = (SEG_KVL[0] + KV_LORA, LANE)
SEG_LAT = (SEG_QL[0], 1024)
IN_PAD = SEG_LAT[0] + SEG_LAT[1]
IN_TILE = IN_PAD // 4
assert SEG_KR[0] + LANE <= IN_PAD and SEG_LAT[0] % SEG_LAT[1] == 0
IN_COLS = Q_LORA + KV_LORA + ROPE + D_MLA + 2 * D_CONV + D_CONV

ADAM_LR = 0.001
ADAM_B1 = 0.9
ADAM_B2 = 0.999
ADAM_EPS = 1e-08
ADAM_WD = 0.01
ADAM_STEP = 10

VMEM_LIMIT = 56 * 1024 * 1024
ATT_T = 512
ROW_T = 256
CONV_T = 256
MESH_ID = pl.DeviceIdType.MESH


def _cp(sem=None):
    kw = dict(vmem_limit_bytes=VMEM_LIMIT)
    if sem is not None:
        kw["dimension_semantics"] = sem
    return pltpu.CompilerParams(**kw)


def _sds(shape, dtype):
    return jax.ShapeDtypeStruct(shape, dtype)


def _silu(x):
    return x * jax.nn.sigmoid(x)


def _dsilu(x):
    s = jax.nn.sigmoid(x)
    return s * (1.0 + x * (1.0 - s))


def _rowspec(t, width, col=0):
    return pl.BlockSpec((t, width), lambda i: (i, col))


def _vecspec(width):
    return pl.BlockSpec((1, width), lambda i: (0, 0))


def _colsum(v):
    return jnp.sum(v, axis=0, keepdims=True)


def _mm(a, b, *, name, ta=False, tb=False, out_dtype=F32, tm=512, tn=512, tk=None, n_outer=False, after=None,
        residual=None):
    if ta:
        kdim, m = a.shape
    else:
        m, kdim = a.shape
    if tb:
        n, k2 = b.shape
    else:
        k2, n = b.shape
    assert kdim == k2, (a.shape, b.shape)
    tm, tn = min(tm, m), min(tn, n)
    tk = kdim if tk is None else min(tk, kdim)
    assert m % tm == 0 and n % tn == 0 and kdim % tk == 0, (m, n, kdim, tm, tn, tk)
    nk = kdim // tk
    dims = (((0 if ta else 1,), (1 if tb else 0,)), ((), ()))

    n_extra = 0 if after is None else 1
    assert residual is None or nk == 1

    def body(a_ref, b_ref, *rest):
        if residual is not None:
            x_ref, gate_ref = rest[:2]
            rest = rest[2:]
        o_ref, scratch = rest[n_extra], rest[n_extra + 1:]
        prod = lax.dot_general(a_ref[...].astype(MXU_DTYPE), b_ref[...].astype(MXU_DTYPE), dims,
                               preferred_element_type=F32)
        if residual is not None:
            o_ref[...] = prod.astype(o_ref.dtype)
            scratch[0][...] = x_ref[...] + gate_ref[...] * prod
        elif nk == 1:
            o_ref[...] = prod.astype(o_ref.dtype)
        else:
            acc = scratch[0]
            k = pl.program_id(2)

            @pl.when(k == 0)
            def _():
                acc[...] = prod

            @pl.when(k > 0)
            def _():
                acc[...] += prod

            @pl.when(k == nk - 1)
            def _():
                o_ref[...] = acc[...].astype(o_ref.dtype)

    if n_outer:
        ij = lambda g0, g1: (g1, g0)
        grid = (n // tn, m // tm, nk)
    else:
        ij = lambda g0, g1: (g0, g1)
        grid = (m // tm, n // tn, nk)

    def a_map(g0, g1, k):
        i, _ = ij(g0, g1)
        return (k, i) if ta else (i, k)

    def b_map(g0, g1, k):
        _, j = ij(g0, g1)
        return (j, k) if tb else (k, j)

    def o_map(g0, g1, k):
        return ij(g0, g1)

    in_specs = [pl.BlockSpec((tk, tm) if ta else (tm, tk), a_map), pl.BlockSpec((tn, tk) if tb else (tk, tn), b_map)]
    operands = [a, b]
    out_specs, out_shape = pl.BlockSpec((tm, tn), o_map), _sds((m, n), out_dtype)
    if residual is not None:
        in_specs += [pl.BlockSpec((tm, tn), o_map), pl.BlockSpec((1, tn), lambda g0, g1, k: (0, ij(g0, g1)[1]))]
        operands += list(residual)
        out_specs, out_shape = [out_specs, pl.BlockSpec((tm, tn), o_map)], [out_shape, _sds((m, n), F32)]
    if after is not None:
        in_specs.append(_ANY)
        operands.append(after)
    return pl.pallas_call(
        body, name=name, grid=grid, in_specs=in_specs, out_specs=out_specs, out_shape=out_shape,
        scratch_shapes=[pltpu.VMEM((tm, tn), F32)] if nk > 1 else [],
        compiler_params=_cp(("parallel", "parallel", "arbitrary")),
    )(*operands)


def _prenorm(x, g, shift, sc1p, *, name):
    s, d = x.shape
    t = min(2 * ROW_T, s)

    def body(x_ref, g_ref, sh_ref, sc_ref, h_ref):
        xv = x_ref[...]
        r = lax.rsqrt(jnp.mean(xv * xv, axis=-1, keepdims=True) + EPS)
        h_ref[...] = ((xv * r) * g_ref[...] * sc_ref[...] + sh_ref[...]).astype(h_ref.dtype)

    return pl.pallas_call(
        body, name=name, grid=(s // t,),
        in_specs=[_rowspec(t, d), _vecspec(d), _vecspec(d), _vecspec(d)],
        out_specs=_rowspec(t, d), out_shape=_sds((s, d), MXU_DTYPE),
        compiler_params=_cp(("parallel",)),
    )(x, g, shift, sc1p)


def _rope_fwd(r, c_t, s1_t, s2_t):
    return r * c_t + pltpu.roll(r, LANE - ROPE // 2, 1) * s1_t + pltpu.roll(r, ROPE // 2, 1) * s2_t


def _rope_bwd(d, c_t, s1_t, s2_t):
    return d * c_t + pltpu.roll(d * s1_t, ROPE // 2, 1) + pltpu.roll(d * s2_t, LANE - ROPE // 2, 1)


def _lanesum(v):
    return jnp.sum(v, axis=-1, keepdims=True)


def _mla_pre(z, w_q_up, w_kv_up, g_ql, g_kvl, c_t, s1_t, s2_t, gqn, gqr, gkn, gkr, *, name):
    s = z.shape[0]
    t = min(2 * ROW_T, s)
    scale = LOG2E / math.sqrt(QK_DIM)
    wide = 2 * N_HEADS * LANE

    def body(ql_ref, kvl_ref, kr_ref, wq_ref, wkv_ref, gq_ref, gk_ref, c_ref, s1_ref, s2_ref,
             gqn_ref, gqr_ref, gkn_ref, gkr_ref, qn_ref, kn_ref, q_ref, kv_ref, qf_ref, kf_ref, vf_ref):
        for src, g_ref, dst, w_ref, up in ((ql_ref, gq_ref, qn_ref, wq_ref, q_ref),
                                           (kvl_ref, gk_ref, kn_ref, wkv_ref, kv_ref)):
            v = src[...]
            r = lax.rsqrt(jnp.mean(v * v, axis=-1, keepdims=True) + EPS)
            dst[...] = ((v * r) * g_ref[...]).astype(dst.dtype)
            up[...] = jnp.dot(dst[...], w_ref[...], preferred_element_type=F32)
        c_v, s1_v, s2_v = c_ref[...], s1_ref[...], s2_ref[...]
        kr = kr_ref[...]
        kr_ss = _lanesum(kr * kr)
        for h in range(N_HEADS):
            n = q_ref[:, h * LANE:(h + 1) * LANE]
            r = q_ref[:, N_HEADS * LANE + h * LANE:N_HEADS * LANE + (h + 1) * LANE]
            rs = lax.rsqrt((_lanesum(n * n) + _lanesum(r * r)) * (1.0 / QK_DIM) + EPS)
            qf_ref[h, :, 0:LANE] = (((n * rs) * gqn_ref[...]) * scale).astype(qf_ref.dtype)
            rr = _rope_fwd((r * rs) * gqr_ref[...], c_v, s1_v, s2_v)
            qf_ref[h, :, LANE:HEAD_PAD] = (rr * scale).astype(qf_ref.dtype)

            n = kv_ref[:, h * 2 * LANE:h * 2 * LANE + LANE]
            rs = lax.rsqrt((_lanesum(n * n) + kr_ss) * (1.0 / QK_DIM) + EPS)
            kf_ref[h, :, 0:LANE] = ((n * rs) * gkn_ref[...]).astype(kf_ref.dtype)
            kf_ref[h, :, LANE:HEAD_PAD] = _rope_fwd((kr * rs) * gkr_ref[...], c_v, s1_v, s2_v).astype(kf_ref.dtype)
            vf_ref[h, :, 0:V_DIM] = kv_ref[:, h * 2 * LANE + LANE:(h + 1) * 2 * LANE].astype(vf_ref.dtype)
            vf_ref[h, :, V_DIM:] = jnp.ones((t, V_DIM), vf_ref.dtype)

    hspec = lambda w: pl.BlockSpec((N_HEADS, t, w), lambda i: (0, i, 0))
    whole = lambda a: pl.BlockSpec(a.shape, lambda i: (0, 0))
    return pl.pallas_call(
        body, name=name, grid=(s // t,),
        in_specs=[_rowspec(t, Q_LORA, SEG_QL[0] // Q_LORA), _rowspec(t, KV_LORA, SEG_KVL[0] // KV_LORA),
                  _rowspec(t, LANE, SEG_KR[0] // LANE), whole(w_q_up), whole(w_kv_up),
                  _vecspec(Q_LORA), _vecspec(KV_LORA),
                  _rowspec(t, LANE), _rowspec(t, LANE), _rowspec(t, LANE),
                  _vecspec(LANE), _vecspec(LANE), _vecspec(LANE), _vecspec(LANE)],
        out_specs=[_rowspec(t, Q_LORA), _rowspec(t, KV_LORA), _rowspec(t, wide), _rowspec(t, wide),
                   hspec(HEAD_PAD), hspec(HEAD_PAD), hspec(2 * V_DIM)],
        out_shape=[_sds((s, Q_LORA), MXU_DTYPE), _sds((s, KV_LORA), MXU_DTYPE), _sds((s, wide), F32),
                   _sds((s, wide), F32), _sds((N_HEADS, s, HEAD_PAD), MXU_DTYPE),
                   _sds((N_HEADS, s, HEAD_PAD), MXU_DTYPE), _sds((N_HEADS, s, 2 * V_DIM), MXU_DTYPE)],
        compiler_params=_cp(("parallel",)),
    )(z, z, z, w_q_up, w_kv_up, g_ql, g_kvl, c_t, s1_t, s2_t, gqn, gqr, gkn, gkr)


def _causal_mask(t):
    row = lax.broadcasted_iota(jnp.int32, (t, t), 0)
    col = lax.broadcasted_iota(jnp.int32, (t, t), 1)
    return col <= row


NEG = -1e30
LOG2E = math.log2(math.e)


def _flash_fwd(qf, kf, va, *, name):
    nh, s, dk = qf.shape
    dv = va.shape[-1] // 2
    t = min(ATT_T, s)
    n = s // t
    assert dv == LANE and t % LANE == 0

    def body(q_ref, k_ref, v_ref, o_ref, lse_ref, m_s, acc_s, s_buf):
        i = pl.program_id(1)
        m_s[...] = jnp.full(m_s.shape, NEG, F32)
        acc_s[...] = jnp.zeros(acc_s.shape, F32)

        def rows_of(j):
            return pl.ds(pl.multiple_of(j * t, t), t)

        def scores(qi, j):
            return lax.dot_general(q_ref[0, rows_of(qi), :], k_ref[0, rows_of(j), :], (((1,), (1,)), ((), ())),
                                   preferred_element_type=F32)

        def consume(j, slot, masked):
            sc = s_buf[slot]
            if masked:
                sc = jnp.where(_causal_mask(t), sc, NEG)
            m_prev = m_s[...]
            m_new = jnp.maximum(m_prev, jnp.max(sc, axis=-1, keepdims=True))
            alpha = jnp.exp2(m_prev - m_new)
            p = jnp.exp2(sc - jnp.tile(m_new, (1, t // LANE)))
            acc_s[...] = jnp.tile(alpha, (1, 2)) * acc_s[...] + jnp.dot(
                p.astype(MXU_DTYPE), v_ref[0, rows_of(j), :], preferred_element_type=F32)
            m_s[...] = m_new

        nxt = jnp.minimum(i + 1, n - 1)

        @pl.when(i == 0)
        def _():
            s_buf[2] = scores(0, 0)
            consume(0, 2, True)
            s_buf[2] = scores(nxt, 0)

        @pl.when(i > 0)
        def _():
            s_buf[1] = scores(i, 1)
            consume(0, 2, False)

            def pair(a, carry):
                s_buf[0] = scores(i, 2 * a + 2)
                consume(2 * a + 1, 1, False)
                s_buf[1] = scores(i, 2 * a + 3)
                consume(2 * a + 2, 0, False)
                return carry

            lax.fori_loop(0, (i - 1) // 2, pair, 0)

            @pl.when(i % 2 == 1)
            def _():
                s_buf[2] = scores(nxt, 0)
                consume(i, 1, True)

            @pl.when(i % 2 == 0)
            def _():
                s_buf[0] = scores(i, i)
                consume(i - 1, 1, False)
                s_buf[2] = scores(nxt, 0)
                consume(i, 0, True)

        den = acc_s[:, dv:]
        o_ref[...] = acc_s[:, :dv] / den
        lse_ref[0, 0] = jnp.transpose(m_s[...] + jnp.log2(den))[0:1, :]

    head = lambda h, i: (h, 0, 0)
    return pl.pallas_call(
        body, name=name, grid=(nh, n),
        in_specs=[pl.BlockSpec((1, s, dk), head), pl.BlockSpec((1, s, dk), head), pl.BlockSpec((1, s, 2 * dv), head)],
        out_specs=[pl.BlockSpec((t, dv), lambda h, i: (i, h)),
                   pl.BlockSpec((1, 1, 1, t), lambda h, i: (h, i, 0, 0))],
        out_shape=[_sds((s, nh * dv), F32), _sds((nh, n, 1, t), F32)],
        scratch_shapes=[pltpu.VMEM((t, LANE), F32), pltpu.VMEM((t, 2 * dv), F32), pltpu.VMEM((3, t, t), F32)],
        compiler_params=_cp(("arbitrary", "arbitrary")),
    )(qf, kf, va)


def _shifted_copies(ext_ref):
    rows = ext_ref.shape[1] - 8
    for s in range(1, 8):
        ext_ref[s, 0:rows, :] = ext_ref[0, s:s + rows, :]


def _windows(ext_ref, offsets, t_rows, lane0, lanes):
    for s in range(8):
        group = [o for o in offsets if o % 8 == s]
        if not group:
            continue
        lo, hi = min(group) - s, max(group) - s
        wide = ext_ref[s, pl.ds(lo, hi - lo + t_rows), lane0:lane0 + lanes]
        for o in group:
            yield o, wide[o - s - lo:o - s - lo + t_rows]


def _dw_taps(ext_ref, w_ref, row0, t_rows, lane0, lanes, first_off):
    acc = None
    for off, win in _windows(ext_ref, [row0 + first_off + k for k in range(CONV_K)], t_rows, lane0, lanes):
        k = off - row0 - first_off
        term = w_ref[k:k + 1, lane0:lane0 + lanes] * win
        acc = term if acc is None else acc + term
    return acc


CONV_RC = 32
CONV_LC = 256


def _conv_fwd(z, glu_b, dw_w, dw_b, ln_g, ln_b, w_pw, *, name):
    s = z.shape[0]
    t = min(CONV_T, s)
    c2 = 2 * D_CONV
    hb = t // HALO

    def body(zm_ref, zh_ref, gb_ref, w_ref, wb_ref, g_ref, b_ref, wpw_ref, u1_ref, u3_ref, u4_ref, ext):
        i = pl.program_id(0)

        def glu(zv):
            ci = zv + gb_ref[...]
            return ci[:, :D_CONV] * jax.nn.sigmoid(ci[:, D_CONV:])

        ext[0, HALO:, :] = glu(zm_ref[...])
        ext[0, 0:HALO, :] = jnp.where(i > 0, glu(zh_ref[...]), 0.0)
        _shifted_copies(ext)
        for rc in range(0, t, CONV_RC):
            for lc in range(0, D_CONV, CONV_LC):
                acc = _dw_taps(ext, w_ref, rc, CONV_RC, lc, CONV_LC, HALO - (CONV_K - 1))
                u1_ref[rc:rc + CONV_RC, lc:lc + CONV_LC] = acc + wb_ref[:, lc:lc + CONV_LC]
        u1 = u1_ref[...]
        mu = jnp.mean(u1, axis=-1, keepdims=True)
        cen = u1 - mu
        var = jnp.mean(cen * cen, axis=-1, keepdims=True)
        u2 = (cen * lax.rsqrt(var + EPS)) * g_ref[...] + b_ref[...]
        u3_ref[...] = _silu(u2).astype(u3_ref.dtype)
        u4_ref[...] = jnp.dot(u3_ref[...], wpw_ref[...], preferred_element_type=F32)

    return pl.pallas_call(
        body, name=name, grid=(s // t,),
        in_specs=[_rowspec(t, c2), pl.BlockSpec((HALO, c2), lambda i: (jnp.maximum(i * hb - 1, 0), 0)),
                  _vecspec(c2), pl.BlockSpec((HALO, D_CONV), lambda i: (0, 0)), _vecspec(D_CONV),
                  _vecspec(D_CONV), _vecspec(D_CONV), pl.BlockSpec((D_CONV, D_CONV), lambda i: (0, 0))],
        out_specs=[_rowspec(t, D_CONV), _rowspec(t, D_CONV), _rowspec(t, D_CONV)],
        out_shape=[_sds((s, D_CONV), F32), _sds((s, D_CONV), MXU_DTYPE), _sds((s, D_CONV), F32)],
        scratch_shapes=[pltpu.VMEM((8, t + HALO, D_CONV), F32)],
        compiler_params=_cp(("parallel",)),
    )(z, z, glu_b, dw_w, dw_b, ln_g, ln_b, w_pw)


def _gate_cat(o, z, u4m, b_pw, *, name):
    s = o.shape[0]
    t = min(2 * ROW_T, s)

    def body(o_ref, mg_ref, u4_ref, cg_ref, b_ref, cat_ref):
        cat_ref[:, :D_MLA] = (o_ref[...] * _silu(mg_ref[...])).astype(cat_ref.dtype)
        cat_ref[:, D_MLA:] = ((u4_ref[...] + b_ref[...]) * _silu(cg_ref[...])).astype(cat_ref.dtype)

    return pl.pallas_call(
        body, name=name, grid=(s // t,),
        in_specs=[_rowspec(t, D_MLA), _rowspec(t, D_MLA, SEG_MG[0] // D_MLA), _rowspec(t, D_CONV),
                  _rowspec(t, D_CONV, SEG_CG[0] // D_CONV), _vecspec(D_CONV)],
        out_specs=_rowspec(t, D_MLA + D_CONV), out_shape=_sds((s, D_MLA + D_CONV), MXU_DTYPE),
        compiler_params=_cp(("parallel",)),
    )(o, z, u4m, z, b_pw)


def _gated_residual_bwd(gx, y_ref, gate_ref, dy_ref, dgate_ref):
    dy_ref[...] = (gx * gate_ref[...]).astype(dy_ref.dtype)
    dgate_ref[...] += _colsum(gx * y_ref[...])


def _loss_head(xf, target, y, gate, *, name):
    s, d = xf.shape
    t = min(2 * ROW_T, s)

    def body(x_ref, t_ref, y_ref, gate_ref, gx_ref, loss_ref, dy_ref, dgate_ref):
        @pl.when(pl.program_id(0) == 0)
        def _():
            loss_ref[...] = jnp.zeros(loss_ref.shape, F32)
            dgate_ref[...] = jnp.zeros(dgate_ref.shape, F32)

        err = x_ref[...] - t_ref[...]
        gx = err * (1.0 / d)
        gx_ref[...] = gx
        loss_ref[...] += 0.5 * jnp.sum(_lanesum(err * err) * (1.0 / d), axis=0, keepdims=True)
        _gated_residual_bwd(gx, y_ref, gate_ref, dy_ref, dgate_ref)

    return pl.pallas_call(
        body, name=name, grid=(s // t,),
        in_specs=[_rowspec(t, d), _rowspec(t, d), _rowspec(t, d), _vecspec(d)],
        out_specs=[_rowspec(t, d), pl.BlockSpec((1, 1), lambda i: (0, 0)), _rowspec(t, d), _vecspec(d)],
        out_shape=[_sds((s, d), F32), _sds((1, 1), F32), _sds((s, d), MXU_DTYPE), _sds((1, d), F32)],
        compiler_params=_cp(("arbitrary",)),
    )(xf, target, y, gate)


def _acc_init(refs):
    @pl.when(pl.program_id(0) == 0)
    def _():
        for r in refs:
            r[...] = jnp.zeros(r.shape, r.dtype)


def _gate_bwd(dy, w_out, o, z, u4m, b_pw, *, name):
    s, d = dy.shape
    t = min(2 * ROW_T, s)
    gates = D_MLA + D_CONV
    assert SEG_CG[0] == SEG_MG[0] + D_MLA and SEG_MG[0] % gates == 0

    def body(dy_ref, w_ref, o_ref, mg_ref, u4_ref, cg_ref, b_ref,
             do_ref, delta_ref, du4_ref, gb_ref, dz_ref):
        _acc_init([gb_ref])
        dcat = lax.dot_general(dy_ref[...], w_ref[...], (((1,), (1,)), ((), ())), preferred_element_type=F32)
        dm, ov, mg = dcat[:, :D_MLA], o_ref[...], mg_ref[...]
        do = dm * _silu(mg)
        do_ref[...] = do.astype(do_ref.dtype)
        dz_ref[:, :D_MLA] = (dm * ov * _dsilu(mg)).astype(dz_ref.dtype)
        prod = do * ov
        for h in range(N_HEADS):
            rowsum = jnp.broadcast_to(_lanesum(prod[:, h * V_DIM:(h + 1) * V_DIM]), (t, LANE))
            delta_ref[h, 0] = jnp.transpose(rowsum)[0:1, :]
        dc, cg = dcat[:, D_MLA:], cg_ref[...]
        du4 = dc * _silu(cg)
        du4_ref[...] = du4.astype(du4_ref.dtype)
        dz_ref[:, D_MLA:] = (dc * (u4_ref[...] + b_ref[...]) * _dsilu(cg)).astype(dz_ref.dtype)
        gb_ref[...] += _colsum(du4)

    return pl.pallas_call(
        body, name=name, grid=(s // t,),
        in_specs=[_rowspec(t, d), pl.BlockSpec((gates, d), lambda i: (0, 0)), _rowspec(t, D_MLA),
                  _rowspec(t, D_MLA, SEG_MG[0] // D_MLA), _rowspec(t, D_CONV),
                  _rowspec(t, D_CONV, SEG_CG[0] // D_CONV), _vecspec(D_CONV)],
        out_specs=[_rowspec(t, D_MLA), pl.BlockSpec((N_HEADS, 1, 1, t), lambda i: (0, i, 0, 0)),
                   _rowspec(t, D_CONV), _vecspec(D_CONV), _rowspec(t, gates, SEG_MG[0] // gates)],
        out_shape=[_sds((s, D_MLA), MXU_DTYPE), _sds((N_HEADS, s // t, 1, t), F32),
                   _sds((s, D_CONV), MXU_DTYPE), _sds((1, D_CONV), F32), _sds((s, IN_PAD), MXU_DTYPE)],
        compiler_params=_cp(("arbitrary",)),
    )(dy, w_out, o, z, u4m, z, b_pw)


def _conv_bwd(du3, u1, z, dz, glu_b, dw_w, ln_g, ln_b, *, name):
    s = z.shape[0]
    t = min(CONV_T, s)
    c2 = 2 * D_CONV
    hb = t // HALO
    n_blk = s // t
    last_halo = s // HALO - 1

    def body(d3m_ref, d3h_ref, u1m_ref, u1h_ref, zm_ref, zh_ref, gb_ref, w_ref, g_ref, b_ref, dz_in_ref,
             dci_ref, gg_ref, gbn_ref, gwb_ref, ggb_ref, gw_ref, dext, uext, du0_s, gw_acc):
        i = pl.program_id(0)
        _acc_init([gg_ref, gbn_ref, gwb_ref, ggb_ref, gw_acc])

        def ln_bwd(d3, u1v):
            mu = jnp.mean(u1v, axis=-1, keepdims=True)
            cen = u1v - mu
            rstd = lax.rsqrt(jnp.mean(cen * cen, axis=-1, keepdims=True) + EPS)
            uh = cen * rstd
            d2 = d3 * _dsilu(uh * g_ref[...] + b_ref[...])
            dh = d2 * g_ref[...]
            d1 = rstd * (dh - jnp.mean(dh, axis=-1, keepdims=True) - uh * jnp.mean(dh * uh, axis=-1, keepdims=True))
            return d1, d2, uh

        d1, d2, uh = ln_bwd(d3m_ref[...], u1m_ref[...])
        gg_ref[...] += _colsum(d2 * uh)
        gbn_ref[...] += _colsum(d2)
        gwb_ref[...] += _colsum(d1)
        dext[0, 0:t, :] = d1
        d1h, _, _ = ln_bwd(d3h_ref[...], u1h_ref[...])
        dext[0, t:, :] = jnp.where(i < n_blk - 1, d1h, 0.0)
        _shifted_copies(dext)

        def glu_parts(zv):
            ci = zv + gb_ref[...]
            return ci[:, :D_CONV], jax.nn.sigmoid(ci[:, D_CONV:])

        val, sg = glu_parts(zm_ref[...])
        uext[0, HALO:, :] = val * sg
        valh, sgh = glu_parts(zh_ref[...])
        uext[0, 0:HALO, :] = jnp.where(i > 0, valh * sgh, 0.0)
        _shifted_copies(uext)

        for rc in range(0, t, CONV_RC):
            for lc in range(0, D_CONV, CONV_LC):
                acc = None
                for off, win in _windows(dext, [rc + k for k in range(CONV_K)], CONV_RC, lc, CONV_LC):
                    k = (CONV_K - 1) - (off - rc)
                    term = w_ref[k:k + 1, lc:lc + CONV_LC] * win
                    acc = term if acc is None else acc + term
                du0_s[rc:rc + CONV_RC, lc:lc + CONV_LC] = acc
                dchunk = dext[0, rc:rc + CONV_RC, lc:lc + CONV_LC]
                first = rc + HALO - (CONV_K - 1)
                for off, win in _windows(uext, [first + k for k in range(CONV_K)], CONV_RC, lc, CONV_LC):
                    k = off - first
                    pr = dchunk * win
                    part = pr[0:8]
                    for r8 in range(8, CONV_RC, 8):
                        part = part + pr[r8:r8 + 8]
                    gw_acc[k, :, lc:lc + CONV_LC] += part

        du0 = du0_s[...]
        dval = du0 * sg
        dgt = du0 * val * sg * (1.0 - sg)
        dci_ref[:, :D_CONV] = dval.astype(dci_ref.dtype)
        dci_ref[:, D_CONV:] = dgt.astype(dci_ref.dtype)
        ggb_ref[:, :D_CONV] += _colsum(dval)
        ggb_ref[:, D_CONV:] += _colsum(dgt)

        @pl.when(i == n_blk - 1)
        def _():
            gw_ref[...] = jnp.sum(gw_acc[...], axis=1)

    halo_next = lambda w: pl.BlockSpec((HALO, w), lambda i: (jnp.minimum((i + 1) * hb, last_halo), 0))
    return pl.pallas_call(
        body, name=name, grid=(n_blk,),
        in_specs=[_rowspec(t, D_CONV), halo_next(D_CONV), _rowspec(t, D_CONV), halo_next(D_CONV),
                  _rowspec(t, c2), pl.BlockSpec((HALO, c2), lambda i: (jnp.maximum(i * hb - 1, 0), 0)),
                  _vecspec(c2), pl.BlockSpec((HALO, D_CONV), lambda i: (0, 0)), _vecspec(D_CONV), _vecspec(D_CONV),
                  _ANY],
        out_specs=[_rowspec(t, c2, SEG_CI[0] // c2), _vecspec(D_CONV), _vecspec(D_CONV), _vecspec(D_CONV),
                   _vecspec(c2), pl.BlockSpec((HALO, D_CONV), lambda i: (0, 0))],
        out_shape=[_sds(dz.shape, dz.dtype), _sds((1, D_CONV), F32), _sds((1, D_CONV), F32), _sds((1, D_CONV), F32),
                   _sds((1, c2), F32), _sds((HALO, D_CONV), F32)],
        scratch_shapes=[pltpu.VMEM((8, t + HALO, D_CONV), F32), pltpu.VMEM((8, t + HALO, D_CONV), F32),
                        pltpu.VMEM((t, D_CONV), F32), pltpu.VMEM((HALO, 8, D_CONV), F32)],
        input_output_aliases={10: 0},
        compiler_params=_cp(("arbitrary",)),
    )(du3, du3, u1, u1, z, z, glu_b, dw_w, ln_g, ln_b, dz)


def _flash_bwd(qf, kf, va, do, lse_t, delta_t, *, name):
    nh, s, dk = qf.shape
    dv = va.shape[-1] // 2
    t = min(ATT_T, s)
    n = s // t
    nt = (((1,), (1,)), ((), ()))
    tn = (((0,), (0,)), ((), ()))

    def body(q_ref, do_ref, lse_ref, dl_ref, k_ref, v_ref, dq_ref, dk_ref, dv_ref,
             dk_s, dv_s, st_buf, dpt_buf):
        n_un = pl.program_id(1)
        j = n - 1 - n_un
        nxt = jnp.maximum(j - 1, 0)

        @pl.when(n_un == 0)
        def _():
            dq_ref[...] = jnp.zeros(dq_ref.shape, F32)

        dk_s[...] = jnp.zeros(dk_s.shape, F32)
        dv_s[...] = jnp.zeros(dv_s.shape, F32)

        def rows_at(blk):
            return pl.ds(pl.multiple_of(blk * t, t), t)

        def rows_of(b):
            return rows_at(n - 1 - b)

        k = k_ref[0, rows_at(j), :]

        def produce(kj, b, slot):
            rows = rows_of(b)
            st_buf[slot] = lax.dot_general(k_ref[0, rows_at(kj), :], q_ref[0, rows, :], nt,
                                           preferred_element_type=F32)
            dpt_buf[slot] = lax.dot_general(v_ref[0, rows_at(kj), 0:dv], do_ref[rows, :], nt,
                                            preferred_element_type=F32)

        def consume(b, slot, masked):
            i = n - 1 - b
            rows = rows_of(b)
            q, dov = q_ref[0, rows, :], do_ref[rows, :]
            pt = jnp.exp2(st_buf[slot] - lse_ref[0, i])
            if masked:
                key = lax.broadcasted_iota(jnp.int32, (t, t), 0)
                qry = lax.broadcasted_iota(jnp.int32, (t, t), 1)
                pt = jnp.where(key <= qry, pt, 0.0)
            dv_s[...] += jnp.dot(pt.astype(MXU_DTYPE), dov, preferred_element_type=F32)
            dst = (pt * (dpt_buf[slot] - dl_ref[0, i])).astype(MXU_DTYPE)
            dk_s[...] += jnp.dot(dst, q, preferred_element_type=F32)
            dq_ref[0, rows, :] += lax.dot_general(dst, k, tn, preferred_element_type=F32)

        @pl.when(n_un == 0)
        def _():
            produce(j, 0, 2)
            consume(0, 2, True)
            produce(nxt, 0, 2)

        @pl.when(n_un > 0)
        def _():
            produce(j, 1, 1)
            consume(0, 2, False)

            def pair(a, carry):
                produce(j, 2 * a + 2, 0)
                consume(2 * a + 1, 1, False)
                produce(j, 2 * a + 3, 1)
                consume(2 * a + 2, 0, False)
                return carry

            lax.fori_loop(0, (n_un - 1) // 2, pair, 0)

            @pl.when(n_un % 2 == 1)
            def _():
                produce(nxt, 0, 2)
                consume(n_un, 1, True)

            @pl.when(n_un % 2 == 0)
            def _():
                produce(j, n_un, 0)
                consume(n_un - 1, 1, False)
                produce(nxt, 0, 2)
                consume(n_un, 0, True)

        dk_ref[0] = dk_s[...]
        dv_ref[0] = dv_s[...]

    head = lambda h, j: (h, 0, 0)
    rowv = pl.BlockSpec((1, n, 1, t), lambda h, j: (h, 0, 0, 0))
    return pl.pallas_call(
        body, name=name, grid=(nh, n),
        in_specs=[pl.BlockSpec((1, s, dk), head),
                  pl.BlockSpec((s, dv), lambda h, j: (0, h)),
                  rowv, rowv,
                  pl.BlockSpec((1, s, dk), head),
                  pl.BlockSpec((1, s, 2 * dv), head)],
        out_specs=[pl.BlockSpec((1, s, dk), head),
                   pl.BlockSpec((1, t, dk), lambda h, g: (h, n - 1 - g, 0)),
                   pl.BlockSpec((1, t, dv), lambda h, g: (h, n - 1 - g, 0))],
        out_shape=[_sds((nh, s, dk), F32), _sds((nh, s, dk), F32), _sds((nh, s, dv), F32)],
        scratch_shapes=[pltpu.VMEM((t, dk), F32), pltpu.VMEM((t, dv), F32),
                        pltpu.VMEM((3, t, t), F32), pltpu.VMEM((3, t, t), F32)],
        compiler_params=_cp(("arbitrary", "arbitrary")),
    )(qf, do, lse_t, delta_t, kf, va)


def _mla_bwd(dqf, dkf, dvf, q_raw, kv, z, dz, qn, kn, w_q_up, w_kv_up, g_ql, g_kvl, c_t, s1_t, s2_t,
             gqn, gqr, gkn, gkr, *, name):
    s = q_raw.shape[0]
    t = min(ROW_T, s)
    scale = 1.0 / math.sqrt(QK_DIM)
    o_ql, o_kvl, o_kr = (seg[0] - SEG_LAT[0] for seg in (SEG_QL, SEG_KVL, SEG_KR))
    tn = (((0,), (0,)), ((), ()))
    nt = (((1,), (1,)), ((), ()))

    def body(dq_ref, dk_ref, dv_ref, q_ref, kv_ref, kr_ref, ql_ref, kvl_ref, qn_ref, kn_ref, wq_ref, wkv_ref,
             gq_ref, gk_ref, c_ref, s1_ref, s2_ref, gqn_ref, gqr_ref, gkn_ref, gkr_ref, dz_in_ref,
             dz_ref, gwq_ref, gwkv_ref, ggq_ref, ggk_ref, gql_ref, gkvl_ref, dqr_ref, dkv_ref):
        _acc_init([gwq_ref, gwkv_ref, ggq_ref, ggk_ref, gql_ref, gkvl_ref])
        c_v, s1_v, s2_v = c_ref[...], s1_ref[...], s2_ref[...]
        kr = kr_ref[...]
        kr_ss = _lanesum(kr * kr)
        dkr = jnp.zeros(kr.shape, F32)
        ggq_n = ggq_r = ggk_n = ggk_r = jnp.zeros((1, LANE), F32)

        def norm_bwd(n, r, rs, dyn, dyr, gn, gr):
            nh_, rh_ = n * rs, r * rs
            dnh, drh = dyn * gn, dyr * gr
            dot = (_lanesum(dnh * nh_) + _lanesum(drh * rh_)) * (1.0 / QK_DIM)
            return rs * (dnh - nh_ * dot), rs * (drh - rh_ * dot), _colsum(dyn * nh_), _colsum(dyr * rh_)

        for h in range(N_HEADS):
            n = q_ref[:, h * LANE:(h + 1) * LANE]
            r = q_ref[:, N_HEADS * LANE + h * LANE:N_HEADS * LANE + (h + 1) * LANE]
            rs = lax.rsqrt((_lanesum(n * n) + _lanesum(r * r)) * (1.0 / QK_DIM) + EPS)
            dyn = dq_ref[h, :, 0:LANE] * scale
            dyr = _rope_bwd(dq_ref[h, :, LANE:HEAD_PAD] * scale, c_v, s1_v, s2_v)
            dn, dr, g_n, g_r = norm_bwd(n, r, rs, dyn, dyr, gqn_ref[...], gqr_ref[...])
            dqr_ref[:, h * LANE:(h + 1) * LANE] = dn.astype(dqr_ref.dtype)
            dqr_ref[:, N_HEADS * LANE + h * LANE:N_HEADS * LANE + (h + 1) * LANE] = dr.astype(dqr_ref.dtype)
            ggq_n, ggq_r = ggq_n + g_n, ggq_r + g_r

            n = kv_ref[:, h * 2 * LANE:h * 2 * LANE + LANE]
            rs = lax.rsqrt((_lanesum(n * n) + kr_ss) * (1.0 / QK_DIM) + EPS)
            dyn = dk_ref[h, :, 0:LANE] * (1.0 / LOG2E)
            dyr = _rope_bwd(dk_ref[h, :, LANE:HEAD_PAD] * (1.0 / LOG2E), c_v, s1_v, s2_v)
            dn, dr, g_n, g_r = norm_bwd(n, kr, rs, dyn, dyr, gkn_ref[...], gkr_ref[...])
            dkv_ref[:, h * 2 * LANE:h * 2 * LANE + LANE] = dn.astype(dkv_ref.dtype)
            dkv_ref[:, h * 2 * LANE + LANE:(h + 1) * 2 * LANE] = dv_ref[h].astype(dkv_ref.dtype)
            dkr = dkr + dr
            ggk_n, ggk_r = ggk_n + g_n, ggk_r + g_r

        ggq_ref[:, 0:LANE] += ggq_n
        ggq_ref[:, LANE:] += ggq_r
        ggk_ref[:, 0:LANE] += ggk_n
        ggk_ref[:, LANE:] += ggk_r

        for d_ref, x_ref, w_ref, gw_ref, src, g_ref, off, gg_ref in (
                (dqr_ref, qn_ref, wq_ref, gwq_ref, ql_ref, gq_ref, o_ql, gql_ref),
                (dkv_ref, kn_ref, wkv_ref, gwkv_ref, kvl_ref, gk_ref, o_kvl, gkvl_ref)):
            dup = d_ref[...]
            gw_ref[...] += lax.dot_general(x_ref[...], dup, tn, preferred_element_type=F32)
            dy = lax.dot_general(dup, w_ref[...], nt, preferred_element_type=F32)
            v = src[...]
            r = lax.rsqrt(jnp.mean(v * v, axis=-1, keepdims=True) + EPS)
            vh = v * r
            dvh = dy * g_ref[...]
            dz_ref[:, off:off + v.shape[1]] = (
                r * (dvh - vh * jnp.mean(dvh * vh, axis=-1, keepdims=True))).astype(dz_ref.dtype)
            gg_ref[...] += _colsum(dy * vh)
        dz_ref[:, o_kr:o_kr + LANE] = dkr.astype(dz_ref.dtype)
        dz_ref[:, o_kr + LANE:] = jnp.zeros((t, SEG_LAT[1] - o_kr - LANE), dz_ref.dtype)

    hspec = lambda w: pl.BlockSpec((N_HEADS, t, w), lambda i: (0, i, 0))
    whole = lambda a: pl.BlockSpec(a.shape, lambda i: (0, 0))
    wide = 2 * N_HEADS * LANE
    return pl.pallas_call(
        body, name=name, grid=(s // t,),
        in_specs=[hspec(HEAD_PAD), hspec(HEAD_PAD), hspec(V_DIM), _rowspec(t, wide), _rowspec(t, wide),
                  _rowspec(t, LANE, SEG_KR[0] // LANE), _rowspec(t, Q_LORA, SEG_QL[0] // Q_LORA),
                  _rowspec(t, KV_LORA, SEG_KVL[0] // KV_LORA), _rowspec(t, Q_LORA), _rowspec(t, KV_LORA),
                  whole(w_q_up), whole(w_kv_up), _vecspec(Q_LORA), _vecspec(KV_LORA),
                  _rowspec(t, LANE), _rowspec(t, LANE), _rowspec(t, LANE),
                  _vecspec(LANE), _vecspec(LANE), _vecspec(LANE), _vecspec(LANE), _ANY],
        out_specs=[_rowspec(t, SEG_LAT[1], SEG_LAT[0] // SEG_LAT[1]), whole(w_q_up), whole(w_kv_up),
                   _vecspec(2 * LANE), _vecspec(2 * LANE), _vecspec(Q_LORA), _vecspec(KV_LORA)],
        out_shape=[_sds(dz.shape, dz.dtype), _sds(w_q_up.shape, F32), _sds(w_kv_up.shape, F32),
                   _sds((1, 2 * LANE), F32), _sds((1, 2 * LANE), F32), _sds((1, Q_LORA), F32),
                   _sds((1, KV_LORA), F32)],
        scratch_shapes=[pltpu.VMEM((t, wide), MXU_DTYPE), pltpu.VMEM((t, wide), MXU_DTYPE)],
        input_output_aliases={21: 0},
        compiler_params=_cp(("arbitrary",)),
    )(dqf, dkf, dvf, q_raw, kv, z, z, z, qn, kn, w_q_up, w_kv_up, g_ql, g_kvl, c_t, s1_t, s2_t,
      gqn, gqr, gkn, gkr, dz)


def _prenorm_bwd(dh, x, gxo, g, sc1p, below=None, *, name):
    s, d = x.shape
    t = min(2 * ROW_T if below is None else ROW_T, s)
    nb = 0 if below is None else 2

    def body(dh_ref, x_ref, gx_ref, g_ref, sc_ref, *rest):
        dx_ref, dsh_ref, dsc_ref, gg_ref = rest[nb:nb + 4]
        _acc_init([dsh_ref, dsc_ref, gg_ref])
        xv, dhv = x_ref[...], dh_ref[...]
        r = lax.rsqrt(jnp.mean(xv * xv, axis=-1, keepdims=True) + EPS)
        xn = xv * r
        dsh_ref[...] += _colsum(dhv)
        dsc_ref[...] += _colsum(dhv * (xn * g_ref[...]))
        dm = dhv * sc_ref[...]
        gg_ref[...] += _colsum(dm * xn)
        dxn = dm * g_ref[...]
        dx = gx_ref[...] + r * (dxn - xn * jnp.mean(dxn * xn, axis=-1, keepdims=True))
        dx_ref[...] = dx
        if below is not None:
            _acc_init([rest[nb + 5]])
            _gated_residual_bwd(dx, rest[0], rest[1], rest[nb + 4], rest[nb + 5])

    vec_out = [_vecspec(d), _vecspec(d), _vecspec(d)]
    vec_shape = [_sds((1, d), F32)] * 3
    return pl.pallas_call(
        body, name=name, grid=(s // t,),
        in_specs=[_rowspec(t, d), _rowspec(t, d), _rowspec(t, d), _vecspec(d), _vecspec(d)]
        + ([_rowspec(t, d), _vecspec(d)] if below is not None else []),
        out_specs=[_rowspec(t, d)] + vec_out + ([_rowspec(t, d), _vecspec(d)] if below is not None else []),
        out_shape=[_sds((s, d), F32)] + vec_shape
        + ([_sds((s, d), MXU_DTYPE), _sds((1, d), F32)] if below is not None else []),
        compiler_params=_cp(("arbitrary",)),
    )(dh, x, gxo, g, sc1p, *(below if below is not None else ()))


def _ada_fwd(c_all, ada_w, ada_b_cols, *, name):
    nl, d, cols = ada_w.shape

    def body(c_ref, w_ref, b_ref, o_ref):
        ca = _silu(c_ref[...]).astype(MXU_DTYPE)
        o_ref[0] = jnp.dot(ca, w_ref[0].astype(MXU_DTYPE), preferred_element_type=F32) + b_ref[0]

    return pl.pallas_call(
        body, name=name, grid=(nl,),
        in_specs=[pl.BlockSpec((N_DEV, d), lambda l: (0, 0)), pl.BlockSpec((1, d, cols), lambda l: (l, 0, 0)),
                  pl.BlockSpec((1, 1, cols), lambda l: (l, 0, 0))],
        out_specs=pl.BlockSpec((1, N_DEV, cols), lambda l: (l, 0, 0)),
        out_shape=_sds((nl, N_DEV, cols), F32),
        compiler_params=_cp(("parallel",)),
    )(c_all, ada_w, ada_b_cols)


def _ada_bwd(c_all_t, dmod_cols, *, name):
    nl, _, cols = dmod_cols.shape
    d = c_all_t.shape[0]

    def body(c_ref, dm_ref, o_ref):
        ca = _silu(c_ref[...]).astype(MXU_DTYPE)
        o_ref[0] = jnp.dot(ca, dm_ref[0].astype(MXU_DTYPE), preferred_element_type=F32)

    return pl.pallas_call(
        body, name=name, grid=(nl,),
        in_specs=[pl.BlockSpec((d, N_DEV), lambda l: (0, 0)), pl.BlockSpec((1, N_DEV, cols), lambda l: (l, 0, 0))],
        out_specs=pl.BlockSpec((1, d, cols), lambda l: (l, 0, 0)),
        out_shape=_sds((nl, d, cols), F32),
        compiler_params=_cp(("parallel",)),
    )(c_all_t, dmod_cols)


def _adamw_math(g, w, m, v):
    mn = ADAM_B1 * m + (1.0 - ADAM_B1) * g
    vn = ADAM_B2 * v + (1.0 - ADAM_B2) * (g * g)
    m_hat = mn / (1.0 - ADAM_B1 ** ADAM_STEP)
    v_hat = vn / (1.0 - ADAM_B2 ** ADAM_STEP)
    return -ADAM_LR * (m_hat / (jnp.sqrt(v_hat) + ADAM_EPS) + ADAM_WD * w), mn, vn


def _adamw_small(items, *, name):
    n = len(items)
    shapes = [it[1].shape for it in items]
    flat = lambda a, lead: a.reshape(lead + (-1, a.shape[-1]))
    operands = []
    for gp, w, m, v in items:
        operands += [flat(gp, (gp.shape[0],)), flat(w, ()), flat(m, ()), flat(v, ())]
    nparts = [it[0].shape[0] for it in items]

    def body(*refs):
        ins, outs = refs[:4 * n], refs[4 * n:]
        for i in range(n):
            g_ref, w_ref, m_ref, v_ref = ins[4 * i:4 * i + 4]
            g = g_ref[0].astype(F32)
            for p in range(1, nparts[i]):
                g = g + g_ref[p].astype(F32)
            outs[4 * i][...] = g
            outs[4 * i + 1][...], outs[4 * i + 2][...], outs[4 * i + 3][...] = _adamw_math(
                g, w_ref[...], m_ref[...], v_ref[...])

    out_shape = []
    for it in items:
        out_shape += [_sds(flat(it[1], ()).shape, F32)] * 4
    outs = pl.pallas_call(body, name=name, out_shape=out_shape, compiler_params=_cp())(*operands)
    return [tuple(o.reshape(shp) for o in outs[4 * i:4 * i + 4]) for i, shp in enumerate(shapes)]


def _adamw_layer(gparts, w, m, v, layer, prev, *, name):
    shape = w.shape
    nl, cols = shape[0], shape[-1]
    rows = w.size // cols // nl
    npart = gparts.shape[0]
    g3 = gparts.reshape(npart, rows, cols)
    w3, m3, v3 = (a.reshape(nl, rows, cols) for a in (w, m, v))
    fits = [t for t in range(min(rows, 256) // 8 * 8, 7, -8)
            if rows % t == 0 and npart * t * cols * g3.dtype.itemsize <= 2 * 1024 * 1024]
    t = fits[0] if fits else rows
    n_prev = 0 if prev is None else 4

    def body(g_ref, w_ref, m_ref, v_ref, *rest):
        go_ref, d_ref, mo_ref, vo_ref = rest[n_prev:]
        g = g_ref[0].astype(F32)
        for p in range(1, npart):
            g = g + g_ref[p].astype(F32)
        go_ref[0] = g
        d_ref[0], mo_ref[0], vo_ref[0] = _adamw_math(g, w_ref[0], m_ref[0], v_ref[0])

    spec = pl.BlockSpec((1, t, cols), lambda i: (layer, i, 0))
    outs = pl.pallas_call(
        body, name=name, grid=(rows // t,),
        in_specs=[pl.BlockSpec((npart, t, cols), lambda i: (0, i, 0)), spec, spec, spec] + [_ANY] * n_prev,
        out_specs=[spec] * 4, out_shape=[_sds((nl, rows, cols), F32)] * 4,
        input_output_aliases={4 + k: k for k in range(n_prev)},
        compiler_params=_cp(("parallel",)),
    )(g3, w3, m3, v3, *([] if prev is None else [a.reshape(nl, rows, cols) for a in prev]))
    return tuple(o.reshape(shape) for o in outs)


def _adamw(gparts, w, m, v, *, name):
    shape = w.shape
    cols = shape[-1]
    per_layer = isinstance(gparts, (list, tuple))
    nl = shape[0] if per_layer else 1
    rows = w.size // cols // nl
    glist = list(gparts) if per_layer else [gparts]
    npart = glist[0].shape[0]
    glist = [g.reshape(npart, rows, cols) for g in glist]
    w3, m3, v3 = (a.reshape(nl, rows, cols) for a in (w, m, v))
    budget = 2 * 1024 * 1024
    fits = [t for t in range(min(rows, 256) // 8 * 8, 7, -8)
            if rows % t == 0 and npart * t * cols * glist[0].dtype.itemsize <= budget]
    t = fits[0] if fits else rows
    nb = rows // t

    def body(*refs):
        g_refs = refs[:nl]
        w_ref, m_ref, v_ref, go_ref, d_ref, mo_ref, vo_ref, g_s = refs[nl:]
        layer = pl.program_id(0)
        for l in range(nl):
            @pl.when(layer == l)
            def _(l=l):
                g = g_refs[l][0].astype(F32)
                for p in range(1, npart):
                    g = g + g_refs[l][p].astype(F32)
                g_s[...] = g

        g = g_s[...]
        go_ref[0] = g
        d_ref[0], mo_ref[0], vo_ref[0] = _adamw_math(g, w_ref[0], m_ref[0], v_ref[0])

    def g_map(l):
        return lambda layer, i: (0, jnp.where(layer == l, i, jnp.where(layer < l, 0, nb - 1)), 0)

    spec = pl.BlockSpec((1, t, cols), lambda layer, i: (layer, i, 0))
    outs = pl.pallas_call(
        body, name=name, grid=(nl, nb),
        in_specs=[pl.BlockSpec((npart, t, cols), g_map(l)) for l in range(nl)] + [spec, spec, spec],
        out_specs=[spec] * 4, out_shape=[_sds((nl, rows, cols), F32)] * 4,
        scratch_shapes=[pltpu.VMEM((t, cols), F32)],
        compiler_params=_cp(("arbitrary", "arbitrary")),
    )(*glist, w3, m3, v3)
    return tuple(o.reshape(shape) for o in outs)


_ANY = pl.BlockSpec(memory_space=pl.ANY)


def _all_gather(blocks, *, name):
    na = len(blocks)

    def body(*refs):
        x_refs, out_refs = refs[:na], refs[na:2 * na]
        send_sems, recv_sems, local_sems = refs[2 * na:]
        x, y, c = lax.axis_index("x"), lax.axis_index("y"), lax.axis_index("c")
        me, sibling = (x, y, c), (x, y, 1 - c)
        chips = [(1 - x, y), (x, 1 - y), (1 - x, 1 - y)]

        def slot(a, px, py, pc):
            return out_refs[a].at[4 * px + 2 * py + pc]

        def copy(a, k, blk, to, src=None):
            return pltpu.make_async_remote_copy(
                src_ref=slot(a, *blk) if src is None else src, dst_ref=slot(a, *blk),
                send_sem=send_sems.at[7 * a + k], recv_sem=recv_sems.at[7 * a + k],
                device_id=to, device_id_type=MESH_ID)

        mine = [pltpu.make_async_copy(x_refs[a], slot(a, *me), local_sems.at[a]) for a in range(na)]
        for cp in mine:
            cp.start()
        first = []
        for a in range(na):
            first.append(copy(a, 0, me, sibling, src=x_refs[a]))
            first += [copy(a, 1 + j, me, (*chip, c), src=x_refs[a]) for j, chip in enumerate(chips)]
        for cp in first:
            cp.start()
        passed = []
        for a in range(na):
            for j, chip in enumerate(chips):
                copy(a, 1 + j, (*chip, c), me).wait_recv()
                fwd = copy(a, 4 + j, (*chip, c), sibling)
                fwd.start()
                passed.append(fwd)
        for a in range(na):
            copy(a, 0, sibling, me).wait_recv()
            for j, chip in enumerate(chips):
                copy(a, 4 + j, (*chip, 1 - c), me).wait_recv()
        for cp in first + passed:
            cp.wait_send()
        for cp in mine:
            cp.wait()

    outs = pl.pallas_call(
        body, name=name, in_specs=[_ANY] * na, out_specs=[_ANY] * na,
        out_shape=[_sds((N_DEV,) + b.shape, b.dtype) for b in blocks],
        scratch_shapes=[pltpu.SemaphoreType.DMA((7 * na,)), pltpu.SemaphoreType.DMA((7 * na,)),
                        pltpu.SemaphoreType.DMA((na,))],
    )(*blocks)
    return list(outs)


_HBM = pl.BlockSpec(memory_space=pltpu.HBM)
_SEM = pl.BlockSpec(memory_space=pltpu.SEMAPHORE)
_EFFECT = pltpu.SideEffectType.DATAFLOW_SIDE_EFFECTING


def _peers(x, y, c):
    out = []
    for k in range(1, N_DEV):
        out.append((1 - x if k & 4 else x, 1 - y if k & 2 else y, 1 - c if k & 1 else c))
    return out


def _own_slots(srcs, scatter, *, name, after=None):
    na = len(srcs)
    n_extra = 0 if after is None else 1
    me = (4 * lax.axis_index("x") + 2 * lax.axis_index("y") + lax.axis_index("c")).astype(jnp.int32).reshape(1)

    def body(me_ref, *refs):
        in_refs, out_refs = refs[:na], refs[na + n_extra:]
        for a in range(na):
            out_refs[a][0] = in_refs[a][0] if scatter else in_refs[a][...]

    def slot_spec(shard):
        zeros = (0,) * len(shard)
        return pl.BlockSpec((1,) + tuple(shard), lambda i, me_ref: (me_ref[0],) + zeros)

    def whole_spec(shape):
        zeros = (0,) * len(shape)
        return pl.BlockSpec(tuple(shape), lambda i, me_ref: zeros)

    shards = [s.shape[1:] if scatter else s.shape for s in srcs]
    in_specs = [slot_spec(sh) if scatter else whole_spec(sh) for sh in shards] + [_ANY] * n_extra
    outs = pl.pallas_call(
        body, name=name,
        grid_spec=pltpu.PrefetchScalarGridSpec(
            num_scalar_prefetch=1, grid=(1,), in_specs=in_specs, out_specs=[slot_spec(sh) for sh in shards]),
        out_shape=[_sds((N_DEV,) + tuple(sh), s.dtype) for sh, s in zip(shards, srcs)],
        compiler_params=_cp(("arbitrary",)),
    )(me, *srcs, *([] if after is None else [after]))
    return list(outs)


_N_COPIES = dict(scatter=7, gather=7, chips=4, forward=3)


def _exchange_copies(src_refs, land_refs, send_sems, recv_sems, mode):
    x, y, c = lax.axis_index("x"), lax.axis_index("y"), lax.axis_index("c")
    me = 4 * x + 2 * y + c
    nc = _N_COPIES[mode]
    chips = [(1 - x, y), (x, 1 - y), (1 - x, 1 - y)]
    cps = []
    for a in range(len(land_refs)):
        if mode in ("scatter", "gather"):
            plan = [((src_refs[a].at[4 * px + 2 * py + pc] if mode == "scatter" else src_refs[a]),
                     land_refs[a].at[me], (px, py, pc)) for px, py, pc in _peers(x, y, c)]
        elif mode == "chips":
            plan = [(src_refs[a], land_refs[a].at[me], to) for to in [(x, y, 1 - c)] + [(*ch, c) for ch in chips]]
        else:
            plan = [(land_refs[a].at[4 * px + 2 * py + c], land_refs[a].at[4 * px + 2 * py + c], (x, y, 1 - c))
                    for px, py in chips]
        for k, (src, dst, to) in enumerate(plan):
            cps.append(pltpu.make_async_remote_copy(
                src_ref=src, dst_ref=dst, send_sem=send_sems.at[nc * a + k], recv_sem=recv_sems.at[nc * a + k],
                device_id=to, device_id_type=MESH_ID))
    return cps


def _exchange_start(srcs, lands, mode, *, name):
    ns, nz = len(srcs), len(lands)
    nsem = _N_COPIES[mode] * nz

    def body(*refs):
        src_refs, land_refs = refs[:ns], refs[ns:ns + nz]
        send_sems, recv_sems = refs[ns + nz], refs[ns + nz + 1]
        token = refs[-1]
        for cp in _exchange_copies(src_refs, land_refs, send_sems, recv_sems, mode):
            cp.start()
        token[...] = jnp.zeros(token.shape, token.dtype)

    hbm = lambda a: pltpu.HBM(a.shape, a.dtype)
    outs = pl.pallas_call(
        body, name=name,
        out_shape=(pltpu.SemaphoreType.DMA((nsem,)), pltpu.SemaphoreType.DMA((nsem,)),
                   *[hbm(a) for a in srcs], *[hbm(a) for a in lands], _sds((8, LANE), F32)),
        in_specs=[_HBM] * (ns + nz),
        out_specs=(_SEM, _SEM, *[_HBM] * (ns + nz), pl.BlockSpec(memory_space=pltpu.VMEM)),
        input_output_aliases={i: 2 + i for i in range(ns + nz)},
        compiler_params=pltpu.CompilerParams(has_side_effects=_EFFECT),
    )(*[pltpu.with_memory_space_constraint(a, pltpu.HBM) for a in list(srcs) + list(lands)])
    return outs[0], outs[1], list(outs[2:2 + ns]), list(outs[2 + ns:2 + ns + nz]), outs[-1]


def _exchange_wait(send_sems, recv_sems, srcs, lands, after, mode, *, name):
    ns, nz = len(srcs), len(lands)

    def body(*refs):
        src_refs, land_refs = refs[:ns], refs[ns:ns + nz]
        s_sems, r_sems = refs[ns + nz], refs[ns + nz + 1]
        for cp in _exchange_copies(src_refs, land_refs, s_sems, r_sems, mode):
            cp.wait_send()
            cp.wait_recv()

    hbm = lambda a: pltpu.HBM(a.shape, a.dtype)
    outs = pl.pallas_call(
        body, name=name,
        out_shape=(*[hbm(a) for a in srcs], *[hbm(a) for a in lands]),
        in_specs=[_HBM] * (ns + nz) + [_SEM, _SEM, _ANY],
        out_specs=tuple([_HBM] * (ns + nz)),
        input_output_aliases={i: i for i in range(ns + nz)},
        compiler_params=pltpu.CompilerParams(has_side_effects=_EFFECT),
    )(*srcs, *lands, send_sems, recv_sems, after)
    return list(outs[ns:])


_WIN_SEGS = (("ql", 0, Q_LORA, SEG_QL[0]), ("kvl", Q_LORA, KV_LORA, SEG_KVL[0]),
             ("kr", Q_LORA + KV_LORA, ROPE, SEG_KR[0]), ("mg", Q_LORA + KV_LORA + ROPE, D_MLA, SEG_MG[0]),
             ("ci", Q_LORA + KV_LORA + ROPE + D_MLA, 2 * D_CONV, SEG_CI[0]),
             ("cg", Q_LORA + KV_LORA + ROPE + D_MLA + 2 * D_CONV, D_CONV, SEG_CG[0]))
_WIN_SHARD = IN_COLS // N_DEV


def _win_pieces():
    out = []
    for _, o, n, new in _WIN_SEGS:
        for j in range(N_DEV):
            lo, hi = max(o, j * _WIN_SHARD), min(o + n, (j + 1) * _WIN_SHARD)
            if lo < hi:
                out.append((j, lo - j * _WIN_SHARD, new + lo - o, hi - lo))
    return out


WIN_T = 512


def _win_assemble(w_all, *, name):
    d = w_all.shape[2]
    t = min(WIN_T, d)
    pieces = sorted(_win_pieces(), key=lambda p: p[2])
    assert all(lo % 8 == 0 and n % 8 == 0 for _, lo, _, n in pieces)

    def body(w_ref, o_ref):
        rows = [w_ref[j].astype(F32)[lo:lo + n, :] for j, lo, _, n in pieces]
        rows.append(jnp.zeros((IN_PAD - (SEG_KR[0] + ROPE), t), F32))
        o_ref[...] = jnp.concatenate(rows, axis=0).astype(o_ref.dtype)

    return pl.pallas_call(
        body, name=name, grid=(d // t,),
        in_specs=[pl.BlockSpec((N_DEV, _WIN_SHARD, t), lambda i: (0, 0, i))],
        out_specs=pl.BlockSpec((IN_PAD, t), lambda i: (0, i)), out_shape=_sds((IN_PAD, d), w_all.dtype),
        compiler_params=_cp(("parallel",)),
    )(w_all)


def _win_split(grad, *, name):
    d = grad.shape[1]
    t = min(WIN_T, d)
    by_shard = [sorted([p for p in _win_pieces() if p[0] == j], key=lambda p: p[1]) for j in range(N_DEV)]

    def body(g_ref, o_ref):
        for j in range(N_DEV):
            rows = [g_ref[new:new + n, :] for _, _, new, n in by_shard[j]]
            o_ref[j] = jnp.concatenate(rows, axis=0).astype(o_ref.dtype)

    return pl.pallas_call(
        body, name=name, grid=(d // t,),
        in_specs=[pl.BlockSpec((IN_PAD, t), lambda i: (0, i))],
        out_specs=pl.BlockSpec((N_DEV, _WIN_SHARD, t), lambda i: (0, 0, i)),
        out_shape=_sds((N_DEV, _WIN_SHARD, d), WIRE_DTYPE),
        compiler_params=_cp(("parallel",)),
    )(grad)


def _cols_to_shards(a):
    r, n = a.shape
    return a.reshape(r, N_DEV, n // N_DEV).transpose(1, 0, 2)


def _shards_to_cols(a):
    nd, r, w = a.shape
    return a.transpose(1, 0, 2).reshape(r, nd * w)


def _qup_permute(w):
    w3 = w.reshape(w.shape[0], N_HEADS, QK_DIM)
    nope = w3[:, :, :NOPE].reshape(w.shape[0], N_HEADS * NOPE)
    rope = jnp.pad(w3[:, :, NOPE:], ((0, 0), (0, 0), (0, LANE - ROPE))).reshape(w.shape[0], N_HEADS * LANE)
    return jnp.concatenate([nope, rope], axis=1)


def _qup_unpermute(g):
    r = g.shape[0]
    nope = g[:, :N_HEADS * NOPE].reshape(r, N_HEADS, NOPE)
    rope = g[:, N_HEADS * NOPE:].reshape(r, N_HEADS, LANE)[:, :, :ROPE]
    return jnp.concatenate([nope, rope], axis=2).reshape(r, N_HEADS * QK_DIM)


def _norm_tiles(g):
    return g[:NOPE].reshape(1, LANE), jnp.pad(g[NOPE:], (0, LANE - ROPE)).reshape(1, LANE)


def _rope_tiles(positions):
    inv_freq = 1.0 / (ROPE_THETA ** (jnp.arange(0, ROPE, 2, dtype=F32) / ROPE))
    ang = positions.astype(F32)[:, None] * inv_freq
    cos, sin = jnp.cos(ang), jnp.sin(ang)
    zq = jnp.zeros_like(cos)
    c_t = jnp.concatenate([cos, cos, zq, zq], axis=1)
    s1_t = jnp.concatenate([-sin, zq, zq, zq], axis=1)
    s2_t = jnp.concatenate([zq, sin, zq, zq], axis=1)
    return c_t, s1_t, s2_t


_BIG = ("w_in", "w_q_up", "w_kv_up", "w_pw", "w_out")
_COL_SHARDED = ("w_q_up", "w_kv_up")


def _unpack_rows(buf, shapes):
    out, r0 = [], 0
    lead = buf.shape[:-2]
    for shp in shapes:
        n = math.prod(shp) // LANE
        out.append(buf[..., r0:r0 + n, :].reshape(lead + tuple(shp)))
        r0 += n
    return out


_SMALL = (("dmod", 3 * D_MODEL), ("norm_g", D_MODEL), ("q_lat_g", Q_LORA), ("kv_lat_g", KV_LORA),
          ("q_norm_g", 2 * LANE), ("k_norm_g", 2 * LANE), ("glu_b", 2 * D_CONV), ("dw_w", HALO * D_CONV),
          ("dw_b", D_CONV), ("conv_ln_g", D_CONV), ("conv_ln_b", D_CONV), ("b_pw", D_CONV))


def _layer_fwd(x, p, rope, l, late=None):
    n = lambda s: f"{s}_l{l}"
    c_t, s1_t, s2_t = rope
    h = _prenorm(x, p["norm_g"], p["shift"], p["sc1p"], name=n("prenorm"))
    z = _mm(h, p["w_in"], tb=True, name=n("in_proj"), tm=1024, tn=IN_TILE, n_outer=True)
    if late is not None:
        p = {**p, **late(z)}
    qn, kn, q_raw, kv, qf, kf, vf = _mla_pre(z, p["w_q_up"], p["w_kv_up"], p["q_lat_g"], p["kv_lat_g"],
                                             c_t, s1_t, s2_t, *p["qk_tiles"], name=n("mla_pre"))
    o, lse = _flash_fwd(qf, kf, vf, name=n("flash_fwd"))
    u1, u3, u4m = _conv_fwd(z, p["glu_b"], p["dw_w"], p["dw_b"], p["conv_ln_g"], p["conv_ln_b"], p["w_pw"],
                            name=n("conv_fwd"))
    cat = _gate_cat(o, z, u4m, p["b_pw"], name=n("gate_cat"))
    y, x_next = _mm(cat, p["w_out"], name=n("out_proj"), tm=1024, tn=1024, residual=(x, p["gate"]))
    saved = dict(x=x, h=h, z=z, qn=qn, kn=kn, q_raw=q_raw, kv=kv, qf=qf, kf=kf, vf=vf, o=o, lse=lse,
                 u1=u1, u3=u3, u4m=u4m, cat=cat, y=y)
    return x_next, saved, p


def _layer_bwd(gxo, dy, dgate, p, sv, rope, l, below=None, hook_rest=None, hook_w_in=None):
    n = lambda s: f"{s}_l{l}"
    c_t, s1_t, s2_t = rope
    z = sv["z"]
    g_w_out = _mm(sv["cat"], dy, ta=True, name=n("g_w_out"), tm=1024, tn=1024, after=p.get("after_start"))
    do, delta, du4, g_b_pw, dz = _gate_bwd(dy, p["w_out"], sv["o"], z, sv["u4m"], p["b_pw"], name=n("gate_bwd"))
    g_w_pw = _mm(sv["u3"], du4, ta=True, name=n("g_w_pw"), tm=1024, tn=1024)
    du3 = _mm(du4, p["w_pw"], tb=True, name=n("d_u3"), tn=1024)
    dz, g_ln_g, g_ln_b, g_dw_b, g_glu_b, g_dw_w = _conv_bwd(
        du3, sv["u1"], z, dz, p["glu_b"], p["dw_w"], p["conv_ln_g"], p["conv_ln_b"], name=n("conv_bwd"))
    dqf, dkf, dvf = _flash_bwd(sv["qf"], sv["kf"], sv["vf"], do, sv["lse"], delta.reshape(sv["lse"].shape),
                               name=n("flash_bwd"))
    dz, g_w_q_up, g_w_kv_up, g_qn, g_kn, g_ql, g_kvl = _mla_bwd(
        dqf, dkf, dvf, sv["q_raw"], sv["kv"], z, dz, sv["qn"], sv["kn"], p["w_q_up"], p["w_kv_up"],
        p["q_lat_g"], p["kv_lat_g"], c_t, s1_t, s2_t, *p["qk_tiles"], name=n("mla_bwd"))
    big = dict(w_q_up=g_w_q_up, w_kv_up=g_w_kv_up, w_pw=g_w_pw, w_out=g_w_out)
    after = None if hook_rest is None else hook_rest(big)
    g_w_in = _mm(dz, sv["h"], ta=True, name=n("g_w_in"), tm=1024, tn=1024, after=after)
    big["w_in"] = g_w_in
    after = None if hook_w_in is None else hook_w_in(g_w_in)
    dh = _mm(dz, p["w_in"], name=n("d_h"), tn=1024, after=after)
    dx, dshift, dscale, g_norm, *down = _prenorm_bwd(dh, sv["x"], gxo, p["norm_g"], p["sc1p"], below,
                                                     name=n("prenorm_bwd"))
    small = dict(dmod=jnp.concatenate([dshift, dscale, dgate], axis=1), norm_g=g_norm, q_lat_g=g_ql, kv_lat_g=g_kvl,
                 q_norm_g=g_qn, k_norm_g=g_kn, glu_b=g_glu_b, dw_w=g_dw_w, dw_b=g_dw_b,
                 conv_ln_g=g_ln_g, conv_ln_b=g_ln_b, b_pw=g_b_pw)
    return (dx, *down), big, small


def _layer_params(l, full, mod_l, small):
    d = D_MODEL
    row = lambda a: a.reshape(1, -1)
    shift, scale, gate = mod_l[:, :d], mod_l[:, d:2 * d], mod_l[:, 2 * d:]
    dw_w = jnp.pad(full["dw_w"][l], ((0, HALO - CONV_K), (0, 0)))
    return dict(
        shift=shift, sc1p=1.0 + scale, gate=gate, norm_g=row(small["norm_g"][l]),
        **{k: full[k][l] for k in _BIG if k in full}, dw_w=dw_w,
        q_lat_g=row(small["q_lat_g"][l]), kv_lat_g=row(small["kv_lat_g"][l]),
        qk_tiles=_norm_tiles(small["q_norm_g"][l]) + _norm_tiles(small["k_norm_g"][l]),
        glu_b=row(small["glu_b"][l]), dw_b=row(small["dw_b"][l]), conv_ln_g=row(small["conv_ln_g"][l]),
        conv_ln_b=row(small["conv_ln_b"][l]), b_pw=row(small["b_pw"][l]))


def kernel(x, c, positions, ada_w, ada_b, norm_g, w_in, q_lat_g, w_q_up, kv_lat_g, w_kv_up, q_norm_g, k_norm_g, glu_b, dw_w, dw_b, conv_ln_g, conv_ln_b, w_pw, b_pw, w_out, loss_target, m_ada_w, m_ada_b, m_norm_g, m_w_in, m_q_lat_g, m_w_q_up, m_kv_lat_g, m_w_kv_up, m_q_norm_g, m_k_norm_g, m_glu_b, m_dw_w, m_dw_b, m_conv_ln_g, m_conv_ln_b, m_w_pw, m_b_pw, m_w_out, v_ada_w, v_ada_b, v_norm_g, v_w_in, v_q_lat_g, v_w_q_up, v_kv_lat_g, v_w_kv_up, v_q_norm_g, v_k_norm_g, v_glu_b, v_dw_w, v_dw_b, v_conv_ln_g, v_conv_ln_b, v_w_pw, v_b_pw, v_w_out):
    names = ("ada_w", "ada_b", "norm_g", "w_in", "q_lat_g", "w_q_up", "kv_lat_g", "w_kv_up", "q_norm_g",
             "k_norm_g", "glu_b", "dw_w", "dw_b", "conv_ln_g", "conv_ln_b", "w_pw", "b_pw", "w_out")
    w_loc = dict(zip(names, (ada_w, ada_b, norm_g, w_in, q_lat_g, w_q_up, kv_lat_g, w_kv_up, q_norm_g, k_norm_g,
                             glu_b, dw_w, dw_b, conv_ln_g, conv_ln_b, w_pw, b_pw, w_out)))
    m_loc = dict(zip(names, (m_ada_w, m_ada_b, m_norm_g, m_w_in, m_q_lat_g, m_w_q_up, m_kv_lat_g, m_w_kv_up,
                             m_q_norm_g, m_k_norm_g, m_glu_b, m_dw_w, m_dw_b, m_conv_ln_g, m_conv_ln_b, m_w_pw,
                             m_b_pw, m_w_out)))
    v_loc = dict(zip(names, (v_ada_w, v_ada_b, v_norm_g, v_w_in, v_q_lat_g, v_w_q_up, v_kv_lat_g, v_w_kv_up,
                             v_q_norm_g, v_k_norm_g, v_glu_b, v_dw_w, v_dw_b, v_conv_ln_g, v_conv_ln_b, v_w_pw,
                             v_b_pw, v_w_out)))
    nl, d = N_LAYERS, D_MODEL
    me = 4 * lax.axis_index("x") + 2 * lax.axis_index("y") + lax.axis_index("c")
    x2, tgt = x[0], loss_target[0]
    ada_cols = ada_w.shape[-1]

    tr = lambda a: jnp.swapaxes(a, 1, 2)
    w_loc, m_loc, v_loc = ({**dd, "w_in": tr(dd["w_in"])} for dd in (w_loc, m_loc, v_loc))
    w_in0 = [w_loc["w_in"][0].astype(WIRE_DTYPE)]
    fly_c = _exchange_start(w_in0, _own_slots(w_in0, False, name="own_w_in_l0"), "chips", name="gather_start_w_in_l0")
    held = dict(c=c, positions=positions, ada_b=ada_b, norm_g=norm_g, q_lat_g=q_lat_g, kv_lat_g=kv_lat_g,
                q_norm_g=q_norm_g, k_norm_g=k_norm_g, glu_b=glu_b, dw_w=dw_w, dw_b=dw_b, conv_ln_g=conv_ln_g,
                conv_ln_b=conv_ln_b, b_pw=b_pw, big={k: w_loc[k] for k in _BIG})
    tok_c, held = lax.optimization_barrier((fly_c[4], held))
    c, positions, ada_b, norm_g, q_lat_g, kv_lat_g, q_norm_g, k_norm_g, glu_b, dw_w, dw_b, conv_ln_g, conv_ln_b, b_pw = (
        held[k] for k in ("c", "positions", "ada_b", "norm_g", "q_lat_g", "kv_lat_g", "q_norm_g", "k_norm_g", "glu_b",
                          "dw_w", "dw_b", "conv_ln_g", "conv_ln_b", "b_pw"))
    wire = {k: held["big"][k].astype(WIRE_DTYPE) for k in _BIG}

    dw_pad = jnp.pad(dw_w, ((0, 0), (0, HALO - CONV_K), (0, 0)))
    c_rows = c.reshape(d // LANE, LANE) + tok_c[0:1, :]
    c_all, dw_all = _all_gather([c_rows, dw_pad], name="gather_c")
    c_all = c_all.reshape(N_DEV, d)
    ada_b_cols = lax.dynamic_slice_in_dim(ada_b, me * ada_cols, ada_cols, axis=1).reshape(nl, 1, ada_cols)
    mod_cols = _ada_fwd(c_all, ada_w, ada_b_cols, name="ada_fwd")
    mod_all = _all_gather([mod_cols], name="gather_mod")[0]
    mod_me = lax.dynamic_index_in_dim(mod_all, me, axis=2, keepdims=False)
    mod = mod_me.transpose(1, 0, 2).reshape(nl, 1, N_DEV * ada_cols)

    from_chips = _exchange_wait(*fly_c[:4], mod, "chips", name="gather_wait_w_in_l0")
    fly_f = _exchange_start([], from_chips, "forward", name="forward_start_w_in_l0")
    w_in_all0 = _exchange_wait(*fly_f[:4], fly_f[4], "forward", name="forward_wait_w_in_l0")[0]
    rest0 = [wire[k][0] for k in _BIG[1:]]
    fly_r0, fly_w1 = {}, {}
    fly_r0["x"] = _exchange_start(rest0, _own_slots(rest0, False, name="own_weights_l0_rest", after=w_in_all0),
                                  "gather", name="gather_start_l0_rest")

    def layout_rest(parts):
        return dict(w_q_up=_qup_permute(_shards_to_cols(parts[0])), w_kv_up=_shards_to_cols(parts[1]),
                    w_pw=parts[2].reshape(D_CONV, D_CONV), w_out=parts[3].reshape(D_MLA + D_CONV, d))

    small_in = dict(norm_g=norm_g, q_lat_g=q_lat_g, kv_lat_g=kv_lat_g, q_norm_g=q_norm_g, k_norm_g=k_norm_g,
                    glu_b=glu_b, dw_b=dw_b, conv_ln_g=conv_ln_g, conv_ln_b=conv_ln_b, b_pw=b_pw)
    dw_full = [_shards_to_cols(dw_all[:, l])[:CONV_K] for l in range(nl)]
    rope = _rope_tiles(positions[0])

    def layer_params(l, w_in_all, rest, mod_l):
        full = dict(dw_w=dw_full)
        if w_in_all is not None:
            full["w_in"] = {l: _win_assemble(w_in_all, name=f"w_in_assemble_l{l}")}
        if rest is not None:
            full.update({k: {l: a} for k, a in layout_rest(rest).items()})
        return _layer_params(l, full, mod_l, small_in)

    src1 = [wire[k][1] for k in _BIG]
    fly_w1["x"] = _exchange_start(src1, _own_slots(src1, False, name="own_weights_l1", after=fly_r0["x"][4]), "gather",
                                  name="gather_start_l1")

    def late_l0(z):
        return layout_rest(_exchange_wait(*fly_r0["x"][:4], z, "gather", name="gather_wait_l0_rest"))

    params, saved = [None] * nl, [None] * nl
    p0 = layer_params(0, w_in_all0, None, mod[0] + fly_w1["x"][4][0, 0])
    xs, saved[0], params[0] = _layer_fwd(x2, p0, rope, 0, late=late_l0)
    parts1 = _exchange_wait(*fly_w1["x"][:4], xs, "gather", name="gather_wait_l1")
    params[1] = layer_params(1, parts1[0], parts1[1:], mod[1])
    xs, saved[1], _ = _layer_fwd(xs, params[1], rope, 1)
    gx, loss_part, dy, dgate = _loss_head(xs, tgt, saved[1]["y"], params[1]["gate"], name="loss_head")
    loss = lax.psum(loss_part[0, 0], ("x", "y", "c"))

    def shard_major(k, g):
        if k == "w_q_up":
            g = _qup_unpermute(g)
        if k in _COL_SHARDED:
            return _cols_to_shards(g)
        return g.reshape((N_DEV, g.shape[0] // N_DEV, g.shape[1]))

    def scatter_start(send, tag):
        lands = _own_slots(send, True, name=f"own_grads_{tag}")
        return _exchange_start(send, lands, "scatter", name=f"scatter_start_{tag}")

    def wire_rest(big):
        return [shard_major(k, big[k]).astype(WIRE_DTYPE) for k in _BIG[1:]]

    big_g, small_g, flying = [None] * nl, [None] * nl, {}
    (gx, dy, dgate), big_g[1], small_g[1] = _layer_bwd(gx, dy, dgate, params[1], saved[1], rope, 1,
                                                       below=(saved[0]["y"], params[0]["gate"]))
    flying["l1"] = scatter_start([_win_split(big_g[1]["w_in"], name="w_in_split_l1")] + wire_rest(big_g[1]), "l1")
    p0 = dict(params[0], after_start=flying["l1"][4], b_pw=params[0]["b_pw"] + flying["l1"][4][0, 0])

    def start_rest_l0(big):
        flying["l0_rest"] = scatter_start(wire_rest(big), "l0_rest")
        return flying["l0_rest"][4]

    res, arrived = {}, [None] * nl

    def start_w_in_l0(g_w_in):
        flying["l0_w_in"] = scatter_start([_win_split(g_w_in, name="w_in_split_l0")], "l0_w_in")
        tok = flying["l0_w_in"][4]
        arrived[1] = _exchange_wait(*flying["l1"][:4], tok, "scatter", name="scatter_wait_l1")
        arrived[0] = [None] + _exchange_wait(*flying["l0_rest"][:4], tok, "scatter", name="scatter_wait_l0_rest")
        for i, k in enumerate(_BIG):
            if i > 0:
                res[k] = _adamw([arrived[l][i] for l in range(nl)], w_loc[k], m_loc[k], v_loc[k], name=f"adamw_{k}")
        res["w_in_l1"] = _adamw_layer(arrived[1][0], w_loc["w_in"], m_loc["w_in"], v_loc["w_in"], 1, None,
                                      name="adamw_w_in_l1")
        return res["w_in_l1"][0]

    (gx,), big_g[0], small_g[0] = _layer_bwd(gx, dy, dgate, p0, saved[0], rope, 0, hook_rest=start_rest_l0,
                                             hook_w_in=start_w_in_l0)

    tile = 8 * LANE
    padded = [(k, nn, -(-nn // tile) * tile) for k, nn in _SMALL]
    spk = jnp.concatenate([jnp.pad(small_g[l][k].reshape(-1), (0, np_ - nn)).reshape(-1, LANE)
                           for l in range(nl) for k, nn, np_ in padded], axis=0)
    s_all = _all_gather([spk], name="gather_small_grads")[0]
    s_rows = sum(np_ for _, _, np_ in padded) // LANE
    s_all = s_all.reshape(N_DEV, nl, s_rows, LANE)
    s_parts = {k: a[..., :nn] for (k, nn, _), a in
               zip(padded, _unpack_rows(s_all, [(np_,) for _, _, np_ in padded]))}

    dmod_all = s_parts["dmod"]
    dmod_cols = lax.dynamic_slice_in_dim(dmod_all, me * ada_cols, ada_cols, axis=2).transpose(1, 0, 2)
    g_ada_w = _ada_bwd(c_all.T, dmod_cols, name="ada_bwd")
    gp = {}
    gp["ada_w"] = g_ada_w[None]
    gp["ada_b"] = dmod_all
    for k in ("norm_g", "q_lat_g", "kv_lat_g", "glu_b", "dw_b", "conv_ln_g", "conv_ln_b", "b_pw"):
        gp[k] = s_parts[k]
    for k in ("q_norm_g", "k_norm_g"):
        t = s_parts[k]
        gp[k] = jnp.concatenate([t[..., :NOPE], t[..., LANE:LANE + ROPE]], axis=-1)
    dw_g = s_parts["dw_w"].reshape(N_DEV, nl, HALO, D_CONV)[:, :, :CONV_K]
    gp["dw_w"] = lax.dynamic_slice_in_dim(dw_g, me * LANE, LANE, axis=3)

    res["ada_w"] = _adamw(gp["ada_w"], w_loc["ada_w"], m_loc["ada_w"], v_loc["ada_w"], name="adamw_ada_w")
    small_names = [k for k in names if k not in _BIG and k != "ada_w"]
    res.update(zip(small_names, _adamw_small([(gp[k], w_loc[k], m_loc[k], v_loc[k]) for k in small_names],
                                             name="adamw_small")))
    arrived[0][0] = _exchange_wait(*flying["l0_w_in"][:4], res["ada_w"][1], "scatter", name="scatter_wait_l0_w_in")[0]
    w_in_res = _adamw_layer(arrived[0][0], w_loc["w_in"], m_loc["w_in"], v_loc["w_in"], 0, res.pop("w_in_l1"),
                            name="adamw_w_in_l0")
    res["w_in"] = tuple(tr(a) for a in w_in_res)
    out = [loss, gx[None]]
    for idx in range(4):
        out += [res[k][idx] for k in names]
    return tuple(out)
```
